```python
import math
import jax, jax.numpy as jnp
from jax import lax
import numpy as np

D_MODEL = 2048
BATCH = 8
SEQ = 2048
DEPTH = 1

CHUNK = 64
Q_BLOCK = 128
HEAD_DIM = 128
N_HEADS_TOTAL = D_MODEL // HEAD_DIM
N_MEM_HEADS = 4
N_FOX_HEADS = (N_HEADS_TOTAL - N_MEM_HEADS) // 2
N_GDN_HEADS = N_HEADS_TOTAL - N_MEM_HEADS - N_FOX_HEADS
FOX_W = N_FOX_HEADS * HEAD_DIM
GDN_W = N_GDN_HEADS * HEAD_DIM
MEM_W = N_MEM_HEADS * HEAD_DIM
MIX_W = FOX_W + GDN_W + MEM_W
MEM_LEN = 256
CONV_WIDTH = 4
FF_DIM = -(-8 * D_MODEL // (3 * 256)) * 256
NORM_EPS = 1e-6

_SIZES = (FOX_W, FOX_W, FOX_W, N_FOX_HEADS,
          3 * GDN_W, GDN_W, N_GDN_HEADS, N_GDN_HEADS,
          MEM_W)
IN_COLS = sum(_SIZES)
SPLITS = tuple(int(s) for s in np.cumsum(_SIZES)[:-1])

kernel_name = "hymba_fox_gdn_memory_layer"


def rms_norm(x, gain):
    xf = x.astype(jnp.float32)
    y = xf * lax.rsqrt(jnp.mean(xf * xf, axis=-1, keepdims=True) + NORM_EPS)
    return (y * gain.astype(jnp.float32)).astype(x.dtype)


def l2_norm(x):
    return x * lax.rsqrt(jnp.sum(x * x, axis=-1, keepdims=True) + NORM_EPS)


def causal_depthwise_conv(x, w):
    k_w, c = w.shape
    return lax.conv_general_dilated(
        x, w[:, None, :].astype(x.dtype), window_strides=(1,), padding=[(k_w - 1, 0)],
        dimension_numbers=("NWC", "WIO", "NWC"), feature_group_count=c)


def forgetting_attention(q, k, v, log_f):
    b, t, h, d = q.shape
    c = jnp.cumsum(log_f, axis=1).transpose(0, 2, 1)
    scale = d ** -0.5
    outs = []
    for i in range(t // Q_BLOCK):
        q0, q1 = i * Q_BLOCK, (i + 1) * Q_BLOCK
        s = jnp.einsum("bqhd,bkhd->bhqk", q[:, q0:q1], k[:, :q1]).astype(jnp.float32) * scale
        s = s + c[:, :, q0:q1, None] - c[:, :, None, :q1]
        mask = jnp.arange(q1)[None, :] <= (q0 + jnp.arange(Q_BLOCK))[:, None]
        p = jax.nn.softmax(jnp.where(mask, s, -jnp.inf), axis=-1)
        outs.append(jnp.einsum("bhqk,bkhd->bqhd", p.astype(v.dtype), v[:, :q1]))
    return jnp.concatenate(outs, axis=1)


def chunked_gated_delta_rule(q, k, v, g, beta):
    b, t, h, dk = q.shape
    dv = v.shape[-1]
    n = t // CHUNK

    def to_chunks(a):
        return a.reshape(b, n, CHUNK, h, -1).transpose(1, 0, 3, 2, 4)

    q, k, v = to_chunks(q), to_chunks(k), to_chunks(v)
    g = g.reshape(b, n, CHUNK, h).transpose(1, 0, 3, 2)
    beta = beta.reshape(b, n, CHUNK, h).transpose(1, 0, 3, 2)
    gc = jnp.cumsum(g, axis=-1)
    tril = jnp.tril(jnp.ones((CHUNK, CHUNK), bool))
    strict = jnp.tril(jnp.ones((CHUNK, CHUNK), bool), -1)
    decay = jnp.exp(jnp.where(tril, gc[..., :, None] - gc[..., None, :], -jnp.inf))
    kb = k * beta[..., None]
    vb = v * beta[..., None]
    lower = jnp.where(strict, jnp.einsum("nbhcd,nbhed->nbhce", kb, k) * decay, 0.0)
    a_mat = lower + jnp.eye(CHUNK, dtype=lower.dtype)
    rhs = jnp.concatenate([vb, kb * jnp.exp(gc)[..., None]], axis=-1)
    sol = lax.linalg.triangular_solve(a_mat, rhs, left_side=True, lower=True, unit_diagonal=True)
    u, w = sol[..., :dv], sol[..., dv:]
    attn_intra = jnp.where(tril, jnp.einsum("nbhcd,nbhed->nbhce", q, k) * decay, 0.0)
    qg = q * jnp.exp(gc)[..., None]
    g_last = gc[..., -1]
    kdec = k * jnp.exp(g_last[..., None] - gc)[..., None]

    def step(state, inp):
        u_i, w_i, qg_i, at_i, kd_i, gl_i = inp
        v_new = u_i - jnp.einsum("bhcd,bhde->bhce", w_i, state)
        o = jnp.einsum("bhcd,bhde->bhce", qg_i, state) + jnp.einsum("bhcs,bhse->bhce", at_i, v_new)
        state = state * jnp.exp(gl_i)[..., None, None] + jnp.einsum("bhcd,bhce->bhde", kd_i, v_new)
        return state, o

    s0 = jnp.zeros((b, h, dk, dv), jnp.float32)
    _, o = lax.scan(step, s0, (u, w, qg, attn_intra, kdec, g_last))
    return o.transpose(1, 0, 3, 2, 4).reshape(b, t, h, dv)


def gated_deltanet(qkv, z, a, bt, conv_w, a_log, dt_bias, out_gain):
    b, t, _ = qkv.shape
    qkv = jax.nn.silu(causal_depthwise_conv(qkv, conv_w))
    q, k, v = jnp.split(qkv, 3, axis=-1)
    q = l2_norm(q.reshape(b, t, N_GDN_HEADS, HEAD_DIM).astype(jnp.float32)) * (HEAD_DIM ** -0.5)
    k = l2_norm(k.reshape(b, t, N_GDN_HEADS, HEAD_DIM).astype(jnp.float32))
    v = v.reshape(b, t, N_GDN_HEADS, HEAD_DIM).astype(jnp.float32)
    beta = jax.nn.sigmoid(bt.astype(jnp.float32))
    g = -jnp.exp(a_log.astype(jnp.float32)) * jax.nn.softplus(a.astype(jnp.float32) + dt_bias.astype(jnp.float32))
    o = chunked_gated_delta_rule(q, k, v, g, beta)
    o = rms_norm(o, out_gain) * jax.nn.silu(z.reshape(b, t, N_GDN_HEADS, HEAD_DIM).astype(jnp.float32))
    return o.reshape(b, t, GDN_W).astype(qkv.dtype)


def memory_attention(q, k, v):
    s = jnp.einsum("bthd,bmhd->bhtm", q, k).astype(jnp.float32) * (q.shape[-1] ** -0.5)
    p = jax.nn.softmax(s, axis=-1)
    return jnp.einsum("bhtm,bmhd->bthd", p.astype(v.dtype), v)


def _fwd_setup_inputs(seed: int = 0) -> dict:
    key = jax.random.key(seed)
    ks = jax.random.split(key, 24)
    nrm = lambda k, shape, s: jax.random.normal(k, shape, jnp.float32) * s
    gain = lambda k, shape: 1.0 + 0.02 * jax.random.normal(k, shape, jnp.float32)
    dt = jnp.exp(jax.random.uniform(ks[9], (DEPTH, N_GDN_HEADS), jnp.float32,
                                    minval=math.log(1e-3), maxval=math.log(1e-1)))
    return {
        "x": nrm(ks[0], (BATCH, SEQ, D_MODEL), 1.0),
        "mem": nrm(ks[1], (BATCH, MEM_LEN, D_MODEL), 1.0),
        "norm_mix": gain(ks[2], (DEPTH, D_MODEL)),
        "w_in": nrm(ks[3], (DEPTH, D_MODEL, IN_COLS), D_MODEL ** -0.5),
        "fox_f_bias": 2.0 + 0.1 * jax.random.normal(ks[4], (DEPTH, N_FOX_HEADS), jnp.float32),
        "fox_q_norm": gain(ks[5], (DEPTH, HEAD_DIM)),
        "fox_k_norm": gain(ks[6], (DEPTH, HEAD_DIM)),
        "gdn_conv": nrm(ks[7], (DEPTH, CONV_WIDTH, 3 * GDN_W), CONV_WIDTH ** -0.5),
        "gdn_a_log": jnp.log(jax.random.uniform(ks[8], (DEPTH, N_GDN_HEADS), jnp.float32, minval=1.0, maxval=16.0)),
        "gdn_dt_bias": dt + jnp.log(-jnp.expm1(-dt)),
        "gdn_out_norm": gain(ks[10], (DEPTH, HEAD_DIM)),
        "mem_norm": gain(ks[11], (DEPTH, D_MODEL)),
        "w_mem_kv": nrm(ks[12], (DEPTH, D_MODEL, 2 * MEM_W), D_MODEL ** -0.5),
        "mem_q_norm": gain(ks[13], (DEPTH, HEAD_DIM)),
        "mem_k_norm": gain(ks[14], (DEPTH, HEAD_DIM)),
        "w_out": nrm(ks[15], (DEPTH, MIX_W, D_MODEL), MIX_W ** -0.5),
        "norm_ffn": gain(ks[16], (DEPTH, D_MODEL)),
        "w_gate_up": nrm(ks[17], (DEPTH, D_MODEL, 2 * FF_DIM), D_MODEL ** -0.5),
        "w_down": nrm(ks[18], (DEPTH, FF_DIM, D_MODEL), FF_DIM ** -0.5),
    }


def _fwd_reference(x, mem, norm_mix, w_in, fox_f_bias, fox_q_norm, fox_k_norm, gdn_conv, gdn_a_log,
              gdn_dt_bias, gdn_out_norm, mem_norm, w_mem_kv, mem_q_norm, mem_k_norm, w_out,
              norm_ffn, w_gate_up, w_down):
    b, t, _ = x.shape
    m = mem.shape[1]
    h = x
    for l in range(DEPTH):
        u = rms_norm(h, norm_mix[l])
        p = u @ w_in[l]
        fq, fk, fv, ff, gqkv, gz, ga, gb, mq = jnp.split(p, SPLITS, axis=-1)

        fq = rms_norm(fq.reshape(b, t, N_FOX_HEADS, HEAD_DIM), fox_q_norm[l])
        fk = rms_norm(fk.reshape(b, t, N_FOX_HEADS, HEAD_DIM), fox_k_norm[l])
        fv = fv.reshape(b, t, N_FOX_HEADS, HEAD_DIM)
        log_f = jax.nn.log_sigmoid(ff.astype(jnp.float32) + fox_f_bias[l].astype(jnp.float32))
        o_fox = forgetting_attention(fq, fk, fv, log_f).reshape(b, t, FOX_W)

        o_gdn = gated_deltanet(gqkv, gz, ga, gb, gdn_conv[l], gdn_a_log[l], gdn_dt_bias[l], gdn_out_norm[l])

        mkv = rms_norm(mem, mem_norm[l]) @ w_mem_kv[l]
        mk, mv = jnp.split(mkv, 2, axis=-1)
        mq = rms_norm(mq.reshape(b, t, N_MEM_HEADS, HEAD_DIM), mem_q_norm[l])
        mk = rms_norm(mk.reshape(b, m, N_MEM_HEADS, HEAD_DIM), mem_k_norm[l])
        mv = mv.reshape(b, m, N_MEM_HEADS, HEAD_DIM)
        o_mem = memory_attention(mq, mk, mv).reshape(b, t, MEM_W)

        mix = jnp.concatenate([o_fox, o_gdn.astype(o_fox.dtype), o_mem], axis=-1)
        h = h + mix @ w_out[l]

        gate, up = jnp.split(rms_norm(h, norm_ffn[l]) @ w_gate_up[l], 2, axis=-1)
        h = h + (jax.nn.silu(gate) * up) @ w_down[l]
    return h


import jax as _jax
import jax.numpy as _jnp

TWIN_FORMAT = 'train_step'
FWD_PARAMS = ['x', 'mem', 'norm_mix', 'w_in', 'fox_f_bias', 'fox_q_norm', 'fox_k_norm', 'gdn_conv', 'gdn_a_log', 'gdn_dt_bias', 'gdn_out_norm', 'mem_norm', 'w_mem_kv', 'mem_q_norm', 'mem_k_norm', 'w_out', 'norm_ffn', 'w_gate_up', 'w_down']
TWIN_WEIGHTS = ['norm_mix', 'w_in', 'fox_f_bias', 'fox_q_norm', 'fox_k_norm', 'gdn_conv', 'gdn_a_log', 'gdn_dt_bias', 'gdn_out_norm', 'mem_norm', 'w_mem_kv', 'mem_q_norm', 'mem_k_norm', 'w_out', 'norm_ffn', 'w_gate_up', 'w_down']
TWIN_DIFF_INPUT = 'x'
TWIN_INPUTS = ['x', 'mem', 'norm_mix', 'w_in', 'fox_f_bias', 'fox_q_norm', 'fox_k_norm', 'gdn_conv', 'gdn_a_log', 'gdn_dt_bias', 'gdn_out_norm', 'mem_norm', 'w_mem_kv', 'mem_q_norm', 'mem_k_norm', 'w_out', 'norm_ffn', 'w_gate_up', 'w_down', 'loss_target', 'm_norm_mix', 'm_w_in', 'm_fox_f_bias', 'm_fox_q_norm', 'm_fox_k_norm', 'm_gdn_conv', 'm_gdn_a_log', 'm_gdn_dt_bias', 'm_gdn_out_norm', 'm_mem_norm', 'm_w_mem_kv', 'm_mem_q_norm', 'm_mem_k_norm', 'm_w_out', 'm_norm_ffn', 'm_w_gate_up', 'm_w_down', 'v_norm_mix', 'v_w_in', 'v_fox_f_bias', 'v_fox_q_norm', 'v_fox_k_norm', 'v_gdn_conv', 'v_gdn_a_log', 'v_gdn_dt_bias', 'v_gdn_out_norm', 'v_mem_norm', 'v_w_mem_kv', 'v_mem_q_norm', 'v_mem_k_norm', 'v_w_out', 'v_norm_ffn', 'v_w_gate_up', 'v_w_down']
TWIN_OUTPUTS = ['loss', 'grad_x', 'grad_norm_mix', 'grad_w_in', 'grad_fox_f_bias', 'grad_fox_q_norm', 'grad_fox_k_norm', 'grad_gdn_conv', 'grad_gdn_a_log', 'grad_gdn_dt_bias', 'grad_gdn_out_norm', 'grad_mem_norm', 'grad_w_mem_kv', 'grad_mem_q_norm', 'grad_mem_k_norm', 'grad_w_out', 'grad_norm_ffn', 'grad_w_gate_up', 'grad_w_down', 'delta_norm_mix', 'delta_w_in', 'delta_fox_f_bias', 'delta_fox_q_norm', 'delta_fox_k_norm', 'delta_gdn_conv', 'delta_gdn_a_log', 'delta_gdn_dt_bias', 'delta_gdn_out_norm', 'delta_mem_norm', 'delta_w_mem_kv', 'delta_mem_q_norm', 'delta_mem_k_norm', 'delta_w_out', 'delta_norm_ffn', 'delta_w_gate_up', 'delta_w_down', 'new_m_norm_mix', 'new_m_w_in', 'new_m_fox_f_bias', 'new_m_fox_q_norm', 'new_m_fox_k_norm', 'new_m_gdn_conv', 'new_m_gdn_a_log', 'new_m_gdn_dt_bias', 'new_m_gdn_out_norm', 'new_m_mem_norm', 'new_m_w_mem_kv', 'new_m_mem_q_norm', 'new_m_mem_k_norm', 'new_m_w_out', 'new_m_norm_ffn', 'new_m_w_gate_up', 'new_m_w_down', 'new_v_norm_mix', 'new_v_w_in', 'new_v_fox_f_bias', 'new_v_fox_q_norm', 'new_v_fox_k_norm', 'new_v_gdn_conv', 'new_v_gdn_a_log', 'new_v_gdn_dt_bias', 'new_v_gdn_out_norm', 'new_v_mem_norm', 'new_v_w_mem_kv', 'new_v_mem_q_norm', 'new_v_mem_k_norm', 'new_v_w_out', 'new_v_norm_ffn', 'new_v_w_gate_up', 'new_v_w_down']
TWIN_LEAF_KINDS = {'loss': 'loss', 'grad_x': 'grad_x', 'grad_norm_mix': 'grad_w', 'grad_w_in': 'grad_w', 'grad_fox_f_bias': 'grad_w', 'grad_fox_q_norm': 'grad_w', 'grad_fox_k_norm': 'grad_w', 'grad_gdn_conv': 'grad_w', 'grad_gdn_a_log': 'grad_w', 'grad_gdn_dt_bias': 'grad_w', 'grad_gdn_out_norm': 'grad_w', 'grad_mem_norm': 'grad_w', 'grad_w_mem_kv': 'grad_w', 'grad_mem_q_norm': 'grad_w', 'grad_mem_k_norm': 'grad_w', 'grad_w_out': 'grad_w', 'grad_norm_ffn': 'grad_w', 'grad_w_gate_up': 'grad_w', 'grad_w_down': 'grad_w', 'delta_norm_mix': 'delta_w', 'delta_w_in': 'delta_w', 'delta_fox_f_bias': 'delta_w', 'delta_fox_q_norm': 'delta_w', 'delta_fox_k_norm': 'delta_w', 'delta_gdn_conv': 'delta_w', 'delta_gdn_a_log': 'delta_w', 'delta_gdn_dt_bias': 'delta_w', 'delta_gdn_out_norm': 'delta_w', 'delta_mem_norm': 'delta_w', 'delta_w_mem_kv': 'delta_w', 'delta_mem_q_norm': 'delta_w', 'delta_mem_k_norm': 'delta_w', 'delta_w_out': 'delta_w', 'delta_norm_ffn': 'delta_w', 'delta_w_gate_up': 'delta_w', 'delta_w_down': 'delta_w', 'new_m_norm_mix': 'new_m', 'new_m_w_in': 'new_m', 'new_m_fox_f_bias': 'new_m', 'new_m_fox_q_norm': 'new_m', 'new_m_fox_k_norm': 'new_m', 'new_m_gdn_conv': 'new_m', 'new_m_gdn_a_log': 'new_m', 'new_m_gdn_dt_bias': 'new_m', 'new_m_gdn_out_norm': 'new_m', 'new_m_mem_norm': 'new_m', 'new_m_w_mem_kv': 'new_m', 'new_m_mem_q_norm': 'new_m', 'new_m_mem_k_norm': 'new_m', 'new_m_w_out': 'new_m', 'new_m_norm_ffn': 'new_m', 'new_m_w_gate_up': 'new_m', 'new_m_w_down': 'new_m', 'new_v_norm_mix': 'new_v', 'new_v_w_in': 'new_v', 'new_v_fox_f_bias': 'new_v', 'new_v_fox_q_norm': 'new_v', 'new_v_fox_k_norm': 'new_v', 'new_v_gdn_conv': 'new_v', 'new_v_gdn_a_log': 'new_v', 'new_v_gdn_dt_bias': 'new_v', 'new_v_gdn_out_norm': 'new_v', 'new_v_mem_norm': 'new_v', 'new_v_w_mem_kv': 'new_v', 'new_v_mem_q_norm': 'new_v', 'new_v_mem_k_norm': 'new_v', 'new_v_w_out': 'new_v', 'new_v_norm_ffn': 'new_v', 'new_v_w_gate_up': 'new_v', 'new_v_w_down': 'new_v'}


def _forward(args):
    return _fwd_reference(*[args[k] for k in FWD_PARAMS])


def _output_shape():
    out = _jax.eval_shape(lambda: _forward(_fwd_setup_inputs(0)))
    return out.shape, out.dtype

N_MICROBATCH = 1
ADAM_LR = 0.001
ADAM_B1 = 0.9
ADAM_B2 = 0.999
ADAM_EPS = 1e-08
ADAM_WD = 0.01
ADAM_STEP = 10
PER_EXAMPLE_BATCH_AXIS = {'x': 0, 'mem': 0, 'loss_target': 0}
SHARED_INPUTS = []
_WEIGHT_DTYPES = {'norm_mix': _jnp.float32, 'w_in': _jnp.float32, 'fox_f_bias': _jnp.float32, 'fox_q_norm': _jnp.float32, 'fox_k_norm': _jnp.float32, 'gdn_conv': _jnp.float32, 'gdn_a_log': _jnp.float32, 'gdn_dt_bias': _jnp.float32, 'gdn_out_norm': _jnp.float32, 'mem_norm': _jnp.float32, 'w_mem_kv': _jnp.float32, 'mem_q_norm': _jnp.float32, 'mem_k_norm': _jnp.float32, 'w_out': _jnp.float32, 'norm_ffn': _jnp.float32, 'w_gate_up': _jnp.float32, 'w_down': _jnp.float32}
MOMENT_SCALE = {'norm_mix': 1.819100e+00, 'w_in': 9.177868e-02, 'fox_f_bias': 5.132450e+01, 'fox_q_norm': 3.018229e+00, 'fox_k_norm': 3.016995e+00, 'gdn_conv': 1.303558e-01, 'gdn_a_log': 8.846419e+00, 'gdn_dt_bias': 8.155373e+00, 'gdn_out_norm': 1.737866e+01, 'mem_norm': 2.862751e-02, 'w_mem_kv': 3.107238e-02, 'mem_q_norm': 2.894289e-01, 'mem_k_norm': 2.898963e-01, 'w_out': 1.172084e-01, 'norm_ffn': 6.211741e+00, 'w_gate_up': 5.763438e-02, 'w_down': 7.269417e-02}


def _to_microbatches(a, axis):
    t = _jnp.moveaxis(a, axis, 0)
    t = t.reshape((N_MICROBATCH, t.shape[0] // N_MICROBATCH) + t.shape[1:])
    return _jnp.moveaxis(t, 1, axis + 1)


def setup_inputs(seed: int = 0) -> dict:
    inp = _fwd_setup_inputs(seed)
    key = _jax.random.fold_in(_jax.random.key(seed), 7919)
    shape, _ = _output_shape()
    out = dict(inp)
    out["loss_target"] = _jax.random.normal(_jax.random.fold_in(key, 0), shape, _jnp.float32)
    for i, name in enumerate(TWIN_WEIGHTS):
        w = inp[name].astype(_jnp.float32)
        if MOMENT_SCALE is None:
            s = _jnp.sqrt(_jnp.mean(_jnp.square(w)) + 1e-30)
        else:
            s = MOMENT_SCALE[name]
        km, kv = _jax.random.split(_jax.random.fold_in(key, i + 1))
        out[name] = w
        out["m_" + name] = s * _jax.random.normal(km, w.shape, _jnp.float32)
        out["v_" + name] = (s * s) * _jax.random.uniform(kv, w.shape, _jnp.float32, 0.5, 1.5)
    if N_MICROBATCH > 1:
        for name, axis in PER_EXAMPLE_BATCH_AXIS.items():
            out[name] = _to_microbatches(out[name], axis)
    return {'x': out['x'], 'mem': out['mem'], 'norm_mix': out['norm_mix'], 'w_in': out['w_in'], 'fox_f_bias': out['fox_f_bias'], 'fox_q_norm': out['fox_q_norm'], 'fox_k_norm': out['fox_k_norm'], 'gdn_conv': out['gdn_conv'], 'gdn_a_log': out['gdn_a_log'], 'gdn_dt_bias': out['gdn_dt_bias'], 'gdn_out_norm': out['gdn_out_norm'], 'mem_norm': out['mem_norm'], 'w_mem_kv': out['w_mem_kv'], 'mem_q_norm': out['mem_q_norm'], 'mem_k_norm': out['mem_k_norm'], 'w_out': out['w_out'], 'norm_ffn': out['norm_ffn'], 'w_gate_up': out['w_gate_up'], 'w_down': out['w_down'], 'loss_target': out['loss_target'], 'm_norm_mix': out['m_norm_mix'], 'm_w_in': out['m_w_in'], 'm_fox_f_bias': out['m_fox_f_bias'], 'm_fox_q_norm': out['m_fox_q_norm'], 'm_fox_k_norm': out['m_fox_k_norm'], 'm_gdn_conv': out['m_gdn_conv'], 'm_gdn_a_log': out['m_gdn_a_log'], 'm_gdn_dt_bias': out['m_gdn_dt_bias'], 'm_gdn_out_norm': out['m_gdn_out_norm'], 'm_mem_norm': out['m_mem_norm'], 'm_w_mem_kv': out['m_w_mem_kv'], 'm_mem_q_norm': out['m_mem_q_norm'], 'm_mem_k_norm': out['m_mem_k_norm'], 'm_w_out': out['m_w_out'], 'm_norm_ffn': out['m_norm_ffn'], 'm_w_gate_up': out['m_w_gate_up'], 'm_w_down': out['m_w_down'], 'v_norm_mix': out['v_norm_mix'], 'v_w_in': out['v_w_in'], 'v_fox_f_bias': out['v_fox_f_bias'], 'v_fox_q_norm': out['v_fox_q_norm'], 'v_fox_k_norm': out['v_fox_k_norm'], 'v_gdn_conv': out['v_gdn_conv'], 'v_gdn_a_log': out['v_gdn_a_log'], 'v_gdn_dt_bias': out['v_gdn_dt_bias'], 'v_gdn_out_norm': out['v_gdn_out_norm'], 'v_mem_norm': out['v_mem_norm'], 'v_w_mem_kv': out['v_w_mem_kv'], 'v_mem_q_norm': out['v_mem_q_norm'], 'v_mem_k_norm': out['v_mem_k_norm'], 'v_w_out': out['v_w_out'], 'v_norm_ffn': out['v_norm_ffn'], 'v_w_gate_up': out['v_w_gate_up'], 'v_w_down': out['v_w_down']}


def _loss(weights, diff, rest, loss_target):
    with _jax.named_scope("forward"):
        args = {**rest, TWIN_DIFF_INPUT: diff, **{k: w.astype(_WEIGHT_DTYPES[k]) for k, w in weights.items()}}
        y = _forward(args)
    with _jax.named_scope("loss_head"):
        err = _jnp.square(y.astype(_jnp.float32) - loss_target)
        return 0.5 * _jnp.sum(_jnp.mean(err, axis=-1)) if err.ndim else 0.5 * err


def _adamw(w, g, m, v):
    m = ADAM_B1 * m + (1.0 - ADAM_B1) * g
    v = ADAM_B2 * v + (1.0 - ADAM_B2) * _jnp.square(g)
    m_hat = m / (1.0 - ADAM_B1 ** ADAM_STEP)
    v_hat = v / (1.0 - ADAM_B2 ** ADAM_STEP)
    delta = -ADAM_LR * (m_hat / (_jnp.sqrt(v_hat) + ADAM_EPS) + ADAM_WD * w)
    return delta, m, v


def reference(x, mem, norm_mix, w_in, fox_f_bias, fox_q_norm, fox_k_norm, gdn_conv, gdn_a_log, gdn_dt_bias, gdn_out_norm, mem_norm, w_mem_kv, mem_q_norm, mem_k_norm, w_out, norm_ffn, w_gate_up, w_down, loss_target, m_norm_mix, m_w_in, m_fox_f_bias, m_fox_q_norm, m_fox_k_norm, m_gdn_conv, m_gdn_a_log, m_gdn_dt_bias, m_gdn_out_norm, m_mem_norm, m_w_mem_kv, m_mem_q_norm, m_mem_k_norm, m_w_out, m_norm_ffn, m_w_gate_up, m_w_down, v_norm_mix, v_w_in, v_fox_f_bias, v_fox_q_norm, v_fox_k_norm, v_gdn_conv, v_gdn_a_log, v_gdn_dt_bias, v_gdn_out_norm, v_mem_norm, v_w_mem_kv, v_mem_q_norm, v_mem_k_norm, v_w_out, v_norm_ffn, v_w_gate_up, v_w_down):
    given = dict(x=x, mem=mem, norm_mix=norm_mix, w_in=w_in, fox_f_bias=fox_f_bias, fox_q_norm=fox_q_norm, fox_k_norm=fox_k_norm, gdn_conv=gdn_conv, gdn_a_log=gdn_a_log, gdn_dt_bias=gdn_dt_bias, gdn_out_norm=gdn_out_norm, mem_norm=mem_norm, w_mem_kv=w_mem_kv, mem_q_norm=mem_q_norm, mem_k_norm=mem_k_norm, w_out=w_out, norm_ffn=norm_ffn, w_gate_up=w_gate_up, w_down=w_down, loss_target=loss_target, m_norm_mix=m_norm_mix, m_w_in=m_w_in, m_fox_f_bias=m_fox_f_bias, m_fox_q_norm=m_fox_q_norm, m_fox_k_norm=m_fox_k_norm, m_gdn_conv=m_gdn_conv, m_gdn_a_log=m_gdn_a_log, m_gdn_dt_bias=m_gdn_dt_bias, m_gdn_out_norm=m_gdn_out_norm, m_mem_norm=m_mem_norm, m_w_mem_kv=m_w_mem_kv, m_mem_q_norm=m_mem_q_norm, m_mem_k_norm=m_mem_k_norm, m_w_out=m_w_out, m_norm_ffn=m_norm_ffn, m_w_gate_up=m_w_gate_up, m_w_down=m_w_down, v_norm_mix=v_norm_mix, v_w_in=v_w_in, v_fox_f_bias=v_fox_f_bias, v_fox_q_norm=v_fox_q_norm, v_fox_k_norm=v_fox_k_norm, v_gdn_conv=v_gdn_conv, v_gdn_a_log=v_gdn_a_log, v_gdn_dt_bias=v_gdn_dt_bias, v_gdn_out_norm=v_gdn_out_norm, v_mem_norm=v_mem_norm, v_w_mem_kv=v_w_mem_kv, v_mem_q_norm=v_mem_q_norm, v_mem_k_norm=v_mem_k_norm, v_w_out=v_w_out, v_norm_ffn=v_norm_ffn, v_w_gate_up=v_w_gate_up, v_w_down=v_w_down)
    weights = {n: given[n] for n in TWIN_WEIGHTS}
    shared = {n: given[n] for n in SHARED_INPUTS}
    per_example = {n: given[n] for n in ['x', 'mem']}
    grad_fn = _jax.value_and_grad(_loss, argnums=(0, 1))

    def one_microbatch(ex, loss_target):
        ex = dict(ex)
        diff = ex.pop(TWIN_DIFF_INPUT)
        return grad_fn(weights, diff, {**shared, **ex}, loss_target)

    if N_MICROBATCH == 1:
        loss, (grad_w, grad_x) = one_microbatch(per_example, given["loss_target"])
    else:
        def body(carry, xs):
            loss_sum, grad_sum = carry
            l_k, (gw_k, gx_k) = one_microbatch(xs[0], xs[1])
            with _jax.named_scope("update"):
                return (loss_sum + l_k, _jax.tree.map(_jnp.add, grad_sum, gw_k)), gx_k

        init = (_jnp.zeros((), _jnp.float32), _jax.tree.map(_jnp.zeros_like, weights))
        (loss, grad_w), grad_x = _jax.lax.scan(body, init, (per_example, given["loss_target"]))
    with _jax.named_scope("update"):
        delta_w, new_m, new_v = {}, {}, {}
        for n in TWIN_WEIGHTS:
            delta_w[n], new_m[n], new_v[n] = _adamw(weights[n], grad_w[n], given["m_" + n], given["v_" + n])
    return (loss, grad_x, *[grad_w[n] for n in TWIN_WEIGHTS], *[delta_w[n] for n in TWIN_WEIGHTS],
            *[new_m[n] for n in TWIN_WEIGHTS], *[new_v[n] for n in TWIN_WEIGHTS])
```

```python
import functools
import math

import jax
import jax.numpy as jnp
from jax import lax
from jax.experimental import pallas as pl
from jax.experimental.pallas import tpu as pltpu

F32 = jnp.float32
BF16 = jnp.bfloat16
HI = lax.Precision.HIGHEST
SDS = jax.ShapeDtypeStruct

N_DEV = 8
HD = 128
NF, NG, NM = 6, 6, 4
CHUNK = 64
GROUP = 4
NORM_EPS = 1e-6
FQ, FK, FV, GQ, GK, GV, GZ, MQ, SM, NPB = 0, 6, 12, 18, 24, 30, 36, 42, 46, 48
PC = NPB * HD
L_FF, L_GA, L_GB = 0, 6, 12
VMEM_LIMIT = 56 * 1024 * 1024

ADAM_LR, ADAM_B1, ADAM_B2, ADAM_EPS, ADAM_WD, ADAM_STEP = 0.001, 0.9, 0.999, 1e-08, 0.01, 10

NN = (((1,), (0,)), ((), ()))
NT = (((1,), (1,)), ((), ()))
TN = (((0,), (0,)), ((), ()))
MESH = pl.DeviceIdType.MESH


def _cp(*sem):
    return pltpu.CompilerParams(dimension_semantics=tuple(sem) if sem else None, vmem_limit_bytes=VMEM_LIMIT)


def _dot(a, b, dims=NN):
    return lax.dot_general(a, b, dims, preferred_element_type=F32)


def _bdot(a, b):
    return _dot(a.astype(BF16), b.astype(BF16))


def _iota(shape, axis):
    return lax.broadcasted_iota(jnp.int32, shape, axis)


def _rms(x, gain):
    return x * lax.rsqrt(jnp.mean(x * x, axis=-1, keepdims=True) + NORM_EPS) * gain


def _sigmoid(x):
    z = jnp.exp(-jnp.abs(x))
    return jnp.where(x >= 0, 1.0 / (1.0 + z), z / (1.0 + z))


def _silu(x):
    return x * _sigmoid(x)


def _softplus(x):
    return jnp.maximum(x, 0.0) + jnp.log(1.0 + jnp.exp(-jnp.abs(x)))


def _lane_pick(x, lane):
    oh = (_iota((1, x.shape[-1]), 1) == lane).astype(F32)
    return jnp.sum(x * oh, axis=-1, keepdims=True)


def _cumsum_rows(x):
    tril = (_iota((HD, HD), 0) >= _iota((HD, HD), 1)).astype(F32)
    carry = jnp.zeros((1, x.shape[1]), F32)
    outs = []
    for b in range(x.shape[0] // HD):
        blk = x[b * HD:(b + 1) * HD]
        outs.append(jnp.dot(tril, blk, precision=HI, preferred_element_type=F32) + carry)
        carry = carry + jnp.sum(blk, axis=0, keepdims=True)
    return jnp.concatenate(outs, axis=0)


def _row_spec(r, tm):
    if isinstance(r, tuple):
        arr, width, cb = r
        return arr, pl.BlockSpec((tm, width), lambda i, cb=cb: (i, cb))
    return r, pl.BlockSpec((tm, r.shape[1]), lambda i: (i, 0))


def _rowwise(name, fn, rows, consts, outs, tm):
    arrs, specs = zip(*[_row_spec(r, tm) for r in rows])
    n_rows = arrs[0].shape[0]
    nr, nc = len(rows), len(consts)

    def body(*refs):
        res = fn(*[r[...] for r in refs[:nr + nc]])
        for o, v in zip(refs[nr + nc:], res):
            o[...] = v.astype(o.dtype)

    return pl.pallas_call(
        body, grid=(n_rows // tm,), name=name,
        in_specs=list(specs) + [pl.BlockSpec(c.shape, lambda i: (0, 0)) for c in consts],
        out_specs=[pl.BlockSpec((tm, w), lambda i: (i, 0)) for w, _ in outs],
        out_shape=[SDS((n_rows, w), dt) for w, dt in outs],
        compiler_params=_cp("parallel"),
    )(*arrs, *consts)


def _rowwise_vjp(name, fn, rows, consts, cts, grad_dtypes, tm):
    arrs, specs = zip(*[_row_spec(r, tm) for r in rows])
    ct_arrs, ct_specs = zip(*[_row_spec(r, tm) for r in cts])
    n_rows = arrs[0].shape[0]
    nr, nc, nct, ng = len(rows), len(consts), len(cts), len(grad_dtypes)
    widths = [s.block_shape[1] for s in specs[:ng]]

    def body(*refs):
        vals = [r[...].astype(F32) for r in refs[:nr + nc]]
        ctv = tuple(r[...].astype(F32) for r in refs[nr + nc:nr + nc + nct])
        _, vjp = jax.vjp(fn, *vals)
        grads = vjp(ctv)
        outs = refs[nr + nc + nct:]
        for o, g in zip(outs[:ng], grads[:ng]):
            o[...] = g.astype(o.dtype)

        @pl.when(pl.program_id(0) == 0)
        def _():
            for o in outs[ng:]:
                o[...] = jnp.zeros_like(o)

        for o, g in zip(outs[ng:], grads[nr:]):
            o[...] += g

    return pl.pallas_call(
        body, grid=(n_rows // tm,), name=name,
        in_specs=list(specs) + [pl.BlockSpec(c.shape, lambda i: (0, 0)) for c in consts] + list(ct_specs),
        out_specs=[pl.BlockSpec((tm, w), lambda i: (i, 0)) for w in widths]
        + [pl.BlockSpec(c.shape, lambda i: (0, 0)) for c in consts],
        out_shape=[SDS((n_rows, w), dt) for w, dt in zip(widths, grad_dtypes)] + [SDS(c.shape, F32) for c in consts],
        compiler_params=_cp("arbitrary"),
    )(*arrs, *consts, *ct_arrs)


def _tile(n, pref):
    t = min(n, pref)
    while n % t or (t % HD and t != n):
        t -= 1
    return t


def _matmul(name, a, b, dims, out_dtype, tm, tn, residual=None):
    ta, tb = dims == TN, dims == NT
    m = a.shape[1] if ta else a.shape[0]
    k = a.shape[0] if ta else a.shape[1]
    n = b.shape[0] if tb else b.shape[1]
    tm, tn = _tile(m, tm), _tile(n, tn)

    def body(*refs):
        acc = _dot(refs[0][...], refs[1][...], dims)
        if residual is not None:
            acc = acc + refs[2][...]
        refs[-1][...] = acc.astype(out_dtype)

    in_specs = [pl.BlockSpec((k, tm), lambda i, j: (0, i)) if ta else pl.BlockSpec((tm, k), lambda i, j: (i, 0)),
                pl.BlockSpec((tn, k), lambda i, j: (j, 0)) if tb else pl.BlockSpec((k, tn), lambda i, j: (0, j))]
    ops = [a, b]
    if residual is not None:
        in_specs.append(pl.BlockSpec((tm, tn), lambda i, j: (i, j)))
        ops.append(residual)
    return pl.pallas_call(
        body, grid=(m // tm, n // tn), name=name, in_specs=in_specs,
        out_specs=pl.BlockSpec((tm, tn), lambda i, j: (i, j)), out_shape=SDS((m, n), out_dtype),
        compiler_params=_cp("parallel", "parallel"),
    )(*ops)


def _ffn_up(h1n, wgu):
    t, d = h1n.shape
    w = wgu.shape[3]
    tm = _tile(t, 512)

    def body(a, b, gu, act):
        x = a[...]
        g = _dot(x, b[0])
        u = _dot(x, b[1])
        gu[0] = g.astype(BF16)
        gu[1] = u.astype(BF16)
        act[...] = (_silu(g) * u).astype(BF16)

    return pl.pallas_call(
        body, grid=(4, t // tm), name="ffn_up",
        in_specs=[pl.BlockSpec((tm, d), lambda j, i: (i, 0)), pl.BlockSpec((2, None, d, w), lambda j, i: (0, j, 0, 0))],
        out_specs=[pl.BlockSpec((2, None, tm, w), lambda j, i: (0, j, i, 0)), pl.BlockSpec((tm, w), lambda j, i: (i, j))],
        out_shape=[SDS((2, 4, t, w), BF16), SDS((t, 4 * w), BF16)],
        compiler_params=_cp("parallel", "parallel"),
    )(h1n, wgu)


def _ffn_down_loss(act, wdown, h1, target):
    t, f = act.shape
    d = wdown.shape[1]
    tm, tn = _tile(t, 512), _tile(d, 512)

    def body(a, b, h, tg, dy, dyb, ls):
        e = _dot(a[...], b[...]) + h[...] - tg[...]
        g = e * (1.0 / d)
        dy[...] = g
        dyb[...] = g.astype(BF16)
        ls[...] = jnp.broadcast_to(jnp.sum(e * e), (8, HD))

    return pl.pallas_call(
        body, grid=(t // tm, d // tn), name="ffn_down_loss",
        in_specs=[pl.BlockSpec((tm, f), lambda i, j: (i, 0)), pl.BlockSpec((f, tn), lambda i, j: (0, j)),
                  pl.BlockSpec((tm, tn), lambda i, j: (i, j)), pl.BlockSpec((tm, tn), lambda i, j: (i, j))],
        out_specs=[pl.BlockSpec((tm, tn), lambda i, j: (i, j)), pl.BlockSpec((tm, tn), lambda i, j: (i, j)),
                   pl.BlockSpec((8, HD), lambda i, j: (i, j))],
        out_shape=[SDS((t, d), F32), SDS((t, d), BF16), SDS((8 * (t // tm), HD * (d // tn)), F32)],
        compiler_params=_cp("parallel", "parallel"),
    )(act, wdown, h1, target)


def _ffn_down_bwd(dyb, wdown4, gu):
    t, d = dyb.shape
    w = wdown4.shape[1]
    tm = _tile(t, 512)

    def body(a, b, gu_ref, out):
        da = _dot(a[...], b[...], NT)
        g = gu_ref[0].astype(F32)
        u = gu_ref[1].astype(F32)
        s = _sigmoid(g)
        out[0] = (da * u * (s * (1.0 + g * (1.0 - s)))).astype(BF16)
        out[1] = (da * g * s).astype(BF16)

    return pl.pallas_call(
        body, grid=(4, t // tm), name="ffn_down_bwd",
        in_specs=[pl.BlockSpec((tm, d), lambda j, i: (i, 0)), pl.BlockSpec((None, w, d), lambda j, i: (j, 0, 0)),
                  pl.BlockSpec((2, None, tm, w), lambda j, i: (0, j, i, 0))],
        out_specs=pl.BlockSpec((2, None, tm, w), lambda j, i: (0, j, i, 0)),
        out_shape=SDS((2, 4, t, w), BF16),
        compiler_params=_cp("parallel", "parallel"),
    )(dyb, wdown4, gu)


def _ffn_up_bwd_x(dgu, wgu):
    _, t, w = dgu.shape
    d = wgu.shape[1]
    tm = _tile(t, 512)

    def body(a, b, out):
        @pl.when(pl.program_id(1) == 0)
        def _():
            out[...] = jnp.zeros_like(out)
        out[...] += _dot(a[...], b[...], NT)

    return pl.pallas_call(
        body, grid=(t // tm, 8), name="ffn_up_bwd_x",
        in_specs=[pl.BlockSpec((None, tm, w), lambda i, j: (j, i, 0)), pl.BlockSpec((None, d, w), lambda i, j: (j, 0, 0))],
        out_specs=pl.BlockSpec((tm, d), lambda i, j: (i, 0)), out_shape=SDS((t, d), F32),
        compiler_params=_cp("parallel", "arbitrary"),
    )(dgu, wgu)


def _ffn_up_bwd_w(h1n, dgu):
    _, t, w = dgu.shape
    d = h1n.shape[1]
    tm = _tile(d, 512)

    def body(a, b, out):
        out[...] = _dot(a[...], b[...], TN).astype(BF16)

    return pl.pallas_call(
        body, grid=(8, d // tm), name="ffn_up_bwd_w",
        in_specs=[pl.BlockSpec((t, tm), lambda j, i: (0, i)), pl.BlockSpec((None, t, w), lambda j, i: (j, 0, 0))],
        out_specs=pl.BlockSpec((None, tm, w), lambda j, i: (j, i, 0)), out_shape=SDS((8, d, w), BF16),
        compiler_params=_cp("parallel", "parallel"),
    )(h1n, dgu)


def _fox_prep(fq, fk, sm, fb, qg, kg, h):
    qn = _rms(fq, qg)
    kn = _rms(fk, kg)
    c = _cumsum_rows(-_softplus(-(sm + fb)))
    ccol = _lane_pick(c, L_FF + h)
    crow = jnp.sum(c.T * (_iota((HD, 1), 0) == L_FF + h).astype(F32), axis=0, keepdims=True)
    return qn, kn, ccol, crow


def _fox_block(q, k, v, cc, cr, off):
    s = _dot(q.astype(BF16), k.astype(BF16), NT) * (HD ** -0.5) + cc - cr
    s = jnp.where(_iota(s.shape, 1) <= _iota(s.shape, 0) + off, s, -1e30)
    e = jnp.exp(s - lax.stop_gradient(jnp.max(s, axis=1, keepdims=True)))
    p = e / jnp.sum(e, axis=1, keepdims=True)
    return _dot(p.astype(BF16), v.astype(BF16))


ONE_BUFFER = pl.Buffered(1)


def _pcol(t, cb):
    return pl.BlockSpec((t, HD), lambda h, cb=cb: (0, cb + h), pipeline_mode=ONE_BUFFER)


def _smcol(t):
    return pl.BlockSpec((t, HD), lambda h: (0, SM), pipeline_mode=ONE_BUFFER)


def _head(t):
    return pl.BlockSpec((t, HD), lambda h: (0, h), pipeline_mode=ONE_BUFFER)


def _small(n):
    return pl.BlockSpec((n, HD), lambda h: (0, 0), pipeline_mode=ONE_BUFFER)


def _fox_fwd(p, fb, qg, kg, bq):
    t = p.shape[0]

    def body(fq, fk, fv, sm, fb_r, qg_r, kg_r, o, qn_s, cc_s):
        h = pl.program_id(0)
        qn, kn, ccol, crow = _fox_prep(fq[...], fk[...], sm[...], fb_r[...], qg_r[...], kg_r[...], h)
        qn_s[...] = qn
        cc_s[...] = ccol
        knb = kn.astype(BF16)
        vb = fv[...].astype(BF16)

        def step(i, carry):
            off = pl.multiple_of(i * bq, bq)
            rows = pl.ds(off, bq)
            o[rows, :] = _fox_block(qn_s[rows, :], knb, vb, cc_s[rows, :], crow, off).astype(o.dtype)
            return carry

        lax.fori_loop(0, t // bq, step, 0)

    return pl.pallas_call(
        body, grid=(NF,), name="fox_fwd",
        in_specs=[_pcol(t, FQ), _pcol(t, FK), _pcol(t, FV), _smcol(t), _small(1), _small(1), _small(1)],
        out_specs=_head(t), out_shape=SDS((t, NF * HD), BF16),
        scratch_shapes=[pltpu.VMEM((t, HD), F32), pltpu.VMEM((t, 1), F32)],
        compiler_params=_cp("parallel"),
    )(p, p, p, p, fb, qg, kg)


def _fox_bwd(p, fb, qg, kg, dmix, bq):
    t = p.shape[0]

    def body(fq, fk, fv, sm, fb_r, qg_r, kg_r, do, dfq, dfk, dfv, dsm, dfb, dqg, dkg,
             qn_s, cc_s, dqn_s, dcc_s, dkn_s, dv_s, dcr_s):
        h = pl.program_id(0)
        qn, kn, ccol, crow = _fox_prep(fq[...], fk[...], sm[...], fb_r[...], qg_r[...], kg_r[...], h)
        qn_s[...] = qn
        cc_s[...] = ccol
        v = fv[...]
        dkn_s[...] = jnp.zeros_like(dkn_s)
        dv_s[...] = jnp.zeros_like(dv_s)
        dcr_s[...] = jnp.zeros_like(dcr_s)

        def step(i, carry):
            off = pl.multiple_of(i * bq, bq)
            rows = pl.ds(off, bq)
            _, vjp = jax.vjp(lambda a, b, c, d, e: _fox_block(a, b, c, d, e, off),
                             qn_s[rows, :], kn, v, cc_s[rows, :], crow)
            dq, dk, dv, dcc, dcr = vjp(do[rows, :])
            dqn_s[rows, :] = dq
            dcc_s[rows, :] = dcc
            dkn_s[...] += dk
            dv_s[...] += dv
            dcr_s[...] += dcr
            return carry

        lax.fori_loop(0, t // bq, step, 0)
        _, prep_vjp = jax.vjp(lambda a, b, c, d, e, f: _fox_prep(a, b, c, d, e, f, h),
                              fq[...], fk[...], sm[...], fb_r[...], qg_r[...], kg_r[...])
        g_fq, g_fk, g_sm, g_fb, g_qg, g_kg = prep_vjp((dqn_s[...], dkn_s[...], dcc_s[...], dcr_s[...]))
        dfq[...] = g_fq.astype(dfq.dtype)
        dfk[...] = g_fk.astype(dfk.dtype)
        dfv[...] = dv_s[...].astype(dfv.dtype)

        @pl.when(h == 0)
        def _():
            for r in (dsm, dfb, dqg, dkg):
                r[...] = jnp.zeros_like(r)

        dsm[...] += g_sm
        dfb[...] += g_fb
        dqg[...] += g_qg
        dkg[...] += g_kg

    head = _head(t)
    return pl.pallas_call(
        body, grid=(NF,), name="fox_bwd",
        in_specs=[_pcol(t, FQ), _pcol(t, FK), _pcol(t, FV), _smcol(t), _small(1), _small(1), _small(1), head],
        out_specs=[head, head, head, _small(t), _small(1), _small(1), _small(1)],
        out_shape=[SDS((t, NF * HD), BF16)] * 3 + [SDS((t, HD), F32)] + [SDS((1, HD), F32)] * 3,
        scratch_shapes=[pltpu.VMEM((t, HD), F32), pltpu.VMEM((t, 1), F32), pltpu.VMEM((t, HD), F32),
                        pltpu.VMEM((t, 1), F32), pltpu.VMEM((t, HD), F32), pltpu.VMEM((t, HD), F32),
                        pltpu.VMEM((1, t), F32)],
        compiler_params=_cp("arbitrary"),
    )(p, p, p, p, fb, qg, kg, dmix)


def _mem_attn(mq, mk, mv, qg, kg):
    s = _dot(_rms(mq, qg).astype(BF16), _rms(mk, kg).astype(BF16), NT) * (HD ** -0.5)
    e = jnp.exp(s - lax.stop_gradient(jnp.max(s, axis=1, keepdims=True)))
    p = e / jnp.sum(e, axis=1, keepdims=True)
    return _dot(p.astype(BF16), mv.astype(BF16))


def _mem_fwd(p, mkv, qg, kg):
    t, ml = p.shape[0], mkv.shape[0]

    def body(mq, mk, mv, qg_r, kg_r, o):
        o[...] = _mem_attn(mq[...], mk[...], mv[...], qg_r[...], kg_r[...]).astype(o.dtype)

    return pl.pallas_call(
        body, grid=(NM,), name="mem_fwd",
        in_specs=[_pcol(t, MQ), pl.BlockSpec((ml, HD), lambda h: (0, h)), pl.BlockSpec((ml, HD), lambda h: (0, NM + h)),
                  _small(1), _small(1)],
        out_specs=pl.BlockSpec((t, HD), lambda h: (0, h)), out_shape=SDS((t, NM * HD), BF16),
        compiler_params=_cp("parallel"),
    )(p, mkv, mkv, qg, kg)


def _mem_bwd(p, mkv, qg, kg, dmix):
    t, ml = p.shape[0], mkv.shape[0]

    def body(mq, mk, mv, qg_r, kg_r, do, dmq, dmk, dmv, dqg, dkg):
        _, vjp = jax.vjp(_mem_attn, mq[...], mk[...], mv[...], qg_r[...], kg_r[...])
        g_q, g_k, g_v, g_qg, g_kg = vjp(do[...])
        dmq[...] = g_q.astype(dmq.dtype)
        dmk[...] = g_k
        dmv[...] = g_v

        @pl.when(pl.program_id(0) == 0)
        def _():
            dqg[...] = jnp.zeros_like(dqg)
            dkg[...] = jnp.zeros_like(dkg)

        dqg[...] += g_qg
        dkg[...] += g_kg

    return pl.pallas_call(
        body, grid=(NM,), name="mem_bwd",
        in_specs=[_pcol(t, MQ), pl.BlockSpec((ml, HD), lambda h: (0, h)), pl.BlockSpec((ml, HD), lambda h: (0, NM + h)),
                  _small(1), _small(1), pl.BlockSpec((t, HD), lambda h: (0, NF + NG + h))],
        out_specs=[pl.BlockSpec((t, HD), lambda h: (0, h)), pl.BlockSpec((ml, HD), lambda h: (0, h)),
                   pl.BlockSpec((ml, HD), lambda h: (0, h)), _small(1), _small(1)],
        out_shape=[SDS((t, NM * HD), BF16), SDS((ml, NM * HD), F32), SDS((ml, NM * HD), F32),
                   SDS((1, HD), F32), SDS((1, HD), F32)],
        compiler_params=_cp("arbitrary"),
    )(p, mkv, mkv, qg, kg, dmix)


def _shift_down(x, s):
    if s == 0:
        return x
    return jnp.where(_iota(x.shape, 0) >= s, pltpu.roll(x, s, 0), 0.0)


def _shift_up(x, s):
    if s == 0:
        return x
    n = x.shape[0]
    return jnp.where(_iota(x.shape, 0) < n - s, pltpu.roll(x, n - s, 0), 0.0)


@jax.custom_vjp
def _conv4(x, w0, w1, w2, w3):
    return w0 * _shift_down(x, 3) + w1 * _shift_down(x, 2) + w2 * _shift_down(x, 1) + w3 * x


def _conv4_fwd(x, w0, w1, w2, w3):
    return _conv4(x, w0, w1, w2, w3), (x, w0, w1, w2, w3)


def _conv4_bwd(res, dy):
    x, w0, w1, w2, w3 = res
    dx = w0 * _shift_up(dy, 3) + w1 * _shift_up(dy, 2) + w2 * _shift_up(dy, 1) + w3 * dy
    dws = tuple(jnp.sum(dy * _shift_down(x, 3 - k), axis=0, keepdims=True) for k in range(4))
    return (dx,) + dws


_conv4.defvjp(_conv4_fwd, _conv4_bwd)


def _gdn_prep(gq, gk, gv, sm, taps, alog, dtb, h):
    q, k, v = [_silu(_conv4(x, *taps[4 * j:4 * j + 4])) for j, x in enumerate((gq, gk, gv))]
    q = q * lax.rsqrt(jnp.sum(q * q, axis=-1, keepdims=True) + NORM_EPS) * (HD ** -0.5)
    k = k * lax.rsqrt(jnp.sum(k * k, axis=-1, keepdims=True) + NORM_EPS)
    g = _lane_pick(-jnp.exp(alog) * _softplus(sm + dtb), L_GA + h)
    beta = _lane_pick(_sigmoid(sm), L_GB + h)
    return q, k, v, g, beta


def _hdot(a, b, spec):
    return jnp.einsum(spec, a, b, precision=HI, preferred_element_type=F32)


def _gdn_intra(q, k, v, g, beta):
    n = q.shape[0]
    r, c = _iota((CHUNK, CHUNK), 0), _iota((CHUNK, CHUNK), 1)
    tril, strict = r >= c, r > c
    trilf = jnp.broadcast_to(tril.astype(F32), (n, CHUNK, CHUNK))
    gcm = _hdot(trilf, jnp.broadcast_to(g, (n, CHUNK, CHUNK)), "cij,cjk->cik")
    gcf = _hdot(trilf, jnp.broadcast_to(g, (n, CHUNK, HD)), "cij,cjk->cik")
    lane0 = (_iota((1, 1, CHUNK), 2) == 0).astype(F32)
    gcr = _hdot(jnp.ones((n, CHUNK, CHUNK), F32), gcm * lane0, "cil,cjl->cij")
    decay = jnp.where(tril, jnp.exp(jnp.where(tril, gcm - gcr, 0.0)), 0.0)
    egc = jnp.exp(gcf)
    kb = k * beta
    low = jnp.where(strict, _hdot(kb, k, "cid,cjd->cij") * decay, 0.0)
    inv = (r == c).astype(F32) - low
    pw = low
    for _ in range(5):
        pw = _hdot(pw, pw, "cij,cjk->cik")
        inv = inv + _hdot(inv, pw, "cij,cjk->cik")
    u = _hdot(inv, v * beta, "cij,cjd->cid")
    w = _hdot(inv, kb * egc, "cij,cjd->cid")
    at = jnp.where(tril, _hdot(q, k, "cid,cjd->cij") * decay, 0.0)
    gl = jnp.sum(jnp.broadcast_to(g, (n, CHUNK, HD)), axis=1, keepdims=True)
    return u, w, q * egc, at, k * jnp.exp(gl - gcf), gl


def _gdn_step(s, u, w, qg, at, kd, gl):
    vn = u - jnp.dot(w, s, precision=HI, preferred_element_type=F32)
    o = jnp.dot(qg, s, precision=HI, preferred_element_type=F32) + jnp.dot(at, vn, precision=HI, preferred_element_type=F32)
    s2 = s * jnp.exp(gl) + lax.dot_general(kd, vn, TN, precision=HI, preferred_element_type=F32)
    return o, s2


def _gdn_scratch(nc):
    big = pltpu.VMEM((nc, CHUNK, HD), F32)
    return [big, big, big, pltpu.VMEM((nc, CHUNK, 1), F32), pltpu.VMEM((nc, CHUNK, 1), F32),
            big, big, big, pltpu.VMEM((nc, CHUNK, CHUNK), F32), big, pltpu.VMEM((nc, 1, HD), F32)]


def _gdn_in_specs(t):
    cw = lambda cb: pl.BlockSpec((4, HD), lambda h, cb=cb: (0, cb + h))
    return [_pcol(t, GQ), _pcol(t, GK), _pcol(t, GV), _smcol(t), cw(0), cw(NG), cw(2 * NG), _small(1), _small(1)]


def _taps(wq, wk, wv):
    return tuple(w[k:k + 1, :] for w in (wq, wk, wv) for k in range(4))


def _gdn_stage(vals, refs):
    nc = refs[0].shape[0]
    for v, r in zip(vals, refs):
        r[...] = v.reshape(nc, CHUNK, v.shape[-1])


def _gdn_intra_all(chunked, intra):
    nc = chunked[0].shape[0]
    grp_n = math.gcd(nc, GROUP)

    def grp(i, carry):
        sl = pl.ds(pl.multiple_of(i * grp_n, grp_n), grp_n)
        for r, val in zip(intra, _gdn_intra(*[c[sl] for c in chunked])):
            r[sl] = val
        return carry

    lax.fori_loop(0, nc // grp_n, grp, 0)


def _gdn_fwd(p, conv, alog, dtb):
    t = p.shape[0]
    nc = t // CHUNK

    def body(gq, gk, gv, sm, wq, wk, wv, al, db, o, *scr):
        h = pl.program_id(0)
        chunked, intra = scr[:5], scr[5:]
        _gdn_stage(_gdn_prep(gq[...], gk[...], gv[...], sm[...], _taps(wq, wk, wv), al[...], db[...], h), chunked)
        _gdn_intra_all(chunked, intra)

        def step(c, s):
            oc, s2 = _gdn_step(s, *[r[c] for r in intra])
            o[pl.ds(pl.multiple_of(c * CHUNK, CHUNK), CHUNK), :] = oc
            return s2

        lax.fori_loop(0, nc, step, jnp.zeros((HD, HD), F32))

    return pl.pallas_call(
        body, grid=(NG,), name="gdn_fwd", in_specs=_gdn_in_specs(t),
        out_specs=_head(t), out_shape=SDS((t, NG * HD), F32),
        scratch_shapes=_gdn_scratch(nc), compiler_params=_cp("parallel"),
    )(p, p, p, p, conv, conv, conv, alog, dtb)


def _gdn_bwd(p, conv, alog, dtb, do_raw):
    t = p.shape[0]
    nc = t // CHUNK

    def body(gq, gk, gv, sm, wq, wk, wv, al, db, do, dgq, dgk, dgv, dsm, dwq, dwk, dwv, dal, ddb, *scr):
        h = pl.program_id(0)
        chunked, intra, states = scr[:5], scr[5:11], scr[11]
        _gdn_stage(_gdn_prep(gq[...], gk[...], gv[...], sm[...], _taps(wq, wk, wv), al[...], db[...], h), chunked)
        _gdn_intra_all(chunked, intra)

        def fwd(c, s):
            states[c] = s
            return _gdn_step(s, *[r[c] for r in intra])[1]

        lax.fori_loop(0, nc, fwd, jnp.zeros((HD, HD), F32))

        def bwd(i, ds):
            c = nc - 1 - i
            _, vjp = jax.vjp(_gdn_step, states[c], *[r[c] for r in intra])
            grads = vjp((do[pl.ds(pl.multiple_of(c * CHUNK, CHUNK), CHUNK), :], ds))
            for r, gval in zip(intra, grads[1:]):
                r[c] = gval
            return grads[0]

        lax.fori_loop(0, nc, bwd, jnp.zeros((HD, HD), F32))

        grp_n = math.gcd(nc, GROUP)

        def grp(i, carry):
            sl = pl.ds(pl.multiple_of(i * grp_n, grp_n), grp_n)
            _, vjp = jax.vjp(_gdn_intra, *[r[sl] for r in chunked])
            for r, gval in zip(chunked, vjp(tuple(r[sl] for r in intra))):
                r[sl] = gval
            return carry

        lax.fori_loop(0, nc // grp_n, grp, 0)
        _, prep_vjp = jax.vjp(
            lambda *a: _gdn_prep(*a, h), gq[...], gk[...], gv[...], sm[...], _taps(wq, wk, wv), al[...], db[...])
        grads = prep_vjp(tuple(r[...].reshape(t, r.shape[-1]) for r in chunked))
        for r, gval in zip((dgq, dgk, dgv), grads[:3]):
            r[...] = gval.astype(r.dtype)
        for j, r in enumerate((dwq, dwk, dwv)):
            for k in range(4):
                r[k:k + 1, :] = grads[4][4 * j + k]

        @pl.when(h == 0)
        def _():
            for r in (dsm, dal, ddb):
                r[...] = jnp.zeros_like(r)

        dsm[...] += grads[3]
        dal[...] += grads[5]
        ddb[...] += grads[6]

    head = _head(t)
    taps = pl.BlockSpec((4, HD), lambda h: (0, h))
    return pl.pallas_call(
        body, grid=(NG,), name="gdn_bwd", in_specs=_gdn_in_specs(t) + [head],
        out_specs=[head, head, head, _small(t), taps, taps, taps, _small(1), _small(1)],
        out_shape=[SDS((t, NG * HD), BF16)] * 3 + [SDS((t, HD), F32)] + [SDS((4, NG * HD), F32)] * 3 + [SDS((1, HD), F32)] * 2,
        scratch_shapes=_gdn_scratch(nc) + [pltpu.VMEM((nc, HD, HD), F32)], compiler_params=_cp("arbitrary"),
    )(p, p, p, p, conv, conv, conv, alog, dtb, do_raw)


def _gdn_post(o, z, gain):
    return (jnp.concatenate(
        [_rms(o[:, h * HD:(h + 1) * HD], gain) * _silu(z[:, h * HD:(h + 1) * HD]) for h in range(NG)], axis=1),)


def _place():
    return lax.axis_index("x"), lax.axis_index("y"), lax.axis_index("c")


def _all_gather(name, shard):
    def body(x_ref, out_ref, send_sems, recv_sems, local_sem):
        x, y, c = _place()
        me, sibling = (x, y, c), (x, y, 1 - c)
        chips = [(1 - x, y), (x, 1 - y), (1 - x, 1 - y)]

        def blk(px, py, pc):
            return out_ref.at[4 * px + 2 * py + pc]

        def copy(k, block, to, src=None):
            return pltpu.make_async_remote_copy(
                src_ref=blk(*block) if src is None else src, dst_ref=blk(*block),
                send_sem=send_sems.at[k], recv_sem=recv_sems.at[k], device_id=to, device_id_type=MESH)

        mine = pltpu.make_async_copy(x_ref, blk(*me), local_sem)
        mine.start()
        first = [copy(0, me, sibling, src=x_ref)]
        first += [copy(1 + j, me, (*chip, c), src=x_ref) for j, chip in enumerate(chips)]
        for cp in first:
            cp.start()
        passed = [copy(4 + j, (*chip, c), sibling) for j, chip in enumerate(chips)]
        for j, chip in enumerate(chips):
            copy(1 + j, (*chip, c), me).wait_recv()
            passed[j].start()
        copy(0, sibling, me).wait_recv()
        for j, chip in enumerate(chips):
            copy(4 + j, (*chip, 1 - c), me).wait_recv()
        for cp in first + passed:
            cp.wait_send()
        mine.wait()

    return pl.pallas_call(
        body, name=name, out_shape=SDS((N_DEV,) + shard.shape, shard.dtype),
        in_specs=[pl.BlockSpec(memory_space=pltpu.HBM)], out_specs=pl.BlockSpec(memory_space=pltpu.HBM),
        scratch_shapes=[pltpu.SemaphoreType.DMA((7,)), pltpu.SemaphoreType.DMA((7,)), pltpu.SemaphoreType.DMA],
    )(shard)


def _scatter_exchange(name, full):
    def body(g_ref, out_ref, send_sems, recv_sems, local_sem):
        x, y, c = _place()
        me = 4 * x + 2 * y + c
        mine = pltpu.make_async_copy(g_ref.at[me], out_ref.at[me], local_sem)
        mine.start()
        sends, recvs = [], []
        for k in range(1, N_DEV):
            px = 1 - x if k & 4 else x
            py = 1 - y if k & 2 else y
            pc = 1 - c if k & 1 else c
            peer = 4 * px + 2 * py + pc
            sends.append(pltpu.make_async_remote_copy(
                src_ref=g_ref.at[peer], dst_ref=out_ref.at[me], send_sem=send_sems.at[k - 1],
                recv_sem=recv_sems.at[k - 1], device_id=(px, py, pc), device_id_type=MESH))
            recvs.append(pltpu.make_async_remote_copy(
                src_ref=g_ref.at[me], dst_ref=out_ref.at[peer], send_sem=send_sems.at[k - 1],
                recv_sem=recv_sems.at[k - 1], device_id=(px, py, pc), device_id_type=MESH))
        for cp in sends:
            cp.start()
        for cp in recvs:
            cp.wait_recv()
        for cp in sends:
            cp.wait_send()
        mine.wait()

    return pl.pallas_call(
        body, name=name, out_shape=SDS(full.shape, full.dtype),
        in_specs=[pl.BlockSpec(memory_space=pltpu.HBM)], out_specs=pl.BlockSpec(memory_space=pltpu.HBM),
        scratch_shapes=[pltpu.SemaphoreType.DMA((7,)), pltpu.SemaphoreType.DMA((7,)), pltpu.SemaphoreType.DMA],
    )(full)


def _sum_blocks(name, parts):
    _, r, c = parts.shape
    tr = 64 if r % 64 == 0 else r

    def body(x, o):
        acc = x[0].astype(F32)
        for d in range(1, N_DEV):
            acc = acc + x[d].astype(F32)
        o[...] = acc

    return pl.pallas_call(
        body, grid=(r // tr,), name=name, in_specs=[pl.BlockSpec((N_DEV, tr, c), lambda i: (0, i, 0))],
        out_specs=pl.BlockSpec((tr, c), lambda i: (i, 0)), out_shape=SDS((r, c), F32), compiler_params=_cp("parallel"),
    )(parts)


def _reduce_scatter(name, full):
    return _sum_blocks(name + "_sum", _scatter_exchange(name, full))


def _all_reduce_small(name, x, reduce):
    m_per, n = x.shape

    def body(x_ref, out_ref, send_sems, recv_sems, local_sem):
        px, py, pc = _place()
        me, sibling = (px, py, pc), (px, py, 1 - pc)
        chips = [(1 - px, py), (px, 1 - py), (1 - px, 1 - py)]
        buf = out_ref

        def rows(qx, qy, qc):
            return buf.at[pl.ds((4 * qx + 2 * qy + qc) * m_per, m_per), :]

        def copy(k, block, to, src=None):
            return pltpu.make_async_remote_copy(
                src_ref=rows(*block) if src is None else src, dst_ref=rows(*block),
                send_sem=send_sems.at[k], recv_sem=recv_sems.at[k], device_id=to, device_id_type=MESH)

        mine = pltpu.make_async_copy(x_ref, rows(*me), local_sem)
        mine.start()
        first = [copy(0, me, sibling, src=x_ref)]
        first += [copy(1 + j, me, (*chip, pc), src=x_ref) for j, chip in enumerate(chips)]
        for cp in first:
            cp.start()
        passed = [copy(4 + j, (*chip, pc), sibling) for j, chip in enumerate(chips)]
        for j, chip in enumerate(chips):
            copy(1 + j, (*chip, pc), me).wait_recv()
            passed[j].start()
        copy(0, sibling, me).wait_recv()
        for j, chip in enumerate(chips):
            copy(4 + j, (*chip, 1 - pc), me).wait_recv()
        for cp in first + passed:
            cp.wait_send()
        mine.wait()

    gathered = pl.pallas_call(
        body, name=name, out_shape=SDS((N_DEV * m_per, n), x.dtype),
        in_specs=[pl.BlockSpec(memory_space=pltpu.VMEM)], out_specs=pl.BlockSpec(memory_space=pltpu.VMEM),
        scratch_shapes=[pltpu.SemaphoreType.DMA((7,)), pltpu.SemaphoreType.DMA((7,)), pltpu.SemaphoreType.DMA],
    )(x)
    if not reduce:
        return gathered
    return _sum_blocks(name + "_sum", gathered.reshape(N_DEV, m_per, n))


def _adamw(w, g, m, v):
    m = ADAM_B1 * m + (1.0 - ADAM_B1) * g
    v = ADAM_B2 * v + (1.0 - ADAM_B2) * (g * g)
    m_hat = m / (1.0 - ADAM_B1 ** ADAM_STEP)
    v_hat = v / (1.0 - ADAM_B2 ** ADAM_STEP)
    return -ADAM_LR * (m_hat / (jnp.sqrt(v_hat) + ADAM_EPS) + ADAM_WD * w), m, v


def _adamw_call(name, w, g, m, v):
    r, c = w.shape
    tm = 64 if r % 64 == 0 else r
    return _rowwise(name, _adamw, [w, g, m, v], [], [(c, F32)] * 3, tm)


_IN_COLS = 5906


def _perm_in(w):
    pad = jnp.zeros((w.shape[0], PC - _IN_COLS), w.dtype)
    return jnp.concatenate([w[:, :2304], w[:, 2310:4614], w[:, 4614:5382], w[:, 5394:5906], w[:, 2304:2310],
                            w[:, 5382:5394], pad], axis=1)


def _unperm_in(g):
    return jnp.concatenate([g[:, :2304], g[:, 5888:5894], g[:, 2304:4608], g[:, 4608:5376], g[:, 5894:5906],
                            g[:, 5376:5888]], axis=1)


def _lanes(v, at):
    return jnp.pad(v, ((0, 0), (at, HD - at - v.shape[1])))


_PACK = ("norm_mix", "mem_norm", "norm_ffn", "gdn_conv", "fox_q_norm", "fox_k_norm", "gdn_out_norm", "mem_q_norm",
         "mem_k_norm", "fox_f_bias", "gdn_a_log", "gdn_dt_bias", "loss")


def _pack(vals):
    parts = [vals[n].reshape(-1, HD) for n in _PACK]
    used = sum(p.shape[0] for p in parts)
    buf = jnp.concatenate(parts + [jnp.zeros((-used % 8, HD), F32)], axis=0)
    return buf, [(n, p.shape[0]) for n, p in zip(_PACK, parts)]


def _unpack(buf, layout):
    out, at = {}, 0
    for n, rows in layout:
        out[n] = buf[at:at + rows]
        at += rows
    return out


def kernel(x, mem, norm_mix, w_in, fox_f_bias, fox_q_norm, fox_k_norm, gdn_conv, gdn_a_log, gdn_dt_bias, gdn_out_norm, mem_norm, w_mem_kv, mem_q_norm, mem_k_norm, w_out, norm_ffn, w_gate_up, w_down, loss_target, m_norm_mix, m_w_in, m_fox_f_bias, m_fox_q_norm, m_fox_k_norm, m_gdn_conv, m_gdn_a_log, m_gdn_dt_bias, m_gdn_out_norm, m_mem_norm, m_w_mem_kv, m_mem_q_norm, m_mem_k_norm, m_w_out, m_norm_ffn, m_w_gate_up, m_w_down, v_norm_mix, v_w_in, v_fox_f_bias, v_fox_q_norm, v_fox_k_norm, v_gdn_conv, v_gdn_a_log, v_gdn_dt_bias, v_gdn_out_norm, v_mem_norm, v_w_mem_kv, v_mem_q_norm, v_mem_k_norm, v_w_out, v_norm_ffn, v_w_gate_up, v_w_down):
    args = dict(locals())
    d = x.shape[2]
    me = 4 * lax.axis_index("x") + 2 * lax.axis_index("y") + lax.axis_index("c")

    w_in_all = _all_gather("ag_w_in", _perm_in(w_in[0]).astype(BF16)).reshape(d, PC)
    w_kv_all = _all_gather("ag_w_kv", w_mem_kv[0].astype(BF16)).reshape(d, 2 * NM * HD)
    w_out_all = _all_gather("ag_w_out", w_out[0].astype(BF16)).reshape(-1, d)
    wgu = _all_gather("ag_w_gu", w_gate_up[0].astype(BF16))
    w_down_all = _all_gather("ag_w_down", w_down[0].astype(BF16)).reshape(-1, d)
    cshard = gdn_conv[0].shape[1]
    conv_pad = jnp.pad(gdn_conv[0], ((0, 4), (0, 3 * HD - cshard)))
    conv_all = _all_reduce_small("ag_conv", conv_pad, False).reshape(N_DEV, 8, 3 * HD)[:, :4, :cshard]
    conv_all = conv_all.transpose(1, 0, 2).reshape(4, N_DEV * cshard)

    grad_x, loss_local, big_grads, small_grads = _local_step(
        x[0], mem[0], loss_target[0], norm_mix, fox_f_bias, fox_q_norm, fox_k_norm, gdn_a_log, gdn_dt_bias,
        gdn_out_norm, mem_norm, mem_q_norm, mem_k_norm, norm_ffn, w_in_all, w_kv_all, w_out_all, wgu, w_down_all, conv_all)

    grads = {n: _reduce_scatter("rs_" + n, g.reshape((N_DEV, g.shape[-2] * g.shape[0] // N_DEV, g.shape[-1]) if g.ndim == 3
                                                     else (N_DEV, g.shape[0] // N_DEV, g.shape[1])))
             for n, g in big_grads.items()}
    grads["w_in"] = _unperm_in(grads["w_in"])
    small_grads["loss"] = jnp.broadcast_to(loss_local, (1, HD))
    packed, layout = _pack(small_grads)
    small = _unpack(_all_reduce_small("ar_small", packed, True), layout)
    loss = small["loss"][0, 0]
    six = {"fox_f_bias": L_FF, "gdn_a_log": L_GA, "gdn_dt_bias": L_GA}
    for n, rows_n in layout[:-1]:
        gsm = small[n]
        if n == "gdn_conv":
            gsm = lax.dynamic_slice(gsm.reshape(4, N_DEV * cshard), (0, me * cshard), (4, cshard))[None]
        elif n in six:
            gsm = gsm[:, six[n]:six[n] + 6]
        else:
            gsm = gsm.reshape(1, rows_n * HD)
        grads[n] = gsm

    names = ['norm_mix', 'w_in', 'fox_f_bias', 'fox_q_norm', 'fox_k_norm', 'gdn_conv', 'gdn_a_log', 'gdn_dt_bias',
             'gdn_out_norm', 'mem_norm', 'w_mem_kv', 'mem_q_norm', 'mem_k_norm', 'w_out', 'norm_ffn', 'w_gate_up', 'w_down']
    big = ("w_in", "w_mem_kv", "w_out", "w_gate_up", "w_down")
    delta, new_m, new_v = {}, {}, {}
    for n in big:
        delta[n], new_m[n], new_v[n] = [a[None] for a in _adamw_call(
            "adamw_" + n, args[n][0], grads[n], args["m_" + n][0], args["v_" + n][0])]
        grads[n] = grads[n][None]

    def flat(a):
        a = a.reshape(1, -1)
        return jnp.pad(a, ((0, 0), (0, -a.shape[1] % HD))).reshape(-1, HD)

    smalls = [n for n in names if n not in big]
    pk = lambda pre: jnp.concatenate([flat(grads[n] if pre == "g" else args[pre + n]) for n in smalls], axis=0)
    cat = [pk(""), pk("g"), pk("m_"), pk("v_")]
    padr = -cat[0].shape[0] % 8
    cat = [jnp.pad(a, ((0, padr), (0, 0))) for a in cat]
    res = _adamw_call("adamw_small", *cat)
    at = 0
    for n in smalls:
        shape = args[n].shape
        size = math.prod(shape)
        nrow = -(-size // HD)
        for dst, src in zip((delta, new_m, new_v), res):
            dst[n] = src[at:at + nrow].reshape(-1)[:size].reshape(shape)
        at += nrow

    return (loss, grad_x[None], *[grads[n] for n in names], *[delta[n] for n in names],
            *[new_m[n] for n in names], *[new_v[n] for n in names])


def _local_step(xs, ms, tgt, norm_mix, fox_f_bias, fox_q_norm, fox_k_norm, gdn_a_log, gdn_dt_bias, gdn_out_norm,
                mem_norm, mem_q_norm, mem_k_norm, norm_ffn, w_in_all, w_kv_all, w_out_all, wgu, w_down_all, conv_all):
    t, d = xs.shape
    ffw = wgu.shape[2]
    bq = min(t, 256)
    fb, alog, dtb = _lanes(fox_f_bias, L_FF), _lanes(gdn_a_log, L_GA), _lanes(gdn_dt_bias, L_GA)

    rms1 = lambda a, g: (_rms(a, g),)
    (u,) = _rowwise("norm_mix", rms1, [xs], [norm_mix], [(d, BF16)], min(t, 256))
    p = _matmul("proj_in", u, w_in_all, NN, F32, 1024, 768)
    (mem_n,) = _rowwise("norm_mem", rms1, [ms], [mem_norm], [(d, BF16)], ms.shape[0])
    mkv = _matmul("proj_mem", mem_n, w_kv_all, NN, F32, 256, 512)
    o_fox = _fox_fwd(p, fb, fox_q_norm, fox_k_norm, bq)
    o_gdn_raw = _gdn_fwd(p, conv_all, alog, dtb)
    zrow = (p, NG * HD, GZ * HD // (NG * HD))
    (o_gdn,) = _rowwise("gdn_post", _gdn_post, [o_gdn_raw, zrow], [gdn_out_norm], [(NG * HD, BF16)], min(t, 256))
    o_mem = _mem_fwd(p, mkv, mem_q_norm, mem_k_norm)
    mix = jnp.concatenate([o_fox, o_gdn, o_mem], axis=1)
    h1 = _matmul("proj_out", mix, w_out_all, NN, F32, 1024, 512, residual=xs)
    (h1n,) = _rowwise("norm_ffn", rms1, [h1], [norm_ffn], [(d, BF16)], min(t, 256))
    gu, act = _ffn_up(h1n, wgu.reshape(2, 4, d, ffw))
    dy, dyb, lsum = _ffn_down_loss(act, w_down_all, h1, tgt)
    loss_local = (0.5 / d) * jnp.sum(lsum[::8, ::HD])

    dgu = _ffn_down_bwd(dyb, w_down_all.reshape(4, ffw, d), gu).reshape(8, t, ffw)
    g_w_down = _matmul("grad_w_down", act, dyb, TN, BF16, 512, 512)
    dh1n = _ffn_up_bwd_x(dgu, wgu)
    g_w_gu = _ffn_up_bwd_w(h1n, dgu)
    rms2 = lambda a, g: (_rms(a, g), a)
    dh1, g_norm_ffn = _rowwise_vjp("norm_ffn_bwd", rms2, [h1], [norm_ffn], [dh1n, dy], [F32], min(t, 256))
    dh1b = dh1.astype(BF16)

    dmix = _matmul("proj_out_bwd_x", dh1b, w_out_all, NT, F32, 1024, 512)
    g_w_out = _matmul("grad_w_out", mix, dh1b, TN, BF16, 512, 512)
    dfq, dfk, dfv, dsm_fox, g_fb, g_fqn, g_fkn = _fox_bwd(p, fb, fox_q_norm, fox_k_norm, dmix, bq)
    do_raw, dgz, g_gon = _rowwise_vjp("gdn_post_bwd", _gdn_post, [o_gdn_raw, zrow], [gdn_out_norm],
                                      [(dmix, NG * HD, 1)], [F32, BF16], min(t, 256))
    dgq, dgk, dgv, dsm_gdn, dwq, dwk, dwv, g_alog, g_dtb = _gdn_bwd(p, conv_all, alog, dtb, do_raw)
    dmq, dmk, dmv, g_mqn, g_mkn = _mem_bwd(p, mkv, mem_q_norm, mem_k_norm, dmix)
    dmkv = jnp.concatenate([dmk, dmv], axis=1).astype(BF16)
    dmem_n = _matmul("proj_mem_bwd_x", dmkv, w_kv_all, NT, F32, 256, 512)
    g_w_kv = _matmul("grad_w_kv", mem_n, dmkv, TN, BF16, 512, 512)
    g_mem_norm = _rowwise_vjp("norm_mem_bwd", rms1, [ms], [mem_norm], [dmem_n], [], ms.shape[0])[0]
    dp = jnp.concatenate([dfq, dfk, dfv, dgq, dgk, dgv, dgz, dmq, (dsm_fox + dsm_gdn).astype(BF16),
                          jnp.zeros((t, HD), BF16)], axis=1)
    du = _matmul("proj_in_bwd_x", dp, w_in_all, NT, F32, 512, 512)
    g_w_in = _matmul("grad_w_in", u, dp, TN, BF16, 512, 768)
    grad_x, g_norm_mix = _rowwise_vjp("norm_mix_bwd", rms2, [xs], [norm_mix], [du, dh1], [F32], min(t, 256))

    big_grads = {"w_in": g_w_in, "w_mem_kv": g_w_kv, "w_out": g_w_out, "w_gate_up": g_w_gu, "w_down": g_w_down}
    small_grads = {
        "norm_mix": g_norm_mix, "mem_norm": g_mem_norm, "norm_ffn": g_norm_ffn,
        "gdn_conv": jnp.concatenate([dwq, dwk, dwv], axis=1),
        "fox_q_norm": g_fqn, "fox_k_norm": g_fkn, "gdn_out_norm": g_gon, "mem_q_norm": g_mqn, "mem_k_norm": g_mkn,
        "fox_f_bias": g_fb, "gdn_a_log": g_alog, "gdn_dt_bias": g_dtb}
    return grad_x, loss_local, big_grads, small_grads
```

```python
import functools
import math

import jax
import jax.numpy as jnp
from jax import lax
from jax.experimental import pallas as pl
from jax.experimental.pallas import tpu as pltpu

F32 = jnp.float32
BF16 = jnp.bfloat16
HI = lax.Precision.HIGHEST
SDS = jax.ShapeDtypeStruct

N_DEV = 8
HD = 128
NF, NG, NM = 6, 6, 4
CHUNK = 64
GROUP = 4
NORM_EPS = 1e-6
FQ, FK, FV, GQ, GK, GV, GZ, MQ, SM, NPB = 0, 6, 12, 18, 24, 30, 36, 42, 46, 48
PC = NPB * HD
L_FF, L_GA, L_GB = 0, 6, 12
VMEM_LIMIT = 56 * 1024 * 1024

ADAM_LR, ADAM_B1, ADAM_B2, ADAM_EPS, ADAM_WD, ADAM_STEP = 0.001, 0.9, 0.999, 1e-08, 0.01, 10

NN = (((1,), (0,)), ((), ()))
NT = (((1,), (1,)), ((), ()))
TN = (((0,), (0,)), ((), ()))
MESH = pl.DeviceIdType.MESH


def _cp(*sem):
    return pltpu.CompilerParams(dimension_semantics=tuple(sem) if sem else None, vmem_limit_bytes=VMEM_LIMIT)


def _dot(a, b, dims=NN):
    return lax.dot_general(a, b, dims, preferred_element_type=F32)


def _bdot(a, b):
    return _dot(a.astype(BF16), b.astype(BF16))


def _iota(shape, axis):
    return lax.broadcasted_iota(jnp.int32, shape, axis)


def _rms(x, gain):
    return x * lax.rsqrt(jnp.mean(x * x, axis=-1, keepdims=True) + NORM_EPS) * gain


def _sigmoid(x):
    z = jnp.exp(-jnp.abs(x))
    return jnp.where(x >= 0, 1.0 / (1.0 + z), z / (1.0 + z))


def _silu(x):
    return x * _sigmoid(x)


def _softplus(x):
    return jnp.maximum(x, 0.0) + jnp.log(1.0 + jnp.exp(-jnp.abs(x)))


def _lane_pick(x, lane):
    oh = (_iota((1, x.shape[-1]), 1) == lane).astype(F32)
    return jnp.sum(x * oh, axis=-1, keepdims=True)


def _cumsum_rows(x):
    tril = (_iota((HD, HD), 0) >= _iota((HD, HD), 1)).astype(F32)
    carry = jnp.zeros((1, x.shape[1]), F32)
    outs = []
    for b in range(x.shape[0] // HD):
        blk = x[b * HD:(b + 1) * HD]
        outs.append(jnp.dot(tril, blk, precision=HI, preferred_element_type=F32) + carry)
        carry = carry + jnp.sum(blk, axis=0, keepdims=True)
    return jnp.concatenate(outs, axis=0)


def _row_spec(r, tm):
    if isinstance(r, tuple):
        arr, width, cb = r
        return arr, pl.BlockSpec((tm, width), lambda i, cb=cb: (i, cb))
    return r, pl.BlockSpec((tm, r.shape[1]), lambda i: (i, 0))


def _rowwise(name, fn, rows, consts, outs, tm):
    arrs, specs = zip(*[_row_spec(r, tm) for r in rows])
    n_rows = arrs[0].shape[0]
    nr, nc = len(rows), len(consts)

    def body(*refs):
        res = fn(*[r[...] for r in refs[:nr + nc]])
        for o, v in zip(refs[nr + nc:], res):
            o[...] = v.astype(o.dtype)

    return pl.pallas_call(
        body, grid=(n_rows // tm,), name=name,
        in_specs=list(specs) + [pl.BlockSpec(c.shape, lambda i: (0, 0)) for c in consts],
        out_specs=[pl.BlockSpec((tm, w), lambda i: (i, 0)) for w, _ in outs],
        out_shape=[SDS((n_rows, w), dt) for w, dt in outs],
        compiler_params=_cp("parallel"),
    )(*arrs, *consts)


def _rowwise_vjp(name, fn, rows, consts, cts, grad_dtypes, tm):
    arrs, specs = zip(*[_row_spec(r, tm) for r in rows])
    ct_arrs, ct_specs = zip(*[_row_spec(r, tm) for r in cts])
    n_rows = arrs[0].shape[0]
    nr, nc, nct, ng = len(rows), len(consts), len(cts), len(grad_dtypes)
    widths = [s.block_shape[1] for s in specs[:ng]]

    def body(*refs):
        vals = [r[...].astype(F32) for r in refs[:nr + nc]]
        ctv = tuple(r[...].astype(F32) for r in refs[nr + nc:nr + nc + nct])
        _, vjp = jax.vjp(fn, *vals)
        grads = vjp(ctv)
        outs = refs[nr + nc + nct:]
        for o, g in zip(outs[:ng], grads[:ng]):
            o[...] = g.astype(o.dtype)

        @pl.when(pl.program_id(0) == 0)
        def _():
            for o in outs[ng:]:
                o[...] = jnp.zeros_like(o)

        for o, g in zip(outs[ng:], grads[nr:]):
            o[...] += g

    return pl.pallas_call(
        body, grid=(n_rows // tm,), name=name,
        in_specs=list(specs) + [pl.BlockSpec(c.shape, lambda i: (0, 0)) for c in consts] + list(ct_specs),
        out_specs=[pl.BlockSpec((tm, w), lambda i: (i, 0)) for w in widths]
        + [pl.BlockSpec(c.shape, lambda i: (0, 0)) for c in consts],
        out_shape=[SDS((n_rows, w), dt) for w, dt in zip(widths, grad_dtypes)] + [SDS(c.shape, F32) for c in consts],
        compiler_params=_cp("arbitrary"),
    )(*arrs, *consts, *ct_arrs)


def _tile(n, pref):
    t = min(n, pref)
    while n % t or (t % HD and t != n):
        t -= 1
    return t


def _matmul(name, a, b, dims, out_dtype, tm, tn, residual=None):
    ta, tb = dims == TN, dims == NT
    m = a.shape[1] if ta else a.shape[0]
    k = a.shape[0] if ta else a.shape[1]
    n = b.shape[0] if tb else b.shape[1]
    tm, tn = _tile(m, tm), _tile(n, tn)

    def body(*refs):
        acc = _dot(refs[0][...], refs[1][...], dims)
        if residual is not None:
            acc = acc + refs[2][...]
        refs[-1][...] = acc.astype(out_dtype)

    in_specs = [pl.BlockSpec((k, tm), lambda i, j: (0, i)) if ta else pl.BlockSpec((tm, k), lambda i, j: (i, 0)),
                pl.BlockSpec((tn, k), lambda i, j: (j, 0)) if tb else pl.BlockSpec((k, tn), lambda i, j: (0, j))]
    ops = [a, b]
    if residual is not None:
        in_specs.append(pl.BlockSpec((tm, tn), lambda i, j: (i, j)))
        ops.append(residual)
    return pl.pallas_call(
        body, grid=(m // tm, n // tn), name=name, in_specs=in_specs,
        out_specs=pl.BlockSpec((tm, tn), lambda i, j: (i, j)), out_shape=SDS((m, n), out_dtype),
        compiler_params=_cp("parallel", "parallel"),
    )(*ops)


def _ffn_up(h1n, wgu):
    t, d = h1n.shape
    w = wgu.shape[3]
    tm = _tile(t, 512)

    def body(a, b, gu, act):
        x = a[...]
        g = _dot(x, b[0])
        u = _dot(x, b[1])
        gu[0] = g.astype(BF16)
        gu[1] = u.astype(BF16)
        act[...] = (_silu(g) * u).astype(BF16)

    return pl.pallas_call(
        body, grid=(4, t // tm), name="ffn_up",
        in_specs=[pl.BlockSpec((tm, d), lambda j, i: (i, 0)), pl.BlockSpec((2, None, d, w), lambda j, i: (0, j, 0, 0))],
        out_specs=[pl.BlockSpec((2, None, tm, w), lambda j, i: (0, j, i, 0)), pl.BlockSpec((tm, w), lambda j, i: (i, j))],
        out_shape=[SDS((2, 4, t, w), BF16), SDS((t, 4 * w), BF16)],
        compiler_params=_cp("parallel", "parallel"),
    )(h1n, wgu)


def _ffn_down_loss(act, wdown, h1, target):
    t, f = act.shape
    d = wdown.shape[1]
    tm, tn = _tile(t, 512), _tile(d, 512)

    def body(a, b, h, tg, dy, dyb, ls):
        e = _dot(a[...], b[...]) + h[...] - tg[...]
        g = e * (1.0 / d)
        dy[...] = g
        dyb[...] = g.astype(BF16)
        ls[...] = jnp.broadcast_to(jnp.sum(e * e), (8, HD))

    return pl.pallas_call(
        body, grid=(t // tm, d // tn), name="ffn_down_loss",
        in_specs=[pl.BlockSpec((tm, f), lambda i, j: (i, 0)), pl.BlockSpec((f, tn), lambda i, j: (0, j)),
                  pl.BlockSpec((tm, tn), lambda i, j: (i, j)), pl.BlockSpec((tm, tn), lambda i, j: (i, j))],
        out_specs=[pl.BlockSpec((tm, tn), lambda i, j: (i, j)), pl.BlockSpec((tm, tn), lambda i, j: (i, j)),
                   pl.BlockSpec((8, HD), lambda i, j: (i, j))],
        out_shape=[SDS((t, d), F32), SDS((t, d), BF16), SDS((8 * (t // tm), HD * (d // tn)), F32)],
        compiler_params=_cp("parallel", "parallel"),
    )(act, wdown, h1, target)


def _ffn_down_bwd(dyb, wdown4, gu):
    t, d = dyb.shape
    w = wdown4.shape[1]
    tm = _tile(t, 512)

    def body(a, b, gu_ref, out):
        da = _dot(a[...], b[...], NT)
        g = gu_ref[0].astype(F32)
        u = gu_ref[1].astype(F32)
        s = _sigmoid(g)
        out[0] = (da * u * (s * (1.0 + g * (1.0 - s)))).astype(BF16)
        out[1] = (da * g * s).astype(BF16)

    return pl.pallas_call(
        body, grid=(4, t // tm), name="ffn_down_bwd",
        in_specs=[pl.BlockSpec((tm, d), lambda j, i: (i, 0)), pl.BlockSpec((None, w, d), lambda j, i: (j, 0, 0)),
                  pl.BlockSpec((2, None, tm, w), lambda j, i: (0, j, i, 0))],
        out_specs=pl.BlockSpec((2, None, tm, w), lambda j, i: (0, j, i, 0)),
        out_shape=SDS((2, 4, t, w), BF16),
        compiler_params=_cp("parallel", "parallel"),
    )(dyb, wdown4, gu)


def _ffn_up_bwd_x(dgu, wgu):
    _, t, w = dgu.shape
    d = wgu.shape[1]
    tm = _tile(t, 512)

    def body(a, b, out):
        @pl.when(pl.program_id(1) == 0)
        def _():
            out[...] = jnp.zeros_like(out)
        out[...] += _dot(a[...], b[...], NT)

    return pl.pallas_call(
        body, grid=(t // tm, 8), name="ffn_up_bwd_x",
        in_specs=[pl.BlockSpec((None, tm, w), lambda i, j: (j, i, 0)), pl.BlockSpec((None, d, w), lambda i, j: (j, 0, 0))],
        out_specs=pl.BlockSpec((tm, d), lambda i, j: (i, 0)), out_shape=SDS((t, d), F32),
        compiler_params=_cp("parallel", "arbitrary"),
    )(dgu, wgu)


def _ffn_up_bwd_w(h1n, dgu):
    _, t, w = dgu.shape
    d = h1n.shape[1]
    tm = _tile(d, 512)

    def body(a, b, out):
        out[...] = _dot(a[...], b[...], TN).astype(BF16)

    return pl.pallas_call(
        body, grid=(8, d // tm), name="ffn_up_bwd_w",
        in_specs=[pl.BlockSpec((t, tm), lambda j, i: (0, i)), pl.BlockSpec((None, t, w), lambda j, i: (j, 0, 0))],
        out_specs=pl.BlockSpec((None, tm, w), lambda j, i: (j, i, 0)), out_shape=SDS((8, d, w), BF16),
        compiler_params=_cp("parallel", "parallel"),
    )(h1n, dgu)


def _fox_prep(fq, fk, sm, fb, qg, kg, h):
    qn = _rms(fq, qg)
    kn = _rms(fk, kg)
    c = _cumsum_rows(-_softplus(-(sm + fb)))
    ccol = _lane_pick(c, L_FF + h)
    crow = jnp.sum(c.T * (_iota((HD, 1), 0) == L_FF + h).astype(F32), axis=0, keepdims=True)
    return qn, kn, ccol, crow


def _fox_block(q, k, v, cc, cr, off):
    s = _dot(q.astype(BF16), k.astype(BF16), NT) * (HD ** -0.5) + cc - cr
    s = jnp.where(_iota(s.shape, 1) <= _iota(s.shape, 0) + off, s, -1e30)
    e = jnp.exp(s - lax.stop_gradient(jnp.max(s, axis=1, keepdims=True)))
    p = e / jnp.sum(e, axis=1, keepdims=True)
    return _dot(p.astype(BF16), v.astype(BF16))


ONE_BUFFER = pl.Buffered(1)


def _pcol(t, cb):
    return pl.BlockSpec((t, HD), lambda h, cb=cb: (0, cb + h), pipeline_mode=ONE_BUFFER)


def _smcol(t):
    return pl.BlockSpec((t, HD), lambda h: (0, SM), pipeline_mode=ONE_BUFFER)


def _head(t):
    return pl.BlockSpec((t, HD), lambda h: (0, h), pipeline_mode=ONE_BUFFER)


def _small(n):
    return pl.BlockSpec((n, HD), lambda h: (0, 0), pipeline_mode=ONE_BUFFER)


def _fox_fwd(p, fb, qg, kg, bq):
    t = p.shape[0]

    def body(fq, fk, fv, sm, fb_r, qg_r, kg_r, o, qn_s, cc_s):
        h = pl.program_id(0)
        qn, kn, ccol, crow = _fox_prep(fq[...], fk[...], sm[...], fb_r[...], qg_r[...], kg_r[...], h)
        qn_s[...] = qn
        cc_s[...] = ccol
        knb = kn.astype(BF16)
        vb = fv[...].astype(BF16)

        def step(i, carry):
            off = pl.multiple_of(i * bq, bq)
            rows = pl.ds(off, bq)
            o[rows, :] = _fox_block(qn_s[rows, :], knb, vb, cc_s[rows, :], crow, off).astype(o.dtype)
            return carry

        lax.fori_loop(0, t // bq, step, 0)

    return pl.pallas_call(
        body, grid=(NF,), name="fox_fwd",
        in_specs=[_pcol(t, FQ), _pcol(t, FK), _pcol(t, FV), _smcol(t), _small(1), _small(1), _small(1)],
        out_specs=_head(t), out_shape=SDS((t, NF * HD), BF16),
        scratch_shapes=[pltpu.VMEM((t, HD), F32), pltpu.VMEM((t, 1), F32)],
        compiler_params=_cp("parallel"),
    )(p, p, p, p, fb, qg, kg)


def _fox_bwd(p, fb, qg, kg, dmix, bq):
    t = p.shape[0]

    def body(fq, fk, fv, sm, fb_r, qg_r, kg_r, do, dfq, dfk, dfv, dsm, dfb, dqg, dkg,
             qn_s, cc_s, dqn_s, dcc_s, dkn_s, dv_s, dcr_s):
        h = pl.program_id(0)
        qn, kn, ccol, crow = _fox_prep(fq[...], fk[...], sm[...], fb_r[...], qg_r[...], kg_r[...], h)
        qn_s[...] = qn
        cc_s[...] = ccol
        v = fv[...]
        dkn_s[...] = jnp.zeros_like(dkn_s)
        dv_s[...] = jnp.zeros_like(dv_s)
        dcr_s[...] = jnp.zeros_like(dcr_s)

        def step(i, carry):
            off = pl.multiple_of(i * bq, bq)
            rows = pl.ds(off, bq)
            _, vjp = jax.vjp(lambda a, b, c, d, e: _fox_block(a, b, c, d, e, off),
                             qn_s[rows, :], kn, v, cc_s[rows, :], crow)
            dq, dk, dv, dcc, dcr = vjp(do[rows, :])
            dqn_s[rows, :] = dq
            dcc_s[rows, :] = dcc
            dkn_s[...] += dk
            dv_s[...] += dv
            dcr_s[...] += dcr
            return carry

        lax.fori_loop(0, t // bq, step, 0)
        _, prep_vjp = jax.vjp(lambda a, b, c, d, e, f: _fox_prep(a, b, c, d, e, f, h),
                              fq[...], fk[...], sm[...], fb_r[...], qg_r[...], kg_r[...])
        g_fq, g_fk, g_sm, g_fb, g_qg, g_kg = prep_vjp((dqn_s[...], dkn_s[...], dcc_s[...], dcr_s[...]))
        dfq[...] = g_fq.astype(dfq.dtype)
        dfk[...] = g_fk.astype(dfk.dtype)
        dfv[...] = dv_s[...].astype(dfv.dtype)

        @pl.when(h == 0)
        def _():
            for r in (dsm, dfb, dqg, dkg):
                r[...] = jnp.zeros_like(r)

        dsm[...] += g_sm
        dfb[...] += g_fb
        dqg[...] += g_qg
        dkg[...] += g_kg

    head = _head(t)
    return pl.pallas_call(
        body, grid=(NF,), name="fox_bwd",
        in_specs=[_pcol(t, FQ), _pcol(t, FK), _pcol(t, FV), _smcol(t), _small(1), _small(1), _small(1), head],
        out_specs=[head, head, head, _small(t), _small(1), _small(1), _small(1)],
        out_shape=[SDS((t, NF * HD), BF16)] * 3 + [SDS((t, HD), F32)] + [SDS((1, HD), F32)] * 3,
        scratch_shapes=[pltpu.VMEM((t, HD), F32), pltpu.VMEM((t, 1), F32), pltpu.VMEM((t, HD), F32),
                        pltpu.VMEM((t, 1), F32), pltpu.VMEM((t, HD), F32), pltpu.VMEM((t, HD), F32),
                        pltpu.VMEM((1, t), F32)],
        compiler_params=_cp("arbitrary"),
    )(p, p, p, p, fb, qg, kg, dmix)


def _mem_attn(mq, mk, mv, qg, kg):
    s = _dot(_rms(mq, qg).astype(BF16), _rms(mk, kg).astype(BF16), NT) * (HD ** -0.5)
    e = jnp.exp(s - lax.stop_gradient(jnp.max(s, axis=1, keepdims=True)))
    p = e / jnp.sum(e, axis=1, keepdims=True)
    return _dot(p.astype(BF16), mv.astype(BF16))


def _mem_fwd(p, mkv, qg, kg):
    t, ml = p.shape[0], mkv.shape[0]

    def body(mq, mk, mv, qg_r, kg_r, o):
        o[...] = _mem_attn(mq[...], mk[...], mv[...], qg_r[...], kg_r[...]).astype(o.dtype)

    return pl.pallas_call(
        body, grid=(NM,), name="mem_fwd",
        in_specs=[_pcol(t, MQ), pl.BlockSpec((ml, HD), lambda h: (0, h)), pl.BlockSpec((ml, HD), lambda h: (0, NM + h)),
                  _small(1), _small(1)],
        out_specs=pl.BlockSpec((t, HD), lambda h: (0, h)), out_shape=SDS((t, NM * HD), BF16),
        compiler_params=_cp("parallel"),
    )(p, mkv, mkv, qg, kg)


def _mem_bwd(p, mkv, qg, kg, dmix):
    t, ml = p.shape[0], mkv.shape[0]

    def body(mq, mk, mv, qg_r, kg_r, do, dmq, dmk, dmv, dqg, dkg):
        _, vjp = jax.vjp(_mem_attn, mq[...], mk[...], mv[...], qg_r[...], kg_r[...])
        g_q, g_k, g_v, g_qg, g_kg = vjp(do[...])
        dmq[...] = g_q.astype(dmq.dtype)
        dmk[...] = g_k
        dmv[...] = g_v

        @pl.when(pl.program_id(0) == 0)
        def _():
            dqg[...] = jnp.zeros_like(dqg)
            dkg[...] = jnp.zeros_like(dkg)

        dqg[...] += g_qg
        dkg[...] += g_kg

    return pl.pallas_call(
        body, grid=(NM,), name="mem_bwd",
        in_specs=[_pcol(t, MQ), pl.BlockSpec((ml, HD), lambda h: (0, h)), pl.BlockSpec((ml, HD), lambda h: (0, NM + h)),
                  _small(1), _small(1), pl.BlockSpec((t, HD), lambda h: (0, NF + NG + h))],
        out_specs=[pl.BlockSpec((t, HD), lambda h: (0, h)), pl.BlockSpec((ml, HD), lambda h: (0, h)),
                   pl.BlockSpec((ml, HD), lambda h: (0, h)), _small(1), _small(1)],
        out_shape=[SDS((t, NM * HD), BF16), SDS((ml, NM * HD), F32), SDS((ml, NM * HD), F32),
                   SDS((1, HD), F32), SDS((1, HD), F32)],
        compiler_params=_cp("arbitrary"),
    )(p, mkv, mkv, qg, kg, dmix)


def _shift_down(x, s):
    if s == 0:
        return x
    return jnp.where(_iota(x.shape, 0) >= s, pltpu.roll(x, s, 0), 0.0)


def _shift_up(x, s):
    if s == 0:
        return x
    n = x.shape[0]
    return jnp.where(_iota(x.shape, 0) < n - s, pltpu.roll(x, n - s, 0), 0.0)


@jax.custom_vjp
def _conv4(x, w0, w1, w2, w3):
    return w0 * _shift_down(x, 3) + w1 * _shift_down(x, 2) + w2 * _shift_down(x, 1) + w3 * x


def _conv4_fwd(x, w0, w1, w2, w3):
    return _conv4(x, w0, w1, w2, w3), (x, w0, w1, w2, w3)


def _conv4_bwd(res, dy):
    x, w0, w1, w2, w3 = res
    dx = w0 * _shift_up(dy, 3) + w1 * _shift_up(dy, 2) + w2 * _shift_up(dy, 1) + w3 * dy
    dws = tuple(jnp.sum(dy * _shift_down(x, 3 - k), axis=0, keepdims=True) for k in range(4))
    return (dx,) + dws


_conv4.defvjp(_conv4_fwd, _conv4_bwd)


def _gdn_prep(gq, gk, gv, sm, taps, alog, dtb, h):
    q, k, v = [_silu(_conv4(x, *taps[4 * j:4 * j + 4])) for j, x in enumerate((gq, gk, gv))]
    q = q * lax.rsqrt(jnp.sum(q * q, axis=-1, keepdims=True) + NORM_EPS) * (HD ** -0.5)
    k = k * lax.rsqrt(jnp.sum(k * k, axis=-1, keepdims=True) + NORM_EPS)
    g = _lane_pick(-jnp.exp(alog) * _softplus(sm + dtb), L_GA + h)
    beta = _lane_pick(_sigmoid(sm), L_GB + h)
    return q, k, v, g, beta


def _split(x, n):
    parts, rest = [], x
    for i in range(n):
        parts.append(rest.astype(BF16))
        if i + 1 < n:
            rest = rest - parts[-1].astype(F32)
    return parts


def _raw_dot(a, b, form):
    lead = a.ndim - 2
    ca, cb = {"nn": (1, 0), "nt": (1, 1), "tn": (0, 0)}[form]
    batch = ((0,), (0,)) if lead else ((), ())
    return lax.dot_general(a, b, (((ca + lead,), (cb + lead,)), batch), preferred_element_type=F32)


def _pdot_impl(a, b, form, mode):
    if mode == "1":
        return _raw_dot(a.astype(BF16), b.astype(BF16), form)
    if mode == "3":
        (ah, al), (bh, bl) = _split(a, 2), _split(b, 2)
        return _raw_dot(ah, bh, form) + (_raw_dot(al, bh, form) + _raw_dot(ah, bl, form))
    if mode == "xa":
        return sum(_raw_dot(a.astype(BF16), t, form) for t in reversed(_split(b, 3)))
    return sum(_raw_dot(t, b.astype(BF16), form) for t in reversed(_split(a, 3)))


@functools.partial(jax.custom_vjp, nondiff_argnums=(2, 3))
def _pdot(a, b, form, mode):
    return _pdot_impl(a, b, form, mode)


def _pdot_fwd(a, b, form, mode):
    return _pdot_impl(a, b, form, mode), (a, b)


def _pdot_bwd(form, mode, res, ct):
    a, b = res
    da_args, db_args = {"nn": ((ct, b, "nt"), (a, ct, "tn")), "nt": ((ct, b, "nn"), (ct, a, "tn")),
                        "tn": ((b, ct, "nt"), (a, ct, "nn"))}[form]

    def side(args, exact):
        if mode in ("1", "3"):
            return mode
        return "xa" if args[0] is exact else "xb"

    if mode == "xa":
        return jnp.zeros_like(a), _pdot_impl(*db_args, side(db_args, a))
    if mode == "xb":
        return _pdot_impl(*da_args, side(da_args, b)), jnp.zeros_like(b)
    return _pdot_impl(*da_args, mode), _pdot_impl(*db_args, mode)


_pdot.defvjp(_pdot_fwd, _pdot_bwd)

GDN_QK, GDN_INV, GDN_SCAN = "1", "3", "1"


def _gdn_intra(q, k, v, g, beta):
    n = q.shape[0]
    r, c = _iota((CHUNK, CHUNK), 0), _iota((CHUNK, CHUNK), 1)
    tril, strict = r >= c, r > c
    trilf = jnp.broadcast_to(tril.astype(F32), (n, CHUNK, CHUNK))
    gcm = _pdot(trilf, jnp.broadcast_to(g, (n, CHUNK, CHUNK)), "nn", "xa")
    gcf = _pdot(trilf, jnp.broadcast_to(g, (n, CHUNK, HD)), "nn", "xa")
    lane0 = (_iota((1, 1, CHUNK), 2) == 0).astype(F32)
    gcr = _pdot(jnp.ones((n, CHUNK, CHUNK), F32), gcm * lane0, "nt", "xa")
    decay = jnp.where(tril, jnp.exp(jnp.where(tril, gcm - gcr, 0.0)), 0.0)
    egc = jnp.exp(gcf)
    kb = k * beta
    low = jnp.where(strict, _pdot(kb, k, "nt", GDN_QK) * decay, 0.0)
    inv = (r == c).astype(F32) - low
    pw = low
    for _ in range(5):
        pw = _pdot(pw, pw, "nn", GDN_INV)
        inv = inv + _pdot(inv, pw, "nn", GDN_INV)
    u = _pdot(inv, v * beta, "nn", GDN_INV)
    w = _pdot(inv, kb * egc, "nn", GDN_INV)
    at = jnp.where(tril, _pdot(q, k, "nt", GDN_QK) * decay, 0.0)
    gl = jnp.sum(jnp.broadcast_to(g, (n, CHUNK, HD)), axis=1, keepdims=True)
    return u, w, q * egc, at, k * jnp.exp(gl - gcf), gl


def _gdn_step(s, u, w, qg, at, kd, gl):
    vn = u - _pdot(w, s, "nn", GDN_SCAN)
    o = _pdot(qg, s, "nn", GDN_SCAN) + _pdot(at, vn, "nn", GDN_SCAN)
    s2 = s * jnp.exp(gl) + _pdot(kd, vn, "tn", GDN_SCAN)
    return o, s2


def _gdn_scratch(nc):
    big = pltpu.VMEM((nc, CHUNK, HD), F32)
    return [big, big, big, pltpu.VMEM((nc, CHUNK, 1), F32), pltpu.VMEM((nc, CHUNK, 1), F32),
            big, big, big, pltpu.VMEM((nc, CHUNK, CHUNK), F32), big, pltpu.VMEM((nc, 1, HD), F32)]


def _gdn_in_specs(t):
    cw = lambda cb: pl.BlockSpec((4, HD), lambda h, cb=cb: (0, cb + h))
    return [_pcol(t, GQ), _pcol(t, GK), _pcol(t, GV), _smcol(t), cw(0), cw(NG), cw(2 * NG), _small(1), _small(1)]


def _taps(wq, wk, wv):
    return tuple(w[k:k + 1, :] for w in (wq, wk, wv) for k in range(4))


def _gdn_stage(vals, refs):
    nc = refs[0].shape[0]
    for v, r in zip(vals, refs):
        r[...] = v.reshape(nc, CHUNK, v.shape[-1])


def _gdn_intra_all(chunked, intra):
    nc = chunked[0].shape[0]
    grp_n = math.gcd(nc, GROUP)

    def grp(i, carry):
        sl = pl.ds(pl.multiple_of(i * grp_n, grp_n), grp_n)
        for r, val in zip(intra, _gdn_intra(*[c[sl] for c in chunked])):
            r[sl] = val
        return carry

    lax.fori_loop(0, nc // grp_n, grp, 0)


def _gdn_fwd(p, conv, alog, dtb):
    t = p.shape[0]
    nc = t // CHUNK

    def body(gq, gk, gv, sm, wq, wk, wv, al, db, o, *scr):
        h = pl.program_id(0)
        chunked, intra = scr[:5], scr[5:]
        _gdn_stage(_gdn_prep(gq[...], gk[...], gv[...], sm[...], _taps(wq, wk, wv), al[...], db[...], h), chunked)
        _gdn_intra_all(chunked, intra)

        def step(c, s):
            oc, s2 = _gdn_step(s, *[r[c] for r in intra])
            o[pl.ds(pl.multiple_of(c * CHUNK, CHUNK), CHUNK), :] = oc
            return s2

        lax.fori_loop(0, nc, step, jnp.zeros((HD, HD), F32))

    return pl.pallas_call(
        body, grid=(NG,), name="gdn_fwd", in_specs=_gdn_in_specs(t),
        out_specs=_head(t), out_shape=SDS((t, NG * HD), F32),
        scratch_shapes=_gdn_scratch(nc), compiler_params=_cp("parallel"),
    )(p, p, p, p, conv, conv, conv, alog, dtb)


def _gdn_bwd(p, conv, alog, dtb, do_raw):
    t = p.shape[0]
    nc = t // CHUNK

    def body(gq, gk, gv, sm, wq, wk, wv, al, db, do, dgq, dgk, dgv, dsm, dwq, dwk, dwv, dal, ddb, *scr):
        h = pl.program_id(0)
        chunked, intra, states = scr[:5], scr[5:11], scr[11]
        _gdn_stage(_gdn_prep(gq[...], gk[...], gv[...], sm[...], _taps(wq, wk, wv), al[...], db[...], h), chunked)
        _gdn_intra_all(chunked, intra)

        def fwd(c, s):
            states[c] = s
            return _gdn_step(s, *[r[c] for r in intra])[1]

        lax.fori_loop(0, nc, fwd, jnp.zeros((HD, HD), F32))

        def bwd(i, ds):
            c = nc - 1 - i
            _, vjp = jax.vjp(_gdn_step, states[c], *[r[c] for r in intra])
            grads = vjp((do[pl.ds(pl.multiple_of(c * CHUNK, CHUNK), CHUNK), :], ds))
            for r, gval in zip(intra, grads[1:]):
                r[c] = gval
            return grads[0]

        lax.fori_loop(0, nc, bwd, jnp.zeros((HD, HD), F32))

        grp_n = math.gcd(nc, GROUP)

        def grp(i, carry):
            sl = pl.ds(pl.multiple_of(i * grp_n, grp_n), grp_n)
            _, vjp = jax.vjp(_gdn_intra, *[r[sl] for r in chunked])
            for r, gval in zip(chunked, vjp(tuple(r[sl] for r in intra))):
                r[sl] = gval
            return carry

        lax.fori_loop(0, nc // grp_n, grp, 0)
        _, prep_vjp = jax.vjp(
            lambda *a: _gdn_prep(*a, h), gq[...], gk[...], gv[...], sm[...], _taps(wq, wk, wv), al[...], db[...])
        grads = prep_vjp(tuple(r[...].reshape(t, r.shape[-1]) for r in chunked))
        for r, gval in zip((dgq, dgk, dgv), grads[:3]):
            r[...] = gval.astype(r.dtype)
        for j, r in enumerate((dwq, dwk, dwv)):
            for k in range(4):
                r[k:k + 1, :] = grads[4][4 * j + k]

        @pl.when(h == 0)
        def _():
            for r in (dsm, dal, ddb):
                r[...] = jnp.zeros_like(r)

        dsm[...] += grads[3]
        dal[...] += grads[5]
        ddb[...] += grads[6]

    head = _head(t)
    taps = pl.BlockSpec((4, HD), lambda h: (0, h))
    return pl.pallas_call(
        body, grid=(NG,), name="gdn_bwd", in_specs=_gdn_in_specs(t) + [head],
        out_specs=[head, head, head, _small(t), taps, taps, taps, _small(1), _small(1)],
        out_shape=[SDS((t, NG * HD), BF16)] * 3 + [SDS((t, HD), F32)] + [SDS((4, NG * HD), F32)] * 3 + [SDS((1, HD), F32)] * 2,
        scratch_shapes=_gdn_scratch(nc) + [pltpu.VMEM((nc, HD, HD), F32)], compiler_params=_cp("arbitrary"),
    )(p, p, p, p, conv, conv, conv, alog, dtb, do_raw)


def _gdn_post(o, z, gain):
    return (jnp.concatenate(
        [_rms(o[:, h * HD:(h + 1) * HD], gain) * _silu(z[:, h * HD:(h + 1) * HD]) for h in range(NG)], axis=1),)


def _place():
    return lax.axis_index("x"), lax.axis_index("y"), lax.axis_index("c")


def _all_gather(name, shard):
    def body(x_ref, out_ref, send_sems, recv_sems, local_sem):
        x, y, c = _place()
        me, sibling = (x, y, c), (x, y, 1 - c)
        chips = [(1 - x, y), (x, 1 - y), (1 - x, 1 - y)]

        def blk(px, py, pc):
            return out_ref.at[4 * px + 2 * py + pc]

        def copy(k, block, to, src=None):
            return pltpu.make_async_remote_copy(
                src_ref=blk(*block) if src is None else src, dst_ref=blk(*block),
                send_sem=send_sems.at[k], recv_sem=recv_sems.at[k], device_id=to, device_id_type=MESH)

        mine = pltpu.make_async_copy(x_ref, blk(*me), local_sem)
        mine.start()
        first = [copy(0, me, sibling, src=x_ref)]
        first += [copy(1 + j, me, (*chip, c), src=x_ref) for j, chip in enumerate(chips)]
        for cp in first:
            cp.start()
        passed = [copy(4 + j, (*chip, c), sibling) for j, chip in enumerate(chips)]
        for j, chip in enumerate(chips):
            copy(1 + j, (*chip, c), me).wait_recv()
            passed[j].start()
        copy(0, sibling, me).wait_recv()
        for j, chip in enumerate(chips):
            copy(4 + j, (*chip, 1 - c), me).wait_recv()
        for cp in first + passed:
            cp.wait_send()
        mine.wait()

    return pl.pallas_call(
        body, name=name, out_shape=SDS((N_DEV,) + shard.shape, shard.dtype),
        in_specs=[pl.BlockSpec(memory_space=pltpu.HBM)], out_specs=pl.BlockSpec(memory_space=pltpu.HBM),
        scratch_shapes=[pltpu.SemaphoreType.DMA((7,)), pltpu.SemaphoreType.DMA((7,)), pltpu.SemaphoreType.DMA],
    )(shard)


def _scatter_exchange(name, full):
    def body(g_ref, out_ref, send_sems, recv_sems, local_sem):
        x, y, c = _place()
        me = 4 * x + 2 * y + c
        mine = pltpu.make_async_copy(g_ref.at[me], out_ref.at[me], local_sem)
        mine.start()
        sends, recvs = [], []
        for k in range(1, N_DEV):
            px = 1 - x if k & 4 else x
            py = 1 - y if k & 2 else y
            pc = 1 - c if k & 1 else c
            peer = 4 * px + 2 * py + pc
            sends.append(pltpu.make_async_remote_copy(
                src_ref=g_ref.at[peer], dst_ref=out_ref.at[me], send_sem=send_sems.at[k - 1],
                recv_sem=recv_sems.at[k - 1], device_id=(px, py, pc), device_id_type=MESH))
            recvs.append(pltpu.make_async_remote_copy(
                src_ref=g_ref.at[me], dst_ref=out_ref.at[peer], send_sem=send_sems.at[k - 1],
                recv_sem=recv_sems.at[k - 1], device_id=(px, py, pc), device_id_type=MESH))
        for cp in sends:
            cp.start()
        for cp in recvs:
            cp.wait_recv()
        for cp in sends:
            cp.wait_send()
        mine.wait()

    return pl.pallas_call(
        body, name=name, out_shape=SDS(full.shape, full.dtype),
        in_specs=[pl.BlockSpec(memory_space=pltpu.HBM)], out_specs=pl.BlockSpec(memory_space=pltpu.HBM),
        scratch_shapes=[pltpu.SemaphoreType.DMA((7,)), pltpu.SemaphoreType.DMA((7,)), pltpu.SemaphoreType.DMA],
    )(full)


def _sum_blocks(name, parts):
    _, r, c = parts.shape
    tr = 64 if r % 64 == 0 else r

    def body(x, o):
        acc = x[0].astype(F32)
        for d in range(1, N_DEV):
            acc = acc + x[d].astype(F32)
        o[...] = acc

    return pl.pallas_call(
        body, grid=(r // tr,), name=name, in_specs=[pl.BlockSpec((N_DEV, tr, c), lambda i: (0, i, 0))],
        out_specs=pl.BlockSpec((tr, c), lambda i: (i, 0)), out_shape=SDS((r, c), F32), compiler_params=_cp("parallel"),
    )(parts)


def _reduce_scatter(name, full):
    return _sum_blocks(name + "_sum", _scatter_exchange(name, full))


def _all_reduce_small(name, x, reduce):
    m_per, n = x.shape

    def body(x_ref, out_ref, send_sems, recv_sems, local_sem):
        px, py, pc = _place()
        me, sibling = (px, py, pc), (px, py, 1 - pc)
        chips = [(1 - px, py), (px, 1 - py), (1 - px, 1 - py)]
        buf = out_ref

        def rows(qx, qy, qc):
            return buf.at[pl.ds((4 * qx + 2 * qy + qc) * m_per, m_per), :]

        def copy(k, block, to, src=None):
            return pltpu.make_async_remote_copy(
                src_ref=rows(*block) if src is None else src, dst_ref=rows(*block),
                send_sem=send_sems.at[k], recv_sem=recv_sems.at[k], device_id=to, device_id_type=MESH)

        mine = pltpu.make_async_copy(x_ref, rows(*me), local_sem)
        mine.start()
        first = [copy(0, me, sibling, src=x_ref)]
        first += [copy(1 + j, me, (*chip, pc), src=x_ref) for j, chip in enumerate(chips)]
        for cp in first:
            cp.start()
        passed = [copy(4 + j, (*chip, pc), sibling) for j, chip in enumerate(chips)]
        for j, chip in enumerate(chips):
            copy(1 + j, (*chip, pc), me).wait_recv()
            passed[j].start()
        copy(0, sibling, me).wait_recv()
        for j, chip in enumerate(chips):
            copy(4 + j, (*chip, 1 - pc), me).wait_recv()
        for cp in first + passed:
            cp.wait_send()
        mine.wait()

    gathered = pl.pallas_call(
        body, name=name, out_shape=SDS((N_DEV * m_per, n), x.dtype),
        in_specs=[pl.BlockSpec(memory_space=pltpu.VMEM)], out_specs=pl.BlockSpec(memory_space=pltpu.VMEM),
        scratch_shapes=[pltpu.SemaphoreType.DMA((7,)), pltpu.SemaphoreType.DMA((7,)), pltpu.SemaphoreType.DMA],
    )(x)
    if not reduce:
        return gathered
    return _sum_blocks(name + "_sum", gathered.reshape(N_DEV, m_per, n))


def _adamw(w, g, m, v):
    m = ADAM_B1 * m + (1.0 - ADAM_B1) * g
    v = ADAM_B2 * v + (1.0 - ADAM_B2) * (g * g)
    m_hat = m / (1.0 - ADAM_B1 ** ADAM_STEP)
    v_hat = v / (1.0 - ADAM_B2 ** ADAM_STEP)
    return -ADAM_LR * (m_hat / (jnp.sqrt(v_hat) + ADAM_EPS) + ADAM_WD * w), m, v


def _adamw_call(name, w, g, m, v):
    r, c = w.shape
    tm = 64 if r % 64 == 0 else r
    return _rowwise(name, _adamw, [w, g, m, v], [], [(c, F32)] * 3, tm)


_IN_COLS = 5906


def _perm_in(w):
    pad = jnp.zeros((w.shape[0], PC - _IN_COLS), w.dtype)
    return jnp.concatenate([w[:, :2304], w[:, 2310:4614], w[:, 4614:5382], w[:, 5394:5906], w[:, 2304:2310],
                            w[:, 5382:5394], pad], axis=1)


def _unperm_in(g):
    return jnp.concatenate([g[:, :2304], g[:, 5888:5894], g[:, 2304:4608], g[:, 4608:5376], g[:, 5894:5906],
                            g[:, 5376:5888]], axis=1)


def _lanes(v, at):
    return jnp.pad(v, ((0, 0), (at, HD - at - v.shape[1])))


_PACK = ("norm_mix", "mem_norm", "norm_ffn", "gdn_conv", "fox_q_norm", "fox_k_norm", "gdn_out_norm", "mem_q_norm",
         "mem_k_norm", "fox_f_bias", "gdn_a_log", "gdn_dt_bias", "loss")


def _pack(vals):
    parts = [vals[n].reshape(-1, HD) for n in _PACK]
    used = sum(p.shape[0] for p in parts)
    buf = jnp.concatenate(parts + [jnp.zeros((-used % 8, HD), F32)], axis=0)
    return buf, [(n, p.shape[0]) for n, p in zip(_PACK, parts)]


def _unpack(buf, layout):
    out, at = {}, 0
    for n, rows in layout:
        out[n] = buf[at:at + rows]
        at += rows
    return out


def kernel(x, mem, norm_mix, w_in, fox_f_bias, fox_q_norm, fox_k_norm, gdn_conv, gdn_a_log, gdn_dt_bias, gdn_out_norm, mem_norm, w_mem_kv, mem_q_norm, mem_k_norm, w_out, norm_ffn, w_gate_up, w_down, loss_target, m_norm_mix, m_w_in, m_fox_f_bias, m_fox_q_norm, m_fox_k_norm, m_gdn_conv, m_gdn_a_log, m_gdn_dt_bias, m_gdn_out_norm, m_mem_norm, m_w_mem_kv, m_mem_q_norm, m_mem_k_norm, m_w_out, m_norm_ffn, m_w_gate_up, m_w_down, v_norm_mix, v_w_in, v_fox_f_bias, v_fox_q_norm, v_fox_k_norm, v_gdn_conv, v_gdn_a_log, v_gdn_dt_bias, v_gdn_out_norm, v_mem_norm, v_w_mem_kv, v_mem_q_norm, v_mem_k_norm, v_w_out, v_norm_ffn, v_w_gate_up, v_w_down):
    args = dict(locals())
    d = x.shape[2]
    me = 4 * lax.axis_index("x") + 2 * lax.axis_index("y") + lax.axis_index("c")

    w_in_all = _all_gather("ag_w_in", _perm_in(w_in[0]).astype(BF16)).reshape(d, PC)
    w_kv_all = _all_gather("ag_w_kv", w_mem_kv[0].astype(BF16)).reshape(d, 2 * NM * HD)
    w_out_all = _all_gather("ag_w_out", w_out[0].astype(BF16)).reshape(-1, d)
    wgu = _all_gather("ag_w_gu", w_gate_up[0].astype(BF16))
    w_down_all = _all_gather("ag_w_down", w_down[0].astype(BF16)).reshape(-1, d)
    cshard = gdn_conv[0].shape[1]
    conv_pad = jnp.pad(gdn_conv[0], ((0, 4), (0, 3 * HD - cshard)))
    conv_all = _all_reduce_small("ag_conv", conv_pad, False).reshape(N_DEV, 8, 3 * HD)[:, :4, :cshard]
    conv_all = conv_all.transpose(1, 0, 2).reshape(4, N_DEV * cshard)

    grad_x, loss_local, big_grads, small_grads = _local_step(
        x[0], mem[0], loss_target[0], norm_mix, fox_f_bias, fox_q_norm, fox_k_norm, gdn_a_log, gdn_dt_bias,
        gdn_out_norm, mem_norm, mem_q_norm, mem_k_norm, norm_ffn, w_in_all, w_kv_all, w_out_all, wgu, w_down_all, conv_all)

    grads = {n: _reduce_scatter("rs_" + n, g.reshape((N_DEV, g.shape[-2] * g.shape[0] // N_DEV, g.shape[-1]) if g.ndim == 3
                                                     else (N_DEV, g.shape[0] // N_DEV, g.shape[1])))
             for n, g in big_grads.items()}
    grads["w_in"] = _unperm_in(grads["w_in"])
    small_grads["loss"] = jnp.broadcast_to(loss_local, (1, HD))
    packed, layout = _pack(small_grads)
    small = _unpack(_all_reduce_small("ar_small", packed, True), layout)
    loss = small["loss"][0, 0]
    six = {"fox_f_bias": L_FF, "gdn_a_log": L_GA, "gdn_dt_bias": L_GA}
    for n, rows_n in layout[:-1]:
        gsm = small[n]
        if n == "gdn_conv":
            gsm = lax.dynamic_slice(gsm.reshape(4, N_DEV * cshard), (0, me * cshard), (4, cshard))[None]
        elif n in six:
            gsm = gsm[:, six[n]:six[n] + 6]
        else:
            gsm = gsm.reshape(1, rows_n * HD)
        grads[n] = gsm

    names = ['norm_mix', 'w_in', 'fox_f_bias', 'fox_q_norm', 'fox_k_norm', 'gdn_conv', 'gdn_a_log', 'gdn_dt_bias',
             'gdn_out_norm', 'mem_norm', 'w_mem_kv', 'mem_q_norm', 'mem_k_norm', 'w_out', 'norm_ffn', 'w_gate_up', 'w_down']
    big = ("w_in", "w_mem_kv", "w_out", "w_gate_up", "w_down")
    delta, new_m, new_v = {}, {}, {}
    for n in big:
        delta[n], new_m[n], new_v[n] = [a[None] for a in _adamw_call(
            "adamw_" + n, args[n][0], grads[n], args["m_" + n][0], args["v_" + n][0])]
        grads[n] = grads[n][None]

    def flat(a):
        a = a.reshape(1, -1)
        return jnp.pad(a, ((0, 0), (0, -a.shape[1] % HD))).reshape(-1, HD)

    smalls = [n for n in names if n not in big]
    pk = lambda pre: jnp.concatenate([flat(grads[n] if pre == "g" else args[pre + n]) for n in smalls], axis=0)
    cat = [pk(""), pk("g"), pk("m_"), pk("v_")]
    padr = -cat[0].shape[0] % 8
    cat = [jnp.pad(a, ((0, padr), (0, 0))) for a in cat]
    res = _adamw_call("adamw_small", *cat)
    at = 0
    for n in smalls:
        shape = args[n].shape
        size = math.prod(shape)
        nrow = -(-size // HD)
        for dst, src in zip((delta, new_m, new_v), res):
            dst[n] = src[at:at + nrow].reshape(-1)[:size].reshape(shape)
        at += nrow

    return (loss, grad_x[None], *[grads[n] for n in names], *[delta[n] for n in names],
            *[new_m[n] for n in names], *[new_v[n] for n in names])


def _local_step(xs, ms, tgt, norm_mix, fox_f_bias, fox_q_norm, fox_k_norm, gdn_a_log, gdn_dt_bias, gdn_out_norm,
                mem_norm, mem_q_norm, mem_k_norm, norm_ffn, w_in_all, w_kv_all, w_out_all, wgu, w_down_all, conv_all):
    t, d = xs.shape
    ffw = wgu.shape[2]
    bq = min(t, 256)
    fb, alog, dtb = _lanes(fox_f_bias, L_FF), _lanes(gdn_a_log, L_GA), _lanes(gdn_dt_bias, L_GA)

    rms1 = lambda a, g: (_rms(a, g),)
    (u,) = _rowwise("norm_mix", rms1, [xs], [norm_mix], [(d, BF16)], min(t, 256))
    p = _matmul("proj_in", u, w_in_all, NN, F32, 1024, 768)
    (mem_n,) = _rowwise("norm_mem", rms1, [ms], [mem_norm], [(d, BF16)], ms.shape[0])
    mkv = _matmul("proj_mem", mem_n, w_kv_all, NN, F32, 256, 512)
    o_fox = _fox_fwd(p, fb, fox_q_norm, fox_k_norm, bq)
    o_gdn_raw = _gdn_fwd(p, conv_all, alog, dtb)
    zrow = (p, NG * HD, GZ * HD // (NG * HD))
    (o_gdn,) = _rowwise("gdn_post", _gdn_post, [o_gdn_raw, zrow], [gdn_out_norm], [(NG * HD, BF16)], min(t, 256))
    o_mem = _mem_fwd(p, mkv, mem_q_norm, mem_k_norm)
    mix = jnp.concatenate([o_fox, o_gdn, o_mem], axis=1)
    h1 = _matmul("proj_out", mix, w_out_all, NN, F32, 1024, 512, residual=xs)
    (h1n,) = _rowwise("norm_ffn", rms1, [h1], [norm_ffn], [(d, BF16)], min(t, 256))
    gu, act = _ffn_up(h1n, wgu.reshape(2, 4, d, ffw))
    dy, dyb, lsum = _ffn_down_loss(act, w_down_all, h1, tgt)
    loss_local = (0.5 / d) * jnp.sum(lsum[::8, ::HD])

    dgu = _ffn_down_bwd(dyb, w_down_all.reshape(4, ffw, d), gu).reshape(8, t, ffw)
    g_w_down = _matmul("grad_w_down", act, dyb, TN, BF16, 512, 512)
    dh1n = _ffn_up_bwd_x(dgu, wgu)
    g_w_gu = _ffn_up_bwd_w(h1n, dgu)
    rms2 = lambda a, g: (_rms(a, g), a)
    dh1, g_norm_ffn = _rowwise_vjp("norm_ffn_bwd", rms2, [h1], [norm_ffn], [dh1n, dy], [F32], min(t, 256))
    dh1b = dh1.astype(BF16)

    dmix = _matmul("proj_out_bwd_x", dh1b, w_out_all, NT, F32, 1024, 512)
    g_w_out = _matmul("grad_w_out", mix, dh1b, TN, BF16, 512, 512)
    dfq, dfk, dfv, dsm_fox, g_fb, g_fqn, g_fkn = _fox_bwd(p, fb, fox_q_norm, fox_k_norm, dmix, bq)
    do_raw, dgz, g_gon = _rowwise_vjp("gdn_post_bwd", _gdn_post, [o_gdn_raw, zrow], [gdn_out_norm],
                                      [(dmix, NG * HD, 1)], [F32, BF16], min(t, 256))
    dgq, dgk, dgv, dsm_gdn, dwq, dwk, dwv, g_alog, g_dtb = _gdn_bwd(p, conv_all, alog, dtb, do_raw)
    dmq, dmk, dmv, g_mqn, g_mkn = _mem_bwd(p, mkv, mem_q_norm, mem_k_norm, dmix)
    dmkv = jnp.concatenate([dmk, dmv], axis=1).astype(BF16)
    dmem_n = _matmul("proj_mem_bwd_x", dmkv, w_kv_all, NT, F32, 256, 512)
    g_w_kv = _matmul("grad_w_kv", mem_n, dmkv, TN, BF16, 512, 512)
    g_mem_norm = _rowwise_vjp("norm_mem_bwd", rms1, [ms], [mem_norm], [dmem_n], [], ms.shape[0])[0]
    dp = jnp.concatenate([dfq, dfk, dfv, dgq, dgk, dgv, dgz, dmq, (dsm_fox + dsm_gdn).astype(BF16),
                          jnp.zeros((t, HD), BF16)], axis=1)
    du = _matmul("proj_in_bwd_x", dp, w_in_all, NT, F32, 512, 512)
    g_w_in = _matmul("grad_w_in", u, dp, TN, BF16, 512, 768)
    grad_x, g_norm_mix = _rowwise_vjp("norm_mix_bwd", rms2, [xs], [norm_mix], [du, dh1], [F32], min(t, 256))

    big_grads = {"w_in": g_w_in, "w_mem_kv": g_w_kv, "w_out": g_w_out, "w_gate_up": g_w_gu, "w_down": g_w_down}
    small_grads = {
        "norm_mix": g_norm_mix, "mem_norm": g_mem_norm, "norm_ffn": g_norm_ffn,
        "gdn_conv": jnp.concatenate([dwq, dwk, dwv], axis=1),
        "fox_q_norm": g_fqn, "fox_k_norm": g_fkn, "gdn_out_norm": g_gon, "mem_q_norm": g_mqn, "mem_k_norm": g_mkn,
        "fox_f_bias": g_fb, "gdn_a_log": g_alog, "gdn_dt_bias": g_dtb}
    return grad_x, loss_local, big_grads, small_grads
```

```python
import functools
import math

import jax
import jax.numpy as jnp
from jax import lax
from jax.experimental import pallas as pl
from jax.experimental.pallas import tpu as pltpu

F32 = jnp.float32
BF16 = jnp.bfloat16
HI = lax.Precision.HIGHEST
SDS = jax.ShapeDtypeStruct

N_DEV = 8
HD = 128
NF, NG, NM = 6, 6, 4
CHUNK = 64
GROUP = 4
NORM_EPS = 1e-6
FQ, FK, FV, GQ, GK, GV, GZ, MQ, SM, NPB = 0, 6, 12, 18, 24, 30, 36, 42, 46, 48
PC = NPB * HD
L_FF, L_GA, L_GB = 0, 6, 12
VMEM_LIMIT = 56 * 1024 * 1024

ADAM_LR, ADAM_B1, ADAM_B2, ADAM_EPS, ADAM_WD, ADAM_STEP = 0.001, 0.9, 0.999, 1e-08, 0.01, 10

NN = (((1,), (0,)), ((), ()))
NT = (((1,), (1,)), ((), ()))
TN = (((0,), (0,)), ((), ()))
MESH = pl.DeviceIdType.MESH


def _cp(*sem):
    return pltpu.CompilerParams(dimension_semantics=tuple(sem) if sem else None, vmem_limit_bytes=VMEM_LIMIT)


def _dot(a, b, dims=NN):
    return lax.dot_general(a, b, dims, preferred_element_type=F32)


def _bdot(a, b):
    return _dot(a.astype(BF16), b.astype(BF16))


def _iota(shape, axis):
    return lax.broadcasted_iota(jnp.int32, shape, axis)


def _rms(x, gain):
    return x * lax.rsqrt(jnp.mean(x * x, axis=-1, keepdims=True) + NORM_EPS) * gain


def _sigmoid(x):
    z = jnp.exp(-jnp.abs(x))
    return jnp.where(x >= 0, 1.0 / (1.0 + z), z / (1.0 + z))


def _silu(x):
    return x * _sigmoid(x)


def _softplus(x):
    return jnp.maximum(x, 0.0) + jnp.log(1.0 + jnp.exp(-jnp.abs(x)))


def _lane_pick(x, lane):
    oh = (_iota((1, x.shape[-1]), 1) == lane).astype(F32)
    return jnp.sum(x * oh, axis=-1, keepdims=True)


def _cumsum_rows(x):
    tril = (_iota((HD, HD), 0) >= _iota((HD, HD), 1)).astype(F32)
    carry = jnp.zeros((1, x.shape[1]), F32)
    outs = []
    for b in range(x.shape[0] // HD):
        blk = x[b * HD:(b + 1) * HD]
        outs.append(jnp.dot(tril, blk, precision=HI, preferred_element_type=F32) + carry)
        carry = carry + jnp.sum(blk, axis=0, keepdims=True)
    return jnp.concatenate(outs, axis=0)


def _row_spec(r, tm):
    if isinstance(r, tuple):
        arr, width, cb = r
        return arr, pl.BlockSpec((tm, width), lambda i, cb=cb: (i, cb))
    return r, pl.BlockSpec((tm, r.shape[1]), lambda i: (i, 0))


ANY_SPEC = pl.BlockSpec(memory_space=pl.ANY)


def _rowwise(name, fn, rows, consts, outs, tm, deps=()):
    arrs, specs = zip(*[_row_spec(r, tm) for r in rows])
    n_rows = arrs[0].shape[0]
    nr, nc, nd = len(rows), len(consts), len(deps)

    def body(*refs):
        res = fn(*[r[...] for r in refs[:nr + nc]])
        for o, v in zip(refs[nr + nc + nd:], res):
            o[...] = v.astype(o.dtype)

    return pl.pallas_call(
        body, grid=(n_rows // tm,), name=name,
        in_specs=list(specs) + [pl.BlockSpec(c.shape, lambda i: (0, 0)) for c in consts] + [ANY_SPEC] * nd,
        out_specs=[pl.BlockSpec((tm, w), lambda i: (i, 0)) for w, _ in outs],
        out_shape=[SDS((n_rows, w), dt) for w, dt in outs],
        compiler_params=_cp("parallel"),
    )(*arrs, *consts, *deps)


def _rowwise_vjp(name, fn, rows, consts, cts, grad_dtypes, tm, deps=()):
    arrs, specs = zip(*[_row_spec(r, tm) for r in rows])
    ct_arrs, ct_specs = zip(*[_row_spec(r, tm) for r in cts])
    n_rows = arrs[0].shape[0]
    nr, nc, nct, ng, nd = len(rows), len(consts), len(cts), len(grad_dtypes), len(deps)
    widths = [s.block_shape[1] for s in specs[:ng]]

    def body(*refs):
        vals = [r[...].astype(F32) for r in refs[:nr + nc]]
        ctv = tuple(r[...].astype(F32) for r in refs[nr + nc:nr + nc + nct])
        _, vjp = jax.vjp(fn, *vals)
        grads = vjp(ctv)
        outs = refs[nr + nc + nct + nd:]
        for o, g in zip(outs[:ng], grads[:ng]):
            o[...] = g.astype(o.dtype)

        @pl.when(pl.program_id(0) == 0)
        def _():
            for o in outs[ng:]:
                o[...] = jnp.zeros_like(o)

        for o, g in zip(outs[ng:], grads[nr:]):
            o[...] += g

    return pl.pallas_call(
        body, grid=(n_rows // tm,), name=name,
        in_specs=list(specs) + [pl.BlockSpec(c.shape, lambda i: (0, 0)) for c in consts] + list(ct_specs)
        + [ANY_SPEC] * nd,
        out_specs=[pl.BlockSpec((tm, w), lambda i: (i, 0)) for w in widths]
        + [pl.BlockSpec(c.shape, lambda i: (0, 0)) for c in consts],
        out_shape=[SDS((n_rows, w), dt) for w, dt in zip(widths, grad_dtypes)] + [SDS(c.shape, F32) for c in consts],
        compiler_params=_cp("arbitrary"),
    )(*arrs, *consts, *ct_arrs, *deps)


def _tile(n, pref):
    t = min(n, pref)
    while n % t or (t % HD and t != n):
        t -= 1
    return t


def _matmul(name, a, b, dims, out_dtype, tm, tn, residual=None, deps=()):
    ta, tb = dims == TN, dims == NT
    m = a.shape[1] if ta else a.shape[0]
    k = a.shape[0] if ta else a.shape[1]
    n = b.shape[0] if tb else b.shape[1]
    tm, tn = _tile(m, tm), _tile(n, tn)

    def body(*refs):
        acc = _dot(refs[0][...], refs[1][...], dims)
        if residual is not None:
            acc = acc + refs[2][...]
        refs[-1][...] = acc.astype(out_dtype)

    in_specs = [pl.BlockSpec((k, tm), lambda i, j: (0, i)) if ta else pl.BlockSpec((tm, k), lambda i, j: (i, 0)),
                pl.BlockSpec((tn, k), lambda i, j: (j, 0)) if tb else pl.BlockSpec((k, tn), lambda i, j: (0, j))]
    ops = [a, b]
    if residual is not None:
        in_specs.append(pl.BlockSpec((tm, tn), lambda i, j: (i, j)))
        ops.append(residual)
    in_specs += [ANY_SPEC] * len(deps)
    ops += list(deps)
    return pl.pallas_call(
        body, grid=(m // tm, n // tn), name=name, in_specs=in_specs,
        out_specs=pl.BlockSpec((tm, tn), lambda i, j: (i, j)), out_shape=SDS((m, n), out_dtype),
        compiler_params=_cp("parallel", "parallel"),
    )(*ops)


def _ffn_up(h1n, wgu):
    t, d = h1n.shape
    w = wgu.shape[3]
    tm = _tile(t, 512)

    def body(a, b, gu, act):
        x = a[...]
        g = _dot(x, b[0])
        u = _dot(x, b[1])
        gu[0] = g.astype(BF16)
        gu[1] = u.astype(BF16)
        act[...] = (_silu(g) * u).astype(BF16)

    return pl.pallas_call(
        body, grid=(4, t // tm), name="ffn_up",
        in_specs=[pl.BlockSpec((tm, d), lambda j, i: (i, 0)), pl.BlockSpec((2, None, d, w), lambda j, i: (0, j, 0, 0))],
        out_specs=[pl.BlockSpec((2, None, tm, w), lambda j, i: (0, j, i, 0)), pl.BlockSpec((tm, w), lambda j, i: (i, j))],
        out_shape=[SDS((2, 4, t, w), BF16), SDS((t, 4 * w), BF16)],
        compiler_params=_cp("parallel", "parallel"),
    )(h1n, wgu)


def _ffn_down_loss(act, wdown, h1, target):
    t, f = act.shape
    d = wdown.shape[1]
    tm, tn = _tile(t, 512), _tile(d, 512)

    def body(a, b, h, tg, dy, dyb, ls):
        e = _dot(a[...], b[...]) + h[...] - tg[...]
        g = e * (1.0 / d)
        dy[...] = g
        dyb[...] = g.astype(BF16)
        ls[...] = jnp.broadcast_to(jnp.sum(e * e), (8, HD))

    return pl.pallas_call(
        body, grid=(t // tm, d // tn), name="ffn_down_loss",
        in_specs=[pl.BlockSpec((tm, f), lambda i, j: (i, 0)), pl.BlockSpec((f, tn), lambda i, j: (0, j)),
                  pl.BlockSpec((tm, tn), lambda i, j: (i, j)), pl.BlockSpec((tm, tn), lambda i, j: (i, j))],
        out_specs=[pl.BlockSpec((tm, tn), lambda i, j: (i, j)), pl.BlockSpec((tm, tn), lambda i, j: (i, j)),
                   pl.BlockSpec((8, HD), lambda i, j: (i, j))],
        out_shape=[SDS((t, d), F32), SDS((t, d), BF16), SDS((8 * (t // tm), HD * (d // tn)), F32)],
        compiler_params=_cp("parallel", "parallel"),
    )(act, wdown, h1, target)


def _ffn_down_bwd(dyb, wdown4, gu):
    t, d = dyb.shape
    w = wdown4.shape[1]
    tm = _tile(t, 512)

    def body(a, b, gu_ref, out):
        da = _dot(a[...], b[...], NT)
        g = gu_ref[0].astype(F32)
        u = gu_ref[1].astype(F32)
        s = _sigmoid(g)
        out[0] = (da * u * (s * (1.0 + g * (1.0 - s)))).astype(BF16)
        out[1] = (da * g * s).astype(BF16)

    return pl.pallas_call(
        body, grid=(4, t // tm), name="ffn_down_bwd",
        in_specs=[pl.BlockSpec((tm, d), lambda j, i: (i, 0)), pl.BlockSpec((None, w, d), lambda j, i: (j, 0, 0)),
                  pl.BlockSpec((2, None, tm, w), lambda j, i: (0, j, i, 0))],
        out_specs=pl.BlockSpec((2, None, tm, w), lambda j, i: (0, j, i, 0)),
        out_shape=SDS((2, 4, t, w), BF16),
        compiler_params=_cp("parallel", "parallel"),
    )(dyb, wdown4, gu)


def _ffn_up_bwd_x(dgu, wgu):
    _, t, w = dgu.shape
    d = wgu.shape[1]
    tm = _tile(t, 512)

    def body(a, b, out):
        @pl.when(pl.program_id(1) == 0)
        def _():
            out[...] = jnp.zeros_like(out)
        out[...] += _dot(a[...], b[...], NT)

    return pl.pallas_call(
        body, grid=(t // tm, 8), name="ffn_up_bwd_x",
        in_specs=[pl.BlockSpec((None, tm, w), lambda i, j: (j, i, 0)), pl.BlockSpec((None, d, w), lambda i, j: (j, 0, 0))],
        out_specs=pl.BlockSpec((tm, d), lambda i, j: (i, 0)), out_shape=SDS((t, d), F32),
        compiler_params=_cp("parallel", "arbitrary"),
    )(dgu, wgu)


def _ffn_up_bwd_w(h1n, dgu):
    _, t, w = dgu.shape
    d = h1n.shape[1]
    tm = _tile(d, 512)

    def body(a, b, out):
        out[...] = _dot(a[...], b[...], TN).astype(BF16)

    return pl.pallas_call(
        body, grid=(8, d // tm), name="ffn_up_bwd_w",
        in_specs=[pl.BlockSpec((t, tm), lambda j, i: (0, i)), pl.BlockSpec((None, t, w), lambda j, i: (j, 0, 0))],
        out_specs=pl.BlockSpec((None, tm, w), lambda j, i: (j, i, 0)), out_shape=SDS((8, d, w), BF16),
        compiler_params=_cp("parallel", "parallel"),
    )(h1n, dgu)


def _fox_prep(fq, fk, sm, fb, qg, kg, h):
    qn = _rms(fq, qg)
    kn = _rms(fk, kg)
    c = _cumsum_rows(-_softplus(-(sm + fb)))
    ccol = _lane_pick(c, L_FF + h)
    crow = jnp.sum(c.T * (_iota((HD, 1), 0) == L_FF + h).astype(F32), axis=0, keepdims=True)
    return qn, kn, ccol, crow


def _fox_block(q, k, v, cc, cr, off):
    s = _dot(q.astype(BF16), k.astype(BF16), NT) * (HD ** -0.5) + cc - cr
    s = jnp.where(_iota(s.shape, 1) <= _iota(s.shape, 0) + off, s, -1e30)
    e = jnp.exp(s - lax.stop_gradient(jnp.max(s, axis=1, keepdims=True)))
    p = e / jnp.sum(e, axis=1, keepdims=True)
    return _dot(p.astype(BF16), v.astype(BF16))


ONE_BUFFER = pl.Buffered(1)


def _pcol(t, cb):
    return pl.BlockSpec((t, HD), lambda h, cb=cb: (0, cb + h), pipeline_mode=ONE_BUFFER)


def _smcol(t):
    return pl.BlockSpec((t, HD), lambda h: (0, SM), pipeline_mode=ONE_BUFFER)


def _head(t):
    return pl.BlockSpec((t, HD), lambda h: (0, h), pipeline_mode=ONE_BUFFER)


def _small(n):
    return pl.BlockSpec((n, HD), lambda h: (0, 0), pipeline_mode=ONE_BUFFER)


def _fox_fwd(p, fb, qg, kg, bq):
    t = p.shape[0]

    def body(fq, fk, fv, sm, fb_r, qg_r, kg_r, o, qn_s, cc_s):
        h = pl.program_id(0)
        qn, kn, ccol, crow = _fox_prep(fq[...], fk[...], sm[...], fb_r[...], qg_r[...], kg_r[...], h)
        qn_s[...] = qn
        cc_s[...] = ccol
        knb = kn.astype(BF16)
        vb = fv[...].astype(BF16)

        def step(i, carry):
            off = pl.multiple_of(i * bq, bq)
            rows = pl.ds(off, bq)
            o[rows, :] = _fox_block(qn_s[rows, :], knb, vb, cc_s[rows, :], crow, off).astype(o.dtype)
            return carry

        lax.fori_loop(0, t // bq, step, 0)

    return pl.pallas_call(
        body, grid=(NF,), name="fox_fwd",
        in_specs=[_pcol(t, FQ), _pcol(t, FK), _pcol(t, FV), _smcol(t), _small(1), _small(1), _small(1)],
        out_specs=_head(t), out_shape=SDS((t, NF * HD), BF16),
        scratch_shapes=[pltpu.VMEM((t, HD), F32), pltpu.VMEM((t, 1), F32)],
        compiler_params=_cp("parallel"),
    )(p, p, p, p, fb, qg, kg)


def _fox_bwd(p, fb, qg, kg, dmix, bq, deps=()):
    t = p.shape[0]

    def body(*refs):
        fq, fk, fv, sm, fb_r, qg_r, kg_r, do = refs[:8]
        dfq, dfk, dfv, dsm, dfb, dqg, dkg, qn_s, cc_s, dqn_s, dcc_s, dkn_s, dv_s, dcr_s = refs[8 + len(deps):]
        h = pl.program_id(0)
        qn, kn, ccol, crow = _fox_prep(fq[...], fk[...], sm[...], fb_r[...], qg_r[...], kg_r[...], h)
        qn_s[...] = qn
        cc_s[...] = ccol
        v = fv[...]
        dkn_s[...] = jnp.zeros_like(dkn_s)
        dv_s[...] = jnp.zeros_like(dv_s)
        dcr_s[...] = jnp.zeros_like(dcr_s)

        def step(i, carry):
            off = pl.multiple_of(i * bq, bq)
            rows = pl.ds(off, bq)
            _, vjp = jax.vjp(lambda a, b, c, d, e: _fox_block(a, b, c, d, e, off),
                             qn_s[rows, :], kn, v, cc_s[rows, :], crow)
            dq, dk, dv, dcc, dcr = vjp(do[rows, :])
            dqn_s[rows, :] = dq
            dcc_s[rows, :] = dcc
            dkn_s[...] += dk
            dv_s[...] += dv
            dcr_s[...] += dcr
            return carry

        lax.fori_loop(0, t // bq, step, 0)
        _, prep_vjp = jax.vjp(lambda a, b, c, d, e, f: _fox_prep(a, b, c, d, e, f, h),
                              fq[...], fk[...], sm[...], fb_r[...], qg_r[...], kg_r[...])
        g_fq, g_fk, g_sm, g_fb, g_qg, g_kg = prep_vjp((dqn_s[...], dkn_s[...], dcc_s[...], dcr_s[...]))
        dfq[...] = g_fq.astype(dfq.dtype)
        dfk[...] = g_fk.astype(dfk.dtype)
        dfv[...] = dv_s[...].astype(dfv.dtype)

        @pl.when(h == 0)
        def _():
            for r in (dsm, dfb, dqg, dkg):
                r[...] = jnp.zeros_like(r)

        dsm[...] += g_sm
        dfb[...] += g_fb
        dqg[...] += g_qg
        dkg[...] += g_kg

    head = _head(t)
    return pl.pallas_call(
        body, grid=(NF,), name="fox_bwd",
        in_specs=[_pcol(t, FQ), _pcol(t, FK), _pcol(t, FV), _smcol(t), _small(1), _small(1), _small(1), head]
        + [ANY_SPEC] * len(deps),
        out_specs=[head, head, head, _small(t), _small(1), _small(1), _small(1)],
        out_shape=[SDS((t, NF * HD), BF16)] * 3 + [SDS((t, HD), F32)] + [SDS((1, HD), F32)] * 3,
        scratch_shapes=[pltpu.VMEM((t, HD), F32), pltpu.VMEM((t, 1), F32), pltpu.VMEM((t, HD), F32),
                        pltpu.VMEM((t, 1), F32), pltpu.VMEM((t, HD), F32), pltpu.VMEM((t, HD), F32),
                        pltpu.VMEM((1, t), F32)],
        compiler_params=_cp("arbitrary"),
    )(p, p, p, p, fb, qg, kg, dmix, *deps)


def _mem_attn(mq, mk, mv, qg, kg):
    s = _dot(_rms(mq, qg).astype(BF16), _rms(mk, kg).astype(BF16), NT) * (HD ** -0.5)
    e = jnp.exp(s - lax.stop_gradient(jnp.max(s, axis=1, keepdims=True)))
    p = e / jnp.sum(e, axis=1, keepdims=True)
    return _dot(p.astype(BF16), mv.astype(BF16))


def _mem_fwd(p, mkv, qg, kg):
    t, ml = p.shape[0], mkv.shape[0]

    def body(mq, mk, mv, qg_r, kg_r, o):
        o[...] = _mem_attn(mq[...], mk[...], mv[...], qg_r[...], kg_r[...]).astype(o.dtype)

    return pl.pallas_call(
        body, grid=(NM,), name="mem_fwd",
        in_specs=[_pcol(t, MQ), pl.BlockSpec((ml, HD), lambda h: (0, h)), pl.BlockSpec((ml, HD), lambda h: (0, NM + h)),
                  _small(1), _small(1)],
        out_specs=pl.BlockSpec((t, HD), lambda h: (0, h)), out_shape=SDS((t, NM * HD), BF16),
        compiler_params=_cp("parallel"),
    )(p, mkv, mkv, qg, kg)


def _mem_bwd(p, mkv, qg, kg, dmix, deps=()):
    t, ml = p.shape[0], mkv.shape[0]

    def body(*refs):
        mq, mk, mv, qg_r, kg_r, do = refs[:6]
        dmq, dmk, dmv, dqg, dkg = refs[6 + len(deps):]
        _, vjp = jax.vjp(_mem_attn, mq[...], mk[...], mv[...], qg_r[...], kg_r[...])
        g_q, g_k, g_v, g_qg, g_kg = vjp(do[...])
        dmq[...] = g_q.astype(dmq.dtype)
        dmk[...] = g_k
        dmv[...] = g_v

        @pl.when(pl.program_id(0) == 0)
        def _():
            dqg[...] = jnp.zeros_like(dqg)
            dkg[...] = jnp.zeros_like(dkg)

        dqg[...] += g_qg
        dkg[...] += g_kg

    return pl.pallas_call(
        body, grid=(NM,), name="mem_bwd",
        in_specs=[_pcol(t, MQ), pl.BlockSpec((ml, HD), lambda h: (0, h)), pl.BlockSpec((ml, HD), lambda h: (0, NM + h)),
                  _small(1), _small(1), pl.BlockSpec((t, HD), lambda h: (0, NF + NG + h))] + [ANY_SPEC] * len(deps),
        out_specs=[pl.BlockSpec((t, HD), lambda h: (0, h)), pl.BlockSpec((ml, HD), lambda h: (0, h)),
                   pl.BlockSpec((ml, HD), lambda h: (0, h)), _small(1), _small(1)],
        out_shape=[SDS((t, NM * HD), BF16), SDS((ml, NM * HD), F32), SDS((ml, NM * HD), F32),
                   SDS((1, HD), F32), SDS((1, HD), F32)],
        compiler_params=_cp("arbitrary"),
    )(p, mkv, mkv, qg, kg, dmix, *deps)


def _shift_down(x, s):
    if s == 0:
        return x
    return jnp.where(_iota(x.shape, 0) >= s, pltpu.roll(x, s, 0), 0.0)


def _shift_up(x, s):
    if s == 0:
        return x
    n = x.shape[0]
    return jnp.where(_iota(x.shape, 0) < n - s, pltpu.roll(x, n - s, 0), 0.0)


@jax.custom_vjp
def _conv4(x, w0, w1, w2, w3):
    return w0 * _shift_down(x, 3) + w1 * _shift_down(x, 2) + w2 * _shift_down(x, 1) + w3 * x


def _conv4_fwd(x, w0, w1, w2, w3):
    return _conv4(x, w0, w1, w2, w3), (x, w0, w1, w2, w3)


def _conv4_bwd(res, dy):
    x, w0, w1, w2, w3 = res
    dx = w0 * _shift_up(dy, 3) + w1 * _shift_up(dy, 2) + w2 * _shift_up(dy, 1) + w3 * dy
    dws = tuple(jnp.sum(dy * _shift_down(x, 3 - k), axis=0, keepdims=True) for k in range(4))
    return (dx,) + dws


_conv4.defvjp(_conv4_fwd, _conv4_bwd)


def _gdn_prep(gq, gk, gv, sm, taps, alog, dtb, h):
    q, k, v = [_silu(_conv4(x, *taps[4 * j:4 * j + 4])) for j, x in enumerate((gq, gk, gv))]
    q = q * lax.rsqrt(jnp.sum(q * q, axis=-1, keepdims=True) + NORM_EPS) * (HD ** -0.5)
    k = k * lax.rsqrt(jnp.sum(k * k, axis=-1, keepdims=True) + NORM_EPS)
    g = _lane_pick(-jnp.exp(alog) * _softplus(sm + dtb), L_GA + h)
    beta = _lane_pick(_sigmoid(sm), L_GB + h)
    return q, k, v, g, beta


def _split(x, n):
    parts, rest = [], x
    for i in range(n):
        parts.append(rest.astype(BF16))
        if i + 1 < n:
            rest = rest - parts[-1].astype(F32)
    return parts


def _raw_dot(a, b, form):
    lead = a.ndim - 2
    ca, cb = {"nn": (1, 0), "nt": (1, 1), "tn": (0, 0)}[form]
    batch = ((0,), (0,)) if lead else ((), ())
    return lax.dot_general(a, b, (((ca + lead,), (cb + lead,)), batch), preferred_element_type=F32)


def _pdot_impl(a, b, form, mode):
    if mode == "1":
        return _raw_dot(a.astype(BF16), b.astype(BF16), form)
    if mode == "3":
        (ah, al), (bh, bl) = _split(a, 2), _split(b, 2)
        return _raw_dot(ah, bh, form) + (_raw_dot(al, bh, form) + _raw_dot(ah, bl, form))
    if mode == "xa":
        return sum(_raw_dot(a.astype(BF16), t, form) for t in reversed(_split(b, 3)))
    return sum(_raw_dot(t, b.astype(BF16), form) for t in reversed(_split(a, 3)))


@functools.partial(jax.custom_vjp, nondiff_argnums=(2, 3))
def _pdot(a, b, form, mode):
    return _pdot_impl(a, b, form, mode)


def _pdot_fwd(a, b, form, mode):
    return _pdot_impl(a, b, form, mode), (a, b)


def _pdot_bwd(form, mode, res, ct):
    a, b = res
    da_args, db_args = {"nn": ((ct, b, "nt"), (a, ct, "tn")), "nt": ((ct, b, "nn"), (ct, a, "tn")),
                        "tn": ((b, ct, "nt"), (a, ct, "nn"))}[form]

    def side(args, exact):
        if mode in ("1", "3"):
            return mode
        return "xa" if args[0] is exact else "xb"

    if mode == "xa":
        return jnp.zeros_like(a), _pdot_impl(*db_args, side(db_args, a))
    if mode == "xb":
        return _pdot_impl(*da_args, side(da_args, b)), jnp.zeros_like(b)
    return _pdot_impl(*da_args, mode), _pdot_impl(*db_args, mode)


_pdot.defvjp(_pdot_fwd, _pdot_bwd)

GDN_QK, GDN_INV, GDN_SCAN = "1", "3", "1"


def _gdn_intra(q, k, v, g, beta):
    n = q.shape[0]
    r, c = _iota((CHUNK, CHUNK), 0), _iota((CHUNK, CHUNK), 1)
    tril, strict = r >= c, r > c
    trilf = jnp.broadcast_to(tril.astype(F32), (n, CHUNK, CHUNK))
    gcm = _pdot(trilf, jnp.broadcast_to(g, (n, CHUNK, CHUNK)), "nn", "xa")
    gcf = _pdot(trilf, jnp.broadcast_to(g, (n, CHUNK, HD)), "nn", "xa")
    lane0 = (_iota((1, 1, CHUNK), 2) == 0).astype(F32)
    gcr = _pdot(jnp.ones((n, CHUNK, CHUNK), F32), gcm * lane0, "nt", "xa")
    decay = jnp.where(tril, jnp.exp(jnp.where(tril, gcm - gcr, 0.0)), 0.0)
    egc = jnp.exp(gcf)
    kb = k * beta
    low = jnp.where(strict, _pdot(kb, k, "nt", GDN_QK) * decay, 0.0)
    inv = (r == c).astype(F32) - low
    pw = low
    for _ in range(5):
        pw = _pdot(pw, pw, "nn", GDN_INV)
        inv = inv + _pdot(inv, pw, "nn", GDN_INV)
    u = _pdot(inv, v * beta, "nn", GDN_INV)
    w = _pdot(inv, kb * egc, "nn", GDN_INV)
    at = jnp.where(tril, _pdot(q, k, "nt", GDN_QK) * decay, 0.0)
    gl = jnp.sum(jnp.broadcast_to(g, (n, CHUNK, HD)), axis=1, keepdims=True)
    return u, w, q * egc, at, k * jnp.exp(gl - gcf), gl


def _gdn_step(s, u, w, qg, at, kd, gl):
    vn = u - _pdot(w, s, "nn", GDN_SCAN)
    o = _pdot(qg, s, "nn", GDN_SCAN) + _pdot(at, vn, "nn", GDN_SCAN)
    s2 = s * jnp.exp(gl) + _pdot(kd, vn, "tn", GDN_SCAN)
    return o, s2


def _gdn_scratch(nc):
    big = pltpu.VMEM((nc, CHUNK, HD), F32)
    return [big, big, big, pltpu.VMEM((nc, CHUNK, 1), F32), pltpu.VMEM((nc, CHUNK, 1), F32),
            big, big, big, pltpu.VMEM((nc, CHUNK, CHUNK), F32), big, pltpu.VMEM((nc, 1, HD), F32)]


def _gdn_in_specs(t):
    cw = lambda cb: pl.BlockSpec((4, HD), lambda h, cb=cb: (0, cb + h))
    return [_pcol(t, GQ), _pcol(t, GK), _pcol(t, GV), _smcol(t), cw(0), cw(NG), cw(2 * NG), _small(1), _small(1)]


def _taps(wq, wk, wv):
    return tuple(w[k:k + 1, :] for w in (wq, wk, wv) for k in range(4))


def _gdn_stage(vals, refs):
    nc = refs[0].shape[0]
    for v, r in zip(vals, refs):
        r[...] = v.reshape(nc, CHUNK, v.shape[-1])


def _gdn_intra_all(chunked, intra):
    nc = chunked[0].shape[0]
    grp_n = math.gcd(nc, GROUP)

    def grp(i, carry):
        sl = pl.ds(pl.multiple_of(i * grp_n, grp_n), grp_n)
        for r, val in zip(intra, _gdn_intra(*[c[sl] for c in chunked])):
            r[sl] = val
        return carry

    lax.fori_loop(0, nc // grp_n, grp, 0)


def _gdn_fwd(p, conv, alog, dtb):
    t = p.shape[0]
    nc = t // CHUNK

    def body(gq, gk, gv, sm, wq, wk, wv, al, db, o, *scr):
        h = pl.program_id(0)
        chunked, intra = scr[:5], scr[5:]
        _gdn_stage(_gdn_prep(gq[...], gk[...], gv[...], sm[...], _taps(wq, wk, wv), al[...], db[...], h), chunked)
        _gdn_intra_all(chunked, intra)

        def step(c, s):
            oc, s2 = _gdn_step(s, *[r[c] for r in intra])
            o[pl.ds(pl.multiple_of(c * CHUNK, CHUNK), CHUNK), :] = oc
            return s2

        lax.fori_loop(0, nc, step, jnp.zeros((HD, HD), F32))

    return pl.pallas_call(
        body, grid=(NG,), name="gdn_fwd", in_specs=_gdn_in_specs(t),
        out_specs=_head(t), out_shape=SDS((t, NG * HD), F32),
        scratch_shapes=_gdn_scratch(nc), compiler_params=_cp("parallel"),
    )(p, p, p, p, conv, conv, conv, alog, dtb)


def _gdn_bwd(p, conv, alog, dtb, do_raw):
    t = p.shape[0]
    nc = t // CHUNK

    def body(gq, gk, gv, sm, wq, wk, wv, al, db, do, dgq, dgk, dgv, dsm, dwq, dwk, dwv, dal, ddb, *scr):
        h = pl.program_id(0)
        chunked, intra, states = scr[:5], scr[5:11], scr[11]
        _gdn_stage(_gdn_prep(gq[...], gk[...], gv[...], sm[...], _taps(wq, wk, wv), al[...], db[...], h), chunked)
        _gdn_intra_all(chunked, intra)

        def fwd(c, s):
            states[c] = s
            return _gdn_step(s, *[r[c] for r in intra])[1]

        lax.fori_loop(0, nc, fwd, jnp.zeros((HD, HD), F32))

        def bwd(i, ds):
            c = nc - 1 - i
            _, vjp = jax.vjp(_gdn_step, states[c], *[r[c] for r in intra])
            grads = vjp((do[pl.ds(pl.multiple_of(c * CHUNK, CHUNK), CHUNK), :], ds))
            for r, gval in zip(intra, grads[1:]):
                r[c] = gval
            return grads[0]

        lax.fori_loop(0, nc, bwd, jnp.zeros((HD, HD), F32))

        grp_n = math.gcd(nc, GROUP)

        def grp(i, carry):
            sl = pl.ds(pl.multiple_of(i * grp_n, grp_n), grp_n)
            _, vjp = jax.vjp(_gdn_intra, *[r[sl] for r in chunked])
            for r, gval in zip(chunked, vjp(tuple(r[sl] for r in intra))):
                r[sl] = gval
            return carry

        lax.fori_loop(0, nc // grp_n, grp, 0)
        _, prep_vjp = jax.vjp(
            lambda *a: _gdn_prep(*a, h), gq[...], gk[...], gv[...], sm[...], _taps(wq, wk, wv), al[...], db[...])
        grads = prep_vjp(tuple(r[...].reshape(t, r.shape[-1]) for r in chunked))
        for r, gval in zip((dgq, dgk, dgv), grads[:3]):
            r[...] = gval.astype(r.dtype)
        for j, r in enumerate((dwq, dwk, dwv)):
            for k in range(4):
                r[k:k + 1, :] = grads[4][4 * j + k]

        @pl.when(h == 0)
        def _():
            for r in (dsm, dal, ddb):
                r[...] = jnp.zeros_like(r)

        dsm[...] += grads[3]
        dal[...] += grads[5]
        ddb[...] += grads[6]

    head = _head(t)
    taps = pl.BlockSpec((4, HD), lambda h: (0, h))
    return pl.pallas_call(
        body, grid=(NG,), name="gdn_bwd", in_specs=_gdn_in_specs(t) + [head],
        out_specs=[head, head, head, _small(t), taps, taps, taps, _small(1), _small(1)],
        out_shape=[SDS((t, NG * HD), BF16)] * 3 + [SDS((t, HD), F32)] + [SDS((4, NG * HD), F32)] * 3 + [SDS((1, HD), F32)] * 2,
        scratch_shapes=_gdn_scratch(nc) + [pltpu.VMEM((nc, HD, HD), F32)], compiler_params=_cp("arbitrary"),
    )(p, p, p, p, conv, conv, conv, alog, dtb, do_raw)


def _gdn_post(o, z, gain):
    return (jnp.concatenate(
        [_rms(o[:, h * HD:(h + 1) * HD], gain) * _silu(z[:, h * HD:(h + 1) * HD]) for h in range(NG)], axis=1),)


def _place():
    return lax.axis_index("x"), lax.axis_index("y"), lax.axis_index("c")


def _all_gather(name, shard):
    def body(x_ref, out_ref, send_sems, recv_sems, local_sem):
        x, y, c = _place()
        me, sibling = (x, y, c), (x, y, 1 - c)
        chips = [(1 - x, y), (x, 1 - y), (1 - x, 1 - y)]

        def blk(px, py, pc):
            return out_ref.at[4 * px + 2 * py + pc]

        def copy(k, block, to, src=None):
            return pltpu.make_async_remote_copy(
                src_ref=blk(*block) if src is None else src, dst_ref=blk(*block),
                send_sem=send_sems.at[k], recv_sem=recv_sems.at[k], device_id=to, device_id_type=MESH)

        mine = pltpu.make_async_copy(x_ref, blk(*me), local_sem)
        mine.start()
        first = [copy(0, me, sibling, src=x_ref)]
        first += [copy(1 + j, me, (*chip, c), src=x_ref) for j, chip in enumerate(chips)]
        for cp in first:
            cp.start()
        passed = [copy(4 + j, (*chip, c), sibling) for j, chip in enumerate(chips)]
        for j, chip in enumerate(chips):
            copy(1 + j, (*chip, c), me).wait_recv()
            passed[j].start()
        copy(0, sibling, me).wait_recv()
        for j, chip in enumerate(chips):
            copy(4 + j, (*chip, 1 - c), me).wait_recv()
        for cp in first + passed:
            cp.wait_send()
        mine.wait()

    return pl.pallas_call(
        body, name=name, out_shape=SDS((N_DEV,) + shard.shape, shard.dtype),
        in_specs=[pl.BlockSpec(memory_space=pltpu.HBM)], out_specs=pl.BlockSpec(memory_space=pltpu.HBM),
        scratch_shapes=[pltpu.SemaphoreType.DMA((7,)), pltpu.SemaphoreType.DMA((7,)), pltpu.SemaphoreType.DMA],
    )(shard)


def _scatter_exchange(name, full):
    def body(g_ref, out_ref, send_sems, recv_sems, local_sem):
        x, y, c = _place()
        me = 4 * x + 2 * y + c
        mine = pltpu.make_async_copy(g_ref.at[me], out_ref.at[me], local_sem)
        mine.start()
        sends, recvs = [], []
        for k in range(1, N_DEV):
            px = 1 - x if k & 4 else x
            py = 1 - y if k & 2 else y
            pc = 1 - c if k & 1 else c
            peer = 4 * px + 2 * py + pc
            sends.append(pltpu.make_async_remote_copy(
                src_ref=g_ref.at[peer], dst_ref=out_ref.at[me], send_sem=send_sems.at[k - 1],
                recv_sem=recv_sems.at[k - 1], device_id=(px, py, pc), device_id_type=MESH))
            recvs.append(pltpu.make_async_remote_copy(
                src_ref=g_ref.at[me], dst_ref=out_ref.at[peer], send_sem=send_sems.at[k - 1],
                recv_sem=recv_sems.at[k - 1], device_id=(px, py, pc), device_id_type=MESH))
        for cp in sends:
            cp.start()
        for cp in recvs:
            cp.wait_recv()
        for cp in sends:
            cp.wait_send()
        mine.wait()

    return pl.pallas_call(
        body, name=name, out_shape=SDS(full.shape, full.dtype),
        in_specs=[pl.BlockSpec(memory_space=pltpu.HBM)], out_specs=pl.BlockSpec(memory_space=pltpu.HBM),
        scratch_shapes=[pltpu.SemaphoreType.DMA((7,)), pltpu.SemaphoreType.DMA((7,)), pltpu.SemaphoreType.DMA],
    )(full)


def _sum_blocks(name, parts):
    _, r, c = parts.shape
    tr = 64 if r % 64 == 0 else r

    def body(x, o):
        acc = x[0].astype(F32)
        for d in range(1, N_DEV):
            acc = acc + x[d].astype(F32)
        o[...] = acc

    return pl.pallas_call(
        body, grid=(r // tr,), name=name, in_specs=[pl.BlockSpec((N_DEV, tr, c), lambda i: (0, i, 0))],
        out_specs=pl.BlockSpec((tr, c), lambda i: (i, 0)), out_shape=SDS((r, c), F32), compiler_params=_cp("parallel"),
    )(parts)


def _reduce_scatter(name, full):
    return _sum_blocks(name + "_sum", _scatter_exchange(name, full))


def _all_reduce_small(name, x, reduce):
    m_per, n = x.shape

    def body(x_ref, out_ref, send_sems, recv_sems, local_sem):
        px, py, pc = _place()
        me, sibling = (px, py, pc), (px, py, 1 - pc)
        chips = [(1 - px, py), (px, 1 - py), (1 - px, 1 - py)]
        buf = out_ref

        def rows(qx, qy, qc):
            return buf.at[pl.ds((4 * qx + 2 * qy + qc) * m_per, m_per), :]

        def copy(k, block, to, src=None):
            return pltpu.make_async_remote_copy(
                src_ref=rows(*block) if src is None else src, dst_ref=rows(*block),
                send_sem=send_sems.at[k], recv_sem=recv_sems.at[k], device_id=to, device_id_type=MESH)

        mine = pltpu.make_async_copy(x_ref, rows(*me), local_sem)
        mine.start()
        first = [copy(0, me, sibling, src=x_ref)]
        first += [copy(1 + j, me, (*chip, pc), src=x_ref) for j, chip in enumerate(chips)]
        for cp in first:
            cp.start()
        passed = [copy(4 + j, (*chip, pc), sibling) for j, chip in enumerate(chips)]
        for j, chip in enumerate(chips):
            copy(1 + j, (*chip, pc), me).wait_recv()
            passed[j].start()
        copy(0, sibling, me).wait_recv()
        for j, chip in enumerate(chips):
            copy(4 + j, (*chip, 1 - pc), me).wait_recv()
        for cp in first + passed:
            cp.wait_send()
        mine.wait()

    gathered = pl.pallas_call(
        body, name=name, out_shape=SDS((N_DEV * m_per, n), x.dtype),
        in_specs=[pl.BlockSpec(memory_space=pltpu.VMEM)], out_specs=pl.BlockSpec(memory_space=pltpu.VMEM),
        scratch_shapes=[pltpu.SemaphoreType.DMA((7,)), pltpu.SemaphoreType.DMA((7,)), pltpu.SemaphoreType.DMA],
    )(x)
    if not reduce:
        return gathered
    return _sum_blocks(name + "_sum", gathered.reshape(N_DEV, m_per, n))


HBM_SPEC = pl.BlockSpec(memory_space=pltpu.HBM)
SEM_SPEC = pl.BlockSpec(memory_space=pltpu.SEMAPHORE)
EFFECT = pltpu.SideEffectType.DATAFLOW_SIDE_EFFECTING


def _copies_start(name, bufs, n_remote, n_local, build, deps):
    nb, nd = len(bufs), len(deps)
    sem_shapes = [pltpu.SemaphoreType.DMA((n_remote,)), pltpu.SemaphoreType.DMA((n_remote,))]
    if n_local:
        sem_shapes.append(pltpu.SemaphoreType.DMA((n_local,)))
    ns = len(sem_shapes)

    def body(*refs):
        sems = refs[nb + nd:nb + nd + ns]
        remote, local = build(refs[:nb], *sems, *([None] * (3 - ns)))
        for cp in local + remote:
            cp.start()
        refs[-1][...] = jnp.zeros((8, HD), F32)

    outs = pl.pallas_call(
        body, name=name,
        out_shape=(*sem_shapes, *[pltpu.HBM(b.shape, b.dtype) for b in bufs], SDS((8, HD), F32)),
        in_specs=[HBM_SPEC] * nb + [ANY_SPEC] * nd,
        out_specs=(*[SEM_SPEC] * ns, *[HBM_SPEC] * nb, pl.BlockSpec(memory_space=pltpu.VMEM)),
        input_output_aliases={i: ns + i for i in range(nb)},
        compiler_params=pltpu.CompilerParams(has_side_effects=EFFECT),
    )(*[pltpu.with_memory_space_constraint(b, pltpu.HBM) for b in bufs], *deps)
    return list(outs[:ns]), list(outs[ns:ns + nb]), outs[-1]


def _copies_wait(name, bufs, sems, build, after):
    nb, ns = len(bufs), len(sems)

    def body(*refs):
        remote, local = build(refs[:nb], *refs[nb:nb + ns], *([None] * (3 - ns)))
        for cp in local:
            cp.wait()
        for cp in remote:
            cp.wait_send()
            cp.wait_recv()

    outs = pl.pallas_call(
        body, name=name, out_shape=tuple(pltpu.HBM(b.shape, b.dtype) for b in bufs),
        in_specs=[HBM_SPEC] * nb + [SEM_SPEC] * ns + [ANY_SPEC] * len(after), out_specs=tuple([HBM_SPEC] * nb),
        input_output_aliases={i: i for i in range(nb)},
        compiler_params=pltpu.CompilerParams(has_side_effects=EFFECT),
    )(*bufs, *sems, *after)
    return list(outs)


def _remote(src, dst, send, recv, k, to):
    return pltpu.make_async_remote_copy(src_ref=src, dst_ref=dst, send_sem=send.at[k], recv_sem=recv.at[k],
                                        device_id=to, device_id_type=MESH)


class _Gather:
    def __init__(self, name, shards, deps):
        self.name, self.n = name, len(shards)
        lands = [lax.empty((N_DEV,) + s.shape, s.dtype) for s in shards]
        self.sems, bufs, self.token = _copies_start(name + "_s1", list(shards) + lands, 4 * self.n, self.n, self._stage1, deps)
        self.shards, self.lands = bufs[:self.n], bufs[self.n:]

    def _stage1(self, refs, send, recv, loc):
        x, y, c = _place()
        me = 4 * x + 2 * y + c
        targets = [(x, y, 1 - c), (1 - x, y, c), (x, 1 - y, c), (1 - x, 1 - y, c)]
        remote, local = [], []
        for i in range(self.n):
            src, land = refs[i], refs[self.n + i]
            local.append(pltpu.make_async_copy(src, land.at[me], loc.at[i]))
            remote += [_remote(src, land.at[me], send, recv, 4 * i + k, to) for k, to in enumerate(targets)]
        return remote, local

    def _stage2(self, refs, send, recv, loc):
        x, y, c = _place()
        remote = []
        for i in range(self.n):
            for j, (cx, cy) in enumerate([(1 - x, y), (x, 1 - y), (1 - x, 1 - y)]):
                blk = refs[i].at[4 * cx + 2 * cy + c]
                remote.append(_remote(blk, blk, send, recv, 3 * i + j, (x, y, 1 - c)))
        return remote, []

    def mid(self, after):
        bufs = _copies_wait(self.name + "_w1", self.shards + self.lands, self.sems, self._stage1, after)
        self.sems, self.lands, self.token = _copies_start(self.name + "_s2", bufs[self.n:], 3 * self.n, 0, self._stage2, ())

    def end(self, after):
        return _copies_wait(self.name + "_w2", self.lands, self.sems, self._stage2, after)


def _pair_add(name, g, got, c):
    _, r, cols = g.shape
    tr = 64 if r % 64 == 0 else r

    def body(s, a, b, o):
        o[...] = (a[...].astype(F32) + b[...].astype(F32)).astype(o.dtype)

    return pl.pallas_call(
        body, name=name, out_shape=SDS((4, r, cols), g.dtype),
        grid_spec=pltpu.PrefetchScalarGridSpec(
            num_scalar_prefetch=1, grid=(4, r // tr),
            in_specs=[pl.BlockSpec((None, tr, cols), lambda j, i, s: (2 * j + s[0], i, 0)),
                      pl.BlockSpec((None, tr, cols), lambda j, i, s: (j, i, 0))],
            out_specs=pl.BlockSpec((None, tr, cols), lambda j, i, s: (j, i, 0))),
        compiler_params=_cp("parallel", "parallel"),
    )(c.reshape(1), g, got)


def _quad_sum(name, part, got, chip):
    _, r, cols = part.shape
    tr = 64 if r % 64 == 0 else r

    def body(s, a, b1, b2, b3, o):
        o[...] = ((a[...].astype(F32) + b1[...].astype(F32)) + b2[...].astype(F32)) + b3[...].astype(F32)

    blk = lambda k: pl.BlockSpec((None, tr, cols), lambda i, s, k=k: (jnp.bitwise_xor(s[0], k), i, 0))
    return pl.pallas_call(
        body, name=name, out_shape=SDS((r, cols), F32),
        grid_spec=pltpu.PrefetchScalarGridSpec(
            num_scalar_prefetch=1, grid=(r // tr,), in_specs=[blk(0), blk(1), blk(2), blk(3)],
            out_specs=pl.BlockSpec((tr, cols), lambda i, s: (i, 0))),
        compiler_params=_cp("parallel"),
    )(chip.reshape(1), part, got, got, got)


class _Scatter:
    def __init__(self, name, grads, deps):
        self.name, self.n = name, len(grads)
        got = [lax.empty((4,) + g.shape[1:], g.dtype) for g in grads]
        self.sems, bufs, self.token = _copies_start(name + "_s1", list(grads) + got, 4 * self.n, 0, self._stage1, deps)
        self.grads, self.got = bufs[:self.n], bufs[self.n:]

    def _stage1(self, refs, send, recv, loc):
        x, y, c = _place()
        remote = []
        for i in range(self.n):
            remote += [_remote(refs[i].at[2 * j + 1 - c], refs[self.n + i].at[j], send, recv, 4 * i + j, (x, y, 1 - c))
                       for j in range(4)]
        return remote, []

    def _stage2(self, refs, send, recv, loc):
        x, y, c = _place()
        remote = []
        for i in range(self.n):
            for k in (1, 2, 3):
                tx = 1 - x if k & 2 else x
                ty = 1 - y if k & 1 else y
                remote.append(_remote(refs[i].at[2 * tx + ty], refs[self.n + i].at[2 * x + y], send, recv,
                                      3 * i + k - 1, (tx, ty, c)))
        return remote, []

    def mid(self, after):
        bufs = _copies_wait(self.name + "_w1", self.grads + self.got, self.sems, self._stage1, after)
        c = lax.axis_index("c").astype(jnp.int32)
        parts = [_pair_add(f"{self.name}_add{i}", bufs[i], bufs[self.n + i], c) for i in range(self.n)]
        got = [lax.empty(p.shape, p.dtype) for p in parts]
        self.sems, bufs, self.token = _copies_start(self.name + "_s2", parts + got, 3 * self.n, 0, self._stage2, ())
        self.parts, self.got = bufs[:self.n], bufs[self.n:]

    def end(self, after):
        bufs = _copies_wait(self.name + "_w2", self.parts + self.got, self.sems, self._stage2, after)
        chip = (2 * lax.axis_index("x") + lax.axis_index("y")).astype(jnp.int32)
        return [_quad_sum(f"{self.name}_sum{i}", bufs[i], bufs[self.n + i], chip) for i in range(self.n)]


def _adamw(w, g, m, v):
    m = ADAM_B1 * m + (1.0 - ADAM_B1) * g
    v = ADAM_B2 * v + (1.0 - ADAM_B2) * (g * g)
    m_hat = m / (1.0 - ADAM_B1 ** ADAM_STEP)
    v_hat = v / (1.0 - ADAM_B2 ** ADAM_STEP)
    return -ADAM_LR * (m_hat / (jnp.sqrt(v_hat) + ADAM_EPS) + ADAM_WD * w), m, v


def _adamw_call(name, w, g, m, v):
    r, c = w.shape
    tm = 64 if r % 64 == 0 else r
    return _rowwise(name, _adamw, [w, g, m, v], [], [(c, F32)] * 3, tm)


_IN_COLS = 5906


def _perm_in(w):
    pad = jnp.zeros((w.shape[0], PC - _IN_COLS), w.dtype)
    return jnp.concatenate([w[:, :2304], w[:, 2310:4614], w[:, 4614:5382], w[:, 5394:5906], w[:, 2304:2310],
                            w[:, 5382:5394], pad], axis=1)


def _unperm_in(g):
    return jnp.concatenate([g[:, :2304], g[:, 5888:5894], g[:, 2304:4608], g[:, 4608:5376], g[:, 5894:5906],
                            g[:, 5376:5888]], axis=1)


def _lanes(v, at):
    return jnp.pad(v, ((0, 0), (at, HD - at - v.shape[1])))


_PACK = ("norm_mix", "mem_norm", "norm_ffn", "gdn_conv", "fox_q_norm", "fox_k_norm", "gdn_out_norm", "mem_q_norm",
         "mem_k_norm", "fox_f_bias", "gdn_a_log", "gdn_dt_bias", "loss")


def _pack(vals):
    parts = [vals[n].reshape(-1, HD) for n in _PACK]
    used = sum(p.shape[0] for p in parts)
    buf = jnp.concatenate(parts + [jnp.zeros((-used % 8, HD), F32)], axis=0)
    return buf, [(n, p.shape[0]) for n, p in zip(_PACK, parts)]


def _unpack(buf, layout):
    out, at = {}, 0
    for n, rows in layout:
        out[n] = buf[at:at + rows]
        at += rows
    return out


def kernel(x, mem, norm_mix, w_in, fox_f_bias, fox_q_norm, fox_k_norm, gdn_conv, gdn_a_log, gdn_dt_bias, gdn_out_norm, mem_norm, w_mem_kv, mem_q_norm, mem_k_norm, w_out, norm_ffn, w_gate_up, w_down, loss_target, m_norm_mix, m_w_in, m_fox_f_bias, m_fox_q_norm, m_fox_k_norm, m_gdn_conv, m_gdn_a_log, m_gdn_dt_bias, m_gdn_out_norm, m_mem_norm, m_w_mem_kv, m_mem_q_norm, m_mem_k_norm, m_w_out, m_norm_ffn, m_w_gate_up, m_w_down, v_norm_mix, v_w_in, v_fox_f_bias, v_fox_q_norm, v_fox_k_norm, v_gdn_conv, v_gdn_a_log, v_gdn_dt_bias, v_gdn_out_norm, v_mem_norm, v_w_mem_kv, v_mem_q_norm, v_mem_k_norm, v_w_out, v_norm_ffn, v_w_gate_up, v_w_down):
    args = dict(locals())
    d = x.shape[2]
    me = 4 * lax.axis_index("x") + 2 * lax.axis_index("y") + lax.axis_index("c")

    w_in_all = _all_gather("ag_w_in", _perm_in(w_in[0]).astype(BF16)).reshape(d, PC)
    w_kv_all = _all_gather("ag_w_kv", w_mem_kv[0].astype(BF16)).reshape(d, 2 * NM * HD)
    cshard = gdn_conv[0].shape[1]
    conv_pad = jnp.pad(gdn_conv[0], ((0, 4), (0, 3 * HD - cshard)))
    conv_all = _all_reduce_small("ag_conv", conv_pad, False).reshape(N_DEV, 8, 3 * HD)[:, :4, :cshard]
    conv_all = conv_all.transpose(1, 0, 2).reshape(4, N_DEV * cshard)
    comm = _StepComm([w_out[0].astype(BF16), w_gate_up[0].astype(BF16), w_down[0].astype(BF16)], [w_in_all, w_kv_all])

    grad_x, loss_local, big_grads, small_grads = _local_step(
        x[0], mem[0], loss_target[0], norm_mix, fox_f_bias, fox_q_norm, fox_k_norm, gdn_a_log, gdn_dt_bias,
        gdn_out_norm, mem_norm, mem_q_norm, mem_k_norm, norm_ffn, w_in_all, w_kv_all, conv_all, comm)

    blocks = lambda g: g.reshape(N_DEV, g.shape[0] // N_DEV, g.shape[1])
    last = _Scatter("rs_b", [blocks(big_grads[n]) for n in ("w_in", "w_mem_kv", "w_out")], ())
    last.mid(())
    grads = dict(zip(("w_down", "w_gate_up"), comm.ffn_grads_end([grad_x])))
    grads.update(zip(("w_in", "w_mem_kv", "w_out"), last.end(())))
    grads["w_in"] = _unperm_in(grads["w_in"])
    small_grads["loss"] = jnp.broadcast_to(loss_local, (1, HD))
    packed, layout = _pack(small_grads)
    small = _unpack(_all_reduce_small("ar_small", packed, True), layout)
    loss = small["loss"][0, 0]
    six = {"fox_f_bias": L_FF, "gdn_a_log": L_GA, "gdn_dt_bias": L_GA}
    for n, rows_n in layout[:-1]:
        gsm = small[n]
        if n == "gdn_conv":
            gsm = lax.dynamic_slice(gsm.reshape(4, N_DEV * cshard), (0, me * cshard), (4, cshard))[None]
        elif n in six:
            gsm = gsm[:, six[n]:six[n] + 6]
        else:
            gsm = gsm.reshape(1, rows_n * HD)
        grads[n] = gsm

    names = ['norm_mix', 'w_in', 'fox_f_bias', 'fox_q_norm', 'fox_k_norm', 'gdn_conv', 'gdn_a_log', 'gdn_dt_bias',
             'gdn_out_norm', 'mem_norm', 'w_mem_kv', 'mem_q_norm', 'mem_k_norm', 'w_out', 'norm_ffn', 'w_gate_up', 'w_down']
    big = ("w_in", "w_mem_kv", "w_out", "w_gate_up", "w_down")
    delta, new_m, new_v = {}, {}, {}
    for n in big:
        delta[n], new_m[n], new_v[n] = [a[None] for a in _adamw_call(
            "adamw_" + n, args[n][0], grads[n], args["m_" + n][0], args["v_" + n][0])]
        grads[n] = grads[n][None]

    def flat(a):
        a = a.reshape(1, -1)
        return jnp.pad(a, ((0, 0), (0, -a.shape[1] % HD))).reshape(-1, HD)

    smalls = [n for n in names if n not in big]
    pk = lambda pre: jnp.concatenate([flat(grads[n] if pre == "g" else args[pre + n]) for n in smalls], axis=0)
    cat = [pk(""), pk("g"), pk("m_"), pk("v_")]
    padr = -cat[0].shape[0] % 8
    cat = [jnp.pad(a, ((0, padr), (0, 0))) for a in cat]
    res = _adamw_call("adamw_small", *cat)
    at = 0
    for n in smalls:
        shape = args[n].shape
        size = math.prod(shape)
        nrow = -(-size // HD)
        for dst, src in zip((delta, new_m, new_v), res):
            dst[n] = src[at:at + nrow].reshape(-1)[:size].reshape(shape)
        at += nrow

    return (loss, grad_x[None], *[grads[n] for n in names], *[delta[n] for n in names],
            *[new_m[n] for n in names], *[new_v[n] for n in names])


class _StepComm:
    def __init__(self, late_shards, after):
        self.gather = _Gather("ag_late", late_shards, after)

    def start_deps(self):
        return [self.gather.token]

    def after_mixers(self, after):
        self.gather.mid(after)
        return [self.gather.token]

    def late_weights(self, after):
        w_out_all, wgu, w_down_all = self.gather.end(after)
        return w_out_all.reshape(-1, w_out_all.shape[-1]), wgu, w_down_all.reshape(-1, w_down_all.shape[-1])

    def ffn_grads(self, g_w_down, g_w_gu):
        self.scatter = _Scatter("rs_ffn", [g_w_down.reshape(N_DEV, -1, g_w_down.shape[1]), g_w_gu], ())
        return [self.scatter.token]

    def ffn_grads_mid(self, after):
        self.scatter.mid(after)
        return [self.scatter.token]

    def ffn_grads_end(self, after):
        return self.scatter.end(after)


def _local_step(xs, ms, tgt, norm_mix, fox_f_bias, fox_q_norm, fox_k_norm, gdn_a_log, gdn_dt_bias, gdn_out_norm,
                mem_norm, mem_q_norm, mem_k_norm, norm_ffn, w_in_all, w_kv_all, conv_all, comm):
    t, d = xs.shape
    bq = min(t, 256)
    fb, alog, dtb = _lanes(fox_f_bias, L_FF), _lanes(gdn_a_log, L_GA), _lanes(gdn_dt_bias, L_GA)

    rms1 = lambda a, g: (_rms(a, g),)
    (u,) = _rowwise("norm_mix", rms1, [xs], [norm_mix], [(d, BF16)], min(t, 256), deps=comm.start_deps())
    p = _matmul("proj_in", u, w_in_all, NN, F32, 1024, 768)
    o_fox = _fox_fwd(p, fb, fox_q_norm, fox_k_norm, bq)
    o_gdn_raw = _gdn_fwd(p, conv_all, alog, dtb)
    zrow = (p, NG * HD, GZ * HD // (NG * HD))
    (o_gdn,) = _rowwise("gdn_post", _gdn_post, [o_gdn_raw, zrow], [gdn_out_norm], [(NG * HD, BF16)], min(t, 256))
    deps = comm.after_mixers([o_fox, o_gdn])
    (mem_n,) = _rowwise("norm_mem", rms1, [ms], [mem_norm], [(d, BF16)], ms.shape[0], deps=deps)
    mkv = _matmul("proj_mem", mem_n, w_kv_all, NN, F32, 256, 512)
    o_mem = _mem_fwd(p, mkv, mem_q_norm, mem_k_norm)
    w_out_all, wgu, w_down_all = comm.late_weights([o_mem])
    ffw = wgu.shape[2]
    mix = jnp.concatenate([o_fox, o_gdn, o_mem], axis=1)
    h1 = _matmul("proj_out", mix, w_out_all, NN, F32, 1024, 512, residual=xs)
    (h1n,) = _rowwise("norm_ffn", rms1, [h1], [norm_ffn], [(d, BF16)], min(t, 256))
    gu, act = _ffn_up(h1n, wgu.reshape(2, 4, d, ffw))
    dy, dyb, lsum = _ffn_down_loss(act, w_down_all, h1, tgt)
    loss_local = (0.5 / d) * jnp.sum(lsum[::8, ::HD])

    dgu = _ffn_down_bwd(dyb, w_down_all.reshape(4, ffw, d), gu).reshape(8, t, ffw)
    g_w_down = _matmul("grad_w_down", act, dyb, TN, BF16, 512, 512)
    dh1n = _ffn_up_bwd_x(dgu, wgu)
    g_w_gu = _ffn_up_bwd_w(h1n, dgu)
    deps = comm.ffn_grads(g_w_down, g_w_gu)
    rms2 = lambda a, g: (_rms(a, g), a)
    dh1, g_norm_ffn = _rowwise_vjp("norm_ffn_bwd", rms2, [h1], [norm_ffn], [dh1n, dy], [F32], min(t, 256), deps=deps)
    dh1b = dh1.astype(BF16)

    dmix = _matmul("proj_out_bwd_x", dh1b, w_out_all, NT, F32, 1024, 512)
    g_w_out = _matmul("grad_w_out", mix, dh1b, TN, BF16, 512, 512)
    deps = comm.ffn_grads_mid([dmix, g_w_out])
    dfq, dfk, dfv, dsm_fox, g_fb, g_fqn, g_fkn = _fox_bwd(p, fb, fox_q_norm, fox_k_norm, dmix, bq, deps=deps)
    do_raw, dgz, g_gon = _rowwise_vjp("gdn_post_bwd", _gdn_post, [o_gdn_raw, zrow], [gdn_out_norm],
                                      [(dmix, NG * HD, 1)], [F32, BF16], min(t, 256), deps=deps)
    dgq, dgk, dgv, dsm_gdn, dwq, dwk, dwv, g_alog, g_dtb = _gdn_bwd(p, conv_all, alog, dtb, do_raw)
    dmq, dmk, dmv, g_mqn, g_mkn = _mem_bwd(p, mkv, mem_q_norm, mem_k_norm, dmix, deps=deps)
    dmkv = jnp.concatenate([dmk, dmv], axis=1).astype(BF16)
    dmem_n = _matmul("proj_mem_bwd_x", dmkv, w_kv_all, NT, F32, 256, 512)
    g_w_kv = _matmul("grad_w_kv", mem_n, dmkv, TN, BF16, 512, 512)
    g_mem_norm = _rowwise_vjp("norm_mem_bwd", rms1, [ms], [mem_norm], [dmem_n], [], ms.shape[0])[0]
    dp = jnp.concatenate([dfq, dfk, dfv, dgq, dgk, dgv, dgz, dmq, (dsm_fox + dsm_gdn).astype(BF16),
                          jnp.zeros((t, HD), BF16)], axis=1)
    du = _matmul("proj_in_bwd_x", dp, w_in_all, NT, F32, 512, 512)
    g_w_in = _matmul("grad_w_in", u, dp, TN, BF16, 512, 768)
    grad_x, g_norm_mix = _rowwise_vjp("norm_mix_bwd", rms2, [xs], [norm_mix], [du, dh1], [F32], min(t, 256))

    big_grads = {"w_in": g_w_in, "w_mem_kv": g_w_kv, "w_out": g_w_out}
    small_grads = {
        "norm_mix": g_norm_mix, "mem_norm": g_mem_norm, "norm_ffn": g_norm_ffn,
        "gdn_conv": jnp.concatenate([dwq, dwk, dwv], axis=1),
        "fox_q_norm": g_fqn, "fox_k_norm": g_fkn, "gdn_out_norm": g_gon, "mem_q_norm": g_mqn, "mem_k_norm": g_mkn,
        "fox_f_bias": g_fb, "gdn_a_log": g_alog, "gdn_dt_bias": g_dtb}
    return grad_x, loss_local, big_grads, small_grads
```

```python
import functools
import math

import jax
import jax.numpy as jnp
from jax import lax
from jax.experimental import pallas as pl
from jax.experimental.pallas import tpu as pltpu

F32 = jnp.float32
BF16 = jnp.bfloat16
HI = lax.Precision.HIGHEST
SDS = jax.ShapeDtypeStruct

N_DEV = 8
HD = 128
NF, NG, NM = 6, 6, 4
CHUNK = 64
GROUP = 4
NORM_EPS = 1e-6
FQ, FK, FV, GQ, GK, GV, GZ, MQ, SM, NPB = 0, 6, 12, 18, 24, 30, 36, 42, 46, 48
PC = NPB * HD
L_FF, L_GA, L_GB = 0, 6, 12
VMEM_LIMIT = 56 * 1024 * 1024

ADAM_LR, ADAM_B1, ADAM_B2, ADAM_EPS, ADAM_WD, ADAM_STEP = 0.001, 0.9, 0.999, 1e-08, 0.01, 10

NN = (((1,), (0,)), ((), ()))
NT = (((1,), (1,)), ((), ()))
TN = (((0,), (0,)), ((), ()))
MESH = pl.DeviceIdType.MESH


def _cp(*sem):
    return pltpu.CompilerParams(dimension_semantics=tuple(sem) if sem else None, vmem_limit_bytes=VMEM_LIMIT)


def _dot(a, b, dims=NN):
    return lax.dot_general(a, b, dims, preferred_element_type=F32)


def _bdot(a, b):
    return _dot(a.astype(BF16), b.astype(BF16))


def _iota(shape, axis):
    return lax.broadcasted_iota(jnp.int32, shape, axis)


def _rms(x, gain):
    return x * lax.rsqrt(jnp.mean(x * x, axis=-1, keepdims=True) + NORM_EPS) * gain


def _sigmoid(x):
    z = jnp.exp(-jnp.abs(x))
    return jnp.where(x >= 0, 1.0 / (1.0 + z), z / (1.0 + z))


def _silu(x):
    return x * _sigmoid(x)


def _softplus(x):
    return jnp.maximum(x, 0.0) + jnp.log(1.0 + jnp.exp(-jnp.abs(x)))


def _lane_pick(x, lane):
    oh = (_iota((1, x.shape[-1]), 1) == lane).astype(F32)
    return jnp.sum(x * oh, axis=-1, keepdims=True)


def _cumsum_rows(x):
    tril = (_iota((HD, HD), 0) >= _iota((HD, HD), 1)).astype(F32)
    carry = jnp.zeros((1, x.shape[1]), F32)
    outs = []
    for b in range(x.shape[0] // HD):
        blk = x[b * HD:(b + 1) * HD]
        outs.append(jnp.dot(tril, blk, precision=HI, preferred_element_type=F32) + carry)
        carry = carry + jnp.sum(blk, axis=0, keepdims=True)
    return jnp.concatenate(outs, axis=0)


def _row_spec(r, tm):
    if isinstance(r, tuple):
        arr, width, cb = r
        return arr, pl.BlockSpec((tm, width), lambda i, cb=cb: (i, cb))
    return r, pl.BlockSpec((tm, r.shape[1]), lambda i: (i, 0))


ANY_SPEC = pl.BlockSpec(memory_space=pl.ANY)


def _rowwise(name, fn, rows, consts, outs, tm, deps=()):
    arrs, specs = zip(*[_row_spec(r, tm) for r in rows])
    n_rows = arrs[0].shape[0]
    nr, nc, nd = len(rows), len(consts), len(deps)

    def body(*refs):
        res = fn(*[r[...] for r in refs[:nr + nc]])
        for o, v in zip(refs[nr + nc + nd:], res):
            o[...] = v.astype(o.dtype)

    return pl.pallas_call(
        body, grid=(n_rows // tm,), name=name,
        in_specs=list(specs) + [pl.BlockSpec(c.shape, lambda i: (0, 0)) for c in consts] + [ANY_SPEC] * nd,
        out_specs=[pl.BlockSpec((tm, w), lambda i: (i, 0)) for w, _ in outs],
        out_shape=[SDS((n_rows, w), dt) for w, dt in outs],
        compiler_params=_cp("parallel"),
    )(*arrs, *consts, *deps)


def _rowwise_vjp(name, fn, rows, consts, cts, grad_dtypes, tm, deps=()):
    arrs, specs = zip(*[_row_spec(r, tm) for r in rows])
    ct_arrs, ct_specs = zip(*[_row_spec(r, tm) for r in cts])
    n_rows = arrs[0].shape[0]
    nr, nc, nct, ng, nd = len(rows), len(consts), len(cts), len(grad_dtypes), len(deps)
    widths = [s.block_shape[1] for s in specs[:ng]]

    def body(*refs):
        vals = [r[...].astype(F32) for r in refs[:nr + nc]]
        ctv = tuple(r[...].astype(F32) for r in refs[nr + nc:nr + nc + nct])
        _, vjp = jax.vjp(fn, *vals)
        grads = vjp(ctv)
        outs = refs[nr + nc + nct + nd:]
        for o, g in zip(outs[:ng], grads[:ng]):
            o[...] = g.astype(o.dtype)

        @pl.when(pl.program_id(0) == 0)
        def _():
            for o in outs[ng:]:
                o[...] = jnp.zeros_like(o)

        for o, g in zip(outs[ng:], grads[nr:]):
            o[...] += g

    return pl.pallas_call(
        body, grid=(n_rows // tm,), name=name,
        in_specs=list(specs) + [pl.BlockSpec(c.shape, lambda i: (0, 0)) for c in consts] + list(ct_specs)
        + [ANY_SPEC] * nd,
        out_specs=[pl.BlockSpec((tm, w), lambda i: (i, 0)) for w in widths]
        + [pl.BlockSpec(c.shape, lambda i: (0, 0)) for c in consts],
        out_shape=[SDS((n_rows, w), dt) for w, dt in zip(widths, grad_dtypes)] + [SDS(c.shape, F32) for c in consts],
        compiler_params=_cp("arbitrary"),
    )(*arrs, *consts, *ct_arrs, *deps)


def _tile(n, pref):
    t = min(n, pref)
    while n % t or (t % HD and t != n):
        t -= 1
    return t


def _matmul(name, a, b, dims, out_dtype, tm, tn, residual=None, deps=()):
    ta, tb = dims == TN, dims == NT
    m = a.shape[1] if ta else a.shape[0]
    k = a.shape[0] if ta else a.shape[1]
    n = b.shape[0] if tb else b.shape[1]
    tm, tn = _tile(m, tm), _tile(n, tn)

    def body(*refs):
        acc = _dot(refs[0][...], refs[1][...], dims)
        if residual is not None:
            acc = acc + refs[2][...]
        refs[-1][...] = acc.astype(out_dtype)

    in_specs = [pl.BlockSpec((k, tm), lambda i, j: (0, i)) if ta else pl.BlockSpec((tm, k), lambda i, j: (i, 0)),
                pl.BlockSpec((tn, k), lambda i, j: (j, 0)) if tb else pl.BlockSpec((k, tn), lambda i, j: (0, j))]
    ops = [a, b]
    if residual is not None:
        in_specs.append(pl.BlockSpec((tm, tn), lambda i, j: (i, j)))
        ops.append(residual)
    in_specs += [ANY_SPEC] * len(deps)
    ops += list(deps)
    return pl.pallas_call(
        body, grid=(m // tm, n // tn), name=name, in_specs=in_specs,
        out_specs=pl.BlockSpec((tm, tn), lambda i, j: (i, j)), out_shape=SDS((m, n), out_dtype),
        compiler_params=_cp("parallel", "parallel"),
    )(*ops)


def _ffn_up(h1n, wgu):
    t, d = h1n.shape
    w = wgu.shape[3]
    tm = _tile(t, 512)

    def body(a, b, gu, act):
        x = a[...]
        g = _dot(x, b[0])
        u = _dot(x, b[1])
        gu[0] = g.astype(BF16)
        gu[1] = u.astype(BF16)
        act[...] = (_silu(g) * u).astype(BF16)

    return pl.pallas_call(
        body, grid=(4, t // tm), name="ffn_up",
        in_specs=[pl.BlockSpec((tm, d), lambda j, i: (i, 0)), pl.BlockSpec((2, None, d, w), lambda j, i: (0, j, 0, 0))],
        out_specs=[pl.BlockSpec((2, None, tm, w), lambda j, i: (0, j, i, 0)), pl.BlockSpec((tm, w), lambda j, i: (i, j))],
        out_shape=[SDS((2, 4, t, w), BF16), SDS((t, 4 * w), BF16)],
        compiler_params=_cp("parallel", "parallel"),
    )(h1n, wgu)


def _ffn_down_loss(act, wdown, h1, target):
    t, f = act.shape
    d = wdown.shape[1]
    tm, tn = _tile(t, 512), _tile(d, 512)

    def body(a, b, h, tg, dy, dyb, ls):
        e = _dot(a[...], b[...]) + h[...] - tg[...]
        g = e * (1.0 / d)
        dy[...] = g
        dyb[...] = g.astype(BF16)
        ls[...] = jnp.broadcast_to(jnp.sum(e * e), (8, HD))

    return pl.pallas_call(
        body, grid=(t // tm, d // tn), name="ffn_down_loss",
        in_specs=[pl.BlockSpec((tm, f), lambda i, j: (i, 0)), pl.BlockSpec((f, tn), lambda i, j: (0, j)),
                  pl.BlockSpec((tm, tn), lambda i, j: (i, j)), pl.BlockSpec((tm, tn), lambda i, j: (i, j))],
        out_specs=[pl.BlockSpec((tm, tn), lambda i, j: (i, j)), pl.BlockSpec((tm, tn), lambda i, j: (i, j)),
                   pl.BlockSpec((8, HD), lambda i, j: (i, j))],
        out_shape=[SDS((t, d), F32), SDS((t, d), BF16), SDS((8 * (t // tm), HD * (d // tn)), F32)],
        compiler_params=_cp("parallel", "parallel"),
    )(act, wdown, h1, target)


def _ffn_down_bwd(dyb, wdown4, gu):
    t, d = dyb.shape
    w = wdown4.shape[1]
    tm = _tile(t, 512)

    def body(a, b, gu_ref, out):
        da = _dot(a[...], b[...], NT)
        g = gu_ref[0].astype(F32)
        u = gu_ref[1].astype(F32)
        s = _sigmoid(g)
        out[0] = (da * u * (s * (1.0 + g * (1.0 - s)))).astype(BF16)
        out[1] = (da * g * s).astype(BF16)

    return pl.pallas_call(
        body, grid=(4, t // tm), name="ffn_down_bwd",
        in_specs=[pl.BlockSpec((tm, d), lambda j, i: (i, 0)), pl.BlockSpec((None, w, d), lambda j, i: (j, 0, 0)),
                  pl.BlockSpec((2, None, tm, w), lambda j, i: (0, j, i, 0))],
        out_specs=pl.BlockSpec((2, None, tm, w), lambda j, i: (0, j, i, 0)),
        out_shape=SDS((2, 4, t, w), BF16),
        compiler_params=_cp("parallel", "parallel"),
    )(dyb, wdown4, gu)


def _ffn_up_bwd_x(dgu, wgu):
    _, t, w = dgu.shape
    d = wgu.shape[1]
    tm = _tile(t, 512)

    def body(a, b, out):
        @pl.when(pl.program_id(1) == 0)
        def _():
            out[...] = jnp.zeros_like(out)
        out[...] += _dot(a[...], b[...], NT)

    return pl.pallas_call(
        body, grid=(t // tm, 8), name="ffn_up_bwd_x",
        in_specs=[pl.BlockSpec((None, tm, w), lambda i, j: (j, i, 0)), pl.BlockSpec((None, d, w), lambda i, j: (j, 0, 0))],
        out_specs=pl.BlockSpec((tm, d), lambda i, j: (i, 0)), out_shape=SDS((t, d), F32),
        compiler_params=_cp("parallel", "arbitrary"),
    )(dgu, wgu)


def _ffn_up_bwd_w(h1n, dgu):
    _, t, w = dgu.shape
    d = h1n.shape[1]
    tm = _tile(d, 512)

    def body(a, b, out):
        out[...] = _dot(a[...], b[...], TN).astype(BF16)

    return pl.pallas_call(
        body, grid=(8, d // tm), name="ffn_up_bwd_w",
        in_specs=[pl.BlockSpec((t, tm), lambda j, i: (0, i)), pl.BlockSpec((None, t, w), lambda j, i: (j, 0, 0))],
        out_specs=pl.BlockSpec((None, tm, w), lambda j, i: (j, i, 0)), out_shape=SDS((8, d, w), BF16),
        compiler_params=_cp("parallel", "parallel"),
    )(h1n, dgu)


def _fox_prep(fq, fk, sm, fb, qg, kg, h):
    qn = _rms(fq, qg)
    kn = _rms(fk, kg)
    c = _cumsum_rows(-_softplus(-(sm + fb)))
    ccol = _lane_pick(c, L_FF + h)
    crow = jnp.sum(c.T * (_iota((HD, 1), 0) == L_FF + h).astype(F32), axis=0, keepdims=True)
    return qn, kn, ccol, crow


def _fox_block(q, k, v, cc, cr, off):
    s = _dot(q.astype(BF16), k.astype(BF16), NT) * (HD ** -0.5) + cc - cr
    s = jnp.where(_iota(s.shape, 1) <= _iota(s.shape, 0) + off, s, -1e30)
    e = jnp.exp(s - lax.stop_gradient(jnp.max(s, axis=1, keepdims=True)))
    p = e / jnp.sum(e, axis=1, keepdims=True)
    return _dot(p.astype(BF16), v.astype(BF16))


ONE_BUFFER = pl.Buffered(1)


def _pcol(t, cb):
    return pl.BlockSpec((t, HD), lambda h, cb=cb: (0, cb + h), pipeline_mode=ONE_BUFFER)


def _smcol(t):
    return pl.BlockSpec((t, HD), lambda h: (0, SM), pipeline_mode=ONE_BUFFER)


def _head(t):
    return pl.BlockSpec((t, HD), lambda h: (0, h), pipeline_mode=ONE_BUFFER)


def _small(n):
    return pl.BlockSpec((n, HD), lambda h: (0, 0), pipeline_mode=ONE_BUFFER)


def _fox_fwd(p, fb, qg, kg, bq):
    t = p.shape[0]

    def body(fq, fk, fv, sm, fb_r, qg_r, kg_r, o, qn_s, cc_s):
        h = pl.program_id(0)
        qn, kn, ccol, crow = _fox_prep(fq[...], fk[...], sm[...], fb_r[...], qg_r[...], kg_r[...], h)
        qn_s[...] = qn
        cc_s[...] = ccol
        knb = kn.astype(BF16)
        vb = fv[...].astype(BF16)

        def step(i, carry):
            off = pl.multiple_of(i * bq, bq)
            rows = pl.ds(off, bq)
            o[rows, :] = _fox_block(qn_s[rows, :], knb, vb, cc_s[rows, :], crow, off).astype(o.dtype)
            return carry

        lax.fori_loop(0, t // bq, step, 0)

    return pl.pallas_call(
        body, grid=(NF,), name="fox_fwd",
        in_specs=[_pcol(t, FQ), _pcol(t, FK), _pcol(t, FV), _smcol(t), _small(1), _small(1), _small(1)],
        out_specs=_head(t), out_shape=SDS((t, NF * HD), BF16),
        scratch_shapes=[pltpu.VMEM((t, HD), F32), pltpu.VMEM((t, 1), F32)],
        compiler_params=_cp("parallel"),
    )(p, p, p, p, fb, qg, kg)


def _fox_bwd(p, fb, qg, kg, dmix, bq, deps=()):
    t = p.shape[0]

    def body(*refs):
        fq, fk, fv, sm, fb_r, qg_r, kg_r, do = refs[:8]
        dfq, dfk, dfv, dsm, dfb, dqg, dkg, qn_s, cc_s, dqn_s, dcc_s, dkn_s, dv_s, dcr_s = refs[8 + len(deps):]
        h = pl.program_id(0)
        qn, kn, ccol, crow = _fox_prep(fq[...], fk[...], sm[...], fb_r[...], qg_r[...], kg_r[...], h)
        qn_s[...] = qn
        cc_s[...] = ccol
        v = fv[...]
        dkn_s[...] = jnp.zeros_like(dkn_s)
        dv_s[...] = jnp.zeros_like(dv_s)
        dcr_s[...] = jnp.zeros_like(dcr_s)

        def step(i, carry):
            off = pl.multiple_of(i * bq, bq)
            rows = pl.ds(off, bq)
            _, vjp = jax.vjp(lambda a, b, c, d, e: _fox_block(a, b, c, d, e, off),
                             qn_s[rows, :], kn, v, cc_s[rows, :], crow)
            dq, dk, dv, dcc, dcr = vjp(do[rows, :])
            dqn_s[rows, :] = dq
            dcc_s[rows, :] = dcc
            dkn_s[...] += dk
            dv_s[...] += dv
            dcr_s[...] += dcr
            return carry

        lax.fori_loop(0, t // bq, step, 0)
        _, prep_vjp = jax.vjp(lambda a, b, c, d, e, f: _fox_prep(a, b, c, d, e, f, h),
                              fq[...], fk[...], sm[...], fb_r[...], qg_r[...], kg_r[...])
        g_fq, g_fk, g_sm, g_fb, g_qg, g_kg = prep_vjp((dqn_s[...], dkn_s[...], dcc_s[...], dcr_s[...]))
        dfq[...] = g_fq.astype(dfq.dtype)
        dfk[...] = g_fk.astype(dfk.dtype)
        dfv[...] = dv_s[...].astype(dfv.dtype)

        @pl.when(h == 0)
        def _():
            for r in (dsm, dfb, dqg, dkg):
                r[...] = jnp.zeros_like(r)

        dsm[...] += g_sm
        dfb[...] += g_fb
        dqg[...] += g_qg
        dkg[...] += g_kg

    head = _head(t)
    return pl.pallas_call(
        body, grid=(NF,), name="fox_bwd",
        in_specs=[_pcol(t, FQ), _pcol(t, FK), _pcol(t, FV), _smcol(t), _small(1), _small(1), _small(1), head]
        + [ANY_SPEC] * len(deps),
        out_specs=[head, head, head, _small(t), _small(1), _small(1), _small(1)],
        out_shape=[SDS((t, NF * HD), BF16)] * 3 + [SDS((t, HD), F32)] + [SDS((1, HD), F32)] * 3,
        scratch_shapes=[pltpu.VMEM((t, HD), F32), pltpu.VMEM((t, 1), F32), pltpu.VMEM((t, HD), F32),
                        pltpu.VMEM((t, 1), F32), pltpu.VMEM((t, HD), F32), pltpu.VMEM((t, HD), F32),
                        pltpu.VMEM((1, t), F32)],
        compiler_params=_cp("arbitrary"),
    )(p, p, p, p, fb, qg, kg, dmix, *deps)


def _mem_attn(mq, mk, mv, qg, kg):
    s = _dot(_rms(mq, qg).astype(BF16), _rms(mk, kg).astype(BF16), NT) * (HD ** -0.5)
    e = jnp.exp(s - lax.stop_gradient(jnp.max(s, axis=1, keepdims=True)))
    p = e / jnp.sum(e, axis=1, keepdims=True)
    return _dot(p.astype(BF16), mv.astype(BF16))


def _mem_fwd(p, mkv, qg, kg):
    t, ml = p.shape[0], mkv.shape[0]

    def body(mq, mk, mv, qg_r, kg_r, o):
        o[...] = _mem_attn(mq[...], mk[...], mv[...], qg_r[...], kg_r[...]).astype(o.dtype)

    return pl.pallas_call(
        body, grid=(NM,), name="mem_fwd",
        in_specs=[_pcol(t, MQ), pl.BlockSpec((ml, HD), lambda h: (0, h)), pl.BlockSpec((ml, HD), lambda h: (0, NM + h)),
                  _small(1), _small(1)],
        out_specs=pl.BlockSpec((t, HD), lambda h: (0, h)), out_shape=SDS((t, NM * HD), BF16),
        compiler_params=_cp("parallel"),
    )(p, mkv, mkv, qg, kg)


def _mem_bwd(p, mkv, qg, kg, dmix, deps=()):
    t, ml = p.shape[0], mkv.shape[0]

    def body(*refs):
        mq, mk, mv, qg_r, kg_r, do = refs[:6]
        dmq, dmk, dmv, dqg, dkg = refs[6 + len(deps):]
        _, vjp = jax.vjp(_mem_attn, mq[...], mk[...], mv[...], qg_r[...], kg_r[...])
        g_q, g_k, g_v, g_qg, g_kg = vjp(do[...])
        dmq[...] = g_q.astype(dmq.dtype)
        dmk[...] = g_k
        dmv[...] = g_v

        @pl.when(pl.program_id(0) == 0)
        def _():
            dqg[...] = jnp.zeros_like(dqg)
            dkg[...] = jnp.zeros_like(dkg)

        dqg[...] += g_qg
        dkg[...] += g_kg

    return pl.pallas_call(
        body, grid=(NM,), name="mem_bwd",
        in_specs=[_pcol(t, MQ), pl.BlockSpec((ml, HD), lambda h: (0, h)), pl.BlockSpec((ml, HD), lambda h: (0, NM + h)),
                  _small(1), _small(1), pl.BlockSpec((t, HD), lambda h: (0, NF + NG + h))] + [ANY_SPEC] * len(deps),
        out_specs=[pl.BlockSpec((t, HD), lambda h: (0, h)), pl.BlockSpec((ml, HD), lambda h: (0, h)),
                   pl.BlockSpec((ml, HD), lambda h: (0, h)), _small(1), _small(1)],
        out_shape=[SDS((t, NM * HD), BF16), SDS((ml, NM * HD), F32), SDS((ml, NM * HD), F32),
                   SDS((1, HD), F32), SDS((1, HD), F32)],
        compiler_params=_cp("arbitrary"),
    )(p, mkv, mkv, qg, kg, dmix, *deps)


def _shift_down(x, s):
    if s == 0:
        return x
    return jnp.where(_iota(x.shape, 0) >= s, pltpu.roll(x, s, 0), 0.0)


def _shift_up(x, s):
    if s == 0:
        return x
    n = x.shape[0]
    return jnp.where(_iota(x.shape, 0) < n - s, pltpu.roll(x, n - s, 0), 0.0)


@jax.custom_vjp
def _conv4(x, w0, w1, w2, w3):
    return w0 * _shift_down(x, 3) + w1 * _shift_down(x, 2) + w2 * _shift_down(x, 1) + w3 * x


def _conv4_fwd(x, w0, w1, w2, w3):
    return _conv4(x, w0, w1, w2, w3), (x, w0, w1, w2, w3)


def _conv4_bwd(res, dy):
    x, w0, w1, w2, w3 = res
    dx = w0 * _shift_up(dy, 3) + w1 * _shift_up(dy, 2) + w2 * _shift_up(dy, 1) + w3 * dy
    dws = tuple(jnp.sum(dy * _shift_down(x, 3 - k), axis=0, keepdims=True) for k in range(4))
    return (dx,) + dws


_conv4.defvjp(_conv4_fwd, _conv4_bwd)


def _gdn_prep(gq, gk, gv, sm, taps, alog, dtb, h):
    q, k, v = [_silu(_conv4(x, *taps[4 * j:4 * j + 4])) for j, x in enumerate((gq, gk, gv))]
    q = q * lax.rsqrt(jnp.sum(q * q, axis=-1, keepdims=True) + NORM_EPS) * (HD ** -0.5)
    k = k * lax.rsqrt(jnp.sum(k * k, axis=-1, keepdims=True) + NORM_EPS)
    g = _lane_pick(-jnp.exp(alog) * _softplus(sm + dtb), L_GA + h)
    beta = _lane_pick(_sigmoid(sm), L_GB + h)
    return q, k, v, g, beta


def _split(x, n):
    parts, rest = [], x
    for i in range(n):
        parts.append(rest.astype(BF16))
        if i + 1 < n:
            rest = rest - parts[-1].astype(F32)
    return parts


def _raw_dot(a, b, form):
    lead = a.ndim - 2
    ca, cb = {"nn": (1, 0), "nt": (1, 1), "tn": (0, 0)}[form]
    batch = ((0,), (0,)) if lead else ((), ())
    return lax.dot_general(a, b, (((ca + lead,), (cb + lead,)), batch), preferred_element_type=F32)


def _pdot_impl(a, b, form, mode):
    if mode == "1":
        return _raw_dot(a.astype(BF16), b.astype(BF16), form)
    if mode == "3":
        (ah, al), (bh, bl) = _split(a, 2), _split(b, 2)
        return _raw_dot(ah, bh, form) + (_raw_dot(al, bh, form) + _raw_dot(ah, bl, form))
    if mode == "xa":
        return sum(_raw_dot(a.astype(BF16), t, form) for t in reversed(_split(b, 3)))
    return sum(_raw_dot(t, b.astype(BF16), form) for t in reversed(_split(a, 3)))


@functools.partial(jax.custom_vjp, nondiff_argnums=(2, 3))
def _pdot(a, b, form, mode):
    return _pdot_impl(a, b, form, mode)


def _pdot_fwd(a, b, form, mode):
    return _pdot_impl(a, b, form, mode), (a, b)


def _pdot_bwd(form, mode, res, ct):
    a, b = res
    da_args, db_args = {"nn": ((ct, b, "nt"), (a, ct, "tn")), "nt": ((ct, b, "nn"), (ct, a, "tn")),
                        "tn": ((b, ct, "nt"), (a, ct, "nn"))}[form]

    def side(args, exact):
        if mode in ("1", "3"):
            return mode
        return "xa" if args[0] is exact else "xb"

    if mode == "xa":
        return jnp.zeros_like(a), _pdot_impl(*db_args, side(db_args, a))
    if mode == "xb":
        return _pdot_impl(*da_args, side(da_args, b)), jnp.zeros_like(b)
    return _pdot_impl(*da_args, mode), _pdot_impl(*db_args, mode)


_pdot.defvjp(_pdot_fwd, _pdot_bwd)

GDN_QK, GDN_INV, GDN_SCAN = "1", "3", "1"


def _gdn_intra(q, k, v, g, beta):
    n = q.shape[0]
    r, c = _iota((CHUNK, CHUNK), 0), _iota((CHUNK, CHUNK), 1)
    tril, strict = r >= c, r > c
    trilf = jnp.broadcast_to(tril.astype(F32), (n, CHUNK, CHUNK))
    gcm = _pdot(trilf, jnp.broadcast_to(g, (n, CHUNK, CHUNK)), "nn", "xa")
    gcf = _pdot(trilf, jnp.broadcast_to(g, (n, CHUNK, HD)), "nn", "xa")
    lane0 = (_iota((1, 1, CHUNK), 2) == 0).astype(F32)
    gcr = _pdot(jnp.ones((n, CHUNK, CHUNK), F32), gcm * lane0, "nt", "xa")
    decay = jnp.where(tril, jnp.exp(jnp.where(tril, gcm - gcr, 0.0)), 0.0)
    egc = jnp.exp(gcf)
    kb = k * beta
    low = jnp.where(strict, _pdot(kb, k, "nt", GDN_QK) * decay, 0.0)
    inv = (r == c).astype(F32) - low
    pw = low
    for _ in range(5):
        pw = _pdot(pw, pw, "nn", GDN_INV)
        inv = inv + _pdot(inv, pw, "nn", GDN_INV)
    u = _pdot(inv, v * beta, "nn", GDN_INV)
    w = _pdot(inv, kb * egc, "nn", GDN_INV)
    at = jnp.where(tril, _pdot(q, k, "nt", GDN_QK) * decay, 0.0)
    gl = jnp.sum(jnp.broadcast_to(g, (n, CHUNK, HD)), axis=1, keepdims=True)
    return u, w, q * egc, at, k * jnp.exp(gl - gcf), gl


def _gdn_step(s, u, w, qg, at, kd, gl):
    vn = u - _pdot(w, s, "nn", GDN_SCAN)
    o = _pdot(qg, s, "nn", GDN_SCAN) + _pdot(at, vn, "nn", GDN_SCAN)
    s2 = s * jnp.exp(gl) + _pdot(kd, vn, "tn", GDN_SCAN)
    return o, s2


def _gdn_scratch(nc):
    big = pltpu.VMEM((nc, CHUNK, HD), F32)
    return [big, big, big, pltpu.VMEM((nc, CHUNK, 1), F32), pltpu.VMEM((nc, CHUNK, 1), F32),
            big, big, big, pltpu.VMEM((nc, CHUNK, CHUNK), F32), big, pltpu.VMEM((nc, 1, HD), F32)]


def _gdn_in_specs(t):
    cw = lambda cb: pl.BlockSpec((4, HD), lambda h, cb=cb: (0, cb + h))
    return [_pcol(t, GQ), _pcol(t, GK), _pcol(t, GV), _smcol(t), cw(0), cw(NG), cw(2 * NG), _small(1), _small(1)]


def _taps(wq, wk, wv):
    return tuple(w[k:k + 1, :] for w in (wq, wk, wv) for k in range(4))


def _gdn_stage(vals, refs):
    nc = refs[0].shape[0]
    for v, r in zip(vals, refs):
        r[...] = v.reshape(nc, CHUNK, v.shape[-1])


def _gdn_intra_all(chunked, intra):
    nc = chunked[0].shape[0]
    grp_n = math.gcd(nc, GROUP)

    def grp(i, carry):
        sl = pl.ds(pl.multiple_of(i * grp_n, grp_n), grp_n)
        for r, val in zip(intra, _gdn_intra(*[c[sl] for c in chunked])):
            r[sl] = val
        return carry

    lax.fori_loop(0, nc // grp_n, grp, 0)


def _gdn_fwd(p, conv, alog, dtb):
    t = p.shape[0]
    nc = t // CHUNK

    def body(gq, gk, gv, sm, wq, wk, wv, al, db, o, *scr):
        h = pl.program_id(0)
        chunked, intra = scr[:5], scr[5:]
        _gdn_stage(_gdn_prep(gq[...], gk[...], gv[...], sm[...], _taps(wq, wk, wv), al[...], db[...], h), chunked)
        _gdn_intra_all(chunked, intra)

        def step(c, s):
            oc, s2 = _gdn_step(s, *[r[c] for r in intra])
            o[pl.ds(pl.multiple_of(c * CHUNK, CHUNK), CHUNK), :] = oc
            return s2

        lax.fori_loop(0, nc, step, jnp.zeros((HD, HD), F32))

    return pl.pallas_call(
        body, grid=(NG,), name="gdn_fwd", in_specs=_gdn_in_specs(t),
        out_specs=_head(t), out_shape=SDS((t, NG * HD), F32),
        scratch_shapes=_gdn_scratch(nc), compiler_params=_cp("parallel"),
    )(p, p, p, p, conv, conv, conv, alog, dtb)


def _gdn_bwd(p, conv, alog, dtb, do_raw):
    t = p.shape[0]
    nc = t // CHUNK

    def body(gq, gk, gv, sm, wq, wk, wv, al, db, do, dgq, dgk, dgv, dsm, dwq, dwk, dwv, dal, ddb, *scr):
        h = pl.program_id(0)
        chunked, intra, states = scr[:5], scr[5:11], scr[11]
        _gdn_stage(_gdn_prep(gq[...], gk[...], gv[...], sm[...], _taps(wq, wk, wv), al[...], db[...], h), chunked)
        _gdn_intra_all(chunked, intra)

        def fwd(c, s):
            states[c] = s
            return _gdn_step(s, *[r[c] for r in intra])[1]

        lax.fori_loop(0, nc, fwd, jnp.zeros((HD, HD), F32))

        def bwd(i, ds):
            c = nc - 1 - i
            _, vjp = jax.vjp(_gdn_step, states[c], *[r[c] for r in intra])
            grads = vjp((do[pl.ds(pl.multiple_of(c * CHUNK, CHUNK), CHUNK), :], ds))
            for r, gval in zip(intra, grads[1:]):
                r[c] = gval
            return grads[0]

        lax.fori_loop(0, nc, bwd, jnp.zeros((HD, HD), F32))

        grp_n = math.gcd(nc, GROUP)

        def grp(i, carry):
            sl = pl.ds(pl.multiple_of(i * grp_n, grp_n), grp_n)
            _, vjp = jax.vjp(_gdn_intra, *[r[sl] for r in chunked])
            for r, gval in zip(chunked, vjp(tuple(r[sl] for r in intra))):
                r[sl] = gval
            return carry

        lax.fori_loop(0, nc // grp_n, grp, 0)
        _, prep_vjp = jax.vjp(
            lambda *a: _gdn_prep(*a, h), gq[...], gk[...], gv[...], sm[...], _taps(wq, wk, wv), al[...], db[...])
        grads = prep_vjp(tuple(r[...].reshape(t, r.shape[-1]) for r in chunked))
        for r, gval in zip((dgq, dgk, dgv), grads[:3]):
            r[...] = gval.astype(r.dtype)
        for j, r in enumerate((dwq, dwk, dwv)):
            for k in range(4):
                r[k:k + 1, :] = grads[4][4 * j + k]

        @pl.when(h == 0)
        def _():
            for r in (dsm, dal, ddb):
                r[...] = jnp.zeros_like(r)

        dsm[...] += grads[3]
        dal[...] += grads[5]
        ddb[...] += grads[6]

    head = _head(t)
    taps = pl.BlockSpec((4, HD), lambda h: (0, h))
    return pl.pallas_call(
        body, grid=(NG,), name="gdn_bwd", in_specs=_gdn_in_specs(t) + [head],
        out_specs=[head, head, head, _small(t), taps, taps, taps, _small(1), _small(1)],
        out_shape=[SDS((t, NG * HD), BF16)] * 3 + [SDS((t, HD), F32)] + [SDS((4, NG * HD), F32)] * 3 + [SDS((1, HD), F32)] * 2,
        scratch_shapes=_gdn_scratch(nc) + [pltpu.VMEM((nc, HD, HD), F32)], compiler_params=_cp("arbitrary"),
    )(p, p, p, p, conv, conv, conv, alog, dtb, do_raw)


def _gdn_post(o, z, gain):
    return (jnp.concatenate(
        [_rms(o[:, h * HD:(h + 1) * HD], gain) * _silu(z[:, h * HD:(h + 1) * HD]) for h in range(NG)], axis=1),)


def _place():
    return lax.axis_index("x"), lax.axis_index("y"), lax.axis_index("c")


def _all_gather(name, shard):
    def body(x_ref, out_ref, send_sems, recv_sems, local_sem):
        x, y, c = _place()
        me, sibling = (x, y, c), (x, y, 1 - c)
        chips = [(1 - x, y), (x, 1 - y), (1 - x, 1 - y)]

        def blk(px, py, pc):
            return out_ref.at[4 * px + 2 * py + pc]

        def copy(k, block, to, src=None):
            return pltpu.make_async_remote_copy(
                src_ref=blk(*block) if src is None else src, dst_ref=blk(*block),
                send_sem=send_sems.at[k], recv_sem=recv_sems.at[k], device_id=to, device_id_type=MESH)

        mine = pltpu.make_async_copy(x_ref, blk(*me), local_sem)
        mine.start()
        first = [copy(0, me, sibling, src=x_ref)]
        first += [copy(1 + j, me, (*chip, c), src=x_ref) for j, chip in enumerate(chips)]
        for cp in first:
            cp.start()
        passed = [copy(4 + j, (*chip, c), sibling) for j, chip in enumerate(chips)]
        for j, chip in enumerate(chips):
            copy(1 + j, (*chip, c), me).wait_recv()
            passed[j].start()
        copy(0, sibling, me).wait_recv()
        for j, chip in enumerate(chips):
            copy(4 + j, (*chip, 1 - c), me).wait_recv()
        for cp in first + passed:
            cp.wait_send()
        mine.wait()

    return pl.pallas_call(
        body, name=name, out_shape=SDS((N_DEV,) + shard.shape, shard.dtype),
        in_specs=[pl.BlockSpec(memory_space=pltpu.HBM)], out_specs=pl.BlockSpec(memory_space=pltpu.HBM),
        scratch_shapes=[pltpu.SemaphoreType.DMA((7,)), pltpu.SemaphoreType.DMA((7,)), pltpu.SemaphoreType.DMA],
    )(shard)


def _scatter_exchange(name, full):
    def body(g_ref, out_ref, send_sems, recv_sems, local_sem):
        x, y, c = _place()
        me = 4 * x + 2 * y + c
        mine = pltpu.make_async_copy(g_ref.at[me], out_ref.at[me], local_sem)
        mine.start()
        sends, recvs = [], []
        for k in range(1, N_DEV):
            px = 1 - x if k & 4 else x
            py = 1 - y if k & 2 else y
            pc = 1 - c if k & 1 else c
            peer = 4 * px + 2 * py + pc
            sends.append(pltpu.make_async_remote_copy(
                src_ref=g_ref.at[peer], dst_ref=out_ref.at[me], send_sem=send_sems.at[k - 1],
                recv_sem=recv_sems.at[k - 1], device_id=(px, py, pc), device_id_type=MESH))
            recvs.append(pltpu.make_async_remote_copy(
                src_ref=g_ref.at[me], dst_ref=out_ref.at[peer], send_sem=send_sems.at[k - 1],
                recv_sem=recv_sems.at[k - 1], device_id=(px, py, pc), device_id_type=MESH))
        for cp in sends:
            cp.start()
        for cp in recvs:
            cp.wait_recv()
        for cp in sends:
            cp.wait_send()
        mine.wait()

    return pl.pallas_call(
        body, name=name, out_shape=SDS(full.shape, full.dtype),
        in_specs=[pl.BlockSpec(memory_space=pltpu.HBM)], out_specs=pl.BlockSpec(memory_space=pltpu.HBM),
        scratch_shapes=[pltpu.SemaphoreType.DMA((7,)), pltpu.SemaphoreType.DMA((7,)), pltpu.SemaphoreType.DMA],
    )(full)


def _sum_blocks(name, parts):
    _, r, c = parts.shape
    tr = 64 if r % 64 == 0 else r

    def body(x, o):
        acc = x[0].astype(F32)
        for d in range(1, N_DEV):
            acc = acc + x[d].astype(F32)
        o[...] = acc

    return pl.pallas_call(
        body, grid=(r // tr,), name=name, in_specs=[pl.BlockSpec((N_DEV, tr, c), lambda i: (0, i, 0))],
        out_specs=pl.BlockSpec((tr, c), lambda i: (i, 0)), out_shape=SDS((r, c), F32), compiler_params=_cp("parallel"),
    )(parts)


def _reduce_scatter(name, full):
    return _sum_blocks(name + "_sum", _scatter_exchange(name, full))


def _all_reduce_small(name, x, reduce):
    m_per, n = x.shape

    def body(x_ref, out_ref, send_sems, recv_sems, local_sem):
        px, py, pc = _place()
        me, sibling = (px, py, pc), (px, py, 1 - pc)
        chips = [(1 - px, py), (px, 1 - py), (1 - px, 1 - py)]
        buf = out_ref

        def rows(qx, qy, qc):
            return buf.at[pl.ds((4 * qx + 2 * qy + qc) * m_per, m_per), :]

        def copy(k, block, to, src=None):
            return pltpu.make_async_remote_copy(
                src_ref=rows(*block) if src is None else src, dst_ref=rows(*block),
                send_sem=send_sems.at[k], recv_sem=recv_sems.at[k], device_id=to, device_id_type=MESH)

        mine = pltpu.make_async_copy(x_ref, rows(*me), local_sem)
        mine.start()
        first = [copy(0, me, sibling, src=x_ref)]
        first += [copy(1 + j, me, (*chip, pc), src=x_ref) for j, chip in enumerate(chips)]
        for cp in first:
            cp.start()
        passed = [copy(4 + j, (*chip, pc), sibling) for j, chip in enumerate(chips)]
        for j, chip in enumerate(chips):
            copy(1 + j, (*chip, pc), me).wait_recv()
            passed[j].start()
        copy(0, sibling, me).wait_recv()
        for j, chip in enumerate(chips):
            copy(4 + j, (*chip, 1 - pc), me).wait_recv()
        for cp in first + passed:
            cp.wait_send()
        mine.wait()

    gathered = pl.pallas_call(
        body, name=name, out_shape=SDS((N_DEV * m_per, n), x.dtype),
        in_specs=[pl.BlockSpec(memory_space=pltpu.VMEM)], out_specs=pl.BlockSpec(memory_space=pltpu.VMEM),
        scratch_shapes=[pltpu.SemaphoreType.DMA((7,)), pltpu.SemaphoreType.DMA((7,)), pltpu.SemaphoreType.DMA],
    )(x)
    if not reduce:
        return gathered
    return _sum_blocks(name + "_sum", gathered.reshape(N_DEV, m_per, n))


HBM_SPEC = pl.BlockSpec(memory_space=pltpu.HBM)
SEM_SPEC = pl.BlockSpec(memory_space=pltpu.SEMAPHORE)
EFFECT = pltpu.SideEffectType.DATAFLOW_SIDE_EFFECTING


def _copies_start(name, bufs, n_remote, n_local, build, deps):
    nb, nd = len(bufs), len(deps)
    sem_shapes = [pltpu.SemaphoreType.DMA((n_remote,)), pltpu.SemaphoreType.DMA((n_remote,))]
    if n_local:
        sem_shapes.append(pltpu.SemaphoreType.DMA((n_local,)))
    ns = len(sem_shapes)

    def body(*refs):
        sems = refs[nb + nd:nb + nd + ns]
        remote, local = build(refs[:nb], *sems, *([None] * (3 - ns)))
        for cp in local + remote:
            cp.start()
        refs[-1][...] = jnp.zeros((8, HD), F32)

    outs = pl.pallas_call(
        body, name=name,
        out_shape=(*sem_shapes, *[pltpu.HBM(b.shape, b.dtype) for b in bufs], SDS((8, HD), F32)),
        in_specs=[HBM_SPEC] * nb + [ANY_SPEC] * nd,
        out_specs=(*[SEM_SPEC] * ns, *[HBM_SPEC] * nb, pl.BlockSpec(memory_space=pltpu.VMEM)),
        input_output_aliases={i: ns + i for i in range(nb)},
        compiler_params=pltpu.CompilerParams(has_side_effects=EFFECT),
    )(*[pltpu.with_memory_space_constraint(b, pltpu.HBM) for b in bufs], *deps)
    return list(outs[:ns]), list(outs[ns:ns + nb]), outs[-1]


def _copies_wait(name, bufs, sems, build, after):
    nb, ns = len(bufs), len(sems)

    def body(*refs):
        remote, local = build(refs[:nb], *refs[nb:nb + ns], *([None] * (3 - ns)))
        for cp in local:
            cp.wait()
        for cp in remote:
            cp.wait_send()
            cp.wait_recv()

    outs = pl.pallas_call(
        body, name=name, out_shape=tuple(pltpu.HBM(b.shape, b.dtype) for b in bufs),
        in_specs=[HBM_SPEC] * nb + [SEM_SPEC] * ns + [ANY_SPEC] * len(after), out_specs=tuple([HBM_SPEC] * nb),
        input_output_aliases={i: i for i in range(nb)},
        compiler_params=pltpu.CompilerParams(has_side_effects=EFFECT),
    )(*bufs, *sems, *after)
    return list(outs)


def _remote(src, dst, send, recv, k, to):
    return pltpu.make_async_remote_copy(src_ref=src, dst_ref=dst, send_sem=send.at[k], recv_sem=recv.at[k],
                                        device_id=to, device_id_type=MESH)


class _Gather:
    def __init__(self, name, shards, deps):
        self.name, self.n = name, len(shards)
        lands = [lax.empty((N_DEV,) + s.shape, s.dtype) for s in shards]
        self.sems, bufs, self.token = _copies_start(name + "_s1", list(shards) + lands, 4 * self.n, self.n, self._stage1, deps)
        self.shards, self.lands = bufs[:self.n], bufs[self.n:]

    def _stage1(self, refs, send, recv, loc):
        x, y, c = _place()
        me = 4 * x + 2 * y + c
        targets = [(x, y, 1 - c), (1 - x, y, c), (x, 1 - y, c), (1 - x, 1 - y, c)]
        remote, local = [], []
        for i in range(self.n):
            src, land = refs[i], refs[self.n + i]
            local.append(pltpu.make_async_copy(src, land.at[me], loc.at[i]))
            remote += [_remote(src, land.at[me], send, recv, 4 * i + k, to) for k, to in enumerate(targets)]
        return remote, local

    def _stage2(self, refs, send, recv, loc):
        x, y, c = _place()
        remote = []
        for i in range(self.n):
            for j, (cx, cy) in enumerate([(1 - x, y), (x, 1 - y), (1 - x, 1 - y)]):
                blk = refs[i].at[4 * cx + 2 * cy + c]
                remote.append(_remote(blk, blk, send, recv, 3 * i + j, (x, y, 1 - c)))
        return remote, []

    def mid(self, after):
        bufs = _copies_wait(self.name + "_w1", self.shards + self.lands, self.sems, self._stage1, after)
        self.sems, self.lands, self.token = _copies_start(self.name + "_s2", bufs[self.n:], 3 * self.n, 0, self._stage2, ())

    def end(self, after):
        return _copies_wait(self.name + "_w2", self.lands, self.sems, self._stage2, after)


def _rows_tile(r, row_bytes, target=1 << 20):
    tr = r
    while tr % 32 == 0 and tr * row_bytes > target:
        tr //= 2
    return tr


def _pair_add(name, g, got, c):
    _, r, cols = g.shape
    tr = _rows_tile(r, cols * 2)

    def body(s, a, b, o):
        o[...] = (a[...].astype(F32) + b[...].astype(F32)).astype(o.dtype)

    return pl.pallas_call(
        body, name=name, out_shape=SDS((4, r, cols), g.dtype),
        grid_spec=pltpu.PrefetchScalarGridSpec(
            num_scalar_prefetch=1, grid=(4, r // tr),
            in_specs=[pl.BlockSpec((None, tr, cols), lambda j, i, s: (2 * j + s[0], i, 0)),
                      pl.BlockSpec((None, tr, cols), lambda j, i, s: (j, i, 0))],
            out_specs=pl.BlockSpec((None, tr, cols), lambda j, i, s: (j, i, 0))),
        compiler_params=_cp("parallel", "parallel"),
    )(c.reshape(1), g, got)


def _quad_sum(name, part, got, chip):
    _, r, cols = part.shape
    tr = _rows_tile(r, cols * 4)

    def body(s, a, b1, b2, b3, o):
        o[...] = ((a[...].astype(F32) + b1[...].astype(F32)) + b2[...].astype(F32)) + b3[...].astype(F32)

    blk = lambda k: pl.BlockSpec((None, tr, cols), lambda i, s, k=k: (jnp.bitwise_xor(s[0], k), i, 0))
    return pl.pallas_call(
        body, name=name, out_shape=SDS((r, cols), F32),
        grid_spec=pltpu.PrefetchScalarGridSpec(
            num_scalar_prefetch=1, grid=(r // tr,), in_specs=[blk(0), blk(1), blk(2), blk(3)],
            out_specs=pl.BlockSpec((tr, cols), lambda i, s: (i, 0))),
        compiler_params=_cp("parallel"),
    )(chip.reshape(1), part, got, got, got)


class _Scatter:
    def __init__(self, name, grads, deps):
        self.name, self.n = name, len(grads)
        got = [lax.empty((4,) + g.shape[1:], g.dtype) for g in grads]
        self.sems, bufs, self.token = _copies_start(name + "_s1", list(grads) + got, 4 * self.n, 0, self._stage1, deps)
        self.grads, self.got = bufs[:self.n], bufs[self.n:]

    def _stage1(self, refs, send, recv, loc):
        x, y, c = _place()
        remote = []
        for i in range(self.n):
            remote += [_remote(refs[i].at[2 * j + 1 - c], refs[self.n + i].at[j], send, recv, 4 * i + j, (x, y, 1 - c))
                       for j in range(4)]
        return remote, []

    def _stage2(self, refs, send, recv, loc):
        x, y, c = _place()
        remote = []
        for i in range(self.n):
            for k in (1, 2, 3):
                tx = 1 - x if k & 2 else x
                ty = 1 - y if k & 1 else y
                remote.append(_remote(refs[i].at[2 * tx + ty], refs[self.n + i].at[2 * x + y], send, recv,
                                      3 * i + k - 1, (tx, ty, c)))
        return remote, []

    def mid(self, after):
        bufs = _copies_wait(self.name + "_w1", self.grads + self.got, self.sems, self._stage1, after)
        c = lax.axis_index("c").astype(jnp.int32)
        parts = [_pair_add(f"{self.name}_add{i}", bufs[i], bufs[self.n + i], c) for i in range(self.n)]
        got = [lax.empty(p.shape, p.dtype) for p in parts]
        self.sems, bufs, self.token = _copies_start(self.name + "_s2", parts + got, 3 * self.n, 0, self._stage2, ())
        self.parts, self.got = bufs[:self.n], bufs[self.n:]

    def end(self, after):
        bufs = _copies_wait(self.name + "_w2", self.parts + self.got, self.sems, self._stage2, after)
        chip = (2 * lax.axis_index("x") + lax.axis_index("y")).astype(jnp.int32)
        return [_quad_sum(f"{self.name}_sum{i}", bufs[i], bufs[self.n + i], chip) for i in range(self.n)]


def _adamw(w, g, m, v):
    m = ADAM_B1 * m + (1.0 - ADAM_B1) * g
    v = ADAM_B2 * v + (1.0 - ADAM_B2) * (g * g)
    m_hat = m / (1.0 - ADAM_B1 ** ADAM_STEP)
    v_hat = v / (1.0 - ADAM_B2 ** ADAM_STEP)
    return -ADAM_LR * (m_hat / (jnp.sqrt(v_hat) + ADAM_EPS) + ADAM_WD * w), m, v


def _adamw_call(name, w, g, m, v):
    r, c = w.shape
    tm = 64 if r % 64 == 0 else r
    return _rowwise(name, _adamw, [w, g, m, v], [], [(c, F32)] * 3, tm)


_IN_COLS = 5906


def _perm_in(w):
    pad = jnp.zeros((w.shape[0], PC - _IN_COLS), w.dtype)
    return jnp.concatenate([w[:, :2304], w[:, 2310:4614], w[:, 4614:5382], w[:, 5394:5906], w[:, 2304:2310],
                            w[:, 5382:5394], pad], axis=1)


def _unperm_in(g):
    return jnp.concatenate([g[:, :2304], g[:, 5888:5894], g[:, 2304:4608], g[:, 4608:5376], g[:, 5894:5906],
                            g[:, 5376:5888]], axis=1)


def _lanes(v, at):
    return jnp.pad(v, ((0, 0), (at, HD - at - v.shape[1])))


_PACK = ("norm_mix", "mem_norm", "norm_ffn", "gdn_conv", "fox_q_norm", "fox_k_norm", "gdn_out_norm", "mem_q_norm",
         "mem_k_norm", "fox_f_bias", "gdn_a_log", "gdn_dt_bias", "loss")


def _pack(vals):
    parts = [vals[n].reshape(-1, HD) for n in _PACK]
    used = sum(p.shape[0] for p in parts)
    buf = jnp.concatenate(parts + [jnp.zeros((-used % 8, HD), F32)], axis=0)
    return buf, [(n, p.shape[0]) for n, p in zip(_PACK, parts)]


def _unpack(buf, layout):
    out, at = {}, 0
    for n, rows in layout:
        out[n] = buf[at:at + rows]
        at += rows
    return out


def kernel(x, mem, norm_mix, w_in, fox_f_bias, fox_q_norm, fox_k_norm, gdn_conv, gdn_a_log, gdn_dt_bias, gdn_out_norm, mem_norm, w_mem_kv, mem_q_norm, mem_k_norm, w_out, norm_ffn, w_gate_up, w_down, loss_target, m_norm_mix, m_w_in, m_fox_f_bias, m_fox_q_norm, m_fox_k_norm, m_gdn_conv, m_gdn_a_log, m_gdn_dt_bias, m_gdn_out_norm, m_mem_norm, m_w_mem_kv, m_mem_q_norm, m_mem_k_norm, m_w_out, m_norm_ffn, m_w_gate_up, m_w_down, v_norm_mix, v_w_in, v_fox_f_bias, v_fox_q_norm, v_fox_k_norm, v_gdn_conv, v_gdn_a_log, v_gdn_dt_bias, v_gdn_out_norm, v_mem_norm, v_w_mem_kv, v_mem_q_norm, v_mem_k_norm, v_w_out, v_norm_ffn, v_w_gate_up, v_w_down):
    args = dict(locals())
    d = x.shape[2]
    me = 4 * lax.axis_index("x") + 2 * lax.axis_index("y") + lax.axis_index("c")

    w_in_all = _all_gather("ag_w_in", _perm_in(w_in[0]).astype(BF16)).reshape(d, PC)
    w_kv_all = _all_gather("ag_w_kv", w_mem_kv[0].astype(BF16)).reshape(d, 2 * NM * HD)
    cshard = gdn_conv[0].shape[1]
    conv_pad = jnp.pad(gdn_conv[0], ((0, 4), (0, 3 * HD - cshard)))
    conv_all = _all_reduce_small("ag_conv", conv_pad, False).reshape(N_DEV, 8, 3 * HD)[:, :4, :cshard]
    conv_all = conv_all.transpose(1, 0, 2).reshape(4, N_DEV * cshard)
    comm = _StepComm([w_out[0].astype(BF16), w_gate_up[0].astype(BF16), w_down[0].astype(BF16)],
                     [w_in_all, w_kv_all, conv_all])

    grad_x, loss_local, big_grads, small_grads = _local_step(
        x[0], mem[0], loss_target[0], norm_mix, fox_f_bias, fox_q_norm, fox_k_norm, gdn_a_log, gdn_dt_bias,
        gdn_out_norm, mem_norm, mem_q_norm, mem_k_norm, norm_ffn, w_in_all, w_kv_all, conv_all, comm)

    blocks = lambda g: g.reshape(N_DEV, g.shape[0] // N_DEV, g.shape[1])
    last = _Scatter("rs_b", [blocks(big_grads[n]) for n in ("w_in", "w_mem_kv", "w_out")], ())
    last.mid(())
    grads = dict(zip(("w_down", "w_gate_up"), comm.ffn_grads_end([grad_x])))
    grads.update(zip(("w_in", "w_mem_kv", "w_out"), last.end(())))
    grads["w_in"] = _unperm_in(grads["w_in"])
    small_grads["loss"] = jnp.broadcast_to(loss_local, (1, HD))
    packed, layout = _pack(small_grads)
    small = _unpack(_all_reduce_small("ar_small", packed, True), layout)
    loss = small["loss"][0, 0]
    six = {"fox_f_bias": L_FF, "gdn_a_log": L_GA, "gdn_dt_bias": L_GA}
    for n, rows_n in layout[:-1]:
        gsm = small[n]
        if n == "gdn_conv":
            gsm = lax.dynamic_slice(gsm.reshape(4, N_DEV * cshard), (0, me * cshard), (4, cshard))[None]
        elif n in six:
            gsm = gsm[:, six[n]:six[n] + 6]
        else:
            gsm = gsm.reshape(1, rows_n * HD)
        grads[n] = gsm

    names = ['norm_mix', 'w_in', 'fox_f_bias', 'fox_q_norm', 'fox_k_norm', 'gdn_conv', 'gdn_a_log', 'gdn_dt_bias',
             'gdn_out_norm', 'mem_norm', 'w_mem_kv', 'mem_q_norm', 'mem_k_norm', 'w_out', 'norm_ffn', 'w_gate_up', 'w_down']
    big = ("w_in", "w_mem_kv", "w_out", "w_gate_up", "w_down")
    delta, new_m, new_v = {}, {}, {}
    for n in big:
        delta[n], new_m[n], new_v[n] = [a[None] for a in _adamw_call(
            "adamw_" + n, args[n][0], grads[n], args["m_" + n][0], args["v_" + n][0])]
        grads[n] = grads[n][None]

    def flat(a):
        a = a.reshape(1, -1)
        return jnp.pad(a, ((0, 0), (0, -a.shape[1] % HD))).reshape(-1, HD)

    smalls = [n for n in names if n not in big]
    pk = lambda pre: jnp.concatenate([flat(grads[n] if pre == "g" else args[pre + n]) for n in smalls], axis=0)
    cat = [pk(""), pk("g"), pk("m_"), pk("v_")]
    padr = -cat[0].shape[0] % 8
    cat = [jnp.pad(a, ((0, padr), (0, 0))) for a in cat]
    res = _adamw_call("adamw_small", *cat)
    at = 0
    for n in smalls:
        shape = args[n].shape
        size = math.prod(shape)
        nrow = -(-size // HD)
        for dst, src in zip((delta, new_m, new_v), res):
            dst[n] = src[at:at + nrow].reshape(-1)[:size].reshape(shape)
        at += nrow

    return (loss, grad_x[None], *[grads[n] for n in names], *[delta[n] for n in names],
            *[new_m[n] for n in names], *[new_v[n] for n in names])


class _StepComm:
    def __init__(self, late_shards, after):
        self.gather = _Gather("ag_late", late_shards, after)

    def start_deps(self):
        return [self.gather.token]

    def after_mixers(self, after):
        self.gather.mid(after)
        return [self.gather.token]

    def late_weights(self, after):
        w_out_all, wgu, w_down_all = self.gather.end(after)
        return w_out_all.reshape(-1, w_out_all.shape[-1]), wgu, w_down_all.reshape(-1, w_down_all.shape[-1])

    def ffn_grads(self, g_w_down, g_w_gu):
        self.scatter = _Scatter("rs_ffn", [g_w_down.reshape(N_DEV, -1, g_w_down.shape[1]), g_w_gu], ())
        return [self.scatter.token]

    def ffn_grads_mid(self, after):
        self.scatter.mid(after)
        return [self.scatter.token]

    def ffn_grads_end(self, after):
        return self.scatter.end(after)


def _local_step(xs, ms, tgt, norm_mix, fox_f_bias, fox_q_norm, fox_k_norm, gdn_a_log, gdn_dt_bias, gdn_out_norm,
                mem_norm, mem_q_norm, mem_k_norm, norm_ffn, w_in_all, w_kv_all, conv_all, comm):
    t, d = xs.shape
    bq = min(t, 256)
    fb, alog, dtb = _lanes(fox_f_bias, L_FF), _lanes(gdn_a_log, L_GA), _lanes(gdn_dt_bias, L_GA)

    rms1 = lambda a, g: (_rms(a, g),)
    (u,) = _rowwise("norm_mix", rms1, [xs], [norm_mix], [(d, BF16)], min(t, 256), deps=comm.start_deps())
    p = _matmul("proj_in", u, w_in_all, NN, F32, 1024, 768)
    o_fox = _fox_fwd(p, fb, fox_q_norm, fox_k_norm, bq)
    o_gdn_raw = _gdn_fwd(p, conv_all, alog, dtb)
    zrow = (p, NG * HD, GZ * HD // (NG * HD))
    (o_gdn,) = _rowwise("gdn_post", _gdn_post, [o_gdn_raw, zrow], [gdn_out_norm], [(NG * HD, BF16)], min(t, 256))
    deps = comm.after_mixers([o_fox, o_gdn])
    (mem_n,) = _rowwise("norm_mem", rms1, [ms], [mem_norm], [(d, BF16)], ms.shape[0], deps=deps)
    mkv = _matmul("proj_mem", mem_n, w_kv_all, NN, F32, 256, 512)
    o_mem = _mem_fwd(p, mkv, mem_q_norm, mem_k_norm)
    w_out_all, wgu, w_down_all = comm.late_weights([o_mem])
    ffw = wgu.shape[2]
    mix = jnp.concatenate([o_fox, o_gdn, o_mem], axis=1)
    h1 = _matmul("proj_out", mix, w_out_all, NN, F32, 1024, 512, residual=xs)
    (h1n,) = _rowwise("norm_ffn", rms1, [h1], [norm_ffn], [(d, BF16)], min(t, 256))
    gu, act = _ffn_up(h1n, wgu.reshape(2, 4, d, ffw))
    dy, dyb, lsum = _ffn_down_loss(act, w_down_all, h1, tgt)
    loss_local = (0.5 / d) * jnp.sum(lsum[::8, ::HD])

    dgu = _ffn_down_bwd(dyb, w_down_all.reshape(4, ffw, d), gu).reshape(8, t, ffw)
    g_w_down = _matmul("grad_w_down", act, dyb, TN, BF16, 512, 512)
    dh1n = _ffn_up_bwd_x(dgu, wgu)
    g_w_gu = _ffn_up_bwd_w(h1n, dgu)
    deps = comm.ffn_grads(g_w_down, g_w_gu)
    rms2 = lambda a, g: (_rms(a, g), a)
    dh1, g_norm_ffn = _rowwise_vjp("norm_ffn_bwd", rms2, [h1], [norm_ffn], [dh1n, dy], [F32], min(t, 256), deps=deps)
    dh1b = dh1.astype(BF16)

    dmix = _matmul("proj_out_bwd_x", dh1b, w_out_all, NT, F32, 1024, 512)
    g_w_out = _matmul("grad_w_out", mix, dh1b, TN, BF16, 512, 512)
    deps = comm.ffn_grads_mid([dmix, g_w_out])
    dfq, dfk, dfv, dsm_fox, g_fb, g_fqn, g_fkn = _fox_bwd(p, fb, fox_q_norm, fox_k_norm, dmix, bq, deps=deps)
    do_raw, dgz, g_gon = _rowwise_vjp("gdn_post_bwd", _gdn_post, [o_gdn_raw, zrow], [gdn_out_norm],
                                      [(dmix, NG * HD, 1)], [F32, BF16], min(t, 256), deps=deps)
    dgq, dgk, dgv, dsm_gdn, dwq, dwk, dwv, g_alog, g_dtb = _gdn_bwd(p, conv_all, alog, dtb, do_raw)
    dmq, dmk, dmv, g_mqn, g_mkn = _mem_bwd(p, mkv, mem_q_norm, mem_k_norm, dmix, deps=deps)
    dmkv = jnp.concatenate([dmk, dmv], axis=1).astype(BF16)
    dmem_n = _matmul("proj_mem_bwd_x", dmkv, w_kv_all, NT, F32, 256, 512)
    g_w_kv = _matmul("grad_w_kv", mem_n, dmkv, TN, BF16, 512, 512)
    g_mem_norm = _rowwise_vjp("norm_mem_bwd", rms1, [ms], [mem_norm], [dmem_n], [], ms.shape[0])[0]
    dp = jnp.concatenate([dfq, dfk, dfv, dgq, dgk, dgv, dgz, dmq, (dsm_fox + dsm_gdn).astype(BF16),
                          jnp.zeros((t, HD), BF16)], axis=1)
    du = _matmul("proj_in_bwd_x", dp, w_in_all, NT, F32, 512, 512)
    g_w_in = _matmul("grad_w_in", u, dp, TN, BF16, 512, 768)
    grad_x, g_norm_mix = _rowwise_vjp("norm_mix_bwd", rms2, [xs], [norm_mix], [du, dh1], [F32], min(t, 256))

    big_grads = {"w_in": g_w_in, "w_mem_kv": g_w_kv, "w_out": g_w_out}
    small_grads = {
        "norm_mix": g_norm_mix, "mem_norm": g_mem_norm, "norm_ffn": g_norm_ffn,
        "gdn_conv": jnp.concatenate([dwq, dwk, dwv], axis=1),
        "fox_q_norm": g_fqn, "fox_k_norm": g_fkn, "gdn_out_norm": g_gon, "mem_q_norm": g_mqn, "mem_k_norm": g_mkn,
        "fox_f_bias": g_fb, "gdn_a_log": g_alog, "gdn_dt_bias": g_dtb}
    return grad_x, loss_local, big_grads, small_grads
```

```python
import functools
import math

import jax
import jax.numpy as jnp
from jax import lax
from jax.experimental import pallas as pl
from jax.experimental.pallas import tpu as pltpu

F32 = jnp.float32
BF16 = jnp.bfloat16
HI = lax.Precision.HIGHEST
SDS = jax.ShapeDtypeStruct

N_DEV = 8
HD = 128
NF, NG, NM = 6, 6, 4
CHUNK = 64
GROUP = 4
NORM_EPS = 1e-6
FQ, FK, FV, GQ, GK, GV, GZ, MQ, SM, NPB = 0, 6, 12, 18, 24, 30, 36, 42, 46, 48
PC = NPB * HD
L_FF, L_GA, L_GB = 0, 6, 12
VMEM_LIMIT = 56 * 1024 * 1024

ADAM_LR, ADAM_B1, ADAM_B2, ADAM_EPS, ADAM_WD, ADAM_STEP = 0.001, 0.9, 0.999, 1e-08, 0.01, 10

NN = (((1,), (0,)), ((), ()))
NT = (((1,), (1,)), ((), ()))
TN = (((0,), (0,)), ((), ()))
MESH = pl.DeviceIdType.MESH


def _cp(*sem):
    return pltpu.CompilerParams(dimension_semantics=tuple(sem) if sem else None, vmem_limit_bytes=VMEM_LIMIT)


def _dot(a, b, dims=NN):
    return lax.dot_general(a, b, dims, preferred_element_type=F32)


def _bdot(a, b):
    return _dot(a.astype(BF16), b.astype(BF16))


def _iota(shape, axis):
    return lax.broadcasted_iota(jnp.int32, shape, axis)


def _rms(x, gain):
    return x * lax.rsqrt(jnp.mean(x * x, axis=-1, keepdims=True) + NORM_EPS) * gain


def _sigmoid(x):
    z = jnp.exp(-jnp.abs(x))
    return jnp.where(x >= 0, 1.0 / (1.0 + z), z / (1.0 + z))


def _silu(x):
    return x * _sigmoid(x)


def _softplus(x):
    return jnp.maximum(x, 0.0) + jnp.log(1.0 + jnp.exp(-jnp.abs(x)))


def _lane_pick(x, lane):
    oh = (_iota((1, x.shape[-1]), 1) == lane).astype(F32)
    return jnp.sum(x * oh, axis=-1, keepdims=True)


def _cumsum_rows(x):
    tril = (_iota((HD, HD), 0) >= _iota((HD, HD), 1)).astype(F32)
    carry = jnp.zeros((1, x.shape[1]), F32)
    outs = []
    for b in range(x.shape[0] // HD):
        blk = x[b * HD:(b + 1) * HD]
        outs.append(jnp.dot(tril, blk, precision=HI, preferred_element_type=F32) + carry)
        carry = carry + jnp.sum(blk, axis=0, keepdims=True)
    return jnp.concatenate(outs, axis=0)


def _row_spec(r, tm):
    if isinstance(r, tuple):
        arr, width, cb = r
        return arr, pl.BlockSpec((tm, width), lambda i, cb=cb: (i, cb))
    return r, pl.BlockSpec((tm, r.shape[1]), lambda i: (i, 0))


ANY_SPEC = pl.BlockSpec(memory_space=pl.ANY)


def _rowwise(name, fn, rows, consts, outs, tm, deps=()):
    arrs, specs = zip(*[_row_spec(r, tm) for r in rows])
    n_rows = arrs[0].shape[0]
    nr, nc, nd = len(rows), len(consts), len(deps)

    def body(*refs):
        res = fn(*[r[...] for r in refs[:nr + nc]])
        for o, v in zip(refs[nr + nc + nd:], res):
            o[...] = v.astype(o.dtype)

    return pl.pallas_call(
        body, grid=(n_rows // tm,), name=name,
        in_specs=list(specs) + [pl.BlockSpec(c.shape, lambda i: (0, 0)) for c in consts] + [ANY_SPEC] * nd,
        out_specs=[pl.BlockSpec((tm, w), lambda i: (i, 0)) for w, _ in outs],
        out_shape=[SDS((n_rows, w), dt) for w, dt in outs],
        compiler_params=_cp("parallel"),
    )(*arrs, *consts, *deps)


def _rowwise_vjp(name, fn, rows, consts, cts, grad_dtypes, tm, deps=()):
    arrs, specs = zip(*[_row_spec(r, tm) for r in rows])
    ct_arrs, ct_specs = zip(*[_row_spec(r, tm) for r in cts])
    n_rows = arrs[0].shape[0]
    nr, nc, nct, ng, nd = len(rows), len(consts), len(cts), len(grad_dtypes), len(deps)
    widths = [s.block_shape[1] for s in specs[:ng]]

    def body(*refs):
        vals = [r[...].astype(F32) for r in refs[:nr + nc]]
        ctv = tuple(r[...].astype(F32) for r in refs[nr + nc:nr + nc + nct])
        _, vjp = jax.vjp(fn, *vals)
        grads = vjp(ctv)
        outs = refs[nr + nc + nct + nd:]
        for o, g in zip(outs[:ng], grads[:ng]):
            o[...] = g.astype(o.dtype)

        @pl.when(pl.program_id(0) == 0)
        def _():
            for o in outs[ng:]:
                o[...] = jnp.zeros_like(o)

        for o, g in zip(outs[ng:], grads[nr:]):
            o[...] += g

    return pl.pallas_call(
        body, grid=(n_rows // tm,), name=name,
        in_specs=list(specs) + [pl.BlockSpec(c.shape, lambda i: (0, 0)) for c in consts] + list(ct_specs)
        + [ANY_SPEC] * nd,
        out_specs=[pl.BlockSpec((tm, w), lambda i: (i, 0)) for w in widths]
        + [pl.BlockSpec(c.shape, lambda i: (0, 0)) for c in consts],
        out_shape=[SDS((n_rows, w), dt) for w, dt in zip(widths, grad_dtypes)] + [SDS(c.shape, F32) for c in consts],
        compiler_params=_cp("arbitrary"),
    )(*arrs, *consts, *ct_arrs, *deps)


def _tile(n, pref):
    t = min(n, pref)
    while n % t or (t % HD and t != n):
        t -= 1
    return t


def _matmul(name, a, b, dims, out_dtype, tm, tn, residual=None, deps=()):
    ta, tb = dims == TN, dims == NT
    m = a.shape[1] if ta else a.shape[0]
    k = a.shape[0] if ta else a.shape[1]
    n = b.shape[0] if tb else b.shape[1]
    tm, tn = _tile(m, tm), _tile(n, tn)

    def body(*refs):
        acc = _dot(refs[0][...], refs[1][...], dims)
        if residual is not None:
            acc = acc + refs[2][...]
        refs[-1][...] = acc.astype(out_dtype)

    in_specs = [pl.BlockSpec((k, tm), lambda i, j: (0, i)) if ta else pl.BlockSpec((tm, k), lambda i, j: (i, 0)),
                pl.BlockSpec((tn, k), lambda i, j: (j, 0)) if tb else pl.BlockSpec((k, tn), lambda i, j: (0, j))]
    ops = [a, b]
    if residual is not None:
        in_specs.append(pl.BlockSpec((tm, tn), lambda i, j: (i, j)))
        ops.append(residual)
    in_specs += [ANY_SPEC] * len(deps)
    ops += list(deps)
    return pl.pallas_call(
        body, grid=(m // tm, n // tn), name=name, in_specs=in_specs,
        out_specs=pl.BlockSpec((tm, tn), lambda i, j: (i, j)), out_shape=SDS((m, n), out_dtype),
        compiler_params=_cp("parallel", "parallel"),
    )(*ops)


def _ffn_up(h1n, wgu):
    t, d = h1n.shape
    w = wgu.shape[3]
    tm = _tile(t, 512)

    def body(a, b, gu, act):
        x = a[...]
        g = _dot(x, b[0])
        u = _dot(x, b[1])
        gu[0] = g.astype(BF16)
        gu[1] = u.astype(BF16)
        act[...] = (_silu(g) * u).astype(BF16)

    return pl.pallas_call(
        body, grid=(4, t // tm), name="ffn_up",
        in_specs=[pl.BlockSpec((tm, d), lambda j, i: (i, 0)), pl.BlockSpec((2, None, d, w), lambda j, i: (0, j, 0, 0))],
        out_specs=[pl.BlockSpec((2, None, tm, w), lambda j, i: (0, j, i, 0)), pl.BlockSpec((tm, w), lambda j, i: (i, j))],
        out_shape=[SDS((2, 4, t, w), BF16), SDS((t, 4 * w), BF16)],
        compiler_params=_cp("parallel", "parallel"),
    )(h1n, wgu)


def _ffn_down_loss(act, wdown, h1, target):
    t, f = act.shape
    d = wdown.shape[1]
    tm, tn = _tile(t, 512), _tile(d, 512)

    def body(a, b, h, tg, dy, dyb, ls):
        e = _dot(a[...], b[...]) + h[...] - tg[...]
        g = e * (1.0 / d)
        dy[...] = g
        dyb[...] = g.astype(BF16)
        ls[...] = jnp.broadcast_to(jnp.sum(e * e), (8, HD))

    return pl.pallas_call(
        body, grid=(t // tm, d // tn), name="ffn_down_loss",
        in_specs=[pl.BlockSpec((tm, f), lambda i, j: (i, 0)), pl.BlockSpec((f, tn), lambda i, j: (0, j)),
                  pl.BlockSpec((tm, tn), lambda i, j: (i, j)), pl.BlockSpec((tm, tn), lambda i, j: (i, j))],
        out_specs=[pl.BlockSpec((tm, tn), lambda i, j: (i, j)), pl.BlockSpec((tm, tn), lambda i, j: (i, j)),
                   pl.BlockSpec((8, HD), lambda i, j: (i, j))],
        out_shape=[SDS((t, d), F32), SDS((t, d), BF16), SDS((8 * (t // tm), HD * (d // tn)), F32)],
        compiler_params=_cp("parallel", "parallel"),
    )(act, wdown, h1, target)


def _ffn_down_bwd(dyb, wdown4, gu):
    t, d = dyb.shape
    w = wdown4.shape[1]
    tm = _tile(t, 512)

    def body(a, b, gu_ref, out):
        da = _dot(a[...], b[...], NT)
        g = gu_ref[0].astype(F32)
        u = gu_ref[1].astype(F32)
        s = _sigmoid(g)
        out[0] = (da * u * (s * (1.0 + g * (1.0 - s)))).astype(BF16)
        out[1] = (da * g * s).astype(BF16)

    return pl.pallas_call(
        body, grid=(4, t // tm), name="ffn_down_bwd",
        in_specs=[pl.BlockSpec((tm, d), lambda j, i: (i, 0)), pl.BlockSpec((None, w, d), lambda j, i: (j, 0, 0)),
                  pl.BlockSpec((2, None, tm, w), lambda j, i: (0, j, i, 0))],
        out_specs=pl.BlockSpec((2, None, tm, w), lambda j, i: (0, j, i, 0)),
        out_shape=SDS((2, 4, t, w), BF16),
        compiler_params=_cp("parallel", "parallel"),
    )(dyb, wdown4, gu)


def _ffn_up_bwd_x(dgu, wgu):
    _, t, w = dgu.shape
    d = wgu.shape[1]
    tm = _tile(t, 512)

    def body(a, b, out):
        @pl.when(pl.program_id(1) == 0)
        def _():
            out[...] = jnp.zeros_like(out)
        out[...] += _dot(a[...], b[...], NT)

    return pl.pallas_call(
        body, grid=(t // tm, 8), name="ffn_up_bwd_x",
        in_specs=[pl.BlockSpec((None, tm, w), lambda i, j: (j, i, 0)), pl.BlockSpec((None, d, w), lambda i, j: (j, 0, 0))],
        out_specs=pl.BlockSpec((tm, d), lambda i, j: (i, 0)), out_shape=SDS((t, d), F32),
        compiler_params=_cp("parallel", "arbitrary"),
    )(dgu, wgu)


def _ffn_up_bwd_w(h1n, dgu):
    _, t, w = dgu.shape
    d = h1n.shape[1]
    tm = _tile(d, 512)

    def body(a, b, out):
        out[...] = _dot(a[...], b[...], TN).astype(BF16)

    return pl.pallas_call(
        body, grid=(8, d // tm), name="ffn_up_bwd_w",
        in_specs=[pl.BlockSpec((t, tm), lambda j, i: (0, i)), pl.BlockSpec((None, t, w), lambda j, i: (j, 0, 0))],
        out_specs=pl.BlockSpec((None, tm, w), lambda j, i: (j, i, 0)), out_shape=SDS((8, d, w), BF16),
        compiler_params=_cp("parallel", "parallel"),
    )(h1n, dgu)


def _fox_prep(fq, fk, sm, fb, qg, kg, h):
    qn = _rms(fq, qg)
    kn = _rms(fk, kg)
    c = _cumsum_rows(-_softplus(-(sm + fb)))
    ccol = _lane_pick(c, L_FF + h)
    crow = jnp.sum(c.T * (_iota((HD, 1), 0) == L_FF + h).astype(F32), axis=0, keepdims=True)
    return qn, kn, ccol, crow


def _fox_block(q, k, v, cc, cr, off):
    s = _dot(q.astype(BF16), k.astype(BF16), NT) * (HD ** -0.5) + cc - cr
    s = jnp.where(_iota(s.shape, 1) <= _iota(s.shape, 0) + off, s, -1e30)
    e = jnp.exp(s - lax.stop_gradient(jnp.max(s, axis=1, keepdims=True)))
    p = e / jnp.sum(e, axis=1, keepdims=True)
    return _dot(p.astype(BF16), v.astype(BF16))


ONE_BUFFER = pl.Buffered(1)


def _pcol(t, cb):
    return pl.BlockSpec((t, HD), lambda h, cb=cb: (0, cb + h), pipeline_mode=ONE_BUFFER)


def _smcol(t):
    return pl.BlockSpec((t, HD), lambda h: (0, SM), pipeline_mode=ONE_BUFFER)


def _head(t):
    return pl.BlockSpec((t, HD), lambda h: (0, h), pipeline_mode=ONE_BUFFER)


def _small(n):
    return pl.BlockSpec((n, HD), lambda h: (0, 0), pipeline_mode=ONE_BUFFER)


def _fox_fwd(p, fb, qg, kg, bq):
    t = p.shape[0]

    def body(fq, fk, fv, sm, fb_r, qg_r, kg_r, o, qn_s, cc_s):
        h = pl.program_id(0)
        qn, kn, ccol, crow = _fox_prep(fq[...], fk[...], sm[...], fb_r[...], qg_r[...], kg_r[...], h)
        qn_s[...] = qn
        cc_s[...] = ccol
        knb = kn.astype(BF16)
        vb = fv[...].astype(BF16)
        for i in range(t // bq):
            rows, ext = pl.ds(i * bq, bq), (i + 1) * bq
            o[rows, :] = _fox_block(qn_s[rows, :], knb[:ext], vb[:ext], cc_s[rows, :], crow[:, :ext], i * bq).astype(o.dtype)

    return pl.pallas_call(
        body, grid=(NF,), name="fox_fwd",
        in_specs=[_pcol(t, FQ), _pcol(t, FK), _pcol(t, FV), _smcol(t), _small(1), _small(1), _small(1)],
        out_specs=_head(t), out_shape=SDS((t, NF * HD), BF16),
        scratch_shapes=[pltpu.VMEM((t, HD), F32), pltpu.VMEM((t, 1), F32)],
        compiler_params=_cp("parallel"),
    )(p, p, p, p, fb, qg, kg)


def _fox_bwd(p, fb, qg, kg, dmix, bq, deps=()):
    t = p.shape[0]

    def body(*refs):
        fq, fk, fv, sm, fb_r, qg_r, kg_r, do = refs[:8]
        dfq, dfk, dfv, dsm, dfb, dqg, dkg, qn_s, cc_s, dqn_s, dcc_s, dkn_s, dv_s, dcr_s = refs[8 + len(deps):]
        h = pl.program_id(0)
        qn, kn, ccol, crow = _fox_prep(fq[...], fk[...], sm[...], fb_r[...], qg_r[...], kg_r[...], h)
        qn_s[...] = qn
        cc_s[...] = ccol
        v = fv[...]
        dkn_s[...] = jnp.zeros_like(dkn_s)
        dv_s[...] = jnp.zeros_like(dv_s)
        dcr_s[...] = jnp.zeros_like(dcr_s)

        for i in range(t // bq):
            rows, ext = pl.ds(i * bq, bq), (i + 1) * bq
            _, vjp = jax.vjp(lambda a, b, c, d, e, off=i * bq: _fox_block(a, b, c, d, e, off),
                             qn_s[rows, :], kn[:ext], v[:ext], cc_s[rows, :], crow[:, :ext])
            dq, dk, dv, dcc, dcr = vjp(do[rows, :])
            dqn_s[rows, :] = dq
            dcc_s[rows, :] = dcc
            dkn_s[:ext, :] += dk
            dv_s[:ext, :] += dv
            dcr_s[:, :ext] += dcr
        _, prep_vjp = jax.vjp(lambda a, b, c, d, e, f: _fox_prep(a, b, c, d, e, f, h),
                              fq[...], fk[...], sm[...], fb_r[...], qg_r[...], kg_r[...])
        g_fq, g_fk, g_sm, g_fb, g_qg, g_kg = prep_vjp((dqn_s[...], dkn_s[...], dcc_s[...], dcr_s[...]))
        dfq[...] = g_fq.astype(dfq.dtype)
        dfk[...] = g_fk.astype(dfk.dtype)
        dfv[...] = dv_s[...].astype(dfv.dtype)

        @pl.when(h == 0)
        def _():
            for r in (dsm, dfb, dqg, dkg):
                r[...] = jnp.zeros_like(r)

        dsm[...] += g_sm
        dfb[...] += g_fb
        dqg[...] += g_qg
        dkg[...] += g_kg

    head = _head(t)
    return pl.pallas_call(
        body, grid=(NF,), name="fox_bwd",
        in_specs=[_pcol(t, FQ), _pcol(t, FK), _pcol(t, FV), _smcol(t), _small(1), _small(1), _small(1), head]
        + [ANY_SPEC] * len(deps),
        out_specs=[head, head, head, _small(t), _small(1), _small(1), _small(1)],
        out_shape=[SDS((t, NF * HD), BF16)] * 3 + [SDS((t, HD), F32)] + [SDS((1, HD), F32)] * 3,
        scratch_shapes=[pltpu.VMEM((t, HD), F32), pltpu.VMEM((t, 1), F32), pltpu.VMEM((t, HD), F32),
                        pltpu.VMEM((t, 1), F32), pltpu.VMEM((t, HD), F32), pltpu.VMEM((t, HD), F32),
                        pltpu.VMEM((1, t), F32)],
        compiler_params=_cp("arbitrary"),
    )(p, p, p, p, fb, qg, kg, dmix, *deps)


def _mem_attn(mq, mk, mv, qg, kg):
    s = _dot(_rms(mq, qg).astype(BF16), _rms(mk, kg).astype(BF16), NT) * (HD ** -0.5)
    e = jnp.exp(s - lax.stop_gradient(jnp.max(s, axis=1, keepdims=True)))
    p = e / jnp.sum(e, axis=1, keepdims=True)
    return _dot(p.astype(BF16), mv.astype(BF16))


def _mem_fwd(p, mkv, qg, kg):
    t, ml = p.shape[0], mkv.shape[0]

    def body(mq, mk, mv, qg_r, kg_r, o):
        o[...] = _mem_attn(mq[...], mk[...], mv[...], qg_r[...], kg_r[...]).astype(o.dtype)

    return pl.pallas_call(
        body, grid=(NM,), name="mem_fwd",
        in_specs=[_pcol(t, MQ), pl.BlockSpec((ml, HD), lambda h: (0, h)), pl.BlockSpec((ml, HD), lambda h: (0, NM + h)),
                  _small(1), _small(1)],
        out_specs=pl.BlockSpec((t, HD), lambda h: (0, h)), out_shape=SDS((t, NM * HD), BF16),
        compiler_params=_cp("parallel"),
    )(p, mkv, mkv, qg, kg)


def _mem_bwd(p, mkv, qg, kg, dmix, deps=()):
    t, ml = p.shape[0], mkv.shape[0]

    def body(*refs):
        mq, mk, mv, qg_r, kg_r, do = refs[:6]
        dmq, dmk, dmv, dqg, dkg = refs[6 + len(deps):]
        _, vjp = jax.vjp(_mem_attn, mq[...], mk[...], mv[...], qg_r[...], kg_r[...])
        g_q, g_k, g_v, g_qg, g_kg = vjp(do[...])
        dmq[...] = g_q.astype(dmq.dtype)
        dmk[...] = g_k
        dmv[...] = g_v

        @pl.when(pl.program_id(0) == 0)
        def _():
            dqg[...] = jnp.zeros_like(dqg)
            dkg[...] = jnp.zeros_like(dkg)

        dqg[...] += g_qg
        dkg[...] += g_kg

    return pl.pallas_call(
        body, grid=(NM,), name="mem_bwd",
        in_specs=[_pcol(t, MQ), pl.BlockSpec((ml, HD), lambda h: (0, h)), pl.BlockSpec((ml, HD), lambda h: (0, NM + h)),
                  _small(1), _small(1), pl.BlockSpec((t, HD), lambda h: (0, NF + NG + h))] + [ANY_SPEC] * len(deps),
        out_specs=[pl.BlockSpec((t, HD), lambda h: (0, h)), pl.BlockSpec((ml, HD), lambda h: (0, h)),
                   pl.BlockSpec((ml, HD), lambda h: (0, h)), _small(1), _small(1)],
        out_shape=[SDS((t, NM * HD), BF16), SDS((ml, NM * HD), F32), SDS((ml, NM * HD), F32),
                   SDS((1, HD), F32), SDS((1, HD), F32)],
        compiler_params=_cp("arbitrary"),
    )(p, mkv, mkv, qg, kg, dmix, *deps)


def _shift_down(x, s):
    if s == 0:
        return x
    return jnp.where(_iota(x.shape, 0) >= s, pltpu.roll(x, s, 0), 0.0)


def _shift_up(x, s):
    if s == 0:
        return x
    n = x.shape[0]
    return jnp.where(_iota(x.shape, 0) < n - s, pltpu.roll(x, n - s, 0), 0.0)


@jax.custom_vjp
def _conv4(x, w0, w1, w2, w3):
    return w0 * _shift_down(x, 3) + w1 * _shift_down(x, 2) + w2 * _shift_down(x, 1) + w3 * x


def _conv4_fwd(x, w0, w1, w2, w3):
    return _conv4(x, w0, w1, w2, w3), (x, w0, w1, w2, w3)


def _conv4_bwd(res, dy):
    x, w0, w1, w2, w3 = res
    dx = w0 * _shift_up(dy, 3) + w1 * _shift_up(dy, 2) + w2 * _shift_up(dy, 1) + w3 * dy
    dws = tuple(jnp.sum(dy * _shift_down(x, 3 - k), axis=0, keepdims=True) for k in range(4))
    return (dx,) + dws


_conv4.defvjp(_conv4_fwd, _conv4_bwd)


def _gdn_prep(gq, gk, gv, sm, taps, alog, dtb, h):
    q, k, v = [_silu(_conv4(x, *taps[4 * j:4 * j + 4])) for j, x in enumerate((gq, gk, gv))]
    q = q * lax.rsqrt(jnp.sum(q * q, axis=-1, keepdims=True) + NORM_EPS) * (HD ** -0.5)
    k = k * lax.rsqrt(jnp.sum(k * k, axis=-1, keepdims=True) + NORM_EPS)
    g = _lane_pick(-jnp.exp(alog) * _softplus(sm + dtb), L_GA + h)
    beta = _lane_pick(_sigmoid(sm), L_GB + h)
    return q, k, v, g, beta


def _split(x, n):
    parts, rest = [], x
    for i in range(n):
        parts.append(rest.astype(BF16))
        if i + 1 < n:
            rest = rest - parts[-1].astype(F32)
    return parts


def _raw_dot(a, b, form):
    lead = a.ndim - 2
    ca, cb = {"nn": (1, 0), "nt": (1, 1), "tn": (0, 0)}[form]
    batch = ((0,), (0,)) if lead else ((), ())
    return lax.dot_general(a, b, (((ca + lead,), (cb + lead,)), batch), preferred_element_type=F32)


def _pdot_impl(a, b, form, mode):
    if mode == "1":
        return _raw_dot(a.astype(BF16), b.astype(BF16), form)
    if mode == "3":
        (ah, al), (bh, bl) = _split(a, 2), _split(b, 2)
        return _raw_dot(ah, bh, form) + (_raw_dot(al, bh, form) + _raw_dot(ah, bl, form))
    if mode == "xa":
        return sum(_raw_dot(a.astype(BF16), t, form) for t in reversed(_split(b, 3)))
    return sum(_raw_dot(t, b.astype(BF16), form) for t in reversed(_split(a, 3)))


@functools.partial(jax.custom_vjp, nondiff_argnums=(2, 3))
def _pdot(a, b, form, mode):
    return _pdot_impl(a, b, form, mode)


def _pdot_fwd(a, b, form, mode):
    return _pdot_impl(a, b, form, mode), (a, b)


def _pdot_bwd(form, mode, res, ct):
    a, b = res
    da_args, db_args = {"nn": ((ct, b, "nt"), (a, ct, "tn")), "nt": ((ct, b, "nn"), (ct, a, "tn")),
                        "tn": ((b, ct, "nt"), (a, ct, "nn"))}[form]

    def side(args, exact):
        if mode in ("1", "3"):
            return mode
        return "xa" if args[0] is exact else "xb"

    if mode == "xa":
        return jnp.zeros_like(a), _pdot_impl(*db_args, side(db_args, a))
    if mode == "xb":
        return _pdot_impl(*da_args, side(da_args, b)), jnp.zeros_like(b)
    return _pdot_impl(*da_args, mode), _pdot_impl(*db_args, mode)


_pdot.defvjp(_pdot_fwd, _pdot_bwd)

GDN_QK, GDN_INV, GDN_SCAN = "1", "3", "1"


@jax.custom_vjp
def _tri_inv(low):
    eye = (_iota((CHUNK, CHUNK), 0) == _iota((CHUNK, CHUNK), 1)).astype(F32)
    inv = eye - low
    pw = low
    for _ in range(5):
        pw = _pdot_impl(pw, pw, "nn", GDN_INV)
        inv = inv + _pdot_impl(inv, pw, "nn", GDN_INV)
    return inv


def _tri_inv_fwd(low):
    inv = _tri_inv(low)
    return inv, inv


def _tri_inv_bwd(inv, ct):
    return (-_pdot_impl(_pdot_impl(inv, ct, "tn", GDN_INV), inv, "nt", GDN_INV),)


_tri_inv.defvjp(_tri_inv_fwd, _tri_inv_bwd)


def _gdn_intra(q, k, v, g, beta):
    n = q.shape[0]
    r, c = _iota((CHUNK, CHUNK), 0), _iota((CHUNK, CHUNK), 1)
    tril, strict = r >= c, r > c
    trilf = jnp.broadcast_to(tril.astype(F32), (n, CHUNK, CHUNK))
    gcm = _pdot(trilf, jnp.broadcast_to(g, (n, CHUNK, CHUNK)), "nn", "xa")
    gcf = _pdot(trilf, jnp.broadcast_to(g, (n, CHUNK, HD)), "nn", "xa")
    lane0 = (_iota((1, 1, CHUNK), 2) == 0).astype(F32)
    gcr = _pdot(jnp.ones((n, CHUNK, CHUNK), F32), gcm * lane0, "nt", "xa")
    decay = jnp.where(tril, jnp.exp(jnp.where(tril, gcm - gcr, 0.0)), 0.0)
    egc = jnp.exp(gcf)
    kb = k * beta
    low = jnp.where(strict, _pdot(kb, k, "nt", GDN_QK) * decay, 0.0)
    inv = _tri_inv(low)
    u = _pdot(inv, v * beta, "nn", GDN_INV)
    w = _pdot(inv, kb * egc, "nn", GDN_INV)
    at = jnp.where(tril, _pdot(q, k, "nt", GDN_QK) * decay, 0.0)
    gl = jnp.sum(jnp.broadcast_to(g, (n, CHUNK, HD)), axis=1, keepdims=True)
    return u, w, q * egc, at, k * jnp.exp(gl - gcf), gl


def _gdn_step(s, u, w, qg, at, kd, gl):
    vn = u - _pdot(w, s, "nn", GDN_SCAN)
    o = _pdot(qg, s, "nn", GDN_SCAN) + _pdot(at, vn, "nn", GDN_SCAN)
    s2 = s * jnp.exp(gl) + _pdot(kd, vn, "tn", GDN_SCAN)
    return o, s2


def _gdn_scratch(nc):
    big = pltpu.VMEM((nc, CHUNK, HD), F32)
    return [big, big, big, pltpu.VMEM((nc, CHUNK, 1), F32), pltpu.VMEM((nc, CHUNK, 1), F32),
            big, big, big, pltpu.VMEM((nc, CHUNK, CHUNK), F32), big, pltpu.VMEM((nc, 1, HD), F32)]


def _gdn_in_specs(t):
    cw = lambda cb: pl.BlockSpec((4, HD), lambda h, cb=cb: (0, cb + h))
    return [_pcol(t, GQ), _pcol(t, GK), _pcol(t, GV), _smcol(t), cw(0), cw(NG), cw(2 * NG), _small(1), _small(1)]


def _taps(wq, wk, wv):
    return tuple(w[k:k + 1, :] for w in (wq, wk, wv) for k in range(4))


def _gdn_stage(vals, refs):
    nc = refs[0].shape[0]
    for v, r in zip(vals, refs):
        r[...] = v.reshape(nc, CHUNK, v.shape[-1])


def _gdn_intra_all(chunked, intra):
    nc = chunked[0].shape[0]
    grp_n = math.gcd(nc, GROUP)

    def grp(i, carry):
        sl = pl.ds(pl.multiple_of(i * grp_n, grp_n), grp_n)
        for r, val in zip(intra, _gdn_intra(*[c[sl] for c in chunked])):
            r[sl] = val
        return carry

    lax.fori_loop(0, nc // grp_n, grp, 0)


def _gdn_fwd(p, conv, alog, dtb):
    t = p.shape[0]
    nc = t // CHUNK

    def body(gq, gk, gv, sm, wq, wk, wv, al, db, o, *scr):
        h = pl.program_id(0)
        chunked, intra = scr[:5], scr[5:]
        _gdn_stage(_gdn_prep(gq[...], gk[...], gv[...], sm[...], _taps(wq, wk, wv), al[...], db[...], h), chunked)
        _gdn_intra_all(chunked, intra)

        def step(c, s):
            oc, s2 = _gdn_step(s, *[r[c] for r in intra])
            o[pl.ds(pl.multiple_of(c * CHUNK, CHUNK), CHUNK), :] = oc
            return s2

        lax.fori_loop(0, nc, step, jnp.zeros((HD, HD), F32))

    return pl.pallas_call(
        body, grid=(NG,), name="gdn_fwd", in_specs=_gdn_in_specs(t),
        out_specs=_head(t), out_shape=SDS((t, NG * HD), F32),
        scratch_shapes=_gdn_scratch(nc), compiler_params=_cp("parallel"),
    )(p, p, p, p, conv, conv, conv, alog, dtb)


def _gdn_bwd(p, conv, alog, dtb, do_raw):
    t = p.shape[0]
    nc = t // CHUNK

    def body(gq, gk, gv, sm, wq, wk, wv, al, db, do, dgq, dgk, dgv, dsm, dwq, dwk, dwv, dal, ddb, *scr):
        h = pl.program_id(0)
        chunked, intra, states = scr[:5], scr[5:11], scr[11]
        _gdn_stage(_gdn_prep(gq[...], gk[...], gv[...], sm[...], _taps(wq, wk, wv), al[...], db[...], h), chunked)
        _gdn_intra_all(chunked, intra)

        def fwd(c, s):
            states[c] = s
            return _gdn_step(s, *[r[c] for r in intra])[1]

        lax.fori_loop(0, nc, fwd, jnp.zeros((HD, HD), F32))

        def bwd(i, ds):
            c = nc - 1 - i
            _, vjp = jax.vjp(_gdn_step, states[c], *[r[c] for r in intra])
            grads = vjp((do[pl.ds(pl.multiple_of(c * CHUNK, CHUNK), CHUNK), :], ds))
            for r, gval in zip(intra, grads[1:]):
                r[c] = gval
            return grads[0]

        lax.fori_loop(0, nc, bwd, jnp.zeros((HD, HD), F32))

        grp_n = math.gcd(nc, GROUP)

        def grp(i, carry):
            sl = pl.ds(pl.multiple_of(i * grp_n, grp_n), grp_n)
            _, vjp = jax.vjp(_gdn_intra, *[r[sl] for r in chunked])
            for r, gval in zip(chunked, vjp(tuple(r[sl] for r in intra))):
                r[sl] = gval
            return carry

        lax.fori_loop(0, nc // grp_n, grp, 0)
        _, prep_vjp = jax.vjp(
            lambda *a: _gdn_prep(*a, h), gq[...], gk[...], gv[...], sm[...], _taps(wq, wk, wv), al[...], db[...])
        grads = prep_vjp(tuple(r[...].reshape(t, r.shape[-1]) for r in chunked))
        for r, gval in zip((dgq, dgk, dgv), grads[:3]):
            r[...] = gval.astype(r.dtype)
        for j, r in enumerate((dwq, dwk, dwv)):
            for k in range(4):
                r[k:k + 1, :] = grads[4][4 * j + k]

        @pl.when(h == 0)
        def _():
            for r in (dsm, dal, ddb):
                r[...] = jnp.zeros_like(r)

        dsm[...] += grads[3]
        dal[...] += grads[5]
        ddb[...] += grads[6]

    head = _head(t)
    taps = pl.BlockSpec((4, HD), lambda h: (0, h))
    return pl.pallas_call(
        body, grid=(NG,), name="gdn_bwd", in_specs=_gdn_in_specs(t) + [head],
        out_specs=[head, head, head, _small(t), taps, taps, taps, _small(1), _small(1)],
        out_shape=[SDS((t, NG * HD), BF16)] * 3 + [SDS((t, HD), F32)] + [SDS((4, NG * HD), F32)] * 3 + [SDS((1, HD), F32)] * 2,
        scratch_shapes=_gdn_scratch(nc) + [pltpu.VMEM((nc, HD, HD), F32)], compiler_params=_cp("arbitrary"),
    )(p, p, p, p, conv, conv, conv, alog, dtb, do_raw)


def _gdn_post(o, z, gain):
    return (jnp.concatenate(
        [_rms(o[:, h * HD:(h + 1) * HD], gain) * _silu(z[:, h * HD:(h + 1) * HD]) for h in range(NG)], axis=1),)


def _place():
    return lax.axis_index("x"), lax.axis_index("y"), lax.axis_index("c")


def _all_gather(name, shard):
    def body(x_ref, out_ref, send_sems, recv_sems, local_sem):
        x, y, c = _place()
        me, sibling = (x, y, c), (x, y, 1 - c)
        chips = [(1 - x, y), (x, 1 - y), (1 - x, 1 - y)]

        def blk(px, py, pc):
            return out_ref.at[4 * px + 2 * py + pc]

        def copy(k, block, to, src=None):
            return pltpu.make_async_remote_copy(
                src_ref=blk(*block) if src is None else src, dst_ref=blk(*block),
                send_sem=send_sems.at[k], recv_sem=recv_sems.at[k], device_id=to, device_id_type=MESH)

        mine = pltpu.make_async_copy(x_ref, blk(*me), local_sem)
        mine.start()
        first = [copy(0, me, sibling, src=x_ref)]
        first += [copy(1 + j, me, (*chip, c), src=x_ref) for j, chip in enumerate(chips)]
        for cp in first:
            cp.start()
        passed = [copy(4 + j, (*chip, c), sibling) for j, chip in enumerate(chips)]
        for j, chip in enumerate(chips):
            copy(1 + j, (*chip, c), me).wait_recv()
            passed[j].start()
        copy(0, sibling, me).wait_recv()
        for j, chip in enumerate(chips):
            copy(4 + j, (*chip, 1 - c), me).wait_recv()
        for cp in first + passed:
            cp.wait_send()
        mine.wait()

    return pl.pallas_call(
        body, name=name, out_shape=SDS((N_DEV,) + shard.shape, shard.dtype),
        in_specs=[pl.BlockSpec(memory_space=pltpu.HBM)], out_specs=pl.BlockSpec(memory_space=pltpu.HBM),
        scratch_shapes=[pltpu.SemaphoreType.DMA((7,)), pltpu.SemaphoreType.DMA((7,)), pltpu.SemaphoreType.DMA],
    )(shard)


def _scatter_exchange(name, full):
    def body(g_ref, out_ref, send_sems, recv_sems, local_sem):
        x, y, c = _place()
        me = 4 * x + 2 * y + c
        mine = pltpu.make_async_copy(g_ref.at[me], out_ref.at[me], local_sem)
        mine.start()
        sends, recvs = [], []
        for k in range(1, N_DEV):
            px = 1 - x if k & 4 else x
            py = 1 - y if k & 2 else y
            pc = 1 - c if k & 1 else c
            peer = 4 * px + 2 * py + pc
            sends.append(pltpu.make_async_remote_copy(
                src_ref=g_ref.at[peer], dst_ref=out_ref.at[me], send_sem=send_sems.at[k - 1],
                recv_sem=recv_sems.at[k - 1], device_id=(px, py, pc), device_id_type=MESH))
            recvs.append(pltpu.make_async_remote_copy(
                src_ref=g_ref.at[me], dst_ref=out_ref.at[peer], send_sem=send_sems.at[k - 1],
                recv_sem=recv_sems.at[k - 1], device_id=(px, py, pc), device_id_type=MESH))
        for cp in sends:
            cp.start()
        for cp in recvs:
            cp.wait_recv()
        for cp in sends:
            cp.wait_send()
        mine.wait()

    return pl.pallas_call(
        body, name=name, out_shape=SDS(full.shape, full.dtype),
        in_specs=[pl.BlockSpec(memory_space=pltpu.HBM)], out_specs=pl.BlockSpec(memory_space=pltpu.HBM),
        scratch_shapes=[pltpu.SemaphoreType.DMA((7,)), pltpu.SemaphoreType.DMA((7,)), pltpu.SemaphoreType.DMA],
    )(full)


def _sum_blocks(name, parts):
    _, r, c = parts.shape
    tr = 64 if r % 64 == 0 else r

    def body(x, o):
        acc = x[0].astype(F32)
        for d in range(1, N_DEV):
            acc = acc + x[d].astype(F32)
        o[...] = acc

    return pl.pallas_call(
        body, grid=(r // tr,), name=name, in_specs=[pl.BlockSpec((N_DEV, tr, c), lambda i: (0, i, 0))],
        out_specs=pl.BlockSpec((tr, c), lambda i: (i, 0)), out_shape=SDS((r, c), F32), compiler_params=_cp("parallel"),
    )(parts)


def _reduce_scatter(name, full):
    return _sum_blocks(name + "_sum", _scatter_exchange(name, full))


def _all_reduce_small(name, x, reduce):
    m_per, n = x.shape

    def body(x_ref, out_ref, send_sems, recv_sems, local_sem):
        px, py, pc = _place()
        me, sibling = (px, py, pc), (px, py, 1 - pc)
        chips = [(1 - px, py), (px, 1 - py), (1 - px, 1 - py)]
        buf = out_ref

        def rows(qx, qy, qc):
            return buf.at[pl.ds((4 * qx + 2 * qy + qc) * m_per, m_per), :]

        def copy(k, block, to, src=None):
            return pltpu.make_async_remote_copy(
                src_ref=rows(*block) if src is None else src, dst_ref=rows(*block),
                send_sem=send_sems.at[k], recv_sem=recv_sems.at[k], device_id=to, device_id_type=MESH)

        mine = pltpu.make_async_copy(x_ref, rows(*me), local_sem)
        mine.start()
        first = [copy(0, me, sibling, src=x_ref)]
        first += [copy(1 + j, me, (*chip, pc), src=x_ref) for j, chip in enumerate(chips)]
        for cp in first:
            cp.start()
        passed = [copy(4 + j, (*chip, pc), sibling) for j, chip in enumerate(chips)]
        for j, chip in enumerate(chips):
            copy(1 + j, (*chip, pc), me).wait_recv()
            passed[j].start()
        copy(0, sibling, me).wait_recv()
        for j, chip in enumerate(chips):
            copy(4 + j, (*chip, 1 - pc), me).wait_recv()
        for cp in first + passed:
            cp.wait_send()
        mine.wait()

    gathered = pl.pallas_call(
        body, name=name, out_shape=SDS((N_DEV * m_per, n), x.dtype),
        in_specs=[pl.BlockSpec(memory_space=pltpu.VMEM)], out_specs=pl.BlockSpec(memory_space=pltpu.VMEM),
        scratch_shapes=[pltpu.SemaphoreType.DMA((7,)), pltpu.SemaphoreType.DMA((7,)), pltpu.SemaphoreType.DMA],
    )(x)
    if not reduce:
        return gathered
    return _sum_blocks(name + "_sum", gathered.reshape(N_DEV, m_per, n))


HBM_SPEC = pl.BlockSpec(memory_space=pltpu.HBM)
SEM_SPEC = pl.BlockSpec(memory_space=pltpu.SEMAPHORE)
EFFECT = pltpu.SideEffectType.DATAFLOW_SIDE_EFFECTING


def _copies_start(name, bufs, n_remote, n_local, build, deps):
    nb, nd = len(bufs), len(deps)
    sem_shapes = [pltpu.SemaphoreType.DMA((n_remote,)), pltpu.SemaphoreType.DMA((n_remote,))]
    if n_local:
        sem_shapes.append(pltpu.SemaphoreType.DMA((n_local,)))
    ns = len(sem_shapes)

    def body(*refs):
        sems = refs[nb + nd:nb + nd + ns]
        remote, local = build(refs[:nb], *sems, *([None] * (3 - ns)))
        for cp in local + remote:
            cp.start()
        refs[-1][...] = jnp.zeros((8, HD), F32)

    outs = pl.pallas_call(
        body, name=name,
        out_shape=(*sem_shapes, *[pltpu.HBM(b.shape, b.dtype) for b in bufs], SDS((8, HD), F32)),
        in_specs=[HBM_SPEC] * nb + [ANY_SPEC] * nd,
        out_specs=(*[SEM_SPEC] * ns, *[HBM_SPEC] * nb, pl.BlockSpec(memory_space=pltpu.VMEM)),
        input_output_aliases={i: ns + i for i in range(nb)},
        compiler_params=pltpu.CompilerParams(has_side_effects=EFFECT),
    )(*[pltpu.with_memory_space_constraint(b, pltpu.HBM) for b in bufs], *deps)
    return list(outs[:ns]), list(outs[ns:ns + nb]), outs[-1]


def _copies_wait(name, bufs, sems, build, after):
    nb, ns = len(bufs), len(sems)

    def body(*refs):
        remote, local = build(refs[:nb], *refs[nb:nb + ns], *([None] * (3 - ns)))
        for cp in local:
            cp.wait()
        for cp in remote:
            cp.wait_send()
            cp.wait_recv()

    outs = pl.pallas_call(
        body, name=name, out_shape=tuple(pltpu.HBM(b.shape, b.dtype) for b in bufs),
        in_specs=[HBM_SPEC] * nb + [SEM_SPEC] * ns + [ANY_SPEC] * len(after), out_specs=tuple([HBM_SPEC] * nb),
        input_output_aliases={i: i for i in range(nb)},
        compiler_params=pltpu.CompilerParams(has_side_effects=EFFECT),
    )(*bufs, *sems, *after)
    return list(outs)


def _remote(src, dst, send, recv, k, to):
    return pltpu.make_async_remote_copy(src_ref=src, dst_ref=dst, send_sem=send.at[k], recv_sem=recv.at[k],
                                        device_id=to, device_id_type=MESH)


class _Gather:
    def __init__(self, name, shards, deps):
        self.name, self.n = name, len(shards)
        lands = [lax.empty((N_DEV,) + s.shape, s.dtype) for s in shards]
        self.sems, bufs, self.token = _copies_start(name + "_s1", list(shards) + lands, 4 * self.n, self.n, self._stage1, deps)
        self.shards, self.lands = bufs[:self.n], bufs[self.n:]

    def _stage1(self, refs, send, recv, loc):
        x, y, c = _place()
        me = 4 * x + 2 * y + c
        targets = [(x, y, 1 - c), (1 - x, y, c), (x, 1 - y, c), (1 - x, 1 - y, c)]
        remote, local = [], []
        for i in range(self.n):
            src, land = refs[i], refs[self.n + i]
            local.append(pltpu.make_async_copy(src, land.at[me], loc.at[i]))
            remote += [_remote(src, land.at[me], send, recv, 4 * i + k, to) for k, to in enumerate(targets)]
        return remote, local

    def _stage2(self, refs, send, recv, loc):
        x, y, c = _place()
        remote = []
        for i in range(self.n):
            for j, (cx, cy) in enumerate([(1 - x, y), (x, 1 - y), (1 - x, 1 - y)]):
                blk = refs[i].at[4 * cx + 2 * cy + c]
                remote.append(_remote(blk, blk, send, recv, 3 * i + j, (x, y, 1 - c)))
        return remote, []

    def mid(self, after):
        bufs = _copies_wait(self.name + "_w1", self.shards + self.lands, self.sems, self._stage1, after)
        self.sems, self.lands, self.token = _copies_start(self.name + "_s2", bufs[self.n:], 3 * self.n, 0, self._stage2, ())

    def end(self, after):
        return _copies_wait(self.name + "_w2", self.lands, self.sems, self._stage2, after)


def _rows_tile(r, row_bytes, target=1 << 20):
    tr = r
    while tr % 32 == 0 and tr * row_bytes > target:
        tr //= 2
    return tr


def _pair_add(name, g, got, c):
    _, r, cols = g.shape
    tr = _rows_tile(r, cols * 2)

    def body(s, a, b, o):
        o[...] = (a[...].astype(F32) + b[...].astype(F32)).astype(o.dtype)

    return pl.pallas_call(
        body, name=name, out_shape=SDS((4, r, cols), g.dtype),
        grid_spec=pltpu.PrefetchScalarGridSpec(
            num_scalar_prefetch=1, grid=(4, r // tr),
            in_specs=[pl.BlockSpec((None, tr, cols), lambda j, i, s: (2 * j + s[0], i, 0)),
                      pl.BlockSpec((None, tr, cols), lambda j, i, s: (j, i, 0))],
            out_specs=pl.BlockSpec((None, tr, cols), lambda j, i, s: (j, i, 0))),
        compiler_params=_cp("parallel", "parallel"),
    )(c.reshape(1), g, got)


def _quad_sum(name, part, got, chip):
    _, r, cols = part.shape
    tr = _rows_tile(r, cols * 4)

    def body(s, a, b1, b2, b3, o):
        o[...] = ((a[...].astype(F32) + b1[...].astype(F32)) + b2[...].astype(F32)) + b3[...].astype(F32)

    blk = lambda k: pl.BlockSpec((None, tr, cols), lambda i, s, k=k: (jnp.bitwise_xor(s[0], k), i, 0))
    return pl.pallas_call(
        body, name=name, out_shape=SDS((r, cols), F32),
        grid_spec=pltpu.PrefetchScalarGridSpec(
            num_scalar_prefetch=1, grid=(r // tr,), in_specs=[blk(0), blk(1), blk(2), blk(3)],
            out_specs=pl.BlockSpec((tr, cols), lambda i, s: (i, 0))),
        compiler_params=_cp("parallel"),
    )(chip.reshape(1), part, got, got, got)


class _Scatter:
    def __init__(self, name, grads, deps):
        self.name, self.n = name, len(grads)
        got = [lax.empty((4,) + g.shape[1:], g.dtype) for g in grads]
        self.sems, bufs, self.token = _copies_start(name + "_s1", list(grads) + got, 4 * self.n, 0, self._stage1, deps)
        self.grads, self.got = bufs[:self.n], bufs[self.n:]

    def _stage1(self, refs, send, recv, loc):
        x, y, c = _place()
        remote = []
        for i in range(self.n):
            remote += [_remote(refs[i].at[2 * j + 1 - c], refs[self.n + i].at[j], send, recv, 4 * i + j, (x, y, 1 - c))
                       for j in range(4)]
        return remote, []

    def _stage2(self, refs, send, recv, loc):
        x, y, c = _place()
        remote = []
        for i in range(self.n):
            for k in (1, 2, 3):
                tx = 1 - x if k & 2 else x
                ty = 1 - y if k & 1 else y
                remote.append(_remote(refs[i].at[2 * tx + ty], refs[self.n + i].at[2 * x + y], send, recv,
                                      3 * i + k - 1, (tx, ty, c)))
        return remote, []

    def mid(self, after):
        bufs = _copies_wait(self.name + "_w1", self.grads + self.got, self.sems, self._stage1, after)
        c = lax.axis_index("c").astype(jnp.int32)
        parts = [_pair_add(f"{self.name}_add{i}", bufs[i], bufs[self.n + i], c) for i in range(self.n)]
        got = [lax.empty(p.shape, p.dtype) for p in parts]
        self.sems, bufs, self.token = _copies_start(self.name + "_s2", parts + got, 3 * self.n, 0, self._stage2, ())
        self.parts, self.got = bufs[:self.n], bufs[self.n:]

    def end(self, after):
        bufs = _copies_wait(self.name + "_w2", self.parts + self.got, self.sems, self._stage2, after)
        chip = (2 * lax.axis_index("x") + lax.axis_index("y")).astype(jnp.int32)
        return [_quad_sum(f"{self.name}_sum{i}", bufs[i], bufs[self.n + i], chip) for i in range(self.n)]


def _adamw(w, g, m, v):
    m = ADAM_B1 * m + (1.0 - ADAM_B1) * g
    v = ADAM_B2 * v + (1.0 - ADAM_B2) * (g * g)
    m_hat = m / (1.0 - ADAM_B1 ** ADAM_STEP)
    v_hat = v / (1.0 - ADAM_B2 ** ADAM_STEP)
    return -ADAM_LR * (m_hat / (jnp.sqrt(v_hat) + ADAM_EPS) + ADAM_WD * w), m, v


def _adamw_call(name, w, g, m, v):
    r, c = w.shape
    tm = 64 if r % 64 == 0 else r
    return _rowwise(name, _adamw, [w, g, m, v], [], [(c, F32)] * 3, tm)


_IN_COLS = 5906


def _perm_in(w):
    pad = jnp.zeros((w.shape[0], PC - _IN_COLS), w.dtype)
    return jnp.concatenate([w[:, :2304], w[:, 2310:4614], w[:, 4614:5382], w[:, 5394:5906], w[:, 2304:2310],
                            w[:, 5382:5394], pad], axis=1)


def _unperm_in(g):
    return jnp.concatenate([g[:, :2304], g[:, 5888:5894], g[:, 2304:4608], g[:, 4608:5376], g[:, 5894:5906],
                            g[:, 5376:5888]], axis=1)


def _lanes(v, at):
    return jnp.pad(v, ((0, 0), (at, HD - at - v.shape[1])))


_PACK = ("norm_mix", "mem_norm", "norm_ffn", "gdn_conv", "fox_q_norm", "fox_k_norm", "gdn_out_norm", "mem_q_norm",
         "mem_k_norm", "fox_f_bias", "gdn_a_log", "gdn_dt_bias", "loss")


def _pack(vals):
    parts = [vals[n].reshape(-1, HD) for n in _PACK]
    used = sum(p.shape[0] for p in parts)
    buf = jnp.concatenate(parts + [jnp.zeros((-used % 8, HD), F32)], axis=0)
    return buf, [(n, p.shape[0]) for n, p in zip(_PACK, parts)]


def _unpack(buf, layout):
    out, at = {}, 0
    for n, rows in layout:
        out[n] = buf[at:at + rows]
        at += rows
    return out


def kernel(x, mem, norm_mix, w_in, fox_f_bias, fox_q_norm, fox_k_norm, gdn_conv, gdn_a_log, gdn_dt_bias, gdn_out_norm, mem_norm, w_mem_kv, mem_q_norm, mem_k_norm, w_out, norm_ffn, w_gate_up, w_down, loss_target, m_norm_mix, m_w_in, m_fox_f_bias, m_fox_q_norm, m_fox_k_norm, m_gdn_conv, m_gdn_a_log, m_gdn_dt_bias, m_gdn_out_norm, m_mem_norm, m_w_mem_kv, m_mem_q_norm, m_mem_k_norm, m_w_out, m_norm_ffn, m_w_gate_up, m_w_down, v_norm_mix, v_w_in, v_fox_f_bias, v_fox_q_norm, v_fox_k_norm, v_gdn_conv, v_gdn_a_log, v_gdn_dt_bias, v_gdn_out_norm, v_mem_norm, v_w_mem_kv, v_mem_q_norm, v_mem_k_norm, v_w_out, v_norm_ffn, v_w_gate_up, v_w_down):
    args = dict(locals())
    d = x.shape[2]
    me = 4 * lax.axis_index("x") + 2 * lax.axis_index("y") + lax.axis_index("c")

    w_in_all = _all_gather("ag_w_in", _perm_in(w_in[0]).astype(BF16)).reshape(d, PC)
    w_kv_all = _all_gather("ag_w_kv", w_mem_kv[0].astype(BF16)).reshape(d, 2 * NM * HD)
    cshard = gdn_conv[0].shape[1]
    conv_pad = jnp.pad(gdn_conv[0], ((0, 4), (0, 3 * HD - cshard)))
    conv_all = _all_reduce_small("ag_conv", conv_pad, False).reshape(N_DEV, 8, 3 * HD)[:, :4, :cshard]
    conv_all = conv_all.transpose(1, 0, 2).reshape(4, N_DEV * cshard)
    comm = _StepComm([w_out[0].astype(BF16), w_gate_up[0].astype(BF16), w_down[0].astype(BF16)],
                     [w_in_all, w_kv_all, conv_all])

    grad_x, loss_local, big_grads, small_grads = _local_step(
        x[0], mem[0], loss_target[0], norm_mix, fox_f_bias, fox_q_norm, fox_k_norm, gdn_a_log, gdn_dt_bias,
        gdn_out_norm, mem_norm, mem_q_norm, mem_k_norm, norm_ffn, w_in_all, w_kv_all, conv_all, comm)

    blocks = lambda g: g.reshape(N_DEV, g.shape[0] // N_DEV, g.shape[1])
    last = _Scatter("rs_b", [blocks(big_grads[n]) for n in ("w_in", "w_mem_kv", "w_out")], ())
    last.mid(())
    grads = dict(zip(("w_down", "w_gate_up"), comm.ffn_grads_end([grad_x])))
    grads.update(zip(("w_in", "w_mem_kv", "w_out"), last.end(())))
    grads["w_in"] = _unperm_in(grads["w_in"])
    small_grads["loss"] = jnp.broadcast_to(loss_local, (1, HD))
    packed, layout = _pack(small_grads)
    small = _unpack(_all_reduce_small("ar_small", packed, True), layout)
    loss = small["loss"][0, 0]
    six = {"fox_f_bias": L_FF, "gdn_a_log": L_GA, "gdn_dt_bias": L_GA}
    for n, rows_n in layout[:-1]:
        gsm = small[n]
        if n == "gdn_conv":
            gsm = lax.dynamic_slice(gsm.reshape(4, N_DEV * cshard), (0, me * cshard), (4, cshard))[None]
        elif n in six:
            gsm = gsm[:, six[n]:six[n] + 6]
        else:
            gsm = gsm.reshape(1, rows_n * HD)
        grads[n] = gsm

    names = ['norm_mix', 'w_in', 'fox_f_bias', 'fox_q_norm', 'fox_k_norm', 'gdn_conv', 'gdn_a_log', 'gdn_dt_bias',
             'gdn_out_norm', 'mem_norm', 'w_mem_kv', 'mem_q_norm', 'mem_k_norm', 'w_out', 'norm_ffn', 'w_gate_up', 'w_down']
    big = ("w_in", "w_mem_kv", "w_out", "w_gate_up", "w_down")
    delta, new_m, new_v = {}, {}, {}
    for n in big:
        delta[n], new_m[n], new_v[n] = [a[None] for a in _adamw_call(
            "adamw_" + n, args[n][0], grads[n], args["m_" + n][0], args["v_" + n][0])]
        grads[n] = grads[n][None]

    def flat(a):
        a = a.reshape(1, -1)
        return jnp.pad(a, ((0, 0), (0, -a.shape[1] % HD))).reshape(-1, HD)

    smalls = [n for n in names if n not in big]
    pk = lambda pre: jnp.concatenate([flat(grads[n] if pre == "g" else args[pre + n]) for n in smalls], axis=0)
    cat = [pk(""), pk("g"), pk("m_"), pk("v_")]
    padr = -cat[0].shape[0] % 8
    cat = [jnp.pad(a, ((0, padr), (0, 0))) for a in cat]
    res = _adamw_call("adamw_small", *cat)
    at = 0
    for n in smalls:
        shape = args[n].shape
        size = math.prod(shape)
        nrow = -(-size // HD)
        for dst, src in zip((delta, new_m, new_v), res):
            dst[n] = src[at:at + nrow].reshape(-1)[:size].reshape(shape)
        at += nrow

    return (loss, grad_x[None], *[grads[n] for n in names], *[delta[n] for n in names],
            *[new_m[n] for n in names], *[new_v[n] for n in names])


class _StepComm:
    def __init__(self, late_shards, after):
        self.gather = _Gather("ag_late", late_shards, after)

    def start_deps(self):
        return [self.gather.token]

    def after_mixers(self, after):
        self.gather.mid(after)
        return [self.gather.token]

    def late_weights(self, after):
        w_out_all, wgu, w_down_all = self.gather.end(after)
        return w_out_all.reshape(-1, w_out_all.shape[-1]), wgu, w_down_all.reshape(-1, w_down_all.shape[-1])

    def ffn_grads(self, g_w_down, g_w_gu):
        self.scatter = _Scatter("rs_ffn", [g_w_down.reshape(N_DEV, -1, g_w_down.shape[1]), g_w_gu], ())
        return [self.scatter.token]

    def ffn_grads_mid(self, after):
        self.scatter.mid(after)
        return [self.scatter.token]

    def ffn_grads_end(self, after):
        return self.scatter.end(after)


def _local_step(xs, ms, tgt, norm_mix, fox_f_bias, fox_q_norm, fox_k_norm, gdn_a_log, gdn_dt_bias, gdn_out_norm,
                mem_norm, mem_q_norm, mem_k_norm, norm_ffn, w_in_all, w_kv_all, conv_all, comm):
    t, d = xs.shape
    bq = min(t, 256)
    fb, alog, dtb = _lanes(fox_f_bias, L_FF), _lanes(gdn_a_log, L_GA), _lanes(gdn_dt_bias, L_GA)

    rms1 = lambda a, g: (_rms(a, g),)
    (u,) = _rowwise("norm_mix", rms1, [xs], [norm_mix], [(d, BF16)], min(t, 256), deps=comm.start_deps())
    p = _matmul("proj_in", u, w_in_all, NN, F32, 1024, 768)
    o_fox = _fox_fwd(p, fb, fox_q_norm, fox_k_norm, bq)
    o_gdn_raw = _gdn_fwd(p, conv_all, alog, dtb)
    zrow = (p, NG * HD, GZ * HD // (NG * HD))
    (o_gdn,) = _rowwise("gdn_post", _gdn_post, [o_gdn_raw, zrow], [gdn_out_norm], [(NG * HD, BF16)], min(t, 256))
    deps = comm.after_mixers([o_fox, o_gdn])
    (mem_n,) = _rowwise("norm_mem", rms1, [ms], [mem_norm], [(d, BF16)], ms.shape[0], deps=deps)
    mkv = _matmul("proj_mem", mem_n, w_kv_all, NN, F32, 256, 512)
    o_mem = _mem_fwd(p, mkv, mem_q_norm, mem_k_norm)
    w_out_all, wgu, w_down_all = comm.late_weights([o_mem])
    ffw = wgu.shape[2]
    mix = jnp.concatenate([o_fox, o_gdn, o_mem], axis=1)
    h1 = _matmul("proj_out", mix, w_out_all, NN, F32, 1024, 512, residual=xs)
    (h1n,) = _rowwise("norm_ffn", rms1, [h1], [norm_ffn], [(d, BF16)], min(t, 256))
    gu, act = _ffn_up(h1n, wgu.reshape(2, 4, d, ffw))
    dy, dyb, lsum = _ffn_down_loss(act, w_down_all, h1, tgt)
    loss_local = (0.5 / d) * jnp.sum(lsum[::8, ::HD])

    dgu = _ffn_down_bwd(dyb, w_down_all.reshape(4, ffw, d), gu).reshape(8, t, ffw)
    g_w_down = _matmul("grad_w_down", act, dyb, TN, BF16, 512, 512)
    dh1n = _ffn_up_bwd_x(dgu, wgu)
    g_w_gu = _ffn_up_bwd_w(h1n, dgu)
    deps = comm.ffn_grads(g_w_down, g_w_gu)
    rms2 = lambda a, g: (_rms(a, g), a)
    dh1, g_norm_ffn = _rowwise_vjp("norm_ffn_bwd", rms2, [h1], [norm_ffn], [dh1n, dy], [F32], min(t, 256), deps=deps)
    dh1b = dh1.astype(BF16)

    dmix = _matmul("proj_out_bwd_x", dh1b, w_out_all, NT, F32, 1024, 512)
    g_w_out = _matmul("grad_w_out", mix, dh1b, TN, BF16, 512, 512)
    deps = comm.ffn_grads_mid([dmix, g_w_out])
    dfq, dfk, dfv, dsm_fox, g_fb, g_fqn, g_fkn = _fox_bwd(p, fb, fox_q_norm, fox_k_norm, dmix, bq, deps=deps)
    do_raw, dgz, g_gon = _rowwise_vjp("gdn_post_bwd", _gdn_post, [o_gdn_raw, zrow], [gdn_out_norm],
                                      [(dmix, NG * HD, 1)], [F32, BF16], min(t, 256), deps=deps)
    dgq, dgk, dgv, dsm_gdn, dwq, dwk, dwv, g_alog, g_dtb = _gdn_bwd(p, conv_all, alog, dtb, do_raw)
    dmq, dmk, dmv, g_mqn, g_mkn = _mem_bwd(p, mkv, mem_q_norm, mem_k_norm, dmix, deps=deps)
    dmkv = jnp.concatenate([dmk, dmv], axis=1).astype(BF16)
    dmem_n = _matmul("proj_mem_bwd_x", dmkv, w_kv_all, NT, F32, 256, 512)
    g_w_kv = _matmul("grad_w_kv", mem_n, dmkv, TN, BF16, 512, 512)
    g_mem_norm = _rowwise_vjp("norm_mem_bwd", rms1, [ms], [mem_norm], [dmem_n], [], ms.shape[0])[0]
    dp = jnp.concatenate([dfq, dfk, dfv, dgq, dgk, dgv, dgz, dmq, (dsm_fox + dsm_gdn).astype(BF16),
                          jnp.zeros((t, HD), BF16)], axis=1)
    du = _matmul("proj_in_bwd_x", dp, w_in_all, NT, F32, 512, 512)
    g_w_in = _matmul("grad_w_in", u, dp, TN, BF16, 512, 768)
    grad_x, g_norm_mix = _rowwise_vjp("norm_mix_bwd", rms2, [xs], [norm_mix], [du, dh1], [F32], min(t, 256))

    big_grads = {"w_in": g_w_in, "w_mem_kv": g_w_kv, "w_out": g_w_out}
    small_grads = {
        "norm_mix": g_norm_mix, "mem_norm": g_mem_norm, "norm_ffn": g_norm_ffn,
        "gdn_conv": jnp.concatenate([dwq, dwk, dwv], axis=1),
        "fox_q_norm": g_fqn, "fox_k_norm": g_fkn, "gdn_out_norm": g_gon, "mem_q_norm": g_mqn, "mem_k_norm": g_mkn,
        "fox_f_bias": g_fb, "gdn_a_log": g_alog, "gdn_dt_bias": g_dtb}
    return grad_x, loss_local, big_grads, small_grads
```

```python
import functools
import math

import jax
import jax.numpy as jnp
from jax import lax
from jax.experimental import pallas as pl
from jax.experimental.pallas import tpu as pltpu

F32 = jnp.float32
BF16 = jnp.bfloat16
HI = lax.Precision.HIGHEST
SDS = jax.ShapeDtypeStruct

N_DEV = 8
HD = 128
NF, NG, NM = 6, 6, 4
CHUNK = 64
GROUP = 4
NORM_EPS = 1e-6
GQ, GK, GV, GZ, FQ, FK, FV, MQ, SM, NPB = 0, 6, 12, 18, 24, 30, 36, 42, 46, 48
PC = NPB * HD
HALF = PC // 2
L_FF, L_GA, L_GB = 0, 6, 12
VMEM_LIMIT = 56 * 1024 * 1024

ADAM_LR, ADAM_B1, ADAM_B2, ADAM_EPS, ADAM_WD, ADAM_STEP = 0.001, 0.9, 0.999, 1e-08, 0.01, 10

NN = (((1,), (0,)), ((), ()))
NT = (((1,), (1,)), ((), ()))
TN = (((0,), (0,)), ((), ()))
MESH = pl.DeviceIdType.MESH


def _cp(*sem):
    return pltpu.CompilerParams(dimension_semantics=tuple(sem) if sem else None, vmem_limit_bytes=VMEM_LIMIT)


def _dot(a, b, dims=NN):
    return lax.dot_general(a, b, dims, preferred_element_type=F32)


def _bdot(a, b):
    return _dot(a.astype(BF16), b.astype(BF16))


def _iota(shape, axis):
    return lax.broadcasted_iota(jnp.int32, shape, axis)


def _rms(x, gain):
    return x * lax.rsqrt(jnp.mean(x * x, axis=-1, keepdims=True) + NORM_EPS) * gain


def _sigmoid(x):
    z = jnp.exp(-jnp.abs(x))
    return jnp.where(x >= 0, 1.0 / (1.0 + z), z / (1.0 + z))


def _silu(x):
    return x * _sigmoid(x)


def _softplus(x):
    return jnp.maximum(x, 0.0) + jnp.log(1.0 + jnp.exp(-jnp.abs(x)))


def _lane_pick(x, lane):
    oh = (_iota((1, x.shape[-1]), 1) == lane).astype(F32)
    return jnp.sum(x * oh, axis=-1, keepdims=True)


def _cumsum_rows(x):
    tril = (_iota((HD, HD), 0) >= _iota((HD, HD), 1)).astype(F32)
    carry = jnp.zeros((1, x.shape[1]), F32)
    outs = []
    for b in range(x.shape[0] // HD):
        blk = x[b * HD:(b + 1) * HD]
        outs.append(jnp.dot(tril, blk, precision=HI, preferred_element_type=F32) + carry)
        carry = carry + jnp.sum(blk, axis=0, keepdims=True)
    return jnp.concatenate(outs, axis=0)


def _row_spec(r, tm):
    if isinstance(r, tuple):
        arr, width, cb = r
        return arr, pl.BlockSpec((tm, width), lambda i, cb=cb: (i, cb))
    return r, pl.BlockSpec((tm, r.shape[1]), lambda i: (i, 0))


ANY_SPEC = pl.BlockSpec(memory_space=pl.ANY)


def _rowwise(name, fn, rows, consts, outs, tm, deps=()):
    arrs, specs = zip(*[_row_spec(r, tm) for r in rows])
    n_rows = arrs[0].shape[0]
    nr, nc, nd = len(rows), len(consts), len(deps)

    def body(*refs):
        res = fn(*[r[...] for r in refs[:nr + nc]])
        for o, v in zip(refs[nr + nc + nd:], res):
            o[...] = v.astype(o.dtype)

    return pl.pallas_call(
        body, grid=(n_rows // tm,), name=name,
        in_specs=list(specs) + [pl.BlockSpec(c.shape, lambda i: (0, 0)) for c in consts] + [ANY_SPEC] * nd,
        out_specs=[pl.BlockSpec((tm, w), lambda i: (i, 0)) for w, _ in outs],
        out_shape=[SDS((n_rows, w), dt) for w, dt in outs],
        compiler_params=_cp("parallel"),
    )(*arrs, *consts, *deps)


def _rowwise_vjp(name, fn, rows, consts, cts, grad_dtypes, tm, deps=()):
    arrs, specs = zip(*[_row_spec(r, tm) for r in rows])
    ct_arrs, ct_specs = zip(*[_row_spec(r, tm) for r in cts])
    n_rows = arrs[0].shape[0]
    nr, nc, nct, ng, nd = len(rows), len(consts), len(cts), len(grad_dtypes), len(deps)
    widths = [s.block_shape[1] for s in specs[:ng]]

    def body(*refs):
        vals = [r[...].astype(F32) for r in refs[:nr + nc]]
        ctv = tuple(r[...].astype(F32) for r in refs[nr + nc:nr + nc + nct])
        _, vjp = jax.vjp(fn, *vals)
        grads = vjp(ctv)
        outs = refs[nr + nc + nct + nd:]
        for o, g in zip(outs[:ng], grads[:ng]):
            o[...] = g.astype(o.dtype)

        @pl.when(pl.program_id(0) == 0)
        def _():
            for o in outs[ng:]:
                o[...] = jnp.zeros_like(o)

        for o, g in zip(outs[ng:], grads[nr:]):
            o[...] += g

    return pl.pallas_call(
        body, grid=(n_rows // tm,), name=name,
        in_specs=list(specs) + [pl.BlockSpec(c.shape, lambda i: (0, 0)) for c in consts] + list(ct_specs)
        + [ANY_SPEC] * nd,
        out_specs=[pl.BlockSpec((tm, w), lambda i: (i, 0)) for w in widths]
        + [pl.BlockSpec(c.shape, lambda i: (0, 0)) for c in consts],
        out_shape=[SDS((n_rows, w), dt) for w, dt in zip(widths, grad_dtypes)] + [SDS(c.shape, F32) for c in consts],
        compiler_params=_cp("arbitrary"),
    )(*arrs, *consts, *ct_arrs, *deps)


def _tile(n, pref):
    t = min(n, pref)
    while n % t or (t % HD and t != n):
        t -= 1
    return t


def _matmul(name, a, b, dims, out_dtype, tm, tn, residual=None, deps=(), kidx=0):
    ta, tb = dims == TN, dims == NT
    m = a.shape[1] if ta else a.shape[0]
    k = a.shape[0] if ta else a.shape[1]
    n = b.shape[0] if tb else b.shape[1]
    tm, tn = _tile(m, tm), _tile(n, tn)
    assert kidx == 0 or (tb and b.shape[1] % k == 0)

    def body(*refs):
        acc = _dot(refs[0][...], refs[1][...], dims)
        if residual is not None:
            acc = acc + refs[2][...]
        refs[-1][...] = acc.astype(out_dtype)

    in_specs = [pl.BlockSpec((k, tm), lambda i, j: (0, i)) if ta else pl.BlockSpec((tm, k), lambda i, j: (i, 0)),
                pl.BlockSpec((tn, k), lambda i, j: (j, kidx)) if tb else pl.BlockSpec((k, tn), lambda i, j: (0, j))]
    ops = [a, b]
    if residual is not None:
        in_specs.append(pl.BlockSpec((tm, tn), lambda i, j: (i, j)))
        ops.append(residual)
    in_specs += [ANY_SPEC] * len(deps)
    ops += list(deps)
    return pl.pallas_call(
        body, grid=(m // tm, n // tn), name=name, in_specs=in_specs,
        out_specs=pl.BlockSpec((tm, tn), lambda i, j: (i, j)), out_shape=SDS((m, n), out_dtype),
        compiler_params=_cp("parallel", "parallel"),
    )(*ops)


def _ffn_up(h1n, wgu):
    t, d = h1n.shape
    w = wgu.shape[3]
    tm = _tile(t, 512)

    def body(a, b, gu, act):
        x = a[...]
        g = _dot(x, b[0])
        u = _dot(x, b[1])
        gu[0] = g.astype(BF16)
        gu[1] = u.astype(BF16)
        act[...] = (_silu(g) * u).astype(BF16)

    return pl.pallas_call(
        body, grid=(4, t // tm), name="ffn_up",
        in_specs=[pl.BlockSpec((tm, d), lambda j, i: (i, 0)), pl.BlockSpec((2, None, d, w), lambda j, i: (0, j, 0, 0))],
        out_specs=[pl.BlockSpec((2, None, tm, w), lambda j, i: (0, j, i, 0)), pl.BlockSpec((tm, w), lambda j, i: (i, j))],
        out_shape=[SDS((2, 4, t, w), BF16), SDS((t, 4 * w), BF16)],
        compiler_params=_cp("parallel", "parallel"),
    )(h1n, wgu)


def _ffn_down_loss(act, wdown, h1, target):
    t, f = act.shape
    d = wdown.shape[1]
    tm, tn = _tile(t, 512), _tile(d, 512)

    def body(a, b, h, tg, dy, dyb, ls):
        e = _dot(a[...], b[...]) + h[...] - tg[...]
        g = e * (1.0 / d)
        dy[...] = g
        dyb[...] = g.astype(BF16)
        ls[...] = jnp.broadcast_to(jnp.sum(e * e), (8, HD))

    return pl.pallas_call(
        body, grid=(t // tm, d // tn), name="ffn_down_loss",
        in_specs=[pl.BlockSpec((tm, f), lambda i, j: (i, 0)), pl.BlockSpec((f, tn), lambda i, j: (0, j)),
                  pl.BlockSpec((tm, tn), lambda i, j: (i, j)), pl.BlockSpec((tm, tn), lambda i, j: (i, j))],
        out_specs=[pl.BlockSpec((tm, tn), lambda i, j: (i, j)), pl.BlockSpec((tm, tn), lambda i, j: (i, j)),
                   pl.BlockSpec((8, HD), lambda i, j: (i, j))],
        out_shape=[SDS((t, d), F32), SDS((t, d), BF16), SDS((8 * (t // tm), HD * (d // tn)), F32)],
        compiler_params=_cp("parallel", "parallel"),
    )(act, wdown, h1, target)


def _ffn_down_bwd(dyb, wdown4, gu):
    t, d = dyb.shape
    w = wdown4.shape[1]
    tm = _tile(t, 512)

    def body(a, b, gu_ref, out):
        da = _dot(a[...], b[...], NT)
        g = gu_ref[0].astype(F32)
        u = gu_ref[1].astype(F32)
        s = _sigmoid(g)
        out[0] = (da * u * (s * (1.0 + g * (1.0 - s)))).astype(BF16)
        out[1] = (da * g * s).astype(BF16)

    return pl.pallas_call(
        body, grid=(4, t // tm), name="ffn_down_bwd",
        in_specs=[pl.BlockSpec((tm, d), lambda j, i: (i, 0)), pl.BlockSpec((None, w, d), lambda j, i: (j, 0, 0)),
                  pl.BlockSpec((2, None, tm, w), lambda j, i: (0, j, i, 0))],
        out_specs=pl.BlockSpec((2, None, tm, w), lambda j, i: (0, j, i, 0)),
        out_shape=SDS((2, 4, t, w), BF16),
        compiler_params=_cp("parallel", "parallel"),
    )(dyb, wdown4, gu)


def _ffn_up_bwd_x(dgu, wgu):
    _, t, w = dgu.shape
    d = wgu.shape[1]
    tm = _tile(t, 512)

    def body(a, b, out):
        @pl.when(pl.program_id(1) == 0)
        def _():
            out[...] = jnp.zeros_like(out)
        out[...] += _dot(a[...], b[...], NT)

    return pl.pallas_call(
        body, grid=(t // tm, 8), name="ffn_up_bwd_x",
        in_specs=[pl.BlockSpec((None, tm, w), lambda i, j: (j, i, 0)), pl.BlockSpec((None, d, w), lambda i, j: (j, 0, 0))],
        out_specs=pl.BlockSpec((tm, d), lambda i, j: (i, 0)), out_shape=SDS((t, d), F32),
        compiler_params=_cp("parallel", "arbitrary"),
    )(dgu, wgu)


def _ffn_up_bwd_w(h1n, dgu):
    _, t, w = dgu.shape
    d = h1n.shape[1]
    tm = _tile(d, 512)

    def body(a, b, out):
        out[...] = _dot(a[...], b[...], TN).astype(BF16)

    return pl.pallas_call(
        body, grid=(8, d // tm), name="ffn_up_bwd_w",
        in_specs=[pl.BlockSpec((t, tm), lambda j, i: (0, i)), pl.BlockSpec((None, t, w), lambda j, i: (j, 0, 0))],
        out_specs=pl.BlockSpec((None, tm, w), lambda j, i: (j, i, 0)), out_shape=SDS((8, d, w), BF16),
        compiler_params=_cp("parallel", "parallel"),
    )(h1n, dgu)


def _fox_prep(fq, fk, sm, fb, qg, kg, h):
    qn = _rms(fq, qg)
    kn = _rms(fk, kg)
    c = _cumsum_rows(-_softplus(-(sm + fb)))
    ccol = _lane_pick(c, L_FF + h)
    crow = jnp.sum(c.T * (_iota((HD, 1), 0) == L_FF + h).astype(F32), axis=0, keepdims=True)
    return qn, kn, ccol, crow


def _fox_block(q, k, v, cc, cr, off):
    s = _dot(q.astype(BF16), k.astype(BF16), NT) * (HD ** -0.5) + cc - cr
    s = jnp.where(_iota(s.shape, 1) <= _iota(s.shape, 0) + off, s, -1e30)
    e = jnp.exp(s - lax.stop_gradient(jnp.max(s, axis=1, keepdims=True)))
    p = e / jnp.sum(e, axis=1, keepdims=True)
    return _dot(p.astype(BF16), v.astype(BF16))


ONE_BUFFER = pl.Buffered(1)


def _pcol(t, cb):
    return pl.BlockSpec((t, HD), lambda h, cb=cb: (0, cb + h), pipeline_mode=ONE_BUFFER)


def _smcol(t):
    return pl.BlockSpec((t, HD), lambda h: (0, SM), pipeline_mode=ONE_BUFFER)


def _head(t):
    return pl.BlockSpec((t, HD), lambda h: (0, h), pipeline_mode=ONE_BUFFER)


def _small(n):
    return pl.BlockSpec((n, HD), lambda h: (0, 0), pipeline_mode=ONE_BUFFER)


def _fox_fwd(p, fb, qg, kg, bq):
    t = p.shape[0]

    def body(fq, fk, fv, sm, fb_r, qg_r, kg_r, o, qn_s, cc_s):
        h = pl.program_id(0)
        qn, kn, ccol, crow = _fox_prep(fq[...], fk[...], sm[...], fb_r[...], qg_r[...], kg_r[...], h)
        qn_s[...] = qn
        cc_s[...] = ccol
        knb = kn.astype(BF16)
        vb = fv[...].astype(BF16)
        for i in range(t // bq):
            rows, ext = pl.ds(i * bq, bq), (i + 1) * bq
            o[rows, :] = _fox_block(qn_s[rows, :], knb[:ext], vb[:ext], cc_s[rows, :], crow[:, :ext], i * bq).astype(o.dtype)

    return pl.pallas_call(
        body, grid=(NF,), name="fox_fwd",
        in_specs=[_pcol(t, FQ), _pcol(t, FK), _pcol(t, FV), _smcol(t), _small(1), _small(1), _small(1)],
        out_specs=_head(t), out_shape=SDS((t, NF * HD), BF16),
        scratch_shapes=[pltpu.VMEM((t, HD), F32), pltpu.VMEM((t, 1), F32)],
        compiler_params=_cp("parallel"),
    )(p, p, p, p, fb, qg, kg)


def _fox_bwd(p, fb, qg, kg, dmix, bq, deps=()):
    t = p.shape[0]

    def body(*refs):
        fq, fk, fv, sm, fb_r, qg_r, kg_r, do = refs[:8]
        dfq, dfk, dfv, dsm, dfb, dqg, dkg, qn_s, cc_s, dqn_s, dcc_s, dkn_s, dv_s, dcr_s = refs[8 + len(deps):]
        h = pl.program_id(0)
        qn, kn, ccol, crow = _fox_prep(fq[...], fk[...], sm[...], fb_r[...], qg_r[...], kg_r[...], h)
        qn_s[...] = qn
        cc_s[...] = ccol
        v = fv[...]
        dkn_s[...] = jnp.zeros_like(dkn_s)
        dv_s[...] = jnp.zeros_like(dv_s)
        dcr_s[...] = jnp.zeros_like(dcr_s)

        for i in range(t // bq):
            rows, ext = pl.ds(i * bq, bq), (i + 1) * bq
            _, vjp = jax.vjp(lambda a, b, c, d, e, off=i * bq: _fox_block(a, b, c, d, e, off),
                             qn_s[rows, :], kn[:ext], v[:ext], cc_s[rows, :], crow[:, :ext])
            dq, dk, dv, dcc, dcr = vjp(do[rows, :])
            dqn_s[rows, :] = dq
            dcc_s[rows, :] = dcc
            dkn_s[:ext, :] += dk
            dv_s[:ext, :] += dv
            dcr_s[:, :ext] += dcr
        _, prep_vjp = jax.vjp(lambda a, b, c, d, e, f: _fox_prep(a, b, c, d, e, f, h),
                              fq[...], fk[...], sm[...], fb_r[...], qg_r[...], kg_r[...])
        g_fq, g_fk, g_sm, g_fb, g_qg, g_kg = prep_vjp((dqn_s[...], dkn_s[...], dcc_s[...], dcr_s[...]))
        dfq[...] = g_fq.astype(dfq.dtype)
        dfk[...] = g_fk.astype(dfk.dtype)
        dfv[...] = dv_s[...].astype(dfv.dtype)

        @pl.when(h == 0)
        def _():
            for r in (dsm, dfb, dqg, dkg):
                r[...] = jnp.zeros_like(r)

        dsm[...] += g_sm
        dfb[...] += g_fb
        dqg[...] += g_qg
        dkg[...] += g_kg

    head = _head(t)
    return pl.pallas_call(
        body, grid=(NF,), name="fox_bwd",
        in_specs=[_pcol(t, FQ), _pcol(t, FK), _pcol(t, FV), _smcol(t), _small(1), _small(1), _small(1), head]
        + [ANY_SPEC] * len(deps),
        out_specs=[head, head, head, _small(t), _small(1), _small(1), _small(1)],
        out_shape=[SDS((t, NF * HD), BF16)] * 3 + [SDS((t, HD), F32)] + [SDS((1, HD), F32)] * 3,
        scratch_shapes=[pltpu.VMEM((t, HD), F32), pltpu.VMEM((t, 1), F32), pltpu.VMEM((t, HD), F32),
                        pltpu.VMEM((t, 1), F32), pltpu.VMEM((t, HD), F32), pltpu.VMEM((t, HD), F32),
                        pltpu.VMEM((1, t), F32)],
        compiler_params=_cp("arbitrary"),
    )(p, p, p, p, fb, qg, kg, dmix, *deps)


def _mem_attn(mq, mk, mv, qg, kg):
    s = _dot(_rms(mq, qg).astype(BF16), _rms(mk, kg).astype(BF16), NT) * (HD ** -0.5)
    e = jnp.exp(s - lax.stop_gradient(jnp.max(s, axis=1, keepdims=True)))
    p = e / jnp.sum(e, axis=1, keepdims=True)
    return _dot(p.astype(BF16), mv.astype(BF16))


def _mem_fwd(p, mkv, qg, kg):
    t, ml = p.shape[0], mkv.shape[0]

    def body(mq, mk, mv, qg_r, kg_r, o):
        o[...] = _mem_attn(mq[...], mk[...], mv[...], qg_r[...], kg_r[...]).astype(o.dtype)

    return pl.pallas_call(
        body, grid=(NM,), name="mem_fwd",
        in_specs=[_pcol(t, MQ), pl.BlockSpec((ml, HD), lambda h: (0, h)), pl.BlockSpec((ml, HD), lambda h: (0, NM + h)),
                  _small(1), _small(1)],
        out_specs=pl.BlockSpec((t, HD), lambda h: (0, h)), out_shape=SDS((t, NM * HD), BF16),
        compiler_params=_cp("parallel"),
    )(p, mkv, mkv, qg, kg)


def _mem_bwd(p, mkv, qg, kg, dmix, deps=()):
    t, ml = p.shape[0], mkv.shape[0]

    def body(*refs):
        mq, mk, mv, qg_r, kg_r, do = refs[:6]
        dmq, dmk, dmv, dqg, dkg = refs[6 + len(deps):]
        _, vjp = jax.vjp(_mem_attn, mq[...], mk[...], mv[...], qg_r[...], kg_r[...])
        g_q, g_k, g_v, g_qg, g_kg = vjp(do[...])
        dmq[...] = g_q.astype(dmq.dtype)
        dmk[...] = g_k
        dmv[...] = g_v

        @pl.when(pl.program_id(0) == 0)
        def _():
            dqg[...] = jnp.zeros_like(dqg)
            dkg[...] = jnp.zeros_like(dkg)

        dqg[...] += g_qg
        dkg[...] += g_kg

    return pl.pallas_call(
        body, grid=(NM,), name="mem_bwd",
        in_specs=[_pcol(t, MQ), pl.BlockSpec((ml, HD), lambda h: (0, h)), pl.BlockSpec((ml, HD), lambda h: (0, NM + h)),
                  _small(1), _small(1), pl.BlockSpec((t, HD), lambda h: (0, NF + NG + h))] + [ANY_SPEC] * len(deps),
        out_specs=[pl.BlockSpec((t, HD), lambda h: (0, h)), pl.BlockSpec((ml, HD), lambda h: (0, h)),
                   pl.BlockSpec((ml, HD), lambda h: (0, h)), _small(1), _small(1)],
        out_shape=[SDS((t, NM * HD), BF16), SDS((ml, NM * HD), F32), SDS((ml, NM * HD), F32),
                   SDS((1, HD), F32), SDS((1, HD), F32)],
        compiler_params=_cp("arbitrary"),
    )(p, mkv, mkv, qg, kg, dmix, *deps)


def _shift_down(x, s):
    if s == 0:
        return x
    return jnp.where(_iota(x.shape, 0) >= s, pltpu.roll(x, s, 0), 0.0)


def _shift_up(x, s):
    if s == 0:
        return x
    n = x.shape[0]
    return jnp.where(_iota(x.shape, 0) < n - s, pltpu.roll(x, n - s, 0), 0.0)


@jax.custom_vjp
def _conv4(x, w0, w1, w2, w3):
    return w0 * _shift_down(x, 3) + w1 * _shift_down(x, 2) + w2 * _shift_down(x, 1) + w3 * x


def _conv4_fwd(x, w0, w1, w2, w3):
    return _conv4(x, w0, w1, w2, w3), (x, w0, w1, w2, w3)


def _conv4_bwd(res, dy):
    x, w0, w1, w2, w3 = res
    dx = w0 * _shift_up(dy, 3) + w1 * _shift_up(dy, 2) + w2 * _shift_up(dy, 1) + w3 * dy
    dws = tuple(jnp.sum(dy * _shift_down(x, 3 - k), axis=0, keepdims=True) for k in range(4))
    return (dx,) + dws


_conv4.defvjp(_conv4_fwd, _conv4_bwd)


def _gdn_prep(gq, gk, gv, sm, taps, alog, dtb, h):
    q, k, v = [_silu(_conv4(x, *taps[4 * j:4 * j + 4])) for j, x in enumerate((gq, gk, gv))]
    q = q * lax.rsqrt(jnp.sum(q * q, axis=-1, keepdims=True) + NORM_EPS) * (HD ** -0.5)
    k = k * lax.rsqrt(jnp.sum(k * k, axis=-1, keepdims=True) + NORM_EPS)
    g = _lane_pick(-jnp.exp(alog) * _softplus(sm + dtb), L_GA + h)
    beta = _lane_pick(_sigmoid(sm), L_GB + h)
    return q, k, v, g, beta


def _split(x, n):
    parts, rest = [], x
    for i in range(n):
        parts.append(rest.astype(BF16))
        if i + 1 < n:
            rest = rest - parts[-1].astype(F32)
    return parts


def _raw_dot(a, b, form):
    lead = a.ndim - 2
    ca, cb = {"nn": (1, 0), "nt": (1, 1), "tn": (0, 0)}[form]
    batch = ((0,), (0,)) if lead else ((), ())
    return lax.dot_general(a, b, (((ca + lead,), (cb + lead,)), batch), preferred_element_type=F32)


def _pdot_impl(a, b, form, mode):
    if mode == "1":
        return _raw_dot(a.astype(BF16), b.astype(BF16), form)
    if mode == "3":
        (ah, al), (bh, bl) = _split(a, 2), _split(b, 2)
        return _raw_dot(ah, bh, form) + (_raw_dot(al, bh, form) + _raw_dot(ah, bl, form))
    if mode == "xa":
        return sum(_raw_dot(a.astype(BF16), t, form) for t in reversed(_split(b, 3)))
    return sum(_raw_dot(t, b.astype(BF16), form) for t in reversed(_split(a, 3)))


@functools.partial(jax.custom_vjp, nondiff_argnums=(2, 3))
def _pdot(a, b, form, mode):
    return _pdot_impl(a, b, form, mode)


def _pdot_fwd(a, b, form, mode):
    return _pdot_impl(a, b, form, mode), (a, b)


def _pdot_bwd(form, mode, res, ct):
    a, b = res
    da_args, db_args = {"nn": ((ct, b, "nt"), (a, ct, "tn")), "nt": ((ct, b, "nn"), (ct, a, "tn")),
                        "tn": ((b, ct, "nt"), (a, ct, "nn"))}[form]

    def side(args, exact):
        if mode in ("1", "3"):
            return mode
        return "xa" if args[0] is exact else "xb"

    if mode == "xa":
        return jnp.zeros_like(a), _pdot_impl(*db_args, side(db_args, a))
    if mode == "xb":
        return _pdot_impl(*da_args, side(da_args, b)), jnp.zeros_like(b)
    return _pdot_impl(*da_args, mode), _pdot_impl(*db_args, mode)


_pdot.defvjp(_pdot_fwd, _pdot_bwd)

GDN_QK, GDN_INV, GDN_SCAN = "1", "3", "1"


@jax.custom_vjp
def _tri_inv(low):
    eye = (_iota((CHUNK, CHUNK), 0) == _iota((CHUNK, CHUNK), 1)).astype(F32)
    inv = eye - low
    pw = low
    for _ in range(5):
        pw = _pdot_impl(pw, pw, "nn", GDN_INV)
        inv = inv + _pdot_impl(inv, pw, "nn", GDN_INV)
    return inv


def _tri_inv_fwd(low):
    inv = _tri_inv(low)
    return inv, inv


def _tri_inv_bwd(inv, ct):
    return (-_pdot_impl(_pdot_impl(inv, ct, "tn", GDN_INV), inv, "nt", GDN_INV),)


_tri_inv.defvjp(_tri_inv_fwd, _tri_inv_bwd)


def _gdn_intra(q, k, v, g, beta):
    n = q.shape[0]
    r, c = _iota((CHUNK, CHUNK), 0), _iota((CHUNK, CHUNK), 1)
    tril, strict = r >= c, r > c
    trilf = jnp.broadcast_to(tril.astype(F32), (n, CHUNK, CHUNK))
    gcm = _pdot(trilf, jnp.broadcast_to(g, (n, CHUNK, CHUNK)), "nn", "xa")
    gcf = _pdot(trilf, jnp.broadcast_to(g, (n, CHUNK, HD)), "nn", "xa")
    lane0 = (_iota((1, 1, CHUNK), 2) == 0).astype(F32)
    gcr = _pdot(jnp.ones((n, CHUNK, CHUNK), F32), gcm * lane0, "nt", "xa")
    decay = jnp.where(tril, jnp.exp(jnp.where(tril, gcm - gcr, 0.0)), 0.0)
    egc = jnp.exp(gcf)
    kb = k * beta
    low = jnp.where(strict, _pdot(kb, k, "nt", GDN_QK) * decay, 0.0)
    inv = _tri_inv(low)
    u = _pdot(inv, v * beta, "nn", GDN_INV)
    w = _pdot(inv, kb * egc, "nn", GDN_INV)
    at = jnp.where(tril, _pdot(q, k, "nt", GDN_QK) * decay, 0.0)
    gl = jnp.sum(jnp.broadcast_to(g, (n, CHUNK, HD)), axis=1, keepdims=True)
    return u, w, q * egc, at, k * jnp.exp(gl - gcf), gl


def _gdn_step(s, u, w, qg, at, kd, gl):
    vn = u - _pdot(w, s, "nn", GDN_SCAN)
    o = _pdot(qg, s, "nn", GDN_SCAN) + _pdot(at, vn, "nn", GDN_SCAN)
    s2 = s * jnp.exp(gl) + _pdot(kd, vn, "tn", GDN_SCAN)
    return o, s2


def _gdn_scratch(nc):
    big = pltpu.VMEM((nc, CHUNK, HD), F32)
    return [big, big, big, pltpu.VMEM((nc, CHUNK, 1), F32), pltpu.VMEM((nc, CHUNK, 1), F32),
            big, big, big, pltpu.VMEM((nc, CHUNK, CHUNK), F32), big, pltpu.VMEM((nc, 1, HD), F32)]


def _gdn_in_specs(t):
    cw = lambda cb: pl.BlockSpec((4, HD), lambda h, cb=cb: (0, cb + h))
    return [_pcol(t, GQ), _pcol(t, GK), _pcol(t, GV), _smcol(t), cw(0), cw(NG), cw(2 * NG), _small(1), _small(1)]


def _taps(wq, wk, wv):
    return tuple(w[k:k + 1, :] for w in (wq, wk, wv) for k in range(4))


def _gdn_stage(vals, refs):
    nc = refs[0].shape[0]
    for v, r in zip(vals, refs):
        r[...] = v.reshape(nc, CHUNK, v.shape[-1])


def _gdn_intra_all(chunked, intra):
    nc = chunked[0].shape[0]
    grp_n = math.gcd(nc, GROUP)

    def grp(i, carry):
        sl = pl.ds(pl.multiple_of(i * grp_n, grp_n), grp_n)
        for r, val in zip(intra, _gdn_intra(*[c[sl] for c in chunked])):
            r[sl] = val
        return carry

    lax.fori_loop(0, nc // grp_n, grp, 0)


def _gdn_fwd(p, conv, alog, dtb):
    t = p.shape[0]
    nc = t // CHUNK

    def body(gq, gk, gv, sm, wq, wk, wv, al, db, o, *scr):
        h = pl.program_id(0)
        chunked, intra = scr[:5], scr[5:]
        _gdn_stage(_gdn_prep(gq[...], gk[...], gv[...], sm[...], _taps(wq, wk, wv), al[...], db[...], h), chunked)
        _gdn_intra_all(chunked, intra)

        def step(c, s):
            oc, s2 = _gdn_step(s, *[r[c] for r in intra])
            o[pl.ds(pl.multiple_of(c * CHUNK, CHUNK), CHUNK), :] = oc
            return s2

        lax.fori_loop(0, nc, step, jnp.zeros((HD, HD), F32))

    return pl.pallas_call(
        body, grid=(NG,), name="gdn_fwd", in_specs=_gdn_in_specs(t),
        out_specs=_head(t), out_shape=SDS((t, NG * HD), F32),
        scratch_shapes=_gdn_scratch(nc), compiler_params=_cp("parallel"),
    )(p, p, p, p, conv, conv, conv, alog, dtb)


def _gdn_bwd(p, conv, alog, dtb, do_raw):
    t = p.shape[0]
    nc = t // CHUNK

    def body(gq, gk, gv, sm, wq, wk, wv, al, db, do, dgq, dgk, dgv, dsm, dwq, dwk, dwv, dal, ddb, *scr):
        h = pl.program_id(0)
        chunked, intra, states = scr[:5], scr[5:11], scr[11]
        _gdn_stage(_gdn_prep(gq[...], gk[...], gv[...], sm[...], _taps(wq, wk, wv), al[...], db[...], h), chunked)
        _gdn_intra_all(chunked, intra)

        def fwd(c, s):
            states[c] = s
            return _gdn_step(s, *[r[c] for r in intra])[1]

        lax.fori_loop(0, nc, fwd, jnp.zeros((HD, HD), F32))

        def bwd(i, ds):
            c = nc - 1 - i
            _, vjp = jax.vjp(_gdn_step, states[c], *[r[c] for r in intra])
            grads = vjp((do[pl.ds(pl.multiple_of(c * CHUNK, CHUNK), CHUNK), :], ds))
            for r, gval in zip(intra, grads[1:]):
                r[c] = gval
            return grads[0]

        lax.fori_loop(0, nc, bwd, jnp.zeros((HD, HD), F32))

        grp_n = math.gcd(nc, GROUP)

        def grp(i, carry):
            sl = pl.ds(pl.multiple_of(i * grp_n, grp_n), grp_n)
            _, vjp = jax.vjp(_gdn_intra, *[r[sl] for r in chunked])
            for r, gval in zip(chunked, vjp(tuple(r[sl] for r in intra))):
                r[sl] = gval
            return carry

        lax.fori_loop(0, nc // grp_n, grp, 0)
        _, prep_vjp = jax.vjp(
            lambda *a: _gdn_prep(*a, h), gq[...], gk[...], gv[...], sm[...], _taps(wq, wk, wv), al[...], db[...])
        grads = prep_vjp(tuple(r[...].reshape(t, r.shape[-1]) for r in chunked))
        for r, gval in zip((dgq, dgk, dgv), grads[:3]):
            r[...] = gval.astype(r.dtype)
        for j, r in enumerate((dwq, dwk, dwv)):
            for k in range(4):
                r[k:k + 1, :] = grads[4][4 * j + k]

        @pl.when(h == 0)
        def _():
            for r in (dsm, dal, ddb):
                r[...] = jnp.zeros_like(r)

        dsm[...] += grads[3]
        dal[...] += grads[5]
        ddb[...] += grads[6]

    head = _head(t)
    taps = pl.BlockSpec((4, HD), lambda h: (0, h))
    return pl.pallas_call(
        body, grid=(NG,), name="gdn_bwd", in_specs=_gdn_in_specs(t) + [head],
        out_specs=[head, head, head, _small(t), taps, taps, taps, _small(1), _small(1)],
        out_shape=[SDS((t, NG * HD), BF16)] * 3 + [SDS((t, HD), F32)] + [SDS((4, NG * HD), F32)] * 3 + [SDS((1, HD), F32)] * 2,
        scratch_shapes=_gdn_scratch(nc) + [pltpu.VMEM((nc, HD, HD), F32)], compiler_params=_cp("arbitrary"),
    )(p, p, p, p, conv, conv, conv, alog, dtb, do_raw)


def _gdn_post(o, z, gain):
    return (jnp.concatenate(
        [_rms(o[:, h * HD:(h + 1) * HD], gain) * _silu(z[:, h * HD:(h + 1) * HD]) for h in range(NG)], axis=1),)


def _place():
    return lax.axis_index("x"), lax.axis_index("y"), lax.axis_index("c")


def _all_gather(name, shard):
    def body(x_ref, out_ref, send_sems, recv_sems, local_sem):
        x, y, c = _place()
        me, sibling = (x, y, c), (x, y, 1 - c)
        chips = [(1 - x, y), (x, 1 - y), (1 - x, 1 - y)]

        def blk(px, py, pc):
            return out_ref.at[4 * px + 2 * py + pc]

        def copy(k, block, to, src=None):
            return pltpu.make_async_remote_copy(
                src_ref=blk(*block) if src is None else src, dst_ref=blk(*block),
                send_sem=send_sems.at[k], recv_sem=recv_sems.at[k], device_id=to, device_id_type=MESH)

        mine = pltpu.make_async_copy(x_ref, blk(*me), local_sem)
        mine.start()
        first = [copy(0, me, sibling, src=x_ref)]
        first += [copy(1 + j, me, (*chip, c), src=x_ref) for j, chip in enumerate(chips)]
        for cp in first:
            cp.start()
        passed = [copy(4 + j, (*chip, c), sibling) for j, chip in enumerate(chips)]
        for j, chip in enumerate(chips):
            copy(1 + j, (*chip, c), me).wait_recv()
            passed[j].start()
        copy(0, sibling, me).wait_recv()
        for j, chip in enumerate(chips):
            copy(4 + j, (*chip, 1 - c), me).wait_recv()
        for cp in first + passed:
            cp.wait_send()
        mine.wait()

    return pl.pallas_call(
        body, name=name, out_shape=SDS((N_DEV,) + shard.shape, shard.dtype),
        in_specs=[pl.BlockSpec(memory_space=pltpu.HBM)], out_specs=pl.BlockSpec(memory_space=pltpu.HBM),
        scratch_shapes=[pltpu.SemaphoreType.DMA((7,)), pltpu.SemaphoreType.DMA((7,)), pltpu.SemaphoreType.DMA],
    )(shard)


def _scatter_exchange(name, full):
    def body(g_ref, out_ref, send_sems, recv_sems, local_sem):
        x, y, c = _place()
        me = 4 * x + 2 * y + c
        mine = pltpu.make_async_copy(g_ref.at[me], out_ref.at[me], local_sem)
        mine.start()
        sends, recvs = [], []
        for k in range(1, N_DEV):
            px = 1 - x if k & 4 else x
            py = 1 - y if k & 2 else y
            pc = 1 - c if k & 1 else c
            peer = 4 * px + 2 * py + pc
            sends.append(pltpu.make_async_remote_copy(
                src_ref=g_ref.at[peer], dst_ref=out_ref.at[me], send_sem=send_sems.at[k - 1],
                recv_sem=recv_sems.at[k - 1], device_id=(px, py, pc), device_id_type=MESH))
            recvs.append(pltpu.make_async_remote_copy(
                src_ref=g_ref.at[me], dst_ref=out_ref.at[peer], send_sem=send_sems.at[k - 1],
                recv_sem=recv_sems.at[k - 1], device_id=(px, py, pc), device_id_type=MESH))
        for cp in sends:
            cp.start()
        for cp in recvs:
            cp.wait_recv()
        for cp in sends:
            cp.wait_send()
        mine.wait()

    return pl.pallas_call(
        body, name=name, out_shape=SDS(full.shape, full.dtype),
        in_specs=[pl.BlockSpec(memory_space=pltpu.HBM)], out_specs=pl.BlockSpec(memory_space=pltpu.HBM),
        scratch_shapes=[pltpu.SemaphoreType.DMA((7,)), pltpu.SemaphoreType.DMA((7,)), pltpu.SemaphoreType.DMA],
    )(full)


def _sum_blocks(name, parts):
    _, r, c = parts.shape
    tr = 64 if r % 64 == 0 else r

    def body(x, o):
        acc = x[0].astype(F32)
        for d in range(1, N_DEV):
            acc = acc + x[d].astype(F32)
        o[...] = acc

    return pl.pallas_call(
        body, grid=(r // tr,), name=name, in_specs=[pl.BlockSpec((N_DEV, tr, c), lambda i: (0, i, 0))],
        out_specs=pl.BlockSpec((tr, c), lambda i: (i, 0)), out_shape=SDS((r, c), F32), compiler_params=_cp("parallel"),
    )(parts)


def _reduce_scatter(name, full):
    return _sum_blocks(name + "_sum", _scatter_exchange(name, full))


def _all_reduce_small(name, x, reduce):
    m_per, n = x.shape

    def body(x_ref, out_ref, send_sems, recv_sems, local_sem):
        px, py, pc = _place()
        me, sibling = (px, py, pc), (px, py, 1 - pc)
        chips = [(1 - px, py), (px, 1 - py), (1 - px, 1 - py)]
        buf = out_ref

        def rows(qx, qy, qc):
            return buf.at[pl.ds((4 * qx + 2 * qy + qc) * m_per, m_per), :]

        def copy(k, block, to, src=None):
            return pltpu.make_async_remote_copy(
                src_ref=rows(*block) if src is None else src, dst_ref=rows(*block),
                send_sem=send_sems.at[k], recv_sem=recv_sems.at[k], device_id=to, device_id_type=MESH)

        mine = pltpu.make_async_copy(x_ref, rows(*me), local_sem)
        mine.start()
        first = [copy(0, me, sibling, src=x_ref)]
        first += [copy(1 + j, me, (*chip, pc), src=x_ref) for j, chip in enumerate(chips)]
        for cp in first:
            cp.start()
        passed = [copy(4 + j, (*chip, pc), sibling) for j, chip in enumerate(chips)]
        for j, chip in enumerate(chips):
            copy(1 + j, (*chip, pc), me).wait_recv()
            passed[j].start()
        copy(0, sibling, me).wait_recv()
        for j, chip in enumerate(chips):
            copy(4 + j, (*chip, 1 - pc), me).wait_recv()
        for cp in first + passed:
            cp.wait_send()
        mine.wait()

    gathered = pl.pallas_call(
        body, name=name, out_shape=SDS((N_DEV * m_per, n), x.dtype),
        in_specs=[pl.BlockSpec(memory_space=pltpu.VMEM)], out_specs=pl.BlockSpec(memory_space=pltpu.VMEM),
        scratch_shapes=[pltpu.SemaphoreType.DMA((7,)), pltpu.SemaphoreType.DMA((7,)), pltpu.SemaphoreType.DMA],
    )(x)
    if not reduce:
        return gathered
    return _sum_blocks(name + "_sum", gathered.reshape(N_DEV, m_per, n))


HBM_SPEC = pl.BlockSpec(memory_space=pltpu.HBM)
SEM_SPEC = pl.BlockSpec(memory_space=pltpu.SEMAPHORE)
EFFECT = pltpu.SideEffectType.DATAFLOW_SIDE_EFFECTING


def _copies_start(name, bufs, n_remote, n_local, build, deps):
    nb, nd = len(bufs), len(deps)
    sem_shapes = [pltpu.SemaphoreType.DMA((n_remote,)), pltpu.SemaphoreType.DMA((n_remote,))]
    if n_local:
        sem_shapes.append(pltpu.SemaphoreType.DMA((n_local,)))
    ns = len(sem_shapes)

    def body(*refs):
        sems = refs[nb + nd:nb + nd + ns]
        remote, local = build(refs[:nb], *sems, *([None] * (3 - ns)))
        for cp in local + remote:
            cp.start()
        refs[-1][...] = jnp.zeros((8, HD), F32)

    outs = pl.pallas_call(
        body, name=name,
        out_shape=(*sem_shapes, *[pltpu.HBM(b.shape, b.dtype) for b in bufs], SDS((8, HD), F32)),
        in_specs=[HBM_SPEC] * nb + [ANY_SPEC] * nd,
        out_specs=(*[SEM_SPEC] * ns, *[HBM_SPEC] * nb, pl.BlockSpec(memory_space=pltpu.VMEM)),
        input_output_aliases={i: ns + i for i in range(nb)},
        compiler_params=pltpu.CompilerParams(has_side_effects=EFFECT),
    )(*[pltpu.with_memory_space_constraint(b, pltpu.HBM) for b in bufs], *deps)
    return list(outs[:ns]), list(outs[ns:ns + nb]), outs[-1]


def _copies_wait(name, bufs, sems, build, after):
    nb, ns = len(bufs), len(sems)

    def body(*refs):
        remote, local = build(refs[:nb], *refs[nb:nb + ns], *([None] * (3 - ns)))
        for cp in local:
            cp.wait()
        for cp in remote:
            cp.wait_send()
            cp.wait_recv()

    outs = pl.pallas_call(
        body, name=name, out_shape=tuple(pltpu.HBM(b.shape, b.dtype) for b in bufs),
        in_specs=[HBM_SPEC] * nb + [SEM_SPEC] * ns + [ANY_SPEC] * len(after), out_specs=tuple([HBM_SPEC] * nb),
        input_output_aliases={i: i for i in range(nb)},
        compiler_params=pltpu.CompilerParams(has_side_effects=EFFECT),
    )(*bufs, *sems, *after)
    return list(outs)


def _remote(src, dst, send, recv, k, to):
    return pltpu.make_async_remote_copy(src_ref=src, dst_ref=dst, send_sem=send.at[k], recv_sem=recv.at[k],
                                        device_id=to, device_id_type=MESH)


class _Gather:
    def __init__(self, name, shards, deps):
        self.name, self.n = name, len(shards)
        lands = [lax.empty((N_DEV,) + s.shape, s.dtype) for s in shards]
        self.sems, bufs, self.token = _copies_start(name + "_s1", list(shards) + lands, 4 * self.n, self.n, self._stage1, deps)
        self.shards, self.lands = bufs[:self.n], bufs[self.n:]

    def _stage1(self, refs, send, recv, loc):
        x, y, c = _place()
        me = 4 * x + 2 * y + c
        targets = [(x, y, 1 - c), (1 - x, y, c), (x, 1 - y, c), (1 - x, 1 - y, c)]
        remote, local = [], []
        for i in range(self.n):
            src, land = refs[i], refs[self.n + i]
            local.append(pltpu.make_async_copy(src, land.at[me], loc.at[i]))
            remote += [_remote(src, land.at[me], send, recv, 4 * i + k, to) for k, to in enumerate(targets)]
        return remote, local

    def _stage2(self, refs, send, recv, loc):
        x, y, c = _place()
        remote = []
        for i in range(self.n):
            for j, (cx, cy) in enumerate([(1 - x, y), (x, 1 - y), (1 - x, 1 - y)]):
                blk = refs[i].at[4 * cx + 2 * cy + c]
                remote.append(_remote(blk, blk, send, recv, 3 * i + j, (x, y, 1 - c)))
        return remote, []

    def mid(self, after):
        bufs = _copies_wait(self.name + "_w1", self.shards + self.lands, self.sems, self._stage1, after)
        self.sems, self.lands, self.token = _copies_start(self.name + "_s2", bufs[self.n:], 3 * self.n, 0, self._stage2, ())

    def end(self, after):
        return _copies_wait(self.name + "_w2", self.lands, self.sems, self._stage2, after)


def _rows_tile(r, row_bytes, target=1 << 20):
    tr = r
    while tr % 32 == 0 and tr * row_bytes > target:
        tr //= 2
    return tr


def _pair_add(name, g, got, c):
    _, r, cols = g.shape
    tr = _rows_tile(r, cols * 2)

    def body(s, a, b, o):
        o[...] = (a[...].astype(F32) + b[...].astype(F32)).astype(o.dtype)

    return pl.pallas_call(
        body, name=name, out_shape=SDS((4, r, cols), g.dtype),
        grid_spec=pltpu.PrefetchScalarGridSpec(
            num_scalar_prefetch=1, grid=(4, r // tr),
            in_specs=[pl.BlockSpec((None, tr, cols), lambda j, i, s: (2 * j + s[0], i, 0)),
                      pl.BlockSpec((None, tr, cols), lambda j, i, s: (j, i, 0))],
            out_specs=pl.BlockSpec((None, tr, cols), lambda j, i, s: (j, i, 0))),
        compiler_params=_cp("parallel", "parallel"),
    )(c.reshape(1), g, got)


def _quad_sum(name, part, got, chip):
    _, r, cols = part.shape
    tr = _rows_tile(r, cols * 4)

    def body(s, a, b1, b2, b3, o):
        o[...] = ((a[...].astype(F32) + b1[...].astype(F32)) + b2[...].astype(F32)) + b3[...].astype(F32)

    blk = lambda k: pl.BlockSpec((None, tr, cols), lambda i, s, k=k: (jnp.bitwise_xor(s[0], k), i, 0))
    return pl.pallas_call(
        body, name=name, out_shape=SDS((r, cols), F32),
        grid_spec=pltpu.PrefetchScalarGridSpec(
            num_scalar_prefetch=1, grid=(r // tr,), in_specs=[blk(0), blk(1), blk(2), blk(3)],
            out_specs=pl.BlockSpec((tr, cols), lambda i, s: (i, 0))),
        compiler_params=_cp("parallel"),
    )(chip.reshape(1), part, got, got, got)


class _Scatter:
    def __init__(self, name, grads, deps):
        self.name, self.n = name, len(grads)
        got = [lax.empty((4,) + g.shape[1:], g.dtype) for g in grads]
        self.sems, bufs, self.token = _copies_start(name + "_s1", list(grads) + got, 4 * self.n, 0, self._stage1, deps)
        self.grads, self.got = bufs[:self.n], bufs[self.n:]

    def _stage1(self, refs, send, recv, loc):
        x, y, c = _place()
        remote = []
        for i in range(self.n):
            remote += [_remote(refs[i].at[2 * j + 1 - c], refs[self.n + i].at[j], send, recv, 4 * i + j, (x, y, 1 - c))
                       for j in range(4)]
        return remote, []

    def _stage2(self, refs, send, recv, loc):
        x, y, c = _place()
        remote = []
        for i in range(self.n):
            for k in (1, 2, 3):
                tx = 1 - x if k & 2 else x
                ty = 1 - y if k & 1 else y
                remote.append(_remote(refs[i].at[2 * tx + ty], refs[self.n + i].at[2 * x + y], send, recv,
                                      3 * i + k - 1, (tx, ty, c)))
        return remote, []

    def mid(self, after):
        bufs = _copies_wait(self.name + "_w1", self.grads + self.got, self.sems, self._stage1, after)
        c = lax.axis_index("c").astype(jnp.int32)
        parts = [_pair_add(f"{self.name}_add{i}", bufs[i], bufs[self.n + i], c) for i in range(self.n)]
        got = [lax.empty(p.shape, p.dtype) for p in parts]
        self.sems, bufs, self.token = _copies_start(self.name + "_s2", parts + got, 3 * self.n, 0, self._stage2, ())
        self.parts, self.got = bufs[:self.n], bufs[self.n:]

    def end(self, after):
        bufs = _copies_wait(self.name + "_w2", self.parts + self.got, self.sems, self._stage2, after)
        chip = (2 * lax.axis_index("x") + lax.axis_index("y")).astype(jnp.int32)
        return [_quad_sum(f"{self.name}_sum{i}", bufs[i], bufs[self.n + i], chip) for i in range(self.n)]


def _adamw(w, g, m, v):
    m = ADAM_B1 * m + (1.0 - ADAM_B1) * g
    v = ADAM_B2 * v + (1.0 - ADAM_B2) * (g * g)
    m_hat = m / (1.0 - ADAM_B1 ** ADAM_STEP)
    v_hat = v / (1.0 - ADAM_B2 ** ADAM_STEP)
    return -ADAM_LR * (m_hat / (jnp.sqrt(v_hat) + ADAM_EPS) + ADAM_WD * w), m, v


def _adamw_call(name, w, g, m, v):
    r, c = w.shape
    tm = 64 if r % 64 == 0 else r
    return _rowwise(name, _adamw, [w, g, m, v], [], [(c, F32)] * 3, tm)


_IN_COLS = 5906


def _perm_in(w):
    pad = jnp.zeros((w.shape[0], PC - _IN_COLS), w.dtype)
    return jnp.concatenate([w[:, 2310:4614], w[:, 4614:5382], w[:, :2304], w[:, 5394:5906], w[:, 2304:2310],
                            w[:, 5382:5394], pad], axis=1)


def _unperm_in(ga, gb):
    return jnp.concatenate([gb[:, :2304], gb[:, 2816:2822], ga[:, :2304], ga[:, 2304:3072], gb[:, 2822:2834],
                            gb[:, 2304:2816]], axis=1)


def _lanes(v, at):
    return jnp.pad(v, ((0, 0), (at, HD - at - v.shape[1])))


_PACK = ("norm_mix", "mem_norm", "norm_ffn", "gdn_conv", "fox_q_norm", "fox_k_norm", "gdn_out_norm", "mem_q_norm",
         "mem_k_norm", "fox_f_bias", "gdn_a_log", "gdn_dt_bias", "loss")


def _pack(vals):
    parts = [vals[n].reshape(-1, HD) for n in _PACK]
    used = sum(p.shape[0] for p in parts)
    buf = jnp.concatenate(parts + [jnp.zeros((-used % 8, HD), F32)], axis=0)
    return buf, [(n, p.shape[0]) for n, p in zip(_PACK, parts)]


def _unpack(buf, layout):
    out, at = {}, 0
    for n, rows in layout:
        out[n] = buf[at:at + rows]
        at += rows
    return out


def kernel(x, mem, norm_mix, w_in, fox_f_bias, fox_q_norm, fox_k_norm, gdn_conv, gdn_a_log, gdn_dt_bias, gdn_out_norm, mem_norm, w_mem_kv, mem_q_norm, mem_k_norm, w_out, norm_ffn, w_gate_up, w_down, loss_target, m_norm_mix, m_w_in, m_fox_f_bias, m_fox_q_norm, m_fox_k_norm, m_gdn_conv, m_gdn_a_log, m_gdn_dt_bias, m_gdn_out_norm, m_mem_norm, m_w_mem_kv, m_mem_q_norm, m_mem_k_norm, m_w_out, m_norm_ffn, m_w_gate_up, m_w_down, v_norm_mix, v_w_in, v_fox_f_bias, v_fox_q_norm, v_fox_k_norm, v_gdn_conv, v_gdn_a_log, v_gdn_dt_bias, v_gdn_out_norm, v_mem_norm, v_w_mem_kv, v_mem_q_norm, v_mem_k_norm, v_w_out, v_norm_ffn, v_w_gate_up, v_w_down):
    args = dict(locals())
    d = x.shape[2]
    me = 4 * lax.axis_index("x") + 2 * lax.axis_index("y") + lax.axis_index("c")

    w_in_all = _all_gather("ag_w_in", _perm_in(w_in[0]).astype(BF16)).reshape(d, PC)
    w_kv_all = _all_gather("ag_w_kv", w_mem_kv[0].astype(BF16)).reshape(d, 2 * NM * HD)
    cshard = gdn_conv[0].shape[1]
    conv_pad = jnp.pad(gdn_conv[0], ((0, 4), (0, 3 * HD - cshard)))
    conv_all = _all_reduce_small("ag_conv", conv_pad, False).reshape(N_DEV, 8, 3 * HD)[:, :4, :cshard]
    conv_all = conv_all.transpose(1, 0, 2).reshape(4, N_DEV * cshard)
    comm = _StepComm([w_out[0].astype(BF16), w_gate_up[0].astype(BF16), w_down[0].astype(BF16)],
                     [w_in_all, w_kv_all, conv_all])

    grad_x, loss_local, small_grads = _local_step(
        x[0], mem[0], loss_target[0], norm_mix, fox_f_bias, fox_q_norm, fox_k_norm, gdn_a_log, gdn_dt_bias,
        gdn_out_norm, mem_norm, mem_q_norm, mem_k_norm, norm_ffn, w_in_all, w_kv_all, conv_all, comm)

    red = comm.finish([grad_x])
    grads = {"w_down": red["ffn"][0], "w_gate_up": red["ffn"][1], "w_out": red["a"][1], "w_mem_kv": red["b"][1],
             "w_in": _unperm_in(red["a"][0], red["b"][0])}
    small_grads["loss"] = jnp.broadcast_to(loss_local, (1, HD))
    packed, layout = _pack(small_grads)
    small = _unpack(_all_reduce_small("ar_small", packed, True), layout)
    loss = small["loss"][0, 0]
    six = {"fox_f_bias": L_FF, "gdn_a_log": L_GA, "gdn_dt_bias": L_GA}
    for n, rows_n in layout[:-1]:
        gsm = small[n]
        if n == "gdn_conv":
            gsm = lax.dynamic_slice(gsm.reshape(4, N_DEV * cshard), (0, me * cshard), (4, cshard))[None]
        elif n in six:
            gsm = gsm[:, six[n]:six[n] + 6]
        else:
            gsm = gsm.reshape(1, rows_n * HD)
        grads[n] = gsm

    names = ['norm_mix', 'w_in', 'fox_f_bias', 'fox_q_norm', 'fox_k_norm', 'gdn_conv', 'gdn_a_log', 'gdn_dt_bias',
             'gdn_out_norm', 'mem_norm', 'w_mem_kv', 'mem_q_norm', 'mem_k_norm', 'w_out', 'norm_ffn', 'w_gate_up', 'w_down']
    big = ("w_in", "w_mem_kv", "w_out", "w_gate_up", "w_down")
    delta, new_m, new_v = {}, {}, {}
    for n in big:
        delta[n], new_m[n], new_v[n] = [a[None] for a in _adamw_call(
            "adamw_" + n, args[n][0], grads[n], args["m_" + n][0], args["v_" + n][0])]
        grads[n] = grads[n][None]

    def flat(a):
        a = a.reshape(1, -1)
        return jnp.pad(a, ((0, 0), (0, -a.shape[1] % HD))).reshape(-1, HD)

    smalls = [n for n in names if n not in big]
    pk = lambda pre: jnp.concatenate([flat(grads[n] if pre == "g" else args[pre + n]) for n in smalls], axis=0)
    cat = [pk(""), pk("g"), pk("m_"), pk("v_")]
    padr = -cat[0].shape[0] % 8
    cat = [jnp.pad(a, ((0, padr), (0, 0))) for a in cat]
    res = _adamw_call("adamw_small", *cat)
    at = 0
    for n in smalls:
        shape = args[n].shape
        size = math.prod(shape)
        nrow = -(-size // HD)
        for dst, src in zip((delta, new_m, new_v), res):
            dst[n] = src[at:at + nrow].reshape(-1)[:size].reshape(shape)
        at += nrow

    return (loss, grad_x[None], *[grads[n] for n in names], *[delta[n] for n in names],
            *[new_m[n] for n in names], *[new_v[n] for n in names])


class _StepComm:
    def __init__(self, late_shards, after):
        self.gather = _Gather("ag_late", late_shards, after)
        self.scatters = {}

    def start_deps(self):
        return [self.gather.token]

    def after_mixers(self, after):
        self.gather.mid(after)
        return [self.gather.token]

    def late_weights(self, after):
        w_out_all, wgu, w_down_all = self.gather.end(after)
        return w_out_all.reshape(-1, w_out_all.shape[-1]), wgu, w_down_all.reshape(-1, w_down_all.shape[-1])

    def send(self, tag, grads):
        blocks = [g if g.ndim == 3 else g.reshape(N_DEV, g.shape[0] // N_DEV, g.shape[1]) for g in grads]
        self.scatters[tag] = _Scatter("rs_" + tag, blocks, ())
        return [self.scatters[tag].token]

    def mid(self, tag, after):
        self.scatters[tag].mid(after)
        return [self.scatters[tag].token]

    def finish(self, after):
        return {tag: sc.end(after) for tag, sc in self.scatters.items()}


def _local_step(xs, ms, tgt, norm_mix, fox_f_bias, fox_q_norm, fox_k_norm, gdn_a_log, gdn_dt_bias, gdn_out_norm,
                mem_norm, mem_q_norm, mem_k_norm, norm_ffn, w_in_all, w_kv_all, conv_all, comm):
    t, d = xs.shape
    bq = min(t, 256)
    fb, alog, dtb = _lanes(fox_f_bias, L_FF), _lanes(gdn_a_log, L_GA), _lanes(gdn_dt_bias, L_GA)

    rms1 = lambda a, g: (_rms(a, g),)
    (u,) = _rowwise("norm_mix", rms1, [xs], [norm_mix], [(d, BF16)], min(t, 256), deps=comm.start_deps())
    p = _matmul("proj_in", u, w_in_all, NN, F32, 1024, 768)
    o_fox = _fox_fwd(p, fb, fox_q_norm, fox_k_norm, bq)
    o_gdn_raw = _gdn_fwd(p, conv_all, alog, dtb)
    zrow = (p, NG * HD, GZ * HD // (NG * HD))
    (o_gdn,) = _rowwise("gdn_post", _gdn_post, [o_gdn_raw, zrow], [gdn_out_norm], [(NG * HD, BF16)], min(t, 256))
    deps = comm.after_mixers([o_fox, o_gdn])
    (mem_n,) = _rowwise("norm_mem", rms1, [ms], [mem_norm], [(d, BF16)], ms.shape[0], deps=deps)
    mkv = _matmul("proj_mem", mem_n, w_kv_all, NN, F32, 256, 512)
    o_mem = _mem_fwd(p, mkv, mem_q_norm, mem_k_norm)
    w_out_all, wgu, w_down_all = comm.late_weights([o_mem])
    ffw = wgu.shape[2]
    mix = jnp.concatenate([o_fox, o_gdn, o_mem], axis=1)
    h1 = _matmul("proj_out", mix, w_out_all, NN, F32, 1024, 512, residual=xs)
    (h1n,) = _rowwise("norm_ffn", rms1, [h1], [norm_ffn], [(d, BF16)], min(t, 256))
    gu, act = _ffn_up(h1n, wgu.reshape(2, 4, d, ffw))
    dy, dyb, lsum = _ffn_down_loss(act, w_down_all, h1, tgt)
    loss_local = (0.5 / d) * jnp.sum(lsum[::8, ::HD])

    dgu = _ffn_down_bwd(dyb, w_down_all.reshape(4, ffw, d), gu).reshape(8, t, ffw)
    g_w_down = _matmul("grad_w_down", act, dyb, TN, BF16, 512, 512)
    dh1n = _ffn_up_bwd_x(dgu, wgu)
    g_w_gu = _ffn_up_bwd_w(h1n, dgu)
    deps = comm.send("ffn", [g_w_down, g_w_gu])
    rms2 = lambda a, g: (_rms(a, g), a)
    dh1, g_norm_ffn = _rowwise_vjp("norm_ffn_bwd", rms2, [h1], [norm_ffn], [dh1n, dy], [F32], min(t, 256), deps=deps)
    dh1b = dh1.astype(BF16)

    dmix = _matmul("proj_out_bwd_x", dh1b, w_out_all, NT, F32, 1024, 512)
    g_w_out = _matmul("grad_w_out", mix, dh1b, TN, BF16, 512, 512)
    deps = comm.mid("ffn", [dmix, g_w_out])
    do_raw, dgz, g_gon = _rowwise_vjp("gdn_post_bwd", _gdn_post, [o_gdn_raw, zrow], [gdn_out_norm],
                                      [(dmix, NG * HD, 1)], [F32, BF16], min(t, 256), deps=deps)
    dgq, dgk, dgv, dsm_gdn, dwq, dwk, dwv, g_alog, g_dtb = _gdn_bwd(p, conv_all, alog, dtb, do_raw)
    dp_a = jnp.concatenate([dgq, dgk, dgv, dgz], axis=1)
    g_w_in_a = _matmul("grad_w_in_a", u, dp_a, TN, BF16, 512, 768)
    deps = comm.send("a", [g_w_in_a, g_w_out])
    dmq, dmk, dmv, g_mqn, g_mkn = _mem_bwd(p, mkv, mem_q_norm, mem_k_norm, dmix, deps=deps)
    dmkv = jnp.concatenate([dmk, dmv], axis=1).astype(BF16)
    dmem_n = _matmul("proj_mem_bwd_x", dmkv, w_kv_all, NT, F32, 256, 512)
    g_w_kv = _matmul("grad_w_kv", mem_n, dmkv, TN, BF16, 512, 512)
    g_mem_norm = _rowwise_vjp("norm_mem_bwd", rms1, [ms], [mem_norm], [dmem_n], [], ms.shape[0])[0]
    deps = comm.mid("a", [g_mem_norm, g_w_kv])
    dfq, dfk, dfv, dsm_fox, g_fb, g_fqn, g_fkn = _fox_bwd(p, fb, fox_q_norm, fox_k_norm, dmix, bq, deps=deps)
    dp_b = jnp.concatenate([dfq, dfk, dfv, dmq, (dsm_fox + dsm_gdn).astype(BF16), jnp.zeros((t, HD), BF16)], axis=1)
    g_w_in_b = _matmul("grad_w_in_b", u, dp_b, TN, BF16, 512, 768)
    deps = comm.send("b", [g_w_in_b, g_w_kv])
    du_a = _matmul("proj_in_bwd_a", dp_a, w_in_all, NT, F32, 512, 512, kidx=0, deps=deps)
    deps = comm.mid("b", [du_a])
    du = _matmul("proj_in_bwd_b", dp_b, w_in_all, NT, F32, 512, 512, kidx=1, residual=du_a, deps=deps)
    grad_x, g_norm_mix = _rowwise_vjp("norm_mix_bwd", rms2, [xs], [norm_mix], [du, dh1], [F32], min(t, 256))

    small_grads = {
        "norm_mix": g_norm_mix, "mem_norm": g_mem_norm, "norm_ffn": g_norm_ffn,
        "gdn_conv": jnp.concatenate([dwq, dwk, dwv], axis=1),
        "fox_q_norm": g_fqn, "fox_k_norm": g_fkn, "gdn_out_norm": g_gon, "mem_q_norm": g_mqn, "mem_k_norm": g_mkn,
        "fox_f_bias": g_fb, "gdn_a_log": g_alog, "gdn_dt_bias": g_dtb}
    return grad_x, loss_local, small_grads
```

```python
import functools
import math

import jax
import jax.numpy as jnp
from jax import lax
from jax.experimental import pallas as pl
from jax.experimental.pallas import tpu as pltpu

F32 = jnp.float32
BF16 = jnp.bfloat16
HI = lax.Precision.HIGHEST
SDS = jax.ShapeDtypeStruct

N_DEV = 8
HD = 128
NF, NG, NM = 6, 6, 4
CHUNK = 64
GROUP = 4
NORM_EPS = 1e-6
GQ, GK, GV, GZ, FQ, FK, FV, MQ, SM, NPB = 0, 6, 12, 18, 24, 30, 36, 42, 46, 48
PC = NPB * HD
HALF = PC // 2
L_FF, L_GA, L_GB = 0, 6, 12
VMEM_LIMIT = 56 * 1024 * 1024

ADAM_LR, ADAM_B1, ADAM_B2, ADAM_EPS, ADAM_WD, ADAM_STEP = 0.001, 0.9, 0.999, 1e-08, 0.01, 10

NN = (((1,), (0,)), ((), ()))
NT = (((1,), (1,)), ((), ()))
TN = (((0,), (0,)), ((), ()))
MESH = pl.DeviceIdType.MESH


def _cp(*sem):
    return pltpu.CompilerParams(dimension_semantics=tuple(sem) if sem else None, vmem_limit_bytes=VMEM_LIMIT)


def _dot(a, b, dims=NN):
    return lax.dot_general(a, b, dims, preferred_element_type=F32)


def _bdot(a, b):
    return _dot(a.astype(BF16), b.astype(BF16))


def _iota(shape, axis):
    return lax.broadcasted_iota(jnp.int32, shape, axis)


def _rms(x, gain):
    return x * lax.rsqrt(jnp.mean(x * x, axis=-1, keepdims=True) + NORM_EPS) * gain


def _sigmoid(x):
    z = jnp.exp(-jnp.abs(x))
    return jnp.where(x >= 0, 1.0 / (1.0 + z), z / (1.0 + z))


def _silu(x):
    return x * _sigmoid(x)


def _softplus(x):
    return jnp.maximum(x, 0.0) + jnp.log(1.0 + jnp.exp(-jnp.abs(x)))


def _lane_pick(x, lane):
    oh = (_iota((1, x.shape[-1]), 1) == lane).astype(F32)
    return jnp.sum(x * oh, axis=-1, keepdims=True)


def _cumsum_rows(x):
    tril = (_iota((HD, HD), 0) >= _iota((HD, HD), 1)).astype(F32)
    carry = jnp.zeros((1, x.shape[1]), F32)
    outs = []
    for b in range(x.shape[0] // HD):
        blk = x[b * HD:(b + 1) * HD]
        outs.append(jnp.dot(tril, blk, precision=HI, preferred_element_type=F32) + carry)
        carry = carry + jnp.sum(blk, axis=0, keepdims=True)
    return jnp.concatenate(outs, axis=0)


def _row_spec(r, tm):
    if isinstance(r, tuple):
        arr, width, cb = r
        return arr, pl.BlockSpec((tm, width), lambda i, cb=cb: (i, cb))
    return r, pl.BlockSpec((tm, r.shape[1]), lambda i: (i, 0))


ANY_SPEC = pl.BlockSpec(memory_space=pl.ANY)


def _rowwise(name, fn, rows, consts, outs, tm, deps=()):
    arrs, specs = zip(*[_row_spec(r, tm) for r in rows])
    n_rows = arrs[0].shape[0]
    nr, nc, nd = len(rows), len(consts), len(deps)

    def body(*refs):
        res = fn(*[r[...] for r in refs[:nr + nc]])
        for o, v in zip(refs[nr + nc + nd:], res):
            o[...] = v.astype(o.dtype)

    return pl.pallas_call(
        body, grid=(n_rows // tm,), name=name,
        in_specs=list(specs) + [pl.BlockSpec(c.shape, lambda i: (0, 0)) for c in consts] + [ANY_SPEC] * nd,
        out_specs=[pl.BlockSpec((tm, w), lambda i: (i, 0)) for w, _ in outs],
        out_shape=[SDS((n_rows, w), dt) for w, dt in outs],
        compiler_params=_cp("parallel"),
    )(*arrs, *consts, *deps)


def _rowwise_vjp(name, fn, rows, consts, cts, grad_dtypes, tm, deps=()):
    arrs, specs = zip(*[_row_spec(r, tm) for r in rows])
    ct_arrs, ct_specs = zip(*[_row_spec(r, tm) for r in cts])
    n_rows = arrs[0].shape[0]
    nr, nc, nct, ng, nd = len(rows), len(consts), len(cts), len(grad_dtypes), len(deps)
    widths = [s.block_shape[1] for s in specs[:ng]]

    def body(*refs):
        vals = [r[...].astype(F32) for r in refs[:nr + nc]]
        ctv = tuple(r[...].astype(F32) for r in refs[nr + nc:nr + nc + nct])
        _, vjp = jax.vjp(fn, *vals)
        grads = vjp(ctv)
        outs = refs[nr + nc + nct + nd:]
        for o, g in zip(outs[:ng], grads[:ng]):
            o[...] = g.astype(o.dtype)

        @pl.when(pl.program_id(0) == 0)
        def _():
            for o in outs[ng:]:
                o[...] = jnp.zeros_like(o)

        for o, g in zip(outs[ng:], grads[nr:]):
            o[...] += g

    return pl.pallas_call(
        body, grid=(n_rows // tm,), name=name,
        in_specs=list(specs) + [pl.BlockSpec(c.shape, lambda i: (0, 0)) for c in consts] + list(ct_specs)
        + [ANY_SPEC] * nd,
        out_specs=[pl.BlockSpec((tm, w), lambda i: (i, 0)) for w in widths]
        + [pl.BlockSpec(c.shape, lambda i: (0, 0)) for c in consts],
        out_shape=[SDS((n_rows, w), dt) for w, dt in zip(widths, grad_dtypes)] + [SDS(c.shape, F32) for c in consts],
        compiler_params=_cp("arbitrary"),
    )(*arrs, *consts, *ct_arrs, *deps)


def _tile(n, pref):
    t = min(n, pref)
    while n % t or (t % HD and t != n):
        t -= 1
    return t


def _matmul(name, a, b, dims, out_dtype, tm, tn, residual=None, deps=(), kidx=0):
    ta, tb = dims == TN, dims == NT
    m = a.shape[1] if ta else a.shape[0]
    k = a.shape[0] if ta else a.shape[1]
    n = b.shape[0] if tb else b.shape[1]
    tm, tn = _tile(m, tm), _tile(n, tn)
    assert kidx == 0 or (tb and b.shape[1] % k == 0)

    def body(*refs):
        acc = _dot(refs[0][...], refs[1][...], dims)
        if residual is not None:
            acc = acc + refs[2][...]
        refs[-1][...] = acc.astype(out_dtype)

    in_specs = [pl.BlockSpec((k, tm), lambda i, j: (0, i)) if ta else pl.BlockSpec((tm, k), lambda i, j: (i, 0)),
                pl.BlockSpec((tn, k), lambda i, j: (j, kidx)) if tb else pl.BlockSpec((k, tn), lambda i, j: (0, j))]
    ops = [a, b]
    if residual is not None:
        in_specs.append(pl.BlockSpec((tm, tn), lambda i, j: (i, j)))
        ops.append(residual)
    in_specs += [ANY_SPEC] * len(deps)
    ops += list(deps)
    return pl.pallas_call(
        body, grid=(m // tm, n // tn), name=name, in_specs=in_specs,
        out_specs=pl.BlockSpec((tm, tn), lambda i, j: (i, j)), out_shape=SDS((m, n), out_dtype),
        compiler_params=_cp("parallel", "parallel"),
    )(*ops)


def _ffn_up(h1n, wgu):
    t, d = h1n.shape
    w = wgu.shape[3]
    tm = _tile(t, 512)

    def body(a, b, gu, act):
        x = a[...]
        g = _dot(x, b[0])
        u = _dot(x, b[1])
        gu[0] = g.astype(BF16)
        gu[1] = u.astype(BF16)
        act[...] = (_silu(g) * u).astype(BF16)

    return pl.pallas_call(
        body, grid=(4, t // tm), name="ffn_up",
        in_specs=[pl.BlockSpec((tm, d), lambda j, i: (i, 0)), pl.BlockSpec((2, None, d, w), lambda j, i: (0, j, 0, 0))],
        out_specs=[pl.BlockSpec((2, None, tm, w), lambda j, i: (0, j, i, 0)), pl.BlockSpec((tm, w), lambda j, i: (i, j))],
        out_shape=[SDS((2, 4, t, w), BF16), SDS((t, 4 * w), BF16)],
        compiler_params=_cp("parallel", "parallel"),
    )(h1n, wgu)


def _ffn_down_loss(act, wdown, h1, target):
    t, f = act.shape
    d = wdown.shape[1]
    tm, tn = _tile(t, 512), _tile(d, 512)

    def body(a, b, h, tg, dy, dyb, ls):
        e = _dot(a[...], b[...]) + h[...] - tg[...]
        g = e * (1.0 / d)
        dy[...] = g
        dyb[...] = g.astype(BF16)
        ls[...] = jnp.broadcast_to(jnp.sum(e * e), (8, HD))

    return pl.pallas_call(
        body, grid=(t // tm, d // tn), name="ffn_down_loss",
        in_specs=[pl.BlockSpec((tm, f), lambda i, j: (i, 0)), pl.BlockSpec((f, tn), lambda i, j: (0, j)),
                  pl.BlockSpec((tm, tn), lambda i, j: (i, j)), pl.BlockSpec((tm, tn), lambda i, j: (i, j))],
        out_specs=[pl.BlockSpec((tm, tn), lambda i, j: (i, j)), pl.BlockSpec((tm, tn), lambda i, j: (i, j)),
                   pl.BlockSpec((8, HD), lambda i, j: (i, j))],
        out_shape=[SDS((t, d), F32), SDS((t, d), BF16), SDS((8 * (t // tm), HD * (d // tn)), F32)],
        compiler_params=_cp("parallel", "parallel"),
    )(act, wdown, h1, target)


def _ffn_down_bwd(dyb, wdown4, gu):
    t, d = dyb.shape
    w = wdown4.shape[1]
    tm = _tile(t, 512)

    def body(a, b, gu_ref, out):
        da = _dot(a[...], b[...], NT)
        g = gu_ref[0].astype(F32)
        u = gu_ref[1].astype(F32)
        s = _sigmoid(g)
        out[0] = (da * u * (s * (1.0 + g * (1.0 - s)))).astype(BF16)
        out[1] = (da * g * s).astype(BF16)

    return pl.pallas_call(
        body, grid=(4, t // tm), name="ffn_down_bwd",
        in_specs=[pl.BlockSpec((tm, d), lambda j, i: (i, 0)), pl.BlockSpec((None, w, d), lambda j, i: (j, 0, 0)),
                  pl.BlockSpec((2, None, tm, w), lambda j, i: (0, j, i, 0))],
        out_specs=pl.BlockSpec((2, None, tm, w), lambda j, i: (0, j, i, 0)),
        out_shape=SDS((2, 4, t, w), BF16),
        compiler_params=_cp("parallel", "parallel"),
    )(dyb, wdown4, gu)


def _ffn_up_bwd_x(dgu, wgu):
    _, t, w = dgu.shape
    d = wgu.shape[1]
    tm = _tile(t, 512)

    def body(a, b, out):
        @pl.when(pl.program_id(1) == 0)
        def _():
            out[...] = jnp.zeros_like(out)
        out[...] += _dot(a[...], b[...], NT)

    return pl.pallas_call(
        body, grid=(t // tm, 8), name="ffn_up_bwd_x",
        in_specs=[pl.BlockSpec((None, tm, w), lambda i, j: (j, i, 0)), pl.BlockSpec((None, d, w), lambda i, j: (j, 0, 0))],
        out_specs=pl.BlockSpec((tm, d), lambda i, j: (i, 0)), out_shape=SDS((t, d), F32),
        compiler_params=_cp("parallel", "arbitrary"),
    )(dgu, wgu)


def _ffn_up_bwd_w(h1n, dgu):
    _, t, w = dgu.shape
    d = h1n.shape[1]
    tm = _tile(d, 512)

    def body(a, b, out):
        out[...] = _dot(a[...], b[...], TN).astype(BF16)

    return pl.pallas_call(
        body, grid=(8, d // tm), name="ffn_up_bwd_w",
        in_specs=[pl.BlockSpec((t, tm), lambda j, i: (0, i)), pl.BlockSpec((None, t, w), lambda j, i: (j, 0, 0))],
        out_specs=pl.BlockSpec((None, tm, w), lambda j, i: (j, i, 0)), out_shape=SDS((8, d, w), BF16),
        compiler_params=_cp("parallel", "parallel"),
    )(h1n, dgu)


def _fox_prep(fq, fk, sm, fb, qg, kg, h):
    qn = _rms(fq, qg)
    kn = _rms(fk, kg)
    c = _cumsum_rows(-_softplus(-(sm + fb)))
    ccol = _lane_pick(c, L_FF + h)
    crow = jnp.sum(c.T * (_iota((HD, 1), 0) == L_FF + h).astype(F32), axis=0, keepdims=True)
    return qn, kn, ccol, crow


def _fox_block(q, k, v, cc, cr, off):
    s = _dot(q.astype(BF16), k.astype(BF16), NT) * (HD ** -0.5) + cc - cr
    s = jnp.where(_iota(s.shape, 1) <= _iota(s.shape, 0) + off, s, -1e30)
    e = jnp.exp(s - lax.stop_gradient(jnp.max(s, axis=1, keepdims=True)))
    p = e / jnp.sum(e, axis=1, keepdims=True)
    return _dot(p.astype(BF16), v.astype(BF16))


ONE_BUFFER = pl.Buffered(1)


def _pcol(t, cb):
    return pl.BlockSpec((t, HD), lambda h, cb=cb: (0, cb + h), pipeline_mode=ONE_BUFFER)


def _smcol(t):
    return pl.BlockSpec((t, HD), lambda h: (0, SM), pipeline_mode=ONE_BUFFER)


def _head(t):
    return pl.BlockSpec((t, HD), lambda h: (0, h), pipeline_mode=ONE_BUFFER)


def _small(n):
    return pl.BlockSpec((n, HD), lambda h: (0, 0), pipeline_mode=ONE_BUFFER)


def _fox_fwd(p, fb, qg, kg, bq):
    t = p.shape[0]

    def body(fq, fk, fv, sm, fb_r, qg_r, kg_r, o, qn_s, cc_s):
        h = pl.program_id(0)
        qn, kn, ccol, crow = _fox_prep(fq[...], fk[...], sm[...], fb_r[...], qg_r[...], kg_r[...], h)
        qn_s[...] = qn
        cc_s[...] = ccol
        knb = kn.astype(BF16)
        vb = fv[...].astype(BF16)
        for i in range(t // bq):
            rows, ext = pl.ds(i * bq, bq), (i + 1) * bq
            o[rows, :] = _fox_block(qn_s[rows, :], knb[:ext], vb[:ext], cc_s[rows, :], crow[:, :ext], i * bq).astype(o.dtype)

    return pl.pallas_call(
        body, grid=(NF,), name="fox_fwd",
        in_specs=[_pcol(t, FQ), _pcol(t, FK), _pcol(t, FV), _smcol(t), _small(1), _small(1), _small(1)],
        out_specs=_head(t), out_shape=SDS((t, NF * HD), BF16),
        scratch_shapes=[pltpu.VMEM((t, HD), F32), pltpu.VMEM((t, 1), F32)],
        compiler_params=_cp("parallel"),
    )(p, p, p, p, fb, qg, kg)


def _fox_bwd(p, fb, qg, kg, dmix, bq, deps=()):
    t = p.shape[0]

    def body(*refs):
        fq, fk, fv, sm, fb_r, qg_r, kg_r, do = refs[:8]
        dfq, dfk, dfv, dsm, dfb, dqg, dkg, qn_s, cc_s, dqn_s, dcc_s, dkn_s, dv_s, dcr_s = refs[8 + len(deps):]
        h = pl.program_id(0)
        qn, kn, ccol, crow = _fox_prep(fq[...], fk[...], sm[...], fb_r[...], qg_r[...], kg_r[...], h)
        qn_s[...] = qn
        cc_s[...] = ccol
        v = fv[...]
        dkn_s[...] = jnp.zeros_like(dkn_s)
        dv_s[...] = jnp.zeros_like(dv_s)
        dcr_s[...] = jnp.zeros_like(dcr_s)

        for i in range(t // bq):
            rows, ext = pl.ds(i * bq, bq), (i + 1) * bq
            _, vjp = jax.vjp(lambda a, b, c, d, e, off=i * bq: _fox_block(a, b, c, d, e, off),
                             qn_s[rows, :], kn[:ext], v[:ext], cc_s[rows, :], crow[:, :ext])
            dq, dk, dv, dcc, dcr = vjp(do[rows, :])
            dqn_s[rows, :] = dq
            dcc_s[rows, :] = dcc
            dkn_s[:ext, :] += dk
            dv_s[:ext, :] += dv
            dcr_s[:, :ext] += dcr
        _, prep_vjp = jax.vjp(lambda a, b, c, d, e, f: _fox_prep(a, b, c, d, e, f, h),
                              fq[...], fk[...], sm[...], fb_r[...], qg_r[...], kg_r[...])
        g_fq, g_fk, g_sm, g_fb, g_qg, g_kg = prep_vjp((dqn_s[...], dkn_s[...], dcc_s[...], dcr_s[...]))
        dfq[...] = g_fq.astype(dfq.dtype)
        dfk[...] = g_fk.astype(dfk.dtype)
        dfv[...] = dv_s[...].astype(dfv.dtype)

        @pl.when(h == 0)
        def _():
            for r in (dsm, dfb, dqg, dkg):
                r[...] = jnp.zeros_like(r)

        dsm[...] += g_sm
        dfb[...] += g_fb
        dqg[...] += g_qg
        dkg[...] += g_kg

    head = _head(t)
    return pl.pallas_call(
        body, grid=(NF,), name="fox_bwd",
        in_specs=[_pcol(t, FQ), _pcol(t, FK), _pcol(t, FV), _smcol(t), _small(1), _small(1), _small(1), head]
        + [ANY_SPEC] * len(deps),
        out_specs=[head, head, head, _small(t), _small(1), _small(1), _small(1)],
        out_shape=[SDS((t, NF * HD), BF16)] * 3 + [SDS((t, HD), F32)] + [SDS((1, HD), F32)] * 3,
        scratch_shapes=[pltpu.VMEM((t, HD), F32), pltpu.VMEM((t, 1), F32), pltpu.VMEM((t, HD), F32),
                        pltpu.VMEM((t, 1), F32), pltpu.VMEM((t, HD), F32), pltpu.VMEM((t, HD), F32),
                        pltpu.VMEM((1, t), F32)],
        compiler_params=_cp("arbitrary"),
    )(p, p, p, p, fb, qg, kg, dmix, *deps)


def _mem_attn(mq, mk, mv, qg, kg):
    s = _dot(_rms(mq, qg).astype(BF16), _rms(mk, kg).astype(BF16), NT) * (HD ** -0.5)
    e = jnp.exp(s - lax.stop_gradient(jnp.max(s, axis=1, keepdims=True)))
    p = e / jnp.sum(e, axis=1, keepdims=True)
    return _dot(p.astype(BF16), mv.astype(BF16))


def _mem_fwd(p, mkv, qg, kg):
    t, ml = p.shape[0], mkv.shape[0]

    def body(mq, mk, mv, qg_r, kg_r, o):
        o[...] = _mem_attn(mq[...], mk[...], mv[...], qg_r[...], kg_r[...]).astype(o.dtype)

    return pl.pallas_call(
        body, grid=(NM,), name="mem_fwd",
        in_specs=[_pcol(t, MQ), pl.BlockSpec((ml, HD), lambda h: (0, h)), pl.BlockSpec((ml, HD), lambda h: (0, NM + h)),
                  _small(1), _small(1)],
        out_specs=pl.BlockSpec((t, HD), lambda h: (0, h)), out_shape=SDS((t, NM * HD), BF16),
        compiler_params=_cp("parallel"),
    )(p, mkv, mkv, qg, kg)


def _mem_bwd(p, mkv, qg, kg, dmix, deps=()):
    t, ml = p.shape[0], mkv.shape[0]

    def body(*refs):
        mq, mk, mv, qg_r, kg_r, do = refs[:6]
        dmq, dmk, dmv, dqg, dkg = refs[6 + len(deps):]
        _, vjp = jax.vjp(_mem_attn, mq[...], mk[...], mv[...], qg_r[...], kg_r[...])
        g_q, g_k, g_v, g_qg, g_kg = vjp(do[...])
        dmq[...] = g_q.astype(dmq.dtype)
        dmk[...] = g_k
        dmv[...] = g_v

        @pl.when(pl.program_id(0) == 0)
        def _():
            dqg[...] = jnp.zeros_like(dqg)
            dkg[...] = jnp.zeros_like(dkg)

        dqg[...] += g_qg
        dkg[...] += g_kg

    return pl.pallas_call(
        body, grid=(NM,), name="mem_bwd",
        in_specs=[_pcol(t, MQ), pl.BlockSpec((ml, HD), lambda h: (0, h)), pl.BlockSpec((ml, HD), lambda h: (0, NM + h)),
                  _small(1), _small(1), pl.BlockSpec((t, HD), lambda h: (0, NF + NG + h))] + [ANY_SPEC] * len(deps),
        out_specs=[pl.BlockSpec((t, HD), lambda h: (0, h)), pl.BlockSpec((ml, HD), lambda h: (0, h)),
                   pl.BlockSpec((ml, HD), lambda h: (0, h)), _small(1), _small(1)],
        out_shape=[SDS((t, NM * HD), BF16), SDS((ml, NM * HD), F32), SDS((ml, NM * HD), F32),
                   SDS((1, HD), F32), SDS((1, HD), F32)],
        compiler_params=_cp("arbitrary"),
    )(p, mkv, mkv, qg, kg, dmix, *deps)


def _shift_down(x, s):
    if s == 0:
        return x
    return jnp.where(_iota(x.shape, 0) >= s, pltpu.roll(x, s, 0), 0.0)


def _shift_up(x, s):
    if s == 0:
        return x
    n = x.shape[0]
    return jnp.where(_iota(x.shape, 0) < n - s, pltpu.roll(x, n - s, 0), 0.0)


@jax.custom_vjp
def _conv4(x, w0, w1, w2, w3):
    return w0 * _shift_down(x, 3) + w1 * _shift_down(x, 2) + w2 * _shift_down(x, 1) + w3 * x


def _conv4_fwd(x, w0, w1, w2, w3):
    return _conv4(x, w0, w1, w2, w3), (x, w0, w1, w2, w3)


def _conv4_bwd(res, dy):
    x, w0, w1, w2, w3 = res
    dx = w0 * _shift_up(dy, 3) + w1 * _shift_up(dy, 2) + w2 * _shift_up(dy, 1) + w3 * dy
    dws = tuple(jnp.sum(dy * _shift_down(x, 3 - k), axis=0, keepdims=True) for k in range(4))
    return (dx,) + dws


_conv4.defvjp(_conv4_fwd, _conv4_bwd)


def _gdn_prep(gq, gk, gv, sm, taps, alog, dtb, h):
    q, k, v = [_silu(_conv4(x, *taps[4 * j:4 * j + 4])) for j, x in enumerate((gq, gk, gv))]
    q = q * lax.rsqrt(jnp.sum(q * q, axis=-1, keepdims=True) + NORM_EPS) * (HD ** -0.5)
    k = k * lax.rsqrt(jnp.sum(k * k, axis=-1, keepdims=True) + NORM_EPS)
    g = _lane_pick(-jnp.exp(alog) * _softplus(sm + dtb), L_GA + h)
    beta = _lane_pick(_sigmoid(sm), L_GB + h)
    return q, k, v, g, beta


def _split(x, n):
    parts, rest = [], x
    for i in range(n):
        parts.append(rest.astype(BF16))
        if i + 1 < n:
            rest = rest - parts[-1].astype(F32)
    return parts


def _raw_dot(a, b, form):
    lead = a.ndim - 2
    ca, cb = {"nn": (1, 0), "nt": (1, 1), "tn": (0, 0)}[form]
    batch = ((0,), (0,)) if lead else ((), ())
    return lax.dot_general(a, b, (((ca + lead,), (cb + lead,)), batch), preferred_element_type=F32)


def _pdot_impl(a, b, form, mode):
    if mode == "1":
        return _raw_dot(a.astype(BF16), b.astype(BF16), form)
    if mode == "3":
        (ah, al), (bh, bl) = _split(a, 2), _split(b, 2)
        return _raw_dot(ah, bh, form) + (_raw_dot(al, bh, form) + _raw_dot(ah, bl, form))
    if mode == "xa":
        return sum(_raw_dot(a.astype(BF16), t, form) for t in reversed(_split(b, 3)))
    return sum(_raw_dot(t, b.astype(BF16), form) for t in reversed(_split(a, 3)))


@functools.partial(jax.custom_vjp, nondiff_argnums=(2, 3))
def _pdot(a, b, form, mode):
    return _pdot_impl(a, b, form, mode)


def _pdot_fwd(a, b, form, mode):
    return _pdot_impl(a, b, form, mode), (a, b)


def _pdot_bwd(form, mode, res, ct):
    a, b = res
    da_args, db_args = {"nn": ((ct, b, "nt"), (a, ct, "tn")), "nt": ((ct, b, "nn"), (ct, a, "tn")),
                        "tn": ((b, ct, "nt"), (a, ct, "nn"))}[form]

    def side(args, exact):
        if mode in ("1", "3"):
            return mode
        return "xa" if args[0] is exact else "xb"

    if mode == "xa":
        return jnp.zeros_like(a), _pdot_impl(*db_args, side(db_args, a))
    if mode == "xb":
        return _pdot_impl(*da_args, side(da_args, b)), jnp.zeros_like(b)
    return _pdot_impl(*da_args, mode), _pdot_impl(*db_args, mode)


_pdot.defvjp(_pdot_fwd, _pdot_bwd)

GDN_QK, GDN_INV, GDN_SCAN = "1", "3", "1"


@jax.custom_vjp
def _tri_inv(low):
    eye = (_iota((CHUNK, CHUNK), 0) == _iota((CHUNK, CHUNK), 1)).astype(F32)
    inv = eye - low
    pw = low
    for _ in range(5):
        pw = _pdot_impl(pw, pw, "nn", GDN_INV)
        inv = inv + _pdot_impl(inv, pw, "nn", GDN_INV)
    return inv


def _tri_inv_fwd(low):
    inv = _tri_inv(low)
    return inv, inv


def _tri_inv_bwd(inv, ct):
    return (-_pdot_impl(_pdot_impl(inv, ct, "tn", GDN_INV), inv, "nt", GDN_INV),)


_tri_inv.defvjp(_tri_inv_fwd, _tri_inv_bwd)


def _gdn_intra(q, k, v, g, beta):
    n = q.shape[0]
    r, c = _iota((CHUNK, CHUNK), 0), _iota((CHUNK, CHUNK), 1)
    tril, strict = r >= c, r > c
    trilf = jnp.broadcast_to(tril.astype(F32), (n, CHUNK, CHUNK))
    gcm = _pdot(trilf, jnp.broadcast_to(g, (n, CHUNK, CHUNK)), "nn", "xa")
    gcf = _pdot(trilf, jnp.broadcast_to(g, (n, CHUNK, HD)), "nn", "xa")
    lane0 = (_iota((1, 1, CHUNK), 2) == 0).astype(F32)
    gcr = _pdot(jnp.ones((n, CHUNK, CHUNK), F32), gcm * lane0, "nt", "xa")
    decay = jnp.where(tril, jnp.exp(jnp.where(tril, gcm - gcr, 0.0)), 0.0)
    egc = jnp.exp(gcf)
    kb = k * beta
    low = jnp.where(strict, _pdot(kb, k, "nt", GDN_QK) * decay, 0.0)
    inv = _tri_inv(low)
    u = _pdot(inv, v * beta, "nn", GDN_INV)
    w = _pdot(inv, kb * egc, "nn", GDN_INV)
    at = jnp.where(tril, _pdot(q, k, "nt", GDN_QK) * decay, 0.0)
    gl = jnp.sum(jnp.broadcast_to(g, (n, CHUNK, HD)), axis=1, keepdims=True)
    return u, w, q * egc, at, k * jnp.exp(gl - gcf), gl


def _gdn_step(s, u, w, qg, at, kd, gl):
    vn = u - _pdot(w, s, "nn", GDN_SCAN)
    o = _pdot(qg, s, "nn", GDN_SCAN) + _pdot(at, vn, "nn", GDN_SCAN)
    s2 = s * jnp.exp(gl) + _pdot(kd, vn, "tn", GDN_SCAN)
    return o, s2


SCAN_HEADS = 3


def _gdn_chunked_scratch(nc):
    big = pltpu.VMEM((nc, CHUNK, HD), F32)
    return [big, big, big, pltpu.VMEM((nc, CHUNK, 1), F32), pltpu.VMEM((nc, CHUNK, 1), F32)]


def _gdn_term_shapes(nc):
    return [(nc, CHUNK, HD), (nc, CHUNK, HD), (nc, CHUNK, HD), (nc, CHUNK, CHUNK), (nc, CHUNK, HD), (nc, 1, HD)]


def _per_head(shape, heads=None):
    lead = (None,) if heads is None else (heads,)
    return pl.BlockSpec(lead + tuple(shape), lambda h: (h,) + (0,) * len(shape), pipeline_mode=ONE_BUFFER)


def _gdn_in_specs(t):
    cw = lambda cb: pl.BlockSpec((4, HD), lambda h, cb=cb: (0, cb + h))
    return [_pcol(t, GQ), _pcol(t, GK), _pcol(t, GV), _smcol(t), cw(0), cw(NG), cw(2 * NG), _small(1), _small(1)]


def _taps(wq, wk, wv):
    return tuple(w[k:k + 1, :] for w in (wq, wk, wv) for k in range(4))


def _gdn_stage(vals, refs):
    nc = refs[0].shape[0]
    for v, r in zip(vals, refs):
        r[...] = v.reshape(nc, CHUNK, v.shape[-1])


def _gdn_intra_all(chunked, intra):
    nc = chunked[0].shape[0]
    grp_n = math.gcd(nc, GROUP)

    def grp(i, carry):
        sl = pl.ds(pl.multiple_of(i * grp_n, grp_n), grp_n)
        for r, val in zip(intra, _gdn_intra(*[c[sl] for c in chunked])):
            r[sl] = val
        return carry

    lax.fori_loop(0, nc // grp_n, grp, 0)


def _gdn_fwd(p, conv, alog, dtb):
    t = p.shape[0]
    nc = t // CHUNK
    terms = _gdn_term_shapes(nc)

    def body(gq, gk, gv, sm, wq, wk, wv, al, db, o, *rest):
        h = pl.program_id(0)
        intra, states, chunked = rest[:6], rest[6], rest[7:]
        _gdn_stage(_gdn_prep(gq[...], gk[...], gv[...], sm[...], _taps(wq, wk, wv), al[...], db[...], h), chunked)
        _gdn_intra_all(chunked, intra)

        def step(c, s):
            states[c] = s
            oc, s2 = _gdn_step(s, *[r[c] for r in intra])
            o[pl.ds(pl.multiple_of(c * CHUNK, CHUNK), CHUNK), :] = oc
            return s2

        lax.fori_loop(0, nc, step, jnp.zeros((HD, HD), F32))

    outs = pl.pallas_call(
        body, grid=(NG,), name="gdn_fwd", in_specs=_gdn_in_specs(t),
        out_specs=[_head(t)] + [_per_head(sh) for sh in terms] + [_per_head((nc, HD, HD))],
        out_shape=[SDS((t, NG * HD), F32)] + [SDS((NG,) + sh, F32) for sh in terms] + [SDS((NG, nc, HD, HD), F32)],
        scratch_shapes=_gdn_chunked_scratch(nc), compiler_params=_cp("parallel"),
    )(p, p, p, p, conv, conv, conv, alog, dtb)
    return outs[0], list(outs[1:])


def _gdn_bwd_scan(saved, do_raw):
    nc = saved[0].shape[1]
    terms = _gdn_term_shapes(nc)

    def body(*refs):
        intra, states, do, outs = refs[:6], refs[6], refs[7], refs[8:]

        def bwd(i, dss):
            c = nc - 1 - i
            rows = pl.ds(pl.multiple_of(c * CHUNK, CHUNK), CHUNK)
            new = []
            for hh in range(SCAN_HEADS):
                _, vjp = jax.vjp(_gdn_step, states[hh, c], *[r[hh, c] for r in intra])
                grads = vjp((do[rows, hh * HD:(hh + 1) * HD], dss[hh]))
                for r, gval in zip(outs, grads[1:]):
                    r[hh, c] = gval
                new.append(grads[0])
            return tuple(new)

        lax.fori_loop(0, nc, bwd, tuple(jnp.zeros((HD, HD), F32) for _ in range(SCAN_HEADS)))

    return pl.pallas_call(
        body, grid=(NG // SCAN_HEADS,), name="gdn_bwd_scan",
        in_specs=[_per_head(sh, SCAN_HEADS) for sh in terms] + [_per_head((nc, HD, HD), SCAN_HEADS)]
        + [pl.BlockSpec((nc * CHUNK, SCAN_HEADS * HD), lambda h: (0, h), pipeline_mode=ONE_BUFFER)],
        out_specs=[_per_head(sh, SCAN_HEADS) for sh in terms],
        out_shape=[SDS((NG,) + sh, F32) for sh in terms], compiler_params=_cp("parallel"),
    )(*saved, do_raw)


def _gdn_bwd(p, conv, alog, dtb, dterms):
    t = p.shape[0]
    nc = t // CHUNK
    terms = _gdn_term_shapes(nc)

    def body(*refs):
        gq, gk, gv, sm, wq, wk, wv, al, db = refs[:9]
        dintra = refs[9:15]
        dgq, dgk, dgv, dsm, dwq, dwk, dwv, dal, ddb = refs[15:24]
        chunked = refs[24:]
        h = pl.program_id(0)
        _gdn_stage(_gdn_prep(gq[...], gk[...], gv[...], sm[...], _taps(wq, wk, wv), al[...], db[...], h), chunked)
        grp_n = math.gcd(nc, GROUP)

        def grp(i, carry):
            sl = pl.ds(pl.multiple_of(i * grp_n, grp_n), grp_n)
            _, vjp = jax.vjp(_gdn_intra, *[r[sl] for r in chunked])
            for r, gval in zip(chunked, vjp(tuple(r[sl] for r in dintra))):
                r[sl] = gval
            return carry

        lax.fori_loop(0, nc // grp_n, grp, 0)
        _, prep_vjp = jax.vjp(
            lambda *a: _gdn_prep(*a, h), gq[...], gk[...], gv[...], sm[...], _taps(wq, wk, wv), al[...], db[...])
        grads = prep_vjp(tuple(r[...].reshape(t, r.shape[-1]) for r in chunked))
        for r, gval in zip((dgq, dgk, dgv), grads[:3]):
            r[...] = gval.astype(r.dtype)
        for j, r in enumerate((dwq, dwk, dwv)):
            for k in range(4):
                r[k:k + 1, :] = grads[4][4 * j + k]

        @pl.when(h == 0)
        def _():
            for r in (dsm, dal, ddb):
                r[...] = jnp.zeros_like(r)

        dsm[...] += grads[3]
        dal[...] += grads[5]
        ddb[...] += grads[6]

    head = _head(t)
    taps = pl.BlockSpec((4, HD), lambda h: (0, h))
    return pl.pallas_call(
        body, grid=(NG,), name="gdn_bwd", in_specs=_gdn_in_specs(t) + [_per_head(sh) for sh in terms],
        out_specs=[head, head, head, _small(t), taps, taps, taps, _small(1), _small(1)],
        out_shape=[SDS((t, NG * HD), BF16)] * 3 + [SDS((t, HD), F32)] + [SDS((4, NG * HD), F32)] * 3 + [SDS((1, HD), F32)] * 2,
        scratch_shapes=_gdn_chunked_scratch(nc), compiler_params=_cp("arbitrary"),
    )(p, p, p, p, conv, conv, conv, alog, dtb, *dterms)


def _gdn_post(o, z, gain):
    return (jnp.concatenate(
        [_rms(o[:, h * HD:(h + 1) * HD], gain) * _silu(z[:, h * HD:(h + 1) * HD]) for h in range(NG)], axis=1),)


def _place():
    return lax.axis_index("x"), lax.axis_index("y"), lax.axis_index("c")


def _all_gather(name, shard):
    def body(x_ref, out_ref, send_sems, recv_sems, local_sem):
        x, y, c = _place()
        me, sibling = (x, y, c), (x, y, 1 - c)
        chips = [(1 - x, y), (x, 1 - y), (1 - x, 1 - y)]

        def blk(px, py, pc):
            return out_ref.at[4 * px + 2 * py + pc]

        def copy(k, block, to, src=None):
            return pltpu.make_async_remote_copy(
                src_ref=blk(*block) if src is None else src, dst_ref=blk(*block),
                send_sem=send_sems.at[k], recv_sem=recv_sems.at[k], device_id=to, device_id_type=MESH)

        mine = pltpu.make_async_copy(x_ref, blk(*me), local_sem)
        mine.start()
        first = [copy(0, me, sibling, src=x_ref)]
        first += [copy(1 + j, me, (*chip, c), src=x_ref) for j, chip in enumerate(chips)]
        for cp in first:
            cp.start()
        passed = [copy(4 + j, (*chip, c), sibling) for j, chip in enumerate(chips)]
        for j, chip in enumerate(chips):
            copy(1 + j, (*chip, c), me).wait_recv()
            passed[j].start()
        copy(0, sibling, me).wait_recv()
        for j, chip in enumerate(chips):
            copy(4 + j, (*chip, 1 - c), me).wait_recv()
        for cp in first + passed:
            cp.wait_send()
        mine.wait()

    return pl.pallas_call(
        body, name=name, out_shape=SDS((N_DEV,) + shard.shape, shard.dtype),
        in_specs=[pl.BlockSpec(memory_space=pltpu.HBM)], out_specs=pl.BlockSpec(memory_space=pltpu.HBM),
        scratch_shapes=[pltpu.SemaphoreType.DMA((7,)), pltpu.SemaphoreType.DMA((7,)), pltpu.SemaphoreType.DMA],
    )(shard)


def _scatter_exchange(name, full):
    def body(g_ref, out_ref, send_sems, recv_sems, local_sem):
        x, y, c = _place()
        me = 4 * x + 2 * y + c
        mine = pltpu.make_async_copy(g_ref.at[me], out_ref.at[me], local_sem)
        mine.start()
        sends, recvs = [], []
        for k in range(1, N_DEV):
            px = 1 - x if k & 4 else x
            py = 1 - y if k & 2 else y
            pc = 1 - c if k & 1 else c
            peer = 4 * px + 2 * py + pc
            sends.append(pltpu.make_async_remote_copy(
                src_ref=g_ref.at[peer], dst_ref=out_ref.at[me], send_sem=send_sems.at[k - 1],
                recv_sem=recv_sems.at[k - 1], device_id=(px, py, pc), device_id_type=MESH))
            recvs.append(pltpu.make_async_remote_copy(
                src_ref=g_ref.at[me], dst_ref=out_ref.at[peer], send_sem=send_sems.at[k - 1],
                recv_sem=recv_sems.at[k - 1], device_id=(px, py, pc), device_id_type=MESH))
        for cp in sends:
            cp.start()
        for cp in recvs:
            cp.wait_recv()
        for cp in sends:
            cp.wait_send()
        mine.wait()

    return pl.pallas_call(
        body, name=name, out_shape=SDS(full.shape, full.dtype),
        in_specs=[pl.BlockSpec(memory_space=pltpu.HBM)], out_specs=pl.BlockSpec(memory_space=pltpu.HBM),
        scratch_shapes=[pltpu.SemaphoreType.DMA((7,)), pltpu.SemaphoreType.DMA((7,)), pltpu.SemaphoreType.DMA],
    )(full)


def _sum_blocks(name, parts):
    _, r, c = parts.shape
    tr = 64 if r % 64 == 0 else r

    def body(x, o):
        acc = x[0].astype(F32)
        for d in range(1, N_DEV):
            acc = acc + x[d].astype(F32)
        o[...] = acc

    return pl.pallas_call(
        body, grid=(r // tr,), name=name, in_specs=[pl.BlockSpec((N_DEV, tr, c), lambda i: (0, i, 0))],
        out_specs=pl.BlockSpec((tr, c), lambda i: (i, 0)), out_shape=SDS((r, c), F32), compiler_params=_cp("parallel"),
    )(parts)


def _reduce_scatter(name, full):
    return _sum_blocks(name + "_sum", _scatter_exchange(name, full))


def _all_reduce_small(name, x, reduce):
    m_per, n = x.shape

    def body(x_ref, out_ref, send_sems, recv_sems, local_sem):
        px, py, pc = _place()
        me, sibling = (px, py, pc), (px, py, 1 - pc)
        chips = [(1 - px, py), (px, 1 - py), (1 - px, 1 - py)]
        buf = out_ref

        def rows(qx, qy, qc):
            return buf.at[pl.ds((4 * qx + 2 * qy + qc) * m_per, m_per), :]

        def copy(k, block, to, src=None):
            return pltpu.make_async_remote_copy(
                src_ref=rows(*block) if src is None else src, dst_ref=rows(*block),
                send_sem=send_sems.at[k], recv_sem=recv_sems.at[k], device_id=to, device_id_type=MESH)

        mine = pltpu.make_async_copy(x_ref, rows(*me), local_sem)
        mine.start()
        first = [copy(0, me, sibling, src=x_ref)]
        first += [copy(1 + j, me, (*chip, pc), src=x_ref) for j, chip in enumerate(chips)]
        for cp in first:
            cp.start()
        passed = [copy(4 + j, (*chip, pc), sibling) for j, chip in enumerate(chips)]
        for j, chip in enumerate(chips):
            copy(1 + j, (*chip, pc), me).wait_recv()
            passed[j].start()
        copy(0, sibling, me).wait_recv()
        for j, chip in enumerate(chips):
            copy(4 + j, (*chip, 1 - pc), me).wait_recv()
        for cp in first + passed:
            cp.wait_send()
        mine.wait()

    gathered = pl.pallas_call(
        body, name=name, out_shape=SDS((N_DEV * m_per, n), x.dtype),
        in_specs=[pl.BlockSpec(memory_space=pltpu.VMEM)], out_specs=pl.BlockSpec(memory_space=pltpu.VMEM),
        scratch_shapes=[pltpu.SemaphoreType.DMA((7,)), pltpu.SemaphoreType.DMA((7,)), pltpu.SemaphoreType.DMA],
    )(x)
    if not reduce:
        return gathered
    return _sum_blocks(name + "_sum", gathered.reshape(N_DEV, m_per, n))


HBM_SPEC = pl.BlockSpec(memory_space=pltpu.HBM)
SEM_SPEC = pl.BlockSpec(memory_space=pltpu.SEMAPHORE)
EFFECT = pltpu.SideEffectType.DATAFLOW_SIDE_EFFECTING


def _copies_start(name, bufs, n_remote, n_local, build, deps):
    nb, nd = len(bufs), len(deps)
    sem_shapes = [pltpu.SemaphoreType.DMA((n_remote,)), pltpu.SemaphoreType.DMA((n_remote,))]
    if n_local:
        sem_shapes.append(pltpu.SemaphoreType.DMA((n_local,)))
    ns = len(sem_shapes)

    def body(*refs):
        sems = refs[nb + nd:nb + nd + ns]
        remote, local = build(refs[:nb], *sems, *([None] * (3 - ns)))
        for cp in local + remote:
            cp.start()
        refs[-1][...] = jnp.zeros((8, HD), F32)

    outs = pl.pallas_call(
        body, name=name,
        out_shape=(*sem_shapes, *[pltpu.HBM(b.shape, b.dtype) for b in bufs], SDS((8, HD), F32)),
        in_specs=[HBM_SPEC] * nb + [ANY_SPEC] * nd,
        out_specs=(*[SEM_SPEC] * ns, *[HBM_SPEC] * nb, pl.BlockSpec(memory_space=pltpu.VMEM)),
        input_output_aliases={i: ns + i for i in range(nb)},
        compiler_params=pltpu.CompilerParams(has_side_effects=EFFECT),
    )(*[pltpu.with_memory_space_constraint(b, pltpu.HBM) for b in bufs], *deps)
    return list(outs[:ns]), list(outs[ns:ns + nb]), outs[-1]


def _copies_wait(name, bufs, sems, build, after):
    nb, ns = len(bufs), len(sems)

    def body(*refs):
        remote, local = build(refs[:nb], *refs[nb:nb + ns], *([None] * (3 - ns)))
        for cp in local:
            cp.wait()
        for cp in remote:
            cp.wait_send()
            cp.wait_recv()

    outs = pl.pallas_call(
        body, name=name, out_shape=tuple(pltpu.HBM(b.shape, b.dtype) for b in bufs),
        in_specs=[HBM_SPEC] * nb + [SEM_SPEC] * ns + [ANY_SPEC] * len(after), out_specs=tuple([HBM_SPEC] * nb),
        input_output_aliases={i: i for i in range(nb)},
        compiler_params=pltpu.CompilerParams(has_side_effects=EFFECT),
    )(*bufs, *sems, *after)
    return list(outs)


def _remote(src, dst, send, recv, k, to):
    return pltpu.make_async_remote_copy(src_ref=src, dst_ref=dst, send_sem=send.at[k], recv_sem=recv.at[k],
                                        device_id=to, device_id_type=MESH)


class _Gather:
    def __init__(self, name, shards, deps):
        self.name, self.n = name, len(shards)
        lands = [lax.empty((N_DEV,) + s.shape, s.dtype) for s in shards]
        self.sems, bufs, self.token = _copies_start(name + "_s1", list(shards) + lands, 4 * self.n, self.n, self._stage1, deps)
        self.shards, self.lands = bufs[:self.n], bufs[self.n:]

    def _stage1(self, refs, send, recv, loc):
        x, y, c = _place()
        me = 4 * x + 2 * y + c
        targets = [(x, y, 1 - c), (1 - x, y, c), (x, 1 - y, c), (1 - x, 1 - y, c)]
        remote, local = [], []
        for i in range(self.n):
            src, land = refs[i], refs[self.n + i]
            local.append(pltpu.make_async_copy(src, land.at[me], loc.at[i]))
            remote += [_remote(src, land.at[me], send, recv, 4 * i + k, to) for k, to in enumerate(targets)]
        return remote, local

    def _stage2(self, refs, send, recv, loc):
        x, y, c = _place()
        remote = []
        for i in range(self.n):
            for j, (cx, cy) in enumerate([(1 - x, y), (x, 1 - y), (1 - x, 1 - y)]):
                blk = refs[i].at[4 * cx + 2 * cy + c]
                remote.append(_remote(blk, blk, send, recv, 3 * i + j, (x, y, 1 - c)))
        return remote, []

    def mid(self, after):
        bufs = _copies_wait(self.name + "_w1", self.shards + self.lands, self.sems, self._stage1, after)
        self.sems, self.lands, self.token = _copies_start(self.name + "_s2", bufs[self.n:], 3 * self.n, 0, self._stage2, ())

    def end(self, after):
        return _copies_wait(self.name + "_w2", self.lands, self.sems, self._stage2, after)


def _rows_tile(r, row_bytes, target=1 << 20):
    tr = r
    while tr % 32 == 0 and tr * row_bytes > target:
        tr //= 2
    return tr


def _pair_add(name, g, got, c):
    _, r, cols = g.shape
    tr = _rows_tile(r, cols * 2)

    def body(s, a, b, o):
        o[...] = (a[...].astype(F32) + b[...].astype(F32)).astype(o.dtype)

    return pl.pallas_call(
        body, name=name, out_shape=SDS((4, r, cols), g.dtype),
        grid_spec=pltpu.PrefetchScalarGridSpec(
            num_scalar_prefetch=1, grid=(4, r // tr),
            in_specs=[pl.BlockSpec((None, tr, cols), lambda j, i, s: (2 * j + s[0], i, 0)),
                      pl.BlockSpec((None, tr, cols), lambda j, i, s: (j, i, 0))],
            out_specs=pl.BlockSpec((None, tr, cols), lambda j, i, s: (j, i, 0))),
        compiler_params=_cp("parallel", "parallel"),
    )(c.reshape(1), g, got)


def _quad_sum(name, part, got, chip):
    _, r, cols = part.shape
    tr = _rows_tile(r, cols * 4)

    def body(s, a, b1, b2, b3, o):
        o[...] = ((a[...].astype(F32) + b1[...].astype(F32)) + b2[...].astype(F32)) + b3[...].astype(F32)

    blk = lambda k: pl.BlockSpec((None, tr, cols), lambda i, s, k=k: (jnp.bitwise_xor(s[0], k), i, 0))
    return pl.pallas_call(
        body, name=name, out_shape=SDS((r, cols), F32),
        grid_spec=pltpu.PrefetchScalarGridSpec(
            num_scalar_prefetch=1, grid=(r // tr,), in_specs=[blk(0), blk(1), blk(2), blk(3)],
            out_specs=pl.BlockSpec((tr, cols), lambda i, s: (i, 0))),
        compiler_params=_cp("parallel"),
    )(chip.reshape(1), part, got, got, got)


class _Scatter:
    def __init__(self, name, grads, deps):
        self.name, self.n = name, len(grads)
        got = [lax.empty((4,) + g.shape[1:], g.dtype) for g in grads]
        self.sems, bufs, self.token = _copies_start(name + "_s1", list(grads) + got, 4 * self.n, 0, self._stage1, deps)
        self.grads, self.got = bufs[:self.n], bufs[self.n:]

    def _stage1(self, refs, send, recv, loc):
        x, y, c = _place()
        remote = []
        for i in range(self.n):
            remote += [_remote(refs[i].at[2 * j + 1 - c], refs[self.n + i].at[j], send, recv, 4 * i + j, (x, y, 1 - c))
                       for j in range(4)]
        return remote, []

    def _stage2(self, refs, send, recv, loc):
        x, y, c = _place()
        remote = []
        for i in range(self.n):
            for k in (1, 2, 3):
                tx = 1 - x if k & 2 else x
                ty = 1 - y if k & 1 else y
                remote.append(_remote(refs[i].at[2 * tx + ty], refs[self.n + i].at[2 * x + y], send, recv,
                                      3 * i + k - 1, (tx, ty, c)))
        return remote, []

    def mid(self, after):
        bufs = _copies_wait(self.name + "_w1", self.grads + self.got, self.sems, self._stage1, after)
        c = lax.axis_index("c").astype(jnp.int32)
        parts = [_pair_add(f"{self.name}_add{i}", bufs[i], bufs[self.n + i], c) for i in range(self.n)]
        got = [lax.empty(p.shape, p.dtype) for p in parts]
        self.sems, bufs, self.token = _copies_start(self.name + "_s2", parts + got, 3 * self.n, 0, self._stage2, ())
        self.parts, self.got = bufs[:self.n], bufs[self.n:]

    def end(self, after):
        bufs = _copies_wait(self.name + "_w2", self.parts + self.got, self.sems, self._stage2, after)
        chip = (2 * lax.axis_index("x") + lax.axis_index("y")).astype(jnp.int32)
        return [_quad_sum(f"{self.name}_sum{i}", bufs[i], bufs[self.n + i], chip) for i in range(self.n)]


def _adamw(w, g, m, v):
    m = ADAM_B1 * m + (1.0 - ADAM_B1) * g
    v = ADAM_B2 * v + (1.0 - ADAM_B2) * (g * g)
    m_hat = m / (1.0 - ADAM_B1 ** ADAM_STEP)
    v_hat = v / (1.0 - ADAM_B2 ** ADAM_STEP)
    return -ADAM_LR * (m_hat / (jnp.sqrt(v_hat) + ADAM_EPS) + ADAM_WD * w), m, v


def _adamw_call(name, w, g, m, v):
    r, c = w.shape
    tm = 64 if r % 64 == 0 else r
    return _rowwise(name, _adamw, [w, g, m, v], [], [(c, F32)] * 3, tm)


_IN_COLS = 5906


def _perm_in(w):
    pad = jnp.zeros((w.shape[0], PC - _IN_COLS), w.dtype)
    return jnp.concatenate([w[:, 2310:4614], w[:, 4614:5382], w[:, :2304], w[:, 5394:5906], w[:, 2304:2310],
                            w[:, 5382:5394], pad], axis=1)


def _unperm_in(ga, gb):
    return jnp.concatenate([gb[:, :2304], gb[:, 2816:2822], ga[:, :2304], ga[:, 2304:3072], gb[:, 2822:2834],
                            gb[:, 2304:2816]], axis=1)


def _lanes(v, at):
    return jnp.pad(v, ((0, 0), (at, HD - at - v.shape[1])))


_PACK = ("norm_mix", "mem_norm", "norm_ffn", "gdn_conv", "fox_q_norm", "fox_k_norm", "gdn_out_norm", "mem_q_norm",
         "mem_k_norm", "fox_f_bias", "gdn_a_log", "gdn_dt_bias", "loss")


def _pack(vals):
    parts = [vals[n].reshape(-1, HD) for n in _PACK]
    used = sum(p.shape[0] for p in parts)
    buf = jnp.concatenate(parts + [jnp.zeros((-used % 8, HD), F32)], axis=0)
    return buf, [(n, p.shape[0]) for n, p in zip(_PACK, parts)]


def _unpack(buf, layout):
    out, at = {}, 0
    for n, rows in layout:
        out[n] = buf[at:at + rows]
        at += rows
    return out


def kernel(x, mem, norm_mix, w_in, fox_f_bias, fox_q_norm, fox_k_norm, gdn_conv, gdn_a_log, gdn_dt_bias, gdn_out_norm, mem_norm, w_mem_kv, mem_q_norm, mem_k_norm, w_out, norm_ffn, w_gate_up, w_down, loss_target, m_norm_mix, m_w_in, m_fox_f_bias, m_fox_q_norm, m_fox_k_norm, m_gdn_conv, m_gdn_a_log, m_gdn_dt_bias, m_gdn_out_norm, m_mem_norm, m_w_mem_kv, m_mem_q_norm, m_mem_k_norm, m_w_out, m_norm_ffn, m_w_gate_up, m_w_down, v_norm_mix, v_w_in, v_fox_f_bias, v_fox_q_norm, v_fox_k_norm, v_gdn_conv, v_gdn_a_log, v_gdn_dt_bias, v_gdn_out_norm, v_mem_norm, v_w_mem_kv, v_mem_q_norm, v_mem_k_norm, v_w_out, v_norm_ffn, v_w_gate_up, v_w_down):
    args = dict(locals())
    d = x.shape[2]
    me = 4 * lax.axis_index("x") + 2 * lax.axis_index("y") + lax.axis_index("c")

    w_in_all = _all_gather("ag_w_in", _perm_in(w_in[0]).astype(BF16)).reshape(d, PC)
    w_kv_all = _all_gather("ag_w_kv", w_mem_kv[0].astype(BF16)).reshape(d, 2 * NM * HD)
    cshard = gdn_conv[0].shape[1]
    conv_pad = jnp.pad(gdn_conv[0], ((0, 4), (0, 3 * HD - cshard)))
    conv_all = _all_reduce_small("ag_conv", conv_pad, False).reshape(N_DEV, 8, 3 * HD)[:, :4, :cshard]
    conv_all = conv_all.transpose(1, 0, 2).reshape(4, N_DEV * cshard)
    comm = _StepComm([w_out[0].astype(BF16), w_gate_up[0].astype(BF16), w_down[0].astype(BF16)],
                     [w_in_all, w_kv_all, conv_all])

    grad_x, loss_local, small_grads = _local_step(
        x[0], mem[0], loss_target[0], norm_mix, fox_f_bias, fox_q_norm, fox_k_norm, gdn_a_log, gdn_dt_bias,
        gdn_out_norm, mem_norm, mem_q_norm, mem_k_norm, norm_ffn, w_in_all, w_kv_all, conv_all, comm)

    red = comm.finish([grad_x])
    grads = {"w_down": red["ffn"][0], "w_gate_up": red["ffn"][1], "w_out": red["a"][1], "w_mem_kv": red["b"][1],
             "w_in": _unperm_in(red["a"][0], red["b"][0])}
    small_grads["loss"] = jnp.broadcast_to(loss_local, (1, HD))
    packed, layout = _pack(small_grads)
    small = _unpack(_all_reduce_small("ar_small", packed, True), layout)
    loss = small["loss"][0, 0]
    six = {"fox_f_bias": L_FF, "gdn_a_log": L_GA, "gdn_dt_bias": L_GA}
    for n, rows_n in layout[:-1]:
        gsm = small[n]
        if n == "gdn_conv":
            gsm = lax.dynamic_slice(gsm.reshape(4, N_DEV * cshard), (0, me * cshard), (4, cshard))[None]
        elif n in six:
            gsm = gsm[:, six[n]:six[n] + 6]
        else:
            gsm = gsm.reshape(1, rows_n * HD)
        grads[n] = gsm

    names = ['norm_mix', 'w_in', 'fox_f_bias', 'fox_q_norm', 'fox_k_norm', 'gdn_conv', 'gdn_a_log', 'gdn_dt_bias',
             'gdn_out_norm', 'mem_norm', 'w_mem_kv', 'mem_q_norm', 'mem_k_norm', 'w_out', 'norm_ffn', 'w_gate_up', 'w_down']
    big = ("w_in", "w_mem_kv", "w_out", "w_gate_up", "w_down")
    delta, new_m, new_v = {}, {}, {}
    for n in big:
        delta[n], new_m[n], new_v[n] = [a[None] for a in _adamw_call(
            "adamw_" + n, args[n][0], grads[n], args["m_" + n][0], args["v_" + n][0])]
        grads[n] = grads[n][None]

    def flat(a):
        a = a.reshape(1, -1)
        return jnp.pad(a, ((0, 0), (0, -a.shape[1] % HD))).reshape(-1, HD)

    smalls = [n for n in names if n not in big]
    pk = lambda pre: jnp.concatenate([flat(grads[n] if pre == "g" else args[pre + n]) for n in smalls], axis=0)
    cat = [pk(""), pk("g"), pk("m_"), pk("v_")]
    padr = -cat[0].shape[0] % 8
    cat = [jnp.pad(a, ((0, padr), (0, 0))) for a in cat]
    res = _adamw_call("adamw_small", *cat)
    at = 0
    for n in smalls:
        shape = args[n].shape
        size = math.prod(shape)
        nrow = -(-size // HD)
        for dst, src in zip((delta, new_m, new_v), res):
            dst[n] = src[at:at + nrow].reshape(-1)[:size].reshape(shape)
        at += nrow

    return (loss, grad_x[None], *[grads[n] for n in names], *[delta[n] for n in names],
            *[new_m[n] for n in names], *[new_v[n] for n in names])


class _StepComm:
    def __init__(self, late_shards, after):
        self.gather = _Gather("ag_late", late_shards, after)
        self.scatters = {}

    def start_deps(self):
        return [self.gather.token]

    def after_mixers(self, after):
        self.gather.mid(after)
        return [self.gather.token]

    def late_weights(self, after):
        w_out_all, wgu, w_down_all = self.gather.end(after)
        return w_out_all.reshape(-1, w_out_all.shape[-1]), wgu, w_down_all.reshape(-1, w_down_all.shape[-1])

    def send(self, tag, grads):
        blocks = [g if g.ndim == 3 else g.reshape(N_DEV, g.shape[0] // N_DEV, g.shape[1]) for g in grads]
        self.scatters[tag] = _Scatter("rs_" + tag, blocks, ())
        return [self.scatters[tag].token]

    def mid(self, tag, after):
        self.scatters[tag].mid(after)
        return [self.scatters[tag].token]

    def finish(self, after):
        return {tag: sc.end(after) for tag, sc in self.scatters.items()}


def _local_step(xs, ms, tgt, norm_mix, fox_f_bias, fox_q_norm, fox_k_norm, gdn_a_log, gdn_dt_bias, gdn_out_norm,
                mem_norm, mem_q_norm, mem_k_norm, norm_ffn, w_in_all, w_kv_all, conv_all, comm):
    t, d = xs.shape
    bq = min(t, 256)
    fb, alog, dtb = _lanes(fox_f_bias, L_FF), _lanes(gdn_a_log, L_GA), _lanes(gdn_dt_bias, L_GA)

    rms1 = lambda a, g: (_rms(a, g),)
    (u,) = _rowwise("norm_mix", rms1, [xs], [norm_mix], [(d, BF16)], min(t, 256), deps=comm.start_deps())
    p = _matmul("proj_in", u, w_in_all, NN, F32, 1024, 768)
    o_fox = _fox_fwd(p, fb, fox_q_norm, fox_k_norm, bq)
    o_gdn_raw, gdn_saved = _gdn_fwd(p, conv_all, alog, dtb)
    zrow = (p, NG * HD, GZ * HD // (NG * HD))
    (o_gdn,) = _rowwise("gdn_post", _gdn_post, [o_gdn_raw, zrow], [gdn_out_norm], [(NG * HD, BF16)], min(t, 256))
    deps = comm.after_mixers([o_fox, o_gdn])
    (mem_n,) = _rowwise("norm_mem", rms1, [ms], [mem_norm], [(d, BF16)], ms.shape[0], deps=deps)
    mkv = _matmul("proj_mem", mem_n, w_kv_all, NN, F32, 256, 512)
    o_mem = _mem_fwd(p, mkv, mem_q_norm, mem_k_norm)
    w_out_all, wgu, w_down_all = comm.late_weights([o_mem])
    ffw = wgu.shape[2]
    mix = jnp.concatenate([o_fox, o_gdn, o_mem], axis=1)
    h1 = _matmul("proj_out", mix, w_out_all, NN, F32, 1024, 512, residual=xs)
    (h1n,) = _rowwise("norm_ffn", rms1, [h1], [norm_ffn], [(d, BF16)], min(t, 256))
    gu, act = _ffn_up(h1n, wgu.reshape(2, 4, d, ffw))
    dy, dyb, lsum = _ffn_down_loss(act, w_down_all, h1, tgt)
    loss_local = (0.5 / d) * jnp.sum(lsum[::8, ::HD])

    dgu = _ffn_down_bwd(dyb, w_down_all.reshape(4, ffw, d), gu).reshape(8, t, ffw)
    g_w_down = _matmul("grad_w_down", act, dyb, TN, BF16, 512, 512)
    dh1n = _ffn_up_bwd_x(dgu, wgu)
    g_w_gu = _ffn_up_bwd_w(h1n, dgu)
    deps = comm.send("ffn", [g_w_down, g_w_gu])
    rms2 = lambda a, g: (_rms(a, g), a)
    dh1, g_norm_ffn = _rowwise_vjp("norm_ffn_bwd", rms2, [h1], [norm_ffn], [dh1n, dy], [F32], min(t, 256), deps=deps)
    dh1b = dh1.astype(BF16)

    dmix = _matmul("proj_out_bwd_x", dh1b, w_out_all, NT, F32, 1024, 512)
    g_w_out = _matmul("grad_w_out", mix, dh1b, TN, BF16, 512, 512)
    deps = comm.mid("ffn", [dmix, g_w_out])
    do_raw, dgz, g_gon = _rowwise_vjp("gdn_post_bwd", _gdn_post, [o_gdn_raw, zrow], [gdn_out_norm],
                                      [(dmix, NG * HD, 1)], [F32, BF16], min(t, 256), deps=deps)
    dterms = _gdn_bwd_scan(gdn_saved, do_raw)
    dgq, dgk, dgv, dsm_gdn, dwq, dwk, dwv, g_alog, g_dtb = _gdn_bwd(p, conv_all, alog, dtb, dterms)
    dp_a = jnp.concatenate([dgq, dgk, dgv, dgz], axis=1)
    g_w_in_a = _matmul("grad_w_in_a", u, dp_a, TN, BF16, 512, 768)
    deps = comm.send("a", [g_w_in_a, g_w_out])
    dmq, dmk, dmv, g_mqn, g_mkn = _mem_bwd(p, mkv, mem_q_norm, mem_k_norm, dmix, deps=deps)
    dmkv = jnp.concatenate([dmk, dmv], axis=1).astype(BF16)
    dmem_n = _matmul("proj_mem_bwd_x", dmkv, w_kv_all, NT, F32, 256, 512)
    g_w_kv = _matmul("grad_w_kv", mem_n, dmkv, TN, BF16, 512, 512)
    g_mem_norm = _rowwise_vjp("norm_mem_bwd", rms1, [ms], [mem_norm], [dmem_n], [], ms.shape[0])[0]
    deps = comm.mid("a", [g_mem_norm, g_w_kv])
    dfq, dfk, dfv, dsm_fox, g_fb, g_fqn, g_fkn = _fox_bwd(p, fb, fox_q_norm, fox_k_norm, dmix, bq, deps=deps)
    dp_b = jnp.concatenate([dfq, dfk, dfv, dmq, (dsm_fox + dsm_gdn).astype(BF16), jnp.zeros((t, HD), BF16)], axis=1)
    g_w_in_b = _matmul("grad_w_in_b", u, dp_b, TN, BF16, 512, 768)
    deps = comm.send("b", [g_w_in_b, g_w_kv])
    du_a = _matmul("proj_in_bwd_a", dp_a, w_in_all, NT, F32, 512, 512, kidx=0, deps=deps)
    deps = comm.mid("b", [du_a])
    du = _matmul("proj_in_bwd_b", dp_b, w_in_all, NT, F32, 512, 512, kidx=1, residual=du_a, deps=deps)
    grad_x, g_norm_mix = _rowwise_vjp("norm_mix_bwd", rms2, [xs], [norm_mix], [du, dh1], [F32], min(t, 256))

    small_grads = {
        "norm_mix": g_norm_mix, "mem_norm": g_mem_norm, "norm_ffn": g_norm_ffn,
        "gdn_conv": jnp.concatenate([dwq, dwk, dwv], axis=1),
        "fox_q_norm": g_fqn, "fox_k_norm": g_fkn, "gdn_out_norm": g_gon, "mem_q_norm": g_mqn, "mem_k_norm": g_mkn,
        "fox_f_bias": g_fb, "gdn_a_log": g_alog, "gdn_dt_bias": g_dtb}
    return grad_x, loss_local, small_grads
```

```python
import functools
import math

import jax
import jax.numpy as jnp
from jax import lax
from jax.experimental import pallas as pl
from jax.experimental.pallas import tpu as pltpu

F32 = jnp.float32
BF16 = jnp.bfloat16
HI = lax.Precision.HIGHEST
SDS = jax.ShapeDtypeStruct

N_DEV = 8
HD = 128
NF, NG, NM = 6, 6, 4
CHUNK = 64
GROUP = 4
NORM_EPS = 1e-6
GQ, GK, GV, GZ = 0, 6, 12, 18
FQ, FK, FV, MQ, SM = 0, 6, 12, 18, 22
HALF = 24 * HD
L_FF, L_GA, L_GB = 0, 6, 12
VMEM_LIMIT = 56 * 1024 * 1024

ADAM_LR, ADAM_B1, ADAM_B2, ADAM_EPS, ADAM_WD, ADAM_STEP = 0.001, 0.9, 0.999, 1e-08, 0.01, 10

NN = (((1,), (0,)), ((), ()))
NT = (((1,), (1,)), ((), ()))
TN = (((0,), (0,)), ((), ()))
MESH = pl.DeviceIdType.MESH


def _cp(*sem):
    return pltpu.CompilerParams(dimension_semantics=tuple(sem) if sem else None, vmem_limit_bytes=VMEM_LIMIT)


def _dot(a, b, dims=NN):
    return lax.dot_general(a, b, dims, preferred_element_type=F32)


def _bdot(a, b):
    return _dot(a.astype(BF16), b.astype(BF16))


def _iota(shape, axis):
    return lax.broadcasted_iota(jnp.int32, shape, axis)


def _rms(x, gain):
    return x * lax.rsqrt(jnp.mean(x * x, axis=-1, keepdims=True) + NORM_EPS) * gain


def _sigmoid(x):
    z = jnp.exp(-jnp.abs(x))
    return jnp.where(x >= 0, 1.0 / (1.0 + z), z / (1.0 + z))


def _silu(x):
    return x * _sigmoid(x)


def _softplus(x):
    return jnp.maximum(x, 0.0) + jnp.log(1.0 + jnp.exp(-jnp.abs(x)))


def _lane_pick(x, lane):
    oh = (_iota((1, x.shape[-1]), 1) == lane).astype(F32)
    return jnp.sum(x * oh, axis=-1, keepdims=True)


def _cumsum_rows(x):
    tril = (_iota((HD, HD), 0) >= _iota((HD, HD), 1)).astype(F32)
    carry = jnp.zeros((1, x.shape[1]), F32)
    outs = []
    for b in range(x.shape[0] // HD):
        blk = x[b * HD:(b + 1) * HD]
        outs.append(jnp.dot(tril, blk, precision=HI, preferred_element_type=F32) + carry)
        carry = carry + jnp.sum(blk, axis=0, keepdims=True)
    return jnp.concatenate(outs, axis=0)


def _row_spec(r, tm):
    if isinstance(r, tuple):
        arr, width, cb = r
        return arr, pl.BlockSpec((tm, width), lambda i, cb=cb: (i, cb))
    return r, pl.BlockSpec((tm, r.shape[1]), lambda i: (i, 0))


ANY_SPEC = pl.BlockSpec(memory_space=pl.ANY)


def _rowwise(name, fn, rows, consts, outs, tm, deps=()):
    arrs, specs = zip(*[_row_spec(r, tm) for r in rows])
    n_rows = arrs[0].shape[0]
    nr, nc, nd = len(rows), len(consts), len(deps)

    def body(*refs):
        res = fn(*[r[...] for r in refs[:nr + nc]])
        for o, v in zip(refs[nr + nc + nd:], res):
            o[...] = v.astype(o.dtype)

    return pl.pallas_call(
        body, grid=(n_rows // tm,), name=name,
        in_specs=list(specs) + [pl.BlockSpec(c.shape, lambda i: (0, 0)) for c in consts] + [ANY_SPEC] * nd,
        out_specs=[pl.BlockSpec((tm, w), lambda i: (i, 0)) for w, _ in outs],
        out_shape=[SDS((n_rows, w), dt) for w, dt in outs],
        compiler_params=_cp("parallel"),
    )(*arrs, *consts, *deps)


def _rowwise_vjp(name, fn, rows, consts, cts, grad_dtypes, tm, deps=()):
    arrs, specs = zip(*[_row_spec(r, tm) for r in rows])
    ct_arrs, ct_specs = zip(*[_row_spec(r, tm) for r in cts])
    n_rows = arrs[0].shape[0]
    nr, nc, nct, ng, nd = len(rows), len(consts), len(cts), len(grad_dtypes), len(deps)
    widths = [s.block_shape[1] for s in specs[:ng]]

    def body(*refs):
        vals = [r[...].astype(F32) for r in refs[:nr + nc]]
        ctv = tuple(r[...].astype(F32) for r in refs[nr + nc:nr + nc + nct])
        _, vjp = jax.vjp(fn, *vals)
        grads = vjp(ctv)
        outs = refs[nr + nc + nct + nd:]
        for o, g in zip(outs[:ng], grads[:ng]):
            o[...] = g.astype(o.dtype)

        @pl.when(pl.program_id(0) == 0)
        def _():
            for o in outs[ng:]:
                o[...] = jnp.zeros_like(o)

        for o, g in zip(outs[ng:], grads[nr:]):
            o[...] += g

    return pl.pallas_call(
        body, grid=(n_rows // tm,), name=name,
        in_specs=list(specs) + [pl.BlockSpec(c.shape, lambda i: (0, 0)) for c in consts] + list(ct_specs)
        + [ANY_SPEC] * nd,
        out_specs=[pl.BlockSpec((tm, w), lambda i: (i, 0)) for w in widths]
        + [pl.BlockSpec(c.shape, lambda i: (0, 0)) for c in consts],
        out_shape=[SDS((n_rows, w), dt) for w, dt in zip(widths, grad_dtypes)] + [SDS(c.shape, F32) for c in consts],
        compiler_params=_cp("arbitrary"),
    )(*arrs, *consts, *ct_arrs, *deps)


def _tile(n, pref):
    t = min(n, pref)
    while n % t or (t % HD and t != n):
        t -= 1
    return t


def _matmul(name, a, b, dims, out_dtype, tm, tn, residual=None, deps=()):
    ta, tb = dims == TN, dims == NT
    m = a.shape[1] if ta else a.shape[0]
    k = a.shape[0] if ta else a.shape[1]
    n = b.shape[0] if tb else b.shape[1]
    tm, tn = _tile(m, tm), _tile(n, tn)

    def body(*refs):
        acc = _dot(refs[0][...], refs[1][...], dims)
        if residual is not None:
            acc = acc + refs[2][...]
        refs[-1][...] = acc.astype(out_dtype)

    in_specs = [pl.BlockSpec((k, tm), lambda i, j: (0, i)) if ta else pl.BlockSpec((tm, k), lambda i, j: (i, 0)),
                pl.BlockSpec((tn, k), lambda i, j: (j, 0)) if tb else pl.BlockSpec((k, tn), lambda i, j: (0, j))]
    ops = [a, b]
    if residual is not None:
        in_specs.append(pl.BlockSpec((tm, tn), lambda i, j: (i, j)))
        ops.append(residual)
    in_specs += [ANY_SPEC] * len(deps)
    ops += list(deps)
    return pl.pallas_call(
        body, grid=(m // tm, n // tn), name=name, in_specs=in_specs,
        out_specs=pl.BlockSpec((tm, tn), lambda i, j: (i, j)), out_shape=SDS((m, n), out_dtype),
        compiler_params=_cp("parallel", "parallel"),
    )(*ops)


def _ffn_up(h1n, wgu):
    t, d = h1n.shape
    w = wgu.shape[3]
    tm = _tile(t, 512)

    def body(a, b, gu, act):
        x = a[...]
        g = _dot(x, b[0])
        u = _dot(x, b[1])
        gu[0] = g.astype(BF16)
        gu[1] = u.astype(BF16)
        act[...] = (_silu(g) * u).astype(BF16)

    return pl.pallas_call(
        body, grid=(4, t // tm), name="ffn_up",
        in_specs=[pl.BlockSpec((tm, d), lambda j, i: (i, 0)), pl.BlockSpec((2, None, d, w), lambda j, i: (0, j, 0, 0))],
        out_specs=[pl.BlockSpec((2, None, tm, w), lambda j, i: (0, j, i, 0)), pl.BlockSpec((tm, w), lambda j, i: (i, j))],
        out_shape=[SDS((2, 4, t, w), BF16), SDS((t, 4 * w), BF16)],
        compiler_params=_cp("parallel", "parallel"),
    )(h1n, wgu)


def _ffn_down_loss(act, wdown, h1, target):
    t, f = act.shape
    d = wdown.shape[1]
    tm, tn = _tile(t, 512), _tile(d, 512)

    def body(a, b, h, tg, dy, dyb, ls):
        e = _dot(a[...], b[...]) + h[...] - tg[...]
        g = e * (1.0 / d)
        dy[...] = g
        dyb[...] = g.astype(BF16)
        ls[...] = jnp.broadcast_to(jnp.sum(e * e), (8, HD))

    return pl.pallas_call(
        body, grid=(t // tm, d // tn), name="ffn_down_loss",
        in_specs=[pl.BlockSpec((tm, f), lambda i, j: (i, 0)), pl.BlockSpec((f, tn), lambda i, j: (0, j)),
                  pl.BlockSpec((tm, tn), lambda i, j: (i, j)), pl.BlockSpec((tm, tn), lambda i, j: (i, j))],
        out_specs=[pl.BlockSpec((tm, tn), lambda i, j: (i, j)), pl.BlockSpec((tm, tn), lambda i, j: (i, j)),
                   pl.BlockSpec((8, HD), lambda i, j: (i, j))],
        out_shape=[SDS((t, d), F32), SDS((t, d), BF16), SDS((8 * (t // tm), HD * (d // tn)), F32)],
        compiler_params=_cp("parallel", "parallel"),
    )(act, wdown, h1, target)


def _ffn_down_bwd(dyb, wdown4, gu):
    t, d = dyb.shape
    w = wdown4.shape[1]
    tm = _tile(t, 512)

    def body(a, b, gu_ref, out):
        da = _dot(a[...], b[...], NT)
        g = gu_ref[0].astype(F32)
        u = gu_ref[1].astype(F32)
        s = _sigmoid(g)
        out[0] = (da * u * (s * (1.0 + g * (1.0 - s)))).astype(BF16)
        out[1] = (da * g * s).astype(BF16)

    return pl.pallas_call(
        body, grid=(4, t // tm), name="ffn_down_bwd",
        in_specs=[pl.BlockSpec((tm, d), lambda j, i: (i, 0)), pl.BlockSpec((None, w, d), lambda j, i: (j, 0, 0)),
                  pl.BlockSpec((2, None, tm, w), lambda j, i: (0, j, i, 0))],
        out_specs=pl.BlockSpec((2, None, tm, w), lambda j, i: (0, j, i, 0)),
        out_shape=SDS((2, 4, t, w), BF16),
        compiler_params=_cp("parallel", "parallel"),
    )(dyb, wdown4, gu)


def _ffn_up_bwd_x(dgu, wgu):
    _, t, w = dgu.shape
    d = wgu.shape[1]
    tm = _tile(t, 512)

    def body(a, b, out):
        @pl.when(pl.program_id(1) == 0)
        def _():
            out[...] = jnp.zeros_like(out)
        out[...] += _dot(a[...], b[...], NT)

    return pl.pallas_call(
        body, grid=(t // tm, 8), name="ffn_up_bwd_x",
        in_specs=[pl.BlockSpec((None, tm, w), lambda i, j: (j, i, 0)), pl.BlockSpec((None, d, w), lambda i, j: (j, 0, 0))],
        out_specs=pl.BlockSpec((tm, d), lambda i, j: (i, 0)), out_shape=SDS((t, d), F32),
        compiler_params=_cp("parallel", "arbitrary"),
    )(dgu, wgu)


def _ffn_up_bwd_w(h1n, dgu):
    _, t, w = dgu.shape
    d = h1n.shape[1]
    tm = _tile(d, 512)

    def body(a, b, out):
        out[...] = _dot(a[...], b[...], TN).astype(BF16)

    return pl.pallas_call(
        body, grid=(8, d // tm), name="ffn_up_bwd_w",
        in_specs=[pl.BlockSpec((t, tm), lambda j, i: (0, i)), pl.BlockSpec((None, t, w), lambda j, i: (j, 0, 0))],
        out_specs=pl.BlockSpec((None, tm, w), lambda j, i: (j, i, 0)), out_shape=SDS((8, d, w), BF16),
        compiler_params=_cp("parallel", "parallel"),
    )(h1n, dgu)


def _fox_prep(fq, fk, sm, fb, qg, kg, h):
    qn = _rms(fq, qg)
    kn = _rms(fk, kg)
    c = _cumsum_rows(-_softplus(-(sm + fb)))
    ccol = _lane_pick(c, L_FF + h)
    crow = jnp.sum(c.T * (_iota((HD, 1), 0) == L_FF + h).astype(F32), axis=0, keepdims=True)
    return qn, kn, ccol, crow


def _fox_block(q, k, v, cc, cr, off):
    s = _dot(q.astype(BF16), k.astype(BF16), NT) * (HD ** -0.5) + cc - cr
    s = jnp.where(_iota(s.shape, 1) <= _iota(s.shape, 0) + off, s, -1e30)
    e = jnp.exp(s - lax.stop_gradient(jnp.max(s, axis=1, keepdims=True)))
    p = e / jnp.sum(e, axis=1, keepdims=True)
    return _dot(p.astype(BF16), v.astype(BF16))


ONE_BUFFER = pl.Buffered(1)


def _pcol(t, cb):
    return pl.BlockSpec((t, HD), lambda h, cb=cb: (0, cb + h), pipeline_mode=ONE_BUFFER)


def _smcol(t):
    return pl.BlockSpec((t, HD), lambda h: (0, SM), pipeline_mode=ONE_BUFFER)


def _head(t):
    return pl.BlockSpec((t, HD), lambda h: (0, h), pipeline_mode=ONE_BUFFER)


def _small(n):
    return pl.BlockSpec((n, HD), lambda h: (0, 0), pipeline_mode=ONE_BUFFER)


def _fox_fwd(p, fb, qg, kg, bq):
    t = p.shape[0]

    def body(fq, fk, fv, sm, fb_r, qg_r, kg_r, o, qn_s, cc_s):
        h = pl.program_id(0)
        qn, kn, ccol, crow = _fox_prep(fq[...], fk[...], sm[...], fb_r[...], qg_r[...], kg_r[...], h)
        qn_s[...] = qn
        cc_s[...] = ccol
        knb = kn.astype(BF16)
        vb = fv[...].astype(BF16)
        for i in range(t // bq):
            rows, ext = pl.ds(i * bq, bq), (i + 1) * bq
            o[rows, :] = _fox_block(qn_s[rows, :], knb[:ext], vb[:ext], cc_s[rows, :], crow[:, :ext], i * bq).astype(o.dtype)

    return pl.pallas_call(
        body, grid=(NF,), name="fox_fwd",
        in_specs=[_pcol(t, FQ), _pcol(t, FK), _pcol(t, FV), _smcol(t), _small(1), _small(1), _small(1)],
        out_specs=_head(t), out_shape=SDS((t, NF * HD), BF16),
        scratch_shapes=[pltpu.VMEM((t, HD), F32), pltpu.VMEM((t, 1), F32)],
        compiler_params=_cp("parallel"),
    )(p, p, p, p, fb, qg, kg)


def _fox_bwd(p, fb, qg, kg, dmix, bq, deps=()):
    t = p.shape[0]

    def body(*refs):
        fq, fk, fv, sm, fb_r, qg_r, kg_r, do = refs[:8]
        dfq, dfk, dfv, dsm, dfb, dqg, dkg, qn_s, cc_s, dqn_s, dcc_s, dkn_s, dv_s, dcr_s = refs[8 + len(deps):]
        h = pl.program_id(0)
        qn, kn, ccol, crow = _fox_prep(fq[...], fk[...], sm[...], fb_r[...], qg_r[...], kg_r[...], h)
        qn_s[...] = qn
        cc_s[...] = ccol
        v = fv[...]
        dkn_s[...] = jnp.zeros_like(dkn_s)
        dv_s[...] = jnp.zeros_like(dv_s)
        dcr_s[...] = jnp.zeros_like(dcr_s)

        for i in range(t // bq):
            rows, ext = pl.ds(i * bq, bq), (i + 1) * bq
            _, vjp = jax.vjp(lambda a, b, c, d, e, off=i * bq: _fox_block(a, b, c, d, e, off),
                             qn_s[rows, :], kn[:ext], v[:ext], cc_s[rows, :], crow[:, :ext])
            dq, dk, dv, dcc, dcr = vjp(do[rows, :])
            dqn_s[rows, :] = dq
            dcc_s[rows, :] = dcc
            dkn_s[:ext, :] += dk
            dv_s[:ext, :] += dv
            dcr_s[:, :ext] += dcr
        _, prep_vjp = jax.vjp(lambda a, b, c, d, e, f: _fox_prep(a, b, c, d, e, f, h),
                              fq[...], fk[...], sm[...], fb_r[...], qg_r[...], kg_r[...])
        g_fq, g_fk, g_sm, g_fb, g_qg, g_kg = prep_vjp((dqn_s[...], dkn_s[...], dcc_s[...], dcr_s[...]))
        dfq[...] = g_fq.astype(dfq.dtype)
        dfk[...] = g_fk.astype(dfk.dtype)
        dfv[...] = dv_s[...].astype(dfv.dtype)

        @pl.when(h == 0)
        def _():
            for r in (dsm, dfb, dqg, dkg):
                r[...] = jnp.zeros_like(r)

        dsm[...] += g_sm
        dfb[...] += g_fb
        dqg[...] += g_qg
        dkg[...] += g_kg

    head = _head(t)
    return pl.pallas_call(
        body, grid=(NF,), name="fox_bwd",
        in_specs=[_pcol(t, FQ), _pcol(t, FK), _pcol(t, FV), _smcol(t), _small(1), _small(1), _small(1), head]
        + [ANY_SPEC] * len(deps),
        out_specs=[head, head, head, _small(t), _small(1), _small(1), _small(1)],
        out_shape=[SDS((t, NF * HD), BF16)] * 3 + [SDS((t, HD), F32)] + [SDS((1, HD), F32)] * 3,
        scratch_shapes=[pltpu.VMEM((t, HD), F32), pltpu.VMEM((t, 1), F32), pltpu.VMEM((t, HD), F32),
                        pltpu.VMEM((t, 1), F32), pltpu.VMEM((t, HD), F32), pltpu.VMEM((t, HD), F32),
                        pltpu.VMEM((1, t), F32)],
        compiler_params=_cp("arbitrary"),
    )(p, p, p, p, fb, qg, kg, dmix, *deps)


def _mem_attn(mq, mk, mv, qg, kg):
    s = _dot(_rms(mq, qg).astype(BF16), _rms(mk, kg).astype(BF16), NT) * (HD ** -0.5)
    e = jnp.exp(s - lax.stop_gradient(jnp.max(s, axis=1, keepdims=True)))
    p = e / jnp.sum(e, axis=1, keepdims=True)
    return _dot(p.astype(BF16), mv.astype(BF16))


def _mem_fwd(p, mkv, qg, kg):
    t, ml = p.shape[0], mkv.shape[0]

    def body(mq, mk, mv, qg_r, kg_r, o):
        o[...] = _mem_attn(mq[...], mk[...], mv[...], qg_r[...], kg_r[...]).astype(o.dtype)

    return pl.pallas_call(
        body, grid=(NM,), name="mem_fwd",
        in_specs=[_pcol(t, MQ), pl.BlockSpec((ml, HD), lambda h: (0, h)), pl.BlockSpec((ml, HD), lambda h: (0, NM + h)),
                  _small(1), _small(1)],
        out_specs=pl.BlockSpec((t, HD), lambda h: (0, h)), out_shape=SDS((t, NM * HD), BF16),
        compiler_params=_cp("parallel"),
    )(p, mkv, mkv, qg, kg)


def _mem_bwd(p, mkv, qg, kg, dmix, deps=()):
    t, ml = p.shape[0], mkv.shape[0]

    def body(*refs):
        mq, mk, mv, qg_r, kg_r, do = refs[:6]
        dmq, dmk, dmv, dqg, dkg = refs[6 + len(deps):]
        _, vjp = jax.vjp(_mem_attn, mq[...], mk[...], mv[...], qg_r[...], kg_r[...])
        g_q, g_k, g_v, g_qg, g_kg = vjp(do[...])
        dmq[...] = g_q.astype(dmq.dtype)
        dmk[...] = g_k
        dmv[...] = g_v

        @pl.when(pl.program_id(0) == 0)
        def _():
            dqg[...] = jnp.zeros_like(dqg)
            dkg[...] = jnp.zeros_like(dkg)

        dqg[...] += g_qg
        dkg[...] += g_kg

    return pl.pallas_call(
        body, grid=(NM,), name="mem_bwd",
        in_specs=[_pcol(t, MQ), pl.BlockSpec((ml, HD), lambda h: (0, h)), pl.BlockSpec((ml, HD), lambda h: (0, NM + h)),
                  _small(1), _small(1), pl.BlockSpec((t, HD), lambda h: (0, NF + NG + h))] + [ANY_SPEC] * len(deps),
        out_specs=[pl.BlockSpec((t, HD), lambda h: (0, h)), pl.BlockSpec((ml, HD), lambda h: (0, h)),
                   pl.BlockSpec((ml, HD), lambda h: (0, h)), _small(1), _small(1)],
        out_shape=[SDS((t, NM * HD), BF16), SDS((ml, NM * HD), F32), SDS((ml, NM * HD), F32),
                   SDS((1, HD), F32), SDS((1, HD), F32)],
        compiler_params=_cp("arbitrary"),
    )(p, mkv, mkv, qg, kg, dmix, *deps)


def _shift_down(x, s):
    if s == 0:
        return x
    return jnp.where(_iota(x.shape, 0) >= s, pltpu.roll(x, s, 0), 0.0)


def _shift_up(x, s):
    if s == 0:
        return x
    n = x.shape[0]
    return jnp.where(_iota(x.shape, 0) < n - s, pltpu.roll(x, n - s, 0), 0.0)


@jax.custom_vjp
def _conv4(x, w0, w1, w2, w3):
    return w0 * _shift_down(x, 3) + w1 * _shift_down(x, 2) + w2 * _shift_down(x, 1) + w3 * x


def _conv4_fwd(x, w0, w1, w2, w3):
    return _conv4(x, w0, w1, w2, w3), (x, w0, w1, w2, w3)


def _conv4_bwd(res, dy):
    x, w0, w1, w2, w3 = res
    dx = w0 * _shift_up(dy, 3) + w1 * _shift_up(dy, 2) + w2 * _shift_up(dy, 1) + w3 * dy
    dws = tuple(jnp.sum(dy * _shift_down(x, 3 - k), axis=0, keepdims=True) for k in range(4))
    return (dx,) + dws


_conv4.defvjp(_conv4_fwd, _conv4_bwd)


def _gdn_prep(gq, gk, gv, sm, taps, alog, dtb, h):
    q, k, v = [_silu(_conv4(x, *taps[4 * j:4 * j + 4])) for j, x in enumerate((gq, gk, gv))]
    q = q * lax.rsqrt(jnp.sum(q * q, axis=-1, keepdims=True) + NORM_EPS) * (HD ** -0.5)
    k = k * lax.rsqrt(jnp.sum(k * k, axis=-1, keepdims=True) + NORM_EPS)
    g = _lane_pick(-jnp.exp(alog) * _softplus(sm + dtb), L_GA + h)
    beta = _lane_pick(_sigmoid(sm), L_GB + h)
    return q, k, v, g, beta


def _split(x, n):
    parts, rest = [], x
    for i in range(n):
        parts.append(rest.astype(BF16))
        if i + 1 < n:
            rest = rest - parts[-1].astype(F32)
    return parts


def _raw_dot(a, b, form):
    lead = a.ndim - 2
    ca, cb = {"nn": (1, 0), "nt": (1, 1), "tn": (0, 0)}[form]
    batch = ((0,), (0,)) if lead else ((), ())
    return lax.dot_general(a, b, (((ca + lead,), (cb + lead,)), batch), preferred_element_type=F32)


def _pdot_impl(a, b, form, mode):
    if mode == "1":
        return _raw_dot(a.astype(BF16), b.astype(BF16), form)
    if mode == "3":
        (ah, al), (bh, bl) = _split(a, 2), _split(b, 2)
        return _raw_dot(ah, bh, form) + (_raw_dot(al, bh, form) + _raw_dot(ah, bl, form))
    if mode == "xa":
        return sum(_raw_dot(a.astype(BF16), t, form) for t in reversed(_split(b, 3)))
    return sum(_raw_dot(t, b.astype(BF16), form) for t in reversed(_split(a, 3)))


@functools.partial(jax.custom_vjp, nondiff_argnums=(2, 3))
def _pdot(a, b, form, mode):
    return _pdot_impl(a, b, form, mode)


def _pdot_fwd(a, b, form, mode):
    return _pdot_impl(a, b, form, mode), (a, b)


def _pdot_bwd(form, mode, res, ct):
    a, b = res
    da_args, db_args = {"nn": ((ct, b, "nt"), (a, ct, "tn")), "nt": ((ct, b, "nn"), (ct, a, "tn")),
                        "tn": ((b, ct, "nt"), (a, ct, "nn"))}[form]

    def side(args, exact):
        if mode in ("1", "3"):
            return mode
        return "xa" if args[0] is exact else "xb"

    if mode == "xa":
        return jnp.zeros_like(a), _pdot_impl(*db_args, side(db_args, a))
    if mode == "xb":
        return _pdot_impl(*da_args, side(da_args, b)), jnp.zeros_like(b)
    return _pdot_impl(*da_args, mode), _pdot_impl(*db_args, mode)


_pdot.defvjp(_pdot_fwd, _pdot_bwd)

GDN_QK, GDN_INV, GDN_SCAN = "1", "1", "1"


@jax.custom_vjp
def _tri_inv(low):
    eye = (_iota((CHUNK, CHUNK), 0) == _iota((CHUNK, CHUNK), 1)).astype(F32)
    inv = eye - low
    pw = low
    for _ in range(5):
        pw = _pdot_impl(pw, pw, "nn", GDN_INV)
        inv = inv + _pdot_impl(inv, pw, "nn", GDN_INV)
    return inv


def _tri_inv_fwd(low):
    inv = _tri_inv(low)
    return inv, inv


def _tri_inv_bwd(inv, ct):
    return (-_pdot_impl(_pdot_impl(inv, ct, "tn", GDN_INV), inv, "nt", GDN_INV),)


_tri_inv.defvjp(_tri_inv_fwd, _tri_inv_bwd)


def _gdn_intra(q, k, v, g, beta):
    n = q.shape[0]
    r, c = _iota((CHUNK, CHUNK), 0), _iota((CHUNK, CHUNK), 1)
    tril, strict = r >= c, r > c
    trilf = jnp.broadcast_to(tril.astype(F32), (n, CHUNK, CHUNK))
    gcm = _pdot(trilf, jnp.broadcast_to(g, (n, CHUNK, CHUNK)), "nn", "xa")
    gcf = _pdot(trilf, jnp.broadcast_to(g, (n, CHUNK, HD)), "nn", "xa")
    lane0 = (_iota((1, 1, CHUNK), 2) == 0).astype(F32)
    gcr = _pdot(jnp.ones((n, CHUNK, CHUNK), F32), gcm * lane0, "nt", "xa")
    decay = jnp.where(tril, jnp.exp(jnp.where(tril, gcm - gcr, 0.0)), 0.0)
    egc = jnp.exp(gcf)
    kb = k * beta
    low = jnp.where(strict, _pdot(kb, k, "nt", GDN_QK) * decay, 0.0)
    inv = _tri_inv(low)
    u = _pdot(inv, v * beta, "nn", GDN_INV)
    w = _pdot(inv, kb * egc, "nn", GDN_INV)
    at = jnp.where(tril, _pdot(q, k, "nt", GDN_QK) * decay, 0.0)
    gl = jnp.sum(jnp.broadcast_to(g, (n, CHUNK, HD)), axis=1, keepdims=True)
    return u, w, q * egc, at, k * jnp.exp(gl - gcf), gl


def _gdn_step(s, u, w, qg, at, kd, gl):
    vn = u - _pdot(w, s, "nn", GDN_SCAN)
    o = _pdot(qg, s, "nn", GDN_SCAN) + _pdot(at, vn, "nn", GDN_SCAN)
    s2 = s * jnp.exp(gl) + _pdot(kd, vn, "tn", GDN_SCAN)
    return o, s2


SCAN_HEADS = 3


def _gdn_chunked_scratch(nc):
    big = pltpu.VMEM((nc, CHUNK, HD), F32)
    return [big, big, big, pltpu.VMEM((nc, CHUNK, 1), F32), pltpu.VMEM((nc, CHUNK, 1), F32)]


def _gdn_term_shapes(nc):
    return [(nc, CHUNK, HD), (nc, CHUNK, HD), (nc, CHUNK, HD), (nc, CHUNK, CHUNK), (nc, CHUNK, HD), (nc, 1, HD)]


def _per_head(shape, heads=None):
    lead = (None,) if heads is None else (heads,)
    return pl.BlockSpec(lead + tuple(shape), lambda h: (h,) + (0,) * len(shape), pipeline_mode=ONE_BUFFER)


def _gdn_in_specs(t):
    cw = lambda cb: pl.BlockSpec((4, HD), lambda h, cb=cb: (0, cb + h))
    return [_pcol(t, GQ), _pcol(t, GK), _pcol(t, GV), _smcol(t), cw(0), cw(NG), cw(2 * NG), _small(1), _small(1)]


def _taps(wq, wk, wv):
    return tuple(w[k:k + 1, :] for w in (wq, wk, wv) for k in range(4))


def _gdn_stage(vals, refs):
    nc = refs[0].shape[0]
    for v, r in zip(vals, refs):
        r[...] = v.reshape(nc, CHUNK, v.shape[-1])


def _gdn_intra_all(chunked, intra):
    nc = chunked[0].shape[0]
    grp_n = math.gcd(nc, GROUP)

    def grp(i, carry):
        sl = pl.ds(pl.multiple_of(i * grp_n, grp_n), grp_n)
        for r, val in zip(intra, _gdn_intra(*[c[sl] for c in chunked])):
            r[sl] = val
        return carry

    lax.fori_loop(0, nc // grp_n, grp, 0)


def _gdn_fwd(pa, pb, conv, alog, dtb):
    t = pa.shape[0]
    nc = t // CHUNK
    terms = _gdn_term_shapes(nc)

    def body(gq, gk, gv, sm, wq, wk, wv, al, db, o, *rest):
        h = pl.program_id(0)
        intra, states, chunked = rest[:6], rest[6], rest[7:]
        _gdn_stage(_gdn_prep(gq[...], gk[...], gv[...], sm[...], _taps(wq, wk, wv), al[...], db[...], h), chunked)
        _gdn_intra_all(chunked, intra)

        def step(c, s):
            states[c] = s
            oc, s2 = _gdn_step(s, *[r[c] for r in intra])
            o[pl.ds(pl.multiple_of(c * CHUNK, CHUNK), CHUNK), :] = oc
            return s2

        lax.fori_loop(0, nc, step, jnp.zeros((HD, HD), F32))

    outs = pl.pallas_call(
        body, grid=(NG,), name="gdn_fwd", in_specs=_gdn_in_specs(t),
        out_specs=[_head(t)] + [_per_head(sh) for sh in terms] + [_per_head((nc, HD, HD))],
        out_shape=[SDS((t, NG * HD), F32)] + [SDS((NG,) + sh, F32) for sh in terms] + [SDS((NG, nc, HD, HD), F32)],
        scratch_shapes=_gdn_chunked_scratch(nc), compiler_params=_cp("parallel"),
    )(pa, pa, pa, pb, conv, conv, conv, alog, dtb)
    return outs[0], list(outs[1:])


def _gdn_bwd_scan(saved, do_raw):
    nc = saved[0].shape[1]
    terms = _gdn_term_shapes(nc)

    def body(*refs):
        intra, states, do, outs = refs[:6], refs[6], refs[7], refs[8:]

        def bwd(i, dss):
            c = nc - 1 - i
            rows = pl.ds(pl.multiple_of(c * CHUNK, CHUNK), CHUNK)
            new = []
            for hh in range(SCAN_HEADS):
                _, vjp = jax.vjp(_gdn_step, states[hh, c], *[r[hh, c] for r in intra])
                grads = vjp((do[rows, hh * HD:(hh + 1) * HD], dss[hh]))
                for r, gval in zip(outs, grads[1:]):
                    r[hh, c] = gval
                new.append(grads[0])
            return tuple(new)

        lax.fori_loop(0, nc, bwd, tuple(jnp.zeros((HD, HD), F32) for _ in range(SCAN_HEADS)))

    return pl.pallas_call(
        body, grid=(NG // SCAN_HEADS,), name="gdn_bwd_scan",
        in_specs=[_per_head(sh, SCAN_HEADS) for sh in terms] + [_per_head((nc, HD, HD), SCAN_HEADS)]
        + [pl.BlockSpec((nc * CHUNK, SCAN_HEADS * HD), lambda h: (0, h), pipeline_mode=ONE_BUFFER)],
        out_specs=[_per_head(sh, SCAN_HEADS) for sh in terms],
        out_shape=[SDS((NG,) + sh, F32) for sh in terms], compiler_params=_cp("parallel"),
    )(*saved, do_raw)


def _gdn_bwd(pa, pb, conv, alog, dtb, dterms):
    t = pa.shape[0]
    nc = t // CHUNK
    terms = _gdn_term_shapes(nc)

    def body(*refs):
        gq, gk, gv, sm, wq, wk, wv, al, db = refs[:9]
        dintra = refs[9:15]
        dgq, dgk, dgv, dsm, dwq, dwk, dwv, dal, ddb = refs[15:24]
        chunked = refs[24:]
        h = pl.program_id(0)
        _gdn_stage(_gdn_prep(gq[...], gk[...], gv[...], sm[...], _taps(wq, wk, wv), al[...], db[...], h), chunked)
        grp_n = math.gcd(nc, GROUP)

        def grp(i, carry):
            sl = pl.ds(pl.multiple_of(i * grp_n, grp_n), grp_n)
            _, vjp = jax.vjp(_gdn_intra, *[r[sl] for r in chunked])
            for r, gval in zip(chunked, vjp(tuple(r[sl] for r in dintra))):
                r[sl] = gval
            return carry

        lax.fori_loop(0, nc // grp_n, grp, 0)
        _, prep_vjp = jax.vjp(
            lambda *a: _gdn_prep(*a, h), gq[...], gk[...], gv[...], sm[...], _taps(wq, wk, wv), al[...], db[...])
        grads = prep_vjp(tuple(r[...].reshape(t, r.shape[-1]) for r in chunked))
        for r, gval in zip((dgq, dgk, dgv), grads[:3]):
            r[...] = gval.astype(r.dtype)
        for j, r in enumerate((dwq, dwk, dwv)):
            for k in range(4):
                r[k:k + 1, :] = grads[4][4 * j + k]

        @pl.when(h == 0)
        def _():
            for r in (dsm, dal, ddb):
                r[...] = jnp.zeros_like(r)

        dsm[...] += grads[3]
        dal[...] += grads[5]
        ddb[...] += grads[6]

    head = _head(t)
    taps = pl.BlockSpec((4, HD), lambda h: (0, h))
    return pl.pallas_call(
        body, grid=(NG,), name="gdn_bwd", in_specs=_gdn_in_specs(t) + [_per_head(sh) for sh in terms],
        out_specs=[head, head, head, _small(t), taps, taps, taps, _small(1), _small(1)],
        out_shape=[SDS((t, NG * HD), BF16)] * 3 + [SDS((t, HD), F32)] + [SDS((4, NG * HD), F32)] * 3 + [SDS((1, HD), F32)] * 2,
        scratch_shapes=_gdn_chunked_scratch(nc), compiler_params=_cp("arbitrary"),
    )(pa, pa, pa, pb, conv, conv, conv, alog, dtb, *dterms)


def _gdn_post(o, z, gain):
    return (jnp.concatenate(
        [_rms(o[:, h * HD:(h + 1) * HD], gain) * _silu(z[:, h * HD:(h + 1) * HD]) for h in range(NG)], axis=1),)


def _place():
    return lax.axis_index("x"), lax.axis_index("y"), lax.axis_index("c")


def _all_gather(name, shard):
    def body(x_ref, out_ref, send_sems, recv_sems, local_sem):
        x, y, c = _place()
        me, sibling = (x, y, c), (x, y, 1 - c)
        chips = [(1 - x, y), (x, 1 - y), (1 - x, 1 - y)]

        def blk(px, py, pc):
            return out_ref.at[4 * px + 2 * py + pc]

        def copy(k, block, to, src=None):
            return pltpu.make_async_remote_copy(
                src_ref=blk(*block) if src is None else src, dst_ref=blk(*block),
                send_sem=send_sems.at[k], recv_sem=recv_sems.at[k], device_id=to, device_id_type=MESH)

        mine = pltpu.make_async_copy(x_ref, blk(*me), local_sem)
        mine.start()
        first = [copy(0, me, sibling, src=x_ref)]
        first += [copy(1 + j, me, (*chip, c), src=x_ref) for j, chip in enumerate(chips)]
        for cp in first:
            cp.start()
        passed = [copy(4 + j, (*chip, c), sibling) for j, chip in enumerate(chips)]
        for j, chip in enumerate(chips):
            copy(1 + j, (*chip, c), me).wait_recv()
            passed[j].start()
        copy(0, sibling, me).wait_recv()
        for j, chip in enumerate(chips):
            copy(4 + j, (*chip, 1 - c), me).wait_recv()
        for cp in first + passed:
            cp.wait_send()
        mine.wait()

    return pl.pallas_call(
        body, name=name, out_shape=SDS((N_DEV,) + shard.shape, shard.dtype),
        in_specs=[pl.BlockSpec(memory_space=pltpu.HBM)], out_specs=pl.BlockSpec(memory_space=pltpu.HBM),
        scratch_shapes=[pltpu.SemaphoreType.DMA((7,)), pltpu.SemaphoreType.DMA((7,)), pltpu.SemaphoreType.DMA],
    )(shard)


def _scatter_exchange(name, full):
    def body(g_ref, out_ref, send_sems, recv_sems, local_sem):
        x, y, c = _place()
        me = 4 * x + 2 * y + c
        mine = pltpu.make_async_copy(g_ref.at[me], out_ref.at[me], local_sem)
        mine.start()
        sends, recvs = [], []
        for k in range(1, N_DEV):
            px = 1 - x if k & 4 else x
            py = 1 - y if k & 2 else y
            pc = 1 - c if k & 1 else c
            peer = 4 * px + 2 * py + pc
            sends.append(pltpu.make_async_remote_copy(
                src_ref=g_ref.at[peer], dst_ref=out_ref.at[me], send_sem=send_sems.at[k - 1],
                recv_sem=recv_sems.at[k - 1], device_id=(px, py, pc), device_id_type=MESH))
            recvs.append(pltpu.make_async_remote_copy(
                src_ref=g_ref.at[me], dst_ref=out_ref.at[peer], send_sem=send_sems.at[k - 1],
                recv_sem=recv_sems.at[k - 1], device_id=(px, py, pc), device_id_type=MESH))
        for cp in sends:
            cp.start()
        for cp in recvs:
            cp.wait_recv()
        for cp in sends:
            cp.wait_send()
        mine.wait()

    return pl.pallas_call(
        body, name=name, out_shape=SDS(full.shape, full.dtype),
        in_specs=[pl.BlockSpec(memory_space=pltpu.HBM)], out_specs=pl.BlockSpec(memory_space=pltpu.HBM),
        scratch_shapes=[pltpu.SemaphoreType.DMA((7,)), pltpu.SemaphoreType.DMA((7,)), pltpu.SemaphoreType.DMA],
    )(full)


def _sum_blocks(name, parts):
    _, r, c = parts.shape
    tr = 64 if r % 64 == 0 else r

    def body(x, o):
        acc = x[0].astype(F32)
        for d in range(1, N_DEV):
            acc = acc + x[d].astype(F32)
        o[...] = acc

    return pl.pallas_call(
        body, grid=(r // tr,), name=name, in_specs=[pl.BlockSpec((N_DEV, tr, c), lambda i: (0, i, 0))],
        out_specs=pl.BlockSpec((tr, c), lambda i: (i, 0)), out_shape=SDS((r, c), F32), compiler_params=_cp("parallel"),
    )(parts)


def _reduce_scatter(name, full):
    return _sum_blocks(name + "_sum", _scatter_exchange(name, full))


def _all_reduce_small(name, x, reduce):
    m_per, n = x.shape

    def body(x_ref, out_ref, send_sems, recv_sems, local_sem):
        px, py, pc = _place()
        me, sibling = (px, py, pc), (px, py, 1 - pc)
        chips = [(1 - px, py), (px, 1 - py), (1 - px, 1 - py)]
        buf = out_ref

        def rows(qx, qy, qc):
            return buf.at[pl.ds((4 * qx + 2 * qy + qc) * m_per, m_per), :]

        def copy(k, block, to, src=None):
            return pltpu.make_async_remote_copy(
                src_ref=rows(*block) if src is None else src, dst_ref=rows(*block),
                send_sem=send_sems.at[k], recv_sem=recv_sems.at[k], device_id=to, device_id_type=MESH)

        mine = pltpu.make_async_copy(x_ref, rows(*me), local_sem)
        mine.start()
        first = [copy(0, me, sibling, src=x_ref)]
        first += [copy(1 + j, me, (*chip, pc), src=x_ref) for j, chip in enumerate(chips)]
        for cp in first:
            cp.start()
        passed = [copy(4 + j, (*chip, pc), sibling) for j, chip in enumerate(chips)]
        for j, chip in enumerate(chips):
            copy(1 + j, (*chip, pc), me).wait_recv()
            passed[j].start()
        copy(0, sibling, me).wait_recv()
        for j, chip in enumerate(chips):
            copy(4 + j, (*chip, 1 - pc), me).wait_recv()
        for cp in first + passed:
            cp.wait_send()
        mine.wait()

    gathered = pl.pallas_call(
        body, name=name, out_shape=SDS((N_DEV * m_per, n), x.dtype),
        in_specs=[pl.BlockSpec(memory_space=pltpu.VMEM)], out_specs=pl.BlockSpec(memory_space=pltpu.VMEM),
        scratch_shapes=[pltpu.SemaphoreType.DMA((7,)), pltpu.SemaphoreType.DMA((7,)), pltpu.SemaphoreType.DMA],
    )(x)
    if not reduce:
        return gathered
    return _sum_blocks(name + "_sum", gathered.reshape(N_DEV, m_per, n))


HBM_SPEC = pl.BlockSpec(memory_space=pltpu.HBM)
SEM_SPEC = pl.BlockSpec(memory_space=pltpu.SEMAPHORE)
EFFECT = pltpu.SideEffectType.DATAFLOW_SIDE_EFFECTING


def _copies_start(name, bufs, n_remote, n_local, build, deps):
    nb, nd = len(bufs), len(deps)
    sem_shapes = [pltpu.SemaphoreType.DMA((n_remote,)), pltpu.SemaphoreType.DMA((n_remote,))]
    if n_local:
        sem_shapes.append(pltpu.SemaphoreType.DMA((n_local,)))
    ns = len(sem_shapes)

    def body(*refs):
        sems = refs[nb + nd:nb + nd + ns]
        remote, local = build(refs[:nb], *sems, *([None] * (3 - ns)))
        for cp in local + remote:
            cp.start()
        refs[-1][...] = jnp.zeros((8, HD), F32)

    outs = pl.pallas_call(
        body, name=name,
        out_shape=(*sem_shapes, *[pltpu.HBM(b.shape, b.dtype) for b in bufs], SDS((8, HD), F32)),
        in_specs=[HBM_SPEC] * nb + [ANY_SPEC] * nd,
        out_specs=(*[SEM_SPEC] * ns, *[HBM_SPEC] * nb, pl.BlockSpec(memory_space=pltpu.VMEM)),
        input_output_aliases={i: ns + i for i in range(nb)},
        compiler_params=pltpu.CompilerParams(has_side_effects=EFFECT),
    )(*[pltpu.with_memory_space_constraint(b, pltpu.HBM) for b in bufs], *deps)
    return list(outs[:ns]), list(outs[ns:ns + nb]), outs[-1]


def _copies_wait(name, bufs, sems, build, after):
    nb, ns = len(bufs), len(sems)

    def body(*refs):
        remote, local = build(refs[:nb], *refs[nb:nb + ns], *([None] * (3 - ns)))
        for cp in local:
            cp.wait()
        for cp in remote:
            cp.wait_send()
            cp.wait_recv()

    outs = pl.pallas_call(
        body, name=name, out_shape=tuple(pltpu.HBM(b.shape, b.dtype) for b in bufs),
        in_specs=[HBM_SPEC] * nb + [SEM_SPEC] * ns + [ANY_SPEC] * len(after), out_specs=tuple([HBM_SPEC] * nb),
        input_output_aliases={i: i for i in range(nb)},
        compiler_params=pltpu.CompilerParams(has_side_effects=EFFECT),
    )(*bufs, *sems, *after)
    return list(outs)


def _remote(src, dst, send, recv, k, to):
    return pltpu.make_async_remote_copy(src_ref=src, dst_ref=dst, send_sem=send.at[k], recv_sem=recv.at[k],
                                        device_id=to, device_id_type=MESH)


class _Gather:
    def __init__(self, name, shards, deps):
        self.name, self.n = name, len(shards)
        lands = [lax.empty((N_DEV,) + s.shape, s.dtype) for s in shards]
        self.sems1, bufs, self.token = _copies_start(
            name + "_s1", list(shards) + lands, 4 * self.n, self.n, self._stage1(range(self.n)), deps)
        self.shards, self.lands, self.sems2 = bufs[:self.n], bufs[self.n:], {}

    def _stage1(self, idxs):
        def build(refs, send, recv, loc):
            x, y, c = _place()
            me = 4 * x + 2 * y + c
            targets = [(x, y, 1 - c), (1 - x, y, c), (x, 1 - y, c), (1 - x, 1 - y, c)]
            remote, local = [], []
            for pos, i in enumerate(idxs):
                src, land = refs[pos], refs[len(idxs) + pos]
                local.append(pltpu.make_async_copy(src, land.at[me], loc.at[i]))
                remote += [_remote(src, land.at[me], send, recv, 4 * i + k, to) for k, to in enumerate(targets)]
            return remote, local
        return build

    @staticmethod
    def _stage2(refs, send, recv, loc):
        x, y, c = _place()
        remote = []
        for pos, land in enumerate(refs):
            for j, (cx, cy) in enumerate([(1 - x, y), (x, 1 - y), (1 - x, 1 - y)]):
                blk = land.at[4 * cx + 2 * cy + c]
                remote.append(_remote(blk, blk, send, recv, 3 * pos + j, (x, y, 1 - c)))
        return remote, []

    def pass_on(self, idxs, after):
        tag, m = "".join(map(str, idxs)), len(idxs)
        bufs = _copies_wait(f"{self.name}_w1_{tag}", [self.shards[i] for i in idxs] + [self.lands[i] for i in idxs],
                            self.sems1, self._stage1(idxs), after)
        self.sems2[tag], lands, token = _copies_start(f"{self.name}_s2_{tag}", bufs[m:], 3 * m, 0, self._stage2, ())
        for pos, i in enumerate(idxs):
            self.lands[i] = lands[pos]
        return [token]

    def get(self, idxs, after):
        tag = "".join(map(str, idxs))
        return _copies_wait(f"{self.name}_w2_{tag}", [self.lands[i] for i in idxs], self.sems2[tag], self._stage2, after)


def _rows_tile(r, row_bytes, target=1 << 20):
    tr = r
    while tr % 32 == 0 and tr * row_bytes > target:
        tr //= 2
    return tr


def _pair_add(name, g, got, c):
    _, r, cols = g.shape
    tr = _rows_tile(r, cols * 2)

    def body(s, a, b, o):
        o[...] = (a[...].astype(F32) + b[...].astype(F32)).astype(o.dtype)

    return pl.pallas_call(
        body, name=name, out_shape=SDS((4, r, cols), g.dtype),
        grid_spec=pltpu.PrefetchScalarGridSpec(
            num_scalar_prefetch=1, grid=(4, r // tr),
            in_specs=[pl.BlockSpec((None, tr, cols), lambda j, i, s: (2 * j + s[0], i, 0)),
                      pl.BlockSpec((None, tr, cols), lambda j, i, s: (j, i, 0))],
            out_specs=pl.BlockSpec((None, tr, cols), lambda j, i, s: (j, i, 0))),
        compiler_params=_cp("parallel", "parallel"),
    )(c.reshape(1), g, got)


def _quad_sum(name, part, got, chip):
    _, r, cols = part.shape
    tr = _rows_tile(r, cols * 4)

    def body(s, a, b1, b2, b3, o):
        o[...] = ((a[...].astype(F32) + b1[...].astype(F32)) + b2[...].astype(F32)) + b3[...].astype(F32)

    blk = lambda k: pl.BlockSpec((None, tr, cols), lambda i, s, k=k: (jnp.bitwise_xor(s[0], k), i, 0))
    return pl.pallas_call(
        body, name=name, out_shape=SDS((r, cols), F32),
        grid_spec=pltpu.PrefetchScalarGridSpec(
            num_scalar_prefetch=1, grid=(r // tr,), in_specs=[blk(0), blk(1), blk(2), blk(3)],
            out_specs=pl.BlockSpec((tr, cols), lambda i, s: (i, 0))),
        compiler_params=_cp("parallel"),
    )(chip.reshape(1), part, got, got, got)


class _Scatter:
    def __init__(self, name, grads, deps):
        self.name, self.n = name, len(grads)
        got = [lax.empty((4,) + g.shape[1:], g.dtype) for g in grads]
        self.sems, bufs, self.token = _copies_start(name + "_s1", list(grads) + got, 4 * self.n, 0, self._stage1, deps)
        self.grads, self.got = bufs[:self.n], bufs[self.n:]

    def _stage1(self, refs, send, recv, loc):
        x, y, c = _place()
        remote = []
        for i in range(self.n):
            remote += [_remote(refs[i].at[2 * j + 1 - c], refs[self.n + i].at[j], send, recv, 4 * i + j, (x, y, 1 - c))
                       for j in range(4)]
        return remote, []

    def _stage2(self, refs, send, recv, loc):
        x, y, c = _place()
        remote = []
        for i in range(self.n):
            for k in (1, 2, 3):
                tx = 1 - x if k & 2 else x
                ty = 1 - y if k & 1 else y
                remote.append(_remote(refs[i].at[2 * tx + ty], refs[self.n + i].at[2 * x + y], send, recv,
                                      3 * i + k - 1, (tx, ty, c)))
        return remote, []

    def mid(self, after):
        bufs = _copies_wait(self.name + "_w1", self.grads + self.got, self.sems, self._stage1, after)
        c = lax.axis_index("c").astype(jnp.int32)
        parts = [_pair_add(f"{self.name}_add{i}", bufs[i], bufs[self.n + i], c) for i in range(self.n)]
        got = [lax.empty(p.shape, p.dtype) for p in parts]
        self.sems, bufs, self.token = _copies_start(self.name + "_s2", parts + got, 3 * self.n, 0, self._stage2, ())
        self.parts, self.got = bufs[:self.n], bufs[self.n:]

    def end(self, after):
        bufs = _copies_wait(self.name + "_w2", self.parts + self.got, self.sems, self._stage2, after)
        chip = (2 * lax.axis_index("x") + lax.axis_index("y")).astype(jnp.int32)
        return [_quad_sum(f"{self.name}_sum{i}", bufs[i], bufs[self.n + i], chip) for i in range(self.n)]


def _adamw(w, g, m, v):
    m = ADAM_B1 * m + (1.0 - ADAM_B1) * g
    v = ADAM_B2 * v + (1.0 - ADAM_B2) * (g * g)
    m_hat = m / (1.0 - ADAM_B1 ** ADAM_STEP)
    v_hat = v / (1.0 - ADAM_B2 ** ADAM_STEP)
    return -ADAM_LR * (m_hat / (jnp.sqrt(v_hat) + ADAM_EPS) + ADAM_WD * w), m, v


def _adamw_call(name, w, g, m, v):
    r, c = w.shape
    tm = 64 if r % 64 == 0 else r
    return _rowwise(name, _adamw, [w, g, m, v], [], [(c, F32)] * 3, tm)


_IN_COLS = 5906


def _perm_in(w):
    pad = jnp.zeros((w.shape[0], 2 * HALF - _IN_COLS), w.dtype)
    return (jnp.concatenate([w[:, 2310:4614], w[:, 4614:5382]], axis=1),
            jnp.concatenate([w[:, :2304], w[:, 5394:5906], w[:, 2304:2310], w[:, 5382:5394], pad], axis=1))


def _unperm_in(ga, gb):
    return jnp.concatenate([gb[:, :2304], gb[:, 2816:2822], ga[:, :2304], ga[:, 2304:3072], gb[:, 2822:2834],
                            gb[:, 2304:2816]], axis=1)


def _lanes(v, at):
    return jnp.pad(v, ((0, 0), (at, HD - at - v.shape[1])))


_PACK = ("norm_mix", "mem_norm", "norm_ffn", "gdn_conv", "fox_q_norm", "fox_k_norm", "gdn_out_norm", "mem_q_norm",
         "mem_k_norm", "fox_f_bias", "gdn_a_log", "gdn_dt_bias", "loss")


def _pack(vals):
    parts = [vals[n].reshape(-1, HD) for n in _PACK]
    used = sum(p.shape[0] for p in parts)
    buf = jnp.concatenate(parts + [jnp.zeros((-used % 8, HD), F32)], axis=0)
    return buf, [(n, p.shape[0]) for n, p in zip(_PACK, parts)]


def _unpack(buf, layout):
    out, at = {}, 0
    for n, rows in layout:
        out[n] = buf[at:at + rows]
        at += rows
    return out


def kernel(x, mem, norm_mix, w_in, fox_f_bias, fox_q_norm, fox_k_norm, gdn_conv, gdn_a_log, gdn_dt_bias, gdn_out_norm, mem_norm, w_mem_kv, mem_q_norm, mem_k_norm, w_out, norm_ffn, w_gate_up, w_down, loss_target, m_norm_mix, m_w_in, m_fox_f_bias, m_fox_q_norm, m_fox_k_norm, m_gdn_conv, m_gdn_a_log, m_gdn_dt_bias, m_gdn_out_norm, m_mem_norm, m_w_mem_kv, m_mem_q_norm, m_mem_k_norm, m_w_out, m_norm_ffn, m_w_gate_up, m_w_down, v_norm_mix, v_w_in, v_fox_f_bias, v_fox_q_norm, v_fox_k_norm, v_gdn_conv, v_gdn_a_log, v_gdn_dt_bias, v_gdn_out_norm, v_mem_norm, v_w_mem_kv, v_mem_q_norm, v_mem_k_norm, v_w_out, v_norm_ffn, v_w_gate_up, v_w_down):
    args = dict(locals())
    d = x.shape[2]
    me = 4 * lax.axis_index("x") + 2 * lax.axis_index("y") + lax.axis_index("c")

    cshard = gdn_conv[0].shape[1]
    conv_pad = jnp.pad(gdn_conv[0], ((0, 4), (0, 3 * HD - cshard)))
    conv_all = _all_reduce_small("ag_conv", conv_pad, False).reshape(N_DEV, 8, 3 * HD)[:, :4, :cshard]
    conv_all = conv_all.transpose(1, 0, 2).reshape(4, N_DEV * cshard)
    w_in_a, w_in_b = _perm_in(w_in[0])
    comm = _StepComm({"in_b": [w_in_b], "in_a": [w_in_a], "kv_out": [w_mem_kv[0], w_out[0]], "gate_up": [w_gate_up[0]],
                      "down": [w_down[0]]}, [conv_all])

    grad_x, loss_local, small_grads = _local_step(
        x[0], mem[0], loss_target[0], norm_mix, fox_f_bias, fox_q_norm, fox_k_norm, gdn_a_log, gdn_dt_bias,
        gdn_out_norm, mem_norm, mem_q_norm, mem_k_norm, norm_ffn, conv_all, comm)

    red = comm.finish([grad_x])
    grads = {"w_down": red["ffn"][0], "w_gate_up": red["ffn"][1], "w_out": red["a"][1], "w_mem_kv": red["b"][1],
             "w_in": _unperm_in(red["a"][0], red["b"][0])}
    small_grads["loss"] = jnp.broadcast_to(loss_local, (1, HD))
    packed, layout = _pack(small_grads)
    small = _unpack(_all_reduce_small("ar_small", packed, True), layout)
    loss = small["loss"][0, 0]
    six = {"fox_f_bias": L_FF, "gdn_a_log": L_GA, "gdn_dt_bias": L_GA}
    for n, rows_n in layout[:-1]:
        gsm = small[n]
        if n == "gdn_conv":
            gsm = lax.dynamic_slice(gsm.reshape(4, N_DEV * cshard), (0, me * cshard), (4, cshard))[None]
        elif n in six:
            gsm = gsm[:, six[n]:six[n] + 6]
        else:
            gsm = gsm.reshape(1, rows_n * HD)
        grads[n] = gsm

    names = ['norm_mix', 'w_in', 'fox_f_bias', 'fox_q_norm', 'fox_k_norm', 'gdn_conv', 'gdn_a_log', 'gdn_dt_bias',
             'gdn_out_norm', 'mem_norm', 'w_mem_kv', 'mem_q_norm', 'mem_k_norm', 'w_out', 'norm_ffn', 'w_gate_up', 'w_down']
    big = ("w_in", "w_mem_kv", "w_out", "w_gate_up", "w_down")
    delta, new_m, new_v = {}, {}, {}
    for n in big:
        delta[n], new_m[n], new_v[n] = [a[None] for a in _adamw_call(
            "adamw_" + n, args[n][0], grads[n], args["m_" + n][0], args["v_" + n][0])]
        grads[n] = grads[n][None]

    def flat(a):
        a = a.reshape(1, -1)
        return jnp.pad(a, ((0, 0), (0, -a.shape[1] % HD))).reshape(-1, HD)

    smalls = [n for n in names if n not in big]
    pk = lambda pre: jnp.concatenate([flat(grads[n] if pre == "g" else args[pre + n]) for n in smalls], axis=0)
    cat = [pk(""), pk("g"), pk("m_"), pk("v_")]
    padr = -cat[0].shape[0] % 8
    cat = [jnp.pad(a, ((0, padr), (0, 0))) for a in cat]
    res = _adamw_call("adamw_small", *cat)
    at = 0
    for n in smalls:
        shape = args[n].shape
        size = math.prod(shape)
        nrow = -(-size // HD)
        for dst, src in zip((delta, new_m, new_v), res):
            dst[n] = src[at:at + nrow].reshape(-1)[:size].reshape(shape)
        at += nrow

    return (loss, grad_x[None], *[grads[n] for n in names], *[delta[n] for n in names],
            *[new_m[n] for n in names], *[new_v[n] for n in names])


class _StepComm:
    def __init__(self, shard_groups, after):
        self.groups, shards = {}, []
        for key, ws in shard_groups.items():
            self.groups[key] = list(range(len(shards), len(shards) + len(ws)))
            shards += [w.astype(BF16) for w in ws]
        self.gather = _Gather("ag", shards, after)
        self.passed, self.scatters = set(), {}

    def start_deps(self):
        return [self.gather.token]

    def pass_on(self, key, after):
        self.passed.add(key)
        return self.gather.pass_on(self.groups[key], after)

    def weights(self, key, after):
        if key not in self.passed:
            after = self.pass_on(key, after)
        return self.gather.get(self.groups[key], after)

    def send(self, tag, grads):
        blocks = [g if g.ndim == 3 else g.reshape(N_DEV, g.shape[0] // N_DEV, g.shape[1]) for g in grads]
        self.scatters[tag] = _Scatter("rs_" + tag, blocks, ())
        return [self.scatters[tag].token]

    def mid(self, tag, after):
        self.scatters[tag].mid(after)
        return [self.scatters[tag].token]

    def finish(self, after):
        return {tag: sc.end(after) for tag, sc in self.scatters.items()}


def _local_step(xs, ms, tgt, norm_mix, fox_f_bias, fox_q_norm, fox_k_norm, gdn_a_log, gdn_dt_bias, gdn_out_norm,
                mem_norm, mem_q_norm, mem_k_norm, norm_ffn, conv_all, comm):
    t, d = xs.shape
    bq = min(t, 256)
    fb, alog, dtb = _lanes(fox_f_bias, L_FF), _lanes(gdn_a_log, L_GA), _lanes(gdn_dt_bias, L_GA)
    flat = lambda w: w.reshape(-1, w.shape[-1])

    rms1 = lambda a, g: (_rms(a, g),)
    (u,) = _rowwise("norm_mix", rms1, [xs], [norm_mix], [(d, BF16)], min(t, 256), deps=comm.start_deps())
    w_in_b = flat(comm.weights("in_b", [u])[0])
    pb = _matmul("proj_in_b", u, w_in_b, NN, F32, 1024, 768)
    o_fox = _fox_fwd(pb, fb, fox_q_norm, fox_k_norm, bq)
    w_in_a = flat(comm.weights("in_a", [o_fox])[0])
    pa = _matmul("proj_in_a", u, w_in_a, NN, F32, 1024, 768)
    o_gdn_raw, gdn_saved = _gdn_fwd(pa, pb, conv_all, alog, dtb)
    zrow = (pa, NG * HD, GZ * HD // (NG * HD))
    (o_gdn,) = _rowwise("gdn_post", _gdn_post, [o_gdn_raw, zrow], [gdn_out_norm], [(NG * HD, BF16)], min(t, 256))
    w_kv_all, w_out_all = [flat(w) for w in comm.weights("kv_out", [o_gdn])]
    (mem_n,) = _rowwise("norm_mem", rms1, [ms], [mem_norm], [(d, BF16)], ms.shape[0])
    mkv = _matmul("proj_mem", mem_n, w_kv_all, NN, F32, 256, 512)
    o_mem = _mem_fwd(pb, mkv, mem_q_norm, mem_k_norm)
    deps = comm.pass_on("gate_up", [o_mem])
    mix = jnp.concatenate([o_fox, o_gdn, o_mem], axis=1)
    h1 = _matmul("proj_out", mix, w_out_all, NN, F32, 1024, 512, residual=xs, deps=deps)
    (h1n,) = _rowwise("norm_ffn", rms1, [h1], [norm_ffn], [(d, BF16)], min(t, 256))
    (wgu,) = comm.weights("gate_up", [h1n])
    ffw = wgu.shape[2]
    gu, act = _ffn_up(h1n, wgu.reshape(2, 4, d, ffw))
    w_down_all = flat(comm.weights("down", [act])[0])
    dy, dyb, lsum = _ffn_down_loss(act, w_down_all, h1, tgt)
    loss_local = (0.5 / d) * jnp.sum(lsum[::8, ::HD])

    dgu = _ffn_down_bwd(dyb, w_down_all.reshape(4, ffw, d), gu).reshape(8, t, ffw)
    g_w_down = _matmul("grad_w_down", act, dyb, TN, BF16, 512, 512)
    dh1n = _ffn_up_bwd_x(dgu, wgu)
    g_w_gu = _ffn_up_bwd_w(h1n, dgu)
    deps = comm.send("ffn", [g_w_down, g_w_gu])
    rms2 = lambda a, g: (_rms(a, g), a)
    dh1, g_norm_ffn = _rowwise_vjp("norm_ffn_bwd", rms2, [h1], [norm_ffn], [dh1n, dy], [F32], min(t, 256), deps=deps)
    dh1b = dh1.astype(BF16)

    dmix = _matmul("proj_out_bwd_x", dh1b, w_out_all, NT, F32, 1024, 512)
    g_w_out = _matmul("grad_w_out", mix, dh1b, TN, BF16, 512, 512)
    deps = comm.mid("ffn", [dmix, g_w_out])
    do_raw, dgz, g_gon = _rowwise_vjp("gdn_post_bwd", _gdn_post, [o_gdn_raw, zrow], [gdn_out_norm],
                                      [(dmix, NG * HD, 1)], [F32, BF16], min(t, 256), deps=deps)
    dterms = _gdn_bwd_scan(gdn_saved, do_raw)
    dgq, dgk, dgv, dsm_gdn, dwq, dwk, dwv, g_alog, g_dtb = _gdn_bwd(pa, pb, conv_all, alog, dtb, dterms)
    dp_a = jnp.concatenate([dgq, dgk, dgv, dgz], axis=1)
    g_w_in_a = _matmul("grad_w_in_a", u, dp_a, TN, BF16, 512, 768)
    deps = comm.send("a", [g_w_in_a, g_w_out])
    dmq, dmk, dmv, g_mqn, g_mkn = _mem_bwd(pb, mkv, mem_q_norm, mem_k_norm, dmix, deps=deps)
    dmkv = jnp.concatenate([dmk, dmv], axis=1).astype(BF16)
    dmem_n = _matmul("proj_mem_bwd_x", dmkv, w_kv_all, NT, F32, 256, 512)
    g_w_kv = _matmul("grad_w_kv", mem_n, dmkv, TN, BF16, 512, 512)
    g_mem_norm = _rowwise_vjp("norm_mem_bwd", rms1, [ms], [mem_norm], [dmem_n], [], ms.shape[0])[0]
    deps = comm.mid("a", [g_mem_norm, g_w_kv])
    dfq, dfk, dfv, dsm_fox, g_fb, g_fqn, g_fkn = _fox_bwd(pb, fb, fox_q_norm, fox_k_norm, dmix, bq, deps=deps)
    dp_b = jnp.concatenate([dfq, dfk, dfv, dmq, (dsm_fox + dsm_gdn).astype(BF16), jnp.zeros((t, HD), BF16)], axis=1)
    g_w_in_b = _matmul("grad_w_in_b", u, dp_b, TN, BF16, 512, 768)
    deps = comm.send("b", [g_w_in_b, g_w_kv])
    du_a = _matmul("proj_in_bwd_a", dp_a, w_in_a, NT, F32, 512, 512, deps=deps)
    deps = comm.mid("b", [du_a])
    du = _matmul("proj_in_bwd_b", dp_b, w_in_b, NT, F32, 512, 512, residual=du_a, deps=deps)
    grad_x, g_norm_mix = _rowwise_vjp("norm_mix_bwd", rms2, [xs], [norm_mix], [du, dh1], [F32], min(t, 256))

    small_grads = {
        "norm_mix": g_norm_mix, "mem_norm": g_mem_norm, "norm_ffn": g_norm_ffn,
        "gdn_conv": jnp.concatenate([dwq, dwk, dwv], axis=1),
        "fox_q_norm": g_fqn, "fox_k_norm": g_fkn, "gdn_out_norm": g_gon, "mem_q_norm": g_mqn, "mem_k_norm": g_mkn,
        "fox_f_bias": g_fb, "gdn_a_log": g_alog, "gdn_dt_bias": g_dtb}
    return grad_x, loss_local, small_grads
```

```python
import functools
import math

import jax
import jax.numpy as jnp
from jax import lax
from jax.experimental import pallas as pl
from jax.experimental.pallas import tpu as pltpu

F32 = jnp.float32
BF16 = jnp.bfloat16
HI = lax.Precision.HIGHEST
SDS = jax.ShapeDtypeStruct

N_DEV = 8
HD = 128
NF, NG, NM = 6, 6, 4
CHUNK = 64
GROUP = 4
NORM_EPS = 1e-6
GQ, GK, GV, GZ = 0, 6, 12, 18
FQ, FK, FV, MQ, SM = 0, 6, 12, 18, 22
HALF = 24 * HD
L_FF, L_GA, L_GB = 0, 6, 12
VMEM_LIMIT = 56 * 1024 * 1024

ADAM_LR, ADAM_B1, ADAM_B2, ADAM_EPS, ADAM_WD, ADAM_STEP = 0.001, 0.9, 0.999, 1e-08, 0.01, 10

NN = (((1,), (0,)), ((), ()))
NT = (((1,), (1,)), ((), ()))
TN = (((0,), (0,)), ((), ()))
MESH = pl.DeviceIdType.MESH


def _cp(*sem):
    return pltpu.CompilerParams(dimension_semantics=tuple(sem) if sem else None, vmem_limit_bytes=VMEM_LIMIT)


def _dot(a, b, dims=NN):
    return lax.dot_general(a, b, dims, preferred_element_type=F32)


def _bdot(a, b):
    return _dot(a.astype(BF16), b.astype(BF16))


def _iota(shape, axis):
    return lax.broadcasted_iota(jnp.int32, shape, axis)


def _rms(x, gain):
    return x * lax.rsqrt(jnp.mean(x * x, axis=-1, keepdims=True) + NORM_EPS) * gain


def _sigmoid(x):
    z = jnp.exp(-jnp.abs(x))
    return jnp.where(x >= 0, 1.0 / (1.0 + z), z / (1.0 + z))


def _silu(x):
    return x * _sigmoid(x)


def _softplus(x):
    return jnp.maximum(x, 0.0) + jnp.log(1.0 + jnp.exp(-jnp.abs(x)))


def _lane_pick(x, lane):
    oh = (_iota((1, x.shape[-1]), 1) == lane).astype(F32)
    return jnp.sum(x * oh, axis=-1, keepdims=True)


def _cumsum_rows(x):
    tril = (_iota((HD, HD), 0) >= _iota((HD, HD), 1)).astype(F32)
    carry = jnp.zeros((1, x.shape[1]), F32)
    outs = []
    for b in range(x.shape[0] // HD):
        blk = x[b * HD:(b + 1) * HD]
        outs.append(jnp.dot(tril, blk, precision=HI, preferred_element_type=F32) + carry)
        carry = carry + jnp.sum(blk, axis=0, keepdims=True)
    return jnp.concatenate(outs, axis=0)


def _row_spec(r, tm):
    if isinstance(r, tuple):
        arr, width, cb = r
        return arr, pl.BlockSpec((tm, width), lambda i, cb=cb: (i, cb))
    return r, pl.BlockSpec((tm, r.shape[1]), lambda i: (i, 0))


ANY_SPEC = pl.BlockSpec(memory_space=pl.ANY)


def _rowwise(name, fn, rows, consts, outs, tm, deps=()):
    arrs, specs = zip(*[_row_spec(r, tm) for r in rows])
    n_rows = arrs[0].shape[0]
    nr, nc, nd = len(rows), len(consts), len(deps)

    def body(*refs):
        res = fn(*[r[...] for r in refs[:nr + nc]])
        for o, v in zip(refs[nr + nc + nd:], res):
            o[...] = v.astype(o.dtype)

    return pl.pallas_call(
        body, grid=(n_rows // tm,), name=name,
        in_specs=list(specs) + [pl.BlockSpec(c.shape, lambda i: (0, 0)) for c in consts] + [ANY_SPEC] * nd,
        out_specs=[pl.BlockSpec((tm, w), lambda i: (i, 0)) for w, _ in outs],
        out_shape=[SDS((n_rows, w), dt) for w, dt in outs],
        compiler_params=_cp("parallel"),
    )(*arrs, *consts, *deps)


def _rowwise_vjp(name, fn, rows, consts, cts, grad_dtypes, tm, deps=()):
    arrs, specs = zip(*[_row_spec(r, tm) for r in rows])
    ct_arrs, ct_specs = zip(*[_row_spec(r, tm) for r in cts])
    n_rows = arrs[0].shape[0]
    nr, nc, nct, ng, nd = len(rows), len(consts), len(cts), len(grad_dtypes), len(deps)
    widths = [s.block_shape[1] for s in specs[:ng]]

    def body(*refs):
        vals = [r[...].astype(F32) for r in refs[:nr + nc]]
        ctv = tuple(r[...].astype(F32) for r in refs[nr + nc:nr + nc + nct])
        _, vjp = jax.vjp(fn, *vals)
        grads = vjp(ctv)
        outs = refs[nr + nc + nct + nd:]
        for o, g in zip(outs[:ng], grads[:ng]):
            o[...] = g.astype(o.dtype)

        @pl.when(pl.program_id(0) == 0)
        def _():
            for o in outs[ng:]:
                o[...] = jnp.zeros_like(o)

        for o, g in zip(outs[ng:], grads[nr:]):
            o[...] += g

    return pl.pallas_call(
        body, grid=(n_rows // tm,), name=name,
        in_specs=list(specs) + [pl.BlockSpec(c.shape, lambda i: (0, 0)) for c in consts] + list(ct_specs)
        + [ANY_SPEC] * nd,
        out_specs=[pl.BlockSpec((tm, w), lambda i: (i, 0)) for w in widths]
        + [pl.BlockSpec(c.shape, lambda i: (0, 0)) for c in consts],
        out_shape=[SDS((n_rows, w), dt) for w, dt in zip(widths, grad_dtypes)] + [SDS(c.shape, F32) for c in consts],
        compiler_params=_cp("arbitrary"),
    )(*arrs, *consts, *ct_arrs, *deps)


def _tile(n, pref):
    t = min(n, pref)
    while n % t or (t % HD and t != n):
        t -= 1
    return t


def _matmul(name, a, b, dims, out_dtype, tm, tn, residual=None, deps=()):
    ta, tb = dims == TN, dims == NT
    m = a.shape[1] if ta else a.shape[0]
    k = a.shape[0] if ta else a.shape[1]
    n = b.shape[0] if tb else b.shape[1]
    tm, tn = _tile(m, tm), _tile(n, tn)

    def body(*refs):
        acc = _dot(refs[0][...], refs[1][...], dims)
        if residual is not None:
            acc = acc + refs[2][...]
        refs[-1][...] = acc.astype(out_dtype)

    in_specs = [pl.BlockSpec((k, tm), lambda i, j: (0, i)) if ta else pl.BlockSpec((tm, k), lambda i, j: (i, 0)),
                pl.BlockSpec((tn, k), lambda i, j: (j, 0)) if tb else pl.BlockSpec((k, tn), lambda i, j: (0, j))]
    ops = [a, b]
    if residual is not None:
        in_specs.append(pl.BlockSpec((tm, tn), lambda i, j: (i, j)))
        ops.append(residual)
    in_specs += [ANY_SPEC] * len(deps)
    ops += list(deps)
    return pl.pallas_call(
        body, grid=(m // tm, n // tn), name=name, in_specs=in_specs,
        out_specs=pl.BlockSpec((tm, tn), lambda i, j: (i, j)), out_shape=SDS((m, n), out_dtype),
        compiler_params=_cp("parallel", "parallel"),
    )(*ops)


def _ffn_up(h1n, wgu):
    t, d = h1n.shape
    w = wgu.shape[3]
    tm = _tile(t, 512)

    def body(a, b, gu, act):
        x = a[...]
        g = _dot(x, b[0])
        u = _dot(x, b[1])
        gu[0] = g.astype(BF16)
        gu[1] = u.astype(BF16)
        act[...] = (_silu(g) * u).astype(BF16)

    return pl.pallas_call(
        body, grid=(4, t // tm), name="ffn_up",
        in_specs=[pl.BlockSpec((tm, d), lambda j, i: (i, 0)), pl.BlockSpec((2, None, d, w), lambda j, i: (0, j, 0, 0))],
        out_specs=[pl.BlockSpec((2, None, tm, w), lambda j, i: (0, j, i, 0)), pl.BlockSpec((tm, w), lambda j, i: (i, j))],
        out_shape=[SDS((2, 4, t, w), BF16), SDS((t, 4 * w), BF16)],
        compiler_params=_cp("parallel", "parallel"),
    )(h1n, wgu)


def _ffn_down_loss(act, wdown, h1, target):
    t, f = act.shape
    d = wdown.shape[1]
    tm, tn = _tile(t, 1024), _tile(d, 512)

    def body(a, b, h, tg, dy, dyb, ls):
        e = _dot(a[...], b[...]) + h[...] - tg[...]
        g = e * (1.0 / d)
        dy[...] = g
        dyb[...] = g.astype(BF16)
        ls[...] = jnp.broadcast_to(jnp.sum(e * e), (8, HD))

    return pl.pallas_call(
        body, grid=(t // tm, d // tn), name="ffn_down_loss",
        in_specs=[pl.BlockSpec((tm, f), lambda i, j: (i, 0)), pl.BlockSpec((f, tn), lambda i, j: (0, j)),
                  pl.BlockSpec((tm, tn), lambda i, j: (i, j)), pl.BlockSpec((tm, tn), lambda i, j: (i, j))],
        out_specs=[pl.BlockSpec((tm, tn), lambda i, j: (i, j)), pl.BlockSpec((tm, tn), lambda i, j: (i, j)),
                   pl.BlockSpec((8, HD), lambda i, j: (i, j))],
        out_shape=[SDS((t, d), F32), SDS((t, d), BF16), SDS((8 * (t // tm), HD * (d // tn)), F32)],
        compiler_params=_cp("parallel", "parallel"),
    )(act, wdown, h1, target)


def _ffn_down_bwd(dyb, wdown4, gu):
    t, d = dyb.shape
    w = wdown4.shape[1]
    tm = _tile(t, 512)

    def body(a, b, gu_ref, out):
        da = _dot(a[...], b[...], NT)
        g = gu_ref[0].astype(F32)
        u = gu_ref[1].astype(F32)
        s = _sigmoid(g)
        out[0] = (da * u * (s * (1.0 + g * (1.0 - s)))).astype(BF16)
        out[1] = (da * g * s).astype(BF16)

    return pl.pallas_call(
        body, grid=(4, t // tm), name="ffn_down_bwd",
        in_specs=[pl.BlockSpec((tm, d), lambda j, i: (i, 0)), pl.BlockSpec((None, w, d), lambda j, i: (j, 0, 0)),
                  pl.BlockSpec((2, None, tm, w), lambda j, i: (0, j, i, 0))],
        out_specs=pl.BlockSpec((2, None, tm, w), lambda j, i: (0, j, i, 0)),
        out_shape=SDS((2, 4, t, w), BF16),
        compiler_params=_cp("parallel", "parallel"),
    )(dyb, wdown4, gu)


def _ffn_up_bwd_x(dgu, wgu):
    _, t, w = dgu.shape
    d = wgu.shape[1]
    tm = _tile(t, 512)

    def body(a, b, out):
        @pl.when(pl.program_id(1) == 0)
        def _():
            out[...] = jnp.zeros_like(out)
        out[...] += _dot(a[...], b[...], NT)

    return pl.pallas_call(
        body, grid=(t // tm, 8), name="ffn_up_bwd_x",
        in_specs=[pl.BlockSpec((None, tm, w), lambda i, j: (j, i, 0)), pl.BlockSpec((None, d, w), lambda i, j: (j, 0, 0))],
        out_specs=pl.BlockSpec((tm, d), lambda i, j: (i, 0)), out_shape=SDS((t, d), F32),
        compiler_params=_cp("parallel", "arbitrary"),
    )(dgu, wgu)


def _ffn_up_bwd_w(h1n, dgu):
    _, t, w = dgu.shape
    d = h1n.shape[1]
    tm = _tile(d, 512)

    def body(a, b, out):
        out[...] = _dot(a[...], b[...], TN).astype(BF16)

    return pl.pallas_call(
        body, grid=(8, d // tm), name="ffn_up_bwd_w",
        in_specs=[pl.BlockSpec((t, tm), lambda j, i: (0, i)), pl.BlockSpec((None, t, w), lambda j, i: (j, 0, 0))],
        out_specs=pl.BlockSpec((None, tm, w), lambda j, i: (j, i, 0)), out_shape=SDS((8, d, w), BF16),
        compiler_params=_cp("parallel", "parallel"),
    )(h1n, dgu)


def _fox_prep(fq, fk, sm, fb, qg, kg, h):
    qn = _rms(fq, qg)
    kn = _rms(fk, kg)
    c = _cumsum_rows(-_softplus(-(sm + fb)))
    ccol = _lane_pick(c, L_FF + h)
    crow = jnp.sum(c.T * (_iota((HD, 1), 0) == L_FF + h).astype(F32), axis=0, keepdims=True)
    return qn, kn, ccol, crow


def _fox_block(q, k, v, cc, cr, off):
    s = _dot(q.astype(BF16), k.astype(BF16), NT) * (HD ** -0.5) + cc - cr
    s = jnp.where(_iota(s.shape, 1) <= _iota(s.shape, 0) + off, s, -1e30)
    e = jnp.exp(s - lax.stop_gradient(jnp.max(s, axis=1, keepdims=True)))
    p = e / jnp.sum(e, axis=1, keepdims=True)
    return _dot(p.astype(BF16), v.astype(BF16))


ONE_BUFFER = pl.Buffered(1)


def _pcol(t, cb):
    return pl.BlockSpec((t, HD), lambda h, cb=cb: (0, cb + h), pipeline_mode=ONE_BUFFER)


def _smcol(t):
    return pl.BlockSpec((t, HD), lambda h: (0, SM), pipeline_mode=ONE_BUFFER)


def _head(t):
    return pl.BlockSpec((t, HD), lambda h: (0, h), pipeline_mode=ONE_BUFFER)


def _small(n):
    return pl.BlockSpec((n, HD), lambda h: (0, 0), pipeline_mode=ONE_BUFFER)


def _fox_fwd(p, fb, qg, kg, bq):
    t = p.shape[0]

    def body(fq, fk, fv, sm, fb_r, qg_r, kg_r, o, qn_s, cc_s):
        h = pl.program_id(0)
        qn, kn, ccol, crow = _fox_prep(fq[...], fk[...], sm[...], fb_r[...], qg_r[...], kg_r[...], h)
        qn_s[...] = qn
        cc_s[...] = ccol
        knb = kn.astype(BF16)
        vb = fv[...].astype(BF16)
        for i in range(t // bq):
            rows, ext = pl.ds(i * bq, bq), (i + 1) * bq
            o[rows, :] = _fox_block(qn_s[rows, :], knb[:ext], vb[:ext], cc_s[rows, :], crow[:, :ext], i * bq).astype(o.dtype)

    return pl.pallas_call(
        body, grid=(NF,), name="fox_fwd",
        in_specs=[_pcol(t, FQ), _pcol(t, FK), _pcol(t, FV), _smcol(t), _small(1), _small(1), _small(1)],
        out_specs=_head(t), out_shape=SDS((t, NF * HD), BF16),
        scratch_shapes=[pltpu.VMEM((t, HD), F32), pltpu.VMEM((t, 1), F32)],
        compiler_params=_cp("parallel"),
    )(p, p, p, p, fb, qg, kg)


def _fox_bwd(p, fb, qg, kg, dmix, bq, deps=()):
    t = p.shape[0]

    def body(*refs):
        fq, fk, fv, sm, fb_r, qg_r, kg_r, do = refs[:8]
        dfq, dfk, dfv, dsm, dfb, dqg, dkg, qn_s, cc_s, dqn_s, dcc_s, dkn_s, dv_s, dcr_s = refs[8 + len(deps):]
        h = pl.program_id(0)
        qn, kn, ccol, crow = _fox_prep(fq[...], fk[...], sm[...], fb_r[...], qg_r[...], kg_r[...], h)
        qn_s[...] = qn
        cc_s[...] = ccol
        v = fv[...]
        dkn_s[...] = jnp.zeros_like(dkn_s)
        dv_s[...] = jnp.zeros_like(dv_s)
        dcr_s[...] = jnp.zeros_like(dcr_s)

        for i in range(t // bq):
            rows, ext = pl.ds(i * bq, bq), (i + 1) * bq
            _, vjp = jax.vjp(lambda a, b, c, d, e, off=i * bq: _fox_block(a, b, c, d, e, off),
                             qn_s[rows, :], kn[:ext], v[:ext], cc_s[rows, :], crow[:, :ext])
            dq, dk, dv, dcc, dcr = vjp(do[rows, :])
            dqn_s[rows, :] = dq
            dcc_s[rows, :] = dcc
            dkn_s[:ext, :] += dk
            dv_s[:ext, :] += dv
            dcr_s[:, :ext] += dcr
        _, prep_vjp = jax.vjp(lambda a, b, c, d, e, f: _fox_prep(a, b, c, d, e, f, h),
                              fq[...], fk[...], sm[...], fb_r[...], qg_r[...], kg_r[...])
        g_fq, g_fk, g_sm, g_fb, g_qg, g_kg = prep_vjp((dqn_s[...], dkn_s[...], dcc_s[...], dcr_s[...]))
        dfq[...] = g_fq.astype(dfq.dtype)
        dfk[...] = g_fk.astype(dfk.dtype)
        dfv[...] = dv_s[...].astype(dfv.dtype)

        @pl.when(h == 0)
        def _():
            for r in (dsm, dfb, dqg, dkg):
                r[...] = jnp.zeros_like(r)

        dsm[...] += g_sm
        dfb[...] += g_fb
        dqg[...] += g_qg
        dkg[...] += g_kg

    head = _head(t)
    return pl.pallas_call(
        body, grid=(NF,), name="fox_bwd",
        in_specs=[_pcol(t, FQ), _pcol(t, FK), _pcol(t, FV), _smcol(t), _small(1), _small(1), _small(1), head]
        + [ANY_SPEC] * len(deps),
        out_specs=[head, head, head, _small(t), _small(1), _small(1), _small(1)],
        out_shape=[SDS((t, NF * HD), BF16)] * 3 + [SDS((t, HD), F32)] + [SDS((1, HD), F32)] * 3,
        scratch_shapes=[pltpu.VMEM((t, HD), F32), pltpu.VMEM((t, 1), F32), pltpu.VMEM((t, HD), F32),
                        pltpu.VMEM((t, 1), F32), pltpu.VMEM((t, HD), F32), pltpu.VMEM((t, HD), F32),
                        pltpu.VMEM((1, t), F32)],
        compiler_params=_cp("arbitrary"),
    )(p, p, p, p, fb, qg, kg, dmix, *deps)


def _mem_attn(mq, mk, mv, qg, kg):
    s = _dot(_rms(mq, qg).astype(BF16), _rms(mk, kg).astype(BF16), NT) * (HD ** -0.5)
    e = jnp.exp(s - lax.stop_gradient(jnp.max(s, axis=1, keepdims=True)))
    p = e / jnp.sum(e, axis=1, keepdims=True)
    return _dot(p.astype(BF16), mv.astype(BF16))


def _mem_fwd(p, mkv, qg, kg):
    t, ml = p.shape[0], mkv.shape[0]

    def body(mq, mk, mv, qg_r, kg_r, o):
        o[...] = _mem_attn(mq[...], mk[...], mv[...], qg_r[...], kg_r[...]).astype(o.dtype)

    return pl.pallas_call(
        body, grid=(NM,), name="mem_fwd",
        in_specs=[_pcol(t, MQ), pl.BlockSpec((ml, HD), lambda h: (0, h)), pl.BlockSpec((ml, HD), lambda h: (0, NM + h)),
                  _small(1), _small(1)],
        out_specs=pl.BlockSpec((t, HD), lambda h: (0, h)), out_shape=SDS((t, NM * HD), BF16),
        compiler_params=_cp("parallel"),
    )(p, mkv, mkv, qg, kg)


def _mem_bwd(p, mkv, qg, kg, dmix, deps=()):
    t, ml = p.shape[0], mkv.shape[0]

    def body(*refs):
        mq, mk, mv, qg_r, kg_r, do = refs[:6]
        dmq, dmk, dmv, dqg, dkg = refs[6 + len(deps):]
        _, vjp = jax.vjp(_mem_attn, mq[...], mk[...], mv[...], qg_r[...], kg_r[...])
        g_q, g_k, g_v, g_qg, g_kg = vjp(do[...])
        dmq[...] = g_q.astype(dmq.dtype)
        dmk[...] = g_k
        dmv[...] = g_v

        @pl.when(pl.program_id(0) == 0)
        def _():
            dqg[...] = jnp.zeros_like(dqg)
            dkg[...] = jnp.zeros_like(dkg)

        dqg[...] += g_qg
        dkg[...] += g_kg

    return pl.pallas_call(
        body, grid=(NM,), name="mem_bwd",
        in_specs=[_pcol(t, MQ), pl.BlockSpec((ml, HD), lambda h: (0, h)), pl.BlockSpec((ml, HD), lambda h: (0, NM + h)),
                  _small(1), _small(1), pl.BlockSpec((t, HD), lambda h: (0, NF + NG + h))] + [ANY_SPEC] * len(deps),
        out_specs=[pl.BlockSpec((t, HD), lambda h: (0, h)), pl.BlockSpec((ml, HD), lambda h: (0, h)),
                   pl.BlockSpec((ml, HD), lambda h: (0, h)), _small(1), _small(1)],
        out_shape=[SDS((t, NM * HD), BF16), SDS((ml, NM * HD), F32), SDS((ml, NM * HD), F32),
                   SDS((1, HD), F32), SDS((1, HD), F32)],
        compiler_params=_cp("arbitrary"),
    )(p, mkv, mkv, qg, kg, dmix, *deps)


def _shift_down(x, s):
    if s == 0:
        return x
    return jnp.where(_iota(x.shape, 0) >= s, pltpu.roll(x, s, 0), 0.0)


def _shift_up(x, s):
    if s == 0:
        return x
    n = x.shape[0]
    return jnp.where(_iota(x.shape, 0) < n - s, pltpu.roll(x, n - s, 0), 0.0)


@jax.custom_vjp
def _conv4(x, w0, w1, w2, w3):
    return w0 * _shift_down(x, 3) + w1 * _shift_down(x, 2) + w2 * _shift_down(x, 1) + w3 * x


def _conv4_fwd(x, w0, w1, w2, w3):
    return _conv4(x, w0, w1, w2, w3), (x, w0, w1, w2, w3)


def _conv4_bwd(res, dy):
    x, w0, w1, w2, w3 = res
    dx = w0 * _shift_up(dy, 3) + w1 * _shift_up(dy, 2) + w2 * _shift_up(dy, 1) + w3 * dy
    dws = tuple(jnp.sum(dy * _shift_down(x, 3 - k), axis=0, keepdims=True) for k in range(4))
    return (dx,) + dws


_conv4.defvjp(_conv4_fwd, _conv4_bwd)


def _gdn_prep(gq, gk, gv, sm, taps, alog, dtb, h):
    q, k, v = [_silu(_conv4(x, *taps[4 * j:4 * j + 4])) for j, x in enumerate((gq, gk, gv))]
    q = q * lax.rsqrt(jnp.sum(q * q, axis=-1, keepdims=True) + NORM_EPS) * (HD ** -0.5)
    k = k * lax.rsqrt(jnp.sum(k * k, axis=-1, keepdims=True) + NORM_EPS)
    g = _lane_pick(-jnp.exp(alog) * _softplus(sm + dtb), L_GA + h)
    beta = _lane_pick(_sigmoid(sm), L_GB + h)
    return q, k, v, g, beta


def _split(x, n):
    parts, rest = [], x
    for i in range(n):
        parts.append(rest.astype(BF16))
        if i + 1 < n:
            rest = rest - parts[-1].astype(F32)
    return parts


def _raw_dot(a, b, form):
    lead = a.ndim - 2
    ca, cb = {"nn": (1, 0), "nt": (1, 1), "tn": (0, 0)}[form]
    batch = ((0,), (0,)) if lead else ((), ())
    return lax.dot_general(a, b, (((ca + lead,), (cb + lead,)), batch), preferred_element_type=F32)


def _pdot_impl(a, b, form, mode):
    if mode == "1":
        return _raw_dot(a.astype(BF16), b.astype(BF16), form)
    if mode == "3":
        (ah, al), (bh, bl) = _split(a, 2), _split(b, 2)
        return _raw_dot(ah, bh, form) + (_raw_dot(al, bh, form) + _raw_dot(ah, bl, form))
    if mode == "xa":
        return sum(_raw_dot(a.astype(BF16), t, form) for t in reversed(_split(b, 3)))
    return sum(_raw_dot(t, b.astype(BF16), form) for t in reversed(_split(a, 3)))


@functools.partial(jax.custom_vjp, nondiff_argnums=(2, 3))
def _pdot(a, b, form, mode):
    return _pdot_impl(a, b, form, mode)


def _pdot_fwd(a, b, form, mode):
    return _pdot_impl(a, b, form, mode), (a, b)


def _pdot_bwd(form, mode, res, ct):
    a, b = res
    da_args, db_args = {"nn": ((ct, b, "nt"), (a, ct, "tn")), "nt": ((ct, b, "nn"), (ct, a, "tn")),
                        "tn": ((b, ct, "nt"), (a, ct, "nn"))}[form]

    def side(args, exact):
        if mode in ("1", "3"):
            return mode
        return "xa" if args[0] is exact else "xb"

    if mode == "xa":
        return jnp.zeros_like(a), _pdot_impl(*db_args, side(db_args, a))
    if mode == "xb":
        return _pdot_impl(*da_args, side(da_args, b)), jnp.zeros_like(b)
    return _pdot_impl(*da_args, mode), _pdot_impl(*db_args, mode)


_pdot.defvjp(_pdot_fwd, _pdot_bwd)

GDN_QK, GDN_INV, GDN_SCAN = "1", "1", "1"


@jax.custom_vjp
def _tri_inv(low):
    eye = (_iota((CHUNK, CHUNK), 0) == _iota((CHUNK, CHUNK), 1)).astype(F32)
    inv = eye - low
    pw = low
    for _ in range(5):
        pw = _pdot_impl(pw, pw, "nn", GDN_INV)
        inv = inv + _pdot_impl(inv, pw, "nn", GDN_INV)
    return inv


def _tri_inv_fwd(low):
    inv = _tri_inv(low)
    return inv, inv


def _tri_inv_bwd(inv, ct):
    return (-_pdot_impl(_pdot_impl(inv, ct, "tn", GDN_INV), inv, "nt", GDN_INV),)


_tri_inv.defvjp(_tri_inv_fwd, _tri_inv_bwd)


def _gdn_intra(q, k, v, g, beta):
    n = q.shape[0]
    r, c = _iota((CHUNK, CHUNK), 0), _iota((CHUNK, CHUNK), 1)
    tril, strict = r >= c, r > c
    trilf = jnp.broadcast_to(tril.astype(F32), (n, CHUNK, CHUNK))
    gcm = _pdot(trilf, jnp.broadcast_to(g, (n, CHUNK, CHUNK)), "nn", "xa")
    gcf = _pdot(trilf, jnp.broadcast_to(g, (n, CHUNK, HD)), "nn", "xa")
    lane0 = (_iota((1, 1, CHUNK), 2) == 0).astype(F32)
    gcr = _pdot(jnp.ones((n, CHUNK, CHUNK), F32), gcm * lane0, "nt", "xa")
    decay = jnp.where(tril, jnp.exp(jnp.where(tril, gcm - gcr, 0.0)), 0.0)
    egc = jnp.exp(gcf)
    kb = k * beta
    low = jnp.where(strict, _pdot(kb, k, "nt", GDN_QK) * decay, 0.0)
    inv = _tri_inv(low)
    u = _pdot(inv, v * beta, "nn", GDN_INV)
    w = _pdot(inv, kb * egc, "nn", GDN_INV)
    at = jnp.where(tril, _pdot(q, k, "nt", GDN_QK) * decay, 0.0)
    gl = jnp.sum(jnp.broadcast_to(g, (n, CHUNK, HD)), axis=1, keepdims=True)
    return u, w, q * egc, at, k * jnp.exp(gl - gcf), gl


def _gdn_step(s, u, w, qg, at, kd, gl):
    vn = u - _pdot(w, s, "nn", GDN_SCAN)
    o = _pdot(qg, s, "nn", GDN_SCAN) + _pdot(at, vn, "nn", GDN_SCAN)
    s2 = s * jnp.exp(gl) + _pdot(kd, vn, "tn", GDN_SCAN)
    return o, s2


SCAN_HEADS = 3


def _gdn_chunked_scratch(nc):
    big = pltpu.VMEM((nc, CHUNK, HD), F32)
    return [big, big, big, pltpu.VMEM((nc, CHUNK, 1), F32), pltpu.VMEM((nc, CHUNK, 1), F32)]


def _gdn_term_shapes(nc):
    return [(nc, CHUNK, HD), (nc, CHUNK, HD), (nc, CHUNK, HD), (nc, CHUNK, CHUNK), (nc, CHUNK, HD), (nc, 1, HD)]


def _per_head(shape, heads=None):
    lead = (None,) if heads is None else (heads,)
    return pl.BlockSpec(lead + tuple(shape), lambda h: (h,) + (0,) * len(shape), pipeline_mode=ONE_BUFFER)


def _gdn_in_specs(t):
    cw = lambda cb: pl.BlockSpec((4, HD), lambda h, cb=cb: (0, cb + h))
    return [_pcol(t, GQ), _pcol(t, GK), _pcol(t, GV), _smcol(t), cw(0), cw(NG), cw(2 * NG), _small(1), _small(1)]


def _taps(wq, wk, wv):
    return tuple(w[k:k + 1, :] for w in (wq, wk, wv) for k in range(4))


def _gdn_stage(vals, refs):
    nc = refs[0].shape[0]
    for v, r in zip(vals, refs):
        r[...] = v.reshape(nc, CHUNK, v.shape[-1])


def _gdn_intra_all(chunked, intra):
    nc = chunked[0].shape[0]
    grp_n = math.gcd(nc, GROUP)

    def grp(i, carry):
        sl = pl.ds(pl.multiple_of(i * grp_n, grp_n), grp_n)
        for r, val in zip(intra, _gdn_intra(*[c[sl] for c in chunked])):
            r[sl] = val
        return carry

    lax.fori_loop(0, nc // grp_n, grp, 0)


def _gdn_fwd(pa, pb, conv, alog, dtb):
    t = pa.shape[0]
    nc = t // CHUNK
    terms = _gdn_term_shapes(nc)

    def body(gq, gk, gv, sm, wq, wk, wv, al, db, o, *rest):
        h = pl.program_id(0)
        intra, states, chunked = rest[:6], rest[6], rest[7:]
        _gdn_stage(_gdn_prep(gq[...], gk[...], gv[...], sm[...], _taps(wq, wk, wv), al[...], db[...], h), chunked)
        _gdn_intra_all(chunked, intra)

        def step(c, s):
            states[c] = s
            oc, s2 = _gdn_step(s, *[r[c] for r in intra])
            o[pl.ds(pl.multiple_of(c * CHUNK, CHUNK), CHUNK), :] = oc
            return s2

        lax.fori_loop(0, nc, step, jnp.zeros((HD, HD), F32))

    outs = pl.pallas_call(
        body, grid=(NG,), name="gdn_fwd", in_specs=_gdn_in_specs(t),
        out_specs=[_head(t)] + [_per_head(sh) for sh in terms] + [_per_head((nc, HD, HD))],
        out_shape=[SDS((t, NG * HD), F32)] + [SDS((NG,) + sh, F32) for sh in terms] + [SDS((NG, nc, HD, HD), F32)],
        scratch_shapes=_gdn_chunked_scratch(nc), compiler_params=_cp("parallel"),
    )(pa, pa, pa, pb, conv, conv, conv, alog, dtb)
    return outs[0], list(outs[1:])


def _gdn_bwd_scan(saved, do_raw):
    nc = saved[0].shape[1]
    terms = _gdn_term_shapes(nc)

    def body(*refs):
        intra, states, do, outs = refs[:6], refs[6], refs[7], refs[8:]

        def bwd(i, dss):
            c = nc - 1 - i
            rows = pl.ds(pl.multiple_of(c * CHUNK, CHUNK), CHUNK)
            new = []
            for hh in range(SCAN_HEADS):
                _, vjp = jax.vjp(_gdn_step, states[hh, c], *[r[hh, c] for r in intra])
                grads = vjp((do[rows, hh * HD:(hh + 1) * HD], dss[hh]))
                for r, gval in zip(outs, grads[1:]):
                    r[hh, c] = gval
                new.append(grads[0])
            return tuple(new)

        lax.fori_loop(0, nc, bwd, tuple(jnp.zeros((HD, HD), F32) for _ in range(SCAN_HEADS)))

    return pl.pallas_call(
        body, grid=(NG // SCAN_HEADS,), name="gdn_bwd_scan",
        in_specs=[_per_head(sh, SCAN_HEADS) for sh in terms] + [_per_head((nc, HD, HD), SCAN_HEADS)]
        + [pl.BlockSpec((nc * CHUNK, SCAN_HEADS * HD), lambda h: (0, h), pipeline_mode=ONE_BUFFER)],
        out_specs=[_per_head(sh, SCAN_HEADS) for sh in terms],
        out_shape=[SDS((NG,) + sh, F32) for sh in terms], compiler_params=_cp("parallel"),
    )(*saved, do_raw)


def _gdn_bwd(pa, pb, conv, alog, dtb, dterms):
    t = pa.shape[0]
    nc = t // CHUNK
    terms = _gdn_term_shapes(nc)

    def body(*refs):
        gq, gk, gv, sm, wq, wk, wv, al, db = refs[:9]
        dintra = refs[9:15]
        dgq, dgk, dgv, dsm, dwq, dwk, dwv, dal, ddb = refs[15:24]
        chunked = refs[24:]
        h = pl.program_id(0)
        _gdn_stage(_gdn_prep(gq[...], gk[...], gv[...], sm[...], _taps(wq, wk, wv), al[...], db[...], h), chunked)
        grp_n = math.gcd(nc, GROUP)

        def grp(i, carry):
            sl = pl.ds(pl.multiple_of(i * grp_n, grp_n), grp_n)
            _, vjp = jax.vjp(_gdn_intra, *[r[sl] for r in chunked])
            for r, gval in zip(chunked, vjp(tuple(r[sl] for r in dintra))):
                r[sl] = gval
            return carry

        lax.fori_loop(0, nc // grp_n, grp, 0)
        _, prep_vjp = jax.vjp(
            lambda *a: _gdn_prep(*a, h), gq[...], gk[...], gv[...], sm[...], _taps(wq, wk, wv), al[...], db[...])
        grads = prep_vjp(tuple(r[...].reshape(t, r.shape[-1]) for r in chunked))
        for r, gval in zip((dgq, dgk, dgv), grads[:3]):
            r[...] = gval.astype(r.dtype)
        for j, r in enumerate((dwq, dwk, dwv)):
            for k in range(4):
                r[k:k + 1, :] = grads[4][4 * j + k]

        @pl.when(h == 0)
        def _():
            for r in (dsm, dal, ddb):
                r[...] = jnp.zeros_like(r)

        dsm[...] += grads[3]
        dal[...] += grads[5]
        ddb[...] += grads[6]

    head = _head(t)
    taps = pl.BlockSpec((4, HD), lambda h: (0, h))
    return pl.pallas_call(
        body, grid=(NG,), name="gdn_bwd", in_specs=_gdn_in_specs(t) + [_per_head(sh) for sh in terms],
        out_specs=[head, head, head, _small(t), taps, taps, taps, _small(1), _small(1)],
        out_shape=[SDS((t, NG * HD), BF16)] * 3 + [SDS((t, HD), F32)] + [SDS((4, NG * HD), F32)] * 3 + [SDS((1, HD), F32)] * 2,
        scratch_shapes=_gdn_chunked_scratch(nc), compiler_params=_cp("arbitrary"),
    )(pa, pa, pa, pb, conv, conv, conv, alog, dtb, *dterms)


def _gdn_post(o, z, gain):
    return (jnp.concatenate(
        [_rms(o[:, h * HD:(h + 1) * HD], gain) * _silu(z[:, h * HD:(h + 1) * HD]) for h in range(NG)], axis=1),)


def _place():
    return lax.axis_index("x"), lax.axis_index("y"), lax.axis_index("c")


def _all_gather(name, shard):
    def body(x_ref, out_ref, send_sems, recv_sems, local_sem):
        x, y, c = _place()
        me, sibling = (x, y, c), (x, y, 1 - c)
        chips = [(1 - x, y), (x, 1 - y), (1 - x, 1 - y)]

        def blk(px, py, pc):
            return out_ref.at[4 * px + 2 * py + pc]

        def copy(k, block, to, src=None):
            return pltpu.make_async_remote_copy(
                src_ref=blk(*block) if src is None else src, dst_ref=blk(*block),
                send_sem=send_sems.at[k], recv_sem=recv_sems.at[k], device_id=to, device_id_type=MESH)

        mine = pltpu.make_async_copy(x_ref, blk(*me), local_sem)
        mine.start()
        first = [copy(0, me, sibling, src=x_ref)]
        first += [copy(1 + j, me, (*chip, c), src=x_ref) for j, chip in enumerate(chips)]
        for cp in first:
            cp.start()
        passed = [copy(4 + j, (*chip, c), sibling) for j, chip in enumerate(chips)]
        for j, chip in enumerate(chips):
            copy(1 + j, (*chip, c), me).wait_recv()
            passed[j].start()
        copy(0, sibling, me).wait_recv()
        for j, chip in enumerate(chips):
            copy(4 + j, (*chip, 1 - c), me).wait_recv()
        for cp in first + passed:
            cp.wait_send()
        mine.wait()

    return pl.pallas_call(
        body, name=name, out_shape=SDS((N_DEV,) + shard.shape, shard.dtype),
        in_specs=[pl.BlockSpec(memory_space=pltpu.HBM)], out_specs=pl.BlockSpec(memory_space=pltpu.HBM),
        scratch_shapes=[pltpu.SemaphoreType.DMA((7,)), pltpu.SemaphoreType.DMA((7,)), pltpu.SemaphoreType.DMA],
    )(shard)


def _scatter_exchange(name, full):
    def body(g_ref, out_ref, send_sems, recv_sems, local_sem):
        x, y, c = _place()
        me = 4 * x + 2 * y + c
        mine = pltpu.make_async_copy(g_ref.at[me], out_ref.at[me], local_sem)
        mine.start()
        sends, recvs = [], []
        for k in range(1, N_DEV):
            px = 1 - x if k & 4 else x
            py = 1 - y if k & 2 else y
            pc = 1 - c if k & 1 else c
            peer = 4 * px + 2 * py + pc
            sends.append(pltpu.make_async_remote_copy(
                src_ref=g_ref.at[peer], dst_ref=out_ref.at[me], send_sem=send_sems.at[k - 1],
                recv_sem=recv_sems.at[k - 1], device_id=(px, py, pc), device_id_type=MESH))
            recvs.append(pltpu.make_async_remote_copy(
                src_ref=g_ref.at[me], dst_ref=out_ref.at[peer], send_sem=send_sems.at[k - 1],
                recv_sem=recv_sems.at[k - 1], device_id=(px, py, pc), device_id_type=MESH))
        for cp in sends:
            cp.start()
        for cp in recvs:
            cp.wait_recv()
        for cp in sends:
            cp.wait_send()
        mine.wait()

    return pl.pallas_call(
        body, name=name, out_shape=SDS(full.shape, full.dtype),
        in_specs=[pl.BlockSpec(memory_space=pltpu.HBM)], out_specs=pl.BlockSpec(memory_space=pltpu.HBM),
        scratch_shapes=[pltpu.SemaphoreType.DMA((7,)), pltpu.SemaphoreType.DMA((7,)), pltpu.SemaphoreType.DMA],
    )(full)


def _sum_blocks(name, parts):
    _, r, c = parts.shape
    tr = 64 if r % 64 == 0 else r

    def body(x, o):
        acc = x[0].astype(F32)
        for d in range(1, N_DEV):
            acc = acc + x[d].astype(F32)
        o[...] = acc

    return pl.pallas_call(
        body, grid=(r // tr,), name=name, in_specs=[pl.BlockSpec((N_DEV, tr, c), lambda i: (0, i, 0))],
        out_specs=pl.BlockSpec((tr, c), lambda i: (i, 0)), out_shape=SDS((r, c), F32), compiler_params=_cp("parallel"),
    )(parts)


def _reduce_scatter(name, full):
    return _sum_blocks(name + "_sum", _scatter_exchange(name, full))


def _all_reduce_small(name, x, reduce):
    m_per, n = x.shape

    def body(x_ref, out_ref, send_sems, recv_sems, local_sem):
        px, py, pc = _place()
        me, sibling = (px, py, pc), (px, py, 1 - pc)
        chips = [(1 - px, py), (px, 1 - py), (1 - px, 1 - py)]
        buf = out_ref

        def rows(qx, qy, qc):
            return buf.at[pl.ds((4 * qx + 2 * qy + qc) * m_per, m_per), :]

        def copy(k, block, to, src=None):
            return pltpu.make_async_remote_copy(
                src_ref=rows(*block) if src is None else src, dst_ref=rows(*block),
                send_sem=send_sems.at[k], recv_sem=recv_sems.at[k], device_id=to, device_id_type=MESH)

        mine = pltpu.make_async_copy(x_ref, rows(*me), local_sem)
        mine.start()
        first = [copy(0, me, sibling, src=x_ref)]
        first += [copy(1 + j, me, (*chip, pc), src=x_ref) for j, chip in enumerate(chips)]
        for cp in first:
            cp.start()
        passed = [copy(4 + j, (*chip, pc), sibling) for j, chip in enumerate(chips)]
        for j, chip in enumerate(chips):
            copy(1 + j, (*chip, pc), me).wait_recv()
            passed[j].start()
        copy(0, sibling, me).wait_recv()
        for j, chip in enumerate(chips):
            copy(4 + j, (*chip, 1 - pc), me).wait_recv()
        for cp in first + passed:
            cp.wait_send()
        mine.wait()

    gathered = pl.pallas_call(
        body, name=name, out_shape=SDS((N_DEV * m_per, n), x.dtype),
        in_specs=[pl.BlockSpec(memory_space=pltpu.VMEM)], out_specs=pl.BlockSpec(memory_space=pltpu.VMEM),
        scratch_shapes=[pltpu.SemaphoreType.DMA((7,)), pltpu.SemaphoreType.DMA((7,)), pltpu.SemaphoreType.DMA],
    )(x)
    if not reduce:
        return gathered
    return _sum_blocks(name + "_sum", gathered.reshape(N_DEV, m_per, n))


HBM_SPEC = pl.BlockSpec(memory_space=pltpu.HBM)
SEM_SPEC = pl.BlockSpec(memory_space=pltpu.SEMAPHORE)
EFFECT = pltpu.SideEffectType.DATAFLOW_SIDE_EFFECTING


def _copies_start(name, bufs, n_remote, n_local, build, deps):
    nb, nd = len(bufs), len(deps)
    sem_shapes = [pltpu.SemaphoreType.DMA((n_remote,)), pltpu.SemaphoreType.DMA((n_remote,))]
    if n_local:
        sem_shapes.append(pltpu.SemaphoreType.DMA((n_local,)))
    ns = len(sem_shapes)

    def body(*refs):
        sems = refs[nb + nd:nb + nd + ns]
        remote, local = build(refs[:nb], *sems, *([None] * (3 - ns)))
        for cp in local + remote:
            cp.start()
        refs[-1][...] = jnp.zeros((8, HD), F32)

    outs = pl.pallas_call(
        body, name=name,
        out_shape=(*sem_shapes, *[pltpu.HBM(b.shape, b.dtype) for b in bufs], SDS((8, HD), F32)),
        in_specs=[HBM_SPEC] * nb + [ANY_SPEC] * nd,
        out_specs=(*[SEM_SPEC] * ns, *[HBM_SPEC] * nb, pl.BlockSpec(memory_space=pltpu.VMEM)),
        input_output_aliases={i: ns + i for i in range(nb)},
        compiler_params=pltpu.CompilerParams(has_side_effects=EFFECT),
    )(*[pltpu.with_memory_space_constraint(b, pltpu.HBM) for b in bufs], *deps)
    return list(outs[:ns]), list(outs[ns:ns + nb]), outs[-1]


def _copies_wait(name, bufs, sems, build, after):
    nb, ns = len(bufs), len(sems)

    def body(*refs):
        remote, local = build(refs[:nb], *refs[nb:nb + ns], *([None] * (3 - ns)))
        for cp in local:
            cp.wait()
        for cp in remote:
            cp.wait_send()
            cp.wait_recv()

    outs = pl.pallas_call(
        body, name=name, out_shape=tuple(pltpu.HBM(b.shape, b.dtype) for b in bufs),
        in_specs=[HBM_SPEC] * nb + [SEM_SPEC] * ns + [ANY_SPEC] * len(after), out_specs=tuple([HBM_SPEC] * nb),
        input_output_aliases={i: i for i in range(nb)},
        compiler_params=pltpu.CompilerParams(has_side_effects=EFFECT),
    )(*bufs, *sems, *after)
    return list(outs)


def _remote(src, dst, send, recv, k, to):
    return pltpu.make_async_remote_copy(src_ref=src, dst_ref=dst, send_sem=send.at[k], recv_sem=recv.at[k],
                                        device_id=to, device_id_type=MESH)


class _Gather:
    def __init__(self, name, shards, deps):
        self.name, self.n = name, len(shards)
        lands = [lax.empty((N_DEV,) + s.shape, s.dtype) for s in shards]
        self.sems1, bufs, self.token = _copies_start(
            name + "_s1", list(shards) + lands, 4 * self.n, self.n, self._stage1(range(self.n)), deps)
        self.shards, self.lands, self.sems2 = bufs[:self.n], bufs[self.n:], {}

    def _stage1(self, idxs):
        def build(refs, send, recv, loc):
            x, y, c = _place()
            me = 4 * x + 2 * y + c
            targets = [(x, y, 1 - c), (1 - x, y, c), (x, 1 - y, c), (1 - x, 1 - y, c)]
            remote, local = [], []
            for pos, i in enumerate(idxs):
                src, land = refs[pos], refs[len(idxs) + pos]
                local.append(pltpu.make_async_copy(src, land.at[me], loc.at[i]))
                remote += [_remote(src, land.at[me], send, recv, 4 * i + k, to) for k, to in enumerate(targets)]
            return remote, local
        return build

    @staticmethod
    def _stage2(refs, send, recv, loc):
        x, y, c = _place()
        remote = []
        for pos, land in enumerate(refs):
            for j, (cx, cy) in enumerate([(1 - x, y), (x, 1 - y), (1 - x, 1 - y)]):
                blk = land.at[4 * cx + 2 * cy + c]
                remote.append(_remote(blk, blk, send, recv, 3 * pos + j, (x, y, 1 - c)))
        return remote, []

    def pass_on(self, idxs, after):
        tag, m = "".join(map(str, idxs)), len(idxs)
        bufs = _copies_wait(f"{self.name}_w1_{tag}", [self.shards[i] for i in idxs] + [self.lands[i] for i in idxs],
                            self.sems1, self._stage1(idxs), after)
        self.sems2[tag], lands, token = _copies_start(f"{self.name}_s2_{tag}", bufs[m:], 3 * m, 0, self._stage2, ())
        for pos, i in enumerate(idxs):
            self.lands[i] = lands[pos]
        return [token]

    def get(self, idxs, after):
        tag = "".join(map(str, idxs))
        return _copies_wait(f"{self.name}_w2_{tag}", [self.lands[i] for i in idxs], self.sems2[tag], self._stage2, after)


def _rows_tile(r, row_bytes, target=1 << 20):
    tr = r
    while tr % 32 == 0 and tr * row_bytes > target:
        tr //= 2
    return tr


def _pair_add(name, g, got, c):
    _, r, cols = g.shape
    tr = _rows_tile(r, cols * 2)

    def body(s, a, b, o):
        o[...] = (a[...].astype(F32) + b[...].astype(F32)).astype(o.dtype)

    return pl.pallas_call(
        body, name=name, out_shape=SDS((4, r, cols), g.dtype),
        grid_spec=pltpu.PrefetchScalarGridSpec(
            num_scalar_prefetch=1, grid=(4, r // tr),
            in_specs=[pl.BlockSpec((None, tr, cols), lambda j, i, s: (2 * j + s[0], i, 0)),
                      pl.BlockSpec((None, tr, cols), lambda j, i, s: (j, i, 0))],
            out_specs=pl.BlockSpec((None, tr, cols), lambda j, i, s: (j, i, 0))),
        compiler_params=_cp("parallel", "parallel"),
    )(c.reshape(1), g, got)


def _quad_sum(name, part, got, chip):
    _, r, cols = part.shape
    tr = _rows_tile(r, cols * 4)

    def body(s, a, b1, b2, b3, o):
        o[...] = ((a[...].astype(F32) + b1[...].astype(F32)) + b2[...].astype(F32)) + b3[...].astype(F32)

    blk = lambda k: pl.BlockSpec((None, tr, cols), lambda i, s, k=k: (jnp.bitwise_xor(s[0], k), i, 0))
    return pl.pallas_call(
        body, name=name, out_shape=SDS((r, cols), F32),
        grid_spec=pltpu.PrefetchScalarGridSpec(
            num_scalar_prefetch=1, grid=(r // tr,), in_specs=[blk(0), blk(1), blk(2), blk(3)],
            out_specs=pl.BlockSpec((tr, cols), lambda i, s: (i, 0))),
        compiler_params=_cp("parallel"),
    )(chip.reshape(1), part, got, got, got)


class _Scatter:
    def __init__(self, name, grads, deps):
        self.name, self.n = name, len(grads)
        got = [lax.empty((4,) + g.shape[1:], g.dtype) for g in grads]
        self.sems, bufs, self.token = _copies_start(name + "_s1", list(grads) + got, 4 * self.n, 0, self._stage1, deps)
        self.grads, self.got = bufs[:self.n], bufs[self.n:]

    def _stage1(self, refs, send, recv, loc):
        x, y, c = _place()
        remote = []
        for i in range(self.n):
            remote += [_remote(refs[i].at[2 * j + 1 - c], refs[self.n + i].at[j], send, recv, 4 * i + j, (x, y, 1 - c))
                       for j in range(4)]
        return remote, []

    def _stage2(self, refs, send, recv, loc):
        x, y, c = _place()
        remote = []
        for i in range(self.n):
            for k in (1, 2, 3):
                tx = 1 - x if k & 2 else x
                ty = 1 - y if k & 1 else y
                remote.append(_remote(refs[i].at[2 * tx + ty], refs[self.n + i].at[2 * x + y], send, recv,
                                      3 * i + k - 1, (tx, ty, c)))
        return remote, []

    def mid(self, after):
        bufs = _copies_wait(self.name + "_w1", self.grads + self.got, self.sems, self._stage1, after)
        c = lax.axis_index("c").astype(jnp.int32)
        parts = [_pair_add(f"{self.name}_add{i}", bufs[i], bufs[self.n + i], c) for i in range(self.n)]
        got = [lax.empty(p.shape, p.dtype) for p in parts]
        self.sems, bufs, self.token = _copies_start(self.name + "_s2", parts + got, 3 * self.n, 0, self._stage2, ())
        self.parts, self.got = bufs[:self.n], bufs[self.n:]

    def end(self, after):
        bufs = _copies_wait(self.name + "_w2", self.parts + self.got, self.sems, self._stage2, after)
        chip = (2 * lax.axis_index("x") + lax.axis_index("y")).astype(jnp.int32)
        return [_quad_sum(f"{self.name}_sum{i}", bufs[i], bufs[self.n + i], chip) for i in range(self.n)]


def _adamw(w, g, m, v):
    m = ADAM_B1 * m + (1.0 - ADAM_B1) * g
    v = ADAM_B2 * v + (1.0 - ADAM_B2) * (g * g)
    m_hat = m / (1.0 - ADAM_B1 ** ADAM_STEP)
    v_hat = v / (1.0 - ADAM_B2 ** ADAM_STEP)
    return -ADAM_LR * (m_hat / (jnp.sqrt(v_hat) + ADAM_EPS) + ADAM_WD * w), m, v


def _adamw_call(name, w, g, m, v):
    r, c = w.shape
    tm = 64 if r % 64 == 0 else r
    return _rowwise(name, _adamw, [w, g, m, v], [], [(c, F32)] * 3, tm)


_IN_COLS = 5906


def _perm_in(w):
    pad = jnp.zeros((w.shape[0], 2 * HALF - _IN_COLS), w.dtype)
    return (jnp.concatenate([w[:, 2310:4614], w[:, 4614:5382]], axis=1),
            jnp.concatenate([w[:, :2304], w[:, 5394:5906], w[:, 2304:2310], w[:, 5382:5394], pad], axis=1))


def _unperm_in(ga, gb):
    return jnp.concatenate([gb[:, :2304], gb[:, 2816:2822], ga[:, :2304], ga[:, 2304:3072], gb[:, 2822:2834],
                            gb[:, 2304:2816]], axis=1)


def _lanes(v, at):
    return jnp.pad(v, ((0, 0), (at, HD - at - v.shape[1])))


_PACK = ("norm_mix", "mem_norm", "norm_ffn", "gdn_conv", "fox_q_norm", "fox_k_norm", "gdn_out_norm", "mem_q_norm",
         "mem_k_norm", "fox_f_bias", "gdn_a_log", "gdn_dt_bias", "loss")


def _pack(vals):
    parts = [vals[n].reshape(-1, HD) for n in _PACK]
    used = sum(p.shape[0] for p in parts)
    buf = jnp.concatenate(parts + [jnp.zeros((-used % 8, HD), F32)], axis=0)
    return buf, [(n, p.shape[0]) for n, p in zip(_PACK, parts)]


def _unpack(buf, layout):
    out, at = {}, 0
    for n, rows in layout:
        out[n] = buf[at:at + rows]
        at += rows
    return out


def kernel(x, mem, norm_mix, w_in, fox_f_bias, fox_q_norm, fox_k_norm, gdn_conv, gdn_a_log, gdn_dt_bias, gdn_out_norm, mem_norm, w_mem_kv, mem_q_norm, mem_k_norm, w_out, norm_ffn, w_gate_up, w_down, loss_target, m_norm_mix, m_w_in, m_fox_f_bias, m_fox_q_norm, m_fox_k_norm, m_gdn_conv, m_gdn_a_log, m_gdn_dt_bias, m_gdn_out_norm, m_mem_norm, m_w_mem_kv, m_mem_q_norm, m_mem_k_norm, m_w_out, m_norm_ffn, m_w_gate_up, m_w_down, v_norm_mix, v_w_in, v_fox_f_bias, v_fox_q_norm, v_fox_k_norm, v_gdn_conv, v_gdn_a_log, v_gdn_dt_bias, v_gdn_out_norm, v_mem_norm, v_w_mem_kv, v_mem_q_norm, v_mem_k_norm, v_w_out, v_norm_ffn, v_w_gate_up, v_w_down):
    args = dict(locals())
    d = x.shape[2]
    me = 4 * lax.axis_index("x") + 2 * lax.axis_index("y") + lax.axis_index("c")

    cshard = gdn_conv[0].shape[1]
    conv_pad = jnp.pad(gdn_conv[0], ((0, 4), (0, 3 * HD - cshard)))
    conv_all = _all_reduce_small("ag_conv", conv_pad, False).reshape(N_DEV, 8, 3 * HD)[:, :4, :cshard]
    conv_all = conv_all.transpose(1, 0, 2).reshape(4, N_DEV * cshard)
    w_in_a, w_in_b = _perm_in(w_in[0])
    comm = _StepComm({"in_b": [w_in_b], "in_a": [w_in_a], "kv_out": [w_mem_kv[0], w_out[0]], "gate_up": [w_gate_up[0]],
                      "down": [w_down[0]]}, [conv_all])

    grad_x, loss_local, small_grads = _local_step(
        x[0], mem[0], loss_target[0], norm_mix, fox_f_bias, fox_q_norm, fox_k_norm, gdn_a_log, gdn_dt_bias,
        gdn_out_norm, mem_norm, mem_q_norm, mem_k_norm, norm_ffn, conv_all, comm)

    red = comm.finish([grad_x])
    grads = {"w_down": red["ffn"][0], "w_gate_up": red["ffn"][1], "w_out": red["a"][1], "w_mem_kv": red["b"][1],
             "w_in": _unperm_in(red["a"][0], red["b"][0])}
    small_grads["loss"] = jnp.broadcast_to(loss_local, (1, HD))
    packed, layout = _pack(small_grads)
    small = _unpack(_all_reduce_small("ar_small", packed, True), layout)
    loss = small["loss"][0, 0]
    six = {"fox_f_bias": L_FF, "gdn_a_log": L_GA, "gdn_dt_bias": L_GA}
    for n, rows_n in layout[:-1]:
        gsm = small[n]
        if n == "gdn_conv":
            gsm = lax.dynamic_slice(gsm.reshape(4, N_DEV * cshard), (0, me * cshard), (4, cshard))[None]
        elif n in six:
            gsm = gsm[:, six[n]:six[n] + 6]
        else:
            gsm = gsm.reshape(1, rows_n * HD)
        grads[n] = gsm

    names = ['norm_mix', 'w_in', 'fox_f_bias', 'fox_q_norm', 'fox_k_norm', 'gdn_conv', 'gdn_a_log', 'gdn_dt_bias',
             'gdn_out_norm', 'mem_norm', 'w_mem_kv', 'mem_q_norm', 'mem_k_norm', 'w_out', 'norm_ffn', 'w_gate_up', 'w_down']
    big = ("w_in", "w_mem_kv", "w_out", "w_gate_up", "w_down")
    delta, new_m, new_v = {}, {}, {}
    for n in big:
        delta[n], new_m[n], new_v[n] = [a[None] for a in _adamw_call(
            "adamw_" + n, args[n][0], grads[n], args["m_" + n][0], args["v_" + n][0])]
        grads[n] = grads[n][None]

    def flat(a):
        a = a.reshape(1, -1)
        return jnp.pad(a, ((0, 0), (0, -a.shape[1] % HD))).reshape(-1, HD)

    smalls = [n for n in names if n not in big]
    pk = lambda pre: jnp.concatenate([flat(grads[n] if pre == "g" else args[pre + n]) for n in smalls], axis=0)
    cat = [pk(""), pk("g"), pk("m_"), pk("v_")]
    padr = -cat[0].shape[0] % 8
    cat = [jnp.pad(a, ((0, padr), (0, 0))) for a in cat]
    res = _adamw_call("adamw_small", *cat)
    at = 0
    for n in smalls:
        shape = args[n].shape
        size = math.prod(shape)
        nrow = -(-size // HD)
        for dst, src in zip((delta, new_m, new_v), res):
            dst[n] = src[at:at + nrow].reshape(-1)[:size].reshape(shape)
        at += nrow

    return (loss, grad_x[None], *[grads[n] for n in names], *[delta[n] for n in names],
            *[new_m[n] for n in names], *[new_v[n] for n in names])


class _StepComm:
    def __init__(self, shard_groups, after):
        self.groups, shards = {}, []
        for key, ws in shard_groups.items():
            self.groups[key] = list(range(len(shards), len(shards) + len(ws)))
            shards += [w.astype(BF16) for w in ws]
        self.gather = _Gather("ag", shards, after)
        self.passed, self.scatters = set(), {}

    def start_deps(self):
        return [self.gather.token]

    def pass_on(self, key, after):
        self.passed.add(key)
        return self.gather.pass_on(self.groups[key], after)

    def weights(self, key, after):
        if key not in self.passed:
            after = self.pass_on(key, after)
        return self.gather.get(self.groups[key], after)

    def send(self, tag, grads):
        blocks = [g if g.ndim == 3 else g.reshape(N_DEV, g.shape[0] // N_DEV, g.shape[1]) for g in grads]
        self.scatters[tag] = _Scatter("rs_" + tag, blocks, ())
        return [self.scatters[tag].token]

    def mid(self, tag, after):
        self.scatters[tag].mid(after)
        return [self.scatters[tag].token]

    def finish(self, after):
        return {tag: sc.end(after) for tag, sc in self.scatters.items()}


def _local_step(xs, ms, tgt, norm_mix, fox_f_bias, fox_q_norm, fox_k_norm, gdn_a_log, gdn_dt_bias, gdn_out_norm,
                mem_norm, mem_q_norm, mem_k_norm, norm_ffn, conv_all, comm):
    t, d = xs.shape
    bq = min(t, 256)
    fb, alog, dtb = _lanes(fox_f_bias, L_FF), _lanes(gdn_a_log, L_GA), _lanes(gdn_dt_bias, L_GA)
    flat = lambda w: w.reshape(-1, w.shape[-1])

    rms1 = lambda a, g: (_rms(a, g),)
    (u,) = _rowwise("norm_mix", rms1, [xs], [norm_mix], [(d, BF16)], min(t, 256), deps=comm.start_deps())
    w_in_b = flat(comm.weights("in_b", [u])[0])
    pb = _matmul("proj_in_b", u, w_in_b, NN, F32, 1024, 768)
    o_fox = _fox_fwd(pb, fb, fox_q_norm, fox_k_norm, bq)
    w_in_a = flat(comm.weights("in_a", [o_fox])[0])
    pa = _matmul("proj_in_a", u, w_in_a, NN, F32, 1024, 768)
    o_gdn_raw, gdn_saved = _gdn_fwd(pa, pb, conv_all, alog, dtb)
    zrow = (pa, NG * HD, GZ * HD // (NG * HD))
    (o_gdn,) = _rowwise("gdn_post", _gdn_post, [o_gdn_raw, zrow], [gdn_out_norm], [(NG * HD, BF16)], min(t, 256))
    w_kv_all, w_out_all = [flat(w) for w in comm.weights("kv_out", [o_gdn])]
    (mem_n,) = _rowwise("norm_mem", rms1, [ms], [mem_norm], [(d, BF16)], ms.shape[0])
    mkv = _matmul("proj_mem", mem_n, w_kv_all, NN, F32, 256, 512)
    o_mem = _mem_fwd(pb, mkv, mem_q_norm, mem_k_norm)
    deps = comm.pass_on("gate_up", [o_mem])
    mix = jnp.concatenate([o_fox, o_gdn, o_mem], axis=1)
    h1 = _matmul("proj_out", mix, w_out_all, NN, F32, 1024, 1024, residual=xs, deps=deps)
    (h1n,) = _rowwise("norm_ffn", rms1, [h1], [norm_ffn], [(d, BF16)], min(t, 256))
    (wgu,) = comm.weights("gate_up", [h1n])
    ffw = wgu.shape[2]
    gu, act = _ffn_up(h1n, wgu.reshape(2, 4, d, ffw))
    w_down_all = flat(comm.weights("down", [act])[0])
    dy, dyb, lsum = _ffn_down_loss(act, w_down_all, h1, tgt)
    loss_local = (0.5 / d) * jnp.sum(lsum[::8, ::HD])

    dgu = _ffn_down_bwd(dyb, w_down_all.reshape(4, ffw, d), gu).reshape(8, t, ffw)
    g_w_down = _matmul("grad_w_down", act, dyb, TN, BF16, 512, 2048)
    dh1n = _ffn_up_bwd_x(dgu, wgu)
    g_w_gu = _ffn_up_bwd_w(h1n, dgu)
    deps = comm.send("ffn", [g_w_down, g_w_gu])
    rms2 = lambda a, g: (_rms(a, g), a)
    dh1, g_norm_ffn = _rowwise_vjp("norm_ffn_bwd", rms2, [h1], [norm_ffn], [dh1n, dy], [F32], min(t, 256), deps=deps)
    dh1b = dh1.astype(BF16)

    dmix = _matmul("proj_out_bwd_x", dh1b, w_out_all, NT, F32, 1024, 1024)
    g_w_out = _matmul("grad_w_out", mix, dh1b, TN, BF16, 1024, 2048)
    deps = comm.mid("ffn", [dmix, g_w_out])
    do_raw, dgz, g_gon = _rowwise_vjp("gdn_post_bwd", _gdn_post, [o_gdn_raw, zrow], [gdn_out_norm],
                                      [(dmix, NG * HD, 1)], [F32, BF16], min(t, 256), deps=deps)
    dterms = _gdn_bwd_scan(gdn_saved, do_raw)
    dgq, dgk, dgv, dsm_gdn, dwq, dwk, dwv, g_alog, g_dtb = _gdn_bwd(pa, pb, conv_all, alog, dtb, dterms)
    dp_a = jnp.concatenate([dgq, dgk, dgv, dgz], axis=1)
    g_w_in_a = _matmul("grad_w_in_a", u, dp_a, TN, BF16, 512, 3072)
    deps = comm.send("a", [g_w_in_a, g_w_out])
    dmq, dmk, dmv, g_mqn, g_mkn = _mem_bwd(pb, mkv, mem_q_norm, mem_k_norm, dmix, deps=deps)
    dmkv = jnp.concatenate([dmk, dmv], axis=1).astype(BF16)
    dmem_n = _matmul("proj_mem_bwd_x", dmkv, w_kv_all, NT, F32, 256, 512)
    g_w_kv = _matmul("grad_w_kv", mem_n, dmkv, TN, BF16, 512, 512)
    g_mem_norm = _rowwise_vjp("norm_mem_bwd", rms1, [ms], [mem_norm], [dmem_n], [], ms.shape[0])[0]
    deps = comm.mid("a", [g_mem_norm, g_w_kv])
    dfq, dfk, dfv, dsm_fox, g_fb, g_fqn, g_fkn = _fox_bwd(pb, fb, fox_q_norm, fox_k_norm, dmix, bq, deps=deps)
    dp_b = jnp.concatenate([dfq, dfk, dfv, dmq, (dsm_fox + dsm_gdn).astype(BF16), jnp.zeros((t, HD), BF16)], axis=1)
    g_w_in_b = _matmul("grad_w_in_b", u, dp_b, TN, BF16, 512, 3072)
    deps = comm.send("b", [g_w_in_b, g_w_kv])
    du_a = _matmul("proj_in_bwd_a", dp_a, w_in_a, NT, F32, 1024, 1024, deps=deps)
    deps = comm.mid("b", [du_a])
    du = _matmul("proj_in_bwd_b", dp_b, w_in_b, NT, F32, 1024, 1024, residual=du_a, deps=deps)
    grad_x, g_norm_mix = _rowwise_vjp("norm_mix_bwd", rms2, [xs], [norm_mix], [du, dh1], [F32], min(t, 256))

    small_grads = {
        "norm_mix": g_norm_mix, "mem_norm": g_mem_norm, "norm_ffn": g_norm_ffn,
        "gdn_conv": jnp.concatenate([dwq, dwk, dwv], axis=1),
        "fox_q_norm": g_fqn, "fox_k_norm": g_fkn, "gdn_out_norm": g_gon, "mem_q_norm": g_mqn, "mem_k_norm": g_mkn,
        "fox_f_bias": g_fb, "gdn_a_log": g_alog, "gdn_dt_bias": g_dtb}
    return grad_x, loss_local, small_grads
```

```python
import functools
import math

import jax
import jax.numpy as jnp
from jax import lax
from jax.experimental import pallas as pl
from jax.experimental.pallas import tpu as pltpu

F32 = jnp.float32
BF16 = jnp.bfloat16
HI = lax.Precision.HIGHEST
SDS = jax.ShapeDtypeStruct

N_DEV = 8
HD = 128
NF, NG, NM = 6, 6, 4
CHUNK = 64
GROUP = 16
NORM_EPS = 1e-6
GQ, GK, GV, GZ = 0, 6, 12, 18
FQ, FK, FV, MQ, SM = 0, 6, 12, 18, 22
HALF = 24 * HD
L_FF, L_GA, L_GB = 0, 6, 12
VMEM_LIMIT = 56 * 1024 * 1024

ADAM_LR, ADAM_B1, ADAM_B2, ADAM_EPS, ADAM_WD, ADAM_STEP = 0.001, 0.9, 0.999, 1e-08, 0.01, 10

NN = (((1,), (0,)), ((), ()))
NT = (((1,), (1,)), ((), ()))
TN = (((0,), (0,)), ((), ()))
MESH = pl.DeviceIdType.MESH


def _cp(*sem):
    return pltpu.CompilerParams(dimension_semantics=tuple(sem) if sem else None, vmem_limit_bytes=VMEM_LIMIT)


def _dot(a, b, dims=NN):
    return lax.dot_general(a, b, dims, preferred_element_type=F32)


def _bdot(a, b):
    return _dot(a.astype(BF16), b.astype(BF16))


def _iota(shape, axis):
    return lax.broadcasted_iota(jnp.int32, shape, axis)


def _rms(x, gain):
    return x * lax.rsqrt(jnp.mean(x * x, axis=-1, keepdims=True) + NORM_EPS) * gain


def _sigmoid(x):
    z = jnp.exp(-jnp.abs(x))
    return jnp.where(x >= 0, 1.0 / (1.0 + z), z / (1.0 + z))


def _silu(x):
    return x * _sigmoid(x)


def _softplus(x):
    return jnp.maximum(x, 0.0) + jnp.log(1.0 + jnp.exp(-jnp.abs(x)))


def _lane_pick(x, lane):
    oh = (_iota((1, x.shape[-1]), 1) == lane).astype(F32)
    return jnp.sum(x * oh, axis=-1, keepdims=True)


def _cumsum_rows(x):
    tril = (_iota((HD, HD), 0) >= _iota((HD, HD), 1)).astype(F32)
    carry = jnp.zeros((1, x.shape[1]), F32)
    outs = []
    for b in range(x.shape[0] // HD):
        blk = x[b * HD:(b + 1) * HD]
        outs.append(jnp.dot(tril, blk, precision=HI, preferred_element_type=F32) + carry)
        carry = carry + jnp.sum(blk, axis=0, keepdims=True)
    return jnp.concatenate(outs, axis=0)


def _row_spec(r, tm):
    if isinstance(r, tuple):
        arr, width, cb = r
        return arr, pl.BlockSpec((tm, width), lambda i, cb=cb: (i, cb))
    return r, pl.BlockSpec((tm, r.shape[1]), lambda i: (i, 0))


ANY_SPEC = pl.BlockSpec(memory_space=pl.ANY)


def _rowwise(name, fn, rows, consts, outs, tm, deps=()):
    arrs, specs = zip(*[_row_spec(r, tm) for r in rows])
    n_rows = arrs[0].shape[0]
    nr, nc, nd = len(rows), len(consts), len(deps)

    def body(*refs):
        res = fn(*[r[...] for r in refs[:nr + nc]])
        for o, v in zip(refs[nr + nc + nd:], res):
            o[...] = v.astype(o.dtype)

    return pl.pallas_call(
        body, grid=(n_rows // tm,), name=name,
        in_specs=list(specs) + [pl.BlockSpec(c.shape, lambda i: (0, 0)) for c in consts] + [ANY_SPEC] * nd,
        out_specs=[pl.BlockSpec((tm, w), lambda i: (i, 0)) for w, _ in outs],
        out_shape=[SDS((n_rows, w), dt) for w, dt in outs],
        compiler_params=_cp("parallel"),
    )(*arrs, *consts, *deps)


def _rowwise_vjp(name, fn, rows, consts, cts, grad_dtypes, tm, deps=()):
    arrs, specs = zip(*[_row_spec(r, tm) for r in rows])
    ct_arrs, ct_specs = zip(*[_row_spec(r, tm) for r in cts])
    n_rows = arrs[0].shape[0]
    nr, nc, nct, ng, nd = len(rows), len(consts), len(cts), len(grad_dtypes), len(deps)
    widths = [s.block_shape[1] for s in specs[:ng]]

    def body(*refs):
        vals = [r[...].astype(F32) for r in refs[:nr + nc]]
        ctv = tuple(r[...].astype(F32) for r in refs[nr + nc:nr + nc + nct])
        _, vjp = jax.vjp(fn, *vals)
        grads = vjp(ctv)
        outs = refs[nr + nc + nct + nd:]
        for o, g in zip(outs[:ng], grads[:ng]):
            o[...] = g.astype(o.dtype)

        @pl.when(pl.program_id(0) == 0)
        def _():
            for o in outs[ng:]:
                o[...] = jnp.zeros_like(o)

        for o, g in zip(outs[ng:], grads[nr:]):
            o[...] += g

    return pl.pallas_call(
        body, grid=(n_rows // tm,), name=name,
        in_specs=list(specs) + [pl.BlockSpec(c.shape, lambda i: (0, 0)) for c in consts] + list(ct_specs)
        + [ANY_SPEC] * nd,
        out_specs=[pl.BlockSpec((tm, w), lambda i: (i, 0)) for w in widths]
        + [pl.BlockSpec(c.shape, lambda i: (0, 0)) for c in consts],
        out_shape=[SDS((n_rows, w), dt) for w, dt in zip(widths, grad_dtypes)] + [SDS(c.shape, F32) for c in consts],
        compiler_params=_cp("arbitrary"),
    )(*arrs, *consts, *ct_arrs, *deps)


def _tile(n, pref):
    t = min(n, pref)
    while n % t or (t % HD and t != n):
        t -= 1
    return t


def _matmul(name, a, b, dims, out_dtype, tm, tn, residual=None, deps=()):
    ta, tb = dims == TN, dims == NT
    m = a.shape[1] if ta else a.shape[0]
    k = a.shape[0] if ta else a.shape[1]
    n = b.shape[0] if tb else b.shape[1]
    tm, tn = _tile(m, tm), _tile(n, tn)

    def body(*refs):
        acc = _dot(refs[0][...], refs[1][...], dims)
        if residual is not None:
            acc = acc + refs[2][...]
        refs[-1][...] = acc.astype(out_dtype)

    in_specs = [pl.BlockSpec((k, tm), lambda i, j: (0, i)) if ta else pl.BlockSpec((tm, k), lambda i, j: (i, 0)),
                pl.BlockSpec((tn, k), lambda i, j: (j, 0)) if tb else pl.BlockSpec((k, tn), lambda i, j: (0, j))]
    ops = [a, b]
    if residual is not None:
        in_specs.append(pl.BlockSpec((tm, tn), lambda i, j: (i, j)))
        ops.append(residual)
    in_specs += [ANY_SPEC] * len(deps)
    ops += list(deps)
    return pl.pallas_call(
        body, grid=(m // tm, n // tn), name=name, in_specs=in_specs,
        out_specs=pl.BlockSpec((tm, tn), lambda i, j: (i, j)), out_shape=SDS((m, n), out_dtype),
        compiler_params=_cp("parallel", "parallel"),
    )(*ops)


def _ffn_up(h1n, wgu):
    t, d = h1n.shape
    w = wgu.shape[3]
    tm = _tile(t, 512)

    def body(a, b, gu, act):
        x = a[...]
        g = _dot(x, b[0])
        u = _dot(x, b[1])
        gu[0] = g.astype(BF16)
        gu[1] = u.astype(BF16)
        act[...] = (_silu(g) * u).astype(BF16)

    return pl.pallas_call(
        body, grid=(4, t // tm), name="ffn_up",
        in_specs=[pl.BlockSpec((tm, d), lambda j, i: (i, 0)), pl.BlockSpec((2, None, d, w), lambda j, i: (0, j, 0, 0))],
        out_specs=[pl.BlockSpec((2, None, tm, w), lambda j, i: (0, j, i, 0)), pl.BlockSpec((tm, w), lambda j, i: (i, j))],
        out_shape=[SDS((2, 4, t, w), BF16), SDS((t, 4 * w), BF16)],
        compiler_params=_cp("parallel", "parallel"),
    )(h1n, wgu)


def _ffn_down_loss(act, wdown, h1, target):
    t, f = act.shape
    d = wdown.shape[1]
    tm, tn = _tile(t, 1024), _tile(d, 512)

    def body(a, b, h, tg, dy, dyb, ls):
        e = _dot(a[...], b[...]) + h[...] - tg[...]
        g = e * (1.0 / d)
        dy[...] = g
        dyb[...] = g.astype(BF16)
        ls[...] = jnp.broadcast_to(jnp.sum(e * e), (8, HD))

    return pl.pallas_call(
        body, grid=(t // tm, d // tn), name="ffn_down_loss",
        in_specs=[pl.BlockSpec((tm, f), lambda i, j: (i, 0)), pl.BlockSpec((f, tn), lambda i, j: (0, j)),
                  pl.BlockSpec((tm, tn), lambda i, j: (i, j)), pl.BlockSpec((tm, tn), lambda i, j: (i, j))],
        out_specs=[pl.BlockSpec((tm, tn), lambda i, j: (i, j)), pl.BlockSpec((tm, tn), lambda i, j: (i, j)),
                   pl.BlockSpec((8, HD), lambda i, j: (i, j))],
        out_shape=[SDS((t, d), F32), SDS((t, d), BF16), SDS((8 * (t // tm), HD * (d // tn)), F32)],
        compiler_params=_cp("parallel", "parallel"),
    )(act, wdown, h1, target)


def _ffn_down_bwd(dyb, wdown4, gu):
    t, d = dyb.shape
    w = wdown4.shape[1]
    tm = _tile(t, 512)

    def body(a, b, gu_ref, out):
        da = _dot(a[...], b[...], NT)
        g = gu_ref[0].astype(F32)
        u = gu_ref[1].astype(F32)
        s = _sigmoid(g)
        out[0] = (da * u * (s * (1.0 + g * (1.0 - s)))).astype(BF16)
        out[1] = (da * g * s).astype(BF16)

    return pl.pallas_call(
        body, grid=(4, t // tm), name="ffn_down_bwd",
        in_specs=[pl.BlockSpec((tm, d), lambda j, i: (i, 0)), pl.BlockSpec((None, w, d), lambda j, i: (j, 0, 0)),
                  pl.BlockSpec((2, None, tm, w), lambda j, i: (0, j, i, 0))],
        out_specs=pl.BlockSpec((2, None, tm, w), lambda j, i: (0, j, i, 0)),
        out_shape=SDS((2, 4, t, w), BF16),
        compiler_params=_cp("parallel", "parallel"),
    )(dyb, wdown4, gu)


def _ffn_up_bwd_x(dgu, wgu):
    _, t, w = dgu.shape
    d = wgu.shape[1]
    tm = _tile(t, 512)

    def body(a, b, out):
        @pl.when(pl.program_id(1) == 0)
        def _():
            out[...] = jnp.zeros_like(out)
        out[...] += _dot(a[...], b[...], NT)

    return pl.pallas_call(
        body, grid=(t // tm, 8), name="ffn_up_bwd_x",
        in_specs=[pl.BlockSpec((None, tm, w), lambda i, j: (j, i, 0)), pl.BlockSpec((None, d, w), lambda i, j: (j, 0, 0))],
        out_specs=pl.BlockSpec((tm, d), lambda i, j: (i, 0)), out_shape=SDS((t, d), F32),
        compiler_params=_cp("parallel", "arbitrary"),
    )(dgu, wgu)


def _ffn_up_bwd_w(h1n, dgu):
    _, t, w = dgu.shape
    d = h1n.shape[1]
    tm = _tile(d, 512)

    def body(a, b, out):
        out[...] = _dot(a[...], b[...], TN).astype(BF16)

    return pl.pallas_call(
        body, grid=(8, d // tm), name="ffn_up_bwd_w",
        in_specs=[pl.BlockSpec((t, tm), lambda j, i: (0, i)), pl.BlockSpec((None, t, w), lambda j, i: (j, 0, 0))],
        out_specs=pl.BlockSpec((None, tm, w), lambda j, i: (j, i, 0)), out_shape=SDS((8, d, w), BF16),
        compiler_params=_cp("parallel", "parallel"),
    )(h1n, dgu)


def _fox_prep(fq, fk, sm, fb, qg, kg, h):
    qn = _rms(fq, qg)
    kn = _rms(fk, kg)
    c = _cumsum_rows(-_softplus(-(sm + fb)))
    ccol = _lane_pick(c, L_FF + h)
    crow = jnp.sum(c.T * (_iota((HD, 1), 0) == L_FF + h).astype(F32), axis=0, keepdims=True)
    return qn, kn, ccol, crow


def _fox_block(q, k, v, cc, cr, off):
    s = _dot(q.astype(BF16), k.astype(BF16), NT) * (HD ** -0.5) + cc - cr
    s = jnp.where(_iota(s.shape, 1) <= _iota(s.shape, 0) + off, s, -1e30)
    e = jnp.exp(s - lax.stop_gradient(jnp.max(s, axis=1, keepdims=True)))
    p = e / jnp.sum(e, axis=1, keepdims=True)
    return _dot(p.astype(BF16), v.astype(BF16))


ONE_BUFFER = pl.Buffered(1)


def _pcol(t, cb):
    return pl.BlockSpec((t, HD), lambda h, cb=cb: (0, cb + h), pipeline_mode=ONE_BUFFER)


def _smcol(t):
    return pl.BlockSpec((t, HD), lambda h: (0, SM), pipeline_mode=ONE_BUFFER)


def _head(t):
    return pl.BlockSpec((t, HD), lambda h: (0, h), pipeline_mode=ONE_BUFFER)


def _small(n):
    return pl.BlockSpec((n, HD), lambda h: (0, 0), pipeline_mode=ONE_BUFFER)


def _fox_fwd(p, fb, qg, kg, bq):
    t = p.shape[0]

    def body(fq, fk, fv, sm, fb_r, qg_r, kg_r, o, qn_s, cc_s):
        h = pl.program_id(0)
        qn, kn, ccol, crow = _fox_prep(fq[...], fk[...], sm[...], fb_r[...], qg_r[...], kg_r[...], h)
        qn_s[...] = qn
        cc_s[...] = ccol
        knb = kn.astype(BF16)
        vb = fv[...].astype(BF16)
        for i in range(t // bq):
            rows, ext = pl.ds(i * bq, bq), (i + 1) * bq
            o[rows, :] = _fox_block(qn_s[rows, :], knb[:ext], vb[:ext], cc_s[rows, :], crow[:, :ext], i * bq).astype(o.dtype)

    return pl.pallas_call(
        body, grid=(NF,), name="fox_fwd",
        in_specs=[_pcol(t, FQ), _pcol(t, FK), _pcol(t, FV), _smcol(t), _small(1), _small(1), _small(1)],
        out_specs=_head(t), out_shape=SDS((t, NF * HD), BF16),
        scratch_shapes=[pltpu.VMEM((t, HD), F32), pltpu.VMEM((t, 1), F32)],
        compiler_params=_cp("parallel"),
    )(p, p, p, p, fb, qg, kg)


def _fox_bwd(p, fb, qg, kg, dmix, bq, deps=()):
    t = p.shape[0]

    def body(*refs):
        fq, fk, fv, sm, fb_r, qg_r, kg_r, do = refs[:8]
        dfq, dfk, dfv, dsm, dfb, dqg, dkg, qn_s, cc_s, dqn_s, dcc_s, dkn_s, dv_s, dcr_s = refs[8 + len(deps):]
        h = pl.program_id(0)
        qn, kn, ccol, crow = _fox_prep(fq[...], fk[...], sm[...], fb_r[...], qg_r[...], kg_r[...], h)
        qn_s[...] = qn
        cc_s[...] = ccol
        v = fv[...]
        dkn_s[...] = jnp.zeros_like(dkn_s)
        dv_s[...] = jnp.zeros_like(dv_s)
        dcr_s[...] = jnp.zeros_like(dcr_s)

        for i in range(t // bq):
            rows, ext = pl.ds(i * bq, bq), (i + 1) * bq
            _, vjp = jax.vjp(lambda a, b, c, d, e, off=i * bq: _fox_block(a, b, c, d, e, off),
                             qn_s[rows, :], kn[:ext], v[:ext], cc_s[rows, :], crow[:, :ext])
            dq, dk, dv, dcc, dcr = vjp(do[rows, :])
            dqn_s[rows, :] = dq
            dcc_s[rows, :] = dcc
            dkn_s[:ext, :] += dk
            dv_s[:ext, :] += dv
            dcr_s[:, :ext] += dcr
        _, prep_vjp = jax.vjp(lambda a, b, c, d, e, f: _fox_prep(a, b, c, d, e, f, h),
                              fq[...], fk[...], sm[...], fb_r[...], qg_r[...], kg_r[...])
        g_fq, g_fk, g_sm, g_fb, g_qg, g_kg = prep_vjp((dqn_s[...], dkn_s[...], dcc_s[...], dcr_s[...]))
        dfq[...] = g_fq.astype(dfq.dtype)
        dfk[...] = g_fk.astype(dfk.dtype)
        dfv[...] = dv_s[...].astype(dfv.dtype)

        @pl.when(h == 0)
        def _():
            for r in (dsm, dfb, dqg, dkg):
                r[...] = jnp.zeros_like(r)

        dsm[...] += g_sm
        dfb[...] += g_fb
        dqg[...] += g_qg
        dkg[...] += g_kg

    head = _head(t)
    return pl.pallas_call(
        body, grid=(NF,), name="fox_bwd",
        in_specs=[_pcol(t, FQ), _pcol(t, FK), _pcol(t, FV), _smcol(t), _small(1), _small(1), _small(1), head]
        + [ANY_SPEC] * len(deps),
        out_specs=[head, head, head, _small(t), _small(1), _small(1), _small(1)],
        out_shape=[SDS((t, NF * HD), BF16)] * 3 + [SDS((t, HD), F32)] + [SDS((1, HD), F32)] * 3,
        scratch_shapes=[pltpu.VMEM((t, HD), F32), pltpu.VMEM((t, 1), F32), pltpu.VMEM((t, HD), F32),
                        pltpu.VMEM((t, 1), F32), pltpu.VMEM((t, HD), F32), pltpu.VMEM((t, HD), F32),
                        pltpu.VMEM((1, t), F32)],
        compiler_params=_cp("arbitrary"),
    )(p, p, p, p, fb, qg, kg, dmix, *deps)


def _mem_attn(mq, mk, mv, qg, kg):
    s = _dot(_rms(mq, qg).astype(BF16), _rms(mk, kg).astype(BF16), NT) * (HD ** -0.5)
    e = jnp.exp(s - lax.stop_gradient(jnp.max(s, axis=1, keepdims=True)))
    p = e / jnp.sum(e, axis=1, keepdims=True)
    return _dot(p.astype(BF16), mv.astype(BF16))


def _mem_fwd(p, mkv, qg, kg):
    t, ml = p.shape[0], mkv.shape[0]

    def body(mq, mk, mv, qg_r, kg_r, o):
        o[...] = _mem_attn(mq[...], mk[...], mv[...], qg_r[...], kg_r[...]).astype(o.dtype)

    return pl.pallas_call(
        body, grid=(NM,), name="mem_fwd",
        in_specs=[_pcol(t, MQ), pl.BlockSpec((ml, HD), lambda h: (0, h)), pl.BlockSpec((ml, HD), lambda h: (0, NM + h)),
                  _small(1), _small(1)],
        out_specs=pl.BlockSpec((t, HD), lambda h: (0, h)), out_shape=SDS((t, NM * HD), BF16),
        compiler_params=_cp("parallel"),
    )(p, mkv, mkv, qg, kg)


def _mem_bwd(p, mkv, qg, kg, dmix, deps=()):
    t, ml = p.shape[0], mkv.shape[0]

    def body(*refs):
        mq, mk, mv, qg_r, kg_r, do = refs[:6]
        dmq, dmk, dmv, dqg, dkg = refs[6 + len(deps):]
        _, vjp = jax.vjp(_mem_attn, mq[...], mk[...], mv[...], qg_r[...], kg_r[...])
        g_q, g_k, g_v, g_qg, g_kg = vjp(do[...])
        dmq[...] = g_q.astype(dmq.dtype)
        dmk[...] = g_k
        dmv[...] = g_v

        @pl.when(pl.program_id(0) == 0)
        def _():
            dqg[...] = jnp.zeros_like(dqg)
            dkg[...] = jnp.zeros_like(dkg)

        dqg[...] += g_qg
        dkg[...] += g_kg

    return pl.pallas_call(
        body, grid=(NM,), name="mem_bwd",
        in_specs=[_pcol(t, MQ), pl.BlockSpec((ml, HD), lambda h: (0, h)), pl.BlockSpec((ml, HD), lambda h: (0, NM + h)),
                  _small(1), _small(1), pl.BlockSpec((t, HD), lambda h: (0, NF + NG + h))] + [ANY_SPEC] * len(deps),
        out_specs=[pl.BlockSpec((t, HD), lambda h: (0, h)), pl.BlockSpec((ml, HD), lambda h: (0, h)),
                   pl.BlockSpec((ml, HD), lambda h: (0, h)), _small(1), _small(1)],
        out_shape=[SDS((t, NM * HD), BF16), SDS((ml, NM * HD), F32), SDS((ml, NM * HD), F32),
                   SDS((1, HD), F32), SDS((1, HD), F32)],
        compiler_params=_cp("arbitrary"),
    )(p, mkv, mkv, qg, kg, dmix, *deps)


def _shift_down(x, s):
    if s == 0:
        return x
    return jnp.where(_iota(x.shape, 0) >= s, pltpu.roll(x, s, 0), 0.0)


def _shift_up(x, s):
    if s == 0:
        return x
    n = x.shape[0]
    return jnp.where(_iota(x.shape, 0) < n - s, pltpu.roll(x, n - s, 0), 0.0)


@jax.custom_vjp
def _conv4(x, w0, w1, w2, w3):
    return w0 * _shift_down(x, 3) + w1 * _shift_down(x, 2) + w2 * _shift_down(x, 1) + w3 * x


def _conv4_fwd(x, w0, w1, w2, w3):
    return _conv4(x, w0, w1, w2, w3), (x, w0, w1, w2, w3)


def _conv4_bwd(res, dy):
    x, w0, w1, w2, w3 = res
    dx = w0 * _shift_up(dy, 3) + w1 * _shift_up(dy, 2) + w2 * _shift_up(dy, 1) + w3 * dy
    dws = tuple(jnp.sum(dy * _shift_down(x, 3 - k), axis=0, keepdims=True) for k in range(4))
    return (dx,) + dws


_conv4.defvjp(_conv4_fwd, _conv4_bwd)


def _gdn_prep(gq, gk, gv, sm, taps, alog, dtb, h):
    q, k, v = [_silu(_conv4(x, *taps[4 * j:4 * j + 4])) for j, x in enumerate((gq, gk, gv))]
    q = q * lax.rsqrt(jnp.sum(q * q, axis=-1, keepdims=True) + NORM_EPS) * (HD ** -0.5)
    k = k * lax.rsqrt(jnp.sum(k * k, axis=-1, keepdims=True) + NORM_EPS)
    g = _lane_pick(-jnp.exp(alog) * _softplus(sm + dtb), L_GA + h)
    beta = _lane_pick(_sigmoid(sm), L_GB + h)
    return q, k, v, g, beta


def _split(x, n):
    parts, rest = [], x
    for i in range(n):
        parts.append(rest.astype(BF16))
        if i + 1 < n:
            rest = rest - parts[-1].astype(F32)
    return parts


def _raw_dot(a, b, form):
    lead = a.ndim - 2
    ca, cb = {"nn": (1, 0), "nt": (1, 1), "tn": (0, 0)}[form]
    batch = ((0,), (0,)) if lead else ((), ())
    return lax.dot_general(a, b, (((ca + lead,), (cb + lead,)), batch), preferred_element_type=F32)


def _pdot_impl(a, b, form, mode):
    if mode == "1":
        return _raw_dot(a.astype(BF16), b.astype(BF16), form)
    if mode == "3":
        (ah, al), (bh, bl) = _split(a, 2), _split(b, 2)
        return _raw_dot(ah, bh, form) + (_raw_dot(al, bh, form) + _raw_dot(ah, bl, form))
    if mode == "xa":
        return sum(_raw_dot(a.astype(BF16), t, form) for t in reversed(_split(b, 3)))
    return sum(_raw_dot(t, b.astype(BF16), form) for t in reversed(_split(a, 3)))


@functools.partial(jax.custom_vjp, nondiff_argnums=(2, 3))
def _pdot(a, b, form, mode):
    return _pdot_impl(a, b, form, mode)


def _pdot_fwd(a, b, form, mode):
    return _pdot_impl(a, b, form, mode), (a, b)


def _pdot_bwd(form, mode, res, ct):
    a, b = res
    da_args, db_args = {"nn": ((ct, b, "nt"), (a, ct, "tn")), "nt": ((ct, b, "nn"), (ct, a, "tn")),
                        "tn": ((b, ct, "nt"), (a, ct, "nn"))}[form]

    def side(args, exact):
        if mode in ("1", "3"):
            return mode
        return "xa" if args[0] is exact else "xb"

    if mode == "xa":
        return jnp.zeros_like(a), _pdot_impl(*db_args, side(db_args, a))
    if mode == "xb":
        return _pdot_impl(*da_args, side(da_args, b)), jnp.zeros_like(b)
    return _pdot_impl(*da_args, mode), _pdot_impl(*db_args, mode)


_pdot.defvjp(_pdot_fwd, _pdot_bwd)

GDN_QK, GDN_INV, GDN_SCAN = "1", "1", "1"


@jax.custom_vjp
def _tri_inv(low):
    eye = (_iota((CHUNK, CHUNK), 0) == _iota((CHUNK, CHUNK), 1)).astype(F32)
    inv = eye - low
    pw = low
    for _ in range(5):
        pw = _pdot_impl(pw, pw, "nn", GDN_INV)
        inv = inv + _pdot_impl(inv, pw, "nn", GDN_INV)
    return inv


def _tri_inv_fwd(low):
    inv = _tri_inv(low)
    return inv, inv


def _tri_inv_bwd(inv, ct):
    return (-_pdot_impl(_pdot_impl(inv, ct, "tn", GDN_INV), inv, "nt", GDN_INV),)


_tri_inv.defvjp(_tri_inv_fwd, _tri_inv_bwd)


def _gdn_intra(q, k, v, g, beta):
    n = q.shape[0]
    r, c = _iota((CHUNK, CHUNK), 0), _iota((CHUNK, CHUNK), 1)
    tril, strict = r >= c, r > c
    trilf = jnp.broadcast_to(tril.astype(F32), (n, CHUNK, CHUNK))
    gcm = _pdot(trilf, jnp.broadcast_to(g, (n, CHUNK, CHUNK)), "nn", "xa")
    gcf = _pdot(trilf, jnp.broadcast_to(g, (n, CHUNK, HD)), "nn", "xa")
    lane0 = (_iota((1, 1, CHUNK), 2) == 0).astype(F32)
    gcr = _pdot(jnp.ones((n, CHUNK, CHUNK), F32), gcm * lane0, "nt", "xa")
    decay = jnp.where(tril, jnp.exp(jnp.where(tril, gcm - gcr, 0.0)), 0.0)
    egc = jnp.exp(gcf)
    kb = k * beta
    low = jnp.where(strict, _pdot(kb, k, "nt", GDN_QK) * decay, 0.0)
    inv = _tri_inv(low)
    u = _pdot(inv, v * beta, "nn", GDN_INV)
    w = _pdot(inv, kb * egc, "nn", GDN_INV)
    at = jnp.where(tril, _pdot(q, k, "nt", GDN_QK) * decay, 0.0)
    gl = jnp.sum(jnp.broadcast_to(g, (n, CHUNK, HD)), axis=1, keepdims=True)
    return u, w, q * egc, at, k * jnp.exp(gl - gcf), gl


def _gdn_step(s, u, w, qg, at, kd, gl):
    vn = u - _pdot(w, s, "nn", GDN_SCAN)
    o = _pdot(qg, s, "nn", GDN_SCAN) + _pdot(at, vn, "nn", GDN_SCAN)
    s2 = s * jnp.exp(gl) + _pdot(kd, vn, "tn", GDN_SCAN)
    return o, s2


SCAN_HEADS = 3


def _gdn_chunked_scratch(nc):
    big = pltpu.VMEM((nc, CHUNK, HD), F32)
    return [big, big, big, pltpu.VMEM((nc, CHUNK, 1), F32), pltpu.VMEM((nc, CHUNK, 1), F32)]


def _gdn_term_shapes(nc):
    return [(nc, CHUNK, HD), (nc, CHUNK, HD), (nc, CHUNK, HD), (nc, CHUNK, CHUNK), (nc, CHUNK, HD), (nc, 1, HD)]


def _per_head(shape, heads=None, one_buffer=True):
    lead = (None,) if heads is None else (heads,)
    return pl.BlockSpec(lead + tuple(shape), lambda h: (h,) + (0,) * len(shape),
                        pipeline_mode=ONE_BUFFER if one_buffer else None)


def _gdn_in_specs(t):
    cw = lambda cb: pl.BlockSpec((4, HD), lambda h, cb=cb: (0, cb + h))
    return [_pcol(t, GQ), _pcol(t, GK), _pcol(t, GV), _smcol(t), cw(0), cw(NG), cw(2 * NG), _small(1), _small(1)]


def _taps(wq, wk, wv):
    return tuple(w[k:k + 1, :] for w in (wq, wk, wv) for k in range(4))


def _gdn_stage(vals, refs):
    nc = refs[0].shape[0]
    for v, r in zip(vals, refs):
        r[...] = v.reshape(nc, CHUNK, v.shape[-1])


def _gdn_intra_all(chunked, intra):
    nc = chunked[0].shape[0]
    grp_n = math.gcd(nc, GROUP)

    def grp(i, carry):
        sl = pl.ds(pl.multiple_of(i * grp_n, grp_n), grp_n)
        for r, val in zip(intra, _gdn_intra(*[c[sl] for c in chunked])):
            r[sl] = val
        return carry

    lax.fori_loop(0, nc // grp_n, grp, 0)


def _gdn_fwd(pa, pb, conv, alog, dtb):
    t = pa.shape[0]
    nc = t // CHUNK
    terms = _gdn_term_shapes(nc)

    def body(gq, gk, gv, sm, wq, wk, wv, al, db, o, *rest):
        h = pl.program_id(0)
        intra, states, chunked = rest[:6], rest[6], rest[7:]
        _gdn_stage(_gdn_prep(gq[...], gk[...], gv[...], sm[...], _taps(wq, wk, wv), al[...], db[...], h), chunked)
        _gdn_intra_all(chunked, intra)

        def step(c, s):
            states[c] = s
            oc, s2 = _gdn_step(s, *[r[c] for r in intra])
            o[pl.ds(pl.multiple_of(c * CHUNK, CHUNK), CHUNK), :] = oc
            return s2

        lax.fori_loop(0, nc, step, jnp.zeros((HD, HD), F32))

    outs = pl.pallas_call(
        body, grid=(NG,), name="gdn_fwd", in_specs=_gdn_in_specs(t),
        out_specs=[_head(t)] + [_per_head(sh, one_buffer=False) for sh in terms]
        + [_per_head((nc, HD, HD), one_buffer=False)],
        out_shape=[SDS((t, NG * HD), F32)] + [SDS((NG,) + sh, F32) for sh in terms] + [SDS((NG, nc, HD, HD), F32)],
        scratch_shapes=_gdn_chunked_scratch(nc), compiler_params=_cp("parallel"),
    )(pa, pa, pa, pb, conv, conv, conv, alog, dtb)
    return outs[0], list(outs[1:])


def _gdn_bwd_scan(saved, do_raw):
    nc = saved[0].shape[1]
    terms = _gdn_term_shapes(nc)

    def body(*refs):
        intra, states, do, outs = refs[:6], refs[6], refs[7], refs[8:]

        def bwd(i, dss):
            c = nc - 1 - i
            rows = pl.ds(pl.multiple_of(c * CHUNK, CHUNK), CHUNK)
            new = []
            for hh in range(SCAN_HEADS):
                _, vjp = jax.vjp(_gdn_step, states[hh, c], *[r[hh, c] for r in intra])
                grads = vjp((do[rows, hh * HD:(hh + 1) * HD], dss[hh]))
                for r, gval in zip(outs, grads[1:]):
                    r[hh, c] = gval
                new.append(grads[0])
            return tuple(new)

        lax.fori_loop(0, nc, bwd, tuple(jnp.zeros((HD, HD), F32) for _ in range(SCAN_HEADS)))

    return pl.pallas_call(
        body, grid=(NG // SCAN_HEADS,), name="gdn_bwd_scan",
        in_specs=[_per_head(sh, SCAN_HEADS) for sh in terms] + [_per_head((nc, HD, HD), SCAN_HEADS)]
        + [pl.BlockSpec((nc * CHUNK, SCAN_HEADS * HD), lambda h: (0, h), pipeline_mode=ONE_BUFFER)],
        out_specs=[_per_head(sh, SCAN_HEADS) for sh in terms],
        out_shape=[SDS((NG,) + sh, F32) for sh in terms], compiler_params=_cp("parallel"),
    )(*saved, do_raw)


def _gdn_bwd(pa, pb, conv, alog, dtb, dterms):
    t = pa.shape[0]
    nc = t // CHUNK
    terms = _gdn_term_shapes(nc)

    def body(*refs):
        gq, gk, gv, sm, wq, wk, wv, al, db = refs[:9]
        dintra = refs[9:15]
        dgq, dgk, dgv, dsm, dwq, dwk, dwv, dal, ddb = refs[15:24]
        chunked = refs[24:]
        h = pl.program_id(0)
        _gdn_stage(_gdn_prep(gq[...], gk[...], gv[...], sm[...], _taps(wq, wk, wv), al[...], db[...], h), chunked)
        grp_n = math.gcd(nc, GROUP)

        def grp(i, carry):
            sl = pl.ds(pl.multiple_of(i * grp_n, grp_n), grp_n)
            _, vjp = jax.vjp(_gdn_intra, *[r[sl] for r in chunked])
            for r, gval in zip(chunked, vjp(tuple(r[sl] for r in dintra))):
                r[sl] = gval
            return carry

        lax.fori_loop(0, nc // grp_n, grp, 0)
        _, prep_vjp = jax.vjp(
            lambda *a: _gdn_prep(*a, h), gq[...], gk[...], gv[...], sm[...], _taps(wq, wk, wv), al[...], db[...])
        grads = prep_vjp(tuple(r[...].reshape(t, r.shape[-1]) for r in chunked))
        for r, gval in zip((dgq, dgk, dgv), grads[:3]):
            r[...] = gval.astype(r.dtype)
        for j, r in enumerate((dwq, dwk, dwv)):
            for k in range(4):
                r[k:k + 1, :] = grads[4][4 * j + k]

        @pl.when(h == 0)
        def _():
            for r in (dsm, dal, ddb):
                r[...] = jnp.zeros_like(r)

        dsm[...] += grads[3]
        dal[...] += grads[5]
        ddb[...] += grads[6]

    head = _head(t)
    taps = pl.BlockSpec((4, HD), lambda h: (0, h))
    return pl.pallas_call(
        body, grid=(NG,), name="gdn_bwd", in_specs=_gdn_in_specs(t) + [_per_head(sh) for sh in terms],
        out_specs=[head, head, head, _small(t), taps, taps, taps, _small(1), _small(1)],
        out_shape=[SDS((t, NG * HD), BF16)] * 3 + [SDS((t, HD), F32)] + [SDS((4, NG * HD), F32)] * 3 + [SDS((1, HD), F32)] * 2,
        scratch_shapes=_gdn_chunked_scratch(nc), compiler_params=_cp("arbitrary"),
    )(pa, pa, pa, pb, conv, conv, conv, alog, dtb, *dterms)


def _gdn_post(o, z, gain):
    return (jnp.concatenate(
        [_rms(o[:, h * HD:(h + 1) * HD], gain) * _silu(z[:, h * HD:(h + 1) * HD]) for h in range(NG)], axis=1),)


def _place():
    return lax.axis_index("x"), lax.axis_index("y"), lax.axis_index("c")


def _all_gather(name, shard):
    def body(x_ref, out_ref, send_sems, recv_sems, local_sem):
        x, y, c = _place()
        me, sibling = (x, y, c), (x, y, 1 - c)
        chips = [(1 - x, y), (x, 1 - y), (1 - x, 1 - y)]

        def blk(px, py, pc):
            return out_ref.at[4 * px + 2 * py + pc]

        def copy(k, block, to, src=None):
            return pltpu.make_async_remote_copy(
                src_ref=blk(*block) if src is None else src, dst_ref=blk(*block),
                send_sem=send_sems.at[k], recv_sem=recv_sems.at[k], device_id=to, device_id_type=MESH)

        mine = pltpu.make_async_copy(x_ref, blk(*me), local_sem)
        mine.start()
        first = [copy(0, me, sibling, src=x_ref)]
        first += [copy(1 + j, me, (*chip, c), src=x_ref) for j, chip in enumerate(chips)]
        for cp in first:
            cp.start()
        passed = [copy(4 + j, (*chip, c), sibling) for j, chip in enumerate(chips)]
        for j, chip in enumerate(chips):
            copy(1 + j, (*chip, c), me).wait_recv()
            passed[j].start()
        copy(0, sibling, me).wait_recv()
        for j, chip in enumerate(chips):
            copy(4 + j, (*chip, 1 - c), me).wait_recv()
        for cp in first + passed:
            cp.wait_send()
        mine.wait()

    return pl.pallas_call(
        body, name=name, out_shape=SDS((N_DEV,) + shard.shape, shard.dtype),
        in_specs=[pl.BlockSpec(memory_space=pltpu.HBM)], out_specs=pl.BlockSpec(memory_space=pltpu.HBM),
        scratch_shapes=[pltpu.SemaphoreType.DMA((7,)), pltpu.SemaphoreType.DMA((7,)), pltpu.SemaphoreType.DMA],
    )(shard)


def _scatter_exchange(name, full):
    def body(g_ref, out_ref, send_sems, recv_sems, local_sem):
        x, y, c = _place()
        me = 4 * x + 2 * y + c
        mine = pltpu.make_async_copy(g_ref.at[me], out_ref.at[me], local_sem)
        mine.start()
        sends, recvs = [], []
        for k in range(1, N_DEV):
            px = 1 - x if k & 4 else x
            py = 1 - y if k & 2 else y
            pc = 1 - c if k & 1 else c
            peer = 4 * px + 2 * py + pc
            sends.append(pltpu.make_async_remote_copy(
                src_ref=g_ref.at[peer], dst_ref=out_ref.at[me], send_sem=send_sems.at[k - 1],
                recv_sem=recv_sems.at[k - 1], device_id=(px, py, pc), device_id_type=MESH))
            recvs.append(pltpu.make_async_remote_copy(
                src_ref=g_ref.at[me], dst_ref=out_ref.at[peer], send_sem=send_sems.at[k - 1],
                recv_sem=recv_sems.at[k - 1], device_id=(px, py, pc), device_id_type=MESH))
        for cp in sends:
            cp.start()
        for cp in recvs:
            cp.wait_recv()
        for cp in sends:
            cp.wait_send()
        mine.wait()

    return pl.pallas_call(
        body, name=name, out_shape=SDS(full.shape, full.dtype),
        in_specs=[pl.BlockSpec(memory_space=pltpu.HBM)], out_specs=pl.BlockSpec(memory_space=pltpu.HBM),
        scratch_shapes=[pltpu.SemaphoreType.DMA((7,)), pltpu.SemaphoreType.DMA((7,)), pltpu.SemaphoreType.DMA],
    )(full)


def _sum_blocks(name, parts):
    _, r, c = parts.shape
    tr = 64 if r % 64 == 0 else r

    def body(x, o):
        acc = x[0].astype(F32)
        for d in range(1, N_DEV):
            acc = acc + x[d].astype(F32)
        o[...] = acc

    return pl.pallas_call(
        body, grid=(r // tr,), name=name, in_specs=[pl.BlockSpec((N_DEV, tr, c), lambda i: (0, i, 0))],
        out_specs=pl.BlockSpec((tr, c), lambda i: (i, 0)), out_shape=SDS((r, c), F32), compiler_params=_cp("parallel"),
    )(parts)


def _reduce_scatter(name, full):
    return _sum_blocks(name + "_sum", _scatter_exchange(name, full))


def _all_reduce_small(name, x, reduce):
    m_per, n = x.shape

    def body(x_ref, out_ref, send_sems, recv_sems, local_sem):
        px, py, pc = _place()
        me, sibling = (px, py, pc), (px, py, 1 - pc)
        chips = [(1 - px, py), (px, 1 - py), (1 - px, 1 - py)]
        buf = out_ref

        def rows(qx, qy, qc):
            return buf.at[pl.ds((4 * qx + 2 * qy + qc) * m_per, m_per), :]

        def copy(k, block, to, src=None):
            return pltpu.make_async_remote_copy(
                src_ref=rows(*block) if src is None else src, dst_ref=rows(*block),
                send_sem=send_sems.at[k], recv_sem=recv_sems.at[k], device_id=to, device_id_type=MESH)

        mine = pltpu.make_async_copy(x_ref, rows(*me), local_sem)
        mine.start()
        first = [copy(0, me, sibling, src=x_ref)]
        first += [copy(1 + j, me, (*chip, pc), src=x_ref) for j, chip in enumerate(chips)]
        for cp in first:
            cp.start()
        passed = [copy(4 + j, (*chip, pc), sibling) for j, chip in enumerate(chips)]
        for j, chip in enumerate(chips):
            copy(1 + j, (*chip, pc), me).wait_recv()
            passed[j].start()
        copy(0, sibling, me).wait_recv()
        for j, chip in enumerate(chips):
            copy(4 + j, (*chip, 1 - pc), me).wait_recv()
        for cp in first + passed:
            cp.wait_send()
        mine.wait()

    gathered = pl.pallas_call(
        body, name=name, out_shape=SDS((N_DEV * m_per, n), x.dtype),
        in_specs=[pl.BlockSpec(memory_space=pltpu.VMEM)], out_specs=pl.BlockSpec(memory_space=pltpu.VMEM),
        scratch_shapes=[pltpu.SemaphoreType.DMA((7,)), pltpu.SemaphoreType.DMA((7,)), pltpu.SemaphoreType.DMA],
    )(x)
    if not reduce:
        return gathered
    return _sum_blocks(name + "_sum", gathered.reshape(N_DEV, m_per, n))


HBM_SPEC = pl.BlockSpec(memory_space=pltpu.HBM)
SEM_SPEC = pl.BlockSpec(memory_space=pltpu.SEMAPHORE)
EFFECT = pltpu.SideEffectType.DATAFLOW_SIDE_EFFECTING


def _copies_start(name, bufs, n_remote, n_local, build, deps):
    nb, nd = len(bufs), len(deps)
    sem_shapes = [pltpu.SemaphoreType.DMA((n_remote,)), pltpu.SemaphoreType.DMA((n_remote,))]
    if n_local:
        sem_shapes.append(pltpu.SemaphoreType.DMA((n_local,)))
    ns = len(sem_shapes)

    def body(*refs):
        sems = refs[nb + nd:nb + nd + ns]
        remote, local = build(refs[:nb], *sems, *([None] * (3 - ns)))
        for cp in local + remote:
            cp.start()
        refs[-1][...] = jnp.zeros((8, HD), F32)

    outs = pl.pallas_call(
        body, name=name,
        out_shape=(*sem_shapes, *[pltpu.HBM(b.shape, b.dtype) for b in bufs], SDS((8, HD), F32)),
        in_specs=[HBM_SPEC] * nb + [ANY_SPEC] * nd,
        out_specs=(*[SEM_SPEC] * ns, *[HBM_SPEC] * nb, pl.BlockSpec(memory_space=pltpu.VMEM)),
        input_output_aliases={i: ns + i for i in range(nb)},
        compiler_params=pltpu.CompilerParams(has_side_effects=EFFECT),
    )(*[pltpu.with_memory_space_constraint(b, pltpu.HBM) for b in bufs], *deps)
    return list(outs[:ns]), list(outs[ns:ns + nb]), outs[-1]


def _copies_wait(name, bufs, sems, build, after):
    nb, ns = len(bufs), len(sems)

    def body(*refs):
        remote, local = build(refs[:nb], *refs[nb:nb + ns], *([None] * (3 - ns)))
        for cp in local:
            cp.wait()
        for cp in remote:
            cp.wait_send()
            cp.wait_recv()

    outs = pl.pallas_call(
        body, name=name, out_shape=tuple(pltpu.HBM(b.shape, b.dtype) for b in bufs),
        in_specs=[HBM_SPEC] * nb + [SEM_SPEC] * ns + [ANY_SPEC] * len(after), out_specs=tuple([HBM_SPEC] * nb),
        input_output_aliases={i: i for i in range(nb)},
        compiler_params=pltpu.CompilerParams(has_side_effects=EFFECT),
    )(*bufs, *sems, *after)
    return list(outs)


def _remote(src, dst, send, recv, k, to):
    return pltpu.make_async_remote_copy(src_ref=src, dst_ref=dst, send_sem=send.at[k], recv_sem=recv.at[k],
                                        device_id=to, device_id_type=MESH)


class _Gather:
    def __init__(self, name, shards, deps):
        self.name, self.n = name, len(shards)
        lands = [lax.empty((N_DEV,) + s.shape, s.dtype) for s in shards]
        self.sems1, bufs, self.token = _copies_start(
            name + "_s1", list(shards) + lands, 4 * self.n, self.n, self._stage1(range(self.n)), deps)
        self.shards, self.lands, self.sems2 = bufs[:self.n], bufs[self.n:], {}

    def _stage1(self, idxs):
        def build(refs, send, recv, loc):
            x, y, c = _place()
            me = 4 * x + 2 * y + c
            targets = [(x, y, 1 - c), (1 - x, y, c), (x, 1 - y, c), (1 - x, 1 - y, c)]
            remote, local = [], []
            for pos, i in enumerate(idxs):
                src, land = refs[pos], refs[len(idxs) + pos]
                local.append(pltpu.make_async_copy(src, land.at[me], loc.at[i]))
                remote += [_remote(src, land.at[me], send, recv, 4 * i + k, to) for k, to in enumerate(targets)]
            return remote, local
        return build

    @staticmethod
    def _stage2(refs, send, recv, loc):
        x, y, c = _place()
        remote = []
        for pos, land in enumerate(refs):
            for j, (cx, cy) in enumerate([(1 - x, y), (x, 1 - y), (1 - x, 1 - y)]):
                blk = land.at[4 * cx + 2 * cy + c]
                remote.append(_remote(blk, blk, send, recv, 3 * pos + j, (x, y, 1 - c)))
        return remote, []

    def pass_on(self, idxs, after):
        tag, m = "".join(map(str, idxs)), len(idxs)
        bufs = _copies_wait(f"{self.name}_w1_{tag}", [self.shards[i] for i in idxs] + [self.lands[i] for i in idxs],
                            self.sems1, self._stage1(idxs), after)
        self.sems2[tag], lands, token = _copies_start(f"{self.name}_s2_{tag}", bufs[m:], 3 * m, 0, self._stage2, ())
        for pos, i in enumerate(idxs):
            self.lands[i] = lands[pos]
        return [token]

    def get(self, idxs, after):
        tag = "".join(map(str, idxs))
        return _copies_wait(f"{self.name}_w2_{tag}", [self.lands[i] for i in idxs], self.sems2[tag], self._stage2, after)


def _rows_tile(r, row_bytes, target=1 << 20):
    tr = r
    while tr % 32 == 0 and tr * row_bytes > target:
        tr //= 2
    return tr


def _pair_add(name, g, got, c):
    _, r, cols = g.shape
    tr = _rows_tile(r, cols * 2)

    def body(s, a, b, o):
        o[...] = (a[...].astype(F32) + b[...].astype(F32)).astype(o.dtype)

    return pl.pallas_call(
        body, name=name, out_shape=SDS((4, r, cols), g.dtype),
        grid_spec=pltpu.PrefetchScalarGridSpec(
            num_scalar_prefetch=1, grid=(4, r // tr),
            in_specs=[pl.BlockSpec((None, tr, cols), lambda j, i, s: (2 * j + s[0], i, 0)),
                      pl.BlockSpec((None, tr, cols), lambda j, i, s: (j, i, 0))],
            out_specs=pl.BlockSpec((None, tr, cols), lambda j, i, s: (j, i, 0))),
        compiler_params=_cp("parallel", "parallel"),
    )(c.reshape(1), g, got)


def _quad_sum(name, part, got, chip):
    _, r, cols = part.shape
    tr = _rows_tile(r, cols * 4)

    def body(s, a, b1, b2, b3, o):
        o[...] = ((a[...].astype(F32) + b1[...].astype(F32)) + b2[...].astype(F32)) + b3[...].astype(F32)

    blk = lambda k: pl.BlockSpec((None, tr, cols), lambda i, s, k=k: (jnp.bitwise_xor(s[0], k), i, 0))
    return pl.pallas_call(
        body, name=name, out_shape=SDS((r, cols), F32),
        grid_spec=pltpu.PrefetchScalarGridSpec(
            num_scalar_prefetch=1, grid=(r // tr,), in_specs=[blk(0), blk(1), blk(2), blk(3)],
            out_specs=pl.BlockSpec((tr, cols), lambda i, s: (i, 0))),
        compiler_params=_cp("parallel"),
    )(chip.reshape(1), part, got, got, got)


class _Scatter:
    def __init__(self, name, grads, deps):
        self.name, self.n = name, len(grads)
        got = [lax.empty((4,) + g.shape[1:], g.dtype) for g in grads]
        self.sems, bufs, self.token = _copies_start(name + "_s1", list(grads) + got, 4 * self.n, 0, self._stage1, deps)
        self.grads, self.got = bufs[:self.n], bufs[self.n:]

    def _stage1(self, refs, send, recv, loc):
        x, y, c = _place()
        remote = []
        for i in range(self.n):
            remote += [_remote(refs[i].at[2 * j + 1 - c], refs[self.n + i].at[j], send, recv, 4 * i + j, (x, y, 1 - c))
                       for j in range(4)]
        return remote, []

    def _stage2(self, refs, send, recv, loc):
        x, y, c = _place()
        remote = []
        for i in range(self.n):
            for k in (1, 2, 3):
                tx = 1 - x if k & 2 else x
                ty = 1 - y if k & 1 else y
                remote.append(_remote(refs[i].at[2 * tx + ty], refs[self.n + i].at[2 * x + y], send, recv,
                                      3 * i + k - 1, (tx, ty, c)))
        return remote, []

    def mid(self, after):
        bufs = _copies_wait(self.name + "_w1", self.grads + self.got, self.sems, self._stage1, after)
        c = lax.axis_index("c").astype(jnp.int32)
        parts = [_pair_add(f"{self.name}_add{i}", bufs[i], bufs[self.n + i], c) for i in range(self.n)]
        got = [lax.empty(p.shape, p.dtype) for p in parts]
        self.sems, bufs, self.token = _copies_start(self.name + "_s2", parts + got, 3 * self.n, 0, self._stage2, ())
        self.parts, self.got = bufs[:self.n], bufs[self.n:]

    def end(self, after):
        bufs = _copies_wait(self.name + "_w2", self.parts + self.got, self.sems, self._stage2, after)
        chip = (2 * lax.axis_index("x") + lax.axis_index("y")).astype(jnp.int32)
        return [_quad_sum(f"{self.name}_sum{i}", bufs[i], bufs[self.n + i], chip) for i in range(self.n)]


def _adamw(w, g, m, v):
    m = ADAM_B1 * m + (1.0 - ADAM_B1) * g
    v = ADAM_B2 * v + (1.0 - ADAM_B2) * (g * g)
    m_hat = m / (1.0 - ADAM_B1 ** ADAM_STEP)
    v_hat = v / (1.0 - ADAM_B2 ** ADAM_STEP)
    return -ADAM_LR * (m_hat / (jnp.sqrt(v_hat) + ADAM_EPS) + ADAM_WD * w), m, v


def _adamw_call(name, w, g, m, v):
    r, c = w.shape
    tm = 64 if r % 64 == 0 else r
    return _rowwise(name, _adamw, [w, g, m, v], [], [(c, F32)] * 3, tm)


_IN_COLS = 5906


def _perm_in(w):
    pad = jnp.zeros((w.shape[0], 2 * HALF - _IN_COLS), w.dtype)
    return (jnp.concatenate([w[:, 2310:4614], w[:, 4614:5382]], axis=1),
            jnp.concatenate([w[:, :2304], w[:, 5394:5906], w[:, 2304:2310], w[:, 5382:5394], pad], axis=1))


def _unperm_in(ga, gb):
    return jnp.concatenate([gb[:, :2304], gb[:, 2816:2822], ga[:, :2304], ga[:, 2304:3072], gb[:, 2822:2834],
                            gb[:, 2304:2816]], axis=1)


def _lanes(v, at):
    return jnp.pad(v, ((0, 0), (at, HD - at - v.shape[1])))


_PACK = ("norm_mix", "mem_norm", "norm_ffn", "gdn_conv", "fox_q_norm", "fox_k_norm", "gdn_out_norm", "mem_q_norm",
         "mem_k_norm", "fox_f_bias", "gdn_a_log", "gdn_dt_bias", "loss")


def _pack(vals):
    parts = [vals[n].reshape(-1, HD) for n in _PACK]
    used = sum(p.shape[0] for p in parts)
    buf = jnp.concatenate(parts + [jnp.zeros((-used % 8, HD), F32)], axis=0)
    return buf, [(n, p.shape[0]) for n, p in zip(_PACK, parts)]


def _unpack(buf, layout):
    out, at = {}, 0
    for n, rows in layout:
        out[n] = buf[at:at + rows]
        at += rows
    return out


def kernel(x, mem, norm_mix, w_in, fox_f_bias, fox_q_norm, fox_k_norm, gdn_conv, gdn_a_log, gdn_dt_bias, gdn_out_norm, mem_norm, w_mem_kv, mem_q_norm, mem_k_norm, w_out, norm_ffn, w_gate_up, w_down, loss_target, m_norm_mix, m_w_in, m_fox_f_bias, m_fox_q_norm, m_fox_k_norm, m_gdn_conv, m_gdn_a_log, m_gdn_dt_bias, m_gdn_out_norm, m_mem_norm, m_w_mem_kv, m_mem_q_norm, m_mem_k_norm, m_w_out, m_norm_ffn, m_w_gate_up, m_w_down, v_norm_mix, v_w_in, v_fox_f_bias, v_fox_q_norm, v_fox_k_norm, v_gdn_conv, v_gdn_a_log, v_gdn_dt_bias, v_gdn_out_norm, v_mem_norm, v_w_mem_kv, v_mem_q_norm, v_mem_k_norm, v_w_out, v_norm_ffn, v_w_gate_up, v_w_down):
    args = dict(locals())
    d = x.shape[2]
    me = 4 * lax.axis_index("x") + 2 * lax.axis_index("y") + lax.axis_index("c")

    cshard = gdn_conv[0].shape[1]
    conv_pad = jnp.pad(gdn_conv[0], ((0, 4), (0, 3 * HD - cshard)))
    conv_all = _all_reduce_small("ag_conv", conv_pad, False).reshape(N_DEV, 8, 3 * HD)[:, :4, :cshard]
    conv_all = conv_all.transpose(1, 0, 2).reshape(4, N_DEV * cshard)
    w_in_a, w_in_b = _perm_in(w_in[0])
    comm = _StepComm({"in_b": [w_in_b], "in_a": [w_in_a], "kv_out": [w_mem_kv[0], w_out[0]], "gate_up": [w_gate_up[0]],
                      "down": [w_down[0]]}, [conv_all])

    grad_x, loss_local, small_grads = _local_step(
        x[0], mem[0], loss_target[0], norm_mix, fox_f_bias, fox_q_norm, fox_k_norm, gdn_a_log, gdn_dt_bias,
        gdn_out_norm, mem_norm, mem_q_norm, mem_k_norm, norm_ffn, conv_all, comm)

    red = comm.finish([grad_x])
    grads = {"w_down": red["ffn"][0], "w_gate_up": red["ffn"][1], "w_out": red["a"][1], "w_mem_kv": red["b"][1],
             "w_in": _unperm_in(red["a"][0], red["b"][0])}
    small_grads["loss"] = jnp.broadcast_to(loss_local, (1, HD))
    packed, layout = _pack(small_grads)
    small = _unpack(_all_reduce_small("ar_small", packed, True), layout)
    loss = small["loss"][0, 0]
    six = {"fox_f_bias": L_FF, "gdn_a_log": L_GA, "gdn_dt_bias": L_GA}
    for n, rows_n in layout[:-1]:
        gsm = small[n]
        if n == "gdn_conv":
            gsm = lax.dynamic_slice(gsm.reshape(4, N_DEV * cshard), (0, me * cshard), (4, cshard))[None]
        elif n in six:
            gsm = gsm[:, six[n]:six[n] + 6]
        else:
            gsm = gsm.reshape(1, rows_n * HD)
        grads[n] = gsm

    names = ['norm_mix', 'w_in', 'fox_f_bias', 'fox_q_norm', 'fox_k_norm', 'gdn_conv', 'gdn_a_log', 'gdn_dt_bias',
             'gdn_out_norm', 'mem_norm', 'w_mem_kv', 'mem_q_norm', 'mem_k_norm', 'w_out', 'norm_ffn', 'w_gate_up', 'w_down']
    big = ("w_in", "w_mem_kv", "w_out", "w_gate_up", "w_down")
    delta, new_m, new_v = {}, {}, {}
    for n in big:
        delta[n], new_m[n], new_v[n] = [a[None] for a in _adamw_call(
            "adamw_" + n, args[n][0], grads[n], args["m_" + n][0], args["v_" + n][0])]
        grads[n] = grads[n][None]

    def flat(a):
        a = a.reshape(1, -1)
        return jnp.pad(a, ((0, 0), (0, -a.shape[1] % HD))).reshape(-1, HD)

    smalls = [n for n in names if n not in big]
    pk = lambda pre: jnp.concatenate([flat(grads[n] if pre == "g" else args[pre + n]) for n in smalls], axis=0)
    cat = [pk(""), pk("g"), pk("m_"), pk("v_")]
    padr = -cat[0].shape[0] % 8
    cat = [jnp.pad(a, ((0, padr), (0, 0))) for a in cat]
    res = _adamw_call("adamw_small", *cat)
    at = 0
    for n in smalls:
        shape = args[n].shape
        size = math.prod(shape)
        nrow = -(-size // HD)
        for dst, src in zip((delta, new_m, new_v), res):
            dst[n] = src[at:at + nrow].reshape(-1)[:size].reshape(shape)
        at += nrow

    return (loss, grad_x[None], *[grads[n] for n in names], *[delta[n] for n in names],
            *[new_m[n] for n in names], *[new_v[n] for n in names])


class _StepComm:
    def __init__(self, shard_groups, after):
        self.groups, shards = {}, []
        for key, ws in shard_groups.items():
            self.groups[key] = list(range(len(shards), len(shards) + len(ws)))
            shards += [w.astype(BF16) for w in ws]
        self.gather = _Gather("ag", shards, after)
        self.passed, self.scatters = set(), {}

    def start_deps(self):
        return [self.gather.token]

    def pass_on(self, key, after):
        self.passed.add(key)
        return self.gather.pass_on(self.groups[key], after)

    def weights(self, key, after):
        if key not in self.passed:
            after = self.pass_on(key, after)
        return self.gather.get(self.groups[key], after)

    def send(self, tag, grads):
        blocks = [g if g.ndim == 3 else g.reshape(N_DEV, g.shape[0] // N_DEV, g.shape[1]) for g in grads]
        self.scatters[tag] = _Scatter("rs_" + tag, blocks, ())
        return [self.scatters[tag].token]

    def mid(self, tag, after):
        self.scatters[tag].mid(after)
        return [self.scatters[tag].token]

    def finish(self, after):
        return {tag: sc.end(after) for tag, sc in self.scatters.items()}


def _local_step(xs, ms, tgt, norm_mix, fox_f_bias, fox_q_norm, fox_k_norm, gdn_a_log, gdn_dt_bias, gdn_out_norm,
                mem_norm, mem_q_norm, mem_k_norm, norm_ffn, conv_all, comm):
    t, d = xs.shape
    bq = min(t, 256)
    fb, alog, dtb = _lanes(fox_f_bias, L_FF), _lanes(gdn_a_log, L_GA), _lanes(gdn_dt_bias, L_GA)
    flat = lambda w: w.reshape(-1, w.shape[-1])

    rms1 = lambda a, g: (_rms(a, g),)
    (u,) = _rowwise("norm_mix", rms1, [xs], [norm_mix], [(d, BF16)], min(t, 256), deps=comm.start_deps())
    w_in_b = flat(comm.weights("in_b", [u])[0])
    pb = _matmul("proj_in_b", u, w_in_b, NN, F32, 1024, 768)
    o_fox = _fox_fwd(pb, fb, fox_q_norm, fox_k_norm, bq)
    w_in_a = flat(comm.weights("in_a", [o_fox])[0])
    pa = _matmul("proj_in_a", u, w_in_a, NN, F32, 1024, 768)
    o_gdn_raw, gdn_saved = _gdn_fwd(pa, pb, conv_all, alog, dtb)
    zrow = (pa, NG * HD, GZ * HD // (NG * HD))
    (o_gdn,) = _rowwise("gdn_post", _gdn_post, [o_gdn_raw, zrow], [gdn_out_norm], [(NG * HD, BF16)], min(t, 256))
    w_kv_all, w_out_all = [flat(w) for w in comm.weights("kv_out", [o_gdn])]
    (mem_n,) = _rowwise("norm_mem", rms1, [ms], [mem_norm], [(d, BF16)], ms.shape[0])
    mkv = _matmul("proj_mem", mem_n, w_kv_all, NN, F32, 256, 512)
    o_mem = _mem_fwd(pb, mkv, mem_q_norm, mem_k_norm)
    deps = comm.pass_on("gate_up", [o_mem])
    mix = jnp.concatenate([o_fox, o_gdn, o_mem], axis=1)
    h1 = _matmul("proj_out", mix, w_out_all, NN, F32, 1024, 1024, residual=xs, deps=deps)
    (h1n,) = _rowwise("norm_ffn", rms1, [h1], [norm_ffn], [(d, BF16)], min(t, 256))
    (wgu,) = comm.weights("gate_up", [h1n])
    ffw = wgu.shape[2]
    gu, act = _ffn_up(h1n, wgu.reshape(2, 4, d, ffw))
    w_down_all = flat(comm.weights("down", [act])[0])
    dy, dyb, lsum = _ffn_down_loss(act, w_down_all, h1, tgt)
    loss_local = (0.5 / d) * jnp.sum(lsum[::8, ::HD])

    dgu = _ffn_down_bwd(dyb, w_down_all.reshape(4, ffw, d), gu).reshape(8, t, ffw)
    g_w_down = _matmul("grad_w_down", act, dyb, TN, BF16, 512, 2048)
    dh1n = _ffn_up_bwd_x(dgu, wgu)
    g_w_gu = _ffn_up_bwd_w(h1n, dgu)
    deps = comm.send("ffn", [g_w_down, g_w_gu])
    rms2 = lambda a, g: (_rms(a, g), a)
    dh1, g_norm_ffn = _rowwise_vjp("norm_ffn_bwd", rms2, [h1], [norm_ffn], [dh1n, dy], [F32], min(t, 256), deps=deps)
    dh1b = dh1.astype(BF16)

    dmix = _matmul("proj_out_bwd_x", dh1b, w_out_all, NT, F32, 1024, 1024)
    g_w_out = _matmul("grad_w_out", mix, dh1b, TN, BF16, 1024, 2048)
    deps = comm.mid("ffn", [dmix, g_w_out])
    do_raw, dgz, g_gon = _rowwise_vjp("gdn_post_bwd", _gdn_post, [o_gdn_raw, zrow], [gdn_out_norm],
                                      [(dmix, NG * HD, 1)], [F32, BF16], min(t, 256), deps=deps)
    dterms = _gdn_bwd_scan(gdn_saved, do_raw)
    dgq, dgk, dgv, dsm_gdn, dwq, dwk, dwv, g_alog, g_dtb = _gdn_bwd(pa, pb, conv_all, alog, dtb, dterms)
    dp_a = jnp.concatenate([dgq, dgk, dgv, dgz], axis=1)
    g_w_in_a = _matmul("grad_w_in_a", u, dp_a, TN, BF16, 512, 3072)
    deps = comm.send("a", [g_w_in_a, g_w_out])
    dmq, dmk, dmv, g_mqn, g_mkn = _mem_bwd(pb, mkv, mem_q_norm, mem_k_norm, dmix, deps=deps)
    dmkv = jnp.concatenate([dmk, dmv], axis=1).astype(BF16)
    dmem_n = _matmul("proj_mem_bwd_x", dmkv, w_kv_all, NT, F32, 256, 512)
    g_w_kv = _matmul("grad_w_kv", mem_n, dmkv, TN, BF16, 512, 512)
    g_mem_norm = _rowwise_vjp("norm_mem_bwd", rms1, [ms], [mem_norm], [dmem_n], [], ms.shape[0])[0]
    deps = comm.mid("a", [g_mem_norm, g_w_kv])
    dfq, dfk, dfv, dsm_fox, g_fb, g_fqn, g_fkn = _fox_bwd(pb, fb, fox_q_norm, fox_k_norm, dmix, bq, deps=deps)
    dp_b = jnp.concatenate([dfq, dfk, dfv, dmq, (dsm_fox + dsm_gdn).astype(BF16), jnp.zeros((t, HD), BF16)], axis=1)
    g_w_in_b = _matmul("grad_w_in_b", u, dp_b, TN, BF16, 512, 3072)
    deps = comm.send("b", [g_w_in_b, g_w_kv])
    du_a = _matmul("proj_in_bwd_a", dp_a, w_in_a, NT, F32, 1024, 1024, deps=deps)
    deps = comm.mid("b", [du_a])
    du = _matmul("proj_in_bwd_b", dp_b, w_in_b, NT, F32, 1024, 1024, residual=du_a, deps=deps)
    grad_x, g_norm_mix = _rowwise_vjp("norm_mix_bwd", rms2, [xs], [norm_mix], [du, dh1], [F32], min(t, 256))

    small_grads = {
        "norm_mix": g_norm_mix, "mem_norm": g_mem_norm, "norm_ffn": g_norm_ffn,
        "gdn_conv": jnp.concatenate([dwq, dwk, dwv], axis=1),
        "fox_q_norm": g_fqn, "fox_k_norm": g_fkn, "gdn_out_norm": g_gon, "mem_q_norm": g_mqn, "mem_k_norm": g_mkn,
        "fox_f_bias": g_fb, "gdn_a_log": g_alog, "gdn_dt_bias": g_dtb}
    return grad_x, loss_local, small_grads
```

```python
import functools
import math

import jax
import jax.numpy as jnp
from jax import lax
from jax.experimental import pallas as pl
from jax.experimental.pallas import tpu as pltpu

F32 = jnp.float32
BF16 = jnp.bfloat16
HI = lax.Precision.HIGHEST
SDS = jax.ShapeDtypeStruct

N_DEV = 8
HD = 128
NF, NG, NM = 6, 6, 4
CHUNK = 64
GROUP = 16
NORM_EPS = 1e-6
GQ, GK, GV, GZ = 0, 6, 12, 18
FQ, FK, FV, MQ, SM = 0, 6, 12, 18, 22
HALF = 24 * HD
L_FF, L_GA, L_GB = 0, 6, 12
VMEM_LIMIT = 56 * 1024 * 1024

ADAM_LR, ADAM_B1, ADAM_B2, ADAM_EPS, ADAM_WD, ADAM_STEP = 0.001, 0.9, 0.999, 1e-08, 0.01, 10

NN = (((1,), (0,)), ((), ()))
NT = (((1,), (1,)), ((), ()))
TN = (((0,), (0,)), ((), ()))
MESH = pl.DeviceIdType.MESH


def _cp(*sem):
    return pltpu.CompilerParams(dimension_semantics=tuple(sem) if sem else None, vmem_limit_bytes=VMEM_LIMIT)


def _dot(a, b, dims=NN):
    return lax.dot_general(a, b, dims, preferred_element_type=F32)


def _bdot(a, b):
    return _dot(a.astype(BF16), b.astype(BF16))


def _iota(shape, axis):
    return lax.broadcasted_iota(jnp.int32, shape, axis)


def _rms(x, gain):
    return x * lax.rsqrt(jnp.mean(x * x, axis=-1, keepdims=True) + NORM_EPS) * gain


def _sigmoid(x):
    z = jnp.exp(-jnp.abs(x))
    return jnp.where(x >= 0, 1.0 / (1.0 + z), z / (1.0 + z))


def _silu(x):
    return x * _sigmoid(x)


def _softplus(x):
    return jnp.maximum(x, 0.0) + jnp.log(1.0 + jnp.exp(-jnp.abs(x)))


def _lane_pick(x, lane):
    oh = (_iota((1, x.shape[-1]), 1) == lane).astype(F32)
    return jnp.sum(x * oh, axis=-1, keepdims=True)


def _cumsum_rows(x):
    tril = (_iota((HD, HD), 0) >= _iota((HD, HD), 1)).astype(F32)
    carry = jnp.zeros((1, x.shape[1]), F32)
    outs = []
    for b in range(x.shape[0] // HD):
        blk = x[b * HD:(b + 1) * HD]
        outs.append(jnp.dot(tril, blk, precision=HI, preferred_element_type=F32) + carry)
        carry = carry + jnp.sum(blk, axis=0, keepdims=True)
    return jnp.concatenate(outs, axis=0)


def _row_spec(r, tm):
    if isinstance(r, tuple):
        arr, width, cb = r
        return arr, pl.BlockSpec((tm, width), lambda i, cb=cb: (i, cb))
    return r, pl.BlockSpec((tm, r.shape[1]), lambda i: (i, 0))


ANY_SPEC = pl.BlockSpec(memory_space=pl.ANY)


def _rowwise(name, fn, rows, consts, outs, tm, deps=()):
    arrs, specs = zip(*[_row_spec(r, tm) for r in rows])
    n_rows = arrs[0].shape[0]
    nr, nc, nd = len(rows), len(consts), len(deps)

    def body(*refs):
        res = fn(*[r[...] for r in refs[:nr + nc]])
        for o, v in zip(refs[nr + nc + nd:], res):
            o[...] = v.astype(o.dtype)

    return pl.pallas_call(
        body, grid=(n_rows // tm,), name=name,
        in_specs=list(specs) + [pl.BlockSpec(c.shape, lambda i: (0, 0)) for c in consts] + [ANY_SPEC] * nd,
        out_specs=[pl.BlockSpec((tm, w), lambda i: (i, 0)) for w, _ in outs],
        out_shape=[SDS((n_rows, w), dt) for w, dt in outs],
        compiler_params=_cp("parallel"),
    )(*arrs, *consts, *deps)


def _rowwise_vjp(name, fn, rows, consts, cts, grad_dtypes, tm, deps=()):
    arrs, specs = zip(*[_row_spec(r, tm) for r in rows])
    ct_arrs, ct_specs = zip(*[_row_spec(r, tm) for r in cts])
    n_rows = arrs[0].shape[0]
    nr, nc, nct, ng, nd = len(rows), len(consts), len(cts), len(grad_dtypes), len(deps)
    widths = [s.block_shape[1] for s in specs[:ng]]

    def body(*refs):
        vals = [r[...].astype(F32) for r in refs[:nr + nc]]
        ctv = tuple(r[...].astype(F32) for r in refs[nr + nc:nr + nc + nct])
        _, vjp = jax.vjp(fn, *vals)
        grads = vjp(ctv)
        outs = refs[nr + nc + nct + nd:]
        for o, g in zip(outs[:ng], grads[:ng]):
            o[...] = g.astype(o.dtype)

        @pl.when(pl.program_id(0) == 0)
        def _():
            for o in outs[ng:]:
                o[...] = jnp.zeros_like(o)

        for o, g in zip(outs[ng:], grads[nr:]):
            o[...] += g

    return pl.pallas_call(
        body, grid=(n_rows // tm,), name=name,
        in_specs=list(specs) + [pl.BlockSpec(c.shape, lambda i: (0, 0)) for c in consts] + list(ct_specs)
        + [ANY_SPEC] * nd,
        out_specs=[pl.BlockSpec((tm, w), lambda i: (i, 0)) for w in widths]
        + [pl.BlockSpec(c.shape, lambda i: (0, 0)) for c in consts],
        out_shape=[SDS((n_rows, w), dt) for w, dt in zip(widths, grad_dtypes)] + [SDS(c.shape, F32) for c in consts],
        compiler_params=_cp("arbitrary"),
    )(*arrs, *consts, *ct_arrs, *deps)


def _tile(n, pref):
    t = min(n, pref)
    while n % t or (t % HD and t != n):
        t -= 1
    return t


def _matmul(name, a, b, dims, out_dtype, tm, tn, residual=None, deps=()):
    ta, tb = dims == TN, dims == NT
    m = a.shape[1] if ta else a.shape[0]
    k = a.shape[0] if ta else a.shape[1]
    n = b.shape[0] if tb else b.shape[1]
    tm, tn = _tile(m, tm), _tile(n, tn)

    def body(*refs):
        acc = _dot(refs[0][...], refs[1][...], dims)
        if residual is not None:
            acc = acc + refs[2][...]
        refs[-1][...] = acc.astype(out_dtype)

    in_specs = [pl.BlockSpec((k, tm), lambda i, j: (0, i)) if ta else pl.BlockSpec((tm, k), lambda i, j: (i, 0)),
                pl.BlockSpec((tn, k), lambda i, j: (j, 0)) if tb else pl.BlockSpec((k, tn), lambda i, j: (0, j))]
    ops = [a, b]
    if residual is not None:
        in_specs.append(pl.BlockSpec((tm, tn), lambda i, j: (i, j)))
        ops.append(residual)
    in_specs += [ANY_SPEC] * len(deps)
    ops += list(deps)
    return pl.pallas_call(
        body, grid=(m // tm, n // tn), name=name, in_specs=in_specs,
        out_specs=pl.BlockSpec((tm, tn), lambda i, j: (i, j)), out_shape=SDS((m, n), out_dtype),
        compiler_params=_cp("parallel", "parallel"),
    )(*ops)


def _ffn_up(h1n, wgu):
    t, d = h1n.shape
    w = wgu.shape[3]
    tm = _tile(t, 512)

    def body(a, b, gu, act):
        x = a[...]
        g = _dot(x, b[0])
        u = _dot(x, b[1])
        gu[0] = g.astype(BF16)
        gu[1] = u.astype(BF16)
        act[...] = (_silu(g) * u).astype(BF16)

    return pl.pallas_call(
        body, grid=(4, t // tm), name="ffn_up",
        in_specs=[pl.BlockSpec((tm, d), lambda j, i: (i, 0)), pl.BlockSpec((2, None, d, w), lambda j, i: (0, j, 0, 0))],
        out_specs=[pl.BlockSpec((2, None, tm, w), lambda j, i: (0, j, i, 0)), pl.BlockSpec((tm, w), lambda j, i: (i, j))],
        out_shape=[SDS((2, 4, t, w), BF16), SDS((t, 4 * w), BF16)],
        compiler_params=_cp("parallel", "parallel"),
    )(h1n, wgu)


def _ffn_down_loss(act, wdown, h1, target):
    t, f = act.shape
    d = wdown.shape[1]
    tm, tn = _tile(t, 1024), _tile(d, 512)

    def body(a, b, h, tg, dy, dyb, ls):
        e = _dot(a[...], b[...]) + h[...] - tg[...]
        g = e * (1.0 / d)
        dy[...] = g
        dyb[...] = g.astype(BF16)
        ls[...] = jnp.broadcast_to(jnp.sum(e * e), (8, HD))

    return pl.pallas_call(
        body, grid=(t // tm, d // tn), name="ffn_down_loss",
        in_specs=[pl.BlockSpec((tm, f), lambda i, j: (i, 0)), pl.BlockSpec((f, tn), lambda i, j: (0, j)),
                  pl.BlockSpec((tm, tn), lambda i, j: (i, j)), pl.BlockSpec((tm, tn), lambda i, j: (i, j))],
        out_specs=[pl.BlockSpec((tm, tn), lambda i, j: (i, j)), pl.BlockSpec((tm, tn), lambda i, j: (i, j)),
                   pl.BlockSpec((8, HD), lambda i, j: (i, j))],
        out_shape=[SDS((t, d), F32), SDS((t, d), BF16), SDS((8 * (t // tm), HD * (d // tn)), F32)],
        compiler_params=_cp("parallel", "parallel"),
    )(act, wdown, h1, target)


def _ffn_down_bwd(dyb, wdown4, gu):
    t, d = dyb.shape
    w = wdown4.shape[1]
    tm = _tile(t, 512)

    def body(a, b, gu_ref, out):
        da = _dot(a[...], b[...], NT)
        g = gu_ref[0].astype(F32)
        u = gu_ref[1].astype(F32)
        s = _sigmoid(g)
        out[0] = (da * u * (s * (1.0 + g * (1.0 - s)))).astype(BF16)
        out[1] = (da * g * s).astype(BF16)

    return pl.pallas_call(
        body, grid=(4, t // tm), name="ffn_down_bwd",
        in_specs=[pl.BlockSpec((tm, d), lambda j, i: (i, 0)), pl.BlockSpec((None, w, d), lambda j, i: (j, 0, 0)),
                  pl.BlockSpec((2, None, tm, w), lambda j, i: (0, j, i, 0))],
        out_specs=pl.BlockSpec((2, None, tm, w), lambda j, i: (0, j, i, 0)),
        out_shape=SDS((2, 4, t, w), BF16),
        compiler_params=_cp("parallel", "parallel"),
    )(dyb, wdown4, gu)


def _ffn_up_bwd_x(dgu, wgu):
    _, t, w = dgu.shape
    d = wgu.shape[1]
    tm = _tile(t, 512)

    def body(a, b, out):
        @pl.when(pl.program_id(1) == 0)
        def _():
            out[...] = jnp.zeros_like(out)
        out[...] += _dot(a[...], b[...], NT)

    return pl.pallas_call(
        body, grid=(t // tm, 8), name="ffn_up_bwd_x",
        in_specs=[pl.BlockSpec((None, tm, w), lambda i, j: (j, i, 0)), pl.BlockSpec((None, d, w), lambda i, j: (j, 0, 0))],
        out_specs=pl.BlockSpec((tm, d), lambda i, j: (i, 0)), out_shape=SDS((t, d), F32),
        compiler_params=_cp("parallel", "arbitrary"),
    )(dgu, wgu)


def _ffn_up_bwd_w(h1n, dgu):
    _, t, w = dgu.shape
    d = h1n.shape[1]
    tm = _tile(d, 512)

    def body(a, b, out):
        out[...] = _dot(a[...], b[...], TN).astype(BF16)

    return pl.pallas_call(
        body, grid=(8, d // tm), name="ffn_up_bwd_w",
        in_specs=[pl.BlockSpec((t, tm), lambda j, i: (0, i)), pl.BlockSpec((None, t, w), lambda j, i: (j, 0, 0))],
        out_specs=pl.BlockSpec((None, tm, w), lambda j, i: (j, i, 0)), out_shape=SDS((8, d, w), BF16),
        compiler_params=_cp("parallel", "parallel"),
    )(h1n, dgu)


def _fox_prep(fq, fk, sm, fb, qg, kg, h):
    qn = _rms(fq, qg)
    kn = _rms(fk, kg)
    c = _cumsum_rows(-_softplus(-(sm + fb)))
    ccol = _lane_pick(c, L_FF + h)
    crow = jnp.sum(c.T * (_iota((HD, 1), 0) == L_FF + h).astype(F32), axis=0, keepdims=True)
    return qn, kn, ccol, crow


def _fox_block(q, k, v, cc, cr, off):
    s = _dot(q.astype(BF16), k.astype(BF16), NT) * (HD ** -0.5) + cc - cr
    s = jnp.where(_iota(s.shape, 1) <= _iota(s.shape, 0) + off, s, -1e30)
    e = jnp.exp(s - lax.stop_gradient(jnp.max(s, axis=1, keepdims=True)))
    p = e / jnp.sum(e, axis=1, keepdims=True)
    return _dot(p.astype(BF16), v.astype(BF16))


ONE_BUFFER = pl.Buffered(1)


def _pcol(t, cb):
    return pl.BlockSpec((t, HD), lambda h, cb=cb: (0, cb + h), pipeline_mode=ONE_BUFFER)


def _smcol(t):
    return pl.BlockSpec((t, HD), lambda h: (0, SM), pipeline_mode=ONE_BUFFER)


def _head(t):
    return pl.BlockSpec((t, HD), lambda h: (0, h), pipeline_mode=ONE_BUFFER)


def _small(n):
    return pl.BlockSpec((n, HD), lambda h: (0, 0), pipeline_mode=ONE_BUFFER)


def _fox_fwd(p, fb, qg, kg, bq):
    t = p.shape[0]

    def body(fq, fk, fv, sm, fb_r, qg_r, kg_r, o, qn_s, cc_s):
        h = pl.program_id(0)
        qn, kn, ccol, crow = _fox_prep(fq[...], fk[...], sm[...], fb_r[...], qg_r[...], kg_r[...], h)
        qn_s[...] = qn
        cc_s[...] = ccol
        knb = kn.astype(BF16)
        vb = fv[...].astype(BF16)
        for i in range(t // bq):
            rows, ext = pl.ds(i * bq, bq), (i + 1) * bq
            o[rows, :] = _fox_block(qn_s[rows, :], knb[:ext], vb[:ext], cc_s[rows, :], crow[:, :ext], i * bq).astype(o.dtype)

    return pl.pallas_call(
        body, grid=(NF,), name="fox_fwd",
        in_specs=[_pcol(t, FQ), _pcol(t, FK), _pcol(t, FV), _smcol(t), _small(1), _small(1), _small(1)],
        out_specs=_head(t), out_shape=SDS((t, NF * HD), BF16),
        scratch_shapes=[pltpu.VMEM((t, HD), F32), pltpu.VMEM((t, 1), F32)],
        compiler_params=_cp("parallel"),
    )(p, p, p, p, fb, qg, kg)


def _fox_bwd(p, fb, qg, kg, dmix, bq, deps=()):
    t = p.shape[0]

    def body(*refs):
        fq, fk, fv, sm, fb_r, qg_r, kg_r, do = refs[:8]
        dfq, dfk, dfv, dsm, dfb, dqg, dkg, qn_s, cc_s, dqn_s, dcc_s, dkn_s, dv_s, dcr_s = refs[8 + len(deps):]
        h = pl.program_id(0)
        qn, kn, ccol, crow = _fox_prep(fq[...], fk[...], sm[...], fb_r[...], qg_r[...], kg_r[...], h)
        qn_s[...] = qn
        cc_s[...] = ccol
        v = fv[...]
        dkn_s[...] = jnp.zeros_like(dkn_s)
        dv_s[...] = jnp.zeros_like(dv_s)
        dcr_s[...] = jnp.zeros_like(dcr_s)

        for i in range(t // bq):
            rows, ext = pl.ds(i * bq, bq), (i + 1) * bq
            _, vjp = jax.vjp(lambda a, b, c, d, e, off=i * bq: _fox_block(a, b, c, d, e, off),
                             qn_s[rows, :], kn[:ext], v[:ext], cc_s[rows, :], crow[:, :ext])
            dq, dk, dv, dcc, dcr = vjp(do[rows, :])
            dqn_s[rows, :] = dq
            dcc_s[rows, :] = dcc
            dkn_s[:ext, :] += dk
            dv_s[:ext, :] += dv
            dcr_s[:, :ext] += dcr
        _, prep_vjp = jax.vjp(lambda a, b, c, d, e, f: _fox_prep(a, b, c, d, e, f, h),
                              fq[...], fk[...], sm[...], fb_r[...], qg_r[...], kg_r[...])
        g_fq, g_fk, g_sm, g_fb, g_qg, g_kg = prep_vjp((dqn_s[...], dkn_s[...], dcc_s[...], dcr_s[...]))
        dfq[...] = g_fq.astype(dfq.dtype)
        dfk[...] = g_fk.astype(dfk.dtype)
        dfv[...] = dv_s[...].astype(dfv.dtype)

        @pl.when(h == 0)
        def _():
            for r in (dsm, dfb, dqg, dkg):
                r[...] = jnp.zeros_like(r)

        dsm[...] += g_sm
        dfb[...] += g_fb
        dqg[...] += g_qg
        dkg[...] += g_kg

    head = _head(t)
    return pl.pallas_call(
        body, grid=(NF,), name="fox_bwd",
        in_specs=[_pcol(t, FQ), _pcol(t, FK), _pcol(t, FV), _smcol(t), _small(1), _small(1), _small(1), head]
        + [ANY_SPEC] * len(deps),
        out_specs=[head, head, head, _small(t), _small(1), _small(1), _small(1)],
        out_shape=[SDS((t, NF * HD), BF16)] * 3 + [SDS((t, HD), F32)] + [SDS((1, HD), F32)] * 3,
        scratch_shapes=[pltpu.VMEM((t, HD), F32), pltpu.VMEM((t, 1), F32), pltpu.VMEM((t, HD), F32),
                        pltpu.VMEM((t, 1), F32), pltpu.VMEM((t, HD), F32), pltpu.VMEM((t, HD), F32),
                        pltpu.VMEM((1, t), F32)],
        compiler_params=_cp("arbitrary"),
    )(p, p, p, p, fb, qg, kg, dmix, *deps)


def _mem_attn(mq, mk, mv, qg, kg):
    s = _dot(_rms(mq, qg).astype(BF16), _rms(mk, kg).astype(BF16), NT) * (HD ** -0.5)
    e = jnp.exp(s - lax.stop_gradient(jnp.max(s, axis=1, keepdims=True)))
    p = e / jnp.sum(e, axis=1, keepdims=True)
    return _dot(p.astype(BF16), mv.astype(BF16))


def _mem_fwd(p, mkv, qg, kg):
    t, ml = p.shape[0], mkv.shape[0]

    def body(mq, mk, mv, qg_r, kg_r, o):
        o[...] = _mem_attn(mq[...], mk[...], mv[...], qg_r[...], kg_r[...]).astype(o.dtype)

    return pl.pallas_call(
        body, grid=(NM,), name="mem_fwd",
        in_specs=[_pcol(t, MQ), pl.BlockSpec((ml, HD), lambda h: (0, h)), pl.BlockSpec((ml, HD), lambda h: (0, NM + h)),
                  _small(1), _small(1)],
        out_specs=pl.BlockSpec((t, HD), lambda h: (0, h)), out_shape=SDS((t, NM * HD), BF16),
        compiler_params=_cp("parallel"),
    )(p, mkv, mkv, qg, kg)


def _mem_bwd(p, mkv, qg, kg, dmix, deps=()):
    t, ml = p.shape[0], mkv.shape[0]

    def body(*refs):
        mq, mk, mv, qg_r, kg_r, do = refs[:6]
        dmq, dmk, dmv, dqg, dkg = refs[6 + len(deps):]
        _, vjp = jax.vjp(_mem_attn, mq[...], mk[...], mv[...], qg_r[...], kg_r[...])
        g_q, g_k, g_v, g_qg, g_kg = vjp(do[...])
        dmq[...] = g_q.astype(dmq.dtype)
        dmk[...] = g_k
        dmv[...] = g_v

        @pl.when(pl.program_id(0) == 0)
        def _():
            dqg[...] = jnp.zeros_like(dqg)
            dkg[...] = jnp.zeros_like(dkg)

        dqg[...] += g_qg
        dkg[...] += g_kg

    return pl.pallas_call(
        body, grid=(NM,), name="mem_bwd",
        in_specs=[_pcol(t, MQ), pl.BlockSpec((ml, HD), lambda h: (0, h)), pl.BlockSpec((ml, HD), lambda h: (0, NM + h)),
                  _small(1), _small(1), pl.BlockSpec((t, HD), lambda h: (0, NF + NG + h))] + [ANY_SPEC] * len(deps),
        out_specs=[pl.BlockSpec((t, HD), lambda h: (0, h)), pl.BlockSpec((ml, HD), lambda h: (0, h)),
                   pl.BlockSpec((ml, HD), lambda h: (0, h)), _small(1), _small(1)],
        out_shape=[SDS((t, NM * HD), BF16), SDS((ml, NM * HD), F32), SDS((ml, NM * HD), F32),
                   SDS((1, HD), F32), SDS((1, HD), F32)],
        compiler_params=_cp("arbitrary"),
    )(p, mkv, mkv, qg, kg, dmix, *deps)


def _shift_down(x, s):
    if s == 0:
        return x
    return jnp.where(_iota(x.shape, 0) >= s, pltpu.roll(x, s, 0), 0.0)


def _shift_up(x, s):
    if s == 0:
        return x
    n = x.shape[0]
    return jnp.where(_iota(x.shape, 0) < n - s, pltpu.roll(x, n - s, 0), 0.0)


@jax.custom_vjp
def _conv4(x, w0, w1, w2, w3):
    return w0 * _shift_down(x, 3) + w1 * _shift_down(x, 2) + w2 * _shift_down(x, 1) + w3 * x


def _conv4_fwd(x, w0, w1, w2, w3):
    return _conv4(x, w0, w1, w2, w3), (x, w0, w1, w2, w3)


def _conv4_bwd(res, dy):
    x, w0, w1, w2, w3 = res
    dx = w0 * _shift_up(dy, 3) + w1 * _shift_up(dy, 2) + w2 * _shift_up(dy, 1) + w3 * dy
    dws = tuple(jnp.sum(dy * _shift_down(x, 3 - k), axis=0, keepdims=True) for k in range(4))
    return (dx,) + dws


_conv4.defvjp(_conv4_fwd, _conv4_bwd)


HALO = 8


def _gdn_prep(gq, gk, gv, sm, taps, alog, dtb, h):
    q, k, v = [_silu(_conv4(x, *taps[4 * j:4 * j + 4]))[HALO:] for j, x in enumerate((gq, gk, gv))]
    q = q * lax.rsqrt(jnp.sum(q * q, axis=-1, keepdims=True) + NORM_EPS) * (HD ** -0.5)
    k = k * lax.rsqrt(jnp.sum(k * k, axis=-1, keepdims=True) + NORM_EPS)
    g = _lane_pick(-jnp.exp(alog) * _softplus(sm + dtb), L_GA + h)
    beta = _lane_pick(_sigmoid(sm), L_GB + h)
    return q, k, v, g, beta


def _split(x, n):
    parts, rest = [], x
    for i in range(n):
        parts.append(rest.astype(BF16))
        if i + 1 < n:
            rest = rest - parts[-1].astype(F32)
    return parts


def _raw_dot(a, b, form):
    lead = a.ndim - 2
    ca, cb = {"nn": (1, 0), "nt": (1, 1), "tn": (0, 0)}[form]
    batch = ((0,), (0,)) if lead else ((), ())
    return lax.dot_general(a, b, (((ca + lead,), (cb + lead,)), batch), preferred_element_type=F32)


def _pdot_impl(a, b, form, mode):
    if mode == "1":
        return _raw_dot(a.astype(BF16), b.astype(BF16), form)
    if mode == "3":
        (ah, al), (bh, bl) = _split(a, 2), _split(b, 2)
        return _raw_dot(ah, bh, form) + (_raw_dot(al, bh, form) + _raw_dot(ah, bl, form))
    if mode == "xa":
        return sum(_raw_dot(a.astype(BF16), t, form) for t in reversed(_split(b, 3)))
    return sum(_raw_dot(t, b.astype(BF16), form) for t in reversed(_split(a, 3)))


@functools.partial(jax.custom_vjp, nondiff_argnums=(2, 3))
def _pdot(a, b, form, mode):
    return _pdot_impl(a, b, form, mode)


def _pdot_fwd(a, b, form, mode):
    return _pdot_impl(a, b, form, mode), (a, b)


def _pdot_bwd(form, mode, res, ct):
    a, b = res
    da_args, db_args = {"nn": ((ct, b, "nt"), (a, ct, "tn")), "nt": ((ct, b, "nn"), (ct, a, "tn")),
                        "tn": ((b, ct, "nt"), (a, ct, "nn"))}[form]

    def side(args, exact):
        if mode in ("1", "3"):
            return mode
        return "xa" if args[0] is exact else "xb"

    if mode == "xa":
        return jnp.zeros_like(a), _pdot_impl(*db_args, side(db_args, a))
    if mode == "xb":
        return _pdot_impl(*da_args, side(da_args, b)), jnp.zeros_like(b)
    return _pdot_impl(*da_args, mode), _pdot_impl(*db_args, mode)


_pdot.defvjp(_pdot_fwd, _pdot_bwd)

GDN_QK, GDN_INV, GDN_SCAN = "1", "1", "1"


@jax.custom_vjp
def _tri_inv(low):
    eye = (_iota((CHUNK, CHUNK), 0) == _iota((CHUNK, CHUNK), 1)).astype(F32)
    inv = eye - low
    pw = low
    for _ in range(5):
        pw = _pdot_impl(pw, pw, "nn", GDN_INV)
        inv = inv + _pdot_impl(inv, pw, "nn", GDN_INV)
    return inv


def _tri_inv_fwd(low):
    inv = _tri_inv(low)
    return inv, inv


def _tri_inv_bwd(inv, ct):
    return (-_pdot_impl(_pdot_impl(inv, ct, "tn", GDN_INV), inv, "nt", GDN_INV),)


_tri_inv.defvjp(_tri_inv_fwd, _tri_inv_bwd)


def _gdn_intra(q, k, v, g, beta):
    n = q.shape[0]
    r, c = _iota((CHUNK, CHUNK), 0), _iota((CHUNK, CHUNK), 1)
    tril, strict = r >= c, r > c
    trilf = jnp.broadcast_to(tril.astype(F32), (n, CHUNK, CHUNK))
    gcm = _pdot(trilf, jnp.broadcast_to(g, (n, CHUNK, CHUNK)), "nn", "xa")
    gcf = _pdot(trilf, jnp.broadcast_to(g, (n, CHUNK, HD)), "nn", "xa")
    lane0 = (_iota((1, 1, CHUNK), 2) == 0).astype(F32)
    gcr = _pdot(jnp.ones((n, CHUNK, CHUNK), F32), gcm * lane0, "nt", "xa")
    decay = jnp.where(tril, jnp.exp(jnp.where(tril, gcm - gcr, 0.0)), 0.0)
    egc = jnp.exp(gcf)
    kb = k * beta
    low = jnp.where(strict, _pdot(kb, k, "nt", GDN_QK) * decay, 0.0)
    inv = _tri_inv(low)
    u = _pdot(inv, v * beta, "nn", GDN_INV)
    w = _pdot(inv, kb * egc, "nn", GDN_INV)
    at = jnp.where(tril, _pdot(q, k, "nt", GDN_QK) * decay, 0.0)
    gl = jnp.sum(jnp.broadcast_to(g, (n, CHUNK, HD)), axis=1, keepdims=True)
    return u, w, q * egc, at, k * jnp.exp(gl - gcf), gl


def _gdn_step(s, u, w, qg, at, kd, gl):
    vn = u - _pdot(w, s, "nn", GDN_SCAN)
    o = _pdot(qg, s, "nn", GDN_SCAN) + _pdot(at, vn, "nn", GDN_SCAN)
    s2 = s * jnp.exp(gl) + _pdot(kd, vn, "tn", GDN_SCAN)
    return o, s2


SCAN_HEADS = 3


def _gdn_chunked_scratch(nc):
    big = pltpu.VMEM((nc, CHUNK, HD), F32)
    return [big, big, big, pltpu.VMEM((nc, CHUNK, 1), F32), pltpu.VMEM((nc, CHUNK, 1), F32)]


def _gdn_term_shapes(nc):
    return [(nc, CHUNK, HD), (nc, CHUNK, HD), (nc, CHUNK, HD), (nc, CHUNK, CHUNK), (nc, CHUNK, HD), (nc, 1, HD)]


def _per_head(shape, heads=None, one_buffer=True):
    lead = (None,) if heads is None else (heads,)
    return pl.BlockSpec(lead + tuple(shape), lambda h: (h,) + (0,) * len(shape),
                        pipeline_mode=ONE_BUFFER if one_buffer else None)


def _gdn_in_specs(t):
    cw = lambda cb: pl.BlockSpec((4, HD), lambda h, cb=cb: (0, cb + h))
    return [_pcol(t, GQ), _pcol(t, GK), _pcol(t, GV), _smcol(t), cw(0), cw(NG), cw(2 * NG), _small(1), _small(1)]


def _taps(wq, wk, wv):
    return tuple(w[k:k + 1, :] for w in (wq, wk, wv) for k in range(4))


def _prep_rows(t):
    return min(t, 256)


def _gdn_pad(srcs, pads):
    for src, pad in zip(srcs, pads):
        pad[0:HALO, :] = jnp.zeros((HALO, HD), F32)
        pad[HALO:, :] = src[...]


def _gdn_stage(pads, sm, taps, al, db, h, chunked):
    t = sm.shape[0]
    rows = _prep_rows(t)
    per = rows // CHUNK

    def tile(i, carry):
        r0 = pl.multiple_of(i * rows, rows)
        vals = _gdn_prep(*[p[pl.ds(r0, rows + HALO), :] for p in pads], sm[pl.ds(r0, rows), :], taps, al, db, h)
        for v, r in zip(vals, chunked):
            r[pl.ds(i * per, per)] = v.reshape(per, CHUNK, v.shape[-1])
        return carry

    lax.fori_loop(0, t // rows, tile, 0)


def _gdn_intra_all(chunked, intra):
    nc = chunked[0].shape[0]
    grp_n = math.gcd(nc, GROUP)

    def grp(i, carry):
        sl = pl.ds(pl.multiple_of(i * grp_n, grp_n), grp_n)
        for r, val in zip(intra, _gdn_intra(*[c[sl] for c in chunked])):
            r[sl] = val
        return carry

    lax.fori_loop(0, nc // grp_n, grp, 0)


def _gdn_fwd(pa, pb, conv, alog, dtb):
    t = pa.shape[0]
    nc = t // CHUNK
    terms = _gdn_term_shapes(nc)

    def body(gq, gk, gv, sm, wq, wk, wv, al, db, *rest):
        h = pl.program_id(0)
        intra, chunked, pads = rest[:6], rest[6:11], rest[11:]
        _gdn_pad((gq, gk, gv), pads)
        _gdn_stage(pads, sm, _taps(wq, wk, wv), al[...], db[...], h, chunked)
        _gdn_intra_all(chunked, intra)

    return pl.pallas_call(
        body, grid=(NG,), name="gdn_fwd", in_specs=_gdn_in_specs(t),
        out_specs=[_per_head(sh, one_buffer=False) for sh in terms], out_shape=[SDS((NG,) + sh, F32) for sh in terms],
        scratch_shapes=_gdn_chunked_scratch(nc) + [pltpu.VMEM((t + HALO, HD), F32)] * 3, compiler_params=_cp("parallel"),
    )(pa, pa, pa, pb, conv, conv, conv, alog, dtb)


def _gdn_scan(terms_in):
    nc = terms_in[0].shape[1]
    terms = _gdn_term_shapes(nc)

    def body(*refs):
        intra, o, states = refs[:6], refs[6], refs[7]

        def step(c, ss):
            rows = pl.ds(pl.multiple_of(c * CHUNK, CHUNK), CHUNK)
            new = []
            for hh in range(SCAN_HEADS):
                states[hh, c] = ss[hh]
                oc, s2 = _gdn_step(ss[hh], *[r[hh, c] for r in intra])
                o[rows, hh * HD:(hh + 1) * HD] = oc
                new.append(s2)
            return tuple(new)

        lax.fori_loop(0, nc, step, tuple(jnp.zeros((HD, HD), F32) for _ in range(SCAN_HEADS)))

    return pl.pallas_call(
        body, grid=(NG // SCAN_HEADS,), name="gdn_scan", in_specs=[_per_head(sh, SCAN_HEADS) for sh in terms],
        out_specs=[pl.BlockSpec((nc * CHUNK, SCAN_HEADS * HD), lambda h: (0, h), pipeline_mode=ONE_BUFFER),
                   _per_head((nc, HD, HD), SCAN_HEADS)],
        out_shape=[SDS((nc * CHUNK, NG * HD), F32), SDS((NG, nc, HD, HD), F32)], compiler_params=_cp("parallel"),
    )(*terms_in)


def _gdn_bwd_scan(saved, do_raw):
    nc = saved[0].shape[1]
    terms = _gdn_term_shapes(nc)

    def body(*refs):
        intra, states, do, outs = refs[:6], refs[6], refs[7], refs[8:]

        def bwd(i, dss):
            c = nc - 1 - i
            rows = pl.ds(pl.multiple_of(c * CHUNK, CHUNK), CHUNK)
            new = []
            for hh in range(SCAN_HEADS):
                _, vjp = jax.vjp(_gdn_step, states[hh, c], *[r[hh, c] for r in intra])
                grads = vjp((do[rows, hh * HD:(hh + 1) * HD], dss[hh]))
                for r, gval in zip(outs, grads[1:]):
                    r[hh, c] = gval
                new.append(grads[0])
            return tuple(new)

        lax.fori_loop(0, nc, bwd, tuple(jnp.zeros((HD, HD), F32) for _ in range(SCAN_HEADS)))

    return pl.pallas_call(
        body, grid=(NG // SCAN_HEADS,), name="gdn_bwd_scan",
        in_specs=[_per_head(sh, SCAN_HEADS) for sh in terms] + [_per_head((nc, HD, HD), SCAN_HEADS)]
        + [pl.BlockSpec((nc * CHUNK, SCAN_HEADS * HD), lambda h: (0, h), pipeline_mode=ONE_BUFFER)],
        out_specs=[_per_head(sh, SCAN_HEADS) for sh in terms],
        out_shape=[SDS((NG,) + sh, F32) for sh in terms], compiler_params=_cp("parallel"),
    )(*saved, do_raw)


def _gdn_bwd(pa, pb, conv, alog, dtb, dterms):
    t = pa.shape[0]
    nc = t // CHUNK
    terms = _gdn_term_shapes(nc)

    def body(*refs):
        gq, gk, gv, sm, wq, wk, wv, al, db = refs[:9]
        dintra = refs[9:15]
        dgq, dgk, dgv, dsm, dwq, dwk, dwv, dal, ddb = refs[15:24]
        chunked, pads, dpads, dsm_s = refs[24:29], refs[29:32], refs[32:35], refs[35]
        h = pl.program_id(0)
        taps = _taps(wq, wk, wv)
        _gdn_pad((gq, gk, gv), pads)
        _gdn_stage(pads, sm, taps, al[...], db[...], h, chunked)
        grp_n = math.gcd(nc, GROUP)

        def grp(i, carry):
            sl = pl.ds(pl.multiple_of(i * grp_n, grp_n), grp_n)
            _, vjp = jax.vjp(_gdn_intra, *[r[sl] for r in chunked])
            for r, gval in zip(chunked, vjp(tuple(r[sl] for r in dintra))):
                r[sl] = gval
            return carry

        lax.fori_loop(0, nc // grp_n, grp, 0)

        rows = _prep_rows(t)
        per = rows // CHUNK
        for r in dpads:
            r[...] = jnp.zeros_like(r)

        def tile(i, small):
            r0 = pl.multiple_of(i * rows, rows)
            win = pl.ds(r0, rows + HALO)
            _, vjp = jax.vjp(lambda *a: _gdn_prep(*a, h), *[p[win, :] for p in pads], sm[pl.ds(r0, rows), :],
                             taps, al[...], db[...])
            grads = vjp(tuple(r[pl.ds(i * per, per)].reshape(rows, r.shape[-1]) for r in chunked))
            for r, gval in zip(dpads, grads[:3]):
                r[win, :] += gval
            dsm_s[pl.ds(r0, rows), :] = grads[3]
            return jax.tree.map(jnp.add, small, (grads[4], grads[5], grads[6]))

        zero = jnp.zeros((1, HD), F32)
        dtaps, g_al, g_db = lax.fori_loop(0, t // rows, tile, ((zero,) * 12, zero, zero))
        for r, dpad in zip((dgq, dgk, dgv), dpads):
            r[...] = dpad[HALO:, :].astype(r.dtype)
        for j, r in enumerate((dwq, dwk, dwv)):
            for k in range(4):
                r[k:k + 1, :] = dtaps[4 * j + k]

        @pl.when(h == 0)
        def _():
            for r in (dsm, dal, ddb):
                r[...] = jnp.zeros_like(r)

        dsm[...] += dsm_s[...]
        dal[...] += g_al
        ddb[...] += g_db

    head = _head(t)
    taps = pl.BlockSpec((4, HD), lambda h: (0, h))
    return pl.pallas_call(
        body, grid=(NG,), name="gdn_bwd", in_specs=_gdn_in_specs(t) + [_per_head(sh) for sh in terms],
        out_specs=[head, head, head, _small(t), taps, taps, taps, _small(1), _small(1)],
        out_shape=[SDS((t, NG * HD), BF16)] * 3 + [SDS((t, HD), F32)] + [SDS((4, NG * HD), F32)] * 3 + [SDS((1, HD), F32)] * 2,
        scratch_shapes=_gdn_chunked_scratch(nc) + [pltpu.VMEM((t + HALO, HD), F32)] * 6 + [pltpu.VMEM((t, HD), F32)],
        compiler_params=_cp("arbitrary"),
    )(pa, pa, pa, pb, conv, conv, conv, alog, dtb, *dterms)


def _gdn_post(o, z, gain):
    return (jnp.concatenate(
        [_rms(o[:, h * HD:(h + 1) * HD], gain) * _silu(z[:, h * HD:(h + 1) * HD]) for h in range(NG)], axis=1),)


def _place():
    return lax.axis_index("x"), lax.axis_index("y"), lax.axis_index("c")


def _all_gather(name, shard):
    def body(x_ref, out_ref, send_sems, recv_sems, local_sem):
        x, y, c = _place()
        me, sibling = (x, y, c), (x, y, 1 - c)
        chips = [(1 - x, y), (x, 1 - y), (1 - x, 1 - y)]

        def blk(px, py, pc):
            return out_ref.at[4 * px + 2 * py + pc]

        def copy(k, block, to, src=None):
            return pltpu.make_async_remote_copy(
                src_ref=blk(*block) if src is None else src, dst_ref=blk(*block),
                send_sem=send_sems.at[k], recv_sem=recv_sems.at[k], device_id=to, device_id_type=MESH)

        mine = pltpu.make_async_copy(x_ref, blk(*me), local_sem)
        mine.start()
        first = [copy(0, me, sibling, src=x_ref)]
        first += [copy(1 + j, me, (*chip, c), src=x_ref) for j, chip in enumerate(chips)]
        for cp in first:
            cp.start()
        passed = [copy(4 + j, (*chip, c), sibling) for j, chip in enumerate(chips)]
        for j, chip in enumerate(chips):
            copy(1 + j, (*chip, c), me).wait_recv()
            passed[j].start()
        copy(0, sibling, me).wait_recv()
        for j, chip in enumerate(chips):
            copy(4 + j, (*chip, 1 - c), me).wait_recv()
        for cp in first + passed:
            cp.wait_send()
        mine.wait()

    return pl.pallas_call(
        body, name=name, out_shape=SDS((N_DEV,) + shard.shape, shard.dtype),
        in_specs=[pl.BlockSpec(memory_space=pltpu.HBM)], out_specs=pl.BlockSpec(memory_space=pltpu.HBM),
        scratch_shapes=[pltpu.SemaphoreType.DMA((7,)), pltpu.SemaphoreType.DMA((7,)), pltpu.SemaphoreType.DMA],
    )(shard)


def _scatter_exchange(name, full):
    def body(g_ref, out_ref, send_sems, recv_sems, local_sem):
        x, y, c = _place()
        me = 4 * x + 2 * y + c
        mine = pltpu.make_async_copy(g_ref.at[me], out_ref.at[me], local_sem)
        mine.start()
        sends, recvs = [], []
        for k in range(1, N_DEV):
            px = 1 - x if k & 4 else x
            py = 1 - y if k & 2 else y
            pc = 1 - c if k & 1 else c
            peer = 4 * px + 2 * py + pc
            sends.append(pltpu.make_async_remote_copy(
                src_ref=g_ref.at[peer], dst_ref=out_ref.at[me], send_sem=send_sems.at[k - 1],
                recv_sem=recv_sems.at[k - 1], device_id=(px, py, pc), device_id_type=MESH))
            recvs.append(pltpu.make_async_remote_copy(
                src_ref=g_ref.at[me], dst_ref=out_ref.at[peer], send_sem=send_sems.at[k - 1],
                recv_sem=recv_sems.at[k - 1], device_id=(px, py, pc), device_id_type=MESH))
        for cp in sends:
            cp.start()
        for cp in recvs:
            cp.wait_recv()
        for cp in sends:
            cp.wait_send()
        mine.wait()

    return pl.pallas_call(
        body, name=name, out_shape=SDS(full.shape, full.dtype),
        in_specs=[pl.BlockSpec(memory_space=pltpu.HBM)], out_specs=pl.BlockSpec(memory_space=pltpu.HBM),
        scratch_shapes=[pltpu.SemaphoreType.DMA((7,)), pltpu.SemaphoreType.DMA((7,)), pltpu.SemaphoreType.DMA],
    )(full)


def _sum_blocks(name, parts):
    _, r, c = parts.shape
    tr = 64 if r % 64 == 0 else r

    def body(x, o):
        acc = x[0].astype(F32)
        for d in range(1, N_DEV):
            acc = acc + x[d].astype(F32)
        o[...] = acc

    return pl.pallas_call(
        body, grid=(r // tr,), name=name, in_specs=[pl.BlockSpec((N_DEV, tr, c), lambda i: (0, i, 0))],
        out_specs=pl.BlockSpec((tr, c), lambda i: (i, 0)), out_shape=SDS((r, c), F32), compiler_params=_cp("parallel"),
    )(parts)


def _reduce_scatter(name, full):
    return _sum_blocks(name + "_sum", _scatter_exchange(name, full))


def _all_reduce_small(name, x, reduce):
    m_per, n = x.shape

    def body(x_ref, out_ref, send_sems, recv_sems, local_sem):
        px, py, pc = _place()
        me, sibling = (px, py, pc), (px, py, 1 - pc)
        chips = [(1 - px, py), (px, 1 - py), (1 - px, 1 - py)]
        buf = out_ref

        def rows(qx, qy, qc):
            return buf.at[pl.ds((4 * qx + 2 * qy + qc) * m_per, m_per), :]

        def copy(k, block, to, src=None):
            return pltpu.make_async_remote_copy(
                src_ref=rows(*block) if src is None else src, dst_ref=rows(*block),
                send_sem=send_sems.at[k], recv_sem=recv_sems.at[k], device_id=to, device_id_type=MESH)

        mine = pltpu.make_async_copy(x_ref, rows(*me), local_sem)
        mine.start()
        first = [copy(0, me, sibling, src=x_ref)]
        first += [copy(1 + j, me, (*chip, pc), src=x_ref) for j, chip in enumerate(chips)]
        for cp in first:
            cp.start()
        passed = [copy(4 + j, (*chip, pc), sibling) for j, chip in enumerate(chips)]
        for j, chip in enumerate(chips):
            copy(1 + j, (*chip, pc), me).wait_recv()
            passed[j].start()
        copy(0, sibling, me).wait_recv()
        for j, chip in enumerate(chips):
            copy(4 + j, (*chip, 1 - pc), me).wait_recv()
        for cp in first + passed:
            cp.wait_send()
        mine.wait()

    gathered = pl.pallas_call(
        body, name=name, out_shape=SDS((N_DEV * m_per, n), x.dtype),
        in_specs=[pl.BlockSpec(memory_space=pltpu.VMEM)], out_specs=pl.BlockSpec(memory_space=pltpu.VMEM),
        scratch_shapes=[pltpu.SemaphoreType.DMA((7,)), pltpu.SemaphoreType.DMA((7,)), pltpu.SemaphoreType.DMA],
    )(x)
    if not reduce:
        return gathered
    return _sum_blocks(name + "_sum", gathered.reshape(N_DEV, m_per, n))


HBM_SPEC = pl.BlockSpec(memory_space=pltpu.HBM)
SEM_SPEC = pl.BlockSpec(memory_space=pltpu.SEMAPHORE)
EFFECT = pltpu.SideEffectType.DATAFLOW_SIDE_EFFECTING


def _copies_start(name, bufs, n_remote, n_local, build, deps):
    nb, nd = len(bufs), len(deps)
    sem_shapes = [pltpu.SemaphoreType.DMA((n_remote,)), pltpu.SemaphoreType.DMA((n_remote,))]
    if n_local:
        sem_shapes.append(pltpu.SemaphoreType.DMA((n_local,)))
    ns = len(sem_shapes)

    def body(*refs):
        sems = refs[nb + nd:nb + nd + ns]
        remote, local = build(refs[:nb], *sems, *([None] * (3 - ns)))
        for cp in local + remote:
            cp.start()
        refs[-1][...] = jnp.zeros((8, HD), F32)

    outs = pl.pallas_call(
        body, name=name,
        out_shape=(*sem_shapes, *[pltpu.HBM(b.shape, b.dtype) for b in bufs], SDS((8, HD), F32)),
        in_specs=[HBM_SPEC] * nb + [ANY_SPEC] * nd,
        out_specs=(*[SEM_SPEC] * ns, *[HBM_SPEC] * nb, pl.BlockSpec(memory_space=pltpu.VMEM)),
        input_output_aliases={i: ns + i for i in range(nb)},
        compiler_params=pltpu.CompilerParams(has_side_effects=EFFECT),
    )(*[pltpu.with_memory_space_constraint(b, pltpu.HBM) for b in bufs], *deps)
    return list(outs[:ns]), list(outs[ns:ns + nb]), outs[-1]


def _copies_wait(name, bufs, sems, build, after):
    nb, ns = len(bufs), len(sems)

    def body(*refs):
        remote, local = build(refs[:nb], *refs[nb:nb + ns], *([None] * (3 - ns)))
        for cp in local:
            cp.wait()
        for cp in remote:
            cp.wait_send()
            cp.wait_recv()

    outs = pl.pallas_call(
        body, name=name, out_shape=tuple(pltpu.HBM(b.shape, b.dtype) for b in bufs),
        in_specs=[HBM_SPEC] * nb + [SEM_SPEC] * ns + [ANY_SPEC] * len(after), out_specs=tuple([HBM_SPEC] * nb),
        input_output_aliases={i: i for i in range(nb)},
        compiler_params=pltpu.CompilerParams(has_side_effects=EFFECT),
    )(*bufs, *sems, *after)
    return list(outs)


def _remote(src, dst, send, recv, k, to):
    return pltpu.make_async_remote_copy(src_ref=src, dst_ref=dst, send_sem=send.at[k], recv_sem=recv.at[k],
                                        device_id=to, device_id_type=MESH)


class _Gather:
    def __init__(self, name, shards, deps):
        self.name, self.n = name, len(shards)
        lands = [lax.empty((N_DEV,) + s.shape, s.dtype) for s in shards]
        self.sems1, bufs, self.token = _copies_start(
            name + "_s1", list(shards) + lands, 4 * self.n, self.n, self._stage1(range(self.n)), deps)
        self.shards, self.lands, self.sems2 = bufs[:self.n], bufs[self.n:], {}

    def _stage1(self, idxs):
        def build(refs, send, recv, loc):
            x, y, c = _place()
            me = 4 * x + 2 * y + c
            targets = [(x, y, 1 - c), (1 - x, y, c), (x, 1 - y, c), (1 - x, 1 - y, c)]
            remote, local = [], []
            for pos, i in enumerate(idxs):
                src, land = refs[pos], refs[len(idxs) + pos]
                local.append(pltpu.make_async_copy(src, land.at[me], loc.at[i]))
                remote += [_remote(src, land.at[me], send, recv, 4 * i + k, to) for k, to in enumerate(targets)]
            return remote, local
        return build

    @staticmethod
    def _stage2(refs, send, recv, loc):
        x, y, c = _place()
        remote = []
        for pos, land in enumerate(refs):
            for j, (cx, cy) in enumerate([(1 - x, y), (x, 1 - y), (1 - x, 1 - y)]):
                blk = land.at[4 * cx + 2 * cy + c]
                remote.append(_remote(blk, blk, send, recv, 3 * pos + j, (x, y, 1 - c)))
        return remote, []

    def pass_on(self, idxs, after):
        tag, m = "".join(map(str, idxs)), len(idxs)
        bufs = _copies_wait(f"{self.name}_w1_{tag}", [self.shards[i] for i in idxs] + [self.lands[i] for i in idxs],
                            self.sems1, self._stage1(idxs), after)
        self.sems2[tag], lands, token = _copies_start(f"{self.name}_s2_{tag}", bufs[m:], 3 * m, 0, self._stage2, ())
        for pos, i in enumerate(idxs):
            self.lands[i] = lands[pos]
        return [token]

    def get(self, idxs, after):
        tag = "".join(map(str, idxs))
        return _copies_wait(f"{self.name}_w2_{tag}", [self.lands[i] for i in idxs], self.sems2[tag], self._stage2, after)


def _rows_tile(r, row_bytes, target=1 << 20):
    tr = r
    while tr % 32 == 0 and tr * row_bytes > target:
        tr //= 2
    return tr


def _pair_add(name, g, got, c):
    _, r, cols = g.shape
    tr = _rows_tile(r, cols * 2)

    def body(s, a, b, o):
        o[...] = (a[...].astype(F32) + b[...].astype(F32)).astype(o.dtype)

    return pl.pallas_call(
        body, name=name, out_shape=SDS((4, r, cols), g.dtype),
        grid_spec=pltpu.PrefetchScalarGridSpec(
            num_scalar_prefetch=1, grid=(4, r // tr),
            in_specs=[pl.BlockSpec((None, tr, cols), lambda j, i, s: (2 * j + s[0], i, 0)),
                      pl.BlockSpec((None, tr, cols), lambda j, i, s: (j, i, 0))],
            out_specs=pl.BlockSpec((None, tr, cols), lambda j, i, s: (j, i, 0))),
        compiler_params=_cp("parallel", "parallel"),
    )(c.reshape(1), g, got)


def _quad_sum(name, part, got, chip):
    _, r, cols = part.shape
    tr = _rows_tile(r, cols * 4)

    def body(s, a, b1, b2, b3, o):
        o[...] = ((a[...].astype(F32) + b1[...].astype(F32)) + b2[...].astype(F32)) + b3[...].astype(F32)

    blk = lambda k: pl.BlockSpec((None, tr, cols), lambda i, s, k=k: (jnp.bitwise_xor(s[0], k), i, 0))
    return pl.pallas_call(
        body, name=name, out_shape=SDS((r, cols), F32),
        grid_spec=pltpu.PrefetchScalarGridSpec(
            num_scalar_prefetch=1, grid=(r // tr,), in_specs=[blk(0), blk(1), blk(2), blk(3)],
            out_specs=pl.BlockSpec((tr, cols), lambda i, s: (i, 0))),
        compiler_params=_cp("parallel"),
    )(chip.reshape(1), part, got, got, got)


class _Scatter:
    def __init__(self, name, grads, deps):
        self.name, self.n = name, len(grads)
        got = [lax.empty((4,) + g.shape[1:], g.dtype) for g in grads]
        self.sems, bufs, self.token = _copies_start(name + "_s1", list(grads) + got, 4 * self.n, 0, self._stage1, deps)
        self.grads, self.got = bufs[:self.n], bufs[self.n:]

    def _stage1(self, refs, send, recv, loc):
        x, y, c = _place()
        remote = []
        for i in range(self.n):
            remote += [_remote(refs[i].at[2 * j + 1 - c], refs[self.n + i].at[j], send, recv, 4 * i + j, (x, y, 1 - c))
                       for j in range(4)]
        return remote, []

    def _stage2(self, refs, send, recv, loc):
        x, y, c = _place()
        remote = []
        for i in range(self.n):
            for k in (1, 2, 3):
                tx = 1 - x if k & 2 else x
                ty = 1 - y if k & 1 else y
                remote.append(_remote(refs[i].at[2 * tx + ty], refs[self.n + i].at[2 * x + y], send, recv,
                                      3 * i + k - 1, (tx, ty, c)))
        return remote, []

    def mid(self, after):
        bufs = _copies_wait(self.name + "_w1", self.grads + self.got, self.sems, self._stage1, after)
        c = lax.axis_index("c").astype(jnp.int32)
        parts = [_pair_add(f"{self.name}_add{i}", bufs[i], bufs[self.n + i], c) for i in range(self.n)]
        got = [lax.empty(p.shape, p.dtype) for p in parts]
        self.sems, bufs, self.token = _copies_start(self.name + "_s2", parts + got, 3 * self.n, 0, self._stage2, ())
        self.parts, self.got = bufs[:self.n], bufs[self.n:]

    def end(self, after):
        bufs = _copies_wait(self.name + "_w2", self.parts + self.got, self.sems, self._stage2, after)
        chip = (2 * lax.axis_index("x") + lax.axis_index("y")).astype(jnp.int32)
        return [_quad_sum(f"{self.name}_sum{i}", bufs[i], bufs[self.n + i], chip) for i in range(self.n)]


def _adamw(w, g, m, v):
    m = ADAM_B1 * m + (1.0 - ADAM_B1) * g
    v = ADAM_B2 * v + (1.0 - ADAM_B2) * (g * g)
    m_hat = m / (1.0 - ADAM_B1 ** ADAM_STEP)
    v_hat = v / (1.0 - ADAM_B2 ** ADAM_STEP)
    return -ADAM_LR * (m_hat / (jnp.sqrt(v_hat) + ADAM_EPS) + ADAM_WD * w), m, v


def _adamw_call(name, w, g, m, v):
    r, c = w.shape
    tm = 64 if r % 64 == 0 else r
    return _rowwise(name, _adamw, [w, g, m, v], [], [(c, F32)] * 3, tm)


_IN_COLS = 5906


def _perm_in(w):
    pad = jnp.zeros((w.shape[0], 2 * HALF - _IN_COLS), w.dtype)
    return (jnp.concatenate([w[:, 2310:4614], w[:, 4614:5382]], axis=1),
            jnp.concatenate([w[:, :2304], w[:, 5394:5906], w[:, 2304:2310], w[:, 5382:5394], pad], axis=1))


def _unperm_in(ga, gb):
    return jnp.concatenate([gb[:, :2304], gb[:, 2816:2822], ga[:, :2304], ga[:, 2304:3072], gb[:, 2822:2834],
                            gb[:, 2304:2816]], axis=1)


def _lanes(v, at):
    return jnp.pad(v, ((0, 0), (at, HD - at - v.shape[1])))


_PACK = ("norm_mix", "mem_norm", "norm_ffn", "gdn_conv", "fox_q_norm", "fox_k_norm", "gdn_out_norm", "mem_q_norm",
         "mem_k_norm", "fox_f_bias", "gdn_a_log", "gdn_dt_bias", "loss")


def _pack(vals):
    parts = [vals[n].reshape(-1, HD) for n in _PACK]
    used = sum(p.shape[0] for p in parts)
    buf = jnp.concatenate(parts + [jnp.zeros((-used % 8, HD), F32)], axis=0)
    return buf, [(n, p.shape[0]) for n, p in zip(_PACK, parts)]


def _unpack(buf, layout):
    out, at = {}, 0
    for n, rows in layout:
        out[n] = buf[at:at + rows]
        at += rows
    return out


def kernel(x, mem, norm_mix, w_in, fox_f_bias, fox_q_norm, fox_k_norm, gdn_conv, gdn_a_log, gdn_dt_bias, gdn_out_norm, mem_norm, w_mem_kv, mem_q_norm, mem_k_norm, w_out, norm_ffn, w_gate_up, w_down, loss_target, m_norm_mix, m_w_in, m_fox_f_bias, m_fox_q_norm, m_fox_k_norm, m_gdn_conv, m_gdn_a_log, m_gdn_dt_bias, m_gdn_out_norm, m_mem_norm, m_w_mem_kv, m_mem_q_norm, m_mem_k_norm, m_w_out, m_norm_ffn, m_w_gate_up, m_w_down, v_norm_mix, v_w_in, v_fox_f_bias, v_fox_q_norm, v_fox_k_norm, v_gdn_conv, v_gdn_a_log, v_gdn_dt_bias, v_gdn_out_norm, v_mem_norm, v_w_mem_kv, v_mem_q_norm, v_mem_k_norm, v_w_out, v_norm_ffn, v_w_gate_up, v_w_down):
    args = dict(locals())
    d = x.shape[2]
    me = 4 * lax.axis_index("x") + 2 * lax.axis_index("y") + lax.axis_index("c")

    cshard = gdn_conv[0].shape[1]
    conv_pad = jnp.pad(gdn_conv[0], ((0, 4), (0, 3 * HD - cshard)))
    conv_all = _all_reduce_small("ag_conv", conv_pad, False).reshape(N_DEV, 8, 3 * HD)[:, :4, :cshard]
    conv_all = conv_all.transpose(1, 0, 2).reshape(4, N_DEV * cshard)
    w_in_a, w_in_b = _perm_in(w_in[0])
    comm = _StepComm({"in_b": [w_in_b], "in_a": [w_in_a], "kv_out": [w_mem_kv[0], w_out[0]], "gate_up": [w_gate_up[0]],
                      "down": [w_down[0]]}, [conv_all])

    grad_x, loss_local, small_grads = _local_step(
        x[0], mem[0], loss_target[0], norm_mix, fox_f_bias, fox_q_norm, fox_k_norm, gdn_a_log, gdn_dt_bias,
        gdn_out_norm, mem_norm, mem_q_norm, mem_k_norm, norm_ffn, conv_all, comm)

    red = comm.finish([grad_x])
    grads = {"w_down": red["ffn"][0], "w_gate_up": red["ffn"][1], "w_out": red["a"][1], "w_mem_kv": red["b"][1],
             "w_in": _unperm_in(red["a"][0], red["b"][0])}
    small_grads["loss"] = jnp.broadcast_to(loss_local, (1, HD))
    packed, layout = _pack(small_grads)
    small = _unpack(_all_reduce_small("ar_small", packed, True), layout)
    loss = small["loss"][0, 0]
    six = {"fox_f_bias": L_FF, "gdn_a_log": L_GA, "gdn_dt_bias": L_GA}
    for n, rows_n in layout[:-1]:
        gsm = small[n]
        if n == "gdn_conv":
            gsm = lax.dynamic_slice(gsm.reshape(4, N_DEV * cshard), (0, me * cshard), (4, cshard))[None]
        elif n in six:
            gsm = gsm[:, six[n]:six[n] + 6]
        else:
            gsm = gsm.reshape(1, rows_n * HD)
        grads[n] = gsm

    names = ['norm_mix', 'w_in', 'fox_f_bias', 'fox_q_norm', 'fox_k_norm', 'gdn_conv', 'gdn_a_log', 'gdn_dt_bias',
             'gdn_out_norm', 'mem_norm', 'w_mem_kv', 'mem_q_norm', 'mem_k_norm', 'w_out', 'norm_ffn', 'w_gate_up', 'w_down']
    big = ("w_in", "w_mem_kv", "w_out", "w_gate_up", "w_down")
    delta, new_m, new_v = {}, {}, {}
    for n in big:
        delta[n], new_m[n], new_v[n] = [a[None] for a in _adamw_call(
            "adamw_" + n, args[n][0], grads[n], args["m_" + n][0], args["v_" + n][0])]
        grads[n] = grads[n][None]

    def flat(a):
        a = a.reshape(1, -1)
        return jnp.pad(a, ((0, 0), (0, -a.shape[1] % HD))).reshape(-1, HD)

    smalls = [n for n in names if n not in big]
    pk = lambda pre: jnp.concatenate([flat(grads[n] if pre == "g" else args[pre + n]) for n in smalls], axis=0)
    cat = [pk(""), pk("g"), pk("m_"), pk("v_")]
    padr = -cat[0].shape[0] % 8
    cat = [jnp.pad(a, ((0, padr), (0, 0))) for a in cat]
    res = _adamw_call("adamw_small", *cat)
    at = 0
    for n in smalls:
        shape = args[n].shape
        size = math.prod(shape)
        nrow = -(-size // HD)
        for dst, src in zip((delta, new_m, new_v), res):
            dst[n] = src[at:at + nrow].reshape(-1)[:size].reshape(shape)
        at += nrow

    return (loss, grad_x[None], *[grads[n] for n in names], *[delta[n] for n in names],
            *[new_m[n] for n in names], *[new_v[n] for n in names])


class _StepComm:
    def __init__(self, shard_groups, after):
        self.groups, shards = {}, []
        for key, ws in shard_groups.items():
            self.groups[key] = list(range(len(shards), len(shards) + len(ws)))
            shards += [w.astype(BF16) for w in ws]
        self.gather = _Gather("ag", shards, after)
        self.passed, self.scatters = set(), {}

    def start_deps(self):
        return [self.gather.token]

    def pass_on(self, key, after):
        self.passed.add(key)
        return self.gather.pass_on(self.groups[key], after)

    def weights(self, key, after):
        if key not in self.passed:
            after = self.pass_on(key, after)
        return self.gather.get(self.groups[key], after)

    def send(self, tag, grads):
        blocks = [g if g.ndim == 3 else g.reshape(N_DEV, g.shape[0] // N_DEV, g.shape[1]) for g in grads]
        self.scatters[tag] = _Scatter("rs_" + tag, blocks, ())
        return [self.scatters[tag].token]

    def mid(self, tag, after):
        self.scatters[tag].mid(after)
        return [self.scatters[tag].token]

    def finish(self, after):
        return {tag: sc.end(after) for tag, sc in self.scatters.items()}


def _local_step(xs, ms, tgt, norm_mix, fox_f_bias, fox_q_norm, fox_k_norm, gdn_a_log, gdn_dt_bias, gdn_out_norm,
                mem_norm, mem_q_norm, mem_k_norm, norm_ffn, conv_all, comm):
    t, d = xs.shape
    bq = min(t, 256)
    fb, alog, dtb = _lanes(fox_f_bias, L_FF), _lanes(gdn_a_log, L_GA), _lanes(gdn_dt_bias, L_GA)
    flat = lambda w: w.reshape(-1, w.shape[-1])

    rms1 = lambda a, g: (_rms(a, g),)
    (u,) = _rowwise("norm_mix", rms1, [xs], [norm_mix], [(d, BF16)], min(t, 256), deps=comm.start_deps())
    w_in_b = flat(comm.weights("in_b", [u])[0])
    pb = _matmul("proj_in_b", u, w_in_b, NN, F32, 1024, 768)
    o_fox = _fox_fwd(pb, fb, fox_q_norm, fox_k_norm, bq)
    w_in_a = flat(comm.weights("in_a", [o_fox])[0])
    pa = _matmul("proj_in_a", u, w_in_a, NN, F32, 1024, 768)
    gdn_terms = _gdn_fwd(pa, pb, conv_all, alog, dtb)
    o_gdn_raw, gdn_states = _gdn_scan(gdn_terms)
    gdn_saved = list(gdn_terms) + [gdn_states]
    zrow = (pa, NG * HD, GZ * HD // (NG * HD))
    (o_gdn,) = _rowwise("gdn_post", _gdn_post, [o_gdn_raw, zrow], [gdn_out_norm], [(NG * HD, BF16)], min(t, 256))
    w_kv_all, w_out_all = [flat(w) for w in comm.weights("kv_out", [o_gdn])]
    (mem_n,) = _rowwise("norm_mem", rms1, [ms], [mem_norm], [(d, BF16)], ms.shape[0])
    mkv = _matmul("proj_mem", mem_n, w_kv_all, NN, F32, 256, 512)
    o_mem = _mem_fwd(pb, mkv, mem_q_norm, mem_k_norm)
    deps = comm.pass_on("gate_up", [o_mem])
    mix = jnp.concatenate([o_fox, o_gdn, o_mem], axis=1)
    h1 = _matmul("proj_out", mix, w_out_all, NN, F32, 1024, 1024, residual=xs, deps=deps)
    (h1n,) = _rowwise("norm_ffn", rms1, [h1], [norm_ffn], [(d, BF16)], min(t, 256))
    (wgu,) = comm.weights("gate_up", [h1n])
    ffw = wgu.shape[2]
    gu, act = _ffn_up(h1n, wgu.reshape(2, 4, d, ffw))
    w_down_all = flat(comm.weights("down", [act])[0])
    dy, dyb, lsum = _ffn_down_loss(act, w_down_all, h1, tgt)
    loss_local = (0.5 / d) * jnp.sum(lsum[::8, ::HD])

    dgu = _ffn_down_bwd(dyb, w_down_all.reshape(4, ffw, d), gu).reshape(8, t, ffw)
    g_w_down = _matmul("grad_w_down", act, dyb, TN, BF16, 512, 2048)
    dh1n = _ffn_up_bwd_x(dgu, wgu)
    g_w_gu = _ffn_up_bwd_w(h1n, dgu)
    deps = comm.send("ffn", [g_w_down, g_w_gu])
    rms2 = lambda a, g: (_rms(a, g), a)
    dh1, g_norm_ffn = _rowwise_vjp("norm_ffn_bwd", rms2, [h1], [norm_ffn], [dh1n, dy], [F32], min(t, 256), deps=deps)
    dh1b = dh1.astype(BF16)

    dmix = _matmul("proj_out_bwd_x", dh1b, w_out_all, NT, F32, 1024, 1024)
    g_w_out = _matmul("grad_w_out", mix, dh1b, TN, BF16, 1024, 2048)
    deps = comm.mid("ffn", [dmix, g_w_out])
    do_raw, dgz, g_gon = _rowwise_vjp("gdn_post_bwd", _gdn_post, [o_gdn_raw, zrow], [gdn_out_norm],
                                      [(dmix, NG * HD, 1)], [F32, BF16], min(t, 256), deps=deps)
    dterms = _gdn_bwd_scan(gdn_saved, do_raw)
    dgq, dgk, dgv, dsm_gdn, dwq, dwk, dwv, g_alog, g_dtb = _gdn_bwd(pa, pb, conv_all, alog, dtb, dterms)
    dp_a = jnp.concatenate([dgq, dgk, dgv, dgz], axis=1)
    g_w_in_a = _matmul("grad_w_in_a", u, dp_a, TN, BF16, 512, 3072)
    deps = comm.send("a", [g_w_in_a, g_w_out])
    dmq, dmk, dmv, g_mqn, g_mkn = _mem_bwd(pb, mkv, mem_q_norm, mem_k_norm, dmix, deps=deps)
    dmkv = jnp.concatenate([dmk, dmv], axis=1).astype(BF16)
    dmem_n = _matmul("proj_mem_bwd_x", dmkv, w_kv_all, NT, F32, 256, 512)
    g_w_kv = _matmul("grad_w_kv", mem_n, dmkv, TN, BF16, 512, 512)
    g_mem_norm = _rowwise_vjp("norm_mem_bwd", rms1, [ms], [mem_norm], [dmem_n], [], ms.shape[0])[0]
    deps = comm.mid("a", [g_mem_norm, g_w_kv])
    dfq, dfk, dfv, dsm_fox, g_fb, g_fqn, g_fkn = _fox_bwd(pb, fb, fox_q_norm, fox_k_norm, dmix, bq, deps=deps)
    dp_b = jnp.concatenate([dfq, dfk, dfv, dmq, (dsm_fox + dsm_gdn).astype(BF16), jnp.zeros((t, HD), BF16)], axis=1)
    g_w_in_b = _matmul("grad_w_in_b", u, dp_b, TN, BF16, 512, 3072)
    deps = comm.send("b", [g_w_in_b, g_w_kv])
    du_a = _matmul("proj_in_bwd_a", dp_a, w_in_a, NT, F32, 1024, 1024, deps=deps)
    deps = comm.mid("b", [du_a])
    du = _matmul("proj_in_bwd_b", dp_b, w_in_b, NT, F32, 1024, 1024, residual=du_a, deps=deps)
    grad_x, g_norm_mix = _rowwise_vjp("norm_mix_bwd", rms2, [xs], [norm_mix], [du, dh1], [F32], min(t, 256))

    small_grads = {
        "norm_mix": g_norm_mix, "mem_norm": g_mem_norm, "norm_ffn": g_norm_ffn,
        "gdn_conv": jnp.concatenate([dwq, dwk, dwv], axis=1),
        "fox_q_norm": g_fqn, "fox_k_norm": g_fkn, "gdn_out_norm": g_gon, "mem_q_norm": g_mqn, "mem_k_norm": g_mkn,
        "fox_f_bias": g_fb, "gdn_a_log": g_alog, "gdn_dt_bias": g_dtb}
    return grad_x, loss_local, small_grads
```

```python
import functools
import math

import jax
import jax.numpy as jnp
from jax import lax
from jax.experimental import pallas as pl
from jax.experimental.pallas import tpu as pltpu

F32 = jnp.float32
BF16 = jnp.bfloat16
HI = lax.Precision.HIGHEST
SDS = jax.ShapeDtypeStruct

N_DEV = 8
HD = 128
NF, NG, NM = 6, 6, 4
CHUNK = 64
GROUP = 16
NORM_EPS = 1e-6
GQ, GK, GV, GZ = 0, 6, 12, 18
FQ, FK, FV, MQ, SM = 0, 6, 12, 18, 22
HALF = 24 * HD
L_FF, L_GA, L_GB = 0, 6, 12
VMEM_LIMIT = 56 * 1024 * 1024

ADAM_LR, ADAM_B1, ADAM_B2, ADAM_EPS, ADAM_WD, ADAM_STEP = 0.001, 0.9, 0.999, 1e-08, 0.01, 10

NN = (((1,), (0,)), ((), ()))
NT = (((1,), (1,)), ((), ()))
TN = (((0,), (0,)), ((), ()))
MESH = pl.DeviceIdType.MESH


def _cp(*sem):
    return pltpu.CompilerParams(dimension_semantics=tuple(sem) if sem else None, vmem_limit_bytes=VMEM_LIMIT)


def _dot(a, b, dims=NN):
    return lax.dot_general(a, b, dims, preferred_element_type=F32)


def _bdot(a, b):
    return _dot(a.astype(BF16), b.astype(BF16))


def _iota(shape, axis):
    return lax.broadcasted_iota(jnp.int32, shape, axis)


def _rms(x, gain):
    return x * lax.rsqrt(jnp.mean(x * x, axis=-1, keepdims=True) + NORM_EPS) * gain


def _sigmoid(x):
    return 0.5 * jnp.tanh(0.5 * x) + 0.5


def _silu(x):
    return x * _sigmoid(x)


def _softplus(x):
    return jnp.maximum(x, 0.0) + jnp.log(1.0 + jnp.exp(-jnp.abs(x)))


def _lane_pick(x, lane):
    oh = (_iota((1, x.shape[-1]), 1) == lane).astype(F32)
    return jnp.sum(x * oh, axis=-1, keepdims=True)


def _cumsum_rows(x):
    tril = (_iota((HD, HD), 0) >= _iota((HD, HD), 1)).astype(F32)
    carry = jnp.zeros((1, x.shape[1]), F32)
    outs = []
    for b in range(x.shape[0] // HD):
        blk = x[b * HD:(b + 1) * HD]
        outs.append(jnp.dot(tril, blk, precision=HI, preferred_element_type=F32) + carry)
        carry = carry + jnp.sum(blk, axis=0, keepdims=True)
    return jnp.concatenate(outs, axis=0)


def _row_spec(r, tm):
    if isinstance(r, tuple):
        arr, width, cb = r
        return arr, pl.BlockSpec((tm, width), lambda i, cb=cb: (i, cb))
    return r, pl.BlockSpec((tm, r.shape[1]), lambda i: (i, 0))


ANY_SPEC = pl.BlockSpec(memory_space=pl.ANY)


def _rowwise(name, fn, rows, consts, outs, tm, deps=()):
    arrs, specs = zip(*[_row_spec(r, tm) for r in rows])
    n_rows = arrs[0].shape[0]
    nr, nc, nd = len(rows), len(consts), len(deps)

    def body(*refs):
        res = fn(*[r[...] for r in refs[:nr + nc]])
        for o, v in zip(refs[nr + nc + nd:], res):
            o[...] = v.astype(o.dtype)

    return pl.pallas_call(
        body, grid=(n_rows // tm,), name=name,
        in_specs=list(specs) + [pl.BlockSpec(c.shape, lambda i: (0, 0)) for c in consts] + [ANY_SPEC] * nd,
        out_specs=[pl.BlockSpec((tm, w), lambda i: (i, 0)) for w, _ in outs],
        out_shape=[SDS((n_rows, w), dt) for w, dt in outs],
        compiler_params=_cp("parallel"),
    )(*arrs, *consts, *deps)


def _rowwise_vjp(name, fn, rows, consts, cts, grad_dtypes, tm, deps=()):
    arrs, specs = zip(*[_row_spec(r, tm) for r in rows])
    ct_arrs, ct_specs = zip(*[_row_spec(r, tm) for r in cts])
    n_rows = arrs[0].shape[0]
    nr, nc, nct, ng, nd = len(rows), len(consts), len(cts), len(grad_dtypes), len(deps)
    widths = [s.block_shape[1] for s in specs[:ng]]

    def body(*refs):
        vals = [r[...].astype(F32) for r in refs[:nr + nc]]
        ctv = tuple(r[...].astype(F32) for r in refs[nr + nc:nr + nc + nct])
        _, vjp = jax.vjp(fn, *vals)
        grads = vjp(ctv)
        outs = refs[nr + nc + nct + nd:]
        for o, g in zip(outs[:ng], grads[:ng]):
            o[...] = g.astype(o.dtype)

        @pl.when(pl.program_id(0) == 0)
        def _():
            for o in outs[ng:]:
                o[...] = jnp.zeros_like(o)

        for o, g in zip(outs[ng:], grads[nr:]):
            o[...] += g

    return pl.pallas_call(
        body, grid=(n_rows // tm,), name=name,
        in_specs=list(specs) + [pl.BlockSpec(c.shape, lambda i: (0, 0)) for c in consts] + list(ct_specs)
        + [ANY_SPEC] * nd,
        out_specs=[pl.BlockSpec((tm, w), lambda i: (i, 0)) for w in widths]
        + [pl.BlockSpec(c.shape, lambda i: (0, 0)) for c in consts],
        out_shape=[SDS((n_rows, w), dt) for w, dt in zip(widths, grad_dtypes)] + [SDS(c.shape, F32) for c in consts],
        compiler_params=_cp("arbitrary"),
    )(*arrs, *consts, *ct_arrs, *deps)


def _tile(n, pref):
    t = min(n, pref)
    while n % t or (t % HD and t != n):
        t -= 1
    return t


def _matmul(name, a, b, dims, out_dtype, tm, tn, residual=None, deps=()):
    ta, tb = dims == TN, dims == NT
    m = a.shape[1] if ta else a.shape[0]
    k = a.shape[0] if ta else a.shape[1]
    n = b.shape[0] if tb else b.shape[1]
    tm, tn = _tile(m, tm), _tile(n, tn)

    def body(*refs):
        acc = _dot(refs[0][...], refs[1][...], dims)
        if residual is not None:
            acc = acc + refs[2][...]
        refs[-1][...] = acc.astype(out_dtype)

    in_specs = [pl.BlockSpec((k, tm), lambda i, j: (0, i)) if ta else pl.BlockSpec((tm, k), lambda i, j: (i, 0)),
                pl.BlockSpec((tn, k), lambda i, j: (j, 0)) if tb else pl.BlockSpec((k, tn), lambda i, j: (0, j))]
    ops = [a, b]
    if residual is not None:
        in_specs.append(pl.BlockSpec((tm, tn), lambda i, j: (i, j)))
        ops.append(residual)
    in_specs += [ANY_SPEC] * len(deps)
    ops += list(deps)
    return pl.pallas_call(
        body, grid=(m // tm, n // tn), name=name, in_specs=in_specs,
        out_specs=pl.BlockSpec((tm, tn), lambda i, j: (i, j)), out_shape=SDS((m, n), out_dtype),
        compiler_params=_cp("parallel", "parallel"),
    )(*ops)


def _ffn_up(h1n, wgu):
    t, d = h1n.shape
    w = wgu.shape[3]
    tm = _tile(t, 512)

    def body(a, b, gu, act):
        x = a[...]
        g = _dot(x, b[0])
        u = _dot(x, b[1])
        gu[0] = g.astype(BF16)
        gu[1] = u.astype(BF16)
        act[...] = (_silu(g) * u).astype(BF16)

    return pl.pallas_call(
        body, grid=(4, t // tm), name="ffn_up",
        in_specs=[pl.BlockSpec((tm, d), lambda j, i: (i, 0)), pl.BlockSpec((2, None, d, w), lambda j, i: (0, j, 0, 0))],
        out_specs=[pl.BlockSpec((2, None, tm, w), lambda j, i: (0, j, i, 0)), pl.BlockSpec((tm, w), lambda j, i: (i, j))],
        out_shape=[SDS((2, 4, t, w), BF16), SDS((t, 4 * w), BF16)],
        compiler_params=_cp("parallel", "parallel"),
    )(h1n, wgu)


def _ffn_down_loss(act, wdown, h1, target):
    t, f = act.shape
    d = wdown.shape[1]
    tm, tn = _tile(t, 1024), _tile(d, 512)

    def body(a, b, h, tg, dy, dyb, ls):
        e = _dot(a[...], b[...]) + h[...] - tg[...]
        g = e * (1.0 / d)
        dy[...] = g
        dyb[...] = g.astype(BF16)
        ls[...] = jnp.broadcast_to(jnp.sum(e * e), (8, HD))

    return pl.pallas_call(
        body, grid=(t // tm, d // tn), name="ffn_down_loss",
        in_specs=[pl.BlockSpec((tm, f), lambda i, j: (i, 0)), pl.BlockSpec((f, tn), lambda i, j: (0, j)),
                  pl.BlockSpec((tm, tn), lambda i, j: (i, j)), pl.BlockSpec((tm, tn), lambda i, j: (i, j))],
        out_specs=[pl.BlockSpec((tm, tn), lambda i, j: (i, j)), pl.BlockSpec((tm, tn), lambda i, j: (i, j)),
                   pl.BlockSpec((8, HD), lambda i, j: (i, j))],
        out_shape=[SDS((t, d), F32), SDS((t, d), BF16), SDS((8 * (t // tm), HD * (d // tn)), F32)],
        compiler_params=_cp("parallel", "parallel"),
    )(act, wdown, h1, target)


def _ffn_down_bwd(dyb, wdown4, gu):
    t, d = dyb.shape
    w = wdown4.shape[1]
    tm = _tile(t, 512)

    def body(a, b, gu_ref, out):
        da = _dot(a[...], b[...], NT)
        g = gu_ref[0].astype(F32)
        u = gu_ref[1].astype(F32)
        s = _sigmoid(g)
        out[0] = (da * u * (s * (1.0 + g * (1.0 - s)))).astype(BF16)
        out[1] = (da * g * s).astype(BF16)

    return pl.pallas_call(
        body, grid=(4, t // tm), name="ffn_down_bwd",
        in_specs=[pl.BlockSpec((tm, d), lambda j, i: (i, 0)), pl.BlockSpec((None, w, d), lambda j, i: (j, 0, 0)),
                  pl.BlockSpec((2, None, tm, w), lambda j, i: (0, j, i, 0))],
        out_specs=pl.BlockSpec((2, None, tm, w), lambda j, i: (0, j, i, 0)),
        out_shape=SDS((2, 4, t, w), BF16),
        compiler_params=_cp("parallel", "parallel"),
    )(dyb, wdown4, gu)


def _ffn_up_bwd_x(dgu, wgu):
    _, t, w = dgu.shape
    d = wgu.shape[1]
    tm = _tile(t, 512)

    def body(a, b, out):
        @pl.when(pl.program_id(1) == 0)
        def _():
            out[...] = jnp.zeros_like(out)
        out[...] += _dot(a[...], b[...], NT)

    return pl.pallas_call(
        body, grid=(t // tm, 8), name="ffn_up_bwd_x",
        in_specs=[pl.BlockSpec((None, tm, w), lambda i, j: (j, i, 0)), pl.BlockSpec((None, d, w), lambda i, j: (j, 0, 0))],
        out_specs=pl.BlockSpec((tm, d), lambda i, j: (i, 0)), out_shape=SDS((t, d), F32),
        compiler_params=_cp("parallel", "arbitrary"),
    )(dgu, wgu)


def _ffn_up_bwd_w(h1n, dgu):
    _, t, w = dgu.shape
    d = h1n.shape[1]
    tm = _tile(d, 512)

    def body(a, b, out):
        out[...] = _dot(a[...], b[...], TN).astype(BF16)

    return pl.pallas_call(
        body, grid=(8, d // tm), name="ffn_up_bwd_w",
        in_specs=[pl.BlockSpec((t, tm), lambda j, i: (0, i)), pl.BlockSpec((None, t, w), lambda j, i: (j, 0, 0))],
        out_specs=pl.BlockSpec((None, tm, w), lambda j, i: (j, i, 0)), out_shape=SDS((8, d, w), BF16),
        compiler_params=_cp("parallel", "parallel"),
    )(h1n, dgu)


def _fox_prep(fq, fk, sm, fb, qg, kg, h):
    qn = _rms(fq, qg)
    kn = _rms(fk, kg)
    c = _cumsum_rows(-_softplus(-(sm + fb)))
    ccol = _lane_pick(c, L_FF + h)
    crow = jnp.sum(c.T * (_iota((HD, 1), 0) == L_FF + h).astype(F32), axis=0, keepdims=True)
    return qn, kn, ccol, crow


def _fox_block(q, k, v, cc, cr, off):
    bq = q.shape[0]
    assert k.shape[0] == off + bq
    s = _dot(q.astype(BF16), k.astype(BF16), NT) * (HD ** -0.5) + cc - cr
    diag = jnp.where(_iota((bq, bq), 1) <= _iota((bq, bq), 0), s[:, off:], -1e30)
    s = jnp.concatenate([s[:, :off], diag], axis=1) if off else diag
    e = jnp.exp(s - lax.stop_gradient(jnp.max(s, axis=1, keepdims=True)))
    p = e / jnp.sum(e, axis=1, keepdims=True)
    return _dot(p.astype(BF16), v.astype(BF16))


ONE_BUFFER = pl.Buffered(1)


def _pcol(t, cb):
    return pl.BlockSpec((t, HD), lambda h, cb=cb: (0, cb + h), pipeline_mode=ONE_BUFFER)


def _smcol(t):
    return pl.BlockSpec((t, HD), lambda h: (0, SM), pipeline_mode=ONE_BUFFER)


def _head(t):
    return pl.BlockSpec((t, HD), lambda h: (0, h), pipeline_mode=ONE_BUFFER)


def _small(n):
    return pl.BlockSpec((n, HD), lambda h: (0, 0), pipeline_mode=ONE_BUFFER)


def _fox_fwd(p, fb, qg, kg, bq):
    t = p.shape[0]

    def body(fq, fk, fv, sm, fb_r, qg_r, kg_r, o, qn_s, cc_s):
        h = pl.program_id(0)
        qn, kn, ccol, crow = _fox_prep(fq[...], fk[...], sm[...], fb_r[...], qg_r[...], kg_r[...], h)
        qn_s[...] = qn
        cc_s[...] = ccol
        knb = kn.astype(BF16)
        vb = fv[...].astype(BF16)
        for i in range(t // bq):
            rows, ext = pl.ds(i * bq, bq), (i + 1) * bq
            o[rows, :] = _fox_block(qn_s[rows, :], knb[:ext], vb[:ext], cc_s[rows, :], crow[:, :ext], i * bq).astype(o.dtype)

    return pl.pallas_call(
        body, grid=(NF,), name="fox_fwd",
        in_specs=[_pcol(t, FQ), _pcol(t, FK), _pcol(t, FV), _smcol(t), _small(1), _small(1), _small(1)],
        out_specs=_head(t), out_shape=SDS((t, NF * HD), BF16),
        scratch_shapes=[pltpu.VMEM((t, HD), F32), pltpu.VMEM((t, 1), F32)],
        compiler_params=_cp("parallel"),
    )(p, p, p, p, fb, qg, kg)


def _fox_bwd(p, fb, qg, kg, dmix, bq, deps=()):
    t = p.shape[0]

    def body(*refs):
        fq, fk, fv, sm, fb_r, qg_r, kg_r, do = refs[:8]
        dfq, dfk, dfv, dsm, dfb, dqg, dkg, qn_s, cc_s, dqn_s, dcc_s, dkn_s, dv_s, dcr_s = refs[8 + len(deps):]
        h = pl.program_id(0)
        qn, kn, ccol, crow = _fox_prep(fq[...], fk[...], sm[...], fb_r[...], qg_r[...], kg_r[...], h)
        qn_s[...] = qn
        cc_s[...] = ccol
        v = fv[...]
        dkn_s[...] = jnp.zeros_like(dkn_s)
        dv_s[...] = jnp.zeros_like(dv_s)
        dcr_s[...] = jnp.zeros_like(dcr_s)

        for i in range(t // bq):
            rows, ext = pl.ds(i * bq, bq), (i + 1) * bq
            _, vjp = jax.vjp(lambda a, b, c, d, e, off=i * bq: _fox_block(a, b, c, d, e, off),
                             qn_s[rows, :], kn[:ext], v[:ext], cc_s[rows, :], crow[:, :ext])
            dq, dk, dv, dcc, dcr = vjp(do[rows, :])
            dqn_s[rows, :] = dq
            dcc_s[rows, :] = dcc
            dkn_s[:ext, :] += dk
            dv_s[:ext, :] += dv
            dcr_s[:, :ext] += dcr
        _, prep_vjp = jax.vjp(lambda a, b, c, d, e, f: _fox_prep(a, b, c, d, e, f, h),
                              fq[...], fk[...], sm[...], fb_r[...], qg_r[...], kg_r[...])
        g_fq, g_fk, g_sm, g_fb, g_qg, g_kg = prep_vjp((dqn_s[...], dkn_s[...], dcc_s[...], dcr_s[...]))
        dfq[...] = g_fq.astype(dfq.dtype)
        dfk[...] = g_fk.astype(dfk.dtype)
        dfv[...] = dv_s[...].astype(dfv.dtype)

        @pl.when(h == 0)
        def _():
            for r in (dsm, dfb, dqg, dkg):
                r[...] = jnp.zeros_like(r)

        dsm[...] += g_sm
        dfb[...] += g_fb
        dqg[...] += g_qg
        dkg[...] += g_kg

    head = _head(t)
    return pl.pallas_call(
        body, grid=(NF,), name="fox_bwd",
        in_specs=[_pcol(t, FQ), _pcol(t, FK), _pcol(t, FV), _smcol(t), _small(1), _small(1), _small(1), head]
        + [ANY_SPEC] * len(deps),
        out_specs=[head, head, head, _small(t), _small(1), _small(1), _small(1)],
        out_shape=[SDS((t, NF * HD), BF16)] * 3 + [SDS((t, HD), F32)] + [SDS((1, HD), F32)] * 3,
        scratch_shapes=[pltpu.VMEM((t, HD), F32), pltpu.VMEM((t, 1), F32), pltpu.VMEM((t, HD), F32),
                        pltpu.VMEM((t, 1), F32), pltpu.VMEM((t, HD), F32), pltpu.VMEM((t, HD), F32),
                        pltpu.VMEM((1, t), F32)],
        compiler_params=_cp("arbitrary"),
    )(p, p, p, p, fb, qg, kg, dmix, *deps)


def _mem_attn(mq, mk, mv, qg, kg):
    s = _dot(_rms(mq, qg).astype(BF16), _rms(mk, kg).astype(BF16), NT) * (HD ** -0.5)
    e = jnp.exp(s - lax.stop_gradient(jnp.max(s, axis=1, keepdims=True)))
    p = e / jnp.sum(e, axis=1, keepdims=True)
    return _dot(p.astype(BF16), mv.astype(BF16))


def _mem_fwd(p, mkv, qg, kg):
    t, ml = p.shape[0], mkv.shape[0]

    def body(mq, mk, mv, qg_r, kg_r, o):
        o[...] = _mem_attn(mq[...], mk[...], mv[...], qg_r[...], kg_r[...]).astype(o.dtype)

    return pl.pallas_call(
        body, grid=(NM,), name="mem_fwd",
        in_specs=[_pcol(t, MQ), pl.BlockSpec((ml, HD), lambda h: (0, h)), pl.BlockSpec((ml, HD), lambda h: (0, NM + h)),
                  _small(1), _small(1)],
        out_specs=pl.BlockSpec((t, HD), lambda h: (0, h)), out_shape=SDS((t, NM * HD), BF16),
        compiler_params=_cp("parallel"),
    )(p, mkv, mkv, qg, kg)


def _mem_bwd(p, mkv, qg, kg, dmix, deps=()):
    t, ml = p.shape[0], mkv.shape[0]

    def body(*refs):
        mq, mk, mv, qg_r, kg_r, do = refs[:6]
        dmq, dmk, dmv, dqg, dkg = refs[6 + len(deps):]
        _, vjp = jax.vjp(_mem_attn, mq[...], mk[...], mv[...], qg_r[...], kg_r[...])
        g_q, g_k, g_v, g_qg, g_kg = vjp(do[...])
        dmq[...] = g_q.astype(dmq.dtype)
        dmk[...] = g_k
        dmv[...] = g_v

        @pl.when(pl.program_id(0) == 0)
        def _():
            dqg[...] = jnp.zeros_like(dqg)
            dkg[...] = jnp.zeros_like(dkg)

        dqg[...] += g_qg
        dkg[...] += g_kg

    return pl.pallas_call(
        body, grid=(NM,), name="mem_bwd",
        in_specs=[_pcol(t, MQ), pl.BlockSpec((ml, HD), lambda h: (0, h)), pl.BlockSpec((ml, HD), lambda h: (0, NM + h)),
                  _small(1), _small(1), pl.BlockSpec((t, HD), lambda h: (0, NF + NG + h))] + [ANY_SPEC] * len(deps),
        out_specs=[pl.BlockSpec((t, HD), lambda h: (0, h)), pl.BlockSpec((ml, HD), lambda h: (0, h)),
                   pl.BlockSpec((ml, HD), lambda h: (0, h)), _small(1), _small(1)],
        out_shape=[SDS((t, NM * HD), BF16), SDS((ml, NM * HD), F32), SDS((ml, NM * HD), F32),
                   SDS((1, HD), F32), SDS((1, HD), F32)],
        compiler_params=_cp("arbitrary"),
    )(p, mkv, mkv, qg, kg, dmix, *deps)


def _shift_down(x, s):
    if s == 0:
        return x
    return jnp.where(_iota(x.shape, 0) >= s, pltpu.roll(x, s, 0), 0.0)


def _shift_up(x, s):
    if s == 0:
        return x
    n = x.shape[0]
    return jnp.where(_iota(x.shape, 0) < n - s, pltpu.roll(x, n - s, 0), 0.0)


@jax.custom_vjp
def _conv4(x, w0, w1, w2, w3):
    return w0 * _shift_down(x, 3) + w1 * _shift_down(x, 2) + w2 * _shift_down(x, 1) + w3 * x


def _conv4_fwd(x, w0, w1, w2, w3):
    return _conv4(x, w0, w1, w2, w3), (x, w0, w1, w2, w3)


def _conv4_bwd(res, dy):
    x, w0, w1, w2, w3 = res
    dx = w0 * _shift_up(dy, 3) + w1 * _shift_up(dy, 2) + w2 * _shift_up(dy, 1) + w3 * dy
    dws = tuple(jnp.sum(dy * _shift_down(x, 3 - k), axis=0, keepdims=True) for k in range(4))
    return (dx,) + dws


_conv4.defvjp(_conv4_fwd, _conv4_bwd)


HALO = 8


def _gdn_prep(gq, gk, gv, sm, taps, alog, dtb, h):
    q, k, v = [_silu(_conv4(x, *taps[4 * j:4 * j + 4]))[HALO:] for j, x in enumerate((gq, gk, gv))]
    q = q * lax.rsqrt(jnp.sum(q * q, axis=-1, keepdims=True) + NORM_EPS) * (HD ** -0.5)
    k = k * lax.rsqrt(jnp.sum(k * k, axis=-1, keepdims=True) + NORM_EPS)
    g = _lane_pick(-jnp.exp(alog) * _softplus(sm + dtb), L_GA + h)
    beta = _lane_pick(_sigmoid(sm), L_GB + h)
    return q, k, v, g, beta


def _split(x, n):
    parts, rest = [], x
    for i in range(n):
        parts.append(rest.astype(BF16))
        if i + 1 < n:
            rest = rest - parts[-1].astype(F32)
    return parts


def _raw_dot(a, b, form):
    lead = a.ndim - 2
    ca, cb = {"nn": (1, 0), "nt": (1, 1), "tn": (0, 0)}[form]
    batch = ((0,), (0,)) if lead else ((), ())
    return lax.dot_general(a, b, (((ca + lead,), (cb + lead,)), batch), preferred_element_type=F32)


def _pdot_impl(a, b, form, mode):
    if mode == "1":
        return _raw_dot(a.astype(BF16), b.astype(BF16), form)
    if mode == "3":
        (ah, al), (bh, bl) = _split(a, 2), _split(b, 2)
        return _raw_dot(ah, bh, form) + (_raw_dot(al, bh, form) + _raw_dot(ah, bl, form))
    if mode == "xa":
        return sum(_raw_dot(a.astype(BF16), t, form) for t in reversed(_split(b, 3)))
    return sum(_raw_dot(t, b.astype(BF16), form) for t in reversed(_split(a, 3)))


@functools.partial(jax.custom_vjp, nondiff_argnums=(2, 3))
def _pdot(a, b, form, mode):
    return _pdot_impl(a, b, form, mode)


def _pdot_fwd(a, b, form, mode):
    return _pdot_impl(a, b, form, mode), (a, b)


def _pdot_bwd(form, mode, res, ct):
    a, b = res
    da_args, db_args = {"nn": ((ct, b, "nt"), (a, ct, "tn")), "nt": ((ct, b, "nn"), (ct, a, "tn")),
                        "tn": ((b, ct, "nt"), (a, ct, "nn"))}[form]

    def side(args, exact):
        if mode in ("1", "3"):
            return mode
        return "xa" if args[0] is exact else "xb"

    if mode == "xa":
        return jnp.zeros_like(a), _pdot_impl(*db_args, side(db_args, a))
    if mode == "xb":
        return _pdot_impl(*da_args, side(da_args, b)), jnp.zeros_like(b)
    return _pdot_impl(*da_args, mode), _pdot_impl(*db_args, mode)


_pdot.defvjp(_pdot_fwd, _pdot_bwd)

GDN_QK, GDN_INV, GDN_SCAN = "1", "1", "1"


@jax.custom_vjp
def _tri_inv(low):
    eye = (_iota((CHUNK, CHUNK), 0) == _iota((CHUNK, CHUNK), 1)).astype(F32)
    inv = eye - low
    pw = low
    for _ in range(5):
        pw = _pdot_impl(pw, pw, "nn", GDN_INV)
        inv = inv + _pdot_impl(inv, pw, "nn", GDN_INV)
    return inv


def _tri_inv_fwd(low):
    inv = _tri_inv(low)
    return inv, inv


def _tri_inv_bwd(inv, ct):
    return (-_pdot_impl(_pdot_impl(inv, ct, "tn", GDN_INV), inv, "nt", GDN_INV),)


_tri_inv.defvjp(_tri_inv_fwd, _tri_inv_bwd)


def _gdn_intra(q, k, v, g, beta):
    n = q.shape[0]
    r, c = _iota((CHUNK, CHUNK), 0), _iota((CHUNK, CHUNK), 1)
    tril, strict = r >= c, r > c
    trilf = jnp.broadcast_to(tril.astype(F32), (n, CHUNK, CHUNK))
    gcm = _pdot(trilf, jnp.broadcast_to(g, (n, CHUNK, CHUNK)), "nn", "xa")
    gcf = _pdot(trilf, jnp.broadcast_to(g, (n, CHUNK, HD)), "nn", "xa")
    lane0 = (_iota((1, 1, CHUNK), 2) == 0).astype(F32)
    gcr = _pdot(jnp.ones((n, CHUNK, CHUNK), F32), gcm * lane0, "nt", "xa")
    decay = jnp.where(tril, jnp.exp(jnp.where(tril, gcm - gcr, 0.0)), 0.0)
    egc = jnp.exp(gcf)
    kb = k * beta
    low = jnp.where(strict, _pdot(kb, k, "nt", GDN_QK) * decay, 0.0)
    inv = _tri_inv(low)
    u = _pdot(inv, v * beta, "nn", GDN_INV)
    w = _pdot(inv, kb * egc, "nn", GDN_INV)
    at = jnp.where(tril, _pdot(q, k, "nt", GDN_QK) * decay, 0.0)
    gl = jnp.sum(jnp.broadcast_to(g, (n, CHUNK, HD)), axis=1, keepdims=True)
    return u, w, q * egc, at, k * jnp.exp(gl - gcf), gl


def _gdn_step(s, u, w, qg, at, kd, gl):
    vn = u - _pdot(w, s, "nn", GDN_SCAN)
    o = _pdot(qg, s, "nn", GDN_SCAN) + _pdot(at, vn, "nn", GDN_SCAN)
    s2 = s * jnp.exp(gl) + _pdot(kd, vn, "tn", GDN_SCAN)
    return o, s2


SCAN_HEADS = 3


def _gdn_chunked_scratch(nc):
    big = pltpu.VMEM((nc, CHUNK, HD), F32)
    return [big, big, big, pltpu.VMEM((nc, CHUNK, 1), F32), pltpu.VMEM((nc, CHUNK, 1), F32)]


def _gdn_term_shapes(nc):
    return [(nc, CHUNK, HD), (nc, CHUNK, HD), (nc, CHUNK, HD), (nc, CHUNK, CHUNK), (nc, CHUNK, HD), (nc, 1, HD)]


def _per_head(shape, heads=None, one_buffer=True):
    lead = (None,) if heads is None else (heads,)
    return pl.BlockSpec(lead + tuple(shape), lambda h: (h,) + (0,) * len(shape),
                        pipeline_mode=ONE_BUFFER if one_buffer else None)


def _gdn_in_specs(t):
    cw = lambda cb: pl.BlockSpec((4, HD), lambda h, cb=cb: (0, cb + h))
    return [_pcol(t, GQ), _pcol(t, GK), _pcol(t, GV), _smcol(t), cw(0), cw(NG), cw(2 * NG), _small(1), _small(1)]


def _taps(wq, wk, wv):
    return tuple(w[k:k + 1, :] for w in (wq, wk, wv) for k in range(4))


def _prep_rows(t):
    return min(t, 256)


def _gdn_pad(srcs, pads):
    for src, pad in zip(srcs, pads):
        pad[0:HALO, :] = jnp.zeros((HALO, HD), F32)
        pad[HALO:, :] = src[...]


def _gdn_stage(pads, sm, taps, al, db, h, chunked):
    t = sm.shape[0]
    rows = _prep_rows(t)
    per = rows // CHUNK

    def tile(i, carry):
        r0 = pl.multiple_of(i * rows, rows)
        vals = _gdn_prep(*[p[pl.ds(r0, rows + HALO), :] for p in pads], sm[pl.ds(r0, rows), :], taps, al, db, h)
        for v, r in zip(vals, chunked):
            r[pl.ds(i * per, per)] = v.reshape(per, CHUNK, v.shape[-1])
        return carry

    lax.fori_loop(0, t // rows, tile, 0)


def _gdn_intra_all(chunked, intra):
    nc = chunked[0].shape[0]
    grp_n = math.gcd(nc, GROUP)

    def grp(i, carry):
        sl = pl.ds(pl.multiple_of(i * grp_n, grp_n), grp_n)
        for r, val in zip(intra, _gdn_intra(*[c[sl] for c in chunked])):
            r[sl] = val
        return carry

    lax.fori_loop(0, nc // grp_n, grp, 0)


def _gdn_fwd(pa, pb, conv, alog, dtb):
    t = pa.shape[0]
    nc = t // CHUNK
    terms = _gdn_term_shapes(nc)

    def body(gq, gk, gv, sm, wq, wk, wv, al, db, *rest):
        h = pl.program_id(0)
        intra, chunked, pads = rest[:6], rest[6:11], rest[11:]
        _gdn_pad((gq, gk, gv), pads)
        _gdn_stage(pads, sm, _taps(wq, wk, wv), al[...], db[...], h, chunked)
        _gdn_intra_all(chunked, intra)

    return pl.pallas_call(
        body, grid=(NG,), name="gdn_fwd", in_specs=_gdn_in_specs(t),
        out_specs=[_per_head(sh, one_buffer=False) for sh in terms], out_shape=[SDS((NG,) + sh, F32) for sh in terms],
        scratch_shapes=_gdn_chunked_scratch(nc) + [pltpu.VMEM((t + HALO, HD), F32)] * 3, compiler_params=_cp("parallel"),
    )(pa, pa, pa, pb, conv, conv, conv, alog, dtb)


def _gdn_scan(terms_in):
    nc = terms_in[0].shape[1]
    terms = _gdn_term_shapes(nc)

    def body(*refs):
        intra, o, states = refs[:6], refs[6], refs[7]

        def step(c, ss):
            rows = pl.ds(pl.multiple_of(c * CHUNK, CHUNK), CHUNK)
            new = []
            for hh in range(SCAN_HEADS):
                states[hh, c] = ss[hh]
                oc, s2 = _gdn_step(ss[hh], *[r[hh, c] for r in intra])
                o[rows, hh * HD:(hh + 1) * HD] = oc
                new.append(s2)
            return tuple(new)

        lax.fori_loop(0, nc, step, tuple(jnp.zeros((HD, HD), F32) for _ in range(SCAN_HEADS)))

    return pl.pallas_call(
        body, grid=(NG // SCAN_HEADS,), name="gdn_scan", in_specs=[_per_head(sh, SCAN_HEADS) for sh in terms],
        out_specs=[pl.BlockSpec((nc * CHUNK, SCAN_HEADS * HD), lambda h: (0, h), pipeline_mode=ONE_BUFFER),
                   _per_head((nc, HD, HD), SCAN_HEADS)],
        out_shape=[SDS((nc * CHUNK, NG * HD), F32), SDS((NG, nc, HD, HD), F32)], compiler_params=_cp("parallel"),
    )(*terms_in)


def _gdn_bwd_scan(saved, do_raw):
    nc = saved[0].shape[1]
    terms = _gdn_term_shapes(nc)

    def body(*refs):
        intra, states, do, outs = refs[:6], refs[6], refs[7], refs[8:]

        def bwd(i, dss):
            c = nc - 1 - i
            rows = pl.ds(pl.multiple_of(c * CHUNK, CHUNK), CHUNK)
            new = []
            for hh in range(SCAN_HEADS):
                _, vjp = jax.vjp(_gdn_step, states[hh, c], *[r[hh, c] for r in intra])
                grads = vjp((do[rows, hh * HD:(hh + 1) * HD], dss[hh]))
                for r, gval in zip(outs, grads[1:]):
                    r[hh, c] = gval
                new.append(grads[0])
            return tuple(new)

        lax.fori_loop(0, nc, bwd, tuple(jnp.zeros((HD, HD), F32) for _ in range(SCAN_HEADS)))

    return pl.pallas_call(
        body, grid=(NG // SCAN_HEADS,), name="gdn_bwd_scan",
        in_specs=[_per_head(sh, SCAN_HEADS) for sh in terms] + [_per_head((nc, HD, HD), SCAN_HEADS)]
        + [pl.BlockSpec((nc * CHUNK, SCAN_HEADS * HD), lambda h: (0, h), pipeline_mode=ONE_BUFFER)],
        out_specs=[_per_head(sh, SCAN_HEADS) for sh in terms],
        out_shape=[SDS((NG,) + sh, F32) for sh in terms], compiler_params=_cp("parallel"),
    )(*saved, do_raw)


def _gdn_bwd(pa, pb, conv, alog, dtb, dterms):
    t = pa.shape[0]
    nc = t // CHUNK
    terms = _gdn_term_shapes(nc)

    def body(*refs):
        gq, gk, gv, sm, wq, wk, wv, al, db = refs[:9]
        dintra = refs[9:15]
        dgq, dgk, dgv, dsm, dwq, dwk, dwv, dal, ddb = refs[15:24]
        chunked, pads, dpads, dsm_s = refs[24:29], refs[29:32], refs[32:35], refs[35]
        h = pl.program_id(0)
        taps = _taps(wq, wk, wv)
        _gdn_pad((gq, gk, gv), pads)
        _gdn_stage(pads, sm, taps, al[...], db[...], h, chunked)
        grp_n = math.gcd(nc, GROUP)

        def grp(i, carry):
            sl = pl.ds(pl.multiple_of(i * grp_n, grp_n), grp_n)
            _, vjp = jax.vjp(_gdn_intra, *[r[sl] for r in chunked])
            for r, gval in zip(chunked, vjp(tuple(r[sl] for r in dintra))):
                r[sl] = gval
            return carry

        lax.fori_loop(0, nc // grp_n, grp, 0)

        rows = _prep_rows(t)
        per = rows // CHUNK
        for r in dpads:
            r[...] = jnp.zeros_like(r)

        def tile(i, small):
            r0 = pl.multiple_of(i * rows, rows)
            win = pl.ds(r0, rows + HALO)
            _, vjp = jax.vjp(lambda *a: _gdn_prep(*a, h), *[p[win, :] for p in pads], sm[pl.ds(r0, rows), :],
                             taps, al[...], db[...])
            grads = vjp(tuple(r[pl.ds(i * per, per)].reshape(rows, r.shape[-1]) for r in chunked))
            for r, gval in zip(dpads, grads[:3]):
                r[win, :] += gval
            dsm_s[pl.ds(r0, rows), :] = grads[3]
            return jax.tree.map(jnp.add, small, (grads[4], grads[5], grads[6]))

        zero = jnp.zeros((1, HD), F32)
        dtaps, g_al, g_db = lax.fori_loop(0, t // rows, tile, ((zero,) * 12, zero, zero))
        for r, dpad in zip((dgq, dgk, dgv), dpads):
            r[...] = dpad[HALO:, :].astype(r.dtype)
        for j, r in enumerate((dwq, dwk, dwv)):
            for k in range(4):
                r[k:k + 1, :] = dtaps[4 * j + k]

        @pl.when(h == 0)
        def _():
            for r in (dsm, dal, ddb):
                r[...] = jnp.zeros_like(r)

        dsm[...] += dsm_s[...]
        dal[...] += g_al
        ddb[...] += g_db

    head = _head(t)
    taps = pl.BlockSpec((4, HD), lambda h: (0, h))
    return pl.pallas_call(
        body, grid=(NG,), name="gdn_bwd", in_specs=_gdn_in_specs(t) + [_per_head(sh) for sh in terms],
        out_specs=[head, head, head, _small(t), taps, taps, taps, _small(1), _small(1)],
        out_shape=[SDS((t, NG * HD), BF16)] * 3 + [SDS((t, HD), F32)] + [SDS((4, NG * HD), F32)] * 3 + [SDS((1, HD), F32)] * 2,
        scratch_shapes=_gdn_chunked_scratch(nc) + [pltpu.VMEM((t + HALO, HD), F32)] * 6 + [pltpu.VMEM((t, HD), F32)],
        compiler_params=_cp("arbitrary"),
    )(pa, pa, pa, pb, conv, conv, conv, alog, dtb, *dterms)


def _gdn_post(o, z, gain):
    return (jnp.concatenate(
        [_rms(o[:, h * HD:(h + 1) * HD], gain) * _silu(z[:, h * HD:(h + 1) * HD]) for h in range(NG)], axis=1),)


def _place():
    return lax.axis_index("x"), lax.axis_index("y"), lax.axis_index("c")


def _all_gather(name, shard):
    def body(x_ref, out_ref, send_sems, recv_sems, local_sem):
        x, y, c = _place()
        me, sibling = (x, y, c), (x, y, 1 - c)
        chips = [(1 - x, y), (x, 1 - y), (1 - x, 1 - y)]

        def blk(px, py, pc):
            return out_ref.at[4 * px + 2 * py + pc]

        def copy(k, block, to, src=None):
            return pltpu.make_async_remote_copy(
                src_ref=blk(*block) if src is None else src, dst_ref=blk(*block),
                send_sem=send_sems.at[k], recv_sem=recv_sems.at[k], device_id=to, device_id_type=MESH)

        mine = pltpu.make_async_copy(x_ref, blk(*me), local_sem)
        mine.start()
        first = [copy(0, me, sibling, src=x_ref)]
        first += [copy(1 + j, me, (*chip, c), src=x_ref) for j, chip in enumerate(chips)]
        for cp in first:
            cp.start()
        passed = [copy(4 + j, (*chip, c), sibling) for j, chip in enumerate(chips)]
        for j, chip in enumerate(chips):
            copy(1 + j, (*chip, c), me).wait_recv()
            passed[j].start()
        copy(0, sibling, me).wait_recv()
        for j, chip in enumerate(chips):
            copy(4 + j, (*chip, 1 - c), me).wait_recv()
        for cp in first + passed:
            cp.wait_send()
        mine.wait()

    return pl.pallas_call(
        body, name=name, out_shape=SDS((N_DEV,) + shard.shape, shard.dtype),
        in_specs=[pl.BlockSpec(memory_space=pltpu.HBM)], out_specs=pl.BlockSpec(memory_space=pltpu.HBM),
        scratch_shapes=[pltpu.SemaphoreType.DMA((7,)), pltpu.SemaphoreType.DMA((7,)), pltpu.SemaphoreType.DMA],
    )(shard)


def _scatter_exchange(name, full):
    def body(g_ref, out_ref, send_sems, recv_sems, local_sem):
        x, y, c = _place()
        me = 4 * x + 2 * y + c
        mine = pltpu.make_async_copy(g_ref.at[me], out_ref.at[me], local_sem)
        mine.start()
        sends, recvs = [], []
        for k in range(1, N_DEV):
            px = 1 - x if k & 4 else x
            py = 1 - y if k & 2 else y
            pc = 1 - c if k & 1 else c
            peer = 4 * px + 2 * py + pc
            sends.append(pltpu.make_async_remote_copy(
                src_ref=g_ref.at[peer], dst_ref=out_ref.at[me], send_sem=send_sems.at[k - 1],
                recv_sem=recv_sems.at[k - 1], device_id=(px, py, pc), device_id_type=MESH))
            recvs.append(pltpu.make_async_remote_copy(
                src_ref=g_ref.at[me], dst_ref=out_ref.at[peer], send_sem=send_sems.at[k - 1],
                recv_sem=recv_sems.at[k - 1], device_id=(px, py, pc), device_id_type=MESH))
        for cp in sends:
            cp.start()
        for cp in recvs:
            cp.wait_recv()
        for cp in sends:
            cp.wait_send()
        mine.wait()

    return pl.pallas_call(
        body, name=name, out_shape=SDS(full.shape, full.dtype),
        in_specs=[pl.BlockSpec(memory_space=pltpu.HBM)], out_specs=pl.BlockSpec(memory_space=pltpu.HBM),
        scratch_shapes=[pltpu.SemaphoreType.DMA((7,)), pltpu.SemaphoreType.DMA((7,)), pltpu.SemaphoreType.DMA],
    )(full)


def _sum_blocks(name, parts):
    _, r, c = parts.shape
    tr = 64 if r % 64 == 0 else r

    def body(x, o):
        acc = x[0].astype(F32)
        for d in range(1, N_DEV):
            acc = acc + x[d].astype(F32)
        o[...] = acc

    return pl.pallas_call(
        body, grid=(r // tr,), name=name, in_specs=[pl.BlockSpec((N_DEV, tr, c), lambda i: (0, i, 0))],
        out_specs=pl.BlockSpec((tr, c), lambda i: (i, 0)), out_shape=SDS((r, c), F32), compiler_params=_cp("parallel"),
    )(parts)


def _reduce_scatter(name, full):
    return _sum_blocks(name + "_sum", _scatter_exchange(name, full))


def _all_reduce_small(name, x, reduce):
    m_per, n = x.shape

    def body(x_ref, out_ref, send_sems, recv_sems, local_sem):
        px, py, pc = _place()
        me, sibling = (px, py, pc), (px, py, 1 - pc)
        chips = [(1 - px, py), (px, 1 - py), (1 - px, 1 - py)]
        buf = out_ref

        def rows(qx, qy, qc):
            return buf.at[pl.ds((4 * qx + 2 * qy + qc) * m_per, m_per), :]

        def copy(k, block, to, src=None):
            return pltpu.make_async_remote_copy(
                src_ref=rows(*block) if src is None else src, dst_ref=rows(*block),
                send_sem=send_sems.at[k], recv_sem=recv_sems.at[k], device_id=to, device_id_type=MESH)

        mine = pltpu.make_async_copy(x_ref, rows(*me), local_sem)
        mine.start()
        first = [copy(0, me, sibling, src=x_ref)]
        first += [copy(1 + j, me, (*chip, pc), src=x_ref) for j, chip in enumerate(chips)]
        for cp in first:
            cp.start()
        passed = [copy(4 + j, (*chip, pc), sibling) for j, chip in enumerate(chips)]
        for j, chip in enumerate(chips):
            copy(1 + j, (*chip, pc), me).wait_recv()
            passed[j].start()
        copy(0, sibling, me).wait_recv()
        for j, chip in enumerate(chips):
            copy(4 + j, (*chip, 1 - pc), me).wait_recv()
        for cp in first + passed:
            cp.wait_send()
        mine.wait()

    gathered = pl.pallas_call(
        body, name=name, out_shape=SDS((N_DEV * m_per, n), x.dtype),
        in_specs=[pl.BlockSpec(memory_space=pltpu.VMEM)], out_specs=pl.BlockSpec(memory_space=pltpu.VMEM),
        scratch_shapes=[pltpu.SemaphoreType.DMA((7,)), pltpu.SemaphoreType.DMA((7,)), pltpu.SemaphoreType.DMA],
    )(x)
    if not reduce:
        return gathered
    return _sum_blocks(name + "_sum", gathered.reshape(N_DEV, m_per, n))


HBM_SPEC = pl.BlockSpec(memory_space=pltpu.HBM)
SEM_SPEC = pl.BlockSpec(memory_space=pltpu.SEMAPHORE)
EFFECT = pltpu.SideEffectType.DATAFLOW_SIDE_EFFECTING


def _copies_start(name, bufs, n_remote, n_local, build, deps):
    nb, nd = len(bufs), len(deps)
    sem_shapes = [pltpu.SemaphoreType.DMA((n_remote,)), pltpu.SemaphoreType.DMA((n_remote,))]
    if n_local:
        sem_shapes.append(pltpu.SemaphoreType.DMA((n_local,)))
    ns = len(sem_shapes)

    def body(*refs):
        sems = refs[nb + nd:nb + nd + ns]
        remote, local = build(refs[:nb], *sems, *([None] * (3 - ns)))
        for cp in local + remote:
            cp.start()
        refs[-1][...] = jnp.zeros((8, HD), F32)

    outs = pl.pallas_call(
        body, name=name,
        out_shape=(*sem_shapes, *[pltpu.HBM(b.shape, b.dtype) for b in bufs], SDS((8, HD), F32)),
        in_specs=[HBM_SPEC] * nb + [ANY_SPEC] * nd,
        out_specs=(*[SEM_SPEC] * ns, *[HBM_SPEC] * nb, pl.BlockSpec(memory_space=pltpu.VMEM)),
        input_output_aliases={i: ns + i for i in range(nb)},
        compiler_params=pltpu.CompilerParams(has_side_effects=EFFECT),
    )(*[pltpu.with_memory_space_constraint(b, pltpu.HBM) for b in bufs], *deps)
    return list(outs[:ns]), list(outs[ns:ns + nb]), outs[-1]


def _copies_wait(name, bufs, sems, build, after):
    nb, ns = len(bufs), len(sems)

    def body(*refs):
        remote, local = build(refs[:nb], *refs[nb:nb + ns], *([None] * (3 - ns)))
        for cp in local:
            cp.wait()
        for cp in remote:
            cp.wait_send()
            cp.wait_recv()

    outs = pl.pallas_call(
        body, name=name, out_shape=tuple(pltpu.HBM(b.shape, b.dtype) for b in bufs),
        in_specs=[HBM_SPEC] * nb + [SEM_SPEC] * ns + [ANY_SPEC] * len(after), out_specs=tuple([HBM_SPEC] * nb),
        input_output_aliases={i: i for i in range(nb)},
        compiler_params=pltpu.CompilerParams(has_side_effects=EFFECT),
    )(*bufs, *sems, *after)
    return list(outs)


def _remote(src, dst, send, recv, k, to):
    return pltpu.make_async_remote_copy(src_ref=src, dst_ref=dst, send_sem=send.at[k], recv_sem=recv.at[k],
                                        device_id=to, device_id_type=MESH)


class _Gather:
    def __init__(self, name, shards, deps):
        self.name, self.n = name, len(shards)
        lands = [lax.empty((N_DEV,) + s.shape, s.dtype) for s in shards]
        self.sems1, bufs, self.token = _copies_start(
            name + "_s1", list(shards) + lands, 4 * self.n, self.n, self._stage1(range(self.n)), deps)
        self.shards, self.lands, self.sems2 = bufs[:self.n], bufs[self.n:], {}

    def _stage1(self, idxs):
        def build(refs, send, recv, loc):
            x, y, c = _place()
            me = 4 * x + 2 * y + c
            targets = [(x, y, 1 - c), (1 - x, y, c), (x, 1 - y, c), (1 - x, 1 - y, c)]
            remote, local = [], []
            for pos, i in enumerate(idxs):
                src, land = refs[pos], refs[len(idxs) + pos]
                local.append(pltpu.make_async_copy(src, land.at[me], loc.at[i]))
                remote += [_remote(src, land.at[me], send, recv, 4 * i + k, to) for k, to in enumerate(targets)]
            return remote, local
        return build

    @staticmethod
    def _stage2(refs, send, recv, loc):
        x, y, c = _place()
        remote = []
        for pos, land in enumerate(refs):
            for j, (cx, cy) in enumerate([(1 - x, y), (x, 1 - y), (1 - x, 1 - y)]):
                blk = land.at[4 * cx + 2 * cy + c]
                remote.append(_remote(blk, blk, send, recv, 3 * pos + j, (x, y, 1 - c)))
        return remote, []

    def pass_on(self, idxs, after):
        tag, m = "".join(map(str, idxs)), len(idxs)
        bufs = _copies_wait(f"{self.name}_w1_{tag}", [self.shards[i] for i in idxs] + [self.lands[i] for i in idxs],
                            self.sems1, self._stage1(idxs), after)
        self.sems2[tag], lands, token = _copies_start(f"{self.name}_s2_{tag}", bufs[m:], 3 * m, 0, self._stage2, ())
        for pos, i in enumerate(idxs):
            self.lands[i] = lands[pos]
        return [token]

    def get(self, idxs, after):
        tag = "".join(map(str, idxs))
        return _copies_wait(f"{self.name}_w2_{tag}", [self.lands[i] for i in idxs], self.sems2[tag], self._stage2, after)


def _rows_tile(r, row_bytes, target=1 << 20):
    tr = r
    while tr % 32 == 0 and tr * row_bytes > target:
        tr //= 2
    return tr


def _pair_add(name, g, got, c):
    _, r, cols = g.shape
    tr = _rows_tile(r, cols * 2)

    def body(s, a, b, o):
        o[...] = (a[...].astype(F32) + b[...].astype(F32)).astype(o.dtype)

    return pl.pallas_call(
        body, name=name, out_shape=SDS((4, r, cols), g.dtype),
        grid_spec=pltpu.PrefetchScalarGridSpec(
            num_scalar_prefetch=1, grid=(4, r // tr),
            in_specs=[pl.BlockSpec((None, tr, cols), lambda j, i, s: (2 * j + s[0], i, 0)),
                      pl.BlockSpec((None, tr, cols), lambda j, i, s: (j, i, 0))],
            out_specs=pl.BlockSpec((None, tr, cols), lambda j, i, s: (j, i, 0))),
        compiler_params=_cp("parallel", "parallel"),
    )(c.reshape(1), g, got)


def _quad_sum(name, part, got, chip):
    _, r, cols = part.shape
    tr = _rows_tile(r, cols * 4)

    def body(s, a, b1, b2, b3, o):
        o[...] = ((a[...].astype(F32) + b1[...].astype(F32)) + b2[...].astype(F32)) + b3[...].astype(F32)

    blk = lambda k: pl.BlockSpec((None, tr, cols), lambda i, s, k=k: (jnp.bitwise_xor(s[0], k), i, 0))
    return pl.pallas_call(
        body, name=name, out_shape=SDS((r, cols), F32),
        grid_spec=pltpu.PrefetchScalarGridSpec(
            num_scalar_prefetch=1, grid=(r // tr,), in_specs=[blk(0), blk(1), blk(2), blk(3)],
            out_specs=pl.BlockSpec((tr, cols), lambda i, s: (i, 0))),
        compiler_params=_cp("parallel"),
    )(chip.reshape(1), part, got, got, got)


class _Scatter:
    def __init__(self, name, grads, deps):
        self.name, self.n = name, len(grads)
        got = [lax.empty((4,) + g.shape[1:], g.dtype) for g in grads]
        self.sems, bufs, self.token = _copies_start(name + "_s1", list(grads) + got, 4 * self.n, 0, self._stage1, deps)
        self.grads, self.got = bufs[:self.n], bufs[self.n:]

    def _stage1(self, refs, send, recv, loc):
        x, y, c = _place()
        remote = []
        for i in range(self.n):
            remote += [_remote(refs[i].at[2 * j + 1 - c], refs[self.n + i].at[j], send, recv, 4 * i + j, (x, y, 1 - c))
                       for j in range(4)]
        return remote, []

    def _stage2(self, refs, send, recv, loc):
        x, y, c = _place()
        remote = []
        for i in range(self.n):
            for k in (1, 2, 3):
                tx = 1 - x if k & 2 else x
                ty = 1 - y if k & 1 else y
                remote.append(_remote(refs[i].at[2 * tx + ty], refs[self.n + i].at[2 * x + y], send, recv,
                                      3 * i + k - 1, (tx, ty, c)))
        return remote, []

    def mid(self, after):
        bufs = _copies_wait(self.name + "_w1", self.grads + self.got, self.sems, self._stage1, after)
        c = lax.axis_index("c").astype(jnp.int32)
        parts = [_pair_add(f"{self.name}_add{i}", bufs[i], bufs[self.n + i], c) for i in range(self.n)]
        got = [lax.empty(p.shape, p.dtype) for p in parts]
        self.sems, bufs, self.token = _copies_start(self.name + "_s2", parts + got, 3 * self.n, 0, self._stage2, ())
        self.parts, self.got = bufs[:self.n], bufs[self.n:]

    def end(self, after):
        bufs = _copies_wait(self.name + "_w2", self.parts + self.got, self.sems, self._stage2, after)
        chip = (2 * lax.axis_index("x") + lax.axis_index("y")).astype(jnp.int32)
        return [_quad_sum(f"{self.name}_sum{i}", bufs[i], bufs[self.n + i], chip) for i in range(self.n)]


def _adamw(w, g, m, v):
    m = ADAM_B1 * m + (1.0 - ADAM_B1) * g
    v = ADAM_B2 * v + (1.0 - ADAM_B2) * (g * g)
    m_hat = m / (1.0 - ADAM_B1 ** ADAM_STEP)
    v_hat = v / (1.0 - ADAM_B2 ** ADAM_STEP)
    return -ADAM_LR * (m_hat / (jnp.sqrt(v_hat) + ADAM_EPS) + ADAM_WD * w), m, v


def _adamw_call(name, w, g, m, v):
    r, c = w.shape
    tm = 64 if r % 64 == 0 else r
    return _rowwise(name, _adamw, [w, g, m, v], [], [(c, F32)] * 3, tm)


_IN_COLS = 5906


def _perm_in(w):
    pad = jnp.zeros((w.shape[0], 2 * HALF - _IN_COLS), w.dtype)
    return (jnp.concatenate([w[:, 2310:4614], w[:, 4614:5382]], axis=1),
            jnp.concatenate([w[:, :2304], w[:, 5394:5906], w[:, 2304:2310], w[:, 5382:5394], pad], axis=1))


def _unperm_in(ga, gb):
    return jnp.concatenate([gb[:, :2304], gb[:, 2816:2822], ga[:, :2304], ga[:, 2304:3072], gb[:, 2822:2834],
                            gb[:, 2304:2816]], axis=1)


def _lanes(v, at):
    return jnp.pad(v, ((0, 0), (at, HD - at - v.shape[1])))


_PACK = ("norm_mix", "mem_norm", "norm_ffn", "gdn_conv", "fox_q_norm", "fox_k_norm", "gdn_out_norm", "mem_q_norm",
         "mem_k_norm", "fox_f_bias", "gdn_a_log", "gdn_dt_bias", "loss")


def _pack(vals):
    parts = [vals[n].reshape(-1, HD) for n in _PACK]
    used = sum(p.shape[0] for p in parts)
    buf = jnp.concatenate(parts + [jnp.zeros((-used % 8, HD), F32)], axis=0)
    return buf, [(n, p.shape[0]) for n, p in zip(_PACK, parts)]


def _unpack(buf, layout):
    out, at = {}, 0
    for n, rows in layout:
        out[n] = buf[at:at + rows]
        at += rows
    return out


def kernel(x, mem, norm_mix, w_in, fox_f_bias, fox_q_norm, fox_k_norm, gdn_conv, gdn_a_log, gdn_dt_bias, gdn_out_norm, mem_norm, w_mem_kv, mem_q_norm, mem_k_norm, w_out, norm_ffn, w_gate_up, w_down, loss_target, m_norm_mix, m_w_in, m_fox_f_bias, m_fox_q_norm, m_fox_k_norm, m_gdn_conv, m_gdn_a_log, m_gdn_dt_bias, m_gdn_out_norm, m_mem_norm, m_w_mem_kv, m_mem_q_norm, m_mem_k_norm, m_w_out, m_norm_ffn, m_w_gate_up, m_w_down, v_norm_mix, v_w_in, v_fox_f_bias, v_fox_q_norm, v_fox_k_norm, v_gdn_conv, v_gdn_a_log, v_gdn_dt_bias, v_gdn_out_norm, v_mem_norm, v_w_mem_kv, v_mem_q_norm, v_mem_k_norm, v_w_out, v_norm_ffn, v_w_gate_up, v_w_down):
    args = dict(locals())
    d = x.shape[2]
    me = 4 * lax.axis_index("x") + 2 * lax.axis_index("y") + lax.axis_index("c")

    cshard = gdn_conv[0].shape[1]
    conv_pad = jnp.pad(gdn_conv[0], ((0, 4), (0, 3 * HD - cshard)))
    conv_all = _all_reduce_small("ag_conv", conv_pad, False).reshape(N_DEV, 8, 3 * HD)[:, :4, :cshard]
    conv_all = conv_all.transpose(1, 0, 2).reshape(4, N_DEV * cshard)
    w_in_a, w_in_b = _perm_in(w_in[0])
    comm = _StepComm({"in_b": [w_in_b], "in_a": [w_in_a], "kv_out": [w_mem_kv[0], w_out[0]], "gate_up": [w_gate_up[0]],
                      "down": [w_down[0]]}, [conv_all])

    grad_x, loss_local, small_grads = _local_step(
        x[0], mem[0], loss_target[0], norm_mix, fox_f_bias, fox_q_norm, fox_k_norm, gdn_a_log, gdn_dt_bias,
        gdn_out_norm, mem_norm, mem_q_norm, mem_k_norm, norm_ffn, conv_all, comm)

    red = comm.finish([grad_x])
    grads = {"w_down": red["ffn"][0], "w_gate_up": red["ffn"][1], "w_out": red["a"][1], "w_mem_kv": red["b"][1],
             "w_in": _unperm_in(red["a"][0], red["b"][0])}
    small_grads["loss"] = jnp.broadcast_to(loss_local, (1, HD))
    packed, layout = _pack(small_grads)
    small = _unpack(_all_reduce_small("ar_small", packed, True), layout)
    loss = small["loss"][0, 0]
    six = {"fox_f_bias": L_FF, "gdn_a_log": L_GA, "gdn_dt_bias": L_GA}
    for n, rows_n in layout[:-1]:
        gsm = small[n]
        if n == "gdn_conv":
            gsm = lax.dynamic_slice(gsm.reshape(4, N_DEV * cshard), (0, me * cshard), (4, cshard))[None]
        elif n in six:
            gsm = gsm[:, six[n]:six[n] + 6]
        else:
            gsm = gsm.reshape(1, rows_n * HD)
        grads[n] = gsm

    names = ['norm_mix', 'w_in', 'fox_f_bias', 'fox_q_norm', 'fox_k_norm', 'gdn_conv', 'gdn_a_log', 'gdn_dt_bias',
             'gdn_out_norm', 'mem_norm', 'w_mem_kv', 'mem_q_norm', 'mem_k_norm', 'w_out', 'norm_ffn', 'w_gate_up', 'w_down']
    big = ("w_in", "w_mem_kv", "w_out", "w_gate_up", "w_down")
    delta, new_m, new_v = {}, {}, {}
    for n in big:
        delta[n], new_m[n], new_v[n] = [a[None] for a in _adamw_call(
            "adamw_" + n, args[n][0], grads[n], args["m_" + n][0], args["v_" + n][0])]
        grads[n] = grads[n][None]

    def flat(a):
        a = a.reshape(1, -1)
        return jnp.pad(a, ((0, 0), (0, -a.shape[1] % HD))).reshape(-1, HD)

    smalls = [n for n in names if n not in big]
    pk = lambda pre: jnp.concatenate([flat(grads[n] if pre == "g" else args[pre + n]) for n in smalls], axis=0)
    cat = [pk(""), pk("g"), pk("m_"), pk("v_")]
    padr = -cat[0].shape[0] % 8
    cat = [jnp.pad(a, ((0, padr), (0, 0))) for a in cat]
    res = _adamw_call("adamw_small", *cat)
    at = 0
    for n in smalls:
        shape = args[n].shape
        size = math.prod(shape)
        nrow = -(-size // HD)
        for dst, src in zip((delta, new_m, new_v), res):
            dst[n] = src[at:at + nrow].reshape(-1)[:size].reshape(shape)
        at += nrow

    return (loss, grad_x[None], *[grads[n] for n in names], *[delta[n] for n in names],
            *[new_m[n] for n in names], *[new_v[n] for n in names])


class _StepComm:
    def __init__(self, shard_groups, after):
        self.groups, shards = {}, []
        for key, ws in shard_groups.items():
            self.groups[key] = list(range(len(shards), len(shards) + len(ws)))
            shards += [w.astype(BF16) for w in ws]
        self.gather = _Gather("ag", shards, after)
        self.passed, self.scatters = set(), {}

    def start_deps(self):
        return [self.gather.token]

    def pass_on(self, key, after):
        self.passed.add(key)
        return self.gather.pass_on(self.groups[key], after)

    def weights(self, key, after):
        if key not in self.passed:
            after = self.pass_on(key, after)
        return self.gather.get(self.groups[key], after)

    def send(self, tag, grads):
        blocks = [g if g.ndim == 3 else g.reshape(N_DEV, g.shape[0] // N_DEV, g.shape[1]) for g in grads]
        self.scatters[tag] = _Scatter("rs_" + tag, blocks, ())
        return [self.scatters[tag].token]

    def mid(self, tag, after):
        self.scatters[tag].mid(after)
        return [self.scatters[tag].token]

    def finish(self, after):
        return {tag: sc.end(after) for tag, sc in self.scatters.items()}


def _local_step(xs, ms, tgt, norm_mix, fox_f_bias, fox_q_norm, fox_k_norm, gdn_a_log, gdn_dt_bias, gdn_out_norm,
                mem_norm, mem_q_norm, mem_k_norm, norm_ffn, conv_all, comm):
    t, d = xs.shape
    bq = min(t, 256)
    fb, alog, dtb = _lanes(fox_f_bias, L_FF), _lanes(gdn_a_log, L_GA), _lanes(gdn_dt_bias, L_GA)
    flat = lambda w: w.reshape(-1, w.shape[-1])

    rms1 = lambda a, g: (_rms(a, g),)
    (u,) = _rowwise("norm_mix", rms1, [xs], [norm_mix], [(d, BF16)], min(t, 256), deps=comm.start_deps())
    w_in_b = flat(comm.weights("in_b", [u])[0])
    pb = _matmul("proj_in_b", u, w_in_b, NN, F32, 1024, 768)
    o_fox = _fox_fwd(pb, fb, fox_q_norm, fox_k_norm, bq)
    w_in_a = flat(comm.weights("in_a", [o_fox])[0])
    pa = _matmul("proj_in_a", u, w_in_a, NN, F32, 1024, 768)
    gdn_terms = _gdn_fwd(pa, pb, conv_all, alog, dtb)
    o_gdn_raw, gdn_states = _gdn_scan(gdn_terms)
    gdn_saved = list(gdn_terms) + [gdn_states]
    zrow = (pa, NG * HD, GZ * HD // (NG * HD))
    (o_gdn,) = _rowwise("gdn_post", _gdn_post, [o_gdn_raw, zrow], [gdn_out_norm], [(NG * HD, BF16)], min(t, 256))
    w_kv_all, w_out_all = [flat(w) for w in comm.weights("kv_out", [o_gdn])]
    (mem_n,) = _rowwise("norm_mem", rms1, [ms], [mem_norm], [(d, BF16)], ms.shape[0])
    mkv = _matmul("proj_mem", mem_n, w_kv_all, NN, F32, 256, 512)
    o_mem = _mem_fwd(pb, mkv, mem_q_norm, mem_k_norm)
    deps = comm.pass_on("gate_up", [o_mem])
    mix = jnp.concatenate([o_fox, o_gdn, o_mem], axis=1)
    h1 = _matmul("proj_out", mix, w_out_all, NN, F32, 1024, 1024, residual=xs, deps=deps)
    (h1n,) = _rowwise("norm_ffn", rms1, [h1], [norm_ffn], [(d, BF16)], min(t, 256))
    (wgu,) = comm.weights("gate_up", [h1n])
    ffw = wgu.shape[2]
    gu, act = _ffn_up(h1n, wgu.reshape(2, 4, d, ffw))
    w_down_all = flat(comm.weights("down", [act])[0])
    dy, dyb, lsum = _ffn_down_loss(act, w_down_all, h1, tgt)
    loss_local = (0.5 / d) * jnp.sum(lsum[::8, ::HD])

    dgu = _ffn_down_bwd(dyb, w_down_all.reshape(4, ffw, d), gu).reshape(8, t, ffw)
    g_w_down = _matmul("grad_w_down", act, dyb, TN, BF16, 512, 2048)
    dh1n = _ffn_up_bwd_x(dgu, wgu)
    g_w_gu = _ffn_up_bwd_w(h1n, dgu)
    deps = comm.send("ffn", [g_w_down, g_w_gu])
    rms2 = lambda a, g: (_rms(a, g), a)
    dh1, g_norm_ffn = _rowwise_vjp("norm_ffn_bwd", rms2, [h1], [norm_ffn], [dh1n, dy], [F32], min(t, 256), deps=deps)
    dh1b = dh1.astype(BF16)

    dmix = _matmul("proj_out_bwd_x", dh1b, w_out_all, NT, F32, 1024, 1024)
    g_w_out = _matmul("grad_w_out", mix, dh1b, TN, BF16, 1024, 2048)
    deps = comm.mid("ffn", [dmix, g_w_out])
    do_raw, dgz, g_gon = _rowwise_vjp("gdn_post_bwd", _gdn_post, [o_gdn_raw, zrow], [gdn_out_norm],
                                      [(dmix, NG * HD, 1)], [F32, BF16], min(t, 256), deps=deps)
    dterms = _gdn_bwd_scan(gdn_saved, do_raw)
    dgq, dgk, dgv, dsm_gdn, dwq, dwk, dwv, g_alog, g_dtb = _gdn_bwd(pa, pb, conv_all, alog, dtb, dterms)
    dp_a = jnp.concatenate([dgq, dgk, dgv, dgz], axis=1)
    g_w_in_a = _matmul("grad_w_in_a", u, dp_a, TN, BF16, 512, 3072)
    deps = comm.send("a", [g_w_in_a, g_w_out])
    dmq, dmk, dmv, g_mqn, g_mkn = _mem_bwd(pb, mkv, mem_q_norm, mem_k_norm, dmix, deps=deps)
    dmkv = jnp.concatenate([dmk, dmv], axis=1).astype(BF16)
    dmem_n = _matmul("proj_mem_bwd_x", dmkv, w_kv_all, NT, F32, 256, 512)
    g_w_kv = _matmul("grad_w_kv", mem_n, dmkv, TN, BF16, 512, 512)
    g_mem_norm = _rowwise_vjp("norm_mem_bwd", rms1, [ms], [mem_norm], [dmem_n], [], ms.shape[0])[0]
    deps = comm.mid("a", [g_mem_norm, g_w_kv])
    dfq, dfk, dfv, dsm_fox, g_fb, g_fqn, g_fkn = _fox_bwd(pb, fb, fox_q_norm, fox_k_norm, dmix, bq, deps=deps)
    dp_b = jnp.concatenate([dfq, dfk, dfv, dmq, (dsm_fox + dsm_gdn).astype(BF16), jnp.zeros((t, HD), BF16)], axis=1)
    g_w_in_b = _matmul("grad_w_in_b", u, dp_b, TN, BF16, 512, 3072)
    deps = comm.send("b", [g_w_in_b, g_w_kv])
    du_a = _matmul("proj_in_bwd_a", dp_a, w_in_a, NT, F32, 1024, 1024, deps=deps)
    deps = comm.mid("b", [du_a])
    du = _matmul("proj_in_bwd_b", dp_b, w_in_b, NT, F32, 1024, 1024, residual=du_a, deps=deps)
    grad_x, g_norm_mix = _rowwise_vjp("norm_mix_bwd", rms2, [xs], [norm_mix], [du, dh1], [F32], min(t, 256))

    small_grads = {
        "norm_mix": g_norm_mix, "mem_norm": g_mem_norm, "norm_ffn": g_norm_ffn,
        "gdn_conv": jnp.concatenate([dwq, dwk, dwv], axis=1),
        "fox_q_norm": g_fqn, "fox_k_norm": g_fkn, "gdn_out_norm": g_gon, "mem_q_norm": g_mqn, "mem_k_norm": g_mkn,
        "fox_f_bias": g_fb, "gdn_a_log": g_alog, "gdn_dt_bias": g_dtb}
    return grad_x, loss_local, small_grads
```

```python
import functools
import math

import jax
import jax.numpy as jnp
from jax import lax
from jax.experimental import pallas as pl
from jax.experimental.pallas import tpu as pltpu

F32 = jnp.float32
BF16 = jnp.bfloat16
HI = lax.Precision.HIGHEST
SDS = jax.ShapeDtypeStruct

N_DEV = 8
HD = 128
NF, NG, NM = 6, 6, 4
CHUNK = 64
GROUP = 16
NORM_EPS = 1e-6
GQ, GK, GV, GZ = 0, 6, 12, 18
FQ, FK, FV, MQ, SM = 0, 6, 12, 18, 22
HALF = 24 * HD
L_FF, L_GA, L_GB = 0, 6, 12
VMEM_LIMIT = 56 * 1024 * 1024

ADAM_LR, ADAM_B1, ADAM_B2, ADAM_EPS, ADAM_WD, ADAM_STEP = 0.001, 0.9, 0.999, 1e-08, 0.01, 10

NN = (((1,), (0,)), ((), ()))
NT = (((1,), (1,)), ((), ()))
TN = (((0,), (0,)), ((), ()))
MESH = pl.DeviceIdType.MESH


def _cp(*sem):
    return pltpu.CompilerParams(dimension_semantics=tuple(sem) if sem else None, vmem_limit_bytes=VMEM_LIMIT)


def _dot(a, b, dims=NN):
    return lax.dot_general(a, b, dims, preferred_element_type=F32)


def _bdot(a, b):
    return _dot(a.astype(BF16), b.astype(BF16))


def _iota(shape, axis):
    return lax.broadcasted_iota(jnp.int32, shape, axis)


def _rms(x, gain):
    return x * lax.rsqrt(jnp.mean(x * x, axis=-1, keepdims=True) + NORM_EPS) * gain


def _sigmoid(x):
    return 0.5 * jnp.tanh(0.5 * x) + 0.5


def _silu(x):
    return x * _sigmoid(x)


def _softplus(x):
    return jnp.maximum(x, 0.0) + jnp.log(1.0 + jnp.exp(-jnp.abs(x)))


def _lane_pick(x, lane):
    oh = (_iota((1, x.shape[-1]), 1) == lane).astype(F32)
    return jnp.sum(x * oh, axis=-1, keepdims=True)


def _cumsum_rows(x):
    tril = (_iota((HD, HD), 0) >= _iota((HD, HD), 1)).astype(F32)
    carry = jnp.zeros((1, x.shape[1]), F32)
    outs = []
    for b in range(x.shape[0] // HD):
        blk = x[b * HD:(b + 1) * HD]
        outs.append(jnp.dot(tril, blk, precision=HI, preferred_element_type=F32) + carry)
        carry = carry + jnp.sum(blk, axis=0, keepdims=True)
    return jnp.concatenate(outs, axis=0)


def _row_spec(r, tm):
    if isinstance(r, tuple):
        arr, width, cb = r
        return arr, pl.BlockSpec((tm, width), lambda i, cb=cb: (i, cb))
    return r, pl.BlockSpec((tm, r.shape[1]), lambda i: (i, 0))


ANY_SPEC = pl.BlockSpec(memory_space=pl.ANY)


def _rowwise(name, fn, rows, consts, outs, tm, deps=()):
    arrs, specs = zip(*[_row_spec(r, tm) for r in rows])
    n_rows = arrs[0].shape[0]
    nr, nc, nd = len(rows), len(consts), len(deps)

    def body(*refs):
        res = fn(*[r[...] for r in refs[:nr + nc]])
        for o, v in zip(refs[nr + nc + nd:], res):
            o[...] = v.astype(o.dtype)

    return pl.pallas_call(
        body, grid=(n_rows // tm,), name=name,
        in_specs=list(specs) + [pl.BlockSpec(c.shape, lambda i: (0, 0)) for c in consts] + [ANY_SPEC] * nd,
        out_specs=[pl.BlockSpec((tm, w), lambda i: (i, 0)) for w, _ in outs],
        out_shape=[SDS((n_rows, w), dt) for w, dt in outs],
        compiler_params=_cp("parallel"),
    )(*arrs, *consts, *deps)


def _rowwise_vjp(name, fn, rows, consts, cts, grad_dtypes, tm, deps=()):
    arrs, specs = zip(*[_row_spec(r, tm) for r in rows])
    ct_arrs, ct_specs = zip(*[_row_spec(r, tm) for r in cts])
    n_rows = arrs[0].shape[0]
    nr, nc, nct, nd = len(rows), len(consts), len(cts), len(deps)
    plan = [(j, dt) for j, dts in enumerate(grad_dtypes) for dt in (dts if isinstance(dts, tuple) else (dts,))]
    ng = len(plan)
    widths = [specs[j].block_shape[1] for j, _ in plan]
    grad_dtypes = [dt for _, dt in plan]

    def body(*refs):
        vals = [r[...].astype(F32) for r in refs[:nr + nc]]
        ctv = tuple(r[...].astype(F32) for r in refs[nr + nc:nr + nc + nct])
        _, vjp = jax.vjp(fn, *vals)
        grads = vjp(ctv)
        outs = refs[nr + nc + nct + nd:]
        for o, (j, _) in zip(outs[:ng], plan):
            o[...] = grads[j].astype(o.dtype)

        @pl.when(pl.program_id(0) == 0)
        def _():
            for o in outs[ng:]:
                o[...] = jnp.zeros_like(o)

        for o, g in zip(outs[ng:], grads[nr:]):
            o[...] += g

    return pl.pallas_call(
        body, grid=(n_rows // tm,), name=name,
        in_specs=list(specs) + [pl.BlockSpec(c.shape, lambda i: (0, 0)) for c in consts] + list(ct_specs)
        + [ANY_SPEC] * nd,
        out_specs=[pl.BlockSpec((tm, w), lambda i: (i, 0)) for w in widths]
        + [pl.BlockSpec(c.shape, lambda i: (0, 0)) for c in consts],
        out_shape=[SDS((n_rows, w), dt) for w, dt in zip(widths, grad_dtypes)] + [SDS(c.shape, F32) for c in consts],
        compiler_params=_cp("arbitrary"),
    )(*arrs, *consts, *ct_arrs, *deps)


def _tile(n, pref):
    t = min(n, pref)
    while n % t or (t % HD and t != n):
        t -= 1
    return t


def _matmul(name, a, b, dims, out_dtype, tm, tn, residual=None, deps=()):
    ta, tb = dims == TN, dims == NT
    m = a.shape[1] if ta else a.shape[0]
    k = a.shape[0] if ta else a.shape[1]
    n = b.shape[0] if tb else b.shape[1]
    tm, tn = _tile(m, tm), _tile(n, tn)

    def body(*refs):
        acc = _dot(refs[0][...], refs[1][...], dims)
        if residual is not None:
            acc = acc + refs[2][...]
        refs[-1][...] = acc.astype(out_dtype)

    in_specs = [pl.BlockSpec((k, tm), lambda i, j: (0, i)) if ta else pl.BlockSpec((tm, k), lambda i, j: (i, 0)),
                pl.BlockSpec((tn, k), lambda i, j: (j, 0)) if tb else pl.BlockSpec((k, tn), lambda i, j: (0, j))]
    ops = [a, b]
    if residual is not None:
        in_specs.append(pl.BlockSpec((tm, tn), lambda i, j: (i, j)))
        ops.append(residual)
    in_specs += [ANY_SPEC] * len(deps)
    ops += list(deps)
    return pl.pallas_call(
        body, grid=(m // tm, n // tn), name=name, in_specs=in_specs,
        out_specs=pl.BlockSpec((tm, tn), lambda i, j: (i, j)), out_shape=SDS((m, n), out_dtype),
        compiler_params=_cp("parallel", "parallel"),
    )(*ops)


def _proj_out_norm(mix, w_out, xs, gain, deps):
    t, k = mix.shape
    d = w_out.shape[1]
    tm = _tile(t, 512)

    def body(*refs):
        a, b, x, g = refs[:4]
        h1, h1n = refs[4 + len(deps):]
        acc = _dot(a[...], b[...]) + x[...]
        h1[...] = acc
        h1n[...] = _rms(acc, g[...]).astype(BF16)

    return pl.pallas_call(
        body, grid=(t // tm,), name="proj_out",
        in_specs=[pl.BlockSpec((tm, k), lambda i: (i, 0)), pl.BlockSpec((k, d), lambda i: (0, 0)),
                  pl.BlockSpec((tm, d), lambda i: (i, 0)), pl.BlockSpec((1, d), lambda i: (0, 0))] + [ANY_SPEC] * len(deps),
        out_specs=[pl.BlockSpec((tm, d), lambda i: (i, 0))] * 2, out_shape=[SDS((t, d), F32), SDS((t, d), BF16)],
        compiler_params=_cp("parallel"),
    )(mix, w_out, xs, gain, *deps)


def _ffn_up(h1n, wgu):
    t, d = h1n.shape
    w = wgu.shape[3]
    tm = _tile(t, 512)

    def body(a, b, gu, act):
        x = a[...]
        g = _dot(x, b[0])
        u = _dot(x, b[1])
        gu[0] = g.astype(BF16)
        gu[1] = u.astype(BF16)
        act[...] = (_silu(g) * u).astype(BF16)

    return pl.pallas_call(
        body, grid=(4, t // tm), name="ffn_up",
        in_specs=[pl.BlockSpec((tm, d), lambda j, i: (i, 0)), pl.BlockSpec((2, None, d, w), lambda j, i: (0, j, 0, 0))],
        out_specs=[pl.BlockSpec((2, None, tm, w), lambda j, i: (0, j, i, 0)), pl.BlockSpec((tm, w), lambda j, i: (i, j))],
        out_shape=[SDS((2, 4, t, w), BF16), SDS((t, 4 * w), BF16)],
        compiler_params=_cp("parallel", "parallel"),
    )(h1n, wgu)


def _ffn_down_loss(act, wdown, h1, target):
    t, f = act.shape
    d = wdown.shape[1]
    tm, tn = _tile(t, 1024), _tile(d, 512)

    def body(a, b, h, tg, dy, dyb, ls):
        e = _dot(a[...], b[...]) + h[...] - tg[...]
        g = e * (1.0 / d)
        dy[...] = g
        dyb[...] = g.astype(BF16)
        ls[...] = jnp.broadcast_to(jnp.sum(e * e), (8, HD))

    return pl.pallas_call(
        body, grid=(t // tm, d // tn), name="ffn_down_loss",
        in_specs=[pl.BlockSpec((tm, f), lambda i, j: (i, 0)), pl.BlockSpec((f, tn), lambda i, j: (0, j)),
                  pl.BlockSpec((tm, tn), lambda i, j: (i, j)), pl.BlockSpec((tm, tn), lambda i, j: (i, j))],
        out_specs=[pl.BlockSpec((tm, tn), lambda i, j: (i, j)), pl.BlockSpec((tm, tn), lambda i, j: (i, j)),
                   pl.BlockSpec((8, HD), lambda i, j: (i, j))],
        out_shape=[SDS((t, d), F32), SDS((t, d), BF16), SDS((8 * (t // tm), HD * (d // tn)), F32)],
        compiler_params=_cp("parallel", "parallel"),
    )(act, wdown, h1, target)


def _ffn_down_bwd(dyb, wdown4, gu):
    t, d = dyb.shape
    w = wdown4.shape[1]
    tm = _tile(t, 512)

    def body(a, b, gu_ref, out):
        da = _dot(a[...], b[...], NT)
        g = gu_ref[0].astype(F32)
        u = gu_ref[1].astype(F32)
        s = _sigmoid(g)
        out[0] = (da * u * (s * (1.0 + g * (1.0 - s)))).astype(BF16)
        out[1] = (da * g * s).astype(BF16)

    return pl.pallas_call(
        body, grid=(4, t // tm), name="ffn_down_bwd",
        in_specs=[pl.BlockSpec((tm, d), lambda j, i: (i, 0)), pl.BlockSpec((None, w, d), lambda j, i: (j, 0, 0)),
                  pl.BlockSpec((2, None, tm, w), lambda j, i: (0, j, i, 0))],
        out_specs=pl.BlockSpec((2, None, tm, w), lambda j, i: (0, j, i, 0)),
        out_shape=SDS((2, 4, t, w), BF16),
        compiler_params=_cp("parallel", "parallel"),
    )(dyb, wdown4, gu)


def _ffn_up_bwd_x(dgu, wgu):
    _, t, w = dgu.shape
    d = wgu.shape[1]
    tm = _tile(t, 512)

    def body(a, b, out):
        @pl.when(pl.program_id(1) == 0)
        def _():
            out[...] = jnp.zeros_like(out)
        out[...] += _dot(a[...], b[...], NT)

    return pl.pallas_call(
        body, grid=(t // tm, 8), name="ffn_up_bwd_x",
        in_specs=[pl.BlockSpec((None, tm, w), lambda i, j: (j, i, 0)), pl.BlockSpec((None, d, w), lambda i, j: (j, 0, 0))],
        out_specs=pl.BlockSpec((tm, d), lambda i, j: (i, 0)), out_shape=SDS((t, d), F32),
        compiler_params=_cp("parallel", "arbitrary"),
    )(dgu, wgu)


def _ffn_up_bwd_w(h1n, dgu):
    _, t, w = dgu.shape
    d = h1n.shape[1]
    tm = _tile(d, 512)

    def body(a, b, out):
        out[...] = _dot(a[...], b[...], TN).astype(BF16)

    return pl.pallas_call(
        body, grid=(8, d // tm), name="ffn_up_bwd_w",
        in_specs=[pl.BlockSpec((t, tm), lambda j, i: (0, i)), pl.BlockSpec((None, t, w), lambda j, i: (j, 0, 0))],
        out_specs=pl.BlockSpec((None, tm, w), lambda j, i: (j, i, 0)), out_shape=SDS((8, d, w), BF16),
        compiler_params=_cp("parallel", "parallel"),
    )(h1n, dgu)


def _fox_prep(fq, fk, sm, fb, qg, kg, h):
    qn = _rms(fq, qg)
    kn = _rms(fk, kg)
    c = _cumsum_rows(-_softplus(-(sm + fb)))
    ccol = _lane_pick(c, L_FF + h)
    crow = jnp.sum(c.T * (_iota((HD, 1), 0) == L_FF + h).astype(F32), axis=0, keepdims=True)
    return qn, kn, ccol, crow


def _fox_block(q, k, v, cc, cr, off):
    bq = q.shape[0]
    assert k.shape[0] == off + bq
    s = _dot(q.astype(BF16), k.astype(BF16), NT) * (HD ** -0.5) + cc - cr
    diag = jnp.where(_iota((bq, bq), 1) <= _iota((bq, bq), 0), s[:, off:], -1e30)
    s = jnp.concatenate([s[:, :off], diag], axis=1) if off else diag
    e = jnp.exp(s - lax.stop_gradient(jnp.max(s, axis=1, keepdims=True)))
    p = e / jnp.sum(e, axis=1, keepdims=True)
    return _dot(p.astype(BF16), v.astype(BF16))


ONE_BUFFER = pl.Buffered(1)


def _pcol(t, cb):
    return pl.BlockSpec((t, HD), lambda h, cb=cb: (0, cb + h), pipeline_mode=ONE_BUFFER)


def _smcol(t):
    return pl.BlockSpec((t, HD), lambda h: (0, SM), pipeline_mode=ONE_BUFFER)


def _head(t):
    return pl.BlockSpec((t, HD), lambda h: (0, h), pipeline_mode=ONE_BUFFER)


def _small(n):
    return pl.BlockSpec((n, HD), lambda h: (0, 0), pipeline_mode=ONE_BUFFER)


def _fox_fwd(p, fb, qg, kg, bq):
    t = p.shape[0]

    def body(fq, fk, fv, sm, fb_r, qg_r, kg_r, o, qn_s, cc_s):
        h = pl.program_id(0)
        qn, kn, ccol, crow = _fox_prep(fq[...], fk[...], sm[...], fb_r[...], qg_r[...], kg_r[...], h)
        qn_s[...] = qn
        cc_s[...] = ccol
        knb = kn.astype(BF16)
        vb = fv[...].astype(BF16)
        for i in range(t // bq):
            rows, ext = pl.ds(i * bq, bq), (i + 1) * bq
            o[rows, :] = _fox_block(qn_s[rows, :], knb[:ext], vb[:ext], cc_s[rows, :], crow[:, :ext], i * bq).astype(o.dtype)

    return pl.pallas_call(
        body, grid=(NF,), name="fox_fwd",
        in_specs=[_pcol(t, FQ), _pcol(t, FK), _pcol(t, FV), _smcol(t), _small(1), _small(1), _small(1)],
        out_specs=_head(t), out_shape=SDS((t, NF * HD), BF16),
        scratch_shapes=[pltpu.VMEM((t, HD), F32), pltpu.VMEM((t, 1), F32)],
        compiler_params=_cp("parallel"),
    )(p, p, p, p, fb, qg, kg)


def _fox_bwd(p, fb, qg, kg, dmix, bq, deps=()):
    t = p.shape[0]

    def body(*refs):
        fq, fk, fv, sm, fb_r, qg_r, kg_r, do = refs[:8]
        dfq, dfk, dfv, dsm, dfb, dqg, dkg, qn_s, cc_s, dqn_s, dcc_s, dkn_s, dv_s, dcr_s = refs[8 + len(deps):]
        h = pl.program_id(0)
        qn, kn, ccol, crow = _fox_prep(fq[...], fk[...], sm[...], fb_r[...], qg_r[...], kg_r[...], h)
        qn_s[...] = qn
        cc_s[...] = ccol
        v = fv[...]
        dkn_s[...] = jnp.zeros_like(dkn_s)
        dv_s[...] = jnp.zeros_like(dv_s)
        dcr_s[...] = jnp.zeros_like(dcr_s)

        for i in range(t // bq):
            rows, ext = pl.ds(i * bq, bq), (i + 1) * bq
            _, vjp = jax.vjp(lambda a, b, c, d, e, off=i * bq: _fox_block(a, b, c, d, e, off),
                             qn_s[rows, :], kn[:ext], v[:ext], cc_s[rows, :], crow[:, :ext])
            dq, dk, dv, dcc, dcr = vjp(do[rows, :])
            dqn_s[rows, :] = dq
            dcc_s[rows, :] = dcc
            dkn_s[:ext, :] += dk
            dv_s[:ext, :] += dv
            dcr_s[:, :ext] += dcr
        _, prep_vjp = jax.vjp(lambda a, b, c, d, e, f: _fox_prep(a, b, c, d, e, f, h),
                              fq[...], fk[...], sm[...], fb_r[...], qg_r[...], kg_r[...])
        g_fq, g_fk, g_sm, g_fb, g_qg, g_kg = prep_vjp((dqn_s[...], dkn_s[...], dcc_s[...], dcr_s[...]))
        dfq[...] = g_fq.astype(dfq.dtype)
        dfk[...] = g_fk.astype(dfk.dtype)
        dfv[...] = dv_s[...].astype(dfv.dtype)

        @pl.when(h == 0)
        def _():
            for r in (dsm, dfb, dqg, dkg):
                r[...] = jnp.zeros_like(r)

        dsm[...] += g_sm
        dfb[...] += g_fb
        dqg[...] += g_qg
        dkg[...] += g_kg

    head = _head(t)
    return pl.pallas_call(
        body, grid=(NF,), name="fox_bwd",
        in_specs=[_pcol(t, FQ), _pcol(t, FK), _pcol(t, FV), _smcol(t), _small(1), _small(1), _small(1), head]
        + [ANY_SPEC] * len(deps),
        out_specs=[head, head, head, _small(t), _small(1), _small(1), _small(1)],
        out_shape=[SDS((t, NF * HD), BF16)] * 3 + [SDS((t, HD), F32)] + [SDS((1, HD), F32)] * 3,
        scratch_shapes=[pltpu.VMEM((t, HD), F32), pltpu.VMEM((t, 1), F32), pltpu.VMEM((t, HD), F32),
                        pltpu.VMEM((t, 1), F32), pltpu.VMEM((t, HD), F32), pltpu.VMEM((t, HD), F32),
                        pltpu.VMEM((1, t), F32)],
        compiler_params=_cp("arbitrary"),
    )(p, p, p, p, fb, qg, kg, dmix, *deps)


def _mem_attn(mq, mk, mv, qg, kg):
    s = _dot(_rms(mq, qg).astype(BF16), _rms(mk, kg).astype(BF16), NT) * (HD ** -0.5)
    e = jnp.exp(s - lax.stop_gradient(jnp.max(s, axis=1, keepdims=True)))
    p = e / jnp.sum(e, axis=1, keepdims=True)
    return _dot(p.astype(BF16), mv.astype(BF16))


def _mem_fwd(p, mkv, qg, kg):
    t, ml = p.shape[0], mkv.shape[0]

    def body(mq, mk, mv, qg_r, kg_r, o):
        o[...] = _mem_attn(mq[...], mk[...], mv[...], qg_r[...], kg_r[...]).astype(o.dtype)

    return pl.pallas_call(
        body, grid=(NM,), name="mem_fwd",
        in_specs=[_pcol(t, MQ), pl.BlockSpec((ml, HD), lambda h: (0, h)), pl.BlockSpec((ml, HD), lambda h: (0, NM + h)),
                  _small(1), _small(1)],
        out_specs=pl.BlockSpec((t, HD), lambda h: (0, h)), out_shape=SDS((t, NM * HD), BF16),
        compiler_params=_cp("parallel"),
    )(p, mkv, mkv, qg, kg)


def _mem_bwd(p, mkv, qg, kg, dmix, deps=()):
    t, ml = p.shape[0], mkv.shape[0]

    def body(*refs):
        mq, mk, mv, qg_r, kg_r, do = refs[:6]
        dmq, dmk, dmv, dqg, dkg = refs[6 + len(deps):]
        _, vjp = jax.vjp(_mem_attn, mq[...], mk[...], mv[...], qg_r[...], kg_r[...])
        g_q, g_k, g_v, g_qg, g_kg = vjp(do[...])
        dmq[...] = g_q.astype(dmq.dtype)
        dmk[...] = g_k
        dmv[...] = g_v

        @pl.when(pl.program_id(0) == 0)
        def _():
            dqg[...] = jnp.zeros_like(dqg)
            dkg[...] = jnp.zeros_like(dkg)

        dqg[...] += g_qg
        dkg[...] += g_kg

    return pl.pallas_call(
        body, grid=(NM,), name="mem_bwd",
        in_specs=[_pcol(t, MQ), pl.BlockSpec((ml, HD), lambda h: (0, h)), pl.BlockSpec((ml, HD), lambda h: (0, NM + h)),
                  _small(1), _small(1), pl.BlockSpec((t, HD), lambda h: (0, NF + NG + h))] + [ANY_SPEC] * len(deps),
        out_specs=[pl.BlockSpec((t, HD), lambda h: (0, h)), pl.BlockSpec((ml, HD), lambda h: (0, h)),
                   pl.BlockSpec((ml, HD), lambda h: (0, h)), _small(1), _small(1)],
        out_shape=[SDS((t, NM * HD), BF16), SDS((ml, NM * HD), F32), SDS((ml, NM * HD), F32),
                   SDS((1, HD), F32), SDS((1, HD), F32)],
        compiler_params=_cp("arbitrary"),
    )(p, mkv, mkv, qg, kg, dmix, *deps)


def _shift_down(x, s):
    if s == 0:
        return x
    return jnp.where(_iota(x.shape, 0) >= s, pltpu.roll(x, s, 0), 0.0)


def _shift_up(x, s):
    if s == 0:
        return x
    n = x.shape[0]
    return jnp.where(_iota(x.shape, 0) < n - s, pltpu.roll(x, n - s, 0), 0.0)


@jax.custom_vjp
def _conv4(x, w0, w1, w2, w3):
    return w0 * _shift_down(x, 3) + w1 * _shift_down(x, 2) + w2 * _shift_down(x, 1) + w3 * x


def _conv4_fwd(x, w0, w1, w2, w3):
    return _conv4(x, w0, w1, w2, w3), (x, w0, w1, w2, w3)


def _conv4_bwd(res, dy):
    x, w0, w1, w2, w3 = res
    dx = w0 * _shift_up(dy, 3) + w1 * _shift_up(dy, 2) + w2 * _shift_up(dy, 1) + w3 * dy
    dws = tuple(jnp.sum(dy * _shift_down(x, 3 - k), axis=0, keepdims=True) for k in range(4))
    return (dx,) + dws


_conv4.defvjp(_conv4_fwd, _conv4_bwd)


HALO = 8


def _gdn_prep(gq, gk, gv, sm, taps, alog, dtb, h):
    q, k, v = [_silu(_conv4(x, *taps[4 * j:4 * j + 4]))[HALO:] for j, x in enumerate((gq, gk, gv))]
    q = q * lax.rsqrt(jnp.sum(q * q, axis=-1, keepdims=True) + NORM_EPS) * (HD ** -0.5)
    k = k * lax.rsqrt(jnp.sum(k * k, axis=-1, keepdims=True) + NORM_EPS)
    g = _lane_pick(-jnp.exp(alog) * _softplus(sm + dtb), L_GA + h)
    beta = _lane_pick(_sigmoid(sm), L_GB + h)
    return q, k, v, g, beta


def _split(x, n):
    parts, rest = [], x
    for i in range(n):
        parts.append(rest.astype(BF16))
        if i + 1 < n:
            rest = rest - parts[-1].astype(F32)
    return parts


def _raw_dot(a, b, form):
    lead = a.ndim - 2
    ca, cb = {"nn": (1, 0), "nt": (1, 1), "tn": (0, 0)}[form]
    batch = ((0,), (0,)) if lead else ((), ())
    return lax.dot_general(a, b, (((ca + lead,), (cb + lead,)), batch), preferred_element_type=F32)


def _pdot_impl(a, b, form, mode):
    if mode == "1":
        return _raw_dot(a.astype(BF16), b.astype(BF16), form)
    if mode == "3":
        (ah, al), (bh, bl) = _split(a, 2), _split(b, 2)
        return _raw_dot(ah, bh, form) + (_raw_dot(al, bh, form) + _raw_dot(ah, bl, form))
    if mode == "xa":
        return sum(_raw_dot(a.astype(BF16), t, form) for t in reversed(_split(b, 3)))
    return sum(_raw_dot(t, b.astype(BF16), form) for t in reversed(_split(a, 3)))


@functools.partial(jax.custom_vjp, nondiff_argnums=(2, 3))
def _pdot(a, b, form, mode):
    return _pdot_impl(a, b, form, mode)


def _pdot_fwd(a, b, form, mode):
    return _pdot_impl(a, b, form, mode), (a, b)


def _pdot_bwd(form, mode, res, ct):
    a, b = res
    da_args, db_args = {"nn": ((ct, b, "nt"), (a, ct, "tn")), "nt": ((ct, b, "nn"), (ct, a, "tn")),
                        "tn": ((b, ct, "nt"), (a, ct, "nn"))}[form]

    def side(args, exact):
        if mode in ("1", "3"):
            return mode
        return "xa" if args[0] is exact else "xb"

    if mode == "xa":
        return jnp.zeros_like(a), _pdot_impl(*db_args, side(db_args, a))
    if mode == "xb":
        return _pdot_impl(*da_args, side(da_args, b)), jnp.zeros_like(b)
    return _pdot_impl(*da_args, mode), _pdot_impl(*db_args, mode)


_pdot.defvjp(_pdot_fwd, _pdot_bwd)

GDN_QK, GDN_INV, GDN_SCAN = "1", "1", "1"


@jax.custom_vjp
def _tri_inv(low):
    eye = (_iota((CHUNK, CHUNK), 0) == _iota((CHUNK, CHUNK), 1)).astype(F32)
    inv = eye - low
    pw = low
    for _ in range(5):
        pw = _pdot_impl(pw, pw, "nn", GDN_INV)
        inv = inv + _pdot_impl(inv, pw, "nn", GDN_INV)
    return inv


def _tri_inv_fwd(low):
    inv = _tri_inv(low)
    return inv, inv


def _tri_inv_bwd(inv, ct):
    return (-_pdot_impl(_pdot_impl(inv, ct, "tn", GDN_INV), inv, "nt", GDN_INV),)


_tri_inv.defvjp(_tri_inv_fwd, _tri_inv_bwd)


def _gdn_intra(q, k, v, g, beta):
    n = q.shape[0]
    r, c = _iota((CHUNK, CHUNK), 0), _iota((CHUNK, CHUNK), 1)
    tril, strict = r >= c, r > c
    trilf = jnp.broadcast_to(tril.astype(F32), (n, CHUNK, CHUNK))
    gcm = _pdot(trilf, jnp.broadcast_to(g, (n, CHUNK, CHUNK)), "nn", "xa")
    gcf = _pdot(trilf, jnp.broadcast_to(g, (n, CHUNK, HD)), "nn", "xa")
    lane0 = (_iota((1, 1, CHUNK), 2) == 0).astype(F32)
    gcr = _pdot(jnp.ones((n, CHUNK, CHUNK), F32), gcm * lane0, "nt", "xa")
    decay = jnp.where(tril, jnp.exp(jnp.where(tril, gcm - gcr, 0.0)), 0.0)
    egc = jnp.exp(gcf)
    kb = k * beta
    low = jnp.where(strict, _pdot(kb, k, "nt", GDN_QK) * decay, 0.0)
    inv = _tri_inv(low)
    u = _pdot(inv, v * beta, "nn", GDN_INV)
    w = _pdot(inv, kb * egc, "nn", GDN_INV)
    at = jnp.where(tril, _pdot(q, k, "nt", GDN_QK) * decay, 0.0)
    gl = jnp.sum(jnp.broadcast_to(g, (n, CHUNK, HD)), axis=1, keepdims=True)
    return u, w, q * egc, at, k * jnp.exp(gl - gcf), gl


def _gdn_step(s, u, w, qg, at, kd, gl):
    vn = u - _pdot(w, s, "nn", GDN_SCAN)
    o = _pdot(qg, s, "nn", GDN_SCAN) + _pdot(at, vn, "nn", GDN_SCAN)
    s2 = s * jnp.exp(gl) + _pdot(kd, vn, "tn", GDN_SCAN)
    return o, s2


SCAN_HEADS = 3


def _gdn_chunked_scratch(nc):
    big = pltpu.VMEM((nc, CHUNK, HD), F32)
    return [big, big, big, pltpu.VMEM((nc, CHUNK, 1), F32), pltpu.VMEM((nc, CHUNK, 1), F32)]


def _gdn_term_shapes(nc):
    return [(nc, CHUNK, HD), (nc, CHUNK, HD), (nc, CHUNK, HD), (nc, CHUNK, CHUNK), (nc, CHUNK, HD), (nc, 1, HD)]


def _per_head(shape, heads=None, one_buffer=True):
    lead = (None,) if heads is None else (heads,)
    return pl.BlockSpec(lead + tuple(shape), lambda h: (h,) + (0,) * len(shape),
                        pipeline_mode=ONE_BUFFER if one_buffer else None)


def _gdn_in_specs(t):
    cw = lambda cb: pl.BlockSpec((4, HD), lambda h, cb=cb: (0, cb + h))
    return [_pcol(t, GQ), _pcol(t, GK), _pcol(t, GV), _smcol(t), cw(0), cw(NG), cw(2 * NG), _small(1), _small(1)]


def _taps(wq, wk, wv):
    return tuple(w[k:k + 1, :] for w in (wq, wk, wv) for k in range(4))


def _prep_rows(t):
    return min(t, 256)


def _gdn_pad(srcs, pads):
    for src, pad in zip(srcs, pads):
        pad[0:HALO, :] = jnp.zeros((HALO, HD), F32)
        pad[HALO:, :] = src[...]


def _gdn_stage(pads, sm, taps, al, db, h, chunked):
    t = sm.shape[0]
    rows = _prep_rows(t)
    per = rows // CHUNK

    def tile(i, carry):
        r0 = pl.multiple_of(i * rows, rows)
        vals = _gdn_prep(*[p[pl.ds(r0, rows + HALO), :] for p in pads], sm[pl.ds(r0, rows), :], taps, al, db, h)
        for v, r in zip(vals, chunked):
            r[pl.ds(i * per, per)] = v.reshape(per, CHUNK, v.shape[-1])
        return carry

    lax.fori_loop(0, t // rows, tile, 0)


def _gdn_intra_all(chunked, intra):
    nc = chunked[0].shape[0]
    grp_n = math.gcd(nc, GROUP)

    def grp(i, carry):
        sl = pl.ds(pl.multiple_of(i * grp_n, grp_n), grp_n)
        for r, val in zip(intra, _gdn_intra(*[c[sl] for c in chunked])):
            r[sl] = val
        return carry

    lax.fori_loop(0, nc // grp_n, grp, 0)


def _gdn_fwd(pa, pb, conv, alog, dtb):
    t = pa.shape[0]
    nc = t // CHUNK
    terms = _gdn_term_shapes(nc)

    def body(gq, gk, gv, sm, wq, wk, wv, al, db, *rest):
        h = pl.program_id(0)
        intra, chunked, pads = rest[:6], rest[6:11], rest[11:]
        _gdn_pad((gq, gk, gv), pads)
        _gdn_stage(pads, sm, _taps(wq, wk, wv), al[...], db[...], h, chunked)
        _gdn_intra_all(chunked, intra)

    return pl.pallas_call(
        body, grid=(NG,), name="gdn_fwd", in_specs=_gdn_in_specs(t),
        out_specs=[_per_head(sh, one_buffer=False) for sh in terms], out_shape=[SDS((NG,) + sh, F32) for sh in terms],
        scratch_shapes=_gdn_chunked_scratch(nc) + [pltpu.VMEM((t + HALO, HD), F32)] * 3, compiler_params=_cp("parallel"),
    )(pa, pa, pa, pb, conv, conv, conv, alog, dtb)


def _gdn_scan(terms_in):
    nc = terms_in[0].shape[1]
    terms = _gdn_term_shapes(nc)

    def body(*refs):
        intra, o, states = refs[:6], refs[6], refs[7]

        def step(c, ss):
            rows = pl.ds(pl.multiple_of(c * CHUNK, CHUNK), CHUNK)
            new = []
            for hh in range(SCAN_HEADS):
                states[hh, c] = ss[hh]
                oc, s2 = _gdn_step(ss[hh], *[r[hh, c] for r in intra])
                o[rows, hh * HD:(hh + 1) * HD] = oc
                new.append(s2)
            return tuple(new)

        lax.fori_loop(0, nc, step, tuple(jnp.zeros((HD, HD), F32) for _ in range(SCAN_HEADS)))

    return pl.pallas_call(
        body, grid=(NG // SCAN_HEADS,), name="gdn_scan", in_specs=[_per_head(sh, SCAN_HEADS) for sh in terms],
        out_specs=[pl.BlockSpec((nc * CHUNK, SCAN_HEADS * HD), lambda h: (0, h), pipeline_mode=ONE_BUFFER),
                   _per_head((nc, HD, HD), SCAN_HEADS)],
        out_shape=[SDS((nc * CHUNK, NG * HD), F32), SDS((NG, nc, HD, HD), F32)], compiler_params=_cp("parallel"),
    )(*terms_in)


def _gdn_bwd_scan(saved, do_raw):
    nc = saved[0].shape[1]
    terms = _gdn_term_shapes(nc)

    def body(*refs):
        intra, states, do, outs = refs[:6], refs[6], refs[7], refs[8:]

        def bwd(i, dss):
            c = nc - 1 - i
            rows = pl.ds(pl.multiple_of(c * CHUNK, CHUNK), CHUNK)
            new = []
            for hh in range(SCAN_HEADS):
                _, vjp = jax.vjp(_gdn_step, states[hh, c], *[r[hh, c] for r in intra])
                grads = vjp((do[rows, hh * HD:(hh + 1) * HD], dss[hh]))
                for r, gval in zip(outs, grads[1:]):
                    r[hh, c] = gval
                new.append(grads[0])
            return tuple(new)

        lax.fori_loop(0, nc, bwd, tuple(jnp.zeros((HD, HD), F32) for _ in range(SCAN_HEADS)))

    return pl.pallas_call(
        body, grid=(NG // SCAN_HEADS,), name="gdn_bwd_scan",
        in_specs=[_per_head(sh, SCAN_HEADS) for sh in terms] + [_per_head((nc, HD, HD), SCAN_HEADS)]
        + [pl.BlockSpec((nc * CHUNK, SCAN_HEADS * HD), lambda h: (0, h), pipeline_mode=ONE_BUFFER)],
        out_specs=[_per_head(sh, SCAN_HEADS) for sh in terms],
        out_shape=[SDS((NG,) + sh, F32) for sh in terms], compiler_params=_cp("parallel"),
    )(*saved, do_raw)


def _gdn_bwd(pa, pb, conv, alog, dtb, dterms):
    t = pa.shape[0]
    nc = t // CHUNK
    terms = _gdn_term_shapes(nc)

    def body(*refs):
        gq, gk, gv, sm, wq, wk, wv, al, db = refs[:9]
        dintra = refs[9:15]
        dgq, dgk, dgv, dsm, dwq, dwk, dwv, dal, ddb = refs[15:24]
        chunked, pads, dpads, dsm_s = refs[24:29], refs[29:32], refs[32:35], refs[35]
        h = pl.program_id(0)
        taps = _taps(wq, wk, wv)
        _gdn_pad((gq, gk, gv), pads)
        _gdn_stage(pads, sm, taps, al[...], db[...], h, chunked)
        grp_n = math.gcd(nc, GROUP)

        def grp(i, carry):
            sl = pl.ds(pl.multiple_of(i * grp_n, grp_n), grp_n)
            _, vjp = jax.vjp(_gdn_intra, *[r[sl] for r in chunked])
            for r, gval in zip(chunked, vjp(tuple(r[sl] for r in dintra))):
                r[sl] = gval
            return carry

        lax.fori_loop(0, nc // grp_n, grp, 0)

        rows = _prep_rows(t)
        per = rows // CHUNK
        for r in dpads:
            r[...] = jnp.zeros_like(r)

        def tile(i, small):
            r0 = pl.multiple_of(i * rows, rows)
            win = pl.ds(r0, rows + HALO)
            _, vjp = jax.vjp(lambda *a: _gdn_prep(*a, h), *[p[win, :] for p in pads], sm[pl.ds(r0, rows), :],
                             taps, al[...], db[...])
            grads = vjp(tuple(r[pl.ds(i * per, per)].reshape(rows, r.shape[-1]) for r in chunked))
            for r, gval in zip(dpads, grads[:3]):
                r[win, :] += gval
            dsm_s[pl.ds(r0, rows), :] = grads[3]
            return jax.tree.map(jnp.add, small, (grads[4], grads[5], grads[6]))

        zero = jnp.zeros((1, HD), F32)
        dtaps, g_al, g_db = lax.fori_loop(0, t // rows, tile, ((zero,) * 12, zero, zero))
        for r, dpad in zip((dgq, dgk, dgv), dpads):
            r[...] = dpad[HALO:, :].astype(r.dtype)
        for j, r in enumerate((dwq, dwk, dwv)):
            for k in range(4):
                r[k:k + 1, :] = dtaps[4 * j + k]

        @pl.when(h == 0)
        def _():
            for r in (dsm, dal, ddb):
                r[...] = jnp.zeros_like(r)

        dsm[...] += dsm_s[...]
        dal[...] += g_al
        ddb[...] += g_db

    head = _head(t)
    taps = pl.BlockSpec((4, HD), lambda h: (0, h))
    return pl.pallas_call(
        body, grid=(NG,), name="gdn_bwd", in_specs=_gdn_in_specs(t) + [_per_head(sh) for sh in terms],
        out_specs=[head, head, head, _small(t), taps, taps, taps, _small(1), _small(1)],
        out_shape=[SDS((t, NG * HD), BF16)] * 3 + [SDS((t, HD), F32)] + [SDS((4, NG * HD), F32)] * 3 + [SDS((1, HD), F32)] * 2,
        scratch_shapes=_gdn_chunked_scratch(nc) + [pltpu.VMEM((t + HALO, HD), F32)] * 6 + [pltpu.VMEM((t, HD), F32)],
        compiler_params=_cp("arbitrary"),
    )(pa, pa, pa, pb, conv, conv, conv, alog, dtb, *dterms)


def _gdn_post(o, z, gain):
    return (jnp.concatenate(
        [_rms(o[:, h * HD:(h + 1) * HD], gain) * _silu(z[:, h * HD:(h + 1) * HD]) for h in range(NG)], axis=1),)


def _place():
    return lax.axis_index("x"), lax.axis_index("y"), lax.axis_index("c")


def _all_gather(name, shard):
    def body(x_ref, out_ref, send_sems, recv_sems, local_sem):
        x, y, c = _place()
        me, sibling = (x, y, c), (x, y, 1 - c)
        chips = [(1 - x, y), (x, 1 - y), (1 - x, 1 - y)]

        def blk(px, py, pc):
            return out_ref.at[4 * px + 2 * py + pc]

        def copy(k, block, to, src=None):
            return pltpu.make_async_remote_copy(
                src_ref=blk(*block) if src is None else src, dst_ref=blk(*block),
                send_sem=send_sems.at[k], recv_sem=recv_sems.at[k], device_id=to, device_id_type=MESH)

        mine = pltpu.make_async_copy(x_ref, blk(*me), local_sem)
        mine.start()
        first = [copy(0, me, sibling, src=x_ref)]
        first += [copy(1 + j, me, (*chip, c), src=x_ref) for j, chip in enumerate(chips)]
        for cp in first:
            cp.start()
        passed = [copy(4 + j, (*chip, c), sibling) for j, chip in enumerate(chips)]
        for j, chip in enumerate(chips):
            copy(1 + j, (*chip, c), me).wait_recv()
            passed[j].start()
        copy(0, sibling, me).wait_recv()
        for j, chip in enumerate(chips):
            copy(4 + j, (*chip, 1 - c), me).wait_recv()
        for cp in first + passed:
            cp.wait_send()
        mine.wait()

    return pl.pallas_call(
        body, name=name, out_shape=SDS((N_DEV,) + shard.shape, shard.dtype),
        in_specs=[pl.BlockSpec(memory_space=pltpu.HBM)], out_specs=pl.BlockSpec(memory_space=pltpu.HBM),
        scratch_shapes=[pltpu.SemaphoreType.DMA((7,)), pltpu.SemaphoreType.DMA((7,)), pltpu.SemaphoreType.DMA],
    )(shard)


def _scatter_exchange(name, full):
    def body(g_ref, out_ref, send_sems, recv_sems, local_sem):
        x, y, c = _place()
        me = 4 * x + 2 * y + c
        mine = pltpu.make_async_copy(g_ref.at[me], out_ref.at[me], local_sem)
        mine.start()
        sends, recvs = [], []
        for k in range(1, N_DEV):
            px = 1 - x if k & 4 else x
            py = 1 - y if k & 2 else y
            pc = 1 - c if k & 1 else c
            peer = 4 * px + 2 * py + pc
            sends.append(pltpu.make_async_remote_copy(
                src_ref=g_ref.at[peer], dst_ref=out_ref.at[me], send_sem=send_sems.at[k - 1],
                recv_sem=recv_sems.at[k - 1], device_id=(px, py, pc), device_id_type=MESH))
            recvs.append(pltpu.make_async_remote_copy(
                src_ref=g_ref.at[me], dst_ref=out_ref.at[peer], send_sem=send_sems.at[k - 1],
                recv_sem=recv_sems.at[k - 1], device_id=(px, py, pc), device_id_type=MESH))
        for cp in sends:
            cp.start()
        for cp in recvs:
            cp.wait_recv()
        for cp in sends:
            cp.wait_send()
        mine.wait()

    return pl.pallas_call(
        body, name=name, out_shape=SDS(full.shape, full.dtype),
        in_specs=[pl.BlockSpec(memory_space=pltpu.HBM)], out_specs=pl.BlockSpec(memory_space=pltpu.HBM),
        scratch_shapes=[pltpu.SemaphoreType.DMA((7,)), pltpu.SemaphoreType.DMA((7,)), pltpu.SemaphoreType.DMA],
    )(full)


def _sum_blocks(name, parts):
    _, r, c = parts.shape
    tr = 64 if r % 64 == 0 else r

    def body(x, o):
        acc = x[0].astype(F32)
        for d in range(1, N_DEV):
            acc = acc + x[d].astype(F32)
        o[...] = acc

    return pl.pallas_call(
        body, grid=(r // tr,), name=name, in_specs=[pl.BlockSpec((N_DEV, tr, c), lambda i: (0, i, 0))],
        out_specs=pl.BlockSpec((tr, c), lambda i: (i, 0)), out_shape=SDS((r, c), F32), compiler_params=_cp("parallel"),
    )(parts)


def _reduce_scatter(name, full):
    return _sum_blocks(name + "_sum", _scatter_exchange(name, full))


def _all_reduce_small(name, x, reduce):
    m_per, n = x.shape

    def body(x_ref, out_ref, send_sems, recv_sems, local_sem):
        px, py, pc = _place()
        me, sibling = (px, py, pc), (px, py, 1 - pc)
        chips = [(1 - px, py), (px, 1 - py), (1 - px, 1 - py)]
        buf = out_ref

        def rows(qx, qy, qc):
            return buf.at[pl.ds((4 * qx + 2 * qy + qc) * m_per, m_per), :]

        def copy(k, block, to, src=None):
            return pltpu.make_async_remote_copy(
                src_ref=rows(*block) if src is None else src, dst_ref=rows(*block),
                send_sem=send_sems.at[k], recv_sem=recv_sems.at[k], device_id=to, device_id_type=MESH)

        mine = pltpu.make_async_copy(x_ref, rows(*me), local_sem)
        mine.start()
        first = [copy(0, me, sibling, src=x_ref)]
        first += [copy(1 + j, me, (*chip, pc), src=x_ref) for j, chip in enumerate(chips)]
        for cp in first:
            cp.start()
        passed = [copy(4 + j, (*chip, pc), sibling) for j, chip in enumerate(chips)]
        for j, chip in enumerate(chips):
            copy(1 + j, (*chip, pc), me).wait_recv()
            passed[j].start()
        copy(0, sibling, me).wait_recv()
        for j, chip in enumerate(chips):
            copy(4 + j, (*chip, 1 - pc), me).wait_recv()
        for cp in first + passed:
            cp.wait_send()
        mine.wait()

    gathered = pl.pallas_call(
        body, name=name, out_shape=SDS((N_DEV * m_per, n), x.dtype),
        in_specs=[pl.BlockSpec(memory_space=pltpu.VMEM)], out_specs=pl.BlockSpec(memory_space=pltpu.VMEM),
        scratch_shapes=[pltpu.SemaphoreType.DMA((7,)), pltpu.SemaphoreType.DMA((7,)), pltpu.SemaphoreType.DMA],
    )(x)
    if not reduce:
        return gathered
    return _sum_blocks(name + "_sum", gathered.reshape(N_DEV, m_per, n))


HBM_SPEC = pl.BlockSpec(memory_space=pltpu.HBM)
SEM_SPEC = pl.BlockSpec(memory_space=pltpu.SEMAPHORE)
EFFECT = pltpu.SideEffectType.DATAFLOW_SIDE_EFFECTING


def _copies_start(name, bufs, n_remote, n_local, build, deps):
    nb, nd = len(bufs), len(deps)
    sem_shapes = [pltpu.SemaphoreType.DMA((n_remote,)), pltpu.SemaphoreType.DMA((n_remote,))]
    if n_local:
        sem_shapes.append(pltpu.SemaphoreType.DMA((n_local,)))
    ns = len(sem_shapes)

    def body(*refs):
        sems = refs[nb + nd:nb + nd + ns]
        remote, local = build(refs[:nb], *sems, *([None] * (3 - ns)))
        for cp in local + remote:
            cp.start()
        refs[-1][...] = jnp.zeros((8, HD), F32)

    outs = pl.pallas_call(
        body, name=name,
        out_shape=(*sem_shapes, *[pltpu.HBM(b.shape, b.dtype) for b in bufs], SDS((8, HD), F32)),
        in_specs=[HBM_SPEC] * nb + [ANY_SPEC] * nd,
        out_specs=(*[SEM_SPEC] * ns, *[HBM_SPEC] * nb, pl.BlockSpec(memory_space=pltpu.VMEM)),
        input_output_aliases={i: ns + i for i in range(nb)},
        compiler_params=pltpu.CompilerParams(has_side_effects=EFFECT),
    )(*[pltpu.with_memory_space_constraint(b, pltpu.HBM) for b in bufs], *deps)
    return list(outs[:ns]), list(outs[ns:ns + nb]), outs[-1]


def _copies_wait(name, bufs, sems, build, after):
    nb, ns = len(bufs), len(sems)

    def body(*refs):
        remote, local = build(refs[:nb], *refs[nb:nb + ns], *([None] * (3 - ns)))
        for cp in local:
            cp.wait()
        for cp in remote:
            cp.wait_send()
            cp.wait_recv()

    outs = pl.pallas_call(
        body, name=name, out_shape=tuple(pltpu.HBM(b.shape, b.dtype) for b in bufs),
        in_specs=[HBM_SPEC] * nb + [SEM_SPEC] * ns + [ANY_SPEC] * len(after), out_specs=tuple([HBM_SPEC] * nb),
        input_output_aliases={i: i for i in range(nb)},
        compiler_params=pltpu.CompilerParams(has_side_effects=EFFECT),
    )(*bufs, *sems, *after)
    return list(outs)


def _remote(src, dst, send, recv, k, to):
    return pltpu.make_async_remote_copy(src_ref=src, dst_ref=dst, send_sem=send.at[k], recv_sem=recv.at[k],
                                        device_id=to, device_id_type=MESH)


class _Gather:
    def __init__(self, name, shards, deps):
        self.name, self.n = name, len(shards)
        lands = [lax.empty((N_DEV,) + s.shape, s.dtype) for s in shards]
        self.sems1, bufs, self.token = _copies_start(
            name + "_s1", list(shards) + lands, 4 * self.n, self.n, self._stage1(range(self.n)), deps)
        self.shards, self.lands, self.sems2 = bufs[:self.n], bufs[self.n:], {}

    def _stage1(self, idxs):
        def build(refs, send, recv, loc):
            x, y, c = _place()
            me = 4 * x + 2 * y + c
            targets = [(x, y, 1 - c), (1 - x, y, c), (x, 1 - y, c), (1 - x, 1 - y, c)]
            remote, local = [], []
            for pos, i in enumerate(idxs):
                src, land = refs[pos], refs[len(idxs) + pos]
                local.append(pltpu.make_async_copy(src, land.at[me], loc.at[i]))
                remote += [_remote(src, land.at[me], send, recv, 4 * i + k, to) for k, to in enumerate(targets)]
            return remote, local
        return build

    @staticmethod
    def _stage2(refs, send, recv, loc):
        x, y, c = _place()
        remote = []
        for pos, land in enumerate(refs):
            for j, (cx, cy) in enumerate([(1 - x, y), (x, 1 - y), (1 - x, 1 - y)]):
                blk = land.at[4 * cx + 2 * cy + c]
                remote.append(_remote(blk, blk, send, recv, 3 * pos + j, (x, y, 1 - c)))
        return remote, []

    def pass_on(self, idxs, after):
        tag, m = "".join(map(str, idxs)), len(idxs)
        bufs = _copies_wait(f"{self.name}_w1_{tag}", [self.shards[i] for i in idxs] + [self.lands[i] for i in idxs],
                            self.sems1, self._stage1(idxs), after)
        self.sems2[tag], lands, token = _copies_start(f"{self.name}_s2_{tag}", bufs[m:], 3 * m, 0, self._stage2, ())
        for pos, i in enumerate(idxs):
            self.lands[i] = lands[pos]
        return [token]

    def get(self, idxs, after):
        tag = "".join(map(str, idxs))
        return _copies_wait(f"{self.name}_w2_{tag}", [self.lands[i] for i in idxs], self.sems2[tag], self._stage2, after)


def _rows_tile(r, row_bytes, target=1 << 20):
    tr = r
    while tr % 32 == 0 and tr * row_bytes > target:
        tr //= 2
    return tr


def _pair_add(name, g, got, c):
    _, r, cols = g.shape
    tr = _rows_tile(r, cols * 2)

    def body(s, a, b, o):
        o[...] = (a[...].astype(F32) + b[...].astype(F32)).astype(o.dtype)

    return pl.pallas_call(
        body, name=name, out_shape=SDS((4, r, cols), g.dtype),
        grid_spec=pltpu.PrefetchScalarGridSpec(
            num_scalar_prefetch=1, grid=(4, r // tr),
            in_specs=[pl.BlockSpec((None, tr, cols), lambda j, i, s: (2 * j + s[0], i, 0)),
                      pl.BlockSpec((None, tr, cols), lambda j, i, s: (j, i, 0))],
            out_specs=pl.BlockSpec((None, tr, cols), lambda j, i, s: (j, i, 0))),
        compiler_params=_cp("parallel", "parallel"),
    )(c.reshape(1), g, got)


def _quad_sum(name, part, got, chip):
    _, r, cols = part.shape
    tr = _rows_tile(r, cols * 4)

    def body(s, a, b1, b2, b3, o):
        o[...] = ((a[...].astype(F32) + b1[...].astype(F32)) + b2[...].astype(F32)) + b3[...].astype(F32)

    blk = lambda k: pl.BlockSpec((None, tr, cols), lambda i, s, k=k: (jnp.bitwise_xor(s[0], k), i, 0))
    return pl.pallas_call(
        body, name=name, out_shape=SDS((r, cols), F32),
        grid_spec=pltpu.PrefetchScalarGridSpec(
            num_scalar_prefetch=1, grid=(r // tr,), in_specs=[blk(0), blk(1), blk(2), blk(3)],
            out_specs=pl.BlockSpec((tr, cols), lambda i, s: (i, 0))),
        compiler_params=_cp("parallel"),
    )(chip.reshape(1), part, got, got, got)


class _Scatter:
    def __init__(self, name, grads, deps):
        self.name, self.n = name, len(grads)
        got = [lax.empty((4,) + g.shape[1:], g.dtype) for g in grads]
        self.sems, bufs, self.token = _copies_start(name + "_s1", list(grads) + got, 4 * self.n, 0, self._stage1, deps)
        self.grads, self.got = bufs[:self.n], bufs[self.n:]

    def _stage1(self, refs, send, recv, loc):
        x, y, c = _place()
        remote = []
        for i in range(self.n):
            remote += [_remote(refs[i].at[2 * j + 1 - c], refs[self.n + i].at[j], send, recv, 4 * i + j, (x, y, 1 - c))
                       for j in range(4)]
        return remote, []

    def _stage2(self, refs, send, recv, loc):
        x, y, c = _place()
        remote = []
        for i in range(self.n):
            for k in (1, 2, 3):
                tx = 1 - x if k & 2 else x
                ty = 1 - y if k & 1 else y
                remote.append(_remote(refs[i].at[2 * tx + ty], refs[self.n + i].at[2 * x + y], send, recv,
                                      3 * i + k - 1, (tx, ty, c)))
        return remote, []

    def mid(self, after):
        bufs = _copies_wait(self.name + "_w1", self.grads + self.got, self.sems, self._stage1, after)
        c = lax.axis_index("c").astype(jnp.int32)
        parts = [_pair_add(f"{self.name}_add{i}", bufs[i], bufs[self.n + i], c) for i in range(self.n)]
        got = [lax.empty(p.shape, p.dtype) for p in parts]
        self.sems, bufs, self.token = _copies_start(self.name + "_s2", parts + got, 3 * self.n, 0, self._stage2, ())
        self.parts, self.got = bufs[:self.n], bufs[self.n:]

    def end(self, after):
        bufs = _copies_wait(self.name + "_w2", self.parts + self.got, self.sems, self._stage2, after)
        chip = (2 * lax.axis_index("x") + lax.axis_index("y")).astype(jnp.int32)
        return [_quad_sum(f"{self.name}_sum{i}", bufs[i], bufs[self.n + i], chip) for i in range(self.n)]


def _adamw(w, g, m, v):
    m = ADAM_B1 * m + (1.0 - ADAM_B1) * g
    v = ADAM_B2 * v + (1.0 - ADAM_B2) * (g * g)
    m_hat = m / (1.0 - ADAM_B1 ** ADAM_STEP)
    v_hat = v / (1.0 - ADAM_B2 ** ADAM_STEP)
    return -ADAM_LR * (m_hat / (jnp.sqrt(v_hat) + ADAM_EPS) + ADAM_WD * w), m, v


def _adamw_call(name, w, g, m, v):
    r, c = w.shape
    tm = 64 if r % 64 == 0 else r
    return _rowwise(name, _adamw, [w, g, m, v], [], [(c, F32)] * 3, tm)


_IN_COLS = 5906


def _perm_in(w):
    pad = jnp.zeros((w.shape[0], 2 * HALF - _IN_COLS), w.dtype)
    return (jnp.concatenate([w[:, 2310:4614], w[:, 4614:5382]], axis=1),
            jnp.concatenate([w[:, :2304], w[:, 5394:5906], w[:, 2304:2310], w[:, 5382:5394], pad], axis=1))


def _unperm_in(ga, gb):
    return jnp.concatenate([gb[:, :2304], gb[:, 2816:2822], ga[:, :2304], ga[:, 2304:3072], gb[:, 2822:2834],
                            gb[:, 2304:2816]], axis=1)


def _lanes(v, at):
    return jnp.pad(v, ((0, 0), (at, HD - at - v.shape[1])))


_PACK = ("norm_mix", "mem_norm", "norm_ffn", "gdn_conv", "fox_q_norm", "fox_k_norm", "gdn_out_norm", "mem_q_norm",
         "mem_k_norm", "fox_f_bias", "gdn_a_log", "gdn_dt_bias", "loss")


def _pack(vals):
    parts = [vals[n].reshape(-1, HD) for n in _PACK]
    used = sum(p.shape[0] for p in parts)
    buf = jnp.concatenate(parts + [jnp.zeros((-used % 8, HD), F32)], axis=0)
    return buf, [(n, p.shape[0]) for n, p in zip(_PACK, parts)]


def _unpack(buf, layout):
    out, at = {}, 0
    for n, rows in layout:
        out[n] = buf[at:at + rows]
        at += rows
    return out


def kernel(x, mem, norm_mix, w_in, fox_f_bias, fox_q_norm, fox_k_norm, gdn_conv, gdn_a_log, gdn_dt_bias, gdn_out_norm, mem_norm, w_mem_kv, mem_q_norm, mem_k_norm, w_out, norm_ffn, w_gate_up, w_down, loss_target, m_norm_mix, m_w_in, m_fox_f_bias, m_fox_q_norm, m_fox_k_norm, m_gdn_conv, m_gdn_a_log, m_gdn_dt_bias, m_gdn_out_norm, m_mem_norm, m_w_mem_kv, m_mem_q_norm, m_mem_k_norm, m_w_out, m_norm_ffn, m_w_gate_up, m_w_down, v_norm_mix, v_w_in, v_fox_f_bias, v_fox_q_norm, v_fox_k_norm, v_gdn_conv, v_gdn_a_log, v_gdn_dt_bias, v_gdn_out_norm, v_mem_norm, v_w_mem_kv, v_mem_q_norm, v_mem_k_norm, v_w_out, v_norm_ffn, v_w_gate_up, v_w_down):
    args = dict(locals())
    d = x.shape[2]
    me = 4 * lax.axis_index("x") + 2 * lax.axis_index("y") + lax.axis_index("c")

    cshard = gdn_conv[0].shape[1]
    conv_pad = jnp.pad(gdn_conv[0], ((0, 4), (0, 3 * HD - cshard)))
    conv_all = _all_reduce_small("ag_conv", conv_pad, False).reshape(N_DEV, 8, 3 * HD)[:, :4, :cshard]
    conv_all = conv_all.transpose(1, 0, 2).reshape(4, N_DEV * cshard)
    w_in_a, w_in_b = _perm_in(w_in[0])
    comm = _StepComm({"in_b": [w_in_b], "in_a": [w_in_a], "kv_out": [w_mem_kv[0], w_out[0]], "gate_up": [w_gate_up[0]],
                      "down": [w_down[0]]}, [conv_all])

    grad_x, loss_local, small_grads = _local_step(
        x[0], mem[0], loss_target[0], norm_mix, fox_f_bias, fox_q_norm, fox_k_norm, gdn_a_log, gdn_dt_bias,
        gdn_out_norm, mem_norm, mem_q_norm, mem_k_norm, norm_ffn, conv_all, comm)

    red = comm.finish([grad_x])
    grads = {"w_down": red["ffn"][0], "w_gate_up": red["ffn"][1], "w_out": red["a"][1], "w_mem_kv": red["b"][1],
             "w_in": _unperm_in(red["a"][0], red["b"][0])}
    small_grads["loss"] = jnp.broadcast_to(loss_local, (1, HD))
    packed, layout = _pack(small_grads)
    small = _unpack(_all_reduce_small("ar_small", packed, True), layout)
    loss = small["loss"][0, 0]
    six = {"fox_f_bias": L_FF, "gdn_a_log": L_GA, "gdn_dt_bias": L_GA}
    for n, rows_n in layout[:-1]:
        gsm = small[n]
        if n == "gdn_conv":
            gsm = lax.dynamic_slice(gsm.reshape(4, N_DEV * cshard), (0, me * cshard), (4, cshard))[None]
        elif n in six:
            gsm = gsm[:, six[n]:six[n] + 6]
        else:
            gsm = gsm.reshape(1, rows_n * HD)
        grads[n] = gsm

    names = ['norm_mix', 'w_in', 'fox_f_bias', 'fox_q_norm', 'fox_k_norm', 'gdn_conv', 'gdn_a_log', 'gdn_dt_bias',
             'gdn_out_norm', 'mem_norm', 'w_mem_kv', 'mem_q_norm', 'mem_k_norm', 'w_out', 'norm_ffn', 'w_gate_up', 'w_down']
    big = ("w_in", "w_mem_kv", "w_out", "w_gate_up", "w_down")
    delta, new_m, new_v = {}, {}, {}
    for n in big:
        delta[n], new_m[n], new_v[n] = [a[None] for a in _adamw_call(
            "adamw_" + n, args[n][0], grads[n], args["m_" + n][0], args["v_" + n][0])]
        grads[n] = grads[n][None]

    def flat(a):
        a = a.reshape(1, -1)
        return jnp.pad(a, ((0, 0), (0, -a.shape[1] % HD))).reshape(-1, HD)

    smalls = [n for n in names if n not in big]
    pk = lambda pre: jnp.concatenate([flat(grads[n] if pre == "g" else args[pre + n]) for n in smalls], axis=0)
    cat = [pk(""), pk("g"), pk("m_"), pk("v_")]
    padr = -cat[0].shape[0] % 8
    cat = [jnp.pad(a, ((0, padr), (0, 0))) for a in cat]
    res = _adamw_call("adamw_small", *cat)
    at = 0
    for n in smalls:
        shape = args[n].shape
        size = math.prod(shape)
        nrow = -(-size // HD)
        for dst, src in zip((delta, new_m, new_v), res):
            dst[n] = src[at:at + nrow].reshape(-1)[:size].reshape(shape)
        at += nrow

    return (loss, grad_x[None], *[grads[n] for n in names], *[delta[n] for n in names],
            *[new_m[n] for n in names], *[new_v[n] for n in names])


class _StepComm:
    def __init__(self, shard_groups, after):
        self.groups, shards = {}, []
        for key, ws in shard_groups.items():
            self.groups[key] = list(range(len(shards), len(shards) + len(ws)))
            shards += [w.astype(BF16) for w in ws]
        self.gather = _Gather("ag", shards, after)
        self.passed, self.scatters = set(), {}

    def start_deps(self):
        return [self.gather.token]

    def pass_on(self, key, after):
        self.passed.add(key)
        return self.gather.pass_on(self.groups[key], after)

    def weights(self, key, after):
        if key not in self.passed:
            after = self.pass_on(key, after)
        return self.gather.get(self.groups[key], after)

    def send(self, tag, grads):
        blocks = [g if g.ndim == 3 else g.reshape(N_DEV, g.shape[0] // N_DEV, g.shape[1]) for g in grads]
        self.scatters[tag] = _Scatter("rs_" + tag, blocks, ())
        return [self.scatters[tag].token]

    def mid(self, tag, after):
        self.scatters[tag].mid(after)
        return [self.scatters[tag].token]

    def finish(self, after):
        return {tag: sc.end(after) for tag, sc in self.scatters.items()}


def _local_step(xs, ms, tgt, norm_mix, fox_f_bias, fox_q_norm, fox_k_norm, gdn_a_log, gdn_dt_bias, gdn_out_norm,
                mem_norm, mem_q_norm, mem_k_norm, norm_ffn, conv_all, comm):
    t, d = xs.shape
    bq = min(t, 256)
    fb, alog, dtb = _lanes(fox_f_bias, L_FF), _lanes(gdn_a_log, L_GA), _lanes(gdn_dt_bias, L_GA)
    flat = lambda w: w.reshape(-1, w.shape[-1])

    rms1 = lambda a, g: (_rms(a, g),)
    (u,) = _rowwise("norm_mix", rms1, [xs], [norm_mix], [(d, BF16)], min(t, 256), deps=comm.start_deps())
    w_in_b = flat(comm.weights("in_b", [u])[0])
    pb = _matmul("proj_in_b", u, w_in_b, NN, F32, 1024, 768)
    o_fox = _fox_fwd(pb, fb, fox_q_norm, fox_k_norm, bq)
    w_in_a = flat(comm.weights("in_a", [o_fox])[0])
    pa = _matmul("proj_in_a", u, w_in_a, NN, F32, 1024, 768)
    gdn_terms = _gdn_fwd(pa, pb, conv_all, alog, dtb)
    o_gdn_raw, gdn_states = _gdn_scan(gdn_terms)
    gdn_saved = list(gdn_terms) + [gdn_states]
    zrow = (pa, NG * HD, GZ * HD // (NG * HD))
    (o_gdn,) = _rowwise("gdn_post", _gdn_post, [o_gdn_raw, zrow], [gdn_out_norm], [(NG * HD, BF16)], min(t, 256))
    w_kv_all, w_out_all = [flat(w) for w in comm.weights("kv_out", [o_gdn])]
    (mem_n,) = _rowwise("norm_mem", rms1, [ms], [mem_norm], [(d, BF16)], ms.shape[0])
    mkv = _matmul("proj_mem", mem_n, w_kv_all, NN, F32, 256, 512)
    o_mem = _mem_fwd(pb, mkv, mem_q_norm, mem_k_norm)
    deps = comm.pass_on("gate_up", [o_mem])
    mix = jnp.concatenate([o_fox, o_gdn, o_mem], axis=1)
    h1, h1n = _proj_out_norm(mix, w_out_all, xs, norm_ffn, deps)
    (wgu,) = comm.weights("gate_up", [h1n])
    ffw = wgu.shape[2]
    gu, act = _ffn_up(h1n, wgu.reshape(2, 4, d, ffw))
    w_down_all = flat(comm.weights("down", [act])[0])
    dy, dyb, lsum = _ffn_down_loss(act, w_down_all, h1, tgt)
    loss_local = (0.5 / d) * jnp.sum(lsum[::8, ::HD])

    dgu = _ffn_down_bwd(dyb, w_down_all.reshape(4, ffw, d), gu).reshape(8, t, ffw)
    g_w_down = _matmul("grad_w_down", act, dyb, TN, BF16, 512, 2048)
    dh1n = _ffn_up_bwd_x(dgu, wgu)
    g_w_gu = _ffn_up_bwd_w(h1n, dgu)
    deps = comm.send("ffn", [g_w_down, g_w_gu])
    rms2 = lambda a, g: (_rms(a, g), a)
    dh1, dh1b, g_norm_ffn = _rowwise_vjp("norm_ffn_bwd", rms2, [h1], [norm_ffn], [dh1n, dy], [(F32, BF16)],
                                         min(t, 256), deps=deps)

    dmix = _matmul("proj_out_bwd_x", dh1b, w_out_all, NT, F32, 1024, 1024)
    g_w_out = _matmul("grad_w_out", mix, dh1b, TN, BF16, 1024, 2048)
    deps = comm.mid("ffn", [dmix, g_w_out])
    do_raw, dgz, g_gon = _rowwise_vjp("gdn_post_bwd", _gdn_post, [o_gdn_raw, zrow], [gdn_out_norm],
                                      [(dmix, NG * HD, 1)], [F32, BF16], min(t, 256), deps=deps)
    dterms = _gdn_bwd_scan(gdn_saved, do_raw)
    dgq, dgk, dgv, dsm_gdn, dwq, dwk, dwv, g_alog, g_dtb = _gdn_bwd(pa, pb, conv_all, alog, dtb, dterms)
    dp_a = jnp.concatenate([dgq, dgk, dgv, dgz], axis=1)
    g_w_in_a = _matmul("grad_w_in_a", u, dp_a, TN, BF16, 512, 3072)
    deps = comm.send("a", [g_w_in_a, g_w_out])
    dmq, dmk, dmv, g_mqn, g_mkn = _mem_bwd(pb, mkv, mem_q_norm, mem_k_norm, dmix, deps=deps)
    dmkv = jnp.concatenate([dmk, dmv], axis=1).astype(BF16)
    dmem_n = _matmul("proj_mem_bwd_x", dmkv, w_kv_all, NT, F32, 256, 512)
    g_w_kv = _matmul("grad_w_kv", mem_n, dmkv, TN, BF16, 512, 512)
    g_mem_norm = _rowwise_vjp("norm_mem_bwd", rms1, [ms], [mem_norm], [dmem_n], [], ms.shape[0])[0]
    deps = comm.mid("a", [g_mem_norm, g_w_kv])
    dfq, dfk, dfv, dsm_fox, g_fb, g_fqn, g_fkn = _fox_bwd(pb, fb, fox_q_norm, fox_k_norm, dmix, bq, deps=deps)
    dp_b = jnp.concatenate([dfq, dfk, dfv, dmq, (dsm_fox + dsm_gdn).astype(BF16), jnp.zeros((t, HD), BF16)], axis=1)
    g_w_in_b = _matmul("grad_w_in_b", u, dp_b, TN, BF16, 512, 3072)
    deps = comm.send("b", [g_w_in_b, g_w_kv])
    du_a = _matmul("proj_in_bwd_a", dp_a, w_in_a, NT, F32, 1024, 1024, deps=deps)
    deps = comm.mid("b", [du_a])
    du = _matmul("proj_in_bwd_b", dp_b, w_in_b, NT, F32, 1024, 1024, residual=du_a, deps=deps)
    grad_x, g_norm_mix = _rowwise_vjp("norm_mix_bwd", rms2, [xs], [norm_mix], [du, dh1], [F32], min(t, 256))

    small_grads = {
        "norm_mix": g_norm_mix, "mem_norm": g_mem_norm, "norm_ffn": g_norm_ffn,
        "gdn_conv": jnp.concatenate([dwq, dwk, dwv], axis=1),
        "fox_q_norm": g_fqn, "fox_k_norm": g_fkn, "gdn_out_norm": g_gon, "mem_q_norm": g_mqn, "mem_k_norm": g_mkn,
        "fox_f_bias": g_fb, "gdn_a_log": g_alog, "gdn_dt_bias": g_dtb}
    return grad_x, loss_local, small_grads
```

```python
import functools
import math

import jax
import jax.numpy as jnp
from jax import lax
from jax.experimental import pallas as pl
from jax.experimental.pallas import tpu as pltpu

F32 = jnp.float32
BF16 = jnp.bfloat16
HI = lax.Precision.HIGHEST
SDS = jax.ShapeDtypeStruct

N_DEV = 8
HD = 128
NF, NG, NM = 6, 6, 4
CHUNK = 64
GROUP = 16
NORM_EPS = 1e-6
GQ, GK, GV, GZ = 0, 6, 12, 18
FQ, FK, FV, MQ, SM = 0, 6, 12, 18, 22
HALF = 24 * HD
L_FF, L_GA, L_GB = 0, 6, 12
VMEM_LIMIT = 56 * 1024 * 1024

ADAM_LR, ADAM_B1, ADAM_B2, ADAM_EPS, ADAM_WD, ADAM_STEP = 0.001, 0.9, 0.999, 1e-08, 0.01, 10

NN = (((1,), (0,)), ((), ()))
NT = (((1,), (1,)), ((), ()))
TN = (((0,), (0,)), ((), ()))
MESH = pl.DeviceIdType.MESH


def _cp(*sem):
    return pltpu.CompilerParams(dimension_semantics=tuple(sem) if sem else None, vmem_limit_bytes=VMEM_LIMIT)


def _dot(a, b, dims=NN):
    return lax.dot_general(a, b, dims, preferred_element_type=F32)


def _bdot(a, b):
    return _dot(a.astype(BF16), b.astype(BF16))


def _iota(shape, axis):
    return lax.broadcasted_iota(jnp.int32, shape, axis)


def _rms(x, gain):
    return x * lax.rsqrt(jnp.mean(x * x, axis=-1, keepdims=True) + NORM_EPS) * gain


def _sigmoid(x):
    return 0.5 * jnp.tanh(0.5 * x) + 0.5


def _silu(x):
    return x * _sigmoid(x)


def _softplus(x):
    return jnp.maximum(x, 0.0) + jnp.log(1.0 + jnp.exp(-jnp.abs(x)))


def _lane_pick(x, lane):
    oh = (_iota((1, x.shape[-1]), 1) == lane).astype(F32)
    return jnp.sum(x * oh, axis=-1, keepdims=True)


def _cumsum_rows(x):
    tril = (_iota((HD, HD), 0) >= _iota((HD, HD), 1)).astype(F32)
    carry = jnp.zeros((1, x.shape[1]), F32)
    outs = []
    for b in range(x.shape[0] // HD):
        blk = x[b * HD:(b + 1) * HD]
        outs.append(jnp.dot(tril, blk, precision=HI, preferred_element_type=F32) + carry)
        carry = carry + jnp.sum(blk, axis=0, keepdims=True)
    return jnp.concatenate(outs, axis=0)


def _row_spec(r, tm):
    if isinstance(r, tuple):
        arr, width, cb = r
        return arr, pl.BlockSpec((tm, width), lambda i, cb=cb: (i, cb))
    return r, pl.BlockSpec((tm, r.shape[1]), lambda i: (i, 0))


ANY_SPEC = pl.BlockSpec(memory_space=pl.ANY)


def _rowwise(name, fn, rows, consts, outs, tm, deps=()):
    arrs, specs = zip(*[_row_spec(r, tm) for r in rows])
    n_rows = arrs[0].shape[0]
    nr, nc, nd = len(rows), len(consts), len(deps)

    def body(*refs):
        res = fn(*[r[...] for r in refs[:nr + nc]])
        for o, v in zip(refs[nr + nc + nd:], res):
            o[...] = v.astype(o.dtype)

    return pl.pallas_call(
        body, grid=(n_rows // tm,), name=name,
        in_specs=list(specs) + [pl.BlockSpec(c.shape, lambda i: (0, 0)) for c in consts] + [ANY_SPEC] * nd,
        out_specs=[pl.BlockSpec((tm, w), lambda i: (i, 0)) for w, _ in outs],
        out_shape=[SDS((n_rows, w), dt) for w, dt in outs],
        compiler_params=_cp("parallel"),
    )(*arrs, *consts, *deps)


def _rowwise_vjp(name, fn, rows, consts, cts, grad_dtypes, tm, deps=()):
    arrs, specs = zip(*[_row_spec(r, tm) for r in rows])
    ct_arrs, ct_specs = zip(*[_row_spec(r, tm) for r in cts])
    n_rows = arrs[0].shape[0]
    nr, nc, nct, nd = len(rows), len(consts), len(cts), len(deps)
    plan = [(j, dt) for j, dts in enumerate(grad_dtypes) for dt in (dts if isinstance(dts, tuple) else (dts,))]
    ng = len(plan)
    widths = [specs[j].block_shape[1] for j, _ in plan]
    grad_dtypes = [dt for _, dt in plan]

    def body(*refs):
        vals = [r[...].astype(F32) for r in refs[:nr + nc]]
        ctv = tuple(r[...].astype(F32) for r in refs[nr + nc:nr + nc + nct])
        _, vjp = jax.vjp(fn, *vals)
        grads = vjp(ctv)
        outs = refs[nr + nc + nct + nd:]
        for o, (j, _) in zip(outs[:ng], plan):
            o[...] = grads[j].astype(o.dtype)

        @pl.when(pl.program_id(0) == 0)
        def _():
            for o in outs[ng:]:
                o[...] = jnp.zeros_like(o)

        for o, g in zip(outs[ng:], grads[nr:]):
            o[...] += g

    return pl.pallas_call(
        body, grid=(n_rows // tm,), name=name,
        in_specs=list(specs) + [pl.BlockSpec(c.shape, lambda i: (0, 0)) for c in consts] + list(ct_specs)
        + [ANY_SPEC] * nd,
        out_specs=[pl.BlockSpec((tm, w), lambda i: (i, 0)) for w in widths]
        + [pl.BlockSpec(c.shape, lambda i: (0, 0)) for c in consts],
        out_shape=[SDS((n_rows, w), dt) for w, dt in zip(widths, grad_dtypes)] + [SDS(c.shape, F32) for c in consts],
        compiler_params=_cp("arbitrary"),
    )(*arrs, *consts, *ct_arrs, *deps)


def _tile(n, pref):
    t = min(n, pref)
    while n % t or (t % HD and t != n):
        t -= 1
    return t


def _matmul(name, a, b, dims, out_dtype, tm, tn, residual=None, deps=()):
    ta, tb = dims == TN, dims == NT
    m = a.shape[1] if ta else a.shape[0]
    k = a.shape[0] if ta else a.shape[1]
    n = b.shape[0] if tb else b.shape[1]
    tm, tn = _tile(m, tm), _tile(n, tn)

    def body(*refs):
        acc = _dot(refs[0][...], refs[1][...], dims)
        if residual is not None:
            acc = acc + refs[2][...]
        refs[-1][...] = acc.astype(out_dtype)

    in_specs = [pl.BlockSpec((k, tm), lambda i, j: (0, i)) if ta else pl.BlockSpec((tm, k), lambda i, j: (i, 0)),
                pl.BlockSpec((tn, k), lambda i, j: (j, 0)) if tb else pl.BlockSpec((k, tn), lambda i, j: (0, j))]
    ops = [a, b]
    if residual is not None:
        in_specs.append(pl.BlockSpec((tm, tn), lambda i, j: (i, j)))
        ops.append(residual)
    in_specs += [ANY_SPEC] * len(deps)
    ops += list(deps)
    return pl.pallas_call(
        body, grid=(m // tm, n // tn), name=name, in_specs=in_specs,
        out_specs=pl.BlockSpec((tm, tn), lambda i, j: (i, j)), out_shape=SDS((m, n), out_dtype),
        compiler_params=_cp("parallel", "parallel"),
    )(*ops)


def _proj_out_norm(mix, w_out, xs, gain, deps):
    t, k = mix.shape
    d = w_out.shape[1]
    tm = _tile(t, 512)

    def body(*refs):
        a, b, x, g = refs[:4]
        h1, h1n = refs[4 + len(deps):]
        acc = _dot(a[...], b[...]) + x[...]
        h1[...] = acc
        h1n[...] = _rms(acc, g[...]).astype(BF16)

    return pl.pallas_call(
        body, grid=(t // tm,), name="proj_out",
        in_specs=[pl.BlockSpec((tm, k), lambda i: (i, 0)), pl.BlockSpec((k, d), lambda i: (0, 0)),
                  pl.BlockSpec((tm, d), lambda i: (i, 0)), pl.BlockSpec((1, d), lambda i: (0, 0))] + [ANY_SPEC] * len(deps),
        out_specs=[pl.BlockSpec((tm, d), lambda i: (i, 0))] * 2, out_shape=[SDS((t, d), F32), SDS((t, d), BF16)],
        compiler_params=_cp("parallel"),
    )(mix, w_out, xs, gain, *deps)


def _ffn_up(h1n, wgu):
    t, d = h1n.shape
    w = wgu.shape[3]
    tm = _tile(t, 512)

    def body(a, b, gu, act):
        x = a[...]
        g = _dot(x, b[0])
        u = _dot(x, b[1])
        gu[0] = g.astype(BF16)
        gu[1] = u.astype(BF16)
        act[...] = (_silu(g) * u).astype(BF16)

    return pl.pallas_call(
        body, grid=(4, t // tm), name="ffn_up",
        in_specs=[pl.BlockSpec((tm, d), lambda j, i: (i, 0)), pl.BlockSpec((2, None, d, w), lambda j, i: (0, j, 0, 0))],
        out_specs=[pl.BlockSpec((2, None, tm, w), lambda j, i: (0, j, i, 0)), pl.BlockSpec((tm, w), lambda j, i: (i, j))],
        out_shape=[SDS((2, 4, t, w), BF16), SDS((t, 4 * w), BF16)],
        compiler_params=_cp("parallel", "parallel"),
    )(h1n, wgu)


def _ffn_down_loss(act, wdown, h1, target):
    t, f = act.shape
    d = wdown.shape[1]
    tm, tn = _tile(t, 1024), _tile(d, 512)

    def body(a, b, h, tg, dy, dyb, ls):
        e = _dot(a[...], b[...]) + h[...] - tg[...]
        g = e * (1.0 / d)
        dy[...] = g
        dyb[...] = g.astype(BF16)
        ls[...] = jnp.broadcast_to(jnp.sum(e * e), (8, HD))

    return pl.pallas_call(
        body, grid=(t // tm, d // tn), name="ffn_down_loss",
        in_specs=[pl.BlockSpec((tm, f), lambda i, j: (i, 0)), pl.BlockSpec((f, tn), lambda i, j: (0, j)),
                  pl.BlockSpec((tm, tn), lambda i, j: (i, j)), pl.BlockSpec((tm, tn), lambda i, j: (i, j))],
        out_specs=[pl.BlockSpec((tm, tn), lambda i, j: (i, j)), pl.BlockSpec((tm, tn), lambda i, j: (i, j)),
                   pl.BlockSpec((8, HD), lambda i, j: (i, j))],
        out_shape=[SDS((t, d), F32), SDS((t, d), BF16), SDS((8 * (t // tm), HD * (d // tn)), F32)],
        compiler_params=_cp("parallel", "parallel"),
    )(act, wdown, h1, target)


def _ffn_down_bwd(dyb, wdown4, gu):
    t, d = dyb.shape
    w = wdown4.shape[1]
    tm = _tile(t, 512)

    def body(a, b, gu_ref, out):
        da = _dot(a[...], b[...], NT)
        g = gu_ref[0].astype(F32)
        u = gu_ref[1].astype(F32)
        s = _sigmoid(g)
        out[0] = (da * u * (s * (1.0 + g * (1.0 - s)))).astype(BF16)
        out[1] = (da * g * s).astype(BF16)

    return pl.pallas_call(
        body, grid=(4, t // tm), name="ffn_down_bwd",
        in_specs=[pl.BlockSpec((tm, d), lambda j, i: (i, 0)), pl.BlockSpec((None, w, d), lambda j, i: (j, 0, 0)),
                  pl.BlockSpec((2, None, tm, w), lambda j, i: (0, j, i, 0))],
        out_specs=pl.BlockSpec((2, None, tm, w), lambda j, i: (0, j, i, 0)),
        out_shape=SDS((2, 4, t, w), BF16),
        compiler_params=_cp("parallel", "parallel"),
    )(dyb, wdown4, gu)


def _ffn_up_bwd_x(dgu, wgu):
    _, t, w = dgu.shape
    d = wgu.shape[1]
    tm = _tile(t, 512)

    def body(a, b, out):
        @pl.when(pl.program_id(1) == 0)
        def _():
            out[...] = jnp.zeros_like(out)
        out[...] += _dot(a[...], b[...], NT)

    return pl.pallas_call(
        body, grid=(t // tm, 8), name="ffn_up_bwd_x",
        in_specs=[pl.BlockSpec((None, tm, w), lambda i, j: (j, i, 0)), pl.BlockSpec((None, d, w), lambda i, j: (j, 0, 0))],
        out_specs=pl.BlockSpec((tm, d), lambda i, j: (i, 0)), out_shape=SDS((t, d), F32),
        compiler_params=_cp("parallel", "arbitrary"),
    )(dgu, wgu)


def _ffn_up_bwd_w(h1n, dgu):
    _, t, w = dgu.shape
    d = h1n.shape[1]
    tm = _tile(d, 512)

    def body(a, b, out):
        out[...] = _dot(a[...], b[...], TN).astype(BF16)

    return pl.pallas_call(
        body, grid=(8, d // tm), name="ffn_up_bwd_w",
        in_specs=[pl.BlockSpec((t, tm), lambda j, i: (0, i)), pl.BlockSpec((None, t, w), lambda j, i: (j, 0, 0))],
        out_specs=pl.BlockSpec((None, tm, w), lambda j, i: (j, i, 0)), out_shape=SDS((8, d, w), BF16),
        compiler_params=_cp("parallel", "parallel"),
    )(h1n, dgu)


def _fox_prep(fq, fk, sm, fb, qg, kg, h):
    qn = _rms(fq, qg)
    kn = _rms(fk, kg)
    c = _cumsum_rows(-_softplus(-(sm + fb)))
    ccol = _lane_pick(c, L_FF + h)
    crow = jnp.sum(c.T * (_iota((HD, 1), 0) == L_FF + h).astype(F32), axis=0, keepdims=True)
    return qn, kn, ccol, crow


def _softmax_times(s, v):
    e = jnp.exp(s - lax.stop_gradient(jnp.max(s, axis=1, keepdims=True)))
    return _dot(e.astype(BF16), v.astype(BF16)) * (1.0 / jnp.sum(e, axis=1, keepdims=True))


def _fox_block(q, k, v, cc, cr, off):
    bq = q.shape[0]
    assert k.shape[0] == off + bq
    s = _dot((q * (HD ** -0.5)).astype(BF16), k.astype(BF16), NT) + cc - cr
    diag = jnp.where(_iota((bq, bq), 1) <= _iota((bq, bq), 0), s[:, off:], -1e30)
    s = jnp.concatenate([s[:, :off], diag], axis=1) if off else diag
    return _softmax_times(s, v)


ONE_BUFFER = pl.Buffered(1)


def _pcol(t, cb):
    return pl.BlockSpec((t, HD), lambda h, cb=cb: (0, cb + h), pipeline_mode=ONE_BUFFER)


def _smcol(t):
    return pl.BlockSpec((t, HD), lambda h: (0, SM), pipeline_mode=ONE_BUFFER)


def _head(t):
    return pl.BlockSpec((t, HD), lambda h: (0, h), pipeline_mode=ONE_BUFFER)


def _small(n):
    return pl.BlockSpec((n, HD), lambda h: (0, 0), pipeline_mode=ONE_BUFFER)


def _fox_fwd(p, fb, qg, kg, bq):
    t = p.shape[0]

    def body(fq, fk, fv, sm, fb_r, qg_r, kg_r, o, qn_s, cc_s):
        h = pl.program_id(0)
        qn, kn, ccol, crow = _fox_prep(fq[...], fk[...], sm[...], fb_r[...], qg_r[...], kg_r[...], h)
        qn_s[...] = qn
        cc_s[...] = ccol
        knb = kn.astype(BF16)
        vb = fv[...].astype(BF16)
        for i in range(t // bq):
            rows, ext = pl.ds(i * bq, bq), (i + 1) * bq
            o[rows, :] = _fox_block(qn_s[rows, :], knb[:ext], vb[:ext], cc_s[rows, :], crow[:, :ext], i * bq).astype(o.dtype)

    return pl.pallas_call(
        body, grid=(NF,), name="fox_fwd",
        in_specs=[_pcol(t, FQ), _pcol(t, FK), _pcol(t, FV), _smcol(t), _small(1), _small(1), _small(1)],
        out_specs=_head(t), out_shape=SDS((t, NF * HD), BF16),
        scratch_shapes=[pltpu.VMEM((t, HD), F32), pltpu.VMEM((t, 1), F32)],
        compiler_params=_cp("parallel"),
    )(p, p, p, p, fb, qg, kg)


def _fox_bwd(p, fb, qg, kg, dmix, bq, deps=()):
    t = p.shape[0]

    def body(*refs):
        fq, fk, fv, sm, fb_r, qg_r, kg_r, do = refs[:8]
        dfq, dfk, dfv, dsm, dfb, dqg, dkg, qn_s, cc_s, dqn_s, dcc_s, dkn_s, dv_s, dcr_s = refs[8 + len(deps):]
        h = pl.program_id(0)
        qn, kn, ccol, crow = _fox_prep(fq[...], fk[...], sm[...], fb_r[...], qg_r[...], kg_r[...], h)
        qn_s[...] = qn
        cc_s[...] = ccol
        v = fv[...]
        dkn_s[...] = jnp.zeros_like(dkn_s)
        dv_s[...] = jnp.zeros_like(dv_s)
        dcr_s[...] = jnp.zeros_like(dcr_s)

        for i in range(t // bq):
            rows, ext = pl.ds(i * bq, bq), (i + 1) * bq
            _, vjp = jax.vjp(lambda a, b, c, d, e, off=i * bq: _fox_block(a, b, c, d, e, off),
                             qn_s[rows, :], kn[:ext], v[:ext], cc_s[rows, :], crow[:, :ext])
            dq, dk, dv, dcc, dcr = vjp(do[rows, :])
            dqn_s[rows, :] = dq
            dcc_s[rows, :] = dcc
            dkn_s[:ext, :] += dk
            dv_s[:ext, :] += dv
            dcr_s[:, :ext] += dcr
        _, prep_vjp = jax.vjp(lambda a, b, c, d, e, f: _fox_prep(a, b, c, d, e, f, h),
                              fq[...], fk[...], sm[...], fb_r[...], qg_r[...], kg_r[...])
        g_fq, g_fk, g_sm, g_fb, g_qg, g_kg = prep_vjp((dqn_s[...], dkn_s[...], dcc_s[...], dcr_s[...]))
        dfq[...] = g_fq.astype(dfq.dtype)
        dfk[...] = g_fk.astype(dfk.dtype)
        dfv[...] = dv_s[...].astype(dfv.dtype)

        @pl.when(h == 0)
        def _():
            for r in (dsm, dfb, dqg, dkg):
                r[...] = jnp.zeros_like(r)

        dsm[...] += g_sm
        dfb[...] += g_fb
        dqg[...] += g_qg
        dkg[...] += g_kg

    head = _head(t)
    return pl.pallas_call(
        body, grid=(NF,), name="fox_bwd",
        in_specs=[_pcol(t, FQ), _pcol(t, FK), _pcol(t, FV), _smcol(t), _small(1), _small(1), _small(1), head]
        + [ANY_SPEC] * len(deps),
        out_specs=[head, head, head, _small(t), _small(1), _small(1), _small(1)],
        out_shape=[SDS((t, NF * HD), BF16)] * 3 + [SDS((t, HD), F32)] + [SDS((1, HD), F32)] * 3,
        scratch_shapes=[pltpu.VMEM((t, HD), F32), pltpu.VMEM((t, 1), F32), pltpu.VMEM((t, HD), F32),
                        pltpu.VMEM((t, 1), F32), pltpu.VMEM((t, HD), F32), pltpu.VMEM((t, HD), F32),
                        pltpu.VMEM((1, t), F32)],
        compiler_params=_cp("arbitrary"),
    )(p, p, p, p, fb, qg, kg, dmix, *deps)


def _mem_attn(mq, mk, mv, qg, kg):
    s = _dot((_rms(mq, qg) * (HD ** -0.5)).astype(BF16), _rms(mk, kg).astype(BF16), NT)
    return _softmax_times(s, mv)


def _mem_fwd(p, mkv, qg, kg):
    t, ml = p.shape[0], mkv.shape[0]

    def body(mq, mk, mv, qg_r, kg_r, o):
        o[...] = _mem_attn(mq[...], mk[...], mv[...], qg_r[...], kg_r[...]).astype(o.dtype)

    return pl.pallas_call(
        body, grid=(NM,), name="mem_fwd",
        in_specs=[_pcol(t, MQ), pl.BlockSpec((ml, HD), lambda h: (0, h)), pl.BlockSpec((ml, HD), lambda h: (0, NM + h)),
                  _small(1), _small(1)],
        out_specs=pl.BlockSpec((t, HD), lambda h: (0, h)), out_shape=SDS((t, NM * HD), BF16),
        compiler_params=_cp("parallel"),
    )(p, mkv, mkv, qg, kg)


def _mem_bwd(p, mkv, qg, kg, dmix, deps=()):
    t, ml = p.shape[0], mkv.shape[0]

    def body(*refs):
        mq, mk, mv, qg_r, kg_r, do = refs[:6]
        dmq, dmk, dmv, dqg, dkg = refs[6 + len(deps):]
        _, vjp = jax.vjp(_mem_attn, mq[...], mk[...], mv[...], qg_r[...], kg_r[...])
        g_q, g_k, g_v, g_qg, g_kg = vjp(do[...])
        dmq[...] = g_q.astype(dmq.dtype)
        dmk[...] = g_k
        dmv[...] = g_v

        @pl.when(pl.program_id(0) == 0)
        def _():
            dqg[...] = jnp.zeros_like(dqg)
            dkg[...] = jnp.zeros_like(dkg)

        dqg[...] += g_qg
        dkg[...] += g_kg

    return pl.pallas_call(
        body, grid=(NM,), name="mem_bwd",
        in_specs=[_pcol(t, MQ), pl.BlockSpec((ml, HD), lambda h: (0, h)), pl.BlockSpec((ml, HD), lambda h: (0, NM + h)),
                  _small(1), _small(1), pl.BlockSpec((t, HD), lambda h: (0, NF + NG + h))] + [ANY_SPEC] * len(deps),
        out_specs=[pl.BlockSpec((t, HD), lambda h: (0, h)), pl.BlockSpec((ml, HD), lambda h: (0, h)),
                   pl.BlockSpec((ml, HD), lambda h: (0, h)), _small(1), _small(1)],
        out_shape=[SDS((t, NM * HD), BF16), SDS((ml, NM * HD), F32), SDS((ml, NM * HD), F32),
                   SDS((1, HD), F32), SDS((1, HD), F32)],
        compiler_params=_cp("arbitrary"),
    )(p, mkv, mkv, qg, kg, dmix, *deps)


def _shift_down(x, s):
    if s == 0:
        return x
    return jnp.where(_iota(x.shape, 0) >= s, pltpu.roll(x, s, 0), 0.0)


def _shift_up(x, s):
    if s == 0:
        return x
    n = x.shape[0]
    return jnp.where(_iota(x.shape, 0) < n - s, pltpu.roll(x, n - s, 0), 0.0)


@jax.custom_vjp
def _conv4(x, w0, w1, w2, w3):
    return w0 * _shift_down(x, 3) + w1 * _shift_down(x, 2) + w2 * _shift_down(x, 1) + w3 * x


def _conv4_fwd(x, w0, w1, w2, w3):
    return _conv4(x, w0, w1, w2, w3), (x, w0, w1, w2, w3)


def _conv4_bwd(res, dy):
    x, w0, w1, w2, w3 = res
    dx = w0 * _shift_up(dy, 3) + w1 * _shift_up(dy, 2) + w2 * _shift_up(dy, 1) + w3 * dy
    dws = tuple(jnp.sum(dy * _shift_down(x, 3 - k), axis=0, keepdims=True) for k in range(4))
    return (dx,) + dws


_conv4.defvjp(_conv4_fwd, _conv4_bwd)


HALO = 8


def _gdn_prep(gq, gk, gv, sm, taps, alog, dtb, h):
    q, k, v = [_silu(_conv4(x, *taps[4 * j:4 * j + 4]))[HALO:] for j, x in enumerate((gq, gk, gv))]
    q = q * lax.rsqrt(jnp.sum(q * q, axis=-1, keepdims=True) + NORM_EPS) * (HD ** -0.5)
    k = k * lax.rsqrt(jnp.sum(k * k, axis=-1, keepdims=True) + NORM_EPS)
    g = _lane_pick(-jnp.exp(alog) * _softplus(sm + dtb), L_GA + h)
    beta = _lane_pick(_sigmoid(sm), L_GB + h)
    return q, k, v, g, beta


def _split(x, n):
    parts, rest = [], x
    for i in range(n):
        parts.append(rest.astype(BF16))
        if i + 1 < n:
            rest = rest - parts[-1].astype(F32)
    return parts


def _raw_dot(a, b, form):
    lead = a.ndim - 2
    ca, cb = {"nn": (1, 0), "nt": (1, 1), "tn": (0, 0)}[form]
    batch = ((0,), (0,)) if lead else ((), ())
    return lax.dot_general(a, b, (((ca + lead,), (cb + lead,)), batch), preferred_element_type=F32)


def _pdot_impl(a, b, form, mode):
    if mode == "1":
        return _raw_dot(a.astype(BF16), b.astype(BF16), form)
    if mode == "3":
        (ah, al), (bh, bl) = _split(a, 2), _split(b, 2)
        return _raw_dot(ah, bh, form) + (_raw_dot(al, bh, form) + _raw_dot(ah, bl, form))
    if mode == "xa":
        return sum(_raw_dot(a.astype(BF16), t, form) for t in reversed(_split(b, 3)))
    return sum(_raw_dot(t, b.astype(BF16), form) for t in reversed(_split(a, 3)))


@functools.partial(jax.custom_vjp, nondiff_argnums=(2, 3))
def _pdot(a, b, form, mode):
    return _pdot_impl(a, b, form, mode)


def _pdot_fwd(a, b, form, mode):
    return _pdot_impl(a, b, form, mode), (a, b)


def _pdot_bwd(form, mode, res, ct):
    a, b = res
    da_args, db_args = {"nn": ((ct, b, "nt"), (a, ct, "tn")), "nt": ((ct, b, "nn"), (ct, a, "tn")),
                        "tn": ((b, ct, "nt"), (a, ct, "nn"))}[form]

    def side(args, exact):
        if mode in ("1", "3"):
            return mode
        return "xa" if args[0] is exact else "xb"

    if mode == "xa":
        return jnp.zeros_like(a), _pdot_impl(*db_args, side(db_args, a))
    if mode == "xb":
        return _pdot_impl(*da_args, side(da_args, b)), jnp.zeros_like(b)
    return _pdot_impl(*da_args, mode), _pdot_impl(*db_args, mode)


_pdot.defvjp(_pdot_fwd, _pdot_bwd)

GDN_QK, GDN_INV, GDN_SCAN = "1", "1", "1"


@jax.custom_vjp
def _tri_inv(low):
    eye = (_iota((CHUNK, CHUNK), 0) == _iota((CHUNK, CHUNK), 1)).astype(F32)
    inv = eye - low
    pw = low
    for _ in range(5):
        pw = _pdot_impl(pw, pw, "nn", GDN_INV)
        inv = inv + _pdot_impl(inv, pw, "nn", GDN_INV)
    return inv


def _tri_inv_fwd(low):
    inv = _tri_inv(low)
    return inv, inv


def _tri_inv_bwd(inv, ct):
    return (-_pdot_impl(_pdot_impl(inv, ct, "tn", GDN_INV), inv, "nt", GDN_INV),)


_tri_inv.defvjp(_tri_inv_fwd, _tri_inv_bwd)


def _gdn_intra(q, k, v, g, beta):
    n = q.shape[0]
    r, c = _iota((CHUNK, CHUNK), 0), _iota((CHUNK, CHUNK), 1)
    tril, strict = r >= c, r > c
    trilf = jnp.broadcast_to(tril.astype(F32), (n, CHUNK, CHUNK))
    gcm = _pdot(trilf, jnp.broadcast_to(g, (n, CHUNK, CHUNK)), "nn", "xa")
    gcf = _pdot(trilf, jnp.broadcast_to(g, (n, CHUNK, HD)), "nn", "xa")
    lane0 = (_iota((1, 1, CHUNK), 2) == 0).astype(F32)
    gcr = _pdot(jnp.ones((n, CHUNK, CHUNK), F32), gcm * lane0, "nt", "xa")
    decay = jnp.where(tril, jnp.exp(jnp.where(tril, gcm - gcr, 0.0)), 0.0)
    egc = jnp.exp(gcf)
    kb = k * beta
    low = jnp.where(strict, _pdot(kb, k, "nt", GDN_QK) * decay, 0.0)
    inv = _tri_inv(low)
    u = _pdot(inv, v * beta, "nn", GDN_INV)
    w = _pdot(inv, kb * egc, "nn", GDN_INV)
    at = jnp.where(tril, _pdot(q, k, "nt", GDN_QK) * decay, 0.0)
    gl = jnp.sum(jnp.broadcast_to(g, (n, CHUNK, HD)), axis=1, keepdims=True)
    return u, w, q * egc, at, k * jnp.exp(gl - gcf), gl


def _gdn_step(s, u, w, qg, at, kd, gl):
    vn = u - _pdot(w, s, "nn", GDN_SCAN)
    o = _pdot(qg, s, "nn", GDN_SCAN) + _pdot(at, vn, "nn", GDN_SCAN)
    s2 = s * jnp.exp(gl) + _pdot(kd, vn, "tn", GDN_SCAN)
    return o, s2


SCAN_HEADS = 3


def _gdn_chunked_scratch(nc):
    big = pltpu.VMEM((nc, CHUNK, HD), F32)
    return [big, big, big, pltpu.VMEM((nc, CHUNK, 1), F32), pltpu.VMEM((nc, CHUNK, 1), F32)]


def _gdn_term_shapes(nc):
    return [(nc, CHUNK, HD), (nc, CHUNK, HD), (nc, CHUNK, HD), (nc, CHUNK, CHUNK), (nc, CHUNK, HD), (nc, 1, HD)]


def _per_head(shape, heads=None, one_buffer=True):
    lead = (None,) if heads is None else (heads,)
    return pl.BlockSpec(lead + tuple(shape), lambda h: (h,) + (0,) * len(shape),
                        pipeline_mode=ONE_BUFFER if one_buffer else None)


def _gdn_in_specs(t):
    cw = lambda cb: pl.BlockSpec((4, HD), lambda h, cb=cb: (0, cb + h))
    return [_pcol(t, GQ), _pcol(t, GK), _pcol(t, GV), _smcol(t), cw(0), cw(NG), cw(2 * NG), _small(1), _small(1)]


def _taps(wq, wk, wv):
    return tuple(w[k:k + 1, :] for w in (wq, wk, wv) for k in range(4))


def _prep_rows(t):
    return min(t, 256)


def _gdn_pad(srcs, pads):
    for src, pad in zip(srcs, pads):
        pad[0:HALO, :] = jnp.zeros((HALO, HD), F32)
        pad[HALO:, :] = src[...]


def _gdn_stage(pads, sm, taps, al, db, h, chunked):
    t = sm.shape[0]
    rows = _prep_rows(t)
    per = rows // CHUNK

    def tile(i, carry):
        r0 = pl.multiple_of(i * rows, rows)
        vals = _gdn_prep(*[p[pl.ds(r0, rows + HALO), :] for p in pads], sm[pl.ds(r0, rows), :], taps, al, db, h)
        for v, r in zip(vals, chunked):
            r[pl.ds(i * per, per)] = v.reshape(per, CHUNK, v.shape[-1])
        return carry

    lax.fori_loop(0, t // rows, tile, 0)


def _gdn_intra_all(chunked, intra):
    nc = chunked[0].shape[0]
    grp_n = math.gcd(nc, GROUP)

    def grp(i, carry):
        sl = pl.ds(pl.multiple_of(i * grp_n, grp_n), grp_n)
        for r, val in zip(intra, _gdn_intra(*[c[sl] for c in chunked])):
            r[sl] = val
        return carry

    lax.fori_loop(0, nc // grp_n, grp, 0)


def _gdn_fwd(pa, pb, conv, alog, dtb):
    t = pa.shape[0]
    nc = t // CHUNK
    terms = _gdn_term_shapes(nc)

    def body(gq, gk, gv, sm, wq, wk, wv, al, db, *rest):
        h = pl.program_id(0)
        intra, chunked, pads = rest[:6], rest[6:11], rest[11:]
        _gdn_pad((gq, gk, gv), pads)
        _gdn_stage(pads, sm, _taps(wq, wk, wv), al[...], db[...], h, chunked)
        _gdn_intra_all(chunked, intra)

    return pl.pallas_call(
        body, grid=(NG,), name="gdn_fwd", in_specs=_gdn_in_specs(t),
        out_specs=[_per_head(sh, one_buffer=False) for sh in terms], out_shape=[SDS((NG,) + sh, F32) for sh in terms],
        scratch_shapes=_gdn_chunked_scratch(nc) + [pltpu.VMEM((t + HALO, HD), F32)] * 3, compiler_params=_cp("parallel"),
    )(pa, pa, pa, pb, conv, conv, conv, alog, dtb)


def _gdn_scan(terms_in):
    nc = terms_in[0].shape[1]
    terms = _gdn_term_shapes(nc)

    def body(*refs):
        intra, o, states = refs[:6], refs[6], refs[7]

        def step(c, ss):
            rows = pl.ds(pl.multiple_of(c * CHUNK, CHUNK), CHUNK)
            new = []
            for hh in range(SCAN_HEADS):
                states[hh, c] = ss[hh]
                oc, s2 = _gdn_step(ss[hh], *[r[hh, c] for r in intra])
                o[rows, hh * HD:(hh + 1) * HD] = oc
                new.append(s2)
            return tuple(new)

        lax.fori_loop(0, nc, step, tuple(jnp.zeros((HD, HD), F32) for _ in range(SCAN_HEADS)))

    return pl.pallas_call(
        body, grid=(NG // SCAN_HEADS,), name="gdn_scan", in_specs=[_per_head(sh, SCAN_HEADS) for sh in terms],
        out_specs=[pl.BlockSpec((nc * CHUNK, SCAN_HEADS * HD), lambda h: (0, h), pipeline_mode=ONE_BUFFER),
                   _per_head((nc, HD, HD), SCAN_HEADS)],
        out_shape=[SDS((nc * CHUNK, NG * HD), F32), SDS((NG, nc, HD, HD), F32)], compiler_params=_cp("parallel"),
    )(*terms_in)


def _gdn_bwd_scan(saved, do_raw):
    nc = saved[0].shape[1]
    terms = _gdn_term_shapes(nc)

    def body(*refs):
        intra, states, do, outs = refs[:6], refs[6], refs[7], refs[8:]

        def bwd(i, dss):
            c = nc - 1 - i
            rows = pl.ds(pl.multiple_of(c * CHUNK, CHUNK), CHUNK)
            new = []
            for hh in range(SCAN_HEADS):
                _, vjp = jax.vjp(_gdn_step, states[hh, c], *[r[hh, c] for r in intra])
                grads = vjp((do[rows, hh * HD:(hh + 1) * HD], dss[hh]))
                for r, gval in zip(outs, grads[1:]):
                    r[hh, c] = gval
                new.append(grads[0])
            return tuple(new)

        lax.fori_loop(0, nc, bwd, tuple(jnp.zeros((HD, HD), F32) for _ in range(SCAN_HEADS)))

    return pl.pallas_call(
        body, grid=(NG // SCAN_HEADS,), name="gdn_bwd_scan",
        in_specs=[_per_head(sh, SCAN_HEADS) for sh in terms] + [_per_head((nc, HD, HD), SCAN_HEADS)]
        + [pl.BlockSpec((nc * CHUNK, SCAN_HEADS * HD), lambda h: (0, h), pipeline_mode=ONE_BUFFER)],
        out_specs=[_per_head(sh, SCAN_HEADS) for sh in terms],
        out_shape=[SDS((NG,) + sh, F32) for sh in terms], compiler_params=_cp("parallel"),
    )(*saved, do_raw)


def _gdn_bwd(pa, pb, conv, alog, dtb, dterms):
    t = pa.shape[0]
    nc = t // CHUNK
    terms = _gdn_term_shapes(nc)

    def body(*refs):
        gq, gk, gv, sm, wq, wk, wv, al, db = refs[:9]
        dintra = refs[9:15]
        dgq, dgk, dgv, dsm, dwq, dwk, dwv, dal, ddb = refs[15:24]
        chunked, pads, dpads, dsm_s = refs[24:29], refs[29:32], refs[32:35], refs[35]
        h = pl.program_id(0)
        taps = _taps(wq, wk, wv)
        _gdn_pad((gq, gk, gv), pads)
        _gdn_stage(pads, sm, taps, al[...], db[...], h, chunked)
        grp_n = math.gcd(nc, GROUP)

        def grp(i, carry):
            sl = pl.ds(pl.multiple_of(i * grp_n, grp_n), grp_n)
            _, vjp = jax.vjp(_gdn_intra, *[r[sl] for r in chunked])
            for r, gval in zip(chunked, vjp(tuple(r[sl] for r in dintra))):
                r[sl] = gval
            return carry

        lax.fori_loop(0, nc // grp_n, grp, 0)

        rows = _prep_rows(t)
        per = rows // CHUNK
        for r in dpads:
            r[...] = jnp.zeros_like(r)

        def tile(i, small):
            r0 = pl.multiple_of(i * rows, rows)
            win = pl.ds(r0, rows + HALO)
            _, vjp = jax.vjp(lambda *a: _gdn_prep(*a, h), *[p[win, :] for p in pads], sm[pl.ds(r0, rows), :],
                             taps, al[...], db[...])
            grads = vjp(tuple(r[pl.ds(i * per, per)].reshape(rows, r.shape[-1]) for r in chunked))
            for r, gval in zip(dpads, grads[:3]):
                r[win, :] += gval
            dsm_s[pl.ds(r0, rows), :] = grads[3]
            return jax.tree.map(jnp.add, small, (grads[4], grads[5], grads[6]))

        zero = jnp.zeros((1, HD), F32)
        dtaps, g_al, g_db = lax.fori_loop(0, t // rows, tile, ((zero,) * 12, zero, zero))
        for r, dpad in zip((dgq, dgk, dgv), dpads):
            r[...] = dpad[HALO:, :].astype(r.dtype)
        for j, r in enumerate((dwq, dwk, dwv)):
            for k in range(4):
                r[k:k + 1, :] = dtaps[4 * j + k]

        @pl.when(h == 0)
        def _():
            for r in (dsm, dal, ddb):
                r[...] = jnp.zeros_like(r)

        dsm[...] += dsm_s[...]
        dal[...] += g_al
        ddb[...] += g_db

    head = _head(t)
    taps = pl.BlockSpec((4, HD), lambda h: (0, h))
    return pl.pallas_call(
        body, grid=(NG,), name="gdn_bwd", in_specs=_gdn_in_specs(t) + [_per_head(sh) for sh in terms],
        out_specs=[head, head, head, _small(t), taps, taps, taps, _small(1), _small(1)],
        out_shape=[SDS((t, NG * HD), BF16)] * 3 + [SDS((t, HD), F32)] + [SDS((4, NG * HD), F32)] * 3 + [SDS((1, HD), F32)] * 2,
        scratch_shapes=_gdn_chunked_scratch(nc) + [pltpu.VMEM((t + HALO, HD), F32)] * 6 + [pltpu.VMEM((t, HD), F32)],
        compiler_params=_cp("arbitrary"),
    )(pa, pa, pa, pb, conv, conv, conv, alog, dtb, *dterms)


def _gdn_post(o, z, gain):
    return (jnp.concatenate(
        [_rms(o[:, h * HD:(h + 1) * HD], gain) * _silu(z[:, h * HD:(h + 1) * HD]) for h in range(NG)], axis=1),)


def _place():
    return lax.axis_index("x"), lax.axis_index("y"), lax.axis_index("c")


def _all_gather(name, shard):
    def body(x_ref, out_ref, send_sems, recv_sems, local_sem):
        x, y, c = _place()
        me, sibling = (x, y, c), (x, y, 1 - c)
        chips = [(1 - x, y), (x, 1 - y), (1 - x, 1 - y)]

        def blk(px, py, pc):
            return out_ref.at[4 * px + 2 * py + pc]

        def copy(k, block, to, src=None):
            return pltpu.make_async_remote_copy(
                src_ref=blk(*block) if src is None else src, dst_ref=blk(*block),
                send_sem=send_sems.at[k], recv_sem=recv_sems.at[k], device_id=to, device_id_type=MESH)

        mine = pltpu.make_async_copy(x_ref, blk(*me), local_sem)
        mine.start()
        first = [copy(0, me, sibling, src=x_ref)]
        first += [copy(1 + j, me, (*chip, c), src=x_ref) for j, chip in enumerate(chips)]
        for cp in first:
            cp.start()
        passed = [copy(4 + j, (*chip, c), sibling) for j, chip in enumerate(chips)]
        for j, chip in enumerate(chips):
            copy(1 + j, (*chip, c), me).wait_recv()
            passed[j].start()
        copy(0, sibling, me).wait_recv()
        for j, chip in enumerate(chips):
            copy(4 + j, (*chip, 1 - c), me).wait_recv()
        for cp in first + passed:
            cp.wait_send()
        mine.wait()

    return pl.pallas_call(
        body, name=name, out_shape=SDS((N_DEV,) + shard.shape, shard.dtype),
        in_specs=[pl.BlockSpec(memory_space=pltpu.HBM)], out_specs=pl.BlockSpec(memory_space=pltpu.HBM),
        scratch_shapes=[pltpu.SemaphoreType.DMA((7,)), pltpu.SemaphoreType.DMA((7,)), pltpu.SemaphoreType.DMA],
    )(shard)


def _scatter_exchange(name, full):
    def body(g_ref, out_ref, send_sems, recv_sems, local_sem):
        x, y, c = _place()
        me = 4 * x + 2 * y + c
        mine = pltpu.make_async_copy(g_ref.at[me], out_ref.at[me], local_sem)
        mine.start()
        sends, recvs = [], []
        for k in range(1, N_DEV):
            px = 1 - x if k & 4 else x
            py = 1 - y if k & 2 else y
            pc = 1 - c if k & 1 else c
            peer = 4 * px + 2 * py + pc
            sends.append(pltpu.make_async_remote_copy(
                src_ref=g_ref.at[peer], dst_ref=out_ref.at[me], send_sem=send_sems.at[k - 1],
                recv_sem=recv_sems.at[k - 1], device_id=(px, py, pc), device_id_type=MESH))
            recvs.append(pltpu.make_async_remote_copy(
                src_ref=g_ref.at[me], dst_ref=out_ref.at[peer], send_sem=send_sems.at[k - 1],
                recv_sem=recv_sems.at[k - 1], device_id=(px, py, pc), device_id_type=MESH))
        for cp in sends:
            cp.start()
        for cp in recvs:
            cp.wait_recv()
        for cp in sends:
            cp.wait_send()
        mine.wait()

    return pl.pallas_call(
        body, name=name, out_shape=SDS(full.shape, full.dtype),
        in_specs=[pl.BlockSpec(memory_space=pltpu.HBM)], out_specs=pl.BlockSpec(memory_space=pltpu.HBM),
        scratch_shapes=[pltpu.SemaphoreType.DMA((7,)), pltpu.SemaphoreType.DMA((7,)), pltpu.SemaphoreType.DMA],
    )(full)


def _sum_blocks(name, parts):
    _, r, c = parts.shape
    tr = 64 if r % 64 == 0 else r

    def body(x, o):
        acc = x[0].astype(F32)
        for d in range(1, N_DEV):
            acc = acc + x[d].astype(F32)
        o[...] = acc

    return pl.pallas_call(
        body, grid=(r // tr,), name=name, in_specs=[pl.BlockSpec((N_DEV, tr, c), lambda i: (0, i, 0))],
        out_specs=pl.BlockSpec((tr, c), lambda i: (i, 0)), out_shape=SDS((r, c), F32), compiler_params=_cp("parallel"),
    )(parts)


def _reduce_scatter(name, full):
    return _sum_blocks(name + "_sum", _scatter_exchange(name, full))


def _all_reduce_small(name, x, reduce):
    m_per, n = x.shape

    def body(x_ref, out_ref, send_sems, recv_sems, local_sem):
        px, py, pc = _place()
        me, sibling = (px, py, pc), (px, py, 1 - pc)
        chips = [(1 - px, py), (px, 1 - py), (1 - px, 1 - py)]
        buf = out_ref

        def rows(qx, qy, qc):
            return buf.at[pl.ds((4 * qx + 2 * qy + qc) * m_per, m_per), :]

        def copy(k, block, to, src=None):
            return pltpu.make_async_remote_copy(
                src_ref=rows(*block) if src is None else src, dst_ref=rows(*block),
                send_sem=send_sems.at[k], recv_sem=recv_sems.at[k], device_id=to, device_id_type=MESH)

        mine = pltpu.make_async_copy(x_ref, rows(*me), local_sem)
        mine.start()
        first = [copy(0, me, sibling, src=x_ref)]
        first += [copy(1 + j, me, (*chip, pc), src=x_ref) for j, chip in enumerate(chips)]
        for cp in first:
            cp.start()
        passed = [copy(4 + j, (*chip, pc), sibling) for j, chip in enumerate(chips)]
        for j, chip in enumerate(chips):
            copy(1 + j, (*chip, pc), me).wait_recv()
            passed[j].start()
        copy(0, sibling, me).wait_recv()
        for j, chip in enumerate(chips):
            copy(4 + j, (*chip, 1 - pc), me).wait_recv()
        for cp in first + passed:
            cp.wait_send()
        mine.wait()

    gathered = pl.pallas_call(
        body, name=name, out_shape=SDS((N_DEV * m_per, n), x.dtype),
        in_specs=[pl.BlockSpec(memory_space=pltpu.VMEM)], out_specs=pl.BlockSpec(memory_space=pltpu.VMEM),
        scratch_shapes=[pltpu.SemaphoreType.DMA((7,)), pltpu.SemaphoreType.DMA((7,)), pltpu.SemaphoreType.DMA],
    )(x)
    if not reduce:
        return gathered
    return _sum_blocks(name + "_sum", gathered.reshape(N_DEV, m_per, n))


HBM_SPEC = pl.BlockSpec(memory_space=pltpu.HBM)
SEM_SPEC = pl.BlockSpec(memory_space=pltpu.SEMAPHORE)
EFFECT = pltpu.SideEffectType.DATAFLOW_SIDE_EFFECTING


def _copies_start(name, bufs, n_remote, n_local, build, deps):
    nb, nd = len(bufs), len(deps)
    sem_shapes = [pltpu.SemaphoreType.DMA((n_remote,)), pltpu.SemaphoreType.DMA((n_remote,))]
    if n_local:
        sem_shapes.append(pltpu.SemaphoreType.DMA((n_local,)))
    ns = len(sem_shapes)

    def body(*refs):
        sems = refs[nb + nd:nb + nd + ns]
        remote, local = build(refs[:nb], *sems, *([None] * (3 - ns)))
        for cp in local + remote:
            cp.start()
        refs[-1][...] = jnp.zeros((8, HD), F32)

    outs = pl.pallas_call(
        body, name=name,
        out_shape=(*sem_shapes, *[pltpu.HBM(b.shape, b.dtype) for b in bufs], SDS((8, HD), F32)),
        in_specs=[HBM_SPEC] * nb + [ANY_SPEC] * nd,
        out_specs=(*[SEM_SPEC] * ns, *[HBM_SPEC] * nb, pl.BlockSpec(memory_space=pltpu.VMEM)),
        input_output_aliases={i: ns + i for i in range(nb)},
        compiler_params=pltpu.CompilerParams(has_side_effects=EFFECT),
    )(*[pltpu.with_memory_space_constraint(b, pltpu.HBM) for b in bufs], *deps)
    return list(outs[:ns]), list(outs[ns:ns + nb]), outs[-1]


def _copies_wait(name, bufs, sems, build, after):
    nb, ns = len(bufs), len(sems)

    def body(*refs):
        remote, local = build(refs[:nb], *refs[nb:nb + ns], *([None] * (3 - ns)))
        for cp in local:
            cp.wait()
        for cp in remote:
            cp.wait_send()
            cp.wait_recv()

    outs = pl.pallas_call(
        body, name=name, out_shape=tuple(pltpu.HBM(b.shape, b.dtype) for b in bufs),
        in_specs=[HBM_SPEC] * nb + [SEM_SPEC] * ns + [ANY_SPEC] * len(after), out_specs=tuple([HBM_SPEC] * nb),
        input_output_aliases={i: i for i in range(nb)},
        compiler_params=pltpu.CompilerParams(has_side_effects=EFFECT),
    )(*bufs, *sems, *after)
    return list(outs)


def _remote(src, dst, send, recv, k, to):
    return pltpu.make_async_remote_copy(src_ref=src, dst_ref=dst, send_sem=send.at[k], recv_sem=recv.at[k],
                                        device_id=to, device_id_type=MESH)


class _Gather:
    def __init__(self, name, shards, deps):
        self.name, self.n = name, len(shards)
        lands = [lax.empty((N_DEV,) + s.shape, s.dtype) for s in shards]
        self.sems1, bufs, self.token = _copies_start(
            name + "_s1", list(shards) + lands, 4 * self.n, self.n, self._stage1(range(self.n)), deps)
        self.shards, self.lands, self.sems2 = bufs[:self.n], bufs[self.n:], {}

    def _stage1(self, idxs):
        def build(refs, send, recv, loc):
            x, y, c = _place()
            me = 4 * x + 2 * y + c
            targets = [(x, y, 1 - c), (1 - x, y, c), (x, 1 - y, c), (1 - x, 1 - y, c)]
            remote, local = [], []
            for pos, i in enumerate(idxs):
                src, land = refs[pos], refs[len(idxs) + pos]
                local.append(pltpu.make_async_copy(src, land.at[me], loc.at[i]))
                remote += [_remote(src, land.at[me], send, recv, 4 * i + k, to) for k, to in enumerate(targets)]
            return remote, local
        return build

    @staticmethod
    def _stage2(refs, send, recv, loc):
        x, y, c = _place()
        remote = []
        for pos, land in enumerate(refs):
            for j, (cx, cy) in enumerate([(1 - x, y), (x, 1 - y), (1 - x, 1 - y)]):
                blk = land.at[4 * cx + 2 * cy + c]
                remote.append(_remote(blk, blk, send, recv, 3 * pos + j, (x, y, 1 - c)))
        return remote, []

    def pass_on(self, idxs, after):
        tag, m = "".join(map(str, idxs)), len(idxs)
        bufs = _copies_wait(f"{self.name}_w1_{tag}", [self.shards[i] for i in idxs] + [self.lands[i] for i in idxs],
                            self.sems1, self._stage1(idxs), after)
        self.sems2[tag], lands, token = _copies_start(f"{self.name}_s2_{tag}", bufs[m:], 3 * m, 0, self._stage2, ())
        for pos, i in enumerate(idxs):
            self.lands[i] = lands[pos]
        return [token]

    def get(self, idxs, after):
        tag = "".join(map(str, idxs))
        return _copies_wait(f"{self.name}_w2_{tag}", [self.lands[i] for i in idxs], self.sems2[tag], self._stage2, after)


def _rows_tile(r, row_bytes, target=1 << 20):
    tr = r
    while tr % 32 == 0 and tr * row_bytes > target:
        tr //= 2
    return tr


def _pair_add(name, g, got, c):
    _, r, cols = g.shape
    tr = _rows_tile(r, cols * 2)

    def body(s, a, b, o):
        o[...] = (a[...].astype(F32) + b[...].astype(F32)).astype(o.dtype)

    return pl.pallas_call(
        body, name=name, out_shape=SDS((4, r, cols), g.dtype),
        grid_spec=pltpu.PrefetchScalarGridSpec(
            num_scalar_prefetch=1, grid=(4, r // tr),
            in_specs=[pl.BlockSpec((None, tr, cols), lambda j, i, s: (2 * j + s[0], i, 0)),
                      pl.BlockSpec((None, tr, cols), lambda j, i, s: (j, i, 0))],
            out_specs=pl.BlockSpec((None, tr, cols), lambda j, i, s: (j, i, 0))),
        compiler_params=_cp("parallel", "parallel"),
    )(c.reshape(1), g, got)


def _quad_sum(name, part, got, chip):
    _, r, cols = part.shape
    tr = _rows_tile(r, cols * 4)

    def body(s, a, b1, b2, b3, o):
        o[...] = ((a[...].astype(F32) + b1[...].astype(F32)) + b2[...].astype(F32)) + b3[...].astype(F32)

    blk = lambda k: pl.BlockSpec((None, tr, cols), lambda i, s, k=k: (jnp.bitwise_xor(s[0], k), i, 0))
    return pl.pallas_call(
        body, name=name, out_shape=SDS((r, cols), F32),
        grid_spec=pltpu.PrefetchScalarGridSpec(
            num_scalar_prefetch=1, grid=(r // tr,), in_specs=[blk(0), blk(1), blk(2), blk(3)],
            out_specs=pl.BlockSpec((tr, cols), lambda i, s: (i, 0))),
        compiler_params=_cp("parallel"),
    )(chip.reshape(1), part, got, got, got)


class _Scatter:
    def __init__(self, name, grads, deps):
        self.name, self.n = name, len(grads)
        got = [lax.empty((4,) + g.shape[1:], g.dtype) for g in grads]
        self.sems, bufs, self.token = _copies_start(name + "_s1", list(grads) + got, 4 * self.n, 0, self._stage1, deps)
        self.grads, self.got = bufs[:self.n], bufs[self.n:]

    def _stage1(self, refs, send, recv, loc):
        x, y, c = _place()
        remote = []
        for i in range(self.n):
            remote += [_remote(refs[i].at[2 * j + 1 - c], refs[self.n + i].at[j], send, recv, 4 * i + j, (x, y, 1 - c))
                       for j in range(4)]
        return remote, []

    def _stage2(self, refs, send, recv, loc):
        x, y, c = _place()
        remote = []
        for i in range(self.n):
            for k in (1, 2, 3):
                tx = 1 - x if k & 2 else x
                ty = 1 - y if k & 1 else y
                remote.append(_remote(refs[i].at[2 * tx + ty], refs[self.n + i].at[2 * x + y], send, recv,
                                      3 * i + k - 1, (tx, ty, c)))
        return remote, []

    def mid(self, after):
        bufs = _copies_wait(self.name + "_w1", self.grads + self.got, self.sems, self._stage1, after)
        c = lax.axis_index("c").astype(jnp.int32)
        parts = [_pair_add(f"{self.name}_add{i}", bufs[i], bufs[self.n + i], c) for i in range(self.n)]
        got = [lax.empty(p.shape, p.dtype) for p in parts]
        self.sems, bufs, self.token = _copies_start(self.name + "_s2", parts + got, 3 * self.n, 0, self._stage2, ())
        self.parts, self.got = bufs[:self.n], bufs[self.n:]

    def end(self, after):
        bufs = _copies_wait(self.name + "_w2", self.parts + self.got, self.sems, self._stage2, after)
        chip = (2 * lax.axis_index("x") + lax.axis_index("y")).astype(jnp.int32)
        return [_quad_sum(f"{self.name}_sum{i}", bufs[i], bufs[self.n + i], chip) for i in range(self.n)]


def _adamw(w, g, m, v):
    m = ADAM_B1 * m + (1.0 - ADAM_B1) * g
    v = ADAM_B2 * v + (1.0 - ADAM_B2) * (g * g)
    m_hat = m / (1.0 - ADAM_B1 ** ADAM_STEP)
    v_hat = v / (1.0 - ADAM_B2 ** ADAM_STEP)
    return -ADAM_LR * (m_hat / (jnp.sqrt(v_hat) + ADAM_EPS) + ADAM_WD * w), m, v


def _adamw_call(name, w, g, m, v):
    r, c = w.shape
    tm = 64 if r % 64 == 0 else r
    return _rowwise(name, _adamw, [w, g, m, v], [], [(c, F32)] * 3, tm)


_IN_COLS = 5906


def _perm_in(w):
    pad = jnp.zeros((w.shape[0], 2 * HALF - _IN_COLS), w.dtype)
    return (jnp.concatenate([w[:, 2310:4614], w[:, 4614:5382]], axis=1),
            jnp.concatenate([w[:, :2304], w[:, 5394:5906], w[:, 2304:2310], w[:, 5382:5394], pad], axis=1))


def _unperm_in(ga, gb):
    return jnp.concatenate([gb[:, :2304], gb[:, 2816:2822], ga[:, :2304], ga[:, 2304:3072], gb[:, 2822:2834],
                            gb[:, 2304:2816]], axis=1)


def _lanes(v, at):
    return jnp.pad(v, ((0, 0), (at, HD - at - v.shape[1])))


_PACK = ("norm_mix", "mem_norm", "norm_ffn", "gdn_conv", "fox_q_norm", "fox_k_norm", "gdn_out_norm", "mem_q_norm",
         "mem_k_norm", "fox_f_bias", "gdn_a_log", "gdn_dt_bias", "loss")


def _pack(vals):
    parts = [vals[n].reshape(-1, HD) for n in _PACK]
    used = sum(p.shape[0] for p in parts)
    buf = jnp.concatenate(parts + [jnp.zeros((-used % 8, HD), F32)], axis=0)
    return buf, [(n, p.shape[0]) for n, p in zip(_PACK, parts)]


def _unpack(buf, layout):
    out, at = {}, 0
    for n, rows in layout:
        out[n] = buf[at:at + rows]
        at += rows
    return out


def kernel(x, mem, norm_mix, w_in, fox_f_bias, fox_q_norm, fox_k_norm, gdn_conv, gdn_a_log, gdn_dt_bias, gdn_out_norm, mem_norm, w_mem_kv, mem_q_norm, mem_k_norm, w_out, norm_ffn, w_gate_up, w_down, loss_target, m_norm_mix, m_w_in, m_fox_f_bias, m_fox_q_norm, m_fox_k_norm, m_gdn_conv, m_gdn_a_log, m_gdn_dt_bias, m_gdn_out_norm, m_mem_norm, m_w_mem_kv, m_mem_q_norm, m_mem_k_norm, m_w_out, m_norm_ffn, m_w_gate_up, m_w_down, v_norm_mix, v_w_in, v_fox_f_bias, v_fox_q_norm, v_fox_k_norm, v_gdn_conv, v_gdn_a_log, v_gdn_dt_bias, v_gdn_out_norm, v_mem_norm, v_w_mem_kv, v_mem_q_norm, v_mem_k_norm, v_w_out, v_norm_ffn, v_w_gate_up, v_w_down):
    args = dict(locals())
    d = x.shape[2]
    me = 4 * lax.axis_index("x") + 2 * lax.axis_index("y") + lax.axis_index("c")

    cshard = gdn_conv[0].shape[1]
    conv_pad = jnp.pad(gdn_conv[0], ((0, 4), (0, 3 * HD - cshard)))
    conv_all = _all_reduce_small("ag_conv", conv_pad, False).reshape(N_DEV, 8, 3 * HD)[:, :4, :cshard]
    conv_all = conv_all.transpose(1, 0, 2).reshape(4, N_DEV * cshard)
    w_in_a, w_in_b = _perm_in(w_in[0])
    comm = _StepComm({"in_b": [w_in_b], "in_a": [w_in_a], "kv_out": [w_mem_kv[0], w_out[0]], "gate_up": [w_gate_up[0]],
                      "down": [w_down[0]]}, [conv_all])

    grad_x, loss_local, small_grads = _local_step(
        x[0], mem[0], loss_target[0], norm_mix, fox_f_bias, fox_q_norm, fox_k_norm, gdn_a_log, gdn_dt_bias,
        gdn_out_norm, mem_norm, mem_q_norm, mem_k_norm, norm_ffn, conv_all, comm)

    red = comm.finish([grad_x])
    grads = {"w_down": red["ffn"][0], "w_gate_up": red["ffn"][1], "w_out": red["a"][1], "w_mem_kv": red["b"][1],
             "w_in": _unperm_in(red["a"][0], red["b"][0])}
    small_grads["loss"] = jnp.broadcast_to(loss_local, (1, HD))
    packed, layout = _pack(small_grads)
    small = _unpack(_all_reduce_small("ar_small", packed, True), layout)
    loss = small["loss"][0, 0]
    six = {"fox_f_bias": L_FF, "gdn_a_log": L_GA, "gdn_dt_bias": L_GA}
    for n, rows_n in layout[:-1]:
        gsm = small[n]
        if n == "gdn_conv":
            gsm = lax.dynamic_slice(gsm.reshape(4, N_DEV * cshard), (0, me * cshard), (4, cshard))[None]
        elif n in six:
            gsm = gsm[:, six[n]:six[n] + 6]
        else:
            gsm = gsm.reshape(1, rows_n * HD)
        grads[n] = gsm

    names = ['norm_mix', 'w_in', 'fox_f_bias', 'fox_q_norm', 'fox_k_norm', 'gdn_conv', 'gdn_a_log', 'gdn_dt_bias',
             'gdn_out_norm', 'mem_norm', 'w_mem_kv', 'mem_q_norm', 'mem_k_norm', 'w_out', 'norm_ffn', 'w_gate_up', 'w_down']
    big = ("w_in", "w_mem_kv", "w_out", "w_gate_up", "w_down")
    delta, new_m, new_v = {}, {}, {}
    for n in big:
        delta[n], new_m[n], new_v[n] = [a[None] for a in _adamw_call(
            "adamw_" + n, args[n][0], grads[n], args["m_" + n][0], args["v_" + n][0])]
        grads[n] = grads[n][None]

    def flat(a):
        a = a.reshape(1, -1)
        return jnp.pad(a, ((0, 0), (0, -a.shape[1] % HD))).reshape(-1, HD)

    smalls = [n for n in names if n not in big]
    pk = lambda pre: jnp.concatenate([flat(grads[n] if pre == "g" else args[pre + n]) for n in smalls], axis=0)
    cat = [pk(""), pk("g"), pk("m_"), pk("v_")]
    padr = -cat[0].shape[0] % 8
    cat = [jnp.pad(a, ((0, padr), (0, 0))) for a in cat]
    res = _adamw_call("adamw_small", *cat)
    at = 0
    for n in smalls:
        shape = args[n].shape
        size = math.prod(shape)
        nrow = -(-size // HD)
        for dst, src in zip((delta, new_m, new_v), res):
            dst[n] = src[at:at + nrow].reshape(-1)[:size].reshape(shape)
        at += nrow

    return (loss, grad_x[None], *[grads[n] for n in names], *[delta[n] for n in names],
            *[new_m[n] for n in names], *[new_v[n] for n in names])


class _StepComm:
    def __init__(self, shard_groups, after):
        self.groups, shards = {}, []
        for key, ws in shard_groups.items():
            self.groups[key] = list(range(len(shards), len(shards) + len(ws)))
            shards += [w.astype(BF16) for w in ws]
        self.gather = _Gather("ag", shards, after)
        self.passed, self.scatters = set(), {}

    def start_deps(self):
        return [self.gather.token]

    def pass_on(self, key, after):
        self.passed.add(key)
        return self.gather.pass_on(self.groups[key], after)

    def weights(self, key, after):
        if key not in self.passed:
            after = self.pass_on(key, after)
        return self.gather.get(self.groups[key], after)

    def send(self, tag, grads):
        blocks = [g if g.ndim == 3 else g.reshape(N_DEV, g.shape[0] // N_DEV, g.shape[1]) for g in grads]
        self.scatters[tag] = _Scatter("rs_" + tag, blocks, ())
        return [self.scatters[tag].token]

    def mid(self, tag, after):
        self.scatters[tag].mid(after)
        return [self.scatters[tag].token]

    def finish(self, after):
        return {tag: sc.end(after) for tag, sc in self.scatters.items()}


def _local_step(xs, ms, tgt, norm_mix, fox_f_bias, fox_q_norm, fox_k_norm, gdn_a_log, gdn_dt_bias, gdn_out_norm,
                mem_norm, mem_q_norm, mem_k_norm, norm_ffn, conv_all, comm):
    t, d = xs.shape
    bq = min(t, 256)
    fb, alog, dtb = _lanes(fox_f_bias, L_FF), _lanes(gdn_a_log, L_GA), _lanes(gdn_dt_bias, L_GA)
    flat = lambda w: w.reshape(-1, w.shape[-1])

    rms1 = lambda a, g: (_rms(a, g),)
    (u,) = _rowwise("norm_mix", rms1, [xs], [norm_mix], [(d, BF16)], min(t, 256), deps=comm.start_deps())
    w_in_b = flat(comm.weights("in_b", [u])[0])
    pb = _matmul("proj_in_b", u, w_in_b, NN, F32, 1024, 768)
    o_fox = _fox_fwd(pb, fb, fox_q_norm, fox_k_norm, bq)
    w_in_a = flat(comm.weights("in_a", [o_fox])[0])
    pa = _matmul("proj_in_a", u, w_in_a, NN, F32, 1024, 768)
    gdn_terms = _gdn_fwd(pa, pb, conv_all, alog, dtb)
    o_gdn_raw, gdn_states = _gdn_scan(gdn_terms)
    gdn_saved = list(gdn_terms) + [gdn_states]
    zrow = (pa, NG * HD, GZ * HD // (NG * HD))
    (o_gdn,) = _rowwise("gdn_post", _gdn_post, [o_gdn_raw, zrow], [gdn_out_norm], [(NG * HD, BF16)], min(t, 256))
    w_kv_all, w_out_all = [flat(w) for w in comm.weights("kv_out", [o_gdn])]
    (mem_n,) = _rowwise("norm_mem", rms1, [ms], [mem_norm], [(d, BF16)], ms.shape[0])
    mkv = _matmul("proj_mem", mem_n, w_kv_all, NN, F32, 256, 512)
    o_mem = _mem_fwd(pb, mkv, mem_q_norm, mem_k_norm)
    deps = comm.pass_on("gate_up", [o_mem])
    mix = jnp.concatenate([o_fox, o_gdn, o_mem], axis=1)
    h1, h1n = _proj_out_norm(mix, w_out_all, xs, norm_ffn, deps)
    (wgu,) = comm.weights("gate_up", [h1n])
    ffw = wgu.shape[2]
    gu, act = _ffn_up(h1n, wgu.reshape(2, 4, d, ffw))
    w_down_all = flat(comm.weights("down", [act])[0])
    dy, dyb, lsum = _ffn_down_loss(act, w_down_all, h1, tgt)
    loss_local = (0.5 / d) * jnp.sum(lsum[::8, ::HD])

    dgu = _ffn_down_bwd(dyb, w_down_all.reshape(4, ffw, d), gu).reshape(8, t, ffw)
    g_w_down = _matmul("grad_w_down", act, dyb, TN, BF16, 512, 2048)
    dh1n = _ffn_up_bwd_x(dgu, wgu)
    g_w_gu = _ffn_up_bwd_w(h1n, dgu)
    deps = comm.send("ffn", [g_w_down, g_w_gu])
    rms2 = lambda a, g: (_rms(a, g), a)
    dh1, dh1b, g_norm_ffn = _rowwise_vjp("norm_ffn_bwd", rms2, [h1], [norm_ffn], [dh1n, dy], [(F32, BF16)],
                                         min(t, 256), deps=deps)

    dmix = _matmul("proj_out_bwd_x", dh1b, w_out_all, NT, F32, 1024, 1024)
    g_w_out = _matmul("grad_w_out", mix, dh1b, TN, BF16, 1024, 2048)
    deps = comm.mid("ffn", [dmix, g_w_out])
    do_raw, dgz, g_gon = _rowwise_vjp("gdn_post_bwd", _gdn_post, [o_gdn_raw, zrow], [gdn_out_norm],
                                      [(dmix, NG * HD, 1)], [F32, BF16], min(t, 256), deps=deps)
    dterms = _gdn_bwd_scan(gdn_saved, do_raw)
    dgq, dgk, dgv, dsm_gdn, dwq, dwk, dwv, g_alog, g_dtb = _gdn_bwd(pa, pb, conv_all, alog, dtb, dterms)
    dp_a = jnp.concatenate([dgq, dgk, dgv, dgz], axis=1)
    g_w_in_a = _matmul("grad_w_in_a", u, dp_a, TN, BF16, 512, 3072)
    deps = comm.send("a", [g_w_in_a, g_w_out])
    dmq, dmk, dmv, g_mqn, g_mkn = _mem_bwd(pb, mkv, mem_q_norm, mem_k_norm, dmix, deps=deps)
    dmkv = jnp.concatenate([dmk, dmv], axis=1).astype(BF16)
    dmem_n = _matmul("proj_mem_bwd_x", dmkv, w_kv_all, NT, F32, 256, 512)
    g_w_kv = _matmul("grad_w_kv", mem_n, dmkv, TN, BF16, 512, 512)
    g_mem_norm = _rowwise_vjp("norm_mem_bwd", rms1, [ms], [mem_norm], [dmem_n], [], ms.shape[0])[0]
    deps = comm.mid("a", [g_mem_norm, g_w_kv])
    dfq, dfk, dfv, dsm_fox, g_fb, g_fqn, g_fkn = _fox_bwd(pb, fb, fox_q_norm, fox_k_norm, dmix, bq, deps=deps)
    dp_b = jnp.concatenate([dfq, dfk, dfv, dmq, (dsm_fox + dsm_gdn).astype(BF16), jnp.zeros((t, HD), BF16)], axis=1)
    g_w_in_b = _matmul("grad_w_in_b", u, dp_b, TN, BF16, 512, 3072)
    deps = comm.send("b", [g_w_in_b, g_w_kv])
    du_a = _matmul("proj_in_bwd_a", dp_a, w_in_a, NT, F32, 1024, 1024, deps=deps)
    deps = comm.mid("b", [du_a])
    du = _matmul("proj_in_bwd_b", dp_b, w_in_b, NT, F32, 1024, 1024, residual=du_a, deps=deps)
    grad_x, g_norm_mix = _rowwise_vjp("norm_mix_bwd", rms2, [xs], [norm_mix], [du, dh1], [F32], min(t, 256))

    small_grads = {
        "norm_mix": g_norm_mix, "mem_norm": g_mem_norm, "norm_ffn": g_norm_ffn,
        "gdn_conv": jnp.concatenate([dwq, dwk, dwv], axis=1),
        "fox_q_norm": g_fqn, "fox_k_norm": g_fkn, "gdn_out_norm": g_gon, "mem_q_norm": g_mqn, "mem_k_norm": g_mkn,
        "fox_f_bias": g_fb, "gdn_a_log": g_alog, "gdn_dt_bias": g_dtb}
    return grad_x, loss_local, small_grads
```

```python
import functools
import math

import jax
import jax.numpy as jnp
from jax import lax
from jax.experimental import pallas as pl
from jax.experimental.pallas import tpu as pltpu

F32 = jnp.float32
BF16 = jnp.bfloat16
HI = lax.Precision.HIGHEST
SDS = jax.ShapeDtypeStruct

N_DEV = 8
HD = 128
NF, NG, NM = 6, 6, 4
CHUNK = 64
GROUP = 16
NORM_EPS = 1e-6
GQ, GK, GV, GZ = 0, 6, 12, 18
FQ, FK, FV, MQ, SM = 0, 6, 12, 18, 22
HALF = 24 * HD
L_FF, L_GA, L_GB = 0, 6, 12
VMEM_LIMIT = 56 * 1024 * 1024

ADAM_LR, ADAM_B1, ADAM_B2, ADAM_EPS, ADAM_WD, ADAM_STEP = 0.001, 0.9, 0.999, 1e-08, 0.01, 10

NN = (((1,), (0,)), ((), ()))
NT = (((1,), (1,)), ((), ()))
TN = (((0,), (0,)), ((), ()))
MESH = pl.DeviceIdType.MESH


def _cp(*sem):
    return pltpu.CompilerParams(dimension_semantics=tuple(sem) if sem else None, vmem_limit_bytes=VMEM_LIMIT)


def _dot(a, b, dims=NN):
    return lax.dot_general(a, b, dims, preferred_element_type=F32)


def _bdot(a, b):
    return _dot(a.astype(BF16), b.astype(BF16))


def _iota(shape, axis):
    return lax.broadcasted_iota(jnp.int32, shape, axis)


def _rms(x, gain):
    return x * lax.rsqrt(jnp.mean(x * x, axis=-1, keepdims=True) + NORM_EPS) * gain


def _sigmoid(x):
    return 0.5 * jnp.tanh(0.5 * x) + 0.5


def _silu(x):
    return x * _sigmoid(x)


def _softplus(x):
    return jnp.maximum(x, 0.0) + jnp.log(1.0 + jnp.exp(-jnp.abs(x)))


def _lane_pick(x, lane):
    oh = (_iota((1, x.shape[-1]), 1) == lane).astype(F32)
    return jnp.sum(x * oh, axis=-1, keepdims=True)


def _cumsum_rows(x):
    tril = (_iota((HD, HD), 0) >= _iota((HD, HD), 1)).astype(F32)
    carry = jnp.zeros((1, x.shape[1]), F32)
    outs = []
    for b in range(x.shape[0] // HD):
        blk = x[b * HD:(b + 1) * HD]
        outs.append(jnp.dot(tril, blk, precision=HI, preferred_element_type=F32) + carry)
        carry = carry + jnp.sum(blk, axis=0, keepdims=True)
    return jnp.concatenate(outs, axis=0)


def _row_spec(r, tm):
    if isinstance(r, tuple):
        arr, width, cb = r
        return arr, pl.BlockSpec((tm, width), lambda i, cb=cb: (i, cb))
    return r, pl.BlockSpec((tm, r.shape[1]), lambda i: (i, 0))


ANY_SPEC = pl.BlockSpec(memory_space=pl.ANY)


def _rowwise(name, fn, rows, consts, outs, tm, deps=()):
    arrs, specs = zip(*[_row_spec(r, tm) for r in rows])
    n_rows = arrs[0].shape[0]
    nr, nc, nd = len(rows), len(consts), len(deps)

    def body(*refs):
        res = fn(*[r[...] for r in refs[:nr + nc]])
        for o, v in zip(refs[nr + nc + nd:], res):
            o[...] = v.astype(o.dtype)

    return pl.pallas_call(
        body, grid=(n_rows // tm,), name=name,
        in_specs=list(specs) + [pl.BlockSpec(c.shape, lambda i: (0, 0)) for c in consts] + [ANY_SPEC] * nd,
        out_specs=[pl.BlockSpec((tm, w), lambda i: (i, 0)) for w, _ in outs],
        out_shape=[SDS((n_rows, w), dt) for w, dt in outs],
        compiler_params=_cp("parallel"),
    )(*arrs, *consts, *deps)


def _rowwise_vjp(name, fn, rows, consts, cts, grad_dtypes, tm, deps=()):
    arrs, specs = zip(*[_row_spec(r, tm) for r in rows])
    ct_arrs, ct_specs = zip(*[_row_spec(r, tm) for r in cts])
    n_rows = arrs[0].shape[0]
    nr, nc, nct, nd = len(rows), len(consts), len(cts), len(deps)
    plan = [(j, dt) for j, dts in enumerate(grad_dtypes) for dt in (dts if isinstance(dts, tuple) else (dts,))]
    ng = len(plan)
    widths = [specs[j].block_shape[1] for j, _ in plan]
    grad_dtypes = [dt for _, dt in plan]

    def body(*refs):
        vals = [r[...].astype(F32) for r in refs[:nr + nc]]
        ctv = tuple(r[...].astype(F32) for r in refs[nr + nc:nr + nc + nct])
        _, vjp = jax.vjp(fn, *vals)
        grads = vjp(ctv)
        outs = refs[nr + nc + nct + nd:]
        for o, (j, _) in zip(outs[:ng], plan):
            o[...] = grads[j].astype(o.dtype)

        @pl.when(pl.program_id(0) == 0)
        def _():
            for o in outs[ng:]:
                o[...] = jnp.zeros_like(o)

        for o, g in zip(outs[ng:], grads[nr:]):
            o[...] += g

    return pl.pallas_call(
        body, grid=(n_rows // tm,), name=name,
        in_specs=list(specs) + [pl.BlockSpec(c.shape, lambda i: (0, 0)) for c in consts] + list(ct_specs)
        + [ANY_SPEC] * nd,
        out_specs=[pl.BlockSpec((tm, w), lambda i: (i, 0)) for w in widths]
        + [pl.BlockSpec(c.shape, lambda i: (0, 0)) for c in consts],
        out_shape=[SDS((n_rows, w), dt) for w, dt in zip(widths, grad_dtypes)] + [SDS(c.shape, F32) for c in consts],
        compiler_params=_cp("arbitrary"),
    )(*arrs, *consts, *ct_arrs, *deps)


def _tile(n, pref):
    t = min(n, pref)
    while n % t or (t % HD and t != n):
        t -= 1
    return t


def _matmul(name, a, b, dims, out_dtype, tm, tn, residual=None, deps=()):
    ta, tb = dims == TN, dims == NT
    m = a.shape[1] if ta else a.shape[0]
    k = a.shape[0] if ta else a.shape[1]
    n = b.shape[0] if tb else b.shape[1]
    tm, tn = _tile(m, tm), _tile(n, tn)

    def body(*refs):
        acc = _dot(refs[0][...], refs[1][...], dims)
        if residual is not None:
            acc = acc + refs[2][...]
        refs[-1][...] = acc.astype(out_dtype)

    in_specs = [pl.BlockSpec((k, tm), lambda i, j: (0, i)) if ta else pl.BlockSpec((tm, k), lambda i, j: (i, 0)),
                pl.BlockSpec((tn, k), lambda i, j: (j, 0)) if tb else pl.BlockSpec((k, tn), lambda i, j: (0, j))]
    ops = [a, b]
    if residual is not None:
        in_specs.append(pl.BlockSpec((tm, tn), lambda i, j: (i, j)))
        ops.append(residual)
    in_specs += [ANY_SPEC] * len(deps)
    ops += list(deps)
    return pl.pallas_call(
        body, grid=(m // tm, n // tn), name=name, in_specs=in_specs,
        out_specs=pl.BlockSpec((tm, tn), lambda i, j: (i, j)), out_shape=SDS((m, n), out_dtype),
        compiler_params=_cp("parallel", "parallel"),
    )(*ops)


def _proj_out_norm(mix, w_out, xs, gain, deps):
    t, k = mix.shape
    d = w_out.shape[1]
    tm = _tile(t, 512)

    def body(*refs):
        a, b, x, g = refs[:4]
        h1, h1n = refs[4 + len(deps):]
        acc = _dot(a[...], b[...]) + x[...]
        h1[...] = acc
        h1n[...] = _rms(acc, g[...]).astype(BF16)

    return pl.pallas_call(
        body, grid=(t // tm,), name="proj_out",
        in_specs=[pl.BlockSpec((tm, k), lambda i: (i, 0)), pl.BlockSpec((k, d), lambda i: (0, 0)),
                  pl.BlockSpec((tm, d), lambda i: (i, 0)), pl.BlockSpec((1, d), lambda i: (0, 0))] + [ANY_SPEC] * len(deps),
        out_specs=[pl.BlockSpec((tm, d), lambda i: (i, 0))] * 2, out_shape=[SDS((t, d), F32), SDS((t, d), BF16)],
        compiler_params=_cp("parallel"),
    )(mix, w_out, xs, gain, *deps)


def _ffn_up(h1n, wgu):
    t, d = h1n.shape
    w = wgu.shape[3]
    tm = _tile(t, 512)

    def body(a, b, gu, act):
        x = a[...]
        g = _dot(x, b[0])
        u = _dot(x, b[1])
        gu[0] = g.astype(BF16)
        gu[1] = u.astype(BF16)
        act[...] = (_silu(g) * u).astype(BF16)

    return pl.pallas_call(
        body, grid=(4, t // tm), name="ffn_up",
        in_specs=[pl.BlockSpec((tm, d), lambda j, i: (i, 0)), pl.BlockSpec((2, None, d, w), lambda j, i: (0, j, 0, 0))],
        out_specs=[pl.BlockSpec((2, None, tm, w), lambda j, i: (0, j, i, 0)), pl.BlockSpec((tm, w), lambda j, i: (i, j))],
        out_shape=[SDS((2, 4, t, w), BF16), SDS((t, 4 * w), BF16)],
        compiler_params=_cp("parallel", "parallel"),
    )(h1n, wgu)


def _ffn_down_loss(act, wdown, h1, target):
    t, f = act.shape
    d = wdown.shape[1]
    tm, tn = _tile(t, 1024), _tile(d, 512)

    def body(a, b, h, tg, dy, dyb, ls):
        e = _dot(a[...], b[...]) + h[...] - tg[...]
        g = e * (1.0 / d)
        dy[...] = g
        dyb[...] = g.astype(BF16)
        ls[...] = jnp.broadcast_to(jnp.sum(e * e), (8, HD))

    return pl.pallas_call(
        body, grid=(t // tm, d // tn), name="ffn_down_loss",
        in_specs=[pl.BlockSpec((tm, f), lambda i, j: (i, 0)), pl.BlockSpec((f, tn), lambda i, j: (0, j)),
                  pl.BlockSpec((tm, tn), lambda i, j: (i, j)), pl.BlockSpec((tm, tn), lambda i, j: (i, j))],
        out_specs=[pl.BlockSpec((tm, tn), lambda i, j: (i, j)), pl.BlockSpec((tm, tn), lambda i, j: (i, j)),
                   pl.BlockSpec((8, HD), lambda i, j: (i, j))],
        out_shape=[SDS((t, d), F32), SDS((t, d), BF16), SDS((8 * (t // tm), HD * (d // tn)), F32)],
        compiler_params=_cp("parallel", "parallel"),
    )(act, wdown, h1, target)


def _ffn_down_bwd(dyb, wdown4, gu):
    t, d = dyb.shape
    w = wdown4.shape[1]
    tm = _tile(t, 512)

    def body(a, b, gu_ref, out):
        da = _dot(a[...], b[...], NT)
        g = gu_ref[0].astype(F32)
        u = gu_ref[1].astype(F32)
        s = _sigmoid(g)
        out[0] = (da * u * (s * (1.0 + g * (1.0 - s)))).astype(BF16)
        out[1] = (da * g * s).astype(BF16)

    return pl.pallas_call(
        body, grid=(4, t // tm), name="ffn_down_bwd",
        in_specs=[pl.BlockSpec((tm, d), lambda j, i: (i, 0)), pl.BlockSpec((None, w, d), lambda j, i: (j, 0, 0)),
                  pl.BlockSpec((2, None, tm, w), lambda j, i: (0, j, i, 0))],
        out_specs=pl.BlockSpec((2, None, tm, w), lambda j, i: (0, j, i, 0)),
        out_shape=SDS((2, 4, t, w), BF16),
        compiler_params=_cp("parallel", "parallel"),
    )(dyb, wdown4, gu)


def _ffn_up_bwd_x(dgu, wgu):
    _, t, w = dgu.shape
    d = wgu.shape[1]
    tm = _tile(t, 512)

    def body(a, b, out):
        @pl.when(pl.program_id(1) == 0)
        def _():
            out[...] = jnp.zeros_like(out)
        out[...] += _dot(a[...], b[...], NT)

    return pl.pallas_call(
        body, grid=(t // tm, 8), name="ffn_up_bwd_x",
        in_specs=[pl.BlockSpec((None, tm, w), lambda i, j: (j, i, 0)), pl.BlockSpec((None, d, w), lambda i, j: (j, 0, 0))],
        out_specs=pl.BlockSpec((tm, d), lambda i, j: (i, 0)), out_shape=SDS((t, d), F32),
        compiler_params=_cp("parallel", "arbitrary"),
    )(dgu, wgu)


def _ffn_up_bwd_w(h1n, dgu):
    _, t, w = dgu.shape
    d = h1n.shape[1]
    tm = _tile(d, 512)

    def body(a, b, out):
        out[...] = _dot(a[...], b[...], TN).astype(BF16)

    return pl.pallas_call(
        body, grid=(8, d // tm), name="ffn_up_bwd_w",
        in_specs=[pl.BlockSpec((t, tm), lambda j, i: (0, i)), pl.BlockSpec((None, t, w), lambda j, i: (j, 0, 0))],
        out_specs=pl.BlockSpec((None, tm, w), lambda j, i: (j, i, 0)), out_shape=SDS((8, d, w), BF16),
        compiler_params=_cp("parallel", "parallel"),
    )(h1n, dgu)


def _fox_prep(fq, fk, sm, fb, qg, kg, h):
    qn = _rms(fq, qg)
    kn = _rms(fk, kg)
    c = _cumsum_rows(-_softplus(-(sm + fb)))
    ccol = _lane_pick(c, L_FF + h)
    crow = jnp.sum(c.T * (_iota((HD, 1), 0) == L_FF + h).astype(F32), axis=0, keepdims=True)
    return qn, kn, ccol, crow


def _softmax_times(s, v):
    e = jnp.exp(s - lax.stop_gradient(jnp.max(s, axis=1, keepdims=True)))
    return _dot(e.astype(BF16), v.astype(BF16)) * (1.0 / jnp.sum(e, axis=1, keepdims=True))


def _fox_block(q, k, v, cc, cr, off):
    bq = q.shape[0]
    assert k.shape[0] == off + bq
    s = _dot((q * (HD ** -0.5)).astype(BF16), k.astype(BF16), NT) + cc - cr
    diag = jnp.where(_iota((bq, bq), 1) <= _iota((bq, bq), 0), s[:, off:], -1e30)
    s = jnp.concatenate([s[:, :off], diag], axis=1) if off else diag
    return _softmax_times(s, v)


ONE_BUFFER = pl.Buffered(1)


def _pcol(t, cb):
    return pl.BlockSpec((t, HD), lambda h, cb=cb: (0, cb + h), pipeline_mode=ONE_BUFFER)


def _smcol(t):
    return pl.BlockSpec((t, HD), lambda h: (0, SM), pipeline_mode=ONE_BUFFER)


def _head(t):
    return pl.BlockSpec((t, HD), lambda h: (0, h), pipeline_mode=ONE_BUFFER)


def _small(n):
    return pl.BlockSpec((n, HD), lambda h: (0, 0), pipeline_mode=ONE_BUFFER)


def _fox_fwd(p, fb, qg, kg, bq):
    t = p.shape[0]

    def body(fq, fk, fv, sm, fb_r, qg_r, kg_r, o, qn_s, cc_s):
        h = pl.program_id(0)
        qn, kn, ccol, crow = _fox_prep(fq[...], fk[...], sm[...], fb_r[...], qg_r[...], kg_r[...], h)
        qn_s[...] = qn
        cc_s[...] = ccol
        knb = kn.astype(BF16)
        vb = fv[...].astype(BF16)
        for i in range(t // bq):
            rows, ext = pl.ds(i * bq, bq), (i + 1) * bq
            o[rows, :] = _fox_block(qn_s[rows, :], knb[:ext], vb[:ext], cc_s[rows, :], crow[:, :ext], i * bq).astype(o.dtype)

    return pl.pallas_call(
        body, grid=(NF,), name="fox_fwd",
        in_specs=[_pcol(t, FQ), _pcol(t, FK), _pcol(t, FV), _smcol(t), _small(1), _small(1), _small(1)],
        out_specs=_head(t), out_shape=SDS((t, NF * HD), BF16),
        scratch_shapes=[pltpu.VMEM((t, HD), F32), pltpu.VMEM((t, 1), F32)],
        compiler_params=_cp("parallel"),
    )(p, p, p, p, fb, qg, kg)


def _fox_bwd(p, fb, qg, kg, dmix, bq, deps=()):
    t = p.shape[0]

    def body(*refs):
        fq, fk, fv, sm, fb_r, qg_r, kg_r, do = refs[:8]
        dfq, dfk, dfv, dsm, dfb, dqg, dkg, qn_s, cc_s, dqn_s, dcc_s, dkn_s, dv_s, dcr_s = refs[8 + len(deps):]
        h = pl.program_id(0)
        qn, kn, ccol, crow = _fox_prep(fq[...], fk[...], sm[...], fb_r[...], qg_r[...], kg_r[...], h)
        qn_s[...] = qn
        cc_s[...] = ccol
        v = fv[...]
        dkn_s[...] = jnp.zeros_like(dkn_s)
        dv_s[...] = jnp.zeros_like(dv_s)
        dcr_s[...] = jnp.zeros_like(dcr_s)

        for i in range(t // bq):
            rows, ext = pl.ds(i * bq, bq), (i + 1) * bq
            _, vjp = jax.vjp(lambda a, b, c, d, e, off=i * bq: _fox_block(a, b, c, d, e, off),
                             qn_s[rows, :], kn[:ext], v[:ext], cc_s[rows, :], crow[:, :ext])
            dq, dk, dv, dcc, dcr = vjp(do[rows, :])
            dqn_s[rows, :] = dq
            dcc_s[rows, :] = dcc
            dkn_s[:ext, :] += dk
            dv_s[:ext, :] += dv
            dcr_s[:, :ext] += dcr
        _, prep_vjp = jax.vjp(lambda a, b, c, d, e, f: _fox_prep(a, b, c, d, e, f, h),
                              fq[...], fk[...], sm[...], fb_r[...], qg_r[...], kg_r[...])
        g_fq, g_fk, g_sm, g_fb, g_qg, g_kg = prep_vjp((dqn_s[...], dkn_s[...], dcc_s[...], dcr_s[...]))
        dfq[...] = g_fq.astype(dfq.dtype)
        dfk[...] = g_fk.astype(dfk.dtype)
        dfv[...] = dv_s[...].astype(dfv.dtype)

        @pl.when(h == 0)
        def _():
            for r in (dsm, dfb, dqg, dkg):
                r[...] = jnp.zeros_like(r)

        dsm[...] += g_sm
        dfb[...] += g_fb
        dqg[...] += g_qg
        dkg[...] += g_kg

    head = _head(t)
    return pl.pallas_call(
        body, grid=(NF,), name="fox_bwd",
        in_specs=[_pcol(t, FQ), _pcol(t, FK), _pcol(t, FV), _smcol(t), _small(1), _small(1), _small(1), head]
        + [ANY_SPEC] * len(deps),
        out_specs=[head, head, head, _small(t), _small(1), _small(1), _small(1)],
        out_shape=[SDS((t, NF * HD), BF16)] * 3 + [SDS((t, HD), F32)] + [SDS((1, HD), F32)] * 3,
        scratch_shapes=[pltpu.VMEM((t, HD), F32), pltpu.VMEM((t, 1), F32), pltpu.VMEM((t, HD), F32),
                        pltpu.VMEM((t, 1), F32), pltpu.VMEM((t, HD), F32), pltpu.VMEM((t, HD), F32),
                        pltpu.VMEM((1, t), F32)],
        compiler_params=_cp("arbitrary"),
    )(p, p, p, p, fb, qg, kg, dmix, *deps)


def _mem_attn(mq, mk, mv, qg, kg):
    s = _dot((_rms(mq, qg) * (HD ** -0.5)).astype(BF16), _rms(mk, kg).astype(BF16), NT)
    return _softmax_times(s, mv)


def _mem_fwd(p, mkv, qg, kg):
    t, ml = p.shape[0], mkv.shape[0]

    def body(mq, mk, mv, qg_r, kg_r, o):
        o[...] = _mem_attn(mq[...], mk[...], mv[...], qg_r[...], kg_r[...]).astype(o.dtype)

    return pl.pallas_call(
        body, grid=(NM,), name="mem_fwd",
        in_specs=[_pcol(t, MQ), pl.BlockSpec((ml, HD), lambda h: (0, h)), pl.BlockSpec((ml, HD), lambda h: (0, NM + h)),
                  _small(1), _small(1)],
        out_specs=pl.BlockSpec((t, HD), lambda h: (0, h)), out_shape=SDS((t, NM * HD), BF16),
        compiler_params=_cp("parallel"),
    )(p, mkv, mkv, qg, kg)


def _mem_bwd(p, mkv, qg, kg, dmix, deps=()):
    t, ml = p.shape[0], mkv.shape[0]

    def body(*refs):
        mq, mk, mv, qg_r, kg_r, do = refs[:6]
        dmq, dmk, dmv, dqg, dkg = refs[6 + len(deps):]
        _, vjp = jax.vjp(_mem_attn, mq[...], mk[...], mv[...], qg_r[...], kg_r[...])
        g_q, g_k, g_v, g_qg, g_kg = vjp(do[...])
        dmq[...] = g_q.astype(dmq.dtype)
        dmk[...] = g_k
        dmv[...] = g_v

        @pl.when(pl.program_id(0) == 0)
        def _():
            dqg[...] = jnp.zeros_like(dqg)
            dkg[...] = jnp.zeros_like(dkg)

        dqg[...] += g_qg
        dkg[...] += g_kg

    return pl.pallas_call(
        body, grid=(NM,), name="mem_bwd",
        in_specs=[_pcol(t, MQ), pl.BlockSpec((ml, HD), lambda h: (0, h)), pl.BlockSpec((ml, HD), lambda h: (0, NM + h)),
                  _small(1), _small(1), pl.BlockSpec((t, HD), lambda h: (0, NF + NG + h))] + [ANY_SPEC] * len(deps),
        out_specs=[pl.BlockSpec((t, HD), lambda h: (0, h)), pl.BlockSpec((ml, HD), lambda h: (0, h)),
                   pl.BlockSpec((ml, HD), lambda h: (0, h)), _small(1), _small(1)],
        out_shape=[SDS((t, NM * HD), BF16), SDS((ml, NM * HD), F32), SDS((ml, NM * HD), F32),
                   SDS((1, HD), F32), SDS((1, HD), F32)],
        compiler_params=_cp("arbitrary"),
    )(p, mkv, mkv, qg, kg, dmix, *deps)


def _shift_down(x, s):
    if s == 0:
        return x
    return jnp.where(_iota(x.shape, 0) >= s, pltpu.roll(x, s, 0), 0.0)


def _shift_up(x, s):
    if s == 0:
        return x
    n = x.shape[0]
    return jnp.where(_iota(x.shape, 0) < n - s, pltpu.roll(x, n - s, 0), 0.0)


@jax.custom_vjp
def _conv4(x, w0, w1, w2, w3):
    return w0 * _shift_down(x, 3) + w1 * _shift_down(x, 2) + w2 * _shift_down(x, 1) + w3 * x


def _conv4_fwd(x, w0, w1, w2, w3):
    return _conv4(x, w0, w1, w2, w3), (x, w0, w1, w2, w3)


def _conv4_bwd(res, dy):
    x, w0, w1, w2, w3 = res
    dx = w0 * _shift_up(dy, 3) + w1 * _shift_up(dy, 2) + w2 * _shift_up(dy, 1) + w3 * dy
    dws = tuple(jnp.sum(dy * _shift_down(x, 3 - k), axis=0, keepdims=True) for k in range(4))
    return (dx,) + dws


_conv4.defvjp(_conv4_fwd, _conv4_bwd)


HALO = 8


def _gdn_prep(gq, gk, gv, sm, taps, alog, dtb, h):
    q, k, v = [_silu(_conv4(x, *taps[4 * j:4 * j + 4]))[HALO:] for j, x in enumerate((gq, gk, gv))]
    q = q * lax.rsqrt(jnp.sum(q * q, axis=-1, keepdims=True) + NORM_EPS) * (HD ** -0.5)
    k = k * lax.rsqrt(jnp.sum(k * k, axis=-1, keepdims=True) + NORM_EPS)
    g = _lane_pick(-jnp.exp(alog) * _softplus(sm + dtb), L_GA + h)
    beta = _lane_pick(_sigmoid(sm), L_GB + h)
    return q, k, v, g, beta


def _split(x, n):
    parts, rest = [], x
    for i in range(n):
        parts.append(rest.astype(BF16))
        if i + 1 < n:
            rest = rest - parts[-1].astype(F32)
    return parts


def _raw_dot(a, b, form):
    lead = a.ndim - 2
    ca, cb = {"nn": (1, 0), "nt": (1, 1), "tn": (0, 0)}[form]
    batch = ((0,), (0,)) if lead else ((), ())
    return lax.dot_general(a, b, (((ca + lead,), (cb + lead,)), batch), preferred_element_type=F32)


def _pdot_impl(a, b, form, mode):
    if mode == "1":
        return _raw_dot(a.astype(BF16), b.astype(BF16), form)
    if mode == "3":
        (ah, al), (bh, bl) = _split(a, 2), _split(b, 2)
        return _raw_dot(ah, bh, form) + (_raw_dot(al, bh, form) + _raw_dot(ah, bl, form))
    if mode == "xa":
        return sum(_raw_dot(a.astype(BF16), t, form) for t in reversed(_split(b, 3)))
    return sum(_raw_dot(t, b.astype(BF16), form) for t in reversed(_split(a, 3)))


@functools.partial(jax.custom_vjp, nondiff_argnums=(2, 3))
def _pdot(a, b, form, mode):
    return _pdot_impl(a, b, form, mode)


def _pdot_fwd(a, b, form, mode):
    return _pdot_impl(a, b, form, mode), (a, b)


def _pdot_bwd(form, mode, res, ct):
    a, b = res
    da_args, db_args = {"nn": ((ct, b, "nt"), (a, ct, "tn")), "nt": ((ct, b, "nn"), (ct, a, "tn")),
                        "tn": ((b, ct, "nt"), (a, ct, "nn"))}[form]

    def side(args, exact):
        if mode in ("1", "3"):
            return mode
        return "xa" if args[0] is exact else "xb"

    if mode == "xa":
        return jnp.zeros_like(a), _pdot_impl(*db_args, side(db_args, a))
    if mode == "xb":
        return _pdot_impl(*da_args, side(da_args, b)), jnp.zeros_like(b)
    return _pdot_impl(*da_args, mode), _pdot_impl(*db_args, mode)


_pdot.defvjp(_pdot_fwd, _pdot_bwd)

GDN_QK, GDN_INV, GDN_SCAN = "1", "1", "1"


@jax.custom_vjp
def _tri_inv(low):
    eye = (_iota((CHUNK, CHUNK), 0) == _iota((CHUNK, CHUNK), 1)).astype(F32)
    inv = eye - low
    pw = low
    for _ in range(5):
        pw = _pdot_impl(pw, pw, "nn", GDN_INV)
        inv = inv + _pdot_impl(inv, pw, "nn", GDN_INV)
    return inv


def _tri_inv_fwd(low):
    inv = _tri_inv(low)
    return inv, inv


def _tri_inv_bwd(inv, ct):
    return (-_pdot_impl(_pdot_impl(inv, ct, "tn", GDN_INV), inv, "nt", GDN_INV),)


_tri_inv.defvjp(_tri_inv_fwd, _tri_inv_bwd)


def _gdn_intra(q, k, v, g, beta):
    n = q.shape[0]
    r, c = _iota((CHUNK, CHUNK), 0), _iota((CHUNK, CHUNK), 1)
    tril, strict = r >= c, r > c
    trilf = jnp.broadcast_to(tril.astype(F32), (n, CHUNK, CHUNK))
    gcm = _pdot(trilf, jnp.broadcast_to(g, (n, CHUNK, CHUNK)), "nn", "xa")
    gcf = _pdot(trilf, jnp.broadcast_to(g, (n, CHUNK, HD)), "nn", "xa")
    lane0 = (_iota((1, 1, CHUNK), 2) == 0).astype(F32)
    gcr = _pdot(jnp.ones((n, CHUNK, CHUNK), F32), gcm * lane0, "nt", "xa")
    decay = jnp.where(tril, jnp.exp(jnp.where(tril, gcm - gcr, 0.0)), 0.0)
    egc = jnp.exp(gcf)
    kb = k * beta
    low = jnp.where(strict, _pdot(kb, k, "nt", GDN_QK) * decay, 0.0)
    inv = _tri_inv(low)
    u = _pdot(inv, v * beta, "nn", GDN_INV)
    w = _pdot(inv, kb * egc, "nn", GDN_INV)
    at = jnp.where(tril, _pdot(q, k, "nt", GDN_QK) * decay, 0.0)
    gl = jnp.sum(jnp.broadcast_to(g, (n, CHUNK, HD)), axis=1, keepdims=True)
    return u, w, q * egc, at, k * jnp.exp(gl - gcf), gl


def _gdn_step(s, u, w, qg, at, kd, gl):
    vn = u - _pdot(w, s, "nn", GDN_SCAN)
    o = _pdot(qg, s, "nn", GDN_SCAN) + _pdot(at, vn, "nn", GDN_SCAN)
    s2 = s * jnp.exp(gl) + _pdot(kd, vn, "tn", GDN_SCAN)
    return o, s2


SCAN_HEADS = 3


def _gdn_chunked_scratch(nc):
    big = pltpu.VMEM((nc, CHUNK, HD), F32)
    return [big, big, big, pltpu.VMEM((nc, CHUNK, 1), F32), pltpu.VMEM((nc, CHUNK, 1), F32)]


def _gdn_term_shapes(nc):
    return [(nc, CHUNK, HD), (nc, CHUNK, HD), (nc, CHUNK, HD), (nc, CHUNK, CHUNK), (nc, CHUNK, HD), (nc, 1, HD)]


def _per_head(shape, heads=None, one_buffer=True):
    lead = (None,) if heads is None else (heads,)
    return pl.BlockSpec(lead + tuple(shape), lambda h: (h,) + (0,) * len(shape),
                        pipeline_mode=ONE_BUFFER if one_buffer else None)


def _gdn_in_specs(t):
    cw = lambda cb: pl.BlockSpec((4, HD), lambda h, cb=cb: (0, cb + h))
    return [_pcol(t, GQ), _pcol(t, GK), _pcol(t, GV), _smcol(t), cw(0), cw(NG), cw(2 * NG), _small(1), _small(1)]


def _taps(wq, wk, wv):
    return tuple(w[k:k + 1, :] for w in (wq, wk, wv) for k in range(4))


def _prep_rows(t):
    return min(t, 256)


def _gdn_pad(srcs, pads):
    for src, pad in zip(srcs, pads):
        pad[0:HALO, :] = jnp.zeros((HALO, HD), F32)
        pad[HALO:, :] = src[...]


def _gdn_stage(pads, sm, taps, al, db, h, chunked):
    t = sm.shape[0]
    rows = _prep_rows(t)
    per = rows // CHUNK

    def tile(i, carry):
        r0 = pl.multiple_of(i * rows, rows)
        vals = _gdn_prep(*[p[pl.ds(r0, rows + HALO), :] for p in pads], sm[pl.ds(r0, rows), :], taps, al, db, h)
        for v, r in zip(vals, chunked):
            r[pl.ds(i * per, per)] = v.reshape(per, CHUNK, v.shape[-1])
        return carry

    lax.fori_loop(0, t // rows, tile, 0)


def _gdn_intra_all(chunked, intra):
    nc = chunked[0].shape[0]
    grp_n = math.gcd(nc, GROUP)

    def grp(i, carry):
        sl = pl.ds(pl.multiple_of(i * grp_n, grp_n), grp_n)
        for r, val in zip(intra, _gdn_intra(*[c[sl] for c in chunked])):
            r[sl] = val
        return carry

    lax.fori_loop(0, nc // grp_n, grp, 0)


def _gdn_fwd(pa, pb, conv, alog, dtb):
    t = pa.shape[0]
    nc = t // CHUNK
    terms = _gdn_term_shapes(nc)

    def body(gq, gk, gv, sm, wq, wk, wv, al, db, *rest):
        h = pl.program_id(0)
        intra, chunked, pads = rest[:6], rest[6:11], rest[11:]
        _gdn_pad((gq, gk, gv), pads)
        _gdn_stage(pads, sm, _taps(wq, wk, wv), al[...], db[...], h, chunked)
        _gdn_intra_all(chunked, intra)

    return pl.pallas_call(
        body, grid=(NG,), name="gdn_fwd", in_specs=_gdn_in_specs(t),
        out_specs=[_per_head(sh, one_buffer=False) for sh in terms], out_shape=[SDS((NG,) + sh, F32) for sh in terms],
        scratch_shapes=_gdn_chunked_scratch(nc) + [pltpu.VMEM((t + HALO, HD), F32)] * 3, compiler_params=_cp("parallel"),
    )(pa, pa, pa, pb, conv, conv, conv, alog, dtb)


def _gdn_scan(terms_in):
    nc = terms_in[0].shape[1]
    terms = _gdn_term_shapes(nc)

    def body(*refs):
        intra, o, states = refs[:6], refs[6], refs[7]

        def step(c, ss):
            rows = pl.ds(pl.multiple_of(c * CHUNK, CHUNK), CHUNK)
            new = []
            for hh in range(SCAN_HEADS):
                states[hh, c] = ss[hh]
                oc, s2 = _gdn_step(ss[hh], *[r[hh, c] for r in intra])
                o[rows, hh * HD:(hh + 1) * HD] = oc
                new.append(s2)
            return tuple(new)

        lax.fori_loop(0, nc, step, tuple(jnp.zeros((HD, HD), F32) for _ in range(SCAN_HEADS)))

    return pl.pallas_call(
        body, grid=(NG // SCAN_HEADS,), name="gdn_scan", in_specs=[_per_head(sh, SCAN_HEADS) for sh in terms],
        out_specs=[pl.BlockSpec((nc * CHUNK, SCAN_HEADS * HD), lambda h: (0, h), pipeline_mode=ONE_BUFFER),
                   _per_head((nc, HD, HD), SCAN_HEADS)],
        out_shape=[SDS((nc * CHUNK, NG * HD), F32), SDS((NG, nc, HD, HD), F32)], compiler_params=_cp("parallel"),
    )(*terms_in)


def _gdn_bwd_scan(saved, do_raw):
    nc = saved[0].shape[1]
    terms = _gdn_term_shapes(nc)

    def body(*refs):
        intra, states, do, outs = refs[:6], refs[6], refs[7], refs[8:]

        def bwd(i, dss):
            c = nc - 1 - i
            rows = pl.ds(pl.multiple_of(c * CHUNK, CHUNK), CHUNK)
            new = []
            for hh in range(SCAN_HEADS):
                _, vjp = jax.vjp(_gdn_step, states[hh, c], *[r[hh, c] for r in intra])
                grads = vjp((do[rows, hh * HD:(hh + 1) * HD], dss[hh]))
                for r, gval in zip(outs, grads[1:]):
                    r[hh, c] = gval
                new.append(grads[0])
            return tuple(new)

        lax.fori_loop(0, nc, bwd, tuple(jnp.zeros((HD, HD), F32) for _ in range(SCAN_HEADS)))

    return pl.pallas_call(
        body, grid=(NG // SCAN_HEADS,), name="gdn_bwd_scan",
        in_specs=[_per_head(sh, SCAN_HEADS) for sh in terms] + [_per_head((nc, HD, HD), SCAN_HEADS)]
        + [pl.BlockSpec((nc * CHUNK, SCAN_HEADS * HD), lambda h: (0, h), pipeline_mode=ONE_BUFFER)],
        out_specs=[_per_head(sh, SCAN_HEADS) for sh in terms],
        out_shape=[SDS((NG,) + sh, F32) for sh in terms], compiler_params=_cp("parallel"),
    )(*saved, do_raw)


def _gdn_bwd(pa, pb, conv, alog, dtb, dterms):
    t = pa.shape[0]
    nc = t // CHUNK
    terms = _gdn_term_shapes(nc)

    def body(*refs):
        gq, gk, gv, sm, wq, wk, wv, al, db = refs[:9]
        dintra = refs[9:15]
        dgq, dgk, dgv, dsm, dwq, dwk, dwv, dal, ddb = refs[15:24]
        chunked, pads, dpads, dsm_s = refs[24:29], refs[29:32], refs[32:35], refs[35]
        h = pl.program_id(0)
        taps = _taps(wq, wk, wv)
        _gdn_pad((gq, gk, gv), pads)
        _gdn_stage(pads, sm, taps, al[...], db[...], h, chunked)
        grp_n = math.gcd(nc, GROUP)

        def grp(i, carry):
            sl = pl.ds(pl.multiple_of(i * grp_n, grp_n), grp_n)
            _, vjp = jax.vjp(_gdn_intra, *[r[sl] for r in chunked])
            for r, gval in zip(chunked, vjp(tuple(r[sl] for r in dintra))):
                r[sl] = gval
            return carry

        lax.fori_loop(0, nc // grp_n, grp, 0)

        rows = _prep_rows(t)
        per = rows // CHUNK
        for r in dpads:
            r[...] = jnp.zeros_like(r)

        def tile(i, small):
            r0 = pl.multiple_of(i * rows, rows)
            win = pl.ds(r0, rows + HALO)
            _, vjp = jax.vjp(lambda *a: _gdn_prep(*a, h), *[p[win, :] for p in pads], sm[pl.ds(r0, rows), :],
                             taps, al[...], db[...])
            grads = vjp(tuple(r[pl.ds(i * per, per)].reshape(rows, r.shape[-1]) for r in chunked))
            for r, gval in zip(dpads, grads[:3]):
                r[win, :] += gval
            dsm_s[pl.ds(r0, rows), :] = grads[3]
            return jax.tree.map(jnp.add, small, (grads[4], grads[5], grads[6]))

        zero = jnp.zeros((1, HD), F32)
        dtaps, g_al, g_db = lax.fori_loop(0, t // rows, tile, ((zero,) * 12, zero, zero))
        for r, dpad in zip((dgq, dgk, dgv), dpads):
            r[...] = dpad[HALO:, :].astype(r.dtype)
        for j, r in enumerate((dwq, dwk, dwv)):
            for k in range(4):
                r[k:k + 1, :] = dtaps[4 * j + k]

        @pl.when(h == 0)
        def _():
            for r in (dsm, dal, ddb):
                r[...] = jnp.zeros_like(r)

        dsm[...] += dsm_s[...]
        dal[...] += g_al
        ddb[...] += g_db

    head = _head(t)
    taps = pl.BlockSpec((4, HD), lambda h: (0, h))
    return pl.pallas_call(
        body, grid=(NG,), name="gdn_bwd", in_specs=_gdn_in_specs(t) + [_per_head(sh) for sh in terms],
        out_specs=[head, head, head, _small(t), taps, taps, taps, _small(1), _small(1)],
        out_shape=[SDS((t, NG * HD), BF16)] * 3 + [SDS((t, HD), F32)] + [SDS((4, NG * HD), F32)] * 3 + [SDS((1, HD), F32)] * 2,
        scratch_shapes=_gdn_chunked_scratch(nc) + [pltpu.VMEM((t + HALO, HD), F32)] * 6 + [pltpu.VMEM((t, HD), F32)],
        compiler_params=_cp("arbitrary"),
    )(pa, pa, pa, pb, conv, conv, conv, alog, dtb, *dterms)


def _gdn_post(o, z, gain):
    return (jnp.concatenate(
        [_rms(o[:, h * HD:(h + 1) * HD], gain) * _silu(z[:, h * HD:(h + 1) * HD]) for h in range(NG)], axis=1),)


def _place():
    return lax.axis_index("x"), lax.axis_index("y"), lax.axis_index("c")


def _all_gather(name, shard):
    def body(x_ref, out_ref, send_sems, recv_sems, local_sem):
        x, y, c = _place()
        me, sibling = (x, y, c), (x, y, 1 - c)
        chips = [(1 - x, y), (x, 1 - y), (1 - x, 1 - y)]

        def blk(px, py, pc):
            return out_ref.at[4 * px + 2 * py + pc]

        def copy(k, block, to, src=None):
            return pltpu.make_async_remote_copy(
                src_ref=blk(*block) if src is None else src, dst_ref=blk(*block),
                send_sem=send_sems.at[k], recv_sem=recv_sems.at[k], device_id=to, device_id_type=MESH)

        mine = pltpu.make_async_copy(x_ref, blk(*me), local_sem)
        mine.start()
        first = [copy(0, me, sibling, src=x_ref)]
        first += [copy(1 + j, me, (*chip, c), src=x_ref) for j, chip in enumerate(chips)]
        for cp in first:
            cp.start()
        passed = [copy(4 + j, (*chip, c), sibling) for j, chip in enumerate(chips)]
        for j, chip in enumerate(chips):
            copy(1 + j, (*chip, c), me).wait_recv()
            passed[j].start()
        copy(0, sibling, me).wait_recv()
        for j, chip in enumerate(chips):
            copy(4 + j, (*chip, 1 - c), me).wait_recv()
        for cp in first + passed:
            cp.wait_send()
        mine.wait()

    return pl.pallas_call(
        body, name=name, out_shape=SDS((N_DEV,) + shard.shape, shard.dtype),
        in_specs=[pl.BlockSpec(memory_space=pltpu.HBM)], out_specs=pl.BlockSpec(memory_space=pltpu.HBM),
        scratch_shapes=[pltpu.SemaphoreType.DMA((7,)), pltpu.SemaphoreType.DMA((7,)), pltpu.SemaphoreType.DMA],
    )(shard)


def _scatter_exchange(name, full):
    def body(g_ref, out_ref, send_sems, recv_sems, local_sem):
        x, y, c = _place()
        me = 4 * x + 2 * y + c
        mine = pltpu.make_async_copy(g_ref.at[me], out_ref.at[me], local_sem)
        mine.start()
        sends, recvs = [], []
        for k in range(1, N_DEV):
            px = 1 - x if k & 4 else x
            py = 1 - y if k & 2 else y
            pc = 1 - c if k & 1 else c
            peer = 4 * px + 2 * py + pc
            sends.append(pltpu.make_async_remote_copy(
                src_ref=g_ref.at[peer], dst_ref=out_ref.at[me], send_sem=send_sems.at[k - 1],
                recv_sem=recv_sems.at[k - 1], device_id=(px, py, pc), device_id_type=MESH))
            recvs.append(pltpu.make_async_remote_copy(
                src_ref=g_ref.at[me], dst_ref=out_ref.at[peer], send_sem=send_sems.at[k - 1],
                recv_sem=recv_sems.at[k - 1], device_id=(px, py, pc), device_id_type=MESH))
        for cp in sends:
            cp.start()
        for cp in recvs:
            cp.wait_recv()
        for cp in sends:
            cp.wait_send()
        mine.wait()

    return pl.pallas_call(
        body, name=name, out_shape=SDS(full.shape, full.dtype),
        in_specs=[pl.BlockSpec(memory_space=pltpu.HBM)], out_specs=pl.BlockSpec(memory_space=pltpu.HBM),
        scratch_shapes=[pltpu.SemaphoreType.DMA((7,)), pltpu.SemaphoreType.DMA((7,)), pltpu.SemaphoreType.DMA],
    )(full)


def _sum_blocks(name, parts):
    _, r, c = parts.shape
    tr = 64 if r % 64 == 0 else r

    def body(x, o):
        acc = x[0].astype(F32)
        for d in range(1, N_DEV):
            acc = acc + x[d].astype(F32)
        o[...] = acc

    return pl.pallas_call(
        body, grid=(r // tr,), name=name, in_specs=[pl.BlockSpec((N_DEV, tr, c), lambda i: (0, i, 0))],
        out_specs=pl.BlockSpec((tr, c), lambda i: (i, 0)), out_shape=SDS((r, c), F32), compiler_params=_cp("parallel"),
    )(parts)


def _reduce_scatter(name, full):
    return _sum_blocks(name + "_sum", _scatter_exchange(name, full))


def _all_reduce_small(name, x, reduce):
    m_per, n = x.shape

    def body(x_ref, out_ref, send_sems, recv_sems, local_sem):
        px, py, pc = _place()
        me, sibling = (px, py, pc), (px, py, 1 - pc)
        chips = [(1 - px, py), (px, 1 - py), (1 - px, 1 - py)]
        buf = out_ref

        def rows(qx, qy, qc):
            return buf.at[pl.ds((4 * qx + 2 * qy + qc) * m_per, m_per), :]

        def copy(k, block, to, src=None):
            return pltpu.make_async_remote_copy(
                src_ref=rows(*block) if src is None else src, dst_ref=rows(*block),
                send_sem=send_sems.at[k], recv_sem=recv_sems.at[k], device_id=to, device_id_type=MESH)

        mine = pltpu.make_async_copy(x_ref, rows(*me), local_sem)
        mine.start()
        first = [copy(0, me, sibling, src=x_ref)]
        first += [copy(1 + j, me, (*chip, pc), src=x_ref) for j, chip in enumerate(chips)]
        for cp in first:
            cp.start()
        passed = [copy(4 + j, (*chip, pc), sibling) for j, chip in enumerate(chips)]
        for j, chip in enumerate(chips):
            copy(1 + j, (*chip, pc), me).wait_recv()
            passed[j].start()
        copy(0, sibling, me).wait_recv()
        for j, chip in enumerate(chips):
            copy(4 + j, (*chip, 1 - pc), me).wait_recv()
        for cp in first + passed:
            cp.wait_send()
        mine.wait()

    gathered = pl.pallas_call(
        body, name=name, out_shape=SDS((N_DEV * m_per, n), x.dtype),
        in_specs=[pl.BlockSpec(memory_space=pltpu.VMEM)], out_specs=pl.BlockSpec(memory_space=pltpu.VMEM),
        scratch_shapes=[pltpu.SemaphoreType.DMA((7,)), pltpu.SemaphoreType.DMA((7,)), pltpu.SemaphoreType.DMA],
    )(x)
    if not reduce:
        return gathered
    return _sum_blocks(name + "_sum", gathered.reshape(N_DEV, m_per, n))


HBM_SPEC = pl.BlockSpec(memory_space=pltpu.HBM)
SEM_SPEC = pl.BlockSpec(memory_space=pltpu.SEMAPHORE)
EFFECT = pltpu.SideEffectType.DATAFLOW_SIDE_EFFECTING


def _copies_start(name, bufs, n_remote, n_local, build, deps):
    nb, nd = len(bufs), len(deps)
    sem_shapes = [pltpu.SemaphoreType.DMA((n_remote,)), pltpu.SemaphoreType.DMA((n_remote,))]
    if n_local:
        sem_shapes.append(pltpu.SemaphoreType.DMA((n_local,)))
    ns = len(sem_shapes)

    def body(*refs):
        sems = refs[nb + nd:nb + nd + ns]
        remote, local = build(refs[:nb], *sems, *([None] * (3 - ns)))
        for cp in local + remote:
            cp.start()
        refs[-1][...] = jnp.zeros((8, HD), F32)

    outs = pl.pallas_call(
        body, name=name,
        out_shape=(*sem_shapes, *[pltpu.HBM(b.shape, b.dtype) for b in bufs], SDS((8, HD), F32)),
        in_specs=[HBM_SPEC] * nb + [ANY_SPEC] * nd,
        out_specs=(*[SEM_SPEC] * ns, *[HBM_SPEC] * nb, pl.BlockSpec(memory_space=pltpu.VMEM)),
        input_output_aliases={i: ns + i for i in range(nb)},
        compiler_params=pltpu.CompilerParams(has_side_effects=EFFECT),
    )(*[pltpu.with_memory_space_constraint(b, pltpu.HBM) for b in bufs], *deps)
    return list(outs[:ns]), list(outs[ns:ns + nb]), outs[-1]


def _copies_wait(name, bufs, sems, build, after):
    nb, ns = len(bufs), len(sems)

    def body(*refs):
        remote, local = build(refs[:nb], *refs[nb:nb + ns], *([None] * (3 - ns)))
        for cp in local:
            cp.wait()
        for cp in remote:
            cp.wait_send()
            cp.wait_recv()

    outs = pl.pallas_call(
        body, name=name, out_shape=tuple(pltpu.HBM(b.shape, b.dtype) for b in bufs),
        in_specs=[HBM_SPEC] * nb + [SEM_SPEC] * ns + [ANY_SPEC] * len(after), out_specs=tuple([HBM_SPEC] * nb),
        input_output_aliases={i: i for i in range(nb)},
        compiler_params=pltpu.CompilerParams(has_side_effects=EFFECT),
    )(*bufs, *sems, *after)
    return list(outs)


def _remote(src, dst, send, recv, k, to):
    return pltpu.make_async_remote_copy(src_ref=src, dst_ref=dst, send_sem=send.at[k], recv_sem=recv.at[k],
                                        device_id=to, device_id_type=MESH)


class _Gather:
    def __init__(self, name, shards, deps):
        self.name, self.n = name, len(shards)
        lands = [lax.empty((N_DEV,) + s.shape, s.dtype) for s in shards]
        self.sems1, bufs, self.token = _copies_start(
            name + "_s1", list(shards) + lands, 4 * self.n, self.n, self._stage1(range(self.n)), deps)
        self.shards, self.lands, self.sems2 = bufs[:self.n], bufs[self.n:], {}

    def _stage1(self, idxs):
        def build(refs, send, recv, loc):
            x, y, c = _place()
            me = 4 * x + 2 * y + c
            targets = [(x, y, 1 - c), (1 - x, y, c), (x, 1 - y, c), (1 - x, 1 - y, c)]
            remote, local = [], []
            for pos, i in enumerate(idxs):
                src, land = refs[pos], refs[len(idxs) + pos]
                local.append(pltpu.make_async_copy(src, land.at[me], loc.at[i]))
                remote += [_remote(src, land.at[me], send, recv, 4 * i + k, to) for k, to in enumerate(targets)]
            return remote, local
        return build

    @staticmethod
    def _stage2(refs, send, recv, loc):
        x, y, c = _place()
        remote = []
        for pos, land in enumerate(refs):
            for j, (cx, cy) in enumerate([(1 - x, y), (x, 1 - y), (1 - x, 1 - y)]):
                blk = land.at[4 * cx + 2 * cy + c]
                remote.append(_remote(blk, blk, send, recv, 3 * pos + j, (x, y, 1 - c)))
        return remote, []

    def pass_on(self, idxs, after):
        tag, m = "".join(map(str, idxs)), len(idxs)
        bufs = _copies_wait(f"{self.name}_w1_{tag}", [self.shards[i] for i in idxs] + [self.lands[i] for i in idxs],
                            self.sems1, self._stage1(idxs), after)
        self.sems2[tag], lands, token = _copies_start(f"{self.name}_s2_{tag}", bufs[m:], 3 * m, 0, self._stage2, ())
        for pos, i in enumerate(idxs):
            self.lands[i] = lands[pos]
        return [token]

    def get(self, idxs, after):
        tag = "".join(map(str, idxs))
        return _copies_wait(f"{self.name}_w2_{tag}", [self.lands[i] for i in idxs], self.sems2[tag], self._stage2, after)


def _rows_tile(r, row_bytes, target=1 << 20):
    tr = r
    while tr % 32 == 0 and tr * row_bytes > target:
        tr //= 2
    return tr


def _pair_add(name, g, got, c):
    _, r, cols = g.shape
    tr = _rows_tile(r, cols * 2)

    def body(s, a, b, o):
        o[...] = (a[...].astype(F32) + b[...].astype(F32)).astype(o.dtype)

    return pl.pallas_call(
        body, name=name, out_shape=SDS((4, r, cols), g.dtype),
        grid_spec=pltpu.PrefetchScalarGridSpec(
            num_scalar_prefetch=1, grid=(4, r // tr),
            in_specs=[pl.BlockSpec((None, tr, cols), lambda j, i, s: (2 * j + s[0], i, 0)),
                      pl.BlockSpec((None, tr, cols), lambda j, i, s: (j, i, 0))],
            out_specs=pl.BlockSpec((None, tr, cols), lambda j, i, s: (j, i, 0))),
        compiler_params=_cp("parallel", "parallel"),
    )(c.reshape(1), g, got)


def _quad_sum(name, part, got, chip):
    _, r, cols = part.shape
    tr = _rows_tile(r, cols * 4)

    def body(s, a, b1, b2, b3, o):
        o[...] = ((a[...].astype(F32) + b1[...].astype(F32)) + b2[...].astype(F32)) + b3[...].astype(F32)

    blk = lambda k: pl.BlockSpec((None, tr, cols), lambda i, s, k=k: (jnp.bitwise_xor(s[0], k), i, 0))
    return pl.pallas_call(
        body, name=name, out_shape=SDS((r, cols), F32),
        grid_spec=pltpu.PrefetchScalarGridSpec(
            num_scalar_prefetch=1, grid=(r // tr,), in_specs=[blk(0), blk(1), blk(2), blk(3)],
            out_specs=pl.BlockSpec((tr, cols), lambda i, s: (i, 0))),
        compiler_params=_cp("parallel"),
    )(chip.reshape(1), part, got, got, got)


class _Scatter:
    def __init__(self, name, grads, deps):
        self.name, self.n = name, len(grads)
        got = [lax.empty((4,) + g.shape[1:], g.dtype) for g in grads]
        self.sems, bufs, self.token = _copies_start(name + "_s1", list(grads) + got, 4 * self.n, 0, self._stage1, deps)
        self.grads, self.got = bufs[:self.n], bufs[self.n:]

    def _stage1(self, refs, send, recv, loc):
        x, y, c = _place()
        remote = []
        for i in range(self.n):
            remote += [_remote(refs[i].at[2 * j + 1 - c], refs[self.n + i].at[j], send, recv, 4 * i + j, (x, y, 1 - c))
                       for j in range(4)]
        return remote, []

    def _stage2(self, refs, send, recv, loc):
        x, y, c = _place()
        remote = []
        for i in range(self.n):
            for k in (1, 2, 3):
                tx = 1 - x if k & 2 else x
                ty = 1 - y if k & 1 else y
                remote.append(_remote(refs[i].at[2 * tx + ty], refs[self.n + i].at[2 * x + y], send, recv,
                                      3 * i + k - 1, (tx, ty, c)))
        return remote, []

    def mid(self, after):
        bufs = _copies_wait(self.name + "_w1", self.grads + self.got, self.sems, self._stage1, after)
        c = lax.axis_index("c").astype(jnp.int32)
        parts = [_pair_add(f"{self.name}_add{i}", bufs[i], bufs[self.n + i], c) for i in range(self.n)]
        got = [lax.empty(p.shape, p.dtype) for p in parts]
        self.sems, bufs, self.token = _copies_start(self.name + "_s2", parts + got, 3 * self.n, 0, self._stage2, ())
        self.parts, self.got = bufs[:self.n], bufs[self.n:]

    def end(self, after):
        bufs = _copies_wait(self.name + "_w2", self.parts + self.got, self.sems, self._stage2, after)
        chip = (2 * lax.axis_index("x") + lax.axis_index("y")).astype(jnp.int32)
        return [_quad_sum(f"{self.name}_sum{i}", bufs[i], bufs[self.n + i], chip) for i in range(self.n)]


def _adamw(w, g, m, v):
    m = ADAM_B1 * m + (1.0 - ADAM_B1) * g
    v = ADAM_B2 * v + (1.0 - ADAM_B2) * (g * g)
    m_hat = m / (1.0 - ADAM_B1 ** ADAM_STEP)
    v_hat = v / (1.0 - ADAM_B2 ** ADAM_STEP)
    return -ADAM_LR * (m_hat / (jnp.sqrt(v_hat) + ADAM_EPS) + ADAM_WD * w), m, v


def _adamw_call(name, w, g, m, v):
    r, c = w.shape
    tm = 64 if r % 64 == 0 else r
    return _rowwise(name, _adamw, [w, g, m, v], [], [(c, F32)] * 3, tm)


_IN_COLS = 5906


def _perm_in(w):
    pad = jnp.zeros((w.shape[0], 2 * HALF - _IN_COLS), w.dtype)
    return (jnp.concatenate([w[:, 2310:4614], w[:, 4614:5382]], axis=1),
            jnp.concatenate([w[:, :2304], w[:, 5394:5906], w[:, 2304:2310], w[:, 5382:5394], pad], axis=1))


def _unperm_in(ga, gb):
    return jnp.concatenate([gb[:, :2304], gb[:, 2816:2822], ga[:, :2304], ga[:, 2304:3072], gb[:, 2822:2834],
                            gb[:, 2304:2816]], axis=1)


def _lanes(v, at):
    return jnp.pad(v, ((0, 0), (at, HD - at - v.shape[1])))


_PACK = ("norm_mix", "mem_norm", "norm_ffn", "gdn_conv", "fox_q_norm", "fox_k_norm", "gdn_out_norm", "mem_q_norm",
         "mem_k_norm", "fox_f_bias", "gdn_a_log", "gdn_dt_bias", "loss")


def _pack(vals):
    parts = [vals[n].reshape(-1, HD) for n in _PACK]
    used = sum(p.shape[0] for p in parts)
    buf = jnp.concatenate(parts + [jnp.zeros((-used % 8, HD), F32)], axis=0)
    return buf, [(n, p.shape[0]) for n, p in zip(_PACK, parts)]


def _unpack(buf, layout):
    out, at = {}, 0
    for n, rows in layout:
        out[n] = buf[at:at + rows]
        at += rows
    return out


def kernel(x, mem, norm_mix, w_in, fox_f_bias, fox_q_norm, fox_k_norm, gdn_conv, gdn_a_log, gdn_dt_bias, gdn_out_norm, mem_norm, w_mem_kv, mem_q_norm, mem_k_norm, w_out, norm_ffn, w_gate_up, w_down, loss_target, m_norm_mix, m_w_in, m_fox_f_bias, m_fox_q_norm, m_fox_k_norm, m_gdn_conv, m_gdn_a_log, m_gdn_dt_bias, m_gdn_out_norm, m_mem_norm, m_w_mem_kv, m_mem_q_norm, m_mem_k_norm, m_w_out, m_norm_ffn, m_w_gate_up, m_w_down, v_norm_mix, v_w_in, v_fox_f_bias, v_fox_q_norm, v_fox_k_norm, v_gdn_conv, v_gdn_a_log, v_gdn_dt_bias, v_gdn_out_norm, v_mem_norm, v_w_mem_kv, v_mem_q_norm, v_mem_k_norm, v_w_out, v_norm_ffn, v_w_gate_up, v_w_down):
    args = dict(locals())
    d = x.shape[2]
    me = 4 * lax.axis_index("x") + 2 * lax.axis_index("y") + lax.axis_index("c")

    cshard = gdn_conv[0].shape[1]
    conv_pad = jnp.pad(gdn_conv[0], ((0, 4), (0, 3 * HD - cshard)))
    conv_all = _all_reduce_small("ag_conv", conv_pad, False).reshape(N_DEV, 8, 3 * HD)[:, :4, :cshard]
    conv_all = conv_all.transpose(1, 0, 2).reshape(4, N_DEV * cshard)
    w_in_a, w_in_b = _perm_in(w_in[0])
    comm = _StepComm({"in_b": [w_in_b], "in_a": [w_in_a], "kv_out": [w_mem_kv[0], w_out[0]], "gate_up": [w_gate_up[0]],
                      "down": [w_down[0]]}, [conv_all])

    grad_x, loss_local, small_grads = _local_step(
        x[0], mem[0], loss_target[0], norm_mix, fox_f_bias, fox_q_norm, fox_k_norm, gdn_a_log, gdn_dt_bias,
        gdn_out_norm, mem_norm, mem_q_norm, mem_k_norm, norm_ffn, conv_all, comm)

    red = comm.finish([grad_x])
    grads = {"w_down": red["ffn"][0], "w_gate_up": red["ffn"][1], "w_out": red["a"][1], "w_mem_kv": red["b"][1],
             "w_in": _unperm_in(red["a"][0], red["b"][0])}
    small_grads["loss"] = jnp.broadcast_to(loss_local, (1, HD))
    packed, layout = _pack(small_grads)
    small = _unpack(_all_reduce_small("ar_small", packed, True), layout)
    loss = small["loss"][0, 0]
    six = {"fox_f_bias": L_FF, "gdn_a_log": L_GA, "gdn_dt_bias": L_GA}
    for n, rows_n in layout[:-1]:
        gsm = small[n]
        if n == "gdn_conv":
            gsm = lax.dynamic_slice(gsm.reshape(4, N_DEV * cshard), (0, me * cshard), (4, cshard))[None]
        elif n in six:
            gsm = gsm[:, six[n]:six[n] + 6]
        else:
            gsm = gsm.reshape(1, rows_n * HD)
        grads[n] = gsm

    names = ['norm_mix', 'w_in', 'fox_f_bias', 'fox_q_norm', 'fox_k_norm', 'gdn_conv', 'gdn_a_log', 'gdn_dt_bias',
             'gdn_out_norm', 'mem_norm', 'w_mem_kv', 'mem_q_norm', 'mem_k_norm', 'w_out', 'norm_ffn', 'w_gate_up', 'w_down']
    big = ("w_in", "w_mem_kv", "w_out", "w_gate_up", "w_down")
    delta, new_m, new_v = {}, {}, {}
    for n in big:
        if n == "w_in":
            halves = [_adamw_call(f"adamw_{n}_{ab}", w, g, m, v) for ab, w, g, m, v in zip(
                "ab", (w_in_a, w_in_b), (red["a"][0], red["b"][0]), _perm_in(m_w_in[0]), _perm_in(v_w_in[0]))]
            delta[n], new_m[n], new_v[n] = [_unperm_in(a, b)[None] for a, b in zip(*halves)]
        else:
            delta[n], new_m[n], new_v[n] = [a[None] for a in _adamw_call(
                "adamw_" + n, args[n][0], grads[n], args["m_" + n][0], args["v_" + n][0])]
        grads[n] = grads[n][None]

    def flat(a):
        a = a.reshape(1, -1)
        return jnp.pad(a, ((0, 0), (0, -a.shape[1] % HD))).reshape(-1, HD)

    smalls = [n for n in names if n not in big]
    pk = lambda pre: jnp.concatenate([flat(grads[n] if pre == "g" else args[pre + n]) for n in smalls], axis=0)
    cat = [pk(""), pk("g"), pk("m_"), pk("v_")]
    padr = -cat[0].shape[0] % 8
    cat = [jnp.pad(a, ((0, padr), (0, 0))) for a in cat]
    res = _adamw_call("adamw_small", *cat)
    at = 0
    for n in smalls:
        shape = args[n].shape
        size = math.prod(shape)
        nrow = -(-size // HD)
        for dst, src in zip((delta, new_m, new_v), res):
            dst[n] = src[at:at + nrow].reshape(-1)[:size].reshape(shape)
        at += nrow

    return (loss, grad_x[None], *[grads[n] for n in names], *[delta[n] for n in names],
            *[new_m[n] for n in names], *[new_v[n] for n in names])


class _StepComm:
    def __init__(self, shard_groups, after):
        self.groups, shards = {}, []
        for key, ws in shard_groups.items():
            self.groups[key] = list(range(len(shards), len(shards) + len(ws)))
            shards += [w.astype(BF16) for w in ws]
        self.gather = _Gather("ag", shards, after)
        self.passed, self.scatters = set(), {}

    def start_deps(self):
        return [self.gather.token]

    def pass_on(self, key, after):
        self.passed.add(key)
        return self.gather.pass_on(self.groups[key], after)

    def weights(self, key, after):
        if key not in self.passed:
            after = self.pass_on(key, after)
        return self.gather.get(self.groups[key], after)

    def send(self, tag, grads):
        blocks = [g if g.ndim == 3 else g.reshape(N_DEV, g.shape[0] // N_DEV, g.shape[1]) for g in grads]
        self.scatters[tag] = _Scatter("rs_" + tag, blocks, ())
        return [self.scatters[tag].token]

    def mid(self, tag, after):
        self.scatters[tag].mid(after)
        return [self.scatters[tag].token]

    def finish(self, after):
        return {tag: sc.end(after) for tag, sc in self.scatters.items()}


def _local_step(xs, ms, tgt, norm_mix, fox_f_bias, fox_q_norm, fox_k_norm, gdn_a_log, gdn_dt_bias, gdn_out_norm,
                mem_norm, mem_q_norm, mem_k_norm, norm_ffn, conv_all, comm):
    t, d = xs.shape
    bq = min(t, 256)
    fb, alog, dtb = _lanes(fox_f_bias, L_FF), _lanes(gdn_a_log, L_GA), _lanes(gdn_dt_bias, L_GA)
    flat = lambda w: w.reshape(-1, w.shape[-1])

    rms1 = lambda a, g: (_rms(a, g),)
    (u,) = _rowwise("norm_mix", rms1, [xs], [norm_mix], [(d, BF16)], min(t, 256), deps=comm.start_deps())
    w_in_b = flat(comm.weights("in_b", [u])[0])
    pb = _matmul("proj_in_b", u, w_in_b, NN, F32, 1024, 768)
    o_fox = _fox_fwd(pb, fb, fox_q_norm, fox_k_norm, bq)
    w_in_a = flat(comm.weights("in_a", [o_fox])[0])
    pa = _matmul("proj_in_a", u, w_in_a, NN, F32, 1024, 768)
    gdn_terms = _gdn_fwd(pa, pb, conv_all, alog, dtb)
    o_gdn_raw, gdn_states = _gdn_scan(gdn_terms)
    gdn_saved = list(gdn_terms) + [gdn_states]
    zrow = (pa, NG * HD, GZ * HD // (NG * HD))
    (o_gdn,) = _rowwise("gdn_post", _gdn_post, [o_gdn_raw, zrow], [gdn_out_norm], [(NG * HD, BF16)], min(t, 256))
    w_kv_all, w_out_all = [flat(w) for w in comm.weights("kv_out", [o_gdn])]
    (mem_n,) = _rowwise("norm_mem", rms1, [ms], [mem_norm], [(d, BF16)], ms.shape[0])
    mkv = _matmul("proj_mem", mem_n, w_kv_all, NN, F32, 256, 512)
    o_mem = _mem_fwd(pb, mkv, mem_q_norm, mem_k_norm)
    deps = comm.pass_on("gate_up", [o_mem])
    mix = jnp.concatenate([o_fox, o_gdn, o_mem], axis=1)
    h1, h1n = _proj_out_norm(mix, w_out_all, xs, norm_ffn, deps)
    (wgu,) = comm.weights("gate_up", [h1n])
    ffw = wgu.shape[2]
    gu, act = _ffn_up(h1n, wgu.reshape(2, 4, d, ffw))
    w_down_all = flat(comm.weights("down", [act])[0])
    dy, dyb, lsum = _ffn_down_loss(act, w_down_all, h1, tgt)
    loss_local = (0.5 / d) * jnp.sum(lsum[::8, ::HD])

    dgu = _ffn_down_bwd(dyb, w_down_all.reshape(4, ffw, d), gu).reshape(8, t, ffw)
    g_w_down = _matmul("grad_w_down", act, dyb, TN, BF16, 512, 2048)
    dh1n = _ffn_up_bwd_x(dgu, wgu)
    g_w_gu = _ffn_up_bwd_w(h1n, dgu)
    deps = comm.send("ffn", [g_w_down, g_w_gu])
    rms2 = lambda a, g: (_rms(a, g), a)
    dh1, dh1b, g_norm_ffn = _rowwise_vjp("norm_ffn_bwd", rms2, [h1], [norm_ffn], [dh1n, dy], [(F32, BF16)],
                                         min(t, 256), deps=deps)

    dmix = _matmul("proj_out_bwd_x", dh1b, w_out_all, NT, F32, 1024, 1024)
    g_w_out = _matmul("grad_w_out", mix, dh1b, TN, BF16, 1024, 2048)
    deps = comm.mid("ffn", [dmix, g_w_out])
    do_raw, dgz, g_gon = _rowwise_vjp("gdn_post_bwd", _gdn_post, [o_gdn_raw, zrow], [gdn_out_norm],
                                      [(dmix, NG * HD, 1)], [F32, BF16], min(t, 256), deps=deps)
    dterms = _gdn_bwd_scan(gdn_saved, do_raw)
    dgq, dgk, dgv, dsm_gdn, dwq, dwk, dwv, g_alog, g_dtb = _gdn_bwd(pa, pb, conv_all, alog, dtb, dterms)
    dp_a = jnp.concatenate([dgq, dgk, dgv, dgz], axis=1)
    g_w_in_a = _matmul("grad_w_in_a", u, dp_a, TN, BF16, 512, 3072)
    deps = comm.send("a", [g_w_in_a, g_w_out])
    dmq, dmk, dmv, g_mqn, g_mkn = _mem_bwd(pb, mkv, mem_q_norm, mem_k_norm, dmix, deps=deps)
    dmkv = jnp.concatenate([dmk, dmv], axis=1).astype(BF16)
    dmem_n = _matmul("proj_mem_bwd_x", dmkv, w_kv_all, NT, F32, 256, 512)
    g_w_kv = _matmul("grad_w_kv", mem_n, dmkv, TN, BF16, 512, 512)
    g_mem_norm = _rowwise_vjp("norm_mem_bwd", rms1, [ms], [mem_norm], [dmem_n], [], ms.shape[0])[0]
    deps = comm.mid("a", [g_mem_norm, g_w_kv])
    dfq, dfk, dfv, dsm_fox, g_fb, g_fqn, g_fkn = _fox_bwd(pb, fb, fox_q_norm, fox_k_norm, dmix, bq, deps=deps)
    dp_b = jnp.concatenate([dfq, dfk, dfv, dmq, (dsm_fox + dsm_gdn).astype(BF16), jnp.zeros((t, HD), BF16)], axis=1)
    g_w_in_b = _matmul("grad_w_in_b", u, dp_b, TN, BF16, 512, 3072)
    deps = comm.send("b", [g_w_in_b, g_w_kv])
    du_a = _matmul("proj_in_bwd_a", dp_a, w_in_a, NT, F32, 1024, 1024, deps=deps)
    deps = comm.mid("b", [du_a])
    du = _matmul("proj_in_bwd_b", dp_b, w_in_b, NT, F32, 1024, 1024, residual=du_a, deps=deps)
    grad_x, g_norm_mix = _rowwise_vjp("norm_mix_bwd", rms2, [xs], [norm_mix], [du, dh1], [F32], min(t, 256))

    small_grads = {
        "norm_mix": g_norm_mix, "mem_norm": g_mem_norm, "norm_ffn": g_norm_ffn,
        "gdn_conv": jnp.concatenate([dwq, dwk, dwv], axis=1),
        "fox_q_norm": g_fqn, "fox_k_norm": g_fkn, "gdn_out_norm": g_gon, "mem_q_norm": g_mqn, "mem_k_norm": g_mkn,
        "fox_f_bias": g_fb, "gdn_a_log": g_alog, "gdn_dt_bias": g_dtb}
    return grad_x, loss_local, small_grads
```

```python
import functools
import math

import jax
import jax.numpy as jnp
from jax import lax
from jax.experimental import pallas as pl
from jax.experimental.pallas import tpu as pltpu

F32 = jnp.float32
BF16 = jnp.bfloat16
HI = lax.Precision.HIGHEST
SDS = jax.ShapeDtypeStruct

N_DEV = 8
HD = 128
NF, NG, NM = 6, 6, 4
CHUNK = 64
GROUP = 16
NORM_EPS = 1e-6
GQ, GK, GV, GZ = 0, 6, 12, 18
FQ, FK, FV, MQ, SM = 0, 6, 12, 18, 22
HALF = 24 * HD
L_FF, L_GA, L_GB = 0, 6, 12
VMEM_LIMIT = 56 * 1024 * 1024

ADAM_LR, ADAM_B1, ADAM_B2, ADAM_EPS, ADAM_WD, ADAM_STEP = 0.001, 0.9, 0.999, 1e-08, 0.01, 10

NN = (((1,), (0,)), ((), ()))
NT = (((1,), (1,)), ((), ()))
TN = (((0,), (0,)), ((), ()))
MESH = pl.DeviceIdType.MESH


def _cp(*sem):
    return pltpu.CompilerParams(dimension_semantics=tuple(sem) if sem else None, vmem_limit_bytes=VMEM_LIMIT)


def _dot(a, b, dims=NN):
    return lax.dot_general(a, b, dims, preferred_element_type=F32)


def _bdot(a, b):
    return _dot(a.astype(BF16), b.astype(BF16))


def _iota(shape, axis):
    return lax.broadcasted_iota(jnp.int32, shape, axis)


def _rms(x, gain):
    return x * lax.rsqrt(jnp.mean(x * x, axis=-1, keepdims=True) + NORM_EPS) * gain


def _sigmoid(x):
    return 0.5 * jnp.tanh(0.5 * x) + 0.5


def _silu(x):
    return x * _sigmoid(x)


def _softplus(x):
    return jnp.maximum(x, 0.0) + jnp.log(1.0 + jnp.exp(-jnp.abs(x)))


def _lane_pick(x, lane):
    oh = (_iota((1, x.shape[-1]), 1) == lane).astype(F32)
    return jnp.sum(x * oh, axis=-1, keepdims=True)


def _cumsum_rows(x):
    tril = (_iota((HD, HD), 0) >= _iota((HD, HD), 1)).astype(F32)
    carry = jnp.zeros((1, x.shape[1]), F32)
    outs = []
    for b in range(x.shape[0] // HD):
        blk = x[b * HD:(b + 1) * HD]
        outs.append(jnp.dot(tril, blk, precision=HI, preferred_element_type=F32) + carry)
        carry = carry + jnp.sum(blk, axis=0, keepdims=True)
    return jnp.concatenate(outs, axis=0)


def _row_spec(r, tm):
    if isinstance(r, tuple):
        arr, width, cb = r
        return arr, pl.BlockSpec((tm, width), lambda i, cb=cb: (i, cb))
    return r, pl.BlockSpec((tm, r.shape[1]), lambda i: (i, 0))


ANY_SPEC = pl.BlockSpec(memory_space=pl.ANY)


def _rowwise(name, fn, rows, consts, outs, tm, deps=()):
    arrs, specs = zip(*[_row_spec(r, tm) for r in rows])
    n_rows = arrs[0].shape[0]
    nr, nc, nd = len(rows), len(consts), len(deps)

    def body(*refs):
        res = fn(*[r[...] for r in refs[:nr + nc]])
        for o, v in zip(refs[nr + nc + nd:], res):
            o[...] = v.astype(o.dtype)

    return pl.pallas_call(
        body, grid=(n_rows // tm,), name=name,
        in_specs=list(specs) + [pl.BlockSpec(c.shape, lambda i: (0, 0)) for c in consts] + [ANY_SPEC] * nd,
        out_specs=[pl.BlockSpec((tm, w), lambda i: (i, 0)) for w, _ in outs],
        out_shape=[SDS((n_rows, w), dt) for w, dt in outs],
        compiler_params=_cp("parallel"),
    )(*arrs, *consts, *deps)


def _rowwise_vjp(name, fn, rows, consts, cts, grad_dtypes, tm, deps=()):
    arrs, specs = zip(*[_row_spec(r, tm) for r in rows])
    ct_arrs, ct_specs = zip(*[_row_spec(r, tm) for r in cts])
    n_rows = arrs[0].shape[0]
    nr, nc, nct, nd = len(rows), len(consts), len(cts), len(deps)
    plan = [(j, dt) for j, dts in enumerate(grad_dtypes) for dt in (dts if isinstance(dts, tuple) else (dts,))]
    ng = len(plan)
    widths = [specs[j].block_shape[1] for j, _ in plan]
    grad_dtypes = [dt for _, dt in plan]

    def body(*refs):
        vals = [r[...].astype(F32) for r in refs[:nr + nc]]
        ctv = tuple(r[...].astype(F32) for r in refs[nr + nc:nr + nc + nct])
        _, vjp = jax.vjp(fn, *vals)
        grads = vjp(ctv)
        outs = refs[nr + nc + nct + nd:]
        for o, (j, _) in zip(outs[:ng], plan):
            o[...] = grads[j].astype(o.dtype)

        @pl.when(pl.program_id(0) == 0)
        def _():
            for o in outs[ng:]:
                o[...] = jnp.zeros_like(o)

        for o, g in zip(outs[ng:], grads[nr:]):
            o[...] += g

    return pl.pallas_call(
        body, grid=(n_rows // tm,), name=name,
        in_specs=list(specs) + [pl.BlockSpec(c.shape, lambda i: (0, 0)) for c in consts] + list(ct_specs)
        + [ANY_SPEC] * nd,
        out_specs=[pl.BlockSpec((tm, w), lambda i: (i, 0)) for w in widths]
        + [pl.BlockSpec(c.shape, lambda i: (0, 0)) for c in consts],
        out_shape=[SDS((n_rows, w), dt) for w, dt in zip(widths, grad_dtypes)] + [SDS(c.shape, F32) for c in consts],
        compiler_params=_cp("arbitrary"),
    )(*arrs, *consts, *ct_arrs, *deps)


def _tile(n, pref):
    t = min(n, pref)
    while n % t or (t % HD and t != n):
        t -= 1
    return t


def _matmul(name, a, b, dims, out_dtype, tm, tn, residual=None, deps=()):
    ta, tb = dims == TN, dims == NT
    m = a.shape[1] if ta else a.shape[0]
    k = a.shape[0] if ta else a.shape[1]
    n = b.shape[0] if tb else b.shape[1]
    tm, tn = _tile(m, tm), _tile(n, tn)

    def body(*refs):
        acc = _dot(refs[0][...], refs[1][...], dims)
        if residual is not None:
            acc = acc + refs[2][...]
        refs[-1][...] = acc.astype(out_dtype)

    in_specs = [pl.BlockSpec((k, tm), lambda i, j: (0, i)) if ta else pl.BlockSpec((tm, k), lambda i, j: (i, 0)),
                pl.BlockSpec((tn, k), lambda i, j: (j, 0)) if tb else pl.BlockSpec((k, tn), lambda i, j: (0, j))]
    ops = [a, b]
    if residual is not None:
        in_specs.append(pl.BlockSpec((tm, tn), lambda i, j: (i, j)))
        ops.append(residual)
    in_specs += [ANY_SPEC] * len(deps)
    ops += list(deps)
    return pl.pallas_call(
        body, grid=(m // tm, n // tn), name=name, in_specs=in_specs,
        out_specs=pl.BlockSpec((tm, tn), lambda i, j: (i, j)), out_shape=SDS((m, n), out_dtype),
        compiler_params=_cp("parallel", "parallel"),
    )(*ops)


def _proj_out_norm(mix, w_out, xs, gain, deps):
    t, k = mix.shape
    d = w_out.shape[1]
    tm = _tile(t, 512)

    def body(*refs):
        a, b, x, g = refs[:4]
        h1, h1n = refs[4 + len(deps):]
        acc = _dot(a[...], b[...]) + x[...]
        h1[...] = acc
        h1n[...] = _rms(acc, g[...]).astype(BF16)

    return pl.pallas_call(
        body, grid=(t // tm,), name="proj_out",
        in_specs=[pl.BlockSpec((tm, k), lambda i: (i, 0)), pl.BlockSpec((k, d), lambda i: (0, 0)),
                  pl.BlockSpec((tm, d), lambda i: (i, 0)), pl.BlockSpec((1, d), lambda i: (0, 0))] + [ANY_SPEC] * len(deps),
        out_specs=[pl.BlockSpec((tm, d), lambda i: (i, 0))] * 2, out_shape=[SDS((t, d), F32), SDS((t, d), BF16)],
        compiler_params=_cp("parallel"),
    )(mix, w_out, xs, gain, *deps)


def _ffn_up(h1n, wgu):
    t, d = h1n.shape
    w = wgu.shape[3]
    tm = _tile(t, 512)

    def body(a, b, gu, act):
        x = a[...]
        g = _dot(x, b[0])
        u = _dot(x, b[1])
        gu[0] = g.astype(BF16)
        gu[1] = u.astype(BF16)
        act[...] = (_silu(g) * u).astype(BF16)

    return pl.pallas_call(
        body, grid=(4, t // tm), name="ffn_up",
        in_specs=[pl.BlockSpec((tm, d), lambda j, i: (i, 0)), pl.BlockSpec((2, None, d, w), lambda j, i: (0, j, 0, 0))],
        out_specs=[pl.BlockSpec((2, None, tm, w), lambda j, i: (0, j, i, 0)), pl.BlockSpec((tm, w), lambda j, i: (i, j))],
        out_shape=[SDS((2, 4, t, w), BF16), SDS((t, 4 * w), BF16)],
        compiler_params=_cp("parallel", "parallel"),
    )(h1n, wgu)


def _ffn_down_loss(act, wdown, h1, target):
    t, f = act.shape
    d = wdown.shape[1]
    tm, tn = _tile(t, 1024), _tile(d, 512)

    def body(a, b, h, tg, dy, dyb, ls):
        e = _dot(a[...], b[...]) + h[...] - tg[...]
        g = e * (1.0 / d)
        dy[...] = g
        dyb[...] = g.astype(BF16)
        ls[...] = jnp.broadcast_to(jnp.sum(e * e), (8, HD))

    return pl.pallas_call(
        body, grid=(t // tm, d // tn), name="ffn_down_loss",
        in_specs=[pl.BlockSpec((tm, f), lambda i, j: (i, 0)), pl.BlockSpec((f, tn), lambda i, j: (0, j)),
                  pl.BlockSpec((tm, tn), lambda i, j: (i, j)), pl.BlockSpec((tm, tn), lambda i, j: (i, j))],
        out_specs=[pl.BlockSpec((tm, tn), lambda i, j: (i, j)), pl.BlockSpec((tm, tn), lambda i, j: (i, j)),
                   pl.BlockSpec((8, HD), lambda i, j: (i, j))],
        out_shape=[SDS((t, d), F32), SDS((t, d), BF16), SDS((8 * (t // tm), HD * (d // tn)), F32)],
        compiler_params=_cp("parallel", "parallel"),
    )(act, wdown, h1, target)


def _ffn_down_bwd(dyb, wdown4, gu):
    t, d = dyb.shape
    w = wdown4.shape[1]
    tm = _tile(t, 512)

    def body(a, b, gu_ref, out):
        da = _dot(a[...], b[...], NT)
        g = gu_ref[0].astype(F32)
        u = gu_ref[1].astype(F32)
        s = _sigmoid(g)
        out[0] = (da * u * (s * (1.0 + g * (1.0 - s)))).astype(BF16)
        out[1] = (da * g * s).astype(BF16)

    return pl.pallas_call(
        body, grid=(4, t // tm), name="ffn_down_bwd",
        in_specs=[pl.BlockSpec((tm, d), lambda j, i: (i, 0)), pl.BlockSpec((None, w, d), lambda j, i: (j, 0, 0)),
                  pl.BlockSpec((2, None, tm, w), lambda j, i: (0, j, i, 0))],
        out_specs=pl.BlockSpec((2, None, tm, w), lambda j, i: (0, j, i, 0)),
        out_shape=SDS((2, 4, t, w), BF16),
        compiler_params=_cp("parallel", "parallel"),
    )(dyb, wdown4, gu)


def _ffn_up_bwd_x(dgu, wgu):
    _, t, w = dgu.shape
    d = wgu.shape[1]
    tm = _tile(t, 512)

    def body(a, b, out):
        @pl.when(pl.program_id(1) == 0)
        def _():
            out[...] = jnp.zeros_like(out)
        out[...] += _dot(a[...], b[...], NT)

    return pl.pallas_call(
        body, grid=(t // tm, 8), name="ffn_up_bwd_x",
        in_specs=[pl.BlockSpec((None, tm, w), lambda i, j: (j, i, 0)), pl.BlockSpec((None, d, w), lambda i, j: (j, 0, 0))],
        out_specs=pl.BlockSpec((tm, d), lambda i, j: (i, 0)), out_shape=SDS((t, d), F32),
        compiler_params=_cp("parallel", "arbitrary"),
    )(dgu, wgu)


def _ffn_up_bwd_w(h1n, dgu):
    _, t, w = dgu.shape
    d = h1n.shape[1]
    tm = _tile(d, 512)

    def body(a, b, out):
        out[...] = _dot(a[...], b[...], TN).astype(BF16)

    return pl.pallas_call(
        body, grid=(8, d // tm), name="ffn_up_bwd_w",
        in_specs=[pl.BlockSpec((t, tm), lambda j, i: (0, i)), pl.BlockSpec((None, t, w), lambda j, i: (j, 0, 0))],
        out_specs=pl.BlockSpec((None, tm, w), lambda j, i: (j, i, 0)), out_shape=SDS((8, d, w), BF16),
        compiler_params=_cp("parallel", "parallel"),
    )(h1n, dgu)


def _fox_prep(fq, fk, sm, fb, qg, kg, h):
    qn = _rms(fq, qg)
    kn = _rms(fk, kg)
    c = _cumsum_rows(-_softplus(-(sm + fb)))
    ccol = _lane_pick(c, L_FF + h)
    crow = jnp.sum(c.T * (_iota((HD, 1), 0) == L_FF + h).astype(F32), axis=0, keepdims=True)
    return qn, kn, ccol, crow


def _softmax_times(s, v):
    e = jnp.exp(s - lax.stop_gradient(jnp.max(s, axis=1, keepdims=True)))
    return _dot(e.astype(BF16), v.astype(BF16)) * (1.0 / jnp.sum(e, axis=1, keepdims=True))


def _fox_block(q, k, v, cc, cr, off):
    bq = q.shape[0]
    assert k.shape[0] == off + bq
    s = _dot((q * (HD ** -0.5)).astype(BF16), k.astype(BF16), NT) + cc - cr
    diag = jnp.where(_iota((bq, bq), 1) <= _iota((bq, bq), 0), s[:, off:], -1e30)
    s = jnp.concatenate([s[:, :off], diag], axis=1) if off else diag
    return _softmax_times(s, v)


ONE_BUFFER = pl.Buffered(1)


def _pcol(t, cb):
    return pl.BlockSpec((t, HD), lambda h, cb=cb: (0, cb + h), pipeline_mode=ONE_BUFFER)


def _smcol(t):
    return pl.BlockSpec((t, HD), lambda h: (0, SM), pipeline_mode=ONE_BUFFER)


def _head(t):
    return pl.BlockSpec((t, HD), lambda h: (0, h), pipeline_mode=ONE_BUFFER)


def _small(n):
    return pl.BlockSpec((n, HD), lambda h: (0, 0), pipeline_mode=ONE_BUFFER)


def _fox_fwd(p, fb, qg, kg, bq):
    t = p.shape[0]

    def body(fq, fk, fv, sm, fb_r, qg_r, kg_r, o, qn_s, cc_s):
        h = pl.program_id(0)
        qn, kn, ccol, crow = _fox_prep(fq[...], fk[...], sm[...], fb_r[...], qg_r[...], kg_r[...], h)
        qn_s[...] = qn
        cc_s[...] = ccol
        knb = kn.astype(BF16)
        vb = fv[...].astype(BF16)
        for i in range(t // bq):
            rows, ext = pl.ds(i * bq, bq), (i + 1) * bq
            o[rows, :] = _fox_block(qn_s[rows, :], knb[:ext], vb[:ext], cc_s[rows, :], crow[:, :ext], i * bq).astype(o.dtype)

    return pl.pallas_call(
        body, grid=(NF,), name="fox_fwd",
        in_specs=[_pcol(t, FQ), _pcol(t, FK), _pcol(t, FV), _smcol(t), _small(1), _small(1), _small(1)],
        out_specs=_head(t), out_shape=SDS((t, NF * HD), BF16),
        scratch_shapes=[pltpu.VMEM((t, HD), F32), pltpu.VMEM((t, 1), F32)],
        compiler_params=_cp("parallel"),
    )(p, p, p, p, fb, qg, kg)


def _fox_bwd(p, fb, qg, kg, dmix, bq, deps=()):
    t = p.shape[0]

    def body(*refs):
        fq, fk, fv, sm, fb_r, qg_r, kg_r, do = refs[:8]
        dfq, dfk, dfv, dsm, dfb, dqg, dkg, qn_s, cc_s, dqn_s, dcc_s, dkn_s, dv_s, dcr_s = refs[8 + len(deps):]
        h = pl.program_id(0)
        qn, kn, ccol, crow = _fox_prep(fq[...], fk[...], sm[...], fb_r[...], qg_r[...], kg_r[...], h)
        qn_s[...] = qn
        cc_s[...] = ccol
        v = fv[...]
        dkn_s[...] = jnp.zeros_like(dkn_s)
        dv_s[...] = jnp.zeros_like(dv_s)
        dcr_s[...] = jnp.zeros_like(dcr_s)

        for i in range(t // bq):
            rows, ext = pl.ds(i * bq, bq), (i + 1) * bq
            _, vjp = jax.vjp(lambda a, b, c, d, e, off=i * bq: _fox_block(a, b, c, d, e, off),
                             qn_s[rows, :], kn[:ext], v[:ext], cc_s[rows, :], crow[:, :ext])
            dq, dk, dv, dcc, dcr = vjp(do[rows, :])
            dqn_s[rows, :] = dq
            dcc_s[rows, :] = dcc
            dkn_s[:ext, :] += dk
            dv_s[:ext, :] += dv
            dcr_s[:, :ext] += dcr
        _, prep_vjp = jax.vjp(lambda a, b, c, d, e, f: _fox_prep(a, b, c, d, e, f, h),
                              fq[...], fk[...], sm[...], fb_r[...], qg_r[...], kg_r[...])
        g_fq, g_fk, g_sm, g_fb, g_qg, g_kg = prep_vjp((dqn_s[...], dkn_s[...], dcc_s[...], dcr_s[...]))
        dfq[...] = g_fq.astype(dfq.dtype)
        dfk[...] = g_fk.astype(dfk.dtype)
        dfv[...] = dv_s[...].astype(dfv.dtype)

        @pl.when(h == 0)
        def _():
            for r in (dsm, dfb, dqg, dkg):
                r[...] = jnp.zeros_like(r)

        dsm[...] += g_sm
        dfb[...] += g_fb
        dqg[...] += g_qg
        dkg[...] += g_kg

    head = _head(t)
    return pl.pallas_call(
        body, grid=(NF,), name="fox_bwd",
        in_specs=[_pcol(t, FQ), _pcol(t, FK), _pcol(t, FV), _smcol(t), _small(1), _small(1), _small(1), head]
        + [ANY_SPEC] * len(deps),
        out_specs=[head, head, head, _small(t), _small(1), _small(1), _small(1)],
        out_shape=[SDS((t, NF * HD), BF16)] * 3 + [SDS((t, HD), F32)] + [SDS((1, HD), F32)] * 3,
        scratch_shapes=[pltpu.VMEM((t, HD), F32), pltpu.VMEM((t, 1), F32), pltpu.VMEM((t, HD), F32),
                        pltpu.VMEM((t, 1), F32), pltpu.VMEM((t, HD), F32), pltpu.VMEM((t, HD), F32),
                        pltpu.VMEM((1, t), F32)],
        compiler_params=_cp("arbitrary"),
    )(p, p, p, p, fb, qg, kg, dmix, *deps)


def _mem_attn(mq, mk, mv, qg, kg):
    s = _dot((_rms(mq, qg) * (HD ** -0.5)).astype(BF16), _rms(mk, kg).astype(BF16), NT)
    return _softmax_times(s, mv)


def _mem_fwd(p, mkv, qg, kg):
    t, ml = p.shape[0], mkv.shape[0]

    def body(mq, mk, mv, qg_r, kg_r, o):
        o[...] = _mem_attn(mq[...], mk[...], mv[...], qg_r[...], kg_r[...]).astype(o.dtype)

    return pl.pallas_call(
        body, grid=(NM,), name="mem_fwd",
        in_specs=[_pcol(t, MQ), pl.BlockSpec((ml, HD), lambda h: (0, h)), pl.BlockSpec((ml, HD), lambda h: (0, NM + h)),
                  _small(1), _small(1)],
        out_specs=pl.BlockSpec((t, HD), lambda h: (0, h)), out_shape=SDS((t, NM * HD), BF16),
        compiler_params=_cp("parallel"),
    )(p, mkv, mkv, qg, kg)


def _mem_bwd(p, mkv, qg, kg, dmix, deps=()):
    t, ml = p.shape[0], mkv.shape[0]

    def body(*refs):
        mq, mk, mv, qg_r, kg_r, do = refs[:6]
        dmq, dmk, dmv, dqg, dkg = refs[6 + len(deps):]
        _, vjp = jax.vjp(_mem_attn, mq[...], mk[...], mv[...], qg_r[...], kg_r[...])
        g_q, g_k, g_v, g_qg, g_kg = vjp(do[...])
        dmq[...] = g_q.astype(dmq.dtype)
        dmk[...] = g_k
        dmv[...] = g_v

        @pl.when(pl.program_id(0) == 0)
        def _():
            dqg[...] = jnp.zeros_like(dqg)
            dkg[...] = jnp.zeros_like(dkg)

        dqg[...] += g_qg
        dkg[...] += g_kg

    return pl.pallas_call(
        body, grid=(NM,), name="mem_bwd",
        in_specs=[_pcol(t, MQ), pl.BlockSpec((ml, HD), lambda h: (0, h)), pl.BlockSpec((ml, HD), lambda h: (0, NM + h)),
                  _small(1), _small(1), pl.BlockSpec((t, HD), lambda h: (0, NF + NG + h))] + [ANY_SPEC] * len(deps),
        out_specs=[pl.BlockSpec((t, HD), lambda h: (0, h)), pl.BlockSpec((ml, HD), lambda h: (0, h)),
                   pl.BlockSpec((ml, HD), lambda h: (0, h)), _small(1), _small(1)],
        out_shape=[SDS((t, NM * HD), BF16), SDS((ml, NM * HD), F32), SDS((ml, NM * HD), F32),
                   SDS((1, HD), F32), SDS((1, HD), F32)],
        compiler_params=_cp("arbitrary"),
    )(p, mkv, mkv, qg, kg, dmix, *deps)


def _shift_down(x, s):
    if s == 0:
        return x
    return jnp.where(_iota(x.shape, 0) >= s, pltpu.roll(x, s, 0), 0.0)


def _shift_up(x, s):
    if s == 0:
        return x
    n = x.shape[0]
    return jnp.where(_iota(x.shape, 0) < n - s, pltpu.roll(x, n - s, 0), 0.0)


@jax.custom_vjp
def _conv4(x, w0, w1, w2, w3):
    return w0 * _shift_down(x, 3) + w1 * _shift_down(x, 2) + w2 * _shift_down(x, 1) + w3 * x


def _conv4_fwd(x, w0, w1, w2, w3):
    return _conv4(x, w0, w1, w2, w3), (x, w0, w1, w2, w3)


def _conv4_bwd(res, dy):
    x, w0, w1, w2, w3 = res
    ups = [_shift_up(dy, 3 - k) for k in range(4)]
    dx = w0 * ups[0] + w1 * ups[1] + w2 * ups[2] + w3 * ups[3]
    return (dx,) + tuple(jnp.sum(up * x, axis=0, keepdims=True) for up in ups)


_conv4.defvjp(_conv4_fwd, _conv4_bwd)


HALO = 8


def _gdn_prep(gq, gk, gv, sm, taps, alog, dtb, h):
    q, k, v = [_silu(_conv4(x, *taps[4 * j:4 * j + 4]))[HALO:] for j, x in enumerate((gq, gk, gv))]
    q = q * lax.rsqrt(jnp.sum(q * q, axis=-1, keepdims=True) + NORM_EPS) * (HD ** -0.5)
    k = k * lax.rsqrt(jnp.sum(k * k, axis=-1, keepdims=True) + NORM_EPS)
    g = _lane_pick(-jnp.exp(alog) * _softplus(sm + dtb), L_GA + h)
    beta = _lane_pick(_sigmoid(sm), L_GB + h)
    return q, k, v, g, beta


def _split(x, n):
    parts, rest = [], x
    for i in range(n):
        parts.append(rest.astype(BF16))
        if i + 1 < n:
            rest = rest - parts[-1].astype(F32)
    return parts


def _raw_dot(a, b, form):
    lead = a.ndim - 2
    ca, cb = {"nn": (1, 0), "nt": (1, 1), "tn": (0, 0)}[form]
    batch = ((0,), (0,)) if lead else ((), ())
    return lax.dot_general(a, b, (((ca + lead,), (cb + lead,)), batch), preferred_element_type=F32)


def _pdot_impl(a, b, form, mode):
    if mode == "1":
        return _raw_dot(a.astype(BF16), b.astype(BF16), form)
    if mode == "3":
        (ah, al), (bh, bl) = _split(a, 2), _split(b, 2)
        return _raw_dot(ah, bh, form) + (_raw_dot(al, bh, form) + _raw_dot(ah, bl, form))
    if mode == "xa":
        return sum(_raw_dot(a.astype(BF16), t, form) for t in reversed(_split(b, 3)))
    return sum(_raw_dot(t, b.astype(BF16), form) for t in reversed(_split(a, 3)))


@functools.partial(jax.custom_vjp, nondiff_argnums=(2, 3))
def _pdot(a, b, form, mode):
    return _pdot_impl(a, b, form, mode)


def _pdot_fwd(a, b, form, mode):
    return _pdot_impl(a, b, form, mode), (a, b)


def _pdot_bwd(form, mode, res, ct):
    a, b = res
    da_args, db_args = {"nn": ((ct, b, "nt"), (a, ct, "tn")), "nt": ((ct, b, "nn"), (ct, a, "tn")),
                        "tn": ((b, ct, "nt"), (a, ct, "nn"))}[form]

    def side(args, exact):
        if mode in ("1", "3"):
            return mode
        return "xa" if args[0] is exact else "xb"

    if mode == "xa":
        return jnp.zeros_like(a), _pdot_impl(*db_args, side(db_args, a))
    if mode == "xb":
        return _pdot_impl(*da_args, side(da_args, b)), jnp.zeros_like(b)
    return _pdot_impl(*da_args, mode), _pdot_impl(*db_args, mode)


_pdot.defvjp(_pdot_fwd, _pdot_bwd)

GDN_QK, GDN_INV, GDN_SCAN = "1", "1", "1"


@jax.custom_vjp
def _tri_inv(low):
    eye = (_iota((CHUNK, CHUNK), 0) == _iota((CHUNK, CHUNK), 1)).astype(F32)
    inv = eye - low
    pw = low
    for _ in range(5):
        pw = _pdot_impl(pw, pw, "nn", GDN_INV)
        inv = inv + _pdot_impl(inv, pw, "nn", GDN_INV)
    return inv


def _tri_inv_fwd(low):
    inv = _tri_inv(low)
    return inv, inv


def _tri_inv_bwd(inv, ct):
    return (-_pdot_impl(_pdot_impl(inv, ct, "tn", GDN_INV), inv, "nt", GDN_INV),)


_tri_inv.defvjp(_tri_inv_fwd, _tri_inv_bwd)


def _gdn_intra(q, k, v, g, beta):
    n = q.shape[0]
    r, c = _iota((CHUNK, CHUNK), 0), _iota((CHUNK, CHUNK), 1)
    tril, strict = r >= c, r > c
    trilf = jnp.broadcast_to(tril.astype(F32), (n, CHUNK, CHUNK))
    gcm = _pdot(trilf, jnp.broadcast_to(g, (n, CHUNK, CHUNK)), "nn", "xa")
    gcf = _pdot(trilf, jnp.broadcast_to(g, (n, CHUNK, HD)), "nn", "xa")
    lane0 = (_iota((1, 1, CHUNK), 2) == 0).astype(F32)
    gcr = _pdot(jnp.ones((n, CHUNK, CHUNK), F32), gcm * lane0, "nt", "xa")
    decay = jnp.where(tril, jnp.exp(jnp.where(tril, gcm - gcr, 0.0)), 0.0)
    egc = jnp.exp(gcf)
    kb = k * beta
    low = jnp.where(strict, _pdot(kb, k, "nt", GDN_QK) * decay, 0.0)
    inv = _tri_inv(low)
    u = _pdot(inv, v * beta, "nn", GDN_INV)
    w = _pdot(inv, kb * egc, "nn", GDN_INV)
    at = jnp.where(tril, _pdot(q, k, "nt", GDN_QK) * decay, 0.0)
    gl = jnp.sum(jnp.broadcast_to(g, (n, CHUNK, HD)), axis=1, keepdims=True)
    return u, w, q * egc, at, k * jnp.exp(gl - gcf), gl


def _gdn_step(s, u, w, qg, at, kd, gl):
    vn = u - _pdot(w, s, "nn", GDN_SCAN)
    o = _pdot(qg, s, "nn", GDN_SCAN) + _pdot(at, vn, "nn", GDN_SCAN)
    s2 = s * jnp.exp(gl) + _pdot(kd, vn, "tn", GDN_SCAN)
    return o, s2


SCAN_HEADS = 3


def _gdn_chunked_scratch(nc):
    big = pltpu.VMEM((nc, CHUNK, HD), F32)
    return [big, big, big, pltpu.VMEM((nc, CHUNK, 1), F32), pltpu.VMEM((nc, CHUNK, 1), F32)]


def _gdn_term_shapes(nc):
    return [(nc, CHUNK, HD), (nc, CHUNK, HD), (nc, CHUNK, HD), (nc, CHUNK, CHUNK), (nc, CHUNK, HD), (nc, 1, HD)]


def _per_head(shape, heads=None, one_buffer=True):
    lead = (None,) if heads is None else (heads,)
    return pl.BlockSpec(lead + tuple(shape), lambda h: (h,) + (0,) * len(shape),
                        pipeline_mode=ONE_BUFFER if one_buffer else None)


def _gdn_in_specs(t):
    cw = lambda cb: pl.BlockSpec((4, HD), lambda h, cb=cb: (0, cb + h))
    return [_pcol(t, GQ), _pcol(t, GK), _pcol(t, GV), _smcol(t), cw(0), cw(NG), cw(2 * NG), _small(1), _small(1)]


def _taps(wq, wk, wv):
    return tuple(w[k:k + 1, :] for w in (wq, wk, wv) for k in range(4))


def _prep_rows(t):
    return min(t, 256)


def _gdn_pad(srcs, pads):
    for src, pad in zip(srcs, pads):
        pad[0:HALO, :] = jnp.zeros((HALO, HD), F32)
        pad[HALO:, :] = src[...]


def _gdn_stage(pads, sm, taps, al, db, h, chunked):
    t = sm.shape[0]
    rows = _prep_rows(t)
    per = rows // CHUNK

    def tile(i, carry):
        r0 = pl.multiple_of(i * rows, rows)
        vals = _gdn_prep(*[p[pl.ds(r0, rows + HALO), :] for p in pads], sm[pl.ds(r0, rows), :], taps, al, db, h)
        for v, r in zip(vals, chunked):
            r[pl.ds(i * per, per)] = v.reshape(per, CHUNK, v.shape[-1])
        return carry

    lax.fori_loop(0, t // rows, tile, 0)


def _gdn_intra_all(chunked, intra):
    nc = chunked[0].shape[0]
    grp_n = math.gcd(nc, GROUP)

    def grp(i, carry):
        sl = pl.ds(pl.multiple_of(i * grp_n, grp_n), grp_n)
        for r, val in zip(intra, _gdn_intra(*[c[sl] for c in chunked])):
            r[sl] = val
        return carry

    lax.fori_loop(0, nc // grp_n, grp, 0)


def _gdn_fwd(pa, pb, conv, alog, dtb):
    t = pa.shape[0]
    nc = t // CHUNK
    terms = _gdn_term_shapes(nc)

    def body(gq, gk, gv, sm, wq, wk, wv, al, db, *rest):
        h = pl.program_id(0)
        intra, chunked, pads = rest[:6], rest[6:11], rest[11:]
        _gdn_pad((gq, gk, gv), pads)
        _gdn_stage(pads, sm, _taps(wq, wk, wv), al[...], db[...], h, chunked)
        _gdn_intra_all(chunked, intra)

    return pl.pallas_call(
        body, grid=(NG,), name="gdn_fwd", in_specs=_gdn_in_specs(t),
        out_specs=[_per_head(sh, one_buffer=False) for sh in terms], out_shape=[SDS((NG,) + sh, F32) for sh in terms],
        scratch_shapes=_gdn_chunked_scratch(nc) + [pltpu.VMEM((t + HALO, HD), F32)] * 3, compiler_params=_cp("parallel"),
    )(pa, pa, pa, pb, conv, conv, conv, alog, dtb)


def _gdn_scan(terms_in):
    nc = terms_in[0].shape[1]
    terms = _gdn_term_shapes(nc)

    def body(*refs):
        intra, o, states = refs[:6], refs[6], refs[7]

        def step(c, ss):
            rows = pl.ds(pl.multiple_of(c * CHUNK, CHUNK), CHUNK)
            new = []
            for hh in range(SCAN_HEADS):
                states[hh, c] = ss[hh]
                oc, s2 = _gdn_step(ss[hh], *[r[hh, c] for r in intra])
                o[rows, hh * HD:(hh + 1) * HD] = oc
                new.append(s2)
            return tuple(new)

        lax.fori_loop(0, nc, step, tuple(jnp.zeros((HD, HD), F32) for _ in range(SCAN_HEADS)))

    return pl.pallas_call(
        body, grid=(NG // SCAN_HEADS,), name="gdn_scan", in_specs=[_per_head(sh, SCAN_HEADS) for sh in terms],
        out_specs=[pl.BlockSpec((nc * CHUNK, SCAN_HEADS * HD), lambda h: (0, h), pipeline_mode=ONE_BUFFER),
                   _per_head((nc, HD, HD), SCAN_HEADS)],
        out_shape=[SDS((nc * CHUNK, NG * HD), F32), SDS((NG, nc, HD, HD), F32)], compiler_params=_cp("parallel"),
    )(*terms_in)


def _gdn_bwd_scan(saved, do_raw):
    nc = saved[0].shape[1]
    terms = _gdn_term_shapes(nc)

    def body(*refs):
        intra, states, do, outs = refs[:6], refs[6], refs[7], refs[8:]

        def bwd(i, dss):
            c = nc - 1 - i
            rows = pl.ds(pl.multiple_of(c * CHUNK, CHUNK), CHUNK)
            new = []
            for hh in range(SCAN_HEADS):
                _, vjp = jax.vjp(_gdn_step, states[hh, c], *[r[hh, c] for r in intra])
                grads = vjp((do[rows, hh * HD:(hh + 1) * HD], dss[hh]))
                for r, gval in zip(outs, grads[1:]):
                    r[hh, c] = gval
                new.append(grads[0])
            return tuple(new)

        lax.fori_loop(0, nc, bwd, tuple(jnp.zeros((HD, HD), F32) for _ in range(SCAN_HEADS)))

    return pl.pallas_call(
        body, grid=(NG // SCAN_HEADS,), name="gdn_bwd_scan",
        in_specs=[_per_head(sh, SCAN_HEADS) for sh in terms] + [_per_head((nc, HD, HD), SCAN_HEADS)]
        + [pl.BlockSpec((nc * CHUNK, SCAN_HEADS * HD), lambda h: (0, h), pipeline_mode=ONE_BUFFER)],
        out_specs=[_per_head(sh, SCAN_HEADS) for sh in terms],
        out_shape=[SDS((NG,) + sh, F32) for sh in terms], compiler_params=_cp("parallel"),
    )(*saved, do_raw)


def _gdn_bwd(pa, pb, conv, alog, dtb, dterms):
    t = pa.shape[0]
    nc = t // CHUNK
    terms = _gdn_term_shapes(nc)

    def body(*refs):
        gq, gk, gv, sm, wq, wk, wv, al, db = refs[:9]
        dintra = refs[9:15]
        dgq, dgk, dgv, dsm, dwq, dwk, dwv, dal, ddb = refs[15:24]
        chunked, pads, dpads, dsm_s = refs[24:29], refs[29:32], refs[32:35], refs[35]
        h = pl.program_id(0)
        taps = _taps(wq, wk, wv)
        _gdn_pad((gq, gk, gv), pads)
        _gdn_stage(pads, sm, taps, al[...], db[...], h, chunked)
        grp_n = math.gcd(nc, GROUP)

        def grp(i, carry):
            sl = pl.ds(pl.multiple_of(i * grp_n, grp_n), grp_n)
            _, vjp = jax.vjp(_gdn_intra, *[r[sl] for r in chunked])
            for r, gval in zip(chunked, vjp(tuple(r[sl] for r in dintra))):
                r[sl] = gval
            return carry

        lax.fori_loop(0, nc // grp_n, grp, 0)

        rows = _prep_rows(t)
        per = rows // CHUNK
        for r in dpads:
            r[...] = jnp.zeros_like(r)

        def tile(i, small):
            r0 = pl.multiple_of(i * rows, rows)
            win = pl.ds(r0, rows + HALO)
            _, vjp = jax.vjp(lambda *a: _gdn_prep(*a, h), *[p[win, :] for p in pads], sm[pl.ds(r0, rows), :],
                             taps, al[...], db[...])
            grads = vjp(tuple(r[pl.ds(i * per, per)].reshape(rows, r.shape[-1]) for r in chunked))
            for r, gval in zip(dpads, grads[:3]):
                r[win, :] += gval
            dsm_s[pl.ds(r0, rows), :] = grads[3]
            return jax.tree.map(jnp.add, small, (grads[4], grads[5], grads[6]))

        zero = jnp.zeros((1, HD), F32)
        dtaps, g_al, g_db = lax.fori_loop(0, t // rows, tile, ((zero,) * 12, zero, zero))
        for r, dpad in zip((dgq, dgk, dgv), dpads):
            r[...] = dpad[HALO:, :].astype(r.dtype)
        for j, r in enumerate((dwq, dwk, dwv)):
            for k in range(4):
                r[k:k + 1, :] = dtaps[4 * j + k]

        @pl.when(h == 0)
        def _():
            for r in (dsm, dal, ddb):
                r[...] = jnp.zeros_like(r)

        dsm[...] += dsm_s[...]
        dal[...] += g_al
        ddb[...] += g_db

    head = _head(t)
    taps = pl.BlockSpec((4, HD), lambda h: (0, h))
    return pl.pallas_call(
        body, grid=(NG,), name="gdn_bwd", in_specs=_gdn_in_specs(t) + [_per_head(sh) for sh in terms],
        out_specs=[head, head, head, _small(t), taps, taps, taps, _small(1), _small(1)],
        out_shape=[SDS((t, NG * HD), BF16)] * 3 + [SDS((t, HD), F32)] + [SDS((4, NG * HD), F32)] * 3 + [SDS((1, HD), F32)] * 2,
        scratch_shapes=_gdn_chunked_scratch(nc) + [pltpu.VMEM((t + HALO, HD), F32)] * 6 + [pltpu.VMEM((t, HD), F32)],
        compiler_params=_cp("arbitrary"),
    )(pa, pa, pa, pb, conv, conv, conv, alog, dtb, *dterms)


def _gdn_post(o, z, gain):
    return (jnp.concatenate(
        [_rms(o[:, h * HD:(h + 1) * HD], gain) * _silu(z[:, h * HD:(h + 1) * HD]) for h in range(NG)], axis=1),)


def _place():
    return lax.axis_index("x"), lax.axis_index("y"), lax.axis_index("c")


def _all_gather(name, shard):
    def body(x_ref, out_ref, send_sems, recv_sems, local_sem):
        x, y, c = _place()
        me, sibling = (x, y, c), (x, y, 1 - c)
        chips = [(1 - x, y), (x, 1 - y), (1 - x, 1 - y)]

        def blk(px, py, pc):
            return out_ref.at[4 * px + 2 * py + pc]

        def copy(k, block, to, src=None):
            return pltpu.make_async_remote_copy(
                src_ref=blk(*block) if src is None else src, dst_ref=blk(*block),
                send_sem=send_sems.at[k], recv_sem=recv_sems.at[k], device_id=to, device_id_type=MESH)

        mine = pltpu.make_async_copy(x_ref, blk(*me), local_sem)
        mine.start()
        first = [copy(0, me, sibling, src=x_ref)]
        first += [copy(1 + j, me, (*chip, c), src=x_ref) for j, chip in enumerate(chips)]
        for cp in first:
            cp.start()
        passed = [copy(4 + j, (*chip, c), sibling) for j, chip in enumerate(chips)]
        for j, chip in enumerate(chips):
            copy(1 + j, (*chip, c), me).wait_recv()
            passed[j].start()
        copy(0, sibling, me).wait_recv()
        for j, chip in enumerate(chips):
            copy(4 + j, (*chip, 1 - c), me).wait_recv()
        for cp in first + passed:
            cp.wait_send()
        mine.wait()

    return pl.pallas_call(
        body, name=name, out_shape=SDS((N_DEV,) + shard.shape, shard.dtype),
        in_specs=[pl.BlockSpec(memory_space=pltpu.HBM)], out_specs=pl.BlockSpec(memory_space=pltpu.HBM),
        scratch_shapes=[pltpu.SemaphoreType.DMA((7,)), pltpu.SemaphoreType.DMA((7,)), pltpu.SemaphoreType.DMA],
    )(shard)


def _scatter_exchange(name, full):
    def body(g_ref, out_ref, send_sems, recv_sems, local_sem):
        x, y, c = _place()
        me = 4 * x + 2 * y + c
        mine = pltpu.make_async_copy(g_ref.at[me], out_ref.at[me], local_sem)
        mine.start()
        sends, recvs = [], []
        for k in range(1, N_DEV):
            px = 1 - x if k & 4 else x
            py = 1 - y if k & 2 else y
            pc = 1 - c if k & 1 else c
            peer = 4 * px + 2 * py + pc
            sends.append(pltpu.make_async_remote_copy(
                src_ref=g_ref.at[peer], dst_ref=out_ref.at[me], send_sem=send_sems.at[k - 1],
                recv_sem=recv_sems.at[k - 1], device_id=(px, py, pc), device_id_type=MESH))
            recvs.append(pltpu.make_async_remote_copy(
                src_ref=g_ref.at[me], dst_ref=out_ref.at[peer], send_sem=send_sems.at[k - 1],
                recv_sem=recv_sems.at[k - 1], device_id=(px, py, pc), device_id_type=MESH))
        for cp in sends:
            cp.start()
        for cp in recvs:
            cp.wait_recv()
        for cp in sends:
            cp.wait_send()
        mine.wait()

    return pl.pallas_call(
        body, name=name, out_shape=SDS(full.shape, full.dtype),
        in_specs=[pl.BlockSpec(memory_space=pltpu.HBM)], out_specs=pl.BlockSpec(memory_space=pltpu.HBM),
        scratch_shapes=[pltpu.SemaphoreType.DMA((7,)), pltpu.SemaphoreType.DMA((7,)), pltpu.SemaphoreType.DMA],
    )(full)


def _sum_blocks(name, parts):
    _, r, c = parts.shape
    tr = 64 if r % 64 == 0 else r

    def body(x, o):
        acc = x[0].astype(F32)
        for d in range(1, N_DEV):
            acc = acc + x[d].astype(F32)
        o[...] = acc

    return pl.pallas_call(
        body, grid=(r // tr,), name=name, in_specs=[pl.BlockSpec((N_DEV, tr, c), lambda i: (0, i, 0))],
        out_specs=pl.BlockSpec((tr, c), lambda i: (i, 0)), out_shape=SDS((r, c), F32), compiler_params=_cp("parallel"),
    )(parts)


def _reduce_scatter(name, full):
    return _sum_blocks(name + "_sum", _scatter_exchange(name, full))


def _all_reduce_small(name, x, reduce):
    m_per, n = x.shape

    def body(x_ref, out_ref, send_sems, recv_sems, local_sem):
        px, py, pc = _place()
        me, sibling = (px, py, pc), (px, py, 1 - pc)
        chips = [(1 - px, py), (px, 1 - py), (1 - px, 1 - py)]
        buf = out_ref

        def rows(qx, qy, qc):
            return buf.at[pl.ds((4 * qx + 2 * qy + qc) * m_per, m_per), :]

        def copy(k, block, to, src=None):
            return pltpu.make_async_remote_copy(
                src_ref=rows(*block) if src is None else src, dst_ref=rows(*block),
                send_sem=send_sems.at[k], recv_sem=recv_sems.at[k], device_id=to, device_id_type=MESH)

        mine = pltpu.make_async_copy(x_ref, rows(*me), local_sem)
        mine.start()
        first = [copy(0, me, sibling, src=x_ref)]
        first += [copy(1 + j, me, (*chip, pc), src=x_ref) for j, chip in enumerate(chips)]
        for cp in first:
            cp.start()
        passed = [copy(4 + j, (*chip, pc), sibling) for j, chip in enumerate(chips)]
        for j, chip in enumerate(chips):
            copy(1 + j, (*chip, pc), me).wait_recv()
            passed[j].start()
        copy(0, sibling, me).wait_recv()
        for j, chip in enumerate(chips):
            copy(4 + j, (*chip, 1 - pc), me).wait_recv()
        for cp in first + passed:
            cp.wait_send()
        mine.wait()

    gathered = pl.pallas_call(
        body, name=name, out_shape=SDS((N_DEV * m_per, n), x.dtype),
        in_specs=[pl.BlockSpec(memory_space=pltpu.VMEM)], out_specs=pl.BlockSpec(memory_space=pltpu.VMEM),
        scratch_shapes=[pltpu.SemaphoreType.DMA((7,)), pltpu.SemaphoreType.DMA((7,)), pltpu.SemaphoreType.DMA],
    )(x)
    if not reduce:
        return gathered
    return _sum_blocks(name + "_sum", gathered.reshape(N_DEV, m_per, n))


HBM_SPEC = pl.BlockSpec(memory_space=pltpu.HBM)
SEM_SPEC = pl.BlockSpec(memory_space=pltpu.SEMAPHORE)
EFFECT = pltpu.SideEffectType.DATAFLOW_SIDE_EFFECTING


def _copies_start(name, bufs, n_remote, n_local, build, deps):
    nb, nd = len(bufs), len(deps)
    sem_shapes = [pltpu.SemaphoreType.DMA((n_remote,)), pltpu.SemaphoreType.DMA((n_remote,))]
    if n_local:
        sem_shapes.append(pltpu.SemaphoreType.DMA((n_local,)))
    ns = len(sem_shapes)

    def body(*refs):
        sems = refs[nb + nd:nb + nd + ns]
        remote, local = build(refs[:nb], *sems, *([None] * (3 - ns)))
        for cp in local + remote:
            cp.start()
        refs[-1][...] = jnp.zeros((8, HD), F32)

    outs = pl.pallas_call(
        body, name=name,
        out_shape=(*sem_shapes, *[pltpu.HBM(b.shape, b.dtype) for b in bufs], SDS((8, HD), F32)),
        in_specs=[HBM_SPEC] * nb + [ANY_SPEC] * nd,
        out_specs=(*[SEM_SPEC] * ns, *[HBM_SPEC] * nb, pl.BlockSpec(memory_space=pltpu.VMEM)),
        input_output_aliases={i: ns + i for i in range(nb)},
        compiler_params=pltpu.CompilerParams(has_side_effects=EFFECT),
    )(*[pltpu.with_memory_space_constraint(b, pltpu.HBM) for b in bufs], *deps)
    return list(outs[:ns]), list(outs[ns:ns + nb]), outs[-1]


def _copies_wait(name, bufs, sems, build, after):
    nb, ns = len(bufs), len(sems)

    def body(*refs):
        remote, local = build(refs[:nb], *refs[nb:nb + ns], *([None] * (3 - ns)))
        for cp in local:
            cp.wait()
        for cp in remote:
            cp.wait_send()
            cp.wait_recv()

    outs = pl.pallas_call(
        body, name=name, out_shape=tuple(pltpu.HBM(b.shape, b.dtype) for b in bufs),
        in_specs=[HBM_SPEC] * nb + [SEM_SPEC] * ns + [ANY_SPEC] * len(after), out_specs=tuple([HBM_SPEC] * nb),
        input_output_aliases={i: i for i in range(nb)},
        compiler_params=pltpu.CompilerParams(has_side_effects=EFFECT),
    )(*bufs, *sems, *after)
    return list(outs)


def _remote(src, dst, send, recv, k, to):
    return pltpu.make_async_remote_copy(src_ref=src, dst_ref=dst, send_sem=send.at[k], recv_sem=recv.at[k],
                                        device_id=to, device_id_type=MESH)


class _Gather:
    def __init__(self, name, shards, deps):
        self.name, self.n = name, len(shards)
        lands = [lax.empty((N_DEV,) + s.shape, s.dtype) for s in shards]
        self.sems1, bufs, self.token = _copies_start(
            name + "_s1", list(shards) + lands, 4 * self.n, self.n, self._stage1(range(self.n)), deps)
        self.shards, self.lands, self.sems2 = bufs[:self.n], bufs[self.n:], {}

    def _stage1(self, idxs):
        def build(refs, send, recv, loc):
            x, y, c = _place()
            me = 4 * x + 2 * y + c
            targets = [(x, y, 1 - c), (1 - x, y, c), (x, 1 - y, c), (1 - x, 1 - y, c)]
            remote, local = [], []
            for pos, i in enumerate(idxs):
                src, land = refs[pos], refs[len(idxs) + pos]
                local.append(pltpu.make_async_copy(src, land.at[me], loc.at[i]))
                remote += [_remote(src, land.at[me], send, recv, 4 * i + k, to) for k, to in enumerate(targets)]
            return remote, local
        return build

    @staticmethod
    def _stage2(refs, send, recv, loc):
        x, y, c = _place()
        remote = []
        for pos, land in enumerate(refs):
            for j, (cx, cy) in enumerate([(1 - x, y), (x, 1 - y), (1 - x, 1 - y)]):
                blk = land.at[4 * cx + 2 * cy + c]
                remote.append(_remote(blk, blk, send, recv, 3 * pos + j, (x, y, 1 - c)))
        return remote, []

    def pass_on(self, idxs, after):
        tag, m = "".join(map(str, idxs)), len(idxs)
        bufs = _copies_wait(f"{self.name}_w1_{tag}", [self.shards[i] for i in idxs] + [self.lands[i] for i in idxs],
                            self.sems1, self._stage1(idxs), after)
        self.sems2[tag], lands, token = _copies_start(f"{self.name}_s2_{tag}", bufs[m:], 3 * m, 0, self._stage2, ())
        for pos, i in enumerate(idxs):
            self.lands[i] = lands[pos]
        return [token]

    def get(self, idxs, after):
        tag = "".join(map(str, idxs))
        return _copies_wait(f"{self.name}_w2_{tag}", [self.lands[i] for i in idxs], self.sems2[tag], self._stage2, after)


def _rows_tile(r, row_bytes, target=1 << 20):
    tr = r
    while tr % 32 == 0 and tr * row_bytes > target:
        tr //= 2
    return tr


def _pair_add(name, g, got, c):
    _, r, cols = g.shape
    tr = _rows_tile(r, cols * 2)

    def body(s, a, b, o):
        o[...] = (a[...].astype(F32) + b[...].astype(F32)).astype(o.dtype)

    return pl.pallas_call(
        body, name=name, out_shape=SDS((4, r, cols), g.dtype),
        grid_spec=pltpu.PrefetchScalarGridSpec(
            num_scalar_prefetch=1, grid=(4, r // tr),
            in_specs=[pl.BlockSpec((None, tr, cols), lambda j, i, s: (2 * j + s[0], i, 0)),
                      pl.BlockSpec((None, tr, cols), lambda j, i, s: (j, i, 0))],
            out_specs=pl.BlockSpec((None, tr, cols), lambda j, i, s: (j, i, 0))),
        compiler_params=_cp("parallel", "parallel"),
    )(c.reshape(1), g, got)


def _quad_sum(name, part, got, chip):
    _, r, cols = part.shape
    tr = _rows_tile(r, cols * 4)

    def body(s, a, b1, b2, b3, o):
        o[...] = ((a[...].astype(F32) + b1[...].astype(F32)) + b2[...].astype(F32)) + b3[...].astype(F32)

    blk = lambda k: pl.BlockSpec((None, tr, cols), lambda i, s, k=k: (jnp.bitwise_xor(s[0], k), i, 0))
    return pl.pallas_call(
        body, name=name, out_shape=SDS((r, cols), F32),
        grid_spec=pltpu.PrefetchScalarGridSpec(
            num_scalar_prefetch=1, grid=(r // tr,), in_specs=[blk(0), blk(1), blk(2), blk(3)],
            out_specs=pl.BlockSpec((tr, cols), lambda i, s: (i, 0))),
        compiler_params=_cp("parallel"),
    )(chip.reshape(1), part, got, got, got)


class _Scatter:
    def __init__(self, name, grads, deps):
        self.name, self.n = name, len(grads)
        got = [lax.empty((4,) + g.shape[1:], g.dtype) for g in grads]
        self.sems, bufs, self.token = _copies_start(name + "_s1", list(grads) + got, 4 * self.n, 0, self._stage1, deps)
        self.grads, self.got = bufs[:self.n], bufs[self.n:]

    def _stage1(self, refs, send, recv, loc):
        x, y, c = _place()
        remote = []
        for i in range(self.n):
            remote += [_remote(refs[i].at[2 * j + 1 - c], refs[self.n + i].at[j], send, recv, 4 * i + j, (x, y, 1 - c))
                       for j in range(4)]
        return remote, []

    def _stage2(self, refs, send, recv, loc):
        x, y, c = _place()
        remote = []
        for i in range(self.n):
            for k in (1, 2, 3):
                tx = 1 - x if k & 2 else x
                ty = 1 - y if k & 1 else y
                remote.append(_remote(refs[i].at[2 * tx + ty], refs[self.n + i].at[2 * x + y], send, recv,
                                      3 * i + k - 1, (tx, ty, c)))
        return remote, []

    def mid(self, after):
        bufs = _copies_wait(self.name + "_w1", self.grads + self.got, self.sems, self._stage1, after)
        c = lax.axis_index("c").astype(jnp.int32)
        parts = [_pair_add(f"{self.name}_add{i}", bufs[i], bufs[self.n + i], c) for i in range(self.n)]
        got = [lax.empty(p.shape, p.dtype) for p in parts]
        self.sems, bufs, self.token = _copies_start(self.name + "_s2", parts + got, 3 * self.n, 0, self._stage2, ())
        self.parts, self.got = bufs[:self.n], bufs[self.n:]

    def end(self, after):
        bufs = _copies_wait(self.name + "_w2", self.parts + self.got, self.sems, self._stage2, after)
        chip = (2 * lax.axis_index("x") + lax.axis_index("y")).astype(jnp.int32)
        return [_quad_sum(f"{self.name}_sum{i}", bufs[i], bufs[self.n + i], chip) for i in range(self.n)]


def _adamw(w, g, m, v):
    m = ADAM_B1 * m + (1.0 - ADAM_B1) * g
    v = ADAM_B2 * v + (1.0 - ADAM_B2) * (g * g)
    m_hat = m / (1.0 - ADAM_B1 ** ADAM_STEP)
    v_hat = v / (1.0 - ADAM_B2 ** ADAM_STEP)
    return -ADAM_LR * (m_hat / (jnp.sqrt(v_hat) + ADAM_EPS) + ADAM_WD * w), m, v


def _adamw_call(name, w, g, m, v):
    r, c = w.shape
    tm = 64 if r % 64 == 0 else r
    return _rowwise(name, _adamw, [w, g, m, v], [], [(c, F32)] * 3, tm)


_IN_COLS = 5906


def _perm_in(w):
    pad = jnp.zeros((w.shape[0], 2 * HALF - _IN_COLS), w.dtype)
    return (jnp.concatenate([w[:, 2310:4614], w[:, 4614:5382]], axis=1),
            jnp.concatenate([w[:, :2304], w[:, 5394:5906], w[:, 2304:2310], w[:, 5382:5394], pad], axis=1))


def _unperm_in(ga, gb):
    return jnp.concatenate([gb[:, :2304], gb[:, 2816:2822], ga[:, :2304], ga[:, 2304:3072], gb[:, 2822:2834],
                            gb[:, 2304:2816]], axis=1)


def _lanes(v, at):
    return jnp.pad(v, ((0, 0), (at, HD - at - v.shape[1])))


_PACK = ("norm_mix", "mem_norm", "norm_ffn", "gdn_conv", "fox_q_norm", "fox_k_norm", "gdn_out_norm", "mem_q_norm",
         "mem_k_norm", "fox_f_bias", "gdn_a_log", "gdn_dt_bias", "loss")


def _pack(vals):
    parts = [vals[n].reshape(-1, HD) for n in _PACK]
    used = sum(p.shape[0] for p in parts)
    buf = jnp.concatenate(parts + [jnp.zeros((-used % 8, HD), F32)], axis=0)
    return buf, [(n, p.shape[0]) for n, p in zip(_PACK, parts)]


def _unpack(buf, layout):
    out, at = {}, 0
    for n, rows in layout:
        out[n] = buf[at:at + rows]
        at += rows
    return out


def kernel(x, mem, norm_mix, w_in, fox_f_bias, fox_q_norm, fox_k_norm, gdn_conv, gdn_a_log, gdn_dt_bias, gdn_out_norm, mem_norm, w_mem_kv, mem_q_norm, mem_k_norm, w_out, norm_ffn, w_gate_up, w_down, loss_target, m_norm_mix, m_w_in, m_fox_f_bias, m_fox_q_norm, m_fox_k_norm, m_gdn_conv, m_gdn_a_log, m_gdn_dt_bias, m_gdn_out_norm, m_mem_norm, m_w_mem_kv, m_mem_q_norm, m_mem_k_norm, m_w_out, m_norm_ffn, m_w_gate_up, m_w_down, v_norm_mix, v_w_in, v_fox_f_bias, v_fox_q_norm, v_fox_k_norm, v_gdn_conv, v_gdn_a_log, v_gdn_dt_bias, v_gdn_out_norm, v_mem_norm, v_w_mem_kv, v_mem_q_norm, v_mem_k_norm, v_w_out, v_norm_ffn, v_w_gate_up, v_w_down):
    args = dict(locals())
    d = x.shape[2]
    me = 4 * lax.axis_index("x") + 2 * lax.axis_index("y") + lax.axis_index("c")

    cshard = gdn_conv[0].shape[1]
    conv_pad = jnp.pad(gdn_conv[0], ((0, 4), (0, 3 * HD - cshard)))
    conv_all = _all_reduce_small("ag_conv", conv_pad, False).reshape(N_DEV, 8, 3 * HD)[:, :4, :cshard]
    conv_all = conv_all.transpose(1, 0, 2).reshape(4, N_DEV * cshard)
    w_in_a, w_in_b = _perm_in(w_in[0])
    comm = _StepComm({"in_b": [w_in_b], "in_a": [w_in_a], "kv_out": [w_mem_kv[0], w_out[0]], "gate_up": [w_gate_up[0]],
                      "down": [w_down[0]]}, [conv_all])

    grad_x, loss_local, small_grads = _local_step(
        x[0], mem[0], loss_target[0], norm_mix, fox_f_bias, fox_q_norm, fox_k_norm, gdn_a_log, gdn_dt_bias,
        gdn_out_norm, mem_norm, mem_q_norm, mem_k_norm, norm_ffn, conv_all, comm)

    red = comm.finish([grad_x])
    grads = {"w_down": red["ffn"][0], "w_gate_up": red["ffn"][1], "w_out": red["a"][1], "w_mem_kv": red["a"][2],
             "w_in": _unperm_in(red["a"][0], red["b"][0])}
    small_grads["loss"] = jnp.broadcast_to(loss_local, (1, HD))
    packed, layout = _pack(small_grads)
    small = _unpack(_all_reduce_small("ar_small", packed, True), layout)
    loss = small["loss"][0, 0]
    six = {"fox_f_bias": L_FF, "gdn_a_log": L_GA, "gdn_dt_bias": L_GA}
    for n, rows_n in layout[:-1]:
        gsm = small[n]
        if n == "gdn_conv":
            gsm = lax.dynamic_slice(gsm.reshape(4, N_DEV * cshard), (0, me * cshard), (4, cshard))[None]
        elif n in six:
            gsm = gsm[:, six[n]:six[n] + 6]
        else:
            gsm = gsm.reshape(1, rows_n * HD)
        grads[n] = gsm

    names = ['norm_mix', 'w_in', 'fox_f_bias', 'fox_q_norm', 'fox_k_norm', 'gdn_conv', 'gdn_a_log', 'gdn_dt_bias',
             'gdn_out_norm', 'mem_norm', 'w_mem_kv', 'mem_q_norm', 'mem_k_norm', 'w_out', 'norm_ffn', 'w_gate_up', 'w_down']
    big = ("w_in", "w_mem_kv", "w_out", "w_gate_up", "w_down")
    delta, new_m, new_v = {}, {}, {}
    for n in big:
        delta[n], new_m[n], new_v[n] = [a[None] for a in _adamw_call(
            "adamw_" + n, args[n][0], grads[n], args["m_" + n][0], args["v_" + n][0])]
        grads[n] = grads[n][None]

    def flat(a):
        a = a.reshape(1, -1)
        return jnp.pad(a, ((0, 0), (0, -a.shape[1] % HD))).reshape(-1, HD)

    smalls = [n for n in names if n not in big]
    pk = lambda pre: jnp.concatenate([flat(grads[n] if pre == "g" else args[pre + n]) for n in smalls], axis=0)
    cat = [pk(""), pk("g"), pk("m_"), pk("v_")]
    padr = -cat[0].shape[0] % 8
    cat = [jnp.pad(a, ((0, padr), (0, 0))) for a in cat]
    res = _adamw_call("adamw_small", *cat)
    at = 0
    for n in smalls:
        shape = args[n].shape
        size = math.prod(shape)
        nrow = -(-size // HD)
        for dst, src in zip((delta, new_m, new_v), res):
            dst[n] = src[at:at + nrow].reshape(-1)[:size].reshape(shape)
        at += nrow

    return (loss, grad_x[None], *[grads[n] for n in names], *[delta[n] for n in names],
            *[new_m[n] for n in names], *[new_v[n] for n in names])


class _StepComm:
    def __init__(self, shard_groups, after):
        self.groups, shards = {}, []
        for key, ws in shard_groups.items():
            self.groups[key] = list(range(len(shards), len(shards) + len(ws)))
            shards += [w.astype(BF16) for w in ws]
        self.gather = _Gather("ag", shards, after)
        self.passed, self.scatters = set(), {}

    def start_deps(self):
        return [self.gather.token]

    def pass_on(self, key, after):
        self.passed.add(key)
        return self.gather.pass_on(self.groups[key], after)

    def weights(self, key, after):
        if key not in self.passed:
            after = self.pass_on(key, after)
        return self.gather.get(self.groups[key], after)

    def send(self, tag, grads):
        blocks = [g if g.ndim == 3 else g.reshape(N_DEV, g.shape[0] // N_DEV, g.shape[1]) for g in grads]
        self.scatters[tag] = _Scatter("rs_" + tag, blocks, ())
        return [self.scatters[tag].token]

    def mid(self, tag, after):
        self.scatters[tag].mid(after)
        return [self.scatters[tag].token]

    def finish(self, after):
        return {tag: sc.end(after) for tag, sc in self.scatters.items()}


def _local_step(xs, ms, tgt, norm_mix, fox_f_bias, fox_q_norm, fox_k_norm, gdn_a_log, gdn_dt_bias, gdn_out_norm,
                mem_norm, mem_q_norm, mem_k_norm, norm_ffn, conv_all, comm):
    t, d = xs.shape
    bq = min(t, 256)
    fb, alog, dtb = _lanes(fox_f_bias, L_FF), _lanes(gdn_a_log, L_GA), _lanes(gdn_dt_bias, L_GA)
    flat = lambda w: w.reshape(-1, w.shape[-1])

    rms1 = lambda a, g: (_rms(a, g),)
    (u,) = _rowwise("norm_mix", rms1, [xs], [norm_mix], [(d, BF16)], min(t, 256), deps=comm.start_deps())
    w_in_b = flat(comm.weights("in_b", [u])[0])
    pb = _matmul("proj_in_b", u, w_in_b, NN, F32, 1024, 768)
    o_fox = _fox_fwd(pb, fb, fox_q_norm, fox_k_norm, bq)
    w_in_a = flat(comm.weights("in_a", [o_fox])[0])
    pa = _matmul("proj_in_a", u, w_in_a, NN, F32, 1024, 768)
    gdn_terms = _gdn_fwd(pa, pb, conv_all, alog, dtb)
    o_gdn_raw, gdn_states = _gdn_scan(gdn_terms)
    gdn_saved = list(gdn_terms) + [gdn_states]
    zrow = (pa, NG * HD, GZ * HD // (NG * HD))
    (o_gdn,) = _rowwise("gdn_post", _gdn_post, [o_gdn_raw, zrow], [gdn_out_norm], [(NG * HD, BF16)], min(t, 256))
    w_kv_all, w_out_all = [flat(w) for w in comm.weights("kv_out", [o_gdn])]
    (mem_n,) = _rowwise("norm_mem", rms1, [ms], [mem_norm], [(d, BF16)], ms.shape[0])
    mkv = _matmul("proj_mem", mem_n, w_kv_all, NN, F32, 256, 512)
    o_mem = _mem_fwd(pb, mkv, mem_q_norm, mem_k_norm)
    deps = comm.pass_on("gate_up", [o_mem])
    mix = jnp.concatenate([o_fox, o_gdn, o_mem], axis=1)
    h1, h1n = _proj_out_norm(mix, w_out_all, xs, norm_ffn, deps)
    (wgu,) = comm.weights("gate_up", [h1n])
    ffw = wgu.shape[2]
    gu, act = _ffn_up(h1n, wgu.reshape(2, 4, d, ffw))
    w_down_all = flat(comm.weights("down", [act])[0])
    dy, dyb, lsum = _ffn_down_loss(act, w_down_all, h1, tgt)
    loss_local = (0.5 / d) * jnp.sum(lsum[::8, ::HD])

    dgu = _ffn_down_bwd(dyb, w_down_all.reshape(4, ffw, d), gu).reshape(8, t, ffw)
    g_w_down = _matmul("grad_w_down", act, dyb, TN, BF16, 512, 2048)
    dh1n = _ffn_up_bwd_x(dgu, wgu)
    g_w_gu = _ffn_up_bwd_w(h1n, dgu)
    deps = comm.send("ffn", [g_w_down, g_w_gu])
    rms2 = lambda a, g: (_rms(a, g), a)
    dh1, dh1b, g_norm_ffn = _rowwise_vjp("norm_ffn_bwd", rms2, [h1], [norm_ffn], [dh1n, dy], [(F32, BF16)],
                                         min(t, 256), deps=deps)

    dmix = _matmul("proj_out_bwd_x", dh1b, w_out_all, NT, F32, 1024, 1024)
    g_w_out = _matmul("grad_w_out", mix, dh1b, TN, BF16, 1024, 2048)
    deps = comm.mid("ffn", [dmix, g_w_out])
    dmq, dmk, dmv, g_mqn, g_mkn = _mem_bwd(pb, mkv, mem_q_norm, mem_k_norm, dmix, deps=deps)
    dmkv = jnp.concatenate([dmk, dmv], axis=1).astype(BF16)
    g_w_kv = _matmul("grad_w_kv", mem_n, dmkv, TN, BF16, 512, 512)
    do_raw, dgz, g_gon = _rowwise_vjp("gdn_post_bwd", _gdn_post, [o_gdn_raw, zrow], [gdn_out_norm],
                                      [(dmix, NG * HD, 1)], [F32, BF16], min(t, 256), deps=deps)
    dterms = _gdn_bwd_scan(gdn_saved, do_raw)
    dgq, dgk, dgv, dsm_gdn, dwq, dwk, dwv, g_alog, g_dtb = _gdn_bwd(pa, pb, conv_all, alog, dtb, dterms)
    dp_a = jnp.concatenate([dgq, dgk, dgv, dgz], axis=1)
    g_w_in_a = _matmul("grad_w_in_a", u, dp_a, TN, BF16, 512, 3072)
    deps = comm.send("a", [g_w_in_a, g_w_out, g_w_kv])
    du_a = _matmul("proj_in_bwd_a", dp_a, w_in_a, NT, F32, 1024, 1024, deps=deps)
    deps = comm.mid("a", [du_a])
    dfq, dfk, dfv, dsm_fox, g_fb, g_fqn, g_fkn = _fox_bwd(pb, fb, fox_q_norm, fox_k_norm, dmix, bq, deps=deps)
    dp_b = jnp.concatenate([dfq, dfk, dfv, dmq, (dsm_fox + dsm_gdn).astype(BF16), jnp.zeros((t, HD), BF16)], axis=1)
    g_w_in_b = _matmul("grad_w_in_b", u, dp_b, TN, BF16, 512, 3072)
    deps = comm.send("b", [g_w_in_b])
    dmem_n = _matmul("proj_mem_bwd_x", dmkv, w_kv_all, NT, F32, 256, 512, deps=deps)
    g_mem_norm = _rowwise_vjp("norm_mem_bwd", rms1, [ms], [mem_norm], [dmem_n], [], ms.shape[0])[0]
    deps = comm.mid("b", [g_mem_norm])
    du = _matmul("proj_in_bwd_b", dp_b, w_in_b, NT, F32, 1024, 1024, residual=du_a, deps=deps)
    grad_x, g_norm_mix = _rowwise_vjp("norm_mix_bwd", rms2, [xs], [norm_mix], [du, dh1], [F32], min(t, 256))

    small_grads = {
        "norm_mix": g_norm_mix, "mem_norm": g_mem_norm, "norm_ffn": g_norm_ffn,
        "gdn_conv": jnp.concatenate([dwq, dwk, dwv], axis=1),
        "fox_q_norm": g_fqn, "fox_k_norm": g_fkn, "gdn_out_norm": g_gon, "mem_q_norm": g_mqn, "mem_k_norm": g_mkn,
        "fox_f_bias": g_fb, "gdn_a_log": g_alog, "gdn_dt_bias": g_dtb}
    return grad_x, loss_local, small_grads
```

```python
import functools
import math

import jax
import jax.numpy as jnp
from jax import lax
from jax.experimental import pallas as pl
from jax.experimental.pallas import tpu as pltpu

F32 = jnp.float32
BF16 = jnp.bfloat16
HI = lax.Precision.HIGHEST
SDS = jax.ShapeDtypeStruct

N_DEV = 8
HD = 128
NF, NG, NM = 6, 6, 4
CHUNK = 64
GROUP = 16
NORM_EPS = 1e-6
GQ, GK, GV, GZ = 0, 6, 12, 18
FQ, FK, FV, MQ, SM = 0, 6, 12, 18, 22
HALF = 24 * HD
L_FF, L_GA, L_GB = 0, 6, 12
VMEM_LIMIT = 56 * 1024 * 1024

ADAM_LR, ADAM_B1, ADAM_B2, ADAM_EPS, ADAM_WD, ADAM_STEP = 0.001, 0.9, 0.999, 1e-08, 0.01, 10

NN = (((1,), (0,)), ((), ()))
NT = (((1,), (1,)), ((), ()))
TN = (((0,), (0,)), ((), ()))
MESH = pl.DeviceIdType.MESH


def _cp(*sem):
    return pltpu.CompilerParams(dimension_semantics=tuple(sem) if sem else None, vmem_limit_bytes=VMEM_LIMIT)


def _dot(a, b, dims=NN):
    return lax.dot_general(a, b, dims, preferred_element_type=F32)


def _bdot(a, b):
    return _dot(a.astype(BF16), b.astype(BF16))


def _iota(shape, axis):
    return lax.broadcasted_iota(jnp.int32, shape, axis)


def _rms(x, gain):
    return x * lax.rsqrt(jnp.mean(x * x, axis=-1, keepdims=True) + NORM_EPS) * gain


def _sigmoid(x):
    return 0.5 * jnp.tanh(0.5 * x) + 0.5


def _silu(x):
    return x * _sigmoid(x)


def _softplus(x):
    return jnp.maximum(x, 0.0) + jnp.log(1.0 + jnp.exp(-jnp.abs(x)))


def _lane_pick(x, lane):
    oh = (_iota((1, x.shape[-1]), 1) == lane).astype(F32)
    return jnp.sum(x * oh, axis=-1, keepdims=True)


def _cumsum_rows(x):
    tril = (_iota((HD, HD), 0) >= _iota((HD, HD), 1)).astype(F32)
    carry = jnp.zeros((1, x.shape[1]), F32)
    outs = []
    for b in range(x.shape[0] // HD):
        blk = x[b * HD:(b + 1) * HD]
        outs.append(jnp.dot(tril, blk, precision=HI, preferred_element_type=F32) + carry)
        carry = carry + jnp.sum(blk, axis=0, keepdims=True)
    return jnp.concatenate(outs, axis=0)


def _row_spec(r, tm):
    if isinstance(r, tuple):
        arr, width, cb = r
        return arr, pl.BlockSpec((tm, width), lambda i, cb=cb: (i, cb))
    return r, pl.BlockSpec((tm, r.shape[1]), lambda i: (i, 0))


ANY_SPEC = pl.BlockSpec(memory_space=pl.ANY)


def _rowwise(name, fn, rows, consts, outs, tm, deps=()):
    arrs, specs = zip(*[_row_spec(r, tm) for r in rows])
    n_rows = arrs[0].shape[0]
    nr, nc, nd = len(rows), len(consts), len(deps)

    def body(*refs):
        res = fn(*[r[...] for r in refs[:nr + nc]])
        for o, v in zip(refs[nr + nc + nd:], res):
            o[...] = v.astype(o.dtype)

    return pl.pallas_call(
        body, grid=(n_rows // tm,), name=name,
        in_specs=list(specs) + [pl.BlockSpec(c.shape, lambda i: (0, 0)) for c in consts] + [ANY_SPEC] * nd,
        out_specs=[pl.BlockSpec((tm, w), lambda i: (i, 0)) for w, _ in outs],
        out_shape=[SDS((n_rows, w), dt) for w, dt in outs],
        compiler_params=_cp("parallel"),
    )(*arrs, *consts, *deps)


def _rowwise_vjp(name, fn, rows, consts, cts, grad_dtypes, tm, deps=()):
    arrs, specs = zip(*[_row_spec(r, tm) for r in rows])
    ct_arrs, ct_specs = zip(*[_row_spec(r, tm) for r in cts])
    n_rows = arrs[0].shape[0]
    nr, nc, nct, nd = len(rows), len(consts), len(cts), len(deps)
    plan = [(j, dt) for j, dts in enumerate(grad_dtypes) for dt in (dts if isinstance(dts, tuple) else (dts,))]
    ng = len(plan)
    widths = [specs[j].block_shape[1] for j, _ in plan]
    grad_dtypes = [dt for _, dt in plan]

    def body(*refs):
        vals = [r[...].astype(F32) for r in refs[:nr + nc]]
        ctv = tuple(r[...].astype(F32) for r in refs[nr + nc:nr + nc + nct])
        _, vjp = jax.vjp(fn, *vals)
        grads = vjp(ctv)
        outs = refs[nr + nc + nct + nd:]
        for o, (j, _) in zip(outs[:ng], plan):
            o[...] = grads[j].astype(o.dtype)

        @pl.when(pl.program_id(0) == 0)
        def _():
            for o in outs[ng:]:
                o[...] = jnp.zeros_like(o)

        for o, g in zip(outs[ng:], grads[nr:]):
            o[...] += g

    return pl.pallas_call(
        body, grid=(n_rows // tm,), name=name,
        in_specs=list(specs) + [pl.BlockSpec(c.shape, lambda i: (0, 0)) for c in consts] + list(ct_specs)
        + [ANY_SPEC] * nd,
        out_specs=[pl.BlockSpec((tm, w), lambda i: (i, 0)) for w in widths]
        + [pl.BlockSpec(c.shape, lambda i: (0, 0)) for c in consts],
        out_shape=[SDS((n_rows, w), dt) for w, dt in zip(widths, grad_dtypes)] + [SDS(c.shape, F32) for c in consts],
        compiler_params=_cp("arbitrary"),
    )(*arrs, *consts, *ct_arrs, *deps)


def _tile(n, pref):
    t = min(n, pref)
    while n % t or (t % HD and t != n):
        t -= 1
    return t


def _matmul(name, a, b, dims, out_dtype, tm, tn, residual=None, deps=()):
    ta, tb = dims == TN, dims == NT
    m = a.shape[1] if ta else a.shape[0]
    k = a.shape[0] if ta else a.shape[1]
    n = b.shape[0] if tb else b.shape[1]
    tm, tn = _tile(m, tm), _tile(n, tn)

    def body(*refs):
        acc = _dot(refs[0][...], refs[1][...], dims)
        if residual is not None:
            acc = acc + refs[2][...]
        refs[-1][...] = acc.astype(out_dtype)

    in_specs = [pl.BlockSpec((k, tm), lambda i, j: (0, i)) if ta else pl.BlockSpec((tm, k), lambda i, j: (i, 0)),
                pl.BlockSpec((tn, k), lambda i, j: (j, 0)) if tb else pl.BlockSpec((k, tn), lambda i, j: (0, j))]
    ops = [a, b]
    if residual is not None:
        in_specs.append(pl.BlockSpec((tm, tn), lambda i, j: (i, j)))
        ops.append(residual)
    in_specs += [ANY_SPEC] * len(deps)
    ops += list(deps)
    return pl.pallas_call(
        body, grid=(m // tm, n // tn), name=name, in_specs=in_specs,
        out_specs=pl.BlockSpec((tm, tn), lambda i, j: (i, j)), out_shape=SDS((m, n), out_dtype),
        compiler_params=_cp("parallel", "parallel"),
    )(*ops)


def _proj_out_norm(mix, w_out, xs, gain, deps):
    t, k = mix.shape
    d = w_out.shape[1]
    tm = _tile(t, 512)

    def body(*refs):
        a, b, x, g = refs[:4]
        h1, h1n = refs[4 + len(deps):]
        acc = _dot(a[...], b[...]) + x[...]
        h1[...] = acc
        h1n[...] = _rms(acc, g[...]).astype(BF16)

    return pl.pallas_call(
        body, grid=(t // tm,), name="proj_out",
        in_specs=[pl.BlockSpec((tm, k), lambda i: (i, 0)), pl.BlockSpec((k, d), lambda i: (0, 0)),
                  pl.BlockSpec((tm, d), lambda i: (i, 0)), pl.BlockSpec((1, d), lambda i: (0, 0))] + [ANY_SPEC] * len(deps),
        out_specs=[pl.BlockSpec((tm, d), lambda i: (i, 0))] * 2, out_shape=[SDS((t, d), F32), SDS((t, d), BF16)],
        compiler_params=_cp("parallel"),
    )(mix, w_out, xs, gain, *deps)


def _ffn_up(h1n, wgu):
    t, d = h1n.shape
    w = wgu.shape[3]
    tm = _tile(t, 512)

    def body(a, b, gu, act):
        x = a[...]
        g = _dot(x, b[0])
        u = _dot(x, b[1])
        gu[0] = g.astype(BF16)
        gu[1] = u.astype(BF16)
        act[...] = (_silu(g) * u).astype(BF16)

    return pl.pallas_call(
        body, grid=(4, t // tm), name="ffn_up",
        in_specs=[pl.BlockSpec((tm, d), lambda j, i: (i, 0)), pl.BlockSpec((2, None, d, w), lambda j, i: (0, j, 0, 0))],
        out_specs=[pl.BlockSpec((2, None, tm, w), lambda j, i: (0, j, i, 0)), pl.BlockSpec((tm, w), lambda j, i: (i, j))],
        out_shape=[SDS((2, 4, t, w), BF16), SDS((t, 4 * w), BF16)],
        compiler_params=_cp("parallel", "parallel"),
    )(h1n, wgu)


def _ffn_down_loss(act, wdown, h1, target):
    t, f = act.shape
    d = wdown.shape[1]
    tm, tn = _tile(t, 1024), _tile(d, 512)

    def body(a, b, h, tg, dyb, ls):
        e = _dot(a[...], b[...]) + h[...] - tg[...]
        dyb[...] = (e * (1.0 / d)).astype(BF16)
        ls[...] = jnp.broadcast_to(jnp.sum(e * e), (8, HD))

    return pl.pallas_call(
        body, grid=(t // tm, d // tn), name="ffn_down_loss",
        in_specs=[pl.BlockSpec((tm, f), lambda i, j: (i, 0)), pl.BlockSpec((f, tn), lambda i, j: (0, j)),
                  pl.BlockSpec((tm, tn), lambda i, j: (i, j)), pl.BlockSpec((tm, tn), lambda i, j: (i, j))],
        out_specs=[pl.BlockSpec((tm, tn), lambda i, j: (i, j)), pl.BlockSpec((8, HD), lambda i, j: (i, j))],
        out_shape=[SDS((t, d), BF16), SDS((8 * (t // tm), HD * (d // tn)), F32)],
        compiler_params=_cp("parallel", "parallel"),
    )(act, wdown, h1, target)


def _ffn_down_bwd(dyb, wdown4, gu):
    t, d = dyb.shape
    w = wdown4.shape[1]
    tm = _tile(t, 512)

    def body(a, b, gu_ref, out):
        da = _dot(a[...], b[...], NT)
        g = gu_ref[0].astype(F32)
        u = gu_ref[1].astype(F32)
        s = _sigmoid(g)
        out[0] = (da * u * (s * (1.0 + g * (1.0 - s)))).astype(BF16)
        out[1] = (da * g * s).astype(BF16)

    return pl.pallas_call(
        body, grid=(4, t // tm), name="ffn_down_bwd",
        in_specs=[pl.BlockSpec((tm, d), lambda j, i: (i, 0)), pl.BlockSpec((None, w, d), lambda j, i: (j, 0, 0)),
                  pl.BlockSpec((2, None, tm, w), lambda j, i: (0, j, i, 0))],
        out_specs=pl.BlockSpec((2, None, tm, w), lambda j, i: (0, j, i, 0)),
        out_shape=SDS((2, 4, t, w), BF16),
        compiler_params=_cp("parallel", "parallel"),
    )(dyb, wdown4, gu)


def _ffn_up_bwd_x(dgu, wgu):
    _, t, w = dgu.shape
    d = wgu.shape[1]
    tm = _tile(t, 512)

    def body(a, b, out):
        @pl.when(pl.program_id(1) == 0)
        def _():
            out[...] = jnp.zeros_like(out)
        out[...] += _dot(a[...], b[...], NT)

    return pl.pallas_call(
        body, grid=(t // tm, 8), name="ffn_up_bwd_x",
        in_specs=[pl.BlockSpec((None, tm, w), lambda i, j: (j, i, 0)), pl.BlockSpec((None, d, w), lambda i, j: (j, 0, 0))],
        out_specs=pl.BlockSpec((tm, d), lambda i, j: (i, 0)), out_shape=SDS((t, d), F32),
        compiler_params=_cp("parallel", "arbitrary"),
    )(dgu, wgu)


def _ffn_up_bwd_w(h1n, dgu):
    _, t, w = dgu.shape
    d = h1n.shape[1]
    tm = _tile(d, 512)

    def body(a, b, out):
        out[...] = _dot(a[...], b[...], TN).astype(BF16)

    return pl.pallas_call(
        body, grid=(8, d // tm), name="ffn_up_bwd_w",
        in_specs=[pl.BlockSpec((t, tm), lambda j, i: (0, i)), pl.BlockSpec((None, t, w), lambda j, i: (j, 0, 0))],
        out_specs=pl.BlockSpec((None, tm, w), lambda j, i: (j, i, 0)), out_shape=SDS((8, d, w), BF16),
        compiler_params=_cp("parallel", "parallel"),
    )(h1n, dgu)


def _fox_prep(fq, fk, sm, fb, qg, kg, h):
    qn = _rms(fq, qg)
    kn = _rms(fk, kg)
    c = _cumsum_rows(-_softplus(-(sm + fb)))
    ccol = _lane_pick(c, L_FF + h)
    crow = jnp.sum(c.T * (_iota((HD, 1), 0) == L_FF + h).astype(F32), axis=0, keepdims=True)
    return qn, kn, ccol, crow


def _softmax_times(s, v):
    e = jnp.exp(s - lax.stop_gradient(jnp.max(s, axis=1, keepdims=True)))
    return _dot(e.astype(BF16), v.astype(BF16)) * (1.0 / jnp.sum(e, axis=1, keepdims=True))


def _fox_block(q, k, v, cc, cr, off):
    bq = q.shape[0]
    assert k.shape[0] == off + bq
    s = _dot((q * (HD ** -0.5)).astype(BF16), k.astype(BF16), NT) + cc - cr
    diag = jnp.where(_iota((bq, bq), 1) <= _iota((bq, bq), 0), s[:, off:], -1e30)
    s = jnp.concatenate([s[:, :off], diag], axis=1) if off else diag
    return _softmax_times(s, v)


ONE_BUFFER = pl.Buffered(1)


def _pcol(t, cb):
    return pl.BlockSpec((t, HD), lambda h, cb=cb: (0, cb + h), pipeline_mode=ONE_BUFFER)


def _smcol(t):
    return pl.BlockSpec((t, HD), lambda h: (0, SM), pipeline_mode=ONE_BUFFER)


def _head(t):
    return pl.BlockSpec((t, HD), lambda h: (0, h), pipeline_mode=ONE_BUFFER)


def _small(n):
    return pl.BlockSpec((n, HD), lambda h: (0, 0), pipeline_mode=ONE_BUFFER)


def _fox_fwd(p, fb, qg, kg, bq):
    t = p.shape[0]

    def body(fq, fk, fv, sm, fb_r, qg_r, kg_r, o, qn_s, cc_s):
        h = pl.program_id(0)
        qn, kn, ccol, crow = _fox_prep(fq[...], fk[...], sm[...], fb_r[...], qg_r[...], kg_r[...], h)
        qn_s[...] = qn
        cc_s[...] = ccol
        knb = kn.astype(BF16)
        vb = fv[...].astype(BF16)
        for i in range(t // bq):
            rows, ext = pl.ds(i * bq, bq), (i + 1) * bq
            o[rows, :] = _fox_block(qn_s[rows, :], knb[:ext], vb[:ext], cc_s[rows, :], crow[:, :ext], i * bq).astype(o.dtype)

    return pl.pallas_call(
        body, grid=(NF,), name="fox_fwd",
        in_specs=[_pcol(t, FQ), _pcol(t, FK), _pcol(t, FV), _smcol(t), _small(1), _small(1), _small(1)],
        out_specs=_head(t), out_shape=SDS((t, NF * HD), BF16),
        scratch_shapes=[pltpu.VMEM((t, HD), F32), pltpu.VMEM((t, 1), F32)],
        compiler_params=_cp("parallel"),
    )(p, p, p, p, fb, qg, kg)


def _fox_bwd(p, fb, qg, kg, dmix, bq, deps=()):
    t = p.shape[0]

    def body(*refs):
        fq, fk, fv, sm, fb_r, qg_r, kg_r, do = refs[:8]
        dfq, dfk, dfv, dsm, dfb, dqg, dkg, qn_s, cc_s, dqn_s, dcc_s, dkn_s, dv_s, dcr_s = refs[8 + len(deps):]
        h = pl.program_id(0)
        qn, kn, ccol, crow = _fox_prep(fq[...], fk[...], sm[...], fb_r[...], qg_r[...], kg_r[...], h)
        qn_s[...] = qn
        cc_s[...] = ccol
        v = fv[...]
        dkn_s[...] = jnp.zeros_like(dkn_s)
        dv_s[...] = jnp.zeros_like(dv_s)
        dcr_s[...] = jnp.zeros_like(dcr_s)

        for i in range(t // bq):
            rows, ext = pl.ds(i * bq, bq), (i + 1) * bq
            _, vjp = jax.vjp(lambda a, b, c, d, e, off=i * bq: _fox_block(a, b, c, d, e, off),
                             qn_s[rows, :], kn[:ext], v[:ext], cc_s[rows, :], crow[:, :ext])
            dq, dk, dv, dcc, dcr = vjp(do[rows, :].astype(F32))
            dqn_s[rows, :] = dq
            dcc_s[rows, :] = dcc
            dkn_s[:ext, :] += dk
            dv_s[:ext, :] += dv
            dcr_s[:, :ext] += dcr
        _, prep_vjp = jax.vjp(lambda a, b, c, d, e, f: _fox_prep(a, b, c, d, e, f, h),
                              fq[...], fk[...], sm[...], fb_r[...], qg_r[...], kg_r[...])
        g_fq, g_fk, g_sm, g_fb, g_qg, g_kg = prep_vjp((dqn_s[...], dkn_s[...], dcc_s[...], dcr_s[...]))
        dfq[...] = g_fq.astype(dfq.dtype)
        dfk[...] = g_fk.astype(dfk.dtype)
        dfv[...] = dv_s[...].astype(dfv.dtype)

        @pl.when(h == 0)
        def _():
            for r in (dsm, dfb, dqg, dkg):
                r[...] = jnp.zeros_like(r)

        dsm[...] += g_sm
        dfb[...] += g_fb
        dqg[...] += g_qg
        dkg[...] += g_kg

    head = _head(t)
    return pl.pallas_call(
        body, grid=(NF,), name="fox_bwd",
        in_specs=[_pcol(t, FQ), _pcol(t, FK), _pcol(t, FV), _smcol(t), _small(1), _small(1), _small(1), head]
        + [ANY_SPEC] * len(deps),
        out_specs=[head, head, head, _small(t), _small(1), _small(1), _small(1)],
        out_shape=[SDS((t, NF * HD), BF16)] * 3 + [SDS((t, HD), F32)] + [SDS((1, HD), F32)] * 3,
        scratch_shapes=[pltpu.VMEM((t, HD), F32), pltpu.VMEM((t, 1), F32), pltpu.VMEM((t, HD), F32),
                        pltpu.VMEM((t, 1), F32), pltpu.VMEM((t, HD), F32), pltpu.VMEM((t, HD), F32),
                        pltpu.VMEM((1, t), F32)],
        compiler_params=_cp("arbitrary"),
    )(p, p, p, p, fb, qg, kg, dmix, *deps)


def _mem_attn(mq, mk, mv, qg, kg):
    s = _dot((_rms(mq, qg) * (HD ** -0.5)).astype(BF16), _rms(mk, kg).astype(BF16), NT)
    return _softmax_times(s, mv)


def _mem_fwd(p, mkv, qg, kg):
    t, ml = p.shape[0], mkv.shape[0]

    def body(mq, mk, mv, qg_r, kg_r, o):
        o[...] = _mem_attn(mq[...], mk[...], mv[...], qg_r[...], kg_r[...]).astype(o.dtype)

    return pl.pallas_call(
        body, grid=(NM,), name="mem_fwd",
        in_specs=[_pcol(t, MQ), pl.BlockSpec((ml, HD), lambda h: (0, h)), pl.BlockSpec((ml, HD), lambda h: (0, NM + h)),
                  _small(1), _small(1)],
        out_specs=pl.BlockSpec((t, HD), lambda h: (0, h)), out_shape=SDS((t, NM * HD), BF16),
        compiler_params=_cp("parallel"),
    )(p, mkv, mkv, qg, kg)


def _mem_bwd(p, mkv, qg, kg, dmix, deps=()):
    t, ml = p.shape[0], mkv.shape[0]

    def body(*refs):
        mq, mk, mv, qg_r, kg_r, do = refs[:6]
        dmq, dmk, dmv, dqg, dkg = refs[6 + len(deps):]
        _, vjp = jax.vjp(_mem_attn, mq[...], mk[...], mv[...], qg_r[...], kg_r[...])
        g_q, g_k, g_v, g_qg, g_kg = vjp(do[...].astype(F32))
        dmq[...] = g_q.astype(dmq.dtype)
        dmk[...] = g_k
        dmv[...] = g_v

        @pl.when(pl.program_id(0) == 0)
        def _():
            dqg[...] = jnp.zeros_like(dqg)
            dkg[...] = jnp.zeros_like(dkg)

        dqg[...] += g_qg
        dkg[...] += g_kg

    return pl.pallas_call(
        body, grid=(NM,), name="mem_bwd",
        in_specs=[_pcol(t, MQ), pl.BlockSpec((ml, HD), lambda h: (0, h)), pl.BlockSpec((ml, HD), lambda h: (0, NM + h)),
                  _small(1), _small(1), pl.BlockSpec((t, HD), lambda h: (0, NF + NG + h))] + [ANY_SPEC] * len(deps),
        out_specs=[pl.BlockSpec((t, HD), lambda h: (0, h)), pl.BlockSpec((ml, HD), lambda h: (0, h)),
                   pl.BlockSpec((ml, HD), lambda h: (0, h)), _small(1), _small(1)],
        out_shape=[SDS((t, NM * HD), BF16), SDS((ml, NM * HD), F32), SDS((ml, NM * HD), F32),
                   SDS((1, HD), F32), SDS((1, HD), F32)],
        compiler_params=_cp("arbitrary"),
    )(p, mkv, mkv, qg, kg, dmix, *deps)


def _shift_down(x, s):
    if s == 0:
        return x
    return jnp.where(_iota(x.shape, 0) >= s, pltpu.roll(x, s, 0), 0.0)


def _shift_up(x, s):
    if s == 0:
        return x
    n = x.shape[0]
    return jnp.where(_iota(x.shape, 0) < n - s, pltpu.roll(x, n - s, 0), 0.0)


@jax.custom_vjp
def _conv4(x, w0, w1, w2, w3):
    return w0 * _shift_down(x, 3) + w1 * _shift_down(x, 2) + w2 * _shift_down(x, 1) + w3 * x


def _conv4_fwd(x, w0, w1, w2, w3):
    return _conv4(x, w0, w1, w2, w3), (x, w0, w1, w2, w3)


def _conv4_bwd(res, dy):
    x, w0, w1, w2, w3 = res
    ups = [_shift_up(dy, 3 - k) for k in range(4)]
    dx = w0 * ups[0] + w1 * ups[1] + w2 * ups[2] + w3 * ups[3]
    return (dx,) + tuple(jnp.sum(up * x, axis=0, keepdims=True) for up in ups)


_conv4.defvjp(_conv4_fwd, _conv4_bwd)


HALO = 8


def _gdn_prep(gq, gk, gv, sm, taps, alog, dtb, h):
    q, k, v = [_silu(_conv4(x, *taps[4 * j:4 * j + 4]))[HALO:] for j, x in enumerate((gq, gk, gv))]
    q = q * lax.rsqrt(jnp.sum(q * q, axis=-1, keepdims=True) + NORM_EPS) * (HD ** -0.5)
    k = k * lax.rsqrt(jnp.sum(k * k, axis=-1, keepdims=True) + NORM_EPS)
    g = _lane_pick(-jnp.exp(alog) * _softplus(sm + dtb), L_GA + h)
    beta = _lane_pick(_sigmoid(sm), L_GB + h)
    return q, k, v, g, beta


def _split(x, n):
    parts, rest = [], x
    for i in range(n):
        parts.append(rest.astype(BF16))
        if i + 1 < n:
            rest = rest - parts[-1].astype(F32)
    return parts


def _raw_dot(a, b, form):
    lead = a.ndim - 2
    ca, cb = {"nn": (1, 0), "nt": (1, 1), "tn": (0, 0)}[form]
    batch = ((0,), (0,)) if lead else ((), ())
    return lax.dot_general(a, b, (((ca + lead,), (cb + lead,)), batch), preferred_element_type=F32)


def _pdot_impl(a, b, form, mode):
    if mode == "1":
        return _raw_dot(a.astype(BF16), b.astype(BF16), form)
    if mode == "3":
        (ah, al), (bh, bl) = _split(a, 2), _split(b, 2)
        return _raw_dot(ah, bh, form) + (_raw_dot(al, bh, form) + _raw_dot(ah, bl, form))
    if mode == "xa":
        return sum(_raw_dot(a.astype(BF16), t, form) for t in reversed(_split(b, 3)))
    return sum(_raw_dot(t, b.astype(BF16), form) for t in reversed(_split(a, 3)))


@functools.partial(jax.custom_vjp, nondiff_argnums=(2, 3))
def _pdot(a, b, form, mode):
    return _pdot_impl(a, b, form, mode)


def _pdot_fwd(a, b, form, mode):
    return _pdot_impl(a, b, form, mode), (a, b)


def _pdot_bwd(form, mode, res, ct):
    a, b = res
    da_args, db_args = {"nn": ((ct, b, "nt"), (a, ct, "tn")), "nt": ((ct, b, "nn"), (ct, a, "tn")),
                        "tn": ((b, ct, "nt"), (a, ct, "nn"))}[form]

    def side(args, exact):
        if mode in ("1", "3"):
            return mode
        return "xa" if args[0] is exact else "xb"

    if mode == "xa":
        return jnp.zeros_like(a), _pdot_impl(*db_args, side(db_args, a))
    if mode == "xb":
        return _pdot_impl(*da_args, side(da_args, b)), jnp.zeros_like(b)
    return _pdot_impl(*da_args, mode), _pdot_impl(*db_args, mode)


_pdot.defvjp(_pdot_fwd, _pdot_bwd)

GDN_QK, GDN_INV, GDN_SCAN = "1", "1", "1"


@jax.custom_vjp
def _tri_inv(low):
    eye = (_iota((CHUNK, CHUNK), 0) == _iota((CHUNK, CHUNK), 1)).astype(F32)
    inv = eye - low
    pw = low
    for _ in range(5):
        pw = _pdot_impl(pw, pw, "nn", GDN_INV)
        inv = inv + _pdot_impl(inv, pw, "nn", GDN_INV)
    return inv


def _tri_inv_fwd(low):
    inv = _tri_inv(low)
    return inv, inv


def _tri_inv_bwd(inv, ct):
    return (-_pdot_impl(_pdot_impl(inv, ct, "tn", GDN_INV), inv, "nt", GDN_INV),)


_tri_inv.defvjp(_tri_inv_fwd, _tri_inv_bwd)


def _gdn_intra(q, k, v, g, beta):
    n = q.shape[0]
    r, c = _iota((CHUNK, CHUNK), 0), _iota((CHUNK, CHUNK), 1)
    tril, strict = r >= c, r > c
    trilf = jnp.broadcast_to(tril.astype(F32), (n, CHUNK, CHUNK))
    gcm = _pdot(trilf, jnp.broadcast_to(g, (n, CHUNK, CHUNK)), "nn", "xa")
    gcf = _pdot(trilf, jnp.broadcast_to(g, (n, CHUNK, HD)), "nn", "xa")
    lane0 = (_iota((1, 1, CHUNK), 2) == 0).astype(F32)
    gcr = _pdot(jnp.ones((n, CHUNK, CHUNK), F32), gcm * lane0, "nt", "xa")
    decay = jnp.where(tril, jnp.exp(jnp.where(tril, gcm - gcr, 0.0)), 0.0)
    egc = jnp.exp(gcf)
    kb = k * beta
    low = jnp.where(strict, _pdot(kb, k, "nt", GDN_QK) * decay, 0.0)
    inv = _tri_inv(low)
    u = _pdot(inv, v * beta, "nn", GDN_INV)
    w = _pdot(inv, kb * egc, "nn", GDN_INV)
    at = jnp.where(tril, _pdot(q, k, "nt", GDN_QK) * decay, 0.0)
    gl = jnp.sum(jnp.broadcast_to(g, (n, CHUNK, HD)), axis=1, keepdims=True)
    return u, w, q * egc, at, k * jnp.exp(gl - gcf), gl


def _gdn_step(s, u, w, qg, at, kd, gl):
    vn = u - _pdot(w, s, "nn", GDN_SCAN)
    o = _pdot(qg, s, "nn", GDN_SCAN) + _pdot(at, vn, "nn", GDN_SCAN)
    s2 = s * jnp.exp(gl) + _pdot(kd, vn, "tn", GDN_SCAN)
    return o, s2


SCAN_HEADS = 3


def _gdn_chunked_scratch(nc):
    big = pltpu.VMEM((nc, CHUNK, HD), F32)
    return [big, big, big, pltpu.VMEM((nc, CHUNK, 1), F32), pltpu.VMEM((nc, CHUNK, 1), F32)]


def _gdn_term_shapes(nc):
    return [(nc, CHUNK, HD), (nc, CHUNK, HD), (nc, CHUNK, HD), (nc, CHUNK, CHUNK), (nc, CHUNK, HD), (nc, 1, HD)]


def _per_head(shape, heads=None, one_buffer=True):
    lead = (None,) if heads is None else (heads,)
    return pl.BlockSpec(lead + tuple(shape), lambda h: (h,) + (0,) * len(shape),
                        pipeline_mode=ONE_BUFFER if one_buffer else None)


def _gdn_in_specs(t):
    cw = lambda cb: pl.BlockSpec((4, HD), lambda h, cb=cb: (0, cb + h))
    return [_pcol(t, GQ), _pcol(t, GK), _pcol(t, GV), _smcol(t), cw(0), cw(NG), cw(2 * NG), _small(1), _small(1)]


def _taps(wq, wk, wv):
    return tuple(w[k:k + 1, :] for w in (wq, wk, wv) for k in range(4))


def _prep_rows(t):
    return min(t, 256)


def _gdn_pad(srcs, pads):
    for src, pad in zip(srcs, pads):
        pad[0:HALO, :] = jnp.zeros((HALO, HD), F32)
        pad[HALO:, :] = src[...]


def _gdn_stage(pads, sm, taps, al, db, h, chunked):
    t = sm.shape[0]
    rows = _prep_rows(t)
    per = rows // CHUNK

    def tile(i, carry):
        r0 = pl.multiple_of(i * rows, rows)
        vals = _gdn_prep(*[p[pl.ds(r0, rows + HALO), :] for p in pads], sm[pl.ds(r0, rows), :], taps, al, db, h)
        for v, r in zip(vals, chunked):
            r[pl.ds(i * per, per)] = v.reshape(per, CHUNK, v.shape[-1])
        return carry

    lax.fori_loop(0, t // rows, tile, 0)


def _gdn_intra_all(chunked, intra):
    nc = chunked[0].shape[0]
    grp_n = math.gcd(nc, GROUP)

    def grp(i, carry):
        sl = pl.ds(pl.multiple_of(i * grp_n, grp_n), grp_n)
        for r, val in zip(intra, _gdn_intra(*[c[sl] for c in chunked])):
            r[sl] = val
        return carry

    lax.fori_loop(0, nc // grp_n, grp, 0)


def _gdn_fwd(pa, pb, conv, alog, dtb):
    t = pa.shape[0]
    nc = t // CHUNK
    terms = _gdn_term_shapes(nc)

    def body(gq, gk, gv, sm, wq, wk, wv, al, db, *rest):
        h = pl.program_id(0)
        intra, chunked, pads = rest[:6], rest[6:11], rest[11:]
        _gdn_pad((gq, gk, gv), pads)
        _gdn_stage(pads, sm, _taps(wq, wk, wv), al[...], db[...], h, chunked)
        _gdn_intra_all(chunked, intra)

    return pl.pallas_call(
        body, grid=(NG,), name="gdn_fwd", in_specs=_gdn_in_specs(t),
        out_specs=[_per_head(sh, one_buffer=False) for sh in terms], out_shape=[SDS((NG,) + sh, F32) for sh in terms],
        scratch_shapes=_gdn_chunked_scratch(nc) + [pltpu.VMEM((t + HALO, HD), F32)] * 3, compiler_params=_cp("parallel"),
    )(pa, pa, pa, pb, conv, conv, conv, alog, dtb)


def _gdn_scan(terms_in):
    nc = terms_in[0].shape[1]
    terms = _gdn_term_shapes(nc)

    def body(*refs):
        intra, o, states = refs[:6], refs[6], refs[7]

        def step(c, ss):
            rows = pl.ds(pl.multiple_of(c * CHUNK, CHUNK), CHUNK)
            loaded = [[r[hh, c] for r in intra] for hh in range(SCAN_HEADS)]
            res = [_gdn_step(ss[hh], *loaded[hh]) for hh in range(SCAN_HEADS)]
            for hh in range(SCAN_HEADS):
                states[hh, c] = ss[hh]
                o[rows, hh * HD:(hh + 1) * HD] = res[hh][0]
            return tuple(r[1] for r in res)

        lax.fori_loop(0, nc, step, tuple(jnp.zeros((HD, HD), F32) for _ in range(SCAN_HEADS)))

    return pl.pallas_call(
        body, grid=(NG // SCAN_HEADS,), name="gdn_scan", in_specs=[_per_head(sh, SCAN_HEADS) for sh in terms],
        out_specs=[pl.BlockSpec((nc * CHUNK, SCAN_HEADS * HD), lambda h: (0, h), pipeline_mode=ONE_BUFFER),
                   _per_head((nc, HD, HD), SCAN_HEADS)],
        out_shape=[SDS((nc * CHUNK, NG * HD), F32), SDS((NG, nc, HD, HD), F32)], compiler_params=_cp("parallel"),
    )(*terms_in)


def _gdn_bwd_scan(saved, do_raw):
    nc = saved[0].shape[1]
    terms = _gdn_term_shapes(nc)

    def body(*refs):
        intra, states, do, outs = refs[:6], refs[6], refs[7], refs[8:]

        def bwd(i, dss):
            c = nc - 1 - i
            rows = pl.ds(pl.multiple_of(c * CHUNK, CHUNK), CHUNK)
            loaded = [[states[hh, c]] + [r[hh, c] for r in intra] for hh in range(SCAN_HEADS)]
            cts = [do[rows, hh * HD:(hh + 1) * HD] for hh in range(SCAN_HEADS)]
            grads = [jax.vjp(_gdn_step, *loaded[hh])[1]((cts[hh], dss[hh])) for hh in range(SCAN_HEADS)]
            for hh in range(SCAN_HEADS):
                for r, gval in zip(outs, grads[hh][1:]):
                    r[hh, c] = gval
            return tuple(g[0] for g in grads)

        lax.fori_loop(0, nc, bwd, tuple(jnp.zeros((HD, HD), F32) for _ in range(SCAN_HEADS)))

    return pl.pallas_call(
        body, grid=(NG // SCAN_HEADS,), name="gdn_bwd_scan",
        in_specs=[_per_head(sh, SCAN_HEADS) for sh in terms] + [_per_head((nc, HD, HD), SCAN_HEADS)]
        + [pl.BlockSpec((nc * CHUNK, SCAN_HEADS * HD), lambda h: (0, h), pipeline_mode=ONE_BUFFER)],
        out_specs=[_per_head(sh, SCAN_HEADS) for sh in terms],
        out_shape=[SDS((NG,) + sh, F32) for sh in terms], compiler_params=_cp("parallel"),
    )(*saved, do_raw)


def _gdn_bwd(pa, pb, conv, alog, dtb, dterms):
    t = pa.shape[0]
    nc = t // CHUNK
    terms = _gdn_term_shapes(nc)

    def body(*refs):
        gq, gk, gv, sm, wq, wk, wv, al, db = refs[:9]
        dintra = refs[9:15]
        dgq, dgk, dgv, dsm, dwq, dwk, dwv, dal, ddb = refs[15:24]
        chunked, pads, dpads, dsm_s = refs[24:29], refs[29:32], refs[32:35], refs[35]
        h = pl.program_id(0)
        taps = _taps(wq, wk, wv)
        _gdn_pad((gq, gk, gv), pads)
        _gdn_stage(pads, sm, taps, al[...], db[...], h, chunked)
        grp_n = math.gcd(nc, GROUP)

        def grp(i, carry):
            sl = pl.ds(pl.multiple_of(i * grp_n, grp_n), grp_n)
            _, vjp = jax.vjp(_gdn_intra, *[r[sl] for r in chunked])
            for r, gval in zip(chunked, vjp(tuple(r[sl] for r in dintra))):
                r[sl] = gval
            return carry

        lax.fori_loop(0, nc // grp_n, grp, 0)

        rows = _prep_rows(t)
        per = rows // CHUNK
        for r in dpads:
            r[...] = jnp.zeros_like(r)

        def tile(i, small):
            r0 = pl.multiple_of(i * rows, rows)
            win = pl.ds(r0, rows + HALO)
            _, vjp = jax.vjp(lambda *a: _gdn_prep(*a, h), *[p[win, :] for p in pads], sm[pl.ds(r0, rows), :],
                             taps, al[...], db[...])
            grads = vjp(tuple(r[pl.ds(i * per, per)].reshape(rows, r.shape[-1]) for r in chunked))
            for r, gval in zip(dpads, grads[:3]):
                r[win, :] += gval
            dsm_s[pl.ds(r0, rows), :] = grads[3]
            return jax.tree.map(jnp.add, small, (grads[4], grads[5], grads[6]))

        zero = jnp.zeros((1, HD), F32)
        dtaps, g_al, g_db = lax.fori_loop(0, t // rows, tile, ((zero,) * 12, zero, zero))
        for r, dpad in zip((dgq, dgk, dgv), dpads):
            r[...] = dpad[HALO:, :].astype(r.dtype)
        for j, r in enumerate((dwq, dwk, dwv)):
            for k in range(4):
                r[k:k + 1, :] = dtaps[4 * j + k]

        @pl.when(h == 0)
        def _():
            for r in (dsm, dal, ddb):
                r[...] = jnp.zeros_like(r)

        dsm[...] += dsm_s[...]
        dal[...] += g_al
        ddb[...] += g_db

    head = _head(t)
    taps = pl.BlockSpec((4, HD), lambda h: (0, h))
    return pl.pallas_call(
        body, grid=(NG,), name="gdn_bwd", in_specs=_gdn_in_specs(t) + [_per_head(sh) for sh in terms],
        out_specs=[head, head, head, _small(t), taps, taps, taps, _small(1), _small(1)],
        out_shape=[SDS((t, NG * HD), BF16)] * 3 + [SDS((t, HD), F32)] + [SDS((4, NG * HD), F32)] * 3 + [SDS((1, HD), F32)] * 2,
        scratch_shapes=_gdn_chunked_scratch(nc) + [pltpu.VMEM((t + HALO, HD), F32)] * 6 + [pltpu.VMEM((t, HD), F32)],
        compiler_params=_cp("arbitrary"),
    )(pa, pa, pa, pb, conv, conv, conv, alog, dtb, *dterms)


def _gdn_post(o, z, gain):
    return (jnp.concatenate(
        [_rms(o[:, h * HD:(h + 1) * HD], gain) * _silu(z[:, h * HD:(h + 1) * HD]) for h in range(NG)], axis=1),)


def _place():
    return lax.axis_index("x"), lax.axis_index("y"), lax.axis_index("c")


def _all_gather(name, shard):
    def body(x_ref, out_ref, send_sems, recv_sems, local_sem):
        x, y, c = _place()
        me, sibling = (x, y, c), (x, y, 1 - c)
        chips = [(1 - x, y), (x, 1 - y), (1 - x, 1 - y)]

        def blk(px, py, pc):
            return out_ref.at[4 * px + 2 * py + pc]

        def copy(k, block, to, src=None):
            return pltpu.make_async_remote_copy(
                src_ref=blk(*block) if src is None else src, dst_ref=blk(*block),
                send_sem=send_sems.at[k], recv_sem=recv_sems.at[k], device_id=to, device_id_type=MESH)

        mine = pltpu.make_async_copy(x_ref, blk(*me), local_sem)
        mine.start()
        first = [copy(0, me, sibling, src=x_ref)]
        first += [copy(1 + j, me, (*chip, c), src=x_ref) for j, chip in enumerate(chips)]
        for cp in first:
            cp.start()
        passed = [copy(4 + j, (*chip, c), sibling) for j, chip in enumerate(chips)]
        for j, chip in enumerate(chips):
            copy(1 + j, (*chip, c), me).wait_recv()
            passed[j].start()
        copy(0, sibling, me).wait_recv()
        for j, chip in enumerate(chips):
            copy(4 + j, (*chip, 1 - c), me).wait_recv()
        for cp in first + passed:
            cp.wait_send()
        mine.wait()

    return pl.pallas_call(
        body, name=name, out_shape=SDS((N_DEV,) + shard.shape, shard.dtype),
        in_specs=[pl.BlockSpec(memory_space=pltpu.HBM)], out_specs=pl.BlockSpec(memory_space=pltpu.HBM),
        scratch_shapes=[pltpu.SemaphoreType.DMA((7,)), pltpu.SemaphoreType.DMA((7,)), pltpu.SemaphoreType.DMA],
    )(shard)


def _scatter_exchange(name, full):
    def body(g_ref, out_ref, send_sems, recv_sems, local_sem):
        x, y, c = _place()
        me = 4 * x + 2 * y + c
        mine = pltpu.make_async_copy(g_ref.at[me], out_ref.at[me], local_sem)
        mine.start()
        sends, recvs = [], []
        for k in range(1, N_DEV):
            px = 1 - x if k & 4 else x
            py = 1 - y if k & 2 else y
            pc = 1 - c if k & 1 else c
            peer = 4 * px + 2 * py + pc
            sends.append(pltpu.make_async_remote_copy(
                src_ref=g_ref.at[peer], dst_ref=out_ref.at[me], send_sem=send_sems.at[k - 1],
                recv_sem=recv_sems.at[k - 1], device_id=(px, py, pc), device_id_type=MESH))
            recvs.append(pltpu.make_async_remote_copy(
                src_ref=g_ref.at[me], dst_ref=out_ref.at[peer], send_sem=send_sems.at[k - 1],
                recv_sem=recv_sems.at[k - 1], device_id=(px, py, pc), device_id_type=MESH))
        for cp in sends:
            cp.start()
        for cp in recvs:
            cp.wait_recv()
        for cp in sends:
            cp.wait_send()
        mine.wait()

    return pl.pallas_call(
        body, name=name, out_shape=SDS(full.shape, full.dtype),
        in_specs=[pl.BlockSpec(memory_space=pltpu.HBM)], out_specs=pl.BlockSpec(memory_space=pltpu.HBM),
        scratch_shapes=[pltpu.SemaphoreType.DMA((7,)), pltpu.SemaphoreType.DMA((7,)), pltpu.SemaphoreType.DMA],
    )(full)


def _sum_blocks(name, parts):
    _, r, c = parts.shape
    tr = 64 if r % 64 == 0 else r

    def body(x, o):
        acc = x[0].astype(F32)
        for d in range(1, N_DEV):
            acc = acc + x[d].astype(F32)
        o[...] = acc

    return pl.pallas_call(
        body, grid=(r // tr,), name=name, in_specs=[pl.BlockSpec((N_DEV, tr, c), lambda i: (0, i, 0))],
        out_specs=pl.BlockSpec((tr, c), lambda i: (i, 0)), out_shape=SDS((r, c), F32), compiler_params=_cp("parallel"),
    )(parts)


def _reduce_scatter(name, full):
    return _sum_blocks(name + "_sum", _scatter_exchange(name, full))


def _all_reduce_small(name, x, reduce):
    m_per, n = x.shape

    def body(x_ref, out_ref, send_sems, recv_sems, local_sem):
        px, py, pc = _place()
        me, sibling = (px, py, pc), (px, py, 1 - pc)
        chips = [(1 - px, py), (px, 1 - py), (1 - px, 1 - py)]
        buf = out_ref

        def rows(qx, qy, qc):
            return buf.at[pl.ds((4 * qx + 2 * qy + qc) * m_per, m_per), :]

        def copy(k, block, to, src=None):
            return pltpu.make_async_remote_copy(
                src_ref=rows(*block) if src is None else src, dst_ref=rows(*block),
                send_sem=send_sems.at[k], recv_sem=recv_sems.at[k], device_id=to, device_id_type=MESH)

        mine = pltpu.make_async_copy(x_ref, rows(*me), local_sem)
        mine.start()
        first = [copy(0, me, sibling, src=x_ref)]
        first += [copy(1 + j, me, (*chip, pc), src=x_ref) for j, chip in enumerate(chips)]
        for cp in first:
            cp.start()
        passed = [copy(4 + j, (*chip, pc), sibling) for j, chip in enumerate(chips)]
        for j, chip in enumerate(chips):
            copy(1 + j, (*chip, pc), me).wait_recv()
            passed[j].start()
        copy(0, sibling, me).wait_recv()
        for j, chip in enumerate(chips):
            copy(4 + j, (*chip, 1 - pc), me).wait_recv()
        for cp in first + passed:
            cp.wait_send()
        mine.wait()

    gathered = pl.pallas_call(
        body, name=name, out_shape=SDS((N_DEV * m_per, n), x.dtype),
        in_specs=[pl.BlockSpec(memory_space=pltpu.VMEM)], out_specs=pl.BlockSpec(memory_space=pltpu.VMEM),
        scratch_shapes=[pltpu.SemaphoreType.DMA((7,)), pltpu.SemaphoreType.DMA((7,)), pltpu.SemaphoreType.DMA],
    )(x)
    if not reduce:
        return gathered
    return _sum_blocks(name + "_sum", gathered.reshape(N_DEV, m_per, n))


HBM_SPEC = pl.BlockSpec(memory_space=pltpu.HBM)
SEM_SPEC = pl.BlockSpec(memory_space=pltpu.SEMAPHORE)
EFFECT = pltpu.SideEffectType.DATAFLOW_SIDE_EFFECTING


def _copies_start(name, bufs, n_remote, n_local, build, deps):
    nb, nd = len(bufs), len(deps)
    sem_shapes = [pltpu.SemaphoreType.DMA((n_remote,)), pltpu.SemaphoreType.DMA((n_remote,))]
    if n_local:
        sem_shapes.append(pltpu.SemaphoreType.DMA((n_local,)))
    ns = len(sem_shapes)

    def body(*refs):
        sems = refs[nb + nd:nb + nd + ns]
        remote, local = build(refs[:nb], *sems, *([None] * (3 - ns)))
        for cp in local + remote:
            cp.start()
        refs[-1][...] = jnp.zeros((8, HD), F32)

    outs = pl.pallas_call(
        body, name=name,
        out_shape=(*sem_shapes, *[pltpu.HBM(b.shape, b.dtype) for b in bufs], SDS((8, HD), F32)),
        in_specs=[HBM_SPEC] * nb + [ANY_SPEC] * nd,
        out_specs=(*[SEM_SPEC] * ns, *[HBM_SPEC] * nb, pl.BlockSpec(memory_space=pltpu.VMEM)),
        input_output_aliases={i: ns + i for i in range(nb)},
        compiler_params=pltpu.CompilerParams(has_side_effects=EFFECT),
    )(*[pltpu.with_memory_space_constraint(b, pltpu.HBM) for b in bufs], *deps)
    return list(outs[:ns]), list(outs[ns:ns + nb]), outs[-1]


def _copies_wait(name, bufs, sems, build, after):
    nb, ns = len(bufs), len(sems)

    def body(*refs):
        remote, local = build(refs[:nb], *refs[nb:nb + ns], *([None] * (3 - ns)))
        for cp in local:
            cp.wait()
        for cp in remote:
            cp.wait_send()
            cp.wait_recv()

    outs = pl.pallas_call(
        body, name=name, out_shape=tuple(pltpu.HBM(b.shape, b.dtype) for b in bufs),
        in_specs=[HBM_SPEC] * nb + [SEM_SPEC] * ns + [ANY_SPEC] * len(after), out_specs=tuple([HBM_SPEC] * nb),
        input_output_aliases={i: i for i in range(nb)},
        compiler_params=pltpu.CompilerParams(has_side_effects=EFFECT),
    )(*bufs, *sems, *after)
    return list(outs)


def _remote(src, dst, send, recv, k, to):
    return pltpu.make_async_remote_copy(src_ref=src, dst_ref=dst, send_sem=send.at[k], recv_sem=recv.at[k],
                                        device_id=to, device_id_type=MESH)


class _Gather:
    def __init__(self, name, shards, deps):
        self.name, self.n = name, len(shards)
        lands = [lax.empty((N_DEV,) + s.shape, s.dtype) for s in shards]
        self.sems1, bufs, self.token = _copies_start(
            name + "_s1", list(shards) + lands, 4 * self.n, self.n, self._stage1(range(self.n)), deps)
        self.shards, self.lands, self.sems2 = bufs[:self.n], bufs[self.n:], {}

    def _stage1(self, idxs):
        def build(refs, send, recv, loc):
            x, y, c = _place()
            me = 4 * x + 2 * y + c
            targets = [(x, y, 1 - c), (1 - x, y, c), (x, 1 - y, c), (1 - x, 1 - y, c)]
            remote, local = [], []
            for pos, i in enumerate(idxs):
                src, land = refs[pos], refs[len(idxs) + pos]
                local.append(pltpu.make_async_copy(src, land.at[me], loc.at[i]))
                remote += [_remote(src, land.at[me], send, recv, 4 * i + k, to) for k, to in enumerate(targets)]
            return remote, local
        return build

    @staticmethod
    def _stage2(refs, send, recv, loc):
        x, y, c = _place()
        remote = []
        for pos, land in enumerate(refs):
            for j, (cx, cy) in enumerate([(1 - x, y), (x, 1 - y), (1 - x, 1 - y)]):
                blk = land.at[4 * cx + 2 * cy + c]
                remote.append(_remote(blk, blk, send, recv, 3 * pos + j, (x, y, 1 - c)))
        return remote, []

    def pass_on(self, idxs, after):
        tag, m = "".join(map(str, idxs)), len(idxs)
        bufs = _copies_wait(f"{self.name}_w1_{tag}", [self.shards[i] for i in idxs] + [self.lands[i] for i in idxs],
                            self.sems1, self._stage1(idxs), after)
        self.sems2[tag], lands, token = _copies_start(f"{self.name}_s2_{tag}", bufs[m:], 3 * m, 0, self._stage2, ())
        for pos, i in enumerate(idxs):
            self.lands[i] = lands[pos]
        return [token]

    def get(self, idxs, after):
        tag = "".join(map(str, idxs))
        return _copies_wait(f"{self.name}_w2_{tag}", [self.lands[i] for i in idxs], self.sems2[tag], self._stage2, after)


def _rows_tile(r, row_bytes, target=1 << 20):
    tr = r
    while tr % 32 == 0 and tr * row_bytes > target:
        tr //= 2
    return tr


def _pair_add(name, g, got, c):
    _, r, cols = g.shape
    tr = _rows_tile(r, cols * 2)

    def body(s, a, b, o):
        o[...] = (a[...].astype(F32) + b[...].astype(F32)).astype(o.dtype)

    return pl.pallas_call(
        body, name=name, out_shape=SDS((4, r, cols), g.dtype),
        grid_spec=pltpu.PrefetchScalarGridSpec(
            num_scalar_prefetch=1, grid=(4, r // tr),
            in_specs=[pl.BlockSpec((None, tr, cols), lambda j, i, s: (2 * j + s[0], i, 0)),
                      pl.BlockSpec((None, tr, cols), lambda j, i, s: (j, i, 0))],
            out_specs=pl.BlockSpec((None, tr, cols), lambda j, i, s: (j, i, 0))),
        compiler_params=_cp("parallel", "parallel"),
    )(c.reshape(1), g, got)


def _quad_sum(name, part, got, chip):
    _, r, cols = part.shape
    tr = _rows_tile(r, cols * 4)

    def body(s, a, b1, b2, b3, o):
        o[...] = ((a[...].astype(F32) + b1[...].astype(F32)) + b2[...].astype(F32)) + b3[...].astype(F32)

    blk = lambda k: pl.BlockSpec((None, tr, cols), lambda i, s, k=k: (jnp.bitwise_xor(s[0], k), i, 0))
    return pl.pallas_call(
        body, name=name, out_shape=SDS((r, cols), F32),
        grid_spec=pltpu.PrefetchScalarGridSpec(
            num_scalar_prefetch=1, grid=(r // tr,), in_specs=[blk(0), blk(1), blk(2), blk(3)],
            out_specs=pl.BlockSpec((tr, cols), lambda i, s: (i, 0))),
        compiler_params=_cp("parallel"),
    )(chip.reshape(1), part, got, got, got)


class _Scatter:
    def __init__(self, name, grads, deps):
        self.name, self.n = name, len(grads)
        got = [lax.empty((4,) + g.shape[1:], g.dtype) for g in grads]
        self.sems, bufs, self.token = _copies_start(name + "_s1", list(grads) + got, 4 * self.n, 0, self._stage1, deps)
        self.grads, self.got = bufs[:self.n], bufs[self.n:]

    def _stage1(self, refs, send, recv, loc):
        x, y, c = _place()
        remote = []
        for i in range(self.n):
            remote += [_remote(refs[i].at[2 * j + 1 - c], refs[self.n + i].at[j], send, recv, 4 * i + j, (x, y, 1 - c))
                       for j in range(4)]
        return remote, []

    def _stage2(self, refs, send, recv, loc):
        x, y, c = _place()
        remote = []
        for i in range(self.n):
            for k in (1, 2, 3):
                tx = 1 - x if k & 2 else x
                ty = 1 - y if k & 1 else y
                remote.append(_remote(refs[i].at[2 * tx + ty], refs[self.n + i].at[2 * x + y], send, recv,
                                      3 * i + k - 1, (tx, ty, c)))
        return remote, []

    def mid(self, after):
        bufs = _copies_wait(self.name + "_w1", self.grads + self.got, self.sems, self._stage1, after)
        c = lax.axis_index("c").astype(jnp.int32)
        parts = [_pair_add(f"{self.name}_add{i}", bufs[i], bufs[self.n + i], c) for i in range(self.n)]
        got = [lax.empty(p.shape, p.dtype) for p in parts]
        self.sems, bufs, self.token = _copies_start(self.name + "_s2", parts + got, 3 * self.n, 0, self._stage2, ())
        self.parts, self.got = bufs[:self.n], bufs[self.n:]

    def end(self, after):
        bufs = _copies_wait(self.name + "_w2", self.parts + self.got, self.sems, self._stage2, after)
        chip = (2 * lax.axis_index("x") + lax.axis_index("y")).astype(jnp.int32)
        return [_quad_sum(f"{self.name}_sum{i}", bufs[i], bufs[self.n + i], chip) for i in range(self.n)]


def _adamw(w, g, m, v):
    m = ADAM_B1 * m + (1.0 - ADAM_B1) * g
    v = ADAM_B2 * v + (1.0 - ADAM_B2) * (g * g)
    m_hat = m / (1.0 - ADAM_B1 ** ADAM_STEP)
    v_hat = v / (1.0 - ADAM_B2 ** ADAM_STEP)
    return -ADAM_LR * (m_hat / (jnp.sqrt(v_hat) + ADAM_EPS) + ADAM_WD * w), m, v


def _adamw_call(name, w, g, m, v):
    r, c = w.shape
    tm = 64 if r % 64 == 0 else r
    return _rowwise(name, _adamw, [w, g, m, v], [], [(c, F32)] * 3, tm)


_IN_COLS = 5906


def _perm_in(w):
    pad = jnp.zeros((w.shape[0], 2 * HALF - _IN_COLS), w.dtype)
    return (jnp.concatenate([w[:, 2310:4614], w[:, 4614:5382]], axis=1),
            jnp.concatenate([w[:, :2304], w[:, 5394:5906], w[:, 2304:2310], w[:, 5382:5394], pad], axis=1))


def _unperm_in(ga, gb):
    return jnp.concatenate([gb[:, :2304], gb[:, 2816:2822], ga[:, :2304], ga[:, 2304:3072], gb[:, 2822:2834],
                            gb[:, 2304:2816]], axis=1)


def _lanes(v, at):
    return jnp.pad(v, ((0, 0), (at, HD - at - v.shape[1])))


_PACK = ("norm_mix", "mem_norm", "norm_ffn", "gdn_conv", "fox_q_norm", "fox_k_norm", "gdn_out_norm", "mem_q_norm",
         "mem_k_norm", "fox_f_bias", "gdn_a_log", "gdn_dt_bias", "loss")


def _pack(vals):
    parts = [vals[n].reshape(-1, HD) for n in _PACK]
    used = sum(p.shape[0] for p in parts)
    buf = jnp.concatenate(parts + [jnp.zeros((-used % 8, HD), F32)], axis=0)
    return buf, [(n, p.shape[0]) for n, p in zip(_PACK, parts)]


def _unpack(buf, layout):
    out, at = {}, 0
    for n, rows in layout:
        out[n] = buf[at:at + rows]
        at += rows
    return out


def kernel(x, mem, norm_mix, w_in, fox_f_bias, fox_q_norm, fox_k_norm, gdn_conv, gdn_a_log, gdn_dt_bias, gdn_out_norm, mem_norm, w_mem_kv, mem_q_norm, mem_k_norm, w_out, norm_ffn, w_gate_up, w_down, loss_target, m_norm_mix, m_w_in, m_fox_f_bias, m_fox_q_norm, m_fox_k_norm, m_gdn_conv, m_gdn_a_log, m_gdn_dt_bias, m_gdn_out_norm, m_mem_norm, m_w_mem_kv, m_mem_q_norm, m_mem_k_norm, m_w_out, m_norm_ffn, m_w_gate_up, m_w_down, v_norm_mix, v_w_in, v_fox_f_bias, v_fox_q_norm, v_fox_k_norm, v_gdn_conv, v_gdn_a_log, v_gdn_dt_bias, v_gdn_out_norm, v_mem_norm, v_w_mem_kv, v_mem_q_norm, v_mem_k_norm, v_w_out, v_norm_ffn, v_w_gate_up, v_w_down):
    args = dict(locals())
    d = x.shape[2]
    me = 4 * lax.axis_index("x") + 2 * lax.axis_index("y") + lax.axis_index("c")

    cshard = gdn_conv[0].shape[1]
    conv_pad = jnp.pad(gdn_conv[0], ((0, 4), (0, 3 * HD - cshard)))
    conv_all = _all_reduce_small("ag_conv", conv_pad, False).reshape(N_DEV, 8, 3 * HD)[:, :4, :cshard]
    conv_all = conv_all.transpose(1, 0, 2).reshape(4, N_DEV * cshard)
    w_in_a, w_in_b = _perm_in(w_in[0])
    comm = _StepComm({"in_b": [w_in_b], "in_a": [w_in_a], "kv_out": [w_mem_kv[0], w_out[0]], "gate_up": [w_gate_up[0]],
                      "down": [w_down[0]]}, [conv_all])

    grad_x, loss_local, small_grads = _local_step(
        x[0], mem[0], loss_target[0], norm_mix, fox_f_bias, fox_q_norm, fox_k_norm, gdn_a_log, gdn_dt_bias,
        gdn_out_norm, mem_norm, mem_q_norm, mem_k_norm, norm_ffn, conv_all, comm)

    red = comm.finish([grad_x])
    grads = {"w_down": red["ffn"][0], "w_gate_up": red["ffn"][1], "w_out": red["a"][1], "w_mem_kv": red["a"][2],
             "w_in": _unperm_in(red["a"][0], red["b"][0])}
    small_grads["loss"] = jnp.broadcast_to(loss_local, (1, HD))
    packed, layout = _pack(small_grads)
    small = _unpack(_all_reduce_small("ar_small", packed, True), layout)
    loss = small["loss"][0, 0]
    six = {"fox_f_bias": L_FF, "gdn_a_log": L_GA, "gdn_dt_bias": L_GA}
    for n, rows_n in layout[:-1]:
        gsm = small[n]
        if n == "gdn_conv":
            gsm = lax.dynamic_slice(gsm.reshape(4, N_DEV * cshard), (0, me * cshard), (4, cshard))[None]
        elif n in six:
            gsm = gsm[:, six[n]:six[n] + 6]
        else:
            gsm = gsm.reshape(1, rows_n * HD)
        grads[n] = gsm

    names = ['norm_mix', 'w_in', 'fox_f_bias', 'fox_q_norm', 'fox_k_norm', 'gdn_conv', 'gdn_a_log', 'gdn_dt_bias',
             'gdn_out_norm', 'mem_norm', 'w_mem_kv', 'mem_q_norm', 'mem_k_norm', 'w_out', 'norm_ffn', 'w_gate_up', 'w_down']
    big = ("w_in", "w_mem_kv", "w_out", "w_gate_up", "w_down")
    delta, new_m, new_v = {}, {}, {}
    for n in big:
        delta[n], new_m[n], new_v[n] = [a[None] for a in _adamw_call(
            "adamw_" + n, args[n][0], grads[n], args["m_" + n][0], args["v_" + n][0])]
        grads[n] = grads[n][None]

    def flat(a):
        a = a.reshape(1, -1)
        return jnp.pad(a, ((0, 0), (0, -a.shape[1] % HD))).reshape(-1, HD)

    smalls = [n for n in names if n not in big]
    pk = lambda pre: jnp.concatenate([flat(grads[n] if pre == "g" else args[pre + n]) for n in smalls], axis=0)
    cat = [pk(""), pk("g"), pk("m_"), pk("v_")]
    padr = -cat[0].shape[0] % 8
    cat = [jnp.pad(a, ((0, padr), (0, 0))) for a in cat]
    res = _adamw_call("adamw_small", *cat)
    at = 0
    for n in smalls:
        shape = args[n].shape
        size = math.prod(shape)
        nrow = -(-size // HD)
        for dst, src in zip((delta, new_m, new_v), res):
            dst[n] = src[at:at + nrow].reshape(-1)[:size].reshape(shape)
        at += nrow

    return (loss, grad_x[None], *[grads[n] for n in names], *[delta[n] for n in names],
            *[new_m[n] for n in names], *[new_v[n] for n in names])


class _StepComm:
    def __init__(self, shard_groups, after):
        self.groups, shards = {}, []
        for key, ws in shard_groups.items():
            self.groups[key] = list(range(len(shards), len(shards) + len(ws)))
            shards += [w.astype(BF16) for w in ws]
        self.gather = _Gather("ag", shards, after)
        self.passed, self.scatters = set(), {}

    def start_deps(self):
        return [self.gather.token]

    def pass_on(self, key, after):
        self.passed.add(key)
        return self.gather.pass_on(self.groups[key], after)

    def weights(self, key, after):
        if key not in self.passed:
            after = self.pass_on(key, after)
        return self.gather.get(self.groups[key], after)

    def send(self, tag, grads):
        blocks = [g if g.ndim == 3 else g.reshape(N_DEV, g.shape[0] // N_DEV, g.shape[1]) for g in grads]
        self.scatters[tag] = _Scatter("rs_" + tag, blocks, ())
        return [self.scatters[tag].token]

    def mid(self, tag, after):
        self.scatters[tag].mid(after)
        return [self.scatters[tag].token]

    def finish(self, after):
        return {tag: sc.end(after) for tag, sc in self.scatters.items()}


def _local_step(xs, ms, tgt, norm_mix, fox_f_bias, fox_q_norm, fox_k_norm, gdn_a_log, gdn_dt_bias, gdn_out_norm,
                mem_norm, mem_q_norm, mem_k_norm, norm_ffn, conv_all, comm):
    t, d = xs.shape
    bq = min(t, 256)
    fb, alog, dtb = _lanes(fox_f_bias, L_FF), _lanes(gdn_a_log, L_GA), _lanes(gdn_dt_bias, L_GA)
    flat = lambda w: w.reshape(-1, w.shape[-1])

    rms1 = lambda a, g: (_rms(a, g),)
    (u,) = _rowwise("norm_mix", rms1, [xs], [norm_mix], [(d, BF16)], min(t, 256), deps=comm.start_deps())
    w_in_b = flat(comm.weights("in_b", [u])[0])
    pb = _matmul("proj_in_b", u, w_in_b, NN, F32, 1024, 768)
    o_fox = _fox_fwd(pb, fb, fox_q_norm, fox_k_norm, bq)
    w_in_a = flat(comm.weights("in_a", [o_fox])[0])
    pa = _matmul("proj_in_a", u, w_in_a, NN, F32, 1024, 768)
    gdn_terms = _gdn_fwd(pa, pb, conv_all, alog, dtb)
    o_gdn_raw, gdn_states = _gdn_scan(gdn_terms)
    gdn_saved = list(gdn_terms) + [gdn_states]
    zrow = (pa, NG * HD, GZ * HD // (NG * HD))
    (o_gdn,) = _rowwise("gdn_post", _gdn_post, [o_gdn_raw, zrow], [gdn_out_norm], [(NG * HD, BF16)], min(t, 256))
    w_kv_all, w_out_all = [flat(w) for w in comm.weights("kv_out", [o_gdn])]
    (mem_n,) = _rowwise("norm_mem", rms1, [ms], [mem_norm], [(d, BF16)], ms.shape[0])
    mkv = _matmul("proj_mem", mem_n, w_kv_all, NN, F32, 256, 512)
    o_mem = _mem_fwd(pb, mkv, mem_q_norm, mem_k_norm)
    deps = comm.pass_on("gate_up", [o_mem])
    mix = jnp.concatenate([o_fox, o_gdn, o_mem], axis=1)
    h1, h1n = _proj_out_norm(mix, w_out_all, xs, norm_ffn, deps)
    (wgu,) = comm.weights("gate_up", [h1n])
    ffw = wgu.shape[2]
    gu, act = _ffn_up(h1n, wgu.reshape(2, 4, d, ffw))
    w_down_all = flat(comm.weights("down", [act])[0])
    dyb, lsum = _ffn_down_loss(act, w_down_all, h1, tgt)
    loss_local = (0.5 / d) * jnp.sum(lsum[::8, ::HD])

    dgu = _ffn_down_bwd(dyb, w_down_all.reshape(4, ffw, d), gu).reshape(8, t, ffw)
    g_w_down = _matmul("grad_w_down", act, dyb, TN, BF16, 512, 2048)
    dh1n = _ffn_up_bwd_x(dgu, wgu)
    g_w_gu = _ffn_up_bwd_w(h1n, dgu)
    deps = comm.send("ffn", [g_w_down, g_w_gu])
    rms2 = lambda a, g: (_rms(a, g), a)
    dh1b, g_norm_ffn = _rowwise_vjp("norm_ffn_bwd", rms2, [h1], [norm_ffn], [dh1n, dyb], [BF16], min(t, 256), deps=deps)

    dmix = _matmul("proj_out_bwd_x", dh1b, w_out_all, NT, BF16, 1024, 1024)
    g_w_out = _matmul("grad_w_out", mix, dh1b, TN, BF16, 1024, 2048)
    deps = comm.mid("ffn", [dmix, g_w_out])
    dmq, dmk, dmv, g_mqn, g_mkn = _mem_bwd(pb, mkv, mem_q_norm, mem_k_norm, dmix, deps=deps)
    dmkv = jnp.concatenate([dmk, dmv], axis=1).astype(BF16)
    g_w_kv = _matmul("grad_w_kv", mem_n, dmkv, TN, BF16, 512, 512)
    do_raw, dgz, g_gon = _rowwise_vjp("gdn_post_bwd", _gdn_post, [o_gdn_raw, zrow], [gdn_out_norm],
                                      [(dmix, NG * HD, 1)], [F32, BF16], min(t, 256), deps=deps)
    dterms = _gdn_bwd_scan(gdn_saved, do_raw)
    dgq, dgk, dgv, dsm_gdn, dwq, dwk, dwv, g_alog, g_dtb = _gdn_bwd(pa, pb, conv_all, alog, dtb, dterms)
    dp_a = jnp.concatenate([dgq, dgk, dgv, dgz], axis=1)
    g_w_in_a = _matmul("grad_w_in_a", u, dp_a, TN, BF16, 512, 3072)
    deps = comm.send("a", [g_w_in_a, g_w_out, g_w_kv])
    du_a = _matmul("proj_in_bwd_a", dp_a, w_in_a, NT, F32, 1024, 1024, deps=deps)
    deps = comm.mid("a", [du_a])
    dfq, dfk, dfv, dsm_fox, g_fb, g_fqn, g_fkn = _fox_bwd(pb, fb, fox_q_norm, fox_k_norm, dmix, bq, deps=deps)
    dp_b = jnp.concatenate([dfq, dfk, dfv, dmq, (dsm_fox + dsm_gdn).astype(BF16), jnp.zeros((t, HD), BF16)], axis=1)
    g_w_in_b = _matmul("grad_w_in_b", u, dp_b, TN, BF16, 512, 3072)
    deps = comm.send("b", [g_w_in_b])
    dmem_n = _matmul("proj_mem_bwd_x", dmkv, w_kv_all, NT, F32, 256, 512, deps=deps)
    g_mem_norm = _rowwise_vjp("norm_mem_bwd", rms1, [ms], [mem_norm], [dmem_n], [], ms.shape[0])[0]
    deps = comm.mid("b", [g_mem_norm])
    du = _matmul("proj_in_bwd_b", dp_b, w_in_b, NT, F32, 1024, 1024, residual=du_a, deps=deps)
    grad_x, g_norm_mix = _rowwise_vjp("norm_mix_bwd", rms2, [xs], [norm_mix], [du, dh1b], [F32], min(t, 256))

    small_grads = {
        "norm_mix": g_norm_mix, "mem_norm": g_mem_norm, "norm_ffn": g_norm_ffn,
        "gdn_conv": jnp.concatenate([dwq, dwk, dwv], axis=1),
        "fox_q_norm": g_fqn, "fox_k_norm": g_fkn, "gdn_out_norm": g_gon, "mem_q_norm": g_mqn, "mem_k_norm": g_mkn,
        "fox_f_bias": g_fb, "gdn_a_log": g_alog, "gdn_dt_bias": g_dtb}
    return grad_x, loss_local, small_grads
```

```python
import functools
import math

import jax
import jax.numpy as jnp
from jax import lax
from jax.experimental import pallas as pl
from jax.experimental.pallas import tpu as pltpu

F32 = jnp.float32
BF16 = jnp.bfloat16
HI = lax.Precision.HIGHEST
SDS = jax.ShapeDtypeStruct

N_DEV = 8
HD = 128
NF, NG, NM = 6, 6, 4
CHUNK = 64
GROUP = 16
NORM_EPS = 1e-6
GQ, GK, GV, GZ = 0, 6, 12, 18
FQ, FK, FV, MQ, SM = 0, 6, 12, 18, 22
HALF = 24 * HD
L_FF, L_GA, L_GB = 0, 6, 12
VMEM_LIMIT = 56 * 1024 * 1024

ADAM_LR, ADAM_B1, ADAM_B2, ADAM_EPS, ADAM_WD, ADAM_STEP = 0.001, 0.9, 0.999, 1e-08, 0.01, 10

NN = (((1,), (0,)), ((), ()))
NT = (((1,), (1,)), ((), ()))
TN = (((0,), (0,)), ((), ()))
MESH = pl.DeviceIdType.MESH


def _cp(*sem):
    return pltpu.CompilerParams(dimension_semantics=tuple(sem) if sem else None, vmem_limit_bytes=VMEM_LIMIT)


def _dot(a, b, dims=NN):
    return lax.dot_general(a, b, dims, preferred_element_type=F32)


def _bdot(a, b):
    return _dot(a.astype(BF16), b.astype(BF16))


def _iota(shape, axis):
    return lax.broadcasted_iota(jnp.int32, shape, axis)


def _rms(x, gain):
    return x * lax.rsqrt(jnp.mean(x * x, axis=-1, keepdims=True) + NORM_EPS) * gain


def _sigmoid(x):
    return 0.5 * jnp.tanh(0.5 * x) + 0.5


def _silu(x):
    return x * _sigmoid(x)


def _softplus(x):
    return jnp.maximum(x, 0.0) + jnp.log(1.0 + jnp.exp(-jnp.abs(x)))


def _lane_pick(x, lane):
    oh = (_iota((1, x.shape[-1]), 1) == lane).astype(F32)
    return jnp.sum(x * oh, axis=-1, keepdims=True)


def _cumsum_rows(x):
    tril = (_iota((HD, HD), 0) >= _iota((HD, HD), 1)).astype(F32)
    carry = jnp.zeros((1, x.shape[1]), F32)
    outs = []
    for b in range(x.shape[0] // HD):
        blk = x[b * HD:(b + 1) * HD]
        outs.append(jnp.dot(tril, blk, precision=HI, preferred_element_type=F32) + carry)
        carry = carry + jnp.sum(blk, axis=0, keepdims=True)
    return jnp.concatenate(outs, axis=0)


def _row_spec(r, tm):
    if isinstance(r, tuple):
        arr, width, cb = r
        return arr, pl.BlockSpec((tm, width), lambda i, cb=cb: (i, cb))
    return r, pl.BlockSpec((tm, r.shape[1]), lambda i: (i, 0))


ANY_SPEC = pl.BlockSpec(memory_space=pl.ANY)


def _rowwise(name, fn, rows, consts, outs, tm, deps=()):
    arrs, specs = zip(*[_row_spec(r, tm) for r in rows])
    n_rows = arrs[0].shape[0]
    nr, nc, nd = len(rows), len(consts), len(deps)

    def body(*refs):
        res = fn(*[r[...] for r in refs[:nr + nc]])
        for o, v in zip(refs[nr + nc + nd:], res):
            o[...] = v.astype(o.dtype)

    return pl.pallas_call(
        body, grid=(n_rows // tm,), name=name,
        in_specs=list(specs) + [pl.BlockSpec(c.shape, lambda i: (0, 0)) for c in consts] + [ANY_SPEC] * nd,
        out_specs=[pl.BlockSpec((tm, w), lambda i: (i, 0)) for w, _ in outs],
        out_shape=[SDS((n_rows, w), dt) for w, dt in outs],
        compiler_params=_cp("parallel"),
    )(*arrs, *consts, *deps)


def _rowwise_vjp(name, fn, rows, consts, cts, grad_dtypes, tm, deps=()):
    arrs, specs = zip(*[_row_spec(r, tm) for r in rows])
    ct_arrs, ct_specs = zip(*[_row_spec(r, tm) for r in cts])
    n_rows = arrs[0].shape[0]
    nr, nc, nct, nd = len(rows), len(consts), len(cts), len(deps)
    plan = [(j, dt) for j, dts in enumerate(grad_dtypes) for dt in (dts if isinstance(dts, tuple) else (dts,))]
    ng = len(plan)
    widths = [specs[j].block_shape[1] for j, _ in plan]
    grad_dtypes = [dt for _, dt in plan]

    def body(*refs):
        vals = [r[...].astype(F32) for r in refs[:nr + nc]]
        ctv = tuple(r[...].astype(F32) for r in refs[nr + nc:nr + nc + nct])
        _, vjp = jax.vjp(fn, *vals)
        grads = vjp(ctv)
        outs = refs[nr + nc + nct + nd:]
        for o, (j, _) in zip(outs[:ng], plan):
            o[...] = grads[j].astype(o.dtype)

        @pl.when(pl.program_id(0) == 0)
        def _():
            for o in outs[ng:]:
                o[...] = jnp.zeros_like(o)

        for o, g in zip(outs[ng:], grads[nr:]):
            o[...] += g

    return pl.pallas_call(
        body, grid=(n_rows // tm,), name=name,
        in_specs=list(specs) + [pl.BlockSpec(c.shape, lambda i: (0, 0)) for c in consts] + list(ct_specs)
        + [ANY_SPEC] * nd,
        out_specs=[pl.BlockSpec((tm, w), lambda i: (i, 0)) for w in widths]
        + [pl.BlockSpec(c.shape, lambda i: (0, 0)) for c in consts],
        out_shape=[SDS((n_rows, w), dt) for w, dt in zip(widths, grad_dtypes)] + [SDS(c.shape, F32) for c in consts],
        compiler_params=_cp("arbitrary"),
    )(*arrs, *consts, *ct_arrs, *deps)


def _tile(n, pref):
    t = min(n, pref)
    while n % t or (t % HD and t != n):
        t -= 1
    return t


def _matmul(name, a, b, dims, out_dtype, tm, tn, residual=None, deps=()):
    ta, tb = dims == TN, dims == NT
    m = a.shape[1] if ta else a.shape[0]
    k = a.shape[0] if ta else a.shape[1]
    n = b.shape[0] if tb else b.shape[1]
    tm, tn = _tile(m, tm), _tile(n, tn)

    def body(*refs):
        acc = _dot(refs[0][...], refs[1][...], dims)
        if residual is not None:
            acc = acc + refs[2][...]
        refs[-1][...] = acc.astype(out_dtype)

    in_specs = [pl.BlockSpec((k, tm), lambda i, j: (0, i)) if ta else pl.BlockSpec((tm, k), lambda i, j: (i, 0)),
                pl.BlockSpec((tn, k), lambda i, j: (j, 0)) if tb else pl.BlockSpec((k, tn), lambda i, j: (0, j))]
    ops = [a, b]
    if residual is not None:
        in_specs.append(pl.BlockSpec((tm, tn), lambda i, j: (i, j)))
        ops.append(residual)
    in_specs += [ANY_SPEC] * len(deps)
    ops += list(deps)
    return pl.pallas_call(
        body, grid=(m // tm, n // tn), name=name, in_specs=in_specs,
        out_specs=pl.BlockSpec((tm, tn), lambda i, j: (i, j)), out_shape=SDS((m, n), out_dtype),
        compiler_params=_cp("parallel", "parallel"),
    )(*ops)


def _proj_out_norm(mix, w_out, xs, gain, deps):
    t, k = mix.shape
    d = w_out.shape[1]
    tm = _tile(t, 512)

    def body(*refs):
        a, b, x, g = refs[:4]
        h1, h1n = refs[4 + len(deps):]
        acc = _dot(a[...], b[...]) + x[...]
        h1[...] = acc
        h1n[...] = _rms(acc, g[...]).astype(BF16)

    return pl.pallas_call(
        body, grid=(t // tm,), name="proj_out",
        in_specs=[pl.BlockSpec((tm, k), lambda i: (i, 0)), pl.BlockSpec((k, d), lambda i: (0, 0)),
                  pl.BlockSpec((tm, d), lambda i: (i, 0)), pl.BlockSpec((1, d), lambda i: (0, 0))] + [ANY_SPEC] * len(deps),
        out_specs=[pl.BlockSpec((tm, d), lambda i: (i, 0))] * 2, out_shape=[SDS((t, d), F32), SDS((t, d), BF16)],
        compiler_params=_cp("parallel"),
    )(mix, w_out, xs, gain, *deps)


def _ffn_up(h1n, wgu):
    t, d = h1n.shape
    w = wgu.shape[3]
    tm = _tile(t, 512)

    def body(a, b, gu, act):
        x = a[...]
        g = _dot(x, b[0])
        u = _dot(x, b[1])
        gu[0] = g.astype(BF16)
        gu[1] = u.astype(BF16)
        act[...] = (_silu(g) * u).astype(BF16)

    return pl.pallas_call(
        body, grid=(4, t // tm), name="ffn_up",
        in_specs=[pl.BlockSpec((tm, d), lambda j, i: (i, 0)), pl.BlockSpec((2, None, d, w), lambda j, i: (0, j, 0, 0))],
        out_specs=[pl.BlockSpec((2, None, tm, w), lambda j, i: (0, j, i, 0)), pl.BlockSpec((tm, w), lambda j, i: (i, j))],
        out_shape=[SDS((2, 4, t, w), BF16), SDS((t, 4 * w), BF16)],
        compiler_params=_cp("parallel", "parallel"),
    )(h1n, wgu)


def _ffn_down_loss(act, wdown, h1, target):
    t, f = act.shape
    d = wdown.shape[1]
    tm, tn = _tile(t, 1024), _tile(d, 512)

    def body(a, b, h, tg, dyb, ls):
        e = _dot(a[...], b[...]) + h[...] - tg[...]
        dyb[...] = (e * (1.0 / d)).astype(BF16)
        ls[...] = jnp.broadcast_to(jnp.sum(e * e), (8, HD))

    return pl.pallas_call(
        body, grid=(t // tm, d // tn), name="ffn_down_loss",
        in_specs=[pl.BlockSpec((tm, f), lambda i, j: (i, 0)), pl.BlockSpec((f, tn), lambda i, j: (0, j)),
                  pl.BlockSpec((tm, tn), lambda i, j: (i, j)), pl.BlockSpec((tm, tn), lambda i, j: (i, j))],
        out_specs=[pl.BlockSpec((tm, tn), lambda i, j: (i, j)), pl.BlockSpec((8, HD), lambda i, j: (i, j))],
        out_shape=[SDS((t, d), BF16), SDS((8 * (t // tm), HD * (d // tn)), F32)],
        compiler_params=_cp("parallel", "parallel"),
    )(act, wdown, h1, target)


def _ffn_down_bwd(dyb, wdown4, gu):
    t, d = dyb.shape
    w = wdown4.shape[1]
    tm = _tile(t, 512)

    def body(a, b, gu_ref, out):
        da = _dot(a[...], b[...], NT)
        g = gu_ref[0].astype(F32)
        u = gu_ref[1].astype(F32)
        s = _sigmoid(g)
        out[0] = (da * u * (s * (1.0 + g * (1.0 - s)))).astype(BF16)
        out[1] = (da * g * s).astype(BF16)

    return pl.pallas_call(
        body, grid=(4, t // tm), name="ffn_down_bwd",
        in_specs=[pl.BlockSpec((tm, d), lambda j, i: (i, 0)), pl.BlockSpec((None, w, d), lambda j, i: (j, 0, 0)),
                  pl.BlockSpec((2, None, tm, w), lambda j, i: (0, j, i, 0))],
        out_specs=pl.BlockSpec((2, None, tm, w), lambda j, i: (0, j, i, 0)),
        out_shape=SDS((2, 4, t, w), BF16),
        compiler_params=_cp("parallel", "parallel"),
    )(dyb, wdown4, gu)


def _ffn_up_bwd_x(dgu, wgu):
    _, t, w = dgu.shape
    d = wgu.shape[1]
    tm = _tile(t, 512)

    def body(a, b, out):
        @pl.when(pl.program_id(1) == 0)
        def _():
            out[...] = jnp.zeros_like(out)
        out[...] += _dot(a[...], b[...], NT)

    return pl.pallas_call(
        body, grid=(t // tm, 8), name="ffn_up_bwd_x",
        in_specs=[pl.BlockSpec((None, tm, w), lambda i, j: (j, i, 0)), pl.BlockSpec((None, d, w), lambda i, j: (j, 0, 0))],
        out_specs=pl.BlockSpec((tm, d), lambda i, j: (i, 0)), out_shape=SDS((t, d), F32),
        compiler_params=_cp("parallel", "arbitrary"),
    )(dgu, wgu)


def _ffn_up_bwd_w(h1n, dgu):
    _, t, w = dgu.shape
    d = h1n.shape[1]
    tm = _tile(d, 512)

    def body(a, b, out):
        out[...] = _dot(a[...], b[...], TN).astype(BF16)

    return pl.pallas_call(
        body, grid=(8, d // tm), name="ffn_up_bwd_w",
        in_specs=[pl.BlockSpec((t, tm), lambda j, i: (0, i)), pl.BlockSpec((None, t, w), lambda j, i: (j, 0, 0))],
        out_specs=pl.BlockSpec((None, tm, w), lambda j, i: (j, i, 0)), out_shape=SDS((8, d, w), BF16),
        compiler_params=_cp("parallel", "parallel"),
    )(h1n, dgu)


def _fox_prep(fq, fk, sm, fb, qg, kg, h):
    qn = _rms(fq, qg)
    kn = _rms(fk, kg)
    c = _cumsum_rows(-_softplus(-(sm + fb)))
    ccol = _lane_pick(c, L_FF + h)
    crow = jnp.sum(c.T * (_iota((HD, 1), 0) == L_FF + h).astype(F32), axis=0, keepdims=True)
    return qn, kn, ccol, crow


def _softmax_times(s, v):
    e = jnp.exp(s - lax.stop_gradient(jnp.max(s, axis=1, keepdims=True)))
    return _dot(e.astype(BF16), v.astype(BF16)) * (1.0 / jnp.sum(e, axis=1, keepdims=True))


def _fox_block(q, k, v, cc, cr, off):
    bq = q.shape[0]
    assert k.shape[0] == off + bq
    s = _dot((q * (HD ** -0.5)).astype(BF16), k.astype(BF16), NT) + cc - cr
    diag = jnp.where(_iota((bq, bq), 1) <= _iota((bq, bq), 0), s[:, off:], -1e30)
    s = jnp.concatenate([s[:, :off], diag], axis=1) if off else diag
    return _softmax_times(s, v)


ONE_BUFFER = pl.Buffered(1)


def _pcol(t, cb):
    return pl.BlockSpec((t, HD), lambda h, cb=cb: (0, cb + h), pipeline_mode=ONE_BUFFER)


def _smcol(t):
    return pl.BlockSpec((t, HD), lambda h: (0, SM), pipeline_mode=ONE_BUFFER)


def _head(t):
    return pl.BlockSpec((t, HD), lambda h: (0, h), pipeline_mode=ONE_BUFFER)


def _small(n):
    return pl.BlockSpec((n, HD), lambda h: (0, 0), pipeline_mode=ONE_BUFFER)


def _fox_fwd(p, fb, qg, kg, bq):
    t = p.shape[0]

    def body(fq, fk, fv, sm, fb_r, qg_r, kg_r, o, qn_s, cc_s):
        h = pl.program_id(0)
        qn, kn, ccol, crow = _fox_prep(fq[...], fk[...], sm[...], fb_r[...], qg_r[...], kg_r[...], h)
        qn_s[...] = qn
        cc_s[...] = ccol
        knb = kn.astype(BF16)
        vb = fv[...].astype(BF16)
        for i in range(t // bq):
            rows, ext = pl.ds(i * bq, bq), (i + 1) * bq
            o[rows, :] = _fox_block(qn_s[rows, :], knb[:ext], vb[:ext], cc_s[rows, :], crow[:, :ext], i * bq).astype(o.dtype)

    return pl.pallas_call(
        body, grid=(NF,), name="fox_fwd",
        in_specs=[_pcol(t, FQ), _pcol(t, FK), _pcol(t, FV), _smcol(t), _small(1), _small(1), _small(1)],
        out_specs=_head(t), out_shape=SDS((t, NF * HD), BF16),
        scratch_shapes=[pltpu.VMEM((t, HD), F32), pltpu.VMEM((t, 1), F32)],
        compiler_params=_cp("parallel"),
    )(p, p, p, p, fb, qg, kg)


def _fox_bwd(p, fb, qg, kg, dmix, bq, deps=()):
    t = p.shape[0]

    def body(*refs):
        fq, fk, fv, sm, fb_r, qg_r, kg_r, do = refs[:8]
        dfq, dfk, dfv, dsm, dfb, dqg, dkg, qn_s, cc_s, dqn_s, dcc_s, dkn_s, dv_s, dcr_s = refs[8 + len(deps):]
        h = pl.program_id(0)
        qn, kn, ccol, crow = _fox_prep(fq[...], fk[...], sm[...], fb_r[...], qg_r[...], kg_r[...], h)
        qn_s[...] = qn
        cc_s[...] = ccol
        v = fv[...]
        dkn_s[...] = jnp.zeros_like(dkn_s)
        dv_s[...] = jnp.zeros_like(dv_s)
        dcr_s[...] = jnp.zeros_like(dcr_s)

        for i in range(t // bq):
            rows, ext = pl.ds(i * bq, bq), (i + 1) * bq
            _, vjp = jax.vjp(lambda a, b, c, d, e, off=i * bq: _fox_block(a, b, c, d, e, off),
                             qn_s[rows, :], kn[:ext], v[:ext], cc_s[rows, :], crow[:, :ext])
            dq, dk, dv, dcc, dcr = vjp(do[rows, :].astype(F32))
            dqn_s[rows, :] = dq
            dcc_s[rows, :] = dcc
            dkn_s[:ext, :] += dk
            dv_s[:ext, :] += dv
            dcr_s[:, :ext] += dcr
        _, prep_vjp = jax.vjp(lambda a, b, c, d, e, f: _fox_prep(a, b, c, d, e, f, h),
                              fq[...], fk[...], sm[...], fb_r[...], qg_r[...], kg_r[...])
        g_fq, g_fk, g_sm, g_fb, g_qg, g_kg = prep_vjp((dqn_s[...], dkn_s[...], dcc_s[...], dcr_s[...]))
        dfq[...] = g_fq.astype(dfq.dtype)
        dfk[...] = g_fk.astype(dfk.dtype)
        dfv[...] = dv_s[...].astype(dfv.dtype)

        @pl.when(h == 0)
        def _():
            for r in (dsm, dfb, dqg, dkg):
                r[...] = jnp.zeros_like(r)

        dsm[...] += g_sm
        dfb[...] += g_fb
        dqg[...] += g_qg
        dkg[...] += g_kg

    head = _head(t)
    return pl.pallas_call(
        body, grid=(NF,), name="fox_bwd",
        in_specs=[_pcol(t, FQ), _pcol(t, FK), _pcol(t, FV), _smcol(t), _small(1), _small(1), _small(1), head]
        + [ANY_SPEC] * len(deps),
        out_specs=[head, head, head, _small(t), _small(1), _small(1), _small(1)],
        out_shape=[SDS((t, NF * HD), BF16)] * 3 + [SDS((t, HD), F32)] + [SDS((1, HD), F32)] * 3,
        scratch_shapes=[pltpu.VMEM((t, HD), F32), pltpu.VMEM((t, 1), F32), pltpu.VMEM((t, HD), F32),
                        pltpu.VMEM((t, 1), F32), pltpu.VMEM((t, HD), F32), pltpu.VMEM((t, HD), F32),
                        pltpu.VMEM((1, t), F32)],
        compiler_params=_cp("arbitrary"),
    )(p, p, p, p, fb, qg, kg, dmix, *deps)


def _mem_attn(mq, mk, mv, qg, kg):
    s = _dot((_rms(mq, qg) * (HD ** -0.5)).astype(BF16), _rms(mk, kg).astype(BF16), NT)
    return _softmax_times(s, mv)


def _mem_fwd(p, mkv, qg, kg):
    t, ml = p.shape[0], mkv.shape[0]

    def body(mq, mk, mv, qg_r, kg_r, o):
        o[...] = _mem_attn(mq[...], mk[...], mv[...], qg_r[...], kg_r[...]).astype(o.dtype)

    return pl.pallas_call(
        body, grid=(NM,), name="mem_fwd",
        in_specs=[_pcol(t, MQ), pl.BlockSpec((ml, HD), lambda h: (0, h)), pl.BlockSpec((ml, HD), lambda h: (0, NM + h)),
                  _small(1), _small(1)],
        out_specs=pl.BlockSpec((t, HD), lambda h: (0, h)), out_shape=SDS((t, NM * HD), BF16),
        compiler_params=_cp("parallel"),
    )(p, mkv, mkv, qg, kg)


def _mem_bwd(p, mkv, qg, kg, dmix, deps=()):
    t, ml = p.shape[0], mkv.shape[0]

    def body(*refs):
        mq, mk, mv, qg_r, kg_r, do = refs[:6]
        dmq, dmk, dmv, dqg, dkg = refs[6 + len(deps):]
        _, vjp = jax.vjp(_mem_attn, mq[...], mk[...], mv[...], qg_r[...], kg_r[...])
        g_q, g_k, g_v, g_qg, g_kg = vjp(do[...].astype(F32))
        dmq[...] = g_q.astype(dmq.dtype)
        dmk[...] = g_k
        dmv[...] = g_v

        @pl.when(pl.program_id(0) == 0)
        def _():
            dqg[...] = jnp.zeros_like(dqg)
            dkg[...] = jnp.zeros_like(dkg)

        dqg[...] += g_qg
        dkg[...] += g_kg

    return pl.pallas_call(
        body, grid=(NM,), name="mem_bwd",
        in_specs=[_pcol(t, MQ), pl.BlockSpec((ml, HD), lambda h: (0, h)), pl.BlockSpec((ml, HD), lambda h: (0, NM + h)),
                  _small(1), _small(1), pl.BlockSpec((t, HD), lambda h: (0, NF + NG + h))] + [ANY_SPEC] * len(deps),
        out_specs=[pl.BlockSpec((t, HD), lambda h: (0, h)), pl.BlockSpec((ml, HD), lambda h: (0, h)),
                   pl.BlockSpec((ml, HD), lambda h: (0, h)), _small(1), _small(1)],
        out_shape=[SDS((t, NM * HD), BF16), SDS((ml, NM * HD), F32), SDS((ml, NM * HD), F32),
                   SDS((1, HD), F32), SDS((1, HD), F32)],
        compiler_params=_cp("arbitrary"),
    )(p, mkv, mkv, qg, kg, dmix, *deps)


def _shift_down(x, s):
    if s == 0:
        return x
    return jnp.where(_iota(x.shape, 0) >= s, pltpu.roll(x, s, 0), 0.0)


def _shift_up(x, s):
    if s == 0:
        return x
    n = x.shape[0]
    return jnp.where(_iota(x.shape, 0) < n - s, pltpu.roll(x, n - s, 0), 0.0)


@jax.custom_vjp
def _conv4(x, w0, w1, w2, w3):
    return w0 * _shift_down(x, 3) + w1 * _shift_down(x, 2) + w2 * _shift_down(x, 1) + w3 * x


def _conv4_fwd(x, w0, w1, w2, w3):
    return _conv4(x, w0, w1, w2, w3), (x, w0, w1, w2, w3)


def _conv4_bwd(res, dy):
    x, w0, w1, w2, w3 = res
    ups = [_shift_up(dy, 3 - k) for k in range(4)]
    dx = w0 * ups[0] + w1 * ups[1] + w2 * ups[2] + w3 * ups[3]
    return (dx,) + tuple(jnp.sum(up * x, axis=0, keepdims=True) for up in ups)


_conv4.defvjp(_conv4_fwd, _conv4_bwd)


HALO = 8


def _gdn_gates(sm, alog, dtb):
    lane = _iota((1, HD), 1)
    g = -jnp.exp(alog) * _softplus(sm + dtb)
    return (jnp.where((lane >= L_GA) & (lane < L_GA + NG), g,
                      jnp.where((lane >= L_GB) & (lane < L_GB + NG), _sigmoid(sm), 0.0)),)


def _gdn_prep(gq, gk, gv, gates, taps, h):
    q, k, v = [_silu(_conv4(x, *taps[4 * j:4 * j + 4]))[HALO:] for j, x in enumerate((gq, gk, gv))]
    q = q * lax.rsqrt(jnp.sum(q * q, axis=-1, keepdims=True) + NORM_EPS) * (HD ** -0.5)
    k = k * lax.rsqrt(jnp.sum(k * k, axis=-1, keepdims=True) + NORM_EPS)
    return q, k, v, _lane_pick(gates, L_GA + h), _lane_pick(gates, L_GB + h)


def _split(x, n):
    parts, rest = [], x
    for i in range(n):
        parts.append(rest.astype(BF16))
        if i + 1 < n:
            rest = rest - parts[-1].astype(F32)
    return parts


def _raw_dot(a, b, form):
    lead = a.ndim - 2
    ca, cb = {"nn": (1, 0), "nt": (1, 1), "tn": (0, 0)}[form]
    batch = ((0,), (0,)) if lead else ((), ())
    return lax.dot_general(a, b, (((ca + lead,), (cb + lead,)), batch), preferred_element_type=F32)


def _pdot_impl(a, b, form, mode):
    if mode == "1":
        return _raw_dot(a.astype(BF16), b.astype(BF16), form)
    if mode == "3":
        (ah, al), (bh, bl) = _split(a, 2), _split(b, 2)
        return _raw_dot(ah, bh, form) + (_raw_dot(al, bh, form) + _raw_dot(ah, bl, form))
    if mode == "xa":
        return sum(_raw_dot(a.astype(BF16), t, form) for t in reversed(_split(b, 3)))
    return sum(_raw_dot(t, b.astype(BF16), form) for t in reversed(_split(a, 3)))


@functools.partial(jax.custom_vjp, nondiff_argnums=(2, 3))
def _pdot(a, b, form, mode):
    return _pdot_impl(a, b, form, mode)


def _pdot_fwd(a, b, form, mode):
    return _pdot_impl(a, b, form, mode), (a, b)


def _pdot_bwd(form, mode, res, ct):
    a, b = res
    da_args, db_args = {"nn": ((ct, b, "nt"), (a, ct, "tn")), "nt": ((ct, b, "nn"), (ct, a, "tn")),
                        "tn": ((b, ct, "nt"), (a, ct, "nn"))}[form]

    def side(args, exact):
        if mode in ("1", "3"):
            return mode
        return "xa" if args[0] is exact else "xb"

    if mode == "xa":
        return jnp.zeros_like(a), _pdot_impl(*db_args, side(db_args, a))
    if mode == "xb":
        return _pdot_impl(*da_args, side(da_args, b)), jnp.zeros_like(b)
    return _pdot_impl(*da_args, mode), _pdot_impl(*db_args, mode)


_pdot.defvjp(_pdot_fwd, _pdot_bwd)

GDN_QK, GDN_INV, GDN_SCAN = "1", "1", "1"


@jax.custom_vjp
def _tri_inv(low):
    eye = (_iota((CHUNK, CHUNK), 0) == _iota((CHUNK, CHUNK), 1)).astype(F32)
    inv = eye - low
    pw = low
    for _ in range(5):
        pw = _pdot_impl(pw, pw, "nn", GDN_INV)
        inv = inv + _pdot_impl(inv, pw, "nn", GDN_INV)
    return inv


def _tri_inv_fwd(low):
    inv = _tri_inv(low)
    return inv, inv


def _tri_inv_bwd(inv, ct):
    return (-_pdot_impl(_pdot_impl(inv, ct, "tn", GDN_INV), inv, "nt", GDN_INV),)


_tri_inv.defvjp(_tri_inv_fwd, _tri_inv_bwd)


def _gdn_intra(q, k, v, g, beta):
    n = q.shape[0]
    r, c = _iota((CHUNK, CHUNK), 0), _iota((CHUNK, CHUNK), 1)
    tril, strict = r >= c, r > c
    trilf = jnp.broadcast_to(tril.astype(F32), (n, CHUNK, CHUNK))
    gcm = _pdot(trilf, jnp.broadcast_to(g, (n, CHUNK, CHUNK)), "nn", "xa")
    gcf = _pdot(trilf, jnp.broadcast_to(g, (n, CHUNK, HD)), "nn", "xa")
    lane0 = (_iota((1, 1, CHUNK), 2) == 0).astype(F32)
    gcr = _pdot(jnp.ones((n, CHUNK, CHUNK), F32), gcm * lane0, "nt", "xa")
    decay = jnp.where(tril, jnp.exp(jnp.where(tril, gcm - gcr, 0.0)), 0.0)
    egc = jnp.exp(gcf)
    kb = k * beta
    low = jnp.where(strict, _pdot(kb, k, "nt", GDN_QK) * decay, 0.0)
    inv = _tri_inv(low)
    u = _pdot(inv, v * beta, "nn", GDN_INV)
    w = _pdot(inv, kb * egc, "nn", GDN_INV)
    at = jnp.where(tril, _pdot(q, k, "nt", GDN_QK) * decay, 0.0)
    gl = jnp.sum(jnp.broadcast_to(g, (n, CHUNK, HD)), axis=1, keepdims=True)
    return u, w, q * egc, at, k * jnp.exp(gl - gcf), gl


def _gdn_step(s, u, w, qg, at, kd, gl):
    vn = u - _pdot(w, s, "nn", GDN_SCAN)
    o = _pdot(qg, s, "nn", GDN_SCAN) + _pdot(at, vn, "nn", GDN_SCAN)
    s2 = s * jnp.exp(gl) + _pdot(kd, vn, "tn", GDN_SCAN)
    return o, s2


SCAN_HEADS = 3


def _gdn_chunked_scratch(nc):
    big = pltpu.VMEM((nc, CHUNK, HD), F32)
    return [big, big, big, pltpu.VMEM((nc, CHUNK, 1), F32), pltpu.VMEM((nc, CHUNK, 1), F32)]


def _gdn_term_shapes(nc):
    return [(nc, CHUNK, HD), (nc, CHUNK, HD), (nc, CHUNK, HD), (nc, CHUNK, CHUNK), (nc, CHUNK, HD), (nc, 1, HD)]


def _per_head(shape, heads=None, one_buffer=True):
    lead = (None,) if heads is None else (heads,)
    return pl.BlockSpec(lead + tuple(shape), lambda h: (h,) + (0,) * len(shape),
                        pipeline_mode=ONE_BUFFER if one_buffer else None)


def _gdn_in_specs(t):
    cw = lambda cb: pl.BlockSpec((4, HD), lambda h, cb=cb: (0, cb + h))
    return [_pcol(t, GQ), _pcol(t, GK), _pcol(t, GV), _small(t), cw(0), cw(NG), cw(2 * NG)]


def _taps(wq, wk, wv):
    return tuple(w[k:k + 1, :] for w in (wq, wk, wv) for k in range(4))


def _prep_rows(t):
    return min(t, 256)


def _gdn_pad(srcs, pads):
    for src, pad in zip(srcs, pads):
        pad[0:HALO, :] = jnp.zeros((HALO, HD), F32)
        pad[HALO:, :] = src[...]


def _gdn_stage(pads, gates, taps, h, chunked):
    t = gates.shape[0]
    rows = _prep_rows(t)
    per = rows // CHUNK

    def tile(i, carry):
        r0 = pl.multiple_of(i * rows, rows)
        vals = _gdn_prep(*[p[pl.ds(r0, rows + HALO), :] for p in pads], gates[pl.ds(r0, rows), :], taps, h)
        for v, r in zip(vals, chunked):
            r[pl.ds(i * per, per)] = v.reshape(per, CHUNK, v.shape[-1])
        return carry

    lax.fori_loop(0, t // rows, tile, 0)


def _gdn_intra_all(chunked, intra):
    nc = chunked[0].shape[0]
    grp_n = math.gcd(nc, GROUP)

    def grp(i, carry):
        sl = pl.ds(pl.multiple_of(i * grp_n, grp_n), grp_n)
        for r, val in zip(intra, _gdn_intra(*[c[sl] for c in chunked])):
            r[sl] = val
        return carry

    lax.fori_loop(0, nc // grp_n, grp, 0)


def _gdn_fwd(pa, gates, conv):
    t = pa.shape[0]
    nc = t // CHUNK
    terms = _gdn_term_shapes(nc)

    def body(gq, gk, gv, gt, wq, wk, wv, *rest):
        h = pl.program_id(0)
        intra, chunked, pads = rest[:6], rest[6:11], rest[11:]
        _gdn_pad((gq, gk, gv), pads)
        _gdn_stage(pads, gt, _taps(wq, wk, wv), h, chunked)
        _gdn_intra_all(chunked, intra)

    return pl.pallas_call(
        body, grid=(NG,), name="gdn_fwd", in_specs=_gdn_in_specs(t),
        out_specs=[_per_head(sh, one_buffer=False) for sh in terms], out_shape=[SDS((NG,) + sh, F32) for sh in terms],
        scratch_shapes=_gdn_chunked_scratch(nc) + [pltpu.VMEM((t + HALO, HD), F32)] * 3, compiler_params=_cp("parallel"),
    )(pa, pa, pa, gates, conv, conv, conv)


def _gdn_scan(terms_in):
    nc = terms_in[0].shape[1]
    terms = _gdn_term_shapes(nc)

    def body(*refs):
        intra, o, states = refs[:6], refs[6], refs[7]

        def step(c, ss):
            rows = pl.ds(pl.multiple_of(c * CHUNK, CHUNK), CHUNK)
            loaded = [[r[hh, c] for r in intra] for hh in range(SCAN_HEADS)]
            res = [_gdn_step(ss[hh], *loaded[hh]) for hh in range(SCAN_HEADS)]
            for hh in range(SCAN_HEADS):
                states[hh, c] = ss[hh]
                o[rows, hh * HD:(hh + 1) * HD] = res[hh][0]
            return tuple(r[1] for r in res)

        lax.fori_loop(0, nc, step, tuple(jnp.zeros((HD, HD), F32) for _ in range(SCAN_HEADS)))

    return pl.pallas_call(
        body, grid=(NG // SCAN_HEADS,), name="gdn_scan", in_specs=[_per_head(sh, SCAN_HEADS) for sh in terms],
        out_specs=[pl.BlockSpec((nc * CHUNK, SCAN_HEADS * HD), lambda h: (0, h), pipeline_mode=ONE_BUFFER),
                   _per_head((nc, HD, HD), SCAN_HEADS)],
        out_shape=[SDS((nc * CHUNK, NG * HD), F32), SDS((NG, nc, HD, HD), F32)], compiler_params=_cp("parallel"),
    )(*terms_in)


def _gdn_bwd_scan(saved, do_raw):
    nc = saved[0].shape[1]
    terms = _gdn_term_shapes(nc)

    def body(*refs):
        intra, states, do, outs = refs[:6], refs[6], refs[7], refs[8:]

        def bwd(i, dss):
            c = nc - 1 - i
            rows = pl.ds(pl.multiple_of(c * CHUNK, CHUNK), CHUNK)
            loaded = [[states[hh, c]] + [r[hh, c] for r in intra] for hh in range(SCAN_HEADS)]
            cts = [do[rows, hh * HD:(hh + 1) * HD] for hh in range(SCAN_HEADS)]
            grads = [jax.vjp(_gdn_step, *loaded[hh])[1]((cts[hh], dss[hh])) for hh in range(SCAN_HEADS)]
            for hh in range(SCAN_HEADS):
                for r, gval in zip(outs, grads[hh][1:]):
                    r[hh, c] = gval
            return tuple(g[0] for g in grads)

        lax.fori_loop(0, nc, bwd, tuple(jnp.zeros((HD, HD), F32) for _ in range(SCAN_HEADS)))

    return pl.pallas_call(
        body, grid=(NG // SCAN_HEADS,), name="gdn_bwd_scan",
        in_specs=[_per_head(sh, SCAN_HEADS) for sh in terms] + [_per_head((nc, HD, HD), SCAN_HEADS)]
        + [pl.BlockSpec((nc * CHUNK, SCAN_HEADS * HD), lambda h: (0, h), pipeline_mode=ONE_BUFFER)],
        out_specs=[_per_head(sh, SCAN_HEADS) for sh in terms],
        out_shape=[SDS((NG,) + sh, F32) for sh in terms], compiler_params=_cp("parallel"),
    )(*saved, do_raw)


def _gdn_bwd(pa, gates, conv, dterms):
    t = pa.shape[0]
    nc = t // CHUNK
    terms = _gdn_term_shapes(nc)

    def body(*refs):
        gq, gk, gv, gt, wq, wk, wv = refs[:7]
        dintra = refs[7:13]
        dgq, dgk, dgv, dgt, dwq, dwk, dwv = refs[13:20]
        chunked, pads, dpads, dgt_s = refs[20:25], refs[25:28], refs[28:31], refs[31]
        h = pl.program_id(0)
        taps = _taps(wq, wk, wv)
        _gdn_pad((gq, gk, gv), pads)
        _gdn_stage(pads, gt, taps, h, chunked)
        grp_n = math.gcd(nc, GROUP)

        def grp(i, carry):
            sl = pl.ds(pl.multiple_of(i * grp_n, grp_n), grp_n)
            _, vjp = jax.vjp(_gdn_intra, *[r[sl] for r in chunked])
            for r, gval in zip(chunked, vjp(tuple(r[sl] for r in dintra))):
                r[sl] = gval
            return carry

        lax.fori_loop(0, nc // grp_n, grp, 0)

        rows = _prep_rows(t)
        per = rows // CHUNK
        for r in dpads:
            r[...] = jnp.zeros_like(r)

        def tile(i, dtaps):
            r0 = pl.multiple_of(i * rows, rows)
            win = pl.ds(r0, rows + HALO)
            _, vjp = jax.vjp(lambda *a: _gdn_prep(*a, h), *[p[win, :] for p in pads], gt[pl.ds(r0, rows), :], taps)
            grads = vjp(tuple(r[pl.ds(i * per, per)].reshape(rows, r.shape[-1]) for r in chunked))
            for r, gval in zip(dpads, grads[:3]):
                r[win, :] += gval
            dgt_s[pl.ds(r0, rows), :] = grads[3]
            return jax.tree.map(jnp.add, dtaps, grads[4])

        dtaps = lax.fori_loop(0, t // rows, tile, (jnp.zeros((1, HD), F32),) * 12)
        for r, dpad in zip((dgq, dgk, dgv), dpads):
            r[...] = dpad[HALO:, :].astype(r.dtype)
        for j, r in enumerate((dwq, dwk, dwv)):
            for k in range(4):
                r[k:k + 1, :] = dtaps[4 * j + k]

        @pl.when(h == 0)
        def _():
            dgt[...] = jnp.zeros_like(dgt)

        dgt[...] += dgt_s[...]

    head = _head(t)
    taps = pl.BlockSpec((4, HD), lambda h: (0, h))
    return pl.pallas_call(
        body, grid=(NG,), name="gdn_bwd", in_specs=_gdn_in_specs(t) + [_per_head(sh) for sh in terms],
        out_specs=[head, head, head, _small(t), taps, taps, taps],
        out_shape=[SDS((t, NG * HD), BF16)] * 3 + [SDS((t, HD), F32)] + [SDS((4, NG * HD), F32)] * 3,
        scratch_shapes=_gdn_chunked_scratch(nc) + [pltpu.VMEM((t + HALO, HD), F32)] * 6 + [pltpu.VMEM((t, HD), F32)],
        compiler_params=_cp("arbitrary"),
    )(pa, pa, pa, gates, conv, conv, conv, *dterms)


def _gdn_post(o, z, gain):
    return (jnp.concatenate(
        [_rms(o[:, h * HD:(h + 1) * HD], gain) * _silu(z[:, h * HD:(h + 1) * HD]) for h in range(NG)], axis=1),)


def _place():
    return lax.axis_index("x"), lax.axis_index("y"), lax.axis_index("c")


def _all_gather(name, shard):
    def body(x_ref, out_ref, send_sems, recv_sems, local_sem):
        x, y, c = _place()
        me, sibling = (x, y, c), (x, y, 1 - c)
        chips = [(1 - x, y), (x, 1 - y), (1 - x, 1 - y)]

        def blk(px, py, pc):
            return out_ref.at[4 * px + 2 * py + pc]

        def copy(k, block, to, src=None):
            return pltpu.make_async_remote_copy(
                src_ref=blk(*block) if src is None else src, dst_ref=blk(*block),
                send_sem=send_sems.at[k], recv_sem=recv_sems.at[k], device_id=to, device_id_type=MESH)

        mine = pltpu.make_async_copy(x_ref, blk(*me), local_sem)
        mine.start()
        first = [copy(0, me, sibling, src=x_ref)]
        first += [copy(1 + j, me, (*chip, c), src=x_ref) for j, chip in enumerate(chips)]
        for cp in first:
            cp.start()
        passed = [copy(4 + j, (*chip, c), sibling) for j, chip in enumerate(chips)]
        for j, chip in enumerate(chips):
            copy(1 + j, (*chip, c), me).wait_recv()
            passed[j].start()
        copy(0, sibling, me).wait_recv()
        for j, chip in enumerate(chips):
            copy(4 + j, (*chip, 1 - c), me).wait_recv()
        for cp in first + passed:
            cp.wait_send()
        mine.wait()

    return pl.pallas_call(
        body, name=name, out_shape=SDS((N_DEV,) + shard.shape, shard.dtype),
        in_specs=[pl.BlockSpec(memory_space=pltpu.HBM)], out_specs=pl.BlockSpec(memory_space=pltpu.HBM),
        scratch_shapes=[pltpu.SemaphoreType.DMA((7,)), pltpu.SemaphoreType.DMA((7,)), pltpu.SemaphoreType.DMA],
    )(shard)


def _scatter_exchange(name, full):
    def body(g_ref, out_ref, send_sems, recv_sems, local_sem):
        x, y, c = _place()
        me = 4 * x + 2 * y + c
        mine = pltpu.make_async_copy(g_ref.at[me], out_ref.at[me], local_sem)
        mine.start()
        sends, recvs = [], []
        for k in range(1, N_DEV):
            px = 1 - x if k & 4 else x
            py = 1 - y if k & 2 else y
            pc = 1 - c if k & 1 else c
            peer = 4 * px + 2 * py + pc
            sends.append(pltpu.make_async_remote_copy(
                src_ref=g_ref.at[peer], dst_ref=out_ref.at[me], send_sem=send_sems.at[k - 1],
                recv_sem=recv_sems.at[k - 1], device_id=(px, py, pc), device_id_type=MESH))
            recvs.append(pltpu.make_async_remote_copy(
                src_ref=g_ref.at[me], dst_ref=out_ref.at[peer], send_sem=send_sems.at[k - 1],
                recv_sem=recv_sems.at[k - 1], device_id=(px, py, pc), device_id_type=MESH))
        for cp in sends:
            cp.start()
        for cp in recvs:
            cp.wait_recv()
        for cp in sends:
            cp.wait_send()
        mine.wait()

    return pl.pallas_call(
        body, name=name, out_shape=SDS(full.shape, full.dtype),
        in_specs=[pl.BlockSpec(memory_space=pltpu.HBM)], out_specs=pl.BlockSpec(memory_space=pltpu.HBM),
        scratch_shapes=[pltpu.SemaphoreType.DMA((7,)), pltpu.SemaphoreType.DMA((7,)), pltpu.SemaphoreType.DMA],
    )(full)


def _sum_blocks(name, parts):
    _, r, c = parts.shape
    tr = 64 if r % 64 == 0 else r

    def body(x, o):
        acc = x[0].astype(F32)
        for d in range(1, N_DEV):
            acc = acc + x[d].astype(F32)
        o[...] = acc

    return pl.pallas_call(
        body, grid=(r // tr,), name=name, in_specs=[pl.BlockSpec((N_DEV, tr, c), lambda i: (0, i, 0))],
        out_specs=pl.BlockSpec((tr, c), lambda i: (i, 0)), out_shape=SDS((r, c), F32), compiler_params=_cp("parallel"),
    )(parts)


def _reduce_scatter(name, full):
    return _sum_blocks(name + "_sum", _scatter_exchange(name, full))


def _all_reduce_small(name, x, reduce):
    m_per, n = x.shape

    def body(x_ref, out_ref, send_sems, recv_sems, local_sem):
        px, py, pc = _place()
        me, sibling = (px, py, pc), (px, py, 1 - pc)
        chips = [(1 - px, py), (px, 1 - py), (1 - px, 1 - py)]
        buf = out_ref

        def rows(qx, qy, qc):
            return buf.at[pl.ds((4 * qx + 2 * qy + qc) * m_per, m_per), :]

        def copy(k, block, to, src=None):
            return pltpu.make_async_remote_copy(
                src_ref=rows(*block) if src is None else src, dst_ref=rows(*block),
                send_sem=send_sems.at[k], recv_sem=recv_sems.at[k], device_id=to, device_id_type=MESH)

        mine = pltpu.make_async_copy(x_ref, rows(*me), local_sem)
        mine.start()
        first = [copy(0, me, sibling, src=x_ref)]
        first += [copy(1 + j, me, (*chip, pc), src=x_ref) for j, chip in enumerate(chips)]
        for cp in first:
            cp.start()
        passed = [copy(4 + j, (*chip, pc), sibling) for j, chip in enumerate(chips)]
        for j, chip in enumerate(chips):
            copy(1 + j, (*chip, pc), me).wait_recv()
            passed[j].start()
        copy(0, sibling, me).wait_recv()
        for j, chip in enumerate(chips):
            copy(4 + j, (*chip, 1 - pc), me).wait_recv()
        for cp in first + passed:
            cp.wait_send()
        mine.wait()

    gathered = pl.pallas_call(
        body, name=name, out_shape=SDS((N_DEV * m_per, n), x.dtype),
        in_specs=[pl.BlockSpec(memory_space=pltpu.VMEM)], out_specs=pl.BlockSpec(memory_space=pltpu.VMEM),
        scratch_shapes=[pltpu.SemaphoreType.DMA((7,)), pltpu.SemaphoreType.DMA((7,)), pltpu.SemaphoreType.DMA],
    )(x)
    if not reduce:
        return gathered
    return _sum_blocks(name + "_sum", gathered.reshape(N_DEV, m_per, n))


HBM_SPEC = pl.BlockSpec(memory_space=pltpu.HBM)
SEM_SPEC = pl.BlockSpec(memory_space=pltpu.SEMAPHORE)
EFFECT = pltpu.SideEffectType.DATAFLOW_SIDE_EFFECTING


def _copies_start(name, bufs, n_remote, n_local, build, deps):
    nb, nd = len(bufs), len(deps)
    sem_shapes = [pltpu.SemaphoreType.DMA((n_remote,)), pltpu.SemaphoreType.DMA((n_remote,))]
    if n_local:
        sem_shapes.append(pltpu.SemaphoreType.DMA((n_local,)))
    ns = len(sem_shapes)

    def body(*refs):
        sems = refs[nb + nd:nb + nd + ns]
        remote, local = build(refs[:nb], *sems, *([None] * (3 - ns)))
        for cp in local + remote:
            cp.start()
        refs[-1][...] = jnp.zeros((8, HD), F32)

    outs = pl.pallas_call(
        body, name=name,
        out_shape=(*sem_shapes, *[pltpu.HBM(b.shape, b.dtype) for b in bufs], SDS((8, HD), F32)),
        in_specs=[HBM_SPEC] * nb + [ANY_SPEC] * nd,
        out_specs=(*[SEM_SPEC] * ns, *[HBM_SPEC] * nb, pl.BlockSpec(memory_space=pltpu.VMEM)),
        input_output_aliases={i: ns + i for i in range(nb)},
        compiler_params=pltpu.CompilerParams(has_side_effects=EFFECT),
    )(*[pltpu.with_memory_space_constraint(b, pltpu.HBM) for b in bufs], *deps)
    return list(outs[:ns]), list(outs[ns:ns + nb]), outs[-1]


def _copies_wait(name, bufs, sems, build, after):
    nb, ns = len(bufs), len(sems)

    def body(*refs):
        remote, local = build(refs[:nb], *refs[nb:nb + ns], *([None] * (3 - ns)))
        for cp in local:
            cp.wait()
        for cp in remote:
            cp.wait_send()
            cp.wait_recv()

    outs = pl.pallas_call(
        body, name=name, out_shape=tuple(pltpu.HBM(b.shape, b.dtype) for b in bufs),
        in_specs=[HBM_SPEC] * nb + [SEM_SPEC] * ns + [ANY_SPEC] * len(after), out_specs=tuple([HBM_SPEC] * nb),
        input_output_aliases={i: i for i in range(nb)},
        compiler_params=pltpu.CompilerParams(has_side_effects=EFFECT),
    )(*bufs, *sems, *after)
    return list(outs)


def _remote(src, dst, send, recv, k, to):
    return pltpu.make_async_remote_copy(src_ref=src, dst_ref=dst, send_sem=send.at[k], recv_sem=recv.at[k],
                                        device_id=to, device_id_type=MESH)


class _Gather:
    def __init__(self, name, shards, deps):
        self.name, self.n = name, len(shards)
        lands = [lax.empty((N_DEV,) + s.shape, s.dtype) for s in shards]
        self.sems1, bufs, self.token = _copies_start(
            name + "_s1", list(shards) + lands, 4 * self.n, self.n, self._stage1(range(self.n)), deps)
        self.shards, self.lands, self.sems2 = bufs[:self.n], bufs[self.n:], {}

    def _stage1(self, idxs):
        def build(refs, send, recv, loc):
            x, y, c = _place()
            me = 4 * x + 2 * y + c
            targets = [(x, y, 1 - c), (1 - x, y, c), (x, 1 - y, c), (1 - x, 1 - y, c)]
            remote, local = [], []
            for pos, i in enumerate(idxs):
                src, land = refs[pos], refs[len(idxs) + pos]
                local.append(pltpu.make_async_copy(src, land.at[me], loc.at[i]))
                remote += [_remote(src, land.at[me], send, recv, 4 * i + k, to) for k, to in enumerate(targets)]
            return remote, local
        return build

    @staticmethod
    def _stage2(refs, send, recv, loc):
        x, y, c = _place()
        remote = []
        for pos, land in enumerate(refs):
            for j, (cx, cy) in enumerate([(1 - x, y), (x, 1 - y), (1 - x, 1 - y)]):
                blk = land.at[4 * cx + 2 * cy + c]
                remote.append(_remote(blk, blk, send, recv, 3 * pos + j, (x, y, 1 - c)))
        return remote, []

    def pass_on(self, idxs, after):
        tag, m = "".join(map(str, idxs)), len(idxs)
        bufs = _copies_wait(f"{self.name}_w1_{tag}", [self.shards[i] for i in idxs] + [self.lands[i] for i in idxs],
                            self.sems1, self._stage1(idxs), after)
        self.sems2[tag], lands, token = _copies_start(f"{self.name}_s2_{tag}", bufs[m:], 3 * m, 0, self._stage2, ())
        for pos, i in enumerate(idxs):
            self.lands[i] = lands[pos]
        return [token]

    def get(self, idxs, after):
        tag = "".join(map(str, idxs))
        return _copies_wait(f"{self.name}_w2_{tag}", [self.lands[i] for i in idxs], self.sems2[tag], self._stage2, after)


def _rows_tile(r, row_bytes, target=1 << 20):
    tr = r
    while tr % 32 == 0 and tr * row_bytes > target:
        tr //= 2
    return tr


def _pair_add(name, g, got, c):
    _, r, cols = g.shape
    tr = _rows_tile(r, cols * 2)

    def body(s, a, b, o):
        o[...] = (a[...].astype(F32) + b[...].astype(F32)).astype(o.dtype)

    return pl.pallas_call(
        body, name=name, out_shape=SDS((4, r, cols), g.dtype),
        grid_spec=pltpu.PrefetchScalarGridSpec(
            num_scalar_prefetch=1, grid=(4, r // tr),
            in_specs=[pl.BlockSpec((None, tr, cols), lambda j, i, s: (2 * j + s[0], i, 0)),
                      pl.BlockSpec((None, tr, cols), lambda j, i, s: (j, i, 0))],
            out_specs=pl.BlockSpec((None, tr, cols), lambda j, i, s: (j, i, 0))),
        compiler_params=_cp("parallel", "parallel"),
    )(c.reshape(1), g, got)


def _quad_sum(name, part, got, chip, wmv=None):
    _, r, cols = part.shape
    tr = _rows_tile(r, cols * 4)
    n_out = 4 if wmv else 1

    def body(s, a, b1, b2, b3, *rest):
        g = ((a[...].astype(F32) + b1[...].astype(F32)) + b2[...].astype(F32)) + b3[...].astype(F32)
        rest[-n_out][...] = g
        if wmv:
            w, m, v = rest[:3]
            rest[-3][...], rest[-2][...], rest[-1][...] = _adamw(w[...], g, m[...], v[...])

    blk = lambda k: pl.BlockSpec((None, tr, cols), lambda i, s, k=k: (jnp.bitwise_xor(s[0], k), i, 0))
    row = pl.BlockSpec((tr, cols), lambda i, s: (i, 0))
    outs = pl.pallas_call(
        body, name=name, out_shape=[SDS((r, cols), F32)] * n_out,
        grid_spec=pltpu.PrefetchScalarGridSpec(
            num_scalar_prefetch=1, grid=(r // tr,), in_specs=[blk(0), blk(1), blk(2), blk(3)] + [row] * (n_out - 1),
            out_specs=[row] * n_out),
        compiler_params=_cp("parallel"),
    )(chip.reshape(1), part, got, got, got, *(wmv or ()))
    return tuple(outs) if wmv else outs[0]


class _Scatter:
    def __init__(self, name, grads, deps):
        self.name, self.n = name, len(grads)
        got = [lax.empty((4,) + g.shape[1:], g.dtype) for g in grads]
        self.sems, bufs, self.token = _copies_start(name + "_s1", list(grads) + got, 4 * self.n, 0, self._stage1, deps)
        self.grads, self.got = bufs[:self.n], bufs[self.n:]

    def _stage1(self, refs, send, recv, loc):
        x, y, c = _place()
        remote = []
        for i in range(self.n):
            remote += [_remote(refs[i].at[2 * j + 1 - c], refs[self.n + i].at[j], send, recv, 4 * i + j, (x, y, 1 - c))
                       for j in range(4)]
        return remote, []

    def _stage2(self, refs, send, recv, loc):
        x, y, c = _place()
        remote = []
        for i in range(self.n):
            for k in (1, 2, 3):
                tx = 1 - x if k & 2 else x
                ty = 1 - y if k & 1 else y
                remote.append(_remote(refs[i].at[2 * tx + ty], refs[self.n + i].at[2 * x + y], send, recv,
                                      3 * i + k - 1, (tx, ty, c)))
        return remote, []

    def mid(self, after):
        bufs = _copies_wait(self.name + "_w1", self.grads + self.got, self.sems, self._stage1, after)
        c = lax.axis_index("c").astype(jnp.int32)
        parts = [_pair_add(f"{self.name}_add{i}", bufs[i], bufs[self.n + i], c) for i in range(self.n)]
        got = [lax.empty(p.shape, p.dtype) for p in parts]
        self.sems, bufs, self.token = _copies_start(self.name + "_s2", parts + got, 3 * self.n, 0, self._stage2, ())
        self.parts, self.got = bufs[:self.n], bufs[self.n:]

    def end(self, after, wmv=None):
        bufs = _copies_wait(self.name + "_w2", self.parts + self.got, self.sems, self._stage2, after)
        chip = (2 * lax.axis_index("x") + lax.axis_index("y")).astype(jnp.int32)
        wmv = wmv or [None] * self.n
        return [_quad_sum(f"{self.name}_sum{i}", bufs[i], bufs[self.n + i], chip, wmv[i]) for i in range(self.n)]


def _adamw(w, g, m, v):
    m = ADAM_B1 * m + (1.0 - ADAM_B1) * g
    v = ADAM_B2 * v + (1.0 - ADAM_B2) * (g * g)
    m_hat = m / (1.0 - ADAM_B1 ** ADAM_STEP)
    v_hat = v / (1.0 - ADAM_B2 ** ADAM_STEP)
    return -ADAM_LR * (m_hat / (jnp.sqrt(v_hat) + ADAM_EPS) + ADAM_WD * w), m, v


def _adamw_call(name, w, g, m, v):
    r, c = w.shape
    tm = 64 if r % 64 == 0 else r
    return _rowwise(name, _adamw, [w, g, m, v], [], [(c, F32)] * 3, tm)


_IN_COLS = 5906


def _perm_in(w):
    pad = jnp.zeros((w.shape[0], 2 * HALF - _IN_COLS), w.dtype)
    return (jnp.concatenate([w[:, 2310:4614], w[:, 4614:5382]], axis=1),
            jnp.concatenate([w[:, :2304], w[:, 5394:5906], w[:, 2304:2310], w[:, 5382:5394], pad], axis=1))


def _unperm_in(ga, gb):
    return jnp.concatenate([gb[:, :2304], gb[:, 2816:2822], ga[:, :2304], ga[:, 2304:3072], gb[:, 2822:2834],
                            gb[:, 2304:2816]], axis=1)


def _lanes(v, at):
    return jnp.pad(v, ((0, 0), (at, HD - at - v.shape[1])))


_PACK = ("norm_mix", "mem_norm", "norm_ffn", "gdn_conv", "fox_q_norm", "fox_k_norm", "gdn_out_norm", "mem_q_norm",
         "mem_k_norm", "fox_f_bias", "gdn_a_log", "gdn_dt_bias", "loss")


def _pack(vals):
    parts = [vals[n].reshape(-1, HD) for n in _PACK]
    used = sum(p.shape[0] for p in parts)
    buf = jnp.concatenate(parts + [jnp.zeros((-used % 8, HD), F32)], axis=0)
    return buf, [(n, p.shape[0]) for n, p in zip(_PACK, parts)]


def _unpack(buf, layout):
    out, at = {}, 0
    for n, rows in layout:
        out[n] = buf[at:at + rows]
        at += rows
    return out


def kernel(x, mem, norm_mix, w_in, fox_f_bias, fox_q_norm, fox_k_norm, gdn_conv, gdn_a_log, gdn_dt_bias, gdn_out_norm, mem_norm, w_mem_kv, mem_q_norm, mem_k_norm, w_out, norm_ffn, w_gate_up, w_down, loss_target, m_norm_mix, m_w_in, m_fox_f_bias, m_fox_q_norm, m_fox_k_norm, m_gdn_conv, m_gdn_a_log, m_gdn_dt_bias, m_gdn_out_norm, m_mem_norm, m_w_mem_kv, m_mem_q_norm, m_mem_k_norm, m_w_out, m_norm_ffn, m_w_gate_up, m_w_down, v_norm_mix, v_w_in, v_fox_f_bias, v_fox_q_norm, v_fox_k_norm, v_gdn_conv, v_gdn_a_log, v_gdn_dt_bias, v_gdn_out_norm, v_mem_norm, v_w_mem_kv, v_mem_q_norm, v_mem_k_norm, v_w_out, v_norm_ffn, v_w_gate_up, v_w_down):
    args = dict(locals())
    d = x.shape[2]
    me = 4 * lax.axis_index("x") + 2 * lax.axis_index("y") + lax.axis_index("c")

    cshard = gdn_conv[0].shape[1]
    conv_pad = jnp.pad(gdn_conv[0], ((0, 4), (0, 3 * HD - cshard)))
    conv_all = _all_reduce_small("ag_conv", conv_pad, False).reshape(N_DEV, 8, 3 * HD)[:, :4, :cshard]
    conv_all = conv_all.transpose(1, 0, 2).reshape(4, N_DEV * cshard)
    w_in_a, w_in_b = _perm_in(w_in[0])
    comm = _StepComm({"in_b": [w_in_b], "in_a": [w_in_a], "kv_out": [w_mem_kv[0], w_out[0]], "gate_up": [w_gate_up[0]],
                      "down": [w_down[0]]}, [conv_all])

    grad_x, loss_local, small_grads = _local_step(
        x[0], mem[0], loss_target[0], norm_mix, fox_f_bias, fox_q_norm, fox_k_norm, gdn_a_log, gdn_dt_bias,
        gdn_out_norm, mem_norm, mem_q_norm, mem_k_norm, norm_ffn, conv_all, comm)

    wmv = lambda n: (args[n][0], args["m_" + n][0], args["v_" + n][0])
    red = comm.finish([grad_x], {"ffn": [wmv("w_down"), wmv("w_gate_up")], "a": [None, wmv("w_out"), wmv("w_mem_kv")],
                                 "b": [None]})
    updated = {"w_down": red["ffn"][0], "w_gate_up": red["ffn"][1], "w_out": red["a"][1], "w_mem_kv": red["a"][2]}
    grads = {n: r[0] for n, r in updated.items()}
    grads["w_in"] = _unperm_in(red["a"][0], red["b"][0])
    small_grads["loss"] = jnp.broadcast_to(loss_local, (1, HD))
    packed, layout = _pack(small_grads)
    small = _unpack(_all_reduce_small("ar_small", packed, True), layout)
    loss = small["loss"][0, 0]
    six = {"fox_f_bias": L_FF, "gdn_a_log": L_GA, "gdn_dt_bias": L_GA}
    for n, rows_n in layout[:-1]:
        gsm = small[n]
        if n == "gdn_conv":
            gsm = lax.dynamic_slice(gsm.reshape(4, N_DEV * cshard), (0, me * cshard), (4, cshard))[None]
        elif n in six:
            gsm = gsm[:, six[n]:six[n] + 6]
        else:
            gsm = gsm.reshape(1, rows_n * HD)
        grads[n] = gsm

    names = ['norm_mix', 'w_in', 'fox_f_bias', 'fox_q_norm', 'fox_k_norm', 'gdn_conv', 'gdn_a_log', 'gdn_dt_bias',
             'gdn_out_norm', 'mem_norm', 'w_mem_kv', 'mem_q_norm', 'mem_k_norm', 'w_out', 'norm_ffn', 'w_gate_up', 'w_down']
    big = ("w_in", "w_mem_kv", "w_out", "w_gate_up", "w_down")
    delta, new_m, new_v = {}, {}, {}
    for n in big:
        res = updated[n][1:] if n in updated else _adamw_call("adamw_" + n, args[n][0], grads[n], *wmv(n)[1:])
        delta[n], new_m[n], new_v[n] = [a[None] for a in res]
        grads[n] = grads[n][None]

    def flat(a):
        a = a.reshape(1, -1)
        return jnp.pad(a, ((0, 0), (0, -a.shape[1] % HD))).reshape(-1, HD)

    smalls = [n for n in names if n not in big]
    pk = lambda pre: jnp.concatenate([flat(grads[n] if pre == "g" else args[pre + n]) for n in smalls], axis=0)
    cat = [pk(""), pk("g"), pk("m_"), pk("v_")]
    padr = -cat[0].shape[0] % 8
    cat = [jnp.pad(a, ((0, padr), (0, 0))) for a in cat]
    res = _adamw_call("adamw_small", *cat)
    at = 0
    for n in smalls:
        shape = args[n].shape
        size = math.prod(shape)
        nrow = -(-size // HD)
        for dst, src in zip((delta, new_m, new_v), res):
            dst[n] = src[at:at + nrow].reshape(-1)[:size].reshape(shape)
        at += nrow

    return (loss, grad_x[None], *[grads[n] for n in names], *[delta[n] for n in names],
            *[new_m[n] for n in names], *[new_v[n] for n in names])


class _StepComm:
    def __init__(self, shard_groups, after):
        self.groups, shards = {}, []
        for key, ws in shard_groups.items():
            self.groups[key] = list(range(len(shards), len(shards) + len(ws)))
            shards += [w.astype(BF16) for w in ws]
        self.gather = _Gather("ag", shards, after)
        self.passed, self.scatters = set(), {}

    def start_deps(self):
        return [self.gather.token]

    def pass_on(self, key, after):
        self.passed.add(key)
        return self.gather.pass_on(self.groups[key], after)

    def weights(self, key, after):
        if key not in self.passed:
            after = self.pass_on(key, after)
        return self.gather.get(self.groups[key], after)

    def send(self, tag, grads):
        blocks = [g if g.ndim == 3 else g.reshape(N_DEV, g.shape[0] // N_DEV, g.shape[1]) for g in grads]
        self.scatters[tag] = _Scatter("rs_" + tag, blocks, ())
        return [self.scatters[tag].token]

    def mid(self, tag, after):
        self.scatters[tag].mid(after)
        return [self.scatters[tag].token]

    def finish(self, after, wmv):
        return {tag: sc.end(after, wmv[tag]) for tag, sc in self.scatters.items()}


def _local_step(xs, ms, tgt, norm_mix, fox_f_bias, fox_q_norm, fox_k_norm, gdn_a_log, gdn_dt_bias, gdn_out_norm,
                mem_norm, mem_q_norm, mem_k_norm, norm_ffn, conv_all, comm):
    t, d = xs.shape
    bq = min(t, 256)
    fb, alog, dtb = _lanes(fox_f_bias, L_FF), _lanes(gdn_a_log, L_GA), _lanes(gdn_dt_bias, L_GA)
    flat = lambda w: w.reshape(-1, w.shape[-1])

    rms1 = lambda a, g: (_rms(a, g),)
    (u,) = _rowwise("norm_mix", rms1, [xs], [norm_mix], [(d, BF16)], min(t, 256), deps=comm.start_deps())
    w_in_b = flat(comm.weights("in_b", [u])[0])
    pb = _matmul("proj_in_b", u, w_in_b, NN, F32, 1024, 768)
    o_fox = _fox_fwd(pb, fb, fox_q_norm, fox_k_norm, bq)
    w_in_a = flat(comm.weights("in_a", [o_fox])[0])
    pa = _matmul("proj_in_a", u, w_in_a, NN, F32, 1024, 768)
    smrow = (pb, HD, SM)
    (gates,) = _rowwise("gdn_gates", _gdn_gates, [smrow], [alog, dtb], [(HD, F32)], min(t, 256))
    gdn_terms = _gdn_fwd(pa, gates, conv_all)
    o_gdn_raw, gdn_states = _gdn_scan(gdn_terms)
    gdn_saved = list(gdn_terms) + [gdn_states]
    zrow = (pa, NG * HD, GZ * HD // (NG * HD))
    (o_gdn,) = _rowwise("gdn_post", _gdn_post, [o_gdn_raw, zrow], [gdn_out_norm], [(NG * HD, BF16)], min(t, 256))
    w_kv_all, w_out_all = [flat(w) for w in comm.weights("kv_out", [o_gdn])]
    (mem_n,) = _rowwise("norm_mem", rms1, [ms], [mem_norm], [(d, BF16)], ms.shape[0])
    mkv = _matmul("proj_mem", mem_n, w_kv_all, NN, F32, 256, 512)
    o_mem = _mem_fwd(pb, mkv, mem_q_norm, mem_k_norm)
    deps = comm.pass_on("gate_up", [o_mem])
    mix = jnp.concatenate([o_fox, o_gdn, o_mem], axis=1)
    h1, h1n = _proj_out_norm(mix, w_out_all, xs, norm_ffn, deps)
    (wgu,) = comm.weights("gate_up", [h1n])
    ffw = wgu.shape[2]
    gu, act = _ffn_up(h1n, wgu.reshape(2, 4, d, ffw))
    w_down_all = flat(comm.weights("down", [act])[0])
    dyb, lsum = _ffn_down_loss(act, w_down_all, h1, tgt)
    loss_local = (0.5 / d) * jnp.sum(lsum[::8, ::HD])

    dgu = _ffn_down_bwd(dyb, w_down_all.reshape(4, ffw, d), gu).reshape(8, t, ffw)
    g_w_down = _matmul("grad_w_down", act, dyb, TN, BF16, 512, 2048)
    dh1n = _ffn_up_bwd_x(dgu, wgu)
    g_w_gu = _ffn_up_bwd_w(h1n, dgu)
    deps = comm.send("ffn", [g_w_down, g_w_gu])
    rms2 = lambda a, g: (_rms(a, g), a)
    dh1b, g_norm_ffn = _rowwise_vjp("norm_ffn_bwd", rms2, [h1], [norm_ffn], [dh1n, dyb], [BF16], min(t, 256), deps=deps)

    dmix = _matmul("proj_out_bwd_x", dh1b, w_out_all, NT, BF16, 1024, 1024)
    g_w_out = _matmul("grad_w_out", mix, dh1b, TN, BF16, 1024, 2048)
    deps = comm.mid("ffn", [dmix, g_w_out])
    dmq, dmk, dmv, g_mqn, g_mkn = _mem_bwd(pb, mkv, mem_q_norm, mem_k_norm, dmix, deps=deps)
    dmkv = jnp.concatenate([dmk, dmv], axis=1).astype(BF16)
    g_w_kv = _matmul("grad_w_kv", mem_n, dmkv, TN, BF16, 512, 512)
    do_raw, dgz, g_gon = _rowwise_vjp("gdn_post_bwd", _gdn_post, [o_gdn_raw, zrow], [gdn_out_norm],
                                      [(dmix, NG * HD, 1)], [F32, BF16], min(t, 256), deps=deps)
    dterms = _gdn_bwd_scan(gdn_saved, do_raw)
    dgq, dgk, dgv, dgates, dwq, dwk, dwv = _gdn_bwd(pa, gates, conv_all, dterms)
    dsm_gdn, g_alog, g_dtb = _rowwise_vjp("gdn_gates_bwd", _gdn_gates, [smrow], [alog, dtb], [dgates], [F32], min(t, 256))
    dp_a = jnp.concatenate([dgq, dgk, dgv, dgz], axis=1)
    g_w_in_a = _matmul("grad_w_in_a", u, dp_a, TN, BF16, 512, 3072)
    deps = comm.send("a", [g_w_in_a, g_w_out, g_w_kv])
    du_a = _matmul("proj_in_bwd_a", dp_a, w_in_a, NT, F32, 1024, 1024, deps=deps)
    deps = comm.mid("a", [du_a])
    dfq, dfk, dfv, dsm_fox, g_fb, g_fqn, g_fkn = _fox_bwd(pb, fb, fox_q_norm, fox_k_norm, dmix, bq, deps=deps)
    dp_b = jnp.concatenate([dfq, dfk, dfv, dmq, (dsm_fox + dsm_gdn).astype(BF16), jnp.zeros((t, HD), BF16)], axis=1)
    g_w_in_b = _matmul("grad_w_in_b", u, dp_b, TN, BF16, 512, 3072)
    deps = comm.send("b", [g_w_in_b])
    dmem_n = _matmul("proj_mem_bwd_x", dmkv, w_kv_all, NT, F32, 256, 512, deps=deps)
    g_mem_norm = _rowwise_vjp("norm_mem_bwd", rms1, [ms], [mem_norm], [dmem_n], [], ms.shape[0])[0]
    deps = comm.mid("b", [g_mem_norm])
    du = _matmul("proj_in_bwd_b", dp_b, w_in_b, NT, F32, 1024, 1024, residual=du_a, deps=deps)
    grad_x, g_norm_mix = _rowwise_vjp("norm_mix_bwd", rms2, [xs], [norm_mix], [du, dh1b], [F32], min(t, 256))

    small_grads = {
        "norm_mix": g_norm_mix, "mem_norm": g_mem_norm, "norm_ffn": g_norm_ffn,
        "gdn_conv": jnp.concatenate([dwq, dwk, dwv], axis=1),
        "fox_q_norm": g_fqn, "fox_k_norm": g_fkn, "gdn_out_norm": g_gon, "mem_q_norm": g_mqn, "mem_k_norm": g_mkn,
        "fox_f_bias": g_fb, "gdn_a_log": g_alog, "gdn_dt_bias": g_dtb}
    return grad_x, loss_local, small_grads
```

```python
import functools
import math

import jax
import jax.numpy as jnp
from jax import lax
from jax.experimental import pallas as pl
from jax.experimental.pallas import tpu as pltpu

F32 = jnp.float32
BF16 = jnp.bfloat16
HI = lax.Precision.HIGHEST
SDS = jax.ShapeDtypeStruct

N_DEV = 8
HD = 128
NF, NG, NM = 6, 6, 4
CHUNK = 64
GROUP = 16
NORM_EPS = 1e-6
GQ, GK, GV, GZ = 0, 6, 12, 18
FQ, FK, FV, MQ, SM = 0, 6, 12, 18, 22
HALF = 24 * HD
L_FF, L_GA, L_GB = 0, 6, 12
VMEM_LIMIT = 56 * 1024 * 1024

ADAM_LR, ADAM_B1, ADAM_B2, ADAM_EPS, ADAM_WD, ADAM_STEP = 0.001, 0.9, 0.999, 1e-08, 0.01, 10

NN = (((1,), (0,)), ((), ()))
NT = (((1,), (1,)), ((), ()))
TN = (((0,), (0,)), ((), ()))
MESH = pl.DeviceIdType.MESH


def _cp(*sem):
    return pltpu.CompilerParams(dimension_semantics=tuple(sem) if sem else None, vmem_limit_bytes=VMEM_LIMIT)


def _dot(a, b, dims=NN):
    return lax.dot_general(a, b, dims, preferred_element_type=F32)


def _bdot(a, b):
    return _dot(a.astype(BF16), b.astype(BF16))


def _iota(shape, axis):
    return lax.broadcasted_iota(jnp.int32, shape, axis)


def _rms(x, gain):
    return x * lax.rsqrt(jnp.mean(x * x, axis=-1, keepdims=True) + NORM_EPS) * gain


def _sigmoid(x):
    return 0.5 * jnp.tanh(0.5 * x) + 0.5


def _silu(x):
    return x * _sigmoid(x)


def _softplus(x):
    return jnp.maximum(x, 0.0) + jnp.log(1.0 + jnp.exp(-jnp.abs(x)))


def _lane_pick(x, lane):
    oh = (_iota((1, x.shape[-1]), 1) == lane).astype(F32)
    return jnp.sum(x * oh, axis=-1, keepdims=True)


def _cumsum_rows(x):
    tril = (_iota((HD, HD), 0) >= _iota((HD, HD), 1)).astype(F32)
    carry = jnp.zeros((1, x.shape[1]), F32)
    outs = []
    for b in range(x.shape[0] // HD):
        blk = x[b * HD:(b + 1) * HD]
        outs.append(jnp.dot(tril, blk, precision=HI, preferred_element_type=F32) + carry)
        carry = carry + jnp.sum(blk, axis=0, keepdims=True)
    return jnp.concatenate(outs, axis=0)


def _row_spec(r, tm):
    if isinstance(r, tuple):
        arr, width, cb = r
        return arr, pl.BlockSpec((tm, width), lambda i, cb=cb: (i, cb))
    return r, pl.BlockSpec((tm, r.shape[1]), lambda i: (i, 0))


ANY_SPEC = pl.BlockSpec(memory_space=pl.ANY)


def _rowwise(name, fn, rows, consts, outs, tm, deps=()):
    arrs, specs = zip(*[_row_spec(r, tm) for r in rows])
    n_rows = arrs[0].shape[0]
    nr, nc, nd = len(rows), len(consts), len(deps)

    def body(*refs):
        res = fn(*[r[...] for r in refs[:nr + nc]])
        for o, v in zip(refs[nr + nc + nd:], res):
            o[...] = v.astype(o.dtype)

    return pl.pallas_call(
        body, grid=(n_rows // tm,), name=name,
        in_specs=list(specs) + [pl.BlockSpec(c.shape, lambda i: (0, 0)) for c in consts] + [ANY_SPEC] * nd,
        out_specs=[pl.BlockSpec((tm, w), lambda i: (i, 0)) for w, _ in outs],
        out_shape=[SDS((n_rows, w), dt) for w, dt in outs],
        compiler_params=_cp("parallel"),
    )(*arrs, *consts, *deps)


def _rowwise_vjp(name, fn, rows, consts, cts, grad_dtypes, tm, deps=()):
    arrs, specs = zip(*[_row_spec(r, tm) for r in rows])
    ct_arrs, ct_specs = zip(*[_row_spec(r, tm) for r in cts])
    n_rows = arrs[0].shape[0]
    nr, nc, nct, nd = len(rows), len(consts), len(cts), len(deps)
    plan = [(j, dt) for j, dts in enumerate(grad_dtypes) for dt in (dts if isinstance(dts, tuple) else (dts,))]
    ng = len(plan)
    widths = [specs[j].block_shape[1] for j, _ in plan]
    grad_dtypes = [dt for _, dt in plan]

    def body(*refs):
        vals = [r[...].astype(F32) for r in refs[:nr + nc]]
        ctv = tuple(r[...].astype(F32) for r in refs[nr + nc:nr + nc + nct])
        _, vjp = jax.vjp(fn, *vals)
        grads = vjp(ctv)
        outs = refs[nr + nc + nct + nd:]
        for o, (j, _) in zip(outs[:ng], plan):
            o[...] = grads[j].astype(o.dtype)

        @pl.when(pl.program_id(0) == 0)
        def _():
            for o in outs[ng:]:
                o[...] = jnp.zeros_like(o)

        for o, g in zip(outs[ng:], grads[nr:]):
            o[...] += g

    return pl.pallas_call(
        body, grid=(n_rows // tm,), name=name,
        in_specs=list(specs) + [pl.BlockSpec(c.shape, lambda i: (0, 0)) for c in consts] + list(ct_specs)
        + [ANY_SPEC] * nd,
        out_specs=[pl.BlockSpec((tm, w), lambda i: (i, 0)) for w in widths]
        + [pl.BlockSpec(c.shape, lambda i: (0, 0)) for c in consts],
        out_shape=[SDS((n_rows, w), dt) for w, dt in zip(widths, grad_dtypes)] + [SDS(c.shape, F32) for c in consts],
        compiler_params=_cp("arbitrary"),
    )(*arrs, *consts, *ct_arrs, *deps)


def _tile(n, pref):
    t = min(n, pref)
    while n % t or (t % HD and t != n):
        t -= 1
    return t


def _matmul(name, a, b, dims, out_dtype, tm, tn, residual=None, deps=()):
    ta, tb = dims == TN, dims == NT
    m = a.shape[1] if ta else a.shape[0]
    k = a.shape[0] if ta else a.shape[1]
    n = b.shape[0] if tb else b.shape[1]
    tm, tn = _tile(m, tm), _tile(n, tn)

    def body(*refs):
        acc = _dot(refs[0][...], refs[1][...], dims)
        if residual is not None:
            acc = acc + refs[2][...]
        refs[-1][...] = acc.astype(out_dtype)

    in_specs = [pl.BlockSpec((k, tm), lambda i, j: (0, i)) if ta else pl.BlockSpec((tm, k), lambda i, j: (i, 0)),
                pl.BlockSpec((tn, k), lambda i, j: (j, 0)) if tb else pl.BlockSpec((k, tn), lambda i, j: (0, j))]
    ops = [a, b]
    if residual is not None:
        in_specs.append(pl.BlockSpec((tm, tn), lambda i, j: (i, j)))
        ops.append(residual)
    in_specs += [ANY_SPEC] * len(deps)
    ops += list(deps)
    return pl.pallas_call(
        body, grid=(m // tm, n // tn), name=name, in_specs=in_specs,
        out_specs=pl.BlockSpec((tm, tn), lambda i, j: (i, j)), out_shape=SDS((m, n), out_dtype),
        compiler_params=_cp("parallel", "parallel"),
    )(*ops)


def _proj_out_norm(mix, w_out, xs, gain, deps):
    t, k = mix.shape
    d = w_out.shape[1]
    tm = _tile(t, 512)

    def body(*refs):
        a, b, x, g = refs[:4]
        h1, h1n = refs[4 + len(deps):]
        acc = _dot(a[...], b[...]) + x[...]
        h1[...] = acc
        h1n[...] = _rms(acc, g[...]).astype(BF16)

    return pl.pallas_call(
        body, grid=(t // tm,), name="proj_out",
        in_specs=[pl.BlockSpec((tm, k), lambda i: (i, 0)), pl.BlockSpec((k, d), lambda i: (0, 0)),
                  pl.BlockSpec((tm, d), lambda i: (i, 0)), pl.BlockSpec((1, d), lambda i: (0, 0))] + [ANY_SPEC] * len(deps),
        out_specs=[pl.BlockSpec((tm, d), lambda i: (i, 0))] * 2, out_shape=[SDS((t, d), F32), SDS((t, d), BF16)],
        compiler_params=_cp("parallel"),
    )(mix, w_out, xs, gain, *deps)


def _ffn_up(h1n, wgu):
    t, d = h1n.shape
    w = wgu.shape[3]
    tm = _tile(t, 512)

    def body(a, b, gu, act):
        x = a[...]
        g = _dot(x, b[0])
        u = _dot(x, b[1])
        gu[0] = g.astype(BF16)
        gu[1] = u.astype(BF16)
        act[...] = (_silu(g) * u).astype(BF16)

    return pl.pallas_call(
        body, grid=(4, t // tm), name="ffn_up",
        in_specs=[pl.BlockSpec((tm, d), lambda j, i: (i, 0)), pl.BlockSpec((2, None, d, w), lambda j, i: (0, j, 0, 0))],
        out_specs=[pl.BlockSpec((2, None, tm, w), lambda j, i: (0, j, i, 0)), pl.BlockSpec((tm, w), lambda j, i: (i, j))],
        out_shape=[SDS((2, 4, t, w), BF16), SDS((t, 4 * w), BF16)],
        compiler_params=_cp("parallel", "parallel"),
    )(h1n, wgu)


def _ffn_down_loss(act, wdown, h1, target):
    t, f = act.shape
    d = wdown.shape[1]
    tm, tn = _tile(t, 1024), _tile(d, 512)

    def body(a, b, h, tg, dyb, ls):
        e = _dot(a[...], b[...]) + h[...] - tg[...]
        dyb[...] = (e * (1.0 / d)).astype(BF16)
        ls[...] = jnp.broadcast_to(jnp.sum(e * e), (8, HD))

    return pl.pallas_call(
        body, grid=(t // tm, d // tn), name="ffn_down_loss",
        in_specs=[pl.BlockSpec((tm, f), lambda i, j: (i, 0)), pl.BlockSpec((f, tn), lambda i, j: (0, j)),
                  pl.BlockSpec((tm, tn), lambda i, j: (i, j)), pl.BlockSpec((tm, tn), lambda i, j: (i, j))],
        out_specs=[pl.BlockSpec((tm, tn), lambda i, j: (i, j)), pl.BlockSpec((8, HD), lambda i, j: (i, j))],
        out_shape=[SDS((t, d), BF16), SDS((8 * (t // tm), HD * (d // tn)), F32)],
        compiler_params=_cp("parallel", "parallel"),
    )(act, wdown, h1, target)


def _ffn_down_bwd(dyb, wdown4, gu):
    t, d = dyb.shape
    w = wdown4.shape[1]
    tm = _tile(t, 512)

    def body(a, b, gu_ref, out):
        da = _dot(a[...], b[...], NT)
        g = gu_ref[0].astype(F32)
        u = gu_ref[1].astype(F32)
        s = _sigmoid(g)
        out[0] = (da * u * (s * (1.0 + g * (1.0 - s)))).astype(BF16)
        out[1] = (da * g * s).astype(BF16)

    return pl.pallas_call(
        body, grid=(4, t // tm), name="ffn_down_bwd",
        in_specs=[pl.BlockSpec((tm, d), lambda j, i: (i, 0)), pl.BlockSpec((None, w, d), lambda j, i: (j, 0, 0)),
                  pl.BlockSpec((2, None, tm, w), lambda j, i: (0, j, i, 0))],
        out_specs=pl.BlockSpec((2, None, tm, w), lambda j, i: (0, j, i, 0)),
        out_shape=SDS((2, 4, t, w), BF16),
        compiler_params=_cp("parallel", "parallel"),
    )(dyb, wdown4, gu)


def _ffn_up_bwd_x(dgu, wgu, h1, gain, dyb, deps):
    _, t, w = dgu.shape
    d = wgu.shape[1]
    tm = _tile(t, 512)

    def body(*refs):
        a, b, h, g, dy = refs[:5]
        dh1, dgain, acc = refs[5 + len(deps):]
        i, j = pl.program_id(0), pl.program_id(1)

        @pl.when(j == 0)
        def _():
            acc[...] = jnp.zeros_like(acc)

        acc[...] += _dot(a[...], b[...], NT)

        @pl.when(j == N_DEV - 1)
        def _():
            _, vjp = jax.vjp(lambda x, gn: _rms(x, gn), h[...], g[...])
            dx, dg = vjp(acc[...])
            dh1[...] = (dx + dy[...].astype(F32)).astype(dh1.dtype)

            @pl.when(i == 0)
            def _():
                dgain[...] = jnp.zeros_like(dgain)

            dgain[...] += dg

    row = pl.BlockSpec((tm, d), lambda i, j: (i, 0))
    return pl.pallas_call(
        body, grid=(t // tm, N_DEV), name="ffn_up_bwd_x",
        in_specs=[pl.BlockSpec((None, tm, w), lambda i, j: (j, i, 0)), pl.BlockSpec((None, d, w), lambda i, j: (j, 0, 0)),
                  row, pl.BlockSpec((1, d), lambda i, j: (0, 0)), row] + [ANY_SPEC] * len(deps),
        out_specs=[row, pl.BlockSpec((1, d), lambda i, j: (0, 0))],
        out_shape=[SDS((t, d), BF16), SDS((1, d), F32)], scratch_shapes=[pltpu.VMEM((tm, d), F32)],
        compiler_params=_cp("arbitrary", "arbitrary"),
    )(dgu, wgu, h1, gain, dyb, *deps)


def _ffn_up_bwd_w(h1n, dgu):
    _, t, w = dgu.shape
    d = h1n.shape[1]
    tm = _tile(d, 512)

    def body(a, b, out):
        out[...] = _dot(a[...], b[...], TN).astype(BF16)

    return pl.pallas_call(
        body, grid=(8, d // tm), name="ffn_up_bwd_w",
        in_specs=[pl.BlockSpec((t, tm), lambda j, i: (0, i)), pl.BlockSpec((None, t, w), lambda j, i: (j, 0, 0))],
        out_specs=pl.BlockSpec((None, tm, w), lambda j, i: (j, i, 0)), out_shape=SDS((8, d, w), BF16),
        compiler_params=_cp("parallel", "parallel"),
    )(h1n, dgu)


def _fox_prep(fq, fk, sm, fb, qg, kg, h):
    qn = _rms(fq, qg)
    kn = _rms(fk, kg)
    c = _cumsum_rows(-_softplus(-(sm + fb)))
    ccol = _lane_pick(c, L_FF + h)
    crow = jnp.sum(c.T * (_iota((HD, 1), 0) == L_FF + h).astype(F32), axis=0, keepdims=True)
    return qn, kn, ccol, crow


def _softmax_times(s, v):
    e = jnp.exp(s - lax.stop_gradient(jnp.max(s, axis=1, keepdims=True)))
    return _dot(e.astype(BF16), v.astype(BF16)) * (1.0 / jnp.sum(e, axis=1, keepdims=True))


def _fox_block(q, k, v, cc, cr, off):
    bq = q.shape[0]
    assert k.shape[0] == off + bq
    s = _dot((q * (HD ** -0.5)).astype(BF16), k.astype(BF16), NT) + cc - cr
    diag = jnp.where(_iota((bq, bq), 1) <= _iota((bq, bq), 0), s[:, off:], -1e30)
    s = jnp.concatenate([s[:, :off], diag], axis=1) if off else diag
    return _softmax_times(s, v)


ONE_BUFFER = pl.Buffered(1)


def _pcol(t, cb):
    return pl.BlockSpec((t, HD), lambda h, cb=cb: (0, cb + h), pipeline_mode=ONE_BUFFER)


def _smcol(t):
    return pl.BlockSpec((t, HD), lambda h: (0, SM), pipeline_mode=ONE_BUFFER)


def _head(t):
    return pl.BlockSpec((t, HD), lambda h: (0, h), pipeline_mode=ONE_BUFFER)


def _small(n):
    return pl.BlockSpec((n, HD), lambda h: (0, 0), pipeline_mode=ONE_BUFFER)


def _fox_fwd(p, fb, qg, kg, bq):
    t = p.shape[0]

    def body(fq, fk, fv, sm, fb_r, qg_r, kg_r, o, qn_s, cc_s):
        h = pl.program_id(0)
        qn, kn, ccol, crow = _fox_prep(fq[...], fk[...], sm[...], fb_r[...], qg_r[...], kg_r[...], h)
        qn_s[...] = qn
        cc_s[...] = ccol
        knb = kn.astype(BF16)
        vb = fv[...].astype(BF16)
        for i in range(t // bq):
            rows, ext = pl.ds(i * bq, bq), (i + 1) * bq
            o[rows, :] = _fox_block(qn_s[rows, :], knb[:ext], vb[:ext], cc_s[rows, :], crow[:, :ext], i * bq).astype(o.dtype)

    return pl.pallas_call(
        body, grid=(NF,), name="fox_fwd",
        in_specs=[_pcol(t, FQ), _pcol(t, FK), _pcol(t, FV), _smcol(t), _small(1), _small(1), _small(1)],
        out_specs=_head(t), out_shape=SDS((t, NF * HD), BF16),
        scratch_shapes=[pltpu.VMEM((t, HD), F32), pltpu.VMEM((t, 1), F32)],
        compiler_params=_cp("parallel"),
    )(p, p, p, p, fb, qg, kg)


def _fox_bwd(p, fb, qg, kg, dmix, bq, deps=()):
    t = p.shape[0]

    def body(*refs):
        fq, fk, fv, sm, fb_r, qg_r, kg_r, do = refs[:8]
        dfq, dfk, dfv, dsm, dfb, dqg, dkg, qn_s, cc_s, dqn_s, dcc_s, dkn_s, dv_s, dcr_s = refs[8 + len(deps):]
        h = pl.program_id(0)
        qn, kn, ccol, crow = _fox_prep(fq[...], fk[...], sm[...], fb_r[...], qg_r[...], kg_r[...], h)
        qn_s[...] = qn
        cc_s[...] = ccol
        v = fv[...]
        dkn_s[...] = jnp.zeros_like(dkn_s)
        dv_s[...] = jnp.zeros_like(dv_s)
        dcr_s[...] = jnp.zeros_like(dcr_s)

        for i in range(t // bq):
            rows, ext = pl.ds(i * bq, bq), (i + 1) * bq
            _, vjp = jax.vjp(lambda a, b, c, d, e, off=i * bq: _fox_block(a, b, c, d, e, off),
                             qn_s[rows, :], kn[:ext], v[:ext], cc_s[rows, :], crow[:, :ext])
            dq, dk, dv, dcc, dcr = vjp(do[rows, :].astype(F32))
            dqn_s[rows, :] = dq
            dcc_s[rows, :] = dcc
            dkn_s[:ext, :] += dk
            dv_s[:ext, :] += dv
            dcr_s[:, :ext] += dcr
        _, prep_vjp = jax.vjp(lambda a, b, c, d, e, f: _fox_prep(a, b, c, d, e, f, h),
                              fq[...], fk[...], sm[...], fb_r[...], qg_r[...], kg_r[...])
        g_fq, g_fk, g_sm, g_fb, g_qg, g_kg = prep_vjp((dqn_s[...], dkn_s[...], dcc_s[...], dcr_s[...]))
        dfq[...] = g_fq.astype(dfq.dtype)
        dfk[...] = g_fk.astype(dfk.dtype)
        dfv[...] = dv_s[...].astype(dfv.dtype)

        @pl.when(h == 0)
        def _():
            for r in (dsm, dfb, dqg, dkg):
                r[...] = jnp.zeros_like(r)

        dsm[...] += g_sm
        dfb[...] += g_fb
        dqg[...] += g_qg
        dkg[...] += g_kg

    head = _head(t)
    return pl.pallas_call(
        body, grid=(NF,), name="fox_bwd",
        in_specs=[_pcol(t, FQ), _pcol(t, FK), _pcol(t, FV), _smcol(t), _small(1), _small(1), _small(1), head]
        + [ANY_SPEC] * len(deps),
        out_specs=[head, head, head, _small(t), _small(1), _small(1), _small(1)],
        out_shape=[SDS((t, NF * HD), BF16)] * 3 + [SDS((t, HD), F32)] + [SDS((1, HD), F32)] * 3,
        scratch_shapes=[pltpu.VMEM((t, HD), F32), pltpu.VMEM((t, 1), F32), pltpu.VMEM((t, HD), F32),
                        pltpu.VMEM((t, 1), F32), pltpu.VMEM((t, HD), F32), pltpu.VMEM((t, HD), F32),
                        pltpu.VMEM((1, t), F32)],
        compiler_params=_cp("arbitrary"),
    )(p, p, p, p, fb, qg, kg, dmix, *deps)


def _mem_attn(mq, mk, mv, qg, kg):
    s = _dot((_rms(mq, qg) * (HD ** -0.5)).astype(BF16), _rms(mk, kg).astype(BF16), NT)
    return _softmax_times(s, mv)


def _mem_fwd(p, mkv, qg, kg):
    t, ml = p.shape[0], mkv.shape[0]

    def body(mq, mk, mv, qg_r, kg_r, o):
        o[...] = _mem_attn(mq[...], mk[...], mv[...], qg_r[...], kg_r[...]).astype(o.dtype)

    return pl.pallas_call(
        body, grid=(NM,), name="mem_fwd",
        in_specs=[_pcol(t, MQ), pl.BlockSpec((ml, HD), lambda h: (0, h)), pl.BlockSpec((ml, HD), lambda h: (0, NM + h)),
                  _small(1), _small(1)],
        out_specs=pl.BlockSpec((t, HD), lambda h: (0, h)), out_shape=SDS((t, NM * HD), BF16),
        compiler_params=_cp("parallel"),
    )(p, mkv, mkv, qg, kg)


def _mem_bwd(p, mkv, qg, kg, dmix, deps=()):
    t, ml = p.shape[0], mkv.shape[0]

    def body(*refs):
        mq, mk, mv, qg_r, kg_r, do = refs[:6]
        dmq, dmk, dmv, dqg, dkg = refs[6 + len(deps):]
        _, vjp = jax.vjp(_mem_attn, mq[...], mk[...], mv[...], qg_r[...], kg_r[...])
        g_q, g_k, g_v, g_qg, g_kg = vjp(do[...].astype(F32))
        dmq[...] = g_q.astype(dmq.dtype)
        dmk[...] = g_k
        dmv[...] = g_v

        @pl.when(pl.program_id(0) == 0)
        def _():
            dqg[...] = jnp.zeros_like(dqg)
            dkg[...] = jnp.zeros_like(dkg)

        dqg[...] += g_qg
        dkg[...] += g_kg

    return pl.pallas_call(
        body, grid=(NM,), name="mem_bwd",
        in_specs=[_pcol(t, MQ), pl.BlockSpec((ml, HD), lambda h: (0, h)), pl.BlockSpec((ml, HD), lambda h: (0, NM + h)),
                  _small(1), _small(1), pl.BlockSpec((t, HD), lambda h: (0, NF + NG + h))] + [ANY_SPEC] * len(deps),
        out_specs=[pl.BlockSpec((t, HD), lambda h: (0, h)), pl.BlockSpec((ml, HD), lambda h: (0, h)),
                   pl.BlockSpec((ml, HD), lambda h: (0, h)), _small(1), _small(1)],
        out_shape=[SDS((t, NM * HD), BF16), SDS((ml, NM * HD), F32), SDS((ml, NM * HD), F32),
                   SDS((1, HD), F32), SDS((1, HD), F32)],
        compiler_params=_cp("arbitrary"),
    )(p, mkv, mkv, qg, kg, dmix, *deps)


def _shift_down(x, s):
    if s == 0:
        return x
    return jnp.where(_iota(x.shape, 0) >= s, pltpu.roll(x, s, 0), 0.0)


def _shift_up(x, s):
    if s == 0:
        return x
    n = x.shape[0]
    return jnp.where(_iota(x.shape, 0) < n - s, pltpu.roll(x, n - s, 0), 0.0)


@jax.custom_vjp
def _conv4(x, w0, w1, w2, w3):
    return w0 * _shift_down(x, 3) + w1 * _shift_down(x, 2) + w2 * _shift_down(x, 1) + w3 * x


def _conv4_fwd(x, w0, w1, w2, w3):
    return _conv4(x, w0, w1, w2, w3), (x, w0, w1, w2, w3)


def _conv4_bwd(res, dy):
    x, w0, w1, w2, w3 = res
    ups = [_shift_up(dy, 3 - k) for k in range(4)]
    dx = w0 * ups[0] + w1 * ups[1] + w2 * ups[2] + w3 * ups[3]
    return (dx,) + tuple(jnp.sum(up * x, axis=0, keepdims=True) for up in ups)


_conv4.defvjp(_conv4_fwd, _conv4_bwd)


HALO = 8


def _gdn_gates(sm, alog, dtb):
    lane = _iota((1, HD), 1)
    g = -jnp.exp(alog) * _softplus(sm + dtb)
    return (jnp.where((lane >= L_GA) & (lane < L_GA + NG), g,
                      jnp.where((lane >= L_GB) & (lane < L_GB + NG), _sigmoid(sm), 0.0)),)


def _gdn_prep(gq, gk, gv, gates, taps, h):
    q, k, v = [_silu(_conv4(x, *taps[4 * j:4 * j + 4]))[HALO:] for j, x in enumerate((gq, gk, gv))]
    q = q * lax.rsqrt(jnp.sum(q * q, axis=-1, keepdims=True) + NORM_EPS) * (HD ** -0.5)
    k = k * lax.rsqrt(jnp.sum(k * k, axis=-1, keepdims=True) + NORM_EPS)
    return q, k, v, _lane_pick(gates, L_GA + h), _lane_pick(gates, L_GB + h)


def _split(x, n):
    parts, rest = [], x
    for i in range(n):
        parts.append(rest.astype(BF16))
        if i + 1 < n:
            rest = rest - parts[-1].astype(F32)
    return parts


def _raw_dot(a, b, form):
    lead = a.ndim - 2
    ca, cb = {"nn": (1, 0), "nt": (1, 1), "tn": (0, 0)}[form]
    batch = ((0,), (0,)) if lead else ((), ())
    return lax.dot_general(a, b, (((ca + lead,), (cb + lead,)), batch), preferred_element_type=F32)


def _pdot_impl(a, b, form, mode):
    if mode == "1":
        return _raw_dot(a.astype(BF16), b.astype(BF16), form)
    if mode == "3":
        (ah, al), (bh, bl) = _split(a, 2), _split(b, 2)
        return _raw_dot(ah, bh, form) + (_raw_dot(al, bh, form) + _raw_dot(ah, bl, form))
    if mode == "xa":
        return sum(_raw_dot(a.astype(BF16), t, form) for t in reversed(_split(b, 3)))
    return sum(_raw_dot(t, b.astype(BF16), form) for t in reversed(_split(a, 3)))


@functools.partial(jax.custom_vjp, nondiff_argnums=(2, 3))
def _pdot(a, b, form, mode):
    return _pdot_impl(a, b, form, mode)


def _pdot_fwd(a, b, form, mode):
    return _pdot_impl(a, b, form, mode), (a, b)


def _pdot_bwd(form, mode, res, ct):
    a, b = res
    da_args, db_args = {"nn": ((ct, b, "nt"), (a, ct, "tn")), "nt": ((ct, b, "nn"), (ct, a, "tn")),
                        "tn": ((b, ct, "nt"), (a, ct, "nn"))}[form]

    def side(args, exact):
        if mode in ("1", "3"):
            return mode
        return "xa" if args[0] is exact else "xb"

    if mode == "xa":
        return jnp.zeros_like(a), _pdot_impl(*db_args, side(db_args, a))
    if mode == "xb":
        return _pdot_impl(*da_args, side(da_args, b)), jnp.zeros_like(b)
    return _pdot_impl(*da_args, mode), _pdot_impl(*db_args, mode)


_pdot.defvjp(_pdot_fwd, _pdot_bwd)

GDN_QK, GDN_INV, GDN_SCAN = "1", "1", "1"


@jax.custom_vjp
def _tri_inv(low):
    eye = (_iota((CHUNK, CHUNK), 0) == _iota((CHUNK, CHUNK), 1)).astype(F32)
    inv = eye - low
    pw = low
    for _ in range(5):
        pw = _pdot_impl(pw, pw, "nn", GDN_INV)
        inv = inv + _pdot_impl(inv, pw, "nn", GDN_INV)
    return inv


def _tri_inv_fwd(low):
    inv = _tri_inv(low)
    return inv, inv


def _tri_inv_bwd(inv, ct):
    return (-_pdot_impl(_pdot_impl(inv, ct, "tn", GDN_INV), inv, "nt", GDN_INV),)


_tri_inv.defvjp(_tri_inv_fwd, _tri_inv_bwd)


def _gdn_intra(q, k, v, g, beta):
    n = q.shape[0]
    r, c = _iota((CHUNK, CHUNK), 0), _iota((CHUNK, CHUNK), 1)
    tril, strict = r >= c, r > c
    trilf = jnp.broadcast_to(tril.astype(F32), (n, CHUNK, CHUNK))
    gcm = _pdot(trilf, jnp.broadcast_to(g, (n, CHUNK, CHUNK)), "nn", "xa")
    gcf = _pdot(trilf, jnp.broadcast_to(g, (n, CHUNK, HD)), "nn", "xa")
    lane0 = (_iota((1, 1, CHUNK), 2) == 0).astype(F32)
    gcr = _pdot(jnp.ones((n, CHUNK, CHUNK), F32), gcm * lane0, "nt", "xa")
    decay = jnp.where(tril, jnp.exp(jnp.where(tril, gcm - gcr, 0.0)), 0.0)
    egc = jnp.exp(gcf)
    kb = k * beta
    low = jnp.where(strict, _pdot(kb, k, "nt", GDN_QK) * decay, 0.0)
    inv = _tri_inv(low)
    u = _pdot(inv, v * beta, "nn", GDN_INV)
    w = _pdot(inv, kb * egc, "nn", GDN_INV)
    at = jnp.where(tril, _pdot(q, k, "nt", GDN_QK) * decay, 0.0)
    gl = jnp.sum(jnp.broadcast_to(g, (n, CHUNK, HD)), axis=1, keepdims=True)
    return u, w, q * egc, at, k * jnp.exp(gl - gcf), gl


def _gdn_step(s, u, w, qg, at, kd, gl):
    vn = u - _pdot(w, s, "nn", GDN_SCAN)
    o = _pdot(qg, s, "nn", GDN_SCAN) + _pdot(at, vn, "nn", GDN_SCAN)
    s2 = s * jnp.exp(gl) + _pdot(kd, vn, "tn", GDN_SCAN)
    return o, s2


SCAN_HEADS = 3


def _gdn_chunked_scratch(nc):
    big = pltpu.VMEM((nc, CHUNK, HD), F32)
    return [big, big, big, pltpu.VMEM((nc, CHUNK, 1), F32), pltpu.VMEM((nc, CHUNK, 1), F32)]


def _gdn_term_shapes(nc):
    return [(nc, CHUNK, HD), (nc, CHUNK, HD), (nc, CHUNK, HD), (nc, CHUNK, CHUNK), (nc, CHUNK, HD), (nc, 1, HD)]


def _per_head(shape, heads=None, one_buffer=True):
    lead = (None,) if heads is None else (heads,)
    return pl.BlockSpec(lead + tuple(shape), lambda h: (h,) + (0,) * len(shape),
                        pipeline_mode=ONE_BUFFER if one_buffer else None)


def _gdn_in_specs(t):
    cw = lambda cb: pl.BlockSpec((4, HD), lambda h, cb=cb: (0, cb + h))
    return [_pcol(t, GQ), _pcol(t, GK), _pcol(t, GV), _small(t), cw(0), cw(NG), cw(2 * NG)]


def _taps(wq, wk, wv):
    return tuple(w[k:k + 1, :] for w in (wq, wk, wv) for k in range(4))


def _prep_rows(t):
    return min(t, 256)


def _gdn_pad(srcs, pads):
    for src, pad in zip(srcs, pads):
        pad[0:HALO, :] = jnp.zeros((HALO, HD), F32)
        pad[HALO:, :] = src[...]


def _gdn_stage(pads, gates, taps, h, chunked):
    t = gates.shape[0]
    rows = _prep_rows(t)
    per = rows // CHUNK

    def tile(i, carry):
        r0 = pl.multiple_of(i * rows, rows)
        vals = _gdn_prep(*[p[pl.ds(r0, rows + HALO), :] for p in pads], gates[pl.ds(r0, rows), :], taps, h)
        for v, r in zip(vals, chunked):
            r[pl.ds(i * per, per)] = v.reshape(per, CHUNK, v.shape[-1])
        return carry

    lax.fori_loop(0, t // rows, tile, 0)


def _gdn_intra_all(chunked, intra):
    nc = chunked[0].shape[0]
    grp_n = math.gcd(nc, GROUP)

    def grp(i, carry):
        sl = pl.ds(pl.multiple_of(i * grp_n, grp_n), grp_n)
        for r, val in zip(intra, _gdn_intra(*[c[sl] for c in chunked])):
            r[sl] = val
        return carry

    lax.fori_loop(0, nc // grp_n, grp, 0)


def _gdn_fwd(pa, gates, conv):
    t = pa.shape[0]
    nc = t // CHUNK
    terms = _gdn_term_shapes(nc)

    def body(gq, gk, gv, gt, wq, wk, wv, *rest):
        h = pl.program_id(0)
        intra, chunked, pads = rest[:6], rest[6:11], rest[11:]
        _gdn_pad((gq, gk, gv), pads)
        _gdn_stage(pads, gt, _taps(wq, wk, wv), h, chunked)
        _gdn_intra_all(chunked, intra)

    return pl.pallas_call(
        body, grid=(NG,), name="gdn_fwd", in_specs=_gdn_in_specs(t),
        out_specs=[_per_head(sh, one_buffer=False) for sh in terms], out_shape=[SDS((NG,) + sh, F32) for sh in terms],
        scratch_shapes=_gdn_chunked_scratch(nc) + [pltpu.VMEM((t + HALO, HD), F32)] * 3, compiler_params=_cp("parallel"),
    )(pa, pa, pa, gates, conv, conv, conv)


def _gdn_scan(terms_in):
    nc = terms_in[0].shape[1]
    terms = _gdn_term_shapes(nc)

    def body(*refs):
        intra, o, states = refs[:6], refs[6], refs[7]

        def step(c, ss):
            rows = pl.ds(pl.multiple_of(c * CHUNK, CHUNK), CHUNK)
            loaded = [[r[hh, c] for r in intra] for hh in range(SCAN_HEADS)]
            res = [_gdn_step(ss[hh], *loaded[hh]) for hh in range(SCAN_HEADS)]
            for hh in range(SCAN_HEADS):
                states[hh, c] = ss[hh]
                o[rows, hh * HD:(hh + 1) * HD] = res[hh][0]
            return tuple(r[1] for r in res)

        lax.fori_loop(0, nc, step, tuple(jnp.zeros((HD, HD), F32) for _ in range(SCAN_HEADS)))

    return pl.pallas_call(
        body, grid=(NG // SCAN_HEADS,), name="gdn_scan", in_specs=[_per_head(sh, SCAN_HEADS) for sh in terms],
        out_specs=[pl.BlockSpec((nc * CHUNK, SCAN_HEADS * HD), lambda h: (0, h), pipeline_mode=ONE_BUFFER),
                   _per_head((nc, HD, HD), SCAN_HEADS)],
        out_shape=[SDS((nc * CHUNK, NG * HD), F32), SDS((NG, nc, HD, HD), F32)], compiler_params=_cp("parallel"),
    )(*terms_in)


def _gdn_bwd_scan(saved, do_raw):
    nc = saved[0].shape[1]
    terms = _gdn_term_shapes(nc)

    def body(*refs):
        intra, states, do, outs = refs[:6], refs[6], refs[7], refs[8:]

        def bwd(i, dss):
            c = nc - 1 - i
            rows = pl.ds(pl.multiple_of(c * CHUNK, CHUNK), CHUNK)
            loaded = [[states[hh, c]] + [r[hh, c] for r in intra] for hh in range(SCAN_HEADS)]
            cts = [do[rows, hh * HD:(hh + 1) * HD] for hh in range(SCAN_HEADS)]
            grads = [jax.vjp(_gdn_step, *loaded[hh])[1]((cts[hh], dss[hh])) for hh in range(SCAN_HEADS)]
            for hh in range(SCAN_HEADS):
                for r, gval in zip(outs, grads[hh][1:]):
                    r[hh, c] = gval
            return tuple(g[0] for g in grads)

        lax.fori_loop(0, nc, bwd, tuple(jnp.zeros((HD, HD), F32) for _ in range(SCAN_HEADS)))

    return pl.pallas_call(
        body, grid=(NG // SCAN_HEADS,), name="gdn_bwd_scan",
        in_specs=[_per_head(sh, SCAN_HEADS) for sh in terms] + [_per_head((nc, HD, HD), SCAN_HEADS)]
        + [pl.BlockSpec((nc * CHUNK, SCAN_HEADS * HD), lambda h: (0, h), pipeline_mode=ONE_BUFFER)],
        out_specs=[_per_head(sh, SCAN_HEADS) for sh in terms],
        out_shape=[SDS((NG,) + sh, F32) for sh in terms], compiler_params=_cp("parallel"),
    )(*saved, do_raw)


def _gdn_bwd(pa, gates, conv, dterms):
    t = pa.shape[0]
    nc = t // CHUNK
    terms = _gdn_term_shapes(nc)

    def body(*refs):
        gq, gk, gv, gt, wq, wk, wv = refs[:7]
        dintra = refs[7:13]
        dgq, dgk, dgv, dgt, dwq, dwk, dwv = refs[13:20]
        chunked, pads, dpads, dgt_s = refs[20:25], refs[25:28], refs[28:31], refs[31]
        h = pl.program_id(0)
        taps = _taps(wq, wk, wv)
        _gdn_pad((gq, gk, gv), pads)
        _gdn_stage(pads, gt, taps, h, chunked)
        grp_n = math.gcd(nc, GROUP)

        def grp(i, carry):
            sl = pl.ds(pl.multiple_of(i * grp_n, grp_n), grp_n)
            _, vjp = jax.vjp(_gdn_intra, *[r[sl] for r in chunked])
            for r, gval in zip(chunked, vjp(tuple(r[sl] for r in dintra))):
                r[sl] = gval
            return carry

        lax.fori_loop(0, nc // grp_n, grp, 0)

        rows = _prep_rows(t)
        per = rows // CHUNK
        for r in dpads:
            r[...] = jnp.zeros_like(r)

        def tile(i, dtaps):
            r0 = pl.multiple_of(i * rows, rows)
            win = pl.ds(r0, rows + HALO)
            _, vjp = jax.vjp(lambda *a: _gdn_prep(*a, h), *[p[win, :] for p in pads], gt[pl.ds(r0, rows), :], taps)
            grads = vjp(tuple(r[pl.ds(i * per, per)].reshape(rows, r.shape[-1]) for r in chunked))
            for r, gval in zip(dpads, grads[:3]):
                r[win, :] += gval
            dgt_s[pl.ds(r0, rows), :] = grads[3]
            return jax.tree.map(jnp.add, dtaps, grads[4])

        dtaps = lax.fori_loop(0, t // rows, tile, (jnp.zeros((1, HD), F32),) * 12)
        for r, dpad in zip((dgq, dgk, dgv), dpads):
            r[...] = dpad[HALO:, :].astype(r.dtype)
        for j, r in enumerate((dwq, dwk, dwv)):
            for k in range(4):
                r[k:k + 1, :] = dtaps[4 * j + k]

        @pl.when(h == 0)
        def _():
            dgt[...] = jnp.zeros_like(dgt)

        dgt[...] += dgt_s[...]

    head = _head(t)
    taps = pl.BlockSpec((4, HD), lambda h: (0, h))
    return pl.pallas_call(
        body, grid=(NG,), name="gdn_bwd", in_specs=_gdn_in_specs(t) + [_per_head(sh) for sh in terms],
        out_specs=[head, head, head, _small(t), taps, taps, taps],
        out_shape=[SDS((t, NG * HD), BF16)] * 3 + [SDS((t, HD), F32)] + [SDS((4, NG * HD), F32)] * 3,
        scratch_shapes=_gdn_chunked_scratch(nc) + [pltpu.VMEM((t + HALO, HD), F32)] * 6 + [pltpu.VMEM((t, HD), F32)],
        compiler_params=_cp("arbitrary"),
    )(pa, pa, pa, gates, conv, conv, conv, *dterms)


def _gdn_post(o, z, gain):
    return (jnp.concatenate(
        [_rms(o[:, h * HD:(h + 1) * HD], gain) * _silu(z[:, h * HD:(h + 1) * HD]) for h in range(NG)], axis=1),)


def _place():
    return lax.axis_index("x"), lax.axis_index("y"), lax.axis_index("c")


def _all_gather(name, shard):
    def body(x_ref, out_ref, send_sems, recv_sems, local_sem):
        x, y, c = _place()
        me, sibling = (x, y, c), (x, y, 1 - c)
        chips = [(1 - x, y), (x, 1 - y), (1 - x, 1 - y)]

        def blk(px, py, pc):
            return out_ref.at[4 * px + 2 * py + pc]

        def copy(k, block, to, src=None):
            return pltpu.make_async_remote_copy(
                src_ref=blk(*block) if src is None else src, dst_ref=blk(*block),
                send_sem=send_sems.at[k], recv_sem=recv_sems.at[k], device_id=to, device_id_type=MESH)

        mine = pltpu.make_async_copy(x_ref, blk(*me), local_sem)
        mine.start()
        first = [copy(0, me, sibling, src=x_ref)]
        first += [copy(1 + j, me, (*chip, c), src=x_ref) for j, chip in enumerate(chips)]
        for cp in first:
            cp.start()
        passed = [copy(4 + j, (*chip, c), sibling) for j, chip in enumerate(chips)]
        for j, chip in enumerate(chips):
            copy(1 + j, (*chip, c), me).wait_recv()
            passed[j].start()
        copy(0, sibling, me).wait_recv()
        for j, chip in enumerate(chips):
            copy(4 + j, (*chip, 1 - c), me).wait_recv()
        for cp in first + passed:
            cp.wait_send()
        mine.wait()

    return pl.pallas_call(
        body, name=name, out_shape=SDS((N_DEV,) + shard.shape, shard.dtype),
        in_specs=[pl.BlockSpec(memory_space=pltpu.HBM)], out_specs=pl.BlockSpec(memory_space=pltpu.HBM),
        scratch_shapes=[pltpu.SemaphoreType.DMA((7,)), pltpu.SemaphoreType.DMA((7,)), pltpu.SemaphoreType.DMA],
    )(shard)


def _scatter_exchange(name, full):
    def body(g_ref, out_ref, send_sems, recv_sems, local_sem):
        x, y, c = _place()
        me = 4 * x + 2 * y + c
        mine = pltpu.make_async_copy(g_ref.at[me], out_ref.at[me], local_sem)
        mine.start()
        sends, recvs = [], []
        for k in range(1, N_DEV):
            px = 1 - x if k & 4 else x
            py = 1 - y if k & 2 else y
            pc = 1 - c if k & 1 else c
            peer = 4 * px + 2 * py + pc
            sends.append(pltpu.make_async_remote_copy(
                src_ref=g_ref.at[peer], dst_ref=out_ref.at[me], send_sem=send_sems.at[k - 1],
                recv_sem=recv_sems.at[k - 1], device_id=(px, py, pc), device_id_type=MESH))
            recvs.append(pltpu.make_async_remote_copy(
                src_ref=g_ref.at[me], dst_ref=out_ref.at[peer], send_sem=send_sems.at[k - 1],
                recv_sem=recv_sems.at[k - 1], device_id=(px, py, pc), device_id_type=MESH))
        for cp in sends:
            cp.start()
        for cp in recvs:
            cp.wait_recv()
        for cp in sends:
            cp.wait_send()
        mine.wait()

    return pl.pallas_call(
        body, name=name, out_shape=SDS(full.shape, full.dtype),
        in_specs=[pl.BlockSpec(memory_space=pltpu.HBM)], out_specs=pl.BlockSpec(memory_space=pltpu.HBM),
        scratch_shapes=[pltpu.SemaphoreType.DMA((7,)), pltpu.SemaphoreType.DMA((7,)), pltpu.SemaphoreType.DMA],
    )(full)


def _sum_blocks(name, parts):
    _, r, c = parts.shape
    tr = 64 if r % 64 == 0 else r

    def body(x, o):
        acc = x[0].astype(F32)
        for d in range(1, N_DEV):
            acc = acc + x[d].astype(F32)
        o[...] = acc

    return pl.pallas_call(
        body, grid=(r // tr,), name=name, in_specs=[pl.BlockSpec((N_DEV, tr, c), lambda i: (0, i, 0))],
        out_specs=pl.BlockSpec((tr, c), lambda i: (i, 0)), out_shape=SDS((r, c), F32), compiler_params=_cp("parallel"),
    )(parts)


def _reduce_scatter(name, full):
    return _sum_blocks(name + "_sum", _scatter_exchange(name, full))


def _all_reduce_small(name, x, reduce):
    m_per, n = x.shape

    def body(x_ref, out_ref, send_sems, recv_sems, local_sem):
        px, py, pc = _place()
        me, sibling = (px, py, pc), (px, py, 1 - pc)
        chips = [(1 - px, py), (px, 1 - py), (1 - px, 1 - py)]
        buf = out_ref

        def rows(qx, qy, qc):
            return buf.at[pl.ds((4 * qx + 2 * qy + qc) * m_per, m_per), :]

        def copy(k, block, to, src=None):
            return pltpu.make_async_remote_copy(
                src_ref=rows(*block) if src is None else src, dst_ref=rows(*block),
                send_sem=send_sems.at[k], recv_sem=recv_sems.at[k], device_id=to, device_id_type=MESH)

        mine = pltpu.make_async_copy(x_ref, rows(*me), local_sem)
        mine.start()
        first = [copy(0, me, sibling, src=x_ref)]
        first += [copy(1 + j, me, (*chip, pc), src=x_ref) for j, chip in enumerate(chips)]
        for cp in first:
            cp.start()
        passed = [copy(4 + j, (*chip, pc), sibling) for j, chip in enumerate(chips)]
        for j, chip in enumerate(chips):
            copy(1 + j, (*chip, pc), me).wait_recv()
            passed[j].start()
        copy(0, sibling, me).wait_recv()
        for j, chip in enumerate(chips):
            copy(4 + j, (*chip, 1 - pc), me).wait_recv()
        for cp in first + passed:
            cp.wait_send()
        mine.wait()

    gathered = pl.pallas_call(
        body, name=name, out_shape=SDS((N_DEV * m_per, n), x.dtype),
        in_specs=[pl.BlockSpec(memory_space=pltpu.VMEM)], out_specs=pl.BlockSpec(memory_space=pltpu.VMEM),
        scratch_shapes=[pltpu.SemaphoreType.DMA((7,)), pltpu.SemaphoreType.DMA((7,)), pltpu.SemaphoreType.DMA],
    )(x)
    if not reduce:
        return gathered
    return _sum_blocks(name + "_sum", gathered.reshape(N_DEV, m_per, n))


HBM_SPEC = pl.BlockSpec(memory_space=pltpu.HBM)
SEM_SPEC = pl.BlockSpec(memory_space=pltpu.SEMAPHORE)
EFFECT = pltpu.SideEffectType.DATAFLOW_SIDE_EFFECTING


def _copies_start(name, bufs, n_remote, n_local, build, deps):
    nb, nd = len(bufs), len(deps)
    sem_shapes = [pltpu.SemaphoreType.DMA((n_remote,)), pltpu.SemaphoreType.DMA((n_remote,))]
    if n_local:
        sem_shapes.append(pltpu.SemaphoreType.DMA((n_local,)))
    ns = len(sem_shapes)

    def body(*refs):
        sems = refs[nb + nd:nb + nd + ns]
        remote, local = build(refs[:nb], *sems, *([None] * (3 - ns)))
        for cp in local + remote:
            cp.start()
        refs[-1][...] = jnp.zeros((8, HD), F32)

    outs = pl.pallas_call(
        body, name=name,
        out_shape=(*sem_shapes, *[pltpu.HBM(b.shape, b.dtype) for b in bufs], SDS((8, HD), F32)),
        in_specs=[HBM_SPEC] * nb + [ANY_SPEC] * nd,
        out_specs=(*[SEM_SPEC] * ns, *[HBM_SPEC] * nb, pl.BlockSpec(memory_space=pltpu.VMEM)),
        input_output_aliases={i: ns + i for i in range(nb)},
        compiler_params=pltpu.CompilerParams(has_side_effects=EFFECT),
    )(*[pltpu.with_memory_space_constraint(b, pltpu.HBM) for b in bufs], *deps)
    return list(outs[:ns]), list(outs[ns:ns + nb]), outs[-1]


def _copies_wait(name, bufs, sems, build, after):
    nb, ns = len(bufs), len(sems)

    def body(*refs):
        remote, local = build(refs[:nb], *refs[nb:nb + ns], *([None] * (3 - ns)))
        for cp in local:
            cp.wait()
        for cp in remote:
            cp.wait_send()
            cp.wait_recv()

    outs = pl.pallas_call(
        body, name=name, out_shape=tuple(pltpu.HBM(b.shape, b.dtype) for b in bufs),
        in_specs=[HBM_SPEC] * nb + [SEM_SPEC] * ns + [ANY_SPEC] * len(after), out_specs=tuple([HBM_SPEC] * nb),
        input_output_aliases={i: i for i in range(nb)},
        compiler_params=pltpu.CompilerParams(has_side_effects=EFFECT),
    )(*bufs, *sems, *after)
    return list(outs)


def _remote(src, dst, send, recv, k, to):
    return pltpu.make_async_remote_copy(src_ref=src, dst_ref=dst, send_sem=send.at[k], recv_sem=recv.at[k],
                                        device_id=to, device_id_type=MESH)


class _Gather:
    def __init__(self, name, shards, deps):
        self.name, self.n = name, len(shards)
        lands = [lax.empty((N_DEV,) + s.shape, s.dtype) for s in shards]
        self.sems1, bufs, self.token = _copies_start(
            name + "_s1", list(shards) + lands, 4 * self.n, self.n, self._stage1(range(self.n)), deps)
        self.shards, self.lands, self.sems2 = bufs[:self.n], bufs[self.n:], {}

    def _stage1(self, idxs):
        def build(refs, send, recv, loc):
            x, y, c = _place()
            me = 4 * x + 2 * y + c
            targets = [(x, y, 1 - c), (1 - x, y, c), (x, 1 - y, c), (1 - x, 1 - y, c)]
            remote, local = [], []
            for pos, i in enumerate(idxs):
                src, land = refs[pos], refs[len(idxs) + pos]
                local.append(pltpu.make_async_copy(src, land.at[me], loc.at[i]))
                remote += [_remote(src, land.at[me], send, recv, 4 * i + k, to) for k, to in enumerate(targets)]
            return remote, local
        return build

    @staticmethod
    def _stage2(refs, send, recv, loc):
        x, y, c = _place()
        remote = []
        for pos, land in enumerate(refs):
            for j, (cx, cy) in enumerate([(1 - x, y), (x, 1 - y), (1 - x, 1 - y)]):
                blk = land.at[4 * cx + 2 * cy + c]
                remote.append(_remote(blk, blk, send, recv, 3 * pos + j, (x, y, 1 - c)))
        return remote, []

    def pass_on(self, idxs, after):
        tag, m = "".join(map(str, idxs)), len(idxs)
        bufs = _copies_wait(f"{self.name}_w1_{tag}", [self.shards[i] for i in idxs] + [self.lands[i] for i in idxs],
                            self.sems1, self._stage1(idxs), after)
        self.sems2[tag], lands, token = _copies_start(f"{self.name}_s2_{tag}", bufs[m:], 3 * m, 0, self._stage2, ())
        for pos, i in enumerate(idxs):
            self.lands[i] = lands[pos]
        return [token]

    def get(self, idxs, after):
        tag = "".join(map(str, idxs))
        return _copies_wait(f"{self.name}_w2_{tag}", [self.lands[i] for i in idxs], self.sems2[tag], self._stage2, after)


def _rows_tile(r, row_bytes, target=1 << 20):
    tr = r
    while tr % 32 == 0 and tr * row_bytes > target:
        tr //= 2
    return tr


def _pair_add(name, g, got, c):
    _, r, cols = g.shape
    tr = _rows_tile(r, cols * 2)

    def body(s, a, b, o):
        o[...] = (a[...].astype(F32) + b[...].astype(F32)).astype(o.dtype)

    return pl.pallas_call(
        body, name=name, out_shape=SDS((4, r, cols), g.dtype),
        grid_spec=pltpu.PrefetchScalarGridSpec(
            num_scalar_prefetch=1, grid=(4, r // tr),
            in_specs=[pl.BlockSpec((None, tr, cols), lambda j, i, s: (2 * j + s[0], i, 0)),
                      pl.BlockSpec((None, tr, cols), lambda j, i, s: (j, i, 0))],
            out_specs=pl.BlockSpec((None, tr, cols), lambda j, i, s: (j, i, 0))),
        compiler_params=_cp("parallel", "parallel"),
    )(c.reshape(1), g, got)


def _quad_sum(name, part, got, chip, wmv=None):
    _, r, cols = part.shape
    tr = _rows_tile(r, cols * 4)
    n_out = 4 if wmv else 1

    def body(s, a, b1, b2, b3, *rest):
        g = ((a[...].astype(F32) + b1[...].astype(F32)) + b2[...].astype(F32)) + b3[...].astype(F32)
        rest[-n_out][...] = g
        if wmv:
            w, m, v = rest[:3]
            rest[-3][...], rest[-2][...], rest[-1][...] = _adamw(w[...], g, m[...], v[...])

    blk = lambda k: pl.BlockSpec((None, tr, cols), lambda i, s, k=k: (jnp.bitwise_xor(s[0], k), i, 0))
    row = pl.BlockSpec((tr, cols), lambda i, s: (i, 0))
    outs = pl.pallas_call(
        body, name=name, out_shape=[SDS((r, cols), F32)] * n_out,
        grid_spec=pltpu.PrefetchScalarGridSpec(
            num_scalar_prefetch=1, grid=(r // tr,), in_specs=[blk(0), blk(1), blk(2), blk(3)] + [row] * (n_out - 1),
            out_specs=[row] * n_out),
        compiler_params=_cp("parallel"),
    )(chip.reshape(1), part, got, got, got, *(wmv or ()))
    return tuple(outs) if wmv else outs[0]


class _Scatter:
    def __init__(self, name, grads, deps):
        self.name, self.n = name, len(grads)
        got = [lax.empty((4,) + g.shape[1:], g.dtype) for g in grads]
        self.sems, bufs, self.token = _copies_start(name + "_s1", list(grads) + got, 4 * self.n, 0, self._stage1, deps)
        self.grads, self.got = bufs[:self.n], bufs[self.n:]

    def _stage1(self, refs, send, recv, loc):
        x, y, c = _place()
        remote = []
        for i in range(self.n):
            remote += [_remote(refs[i].at[2 * j + 1 - c], refs[self.n + i].at[j], send, recv, 4 * i + j, (x, y, 1 - c))
                       for j in range(4)]
        return remote, []

    def _stage2(self, refs, send, recv, loc):
        x, y, c = _place()
        remote = []
        for i in range(self.n):
            for k in (1, 2, 3):
                tx = 1 - x if k & 2 else x
                ty = 1 - y if k & 1 else y
                remote.append(_remote(refs[i].at[2 * tx + ty], refs[self.n + i].at[2 * x + y], send, recv,
                                      3 * i + k - 1, (tx, ty, c)))
        return remote, []

    def mid(self, after):
        bufs = _copies_wait(self.name + "_w1", self.grads + self.got, self.sems, self._stage1, after)
        c = lax.axis_index("c").astype(jnp.int32)
        parts = [_pair_add(f"{self.name}_add{i}", bufs[i], bufs[self.n + i], c) for i in range(self.n)]
        got = [lax.empty(p.shape, p.dtype) for p in parts]
        self.sems, bufs, self.token = _copies_start(self.name + "_s2", parts + got, 3 * self.n, 0, self._stage2, ())
        self.parts, self.got = bufs[:self.n], bufs[self.n:]

    def end(self, after, wmv=None):
        bufs = _copies_wait(self.name + "_w2", self.parts + self.got, self.sems, self._stage2, after)
        chip = (2 * lax.axis_index("x") + lax.axis_index("y")).astype(jnp.int32)
        wmv = wmv or [None] * self.n
        return [_quad_sum(f"{self.name}_sum{i}", bufs[i], bufs[self.n + i], chip, wmv[i]) for i in range(self.n)]


def _adamw(w, g, m, v):
    m = ADAM_B1 * m + (1.0 - ADAM_B1) * g
    v = ADAM_B2 * v + (1.0 - ADAM_B2) * (g * g)
    m_hat = m / (1.0 - ADAM_B1 ** ADAM_STEP)
    v_hat = v / (1.0 - ADAM_B2 ** ADAM_STEP)
    return -ADAM_LR * (m_hat / (jnp.sqrt(v_hat) + ADAM_EPS) + ADAM_WD * w), m, v


def _adamw_call(name, w, g, m, v):
    r, c = w.shape
    tm = 64 if r % 64 == 0 else r
    return _rowwise(name, _adamw, [w, g, m, v], [], [(c, F32)] * 3, tm)


_IN_COLS = 5906


def _perm_in(w):
    pad = jnp.zeros((w.shape[0], 2 * HALF - _IN_COLS), w.dtype)
    return (jnp.concatenate([w[:, 2310:4614], w[:, 4614:5382]], axis=1),
            jnp.concatenate([w[:, :2304], w[:, 5394:5906], w[:, 2304:2310], w[:, 5382:5394], pad], axis=1))


def _unperm_in(ga, gb):
    return jnp.concatenate([gb[:, :2304], gb[:, 2816:2822], ga[:, :2304], ga[:, 2304:3072], gb[:, 2822:2834],
                            gb[:, 2304:2816]], axis=1)


def _lanes(v, at):
    return jnp.pad(v, ((0, 0), (at, HD - at - v.shape[1])))


_PACK = ("norm_mix", "mem_norm", "norm_ffn", "gdn_conv", "fox_q_norm", "fox_k_norm", "gdn_out_norm", "mem_q_norm",
         "mem_k_norm", "fox_f_bias", "gdn_a_log", "gdn_dt_bias", "loss")


def _pack(vals):
    parts = [vals[n].reshape(-1, HD) for n in _PACK]
    used = sum(p.shape[0] for p in parts)
    buf = jnp.concatenate(parts + [jnp.zeros((-used % 8, HD), F32)], axis=0)
    return buf, [(n, p.shape[0]) for n, p in zip(_PACK, parts)]


def _unpack(buf, layout):
    out, at = {}, 0
    for n, rows in layout:
        out[n] = buf[at:at + rows]
        at += rows
    return out


def kernel(x, mem, norm_mix, w_in, fox_f_bias, fox_q_norm, fox_k_norm, gdn_conv, gdn_a_log, gdn_dt_bias, gdn_out_norm, mem_norm, w_mem_kv, mem_q_norm, mem_k_norm, w_out, norm_ffn, w_gate_up, w_down, loss_target, m_norm_mix, m_w_in, m_fox_f_bias, m_fox_q_norm, m_fox_k_norm, m_gdn_conv, m_gdn_a_log, m_gdn_dt_bias, m_gdn_out_norm, m_mem_norm, m_w_mem_kv, m_mem_q_norm, m_mem_k_norm, m_w_out, m_norm_ffn, m_w_gate_up, m_w_down, v_norm_mix, v_w_in, v_fox_f_bias, v_fox_q_norm, v_fox_k_norm, v_gdn_conv, v_gdn_a_log, v_gdn_dt_bias, v_gdn_out_norm, v_mem_norm, v_w_mem_kv, v_mem_q_norm, v_mem_k_norm, v_w_out, v_norm_ffn, v_w_gate_up, v_w_down):
    args = dict(locals())
    d = x.shape[2]
    me = 4 * lax.axis_index("x") + 2 * lax.axis_index("y") + lax.axis_index("c")

    cshard = gdn_conv[0].shape[1]
    conv_pad = jnp.pad(gdn_conv[0], ((0, 4), (0, 3 * HD - cshard)))
    conv_all = _all_reduce_small("ag_conv", conv_pad, False).reshape(N_DEV, 8, 3 * HD)[:, :4, :cshard]
    conv_all = conv_all.transpose(1, 0, 2).reshape(4, N_DEV * cshard)
    w_in_a, w_in_b = _perm_in(w_in[0])
    comm = _StepComm({"in_b": [w_in_b], "in_a": [w_in_a], "kv_out": [w_mem_kv[0], w_out[0]], "gate_up": [w_gate_up[0]],
                      "down": [w_down[0]]}, [conv_all])

    grad_x, loss_local, small_grads = _local_step(
        x[0], mem[0], loss_target[0], norm_mix, fox_f_bias, fox_q_norm, fox_k_norm, gdn_a_log, gdn_dt_bias,
        gdn_out_norm, mem_norm, mem_q_norm, mem_k_norm, norm_ffn, conv_all, comm)

    wmv = lambda n: (args[n][0], args["m_" + n][0], args["v_" + n][0])
    red = comm.finish([grad_x], {"ffn": [wmv("w_down"), wmv("w_gate_up")], "a": [None, wmv("w_out"), wmv("w_mem_kv")],
                                 "b": [None]})
    updated = {"w_down": red["ffn"][0], "w_gate_up": red["ffn"][1], "w_out": red["a"][1], "w_mem_kv": red["a"][2]}
    grads = {n: r[0] for n, r in updated.items()}
    grads["w_in"] = _unperm_in(red["a"][0], red["b"][0])
    small_grads["loss"] = jnp.broadcast_to(loss_local, (1, HD))
    packed, layout = _pack(small_grads)
    small = _unpack(_all_reduce_small("ar_small", packed, True), layout)
    loss = small["loss"][0, 0]
    six = {"fox_f_bias": L_FF, "gdn_a_log": L_GA, "gdn_dt_bias": L_GA}
    for n, rows_n in layout[:-1]:
        gsm = small[n]
        if n == "gdn_conv":
            gsm = lax.dynamic_slice(gsm.reshape(4, N_DEV * cshard), (0, me * cshard), (4, cshard))[None]
        elif n in six:
            gsm = gsm[:, six[n]:six[n] + 6]
        else:
            gsm = gsm.reshape(1, rows_n * HD)
        grads[n] = gsm

    names = ['norm_mix', 'w_in', 'fox_f_bias', 'fox_q_norm', 'fox_k_norm', 'gdn_conv', 'gdn_a_log', 'gdn_dt_bias',
             'gdn_out_norm', 'mem_norm', 'w_mem_kv', 'mem_q_norm', 'mem_k_norm', 'w_out', 'norm_ffn', 'w_gate_up', 'w_down']
    big = ("w_in", "w_mem_kv", "w_out", "w_gate_up", "w_down")
    delta, new_m, new_v = {}, {}, {}
    for n in big:
        res = updated[n][1:] if n in updated else _adamw_call("adamw_" + n, args[n][0], grads[n], *wmv(n)[1:])
        delta[n], new_m[n], new_v[n] = [a[None] for a in res]
        grads[n] = grads[n][None]

    def flat(a):
        a = a.reshape(1, -1)
        return jnp.pad(a, ((0, 0), (0, -a.shape[1] % HD))).reshape(-1, HD)

    smalls = [n for n in names if n not in big]
    pk = lambda pre: jnp.concatenate([flat(grads[n] if pre == "g" else args[pre + n]) for n in smalls], axis=0)
    cat = [pk(""), pk("g"), pk("m_"), pk("v_")]
    padr = -cat[0].shape[0] % 8
    cat = [jnp.pad(a, ((0, padr), (0, 0))) for a in cat]
    res = _adamw_call("adamw_small", *cat)
    at = 0
    for n in smalls:
        shape = args[n].shape
        size = math.prod(shape)
        nrow = -(-size // HD)
        for dst, src in zip((delta, new_m, new_v), res):
            dst[n] = src[at:at + nrow].reshape(-1)[:size].reshape(shape)
        at += nrow

    return (loss, grad_x[None], *[grads[n] for n in names], *[delta[n] for n in names],
            *[new_m[n] for n in names], *[new_v[n] for n in names])


class _StepComm:
    def __init__(self, shard_groups, after):
        self.groups, shards = {}, []
        for key, ws in shard_groups.items():
            self.groups[key] = list(range(len(shards), len(shards) + len(ws)))
            shards += [w.astype(BF16) for w in ws]
        self.gather = _Gather("ag", shards, after)
        self.passed, self.scatters = set(), {}

    def start_deps(self):
        return [self.gather.token]

    def pass_on(self, key, after):
        self.passed.add(key)
        return self.gather.pass_on(self.groups[key], after)

    def weights(self, key, after):
        if key not in self.passed:
            after = self.pass_on(key, after)
        return self.gather.get(self.groups[key], after)

    def send(self, tag, grads):
        blocks = [g if g.ndim == 3 else g.reshape(N_DEV, g.shape[0] // N_DEV, g.shape[1]) for g in grads]
        self.scatters[tag] = _Scatter("rs_" + tag, blocks, ())
        return [self.scatters[tag].token]

    def mid(self, tag, after):
        self.scatters[tag].mid(after)
        return [self.scatters[tag].token]

    def finish(self, after, wmv):
        return {tag: sc.end(after, wmv[tag]) for tag, sc in self.scatters.items()}


def _local_step(xs, ms, tgt, norm_mix, fox_f_bias, fox_q_norm, fox_k_norm, gdn_a_log, gdn_dt_bias, gdn_out_norm,
                mem_norm, mem_q_norm, mem_k_norm, norm_ffn, conv_all, comm):
    t, d = xs.shape
    bq = min(t, 256)
    fb, alog, dtb = _lanes(fox_f_bias, L_FF), _lanes(gdn_a_log, L_GA), _lanes(gdn_dt_bias, L_GA)
    flat = lambda w: w.reshape(-1, w.shape[-1])

    rms1 = lambda a, g: (_rms(a, g),)
    (u,) = _rowwise("norm_mix", rms1, [xs], [norm_mix], [(d, BF16)], min(t, 256), deps=comm.start_deps())
    w_in_b = flat(comm.weights("in_b", [u])[0])
    pb = _matmul("proj_in_b", u, w_in_b, NN, F32, 1024, 768)
    o_fox = _fox_fwd(pb, fb, fox_q_norm, fox_k_norm, bq)
    w_in_a = flat(comm.weights("in_a", [o_fox])[0])
    pa = _matmul("proj_in_a", u, w_in_a, NN, F32, 1024, 768)
    smrow = (pb, HD, SM)
    (gates,) = _rowwise("gdn_gates", _gdn_gates, [smrow], [alog, dtb], [(HD, F32)], min(t, 256))
    gdn_terms = _gdn_fwd(pa, gates, conv_all)
    o_gdn_raw, gdn_states = _gdn_scan(gdn_terms)
    gdn_saved = list(gdn_terms) + [gdn_states]
    zrow = (pa, NG * HD, GZ * HD // (NG * HD))
    (o_gdn,) = _rowwise("gdn_post", _gdn_post, [o_gdn_raw, zrow], [gdn_out_norm], [(NG * HD, BF16)], min(t, 256))
    w_kv_all, w_out_all = [flat(w) for w in comm.weights("kv_out", [o_gdn])]
    (mem_n,) = _rowwise("norm_mem", rms1, [ms], [mem_norm], [(d, BF16)], ms.shape[0])
    mkv = _matmul("proj_mem", mem_n, w_kv_all, NN, F32, 256, 512)
    o_mem = _mem_fwd(pb, mkv, mem_q_norm, mem_k_norm)
    deps = comm.pass_on("gate_up", [o_mem])
    mix = jnp.concatenate([o_fox, o_gdn, o_mem], axis=1)
    h1, h1n = _proj_out_norm(mix, w_out_all, xs, norm_ffn, deps)
    (wgu,) = comm.weights("gate_up", [h1n])
    ffw = wgu.shape[2]
    gu, act = _ffn_up(h1n, wgu.reshape(2, 4, d, ffw))
    w_down_all = flat(comm.weights("down", [act])[0])
    dyb, lsum = _ffn_down_loss(act, w_down_all, h1, tgt)
    loss_local = (0.5 / d) * jnp.sum(lsum[::8, ::HD])

    dgu = _ffn_down_bwd(dyb, w_down_all.reshape(4, ffw, d), gu).reshape(8, t, ffw)
    g_w_down = _matmul("grad_w_down", act, dyb, TN, BF16, 512, 2048)
    g_w_gu = _ffn_up_bwd_w(h1n, dgu)
    deps = comm.send("ffn", [g_w_down, g_w_gu])
    rms2 = lambda a, g: (_rms(a, g), a)
    dh1b, g_norm_ffn = _ffn_up_bwd_x(dgu, wgu, h1, norm_ffn, dyb, deps)

    dmix = _matmul("proj_out_bwd_x", dh1b, w_out_all, NT, BF16, 1024, 1024)
    g_w_out = _matmul("grad_w_out", mix, dh1b, TN, BF16, 1024, 2048)
    deps = comm.mid("ffn", [dmix, g_w_out])
    dmq, dmk, dmv, g_mqn, g_mkn = _mem_bwd(pb, mkv, mem_q_norm, mem_k_norm, dmix, deps=deps)
    dmkv = jnp.concatenate([dmk, dmv], axis=1).astype(BF16)
    g_w_kv = _matmul("grad_w_kv", mem_n, dmkv, TN, BF16, 512, 512)
    do_raw, dgz, g_gon = _rowwise_vjp("gdn_post_bwd", _gdn_post, [o_gdn_raw, zrow], [gdn_out_norm],
                                      [(dmix, NG * HD, 1)], [F32, BF16], min(t, 256), deps=deps)
    dterms = _gdn_bwd_scan(gdn_saved, do_raw)
    dgq, dgk, dgv, dgates, dwq, dwk, dwv = _gdn_bwd(pa, gates, conv_all, dterms)
    dsm_gdn, g_alog, g_dtb = _rowwise_vjp("gdn_gates_bwd", _gdn_gates, [smrow], [alog, dtb], [dgates], [F32], min(t, 256))
    dp_a = jnp.concatenate([dgq, dgk, dgv, dgz], axis=1)
    g_w_in_a = _matmul("grad_w_in_a", u, dp_a, TN, BF16, 512, 3072)
    deps = comm.send("a", [g_w_in_a, g_w_out, g_w_kv])
    du_a = _matmul("proj_in_bwd_a", dp_a, w_in_a, NT, F32, 1024, 1024, deps=deps)
    deps = comm.mid("a", [du_a])
    dfq, dfk, dfv, dsm_fox, g_fb, g_fqn, g_fkn = _fox_bwd(pb, fb, fox_q_norm, fox_k_norm, dmix, bq, deps=deps)
    dp_b = jnp.concatenate([dfq, dfk, dfv, dmq, (dsm_fox + dsm_gdn).astype(BF16), jnp.zeros((t, HD), BF16)], axis=1)
    g_w_in_b = _matmul("grad_w_in_b", u, dp_b, TN, BF16, 512, 3072)
    deps = comm.send("b", [g_w_in_b])
    dmem_n = _matmul("proj_mem_bwd_x", dmkv, w_kv_all, NT, F32, 256, 512, deps=deps)
    g_mem_norm = _rowwise_vjp("norm_mem_bwd", rms1, [ms], [mem_norm], [dmem_n], [], ms.shape[0])[0]
    deps = comm.mid("b", [g_mem_norm])
    du = _matmul("proj_in_bwd_b", dp_b, w_in_b, NT, F32, 1024, 1024, residual=du_a, deps=deps)
    grad_x, g_norm_mix = _rowwise_vjp("norm_mix_bwd", rms2, [xs], [norm_mix], [du, dh1b], [F32], min(t, 256))

    small_grads = {
        "norm_mix": g_norm_mix, "mem_norm": g_mem_norm, "norm_ffn": g_norm_ffn,
        "gdn_conv": jnp.concatenate([dwq, dwk, dwv], axis=1),
        "fox_q_norm": g_fqn, "fox_k_norm": g_fkn, "gdn_out_norm": g_gon, "mem_q_norm": g_mqn, "mem_k_norm": g_mkn,
        "fox_f_bias": g_fb, "gdn_a_log": g_alog, "gdn_dt_bias": g_dtb}
    return grad_x, loss_local, small_grads
```

```python
import functools
import math

import jax
import jax.numpy as jnp
from jax import lax
from jax.experimental import pallas as pl
from jax.experimental.pallas import tpu as pltpu

F32 = jnp.float32
BF16 = jnp.bfloat16
HI = lax.Precision.HIGHEST
SDS = jax.ShapeDtypeStruct

N_DEV = 8
HD = 128
NF, NG, NM = 6, 6, 4
CHUNK = 64
GROUP = 16
NORM_EPS = 1e-6
GQ, GK, GV, GZ = 0, 6, 12, 18
FQ, FK, FV, MQ, SM = 0, 6, 12, 18, 22
HALF = 24 * HD
L_FF, L_GA, L_GB = 0, 6, 12
VMEM_LIMIT = 56 * 1024 * 1024

ADAM_LR, ADAM_B1, ADAM_B2, ADAM_EPS, ADAM_WD, ADAM_STEP = 0.001, 0.9, 0.999, 1e-08, 0.01, 10

NN = (((1,), (0,)), ((), ()))
NT = (((1,), (1,)), ((), ()))
TN = (((0,), (0,)), ((), ()))
MESH = pl.DeviceIdType.MESH


def _cp(*sem):
    return pltpu.CompilerParams(dimension_semantics=tuple(sem) if sem else None, vmem_limit_bytes=VMEM_LIMIT)


def _dot(a, b, dims=NN):
    return lax.dot_general(a, b, dims, preferred_element_type=F32)


def _bdot(a, b):
    return _dot(a.astype(BF16), b.astype(BF16))


def _iota(shape, axis):
    return lax.broadcasted_iota(jnp.int32, shape, axis)


def _rms(x, gain):
    return x * lax.rsqrt(jnp.mean(x * x, axis=-1, keepdims=True) + NORM_EPS) * gain


def _sigmoid(x):
    return 0.5 * jnp.tanh(0.5 * x) + 0.5


def _silu(x):
    return x * _sigmoid(x)


def _softplus(x):
    return jnp.maximum(x, 0.0) + jnp.log(1.0 + jnp.exp(-jnp.abs(x)))


def _lane_pick(x, lane):
    oh = (_iota((1, x.shape[-1]), 1) == lane).astype(F32)
    return jnp.sum(x * oh, axis=-1, keepdims=True)


def _cumsum_rows(x):
    tril = (_iota((HD, HD), 0) >= _iota((HD, HD), 1)).astype(F32)
    carry = jnp.zeros((1, x.shape[1]), F32)
    outs = []
    for b in range(x.shape[0] // HD):
        blk = x[b * HD:(b + 1) * HD]
        outs.append(jnp.dot(tril, blk, precision=HI, preferred_element_type=F32) + carry)
        carry = carry + jnp.sum(blk, axis=0, keepdims=True)
    return jnp.concatenate(outs, axis=0)


def _row_spec(r, tm):
    if isinstance(r, tuple):
        arr, width, cb = r
        return arr, pl.BlockSpec((tm, width), lambda i, cb=cb: (i, cb))
    return r, pl.BlockSpec((tm, r.shape[1]), lambda i: (i, 0))


ANY_SPEC = pl.BlockSpec(memory_space=pl.ANY)


def _rowwise(name, fn, rows, consts, outs, tm, deps=()):
    arrs, specs = zip(*[_row_spec(r, tm) for r in rows])
    n_rows = arrs[0].shape[0]
    nr, nc, nd = len(rows), len(consts), len(deps)

    def body(*refs):
        res = fn(*[r[...] for r in refs[:nr + nc]])
        for o, v in zip(refs[nr + nc + nd:], res):
            o[...] = v.astype(o.dtype)

    return pl.pallas_call(
        body, grid=(n_rows // tm,), name=name,
        in_specs=list(specs) + [pl.BlockSpec(c.shape, lambda i: (0, 0)) for c in consts] + [ANY_SPEC] * nd,
        out_specs=[pl.BlockSpec((tm, w), lambda i: (i, 0)) for w, _ in outs],
        out_shape=[SDS((n_rows, w), dt) for w, dt in outs],
        compiler_params=_cp("parallel"),
    )(*arrs, *consts, *deps)


def _rowwise_vjp(name, fn, rows, consts, cts, grad_dtypes, tm, deps=()):
    arrs, specs = zip(*[_row_spec(r, tm) for r in rows])
    ct_arrs, ct_specs = zip(*[_row_spec(r, tm) for r in cts])
    n_rows = arrs[0].shape[0]
    nr, nc, nct, nd = len(rows), len(consts), len(cts), len(deps)
    plan = [(j, dt) for j, dts in enumerate(grad_dtypes) for dt in (dts if isinstance(dts, tuple) else (dts,))]
    ng = len(plan)
    widths = [specs[j].block_shape[1] for j, _ in plan]
    grad_dtypes = [dt for _, dt in plan]

    def body(*refs):
        vals = [r[...].astype(F32) for r in refs[:nr + nc]]
        ctv = tuple(r[...].astype(F32) for r in refs[nr + nc:nr + nc + nct])
        _, vjp = jax.vjp(fn, *vals)
        grads = vjp(ctv)
        outs = refs[nr + nc + nct + nd:]
        for o, (j, _) in zip(outs[:ng], plan):
            o[...] = grads[j].astype(o.dtype)

        @pl.when(pl.program_id(0) == 0)
        def _():
            for o in outs[ng:]:
                o[...] = jnp.zeros_like(o)

        for o, g in zip(outs[ng:], grads[nr:]):
            o[...] += g

    return pl.pallas_call(
        body, grid=(n_rows // tm,), name=name,
        in_specs=list(specs) + [pl.BlockSpec(c.shape, lambda i: (0, 0)) for c in consts] + list(ct_specs)
        + [ANY_SPEC] * nd,
        out_specs=[pl.BlockSpec((tm, w), lambda i: (i, 0)) for w in widths]
        + [pl.BlockSpec(c.shape, lambda i: (0, 0)) for c in consts],
        out_shape=[SDS((n_rows, w), dt) for w, dt in zip(widths, grad_dtypes)] + [SDS(c.shape, F32) for c in consts],
        compiler_params=_cp("arbitrary"),
    )(*arrs, *consts, *ct_arrs, *deps)


def _tile(n, pref):
    t = min(n, pref)
    while n % t or (t % HD and t != n):
        t -= 1
    return t


def _matmul(name, a, b, dims, out_dtype, tm, tn, residual=None, deps=()):
    ta, tb = dims == TN, dims == NT
    m = a.shape[1] if ta else a.shape[0]
    k = a.shape[0] if ta else a.shape[1]
    n = b.shape[0] if tb else b.shape[1]
    tm, tn = _tile(m, tm), _tile(n, tn)

    def body(*refs):
        acc = _dot(refs[0][...], refs[1][...], dims)
        if residual is not None:
            acc = acc + refs[2][...]
        refs[-1][...] = acc.astype(out_dtype)

    in_specs = [pl.BlockSpec((k, tm), lambda i, j: (0, i)) if ta else pl.BlockSpec((tm, k), lambda i, j: (i, 0)),
                pl.BlockSpec((tn, k), lambda i, j: (j, 0)) if tb else pl.BlockSpec((k, tn), lambda i, j: (0, j))]
    ops = [a, b]
    if residual is not None:
        in_specs.append(pl.BlockSpec((tm, tn), lambda i, j: (i, j)))
        ops.append(residual)
    in_specs += [ANY_SPEC] * len(deps)
    ops += list(deps)
    return pl.pallas_call(
        body, grid=(m // tm, n // tn), name=name, in_specs=in_specs,
        out_specs=pl.BlockSpec((tm, tn), lambda i, j: (i, j)), out_shape=SDS((m, n), out_dtype),
        compiler_params=_cp("parallel", "parallel"),
    )(*ops)


def _proj_out_norm(mix, w_out, xs, gain, deps):
    t, k = mix.shape
    d = w_out.shape[1]
    tm = _tile(t, 512)

    def body(*refs):
        a, b, x, g = refs[:4]
        h1, h1n = refs[4 + len(deps):]
        acc = _dot(a[...], b[...]) + x[...]
        h1[...] = acc
        h1n[...] = _rms(acc, g[...]).astype(BF16)

    return pl.pallas_call(
        body, grid=(t // tm,), name="proj_out",
        in_specs=[pl.BlockSpec((tm, k), lambda i: (i, 0)), pl.BlockSpec((k, d), lambda i: (0, 0)),
                  pl.BlockSpec((tm, d), lambda i: (i, 0)), pl.BlockSpec((1, d), lambda i: (0, 0))] + [ANY_SPEC] * len(deps),
        out_specs=[pl.BlockSpec((tm, d), lambda i: (i, 0))] * 2, out_shape=[SDS((t, d), F32), SDS((t, d), BF16)],
        compiler_params=_cp("parallel"),
    )(mix, w_out, xs, gain, *deps)


def _ffn_up(h1n, wgu):
    t, d = h1n.shape
    w = wgu.shape[3]
    tm = _tile(t, 512)

    def body(a, b, gu, act):
        x = a[...]
        g = _dot(x, b[0])
        u = _dot(x, b[1])
        gu[0] = g.astype(BF16)
        gu[1] = u.astype(BF16)
        act[...] = (_silu(g) * u).astype(BF16)

    return pl.pallas_call(
        body, grid=(4, t // tm), name="ffn_up",
        in_specs=[pl.BlockSpec((tm, d), lambda j, i: (i, 0)), pl.BlockSpec((2, None, d, w), lambda j, i: (0, j, 0, 0))],
        out_specs=[pl.BlockSpec((2, None, tm, w), lambda j, i: (0, j, i, 0)), pl.BlockSpec((tm, w), lambda j, i: (i, j))],
        out_shape=[SDS((2, 4, t, w), BF16), SDS((t, 4 * w), BF16)],
        compiler_params=_cp("parallel", "parallel"),
    )(h1n, wgu)


def _ffn_down_loss(act, wdown, h1, target):
    t, f = act.shape
    d = wdown.shape[1]
    tm, tn = _tile(t, 1024), _tile(d, 512)

    def body(a, b, h, tg, dyb, ls):
        e = _dot(a[...], b[...]) + h[...] - tg[...]
        dyb[...] = (e * (1.0 / d)).astype(BF16)
        ls[...] = jnp.broadcast_to(jnp.sum(e * e), (8, HD))

    return pl.pallas_call(
        body, grid=(t // tm, d // tn), name="ffn_down_loss",
        in_specs=[pl.BlockSpec((tm, f), lambda i, j: (i, 0)), pl.BlockSpec((f, tn), lambda i, j: (0, j)),
                  pl.BlockSpec((tm, tn), lambda i, j: (i, j)), pl.BlockSpec((tm, tn), lambda i, j: (i, j))],
        out_specs=[pl.BlockSpec((tm, tn), lambda i, j: (i, j)), pl.BlockSpec((8, HD), lambda i, j: (i, j))],
        out_shape=[SDS((t, d), BF16), SDS((8 * (t // tm), HD * (d // tn)), F32)],
        compiler_params=_cp("parallel", "parallel"),
    )(act, wdown, h1, target)


def _ffn_down_bwd(dyb, wdown4, gu):
    t, d = dyb.shape
    w = wdown4.shape[1]
    tm = _tile(t, 512)

    def body(a, b, gu_ref, out):
        da = _dot(a[...], b[...], NT)
        g = gu_ref[0].astype(F32)
        u = gu_ref[1].astype(F32)
        s = _sigmoid(g)
        out[0] = (da * u * (s * (1.0 + g * (1.0 - s)))).astype(BF16)
        out[1] = (da * g * s).astype(BF16)

    return pl.pallas_call(
        body, grid=(4, t // tm), name="ffn_down_bwd",
        in_specs=[pl.BlockSpec((tm, d), lambda j, i: (i, 0)), pl.BlockSpec((None, w, d), lambda j, i: (j, 0, 0)),
                  pl.BlockSpec((2, None, tm, w), lambda j, i: (0, j, i, 0))],
        out_specs=pl.BlockSpec((2, None, tm, w), lambda j, i: (0, j, i, 0)),
        out_shape=SDS((2, 4, t, w), BF16),
        compiler_params=_cp("parallel", "parallel"),
    )(dyb, wdown4, gu)


def _ffn_up_bwd_x(dgu, wgu, h1, gain, dyb, deps):
    _, t, w = dgu.shape
    d = wgu.shape[1]
    tm = _tile(t, 512)

    def body(*refs):
        a, b, h, g, dy = refs[:5]
        dh1, dgain, acc = refs[5 + len(deps):]
        i, j = pl.program_id(0), pl.program_id(1)

        @pl.when(j == 0)
        def _():
            acc[...] = jnp.zeros_like(acc)

        acc[...] += _dot(a[...], b[...], NT)

        @pl.when(j == N_DEV - 1)
        def _():
            _, vjp = jax.vjp(lambda x, gn: _rms(x, gn), h[...], g[...])
            dx, dg = vjp(acc[...])
            dh1[...] = (dx + dy[...].astype(F32)).astype(dh1.dtype)

            @pl.when(i == 0)
            def _():
                dgain[...] = jnp.zeros_like(dgain)

            dgain[...] += dg

    row = pl.BlockSpec((tm, d), lambda i, j: (i, 0))
    return pl.pallas_call(
        body, grid=(t // tm, N_DEV), name="ffn_up_bwd_x",
        in_specs=[pl.BlockSpec((None, tm, w), lambda i, j: (j, i, 0)), pl.BlockSpec((None, d, w), lambda i, j: (j, 0, 0)),
                  row, pl.BlockSpec((1, d), lambda i, j: (0, 0)), row] + [ANY_SPEC] * len(deps),
        out_specs=[row, pl.BlockSpec((1, d), lambda i, j: (0, 0))],
        out_shape=[SDS((t, d), BF16), SDS((1, d), F32)], scratch_shapes=[pltpu.VMEM((tm, d), F32)],
        compiler_params=_cp("arbitrary", "arbitrary"),
    )(dgu, wgu, h1, gain, dyb, *deps)


def _ffn_up_bwd_w(h1n, dgu):
    _, t, w = dgu.shape
    d = h1n.shape[1]
    tm = _tile(d, 512)

    def body(a, b, out):
        out[...] = _dot(a[...], b[...], TN).astype(BF16)

    return pl.pallas_call(
        body, grid=(8, d // tm), name="ffn_up_bwd_w",
        in_specs=[pl.BlockSpec((t, tm), lambda j, i: (0, i)), pl.BlockSpec((None, t, w), lambda j, i: (j, 0, 0))],
        out_specs=pl.BlockSpec((None, tm, w), lambda j, i: (j, i, 0)), out_shape=SDS((8, d, w), BF16),
        compiler_params=_cp("parallel", "parallel"),
    )(h1n, dgu)


def _fox_prep(fq, fk, sm, fb, qg, kg, h):
    qn = _rms(fq, qg)
    kn = _rms(fk, kg)
    c = _cumsum_rows(-_softplus(-(sm + fb)))
    ccol = _lane_pick(c, L_FF + h)
    crow = jnp.sum(c.T * (_iota((HD, 1), 0) == L_FF + h).astype(F32), axis=0, keepdims=True)
    return qn, kn, ccol, crow


def _softmax_times(s, v):
    e = jnp.exp(s - lax.stop_gradient(jnp.max(s, axis=1, keepdims=True)))
    return _dot(e.astype(BF16), v.astype(BF16)) * (1.0 / jnp.sum(e, axis=1, keepdims=True))


def _fox_block(q, k, v, cc, cr, off):
    bq = q.shape[0]
    assert k.shape[0] == off + bq
    s = _dot((q * (HD ** -0.5)).astype(BF16), k.astype(BF16), NT) + cc - cr
    diag = jnp.where(_iota((bq, bq), 1) <= _iota((bq, bq), 0), s[:, off:], -1e30)
    s = jnp.concatenate([s[:, :off], diag], axis=1) if off else diag
    return _softmax_times(s, v)


ONE_BUFFER = pl.Buffered(1)


def _pcol(t, cb):
    return pl.BlockSpec((t, HD), lambda h, cb=cb: (0, cb + h), pipeline_mode=ONE_BUFFER)


def _smcol(t):
    return pl.BlockSpec((t, HD), lambda h: (0, SM), pipeline_mode=ONE_BUFFER)


def _head(t):
    return pl.BlockSpec((t, HD), lambda h: (0, h), pipeline_mode=ONE_BUFFER)


def _small(n):
    return pl.BlockSpec((n, HD), lambda h: (0, 0), pipeline_mode=ONE_BUFFER)


def _fox_fwd(p, fb, qg, kg, bq):
    t = p.shape[0]

    def body(fq, fk, fv, sm, fb_r, qg_r, kg_r, o, qn_s, cc_s):
        h = pl.program_id(0)
        qn, kn, ccol, crow = _fox_prep(fq[...], fk[...], sm[...], fb_r[...], qg_r[...], kg_r[...], h)
        qn_s[...] = qn
        cc_s[...] = ccol
        knb = kn.astype(BF16)
        vb = fv[...].astype(BF16)
        for i in range(t // bq):
            rows, ext = pl.ds(i * bq, bq), (i + 1) * bq
            o[rows, :] = _fox_block(qn_s[rows, :], knb[:ext], vb[:ext], cc_s[rows, :], crow[:, :ext], i * bq).astype(o.dtype)

    return pl.pallas_call(
        body, grid=(NF,), name="fox_fwd",
        in_specs=[_pcol(t, FQ), _pcol(t, FK), _pcol(t, FV), _smcol(t), _small(1), _small(1), _small(1)],
        out_specs=_head(t), out_shape=SDS((t, NF * HD), BF16),
        scratch_shapes=[pltpu.VMEM((t, HD), F32), pltpu.VMEM((t, 1), F32)],
        compiler_params=_cp("parallel"),
    )(p, p, p, p, fb, qg, kg)


def _fox_bwd(p, fb, qg, kg, dmix, bq, deps=()):
    t = p.shape[0]

    def body(*refs):
        fq, fk, fv, sm, fb_r, qg_r, kg_r, do = refs[:8]
        dfq, dfk, dfv, dsm, dfb, dqg, dkg, qn_s, cc_s, dqn_s, dcc_s, dkn_s, dv_s, dcr_s = refs[8 + len(deps):]
        h = pl.program_id(0)
        qn, kn, ccol, crow = _fox_prep(fq[...], fk[...], sm[...], fb_r[...], qg_r[...], kg_r[...], h)
        qn_s[...] = qn
        cc_s[...] = ccol
        v = fv[...]
        dkn_s[...] = jnp.zeros_like(dkn_s)
        dv_s[...] = jnp.zeros_like(dv_s)
        dcr_s[...] = jnp.zeros_like(dcr_s)

        for i in range(t // bq):
            rows, ext = pl.ds(i * bq, bq), (i + 1) * bq
            _, vjp = jax.vjp(lambda a, b, c, d, e, off=i * bq: _fox_block(a, b, c, d, e, off),
                             qn_s[rows, :], kn[:ext], v[:ext], cc_s[rows, :], crow[:, :ext])
            dq, dk, dv, dcc, dcr = vjp(do[rows, :].astype(F32))
            dqn_s[rows, :] = dq
            dcc_s[rows, :] = dcc
            dkn_s[:ext, :] += dk
            dv_s[:ext, :] += dv
            dcr_s[:, :ext] += dcr
        _, prep_vjp = jax.vjp(lambda a, b, c, d, e, f: _fox_prep(a, b, c, d, e, f, h),
                              fq[...], fk[...], sm[...], fb_r[...], qg_r[...], kg_r[...])
        g_fq, g_fk, g_sm, g_fb, g_qg, g_kg = prep_vjp((dqn_s[...], dkn_s[...], dcc_s[...], dcr_s[...]))
        dfq[...] = g_fq.astype(dfq.dtype)
        dfk[...] = g_fk.astype(dfk.dtype)
        dfv[...] = dv_s[...].astype(dfv.dtype)

        @pl.when(h == 0)
        def _():
            for r in (dsm, dfb, dqg, dkg):
                r[...] = jnp.zeros_like(r)

        dsm[...] += g_sm
        dfb[...] += g_fb
        dqg[...] += g_qg
        dkg[...] += g_kg

    head = _head(t)
    return pl.pallas_call(
        body, grid=(NF,), name="fox_bwd",
        in_specs=[_pcol(t, FQ), _pcol(t, FK), _pcol(t, FV), _smcol(t), _small(1), _small(1), _small(1), head]
        + [ANY_SPEC] * len(deps),
        out_specs=[head, head, head, _small(t), _small(1), _small(1), _small(1)],
        out_shape=[SDS((t, NF * HD), BF16)] * 3 + [SDS((t, HD), F32)] + [SDS((1, HD), F32)] * 3,
        scratch_shapes=[pltpu.VMEM((t, HD), F32), pltpu.VMEM((t, 1), F32), pltpu.VMEM((t, HD), F32),
                        pltpu.VMEM((t, 1), F32), pltpu.VMEM((t, HD), F32), pltpu.VMEM((t, HD), F32),
                        pltpu.VMEM((1, t), F32)],
        compiler_params=_cp("arbitrary"),
    )(p, p, p, p, fb, qg, kg, dmix, *deps)


def _mem_attn(mq, mk, mv, qg, kg):
    s = _dot((_rms(mq, qg) * (HD ** -0.5)).astype(BF16), _rms(mk, kg).astype(BF16), NT)
    return _softmax_times(s, mv)


def _mem_fwd(p, mkv, qg, kg):
    t, ml = p.shape[0], mkv.shape[0]

    def body(mq, mk, mv, qg_r, kg_r, o):
        o[...] = _mem_attn(mq[...], mk[...], mv[...], qg_r[...], kg_r[...]).astype(o.dtype)

    return pl.pallas_call(
        body, grid=(NM,), name="mem_fwd",
        in_specs=[_pcol(t, MQ), pl.BlockSpec((ml, HD), lambda h: (0, h)), pl.BlockSpec((ml, HD), lambda h: (0, NM + h)),
                  _small(1), _small(1)],
        out_specs=pl.BlockSpec((t, HD), lambda h: (0, h)), out_shape=SDS((t, NM * HD), BF16),
        compiler_params=_cp("parallel"),
    )(p, mkv, mkv, qg, kg)


def _mem_bwd(p, mkv, qg, kg, dmix, deps=()):
    t, ml = p.shape[0], mkv.shape[0]

    def body(*refs):
        mq, mk, mv, qg_r, kg_r, do = refs[:6]
        dmq, dmk, dmv, dqg, dkg = refs[6 + len(deps):]
        _, vjp = jax.vjp(_mem_attn, mq[...], mk[...], mv[...], qg_r[...], kg_r[...])
        g_q, g_k, g_v, g_qg, g_kg = vjp(do[...].astype(F32))
        dmq[...] = g_q.astype(dmq.dtype)
        dmk[...] = g_k
        dmv[...] = g_v

        @pl.when(pl.program_id(0) == 0)
        def _():
            dqg[...] = jnp.zeros_like(dqg)
            dkg[...] = jnp.zeros_like(dkg)

        dqg[...] += g_qg
        dkg[...] += g_kg

    return pl.pallas_call(
        body, grid=(NM,), name="mem_bwd",
        in_specs=[_pcol(t, MQ), pl.BlockSpec((ml, HD), lambda h: (0, h)), pl.BlockSpec((ml, HD), lambda h: (0, NM + h)),
                  _small(1), _small(1), pl.BlockSpec((t, HD), lambda h: (0, NF + NG + h))] + [ANY_SPEC] * len(deps),
        out_specs=[pl.BlockSpec((t, HD), lambda h: (0, h)), pl.BlockSpec((ml, HD), lambda h: (0, h)),
                   pl.BlockSpec((ml, HD), lambda h: (0, h)), _small(1), _small(1)],
        out_shape=[SDS((t, NM * HD), BF16), SDS((ml, NM * HD), F32), SDS((ml, NM * HD), F32),
                   SDS((1, HD), F32), SDS((1, HD), F32)],
        compiler_params=_cp("arbitrary"),
    )(p, mkv, mkv, qg, kg, dmix, *deps)


def _shift_down(x, s):
    if s == 0:
        return x
    return jnp.where(_iota(x.shape, 0) >= s, pltpu.roll(x, s, 0), 0.0)


def _shift_up(x, s):
    if s == 0:
        return x
    n = x.shape[0]
    return jnp.where(_iota(x.shape, 0) < n - s, pltpu.roll(x, n - s, 0), 0.0)


@jax.custom_vjp
def _conv4(x, w0, w1, w2, w3):
    return w0 * _shift_down(x, 3) + w1 * _shift_down(x, 2) + w2 * _shift_down(x, 1) + w3 * x


def _conv4_fwd(x, w0, w1, w2, w3):
    return _conv4(x, w0, w1, w2, w3), (x, w0, w1, w2, w3)


def _conv4_bwd(res, dy):
    x, w0, w1, w2, w3 = res
    ups = [_shift_up(dy, 3 - k) for k in range(4)]
    dx = w0 * ups[0] + w1 * ups[1] + w2 * ups[2] + w3 * ups[3]
    return (dx,) + tuple(jnp.sum(up * x, axis=0, keepdims=True) for up in ups)


_conv4.defvjp(_conv4_fwd, _conv4_bwd)


HALO = 8


def _gdn_gates(sm, alog, dtb):
    lane = _iota((1, HD), 1)
    g = -jnp.exp(alog) * _softplus(sm + dtb)
    return (jnp.where((lane >= L_GA) & (lane < L_GA + NG), g,
                      jnp.where((lane >= L_GB) & (lane < L_GB + NG), _sigmoid(sm), 0.0)),)


def _gdn_prep(gq, gk, gv, gates, taps, h):
    q, k, v = [_silu(_conv4(x, *taps[4 * j:4 * j + 4]))[HALO:] for j, x in enumerate((gq, gk, gv))]
    q = q * lax.rsqrt(jnp.sum(q * q, axis=-1, keepdims=True) + NORM_EPS) * (HD ** -0.5)
    k = k * lax.rsqrt(jnp.sum(k * k, axis=-1, keepdims=True) + NORM_EPS)
    return q, k, v, _lane_pick(gates, L_GA + h), _lane_pick(gates, L_GB + h)


def _split(x, n):
    parts, rest = [], x
    for i in range(n):
        parts.append(rest.astype(BF16))
        if i + 1 < n:
            rest = rest - parts[-1].astype(F32)
    return parts


def _raw_dot(a, b, form):
    lead = a.ndim - 2
    ca, cb = {"nn": (1, 0), "nt": (1, 1), "tn": (0, 0)}[form]
    batch = ((0,), (0,)) if lead else ((), ())
    return lax.dot_general(a, b, (((ca + lead,), (cb + lead,)), batch), preferred_element_type=F32)


def _pdot_impl(a, b, form, mode):
    if mode == "1":
        return _raw_dot(a.astype(BF16), b.astype(BF16), form)
    if mode == "3":
        (ah, al), (bh, bl) = _split(a, 2), _split(b, 2)
        return _raw_dot(ah, bh, form) + (_raw_dot(al, bh, form) + _raw_dot(ah, bl, form))
    if mode == "xa":
        return sum(_raw_dot(a.astype(BF16), t, form) for t in reversed(_split(b, 3)))
    return sum(_raw_dot(t, b.astype(BF16), form) for t in reversed(_split(a, 3)))


@functools.partial(jax.custom_vjp, nondiff_argnums=(2, 3))
def _pdot(a, b, form, mode):
    return _pdot_impl(a, b, form, mode)


def _pdot_fwd(a, b, form, mode):
    return _pdot_impl(a, b, form, mode), (a, b)


def _pdot_bwd(form, mode, res, ct):
    a, b = res
    da_args, db_args = {"nn": ((ct, b, "nt"), (a, ct, "tn")), "nt": ((ct, b, "nn"), (ct, a, "tn")),
                        "tn": ((b, ct, "nt"), (a, ct, "nn"))}[form]

    def side(args, exact):
        if mode in ("1", "3"):
            return mode
        return "xa" if args[0] is exact else "xb"

    if mode == "xa":
        return jnp.zeros_like(a), _pdot_impl(*db_args, side(db_args, a))
    if mode == "xb":
        return _pdot_impl(*da_args, side(da_args, b)), jnp.zeros_like(b)
    return _pdot_impl(*da_args, mode), _pdot_impl(*db_args, mode)


_pdot.defvjp(_pdot_fwd, _pdot_bwd)

GDN_QK, GDN_INV, GDN_SCAN = "1", "1", "1"


@jax.custom_vjp
def _tri_inv(low):
    eye = (_iota((CHUNK, CHUNK), 0) == _iota((CHUNK, CHUNK), 1)).astype(F32)
    inv = eye - low
    pw = low
    for _ in range(5):
        pw = _pdot_impl(pw, pw, "nn", GDN_INV)
        inv = inv + _pdot_impl(inv, pw, "nn", GDN_INV)
    return inv


def _tri_inv_fwd(low):
    inv = _tri_inv(low)
    return inv, inv


def _tri_inv_bwd(inv, ct):
    return (-_pdot_impl(_pdot_impl(inv, ct, "tn", GDN_INV), inv, "nt", GDN_INV),)


_tri_inv.defvjp(_tri_inv_fwd, _tri_inv_bwd)


def _gdn_intra(q, k, v, g, beta):
    n = q.shape[0]
    r, c = _iota((CHUNK, CHUNK), 0), _iota((CHUNK, CHUNK), 1)
    tril, strict = r >= c, r > c
    trilf = jnp.broadcast_to(tril.astype(F32), (n, CHUNK, CHUNK))
    gcm = _pdot(trilf, jnp.broadcast_to(g, (n, CHUNK, CHUNK)), "nn", "xa")
    gcf = _pdot(trilf, jnp.broadcast_to(g, (n, CHUNK, HD)), "nn", "xa")
    lane0 = (_iota((1, 1, CHUNK), 2) == 0).astype(F32)
    gcr = _pdot(jnp.ones((n, CHUNK, CHUNK), F32), gcm * lane0, "nt", "xa")
    decay = jnp.where(tril, jnp.exp(jnp.where(tril, gcm - gcr, 0.0)), 0.0)
    egc = jnp.exp(gcf)
    kb = k * beta
    low = jnp.where(strict, _pdot(kb, k, "nt", GDN_QK) * decay, 0.0)
    inv = _tri_inv(low)
    u = _pdot(inv, v * beta, "nn", GDN_INV)
    w = _pdot(inv, kb * egc, "nn", GDN_INV)
    at = jnp.where(tril, _pdot(q, k, "nt", GDN_QK) * decay, 0.0)
    gl = jnp.sum(jnp.broadcast_to(g, (n, CHUNK, HD)), axis=1, keepdims=True)
    kd = k * jnp.exp(gl - gcf)
    return (_pdot(kd, w, "tn", GDN_SCAN), _pdot(kd, u, "tn", GDN_SCAN), q * egc - _pdot(at, w, "nn", GDN_SCAN),
            _pdot(at, u, "nn", GDN_SCAN), gl)


def _gdn_step(s, kw, ku, a, b, gl):
    return _pdot(a, s, "nn", GDN_SCAN) + b, s * jnp.exp(gl) - _pdot(kw, s, "nn", GDN_SCAN) + ku


SCAN_HEADS = 3
SCAN_UNROLL = 2


def _gdn_chunked_scratch(nc):
    big = pltpu.VMEM((nc, CHUNK, HD), F32)
    return [big, big, big, pltpu.VMEM((nc, CHUNK, 1), F32), pltpu.VMEM((nc, CHUNK, 1), F32)]


N_TERMS = 5


def _gdn_term_shapes(nc):
    return [(nc, HD, HD), (nc, HD, HD), (nc, CHUNK, HD), (nc, CHUNK, HD), (nc, 1, HD)]


def _per_head(shape, heads=None, one_buffer=True):
    lead = (None,) if heads is None else (heads,)
    return pl.BlockSpec(lead + tuple(shape), lambda h: (h,) + (0,) * len(shape),
                        pipeline_mode=ONE_BUFFER if one_buffer else None)


def _gdn_in_specs(t):
    cw = lambda cb: pl.BlockSpec((4, HD), lambda h, cb=cb: (0, cb + h))
    return [_pcol(t, GQ), _pcol(t, GK), _pcol(t, GV), _small(t), cw(0), cw(NG), cw(2 * NG)]


def _taps(wq, wk, wv):
    return tuple(w[k:k + 1, :] for w in (wq, wk, wv) for k in range(4))


def _prep_rows(t):
    return min(t, 256)


def _gdn_pad(srcs, pads):
    for src, pad in zip(srcs, pads):
        pad[0:HALO, :] = jnp.zeros((HALO, HD), F32)
        pad[HALO:, :] = src[...]


def _gdn_stage(pads, gates, taps, h, chunked):
    t = gates.shape[0]
    rows = _prep_rows(t)
    per = rows // CHUNK

    def tile(i, carry):
        r0 = pl.multiple_of(i * rows, rows)
        vals = _gdn_prep(*[p[pl.ds(r0, rows + HALO), :] for p in pads], gates[pl.ds(r0, rows), :], taps, h)
        for v, r in zip(vals, chunked):
            r[pl.ds(i * per, per)] = v.reshape(per, CHUNK, v.shape[-1])
        return carry

    lax.fori_loop(0, t // rows, tile, 0)


def _gdn_intra_all(chunked, intra):
    nc = chunked[0].shape[0]
    grp_n = math.gcd(nc, GROUP)

    def grp(i, carry):
        sl = pl.ds(pl.multiple_of(i * grp_n, grp_n), grp_n)
        for r, val in zip(intra, _gdn_intra(*[c[sl] for c in chunked])):
            r[sl] = val
        return carry

    lax.fori_loop(0, nc // grp_n, grp, 0)


def _gdn_fwd(pa, gates, conv):
    t = pa.shape[0]
    nc = t // CHUNK
    terms = _gdn_term_shapes(nc)

    def body(gq, gk, gv, gt, wq, wk, wv, *rest):
        h = pl.program_id(0)
        intra, chunked, pads = rest[:N_TERMS], rest[N_TERMS:N_TERMS + 5], rest[N_TERMS + 5:]
        _gdn_pad((gq, gk, gv), pads)
        _gdn_stage(pads, gt, _taps(wq, wk, wv), h, chunked)
        _gdn_intra_all(chunked, intra)

    return pl.pallas_call(
        body, grid=(NG,), name="gdn_fwd", in_specs=_gdn_in_specs(t),
        out_specs=[_per_head(sh, one_buffer=False) for sh in terms], out_shape=[SDS((NG,) + sh, F32) for sh in terms],
        scratch_shapes=_gdn_chunked_scratch(nc) + [pltpu.VMEM((t + HALO, HD), F32)] * 3, compiler_params=_cp("parallel"),
    )(pa, pa, pa, gates, conv, conv, conv)


def _gdn_scan(terms_in):
    nc = terms_in[0].shape[1]
    terms = _gdn_term_shapes(nc)

    def body(*refs):
        intra, o, states = refs[:N_TERMS], refs[N_TERMS], refs[N_TERMS + 1]

        def one(c, ss):
            rows = pl.ds(pl.multiple_of(c * CHUNK, CHUNK), CHUNK)
            loaded = [[r[hh, c] for r in intra] for hh in range(SCAN_HEADS)]
            res = [_gdn_step(ss[hh], *loaded[hh]) for hh in range(SCAN_HEADS)]
            for hh in range(SCAN_HEADS):
                states[hh, c] = ss[hh]
                o[rows, hh * HD:(hh + 1) * HD] = res[hh][0]
            return tuple(r[1] for r in res)

        def step(i, ss):
            for k in range(SCAN_UNROLL):
                ss = one(SCAN_UNROLL * i + k, ss)
            return ss

        lax.fori_loop(0, nc // SCAN_UNROLL, step, tuple(jnp.zeros((HD, HD), F32) for _ in range(SCAN_HEADS)))

    return pl.pallas_call(
        body, grid=(NG // SCAN_HEADS,), name="gdn_scan", in_specs=[_per_head(sh, SCAN_HEADS) for sh in terms],
        out_specs=[pl.BlockSpec((nc * CHUNK, SCAN_HEADS * HD), lambda h: (0, h), pipeline_mode=ONE_BUFFER),
                   _per_head((nc, HD, HD), SCAN_HEADS)],
        out_shape=[SDS((nc * CHUNK, NG * HD), F32), SDS((NG, nc, HD, HD), F32)], compiler_params=_cp("parallel"),
    )(*terms_in)


def _gdn_bwd_scan(saved, do_raw):
    nc = saved[0].shape[1]
    terms = _gdn_term_shapes(nc)

    def body(*refs):
        intra, states, do, outs = refs[:N_TERMS], refs[N_TERMS], refs[N_TERMS + 1], refs[N_TERMS + 2:]

        def one(c, dss):
            rows = pl.ds(pl.multiple_of(c * CHUNK, CHUNK), CHUNK)
            loaded = [[states[hh, c]] + [r[hh, c] for r in intra] for hh in range(SCAN_HEADS)]
            cts = [do[rows, hh * HD:(hh + 1) * HD] for hh in range(SCAN_HEADS)]
            grads = [jax.vjp(_gdn_step, *loaded[hh])[1]((cts[hh], dss[hh])) for hh in range(SCAN_HEADS)]
            for hh in range(SCAN_HEADS):
                for r, gval in zip(outs, grads[hh][1:]):
                    r[hh, c] = gval
            return tuple(g[0] for g in grads)

        def bwd(i, dss):
            c = nc - 1 - SCAN_UNROLL * i
            for k in range(SCAN_UNROLL):
                dss = one(c - k, dss)
            return dss

        lax.fori_loop(0, nc // SCAN_UNROLL, bwd, tuple(jnp.zeros((HD, HD), F32) for _ in range(SCAN_HEADS)))

    return pl.pallas_call(
        body, grid=(NG // SCAN_HEADS,), name="gdn_bwd_scan",
        in_specs=[_per_head(sh, SCAN_HEADS) for sh in terms] + [_per_head((nc, HD, HD), SCAN_HEADS)]
        + [pl.BlockSpec((nc * CHUNK, SCAN_HEADS * HD), lambda h: (0, h), pipeline_mode=ONE_BUFFER)],
        out_specs=[_per_head(sh, SCAN_HEADS) for sh in terms],
        out_shape=[SDS((NG,) + sh, F32) for sh in terms], compiler_params=_cp("parallel"),
    )(*saved, do_raw)


def _gdn_bwd(pa, gates, conv, dterms):
    t = pa.shape[0]
    nc = t // CHUNK
    terms = _gdn_term_shapes(nc)

    def body(*refs):
        gq, gk, gv, gt, wq, wk, wv = refs[:7]
        dintra = refs[7:7 + N_TERMS]
        dgq, dgk, dgv, dgt, dwq, dwk, dwv = refs[7 + N_TERMS:14 + N_TERMS]
        chunked, pads, dpads, dgt_s = (refs[14 + N_TERMS:19 + N_TERMS], refs[19 + N_TERMS:22 + N_TERMS],
                                       refs[22 + N_TERMS:25 + N_TERMS], refs[25 + N_TERMS])
        h = pl.program_id(0)
        taps = _taps(wq, wk, wv)
        _gdn_pad((gq, gk, gv), pads)
        _gdn_stage(pads, gt, taps, h, chunked)
        grp_n = math.gcd(nc, GROUP)

        def grp(i, carry):
            sl = pl.ds(pl.multiple_of(i * grp_n, grp_n), grp_n)
            _, vjp = jax.vjp(_gdn_intra, *[r[sl] for r in chunked])
            for r, gval in zip(chunked, vjp(tuple(r[sl] for r in dintra))):
                r[sl] = gval
            return carry

        lax.fori_loop(0, nc // grp_n, grp, 0)

        rows = _prep_rows(t)
        per = rows // CHUNK
        for r in dpads:
            r[...] = jnp.zeros_like(r)

        def tile(i, dtaps):
            r0 = pl.multiple_of(i * rows, rows)
            win = pl.ds(r0, rows + HALO)
            _, vjp = jax.vjp(lambda *a: _gdn_prep(*a, h), *[p[win, :] for p in pads], gt[pl.ds(r0, rows), :], taps)
            grads = vjp(tuple(r[pl.ds(i * per, per)].reshape(rows, r.shape[-1]) for r in chunked))
            for r, gval in zip(dpads, grads[:3]):
                r[win, :] += gval
            dgt_s[pl.ds(r0, rows), :] = grads[3]
            return jax.tree.map(jnp.add, dtaps, grads[4])

        dtaps = lax.fori_loop(0, t // rows, tile, (jnp.zeros((1, HD), F32),) * 12)
        for r, dpad in zip((dgq, dgk, dgv), dpads):
            r[...] = dpad[HALO:, :].astype(r.dtype)
        for j, r in enumerate((dwq, dwk, dwv)):
            for k in range(4):
                r[k:k + 1, :] = dtaps[4 * j + k]

        @pl.when(h == 0)
        def _():
            dgt[...] = jnp.zeros_like(dgt)

        dgt[...] += dgt_s[...]

    head = _head(t)
    taps = pl.BlockSpec((4, HD), lambda h: (0, h))
    return pl.pallas_call(
        body, grid=(NG,), name="gdn_bwd", in_specs=_gdn_in_specs(t) + [_per_head(sh) for sh in terms],
        out_specs=[head, head, head, _small(t), taps, taps, taps],
        out_shape=[SDS((t, NG * HD), BF16)] * 3 + [SDS((t, HD), F32)] + [SDS((4, NG * HD), F32)] * 3,
        scratch_shapes=_gdn_chunked_scratch(nc) + [pltpu.VMEM((t + HALO, HD), F32)] * 6 + [pltpu.VMEM((t, HD), F32)],
        compiler_params=_cp("arbitrary"),
    )(pa, pa, pa, gates, conv, conv, conv, *dterms)


def _gdn_post(o, z, gain):
    return (jnp.concatenate(
        [_rms(o[:, h * HD:(h + 1) * HD], gain) * _silu(z[:, h * HD:(h + 1) * HD]) for h in range(NG)], axis=1),)


def _place():
    return lax.axis_index("x"), lax.axis_index("y"), lax.axis_index("c")


def _all_gather(name, shard):
    def body(x_ref, out_ref, send_sems, recv_sems, local_sem):
        x, y, c = _place()
        me, sibling = (x, y, c), (x, y, 1 - c)
        chips = [(1 - x, y), (x, 1 - y), (1 - x, 1 - y)]

        def blk(px, py, pc):
            return out_ref.at[4 * px + 2 * py + pc]

        def copy(k, block, to, src=None):
            return pltpu.make_async_remote_copy(
                src_ref=blk(*block) if src is None else src, dst_ref=blk(*block),
                send_sem=send_sems.at[k], recv_sem=recv_sems.at[k], device_id=to, device_id_type=MESH)

        mine = pltpu.make_async_copy(x_ref, blk(*me), local_sem)
        mine.start()
        first = [copy(0, me, sibling, src=x_ref)]
        first += [copy(1 + j, me, (*chip, c), src=x_ref) for j, chip in enumerate(chips)]
        for cp in first:
            cp.start()
        passed = [copy(4 + j, (*chip, c), sibling) for j, chip in enumerate(chips)]
        for j, chip in enumerate(chips):
            copy(1 + j, (*chip, c), me).wait_recv()
            passed[j].start()
        copy(0, sibling, me).wait_recv()
        for j, chip in enumerate(chips):
            copy(4 + j, (*chip, 1 - c), me).wait_recv()
        for cp in first + passed:
            cp.wait_send()
        mine.wait()

    return pl.pallas_call(
        body, name=name, out_shape=SDS((N_DEV,) + shard.shape, shard.dtype),
        in_specs=[pl.BlockSpec(memory_space=pltpu.HBM)], out_specs=pl.BlockSpec(memory_space=pltpu.HBM),
        scratch_shapes=[pltpu.SemaphoreType.DMA((7,)), pltpu.SemaphoreType.DMA((7,)), pltpu.SemaphoreType.DMA],
    )(shard)


def _scatter_exchange(name, full):
    def body(g_ref, out_ref, send_sems, recv_sems, local_sem):
        x, y, c = _place()
        me = 4 * x + 2 * y + c
        mine = pltpu.make_async_copy(g_ref.at[me], out_ref.at[me], local_sem)
        mine.start()
        sends, recvs = [], []
        for k in range(1, N_DEV):
            px = 1 - x if k & 4 else x
            py = 1 - y if k & 2 else y
            pc = 1 - c if k & 1 else c
            peer = 4 * px + 2 * py + pc
            sends.append(pltpu.make_async_remote_copy(
                src_ref=g_ref.at[peer], dst_ref=out_ref.at[me], send_sem=send_sems.at[k - 1],
                recv_sem=recv_sems.at[k - 1], device_id=(px, py, pc), device_id_type=MESH))
            recvs.append(pltpu.make_async_remote_copy(
                src_ref=g_ref.at[me], dst_ref=out_ref.at[peer], send_sem=send_sems.at[k - 1],
                recv_sem=recv_sems.at[k - 1], device_id=(px, py, pc), device_id_type=MESH))
        for cp in sends:
            cp.start()
        for cp in recvs:
            cp.wait_recv()
        for cp in sends:
            cp.wait_send()
        mine.wait()

    return pl.pallas_call(
        body, name=name, out_shape=SDS(full.shape, full.dtype),
        in_specs=[pl.BlockSpec(memory_space=pltpu.HBM)], out_specs=pl.BlockSpec(memory_space=pltpu.HBM),
        scratch_shapes=[pltpu.SemaphoreType.DMA((7,)), pltpu.SemaphoreType.DMA((7,)), pltpu.SemaphoreType.DMA],
    )(full)


def _sum_blocks(name, parts):
    _, r, c = parts.shape
    tr = 64 if r % 64 == 0 else r

    def body(x, o):
        acc = x[0].astype(F32)
        for d in range(1, N_DEV):
            acc = acc + x[d].astype(F32)
        o[...] = acc

    return pl.pallas_call(
        body, grid=(r // tr,), name=name, in_specs=[pl.BlockSpec((N_DEV, tr, c), lambda i: (0, i, 0))],
        out_specs=pl.BlockSpec((tr, c), lambda i: (i, 0)), out_shape=SDS((r, c), F32), compiler_params=_cp("parallel"),
    )(parts)


def _reduce_scatter(name, full):
    return _sum_blocks(name + "_sum", _scatter_exchange(name, full))


def _all_reduce_small(name, x, reduce):
    m_per, n = x.shape

    def body(x_ref, out_ref, send_sems, recv_sems, local_sem):
        px, py, pc = _place()
        me, sibling = (px, py, pc), (px, py, 1 - pc)
        chips = [(1 - px, py), (px, 1 - py), (1 - px, 1 - py)]
        buf = out_ref

        def rows(qx, qy, qc):
            return buf.at[pl.ds((4 * qx + 2 * qy + qc) * m_per, m_per), :]

        def copy(k, block, to, src=None):
            return pltpu.make_async_remote_copy(
                src_ref=rows(*block) if src is None else src, dst_ref=rows(*block),
                send_sem=send_sems.at[k], recv_sem=recv_sems.at[k], device_id=to, device_id_type=MESH)

        mine = pltpu.make_async_copy(x_ref, rows(*me), local_sem)
        mine.start()
        first = [copy(0, me, sibling, src=x_ref)]
        first += [copy(1 + j, me, (*chip, pc), src=x_ref) for j, chip in enumerate(chips)]
        for cp in first:
            cp.start()
        passed = [copy(4 + j, (*chip, pc), sibling) for j, chip in enumerate(chips)]
        for j, chip in enumerate(chips):
            copy(1 + j, (*chip, pc), me).wait_recv()
            passed[j].start()
        copy(0, sibling, me).wait_recv()
        for j, chip in enumerate(chips):
            copy(4 + j, (*chip, 1 - pc), me).wait_recv()
        for cp in first + passed:
            cp.wait_send()
        mine.wait()

    gathered = pl.pallas_call(
        body, name=name, out_shape=SDS((N_DEV * m_per, n), x.dtype),
        in_specs=[pl.BlockSpec(memory_space=pltpu.VMEM)], out_specs=pl.BlockSpec(memory_space=pltpu.VMEM),
        scratch_shapes=[pltpu.SemaphoreType.DMA((7,)), pltpu.SemaphoreType.DMA((7,)), pltpu.SemaphoreType.DMA],
    )(x)
    if not reduce:
        return gathered
    return _sum_blocks(name + "_sum", gathered.reshape(N_DEV, m_per, n))


HBM_SPEC = pl.BlockSpec(memory_space=pltpu.HBM)
SEM_SPEC = pl.BlockSpec(memory_space=pltpu.SEMAPHORE)
EFFECT = pltpu.SideEffectType.DATAFLOW_SIDE_EFFECTING


def _copies_start(name, bufs, n_remote, n_local, build, deps):
    nb, nd = len(bufs), len(deps)
    sem_shapes = [pltpu.SemaphoreType.DMA((n_remote,)), pltpu.SemaphoreType.DMA((n_remote,))]
    if n_local:
        sem_shapes.append(pltpu.SemaphoreType.DMA((n_local,)))
    ns = len(sem_shapes)

    def body(*refs):
        sems = refs[nb + nd:nb + nd + ns]
        remote, local = build(refs[:nb], *sems, *([None] * (3 - ns)))
        for cp in local + remote:
            cp.start()
        refs[-1][...] = jnp.zeros((8, HD), F32)

    outs = pl.pallas_call(
        body, name=name,
        out_shape=(*sem_shapes, *[pltpu.HBM(b.shape, b.dtype) for b in bufs], SDS((8, HD), F32)),
        in_specs=[HBM_SPEC] * nb + [ANY_SPEC] * nd,
        out_specs=(*[SEM_SPEC] * ns, *[HBM_SPEC] * nb, pl.BlockSpec(memory_space=pltpu.VMEM)),
        input_output_aliases={i: ns + i for i in range(nb)},
        compiler_params=pltpu.CompilerParams(has_side_effects=EFFECT),
    )(*[pltpu.with_memory_space_constraint(b, pltpu.HBM) for b in bufs], *deps)
    return list(outs[:ns]), list(outs[ns:ns + nb]), outs[-1]


def _copies_wait(name, bufs, sems, build, after):
    nb, ns = len(bufs), len(sems)

    def body(*refs):
        remote, local = build(refs[:nb], *refs[nb:nb + ns], *([None] * (3 - ns)))
        for cp in local:
            cp.wait()
        for cp in remote:
            cp.wait_send()
            cp.wait_recv()

    outs = pl.pallas_call(
        body, name=name, out_shape=tuple(pltpu.HBM(b.shape, b.dtype) for b in bufs),
        in_specs=[HBM_SPEC] * nb + [SEM_SPEC] * ns + [ANY_SPEC] * len(after), out_specs=tuple([HBM_SPEC] * nb),
        input_output_aliases={i: i for i in range(nb)},
        compiler_params=pltpu.CompilerParams(has_side_effects=EFFECT),
    )(*bufs, *sems, *after)
    return list(outs)


def _remote(src, dst, send, recv, k, to):
    return pltpu.make_async_remote_copy(src_ref=src, dst_ref=dst, send_sem=send.at[k], recv_sem=recv.at[k],
                                        device_id=to, device_id_type=MESH)


class _Gather:
    def __init__(self, name, shards, deps):
        self.name, self.n = name, len(shards)
        lands = [lax.empty((N_DEV,) + s.shape, s.dtype) for s in shards]
        self.sems1, bufs, self.token = _copies_start(
            name + "_s1", list(shards) + lands, 4 * self.n, self.n, self._stage1(range(self.n)), deps)
        self.shards, self.lands, self.sems2 = bufs[:self.n], bufs[self.n:], {}

    def _stage1(self, idxs):
        def build(refs, send, recv, loc):
            x, y, c = _place()
            me = 4 * x + 2 * y + c
            targets = [(x, y, 1 - c), (1 - x, y, c), (x, 1 - y, c), (1 - x, 1 - y, c)]
            remote, local = [], []
            for pos, i in enumerate(idxs):
                src, land = refs[pos], refs[len(idxs) + pos]
                local.append(pltpu.make_async_copy(src, land.at[me], loc.at[i]))
                remote += [_remote(src, land.at[me], send, recv, 4 * i + k, to) for k, to in enumerate(targets)]
            return remote, local
        return build

    @staticmethod
    def _stage2(refs, send, recv, loc):
        x, y, c = _place()
        remote = []
        for pos, land in enumerate(refs):
            for j, (cx, cy) in enumerate([(1 - x, y), (x, 1 - y), (1 - x, 1 - y)]):
                blk = land.at[4 * cx + 2 * cy + c]
                remote.append(_remote(blk, blk, send, recv, 3 * pos + j, (x, y, 1 - c)))
        return remote, []

    def pass_on(self, idxs, after):
        tag, m = "".join(map(str, idxs)), len(idxs)
        bufs = _copies_wait(f"{self.name}_w1_{tag}", [self.shards[i] for i in idxs] + [self.lands[i] for i in idxs],
                            self.sems1, self._stage1(idxs), after)
        self.sems2[tag], lands, token = _copies_start(f"{self.name}_s2_{tag}", bufs[m:], 3 * m, 0, self._stage2, ())
        for pos, i in enumerate(idxs):
            self.lands[i] = lands[pos]
        return [token]

    def get(self, idxs, after):
        tag = "".join(map(str, idxs))
        return _copies_wait(f"{self.name}_w2_{tag}", [self.lands[i] for i in idxs], self.sems2[tag], self._stage2, after)


def _rows_tile(r, row_bytes, target=1 << 20):
    tr = r
    while tr % 32 == 0 and tr * row_bytes > target:
        tr //= 2
    return tr


def _pair_add(name, g, got, c):
    _, r, cols = g.shape
    tr = _rows_tile(r, cols * 2)

    def body(s, a, b, o):
        o[...] = (a[...].astype(F32) + b[...].astype(F32)).astype(o.dtype)

    return pl.pallas_call(
        body, name=name, out_shape=SDS((4, r, cols), g.dtype),
        grid_spec=pltpu.PrefetchScalarGridSpec(
            num_scalar_prefetch=1, grid=(4, r // tr),
            in_specs=[pl.BlockSpec((None, tr, cols), lambda j, i, s: (2 * j + s[0], i, 0)),
                      pl.BlockSpec((None, tr, cols), lambda j, i, s: (j, i, 0))],
            out_specs=pl.BlockSpec((None, tr, cols), lambda j, i, s: (j, i, 0))),
        compiler_params=_cp("parallel", "parallel"),
    )(c.reshape(1), g, got)


def _quad_sum(name, part, got, chip, wmv=None):
    _, r, cols = part.shape
    tr = _rows_tile(r, cols * 4)
    n_out = 4 if wmv else 1

    def body(s, a, b1, b2, b3, *rest):
        g = ((a[...].astype(F32) + b1[...].astype(F32)) + b2[...].astype(F32)) + b3[...].astype(F32)
        rest[-n_out][...] = g
        if wmv:
            w, m, v = rest[:3]
            rest[-3][...], rest[-2][...], rest[-1][...] = _adamw(w[...], g, m[...], v[...])

    blk = lambda k: pl.BlockSpec((None, tr, cols), lambda i, s, k=k: (jnp.bitwise_xor(s[0], k), i, 0))
    row = pl.BlockSpec((tr, cols), lambda i, s: (i, 0))
    outs = pl.pallas_call(
        body, name=name, out_shape=[SDS((r, cols), F32)] * n_out,
        grid_spec=pltpu.PrefetchScalarGridSpec(
            num_scalar_prefetch=1, grid=(r // tr,), in_specs=[blk(0), blk(1), blk(2), blk(3)] + [row] * (n_out - 1),
            out_specs=[row] * n_out),
        compiler_params=_cp("parallel"),
    )(chip.reshape(1), part, got, got, got, *(wmv or ()))
    return tuple(outs) if wmv else outs[0]


class _Scatter:
    def __init__(self, name, grads, deps):
        self.name, self.n = name, len(grads)
        got = [lax.empty((4,) + g.shape[1:], g.dtype) for g in grads]
        self.sems, bufs, self.token = _copies_start(name + "_s1", list(grads) + got, 4 * self.n, 0, self._stage1, deps)
        self.grads, self.got = bufs[:self.n], bufs[self.n:]

    def _stage1(self, refs, send, recv, loc):
        x, y, c = _place()
        remote = []
        for i in range(self.n):
            remote += [_remote(refs[i].at[2 * j + 1 - c], refs[self.n + i].at[j], send, recv, 4 * i + j, (x, y, 1 - c))
                       for j in range(4)]
        return remote, []

    def _stage2(self, refs, send, recv, loc):
        x, y, c = _place()
        remote = []
        for i in range(self.n):
            for k in (1, 2, 3):
                tx = 1 - x if k & 2 else x
                ty = 1 - y if k & 1 else y
                remote.append(_remote(refs[i].at[2 * tx + ty], refs[self.n + i].at[2 * x + y], send, recv,
                                      3 * i + k - 1, (tx, ty, c)))
        return remote, []

    def mid(self, after):
        bufs = _copies_wait(self.name + "_w1", self.grads + self.got, self.sems, self._stage1, after)
        c = lax.axis_index("c").astype(jnp.int32)
        parts = [_pair_add(f"{self.name}_add{i}", bufs[i], bufs[self.n + i], c) for i in range(self.n)]
        got = [lax.empty(p.shape, p.dtype) for p in parts]
        self.sems, bufs, self.token = _copies_start(self.name + "_s2", parts + got, 3 * self.n, 0, self._stage2, ())
        self.parts, self.got = bufs[:self.n], bufs[self.n:]

    def end(self, after, wmv=None):
        bufs = _copies_wait(self.name + "_w2", self.parts + self.got, self.sems, self._stage2, after)
        chip = (2 * lax.axis_index("x") + lax.axis_index("y")).astype(jnp.int32)
        wmv = wmv or [None] * self.n
        return [_quad_sum(f"{self.name}_sum{i}", bufs[i], bufs[self.n + i], chip, wmv[i]) for i in range(self.n)]


def _adamw(w, g, m, v):
    m = ADAM_B1 * m + (1.0 - ADAM_B1) * g
    v = ADAM_B2 * v + (1.0 - ADAM_B2) * (g * g)
    m_hat = m / (1.0 - ADAM_B1 ** ADAM_STEP)
    v_hat = v / (1.0 - ADAM_B2 ** ADAM_STEP)
    return -ADAM_LR * (m_hat / (jnp.sqrt(v_hat) + ADAM_EPS) + ADAM_WD * w), m, v


def _adamw_call(name, w, g, m, v):
    r, c = w.shape
    tm = 64 if r % 64 == 0 else r
    return _rowwise(name, _adamw, [w, g, m, v], [], [(c, F32)] * 3, tm)


_IN_COLS = 5906


def _perm_in(w):
    pad = jnp.zeros((w.shape[0], 2 * HALF - _IN_COLS), w.dtype)
    return (jnp.concatenate([w[:, 2310:4614], w[:, 4614:5382]], axis=1),
            jnp.concatenate([w[:, :2304], w[:, 5394:5906], w[:, 2304:2310], w[:, 5382:5394], pad], axis=1))


def _unperm_in(ga, gb):
    return jnp.concatenate([gb[:, :2304], gb[:, 2816:2822], ga[:, :2304], ga[:, 2304:3072], gb[:, 2822:2834],
                            gb[:, 2304:2816]], axis=1)


def _lanes(v, at):
    return jnp.pad(v, ((0, 0), (at, HD - at - v.shape[1])))


_PACK = ("norm_mix", "mem_norm", "norm_ffn", "gdn_conv", "fox_q_norm", "fox_k_norm", "gdn_out_norm", "mem_q_norm",
         "mem_k_norm", "fox_f_bias", "gdn_a_log", "gdn_dt_bias", "loss")


def _pack(vals):
    parts = [vals[n].reshape(-1, HD) for n in _PACK]
    used = sum(p.shape[0] for p in parts)
    buf = jnp.concatenate(parts + [jnp.zeros((-used % 8, HD), F32)], axis=0)
    return buf, [(n, p.shape[0]) for n, p in zip(_PACK, parts)]


def _unpack(buf, layout):
    out, at = {}, 0
    for n, rows in layout:
        out[n] = buf[at:at + rows]
        at += rows
    return out


def kernel(x, mem, norm_mix, w_in, fox_f_bias, fox_q_norm, fox_k_norm, gdn_conv, gdn_a_log, gdn_dt_bias, gdn_out_norm, mem_norm, w_mem_kv, mem_q_norm, mem_k_norm, w_out, norm_ffn, w_gate_up, w_down, loss_target, m_norm_mix, m_w_in, m_fox_f_bias, m_fox_q_norm, m_fox_k_norm, m_gdn_conv, m_gdn_a_log, m_gdn_dt_bias, m_gdn_out_norm, m_mem_norm, m_w_mem_kv, m_mem_q_norm, m_mem_k_norm, m_w_out, m_norm_ffn, m_w_gate_up, m_w_down, v_norm_mix, v_w_in, v_fox_f_bias, v_fox_q_norm, v_fox_k_norm, v_gdn_conv, v_gdn_a_log, v_gdn_dt_bias, v_gdn_out_norm, v_mem_norm, v_w_mem_kv, v_mem_q_norm, v_mem_k_norm, v_w_out, v_norm_ffn, v_w_gate_up, v_w_down):
    args = dict(locals())
    d = x.shape[2]
    me = 4 * lax.axis_index("x") + 2 * lax.axis_index("y") + lax.axis_index("c")

    cshard = gdn_conv[0].shape[1]
    conv_pad = jnp.pad(gdn_conv[0], ((0, 4), (0, 3 * HD - cshard)))
    conv_all = _all_reduce_small("ag_conv", conv_pad, False).reshape(N_DEV, 8, 3 * HD)[:, :4, :cshard]
    conv_all = conv_all.transpose(1, 0, 2).reshape(4, N_DEV * cshard)
    w_in_a, w_in_b = _perm_in(w_in[0])
    comm = _StepComm({"in_b": [w_in_b], "in_a": [w_in_a], "kv_out": [w_mem_kv[0], w_out[0]], "gate_up": [w_gate_up[0]],
                      "down": [w_down[0]]}, [conv_all])

    grad_x, loss_local, small_grads = _local_step(
        x[0], mem[0], loss_target[0], norm_mix, fox_f_bias, fox_q_norm, fox_k_norm, gdn_a_log, gdn_dt_bias,
        gdn_out_norm, mem_norm, mem_q_norm, mem_k_norm, norm_ffn, conv_all, comm)

    wmv = lambda n: (args[n][0], args["m_" + n][0], args["v_" + n][0])
    red = comm.finish([grad_x], {"ffn": [wmv("w_down"), wmv("w_gate_up")], "a": [None, wmv("w_out"), wmv("w_mem_kv")],
                                 "b": [None]})
    updated = {"w_down": red["ffn"][0], "w_gate_up": red["ffn"][1], "w_out": red["a"][1], "w_mem_kv": red["a"][2]}
    grads = {n: r[0] for n, r in updated.items()}
    grads["w_in"] = _unperm_in(red["a"][0], red["b"][0])
    small_grads["loss"] = jnp.broadcast_to(loss_local, (1, HD))
    packed, layout = _pack(small_grads)
    small = _unpack(_all_reduce_small("ar_small", packed, True), layout)
    loss = small["loss"][0, 0]
    six = {"fox_f_bias": L_FF, "gdn_a_log": L_GA, "gdn_dt_bias": L_GA}
    for n, rows_n in layout[:-1]:
        gsm = small[n]
        if n == "gdn_conv":
            gsm = lax.dynamic_slice(gsm.reshape(4, N_DEV * cshard), (0, me * cshard), (4, cshard))[None]
        elif n in six:
            gsm = gsm[:, six[n]:six[n] + 6]
        else:
            gsm = gsm.reshape(1, rows_n * HD)
        grads[n] = gsm

    names = ['norm_mix', 'w_in', 'fox_f_bias', 'fox_q_norm', 'fox_k_norm', 'gdn_conv', 'gdn_a_log', 'gdn_dt_bias',
             'gdn_out_norm', 'mem_norm', 'w_mem_kv', 'mem_q_norm', 'mem_k_norm', 'w_out', 'norm_ffn', 'w_gate_up', 'w_down']
    big = ("w_in", "w_mem_kv", "w_out", "w_gate_up", "w_down")
    delta, new_m, new_v = {}, {}, {}
    for n in big:
        res = updated[n][1:] if n in updated else _adamw_call("adamw_" + n, args[n][0], grads[n], *wmv(n)[1:])
        delta[n], new_m[n], new_v[n] = [a[None] for a in res]
        grads[n] = grads[n][None]

    def flat(a):
        a = a.reshape(1, -1)
        return jnp.pad(a, ((0, 0), (0, -a.shape[1] % HD))).reshape(-1, HD)

    smalls = [n for n in names if n not in big]
    pk = lambda pre: jnp.concatenate([flat(grads[n] if pre == "g" else args[pre + n]) for n in smalls], axis=0)
    cat = [pk(""), pk("g"), pk("m_"), pk("v_")]
    padr = -cat[0].shape[0] % 8
    cat = [jnp.pad(a, ((0, padr), (0, 0))) for a in cat]
    res = _adamw_call("adamw_small", *cat)
    at = 0
    for n in smalls:
        shape = args[n].shape
        size = math.prod(shape)
        nrow = -(-size // HD)
        for dst, src in zip((delta, new_m, new_v), res):
            dst[n] = src[at:at + nrow].reshape(-1)[:size].reshape(shape)
        at += nrow

    return (loss, grad_x[None], *[grads[n] for n in names], *[delta[n] for n in names],
            *[new_m[n] for n in names], *[new_v[n] for n in names])


class _StepComm:
    def __init__(self, shard_groups, after):
        self.groups, shards = {}, []
        for key, ws in shard_groups.items():
            self.groups[key] = list(range(len(shards), len(shards) + len(ws)))
            shards += [w.astype(BF16) for w in ws]
        self.gather = _Gather("ag", shards, after)
        self.passed, self.scatters = set(), {}

    def start_deps(self):
        return [self.gather.token]

    def pass_on(self, key, after):
        self.passed.add(key)
        return self.gather.pass_on(self.groups[key], after)

    def weights(self, key, after):
        if key not in self.passed:
            after = self.pass_on(key, after)
        return self.gather.get(self.groups[key], after)

    def send(self, tag, grads):
        blocks = [g if g.ndim == 3 else g.reshape(N_DEV, g.shape[0] // N_DEV, g.shape[1]) for g in grads]
        self.scatters[tag] = _Scatter("rs_" + tag, blocks, ())
        return [self.scatters[tag].token]

    def mid(self, tag, after):
        self.scatters[tag].mid(after)
        return [self.scatters[tag].token]

    def finish(self, after, wmv):
        return {tag: sc.end(after, wmv[tag]) for tag, sc in self.scatters.items()}


def _local_step(xs, ms, tgt, norm_mix, fox_f_bias, fox_q_norm, fox_k_norm, gdn_a_log, gdn_dt_bias, gdn_out_norm,
                mem_norm, mem_q_norm, mem_k_norm, norm_ffn, conv_all, comm):
    t, d = xs.shape
    bq = min(t, 256)
    fb, alog, dtb = _lanes(fox_f_bias, L_FF), _lanes(gdn_a_log, L_GA), _lanes(gdn_dt_bias, L_GA)
    flat = lambda w: w.reshape(-1, w.shape[-1])

    rms1 = lambda a, g: (_rms(a, g),)
    (u,) = _rowwise("norm_mix", rms1, [xs], [norm_mix], [(d, BF16)], min(t, 256), deps=comm.start_deps())
    w_in_b = flat(comm.weights("in_b", [u])[0])
    pb = _matmul("proj_in_b", u, w_in_b, NN, F32, 1024, 768)
    o_fox = _fox_fwd(pb, fb, fox_q_norm, fox_k_norm, bq)
    w_in_a = flat(comm.weights("in_a", [o_fox])[0])
    pa = _matmul("proj_in_a", u, w_in_a, NN, F32, 1024, 768)
    smrow = (pb, HD, SM)
    (gates,) = _rowwise("gdn_gates", _gdn_gates, [smrow], [alog, dtb], [(HD, F32)], min(t, 256))
    gdn_terms = _gdn_fwd(pa, gates, conv_all)
    o_gdn_raw, gdn_states = _gdn_scan(gdn_terms)
    gdn_saved = list(gdn_terms) + [gdn_states]
    zrow = (pa, NG * HD, GZ * HD // (NG * HD))
    (o_gdn,) = _rowwise("gdn_post", _gdn_post, [o_gdn_raw, zrow], [gdn_out_norm], [(NG * HD, BF16)], min(t, 256))
    w_kv_all, w_out_all = [flat(w) for w in comm.weights("kv_out", [o_gdn])]
    (mem_n,) = _rowwise("norm_mem", rms1, [ms], [mem_norm], [(d, BF16)], ms.shape[0])
    mkv = _matmul("proj_mem", mem_n, w_kv_all, NN, F32, 256, 512)
    o_mem = _mem_fwd(pb, mkv, mem_q_norm, mem_k_norm)
    deps = comm.pass_on("gate_up", [o_mem])
    mix = jnp.concatenate([o_fox, o_gdn, o_mem], axis=1)
    h1, h1n = _proj_out_norm(mix, w_out_all, xs, norm_ffn, deps)
    (wgu,) = comm.weights("gate_up", [h1n])
    ffw = wgu.shape[2]
    gu, act = _ffn_up(h1n, wgu.reshape(2, 4, d, ffw))
    w_down_all = flat(comm.weights("down", [act])[0])
    dyb, lsum = _ffn_down_loss(act, w_down_all, h1, tgt)
    loss_local = (0.5 / d) * jnp.sum(lsum[::8, ::HD])

    dgu = _ffn_down_bwd(dyb, w_down_all.reshape(4, ffw, d), gu).reshape(8, t, ffw)
    g_w_down = _matmul("grad_w_down", act, dyb, TN, BF16, 512, 2048)
    g_w_gu = _ffn_up_bwd_w(h1n, dgu)
    deps = comm.send("ffn", [g_w_down, g_w_gu])
    rms2 = lambda a, g: (_rms(a, g), a)
    dh1b, g_norm_ffn = _ffn_up_bwd_x(dgu, wgu, h1, norm_ffn, dyb, deps)

    dmix = _matmul("proj_out_bwd_x", dh1b, w_out_all, NT, BF16, 1024, 1024)
    g_w_out = _matmul("grad_w_out", mix, dh1b, TN, BF16, 1024, 2048)
    deps = comm.mid("ffn", [dmix, g_w_out])
    dmq, dmk, dmv, g_mqn, g_mkn = _mem_bwd(pb, mkv, mem_q_norm, mem_k_norm, dmix, deps=deps)
    dmkv = jnp.concatenate([dmk, dmv], axis=1).astype(BF16)
    g_w_kv = _matmul("grad_w_kv", mem_n, dmkv, TN, BF16, 512, 512)
    do_raw, dgz, g_gon = _rowwise_vjp("gdn_post_bwd", _gdn_post, [o_gdn_raw, zrow], [gdn_out_norm],
                                      [(dmix, NG * HD, 1)], [F32, BF16], min(t, 256), deps=deps)
    dterms = _gdn_bwd_scan(gdn_saved, do_raw)
    dgq, dgk, dgv, dgates, dwq, dwk, dwv = _gdn_bwd(pa, gates, conv_all, dterms)
    dsm_gdn, g_alog, g_dtb = _rowwise_vjp("gdn_gates_bwd", _gdn_gates, [smrow], [alog, dtb], [dgates], [F32], min(t, 256))
    dp_a = jnp.concatenate([dgq, dgk, dgv, dgz], axis=1)
    g_w_in_a = _matmul("grad_w_in_a", u, dp_a, TN, BF16, 512, 3072)
    deps = comm.send("a", [g_w_in_a, g_w_out, g_w_kv])
    du_a = _matmul("proj_in_bwd_a", dp_a, w_in_a, NT, F32, 1024, 1024, deps=deps)
    deps = comm.mid("a", [du_a])
    dfq, dfk, dfv, dsm_fox, g_fb, g_fqn, g_fkn = _fox_bwd(pb, fb, fox_q_norm, fox_k_norm, dmix, bq, deps=deps)
    dp_b = jnp.concatenate([dfq, dfk, dfv, dmq, (dsm_fox + dsm_gdn).astype(BF16), jnp.zeros((t, HD), BF16)], axis=1)
    g_w_in_b = _matmul("grad_w_in_b", u, dp_b, TN, BF16, 512, 3072)
    deps = comm.send("b", [g_w_in_b])
    dmem_n = _matmul("proj_mem_bwd_x", dmkv, w_kv_all, NT, F32, 256, 512, deps=deps)
    g_mem_norm = _rowwise_vjp("norm_mem_bwd", rms1, [ms], [mem_norm], [dmem_n], [], ms.shape[0])[0]
    deps = comm.mid("b", [g_mem_norm])
    du = _matmul("proj_in_bwd_b", dp_b, w_in_b, NT, F32, 1024, 1024, residual=du_a, deps=deps)
    grad_x, g_norm_mix = _rowwise_vjp("norm_mix_bwd", rms2, [xs], [norm_mix], [du, dh1b], [F32], min(t, 256))

    small_grads = {
        "norm_mix": g_norm_mix, "mem_norm": g_mem_norm, "norm_ffn": g_norm_ffn,
        "gdn_conv": jnp.concatenate([dwq, dwk, dwv], axis=1),
        "fox_q_norm": g_fqn, "fox_k_norm": g_fkn, "gdn_out_norm": g_gon, "mem_q_norm": g_mqn, "mem_k_norm": g_mkn,
        "fox_f_bias": g_fb, "gdn_a_log": g_alog, "gdn_dt_bias": g_dtb}
    return grad_x, loss_local, small_grads
```

```python
import functools
import math

import jax
import jax.numpy as jnp
from jax import lax
from jax.experimental import pallas as pl
from jax.experimental.pallas import tpu as pltpu

F32 = jnp.float32
BF16 = jnp.bfloat16
SDS = jax.ShapeDtypeStruct

N_DEV = 8
HD = 128
NF, NG, NM = 6, 6, 4
CHUNK = 64
GROUP = 16
NORM_EPS = 1e-6
GQ, GK, GV, GZ = 0, 6, 12, 18
FQ, FK, FV, MQ, SM = 0, 6, 12, 18, 22
HALF = 24 * HD
L_FF, L_GA, L_GB = 0, 6, 12
VMEM_LIMIT = 56 * 1024 * 1024

ADAM_LR, ADAM_B1, ADAM_B2, ADAM_EPS, ADAM_WD, ADAM_STEP = 0.001, 0.9, 0.999, 1e-08, 0.01, 10

NN = (((1,), (0,)), ((), ()))
NT = (((1,), (1,)), ((), ()))
TN = (((0,), (0,)), ((), ()))
MESH = pl.DeviceIdType.MESH


def _cp(*sem):
    return pltpu.CompilerParams(dimension_semantics=tuple(sem) if sem else None, vmem_limit_bytes=VMEM_LIMIT)


def _dot(a, b, dims=NN):
    return lax.dot_general(a, b, dims, preferred_element_type=F32)


def _iota(shape, axis):
    return lax.broadcasted_iota(jnp.int32, shape, axis)


def _rms(x, gain):
    return x * lax.rsqrt(jnp.mean(x * x, axis=-1, keepdims=True) + NORM_EPS) * gain


def _sigmoid(x):
    return 0.5 * jnp.tanh(0.5 * x) + 0.5


def _silu(x):
    return x * _sigmoid(x)


def _softplus(x):
    return jnp.maximum(x, 0.0) + jnp.log(1.0 + jnp.exp(-jnp.abs(x)))


def _lane_pick(x, lane):
    oh = (_iota((1, x.shape[-1]), 1) == lane).astype(F32)
    return jnp.sum(x * oh, axis=-1, keepdims=True)


def _cumsum_rows(x):
    tril = (_iota((HD, HD), 0) >= _iota((HD, HD), 1)).astype(F32)
    carry = jnp.zeros((1, x.shape[1]), F32)
    outs = []
    for b in range(x.shape[0] // HD):
        blk = x[b * HD:(b + 1) * HD]
        outs.append(_pdot(tril, blk, "nn", "xa") + carry)
        carry = carry + jnp.sum(blk, axis=0, keepdims=True)
    return jnp.concatenate(outs, axis=0)


def _row_spec(r, tm):
    if isinstance(r, tuple):
        arr, width, cb = r
        return arr, pl.BlockSpec((tm, width), lambda i, cb=cb: (i, cb))
    return r, pl.BlockSpec((tm, r.shape[1]), lambda i: (i, 0))


ANY_SPEC = pl.BlockSpec(memory_space=pl.ANY)


def _rowwise(name, fn, rows, consts, outs, tm, deps=()):
    arrs, specs = zip(*[_row_spec(r, tm) for r in rows])
    n_rows = arrs[0].shape[0]
    nr, nc, nd = len(rows), len(consts), len(deps)

    def body(*refs):
        res = fn(*[r[...] for r in refs[:nr + nc]])
        for o, v in zip(refs[nr + nc + nd:], res):
            o[...] = v.astype(o.dtype)

    return pl.pallas_call(
        body, grid=(n_rows // tm,), name=name,
        in_specs=list(specs) + [pl.BlockSpec(c.shape, lambda i: (0, 0)) for c in consts] + [ANY_SPEC] * nd,
        out_specs=[pl.BlockSpec((tm, w), lambda i: (i, 0)) for w, _ in outs],
        out_shape=[SDS((n_rows, w), dt) for w, dt in outs],
        compiler_params=_cp("parallel"),
    )(*arrs, *consts, *deps)


def _rowwise_vjp(name, fn, rows, consts, cts, grad_dtypes, tm, deps=()):
    arrs, specs = zip(*[_row_spec(r, tm) for r in rows])
    ct_arrs, ct_specs = zip(*[_row_spec(r, tm) for r in cts])
    n_rows = arrs[0].shape[0]
    nr, nc, nct, nd = len(rows), len(consts), len(cts), len(deps)
    plan = [(j, dt) for j, dts in enumerate(grad_dtypes) for dt in (dts if isinstance(dts, tuple) else (dts,))]
    ng = len(plan)
    widths = [specs[j].block_shape[1] for j, _ in plan]
    grad_dtypes = [dt for _, dt in plan]

    def body(*refs):
        vals = [r[...].astype(F32) for r in refs[:nr + nc]]
        ctv = tuple(r[...].astype(F32) for r in refs[nr + nc:nr + nc + nct])
        _, vjp = jax.vjp(fn, *vals)
        grads = vjp(ctv)
        outs = refs[nr + nc + nct + nd:]
        for o, (j, _) in zip(outs[:ng], plan):
            o[...] = grads[j].astype(o.dtype)

        @pl.when(pl.program_id(0) == 0)
        def _():
            for o in outs[ng:]:
                o[...] = jnp.zeros_like(o)

        for o, g in zip(outs[ng:], grads[nr:]):
            o[...] += g

    return pl.pallas_call(
        body, grid=(n_rows // tm,), name=name,
        in_specs=list(specs) + [pl.BlockSpec(c.shape, lambda i: (0, 0)) for c in consts] + list(ct_specs)
        + [ANY_SPEC] * nd,
        out_specs=[pl.BlockSpec((tm, w), lambda i: (i, 0)) for w in widths]
        + [pl.BlockSpec(c.shape, lambda i: (0, 0)) for c in consts],
        out_shape=[SDS((n_rows, w), dt) for w, dt in zip(widths, grad_dtypes)] + [SDS(c.shape, F32) for c in consts],
        compiler_params=_cp("arbitrary"),
    )(*arrs, *consts, *ct_arrs, *deps)


def _tile(n, pref):
    t = min(n, pref)
    while n % t or (t % HD and t != n):
        t -= 1
    return t


def _matmul(name, a, b, dims, out_dtype, tm, tn, residual=None, deps=()):
    ta, tb = dims == TN, dims == NT
    m = a.shape[1] if ta else a.shape[0]
    k = a.shape[0] if ta else a.shape[1]
    n = b.shape[0] if tb else b.shape[1]
    tm, tn = _tile(m, tm), _tile(n, tn)

    def body(*refs):
        acc = _dot(refs[0][...], refs[1][...], dims)
        if residual is not None:
            acc = acc + refs[2][...]
        refs[-1][...] = acc.astype(out_dtype)

    in_specs = [pl.BlockSpec((k, tm), lambda i, j: (0, i)) if ta else pl.BlockSpec((tm, k), lambda i, j: (i, 0)),
                pl.BlockSpec((tn, k), lambda i, j: (j, 0)) if tb else pl.BlockSpec((k, tn), lambda i, j: (0, j))]
    ops = [a, b]
    if residual is not None:
        in_specs.append(pl.BlockSpec((tm, tn), lambda i, j: (i, j)))
        ops.append(residual)
    in_specs += [ANY_SPEC] * len(deps)
    ops += list(deps)
    return pl.pallas_call(
        body, grid=(m // tm, n // tn), name=name, in_specs=in_specs,
        out_specs=pl.BlockSpec((tm, tn), lambda i, j: (i, j)), out_shape=SDS((m, n), out_dtype),
        compiler_params=_cp("parallel", "parallel"),
    )(*ops)


def _proj_out_norm(mix, w_out, xs, gain, deps):
    t, k = mix.shape
    d = w_out.shape[1]
    tm = _tile(t, 512)

    def body(*refs):
        a, b, x, g = refs[:4]
        h1, h1n = refs[4 + len(deps):]
        acc = _dot(a[...], b[...]) + x[...]
        h1[...] = acc
        h1n[...] = _rms(acc, g[...]).astype(BF16)

    return pl.pallas_call(
        body, grid=(t // tm,), name="proj_out",
        in_specs=[pl.BlockSpec((tm, k), lambda i: (i, 0)), pl.BlockSpec((k, d), lambda i: (0, 0)),
                  pl.BlockSpec((tm, d), lambda i: (i, 0)), pl.BlockSpec((1, d), lambda i: (0, 0))] + [ANY_SPEC] * len(deps),
        out_specs=[pl.BlockSpec((tm, d), lambda i: (i, 0))] * 2, out_shape=[SDS((t, d), F32), SDS((t, d), BF16)],
        compiler_params=_cp("parallel"),
    )(mix, w_out, xs, gain, *deps)


def _ffn_up(h1n, wgu):
    t, d = h1n.shape
    w = wgu.shape[3]
    tm = _tile(t, 512)

    def body(a, b, gu, act):
        x = a[...]
        g = _dot(x, b[0])
        u = _dot(x, b[1])
        gu[0] = g.astype(BF16)
        gu[1] = u.astype(BF16)
        act[...] = (_silu(g) * u).astype(BF16)

    return pl.pallas_call(
        body, grid=(4, t // tm), name="ffn_up",
        in_specs=[pl.BlockSpec((tm, d), lambda j, i: (i, 0)), pl.BlockSpec((2, None, d, w), lambda j, i: (0, j, 0, 0))],
        out_specs=[pl.BlockSpec((2, None, tm, w), lambda j, i: (0, j, i, 0)), pl.BlockSpec((tm, w), lambda j, i: (i, j))],
        out_shape=[SDS((2, 4, t, w), BF16), SDS((t, 4 * w), BF16)],
        compiler_params=_cp("parallel", "parallel"),
    )(h1n, wgu)


def _ffn_down_loss(act, wdown, h1, target):
    t, f = act.shape
    d = wdown.shape[1]
    tm, tn = _tile(t, 1024), _tile(d, 512)

    def body(a, b, h, tg, dyb, ls):
        e = _dot(a[...], b[...]) + h[...] - tg[...]
        dyb[...] = (e * (1.0 / d)).astype(BF16)
        ls[...] = jnp.broadcast_to(jnp.sum(e * e), (8, HD))

    return pl.pallas_call(
        body, grid=(t // tm, d // tn), name="ffn_down_loss",
        in_specs=[pl.BlockSpec((tm, f), lambda i, j: (i, 0)), pl.BlockSpec((f, tn), lambda i, j: (0, j)),
                  pl.BlockSpec((tm, tn), lambda i, j: (i, j)), pl.BlockSpec((tm, tn), lambda i, j: (i, j))],
        out_specs=[pl.BlockSpec((tm, tn), lambda i, j: (i, j)), pl.BlockSpec((8, HD), lambda i, j: (i, j))],
        out_shape=[SDS((t, d), BF16), SDS((8 * (t // tm), HD * (d // tn)), F32)],
        compiler_params=_cp("parallel", "parallel"),
    )(act, wdown, h1, target)


def _ffn_down_bwd(dyb, wdown4, gu):
    t, d = dyb.shape
    w = wdown4.shape[1]
    tm = _tile(t, 512)

    def body(a, b, gu_ref, out):
        da = _dot(a[...], b[...], NT)
        g = gu_ref[0].astype(F32)
        u = gu_ref[1].astype(F32)
        s = _sigmoid(g)
        out[0] = (da * u * (s * (1.0 + g * (1.0 - s)))).astype(BF16)
        out[1] = (da * g * s).astype(BF16)

    return pl.pallas_call(
        body, grid=(4, t // tm), name="ffn_down_bwd",
        in_specs=[pl.BlockSpec((tm, d), lambda j, i: (i, 0)), pl.BlockSpec((None, w, d), lambda j, i: (j, 0, 0)),
                  pl.BlockSpec((2, None, tm, w), lambda j, i: (0, j, i, 0))],
        out_specs=pl.BlockSpec((2, None, tm, w), lambda j, i: (0, j, i, 0)),
        out_shape=SDS((2, 4, t, w), BF16),
        compiler_params=_cp("parallel", "parallel"),
    )(dyb, wdown4, gu)


def _ffn_up_bwd_x(dgu, wgu, h1, gain, dyb, deps):
    _, t, w = dgu.shape
    d = wgu.shape[1]
    tm = _tile(t, 512)

    def body(*refs):
        a, b, h, g, dy = refs[:5]
        dh1, dgain, acc = refs[5 + len(deps):]
        i, j = pl.program_id(0), pl.program_id(1)

        @pl.when(j == 0)
        def _():
            acc[...] = jnp.zeros_like(acc)

        acc[...] += _dot(a[...], b[...], NT)

        @pl.when(j == N_DEV - 1)
        def _():
            _, vjp = jax.vjp(lambda x, gn: _rms(x, gn), h[...], g[...])
            dx, dg = vjp(acc[...])
            dh1[...] = (dx + dy[...].astype(F32)).astype(dh1.dtype)

            @pl.when(i == 0)
            def _():
                dgain[...] = jnp.zeros_like(dgain)

            dgain[...] += dg

    row = pl.BlockSpec((tm, d), lambda i, j: (i, 0))
    return pl.pallas_call(
        body, grid=(t // tm, N_DEV), name="ffn_up_bwd_x",
        in_specs=[pl.BlockSpec((None, tm, w), lambda i, j: (j, i, 0)), pl.BlockSpec((None, d, w), lambda i, j: (j, 0, 0)),
                  row, pl.BlockSpec((1, d), lambda i, j: (0, 0)), row] + [ANY_SPEC] * len(deps),
        out_specs=[row, pl.BlockSpec((1, d), lambda i, j: (0, 0))],
        out_shape=[SDS((t, d), BF16), SDS((1, d), F32)], scratch_shapes=[pltpu.VMEM((tm, d), F32)],
        compiler_params=_cp("arbitrary", "arbitrary"),
    )(dgu, wgu, h1, gain, dyb, *deps)


def _ffn_up_bwd_w(h1n, dgu):
    _, t, w = dgu.shape
    d = h1n.shape[1]
    tm = _tile(d, 512)

    def body(a, b, out):
        out[...] = _dot(a[...], b[...], TN).astype(BF16)

    return pl.pallas_call(
        body, grid=(8, d // tm), name="ffn_up_bwd_w",
        in_specs=[pl.BlockSpec((t, tm), lambda j, i: (0, i)), pl.BlockSpec((None, t, w), lambda j, i: (j, 0, 0))],
        out_specs=pl.BlockSpec((None, tm, w), lambda j, i: (j, i, 0)), out_shape=SDS((8, d, w), BF16),
        compiler_params=_cp("parallel", "parallel"),
    )(h1n, dgu)


def _fox_prep(fq, fk, sm, fb, qg, kg, h):
    qn = _rms(fq, qg)
    kn = _rms(fk, kg)
    c = _cumsum_rows(-_softplus(-(sm + fb)))
    ccol = _lane_pick(c, L_FF + h)
    crow = jnp.sum(c.T * (_iota((HD, 1), 0) == L_FF + h).astype(F32), axis=0, keepdims=True)
    return qn, kn, ccol, crow


def _softmax_times(s, v):
    e = jnp.exp(s - lax.stop_gradient(jnp.max(s, axis=1, keepdims=True)))
    return _dot(e.astype(BF16), v.astype(BF16)) * (1.0 / jnp.sum(e, axis=1, keepdims=True))


def _fox_block(q, k, v, cc, cr, off):
    bq = q.shape[0]
    assert k.shape[0] == off + bq
    s = _dot((q * (HD ** -0.5)).astype(BF16), k.astype(BF16), NT) + cc - cr
    diag = jnp.where(_iota((bq, bq), 1) <= _iota((bq, bq), 0), s[:, off:], -1e30)
    s = jnp.concatenate([s[:, :off], diag], axis=1) if off else diag
    return _softmax_times(s, v)


ONE_BUFFER = pl.Buffered(1)


def _pcol(t, cb):
    return pl.BlockSpec((t, HD), lambda h, cb=cb: (0, cb + h), pipeline_mode=ONE_BUFFER)


def _smcol(t):
    return pl.BlockSpec((t, HD), lambda h: (0, SM), pipeline_mode=ONE_BUFFER)


def _head(t):
    return pl.BlockSpec((t, HD), lambda h: (0, h), pipeline_mode=ONE_BUFFER)


def _small(n):
    return pl.BlockSpec((n, HD), lambda h: (0, 0), pipeline_mode=ONE_BUFFER)


def _fox_fwd(p, fb, qg, kg, bq):
    t = p.shape[0]

    def body(fq, fk, fv, sm, fb_r, qg_r, kg_r, o, qn_s, cc_s):
        h = pl.program_id(0)
        qn, kn, ccol, crow = _fox_prep(fq[...], fk[...], sm[...], fb_r[...], qg_r[...], kg_r[...], h)
        qn_s[...] = qn
        cc_s[...] = ccol
        knb = kn.astype(BF16)
        vb = fv[...].astype(BF16)
        for i in range(t // bq):
            rows, ext = pl.ds(i * bq, bq), (i + 1) * bq
            o[rows, :] = _fox_block(qn_s[rows, :], knb[:ext], vb[:ext], cc_s[rows, :], crow[:, :ext], i * bq).astype(o.dtype)

    return pl.pallas_call(
        body, grid=(NF,), name="fox_fwd",
        in_specs=[_pcol(t, FQ), _pcol(t, FK), _pcol(t, FV), _smcol(t), _small(1), _small(1), _small(1)],
        out_specs=_head(t), out_shape=SDS((t, NF * HD), BF16),
        scratch_shapes=[pltpu.VMEM((t, HD), F32), pltpu.VMEM((t, 1), F32)],
        compiler_params=_cp("parallel"),
    )(p, p, p, p, fb, qg, kg)


def _fox_bwd(p, fb, qg, kg, dmix, bq, deps=()):
    t = p.shape[0]

    def body(*refs):
        fq, fk, fv, sm, fb_r, qg_r, kg_r, do = refs[:8]
        dfq, dfk, dfv, dsm, dfb, dqg, dkg, qn_s, cc_s, dqn_s, dcc_s, dkn_s, dv_s, dcr_s = refs[8 + len(deps):]
        h = pl.program_id(0)
        qn, kn, ccol, crow = _fox_prep(fq[...], fk[...], sm[...], fb_r[...], qg_r[...], kg_r[...], h)
        qn_s[...] = qn
        cc_s[...] = ccol
        v = fv[...]
        dkn_s[...] = jnp.zeros_like(dkn_s)
        dv_s[...] = jnp.zeros_like(dv_s)
        dcr_s[...] = jnp.zeros_like(dcr_s)

        for i in range(t // bq):
            rows, ext = pl.ds(i * bq, bq), (i + 1) * bq
            _, vjp = jax.vjp(lambda a, b, c, d, e, off=i * bq: _fox_block(a, b, c, d, e, off),
                             qn_s[rows, :], kn[:ext], v[:ext], cc_s[rows, :], crow[:, :ext])
            dq, dk, dv, dcc, dcr = vjp(do[rows, :].astype(F32))
            dqn_s[rows, :] = dq
            dcc_s[rows, :] = dcc
            dkn_s[:ext, :] += dk
            dv_s[:ext, :] += dv
            dcr_s[:, :ext] += dcr
        _, prep_vjp = jax.vjp(lambda a, b, c, d, e, f: _fox_prep(a, b, c, d, e, f, h),
                              fq[...], fk[...], sm[...], fb_r[...], qg_r[...], kg_r[...])
        g_fq, g_fk, g_sm, g_fb, g_qg, g_kg = prep_vjp((dqn_s[...], dkn_s[...], dcc_s[...], dcr_s[...]))
        dfq[...] = g_fq.astype(dfq.dtype)
        dfk[...] = g_fk.astype(dfk.dtype)
        dfv[...] = dv_s[...].astype(dfv.dtype)

        @pl.when(h == 0)
        def _():
            for r in (dsm, dfb, dqg, dkg):
                r[...] = jnp.zeros_like(r)

        dsm[...] += g_sm
        dfb[...] += g_fb
        dqg[...] += g_qg
        dkg[...] += g_kg

    head = _head(t)
    return pl.pallas_call(
        body, grid=(NF,), name="fox_bwd",
        in_specs=[_pcol(t, FQ), _pcol(t, FK), _pcol(t, FV), _smcol(t), _small(1), _small(1), _small(1), head]
        + [ANY_SPEC] * len(deps),
        out_specs=[head, head, head, _small(t), _small(1), _small(1), _small(1)],
        out_shape=[SDS((t, NF * HD), BF16)] * 3 + [SDS((t, HD), F32)] + [SDS((1, HD), F32)] * 3,
        scratch_shapes=[pltpu.VMEM((t, HD), F32), pltpu.VMEM((t, 1), F32), pltpu.VMEM((t, HD), F32),
                        pltpu.VMEM((t, 1), F32), pltpu.VMEM((t, HD), F32), pltpu.VMEM((t, HD), F32),
                        pltpu.VMEM((1, t), F32)],
        compiler_params=_cp("arbitrary"),
    )(p, p, p, p, fb, qg, kg, dmix, *deps)


def _mem_attn(mq, mk, mv, qg, kg):
    s = _dot((_rms(mq, qg) * (HD ** -0.5)).astype(BF16), _rms(mk, kg).astype(BF16), NT)
    return _softmax_times(s, mv)


def _mem_fwd(p, mkv, qg, kg):
    t, ml = p.shape[0], mkv.shape[0]

    def body(mq, mk, mv, qg_r, kg_r, o):
        o[...] = _mem_attn(mq[...], mk[...], mv[...], qg_r[...], kg_r[...]).astype(o.dtype)

    return pl.pallas_call(
        body, grid=(NM,), name="mem_fwd",
        in_specs=[_pcol(t, MQ), pl.BlockSpec((ml, HD), lambda h: (0, h)), pl.BlockSpec((ml, HD), lambda h: (0, NM + h)),
                  _small(1), _small(1)],
        out_specs=pl.BlockSpec((t, HD), lambda h: (0, h)), out_shape=SDS((t, NM * HD), BF16),
        compiler_params=_cp("parallel"),
    )(p, mkv, mkv, qg, kg)


def _mem_bwd(p, mkv, qg, kg, dmix, deps=()):
    t, ml = p.shape[0], mkv.shape[0]

    def body(*refs):
        mq, mk, mv, qg_r, kg_r, do = refs[:6]
        dmq, dmk, dmv, dqg, dkg = refs[6 + len(deps):]
        _, vjp = jax.vjp(_mem_attn, mq[...], mk[...], mv[...], qg_r[...], kg_r[...])
        g_q, g_k, g_v, g_qg, g_kg = vjp(do[...].astype(F32))
        dmq[...] = g_q.astype(dmq.dtype)
        dmk[...] = g_k
        dmv[...] = g_v

        @pl.when(pl.program_id(0) == 0)
        def _():
            dqg[...] = jnp.zeros_like(dqg)
            dkg[...] = jnp.zeros_like(dkg)

        dqg[...] += g_qg
        dkg[...] += g_kg

    return pl.pallas_call(
        body, grid=(NM,), name="mem_bwd",
        in_specs=[_pcol(t, MQ), pl.BlockSpec((ml, HD), lambda h: (0, h)), pl.BlockSpec((ml, HD), lambda h: (0, NM + h)),
                  _small(1), _small(1), pl.BlockSpec((t, HD), lambda h: (0, NF + NG + h))] + [ANY_SPEC] * len(deps),
        out_specs=[pl.BlockSpec((t, HD), lambda h: (0, h)), pl.BlockSpec((ml, HD), lambda h: (0, h)),
                   pl.BlockSpec((ml, HD), lambda h: (0, h)), _small(1), _small(1)],
        out_shape=[SDS((t, NM * HD), BF16), SDS((ml, NM * HD), F32), SDS((ml, NM * HD), F32),
                   SDS((1, HD), F32), SDS((1, HD), F32)],
        compiler_params=_cp("arbitrary"),
    )(p, mkv, mkv, qg, kg, dmix, *deps)


def _shift_down(x, s):
    if s == 0:
        return x
    return jnp.where(_iota(x.shape, 0) >= s, pltpu.roll(x, s, 0), 0.0)


def _shift_up(x, s):
    if s == 0:
        return x
    n = x.shape[0]
    return jnp.where(_iota(x.shape, 0) < n - s, pltpu.roll(x, n - s, 0), 0.0)


@jax.custom_vjp
def _conv4(x, w0, w1, w2, w3):
    return w0 * _shift_down(x, 3) + w1 * _shift_down(x, 2) + w2 * _shift_down(x, 1) + w3 * x


def _conv4_fwd(x, w0, w1, w2, w3):
    return _conv4(x, w0, w1, w2, w3), (x, w0, w1, w2, w3)


def _conv4_bwd(res, dy):
    x, w0, w1, w2, w3 = res
    ups = [_shift_up(dy, 3 - k) for k in range(4)]
    dx = w0 * ups[0] + w1 * ups[1] + w2 * ups[2] + w3 * ups[3]
    return (dx,) + tuple(jnp.sum(up * x, axis=0, keepdims=True) for up in ups)


_conv4.defvjp(_conv4_fwd, _conv4_bwd)


HALO = 8


def _gdn_gates(sm, alog, dtb):
    lane = _iota((1, HD), 1)
    g = -jnp.exp(alog) * _softplus(sm + dtb)
    return (jnp.where((lane >= L_GA) & (lane < L_GA + NG), g,
                      jnp.where((lane >= L_GB) & (lane < L_GB + NG), _sigmoid(sm), 0.0)),)


def _gdn_prep(gq, gk, gv, gates, taps, h):
    q, k, v = [_silu(_conv4(x, *taps[4 * j:4 * j + 4]))[HALO:] for j, x in enumerate((gq, gk, gv))]
    q = q * lax.rsqrt(jnp.sum(q * q, axis=-1, keepdims=True) + NORM_EPS) * (HD ** -0.5)
    k = k * lax.rsqrt(jnp.sum(k * k, axis=-1, keepdims=True) + NORM_EPS)
    return q, k, v, _lane_pick(gates, L_GA + h), _lane_pick(gates, L_GB + h)


def _split(x, n):
    parts, rest = [], x
    for i in range(n):
        parts.append(rest.astype(BF16))
        if i + 1 < n:
            rest = rest - parts[-1].astype(F32)
    return parts


def _raw_dot(a, b, form):
    lead = a.ndim - 2
    ca, cb = {"nn": (1, 0), "nt": (1, 1), "tn": (0, 0)}[form]
    batch = ((0,), (0,)) if lead else ((), ())
    return lax.dot_general(a, b, (((ca + lead,), (cb + lead,)), batch), preferred_element_type=F32)


def _pdot_impl(a, b, form, mode):
    if mode == "1":
        return _raw_dot(a.astype(BF16), b.astype(BF16), form)
    if mode == "3":
        (ah, al), (bh, bl) = _split(a, 2), _split(b, 2)
        return _raw_dot(ah, bh, form) + (_raw_dot(al, bh, form) + _raw_dot(ah, bl, form))
    if mode == "xa":
        return sum(_raw_dot(a.astype(BF16), t, form) for t in reversed(_split(b, 3)))
    return sum(_raw_dot(t, b.astype(BF16), form) for t in reversed(_split(a, 3)))


@functools.partial(jax.custom_vjp, nondiff_argnums=(2, 3))
def _pdot(a, b, form, mode):
    return _pdot_impl(a, b, form, mode)


def _pdot_fwd(a, b, form, mode):
    return _pdot_impl(a, b, form, mode), (a, b)


def _pdot_bwd(form, mode, res, ct):
    a, b = res
    da_args, db_args = {"nn": ((ct, b, "nt"), (a, ct, "tn")), "nt": ((ct, b, "nn"), (ct, a, "tn")),
                        "tn": ((b, ct, "nt"), (a, ct, "nn"))}[form]

    def side(args, exact):
        if mode in ("1", "3"):
            return mode
        return "xa" if args[0] is exact else "xb"

    if mode == "xa":
        return jnp.zeros_like(a), _pdot_impl(*db_args, side(db_args, a))
    if mode == "xb":
        return _pdot_impl(*da_args, side(da_args, b)), jnp.zeros_like(b)
    return _pdot_impl(*da_args, mode), _pdot_impl(*db_args, mode)


_pdot.defvjp(_pdot_fwd, _pdot_bwd)

GDN_QK, GDN_INV, GDN_SCAN = "1", "1", "1"


@jax.custom_vjp
def _tri_inv(low):
    eye = (_iota((CHUNK, CHUNK), 0) == _iota((CHUNK, CHUNK), 1)).astype(F32)
    inv = eye - low
    pw = low
    for _ in range(5):
        pw = _pdot_impl(pw, pw, "nn", GDN_INV)
        inv = inv + _pdot_impl(inv, pw, "nn", GDN_INV)
    return inv


def _tri_inv_fwd(low):
    inv = _tri_inv(low)
    return inv, inv


def _tri_inv_bwd(inv, ct):
    return (-_pdot_impl(_pdot_impl(inv, ct, "tn", GDN_INV), inv, "nt", GDN_INV),)


_tri_inv.defvjp(_tri_inv_fwd, _tri_inv_bwd)


def _gdn_intra(q, k, v, g, beta):
    n = q.shape[0]
    r, c = _iota((CHUNK, CHUNK), 0), _iota((CHUNK, CHUNK), 1)
    tril, strict = r >= c, r > c
    trilf = jnp.broadcast_to(tril.astype(F32), (n, CHUNK, CHUNK))
    gcm = _pdot(trilf, jnp.broadcast_to(g, (n, CHUNK, CHUNK)), "nn", "xa")
    gcf = _pdot(trilf, jnp.broadcast_to(g, (n, CHUNK, HD)), "nn", "xa")
    lane0 = (_iota((1, 1, CHUNK), 2) == 0).astype(F32)
    gcr = _pdot(jnp.ones((n, CHUNK, CHUNK), F32), gcm * lane0, "nt", "xa")
    decay = jnp.where(tril, jnp.exp(jnp.where(tril, gcm - gcr, 0.0)), 0.0)
    egc = jnp.exp(gcf)
    kb = k * beta
    low = jnp.where(strict, _pdot(kb, k, "nt", GDN_QK) * decay, 0.0)
    inv = _tri_inv(low)
    u = _pdot(inv, v * beta, "nn", GDN_INV)
    w = _pdot(inv, kb * egc, "nn", GDN_INV)
    at = jnp.where(tril, _pdot(q, k, "nt", GDN_QK) * decay, 0.0)
    gl = jnp.sum(jnp.broadcast_to(g, (n, CHUNK, HD)), axis=1, keepdims=True)
    kd = k * jnp.exp(gl - gcf)
    return (_pdot(kd, w, "tn", GDN_SCAN), _pdot(kd, u, "tn", GDN_SCAN), q * egc - _pdot(at, w, "nn", GDN_SCAN),
            _pdot(at, u, "nn", GDN_SCAN), gl)


def _gdn_step(s, kw, ku, a, b, gl):
    return _pdot(a, s, "nn", GDN_SCAN) + b, s * jnp.exp(gl) - _pdot(kw, s, "nn", GDN_SCAN) + ku


SCAN_HEADS = 3
SCAN_UNROLL = 4


def _gdn_chunked_scratch(nc):
    big = pltpu.VMEM((nc, CHUNK, HD), F32)
    return [big, big, big, pltpu.VMEM((nc, CHUNK, 1), F32), pltpu.VMEM((nc, CHUNK, 1), F32)]


N_TERMS = 5


def _gdn_term_shapes(nc):
    return [(nc, HD, HD), (nc, HD, HD), (nc, CHUNK, HD), (nc, CHUNK, HD), (nc, 1, HD)]


def _per_head(shape, heads=None, one_buffer=True):
    lead = (None,) if heads is None else (heads,)
    return pl.BlockSpec(lead + tuple(shape), lambda h: (h,) + (0,) * len(shape),
                        pipeline_mode=ONE_BUFFER if one_buffer else None)


def _gdn_in_specs(t):
    cw = lambda cb: pl.BlockSpec((4, HD), lambda h, cb=cb: (0, cb + h))
    return [_pcol(t, GQ), _pcol(t, GK), _pcol(t, GV), _small(t), cw(0), cw(NG), cw(2 * NG)]


def _taps(wq, wk, wv):
    return tuple(w[k:k + 1, :] for w in (wq, wk, wv) for k in range(4))


def _prep_rows(t):
    return min(t, 256)


def _gdn_pad(srcs, pads):
    for src, pad in zip(srcs, pads):
        pad[0:HALO, :] = jnp.zeros((HALO, HD), F32)
        pad[HALO:, :] = src[...]


def _gdn_stage(pads, gates, taps, h, chunked):
    t = gates.shape[0]
    rows = _prep_rows(t)
    per = rows // CHUNK

    def tile(i, carry):
        r0 = pl.multiple_of(i * rows, rows)
        vals = _gdn_prep(*[p[pl.ds(r0, rows + HALO), :] for p in pads], gates[pl.ds(r0, rows), :], taps, h)
        for v, r in zip(vals, chunked):
            r[pl.ds(i * per, per)] = v.reshape(per, CHUNK, v.shape[-1])
        return carry

    lax.fori_loop(0, t // rows, tile, 0)


def _gdn_intra_all(chunked, intra):
    nc = chunked[0].shape[0]
    grp_n = math.gcd(nc, GROUP)

    def grp(i, carry):
        sl = pl.ds(pl.multiple_of(i * grp_n, grp_n), grp_n)
        for r, val in zip(intra, _gdn_intra(*[c[sl] for c in chunked])):
            r[sl] = val
        return carry

    lax.fori_loop(0, nc // grp_n, grp, 0)


def _gdn_fwd(pa, gates, conv):
    t = pa.shape[0]
    nc = t // CHUNK
    terms = _gdn_term_shapes(nc)

    def body(gq, gk, gv, gt, wq, wk, wv, *rest):
        h = pl.program_id(0)
        intra, chunked, pads = rest[:N_TERMS], rest[N_TERMS:N_TERMS + 5], rest[N_TERMS + 5:]
        _gdn_pad((gq, gk, gv), pads)
        _gdn_stage(pads, gt, _taps(wq, wk, wv), h, chunked)
        _gdn_intra_all(chunked, intra)

    return pl.pallas_call(
        body, grid=(NG,), name="gdn_fwd", in_specs=_gdn_in_specs(t),
        out_specs=[_per_head(sh, one_buffer=False) for sh in terms], out_shape=[SDS((NG,) + sh, F32) for sh in terms],
        scratch_shapes=_gdn_chunked_scratch(nc) + [pltpu.VMEM((t + HALO, HD), F32)] * 3, compiler_params=_cp("parallel"),
    )(pa, pa, pa, gates, conv, conv, conv)


def _gdn_scan(terms_in):
    nc = terms_in[0].shape[1]
    terms = _gdn_term_shapes(nc)

    def body(*refs):
        intra, o, states = refs[:N_TERMS], refs[N_TERMS], refs[N_TERMS + 1]

        def one(c, ss):
            rows = pl.ds(pl.multiple_of(c * CHUNK, CHUNK), CHUNK)
            loaded = [[r[hh, c] for r in intra] for hh in range(SCAN_HEADS)]
            res = [_gdn_step(ss[hh], *loaded[hh]) for hh in range(SCAN_HEADS)]
            for hh in range(SCAN_HEADS):
                states[hh, c] = ss[hh]
                o[rows, hh * HD:(hh + 1) * HD] = res[hh][0]
            return tuple(r[1] for r in res)

        def step(i, ss):
            for k in range(SCAN_UNROLL):
                ss = one(SCAN_UNROLL * i + k, ss)
            return ss

        lax.fori_loop(0, nc // SCAN_UNROLL, step, tuple(jnp.zeros((HD, HD), F32) for _ in range(SCAN_HEADS)))

    return pl.pallas_call(
        body, grid=(NG // SCAN_HEADS,), name="gdn_scan", in_specs=[_per_head(sh, SCAN_HEADS) for sh in terms],
        out_specs=[pl.BlockSpec((nc * CHUNK, SCAN_HEADS * HD), lambda h: (0, h), pipeline_mode=ONE_BUFFER),
                   _per_head((nc, HD, HD), SCAN_HEADS)],
        out_shape=[SDS((nc * CHUNK, NG * HD), F32), SDS((NG, nc, HD, HD), F32)], compiler_params=_cp("parallel"),
    )(*terms_in)


def _gdn_bwd_scan(saved, do_raw):
    nc = saved[0].shape[1]
    terms = _gdn_term_shapes(nc)

    def body(*refs):
        intra, states, do, outs = refs[:N_TERMS], refs[N_TERMS], refs[N_TERMS + 1], refs[N_TERMS + 2:]

        def one(c, dss):
            rows = pl.ds(pl.multiple_of(c * CHUNK, CHUNK), CHUNK)
            loaded = [[states[hh, c]] + [r[hh, c] for r in intra] for hh in range(SCAN_HEADS)]
            cts = [do[rows, hh * HD:(hh + 1) * HD] for hh in range(SCAN_HEADS)]
            grads = [jax.vjp(_gdn_step, *loaded[hh])[1]((cts[hh], dss[hh])) for hh in range(SCAN_HEADS)]
            for hh in range(SCAN_HEADS):
                for r, gval in zip(outs, grads[hh][1:]):
                    r[hh, c] = gval
            return tuple(g[0] for g in grads)

        def bwd(i, dss):
            c = nc - 1 - SCAN_UNROLL * i
            for k in range(SCAN_UNROLL):
                dss = one(c - k, dss)
            return dss

        lax.fori_loop(0, nc // SCAN_UNROLL, bwd, tuple(jnp.zeros((HD, HD), F32) for _ in range(SCAN_HEADS)))

    return pl.pallas_call(
        body, grid=(NG // SCAN_HEADS,), name="gdn_bwd_scan",
        in_specs=[_per_head(sh, SCAN_HEADS) for sh in terms] + [_per_head((nc, HD, HD), SCAN_HEADS)]
        + [pl.BlockSpec((nc * CHUNK, SCAN_HEADS * HD), lambda h: (0, h), pipeline_mode=ONE_BUFFER)],
        out_specs=[_per_head(sh, SCAN_HEADS) for sh in terms],
        out_shape=[SDS((NG,) + sh, F32) for sh in terms], compiler_params=_cp("parallel"),
    )(*saved, do_raw)


def _gdn_bwd(pa, gates, conv, dterms):
    t = pa.shape[0]
    nc = t // CHUNK
    terms = _gdn_term_shapes(nc)

    def body(*refs):
        gq, gk, gv, gt, wq, wk, wv = refs[:7]
        dintra = refs[7:7 + N_TERMS]
        dgq, dgk, dgv, dgt, dwq, dwk, dwv = refs[7 + N_TERMS:14 + N_TERMS]
        chunked, pads, dpads, dgt_s = (refs[14 + N_TERMS:19 + N_TERMS], refs[19 + N_TERMS:22 + N_TERMS],
                                       refs[22 + N_TERMS:25 + N_TERMS], refs[25 + N_TERMS])
        h = pl.program_id(0)
        taps = _taps(wq, wk, wv)
        _gdn_pad((gq, gk, gv), pads)
        _gdn_stage(pads, gt, taps, h, chunked)
        grp_n = math.gcd(nc, GROUP)

        def grp(i, carry):
            sl = pl.ds(pl.multiple_of(i * grp_n, grp_n), grp_n)
            _, vjp = jax.vjp(_gdn_intra, *[r[sl] for r in chunked])
            for r, gval in zip(chunked, vjp(tuple(r[sl] for r in dintra))):
                r[sl] = gval
            return carry

        lax.fori_loop(0, nc // grp_n, grp, 0)

        rows = _prep_rows(t)
        per = rows // CHUNK
        for r in dpads:
            r[...] = jnp.zeros_like(r)

        def tile(i, dtaps):
            r0 = pl.multiple_of(i * rows, rows)
            win = pl.ds(r0, rows + HALO)
            _, vjp = jax.vjp(lambda *a: _gdn_prep(*a, h), *[p[win, :] for p in pads], gt[pl.ds(r0, rows), :], taps)
            grads = vjp(tuple(r[pl.ds(i * per, per)].reshape(rows, r.shape[-1]) for r in chunked))
            for r, gval in zip(dpads, grads[:3]):
                r[win, :] += gval
            dgt_s[pl.ds(r0, rows), :] = grads[3]
            return jax.tree.map(jnp.add, dtaps, grads[4])

        dtaps = lax.fori_loop(0, t // rows, tile, (jnp.zeros((1, HD), F32),) * 12)
        for r, dpad in zip((dgq, dgk, dgv), dpads):
            r[...] = dpad[HALO:, :].astype(r.dtype)
        for j, r in enumerate((dwq, dwk, dwv)):
            for k in range(4):
                r[k:k + 1, :] = dtaps[4 * j + k]

        @pl.when(h == 0)
        def _():
            dgt[...] = jnp.zeros_like(dgt)

        dgt[...] += dgt_s[...]

    head = _head(t)
    taps = pl.BlockSpec((4, HD), lambda h: (0, h))
    return pl.pallas_call(
        body, grid=(NG,), name="gdn_bwd", in_specs=_gdn_in_specs(t) + [_per_head(sh) for sh in terms],
        out_specs=[head, head, head, _small(t), taps, taps, taps],
        out_shape=[SDS((t, NG * HD), BF16)] * 3 + [SDS((t, HD), F32)] + [SDS((4, NG * HD), F32)] * 3,
        scratch_shapes=_gdn_chunked_scratch(nc) + [pltpu.VMEM((t + HALO, HD), F32)] * 6 + [pltpu.VMEM((t, HD), F32)],
        compiler_params=_cp("arbitrary"),
    )(pa, pa, pa, gates, conv, conv, conv, *dterms)


def _gdn_post(o, z, gain):
    return (jnp.concatenate(
        [_rms(o[:, h * HD:(h + 1) * HD], gain) * _silu(z[:, h * HD:(h + 1) * HD]) for h in range(NG)], axis=1),)


def _place():
    return lax.axis_index("x"), lax.axis_index("y"), lax.axis_index("c")


def _sum_blocks(name, parts):
    _, r, c = parts.shape
    tr = 64 if r % 64 == 0 else r

    def body(x, o):
        acc = x[0].astype(F32)
        for d in range(1, N_DEV):
            acc = acc + x[d].astype(F32)
        o[...] = acc

    return pl.pallas_call(
        body, grid=(r // tr,), name=name, in_specs=[pl.BlockSpec((N_DEV, tr, c), lambda i: (0, i, 0))],
        out_specs=pl.BlockSpec((tr, c), lambda i: (i, 0)), out_shape=SDS((r, c), F32), compiler_params=_cp("parallel"),
    )(parts)


def _all_reduce_small(name, x, reduce):
    m_per, n = x.shape

    def body(x_ref, out_ref, send_sems, recv_sems, local_sem):
        px, py, pc = _place()
        me, sibling = (px, py, pc), (px, py, 1 - pc)
        chips = [(1 - px, py), (px, 1 - py), (1 - px, 1 - py)]
        buf = out_ref

        def rows(qx, qy, qc):
            return buf.at[pl.ds((4 * qx + 2 * qy + qc) * m_per, m_per), :]

        def copy(k, block, to, src=None):
            return pltpu.make_async_remote_copy(
                src_ref=rows(*block) if src is None else src, dst_ref=rows(*block),
                send_sem=send_sems.at[k], recv_sem=recv_sems.at[k], device_id=to, device_id_type=MESH)

        mine = pltpu.make_async_copy(x_ref, rows(*me), local_sem)
        mine.start()
        first = [copy(0, me, sibling, src=x_ref)]
        first += [copy(1 + j, me, (*chip, pc), src=x_ref) for j, chip in enumerate(chips)]
        for cp in first:
            cp.start()
        passed = [copy(4 + j, (*chip, pc), sibling) for j, chip in enumerate(chips)]
        for j, chip in enumerate(chips):
            copy(1 + j, (*chip, pc), me).wait_recv()
            passed[j].start()
        copy(0, sibling, me).wait_recv()
        for j, chip in enumerate(chips):
            copy(4 + j, (*chip, 1 - pc), me).wait_recv()
        for cp in first + passed:
            cp.wait_send()
        mine.wait()

    gathered = pl.pallas_call(
        body, name=name, out_shape=SDS((N_DEV * m_per, n), x.dtype),
        in_specs=[pl.BlockSpec(memory_space=pltpu.VMEM)], out_specs=pl.BlockSpec(memory_space=pltpu.VMEM),
        scratch_shapes=[pltpu.SemaphoreType.DMA((7,)), pltpu.SemaphoreType.DMA((7,)), pltpu.SemaphoreType.DMA],
    )(x)
    if not reduce:
        return gathered
    return _sum_blocks(name + "_sum", gathered.reshape(N_DEV, m_per, n))


HBM_SPEC = pl.BlockSpec(memory_space=pltpu.HBM)
SEM_SPEC = pl.BlockSpec(memory_space=pltpu.SEMAPHORE)
EFFECT = pltpu.SideEffectType.DATAFLOW_SIDE_EFFECTING


def _copies_start(name, bufs, n_remote, n_local, build, deps):
    nb, nd = len(bufs), len(deps)
    sem_shapes = [pltpu.SemaphoreType.DMA((n_remote,)), pltpu.SemaphoreType.DMA((n_remote,))]
    if n_local:
        sem_shapes.append(pltpu.SemaphoreType.DMA((n_local,)))
    ns = len(sem_shapes)

    def body(*refs):
        sems = refs[nb + nd:nb + nd + ns]
        remote, local = build(refs[:nb], *sems, *([None] * (3 - ns)))
        for cp in local + remote:
            cp.start()
        refs[-1][...] = jnp.zeros((8, HD), F32)

    outs = pl.pallas_call(
        body, name=name,
        out_shape=(*sem_shapes, *[pltpu.HBM(b.shape, b.dtype) for b in bufs], SDS((8, HD), F32)),
        in_specs=[HBM_SPEC] * nb + [ANY_SPEC] * nd,
        out_specs=(*[SEM_SPEC] * ns, *[HBM_SPEC] * nb, pl.BlockSpec(memory_space=pltpu.VMEM)),
        input_output_aliases={i: ns + i for i in range(nb)},
        compiler_params=pltpu.CompilerParams(has_side_effects=EFFECT),
    )(*[pltpu.with_memory_space_constraint(b, pltpu.HBM) for b in bufs], *deps)
    return list(outs[:ns]), list(outs[ns:ns + nb]), outs[-1]


def _copies_wait(name, bufs, sems, build, after):
    nb, ns = len(bufs), len(sems)

    def body(*refs):
        remote, local = build(refs[:nb], *refs[nb:nb + ns], *([None] * (3 - ns)))
        for cp in local:
            cp.wait()
        for cp in remote:
            cp.wait_send()
            cp.wait_recv()

    outs = pl.pallas_call(
        body, name=name, out_shape=tuple(pltpu.HBM(b.shape, b.dtype) for b in bufs),
        in_specs=[HBM_SPEC] * nb + [SEM_SPEC] * ns + [ANY_SPEC] * len(after), out_specs=tuple([HBM_SPEC] * nb),
        input_output_aliases={i: i for i in range(nb)},
        compiler_params=pltpu.CompilerParams(has_side_effects=EFFECT),
    )(*bufs, *sems, *after)
    return list(outs)


def _remote(src, dst, send, recv, k, to):
    return pltpu.make_async_remote_copy(src_ref=src, dst_ref=dst, send_sem=send.at[k], recv_sem=recv.at[k],
                                        device_id=to, device_id_type=MESH)


class _Gather:
    def __init__(self, name, shards, deps):
        self.name, self.n = name, len(shards)
        lands = [lax.empty((N_DEV,) + s.shape, s.dtype) for s in shards]
        self.sems1, bufs, self.token = _copies_start(
            name + "_s1", list(shards) + lands, 4 * self.n, self.n, self._stage1(range(self.n)), deps)
        self.shards, self.lands, self.sems2 = bufs[:self.n], bufs[self.n:], {}

    def _stage1(self, idxs):
        def build(refs, send, recv, loc):
            x, y, c = _place()
            me = 4 * x + 2 * y + c
            targets = [(x, y, 1 - c), (1 - x, y, c), (x, 1 - y, c), (1 - x, 1 - y, c)]
            remote, local = [], []
            for pos, i in enumerate(idxs):
                src, land = refs[pos], refs[len(idxs) + pos]
                local.append(pltpu.make_async_copy(src, land.at[me], loc.at[i]))
                remote += [_remote(src, land.at[me], send, recv, 4 * i + k, to) for k, to in enumerate(targets)]
            return remote, local
        return build

    @staticmethod
    def _stage2(refs, send, recv, loc):
        x, y, c = _place()
        remote = []
        for pos, land in enumerate(refs):
            for j, (cx, cy) in enumerate([(1 - x, y), (x, 1 - y), (1 - x, 1 - y)]):
                blk = land.at[4 * cx + 2 * cy + c]
                remote.append(_remote(blk, blk, send, recv, 3 * pos + j, (x, y, 1 - c)))
        return remote, []

    def pass_on(self, idxs, after):
        tag, m = "".join(map(str, idxs)), len(idxs)
        bufs = _copies_wait(f"{self.name}_w1_{tag}", [self.shards[i] for i in idxs] + [self.lands[i] for i in idxs],
                            self.sems1, self._stage1(idxs), after)
        self.sems2[tag], lands, token = _copies_start(f"{self.name}_s2_{tag}", bufs[m:], 3 * m, 0, self._stage2, ())
        for pos, i in enumerate(idxs):
            self.lands[i] = lands[pos]
        return [token]

    def get(self, idxs, after):
        tag = "".join(map(str, idxs))
        return _copies_wait(f"{self.name}_w2_{tag}", [self.lands[i] for i in idxs], self.sems2[tag], self._stage2, after)


def _rows_tile(r, row_bytes, target=1 << 20):
    tr = r
    while tr % 32 == 0 and tr * row_bytes > target:
        tr //= 2
    return tr


def _pair_add(name, g, got, c):
    _, r, cols = g.shape
    tr = _rows_tile(r, cols * 2)

    def body(s, a, b, o):
        o[...] = (a[...].astype(F32) + b[...].astype(F32)).astype(o.dtype)

    return pl.pallas_call(
        body, name=name, out_shape=SDS((4, r, cols), g.dtype),
        grid_spec=pltpu.PrefetchScalarGridSpec(
            num_scalar_prefetch=1, grid=(4, r // tr),
            in_specs=[pl.BlockSpec((None, tr, cols), lambda j, i, s: (2 * j + s[0], i, 0)),
                      pl.BlockSpec((None, tr, cols), lambda j, i, s: (j, i, 0))],
            out_specs=pl.BlockSpec((None, tr, cols), lambda j, i, s: (j, i, 0))),
        compiler_params=_cp("parallel", "parallel"),
    )(c.reshape(1), g, got)


def _quad_sum(name, part, got, chip, wmv=None):
    _, r, cols = part.shape
    tr = _rows_tile(r, cols * 4)
    n_out = 4 if wmv else 1

    def body(s, a, b1, b2, b3, *rest):
        g = ((a[...].astype(F32) + b1[...].astype(F32)) + b2[...].astype(F32)) + b3[...].astype(F32)
        rest[-n_out][...] = g
        if wmv:
            w, m, v = rest[:3]
            rest[-3][...], rest[-2][...], rest[-1][...] = _adamw(w[...], g, m[...], v[...])

    blk = lambda k: pl.BlockSpec((None, tr, cols), lambda i, s, k=k: (jnp.bitwise_xor(s[0], k), i, 0))
    row = pl.BlockSpec((tr, cols), lambda i, s: (i, 0))
    outs = pl.pallas_call(
        body, name=name, out_shape=[SDS((r, cols), F32)] * n_out,
        grid_spec=pltpu.PrefetchScalarGridSpec(
            num_scalar_prefetch=1, grid=(r // tr,), in_specs=[blk(0), blk(1), blk(2), blk(3)] + [row] * (n_out - 1),
            out_specs=[row] * n_out),
        compiler_params=_cp("parallel"),
    )(chip.reshape(1), part, got, got, got, *(wmv or ()))
    return tuple(outs) if wmv else outs[0]


class _Scatter:
    def __init__(self, name, grads, deps):
        self.name, self.n = name, len(grads)
        got = [lax.empty((4,) + g.shape[1:], g.dtype) for g in grads]
        self.sems, bufs, self.token = _copies_start(name + "_s1", list(grads) + got, 4 * self.n, 0, self._stage1, deps)
        self.grads, self.got = bufs[:self.n], bufs[self.n:]

    def _stage1(self, refs, send, recv, loc):
        x, y, c = _place()
        remote = []
        for i in range(self.n):
            remote += [_remote(refs[i].at[2 * j + 1 - c], refs[self.n + i].at[j], send, recv, 4 * i + j, (x, y, 1 - c))
                       for j in range(4)]
        return remote, []

    def _stage2(self, refs, send, recv, loc):
        x, y, c = _place()
        remote = []
        for i in range(self.n):
            for k in (1, 2, 3):
                tx = 1 - x if k & 2 else x
                ty = 1 - y if k & 1 else y
                remote.append(_remote(refs[i].at[2 * tx + ty], refs[self.n + i].at[2 * x + y], send, recv,
                                      3 * i + k - 1, (tx, ty, c)))
        return remote, []

    def mid(self, after):
        bufs = _copies_wait(self.name + "_w1", self.grads + self.got, self.sems, self._stage1, after)
        c = lax.axis_index("c").astype(jnp.int32)
        parts = [_pair_add(f"{self.name}_add{i}", bufs[i], bufs[self.n + i], c) for i in range(self.n)]
        got = [lax.empty(p.shape, p.dtype) for p in parts]
        self.sems, bufs, self.token = _copies_start(self.name + "_s2", parts + got, 3 * self.n, 0, self._stage2, ())
        self.parts, self.got = bufs[:self.n], bufs[self.n:]

    def end(self, after, wmv=None):
        bufs = _copies_wait(self.name + "_w2", self.parts + self.got, self.sems, self._stage2, after)
        chip = (2 * lax.axis_index("x") + lax.axis_index("y")).astype(jnp.int32)
        wmv = wmv or [None] * self.n
        return [_quad_sum(f"{self.name}_sum{i}", bufs[i], bufs[self.n + i], chip, wmv[i]) for i in range(self.n)]


def _adamw(w, g, m, v):
    m = ADAM_B1 * m + (1.0 - ADAM_B1) * g
    v = ADAM_B2 * v + (1.0 - ADAM_B2) * (g * g)
    m_hat = m / (1.0 - ADAM_B1 ** ADAM_STEP)
    v_hat = v / (1.0 - ADAM_B2 ** ADAM_STEP)
    return -ADAM_LR * (m_hat / (jnp.sqrt(v_hat) + ADAM_EPS) + ADAM_WD * w), m, v


def _adamw_call(name, w, g, m, v):
    r, c = w.shape
    tm = 64 if r % 64 == 0 else r
    return _rowwise(name, _adamw, [w, g, m, v], [], [(c, F32)] * 3, tm)


_IN_COLS = 5906


def _perm_in(w):
    pad = jnp.zeros((w.shape[0], 2 * HALF - _IN_COLS), w.dtype)
    return (jnp.concatenate([w[:, 2310:4614], w[:, 4614:5382]], axis=1),
            jnp.concatenate([w[:, :2304], w[:, 5394:5906], w[:, 2304:2310], w[:, 5382:5394], pad], axis=1))


def _unperm_in(ga, gb):
    return jnp.concatenate([gb[:, :2304], gb[:, 2816:2822], ga[:, :2304], ga[:, 2304:3072], gb[:, 2822:2834],
                            gb[:, 2304:2816]], axis=1)


def _lanes(v, at):
    return jnp.pad(v, ((0, 0), (at, HD - at - v.shape[1])))


_PACK = ("norm_mix", "mem_norm", "norm_ffn", "gdn_conv", "fox_q_norm", "fox_k_norm", "gdn_out_norm", "mem_q_norm",
         "mem_k_norm", "fox_f_bias", "gdn_a_log", "gdn_dt_bias", "loss")


def _pack(vals):
    parts = [vals[n].reshape(-1, HD) for n in _PACK]
    used = sum(p.shape[0] for p in parts)
    buf = jnp.concatenate(parts + [jnp.zeros((-used % 8, HD), F32)], axis=0)
    return buf, [(n, p.shape[0]) for n, p in zip(_PACK, parts)]


def _unpack(buf, layout):
    out, at = {}, 0
    for n, rows in layout:
        out[n] = buf[at:at + rows]
        at += rows
    return out


def kernel(x, mem, norm_mix, w_in, fox_f_bias, fox_q_norm, fox_k_norm, gdn_conv, gdn_a_log, gdn_dt_bias, gdn_out_norm, mem_norm, w_mem_kv, mem_q_norm, mem_k_norm, w_out, norm_ffn, w_gate_up, w_down, loss_target, m_norm_mix, m_w_in, m_fox_f_bias, m_fox_q_norm, m_fox_k_norm, m_gdn_conv, m_gdn_a_log, m_gdn_dt_bias, m_gdn_out_norm, m_mem_norm, m_w_mem_kv, m_mem_q_norm, m_mem_k_norm, m_w_out, m_norm_ffn, m_w_gate_up, m_w_down, v_norm_mix, v_w_in, v_fox_f_bias, v_fox_q_norm, v_fox_k_norm, v_gdn_conv, v_gdn_a_log, v_gdn_dt_bias, v_gdn_out_norm, v_mem_norm, v_w_mem_kv, v_mem_q_norm, v_mem_k_norm, v_w_out, v_norm_ffn, v_w_gate_up, v_w_down):
    args = dict(locals())
    d = x.shape[2]
    me = 4 * lax.axis_index("x") + 2 * lax.axis_index("y") + lax.axis_index("c")

    cshard = gdn_conv[0].shape[1]
    conv_pad = jnp.pad(gdn_conv[0], ((0, 4), (0, 3 * HD - cshard)))
    conv_all = _all_reduce_small("ag_conv", conv_pad, False).reshape(N_DEV, 8, 3 * HD)[:, :4, :cshard]
    conv_all = conv_all.transpose(1, 0, 2).reshape(4, N_DEV * cshard)
    w_in_a, w_in_b = _perm_in(w_in[0])
    comm = _StepComm({"in_b": [w_in_b], "in_a": [w_in_a], "kv_out": [w_mem_kv[0], w_out[0]], "gate_up": [w_gate_up[0]],
                      "down": [w_down[0]]}, [conv_all])

    grad_x, loss_local, small_grads = _local_step(
        x[0], mem[0], loss_target[0], norm_mix, fox_f_bias, fox_q_norm, fox_k_norm, gdn_a_log, gdn_dt_bias,
        gdn_out_norm, mem_norm, mem_q_norm, mem_k_norm, norm_ffn, conv_all, comm)

    wmv = lambda n: (args[n][0], args["m_" + n][0], args["v_" + n][0])
    red = comm.finish([grad_x], {"ffn": [wmv("w_down"), wmv("w_gate_up")], "a": [None, wmv("w_out"), wmv("w_mem_kv")],
                                 "b": [None]})
    updated = {"w_down": red["ffn"][0], "w_gate_up": red["ffn"][1], "w_out": red["a"][1], "w_mem_kv": red["a"][2]}
    grads = {n: r[0] for n, r in updated.items()}
    grads["w_in"] = _unperm_in(red["a"][0], red["b"][0])
    small_grads["loss"] = jnp.broadcast_to(loss_local, (1, HD))
    packed, layout = _pack(small_grads)
    small = _unpack(_all_reduce_small("ar_small", packed, True), layout)
    loss = small["loss"][0, 0]
    six = {"fox_f_bias": L_FF, "gdn_a_log": L_GA, "gdn_dt_bias": L_GA}
    for n, rows_n in layout[:-1]:
        gsm = small[n]
        if n == "gdn_conv":
            gsm = lax.dynamic_slice(gsm.reshape(4, N_DEV * cshard), (0, me * cshard), (4, cshard))[None]
        elif n in six:
            gsm = gsm[:, six[n]:six[n] + 6]
        else:
            gsm = gsm.reshape(1, rows_n * HD)
        grads[n] = gsm

    names = ['norm_mix', 'w_in', 'fox_f_bias', 'fox_q_norm', 'fox_k_norm', 'gdn_conv', 'gdn_a_log', 'gdn_dt_bias',
             'gdn_out_norm', 'mem_norm', 'w_mem_kv', 'mem_q_norm', 'mem_k_norm', 'w_out', 'norm_ffn', 'w_gate_up', 'w_down']
    big = ("w_in", "w_mem_kv", "w_out", "w_gate_up", "w_down")
    delta, new_m, new_v = {}, {}, {}
    for n in big:
        res = updated[n][1:] if n in updated else _adamw_call("adamw_" + n, args[n][0], grads[n], *wmv(n)[1:])
        delta[n], new_m[n], new_v[n] = [a[None] for a in res]
        grads[n] = grads[n][None]

    def flat(a):
        a = a.reshape(1, -1)
        return jnp.pad(a, ((0, 0), (0, -a.shape[1] % HD))).reshape(-1, HD)

    smalls = [n for n in names if n not in big]
    pk = lambda pre: jnp.concatenate([flat(grads[n] if pre == "g" else args[pre + n]) for n in smalls], axis=0)
    cat = [pk(""), pk("g"), pk("m_"), pk("v_")]
    padr = -cat[0].shape[0] % 8
    cat = [jnp.pad(a, ((0, padr), (0, 0))) for a in cat]
    res = _adamw_call("adamw_small", *cat)
    at = 0
    for n in smalls:
        shape = args[n].shape
        size = math.prod(shape)
        nrow = -(-size // HD)
        for dst, src in zip((delta, new_m, new_v), res):
            dst[n] = src[at:at + nrow].reshape(-1)[:size].reshape(shape)
        at += nrow

    return (loss, grad_x[None], *[grads[n] for n in names], *[delta[n] for n in names],
            *[new_m[n] for n in names], *[new_v[n] for n in names])


class _StepComm:
    def __init__(self, shard_groups, after):
        self.groups, shards = {}, []
        for key, ws in shard_groups.items():
            self.groups[key] = list(range(len(shards), len(shards) + len(ws)))
            shards += [w.astype(BF16) for w in ws]
        self.gather = _Gather("ag", shards, after)
        self.passed, self.scatters = set(), {}

    def start_deps(self):
        return [self.gather.token]

    def pass_on(self, key, after):
        self.passed.add(key)
        return self.gather.pass_on(self.groups[key], after)

    def weights(self, key, after):
        if key not in self.passed:
            after = self.pass_on(key, after)
        return self.gather.get(self.groups[key], after)

    def send(self, tag, grads):
        blocks = [g if g.ndim == 3 else g.reshape(N_DEV, g.shape[0] // N_DEV, g.shape[1]) for g in grads]
        self.scatters[tag] = _Scatter("rs_" + tag, blocks, ())
        return [self.scatters[tag].token]

    def mid(self, tag, after):
        self.scatters[tag].mid(after)
        return [self.scatters[tag].token]

    def finish(self, after, wmv):
        return {tag: sc.end(after, wmv[tag]) for tag, sc in self.scatters.items()}


def _local_step(xs, ms, tgt, norm_mix, fox_f_bias, fox_q_norm, fox_k_norm, gdn_a_log, gdn_dt_bias, gdn_out_norm,
                mem_norm, mem_q_norm, mem_k_norm, norm_ffn, conv_all, comm):
    t, d = xs.shape
    bq = min(t, 256)
    fb, alog, dtb = _lanes(fox_f_bias, L_FF), _lanes(gdn_a_log, L_GA), _lanes(gdn_dt_bias, L_GA)
    flat = lambda w: w.reshape(-1, w.shape[-1])

    rms1 = lambda a, g: (_rms(a, g),)
    (u,) = _rowwise("norm_mix", rms1, [xs], [norm_mix], [(d, BF16)], min(t, 256), deps=comm.start_deps())
    w_in_b = flat(comm.weights("in_b", [u])[0])
    pb = _matmul("proj_in_b", u, w_in_b, NN, F32, 1024, 768)
    o_fox = _fox_fwd(pb, fb, fox_q_norm, fox_k_norm, bq)
    w_in_a = flat(comm.weights("in_a", [o_fox])[0])
    pa = _matmul("proj_in_a", u, w_in_a, NN, F32, 1024, 768)
    smrow = (pb, HD, SM)
    (gates,) = _rowwise("gdn_gates", _gdn_gates, [smrow], [alog, dtb], [(HD, F32)], min(t, 256))
    gdn_terms = _gdn_fwd(pa, gates, conv_all)
    o_gdn_raw, gdn_states = _gdn_scan(gdn_terms)
    gdn_saved = list(gdn_terms) + [gdn_states]
    zrow = (pa, NG * HD, GZ * HD // (NG * HD))
    (o_gdn,) = _rowwise("gdn_post", _gdn_post, [o_gdn_raw, zrow], [gdn_out_norm], [(NG * HD, BF16)], min(t, 256))
    w_kv_all, w_out_all = [flat(w) for w in comm.weights("kv_out", [o_gdn])]
    (mem_n,) = _rowwise("norm_mem", rms1, [ms], [mem_norm], [(d, BF16)], ms.shape[0])
    mkv = _matmul("proj_mem", mem_n, w_kv_all, NN, F32, 256, 512)
    o_mem = _mem_fwd(pb, mkv, mem_q_norm, mem_k_norm)
    deps = comm.pass_on("gate_up", [o_mem])
    mix = jnp.concatenate([o_fox, o_gdn, o_mem], axis=1)
    h1, h1n = _proj_out_norm(mix, w_out_all, xs, norm_ffn, deps)
    (wgu,) = comm.weights("gate_up", [h1n])
    ffw = wgu.shape[2]
    gu, act = _ffn_up(h1n, wgu.reshape(2, 4, d, ffw))
    w_down_all = flat(comm.weights("down", [act])[0])
    dyb, lsum = _ffn_down_loss(act, w_down_all, h1, tgt)
    loss_local = (0.5 / d) * jnp.sum(lsum[::8, ::HD])

    dgu = _ffn_down_bwd(dyb, w_down_all.reshape(4, ffw, d), gu).reshape(8, t, ffw)
    g_w_down = _matmul("grad_w_down", act, dyb, TN, BF16, 512, 2048)
    g_w_gu = _ffn_up_bwd_w(h1n, dgu)
    deps = comm.send("ffn", [g_w_down, g_w_gu])
    rms2 = lambda a, g: (_rms(a, g), a)
    dh1b, g_norm_ffn = _ffn_up_bwd_x(dgu, wgu, h1, norm_ffn, dyb, deps)

    dmix = _matmul("proj_out_bwd_x", dh1b, w_out_all, NT, BF16, 1024, 1024)
    g_w_out = _matmul("grad_w_out", mix, dh1b, TN, BF16, 1024, 2048)
    deps = comm.mid("ffn", [dmix, g_w_out])
    dmq, dmk, dmv, g_mqn, g_mkn = _mem_bwd(pb, mkv, mem_q_norm, mem_k_norm, dmix, deps=deps)
    dmkv = jnp.concatenate([dmk, dmv], axis=1).astype(BF16)
    g_w_kv = _matmul("grad_w_kv", mem_n, dmkv, TN, BF16, 512, 512)
    do_raw, dgz, g_gon = _rowwise_vjp("gdn_post_bwd", _gdn_post, [o_gdn_raw, zrow], [gdn_out_norm],
                                      [(dmix, NG * HD, 1)], [F32, BF16], min(t, 256), deps=deps)
    dterms = _gdn_bwd_scan(gdn_saved, do_raw)
    dgq, dgk, dgv, dgates, dwq, dwk, dwv = _gdn_bwd(pa, gates, conv_all, dterms)
    dsm_gdn, g_alog, g_dtb = _rowwise_vjp("gdn_gates_bwd", _gdn_gates, [smrow], [alog, dtb], [dgates], [F32], min(t, 256))
    dp_a = jnp.concatenate([dgq, dgk, dgv, dgz], axis=1)
    g_w_in_a = _matmul("grad_w_in_a", u, dp_a, TN, BF16, 512, 3072)
    deps = comm.send("a", [g_w_in_a, g_w_out, g_w_kv])
    du_a = _matmul("proj_in_bwd_a", dp_a, w_in_a, NT, F32, 1024, 1024, deps=deps)
    deps = comm.mid("a", [du_a])
    dfq, dfk, dfv, dsm_fox, g_fb, g_fqn, g_fkn = _fox_bwd(pb, fb, fox_q_norm, fox_k_norm, dmix, min(t, 2 * bq), deps=deps)
    dp_b = jnp.concatenate([dfq, dfk, dfv, dmq, (dsm_fox + dsm_gdn).astype(BF16), jnp.zeros((t, HD), BF16)], axis=1)
    g_w_in_b = _matmul("grad_w_in_b", u, dp_b, TN, BF16, 512, 3072)
    deps = comm.send("b", [g_w_in_b])
    dmem_n = _matmul("proj_mem_bwd_x", dmkv, w_kv_all, NT, F32, 256, 512, deps=deps)
    g_mem_norm = _rowwise_vjp("norm_mem_bwd", rms1, [ms], [mem_norm], [dmem_n], [], ms.shape[0])[0]
    deps = comm.mid("b", [g_mem_norm])
    du = _matmul("proj_in_bwd_b", dp_b, w_in_b, NT, F32, 1024, 1024, residual=du_a, deps=deps)
    grad_x, g_norm_mix = _rowwise_vjp("norm_mix_bwd", rms2, [xs], [norm_mix], [du, dh1b], [F32], min(t, 256))

    small_grads = {
        "norm_mix": g_norm_mix, "mem_norm": g_mem_norm, "norm_ffn": g_norm_ffn,
        "gdn_conv": jnp.concatenate([dwq, dwk, dwv], axis=1),
        "fox_q_norm": g_fqn, "fox_k_norm": g_fkn, "gdn_out_norm": g_gon, "mem_q_norm": g_mqn, "mem_k_norm": g_mkn,
        "fox_f_bias": g_fb, "gdn_a_log": g_alog, "gdn_dt_bias": g_dtb}
    return grad_x, loss_local, small_grads
```

```python
import functools
import math

import jax
import jax.numpy as jnp
from jax import lax
from jax.experimental import pallas as pl
from jax.experimental.pallas import tpu as pltpu

F32 = jnp.float32
BF16 = jnp.bfloat16
SDS = jax.ShapeDtypeStruct

N_DEV = 8
HD = 128
NF, NG, NM = 6, 6, 4
CHUNK = 64
GROUP = 16
NORM_EPS = 1e-6
GQ, GK, GV, GZ = 0, 6, 12, 18
FQ, FK, FV, MQ, SM = 0, 6, 12, 18, 22
HALF = 24 * HD
L_FF, L_GA, L_GB = 0, 6, 12
VMEM_LIMIT = 56 * 1024 * 1024

ADAM_LR, ADAM_B1, ADAM_B2, ADAM_EPS, ADAM_WD, ADAM_STEP = 0.001, 0.9, 0.999, 1e-08, 0.01, 10

NN = (((1,), (0,)), ((), ()))
NT = (((1,), (1,)), ((), ()))
TN = (((0,), (0,)), ((), ()))
MESH = pl.DeviceIdType.MESH


def _cp(*sem):
    return pltpu.CompilerParams(dimension_semantics=tuple(sem) if sem else None, vmem_limit_bytes=VMEM_LIMIT)


def _dot(a, b, dims=NN):
    return lax.dot_general(a, b, dims, preferred_element_type=F32)


def _iota(shape, axis):
    return lax.broadcasted_iota(jnp.int32, shape, axis)


def _rms(x, gain):
    return x * lax.rsqrt(jnp.mean(x * x, axis=-1, keepdims=True) + NORM_EPS) * gain


def _sigmoid(x):
    return 0.5 * jnp.tanh(0.5 * x) + 0.5


def _silu(x):
    return x * _sigmoid(x)


def _softplus(x):
    return jnp.maximum(x, 0.0) + jnp.log(1.0 + jnp.exp(-jnp.abs(x)))


def _lane_pick(x, lane):
    oh = (_iota((1, x.shape[-1]), 1) == lane).astype(F32)
    return jnp.sum(x * oh, axis=-1, keepdims=True)


def _cumsum_rows(x):
    tril = (_iota((HD, HD), 0) >= _iota((HD, HD), 1)).astype(F32)
    carry = jnp.zeros((1, x.shape[1]), F32)
    outs = []
    for b in range(x.shape[0] // HD):
        blk = x[b * HD:(b + 1) * HD]
        outs.append(_pdot(tril, blk, "nn", "xa") + carry)
        carry = carry + jnp.sum(blk, axis=0, keepdims=True)
    return jnp.concatenate(outs, axis=0)


def _row_spec(r, tm):
    if isinstance(r, tuple):
        arr, width, cb = r
        return arr, pl.BlockSpec((tm, width), lambda i, cb=cb: (i, cb))
    return r, pl.BlockSpec((tm, r.shape[1]), lambda i: (i, 0))


ANY_SPEC = pl.BlockSpec(memory_space=pl.ANY)


def _rowwise(name, fn, rows, consts, outs, tm, deps=()):
    arrs, specs = zip(*[_row_spec(r, tm) for r in rows])
    n_rows = arrs[0].shape[0]
    nr, nc, nd = len(rows), len(consts), len(deps)

    def body(*refs):
        res = fn(*[r[...] for r in refs[:nr + nc]])
        for o, v in zip(refs[nr + nc + nd:], res):
            o[...] = v.astype(o.dtype)

    return pl.pallas_call(
        body, grid=(n_rows // tm,), name=name,
        in_specs=list(specs) + [pl.BlockSpec(c.shape, lambda i: (0, 0)) for c in consts] + [ANY_SPEC] * nd,
        out_specs=[pl.BlockSpec((tm, w), lambda i: (i, 0)) for w, _ in outs],
        out_shape=[SDS((n_rows, w), dt) for w, dt in outs],
        compiler_params=_cp("parallel"),
    )(*arrs, *consts, *deps)


def _rowwise_vjp(name, fn, rows, consts, cts, grad_dtypes, tm, deps=()):
    arrs, specs = zip(*[_row_spec(r, tm) for r in rows])
    ct_arrs, ct_specs = zip(*[_row_spec(r, tm) for r in cts])
    n_rows = arrs[0].shape[0]
    nr, nc, nct, nd = len(rows), len(consts), len(cts), len(deps)
    plan = [(j, dt) for j, dts in enumerate(grad_dtypes) for dt in (dts if isinstance(dts, tuple) else (dts,))]
    ng = len(plan)
    widths = [specs[j].block_shape[1] for j, _ in plan]
    grad_dtypes = [dt for _, dt in plan]

    def body(*refs):
        vals = [r[...].astype(F32) for r in refs[:nr + nc]]
        ctv = tuple(r[...].astype(F32) for r in refs[nr + nc:nr + nc + nct])
        _, vjp = jax.vjp(fn, *vals)
        grads = vjp(ctv)
        outs = refs[nr + nc + nct + nd:]
        for o, (j, _) in zip(outs[:ng], plan):
            o[...] = grads[j].astype(o.dtype)

        @pl.when(pl.program_id(0) == 0)
        def _():
            for o in outs[ng:]:
                o[...] = jnp.zeros_like(o)

        for o, g in zip(outs[ng:], grads[nr:]):
            o[...] += g

    return pl.pallas_call(
        body, grid=(n_rows // tm,), name=name,
        in_specs=list(specs) + [pl.BlockSpec(c.shape, lambda i: (0, 0)) for c in consts] + list(ct_specs)
        + [ANY_SPEC] * nd,
        out_specs=[pl.BlockSpec((tm, w), lambda i: (i, 0)) for w in widths]
        + [pl.BlockSpec(c.shape, lambda i: (0, 0)) for c in consts],
        out_shape=[SDS((n_rows, w), dt) for w, dt in zip(widths, grad_dtypes)] + [SDS(c.shape, F32) for c in consts],
        compiler_params=_cp("arbitrary"),
    )(*arrs, *consts, *ct_arrs, *deps)


def _tile(n, pref):
    t = min(n, pref)
    while n % t or (t % HD and t != n):
        t -= 1
    return t


def _matmul(name, a, b, dims, out_dtype, tm, tn, residual=None, deps=()):
    ta, tb = dims == TN, dims == NT
    m = a.shape[1] if ta else a.shape[0]
    k = a.shape[0] if ta else a.shape[1]
    n = b.shape[0] if tb else b.shape[1]
    tm, tn = _tile(m, tm), _tile(n, tn)

    def body(*refs):
        acc = _dot(refs[0][...], refs[1][...], dims)
        if residual is not None:
            acc = acc + refs[2][...]
        refs[-1][...] = acc.astype(out_dtype)

    in_specs = [pl.BlockSpec((k, tm), lambda i, j: (0, i)) if ta else pl.BlockSpec((tm, k), lambda i, j: (i, 0)),
                pl.BlockSpec((tn, k), lambda i, j: (j, 0)) if tb else pl.BlockSpec((k, tn), lambda i, j: (0, j))]
    ops = [a, b]
    if residual is not None:
        in_specs.append(pl.BlockSpec((tm, tn), lambda i, j: (i, j)))
        ops.append(residual)
    in_specs += [ANY_SPEC] * len(deps)
    ops += list(deps)
    return pl.pallas_call(
        body, grid=(m // tm, n // tn), name=name, in_specs=in_specs,
        out_specs=pl.BlockSpec((tm, tn), lambda i, j: (i, j)), out_shape=SDS((m, n), out_dtype),
        compiler_params=_cp("parallel", "parallel"),
    )(*ops)


def _proj_out_norm(mix, w_out, xs, gain, deps):
    t, k = mix.shape
    d = w_out.shape[1]
    tm = _tile(t, 512)

    def body(*refs):
        a, b, x, g = refs[:4]
        h1, h1n = refs[4 + len(deps):]
        acc = _dot(a[...], b[...]) + x[...]
        h1[...] = acc
        h1n[...] = _rms(acc, g[...]).astype(BF16)

    return pl.pallas_call(
        body, grid=(t // tm,), name="proj_out",
        in_specs=[pl.BlockSpec((tm, k), lambda i: (i, 0)), pl.BlockSpec((k, d), lambda i: (0, 0)),
                  pl.BlockSpec((tm, d), lambda i: (i, 0)), pl.BlockSpec((1, d), lambda i: (0, 0))] + [ANY_SPEC] * len(deps),
        out_specs=[pl.BlockSpec((tm, d), lambda i: (i, 0))] * 2, out_shape=[SDS((t, d), F32), SDS((t, d), BF16)],
        compiler_params=_cp("parallel"),
    )(mix, w_out, xs, gain, *deps)


def _proj_in_bwd_norm(dp, w, du_a, xs, gain, dh1b, deps):
    t, k = dp.shape
    d = w.shape[0]
    tm = _tile(t, 256)

    def body(*refs):
        a, b, ua, x, g, dh = refs[:6]
        gx, dgain = refs[6 + len(deps):]
        _, vjp = jax.vjp(lambda xx, gn: _rms(xx, gn), x[...], g[...])
        dx, dg = vjp(_dot(a[...], b[...], NT) + ua[...])
        gx[...] = dx + dh[...].astype(F32)

        @pl.when(pl.program_id(0) == 0)
        def _():
            dgain[...] = jnp.zeros_like(dgain)

        dgain[...] += dg

    row = pl.BlockSpec((tm, d), lambda i: (i, 0))
    vec = pl.BlockSpec((1, d), lambda i: (0, 0))
    return pl.pallas_call(
        body, grid=(t // tm,), name="proj_in_bwd_b",
        in_specs=[pl.BlockSpec((tm, k), lambda i: (i, 0)), pl.BlockSpec((d, k), lambda i: (0, 0), pipeline_mode=ONE_BUFFER),
                  row, row, vec, row] + [ANY_SPEC] * len(deps),
        out_specs=[row, vec], out_shape=[SDS((t, d), F32), SDS((1, d), F32)], compiler_params=_cp("arbitrary"),
    )(dp, w, du_a, xs, gain, dh1b, *deps)


def _ffn_up(h1n, wgu):
    t, d = h1n.shape
    w = wgu.shape[3]
    tm = _tile(t, 512)

    def body(a, b, gu, act):
        x = a[...]
        g = _dot(x, b[0])
        u = _dot(x, b[1])
        gu[0] = g.astype(BF16)
        gu[1] = u.astype(BF16)
        act[...] = (_silu(g) * u).astype(BF16)

    return pl.pallas_call(
        body, grid=(4, t // tm), name="ffn_up",
        in_specs=[pl.BlockSpec((tm, d), lambda j, i: (i, 0)), pl.BlockSpec((2, None, d, w), lambda j, i: (0, j, 0, 0))],
        out_specs=[pl.BlockSpec((2, None, tm, w), lambda j, i: (0, j, i, 0)), pl.BlockSpec((tm, w), lambda j, i: (i, j))],
        out_shape=[SDS((2, 4, t, w), BF16), SDS((t, 4 * w), BF16)],
        compiler_params=_cp("parallel", "parallel"),
    )(h1n, wgu)


def _ffn_down_loss(act, wdown, h1, target):
    t, f = act.shape
    d = wdown.shape[1]
    tm, tn = _tile(t, 1024), _tile(d, 512)

    def body(a, b, h, tg, dyb, ls):
        e = _dot(a[...], b[...]) + h[...] - tg[...]
        dyb[...] = (e * (1.0 / d)).astype(BF16)
        ls[...] = jnp.broadcast_to(jnp.sum(e * e), (8, HD))

    return pl.pallas_call(
        body, grid=(t // tm, d // tn), name="ffn_down_loss",
        in_specs=[pl.BlockSpec((tm, f), lambda i, j: (i, 0)), pl.BlockSpec((f, tn), lambda i, j: (0, j)),
                  pl.BlockSpec((tm, tn), lambda i, j: (i, j)), pl.BlockSpec((tm, tn), lambda i, j: (i, j))],
        out_specs=[pl.BlockSpec((tm, tn), lambda i, j: (i, j)), pl.BlockSpec((8, HD), lambda i, j: (i, j))],
        out_shape=[SDS((t, d), BF16), SDS((8 * (t // tm), HD * (d // tn)), F32)],
        compiler_params=_cp("parallel", "parallel"),
    )(act, wdown, h1, target)


def _ffn_down_bwd(dyb, wdown4, gu):
    t, d = dyb.shape
    w = wdown4.shape[1]
    tm = _tile(t, 512)

    def body(a, b, gu_ref, out):
        da = _dot(a[...], b[...], NT)
        g = gu_ref[0].astype(F32)
        u = gu_ref[1].astype(F32)
        s = _sigmoid(g)
        out[0] = (da * u * (s * (1.0 + g * (1.0 - s)))).astype(BF16)
        out[1] = (da * g * s).astype(BF16)

    return pl.pallas_call(
        body, grid=(4, t // tm), name="ffn_down_bwd",
        in_specs=[pl.BlockSpec((tm, d), lambda j, i: (i, 0)), pl.BlockSpec((None, w, d), lambda j, i: (j, 0, 0)),
                  pl.BlockSpec((2, None, tm, w), lambda j, i: (0, j, i, 0))],
        out_specs=pl.BlockSpec((2, None, tm, w), lambda j, i: (0, j, i, 0)),
        out_shape=SDS((2, 4, t, w), BF16),
        compiler_params=_cp("parallel", "parallel"),
    )(dyb, wdown4, gu)


def _ffn_up_bwd_x(dgu, wgu, h1, gain, dyb, deps):
    _, t, w = dgu.shape
    d = wgu.shape[1]
    tm = _tile(t, 512)

    def body(*refs):
        a, b, h, g, dy = refs[:5]
        dh1, dgain, acc = refs[5 + len(deps):]
        i, j = pl.program_id(0), pl.program_id(1)

        @pl.when(j == 0)
        def _():
            acc[...] = jnp.zeros_like(acc)

        acc[...] += _dot(a[...], b[...], NT)

        @pl.when(j == N_DEV - 1)
        def _():
            _, vjp = jax.vjp(lambda x, gn: _rms(x, gn), h[...], g[...])
            dx, dg = vjp(acc[...])
            dh1[...] = (dx + dy[...].astype(F32)).astype(dh1.dtype)

            @pl.when(i == 0)
            def _():
                dgain[...] = jnp.zeros_like(dgain)

            dgain[...] += dg

    row = pl.BlockSpec((tm, d), lambda i, j: (i, 0))
    return pl.pallas_call(
        body, grid=(t // tm, N_DEV), name="ffn_up_bwd_x",
        in_specs=[pl.BlockSpec((None, tm, w), lambda i, j: (j, i, 0)), pl.BlockSpec((None, d, w), lambda i, j: (j, 0, 0)),
                  row, pl.BlockSpec((1, d), lambda i, j: (0, 0)), row] + [ANY_SPEC] * len(deps),
        out_specs=[row, pl.BlockSpec((1, d), lambda i, j: (0, 0))],
        out_shape=[SDS((t, d), BF16), SDS((1, d), F32)], scratch_shapes=[pltpu.VMEM((tm, d), F32)],
        compiler_params=_cp("arbitrary", "arbitrary"),
    )(dgu, wgu, h1, gain, dyb, *deps)


def _ffn_up_bwd_w(h1n, dgu):
    _, t, w = dgu.shape
    d = h1n.shape[1]
    tm = _tile(d, 512)

    def body(a, b, out):
        out[...] = _dot(a[...], b[...], TN).astype(BF16)

    return pl.pallas_call(
        body, grid=(8, d // tm), name="ffn_up_bwd_w",
        in_specs=[pl.BlockSpec((t, tm), lambda j, i: (0, i)), pl.BlockSpec((None, t, w), lambda j, i: (j, 0, 0))],
        out_specs=pl.BlockSpec((None, tm, w), lambda j, i: (j, i, 0)), out_shape=SDS((8, d, w), BF16),
        compiler_params=_cp("parallel", "parallel"),
    )(h1n, dgu)


def _fox_prep(fq, fk, sm, fb, qg, kg, h):
    qn = _rms(fq, qg)
    kn = _rms(fk, kg)
    c = _cumsum_rows(-_softplus(-(sm + fb)))
    ccol = _lane_pick(c, L_FF + h)
    crow = jnp.sum(c.T * (_iota((HD, 1), 0) == L_FF + h).astype(F32), axis=0, keepdims=True)
    return qn, kn, ccol, crow


def _softmax_times(s, v):
    e = jnp.exp(s - lax.stop_gradient(jnp.max(s, axis=1, keepdims=True)))
    return _dot(e.astype(BF16), v.astype(BF16)) * (1.0 / jnp.sum(e, axis=1, keepdims=True))


def _fox_block(q, k, v, cc, cr, off):
    bq = q.shape[0]
    assert k.shape[0] == off + bq
    s = _dot((q * (HD ** -0.5)).astype(BF16), k.astype(BF16), NT) + cc - cr
    diag = jnp.where(_iota((bq, bq), 1) <= _iota((bq, bq), 0), s[:, off:], -1e30)
    s = jnp.concatenate([s[:, :off], diag], axis=1) if off else diag
    return _softmax_times(s, v)


ONE_BUFFER = pl.Buffered(1)


def _pcol(t, cb):
    return pl.BlockSpec((t, HD), lambda h, cb=cb: (0, cb + h), pipeline_mode=ONE_BUFFER)


def _smcol(t):
    return pl.BlockSpec((t, HD), lambda h: (0, SM), pipeline_mode=ONE_BUFFER)


def _head(t):
    return pl.BlockSpec((t, HD), lambda h: (0, h), pipeline_mode=ONE_BUFFER)


def _small(n):
    return pl.BlockSpec((n, HD), lambda h: (0, 0), pipeline_mode=ONE_BUFFER)


def _fox_fwd(p, fb, qg, kg, bq):
    t = p.shape[0]

    def body(fq, fk, fv, sm, fb_r, qg_r, kg_r, o, qn_s, cc_s):
        h = pl.program_id(0)
        qn, kn, ccol, crow = _fox_prep(fq[...], fk[...], sm[...], fb_r[...], qg_r[...], kg_r[...], h)
        qn_s[...] = qn
        cc_s[...] = ccol
        knb = kn.astype(BF16)
        vb = fv[...].astype(BF16)
        for i in range(t // bq):
            rows, ext = pl.ds(i * bq, bq), (i + 1) * bq
            o[rows, :] = _fox_block(qn_s[rows, :], knb[:ext], vb[:ext], cc_s[rows, :], crow[:, :ext], i * bq).astype(o.dtype)

    return pl.pallas_call(
        body, grid=(NF,), name="fox_fwd",
        in_specs=[_pcol(t, FQ), _pcol(t, FK), _pcol(t, FV), _smcol(t), _small(1), _small(1), _small(1)],
        out_specs=_head(t), out_shape=SDS((t, NF * HD), BF16),
        scratch_shapes=[pltpu.VMEM((t, HD), F32), pltpu.VMEM((t, 1), F32)],
        compiler_params=_cp("parallel"),
    )(p, p, p, p, fb, qg, kg)


def _fox_bwd(p, fb, qg, kg, dmix, bq, deps=()):
    t = p.shape[0]

    def body(*refs):
        fq, fk, fv, sm, fb_r, qg_r, kg_r, do = refs[:8]
        dfq, dfk, dfv, dsm, dfb, dqg, dkg, qn_s, cc_s, dqn_s, dcc_s, dkn_s, dv_s, dcr_s = refs[8 + len(deps):]
        h = pl.program_id(0)
        qn, kn, ccol, crow = _fox_prep(fq[...], fk[...], sm[...], fb_r[...], qg_r[...], kg_r[...], h)
        qn_s[...] = qn
        cc_s[...] = ccol
        v = fv[...]
        dkn_s[...] = jnp.zeros_like(dkn_s)
        dv_s[...] = jnp.zeros_like(dv_s)
        dcr_s[...] = jnp.zeros_like(dcr_s)

        for i in range(t // bq):
            rows, ext = pl.ds(i * bq, bq), (i + 1) * bq
            _, vjp = jax.vjp(lambda a, b, c, d, e, off=i * bq: _fox_block(a, b, c, d, e, off),
                             qn_s[rows, :], kn[:ext], v[:ext], cc_s[rows, :], crow[:, :ext])
            dq, dk, dv, dcc, dcr = vjp(do[rows, :].astype(F32))
            dqn_s[rows, :] = dq
            dcc_s[rows, :] = dcc
            dkn_s[:ext, :] += dk
            dv_s[:ext, :] += dv
            dcr_s[:, :ext] += dcr
        _, prep_vjp = jax.vjp(lambda a, b, c, d, e, f: _fox_prep(a, b, c, d, e, f, h),
                              fq[...], fk[...], sm[...], fb_r[...], qg_r[...], kg_r[...])
        g_fq, g_fk, g_sm, g_fb, g_qg, g_kg = prep_vjp((dqn_s[...], dkn_s[...], dcc_s[...], dcr_s[...]))
        dfq[...] = g_fq.astype(dfq.dtype)
        dfk[...] = g_fk.astype(dfk.dtype)
        dfv[...] = dv_s[...].astype(dfv.dtype)

        @pl.when(h == 0)
        def _():
            for r in (dsm, dfb, dqg, dkg):
                r[...] = jnp.zeros_like(r)

        dsm[...] += g_sm
        dfb[...] += g_fb
        dqg[...] += g_qg
        dkg[...] += g_kg

    head = _head(t)
    return pl.pallas_call(
        body, grid=(NF,), name="fox_bwd",
        in_specs=[_pcol(t, FQ), _pcol(t, FK), _pcol(t, FV), _smcol(t), _small(1), _small(1), _small(1), head]
        + [ANY_SPEC] * len(deps),
        out_specs=[head, head, head, _small(t), _small(1), _small(1), _small(1)],
        out_shape=[SDS((t, NF * HD), BF16)] * 3 + [SDS((t, HD), F32)] + [SDS((1, HD), F32)] * 3,
        scratch_shapes=[pltpu.VMEM((t, HD), F32), pltpu.VMEM((t, 1), F32), pltpu.VMEM((t, HD), F32),
                        pltpu.VMEM((t, 1), F32), pltpu.VMEM((t, HD), F32), pltpu.VMEM((t, HD), F32),
                        pltpu.VMEM((1, t), F32)],
        compiler_params=_cp("arbitrary"),
    )(p, p, p, p, fb, qg, kg, dmix, *deps)


def _mem_attn(mq, mk, mv, qg, kg):
    s = _dot((_rms(mq, qg) * (HD ** -0.5)).astype(BF16), _rms(mk, kg).astype(BF16), NT)
    return _softmax_times(s, mv)


def _mem_fwd(p, mkv, qg, kg):
    t, ml = p.shape[0], mkv.shape[0]

    def body(mq, mk, mv, qg_r, kg_r, o):
        o[...] = _mem_attn(mq[...], mk[...], mv[...], qg_r[...], kg_r[...]).astype(o.dtype)

    return pl.pallas_call(
        body, grid=(NM,), name="mem_fwd",
        in_specs=[_pcol(t, MQ), pl.BlockSpec((ml, HD), lambda h: (0, h)), pl.BlockSpec((ml, HD), lambda h: (0, NM + h)),
                  _small(1), _small(1)],
        out_specs=pl.BlockSpec((t, HD), lambda h: (0, h)), out_shape=SDS((t, NM * HD), BF16),
        compiler_params=_cp("parallel"),
    )(p, mkv, mkv, qg, kg)


def _mem_bwd(p, mkv, qg, kg, dmix, deps=()):
    t, ml = p.shape[0], mkv.shape[0]

    def body(*refs):
        mq, mk, mv, qg_r, kg_r, do = refs[:6]
        dmq, dmk, dmv, dqg, dkg = refs[6 + len(deps):]
        _, vjp = jax.vjp(_mem_attn, mq[...], mk[...], mv[...], qg_r[...], kg_r[...])
        g_q, g_k, g_v, g_qg, g_kg = vjp(do[...].astype(F32))
        dmq[...] = g_q.astype(dmq.dtype)
        dmk[...] = g_k
        dmv[...] = g_v

        @pl.when(pl.program_id(0) == 0)
        def _():
            dqg[...] = jnp.zeros_like(dqg)
            dkg[...] = jnp.zeros_like(dkg)

        dqg[...] += g_qg
        dkg[...] += g_kg

    return pl.pallas_call(
        body, grid=(NM,), name="mem_bwd",
        in_specs=[_pcol(t, MQ), pl.BlockSpec((ml, HD), lambda h: (0, h)), pl.BlockSpec((ml, HD), lambda h: (0, NM + h)),
                  _small(1), _small(1), pl.BlockSpec((t, HD), lambda h: (0, NF + NG + h))] + [ANY_SPEC] * len(deps),
        out_specs=[pl.BlockSpec((t, HD), lambda h: (0, h)), pl.BlockSpec((ml, HD), lambda h: (0, h)),
                   pl.BlockSpec((ml, HD), lambda h: (0, h)), _small(1), _small(1)],
        out_shape=[SDS((t, NM * HD), BF16), SDS((ml, NM * HD), F32), SDS((ml, NM * HD), F32),
                   SDS((1, HD), F32), SDS((1, HD), F32)],
        compiler_params=_cp("arbitrary"),
    )(p, mkv, mkv, qg, kg, dmix, *deps)


def _shift_down(x, s):
    if s == 0:
        return x
    return jnp.where(_iota(x.shape, 0) >= s, pltpu.roll(x, s, 0), 0.0)


def _shift_up(x, s):
    if s == 0:
        return x
    n = x.shape[0]
    return jnp.where(_iota(x.shape, 0) < n - s, pltpu.roll(x, n - s, 0), 0.0)


@jax.custom_vjp
def _conv4(x, w0, w1, w2, w3):
    return w0 * _shift_down(x, 3) + w1 * _shift_down(x, 2) + w2 * _shift_down(x, 1) + w3 * x


def _conv4_fwd(x, w0, w1, w2, w3):
    return _conv4(x, w0, w1, w2, w3), (x, w0, w1, w2, w3)


def _conv4_bwd(res, dy):
    x, w0, w1, w2, w3 = res
    ups = [_shift_up(dy, 3 - k) for k in range(4)]
    dx = w0 * ups[0] + w1 * ups[1] + w2 * ups[2] + w3 * ups[3]
    return (dx,) + tuple(jnp.sum(up * x, axis=0, keepdims=True) for up in ups)


_conv4.defvjp(_conv4_fwd, _conv4_bwd)


HALO = 8


def _gdn_gates(sm, alog, dtb):
    lane = _iota((1, HD), 1)
    g = -jnp.exp(alog) * _softplus(sm + dtb)
    return (jnp.where((lane >= L_GA) & (lane < L_GA + NG), g,
                      jnp.where((lane >= L_GB) & (lane < L_GB + NG), _sigmoid(sm), 0.0)),)


def _gdn_prep(gq, gk, gv, gates, taps, h):
    q, k, v = [_silu(_conv4(x, *taps[4 * j:4 * j + 4]))[HALO:] for j, x in enumerate((gq, gk, gv))]
    q = q * lax.rsqrt(jnp.sum(q * q, axis=-1, keepdims=True) + NORM_EPS) * (HD ** -0.5)
    k = k * lax.rsqrt(jnp.sum(k * k, axis=-1, keepdims=True) + NORM_EPS)
    return q, k, v, _lane_pick(gates, L_GA + h), _lane_pick(gates, L_GB + h)


def _split(x, n):
    parts, rest = [], x
    for i in range(n):
        parts.append(rest.astype(BF16))
        if i + 1 < n:
            rest = rest - parts[-1].astype(F32)
    return parts


def _raw_dot(a, b, form):
    lead = a.ndim - 2
    ca, cb = {"nn": (1, 0), "nt": (1, 1), "tn": (0, 0)}[form]
    batch = ((0,), (0,)) if lead else ((), ())
    return lax.dot_general(a, b, (((ca + lead,), (cb + lead,)), batch), preferred_element_type=F32)


def _pdot_impl(a, b, form, mode):
    if mode == "1":
        return _raw_dot(a.astype(BF16), b.astype(BF16), form)
    if mode == "3":
        (ah, al), (bh, bl) = _split(a, 2), _split(b, 2)
        return _raw_dot(ah, bh, form) + (_raw_dot(al, bh, form) + _raw_dot(ah, bl, form))
    if mode == "xa":
        return sum(_raw_dot(a.astype(BF16), t, form) for t in reversed(_split(b, 3)))
    return sum(_raw_dot(t, b.astype(BF16), form) for t in reversed(_split(a, 3)))


@functools.partial(jax.custom_vjp, nondiff_argnums=(2, 3))
def _pdot(a, b, form, mode):
    return _pdot_impl(a, b, form, mode)


def _pdot_fwd(a, b, form, mode):
    return _pdot_impl(a, b, form, mode), (a, b)


def _pdot_bwd(form, mode, res, ct):
    a, b = res
    da_args, db_args = {"nn": ((ct, b, "nt"), (a, ct, "tn")), "nt": ((ct, b, "nn"), (ct, a, "tn")),
                        "tn": ((b, ct, "nt"), (a, ct, "nn"))}[form]

    def side(args, exact):
        if mode in ("1", "3"):
            return mode
        return "xa" if args[0] is exact else "xb"

    if mode == "xa":
        return jnp.zeros_like(a), _pdot_impl(*db_args, side(db_args, a))
    if mode == "xb":
        return _pdot_impl(*da_args, side(da_args, b)), jnp.zeros_like(b)
    return _pdot_impl(*da_args, mode), _pdot_impl(*db_args, mode)


_pdot.defvjp(_pdot_fwd, _pdot_bwd)

GDN_QK, GDN_INV, GDN_SCAN = "1", "1", "1"


@jax.custom_vjp
def _tri_inv(low):
    eye = (_iota((CHUNK, CHUNK), 0) == _iota((CHUNK, CHUNK), 1)).astype(F32)
    inv = eye - low
    pw = low
    for _ in range(5):
        pw = _pdot_impl(pw, pw, "nn", GDN_INV)
        inv = inv + _pdot_impl(inv, pw, "nn", GDN_INV)
    return inv


def _tri_inv_fwd(low):
    inv = _tri_inv(low)
    return inv, inv


def _tri_inv_bwd(inv, ct):
    return (-_pdot_impl(_pdot_impl(inv, ct, "tn", GDN_INV), inv, "nt", GDN_INV),)


_tri_inv.defvjp(_tri_inv_fwd, _tri_inv_bwd)


def _gdn_intra(q, k, v, g, beta):
    n = q.shape[0]
    r, c = _iota((CHUNK, CHUNK), 0), _iota((CHUNK, CHUNK), 1)
    tril, strict = r >= c, r > c
    trilf = jnp.broadcast_to(tril.astype(F32), (n, CHUNK, CHUNK))
    gcm = _pdot(trilf, jnp.broadcast_to(g, (n, CHUNK, CHUNK)), "nn", "xa")
    gcf = _pdot(trilf, jnp.broadcast_to(g, (n, CHUNK, HD)), "nn", "xa")
    lane0 = (_iota((1, 1, CHUNK), 2) == 0).astype(F32)
    gcr = _pdot(jnp.ones((n, CHUNK, CHUNK), F32), gcm * lane0, "nt", "xa")
    decay = jnp.where(tril, jnp.exp(jnp.where(tril, gcm - gcr, 0.0)), 0.0)
    egc = jnp.exp(gcf)
    kb = k * beta
    low = jnp.where(strict, _pdot(kb, k, "nt", GDN_QK) * decay, 0.0)
    inv = _tri_inv(low)
    u = _pdot(inv, v * beta, "nn", GDN_INV)
    w = _pdot(inv, kb * egc, "nn", GDN_INV)
    at = jnp.where(tril, _pdot(q, k, "nt", GDN_QK) * decay, 0.0)
    gl = jnp.sum(jnp.broadcast_to(g, (n, CHUNK, HD)), axis=1, keepdims=True)
    kd = k * jnp.exp(gl - gcf)
    return (_pdot(kd, w, "tn", GDN_SCAN), _pdot(kd, u, "tn", GDN_SCAN), q * egc - _pdot(at, w, "nn", GDN_SCAN),
            _pdot(at, u, "nn", GDN_SCAN), gl)


def _gdn_step(s, kw, ku, a, b, gl):
    return _pdot(a, s, "nn", GDN_SCAN) + b, s * jnp.exp(gl) - _pdot(kw, s, "nn", GDN_SCAN) + ku


SCAN_HEADS = 3
SCAN_UNROLL = 4


def _gdn_chunked_scratch(nc):
    big = pltpu.VMEM((nc, CHUNK, HD), F32)
    return [big, big, big, pltpu.VMEM((nc, CHUNK, 1), F32), pltpu.VMEM((nc, CHUNK, 1), F32)]


N_TERMS = 5


def _gdn_term_shapes(nc):
    return [(nc, HD, HD), (nc, HD, HD), (nc, CHUNK, HD), (nc, CHUNK, HD), (nc, 1, HD)]


def _per_head(shape, heads=None, one_buffer=True):
    lead = (None,) if heads is None else (heads,)
    return pl.BlockSpec(lead + tuple(shape), lambda h: (h,) + (0,) * len(shape),
                        pipeline_mode=ONE_BUFFER if one_buffer else None)


def _gdn_in_specs(t):
    cw = lambda cb: pl.BlockSpec((4, HD), lambda h, cb=cb: (0, cb + h))
    return [_pcol(t, GQ), _pcol(t, GK), _pcol(t, GV), _small(t), cw(0), cw(NG), cw(2 * NG)]


def _taps(wq, wk, wv):
    return tuple(w[k:k + 1, :] for w in (wq, wk, wv) for k in range(4))


def _prep_rows(t):
    return min(t, 256)


def _gdn_pad(srcs, pads):
    for src, pad in zip(srcs, pads):
        pad[0:HALO, :] = jnp.zeros((HALO, HD), F32)
        pad[HALO:, :] = src[...]


def _gdn_stage(pads, gates, taps, h, chunked):
    t = gates.shape[0]
    rows = _prep_rows(t)
    per = rows // CHUNK

    def tile(i, carry):
        r0 = pl.multiple_of(i * rows, rows)
        vals = _gdn_prep(*[p[pl.ds(r0, rows + HALO), :] for p in pads], gates[pl.ds(r0, rows), :], taps, h)
        for v, r in zip(vals, chunked):
            r[pl.ds(i * per, per)] = v.reshape(per, CHUNK, v.shape[-1])
        return carry

    lax.fori_loop(0, t // rows, tile, 0)


def _gdn_intra_all(chunked, intra):
    nc = chunked[0].shape[0]
    grp_n = math.gcd(nc, GROUP)

    def grp(i, carry):
        sl = pl.ds(pl.multiple_of(i * grp_n, grp_n), grp_n)
        for r, val in zip(intra, _gdn_intra(*[c[sl] for c in chunked])):
            r[sl] = val
        return carry

    lax.fori_loop(0, nc // grp_n, grp, 0)


def _gdn_fwd(pa, gates, conv):
    t = pa.shape[0]
    nc = t // CHUNK
    terms = _gdn_term_shapes(nc)

    def body(gq, gk, gv, gt, wq, wk, wv, *rest):
        h = pl.program_id(0)
        intra, chunked, pads = rest[:N_TERMS], rest[N_TERMS:N_TERMS + 5], rest[N_TERMS + 5:]
        _gdn_pad((gq, gk, gv), pads)
        _gdn_stage(pads, gt, _taps(wq, wk, wv), h, chunked)
        _gdn_intra_all(chunked, intra)

    qkv = [(nc, CHUNK, HD)] * 3
    outs = pl.pallas_call(
        body, grid=(NG,), name="gdn_fwd", in_specs=_gdn_in_specs(t),
        out_specs=[_per_head(sh, one_buffer=False) for sh in terms + qkv],
        out_shape=[SDS((NG,) + sh, F32) for sh in terms + qkv],
        scratch_shapes=_gdn_chunked_scratch(nc)[3:] + [pltpu.VMEM((t + HALO, HD), F32)] * 3, compiler_params=_cp("parallel"),
    )(pa, pa, pa, gates, conv, conv, conv)
    return list(outs[:N_TERMS]), list(outs[N_TERMS:])


def _gdn_scan(terms_in):
    nc = terms_in[0].shape[1]
    terms = _gdn_term_shapes(nc)

    def body(*refs):
        intra, o, states = refs[:N_TERMS], refs[N_TERMS], refs[N_TERMS + 1]

        def one(c, ss):
            rows = pl.ds(pl.multiple_of(c * CHUNK, CHUNK), CHUNK)
            loaded = [[r[hh, c] for r in intra] for hh in range(SCAN_HEADS)]
            res = [_gdn_step(ss[hh], *loaded[hh]) for hh in range(SCAN_HEADS)]
            for hh in range(SCAN_HEADS):
                states[hh, c] = ss[hh]
                o[rows, hh * HD:(hh + 1) * HD] = res[hh][0]
            return tuple(r[1] for r in res)

        per_trip = math.gcd(nc, SCAN_UNROLL)

        def step(i, ss):
            for k in range(per_trip):
                ss = one(per_trip * i + k, ss)
            return ss

        lax.fori_loop(0, nc // per_trip, step, tuple(jnp.zeros((HD, HD), F32) for _ in range(SCAN_HEADS)))

    return pl.pallas_call(
        body, grid=(NG // SCAN_HEADS,), name="gdn_scan", in_specs=[_per_head(sh, SCAN_HEADS) for sh in terms],
        out_specs=[pl.BlockSpec((nc * CHUNK, SCAN_HEADS * HD), lambda h: (0, h), pipeline_mode=ONE_BUFFER),
                   _per_head((nc, HD, HD), SCAN_HEADS)],
        out_shape=[SDS((nc * CHUNK, NG * HD), F32), SDS((NG, nc, HD, HD), F32)], compiler_params=_cp("parallel"),
    )(*terms_in)


def _gdn_bwd_scan(saved, do_raw):
    nc = saved[0].shape[1]
    terms = _gdn_term_shapes(nc)

    def body(*refs):
        intra, states, do, outs = refs[:N_TERMS], refs[N_TERMS], refs[N_TERMS + 1], refs[N_TERMS + 2:]

        def one(c, dss):
            rows = pl.ds(pl.multiple_of(c * CHUNK, CHUNK), CHUNK)
            loaded = [[states[hh, c]] + [r[hh, c] for r in intra] for hh in range(SCAN_HEADS)]
            cts = [do[rows, hh * HD:(hh + 1) * HD] for hh in range(SCAN_HEADS)]
            grads = [jax.vjp(_gdn_step, *loaded[hh])[1]((cts[hh], dss[hh])) for hh in range(SCAN_HEADS)]
            for hh in range(SCAN_HEADS):
                for r, gval in zip(outs, grads[hh][1:]):
                    r[hh, c] = gval
            return tuple(g[0] for g in grads)

        per_trip = math.gcd(nc, SCAN_UNROLL)

        def bwd(i, dss):
            c = nc - 1 - per_trip * i
            for k in range(per_trip):
                dss = one(c - k, dss)
            return dss

        lax.fori_loop(0, nc // per_trip, bwd, tuple(jnp.zeros((HD, HD), F32) for _ in range(SCAN_HEADS)))

    return pl.pallas_call(
        body, grid=(NG // SCAN_HEADS,), name="gdn_bwd_scan",
        in_specs=[_per_head(sh, SCAN_HEADS) for sh in terms] + [_per_head((nc, HD, HD), SCAN_HEADS)]
        + [pl.BlockSpec((nc * CHUNK, SCAN_HEADS * HD), lambda h: (0, h), pipeline_mode=ONE_BUFFER)],
        out_specs=[_per_head(sh, SCAN_HEADS) for sh in terms],
        out_shape=[SDS((NG,) + sh, F32) for sh in terms], compiler_params=_cp("parallel"),
    )(*saved, do_raw)


def _gdn_bwd(pa, gates, conv, dterms, qkv):
    t = pa.shape[0]
    nc = t // CHUNK
    terms = _gdn_term_shapes(nc)

    def body(*refs):
        gq, gk, gv, gt, wq, wk, wv = refs[:7]
        dintra, qkv = refs[7:7 + N_TERMS], refs[7 + N_TERMS:10 + N_TERMS]
        dgq, dgk, dgv, dgt, dwq, dwk, dwv = refs[10 + N_TERMS:17 + N_TERMS]
        chunked, pads, dpads, dgt_s = (refs[17 + N_TERMS:22 + N_TERMS], refs[22 + N_TERMS:25 + N_TERMS],
                                       refs[25 + N_TERMS:28 + N_TERMS], refs[28 + N_TERMS])
        h = pl.program_id(0)
        taps = _taps(wq, wk, wv)
        _gdn_pad((gq, gk, gv), pads)
        rows = _prep_rows(t)
        per = rows // CHUNK

        def gates_tile(i, carry):
            gtile = gt[pl.ds(pl.multiple_of(i * rows, rows), rows), :]
            chunked[3][pl.ds(i * per, per)] = _lane_pick(gtile, L_GA + h).reshape(per, CHUNK, 1)
            chunked[4][pl.ds(i * per, per)] = _lane_pick(gtile, L_GB + h).reshape(per, CHUNK, 1)
            return carry

        lax.fori_loop(0, t // rows, gates_tile, 0)
        grp_n = math.gcd(nc, GROUP)

        def grp(i, carry):
            sl = pl.ds(pl.multiple_of(i * grp_n, grp_n), grp_n)
            _, vjp = jax.vjp(_gdn_intra, *[r[sl] for r in qkv], chunked[3][sl], chunked[4][sl])
            for r, gval in zip(chunked, vjp(tuple(r[sl] for r in dintra))):
                r[sl] = gval
            return carry

        lax.fori_loop(0, nc // grp_n, grp, 0)

        for r in dpads:
            r[...] = jnp.zeros_like(r)

        def tile(i, dtaps):
            r0 = pl.multiple_of(i * rows, rows)
            win = pl.ds(r0, rows + HALO)
            _, vjp = jax.vjp(lambda *a: _gdn_prep(*a, h), *[p[win, :] for p in pads], gt[pl.ds(r0, rows), :], taps)
            grads = vjp(tuple(r[pl.ds(i * per, per)].reshape(rows, r.shape[-1]) for r in chunked))
            for r, gval in zip(dpads, grads[:3]):
                r[win, :] += gval
            dgt_s[pl.ds(r0, rows), :] = grads[3]
            return jax.tree.map(jnp.add, dtaps, grads[4])

        dtaps = lax.fori_loop(0, t // rows, tile, (jnp.zeros((1, HD), F32),) * 12)
        for r, dpad in zip((dgq, dgk, dgv), dpads):
            r[...] = dpad[HALO:, :].astype(r.dtype)
        for j, r in enumerate((dwq, dwk, dwv)):
            for k in range(4):
                r[k:k + 1, :] = dtaps[4 * j + k]

        @pl.when(h == 0)
        def _():
            dgt[...] = jnp.zeros_like(dgt)

        dgt[...] += dgt_s[...]

    head = _head(t)
    taps = pl.BlockSpec((4, HD), lambda h: (0, h))
    return pl.pallas_call(
        body, grid=(NG,), name="gdn_bwd",
        in_specs=_gdn_in_specs(t) + [_per_head(sh) for sh in terms + [(nc, CHUNK, HD)] * 3],
        out_specs=[head, head, head, _small(t), taps, taps, taps],
        out_shape=[SDS((t, NG * HD), BF16)] * 3 + [SDS((t, HD), F32)] + [SDS((4, NG * HD), F32)] * 3,
        scratch_shapes=_gdn_chunked_scratch(nc) + [pltpu.VMEM((t + HALO, HD), F32)] * 6 + [pltpu.VMEM((t, HD), F32)],
        compiler_params=_cp("arbitrary"),
    )(pa, pa, pa, gates, conv, conv, conv, *dterms, *qkv)


def _gdn_post(o, z, gain):
    return (jnp.concatenate(
        [_rms(o[:, h * HD:(h + 1) * HD], gain) * _silu(z[:, h * HD:(h + 1) * HD]) for h in range(NG)], axis=1),)


def _place():
    return lax.axis_index("x"), lax.axis_index("y"), lax.axis_index("c")


def _sum_blocks(name, parts):
    _, r, c = parts.shape
    tr = 64 if r % 64 == 0 else r

    def body(x, o):
        acc = x[0].astype(F32)
        for d in range(1, N_DEV):
            acc = acc + x[d].astype(F32)
        o[...] = acc

    return pl.pallas_call(
        body, grid=(r // tr,), name=name, in_specs=[pl.BlockSpec((N_DEV, tr, c), lambda i: (0, i, 0))],
        out_specs=pl.BlockSpec((tr, c), lambda i: (i, 0)), out_shape=SDS((r, c), F32), compiler_params=_cp("parallel"),
    )(parts)


def _all_reduce_small(name, x, reduce):
    m_per, n = x.shape

    def body(x_ref, out_ref, send_sems, recv_sems, local_sem):
        px, py, pc = _place()
        me, sibling = (px, py, pc), (px, py, 1 - pc)
        chips = [(1 - px, py), (px, 1 - py), (1 - px, 1 - py)]
        buf = out_ref

        def rows(qx, qy, qc):
            return buf.at[pl.ds((4 * qx + 2 * qy + qc) * m_per, m_per), :]

        def copy(k, block, to, src=None):
            return pltpu.make_async_remote_copy(
                src_ref=rows(*block) if src is None else src, dst_ref=rows(*block),
                send_sem=send_sems.at[k], recv_sem=recv_sems.at[k], device_id=to, device_id_type=MESH)

        mine = pltpu.make_async_copy(x_ref, rows(*me), local_sem)
        mine.start()
        first = [copy(0, me, sibling, src=x_ref)]
        first += [copy(1 + j, me, (*chip, pc), src=x_ref) for j, chip in enumerate(chips)]
        for cp in first:
            cp.start()
        passed = [copy(4 + j, (*chip, pc), sibling) for j, chip in enumerate(chips)]
        for j, chip in enumerate(chips):
            copy(1 + j, (*chip, pc), me).wait_recv()
            passed[j].start()
        copy(0, sibling, me).wait_recv()
        for j, chip in enumerate(chips):
            copy(4 + j, (*chip, 1 - pc), me).wait_recv()
        for cp in first + passed:
            cp.wait_send()
        mine.wait()

    gathered = pl.pallas_call(
        body, name=name, out_shape=SDS((N_DEV * m_per, n), x.dtype),
        in_specs=[pl.BlockSpec(memory_space=pltpu.VMEM)], out_specs=pl.BlockSpec(memory_space=pltpu.VMEM),
        scratch_shapes=[pltpu.SemaphoreType.DMA((7,)), pltpu.SemaphoreType.DMA((7,)), pltpu.SemaphoreType.DMA],
    )(x)
    if not reduce:
        return gathered
    return _sum_blocks(name + "_sum", gathered.reshape(N_DEV, m_per, n))


HBM_SPEC = pl.BlockSpec(memory_space=pltpu.HBM)
SEM_SPEC = pl.BlockSpec(memory_space=pltpu.SEMAPHORE)
EFFECT = pltpu.SideEffectType.DATAFLOW_SIDE_EFFECTING


def _copies_start(name, bufs, n_remote, n_local, build, deps):
    nb, nd = len(bufs), len(deps)
    sem_shapes = [pltpu.SemaphoreType.DMA((n_remote,)), pltpu.SemaphoreType.DMA((n_remote,))]
    if n_local:
        sem_shapes.append(pltpu.SemaphoreType.DMA((n_local,)))
    ns = len(sem_shapes)

    def body(*refs):
        sems = refs[nb + nd:nb + nd + ns]
        remote, local = build(refs[:nb], *sems, *([None] * (3 - ns)))
        for cp in local + remote:
            cp.start()
        refs[-1][...] = jnp.zeros((8, HD), F32)

    outs = pl.pallas_call(
        body, name=name,
        out_shape=(*sem_shapes, *[pltpu.HBM(b.shape, b.dtype) for b in bufs], SDS((8, HD), F32)),
        in_specs=[HBM_SPEC] * nb + [ANY_SPEC] * nd,
        out_specs=(*[SEM_SPEC] * ns, *[HBM_SPEC] * nb, pl.BlockSpec(memory_space=pltpu.VMEM)),
        input_output_aliases={i: ns + i for i in range(nb)},
        compiler_params=pltpu.CompilerParams(has_side_effects=EFFECT),
    )(*[pltpu.with_memory_space_constraint(b, pltpu.HBM) for b in bufs], *deps)
    return list(outs[:ns]), list(outs[ns:ns + nb]), outs[-1]


def _copies_wait(name, bufs, sems, build, after):
    nb, ns = len(bufs), len(sems)

    def body(*refs):
        remote, local = build(refs[:nb], *refs[nb:nb + ns], *([None] * (3 - ns)))
        for cp in local:
            cp.wait()
        for cp in remote:
            cp.wait_send()
            cp.wait_recv()

    outs = pl.pallas_call(
        body, name=name, out_shape=tuple(pltpu.HBM(b.shape, b.dtype) for b in bufs),
        in_specs=[HBM_SPEC] * nb + [SEM_SPEC] * ns + [ANY_SPEC] * len(after), out_specs=tuple([HBM_SPEC] * nb),
        input_output_aliases={i: i for i in range(nb)},
        compiler_params=pltpu.CompilerParams(has_side_effects=EFFECT),
    )(*bufs, *sems, *after)
    return list(outs)


def _remote(src, dst, send, recv, k, to):
    return pltpu.make_async_remote_copy(src_ref=src, dst_ref=dst, send_sem=send.at[k], recv_sem=recv.at[k],
                                        device_id=to, device_id_type=MESH)


class _Gather:
    def __init__(self, name, shards, deps):
        self.name, self.n = name, len(shards)
        lands = [lax.empty((N_DEV,) + s.shape, s.dtype) for s in shards]
        self.sems1, bufs, self.token = _copies_start(
            name + "_s1", list(shards) + lands, 4 * self.n, self.n, self._stage1(range(self.n)), deps)
        self.shards, self.lands, self.sems2 = bufs[:self.n], bufs[self.n:], {}

    def _stage1(self, idxs):
        def build(refs, send, recv, loc):
            x, y, c = _place()
            me = 4 * x + 2 * y + c
            targets = [(x, y, 1 - c), (1 - x, y, c), (x, 1 - y, c), (1 - x, 1 - y, c)]
            remote, local = [], []
            for pos, i in enumerate(idxs):
                src, land = refs[pos], refs[len(idxs) + pos]
                local.append(pltpu.make_async_copy(src, land.at[me], loc.at[i]))
                remote += [_remote(src, land.at[me], send, recv, 4 * i + k, to) for k, to in enumerate(targets)]
            return remote, local
        return build

    @staticmethod
    def _stage2(refs, send, recv, loc):
        x, y, c = _place()
        remote = []
        for pos, land in enumerate(refs):
            for j, (cx, cy) in enumerate([(1 - x, y), (x, 1 - y), (1 - x, 1 - y)]):
                blk = land.at[4 * cx + 2 * cy + c]
                remote.append(_remote(blk, blk, send, recv, 3 * pos + j, (x, y, 1 - c)))
        return remote, []

    def pass_on(self, idxs, after):
        tag, m = "".join(map(str, idxs)), len(idxs)
        bufs = _copies_wait(f"{self.name}_w1_{tag}", [self.shards[i] for i in idxs] + [self.lands[i] for i in idxs],
                            self.sems1, self._stage1(idxs), after)
        self.sems2[tag], lands, token = _copies_start(f"{self.name}_s2_{tag}", bufs[m:], 3 * m, 0, self._stage2, ())
        for pos, i in enumerate(idxs):
            self.lands[i] = lands[pos]
        return [token]

    def get(self, idxs, after):
        tag = "".join(map(str, idxs))
        return _copies_wait(f"{self.name}_w2_{tag}", [self.lands[i] for i in idxs], self.sems2[tag], self._stage2, after)


def _rows_tile(r, row_bytes, target=1 << 20):
    tr = r
    while tr % 32 == 0 and tr * row_bytes > target:
        tr //= 2
    return tr


def _pair_add(name, g, got, c):
    _, r, cols = g.shape
    tr = _rows_tile(r, cols * 2)

    def body(s, a, b, o):
        o[...] = (a[...].astype(F32) + b[...].astype(F32)).astype(o.dtype)

    return pl.pallas_call(
        body, name=name, out_shape=SDS((4, r, cols), g.dtype),
        grid_spec=pltpu.PrefetchScalarGridSpec(
            num_scalar_prefetch=1, grid=(4, r // tr),
            in_specs=[pl.BlockSpec((None, tr, cols), lambda j, i, s: (2 * j + s[0], i, 0)),
                      pl.BlockSpec((None, tr, cols), lambda j, i, s: (j, i, 0))],
            out_specs=pl.BlockSpec((None, tr, cols), lambda j, i, s: (j, i, 0))),
        compiler_params=_cp("parallel", "parallel"),
    )(c.reshape(1), g, got)


def _quad_sum(name, part, got, chip, wmv=None):
    _, r, cols = part.shape
    tr = _rows_tile(r, cols * 4)
    n_out = 4 if wmv else 1

    def body(s, a, b1, b2, b3, *rest):
        g = ((a[...].astype(F32) + b1[...].astype(F32)) + b2[...].astype(F32)) + b3[...].astype(F32)
        rest[-n_out][...] = g
        if wmv:
            w, m, v = rest[:3]
            rest[-3][...], rest[-2][...], rest[-1][...] = _adamw(w[...], g, m[...], v[...])

    blk = lambda k: pl.BlockSpec((None, tr, cols), lambda i, s, k=k: (jnp.bitwise_xor(s[0], k), i, 0))
    row = pl.BlockSpec((tr, cols), lambda i, s: (i, 0))
    outs = pl.pallas_call(
        body, name=name, out_shape=[SDS((r, cols), F32)] * n_out,
        grid_spec=pltpu.PrefetchScalarGridSpec(
            num_scalar_prefetch=1, grid=(r // tr,), in_specs=[blk(0), blk(1), blk(2), blk(3)] + [row] * (n_out - 1),
            out_specs=[row] * n_out),
        compiler_params=_cp("parallel"),
    )(chip.reshape(1), part, got, got, got, *(wmv or ()))
    return tuple(outs) if wmv else outs[0]


class _Scatter:
    def __init__(self, name, grads, deps):
        self.name, self.n = name, len(grads)
        got = [lax.empty((4,) + g.shape[1:], g.dtype) for g in grads]
        self.sems, bufs, self.token = _copies_start(name + "_s1", list(grads) + got, 4 * self.n, 0, self._stage1, deps)
        self.grads, self.got = bufs[:self.n], bufs[self.n:]

    def _stage1(self, refs, send, recv, loc):
        x, y, c = _place()
        remote = []
        for i in range(self.n):
            remote += [_remote(refs[i].at[2 * j + 1 - c], refs[self.n + i].at[j], send, recv, 4 * i + j, (x, y, 1 - c))
                       for j in range(4)]
        return remote, []

    def _stage2(self, refs, send, recv, loc):
        x, y, c = _place()
        remote = []
        for i in range(self.n):
            for k in (1, 2, 3):
                tx = 1 - x if k & 2 else x
                ty = 1 - y if k & 1 else y
                remote.append(_remote(refs[i].at[2 * tx + ty], refs[self.n + i].at[2 * x + y], send, recv,
                                      3 * i + k - 1, (tx, ty, c)))
        return remote, []

    def mid(self, after):
        bufs = _copies_wait(self.name + "_w1", self.grads + self.got, self.sems, self._stage1, after)
        c = lax.axis_index("c").astype(jnp.int32)
        parts = [_pair_add(f"{self.name}_add{i}", bufs[i], bufs[self.n + i], c) for i in range(self.n)]
        got = [lax.empty(p.shape, p.dtype) for p in parts]
        self.sems, bufs, self.token = _copies_start(self.name + "_s2", parts + got, 3 * self.n, 0, self._stage2, ())
        self.parts, self.got = bufs[:self.n], bufs[self.n:]

    def end(self, after, wmv=None):
        bufs = _copies_wait(self.name + "_w2", self.parts + self.got, self.sems, self._stage2, after)
        chip = (2 * lax.axis_index("x") + lax.axis_index("y")).astype(jnp.int32)
        wmv = wmv or [None] * self.n
        return [_quad_sum(f"{self.name}_sum{i}", bufs[i], bufs[self.n + i], chip, wmv[i]) for i in range(self.n)]


def _adamw(w, g, m, v):
    m = ADAM_B1 * m + (1.0 - ADAM_B1) * g
    v = ADAM_B2 * v + (1.0 - ADAM_B2) * (g * g)
    m_hat = m / (1.0 - ADAM_B1 ** ADAM_STEP)
    v_hat = v / (1.0 - ADAM_B2 ** ADAM_STEP)
    return -ADAM_LR * (m_hat / (jnp.sqrt(v_hat) + ADAM_EPS) + ADAM_WD * w), m, v


def _adamw_call(name, w, g, m, v):
    r, c = w.shape
    tm = 64 if r % 64 == 0 else r
    return _rowwise(name, _adamw, [w, g, m, v], [], [(c, F32)] * 3, tm)


_IN_COLS = 5906


def _perm_in(w):
    pad = jnp.zeros((w.shape[0], 2 * HALF - _IN_COLS), w.dtype)
    return (jnp.concatenate([w[:, 2310:4614], w[:, 4614:5382]], axis=1),
            jnp.concatenate([w[:, :2304], w[:, 5394:5906], w[:, 2304:2310], w[:, 5382:5394], pad], axis=1))


def _unperm_in(ga, gb):
    return jnp.concatenate([gb[:, :2304], gb[:, 2816:2822], ga[:, :2304], ga[:, 2304:3072], gb[:, 2822:2834],
                            gb[:, 2304:2816]], axis=1)


def _lanes(v, at):
    return jnp.pad(v, ((0, 0), (at, HD - at - v.shape[1])))


_PACK = ("norm_mix", "mem_norm", "norm_ffn", "gdn_conv", "fox_q_norm", "fox_k_norm", "gdn_out_norm", "mem_q_norm",
         "mem_k_norm", "fox_f_bias", "gdn_a_log", "gdn_dt_bias", "loss")


def _pack(vals):
    parts = [vals[n].reshape(-1, HD) for n in _PACK]
    used = sum(p.shape[0] for p in parts)
    buf = jnp.concatenate(parts + [jnp.zeros((-used % 8, HD), F32)], axis=0)
    return buf, [(n, p.shape[0]) for n, p in zip(_PACK, parts)]


def _unpack(buf, layout):
    out, at = {}, 0
    for n, rows in layout:
        out[n] = buf[at:at + rows]
        at += rows
    return out


def kernel(x, mem, norm_mix, w_in, fox_f_bias, fox_q_norm, fox_k_norm, gdn_conv, gdn_a_log, gdn_dt_bias, gdn_out_norm, mem_norm, w_mem_kv, mem_q_norm, mem_k_norm, w_out, norm_ffn, w_gate_up, w_down, loss_target, m_norm_mix, m_w_in, m_fox_f_bias, m_fox_q_norm, m_fox_k_norm, m_gdn_conv, m_gdn_a_log, m_gdn_dt_bias, m_gdn_out_norm, m_mem_norm, m_w_mem_kv, m_mem_q_norm, m_mem_k_norm, m_w_out, m_norm_ffn, m_w_gate_up, m_w_down, v_norm_mix, v_w_in, v_fox_f_bias, v_fox_q_norm, v_fox_k_norm, v_gdn_conv, v_gdn_a_log, v_gdn_dt_bias, v_gdn_out_norm, v_mem_norm, v_w_mem_kv, v_mem_q_norm, v_mem_k_norm, v_w_out, v_norm_ffn, v_w_gate_up, v_w_down):
    args = dict(locals())
    d = x.shape[2]
    me = 4 * lax.axis_index("x") + 2 * lax.axis_index("y") + lax.axis_index("c")

    cshard = gdn_conv[0].shape[1]
    conv_pad = jnp.pad(gdn_conv[0], ((0, 4), (0, 3 * HD - cshard)))
    conv_all = _all_reduce_small("ag_conv", conv_pad, False).reshape(N_DEV, 8, 3 * HD)[:, :4, :cshard]
    conv_all = conv_all.transpose(1, 0, 2).reshape(4, N_DEV * cshard)
    w_in_a, w_in_b = _perm_in(w_in[0])
    comm = _StepComm({"in_b": [w_in_b], "in_a": [w_in_a], "kv_out": [w_mem_kv[0], w_out[0]], "gate_up": [w_gate_up[0]],
                      "down": [w_down[0]]}, [conv_all])

    grad_x, loss_local, small_grads = _local_step(
        x[0], mem[0], loss_target[0], norm_mix, fox_f_bias, fox_q_norm, fox_k_norm, gdn_a_log, gdn_dt_bias,
        gdn_out_norm, mem_norm, mem_q_norm, mem_k_norm, norm_ffn, conv_all, comm)

    wmv = lambda n: (args[n][0], args["m_" + n][0], args["v_" + n][0])
    red = comm.finish([grad_x], {"ffn": [wmv("w_down"), wmv("w_gate_up")], "a": [None, wmv("w_out"), wmv("w_mem_kv")],
                                 "b": [None]})
    updated = {"w_down": red["ffn"][0], "w_gate_up": red["ffn"][1], "w_out": red["a"][1], "w_mem_kv": red["a"][2]}
    grads = {n: r[0] for n, r in updated.items()}
    grads["w_in"] = _unperm_in(red["a"][0], red["b"][0])
    small_grads["loss"] = jnp.broadcast_to(loss_local, (1, HD))
    packed, layout = _pack(small_grads)
    small = _unpack(_all_reduce_small("ar_small", packed, True), layout)
    loss = small["loss"][0, 0]
    six = {"fox_f_bias": L_FF, "gdn_a_log": L_GA, "gdn_dt_bias": L_GA}
    for n, rows_n in layout[:-1]:
        gsm = small[n]
        if n == "gdn_conv":
            gsm = lax.dynamic_slice(gsm.reshape(4, N_DEV * cshard), (0, me * cshard), (4, cshard))[None]
        elif n in six:
            gsm = gsm[:, six[n]:six[n] + 6]
        else:
            gsm = gsm.reshape(1, rows_n * HD)
        grads[n] = gsm

    names = ['norm_mix', 'w_in', 'fox_f_bias', 'fox_q_norm', 'fox_k_norm', 'gdn_conv', 'gdn_a_log', 'gdn_dt_bias',
             'gdn_out_norm', 'mem_norm', 'w_mem_kv', 'mem_q_norm', 'mem_k_norm', 'w_out', 'norm_ffn', 'w_gate_up', 'w_down']
    big = ("w_in", "w_mem_kv", "w_out", "w_gate_up", "w_down")
    delta, new_m, new_v = {}, {}, {}
    for n in big:
        res = updated[n][1:] if n in updated else _adamw_call("adamw_" + n, args[n][0], grads[n], *wmv(n)[1:])
        delta[n], new_m[n], new_v[n] = [a[None] for a in res]
        grads[n] = grads[n][None]

    def flat(a):
        a = a.reshape(1, -1)
        return jnp.pad(a, ((0, 0), (0, -a.shape[1] % HD))).reshape(-1, HD)

    smalls = [n for n in names if n not in big]
    pk = lambda pre: jnp.concatenate([flat(grads[n] if pre == "g" else args[pre + n]) for n in smalls], axis=0)
    cat = [pk(""), pk("g"), pk("m_"), pk("v_")]
    padr = -cat[0].shape[0] % 8
    cat = [jnp.pad(a, ((0, padr), (0, 0))) for a in cat]
    res = _adamw_call("adamw_small", *cat)
    at = 0
    for n in smalls:
        shape = args[n].shape
        size = math.prod(shape)
        nrow = -(-size // HD)
        for dst, src in zip((delta, new_m, new_v), res):
            dst[n] = src[at:at + nrow].reshape(-1)[:size].reshape(shape)
        at += nrow

    return (loss, grad_x[None], *[grads[n] for n in names], *[delta[n] for n in names],
            *[new_m[n] for n in names], *[new_v[n] for n in names])


class _StepComm:
    def __init__(self, shard_groups, after):
        self.groups, shards = {}, []
        for key, ws in shard_groups.items():
            self.groups[key] = list(range(len(shards), len(shards) + len(ws)))
            shards += [w.astype(BF16) for w in ws]
        self.gather = _Gather("ag", shards, after)
        self.passed, self.scatters = set(), {}

    def start_deps(self):
        return [self.gather.token]

    def pass_on(self, key, after):
        self.passed.add(key)
        return self.gather.pass_on(self.groups[key], after)

    def weights(self, key, after):
        if key not in self.passed:
            after = self.pass_on(key, after)
        return self.gather.get(self.groups[key], after)

    def send(self, tag, grads):
        blocks = [g if g.ndim == 3 else g.reshape(N_DEV, g.shape[0] // N_DEV, g.shape[1]) for g in grads]
        self.scatters[tag] = _Scatter("rs_" + tag, blocks, ())
        return [self.scatters[tag].token]

    def mid(self, tag, after):
        self.scatters[tag].mid(after)
        return [self.scatters[tag].token]

    def finish(self, after, wmv):
        return {tag: sc.end(after, wmv[tag]) for tag, sc in self.scatters.items()}


def _local_step(xs, ms, tgt, norm_mix, fox_f_bias, fox_q_norm, fox_k_norm, gdn_a_log, gdn_dt_bias, gdn_out_norm,
                mem_norm, mem_q_norm, mem_k_norm, norm_ffn, conv_all, comm):
    t, d = xs.shape
    bq = min(t, 256)
    fb, alog, dtb = _lanes(fox_f_bias, L_FF), _lanes(gdn_a_log, L_GA), _lanes(gdn_dt_bias, L_GA)
    flat = lambda w: w.reshape(-1, w.shape[-1])

    rms1 = lambda a, g: (_rms(a, g),)
    (u,) = _rowwise("norm_mix", rms1, [xs], [norm_mix], [(d, BF16)], min(t, 256), deps=comm.start_deps())
    w_in_b = flat(comm.weights("in_b", [u])[0])
    pb = _matmul("proj_in_b", u, w_in_b, NN, F32, 1024, 768)
    o_fox = _fox_fwd(pb, fb, fox_q_norm, fox_k_norm, bq)
    w_in_a = flat(comm.weights("in_a", [o_fox])[0])
    pa = _matmul("proj_in_a", u, w_in_a, NN, F32, 1024, 768)
    smrow = (pb, HD, SM)
    (gates,) = _rowwise("gdn_gates", _gdn_gates, [smrow], [alog, dtb], [(HD, F32)], min(t, 256))
    gdn_terms, gdn_qkv = _gdn_fwd(pa, gates, conv_all)
    o_gdn_raw, gdn_states = _gdn_scan(gdn_terms)
    gdn_saved = list(gdn_terms) + [gdn_states]
    zrow = (pa, NG * HD, GZ * HD // (NG * HD))
    (o_gdn,) = _rowwise("gdn_post", _gdn_post, [o_gdn_raw, zrow], [gdn_out_norm], [(NG * HD, BF16)], min(t, 256))
    w_kv_all, w_out_all = [flat(w) for w in comm.weights("kv_out", [o_gdn])]
    (mem_n,) = _rowwise("norm_mem", rms1, [ms], [mem_norm], [(d, BF16)], ms.shape[0])
    mkv = _matmul("proj_mem", mem_n, w_kv_all, NN, F32, 256, 512)
    o_mem = _mem_fwd(pb, mkv, mem_q_norm, mem_k_norm)
    deps = comm.pass_on("gate_up", [o_mem])
    mix = jnp.concatenate([o_fox, o_gdn, o_mem], axis=1)
    h1, h1n = _proj_out_norm(mix, w_out_all, xs, norm_ffn, deps)
    (wgu,) = comm.weights("gate_up", [h1n])
    ffw = wgu.shape[2]
    gu, act = _ffn_up(h1n, wgu.reshape(2, 4, d, ffw))
    w_down_all = flat(comm.weights("down", [act])[0])
    dyb, lsum = _ffn_down_loss(act, w_down_all, h1, tgt)
    loss_local = (0.5 / d) * jnp.sum(lsum[::8, ::HD])

    dgu = _ffn_down_bwd(dyb, w_down_all.reshape(4, ffw, d), gu).reshape(8, t, ffw)
    g_w_down = _matmul("grad_w_down", act, dyb, TN, BF16, 512, 2048)
    g_w_gu = _ffn_up_bwd_w(h1n, dgu)
    deps = comm.send("ffn", [g_w_down, g_w_gu])
    rms2 = lambda a, g: (_rms(a, g), a)
    dh1b, g_norm_ffn = _ffn_up_bwd_x(dgu, wgu, h1, norm_ffn, dyb, deps)

    dmix = _matmul("proj_out_bwd_x", dh1b, w_out_all, NT, BF16, 1024, 1024)
    g_w_out = _matmul("grad_w_out", mix, dh1b, TN, BF16, 1024, 2048)
    deps = comm.mid("ffn", [dmix, g_w_out])
    dmq, dmk, dmv, g_mqn, g_mkn = _mem_bwd(pb, mkv, mem_q_norm, mem_k_norm, dmix, deps=deps)
    dmkv = jnp.concatenate([dmk, dmv], axis=1).astype(BF16)
    g_w_kv = _matmul("grad_w_kv", mem_n, dmkv, TN, BF16, 512, 512)
    do_raw, dgz, g_gon = _rowwise_vjp("gdn_post_bwd", _gdn_post, [o_gdn_raw, zrow], [gdn_out_norm],
                                      [(dmix, NG * HD, 1)], [F32, BF16], min(t, 256), deps=deps)
    dterms = _gdn_bwd_scan(gdn_saved, do_raw)
    dgq, dgk, dgv, dgates, dwq, dwk, dwv = _gdn_bwd(pa, gates, conv_all, dterms, gdn_qkv)
    dsm_gdn, g_alog, g_dtb = _rowwise_vjp("gdn_gates_bwd", _gdn_gates, [smrow], [alog, dtb], [dgates], [F32], min(t, 256))
    dp_a = jnp.concatenate([dgq, dgk, dgv, dgz], axis=1)
    g_w_in_a = _matmul("grad_w_in_a", u, dp_a, TN, BF16, 512, 3072)
    deps = comm.send("a", [g_w_in_a, g_w_out, g_w_kv])
    du_a = _matmul("proj_in_bwd_a", dp_a, w_in_a, NT, F32, 1024, 1024, deps=deps)
    deps = comm.mid("a", [du_a])
    dfq, dfk, dfv, dsm_fox, g_fb, g_fqn, g_fkn = _fox_bwd(pb, fb, fox_q_norm, fox_k_norm, dmix, 2 * bq if t % (2 * bq) == 0 else bq,
                                                          deps=deps)
    dp_b = jnp.concatenate([dfq, dfk, dfv, dmq, (dsm_fox + dsm_gdn).astype(BF16), jnp.zeros((t, HD), BF16)], axis=1)
    g_w_in_b = _matmul("grad_w_in_b", u, dp_b, TN, BF16, 512, 3072)
    deps = comm.send("b", [g_w_in_b])
    dmem_n = _matmul("proj_mem_bwd_x", dmkv, w_kv_all, NT, F32, 256, 512, deps=deps)
    g_mem_norm = _rowwise_vjp("norm_mem_bwd", rms1, [ms], [mem_norm], [dmem_n], [], ms.shape[0])[0]
    deps = comm.mid("b", [g_mem_norm])
    grad_x, g_norm_mix = _proj_in_bwd_norm(dp_b, w_in_b, du_a, xs, norm_mix, dh1b, deps)

    small_grads = {
        "norm_mix": g_norm_mix, "mem_norm": g_mem_norm, "norm_ffn": g_norm_ffn,
        "gdn_conv": jnp.concatenate([dwq, dwk, dwv], axis=1),
        "fox_q_norm": g_fqn, "fox_k_norm": g_fkn, "gdn_out_norm": g_gon, "mem_q_norm": g_mqn, "mem_k_norm": g_mkn,
        "fox_f_bias": g_fb, "gdn_a_log": g_alog, "gdn_dt_bias": g_dtb}
    return grad_x, loss_local, small_grads
```

```python
import functools
import math

import jax
import jax.numpy as jnp
from jax import lax
from jax.experimental import pallas as pl
from jax.experimental.pallas import tpu as pltpu

F32 = jnp.float32
BF16 = jnp.bfloat16
SDS = jax.ShapeDtypeStruct

N_DEV = 8
HD = 128
NF, NG, NM = 6, 6, 4
CHUNK = 64
GROUP = 16
NORM_EPS = 1e-6
GQ, GK, GV, GZ = 0, 6, 12, 18
FQ, FK, FV, MQ, SM = 0, 6, 12, 18, 22
HALF = 24 * HD
L_FF, L_GA, L_GB = 0, 6, 12
VMEM_LIMIT = 56 * 1024 * 1024

ADAM_LR, ADAM_B1, ADAM_B2, ADAM_EPS, ADAM_WD, ADAM_STEP = 0.001, 0.9, 0.999, 1e-08, 0.01, 10

NN = (((1,), (0,)), ((), ()))
NT = (((1,), (1,)), ((), ()))
TN = (((0,), (0,)), ((), ()))
MESH = pl.DeviceIdType.MESH


def _cp(*sem):
    return pltpu.CompilerParams(dimension_semantics=tuple(sem) if sem else None, vmem_limit_bytes=VMEM_LIMIT)


def _dot(a, b, dims=NN):
    return lax.dot_general(a, b, dims, preferred_element_type=F32)


def _iota(shape, axis):
    return lax.broadcasted_iota(jnp.int32, shape, axis)


def _rms(x, gain):
    return x * lax.rsqrt(jnp.mean(x * x, axis=-1, keepdims=True) + NORM_EPS) * gain


def _sigmoid(x):
    return 0.5 * jnp.tanh(0.5 * x) + 0.5


def _silu(x):
    return x * _sigmoid(x)


def _softplus(x):
    return jnp.maximum(x, 0.0) + jnp.log(1.0 + jnp.exp(-jnp.abs(x)))


def _lane_pick(x, lane):
    oh = (_iota((1, x.shape[-1]), 1) == lane).astype(F32)
    return jnp.sum(x * oh, axis=-1, keepdims=True)


def _cumsum_rows(x):
    tril = (_iota((HD, HD), 0) >= _iota((HD, HD), 1)).astype(F32)
    carry = jnp.zeros((1, x.shape[1]), F32)
    outs = []
    for b in range(x.shape[0] // HD):
        blk = x[b * HD:(b + 1) * HD]
        outs.append(_pdot(tril, blk, "nn", "xa") + carry)
        carry = carry + jnp.sum(blk, axis=0, keepdims=True)
    return jnp.concatenate(outs, axis=0)


def _row_spec(r, tm):
    if isinstance(r, tuple):
        arr, width, cb = r
        return arr, pl.BlockSpec((tm, width), lambda i, cb=cb: (i, cb))
    return r, pl.BlockSpec((tm, r.shape[1]), lambda i: (i, 0))


ANY_SPEC = pl.BlockSpec(memory_space=pl.ANY)


def _rowwise(name, fn, rows, consts, outs, tm, deps=()):
    arrs, specs = zip(*[_row_spec(r, tm) for r in rows])
    n_rows = arrs[0].shape[0]
    nr, nc, nd = len(rows), len(consts), len(deps)

    def body(*refs):
        res = fn(*[r[...] for r in refs[:nr + nc]])
        for o, v in zip(refs[nr + nc + nd:], res):
            o[...] = v.astype(o.dtype)

    return pl.pallas_call(
        body, grid=(n_rows // tm,), name=name,
        in_specs=list(specs) + [pl.BlockSpec(c.shape, lambda i: (0, 0)) for c in consts] + [ANY_SPEC] * nd,
        out_specs=[pl.BlockSpec((tm, w), lambda i: (i, 0)) for w, _ in outs],
        out_shape=[SDS((n_rows, w), dt) for w, dt in outs],
        compiler_params=_cp("parallel"),
    )(*arrs, *consts, *deps)


def _rowwise_vjp(name, fn, rows, consts, cts, grad_dtypes, tm, deps=()):
    arrs, specs = zip(*[_row_spec(r, tm) for r in rows])
    ct_arrs, ct_specs = zip(*[_row_spec(r, tm) for r in cts])
    n_rows = arrs[0].shape[0]
    nr, nc, nct, nd = len(rows), len(consts), len(cts), len(deps)
    plan = [(j, dt) for j, dts in enumerate(grad_dtypes) for dt in (dts if isinstance(dts, tuple) else (dts,))]
    ng = len(plan)
    widths = [specs[j].block_shape[1] for j, _ in plan]
    grad_dtypes = [dt for _, dt in plan]

    def body(*refs):
        vals = [r[...].astype(F32) for r in refs[:nr + nc]]
        ctv = tuple(r[...].astype(F32) for r in refs[nr + nc:nr + nc + nct])
        _, vjp = jax.vjp(fn, *vals)
        grads = vjp(ctv)
        outs = refs[nr + nc + nct + nd:]
        for o, (j, _) in zip(outs[:ng], plan):
            o[...] = grads[j].astype(o.dtype)

        @pl.when(pl.program_id(0) == 0)
        def _():
            for o in outs[ng:]:
                o[...] = jnp.zeros_like(o)

        for o, g in zip(outs[ng:], grads[nr:]):
            o[...] += g

    return pl.pallas_call(
        body, grid=(n_rows // tm,), name=name,
        in_specs=list(specs) + [pl.BlockSpec(c.shape, lambda i: (0, 0)) for c in consts] + list(ct_specs)
        + [ANY_SPEC] * nd,
        out_specs=[pl.BlockSpec((tm, w), lambda i: (i, 0)) for w in widths]
        + [pl.BlockSpec(c.shape, lambda i: (0, 0)) for c in consts],
        out_shape=[SDS((n_rows, w), dt) for w, dt in zip(widths, grad_dtypes)] + [SDS(c.shape, F32) for c in consts],
        compiler_params=_cp("arbitrary"),
    )(*arrs, *consts, *ct_arrs, *deps)


def _tile(n, pref):
    t = min(n, pref)
    while n % t or (t % HD and t != n):
        t -= 1
    return t


def _matmul(name, a, b, dims, out_dtype, tm, tn, residual=None, deps=()):
    ta, tb = dims == TN, dims == NT
    m = a.shape[1] if ta else a.shape[0]
    k = a.shape[0] if ta else a.shape[1]
    n = b.shape[0] if tb else b.shape[1]
    tm, tn = _tile(m, tm), _tile(n, tn)

    def body(*refs):
        acc = _dot(refs[0][...], refs[1][...], dims)
        if residual is not None:
            acc = acc + refs[2][...]
        refs[-1][...] = acc.astype(out_dtype)

    in_specs = [pl.BlockSpec((k, tm), lambda i, j: (0, i)) if ta else pl.BlockSpec((tm, k), lambda i, j: (i, 0)),
                pl.BlockSpec((tn, k), lambda i, j: (j, 0)) if tb else pl.BlockSpec((k, tn), lambda i, j: (0, j))]
    ops = [a, b]
    if residual is not None:
        in_specs.append(pl.BlockSpec((tm, tn), lambda i, j: (i, j)))
        ops.append(residual)
    in_specs += [ANY_SPEC] * len(deps)
    ops += list(deps)
    return pl.pallas_call(
        body, grid=(m // tm, n // tn), name=name, in_specs=in_specs,
        out_specs=pl.BlockSpec((tm, tn), lambda i, j: (i, j)), out_shape=SDS((m, n), out_dtype),
        compiler_params=_cp("parallel", "parallel"),
    )(*ops)


def _proj_out_norm(mix, w_out, xs, gain, deps):
    t, k = mix.shape
    d = w_out.shape[1]
    tm = _tile(t, 512)

    def body(*refs):
        a, b, x, g = refs[:4]
        h1, h1n = refs[4 + len(deps):]
        acc = _dot(a[...], b[...]) + x[...]
        h1[...] = acc
        h1n[...] = _rms(acc, g[...]).astype(BF16)

    return pl.pallas_call(
        body, grid=(t // tm,), name="proj_out",
        in_specs=[pl.BlockSpec((tm, k), lambda i: (i, 0)), pl.BlockSpec((k, d), lambda i: (0, 0)),
                  pl.BlockSpec((tm, d), lambda i: (i, 0)), pl.BlockSpec((1, d), lambda i: (0, 0))] + [ANY_SPEC] * len(deps),
        out_specs=[pl.BlockSpec((tm, d), lambda i: (i, 0))] * 2, out_shape=[SDS((t, d), F32), SDS((t, d), BF16)],
        compiler_params=_cp("parallel"),
    )(mix, w_out, xs, gain, *deps)


def _proj_in_bwd_norm(dp, w, du_a, xs, gain, dh1b, deps):
    t, k = dp.shape
    d = w.shape[0]
    tm = _tile(t, 256)

    def body(*refs):
        a, b, ua, x, g, dh = refs[:6]
        gx, dgain = refs[6 + len(deps):]
        _, vjp = jax.vjp(lambda xx, gn: _rms(xx, gn), x[...], g[...])
        dx, dg = vjp(_dot(a[...], b[...], NT) + ua[...])
        gx[...] = dx + dh[...].astype(F32)

        @pl.when(pl.program_id(0) == 0)
        def _():
            dgain[...] = jnp.zeros_like(dgain)

        dgain[...] += dg

    row = pl.BlockSpec((tm, d), lambda i: (i, 0))
    vec = pl.BlockSpec((1, d), lambda i: (0, 0))
    return pl.pallas_call(
        body, grid=(t // tm,), name="proj_in_bwd_b",
        in_specs=[pl.BlockSpec((tm, k), lambda i: (i, 0)), pl.BlockSpec((d, k), lambda i: (0, 0), pipeline_mode=ONE_BUFFER),
                  row, row, vec, row] + [ANY_SPEC] * len(deps),
        out_specs=[row, vec], out_shape=[SDS((t, d), F32), SDS((1, d), F32)], compiler_params=_cp("arbitrary"),
    )(dp, w, du_a, xs, gain, dh1b, *deps)


def _ffn_up(h1n, wgu):
    t, d = h1n.shape
    w = wgu.shape[3]
    tm = _tile(t, 512)

    def body(a, b, gu, act):
        x = a[...]
        g = _dot(x, b[0])
        u = _dot(x, b[1])
        gu[0] = g.astype(BF16)
        gu[1] = u.astype(BF16)
        act[...] = (_silu(g) * u).astype(BF16)

    return pl.pallas_call(
        body, grid=(4, t // tm), name="ffn_up",
        in_specs=[pl.BlockSpec((tm, d), lambda j, i: (i, 0)), pl.BlockSpec((2, None, d, w), lambda j, i: (0, j, 0, 0))],
        out_specs=[pl.BlockSpec((2, None, tm, w), lambda j, i: (0, j, i, 0)), pl.BlockSpec((tm, w), lambda j, i: (i, j))],
        out_shape=[SDS((2, 4, t, w), BF16), SDS((t, 4 * w), BF16)],
        compiler_params=_cp("parallel", "parallel"),
    )(h1n, wgu)


def _ffn_down_loss(act, wdown, h1, target):
    t, f = act.shape
    d = wdown.shape[1]
    tm, tn = _tile(t, 1024), _tile(d, 512)

    def body(a, b, h, tg, dyb, ls):
        e = _dot(a[...], b[...]) + h[...] - tg[...]
        dyb[...] = (e * (1.0 / d)).astype(BF16)
        ls[...] = jnp.broadcast_to(jnp.sum(e * e), (8, HD))

    return pl.pallas_call(
        body, grid=(t // tm, d // tn), name="ffn_down_loss",
        in_specs=[pl.BlockSpec((tm, f), lambda i, j: (i, 0)), pl.BlockSpec((f, tn), lambda i, j: (0, j)),
                  pl.BlockSpec((tm, tn), lambda i, j: (i, j)), pl.BlockSpec((tm, tn), lambda i, j: (i, j))],
        out_specs=[pl.BlockSpec((tm, tn), lambda i, j: (i, j)), pl.BlockSpec((8, HD), lambda i, j: (i, j))],
        out_shape=[SDS((t, d), BF16), SDS((8 * (t // tm), HD * (d // tn)), F32)],
        compiler_params=_cp("parallel", "parallel"),
    )(act, wdown, h1, target)


def _ffn_down_bwd(dyb, wdown4, gu):
    t, d = dyb.shape
    w = wdown4.shape[1]
    tm = _tile(t, 512)

    def body(a, b, gu_ref, out):
        da = _dot(a[...], b[...], NT)
        g = gu_ref[0].astype(F32)
        u = gu_ref[1].astype(F32)
        s = _sigmoid(g)
        out[0] = (da * u * (s * (1.0 + g * (1.0 - s)))).astype(BF16)
        out[1] = (da * g * s).astype(BF16)

    return pl.pallas_call(
        body, grid=(4, t // tm), name="ffn_down_bwd",
        in_specs=[pl.BlockSpec((tm, d), lambda j, i: (i, 0)), pl.BlockSpec((None, w, d), lambda j, i: (j, 0, 0)),
                  pl.BlockSpec((2, None, tm, w), lambda j, i: (0, j, i, 0))],
        out_specs=pl.BlockSpec((2, None, tm, w), lambda j, i: (0, j, i, 0)),
        out_shape=SDS((2, 4, t, w), BF16),
        compiler_params=_cp("parallel", "parallel"),
    )(dyb, wdown4, gu)


def _ffn_up_bwd_x(dgu, wgu, h1, gain, dyb, deps):
    _, t, w = dgu.shape
    d = wgu.shape[1]
    tm = _tile(t, 512)

    def body(*refs):
        a, b, h, g, dy = refs[:5]
        dh1, dgain, acc = refs[5 + len(deps):]
        i, j = pl.program_id(0), pl.program_id(1)

        @pl.when(j == 0)
        def _():
            acc[...] = jnp.zeros_like(acc)

        acc[...] += _dot(a[...], b[...], NT)

        @pl.when(j == N_DEV - 1)
        def _():
            _, vjp = jax.vjp(lambda x, gn: _rms(x, gn), h[...], g[...])
            dx, dg = vjp(acc[...])
            dh1[...] = (dx + dy[...].astype(F32)).astype(dh1.dtype)

            @pl.when(i == 0)
            def _():
                dgain[...] = jnp.zeros_like(dgain)

            dgain[...] += dg

    row = pl.BlockSpec((tm, d), lambda i, j: (i, 0))
    return pl.pallas_call(
        body, grid=(t // tm, N_DEV), name="ffn_up_bwd_x",
        in_specs=[pl.BlockSpec((None, tm, w), lambda i, j: (j, i, 0)), pl.BlockSpec((None, d, w), lambda i, j: (j, 0, 0)),
                  row, pl.BlockSpec((1, d), lambda i, j: (0, 0)), row] + [ANY_SPEC] * len(deps),
        out_specs=[row, pl.BlockSpec((1, d), lambda i, j: (0, 0))],
        out_shape=[SDS((t, d), BF16), SDS((1, d), F32)], scratch_shapes=[pltpu.VMEM((tm, d), F32)],
        compiler_params=_cp("arbitrary", "arbitrary"),
    )(dgu, wgu, h1, gain, dyb, *deps)


def _ffn_up_bwd_w(h1n, dgu):
    _, t, w = dgu.shape
    d = h1n.shape[1]
    tm = _tile(d, 512)

    def body(a, b, out):
        out[...] = _dot(a[...], b[...], TN).astype(BF16)

    return pl.pallas_call(
        body, grid=(8, d // tm), name="ffn_up_bwd_w",
        in_specs=[pl.BlockSpec((t, tm), lambda j, i: (0, i)), pl.BlockSpec((None, t, w), lambda j, i: (j, 0, 0))],
        out_specs=pl.BlockSpec((None, tm, w), lambda j, i: (j, i, 0)), out_shape=SDS((8, d, w), BF16),
        compiler_params=_cp("parallel", "parallel"),
    )(h1n, dgu)


def _fox_prep(fq, fk, sm, fb, qg, kg, h):
    qn = _rms(fq, qg)
    kn = _rms(fk, kg)
    c = _cumsum_rows(-_softplus(-(sm + fb)))
    ccol = _lane_pick(c, L_FF + h)
    crow = jnp.sum(c.T * (_iota((HD, 1), 0) == L_FF + h).astype(F32), axis=0, keepdims=True)
    return qn, kn, ccol, crow


def _softmax_times(s, v):
    e = jnp.exp(s - lax.stop_gradient(jnp.max(s, axis=1, keepdims=True)))
    return _dot(e.astype(BF16), v.astype(BF16)) * (1.0 / jnp.sum(e, axis=1, keepdims=True))


def _fox_block(q, k, v, cc, cr, off):
    bq = q.shape[0]
    assert k.shape[0] == off + bq
    s = _dot((q * (HD ** -0.5)).astype(BF16), k.astype(BF16), NT) + cc - cr
    diag = jnp.where(_iota((bq, bq), 1) <= _iota((bq, bq), 0), s[:, off:], -1e30)
    s = jnp.concatenate([s[:, :off], diag], axis=1) if off else diag
    return _softmax_times(s, v)


ONE_BUFFER = pl.Buffered(1)


def _pcol(t, cb):
    return pl.BlockSpec((t, HD), lambda h, cb=cb: (0, cb + h), pipeline_mode=ONE_BUFFER)


def _smcol(t):
    return pl.BlockSpec((t, HD), lambda h: (0, SM), pipeline_mode=ONE_BUFFER)


def _head(t):
    return pl.BlockSpec((t, HD), lambda h: (0, h), pipeline_mode=ONE_BUFFER)


def _small(n):
    return pl.BlockSpec((n, HD), lambda h: (0, 0), pipeline_mode=ONE_BUFFER)


def _fox_fwd(p, fb, qg, kg, bq):
    t = p.shape[0]

    def body(fq, fk, fv, sm, fb_r, qg_r, kg_r, o, qn_s, cc_s):
        h = pl.program_id(0)
        qn, kn, ccol, crow = _fox_prep(fq[...], fk[...], sm[...], fb_r[...], qg_r[...], kg_r[...], h)
        qn_s[...] = qn
        cc_s[...] = ccol
        knb = kn.astype(BF16)
        vb = fv[...].astype(BF16)
        for i in range(t // bq):
            rows, ext = pl.ds(i * bq, bq), (i + 1) * bq
            o[rows, :] = _fox_block(qn_s[rows, :], knb[:ext], vb[:ext], cc_s[rows, :], crow[:, :ext], i * bq).astype(o.dtype)

    return pl.pallas_call(
        body, grid=(NF,), name="fox_fwd",
        in_specs=[_pcol(t, FQ), _pcol(t, FK), _pcol(t, FV), _smcol(t), _small(1), _small(1), _small(1)],
        out_specs=_head(t), out_shape=SDS((t, NF * HD), BF16),
        scratch_shapes=[pltpu.VMEM((t, HD), F32), pltpu.VMEM((t, 1), F32)],
        compiler_params=_cp("parallel"),
    )(p, p, p, p, fb, qg, kg)


def _fox_bwd(p, fb, qg, kg, dmix, bq, deps=()):
    t = p.shape[0]

    def body(*refs):
        fq, fk, fv, sm, fb_r, qg_r, kg_r, do = refs[:8]
        dfq, dfk, dfv, dsm, dfb, dqg, dkg, qn_s, cc_s, dqn_s, dcc_s, dkn_s, dv_s, dcr_s = refs[8 + len(deps):]
        h = pl.program_id(0)
        qn, kn, ccol, crow = _fox_prep(fq[...], fk[...], sm[...], fb_r[...], qg_r[...], kg_r[...], h)
        qn_s[...] = qn
        cc_s[...] = ccol
        v = fv[...]
        dkn_s[...] = jnp.zeros_like(dkn_s)
        dv_s[...] = jnp.zeros_like(dv_s)
        dcr_s[...] = jnp.zeros_like(dcr_s)

        for i in range(t // bq):
            rows, ext = pl.ds(i * bq, bq), (i + 1) * bq
            _, vjp = jax.vjp(lambda a, b, c, d, e, off=i * bq: _fox_block(a, b, c, d, e, off),
                             qn_s[rows, :], kn[:ext], v[:ext], cc_s[rows, :], crow[:, :ext])
            dq, dk, dv, dcc, dcr = vjp(do[rows, :].astype(F32))
            dqn_s[rows, :] = dq
            dcc_s[rows, :] = dcc
            dkn_s[:ext, :] += dk
            dv_s[:ext, :] += dv
            dcr_s[:, :ext] += dcr
        _, prep_vjp = jax.vjp(lambda a, b, c, d, e, f: _fox_prep(a, b, c, d, e, f, h),
                              fq[...], fk[...], sm[...], fb_r[...], qg_r[...], kg_r[...])
        g_fq, g_fk, g_sm, g_fb, g_qg, g_kg = prep_vjp((dqn_s[...], dkn_s[...], dcc_s[...], dcr_s[...]))
        dfq[...] = g_fq.astype(dfq.dtype)
        dfk[...] = g_fk.astype(dfk.dtype)
        dfv[...] = dv_s[...].astype(dfv.dtype)

        @pl.when(h == 0)
        def _():
            for r in (dsm, dfb, dqg, dkg):
                r[...] = jnp.zeros_like(r)

        dsm[...] += g_sm
        dfb[...] += g_fb
        dqg[...] += g_qg
        dkg[...] += g_kg

    head = _head(t)
    return pl.pallas_call(
        body, grid=(NF,), name="fox_bwd",
        in_specs=[_pcol(t, FQ), _pcol(t, FK), _pcol(t, FV), _smcol(t), _small(1), _small(1), _small(1), head]
        + [ANY_SPEC] * len(deps),
        out_specs=[head, head, head, _small(t), _small(1), _small(1), _small(1)],
        out_shape=[SDS((t, NF * HD), BF16)] * 3 + [SDS((t, HD), F32)] + [SDS((1, HD), F32)] * 3,
        scratch_shapes=[pltpu.VMEM((t, HD), F32), pltpu.VMEM((t, 1), F32), pltpu.VMEM((t, HD), F32),
                        pltpu.VMEM((t, 1), F32), pltpu.VMEM((t, HD), F32), pltpu.VMEM((t, HD), F32),
                        pltpu.VMEM((1, t), F32)],
        compiler_params=_cp("arbitrary"),
    )(p, p, p, p, fb, qg, kg, dmix, *deps)


def _mem_attn(mq, mk, mv, qg, kg):
    s = _dot((_rms(mq, qg) * (HD ** -0.5)).astype(BF16), _rms(mk, kg).astype(BF16), NT)
    return _softmax_times(s, mv)


def _mem_fwd(p, mkv, qg, kg):
    t, ml = p.shape[0], mkv.shape[0]

    def body(mq, mk, mv, qg_r, kg_r, o):
        o[...] = _mem_attn(mq[...], mk[...], mv[...], qg_r[...], kg_r[...]).astype(o.dtype)

    return pl.pallas_call(
        body, grid=(NM,), name="mem_fwd",
        in_specs=[_pcol(t, MQ), pl.BlockSpec((ml, HD), lambda h: (0, h)), pl.BlockSpec((ml, HD), lambda h: (0, NM + h)),
                  _small(1), _small(1)],
        out_specs=pl.BlockSpec((t, HD), lambda h: (0, h)), out_shape=SDS((t, NM * HD), BF16),
        compiler_params=_cp("parallel"),
    )(p, mkv, mkv, qg, kg)


def _mem_bwd(p, mkv, qg, kg, dmix, deps=()):
    t, ml = p.shape[0], mkv.shape[0]

    def body(*refs):
        mq, mk, mv, qg_r, kg_r, do = refs[:6]
        dmq, dmk, dmv, dqg, dkg = refs[6 + len(deps):]
        _, vjp = jax.vjp(_mem_attn, mq[...], mk[...], mv[...], qg_r[...], kg_r[...])
        g_q, g_k, g_v, g_qg, g_kg = vjp(do[...].astype(F32))
        dmq[...] = g_q.astype(dmq.dtype)
        dmk[...] = g_k
        dmv[...] = g_v

        @pl.when(pl.program_id(0) == 0)
        def _():
            dqg[...] = jnp.zeros_like(dqg)
            dkg[...] = jnp.zeros_like(dkg)

        dqg[...] += g_qg
        dkg[...] += g_kg

    return pl.pallas_call(
        body, grid=(NM,), name="mem_bwd",
        in_specs=[_pcol(t, MQ), pl.BlockSpec((ml, HD), lambda h: (0, h)), pl.BlockSpec((ml, HD), lambda h: (0, NM + h)),
                  _small(1), _small(1), pl.BlockSpec((t, HD), lambda h: (0, NF + NG + h))] + [ANY_SPEC] * len(deps),
        out_specs=[pl.BlockSpec((t, HD), lambda h: (0, h)), pl.BlockSpec((ml, HD), lambda h: (0, h)),
                   pl.BlockSpec((ml, HD), lambda h: (0, h)), _small(1), _small(1)],
        out_shape=[SDS((t, NM * HD), BF16), SDS((ml, NM * HD), F32), SDS((ml, NM * HD), F32),
                   SDS((1, HD), F32), SDS((1, HD), F32)],
        compiler_params=_cp("arbitrary"),
    )(p, mkv, mkv, qg, kg, dmix, *deps)


def _shift_down(x, s):
    if s == 0:
        return x
    return jnp.where(_iota(x.shape, 0) >= s, pltpu.roll(x, s, 0), 0.0)


def _shift_up(x, s):
    if s == 0:
        return x
    n = x.shape[0]
    return jnp.where(_iota(x.shape, 0) < n - s, pltpu.roll(x, n - s, 0), 0.0)


@jax.custom_vjp
def _conv4(x, w0, w1, w2, w3):
    return w0 * _shift_down(x, 3) + w1 * _shift_down(x, 2) + w2 * _shift_down(x, 1) + w3 * x


def _conv4_fwd(x, w0, w1, w2, w3):
    return _conv4(x, w0, w1, w2, w3), (x, w0, w1, w2, w3)


def _conv4_bwd(res, dy):
    x, w0, w1, w2, w3 = res
    ups = [_shift_up(dy, 3 - k) for k in range(4)]
    dx = w0 * ups[0] + w1 * ups[1] + w2 * ups[2] + w3 * ups[3]
    return (dx,) + tuple(jnp.sum(up * x, axis=0, keepdims=True) for up in ups)


_conv4.defvjp(_conv4_fwd, _conv4_bwd)


HALO = 8


def _gdn_gates(sm, alog, dtb):
    lane = _iota((1, HD), 1)
    g = -jnp.exp(alog) * _softplus(sm + dtb)
    return (jnp.where((lane >= L_GA) & (lane < L_GA + NG), g,
                      jnp.where((lane >= L_GB) & (lane < L_GB + NG), _sigmoid(sm), 0.0)),)


def _gdn_prep(gq, gk, gv, gates, taps, h):
    q, k, v = [_silu(_conv4(x, *taps[4 * j:4 * j + 4]))[HALO:] for j, x in enumerate((gq, gk, gv))]
    q = q * lax.rsqrt(jnp.sum(q * q, axis=-1, keepdims=True) + NORM_EPS) * (HD ** -0.5)
    k = k * lax.rsqrt(jnp.sum(k * k, axis=-1, keepdims=True) + NORM_EPS)
    return q, k, v, _lane_pick(gates, L_GA + h), _lane_pick(gates, L_GB + h)


def _split(x, n):
    parts, rest = [], x
    for i in range(n):
        parts.append(rest.astype(BF16))
        if i + 1 < n:
            rest = rest - parts[-1].astype(F32)
    return parts


def _raw_dot(a, b, form):
    lead = a.ndim - 2
    ca, cb = {"nn": (1, 0), "nt": (1, 1), "tn": (0, 0)}[form]
    batch = ((0,), (0,)) if lead else ((), ())
    return lax.dot_general(a, b, (((ca + lead,), (cb + lead,)), batch), preferred_element_type=F32)


def _pdot_impl(a, b, form, mode):
    if mode == "1":
        return _raw_dot(a.astype(BF16), b.astype(BF16), form)
    if mode == "3":
        (ah, al), (bh, bl) = _split(a, 2), _split(b, 2)
        return _raw_dot(ah, bh, form) + (_raw_dot(al, bh, form) + _raw_dot(ah, bl, form))
    if mode == "xa":
        return sum(_raw_dot(a.astype(BF16), t, form) for t in reversed(_split(b, 3)))
    return sum(_raw_dot(t, b.astype(BF16), form) for t in reversed(_split(a, 3)))


@functools.partial(jax.custom_vjp, nondiff_argnums=(2, 3))
def _pdot(a, b, form, mode):
    return _pdot_impl(a, b, form, mode)


def _pdot_fwd(a, b, form, mode):
    return _pdot_impl(a, b, form, mode), (a, b)


def _pdot_bwd(form, mode, res, ct):
    a, b = res
    da_args, db_args = {"nn": ((ct, b, "nt"), (a, ct, "tn")), "nt": ((ct, b, "nn"), (ct, a, "tn")),
                        "tn": ((b, ct, "nt"), (a, ct, "nn"))}[form]

    def side(args, exact):
        if mode in ("1", "3"):
            return mode
        return "xa" if args[0] is exact else "xb"

    if mode == "xa":
        return jnp.zeros_like(a), _pdot_impl(*db_args, side(db_args, a))
    if mode == "xb":
        return _pdot_impl(*da_args, side(da_args, b)), jnp.zeros_like(b)
    return _pdot_impl(*da_args, mode), _pdot_impl(*db_args, mode)


_pdot.defvjp(_pdot_fwd, _pdot_bwd)

GDN_QK, GDN_INV, GDN_SCAN = "1", "1", "1"


@jax.custom_vjp
def _tri_inv(low):
    eye = (_iota((CHUNK, CHUNK), 0) == _iota((CHUNK, CHUNK), 1)).astype(F32)
    inv = eye - low
    pw = low
    for _ in range(5):
        pw = _pdot_impl(pw, pw, "nn", GDN_INV)
        inv = inv + _pdot_impl(inv, pw, "nn", GDN_INV)
    return inv


def _tri_inv_fwd(low):
    inv = _tri_inv(low)
    return inv, inv


def _tri_inv_bwd(inv, ct):
    return (-_pdot_impl(_pdot_impl(inv, ct, "tn", GDN_INV), inv, "nt", GDN_INV),)


_tri_inv.defvjp(_tri_inv_fwd, _tri_inv_bwd)


def _gdn_intra(q, k, v, g, beta):
    n = q.shape[0]
    r, c = _iota((CHUNK, CHUNK), 0), _iota((CHUNK, CHUNK), 1)
    tril, strict = r >= c, r > c
    trilf = jnp.broadcast_to(tril.astype(F32), (n, CHUNK, CHUNK))
    gcm = _pdot(trilf, jnp.broadcast_to(g, (n, CHUNK, CHUNK)), "nn", "xa")
    gcf = _pdot(trilf, jnp.broadcast_to(g, (n, CHUNK, HD)), "nn", "xa")
    lane0 = (_iota((1, 1, CHUNK), 2) == 0).astype(F32)
    gcr = _pdot(jnp.ones((n, CHUNK, CHUNK), F32), gcm * lane0, "nt", "xa")
    decay = jnp.where(tril, jnp.exp(jnp.where(tril, gcm - gcr, 0.0)), 0.0)
    egc = jnp.exp(gcf)
    kb = k * beta
    low = jnp.where(strict, _pdot(kb, k, "nt", GDN_QK) * decay, 0.0)
    inv = _tri_inv(low)
    u = _pdot(inv, v * beta, "nn", GDN_INV)
    w = _pdot(inv, kb * egc, "nn", GDN_INV)
    at = jnp.where(tril, _pdot(q, k, "nt", GDN_QK) * decay, 0.0)
    gl = jnp.sum(jnp.broadcast_to(g, (n, CHUNK, HD)), axis=1, keepdims=True)
    kd = k * jnp.exp(gl - gcf)
    return (_pdot(kd, w, "tn", GDN_SCAN), _pdot(kd, u, "tn", GDN_SCAN), q * egc - _pdot(at, w, "nn", GDN_SCAN),
            _pdot(at, u, "nn", GDN_SCAN), gl)


def _gdn_step(s, kw, ku, a, b, gl):
    return _pdot(a, s, "nn", GDN_SCAN) + b, s * jnp.exp(gl) - _pdot(kw, s, "nn", GDN_SCAN) + ku


SCAN_HEADS = 3
SCAN_UNROLL = 4


def _gdn_chunked_scratch(nc):
    big = pltpu.VMEM((nc, CHUNK, HD), F32)
    return [big, big, big, pltpu.VMEM((nc, CHUNK, 1), F32), pltpu.VMEM((nc, CHUNK, 1), F32)]


N_TERMS = 5


def _gdn_term_shapes(nc):
    return [(nc, HD, HD), (nc, HD, HD), (nc, CHUNK, HD), (nc, CHUNK, HD), (nc, 1, HD)]


def _per_head(shape, heads=None, one_buffer=True):
    lead = (None,) if heads is None else (heads,)
    return pl.BlockSpec(lead + tuple(shape), lambda h: (h,) + (0,) * len(shape),
                        pipeline_mode=ONE_BUFFER if one_buffer else None)


def _gdn_in_specs(t):
    cw = lambda cb: pl.BlockSpec((4, HD), lambda h, cb=cb: (0, cb + h))
    return [_pcol(t, GQ), _pcol(t, GK), _pcol(t, GV), _small(t), cw(0), cw(NG), cw(2 * NG)]


def _taps(wq, wk, wv):
    return tuple(w[k:k + 1, :] for w in (wq, wk, wv) for k in range(4))


def _prep_rows(t):
    return min(t, 256)


def _gdn_pad(srcs, pads):
    for src, pad in zip(srcs, pads):
        pad[0:HALO, :] = jnp.zeros((HALO, HD), F32)
        pad[HALO:, :] = src[...]


def _gdn_stage(pads, gates, taps, h, chunked):
    t = gates.shape[0]
    rows = _prep_rows(t)
    per = rows // CHUNK

    def tile(i, carry):
        r0 = pl.multiple_of(i * rows, rows)
        vals = _gdn_prep(*[p[pl.ds(r0, rows + HALO), :] for p in pads], gates[pl.ds(r0, rows), :], taps, h)
        for v, r in zip(vals, chunked):
            r[pl.ds(i * per, per)] = v.reshape(per, CHUNK, v.shape[-1])
        return carry

    lax.fori_loop(0, t // rows, tile, 0)


def _gdn_intra_all(chunked, intra):
    nc = chunked[0].shape[0]
    grp_n = math.gcd(nc, GROUP)

    def grp(i, carry):
        sl = pl.ds(pl.multiple_of(i * grp_n, grp_n), grp_n)
        for r, val in zip(intra, _gdn_intra(*[c[sl] for c in chunked])):
            r[sl] = val
        return carry

    lax.fori_loop(0, nc // grp_n, grp, 0)


def _gdn_fwd(pa, gates, conv):
    t = pa.shape[0]
    nc = t // CHUNK
    terms = _gdn_term_shapes(nc)

    def body(gq, gk, gv, gt, wq, wk, wv, *rest):
        h = pl.program_id(0)
        intra, chunked, pads = rest[:N_TERMS], rest[N_TERMS:N_TERMS + 5], rest[N_TERMS + 5:]
        _gdn_pad((gq, gk, gv), pads)
        _gdn_stage(pads, gt, _taps(wq, wk, wv), h, chunked)
        _gdn_intra_all(chunked, intra)

    qkv = [(nc, CHUNK, HD)] * 3
    outs = pl.pallas_call(
        body, grid=(NG,), name="gdn_fwd", in_specs=_gdn_in_specs(t),
        out_specs=[_per_head(sh, one_buffer=False) for sh in terms + qkv],
        out_shape=[SDS((NG,) + sh, F32) for sh in terms + qkv],
        scratch_shapes=_gdn_chunked_scratch(nc)[3:] + [pltpu.VMEM((t + HALO, HD), F32)] * 3, compiler_params=_cp("parallel"),
    )(pa, pa, pa, gates, conv, conv, conv)
    return list(outs[:N_TERMS]), list(outs[N_TERMS:])


def _gdn_scan(terms_in):
    nc = terms_in[0].shape[1]
    terms = _gdn_term_shapes(nc)

    def body(*refs):
        intra, o, states = refs[:N_TERMS], refs[N_TERMS], refs[N_TERMS + 1]

        def one(c, ss):
            rows = pl.ds(pl.multiple_of(c * CHUNK, CHUNK), CHUNK)
            loaded = [[r[hh, c] for r in intra] for hh in range(SCAN_HEADS)]
            res = [_gdn_step(ss[hh], *loaded[hh]) for hh in range(SCAN_HEADS)]
            for hh in range(SCAN_HEADS):
                states[hh, c] = ss[hh]
                o[rows, hh * HD:(hh + 1) * HD] = res[hh][0]
            return tuple(r[1] for r in res)

        per_trip = math.gcd(nc, SCAN_UNROLL)

        def step(i, ss):
            for k in range(per_trip):
                ss = one(per_trip * i + k, ss)
            return ss

        lax.fori_loop(0, nc // per_trip, step, tuple(jnp.zeros((HD, HD), F32) for _ in range(SCAN_HEADS)))

    return pl.pallas_call(
        body, grid=(NG // SCAN_HEADS,), name="gdn_scan", in_specs=[_per_head(sh, SCAN_HEADS) for sh in terms],
        out_specs=[pl.BlockSpec((nc * CHUNK, SCAN_HEADS * HD), lambda h: (0, h), pipeline_mode=ONE_BUFFER),
                   _per_head((nc, HD, HD), SCAN_HEADS)],
        out_shape=[SDS((nc * CHUNK, NG * HD), F32), SDS((NG, nc, HD, HD), F32)], compiler_params=_cp("parallel"),
    )(*terms_in)


def _gdn_bwd_scan(saved, do_raw):
    nc = saved[0].shape[1]
    terms = _gdn_term_shapes(nc)

    def body(*refs):
        intra, states, do, outs = refs[:N_TERMS], refs[N_TERMS], refs[N_TERMS + 1], refs[N_TERMS + 2:]

        def one(c, dss):
            rows = pl.ds(pl.multiple_of(c * CHUNK, CHUNK), CHUNK)
            loaded = [[states[hh, c]] + [r[hh, c] for r in intra] for hh in range(SCAN_HEADS)]
            cts = [do[rows, hh * HD:(hh + 1) * HD] for hh in range(SCAN_HEADS)]
            grads = [jax.vjp(_gdn_step, *loaded[hh])[1]((cts[hh], dss[hh])) for hh in range(SCAN_HEADS)]
            for hh in range(SCAN_HEADS):
                for r, gval in zip(outs, grads[hh][1:]):
                    r[hh, c] = gval
            return tuple(g[0] for g in grads)

        per_trip = math.gcd(nc, SCAN_UNROLL)

        def bwd(i, dss):
            c = nc - 1 - per_trip * i
            for k in range(per_trip):
                dss = one(c - k, dss)
            return dss

        lax.fori_loop(0, nc // per_trip, bwd, tuple(jnp.zeros((HD, HD), F32) for _ in range(SCAN_HEADS)))

    return pl.pallas_call(
        body, grid=(NG // SCAN_HEADS,), name="gdn_bwd_scan",
        in_specs=[_per_head(sh, SCAN_HEADS) for sh in terms] + [_per_head((nc, HD, HD), SCAN_HEADS)]
        + [pl.BlockSpec((nc * CHUNK, SCAN_HEADS * HD), lambda h: (0, h), pipeline_mode=ONE_BUFFER)],
        out_specs=[_per_head(sh, SCAN_HEADS) for sh in terms],
        out_shape=[SDS((NG,) + sh, F32) for sh in terms], compiler_params=_cp("parallel"),
    )(*saved, do_raw)


def _gdn_bwd(pa, gates, conv, dterms, qkv):
    t = pa.shape[0]
    nc = t // CHUNK
    terms = _gdn_term_shapes(nc)

    def body(*refs):
        gq, gk, gv, gt, wq, wk, wv = refs[:7]
        dintra, qkv = refs[7:7 + N_TERMS], refs[7 + N_TERMS:10 + N_TERMS]
        dgq, dgk, dgv, dgt, dwq, dwk, dwv = refs[10 + N_TERMS:17 + N_TERMS]
        chunked, pads, dpads, dgt_s = (refs[17 + N_TERMS:22 + N_TERMS], refs[22 + N_TERMS:25 + N_TERMS],
                                       refs[25 + N_TERMS:28 + N_TERMS], refs[28 + N_TERMS])
        h = pl.program_id(0)
        taps = _taps(wq, wk, wv)
        _gdn_pad((gq, gk, gv), pads)
        rows = _prep_rows(t)
        per = rows // CHUNK

        def gates_tile(i, carry):
            gtile = gt[pl.ds(pl.multiple_of(i * rows, rows), rows), :]
            chunked[3][pl.ds(i * per, per)] = _lane_pick(gtile, L_GA + h).reshape(per, CHUNK, 1)
            chunked[4][pl.ds(i * per, per)] = _lane_pick(gtile, L_GB + h).reshape(per, CHUNK, 1)
            return carry

        lax.fori_loop(0, t // rows, gates_tile, 0)
        grp_n = math.gcd(nc, GROUP)

        def grp(i, carry):
            sl = pl.ds(pl.multiple_of(i * grp_n, grp_n), grp_n)
            _, vjp = jax.vjp(_gdn_intra, *[r[sl] for r in qkv], chunked[3][sl], chunked[4][sl])
            for r, gval in zip(chunked, vjp(tuple(r[sl] for r in dintra))):
                r[sl] = gval
            return carry

        lax.fori_loop(0, nc // grp_n, grp, 0)

        for r in dpads:
            r[...] = jnp.zeros_like(r)

        def tile(i, dtaps):
            r0 = pl.multiple_of(i * rows, rows)
            win = pl.ds(r0, rows + HALO)
            _, vjp = jax.vjp(lambda *a: _gdn_prep(*a, h), *[p[win, :] for p in pads], gt[pl.ds(r0, rows), :], taps)
            grads = vjp(tuple(r[pl.ds(i * per, per)].reshape(rows, r.shape[-1]) for r in chunked))
            for r, gval in zip(dpads, grads[:3]):
                r[win, :] += gval
            dgt_s[pl.ds(r0, rows), :] = grads[3]
            return jax.tree.map(jnp.add, dtaps, grads[4])

        dtaps = lax.fori_loop(0, t // rows, tile, (jnp.zeros((1, HD), F32),) * 12)
        for r, dpad in zip((dgq, dgk, dgv), dpads):
            r[...] = dpad[HALO:, :].astype(r.dtype)
        for j, r in enumerate((dwq, dwk, dwv)):
            for k in range(4):
                r[k:k + 1, :] = dtaps[4 * j + k]

        @pl.when(h == 0)
        def _():
            dgt[...] = jnp.zeros_like(dgt)

        dgt[...] += dgt_s[...]

    head = _head(t)
    taps = pl.BlockSpec((4, HD), lambda h: (0, h))
    return pl.pallas_call(
        body, grid=(NG,), name="gdn_bwd",
        in_specs=_gdn_in_specs(t) + [_per_head(sh) for sh in terms + [(nc, CHUNK, HD)] * 3],
        out_specs=[head, head, head, _small(t), taps, taps, taps],
        out_shape=[SDS((t, NG * HD), BF16)] * 3 + [SDS((t, HD), F32)] + [SDS((4, NG * HD), F32)] * 3,
        scratch_shapes=_gdn_chunked_scratch(nc) + [pltpu.VMEM((t + HALO, HD), F32)] * 6 + [pltpu.VMEM((t, HD), F32)],
        compiler_params=_cp("arbitrary"),
    )(pa, pa, pa, gates, conv, conv, conv, *dterms, *qkv)


def _gdn_post(o, z, gain):
    return (jnp.concatenate(
        [_rms(o[:, h * HD:(h + 1) * HD], gain) * _silu(z[:, h * HD:(h + 1) * HD]) for h in range(NG)], axis=1),)


def _place():
    return lax.axis_index("x"), lax.axis_index("y"), lax.axis_index("c")


def _sum_blocks(name, parts):
    _, r, c = parts.shape
    tr = 64 if r % 64 == 0 else r

    def body(x, o):
        acc = x[0].astype(F32)
        for d in range(1, N_DEV):
            acc = acc + x[d].astype(F32)
        o[...] = acc

    return pl.pallas_call(
        body, grid=(r // tr,), name=name, in_specs=[pl.BlockSpec((N_DEV, tr, c), lambda i: (0, i, 0))],
        out_specs=pl.BlockSpec((tr, c), lambda i: (i, 0)), out_shape=SDS((r, c), F32), compiler_params=_cp("parallel"),
    )(parts)


def _all_reduce_small(name, x, reduce):
    m_per, n = x.shape

    def body(x_ref, out_ref, send_sems, recv_sems, local_sem):
        px, py, pc = _place()
        me, sibling = (px, py, pc), (px, py, 1 - pc)
        chips = [(1 - px, py), (px, 1 - py), (1 - px, 1 - py)]
        buf = out_ref

        def rows(qx, qy, qc):
            return buf.at[pl.ds((4 * qx + 2 * qy + qc) * m_per, m_per), :]

        def copy(k, block, to, src=None):
            return pltpu.make_async_remote_copy(
                src_ref=rows(*block) if src is None else src, dst_ref=rows(*block),
                send_sem=send_sems.at[k], recv_sem=recv_sems.at[k], device_id=to, device_id_type=MESH)

        mine = pltpu.make_async_copy(x_ref, rows(*me), local_sem)
        mine.start()
        first = [copy(0, me, sibling, src=x_ref)]
        first += [copy(1 + j, me, (*chip, pc), src=x_ref) for j, chip in enumerate(chips)]
        for cp in first:
            cp.start()
        passed = [copy(4 + j, (*chip, pc), sibling) for j, chip in enumerate(chips)]
        for j, chip in enumerate(chips):
            copy(1 + j, (*chip, pc), me).wait_recv()
            passed[j].start()
        copy(0, sibling, me).wait_recv()
        for j, chip in enumerate(chips):
            copy(4 + j, (*chip, 1 - pc), me).wait_recv()
        for cp in first + passed:
            cp.wait_send()
        mine.wait()

    gathered = pl.pallas_call(
        body, name=name, out_shape=SDS((N_DEV * m_per, n), x.dtype),
        in_specs=[pl.BlockSpec(memory_space=pltpu.VMEM)], out_specs=pl.BlockSpec(memory_space=pltpu.VMEM),
        scratch_shapes=[pltpu.SemaphoreType.DMA((7,)), pltpu.SemaphoreType.DMA((7,)), pltpu.SemaphoreType.DMA],
    )(x)
    if not reduce:
        return gathered
    return _sum_blocks(name + "_sum", gathered.reshape(N_DEV, m_per, n))


HBM_SPEC = pl.BlockSpec(memory_space=pltpu.HBM)
SEM_SPEC = pl.BlockSpec(memory_space=pltpu.SEMAPHORE)
EFFECT = pltpu.SideEffectType.DATAFLOW_SIDE_EFFECTING


def _copies_start(name, bufs, n_remote, n_local, build, deps):
    nb, nd = len(bufs), len(deps)
    sem_shapes = [pltpu.SemaphoreType.DMA((n_remote,)), pltpu.SemaphoreType.DMA((n_remote,))]
    if n_local:
        sem_shapes.append(pltpu.SemaphoreType.DMA((n_local,)))
    ns = len(sem_shapes)

    def body(*refs):
        sems = refs[nb + nd:nb + nd + ns]
        remote, local = build(refs[:nb], *sems, *([None] * (3 - ns)))
        for cp in local + remote:
            cp.start()
        refs[-1][...] = jnp.zeros((8, HD), F32)

    outs = pl.pallas_call(
        body, name=name,
        out_shape=(*sem_shapes, *[pltpu.HBM(b.shape, b.dtype) for b in bufs], SDS((8, HD), F32)),
        in_specs=[HBM_SPEC] * nb + [ANY_SPEC] * nd,
        out_specs=(*[SEM_SPEC] * ns, *[HBM_SPEC] * nb, pl.BlockSpec(memory_space=pltpu.VMEM)),
        input_output_aliases={i: ns + i for i in range(nb)},
        compiler_params=pltpu.CompilerParams(has_side_effects=EFFECT),
    )(*[pltpu.with_memory_space_constraint(b, pltpu.HBM) for b in bufs], *deps)
    return list(outs[:ns]), list(outs[ns:ns + nb]), outs[-1]


def _copies_wait(name, bufs, sems, build, after):
    nb, ns = len(bufs), len(sems)

    def body(*refs):
        remote, local = build(refs[:nb], *refs[nb:nb + ns], *([None] * (3 - ns)))
        for cp in local:
            cp.wait()
        for cp in remote:
            cp.wait_send()
            cp.wait_recv()

    outs = pl.pallas_call(
        body, name=name, out_shape=tuple(pltpu.HBM(b.shape, b.dtype) for b in bufs),
        in_specs=[HBM_SPEC] * nb + [SEM_SPEC] * ns + [ANY_SPEC] * len(after), out_specs=tuple([HBM_SPEC] * nb),
        input_output_aliases={i: i for i in range(nb)},
        compiler_params=pltpu.CompilerParams(has_side_effects=EFFECT),
    )(*bufs, *sems, *after)
    return list(outs)


def _remote(src, dst, send, recv, k, to):
    return pltpu.make_async_remote_copy(src_ref=src, dst_ref=dst, send_sem=send.at[k], recv_sem=recv.at[k],
                                        device_id=to, device_id_type=MESH)


class _Gather:
    def __init__(self, name, shards, deps):
        self.name, self.n = name, len(shards)
        lands = [lax.empty((N_DEV,) + s.shape, s.dtype) for s in shards]
        self.sems1, bufs, self.token = _copies_start(
            name + "_s1", list(shards) + lands, 4 * self.n, self.n, self._stage1(range(self.n)), deps)
        self.shards, self.lands, self.sems2 = bufs[:self.n], bufs[self.n:], {}

    def _stage1(self, idxs):
        def build(refs, send, recv, loc):
            x, y, c = _place()
            me = 4 * x + 2 * y + c
            targets = [(x, y, 1 - c), (1 - x, y, c), (x, 1 - y, c), (1 - x, 1 - y, c)]
            remote, local = [], []
            for pos, i in enumerate(idxs):
                src, land = refs[pos], refs[len(idxs) + pos]
                local.append(pltpu.make_async_copy(src, land.at[me], loc.at[i]))
                remote += [_remote(src, land.at[me], send, recv, 4 * i + k, to) for k, to in enumerate(targets)]
            return remote, local
        return build

    @staticmethod
    def _stage2(refs, send, recv, loc):
        x, y, c = _place()
        remote = []
        for pos, land in enumerate(refs):
            for j, (cx, cy) in enumerate([(1 - x, y), (x, 1 - y), (1 - x, 1 - y)]):
                blk = land.at[4 * cx + 2 * cy + c]
                remote.append(_remote(blk, blk, send, recv, 3 * pos + j, (x, y, 1 - c)))
        return remote, []

    def pass_on(self, idxs, after):
        tag, m = "".join(map(str, idxs)), len(idxs)
        bufs = _copies_wait(f"{self.name}_w1_{tag}", [self.shards[i] for i in idxs] + [self.lands[i] for i in idxs],
                            self.sems1, self._stage1(idxs), after)
        self.sems2[tag], lands, token = _copies_start(f"{self.name}_s2_{tag}", bufs[m:], 3 * m, 0, self._stage2, ())
        for pos, i in enumerate(idxs):
            self.lands[i] = lands[pos]
        return [token]

    def get(self, idxs, after):
        tag = "".join(map(str, idxs))
        return _copies_wait(f"{self.name}_w2_{tag}", [self.lands[i] for i in idxs], self.sems2[tag], self._stage2, after)


class _RelayGather:
    def __init__(self, name, shards, deps):
        self.name, self.n = name, len(shards)
        lands = [lax.empty((N_DEV,) + s.shape, s.dtype) for s in shards]
        self.sems, bufs, self.token = _copies_start(name + "_s1", list(shards) + lands, 3 * self.n, self.n, self._stage1, deps)
        self.shards, self.lands = bufs[:self.n], bufs[self.n:]

    def _stage1(self, refs, send, recv, loc):
        x, y, c = _place()
        me = 4 * x + 2 * y + c
        remote, local = [], []
        for i in range(self.n):
            src, land = refs[i], refs[self.n + i]
            local.append(pltpu.make_async_copy(src, land.at[me], loc.at[i]))
            remote += [_remote(src, land.at[me], send, recv, 3 * i + k, to)
                       for k, to in enumerate([(x, y, 1 - c), (1 - x, y, c), (x, 1 - y, c)])]
        return remote, local

    @staticmethod
    def _relay(refs, send, recv, loc):
        x, y, c = _place()
        remote = []
        for i, land in enumerate(refs):
            half = land.shape[1] // 2
            from_x = land.at[4 * (1 - x) + 2 * y + c].at[pl.ds(0, half)]
            from_y = land.at[4 * x + 2 * (1 - y) + c].at[pl.ds(half, half)]
            remote += [_remote(from_x, from_x, send, recv, 2 * i, (x, 1 - y, c)),
                       _remote(from_y, from_y, send, recv, 2 * i + 1, (1 - x, y, c))]
        return remote, []

    def forward(self, after):
        bufs = _copies_wait(self.name + "_w1", self.shards + self.lands, self.sems, self._stage1, after)
        self.sems, self.lands, self.token = _copies_start(self.name + "_sf", bufs[self.n:], 2 * self.n, 0, self._relay, ())
        return [self.token]

    def pass_on(self, after):
        lands = _copies_wait(self.name + "_wf", self.lands, self.sems, self._relay, after)
        self.sems, self.lands, self.token = _copies_start(self.name + "_s2", lands, 3 * self.n, 0, _Gather._stage2, ())
        return [self.token]

    def get(self, after):
        return _copies_wait(self.name + "_w2", self.lands, self.sems, _Gather._stage2, after)


def _rows_tile(r, row_bytes, target=1 << 20):
    tr = r
    while tr % 32 == 0 and tr * row_bytes > target:
        tr //= 2
    return tr


def _pair_add(name, g, got, c):
    _, r, cols = g.shape
    tr = _rows_tile(r, cols * 2)

    def body(s, a, b, o):
        o[...] = (a[...].astype(F32) + b[...].astype(F32)).astype(o.dtype)

    return pl.pallas_call(
        body, name=name, out_shape=SDS((4, r, cols), g.dtype),
        grid_spec=pltpu.PrefetchScalarGridSpec(
            num_scalar_prefetch=1, grid=(4, r // tr),
            in_specs=[pl.BlockSpec((None, tr, cols), lambda j, i, s: (2 * j + s[0], i, 0)),
                      pl.BlockSpec((None, tr, cols), lambda j, i, s: (j, i, 0))],
            out_specs=pl.BlockSpec((None, tr, cols), lambda j, i, s: (j, i, 0))),
        compiler_params=_cp("parallel", "parallel"),
    )(c.reshape(1), g, got)


def _quad_sum(name, part, got, chip, wmv=None):
    _, r, cols = part.shape
    tr = _rows_tile(r, cols * 4)
    n_out = 4 if wmv else 1

    def body(s, a, b1, b2, b3, *rest):
        g = ((a[...].astype(F32) + b1[...].astype(F32)) + b2[...].astype(F32)) + b3[...].astype(F32)
        rest[-n_out][...] = g
        if wmv:
            w, m, v = rest[:3]
            rest[-3][...], rest[-2][...], rest[-1][...] = _adamw(w[...], g, m[...], v[...])

    blk = lambda k: pl.BlockSpec((None, tr, cols), lambda i, s, k=k: (jnp.bitwise_xor(s[0], k), i, 0))
    row = pl.BlockSpec((tr, cols), lambda i, s: (i, 0))
    outs = pl.pallas_call(
        body, name=name, out_shape=[SDS((r, cols), F32)] * n_out,
        grid_spec=pltpu.PrefetchScalarGridSpec(
            num_scalar_prefetch=1, grid=(r // tr,), in_specs=[blk(0), blk(1), blk(2), blk(3)] + [row] * (n_out - 1),
            out_specs=[row] * n_out),
        compiler_params=_cp("parallel"),
    )(chip.reshape(1), part, got, got, got, *(wmv or ()))
    return tuple(outs) if wmv else outs[0]


class _Scatter:
    def __init__(self, name, grads, deps):
        self.name, self.n = name, len(grads)
        got = [lax.empty((4,) + g.shape[1:], g.dtype) for g in grads]
        self.sems, bufs, self.token = _copies_start(name + "_s1", list(grads) + got, 4 * self.n, 0, self._stage1, deps)
        self.grads, self.got = bufs[:self.n], bufs[self.n:]

    def _stage1(self, refs, send, recv, loc):
        x, y, c = _place()
        remote = []
        for i in range(self.n):
            remote += [_remote(refs[i].at[2 * j + 1 - c], refs[self.n + i].at[j], send, recv, 4 * i + j, (x, y, 1 - c))
                       for j in range(4)]
        return remote, []

    def _stage2(self, refs, send, recv, loc):
        x, y, c = _place()
        remote = []
        for i in range(self.n):
            for k in (1, 2, 3):
                tx = 1 - x if k & 2 else x
                ty = 1 - y if k & 1 else y
                remote.append(_remote(refs[i].at[2 * tx + ty], refs[self.n + i].at[2 * x + y], send, recv,
                                      3 * i + k - 1, (tx, ty, c)))
        return remote, []

    def mid(self, after):
        bufs = _copies_wait(self.name + "_w1", self.grads + self.got, self.sems, self._stage1, after)
        c = lax.axis_index("c").astype(jnp.int32)
        parts = [_pair_add(f"{self.name}_add{i}", bufs[i], bufs[self.n + i], c) for i in range(self.n)]
        got = [lax.empty(p.shape, p.dtype) for p in parts]
        self.sems, bufs, self.token = _copies_start(self.name + "_s2", parts + got, 3 * self.n, 0, self._stage2, ())
        self.parts, self.got = bufs[:self.n], bufs[self.n:]

    def end(self, after, wmv=None):
        bufs = _copies_wait(self.name + "_w2", self.parts + self.got, self.sems, self._stage2, after)
        chip = (2 * lax.axis_index("x") + lax.axis_index("y")).astype(jnp.int32)
        wmv = wmv or [None] * self.n
        return [_quad_sum(f"{self.name}_sum{i}", bufs[i], bufs[self.n + i], chip, wmv[i]) for i in range(self.n)]


def _adamw(w, g, m, v):
    m = ADAM_B1 * m + (1.0 - ADAM_B1) * g
    v = ADAM_B2 * v + (1.0 - ADAM_B2) * (g * g)
    m_hat = m / (1.0 - ADAM_B1 ** ADAM_STEP)
    v_hat = v / (1.0 - ADAM_B2 ** ADAM_STEP)
    return -ADAM_LR * (m_hat / (jnp.sqrt(v_hat) + ADAM_EPS) + ADAM_WD * w), m, v


def _adamw_call(name, w, g, m, v):
    r, c = w.shape
    tm = 64 if r % 64 == 0 else r
    return _rowwise(name, _adamw, [w, g, m, v], [], [(c, F32)] * 3, tm)


_IN_COLS = 5906


def _perm_in(w):
    pad = jnp.zeros((w.shape[0], 2 * HALF - _IN_COLS), w.dtype)
    return (jnp.concatenate([w[:, 2310:4614], w[:, 4614:5382]], axis=1),
            jnp.concatenate([w[:, :2304], w[:, 5394:5906], w[:, 2304:2310], w[:, 5382:5394], pad], axis=1))


def _unperm_in(ga, gb):
    return jnp.concatenate([gb[:, :2304], gb[:, 2816:2822], ga[:, :2304], ga[:, 2304:3072], gb[:, 2822:2834],
                            gb[:, 2304:2816]], axis=1)


def _lanes(v, at):
    return jnp.pad(v, ((0, 0), (at, HD - at - v.shape[1])))


_PACK = ("norm_mix", "mem_norm", "norm_ffn", "gdn_conv", "fox_q_norm", "fox_k_norm", "gdn_out_norm", "mem_q_norm",
         "mem_k_norm", "fox_f_bias", "gdn_a_log", "gdn_dt_bias", "loss")


def _pack(vals):
    parts = [vals[n].reshape(-1, HD) for n in _PACK]
    used = sum(p.shape[0] for p in parts)
    buf = jnp.concatenate(parts + [jnp.zeros((-used % 8, HD), F32)], axis=0)
    return buf, [(n, p.shape[0]) for n, p in zip(_PACK, parts)]


def _unpack(buf, layout):
    out, at = {}, 0
    for n, rows in layout:
        out[n] = buf[at:at + rows]
        at += rows
    return out


def kernel(x, mem, norm_mix, w_in, fox_f_bias, fox_q_norm, fox_k_norm, gdn_conv, gdn_a_log, gdn_dt_bias, gdn_out_norm, mem_norm, w_mem_kv, mem_q_norm, mem_k_norm, w_out, norm_ffn, w_gate_up, w_down, loss_target, m_norm_mix, m_w_in, m_fox_f_bias, m_fox_q_norm, m_fox_k_norm, m_gdn_conv, m_gdn_a_log, m_gdn_dt_bias, m_gdn_out_norm, m_mem_norm, m_w_mem_kv, m_mem_q_norm, m_mem_k_norm, m_w_out, m_norm_ffn, m_w_gate_up, m_w_down, v_norm_mix, v_w_in, v_fox_f_bias, v_fox_q_norm, v_fox_k_norm, v_gdn_conv, v_gdn_a_log, v_gdn_dt_bias, v_gdn_out_norm, v_mem_norm, v_w_mem_kv, v_mem_q_norm, v_mem_k_norm, v_w_out, v_norm_ffn, v_w_gate_up, v_w_down):
    args = dict(locals())
    d = x.shape[2]
    me = 4 * lax.axis_index("x") + 2 * lax.axis_index("y") + lax.axis_index("c")

    cshard = gdn_conv[0].shape[1]
    conv_pad = jnp.pad(gdn_conv[0], ((0, 4), (0, 3 * HD - cshard)))
    conv_all = _all_reduce_small("ag_conv", conv_pad, False).reshape(N_DEV, 8, 3 * HD)[:, :4, :cshard]
    conv_all = conv_all.transpose(1, 0, 2).reshape(4, N_DEV * cshard)
    w_in_a, w_in_b = _perm_in(w_in[0])
    comm = _StepComm({"in_b": [w_in_b], "in_a": [w_in_a], "kv_out": [w_mem_kv[0], w_out[0]]}, w_gate_up[0], w_down[0],
                     [conv_all])

    grad_x, loss_local, small_grads = _local_step(
        x[0], mem[0], loss_target[0], norm_mix, fox_f_bias, fox_q_norm, fox_k_norm, gdn_a_log, gdn_dt_bias,
        gdn_out_norm, mem_norm, mem_q_norm, mem_k_norm, norm_ffn, conv_all, comm)

    wmv = lambda n: (args[n][0], args["m_" + n][0], args["v_" + n][0])
    red = comm.finish([grad_x], {"ffn": [wmv("w_down"), wmv("w_gate_up")], "a": [None, wmv("w_out"), wmv("w_mem_kv")],
                                 "b": [None]})
    updated = {"w_down": red["ffn"][0], "w_gate_up": red["ffn"][1], "w_out": red["a"][1], "w_mem_kv": red["a"][2]}
    grads = {n: r[0] for n, r in updated.items()}
    grads["w_in"] = _unperm_in(red["a"][0], red["b"][0])
    small_grads["loss"] = jnp.broadcast_to(loss_local, (1, HD))
    packed, layout = _pack(small_grads)
    small = _unpack(_all_reduce_small("ar_small", packed, True), layout)
    loss = small["loss"][0, 0]
    six = {"fox_f_bias": L_FF, "gdn_a_log": L_GA, "gdn_dt_bias": L_GA}
    for n, rows_n in layout[:-1]:
        gsm = small[n]
        if n == "gdn_conv":
            gsm = lax.dynamic_slice(gsm.reshape(4, N_DEV * cshard), (0, me * cshard), (4, cshard))[None]
        elif n in six:
            gsm = gsm[:, six[n]:six[n] + 6]
        else:
            gsm = gsm.reshape(1, rows_n * HD)
        grads[n] = gsm

    names = ['norm_mix', 'w_in', 'fox_f_bias', 'fox_q_norm', 'fox_k_norm', 'gdn_conv', 'gdn_a_log', 'gdn_dt_bias',
             'gdn_out_norm', 'mem_norm', 'w_mem_kv', 'mem_q_norm', 'mem_k_norm', 'w_out', 'norm_ffn', 'w_gate_up', 'w_down']
    big = ("w_in", "w_mem_kv", "w_out", "w_gate_up", "w_down")
    delta, new_m, new_v = {}, {}, {}
    for n in big:
        res = updated[n][1:] if n in updated else _adamw_call("adamw_" + n, args[n][0], grads[n], *wmv(n)[1:])
        delta[n], new_m[n], new_v[n] = [a[None] for a in res]
        grads[n] = grads[n][None]

    def flat(a):
        a = a.reshape(1, -1)
        return jnp.pad(a, ((0, 0), (0, -a.shape[1] % HD))).reshape(-1, HD)

    smalls = [n for n in names if n not in big]
    pk = lambda pre: jnp.concatenate([flat(grads[n] if pre == "g" else args[pre + n]) for n in smalls], axis=0)
    cat = [pk(""), pk("g"), pk("m_"), pk("v_")]
    padr = -cat[0].shape[0] % 8
    cat = [jnp.pad(a, ((0, padr), (0, 0))) for a in cat]
    res = _adamw_call("adamw_small", *cat)
    at = 0
    for n in smalls:
        shape = args[n].shape
        size = math.prod(shape)
        nrow = -(-size // HD)
        for dst, src in zip((delta, new_m, new_v), res):
            dst[n] = src[at:at + nrow].reshape(-1)[:size].reshape(shape)
        at += nrow

    return (loss, grad_x[None], *[grads[n] for n in names], *[delta[n] for n in names],
            *[new_m[n] for n in names], *[new_v[n] for n in names])


class _StepComm:
    def __init__(self, shard_groups, w_gate_up, w_down, after):
        self.groups, shards = {}, []
        for key, ws in shard_groups.items():
            self.groups[key] = list(range(len(shards), len(shards) + len(ws)))
            shards += [w.astype(BF16) for w in ws]
        self.gather = _Gather("ag", shards, after)
        self.relay = _RelayGather("ag_gu", [w_gate_up.astype(BF16)], [self.gather.token])
        self.w_down = w_down.astype(BF16)
        self.passed, self.scatters = set(), {}

    def start_deps(self):
        return [self.relay.token]

    def relay_forward(self, after):
        deps = self.relay.forward(after)
        self.gather_down = _Gather("ag_dn", [self.w_down], deps)
        return [self.gather_down.token]

    def pass_on(self, key, after):
        self.passed.add(key)
        if key == "gate_up":
            return self.relay.pass_on(after)
        return self.gather.pass_on(self.groups[key], after)

    def weights(self, key, after):
        if key == "down":
            return self.gather_down.get([0], self.gather_down.pass_on([0], after))
        if key not in self.passed:
            after = self.pass_on(key, after)
        return self.relay.get(after) if key == "gate_up" else self.gather.get(self.groups[key], after)

    def send(self, tag, grads):
        blocks = [g if g.ndim == 3 else g.reshape(N_DEV, g.shape[0] // N_DEV, g.shape[1]) for g in grads]
        self.scatters[tag] = _Scatter("rs_" + tag, blocks, ())
        return [self.scatters[tag].token]

    def mid(self, tag, after):
        self.scatters[tag].mid(after)
        return [self.scatters[tag].token]

    def finish(self, after, wmv):
        return {tag: sc.end(after, wmv[tag]) for tag, sc in self.scatters.items()}


def _local_step(xs, ms, tgt, norm_mix, fox_f_bias, fox_q_norm, fox_k_norm, gdn_a_log, gdn_dt_bias, gdn_out_norm,
                mem_norm, mem_q_norm, mem_k_norm, norm_ffn, conv_all, comm):
    t, d = xs.shape
    bq = min(t, 256)
    fb, alog, dtb = _lanes(fox_f_bias, L_FF), _lanes(gdn_a_log, L_GA), _lanes(gdn_dt_bias, L_GA)
    flat = lambda w: w.reshape(-1, w.shape[-1])

    rms1 = lambda a, g: (_rms(a, g),)
    (u,) = _rowwise("norm_mix", rms1, [xs], [norm_mix], [(d, BF16)], min(t, 256), deps=comm.start_deps())
    w_in_b = flat(comm.weights("in_b", [u])[0])
    pb = _matmul("proj_in_b", u, w_in_b, NN, F32, 1024, 768)
    o_fox = _fox_fwd(pb, fb, fox_q_norm, fox_k_norm, bq)
    w_in_a = flat(comm.weights("in_a", [o_fox])[0])
    pa = _matmul("proj_in_a", u, w_in_a, NN, F32, 1024, 768)
    smrow = (pb, HD, SM)
    (gates,) = _rowwise("gdn_gates", _gdn_gates, [smrow], [alog, dtb], [(HD, F32)], min(t, 256))
    gdn_terms, gdn_qkv = _gdn_fwd(pa, gates, conv_all)
    o_gdn_raw, gdn_states = _gdn_scan(gdn_terms)
    gdn_saved = list(gdn_terms) + [gdn_states]
    deps = comm.relay_forward([o_gdn_raw])
    zrow = (pa, NG * HD, GZ * HD // (NG * HD))
    (o_gdn,) = _rowwise("gdn_post", _gdn_post, [o_gdn_raw, zrow], [gdn_out_norm], [(NG * HD, BF16)], min(t, 256),
                        deps=deps)
    w_kv_all, w_out_all = [flat(w) for w in comm.weights("kv_out", [o_gdn])]
    (mem_n,) = _rowwise("norm_mem", rms1, [ms], [mem_norm], [(d, BF16)], ms.shape[0])
    mkv = _matmul("proj_mem", mem_n, w_kv_all, NN, F32, 256, 512)
    o_mem = _mem_fwd(pb, mkv, mem_q_norm, mem_k_norm)
    deps = comm.pass_on("gate_up", [o_mem])
    mix = jnp.concatenate([o_fox, o_gdn, o_mem], axis=1)
    h1, h1n = _proj_out_norm(mix, w_out_all, xs, norm_ffn, deps)
    (wgu,) = comm.weights("gate_up", [h1n])
    ffw = wgu.shape[2]
    gu, act = _ffn_up(h1n, wgu.reshape(2, 4, d, ffw))
    w_down_all = flat(comm.weights("down", [act])[0])
    dyb, lsum = _ffn_down_loss(act, w_down_all, h1, tgt)
    loss_local = (0.5 / d) * jnp.sum(lsum[::8, ::HD])

    dgu = _ffn_down_bwd(dyb, w_down_all.reshape(4, ffw, d), gu).reshape(8, t, ffw)
    g_w_down = _matmul("grad_w_down", act, dyb, TN, BF16, 512, 2048)
    g_w_gu = _ffn_up_bwd_w(h1n, dgu)
    deps = comm.send("ffn", [g_w_down, g_w_gu])
    rms2 = lambda a, g: (_rms(a, g), a)
    dh1b, g_norm_ffn = _ffn_up_bwd_x(dgu, wgu, h1, norm_ffn, dyb, deps)

    dmix = _matmul("proj_out_bwd_x", dh1b, w_out_all, NT, BF16, 1024, 1024)
    g_w_out = _matmul("grad_w_out", mix, dh1b, TN, BF16, 1024, 2048)
    deps = comm.mid("ffn", [dmix, g_w_out])
    dmq, dmk, dmv, g_mqn, g_mkn = _mem_bwd(pb, mkv, mem_q_norm, mem_k_norm, dmix, deps=deps)
    dmkv = jnp.concatenate([dmk, dmv], axis=1).astype(BF16)
    g_w_kv = _matmul("grad_w_kv", mem_n, dmkv, TN, BF16, 512, 512)
    do_raw, dgz, g_gon = _rowwise_vjp("gdn_post_bwd", _gdn_post, [o_gdn_raw, zrow], [gdn_out_norm],
                                      [(dmix, NG * HD, 1)], [F32, BF16], min(t, 256), deps=deps)
    dterms = _gdn_bwd_scan(gdn_saved, do_raw)
    dgq, dgk, dgv, dgates, dwq, dwk, dwv = _gdn_bwd(pa, gates, conv_all, dterms, gdn_qkv)
    dsm_gdn, g_alog, g_dtb = _rowwise_vjp("gdn_gates_bwd", _gdn_gates, [smrow], [alog, dtb], [dgates], [F32], min(t, 256))
    dp_a = jnp.concatenate([dgq, dgk, dgv, dgz], axis=1)
    g_w_in_a = _matmul("grad_w_in_a", u, dp_a, TN, BF16, 512, 3072)
    deps = comm.send("a", [g_w_in_a, g_w_out, g_w_kv])
    du_a = _matmul("proj_in_bwd_a", dp_a, w_in_a, NT, F32, 1024, 1024, deps=deps)
    deps = comm.mid("a", [du_a])
    dfq, dfk, dfv, dsm_fox, g_fb, g_fqn, g_fkn = _fox_bwd(pb, fb, fox_q_norm, fox_k_norm, dmix, 2 * bq if t % (2 * bq) == 0 else bq,
                                                          deps=deps)
    dp_b = jnp.concatenate([dfq, dfk, dfv, dmq, (dsm_fox + dsm_gdn).astype(BF16), jnp.zeros((t, HD), BF16)], axis=1)
    g_w_in_b = _matmul("grad_w_in_b", u, dp_b, TN, BF16, 512, 3072)
    deps = comm.send("b", [g_w_in_b])
    dmem_n = _matmul("proj_mem_bwd_x", dmkv, w_kv_all, NT, F32, 256, 512, deps=deps)
    g_mem_norm = _rowwise_vjp("norm_mem_bwd", rms1, [ms], [mem_norm], [dmem_n], [], ms.shape[0])[0]
    deps = comm.mid("b", [g_mem_norm])
    grad_x, g_norm_mix = _proj_in_bwd_norm(dp_b, w_in_b, du_a, xs, norm_mix, dh1b, deps)

    small_grads = {
        "norm_mix": g_norm_mix, "mem_norm": g_mem_norm, "norm_ffn": g_norm_ffn,
        "gdn_conv": jnp.concatenate([dwq, dwk, dwv], axis=1),
        "fox_q_norm": g_fqn, "fox_k_norm": g_fkn, "gdn_out_norm": g_gon, "mem_q_norm": g_mqn, "mem_k_norm": g_mkn,
        "fox_f_bias": g_fb, "gdn_a_log": g_alog, "gdn_dt_bias": g_dtb}
    return grad_x, loss_local, small_grads
```

```python
import functools
import math

import jax
import jax.numpy as jnp
from jax import lax
from jax.experimental import pallas as pl
from jax.experimental.pallas import tpu as pltpu

F32 = jnp.float32
BF16 = jnp.bfloat16
SDS = jax.ShapeDtypeStruct

N_DEV = 8
HD = 128
NF, NG, NM = 6, 6, 4
CHUNK = 64
GROUP = 16
NORM_EPS = 1e-6
GQ, GK, GV, GZ = 0, 6, 12, 18
FQ, FK, FV, MQ, SM = 0, 6, 12, 18, 22
HALF = 24 * HD
L_FF, L_GA, L_GB = 0, 6, 12
VMEM_LIMIT = 56 * 1024 * 1024

ADAM_LR, ADAM_B1, ADAM_B2, ADAM_EPS, ADAM_WD, ADAM_STEP = 0.001, 0.9, 0.999, 1e-08, 0.01, 10

NN = (((1,), (0,)), ((), ()))
NT = (((1,), (1,)), ((), ()))
TN = (((0,), (0,)), ((), ()))
MESH = pl.DeviceIdType.MESH


def _cp(*sem):
    return pltpu.CompilerParams(dimension_semantics=tuple(sem) if sem else None, vmem_limit_bytes=VMEM_LIMIT)


def _dot(a, b, dims=NN):
    return lax.dot_general(a, b, dims, preferred_element_type=F32)


def _iota(shape, axis):
    return lax.broadcasted_iota(jnp.int32, shape, axis)


def _rms(x, gain):
    return x * lax.rsqrt(jnp.mean(x * x, axis=-1, keepdims=True) + NORM_EPS) * gain


def _sigmoid(x):
    return 0.5 * jnp.tanh(0.5 * x) + 0.5


def _silu(x):
    return x * _sigmoid(x)


def _softplus(x):
    return jnp.maximum(x, 0.0) + jnp.log(1.0 + jnp.exp(-jnp.abs(x)))


def _lane_pick(x, lane):
    oh = (_iota((1, x.shape[-1]), 1) == lane).astype(F32)
    return jnp.sum(x * oh, axis=-1, keepdims=True)


def _cumsum_rows(x):
    tril = (_iota((HD, HD), 0) >= _iota((HD, HD), 1)).astype(F32)
    carry = jnp.zeros((1, x.shape[1]), F32)
    outs = []
    for b in range(x.shape[0] // HD):
        blk = x[b * HD:(b + 1) * HD]
        outs.append(_pdot(tril, blk, "nn", "xa") + carry)
        carry = carry + jnp.sum(blk, axis=0, keepdims=True)
    return jnp.concatenate(outs, axis=0)


def _row_spec(r, tm):
    if isinstance(r, tuple):
        arr, width, cb = r
        return arr, pl.BlockSpec((tm, width), lambda i, cb=cb: (i, cb))
    return r, pl.BlockSpec((tm, r.shape[1]), lambda i: (i, 0))


ANY_SPEC = pl.BlockSpec(memory_space=pl.ANY)


def _rowwise(name, fn, rows, consts, outs, tm, deps=()):
    arrs, specs = zip(*[_row_spec(r, tm) for r in rows])
    n_rows = arrs[0].shape[0]
    nr, nc, nd = len(rows), len(consts), len(deps)

    def body(*refs):
        res = fn(*[r[...] for r in refs[:nr + nc]])
        for o, v in zip(refs[nr + nc + nd:], res):
            o[...] = v.astype(o.dtype)

    return pl.pallas_call(
        body, grid=(n_rows // tm,), name=name,
        in_specs=list(specs) + [pl.BlockSpec(c.shape, lambda i: (0, 0)) for c in consts] + [ANY_SPEC] * nd,
        out_specs=[pl.BlockSpec((tm, w), lambda i: (i, 0)) for w, _ in outs],
        out_shape=[SDS((n_rows, w), dt) for w, dt in outs],
        compiler_params=_cp("parallel"),
    )(*arrs, *consts, *deps)


def _rowwise_vjp(name, fn, rows, consts, cts, grad_dtypes, tm, deps=()):
    arrs, specs = zip(*[_row_spec(r, tm) for r in rows])
    ct_arrs, ct_specs = zip(*[_row_spec(r, tm) for r in cts])
    n_rows = arrs[0].shape[0]
    nr, nc, nct, nd = len(rows), len(consts), len(cts), len(deps)
    plan = [(j, dt) for j, dts in enumerate(grad_dtypes) for dt in (dts if isinstance(dts, tuple) else (dts,))]
    ng = len(plan)
    widths = [specs[j].block_shape[1] for j, _ in plan]
    grad_dtypes = [dt for _, dt in plan]

    def body(*refs):
        vals = [r[...].astype(F32) for r in refs[:nr + nc]]
        ctv = tuple(r[...].astype(F32) for r in refs[nr + nc:nr + nc + nct])
        _, vjp = jax.vjp(fn, *vals)
        grads = vjp(ctv)
        outs = refs[nr + nc + nct + nd:]
        for o, (j, _) in zip(outs[:ng], plan):
            o[...] = grads[j].astype(o.dtype)

        @pl.when(pl.program_id(0) == 0)
        def _():
            for o in outs[ng:]:
                o[...] = jnp.zeros_like(o)

        for o, g in zip(outs[ng:], grads[nr:]):
            o[...] += g

    return pl.pallas_call(
        body, grid=(n_rows // tm,), name=name,
        in_specs=list(specs) + [pl.BlockSpec(c.shape, lambda i: (0, 0)) for c in consts] + list(ct_specs)
        + [ANY_SPEC] * nd,
        out_specs=[pl.BlockSpec((tm, w), lambda i: (i, 0)) for w in widths]
        + [pl.BlockSpec(c.shape, lambda i: (0, 0)) for c in consts],
        out_shape=[SDS((n_rows, w), dt) for w, dt in zip(widths, grad_dtypes)] + [SDS(c.shape, F32) for c in consts],
        compiler_params=_cp("arbitrary"),
    )(*arrs, *consts, *ct_arrs, *deps)


def _tile(n, pref):
    t = min(n, pref)
    while n % t or (t % HD and t != n):
        t -= 1
    return t


def _matmul(name, a, b, dims, out_dtype, tm, tn, residual=None, deps=()):
    ta, tb = dims == TN, dims == NT
    m = a.shape[1] if ta else a.shape[0]
    k = a.shape[0] if ta else a.shape[1]
    n = b.shape[0] if tb else b.shape[1]
    tm, tn = _tile(m, tm), _tile(n, tn)

    def body(*refs):
        acc = _dot(refs[0][...], refs[1][...], dims)
        if residual is not None:
            acc = acc + refs[2][...]
        refs[-1][...] = acc.astype(out_dtype)

    in_specs = [pl.BlockSpec((k, tm), lambda i, j: (0, i)) if ta else pl.BlockSpec((tm, k), lambda i, j: (i, 0)),
                pl.BlockSpec((tn, k), lambda i, j: (j, 0)) if tb else pl.BlockSpec((k, tn), lambda i, j: (0, j))]
    ops = [a, b]
    if residual is not None:
        in_specs.append(pl.BlockSpec((tm, tn), lambda i, j: (i, j)))
        ops.append(residual)
    in_specs += [ANY_SPEC] * len(deps)
    ops += list(deps)
    return pl.pallas_call(
        body, grid=(m // tm, n // tn), name=name, in_specs=in_specs,
        out_specs=pl.BlockSpec((tm, tn), lambda i, j: (i, j)), out_shape=SDS((m, n), out_dtype),
        compiler_params=_cp("parallel", "parallel"),
    )(*ops)


def _proj_out_norm(mix, w_out, xs, gain, deps):
    t, k = mix.shape
    d = w_out.shape[1]
    tm = _tile(t, 512)

    def body(*refs):
        a, b, x, g = refs[:4]
        h1, h1n = refs[4 + len(deps):]
        acc = _dot(a[...], b[...]) + x[...]
        h1[...] = acc
        h1n[...] = _rms(acc, g[...]).astype(BF16)

    return pl.pallas_call(
        body, grid=(t // tm,), name="proj_out",
        in_specs=[pl.BlockSpec((tm, k), lambda i: (i, 0)), pl.BlockSpec((k, d), lambda i: (0, 0)),
                  pl.BlockSpec((tm, d), lambda i: (i, 0)), pl.BlockSpec((1, d), lambda i: (0, 0))] + [ANY_SPEC] * len(deps),
        out_specs=[pl.BlockSpec((tm, d), lambda i: (i, 0))] * 2, out_shape=[SDS((t, d), F32), SDS((t, d), BF16)],
        compiler_params=_cp("parallel"),
    )(mix, w_out, xs, gain, *deps)


def _proj_in_bwd_norm(dp, w, du_a, xs, gain, dh1b, deps):
    t, k = dp.shape
    d = w.shape[0]
    tm = _tile(t, 256)

    def body(*refs):
        a, b, ua, x, g, dh = refs[:6]
        gx, dgain = refs[6 + len(deps):]
        _, vjp = jax.vjp(lambda xx, gn: _rms(xx, gn), x[...], g[...])
        dx, dg = vjp(_dot(a[...], b[...], NT) + ua[...])
        gx[...] = dx + dh[...].astype(F32)

        @pl.when(pl.program_id(0) == 0)
        def _():
            dgain[...] = jnp.zeros_like(dgain)

        dgain[...] += dg

    row = pl.BlockSpec((tm, d), lambda i: (i, 0))
    vec = pl.BlockSpec((1, d), lambda i: (0, 0))
    return pl.pallas_call(
        body, grid=(t // tm,), name="proj_in_bwd_b",
        in_specs=[pl.BlockSpec((tm, k), lambda i: (i, 0)), pl.BlockSpec((d, k), lambda i: (0, 0), pipeline_mode=ONE_BUFFER),
                  row, row, vec, row] + [ANY_SPEC] * len(deps),
        out_specs=[row, vec], out_shape=[SDS((t, d), F32), SDS((1, d), F32)], compiler_params=_cp("arbitrary"),
    )(dp, w, du_a, xs, gain, dh1b, *deps)


def _ffn_up(h1n, wgu):
    t, d = h1n.shape
    w = wgu.shape[3]
    tm = _tile(t, 512)

    def body(a, b, gu, act):
        x = a[...]
        g = _dot(x, b[0])
        u = _dot(x, b[1])
        gu[0] = g.astype(BF16)
        gu[1] = u.astype(BF16)
        act[...] = (_silu(g) * u).astype(BF16)

    return pl.pallas_call(
        body, grid=(4, t // tm), name="ffn_up",
        in_specs=[pl.BlockSpec((tm, d), lambda j, i: (i, 0)), pl.BlockSpec((2, None, d, w), lambda j, i: (0, j, 0, 0))],
        out_specs=[pl.BlockSpec((2, None, tm, w), lambda j, i: (0, j, i, 0)), pl.BlockSpec((tm, w), lambda j, i: (i, j))],
        out_shape=[SDS((2, 4, t, w), BF16), SDS((t, 4 * w), BF16)],
        compiler_params=_cp("parallel", "parallel"),
    )(h1n, wgu)


def _ffn_down_loss(act, wdown, h1, target):
    t, f = act.shape
    d = wdown.shape[1]
    tm, tn = _tile(t, 1024), _tile(d, 512)

    def body(a, b, h, tg, dyb, ls):
        e = _dot(a[...], b[...]) + h[...] - tg[...]
        dyb[...] = (e * (1.0 / d)).astype(BF16)
        ls[...] = jnp.broadcast_to(jnp.sum(e * e), (8, HD))

    return pl.pallas_call(
        body, grid=(t // tm, d // tn), name="ffn_down_loss",
        in_specs=[pl.BlockSpec((tm, f), lambda i, j: (i, 0)), pl.BlockSpec((f, tn), lambda i, j: (0, j)),
                  pl.BlockSpec((tm, tn), lambda i, j: (i, j)), pl.BlockSpec((tm, tn), lambda i, j: (i, j))],
        out_specs=[pl.BlockSpec((tm, tn), lambda i, j: (i, j)), pl.BlockSpec((8, HD), lambda i, j: (i, j))],
        out_shape=[SDS((t, d), BF16), SDS((8 * (t // tm), HD * (d // tn)), F32)],
        compiler_params=_cp("parallel", "parallel"),
    )(act, wdown, h1, target)


def _ffn_down_bwd(dyb, wdown4, gu):
    t, d = dyb.shape
    w = wdown4.shape[1]
    tm = _tile(t, 512)

    def body(a, b, gu_ref, out):
        da = _dot(a[...], b[...], NT)
        g = gu_ref[0].astype(F32)
        u = gu_ref[1].astype(F32)
        s = _sigmoid(g)
        out[0] = (da * u * (s * (1.0 + g * (1.0 - s)))).astype(BF16)
        out[1] = (da * g * s).astype(BF16)

    return pl.pallas_call(
        body, grid=(4, t // tm), name="ffn_down_bwd",
        in_specs=[pl.BlockSpec((tm, d), lambda j, i: (i, 0)), pl.BlockSpec((None, w, d), lambda j, i: (j, 0, 0)),
                  pl.BlockSpec((2, None, tm, w), lambda j, i: (0, j, i, 0))],
        out_specs=pl.BlockSpec((2, None, tm, w), lambda j, i: (0, j, i, 0)),
        out_shape=SDS((2, 4, t, w), BF16),
        compiler_params=_cp("parallel", "parallel"),
    )(dyb, wdown4, gu)


def _ffn_up_bwd_x(dgu, wgu, h1, gain, dyb, deps):
    _, t, w = dgu.shape
    d = wgu.shape[1]
    tm = _tile(t, 512)

    def body(*refs):
        a, b, h, g, dy = refs[:5]
        dh1, dgain, acc = refs[5 + len(deps):]
        i, j = pl.program_id(0), pl.program_id(1)

        @pl.when(j == 0)
        def _():
            acc[...] = jnp.zeros_like(acc)

        acc[...] += _dot(a[...], b[...], NT)

        @pl.when(j == N_DEV - 1)
        def _():
            _, vjp = jax.vjp(lambda x, gn: _rms(x, gn), h[...], g[...])
            dx, dg = vjp(acc[...])
            dh1[...] = (dx + dy[...].astype(F32)).astype(dh1.dtype)

            @pl.when(i == 0)
            def _():
                dgain[...] = jnp.zeros_like(dgain)

            dgain[...] += dg

    row = pl.BlockSpec((tm, d), lambda i, j: (i, 0))
    return pl.pallas_call(
        body, grid=(t // tm, N_DEV), name="ffn_up_bwd_x",
        in_specs=[pl.BlockSpec((None, tm, w), lambda i, j: (j, i, 0)), pl.BlockSpec((None, d, w), lambda i, j: (j, 0, 0)),
                  row, pl.BlockSpec((1, d), lambda i, j: (0, 0)), row] + [ANY_SPEC] * len(deps),
        out_specs=[row, pl.BlockSpec((1, d), lambda i, j: (0, 0))],
        out_shape=[SDS((t, d), BF16), SDS((1, d), F32)], scratch_shapes=[pltpu.VMEM((tm, d), F32)],
        compiler_params=_cp("arbitrary", "arbitrary"),
    )(dgu, wgu, h1, gain, dyb, *deps)


def _ffn_up_bwd_w(h1n, dgu):
    _, t, w = dgu.shape
    d = h1n.shape[1]
    tm = _tile(d, 512)

    def body(a, b, out):
        out[...] = _dot(a[...], b[...], TN).astype(BF16)

    return pl.pallas_call(
        body, grid=(8, d // tm), name="ffn_up_bwd_w",
        in_specs=[pl.BlockSpec((t, tm), lambda j, i: (0, i)), pl.BlockSpec((None, t, w), lambda j, i: (j, 0, 0))],
        out_specs=pl.BlockSpec((None, tm, w), lambda j, i: (j, i, 0)), out_shape=SDS((8, d, w), BF16),
        compiler_params=_cp("parallel", "parallel"),
    )(h1n, dgu)


def _fox_prep(fq, fk, sm, fb, qg, kg, h):
    qn = _rms(fq, qg)
    kn = _rms(fk, kg)
    c = _cumsum_rows(-_softplus(-(sm + fb)))
    ccol = _lane_pick(c, L_FF + h)
    crow = jnp.sum(c.T * (_iota((HD, 1), 0) == L_FF + h).astype(F32), axis=0, keepdims=True)
    return qn, kn, ccol, crow


def _softmax_times(s, v):
    e = jnp.exp(s - lax.stop_gradient(jnp.max(s, axis=1, keepdims=True)))
    return _dot(e.astype(BF16), v.astype(BF16)) * (1.0 / jnp.sum(e, axis=1, keepdims=True))


def _fox_block(q, k, v, cc, cr, off):
    bq = q.shape[0]
    assert k.shape[0] == off + bq
    s = _dot((q * (HD ** -0.5)).astype(BF16), k.astype(BF16), NT) + cc - cr
    diag = jnp.where(_iota((bq, bq), 1) <= _iota((bq, bq), 0), s[:, off:], -1e30)
    s = jnp.concatenate([s[:, :off], diag], axis=1) if off else diag
    return _softmax_times(s, v)


ONE_BUFFER = pl.Buffered(1)


def _pcol(t, cb):
    return pl.BlockSpec((t, HD), lambda h, cb=cb: (0, cb + h), pipeline_mode=ONE_BUFFER)


def _smcol(t):
    return pl.BlockSpec((t, HD), lambda h: (0, SM), pipeline_mode=ONE_BUFFER)


def _head(t):
    return pl.BlockSpec((t, HD), lambda h: (0, h), pipeline_mode=ONE_BUFFER)


def _small(n):
    return pl.BlockSpec((n, HD), lambda h: (0, 0), pipeline_mode=ONE_BUFFER)


def _fox_fwd(p, fb, qg, kg, bq):
    t = p.shape[0]

    def body(fq, fk, fv, sm, fb_r, qg_r, kg_r, o, qn_s, cc_s):
        h = pl.program_id(0)
        qn, kn, ccol, crow = _fox_prep(fq[...], fk[...], sm[...], fb_r[...], qg_r[...], kg_r[...], h)
        qn_s[...] = qn
        cc_s[...] = ccol
        knb = kn.astype(BF16)
        vb = fv[...].astype(BF16)
        for i in range(t // bq):
            rows, ext = pl.ds(i * bq, bq), (i + 1) * bq
            o[rows, :] = _fox_block(qn_s[rows, :], knb[:ext], vb[:ext], cc_s[rows, :], crow[:, :ext], i * bq).astype(o.dtype)

    return pl.pallas_call(
        body, grid=(NF,), name="fox_fwd",
        in_specs=[_pcol(t, FQ), _pcol(t, FK), _pcol(t, FV), _smcol(t), _small(1), _small(1), _small(1)],
        out_specs=_head(t), out_shape=SDS((t, NF * HD), BF16),
        scratch_shapes=[pltpu.VMEM((t, HD), F32), pltpu.VMEM((t, 1), F32)],
        compiler_params=_cp("parallel"),
    )(p, p, p, p, fb, qg, kg)


def _fox_bwd(p, fb, qg, kg, dmix, bq, deps=()):
    t = p.shape[0]

    def body(*refs):
        fq, fk, fv, sm, fb_r, qg_r, kg_r, do = refs[:8]
        dfq, dfk, dfv, dsm, dfb, dqg, dkg, qn_s, cc_s, dqn_s, dcc_s, dkn_s, dv_s, dcr_s = refs[8 + len(deps):]
        h = pl.program_id(0)
        qn, kn, ccol, crow = _fox_prep(fq[...], fk[...], sm[...], fb_r[...], qg_r[...], kg_r[...], h)
        qn_s[...] = qn
        cc_s[...] = ccol
        v = fv[...]
        dkn_s[...] = jnp.zeros_like(dkn_s)
        dv_s[...] = jnp.zeros_like(dv_s)
        dcr_s[...] = jnp.zeros_like(dcr_s)

        for i in range(t // bq):
            rows, ext = pl.ds(i * bq, bq), (i + 1) * bq
            _, vjp = jax.vjp(lambda a, b, c, d, e, off=i * bq: _fox_block(a, b, c, d, e, off),
                             qn_s[rows, :], kn[:ext], v[:ext], cc_s[rows, :], crow[:, :ext])
            dq, dk, dv, dcc, dcr = vjp(do[rows, :].astype(F32))
            dqn_s[rows, :] = dq
            dcc_s[rows, :] = dcc
            dkn_s[:ext, :] += dk
            dv_s[:ext, :] += dv
            dcr_s[:, :ext] += dcr
        _, prep_vjp = jax.vjp(lambda a, b, c, d, e, f: _fox_prep(a, b, c, d, e, f, h),
                              fq[...], fk[...], sm[...], fb_r[...], qg_r[...], kg_r[...])
        g_fq, g_fk, g_sm, g_fb, g_qg, g_kg = prep_vjp((dqn_s[...], dkn_s[...], dcc_s[...], dcr_s[...]))
        dfq[...] = g_fq.astype(dfq.dtype)
        dfk[...] = g_fk.astype(dfk.dtype)
        dfv[...] = dv_s[...].astype(dfv.dtype)

        @pl.when(h == 0)
        def _():
            for r in (dsm, dfb, dqg, dkg):
                r[...] = jnp.zeros_like(r)

        dsm[...] += g_sm
        dfb[...] += g_fb
        dqg[...] += g_qg
        dkg[...] += g_kg

    head = _head(t)
    return pl.pallas_call(
        body, grid=(NF,), name="fox_bwd",
        in_specs=[_pcol(t, FQ), _pcol(t, FK), _pcol(t, FV), _smcol(t), _small(1), _small(1), _small(1), head]
        + [ANY_SPEC] * len(deps),
        out_specs=[head, head, head, _small(t), _small(1), _small(1), _small(1)],
        out_shape=[SDS((t, NF * HD), BF16)] * 3 + [SDS((t, HD), F32)] + [SDS((1, HD), F32)] * 3,
        scratch_shapes=[pltpu.VMEM((t, HD), F32), pltpu.VMEM((t, 1), F32), pltpu.VMEM((t, HD), F32),
                        pltpu.VMEM((t, 1), F32), pltpu.VMEM((t, HD), F32), pltpu.VMEM((t, HD), F32),
                        pltpu.VMEM((1, t), F32)],
        compiler_params=_cp("arbitrary"),
    )(p, p, p, p, fb, qg, kg, dmix, *deps)


def _mem_attn(mq, mk, mv, qg, kg):
    s = _dot((_rms(mq, qg) * (HD ** -0.5)).astype(BF16), _rms(mk, kg).astype(BF16), NT)
    return _softmax_times(s, mv)


def _mem_fwd(p, mkv, qg, kg):
    t, ml = p.shape[0], mkv.shape[0]

    def body(mq, mk, mv, qg_r, kg_r, o):
        o[...] = _mem_attn(mq[...], mk[...], mv[...], qg_r[...], kg_r[...]).astype(o.dtype)

    return pl.pallas_call(
        body, grid=(NM,), name="mem_fwd",
        in_specs=[_pcol(t, MQ), pl.BlockSpec((ml, HD), lambda h: (0, h)), pl.BlockSpec((ml, HD), lambda h: (0, NM + h)),
                  _small(1), _small(1)],
        out_specs=pl.BlockSpec((t, HD), lambda h: (0, h)), out_shape=SDS((t, NM * HD), BF16),
        compiler_params=_cp("parallel"),
    )(p, mkv, mkv, qg, kg)


def _mem_bwd(p, mkv, qg, kg, dmix, deps=()):
    t, ml = p.shape[0], mkv.shape[0]

    def body(*refs):
        mq, mk, mv, qg_r, kg_r, do = refs[:6]
        dmq, dmk, dmv, dqg, dkg = refs[6 + len(deps):]
        _, vjp = jax.vjp(_mem_attn, mq[...], mk[...], mv[...], qg_r[...], kg_r[...])
        g_q, g_k, g_v, g_qg, g_kg = vjp(do[...].astype(F32))
        dmq[...] = g_q.astype(dmq.dtype)
        dmk[...] = g_k
        dmv[...] = g_v

        @pl.when(pl.program_id(0) == 0)
        def _():
            dqg[...] = jnp.zeros_like(dqg)
            dkg[...] = jnp.zeros_like(dkg)

        dqg[...] += g_qg
        dkg[...] += g_kg

    return pl.pallas_call(
        body, grid=(NM,), name="mem_bwd",
        in_specs=[_pcol(t, MQ), pl.BlockSpec((ml, HD), lambda h: (0, h)), pl.BlockSpec((ml, HD), lambda h: (0, NM + h)),
                  _small(1), _small(1), pl.BlockSpec((t, HD), lambda h: (0, NF + NG + h))] + [ANY_SPEC] * len(deps),
        out_specs=[pl.BlockSpec((t, HD), lambda h: (0, h)), pl.BlockSpec((ml, HD), lambda h: (0, h)),
                   pl.BlockSpec((ml, HD), lambda h: (0, h)), _small(1), _small(1)],
        out_shape=[SDS((t, NM * HD), BF16), SDS((ml, NM * HD), F32), SDS((ml, NM * HD), F32),
                   SDS((1, HD), F32), SDS((1, HD), F32)],
        compiler_params=_cp("arbitrary"),
    )(p, mkv, mkv, qg, kg, dmix, *deps)


def _shift_down(x, s):
    if s == 0:
        return x
    return jnp.where(_iota(x.shape, 0) >= s, pltpu.roll(x, s, 0), 0.0)


def _shift_up(x, s):
    if s == 0:
        return x
    n = x.shape[0]
    return jnp.where(_iota(x.shape, 0) < n - s, pltpu.roll(x, n - s, 0), 0.0)


@jax.custom_vjp
def _conv4(x, w0, w1, w2, w3):
    return w0 * _shift_down(x, 3) + w1 * _shift_down(x, 2) + w2 * _shift_down(x, 1) + w3 * x


def _conv4_fwd(x, w0, w1, w2, w3):
    return _conv4(x, w0, w1, w2, w3), (x, w0, w1, w2, w3)


def _conv4_bwd(res, dy):
    x, w0, w1, w2, w3 = res
    ups = [_shift_up(dy, 3 - k) for k in range(4)]
    dx = w0 * ups[0] + w1 * ups[1] + w2 * ups[2] + w3 * ups[3]
    return (dx,) + tuple(jnp.sum(up * x, axis=0, keepdims=True) for up in ups)


_conv4.defvjp(_conv4_fwd, _conv4_bwd)


HALO = 8


def _gdn_gates(sm, alog, dtb):
    lane = _iota((1, HD), 1)
    g = -jnp.exp(alog) * _softplus(sm + dtb)
    return (jnp.where((lane >= L_GA) & (lane < L_GA + NG), g,
                      jnp.where((lane >= L_GB) & (lane < L_GB + NG), _sigmoid(sm), 0.0)),)


def _gdn_prep(gq, gk, gv, gates, taps, h):
    q, k, v = [_silu(_conv4(x, *taps[4 * j:4 * j + 4]))[HALO:] for j, x in enumerate((gq, gk, gv))]
    q = q * lax.rsqrt(jnp.sum(q * q, axis=-1, keepdims=True) + NORM_EPS) * (HD ** -0.5)
    k = k * lax.rsqrt(jnp.sum(k * k, axis=-1, keepdims=True) + NORM_EPS)
    return q, k, v, _lane_pick(gates, L_GA + h), _lane_pick(gates, L_GB + h)


def _split(x, n):
    parts, rest = [], x
    for i in range(n):
        parts.append(rest.astype(BF16))
        if i + 1 < n:
            rest = rest - parts[-1].astype(F32)
    return parts


def _raw_dot(a, b, form):
    lead = a.ndim - 2
    ca, cb = {"nn": (1, 0), "nt": (1, 1), "tn": (0, 0)}[form]
    batch = ((0,), (0,)) if lead else ((), ())
    return lax.dot_general(a, b, (((ca + lead,), (cb + lead,)), batch), preferred_element_type=F32)


def _pdot_impl(a, b, form, mode):
    if mode == "1":
        return _raw_dot(a.astype(BF16), b.astype(BF16), form)
    if mode == "3":
        (ah, al), (bh, bl) = _split(a, 2), _split(b, 2)
        return _raw_dot(ah, bh, form) + (_raw_dot(al, bh, form) + _raw_dot(ah, bl, form))
    if mode == "xa":
        return sum(_raw_dot(a.astype(BF16), t, form) for t in reversed(_split(b, 3)))
    return sum(_raw_dot(t, b.astype(BF16), form) for t in reversed(_split(a, 3)))


@functools.partial(jax.custom_vjp, nondiff_argnums=(2, 3))
def _pdot(a, b, form, mode):
    return _pdot_impl(a, b, form, mode)


def _pdot_fwd(a, b, form, mode):
    return _pdot_impl(a, b, form, mode), (a, b)


def _pdot_bwd(form, mode, res, ct):
    a, b = res
    da_args, db_args = {"nn": ((ct, b, "nt"), (a, ct, "tn")), "nt": ((ct, b, "nn"), (ct, a, "tn")),
                        "tn": ((b, ct, "nt"), (a, ct, "nn"))}[form]

    def side(args, exact):
        if mode in ("1", "3"):
            return mode
        return "xa" if args[0] is exact else "xb"

    if mode == "xa":
        return jnp.zeros_like(a), _pdot_impl(*db_args, side(db_args, a))
    if mode == "xb":
        return _pdot_impl(*da_args, side(da_args, b)), jnp.zeros_like(b)
    return _pdot_impl(*da_args, mode), _pdot_impl(*db_args, mode)


_pdot.defvjp(_pdot_fwd, _pdot_bwd)

GDN_QK, GDN_INV, GDN_SCAN = "1", "1", "1"


@jax.custom_vjp
def _tri_inv(low):
    eye = (_iota((CHUNK, CHUNK), 0) == _iota((CHUNK, CHUNK), 1)).astype(F32)
    inv = eye - low
    pw = low
    for _ in range(5):
        pw = _pdot_impl(pw, pw, "nn", GDN_INV)
        inv = inv + _pdot_impl(inv, pw, "nn", GDN_INV)
    return inv


def _tri_inv_fwd(low):
    inv = _tri_inv(low)
    return inv, inv


def _tri_inv_bwd(inv, ct):
    return (-_pdot_impl(_pdot_impl(inv, ct, "tn", GDN_INV), inv, "nt", GDN_INV),)


_tri_inv.defvjp(_tri_inv_fwd, _tri_inv_bwd)


def _gdn_intra(q, k, v, g, beta):
    n = q.shape[0]
    r, c = _iota((CHUNK, CHUNK), 0), _iota((CHUNK, CHUNK), 1)
    tril, strict = r >= c, r > c
    trilf = jnp.broadcast_to(tril.astype(F32), (n, CHUNK, CHUNK))
    gcm = _pdot(trilf, jnp.broadcast_to(g, (n, CHUNK, CHUNK)), "nn", "xa")
    gcf = _pdot(trilf, jnp.broadcast_to(g, (n, CHUNK, HD)), "nn", "xa")
    lane0 = (_iota((1, 1, CHUNK), 2) == 0).astype(F32)
    gcr = _pdot(jnp.ones((n, CHUNK, CHUNK), F32), gcm * lane0, "nt", "xa")
    decay = jnp.where(tril, jnp.exp(jnp.where(tril, gcm - gcr, 0.0)), 0.0)
    egc = jnp.exp(gcf)
    kb = k * beta
    low = jnp.where(strict, _pdot(kb, k, "nt", GDN_QK) * decay, 0.0)
    inv = _tri_inv(low)
    u = _pdot(inv, v * beta, "nn", GDN_INV)
    w = _pdot(inv, kb * egc, "nn", GDN_INV)
    at = jnp.where(tril, _pdot(q, k, "nt", GDN_QK) * decay, 0.0)
    gl = jnp.sum(jnp.broadcast_to(g, (n, CHUNK, HD)), axis=1, keepdims=True)
    kd = k * jnp.exp(gl - gcf)
    return (_pdot(kd, w, "tn", GDN_SCAN), _pdot(kd, u, "tn", GDN_SCAN), q * egc - _pdot(at, w, "nn", GDN_SCAN),
            _pdot(at, u, "nn", GDN_SCAN), gl)


def _gdn_step(s, kw, ku, a, b, gl):
    return _pdot(a, s, "nn", GDN_SCAN) + b, s * jnp.exp(gl) - _pdot(kw, s, "nn", GDN_SCAN) + ku


SCAN_HEADS = 3
SCAN_UNROLL = 4


def _gdn_chunked_scratch(nc):
    big = pltpu.VMEM((nc, CHUNK, HD), F32)
    return [big, big, big, pltpu.VMEM((nc, CHUNK, 1), F32), pltpu.VMEM((nc, CHUNK, 1), F32)]


N_TERMS = 5


def _gdn_term_shapes(nc):
    return [(nc, HD, HD), (nc, HD, HD), (nc, CHUNK, HD), (nc, CHUNK, HD), (nc, 1, HD)]


def _per_head(shape, heads=None, one_buffer=True):
    lead = (None,) if heads is None else (heads,)
    return pl.BlockSpec(lead + tuple(shape), lambda h: (h,) + (0,) * len(shape),
                        pipeline_mode=ONE_BUFFER if one_buffer else None)


def _gdn_in_specs(t):
    cw = lambda cb: pl.BlockSpec((4, HD), lambda h, cb=cb: (0, cb + h))
    return [_pcol(t, GQ), _pcol(t, GK), _pcol(t, GV), _small(t), cw(0), cw(NG), cw(2 * NG)]


def _taps(wq, wk, wv):
    return tuple(w[k:k + 1, :] for w in (wq, wk, wv) for k in range(4))


def _prep_rows(t):
    return min(t, 256)


def _gdn_pad(srcs, pads):
    for src, pad in zip(srcs, pads):
        pad[0:HALO, :] = jnp.zeros((HALO, HD), F32)
        pad[HALO:, :] = src[...]


def _gdn_stage(pads, gates, taps, h, chunked):
    t = gates.shape[0]
    rows = _prep_rows(t)
    per = rows // CHUNK

    def tile(i, carry):
        r0 = pl.multiple_of(i * rows, rows)
        vals = _gdn_prep(*[p[pl.ds(r0, rows + HALO), :] for p in pads], gates[pl.ds(r0, rows), :], taps, h)
        for v, r in zip(vals, chunked):
            r[pl.ds(i * per, per)] = v.reshape(per, CHUNK, v.shape[-1])
        return carry

    lax.fori_loop(0, t // rows, tile, 0)


def _gdn_intra_all(chunked, intra):
    nc = chunked[0].shape[0]
    grp_n = math.gcd(nc, GROUP)

    def grp(i, carry):
        sl = pl.ds(pl.multiple_of(i * grp_n, grp_n), grp_n)
        for r, val in zip(intra, _gdn_intra(*[c[sl] for c in chunked])):
            r[sl] = val
        return carry

    lax.fori_loop(0, nc // grp_n, grp, 0)


def _gdn_fwd(pa, gates, conv):
    t = pa.shape[0]
    nc = t // CHUNK
    terms = _gdn_term_shapes(nc)

    def body(gq, gk, gv, gt, wq, wk, wv, *rest):
        h = pl.program_id(0)
        intra, chunked, pads = rest[:N_TERMS], rest[N_TERMS:N_TERMS + 5], rest[N_TERMS + 5:]
        _gdn_pad((gq, gk, gv), pads)
        _gdn_stage(pads, gt, _taps(wq, wk, wv), h, chunked)
        _gdn_intra_all(chunked, intra)

    qkv = [(nc, CHUNK, HD)] * 3
    outs = pl.pallas_call(
        body, grid=(NG,), name="gdn_fwd", in_specs=_gdn_in_specs(t),
        out_specs=[_per_head(sh, one_buffer=False) for sh in terms + qkv],
        out_shape=[SDS((NG,) + sh, F32) for sh in terms + qkv],
        scratch_shapes=_gdn_chunked_scratch(nc)[3:] + [pltpu.VMEM((t + HALO, HD), F32)] * 3, compiler_params=_cp("parallel"),
    )(pa, pa, pa, gates, conv, conv, conv)
    return list(outs[:N_TERMS]), list(outs[N_TERMS:])


def _gdn_scan(terms_in):
    nc = terms_in[0].shape[1]
    terms = _gdn_term_shapes(nc)

    def body(*refs):
        intra, o, states = refs[:N_TERMS], refs[N_TERMS], refs[N_TERMS + 1]

        def one(c, ss):
            rows = pl.ds(pl.multiple_of(c * CHUNK, CHUNK), CHUNK)
            loaded = [[r[hh, c] for r in intra] for hh in range(SCAN_HEADS)]
            res = [_gdn_step(ss[hh], *loaded[hh]) for hh in range(SCAN_HEADS)]
            for hh in range(SCAN_HEADS):
                states[hh, c] = ss[hh]
                o[rows, hh * HD:(hh + 1) * HD] = res[hh][0]
            return tuple(r[1] for r in res)

        per_trip = math.gcd(nc, SCAN_UNROLL)

        def step(i, ss):
            for k in range(per_trip):
                ss = one(per_trip * i + k, ss)
            return ss

        lax.fori_loop(0, nc // per_trip, step, tuple(jnp.zeros((HD, HD), F32) for _ in range(SCAN_HEADS)))

    return pl.pallas_call(
        body, grid=(NG // SCAN_HEADS,), name="gdn_scan", in_specs=[_per_head(sh, SCAN_HEADS) for sh in terms],
        out_specs=[pl.BlockSpec((nc * CHUNK, SCAN_HEADS * HD), lambda h: (0, h), pipeline_mode=ONE_BUFFER),
                   _per_head((nc, HD, HD), SCAN_HEADS)],
        out_shape=[SDS((nc * CHUNK, NG * HD), F32), SDS((NG, nc, HD, HD), F32)], compiler_params=_cp("parallel"),
    )(*terms_in)


def _gdn_bwd_scan(saved, do_raw):
    nc = saved[0].shape[1]
    terms = _gdn_term_shapes(nc)

    def body(*refs):
        intra, states, do, outs = refs[:N_TERMS], refs[N_TERMS], refs[N_TERMS + 1], refs[N_TERMS + 2:]

        def one(c, dss):
            rows = pl.ds(pl.multiple_of(c * CHUNK, CHUNK), CHUNK)
            loaded = [[states[hh, c]] + [r[hh, c] for r in intra] for hh in range(SCAN_HEADS)]
            cts = [do[rows, hh * HD:(hh + 1) * HD] for hh in range(SCAN_HEADS)]
            grads = [jax.vjp(_gdn_step, *loaded[hh])[1]((cts[hh], dss[hh])) for hh in range(SCAN_HEADS)]
            for hh in range(SCAN_HEADS):
                for r, gval in zip(outs, grads[hh][1:]):
                    r[hh, c] = gval
            return tuple(g[0] for g in grads)

        per_trip = math.gcd(nc, SCAN_UNROLL)

        def bwd(i, dss):
            c = nc - 1 - per_trip * i
            for k in range(per_trip):
                dss = one(c - k, dss)
            return dss

        lax.fori_loop(0, nc // per_trip, bwd, tuple(jnp.zeros((HD, HD), F32) for _ in range(SCAN_HEADS)))

    return pl.pallas_call(
        body, grid=(NG // SCAN_HEADS,), name="gdn_bwd_scan",
        in_specs=[_per_head(sh, SCAN_HEADS) for sh in terms] + [_per_head((nc, HD, HD), SCAN_HEADS)]
        + [pl.BlockSpec((nc * CHUNK, SCAN_HEADS * HD), lambda h: (0, h), pipeline_mode=ONE_BUFFER)],
        out_specs=[_per_head(sh, SCAN_HEADS) for sh in terms],
        out_shape=[SDS((NG,) + sh, F32) for sh in terms], compiler_params=_cp("parallel"),
    )(*saved, do_raw)


def _gdn_bwd(pa, gates, conv, dterms, qkv):
    t = pa.shape[0]
    nc = t // CHUNK
    terms = _gdn_term_shapes(nc)

    def body(*refs):
        gq, gk, gv, gt, wq, wk, wv = refs[:7]
        dintra, qkv = refs[7:7 + N_TERMS], refs[7 + N_TERMS:10 + N_TERMS]
        dgq, dgk, dgv, dgt, dwq, dwk, dwv = refs[10 + N_TERMS:17 + N_TERMS]
        chunked, pads, dpads, dgt_s = (refs[17 + N_TERMS:22 + N_TERMS], refs[22 + N_TERMS:25 + N_TERMS],
                                       refs[25 + N_TERMS:28 + N_TERMS], refs[28 + N_TERMS])
        h = pl.program_id(0)
        taps = _taps(wq, wk, wv)
        _gdn_pad((gq, gk, gv), pads)
        rows = _prep_rows(t)
        per = rows // CHUNK

        def gates_tile(i, carry):
            gtile = gt[pl.ds(pl.multiple_of(i * rows, rows), rows), :]
            chunked[3][pl.ds(i * per, per)] = _lane_pick(gtile, L_GA + h).reshape(per, CHUNK, 1)
            chunked[4][pl.ds(i * per, per)] = _lane_pick(gtile, L_GB + h).reshape(per, CHUNK, 1)
            return carry

        lax.fori_loop(0, t // rows, gates_tile, 0)
        grp_n = math.gcd(nc, GROUP)

        def grp(i, carry):
            sl = pl.ds(pl.multiple_of(i * grp_n, grp_n), grp_n)
            _, vjp = jax.vjp(_gdn_intra, *[r[sl] for r in qkv], chunked[3][sl], chunked[4][sl])
            for r, gval in zip(chunked, vjp(tuple(r[sl] for r in dintra))):
                r[sl] = gval
            return carry

        lax.fori_loop(0, nc // grp_n, grp, 0)

        for r in dpads:
            r[...] = jnp.zeros_like(r)

        def tile(i, dtaps):
            r0 = pl.multiple_of(i * rows, rows)
            win = pl.ds(r0, rows + HALO)
            _, vjp = jax.vjp(lambda *a: _gdn_prep(*a, h), *[p[win, :] for p in pads], gt[pl.ds(r0, rows), :], taps)
            grads = vjp(tuple(r[pl.ds(i * per, per)].reshape(rows, r.shape[-1]) for r in chunked))
            for r, gval in zip(dpads, grads[:3]):
                r[win, :] += gval
            dgt_s[pl.ds(r0, rows), :] = grads[3]
            return jax.tree.map(jnp.add, dtaps, grads[4])

        dtaps = lax.fori_loop(0, t // rows, tile, (jnp.zeros((1, HD), F32),) * 12)
        for r, dpad in zip((dgq, dgk, dgv), dpads):
            r[...] = dpad[HALO:, :].astype(r.dtype)
        for j, r in enumerate((dwq, dwk, dwv)):
            for k in range(4):
                r[k:k + 1, :] = dtaps[4 * j + k]

        @pl.when(h == 0)
        def _():
            dgt[...] = jnp.zeros_like(dgt)

        dgt[...] += dgt_s[...]

    head = _head(t)
    taps = pl.BlockSpec((4, HD), lambda h: (0, h))
    return pl.pallas_call(
        body, grid=(NG,), name="gdn_bwd",
        in_specs=_gdn_in_specs(t) + [_per_head(sh) for sh in terms + [(nc, CHUNK, HD)] * 3],
        out_specs=[head, head, head, _small(t), taps, taps, taps],
        out_shape=[SDS((t, NG * HD), BF16)] * 3 + [SDS((t, HD), F32)] + [SDS((4, NG * HD), F32)] * 3,
        scratch_shapes=_gdn_chunked_scratch(nc) + [pltpu.VMEM((t + HALO, HD), F32)] * 6 + [pltpu.VMEM((t, HD), F32)],
        compiler_params=_cp("arbitrary"),
    )(pa, pa, pa, gates, conv, conv, conv, *dterms, *qkv)


def _gdn_post(o, z, gain):
    return (jnp.concatenate(
        [_rms(o[:, h * HD:(h + 1) * HD], gain) * _silu(z[:, h * HD:(h + 1) * HD]) for h in range(NG)], axis=1),)


def _place():
    return lax.axis_index("x"), lax.axis_index("y"), lax.axis_index("c")


def _sum_blocks(name, parts):
    _, r, c = parts.shape
    tr = 64 if r % 64 == 0 else r

    def body(x, o):
        acc = x[0].astype(F32)
        for d in range(1, N_DEV):
            acc = acc + x[d].astype(F32)
        o[...] = acc

    return pl.pallas_call(
        body, grid=(r // tr,), name=name, in_specs=[pl.BlockSpec((N_DEV, tr, c), lambda i: (0, i, 0))],
        out_specs=pl.BlockSpec((tr, c), lambda i: (i, 0)), out_shape=SDS((r, c), F32), compiler_params=_cp("parallel"),
    )(parts)


def _all_reduce_small(name, x, reduce):
    m_per, n = x.shape

    def body(x_ref, out_ref, send_sems, recv_sems, local_sem):
        px, py, pc = _place()
        me, sibling = (px, py, pc), (px, py, 1 - pc)
        chips = [(1 - px, py), (px, 1 - py), (1 - px, 1 - py)]
        buf = out_ref

        def rows(qx, qy, qc):
            return buf.at[pl.ds((4 * qx + 2 * qy + qc) * m_per, m_per), :]

        def copy(k, block, to, src=None):
            return pltpu.make_async_remote_copy(
                src_ref=rows(*block) if src is None else src, dst_ref=rows(*block),
                send_sem=send_sems.at[k], recv_sem=recv_sems.at[k], device_id=to, device_id_type=MESH)

        mine = pltpu.make_async_copy(x_ref, rows(*me), local_sem)
        mine.start()
        first = [copy(0, me, sibling, src=x_ref)]
        first += [copy(1 + j, me, (*chip, pc), src=x_ref) for j, chip in enumerate(chips)]
        for cp in first:
            cp.start()
        passed = [copy(4 + j, (*chip, pc), sibling) for j, chip in enumerate(chips)]
        for j, chip in enumerate(chips):
            copy(1 + j, (*chip, pc), me).wait_recv()
            passed[j].start()
        copy(0, sibling, me).wait_recv()
        for j, chip in enumerate(chips):
            copy(4 + j, (*chip, 1 - pc), me).wait_recv()
        for cp in first + passed:
            cp.wait_send()
        mine.wait()

    gathered = pl.pallas_call(
        body, name=name, out_shape=SDS((N_DEV * m_per, n), x.dtype),
        in_specs=[pl.BlockSpec(memory_space=pltpu.VMEM)], out_specs=pl.BlockSpec(memory_space=pltpu.VMEM),
        scratch_shapes=[pltpu.SemaphoreType.DMA((7,)), pltpu.SemaphoreType.DMA((7,)), pltpu.SemaphoreType.DMA],
    )(x)
    if not reduce:
        return gathered
    return _sum_blocks(name + "_sum", gathered.reshape(N_DEV, m_per, n))


HBM_SPEC = pl.BlockSpec(memory_space=pltpu.HBM)
SEM_SPEC = pl.BlockSpec(memory_space=pltpu.SEMAPHORE)
EFFECT = pltpu.SideEffectType.DATAFLOW_SIDE_EFFECTING


def _copies_start(name, bufs, n_remote, n_local, build, deps):
    nb, nd = len(bufs), len(deps)
    sem_shapes = [pltpu.SemaphoreType.DMA((n_remote,)), pltpu.SemaphoreType.DMA((n_remote,))]
    if n_local:
        sem_shapes.append(pltpu.SemaphoreType.DMA((n_local,)))
    ns = len(sem_shapes)

    def body(*refs):
        sems = refs[nb + nd:nb + nd + ns]
        remote, local = build(refs[:nb], *sems, *([None] * (3 - ns)))
        for cp in local + remote:
            cp.start()
        refs[-1][...] = jnp.zeros((8, HD), F32)

    outs = pl.pallas_call(
        body, name=name,
        out_shape=(*sem_shapes, *[pltpu.HBM(b.shape, b.dtype) for b in bufs], SDS((8, HD), F32)),
        in_specs=[HBM_SPEC] * nb + [ANY_SPEC] * nd,
        out_specs=(*[SEM_SPEC] * ns, *[HBM_SPEC] * nb, pl.BlockSpec(memory_space=pltpu.VMEM)),
        input_output_aliases={i: ns + i for i in range(nb)},
        compiler_params=pltpu.CompilerParams(has_side_effects=EFFECT),
    )(*[pltpu.with_memory_space_constraint(b, pltpu.HBM) for b in bufs], *deps)
    return list(outs[:ns]), list(outs[ns:ns + nb]), outs[-1]


def _copies_wait(name, bufs, sems, build, after):
    nb, ns = len(bufs), len(sems)

    def body(*refs):
        remote, local = build(refs[:nb], *refs[nb:nb + ns], *([None] * (3 - ns)))
        for cp in local:
            cp.wait()
        for cp in remote:
            cp.wait_send()
            cp.wait_recv()

    outs = pl.pallas_call(
        body, name=name, out_shape=tuple(pltpu.HBM(b.shape, b.dtype) for b in bufs),
        in_specs=[HBM_SPEC] * nb + [SEM_SPEC] * ns + [ANY_SPEC] * len(after), out_specs=tuple([HBM_SPEC] * nb),
        input_output_aliases={i: i for i in range(nb)},
        compiler_params=pltpu.CompilerParams(has_side_effects=EFFECT),
    )(*bufs, *sems, *after)
    return list(outs)


def _remote(src, dst, send, recv, k, to):
    return pltpu.make_async_remote_copy(src_ref=src, dst_ref=dst, send_sem=send.at[k], recv_sem=recv.at[k],
                                        device_id=to, device_id_type=MESH)


class _Gather:
    def __init__(self, name, shards, deps):
        self.name, self.n = name, len(shards)
        lands = [lax.empty((N_DEV,) + s.shape, s.dtype) for s in shards]
        self.sems1, bufs, self.token = _copies_start(
            name + "_s1", list(shards) + lands, 4 * self.n, self.n, self._stage1(range(self.n)), deps)
        self.shards, self.lands, self.sems2 = bufs[:self.n], bufs[self.n:], {}

    def _stage1(self, idxs):
        def build(refs, send, recv, loc):
            x, y, c = _place()
            me = 4 * x + 2 * y + c
            targets = [(x, y, 1 - c), (1 - x, y, c), (x, 1 - y, c), (1 - x, 1 - y, c)]
            remote, local = [], []
            for pos, i in enumerate(idxs):
                src, land = refs[pos], refs[len(idxs) + pos]
                local.append(pltpu.make_async_copy(src, land.at[me], loc.at[i]))
                remote += [_remote(src, land.at[me], send, recv, 4 * i + k, to) for k, to in enumerate(targets)]
            return remote, local
        return build

    @staticmethod
    def _stage2(refs, send, recv, loc):
        x, y, c = _place()
        remote = []
        for pos, land in enumerate(refs):
            for j, (cx, cy) in enumerate([(1 - x, y), (x, 1 - y), (1 - x, 1 - y)]):
                blk = land.at[4 * cx + 2 * cy + c]
                remote.append(_remote(blk, blk, send, recv, 3 * pos + j, (x, y, 1 - c)))
        return remote, []

    def pass_on(self, idxs, after):
        tag, m = "".join(map(str, idxs)), len(idxs)
        bufs = _copies_wait(f"{self.name}_w1_{tag}", [self.shards[i] for i in idxs] + [self.lands[i] for i in idxs],
                            self.sems1, self._stage1(idxs), after)
        self.sems2[tag], lands, token = _copies_start(f"{self.name}_s2_{tag}", bufs[m:], 3 * m, 0, self._stage2, ())
        for pos, i in enumerate(idxs):
            self.lands[i] = lands[pos]
        return [token]

    def get(self, idxs, after):
        tag = "".join(map(str, idxs))
        return _copies_wait(f"{self.name}_w2_{tag}", [self.lands[i] for i in idxs], self.sems2[tag], self._stage2, after)


class _RelayGather:
    def __init__(self, name, shards, deps):
        self.name, self.n = name, len(shards)
        lands = [lax.empty((N_DEV,) + s.shape, s.dtype) for s in shards]
        self.sems, bufs, self.token = _copies_start(name + "_s1", list(shards) + lands, 3 * self.n, self.n, self._stage1, deps)
        self.shards, self.lands = bufs[:self.n], bufs[self.n:]

    def _stage1(self, refs, send, recv, loc):
        x, y, c = _place()
        me = 4 * x + 2 * y + c
        remote, local = [], []
        for i in range(self.n):
            src, land = refs[i], refs[self.n + i]
            local.append(pltpu.make_async_copy(src, land.at[me], loc.at[i]))
            remote += [_remote(src, land.at[me], send, recv, 3 * i + k, to)
                       for k, to in enumerate([(x, y, 1 - c), (1 - x, y, c), (x, 1 - y, c)])]
        return remote, local

    @staticmethod
    def _relay(refs, send, recv, loc):
        x, y, c = _place()
        remote = []
        for i, land in enumerate(refs):
            half = land.shape[1] // 2
            from_x = land.at[4 * (1 - x) + 2 * y + c].at[pl.ds(0, half)]
            from_y = land.at[4 * x + 2 * (1 - y) + c].at[pl.ds(half, half)]
            remote += [_remote(from_x, from_x, send, recv, 2 * i, (x, 1 - y, c)),
                       _remote(from_y, from_y, send, recv, 2 * i + 1, (1 - x, y, c))]
        return remote, []

    def forward(self, after):
        bufs = _copies_wait(self.name + "_w1", self.shards + self.lands, self.sems, self._stage1, after)
        self.sems, self.lands, self.token = _copies_start(self.name + "_sf", bufs[self.n:], 2 * self.n, 0, self._relay, ())
        return [self.token]

    def pass_on(self, after):
        lands = _copies_wait(self.name + "_wf", self.lands, self.sems, self._relay, after)
        self.sems, self.lands, self.token = _copies_start(self.name + "_s2", lands, 3 * self.n, 0, _Gather._stage2, ())
        return [self.token]

    def get(self, after):
        return _copies_wait(self.name + "_w2", self.lands, self.sems, _Gather._stage2, after)


def _rows_tile(r, row_bytes, target=1 << 20):
    tr = r
    while tr % 32 == 0 and tr * row_bytes > target:
        tr //= 2
    return tr


def _pair_add(name, g, got, c):
    _, r, cols = g.shape
    tr = _rows_tile(r, cols * 2)

    def body(s, a, b, o):
        o[...] = (a[...].astype(F32) + b[...].astype(F32)).astype(o.dtype)

    return pl.pallas_call(
        body, name=name, out_shape=SDS((4, r, cols), g.dtype),
        grid_spec=pltpu.PrefetchScalarGridSpec(
            num_scalar_prefetch=1, grid=(4, r // tr),
            in_specs=[pl.BlockSpec((None, tr, cols), lambda j, i, s: (2 * j + s[0], i, 0)),
                      pl.BlockSpec((None, tr, cols), lambda j, i, s: (j, i, 0))],
            out_specs=pl.BlockSpec((None, tr, cols), lambda j, i, s: (j, i, 0))),
        compiler_params=_cp("parallel", "parallel"),
    )(c.reshape(1), g, got)


def _quad_sum(name, part, got, chip, wmv=None):
    _, r, cols = part.shape
    tr = _rows_tile(r, cols * 4)
    n_out = 4 if wmv else 1

    def body(s, a, b1, b2, b3, *rest):
        g = ((a[...].astype(F32) + b1[...].astype(F32)) + b2[...].astype(F32)) + b3[...].astype(F32)
        rest[-n_out][...] = g
        if wmv:
            w, m, v = rest[:3]
            rest[-3][...], rest[-2][...], rest[-1][...] = _adamw(w[...], g, m[...], v[...])

    blk = lambda k: pl.BlockSpec((None, tr, cols), lambda i, s, k=k: (jnp.bitwise_xor(s[0], k), i, 0))
    row = pl.BlockSpec((tr, cols), lambda i, s: (i, 0))
    outs = pl.pallas_call(
        body, name=name, out_shape=[SDS((r, cols), F32)] * n_out,
        grid_spec=pltpu.PrefetchScalarGridSpec(
            num_scalar_prefetch=1, grid=(r // tr,), in_specs=[blk(0), blk(1), blk(2), blk(3)] + [row] * (n_out - 1),
            out_specs=[row] * n_out),
        compiler_params=_cp("parallel"),
    )(chip.reshape(1), part, got, got, got, *(wmv or ()))
    return tuple(outs) if wmv else outs[0]


class _Scatter:
    def __init__(self, name, grads, deps):
        self.name, self.n = name, len(grads)
        got = [lax.empty((4,) + g.shape[1:], g.dtype) for g in grads]
        self.sems, bufs, self.token = _copies_start(name + "_s1", list(grads) + got, 4 * self.n, 0, self._stage1, deps)
        self.grads, self.got = bufs[:self.n], bufs[self.n:]

    def _stage1(self, refs, send, recv, loc):
        x, y, c = _place()
        remote = []
        for i in range(self.n):
            remote += [_remote(refs[i].at[2 * j + 1 - c], refs[self.n + i].at[j], send, recv, 4 * i + j, (x, y, 1 - c))
                       for j in range(4)]
        return remote, []

    def _stage2(self, refs, send, recv, loc):
        x, y, c = _place()
        remote = []
        for i in range(self.n):
            for k in (1, 2, 3):
                tx = 1 - x if k & 2 else x
                ty = 1 - y if k & 1 else y
                remote.append(_remote(refs[i].at[2 * tx + ty], refs[self.n + i].at[2 * x + y], send, recv,
                                      3 * i + k - 1, (tx, ty, c)))
        return remote, []

    def mid(self, after):
        bufs = _copies_wait(self.name + "_w1", self.grads + self.got, self.sems, self._stage1, after)
        c = lax.axis_index("c").astype(jnp.int32)
        parts = [_pair_add(f"{self.name}_add{i}", bufs[i], bufs[self.n + i], c) for i in range(self.n)]
        got = [lax.empty(p.shape, p.dtype) for p in parts]
        self.sems, bufs, self.token = _copies_start(self.name + "_s2", parts + got, 3 * self.n, 0, self._stage2, ())
        self.parts, self.got = bufs[:self.n], bufs[self.n:]

    def end(self, after, wmv=None):
        bufs = _copies_wait(self.name + "_w2", self.parts + self.got, self.sems, self._stage2, after)
        chip = (2 * lax.axis_index("x") + lax.axis_index("y")).astype(jnp.int32)
        wmv = wmv or [None] * self.n
        return [_quad_sum(f"{self.name}_sum{i}", bufs[i], bufs[self.n + i], chip, wmv[i]) for i in range(self.n)]


def _adamw(w, g, m, v):
    m = ADAM_B1 * m + (1.0 - ADAM_B1) * g
    v = ADAM_B2 * v + (1.0 - ADAM_B2) * (g * g)
    m_hat = m / (1.0 - ADAM_B1 ** ADAM_STEP)
    v_hat = v / (1.0 - ADAM_B2 ** ADAM_STEP)
    return -ADAM_LR * (m_hat / (jnp.sqrt(v_hat) + ADAM_EPS) + ADAM_WD * w), m, v


def _adamw_call(name, w, g, m, v):
    r, c = w.shape
    tm = 64 if r % 64 == 0 else r
    return _rowwise(name, _adamw, [w, g, m, v], [], [(c, F32)] * 3, tm)


_IN_COLS = 5906


def _perm_in(w):
    pad = jnp.zeros((w.shape[0], 2 * HALF - _IN_COLS), w.dtype)
    return (jnp.concatenate([w[:, 2310:4614], w[:, 4614:5382]], axis=1),
            jnp.concatenate([w[:, :2304], w[:, 5394:5906], w[:, 2304:2310], w[:, 5382:5394], pad], axis=1))


def _unperm_in(ga, gb):
    return jnp.concatenate([gb[:, :2304], gb[:, 2816:2822], ga[:, :2304], ga[:, 2304:3072], gb[:, 2822:2834],
                            gb[:, 2304:2816]], axis=1)


def _lanes(v, at):
    return jnp.pad(v, ((0, 0), (at, HD - at - v.shape[1])))


_PACK = ("norm_mix", "mem_norm", "norm_ffn", "gdn_conv", "fox_q_norm", "fox_k_norm", "gdn_out_norm", "mem_q_norm",
         "mem_k_norm", "fox_f_bias", "gdn_a_log", "gdn_dt_bias", "loss")


def _pack(vals):
    parts = [vals[n].reshape(-1, HD) for n in _PACK]
    used = sum(p.shape[0] for p in parts)
    buf = jnp.concatenate(parts + [jnp.zeros((-used % 8, HD), F32)], axis=0)
    return buf, [(n, p.shape[0]) for n, p in zip(_PACK, parts)]


def _unpack(buf, layout):
    out, at = {}, 0
    for n, rows in layout:
        out[n] = buf[at:at + rows]
        at += rows
    return out


def kernel(x, mem, norm_mix, w_in, fox_f_bias, fox_q_norm, fox_k_norm, gdn_conv, gdn_a_log, gdn_dt_bias, gdn_out_norm, mem_norm, w_mem_kv, mem_q_norm, mem_k_norm, w_out, norm_ffn, w_gate_up, w_down, loss_target, m_norm_mix, m_w_in, m_fox_f_bias, m_fox_q_norm, m_fox_k_norm, m_gdn_conv, m_gdn_a_log, m_gdn_dt_bias, m_gdn_out_norm, m_mem_norm, m_w_mem_kv, m_mem_q_norm, m_mem_k_norm, m_w_out, m_norm_ffn, m_w_gate_up, m_w_down, v_norm_mix, v_w_in, v_fox_f_bias, v_fox_q_norm, v_fox_k_norm, v_gdn_conv, v_gdn_a_log, v_gdn_dt_bias, v_gdn_out_norm, v_mem_norm, v_w_mem_kv, v_mem_q_norm, v_mem_k_norm, v_w_out, v_norm_ffn, v_w_gate_up, v_w_down):
    args = dict(locals())
    d = x.shape[2]
    me = 4 * lax.axis_index("x") + 2 * lax.axis_index("y") + lax.axis_index("c")

    cshard = gdn_conv[0].shape[1]
    conv_pad = jnp.pad(gdn_conv[0], ((0, 4), (0, 3 * HD - cshard)))
    conv_all = _all_reduce_small("ag_conv", conv_pad, False).reshape(N_DEV, 8, 3 * HD)[:, :4, :cshard]
    conv_all = conv_all.transpose(1, 0, 2).reshape(4, N_DEV * cshard)
    w_in_a, w_in_b = _perm_in(w_in[0])
    comm = _StepComm(w_in_b, {"in_a": [w_in_a], "kv_out": [w_mem_kv[0], w_out[0]]}, w_gate_up[0], w_down[0], [conv_all])

    grad_x, loss_local, small_grads = _local_step(
        x[0], mem[0], loss_target[0], norm_mix, fox_f_bias, fox_q_norm, fox_k_norm, gdn_a_log, gdn_dt_bias,
        gdn_out_norm, mem_norm, mem_q_norm, mem_k_norm, norm_ffn, conv_all, comm)

    wmv = lambda n: (args[n][0], args["m_" + n][0], args["v_" + n][0])
    red = comm.finish([grad_x], {"ffn": [wmv("w_down"), wmv("w_gate_up")], "a": [None, wmv("w_out"), wmv("w_mem_kv")],
                                 "b": [None]})
    updated = {"w_down": red["ffn"][0], "w_gate_up": red["ffn"][1], "w_out": red["a"][1], "w_mem_kv": red["a"][2]}
    grads = {n: r[0] for n, r in updated.items()}
    grads["w_in"] = _unperm_in(red["a"][0], red["b"][0])
    small_grads["loss"] = jnp.broadcast_to(loss_local, (1, HD))
    packed, layout = _pack(small_grads)
    small = _unpack(_all_reduce_small("ar_small", packed, True), layout)
    loss = small["loss"][0, 0]
    six = {"fox_f_bias": L_FF, "gdn_a_log": L_GA, "gdn_dt_bias": L_GA}
    for n, rows_n in layout[:-1]:
        gsm = small[n]
        if n == "gdn_conv":
            gsm = lax.dynamic_slice(gsm.reshape(4, N_DEV * cshard), (0, me * cshard), (4, cshard))[None]
        elif n in six:
            gsm = gsm[:, six[n]:six[n] + 6]
        else:
            gsm = gsm.reshape(1, rows_n * HD)
        grads[n] = gsm

    names = ['norm_mix', 'w_in', 'fox_f_bias', 'fox_q_norm', 'fox_k_norm', 'gdn_conv', 'gdn_a_log', 'gdn_dt_bias',
             'gdn_out_norm', 'mem_norm', 'w_mem_kv', 'mem_q_norm', 'mem_k_norm', 'w_out', 'norm_ffn', 'w_gate_up', 'w_down']
    big = ("w_in", "w_mem_kv", "w_out", "w_gate_up", "w_down")
    delta, new_m, new_v = {}, {}, {}
    for n in big:
        res = updated[n][1:] if n in updated else _adamw_call("adamw_" + n, args[n][0], grads[n], *wmv(n)[1:])
        delta[n], new_m[n], new_v[n] = [a[None] for a in res]
        grads[n] = grads[n][None]

    def flat(a):
        a = a.reshape(1, -1)
        return jnp.pad(a, ((0, 0), (0, -a.shape[1] % HD))).reshape(-1, HD)

    smalls = [n for n in names if n not in big]
    pk = lambda pre: jnp.concatenate([flat(grads[n] if pre == "g" else args[pre + n]) for n in smalls], axis=0)
    cat = [pk(""), pk("g"), pk("m_"), pk("v_")]
    padr = -cat[0].shape[0] % 8
    cat = [jnp.pad(a, ((0, padr), (0, 0))) for a in cat]
    res = _adamw_call("adamw_small", *cat)
    at = 0
    for n in smalls:
        shape = args[n].shape
        size = math.prod(shape)
        nrow = -(-size // HD)
        for dst, src in zip((delta, new_m, new_v), res):
            dst[n] = src[at:at + nrow].reshape(-1)[:size].reshape(shape)
        at += nrow

    return (loss, grad_x[None], *[grads[n] for n in names], *[delta[n] for n in names],
            *[new_m[n] for n in names], *[new_v[n] for n in names])


class _StepComm:
    def __init__(self, first, shard_groups, w_gate_up, w_down, after):
        self.first = _RelayGather("ag_first", [first.astype(BF16)], after)
        self.groups, self.shards = {}, []
        for key, ws in shard_groups.items():
            self.groups[key] = list(range(len(self.shards), len(self.shards) + len(ws)))
            self.shards += [w.astype(BF16) for w in ws]
        self.w_gate_up, self.w_down = w_gate_up.astype(BF16), w_down.astype(BF16)
        self.passed, self.scatters = set(), {}

    def start_deps(self):
        return [self.first.token]

    def first_weights(self, after):
        deps = self.first.forward(after)
        self.gather = _Gather("ag", self.shards, deps)
        self.relay = _RelayGather("ag_gu", [self.w_gate_up], [self.gather.token])
        return self.first.get(self.first.pass_on([self.relay.token]))

    def relay_forward(self, after):
        deps = self.relay.forward(after)
        self.gather_down = _Gather("ag_dn", [self.w_down], deps)
        return [self.gather_down.token]

    def pass_on(self, key, after):
        self.passed.add(key)
        if key == "gate_up":
            return self.relay.pass_on(after)
        return self.gather.pass_on(self.groups[key], after)

    def weights(self, key, after):
        if key == "down":
            return self.gather_down.get([0], self.gather_down.pass_on([0], after))
        if key not in self.passed:
            after = self.pass_on(key, after)
        return self.relay.get(after) if key == "gate_up" else self.gather.get(self.groups[key], after)

    def send(self, tag, grads):
        blocks = [g if g.ndim == 3 else g.reshape(N_DEV, g.shape[0] // N_DEV, g.shape[1]) for g in grads]
        self.scatters[tag] = _Scatter("rs_" + tag, blocks, ())
        return [self.scatters[tag].token]

    def mid(self, tag, after):
        self.scatters[tag].mid(after)
        return [self.scatters[tag].token]

    def finish(self, after, wmv):
        return {tag: sc.end(after, wmv[tag]) for tag, sc in self.scatters.items()}


def _local_step(xs, ms, tgt, norm_mix, fox_f_bias, fox_q_norm, fox_k_norm, gdn_a_log, gdn_dt_bias, gdn_out_norm,
                mem_norm, mem_q_norm, mem_k_norm, norm_ffn, conv_all, comm):
    t, d = xs.shape
    bq = min(t, 256)
    fb, alog, dtb = _lanes(fox_f_bias, L_FF), _lanes(gdn_a_log, L_GA), _lanes(gdn_dt_bias, L_GA)
    flat = lambda w: w.reshape(-1, w.shape[-1])

    rms1 = lambda a, g: (_rms(a, g),)
    (u,) = _rowwise("norm_mix", rms1, [xs], [norm_mix], [(d, BF16)], min(t, 256), deps=comm.start_deps())
    w_in_b = flat(comm.first_weights([u])[0])
    pb = _matmul("proj_in_b", u, w_in_b, NN, F32, 1024, 768)
    o_fox = _fox_fwd(pb, fb, fox_q_norm, fox_k_norm, bq)
    w_in_a = flat(comm.weights("in_a", [o_fox])[0])
    pa = _matmul("proj_in_a", u, w_in_a, NN, F32, 1024, 768)
    smrow = (pb, HD, SM)
    (gates,) = _rowwise("gdn_gates", _gdn_gates, [smrow], [alog, dtb], [(HD, F32)], min(t, 256))
    gdn_terms, gdn_qkv = _gdn_fwd(pa, gates, conv_all)
    o_gdn_raw, gdn_states = _gdn_scan(gdn_terms)
    gdn_saved = list(gdn_terms) + [gdn_states]
    deps = comm.relay_forward([o_gdn_raw])
    zrow = (pa, NG * HD, GZ * HD // (NG * HD))
    (o_gdn,) = _rowwise("gdn_post", _gdn_post, [o_gdn_raw, zrow], [gdn_out_norm], [(NG * HD, BF16)], min(t, 256),
                        deps=deps)
    w_kv_all, w_out_all = [flat(w) for w in comm.weights("kv_out", [o_gdn])]
    (mem_n,) = _rowwise("norm_mem", rms1, [ms], [mem_norm], [(d, BF16)], ms.shape[0])
    mkv = _matmul("proj_mem", mem_n, w_kv_all, NN, F32, 256, 512)
    o_mem = _mem_fwd(pb, mkv, mem_q_norm, mem_k_norm)
    deps = comm.pass_on("gate_up", [o_mem])
    mix = jnp.concatenate([o_fox, o_gdn, o_mem], axis=1)
    h1, h1n = _proj_out_norm(mix, w_out_all, xs, norm_ffn, deps)
    (wgu,) = comm.weights("gate_up", [h1n])
    ffw = wgu.shape[2]
    gu, act = _ffn_up(h1n, wgu.reshape(2, 4, d, ffw))
    w_down_all = flat(comm.weights("down", [act])[0])
    dyb, lsum = _ffn_down_loss(act, w_down_all, h1, tgt)
    loss_local = (0.5 / d) * jnp.sum(lsum[::8, ::HD])

    dgu = _ffn_down_bwd(dyb, w_down_all.reshape(4, ffw, d), gu).reshape(8, t, ffw)
    g_w_down = _matmul("grad_w_down", act, dyb, TN, BF16, 512, 2048)
    g_w_gu = _ffn_up_bwd_w(h1n, dgu)
    deps = comm.send("ffn", [g_w_down, g_w_gu])
    rms2 = lambda a, g: (_rms(a, g), a)
    dh1b, g_norm_ffn = _ffn_up_bwd_x(dgu, wgu, h1, norm_ffn, dyb, deps)

    dmix = _matmul("proj_out_bwd_x", dh1b, w_out_all, NT, BF16, 1024, 1024)
    g_w_out = _matmul("grad_w_out", mix, dh1b, TN, BF16, 1024, 2048)
    deps = comm.mid("ffn", [dmix, g_w_out])
    dmq, dmk, dmv, g_mqn, g_mkn = _mem_bwd(pb, mkv, mem_q_norm, mem_k_norm, dmix, deps=deps)
    dmkv = jnp.concatenate([dmk, dmv], axis=1).astype(BF16)
    g_w_kv = _matmul("grad_w_kv", mem_n, dmkv, TN, BF16, 512, 512)
    do_raw, dgz, g_gon = _rowwise_vjp("gdn_post_bwd", _gdn_post, [o_gdn_raw, zrow], [gdn_out_norm],
                                      [(dmix, NG * HD, 1)], [F32, BF16], min(t, 256), deps=deps)
    dterms = _gdn_bwd_scan(gdn_saved, do_raw)
    dgq, dgk, dgv, dgates, dwq, dwk, dwv = _gdn_bwd(pa, gates, conv_all, dterms, gdn_qkv)
    dsm_gdn, g_alog, g_dtb = _rowwise_vjp("gdn_gates_bwd", _gdn_gates, [smrow], [alog, dtb], [dgates], [F32], min(t, 256))
    dp_a = jnp.concatenate([dgq, dgk, dgv, dgz], axis=1)
    g_w_in_a = _matmul("grad_w_in_a", u, dp_a, TN, BF16, 512, 3072)
    deps = comm.send("a", [g_w_in_a, g_w_out, g_w_kv])
    du_a = _matmul("proj_in_bwd_a", dp_a, w_in_a, NT, F32, 1024, 1024, deps=deps)
    deps = comm.mid("a", [du_a])
    dfq, dfk, dfv, dsm_fox, g_fb, g_fqn, g_fkn = _fox_bwd(pb, fb, fox_q_norm, fox_k_norm, dmix, 2 * bq if t % (2 * bq) == 0 else bq,
                                                          deps=deps)
    dp_b = jnp.concatenate([dfq, dfk, dfv, dmq, (dsm_fox + dsm_gdn).astype(BF16), jnp.zeros((t, HD), BF16)], axis=1)
    g_w_in_b = _matmul("grad_w_in_b", u, dp_b, TN, BF16, 512, 3072)
    deps = comm.send("b", [g_w_in_b])
    dmem_n = _matmul("proj_mem_bwd_x", dmkv, w_kv_all, NT, F32, 256, 512, deps=deps)
    g_mem_norm = _rowwise_vjp("norm_mem_bwd", rms1, [ms], [mem_norm], [dmem_n], [], ms.shape[0])[0]
    deps = comm.mid("b", [g_mem_norm])
    grad_x, g_norm_mix = _proj_in_bwd_norm(dp_b, w_in_b, du_a, xs, norm_mix, dh1b, deps)

    small_grads = {
        "norm_mix": g_norm_mix, "mem_norm": g_mem_norm, "norm_ffn": g_norm_ffn,
        "gdn_conv": jnp.concatenate([dwq, dwk, dwv], axis=1),
        "fox_q_norm": g_fqn, "fox_k_norm": g_fkn, "gdn_out_norm": g_gon, "mem_q_norm": g_mqn, "mem_k_norm": g_mkn,
        "fox_f_bias": g_fb, "gdn_a_log": g_alog, "gdn_dt_bias": g_dtb}
    return grad_x, loss_local, small_grads
```

```python
import functools
import math

import jax
import jax.numpy as jnp
from jax import lax
from jax.experimental import pallas as pl
from jax.experimental.pallas import tpu as pltpu

F32 = jnp.float32
BF16 = jnp.bfloat16
SDS = jax.ShapeDtypeStruct

N_DEV = 8
HD = 128
NF, NG, NM = 6, 6, 4
CHUNK = 64
GROUP = 16
NORM_EPS = 1e-6
GQ, GK, GV, GZ = 0, 6, 12, 18
FQ, FK, FV, MQ, SM = 0, 6, 12, 18, 22
HALF = 24 * HD
L_FF, L_GA, L_GB = 0, 6, 12
VMEM_LIMIT = 56 * 1024 * 1024

ADAM_LR, ADAM_B1, ADAM_B2, ADAM_EPS, ADAM_WD, ADAM_STEP = 0.001, 0.9, 0.999, 1e-08, 0.01, 10

NN = (((1,), (0,)), ((), ()))
NT = (((1,), (1,)), ((), ()))
TN = (((0,), (0,)), ((), ()))
MESH = pl.DeviceIdType.MESH


def _cp(*sem):
    return pltpu.CompilerParams(dimension_semantics=tuple(sem) if sem else None, vmem_limit_bytes=VMEM_LIMIT)


def _dot(a, b, dims=NN):
    return lax.dot_general(a, b, dims, preferred_element_type=F32)


def _iota(shape, axis):
    return lax.broadcasted_iota(jnp.int32, shape, axis)


def _rms(x, gain):
    return x * lax.rsqrt(jnp.mean(x * x, axis=-1, keepdims=True) + NORM_EPS) * gain


def _sigmoid(x):
    return 0.5 * jnp.tanh(0.5 * x) + 0.5


def _silu(x):
    return x * _sigmoid(x)


def _softplus(x):
    return jnp.maximum(x, 0.0) + jnp.log(1.0 + jnp.exp(-jnp.abs(x)))


def _lane_pick(x, lane):
    oh = (_iota((1, x.shape[-1]), 1) == lane).astype(F32)
    return jnp.sum(x * oh, axis=-1, keepdims=True)


def _cumsum_rows(x):
    tril = (_iota((HD, HD), 0) >= _iota((HD, HD), 1)).astype(F32)
    carry = jnp.zeros((1, x.shape[1]), F32)
    outs = []
    for b in range(x.shape[0] // HD):
        blk = x[b * HD:(b + 1) * HD]
        outs.append(_pdot(tril, blk, "nn", "xa") + carry)
        carry = carry + jnp.sum(blk, axis=0, keepdims=True)
    return jnp.concatenate(outs, axis=0)


def _row_spec(r, tm):
    if isinstance(r, tuple):
        arr, width, cb = r
        return arr, pl.BlockSpec((tm, width), lambda i, cb=cb: (i, cb))
    return r, pl.BlockSpec((tm, r.shape[1]), lambda i: (i, 0))


ANY_SPEC = pl.BlockSpec(memory_space=pl.ANY)


def _rowwise(name, fn, rows, consts, outs, tm, deps=()):
    arrs, specs = zip(*[_row_spec(r, tm) for r in rows])
    n_rows = arrs[0].shape[0]
    nr, nc, nd = len(rows), len(consts), len(deps)

    def body(*refs):
        res = fn(*[r[...] for r in refs[:nr + nc]])
        for o, v in zip(refs[nr + nc + nd:], res):
            o[...] = v.astype(o.dtype)

    return pl.pallas_call(
        body, grid=(n_rows // tm,), name=name,
        in_specs=list(specs) + [pl.BlockSpec(c.shape, lambda i: (0, 0)) for c in consts] + [ANY_SPEC] * nd,
        out_specs=[pl.BlockSpec((tm, w), lambda i: (i, 0)) for w, _ in outs],
        out_shape=[SDS((n_rows, w), dt) for w, dt in outs],
        compiler_params=_cp("parallel"),
    )(*arrs, *consts, *deps)


def _rowwise_vjp(name, fn, rows, consts, cts, grad_dtypes, tm, deps=()):
    arrs, specs = zip(*[_row_spec(r, tm) for r in rows])
    ct_arrs, ct_specs = zip(*[_row_spec(r, tm) for r in cts])
    n_rows = arrs[0].shape[0]
    nr, nc, nct, nd = len(rows), len(consts), len(cts), len(deps)
    plan = [(j, dt) for j, dts in enumerate(grad_dtypes) for dt in (dts if isinstance(dts, tuple) else (dts,))]
    ng = len(plan)
    widths = [specs[j].block_shape[1] for j, _ in plan]
    grad_dtypes = [dt for _, dt in plan]

    def body(*refs):
        vals = [r[...].astype(F32) for r in refs[:nr + nc]]
        ctv = tuple(r[...].astype(F32) for r in refs[nr + nc:nr + nc + nct])
        _, vjp = jax.vjp(fn, *vals)
        grads = vjp(ctv)
        outs = refs[nr + nc + nct + nd:]
        for o, (j, _) in zip(outs[:ng], plan):
            o[...] = grads[j].astype(o.dtype)

        @pl.when(pl.program_id(0) == 0)
        def _():
            for o in outs[ng:]:
                o[...] = jnp.zeros_like(o)

        for o, g in zip(outs[ng:], grads[nr:]):
            o[...] += g

    return pl.pallas_call(
        body, grid=(n_rows // tm,), name=name,
        in_specs=list(specs) + [pl.BlockSpec(c.shape, lambda i: (0, 0)) for c in consts] + list(ct_specs)
        + [ANY_SPEC] * nd,
        out_specs=[pl.BlockSpec((tm, w), lambda i: (i, 0)) for w in widths]
        + [pl.BlockSpec(c.shape, lambda i: (0, 0)) for c in consts],
        out_shape=[SDS((n_rows, w), dt) for w, dt in zip(widths, grad_dtypes)] + [SDS(c.shape, F32) for c in consts],
        compiler_params=_cp("arbitrary"),
    )(*arrs, *consts, *ct_arrs, *deps)


def _tile(n, pref):
    t = min(n, pref)
    while n % t or (t % HD and t != n):
        t -= 1
    return t


def _matmul(name, a, b, dims, out_dtype, tm, tn, residual=None, deps=()):
    ta, tb = dims == TN, dims == NT
    m = a.shape[1] if ta else a.shape[0]
    k = a.shape[0] if ta else a.shape[1]
    n = b.shape[0] if tb else b.shape[1]
    tm, tn = _tile(m, tm), _tile(n, tn)

    def body(*refs):
        acc = _dot(refs[0][...], refs[1][...], dims)
        if residual is not None:
            acc = acc + refs[2][...]
        refs[-1][...] = acc.astype(out_dtype)

    in_specs = [pl.BlockSpec((k, tm), lambda i, j: (0, i)) if ta else pl.BlockSpec((tm, k), lambda i, j: (i, 0)),
                pl.BlockSpec((tn, k), lambda i, j: (j, 0)) if tb else pl.BlockSpec((k, tn), lambda i, j: (0, j))]
    ops = [a, b]
    if residual is not None:
        in_specs.append(pl.BlockSpec((tm, tn), lambda i, j: (i, j)))
        ops.append(residual)
    in_specs += [ANY_SPEC] * len(deps)
    ops += list(deps)
    return pl.pallas_call(
        body, grid=(m // tm, n // tn), name=name, in_specs=in_specs,
        out_specs=pl.BlockSpec((tm, tn), lambda i, j: (i, j)), out_shape=SDS((m, n), out_dtype),
        compiler_params=_cp("parallel", "parallel"),
    )(*ops)


def _proj_out_norm(mix, w_out, xs, gain, deps):
    t, k = mix.shape
    d = w_out.shape[1]
    tm = _tile(t, 512)

    def body(*refs):
        a, b, x, g = refs[:4]
        h1, h1n = refs[4 + len(deps):]
        acc = _dot(a[...], b[...]) + x[...]
        h1[...] = acc
        h1n[...] = _rms(acc, g[...]).astype(BF16)

    return pl.pallas_call(
        body, grid=(t // tm,), name="proj_out",
        in_specs=[pl.BlockSpec((tm, k), lambda i: (i, 0)), pl.BlockSpec((k, d), lambda i: (0, 0)),
                  pl.BlockSpec((tm, d), lambda i: (i, 0)), pl.BlockSpec((1, d), lambda i: (0, 0))] + [ANY_SPEC] * len(deps),
        out_specs=[pl.BlockSpec((tm, d), lambda i: (i, 0))] * 2, out_shape=[SDS((t, d), F32), SDS((t, d), BF16)],
        compiler_params=_cp("parallel"),
    )(mix, w_out, xs, gain, *deps)


def _proj_in_bwd_norm(dp, w, du_a, xs, gain, dh1b, deps):
    t, k = dp.shape
    d = w.shape[0]
    tm = _tile(t, 256)

    def body(*refs):
        a, b, ua, x, g, dh = refs[:6]
        gx, dgain = refs[6 + len(deps):]
        _, vjp = jax.vjp(lambda xx, gn: _rms(xx, gn), x[...], g[...])
        dx, dg = vjp(_dot(a[...], b[...], NT) + ua[...])
        gx[...] = dx + dh[...].astype(F32)

        @pl.when(pl.program_id(0) == 0)
        def _():
            dgain[...] = jnp.zeros_like(dgain)

        dgain[...] += dg

    row = pl.BlockSpec((tm, d), lambda i: (i, 0))
    vec = pl.BlockSpec((1, d), lambda i: (0, 0))
    return pl.pallas_call(
        body, grid=(t // tm,), name="proj_in_bwd_b",
        in_specs=[pl.BlockSpec((tm, k), lambda i: (i, 0)), pl.BlockSpec((d, k), lambda i: (0, 0), pipeline_mode=ONE_BUFFER),
                  row, row, vec, row] + [ANY_SPEC] * len(deps),
        out_specs=[row, vec], out_shape=[SDS((t, d), F32), SDS((1, d), F32)], compiler_params=_cp("arbitrary"),
    )(dp, w, du_a, xs, gain, dh1b, *deps)


def _ffn_up(h1n, wgu):
    t, d = h1n.shape
    w = wgu.shape[3]
    tm = _tile(t, 512)

    def body(a, b, gu, act):
        x = a[...]
        g = _dot(x, b[0])
        u = _dot(x, b[1])
        gu[0] = g.astype(BF16)
        gu[1] = u.astype(BF16)
        act[...] = (_silu(g) * u).astype(BF16)

    return pl.pallas_call(
        body, grid=(4, t // tm), name="ffn_up",
        in_specs=[pl.BlockSpec((tm, d), lambda j, i: (i, 0)), pl.BlockSpec((2, None, d, w), lambda j, i: (0, j, 0, 0))],
        out_specs=[pl.BlockSpec((2, None, tm, w), lambda j, i: (0, j, i, 0)), pl.BlockSpec((tm, w), lambda j, i: (i, j))],
        out_shape=[SDS((2, 4, t, w), BF16), SDS((t, 4 * w), BF16)],
        compiler_params=_cp("parallel", "parallel"),
    )(h1n, wgu)


def _ffn_down_loss(act, wdown, h1, target):
    t, f = act.shape
    d = wdown.shape[1]
    tm, tn = _tile(t, 1024), _tile(d, 512)

    def body(a, b, h, tg, dyb, ls):
        e = _dot(a[...], b[...]) + h[...] - tg[...]
        dyb[...] = (e * (1.0 / d)).astype(BF16)
        ls[...] = jnp.broadcast_to(jnp.sum(e * e), (8, HD))

    return pl.pallas_call(
        body, grid=(t // tm, d // tn), name="ffn_down_loss",
        in_specs=[pl.BlockSpec((tm, f), lambda i, j: (i, 0)), pl.BlockSpec((f, tn), lambda i, j: (0, j)),
                  pl.BlockSpec((tm, tn), lambda i, j: (i, j)), pl.BlockSpec((tm, tn), lambda i, j: (i, j))],
        out_specs=[pl.BlockSpec((tm, tn), lambda i, j: (i, j)), pl.BlockSpec((8, HD), lambda i, j: (i, j))],
        out_shape=[SDS((t, d), BF16), SDS((8 * (t // tm), HD * (d // tn)), F32)],
        compiler_params=_cp("parallel", "parallel"),
    )(act, wdown, h1, target)


def _ffn_down_bwd(dyb, wdown4, gu):
    t, d = dyb.shape
    w = wdown4.shape[1]
    tm = _tile(t, 512)

    def body(a, b, gu_ref, out):
        da = _dot(a[...], b[...], NT)
        g = gu_ref[0].astype(F32)
        u = gu_ref[1].astype(F32)
        s = _sigmoid(g)
        out[0] = (da * u * (s * (1.0 + g * (1.0 - s)))).astype(BF16)
        out[1] = (da * g * s).astype(BF16)

    return pl.pallas_call(
        body, grid=(4, t // tm), name="ffn_down_bwd",
        in_specs=[pl.BlockSpec((tm, d), lambda j, i: (i, 0)), pl.BlockSpec((None, w, d), lambda j, i: (j, 0, 0)),
                  pl.BlockSpec((2, None, tm, w), lambda j, i: (0, j, i, 0))],
        out_specs=pl.BlockSpec((2, None, tm, w), lambda j, i: (0, j, i, 0)),
        out_shape=SDS((2, 4, t, w), BF16),
        compiler_params=_cp("parallel", "parallel"),
    )(dyb, wdown4, gu)


def _ffn_up_bwd_x(dgu, wgu, h1, gain, dyb, deps):
    _, t, w = dgu.shape
    d = wgu.shape[1]
    tm = _tile(t, 512)

    def body(*refs):
        a, b, h, g, dy = refs[:5]
        dh1, dgain, acc = refs[5 + len(deps):]
        i, j = pl.program_id(0), pl.program_id(1)

        @pl.when(j == 0)
        def _():
            acc[...] = jnp.zeros_like(acc)

        acc[...] += _dot(a[...], b[...], NT)

        @pl.when(j == N_DEV - 1)
        def _():
            _, vjp = jax.vjp(lambda x, gn: _rms(x, gn), h[...], g[...])
            dx, dg = vjp(acc[...])
            dh1[...] = (dx + dy[...].astype(F32)).astype(dh1.dtype)

            @pl.when(i == 0)
            def _():
                dgain[...] = jnp.zeros_like(dgain)

            dgain[...] += dg

    row = pl.BlockSpec((tm, d), lambda i, j: (i, 0))
    return pl.pallas_call(
        body, grid=(t // tm, N_DEV), name="ffn_up_bwd_x",
        in_specs=[pl.BlockSpec((None, tm, w), lambda i, j: (j, i, 0)), pl.BlockSpec((None, d, w), lambda i, j: (j, 0, 0)),
                  row, pl.BlockSpec((1, d), lambda i, j: (0, 0)), row] + [ANY_SPEC] * len(deps),
        out_specs=[row, pl.BlockSpec((1, d), lambda i, j: (0, 0))],
        out_shape=[SDS((t, d), BF16), SDS((1, d), F32)], scratch_shapes=[pltpu.VMEM((tm, d), F32)],
        compiler_params=_cp("arbitrary", "arbitrary"),
    )(dgu, wgu, h1, gain, dyb, *deps)


def _ffn_up_bwd_w(h1n, dgu):
    _, t, w = dgu.shape
    d = h1n.shape[1]
    tm = _tile(d, 512)

    def body(a, b, out):
        out[...] = _dot(a[...], b[...], TN).astype(BF16)

    return pl.pallas_call(
        body, grid=(8, d // tm), name="ffn_up_bwd_w",
        in_specs=[pl.BlockSpec((t, tm), lambda j, i: (0, i)), pl.BlockSpec((None, t, w), lambda j, i: (j, 0, 0))],
        out_specs=pl.BlockSpec((None, tm, w), lambda j, i: (j, i, 0)), out_shape=SDS((8, d, w), BF16),
        compiler_params=_cp("parallel", "parallel"),
    )(h1n, dgu)


def _fox_prep(fq, fk, sm, fb, qg, kg, h):
    qn = _rms(fq, qg)
    kn = _rms(fk, kg)
    c = _cumsum_rows(-_softplus(-(sm + fb)))
    ccol = _lane_pick(c, L_FF + h)
    crow = jnp.sum(c.T * (_iota((HD, 1), 0) == L_FF + h).astype(F32), axis=0, keepdims=True)
    return qn, kn, ccol, crow


def _softmax_times(s, v):
    e = jnp.exp(s - lax.stop_gradient(jnp.max(s, axis=1, keepdims=True)))
    return _dot(e.astype(BF16), v.astype(BF16)) * (1.0 / jnp.sum(e, axis=1, keepdims=True))


def _fox_block(q, k, v, cc, cr, off):
    bq = q.shape[0]
    assert k.shape[0] == off + bq
    s = _dot((q * (HD ** -0.5)).astype(BF16), k.astype(BF16), NT) + cc - cr
    diag = jnp.where(_iota((bq, bq), 1) <= _iota((bq, bq), 0), s[:, off:], -1e30)
    s = jnp.concatenate([s[:, :off], diag], axis=1) if off else diag
    return _softmax_times(s, v)


ONE_BUFFER = pl.Buffered(1)


def _pcol(t, cb):
    return pl.BlockSpec((t, HD), lambda h, cb=cb: (0, cb + h), pipeline_mode=ONE_BUFFER)


def _smcol(t):
    return pl.BlockSpec((t, HD), lambda h: (0, SM), pipeline_mode=ONE_BUFFER)


def _head(t):
    return pl.BlockSpec((t, HD), lambda h: (0, h), pipeline_mode=ONE_BUFFER)


def _small(n):
    return pl.BlockSpec((n, HD), lambda h: (0, 0), pipeline_mode=ONE_BUFFER)


def _fox_fwd(p, fb, qg, kg, bq):
    t = p.shape[0]

    def body(fq, fk, fv, sm, fb_r, qg_r, kg_r, o, qn_s, cc_s):
        h = pl.program_id(0)
        qn, kn, ccol, crow = _fox_prep(fq[...], fk[...], sm[...], fb_r[...], qg_r[...], kg_r[...], h)
        qn_s[...] = qn
        cc_s[...] = ccol
        knb = kn.astype(BF16)
        vb = fv[...].astype(BF16)
        for i in range(t // bq):
            rows, ext = pl.ds(i * bq, bq), (i + 1) * bq
            o[rows, :] = _fox_block(qn_s[rows, :], knb[:ext], vb[:ext], cc_s[rows, :], crow[:, :ext], i * bq).astype(o.dtype)

    return pl.pallas_call(
        body, grid=(NF,), name="fox_fwd",
        in_specs=[_pcol(t, FQ), _pcol(t, FK), _pcol(t, FV), _smcol(t), _small(1), _small(1), _small(1)],
        out_specs=_head(t), out_shape=SDS((t, NF * HD), BF16),
        scratch_shapes=[pltpu.VMEM((t, HD), F32), pltpu.VMEM((t, 1), F32)],
        compiler_params=_cp("parallel"),
    )(p, p, p, p, fb, qg, kg)


def _fox_bwd(p, fb, qg, kg, dmix, bq, deps=()):
    t = p.shape[0]

    def body(*refs):
        fq, fk, fv, sm, fb_r, qg_r, kg_r, do = refs[:8]
        dfq, dfk, dfv, dsm, dfb, dqg, dkg, qn_s, cc_s, dqn_s, dcc_s, dkn_s, dv_s, dcr_s = refs[8 + len(deps):]
        h = pl.program_id(0)
        qn, kn, ccol, crow = _fox_prep(fq[...], fk[...], sm[...], fb_r[...], qg_r[...], kg_r[...], h)
        qn_s[...] = qn
        cc_s[...] = ccol
        v = fv[...]
        dkn_s[...] = jnp.zeros_like(dkn_s)
        dv_s[...] = jnp.zeros_like(dv_s)
        dcr_s[...] = jnp.zeros_like(dcr_s)

        for i in range(t // bq):
            rows, ext = pl.ds(i * bq, bq), (i + 1) * bq
            _, vjp = jax.vjp(lambda a, b, c, d, e, off=i * bq: _fox_block(a, b, c, d, e, off),
                             qn_s[rows, :], kn[:ext], v[:ext], cc_s[rows, :], crow[:, :ext])
            dq, dk, dv, dcc, dcr = vjp(do[rows, :].astype(F32))
            dqn_s[rows, :] = dq
            dcc_s[rows, :] = dcc
            dkn_s[:ext, :] += dk
            dv_s[:ext, :] += dv
            dcr_s[:, :ext] += dcr
        _, prep_vjp = jax.vjp(lambda a, b, c, d, e, f: _fox_prep(a, b, c, d, e, f, h),
                              fq[...], fk[...], sm[...], fb_r[...], qg_r[...], kg_r[...])
        g_fq, g_fk, g_sm, g_fb, g_qg, g_kg = prep_vjp((dqn_s[...], dkn_s[...], dcc_s[...], dcr_s[...]))
        dfq[...] = g_fq.astype(dfq.dtype)
        dfk[...] = g_fk.astype(dfk.dtype)
        dfv[...] = dv_s[...].astype(dfv.dtype)

        @pl.when(h == 0)
        def _():
            for r in (dsm, dfb, dqg, dkg):
                r[...] = jnp.zeros_like(r)

        dsm[...] += g_sm
        dfb[...] += g_fb
        dqg[...] += g_qg
        dkg[...] += g_kg

    head = _head(t)
    return pl.pallas_call(
        body, grid=(NF,), name="fox_bwd",
        in_specs=[_pcol(t, FQ), _pcol(t, FK), _pcol(t, FV), _smcol(t), _small(1), _small(1), _small(1), head]
        + [ANY_SPEC] * len(deps),
        out_specs=[head, head, head, _small(t), _small(1), _small(1), _small(1)],
        out_shape=[SDS((t, NF * HD), BF16)] * 3 + [SDS((t, HD), F32)] + [SDS((1, HD), F32)] * 3,
        scratch_shapes=[pltpu.VMEM((t, HD), F32), pltpu.VMEM((t, 1), F32), pltpu.VMEM((t, HD), F32),
                        pltpu.VMEM((t, 1), F32), pltpu.VMEM((t, HD), F32), pltpu.VMEM((t, HD), F32),
                        pltpu.VMEM((1, t), F32)],
        compiler_params=_cp("arbitrary"),
    )(p, p, p, p, fb, qg, kg, dmix, *deps)


def _mem_attn(mq, mk, mv, qg, kg):
    s = _dot((_rms(mq, qg) * (HD ** -0.5)).astype(BF16), _rms(mk, kg).astype(BF16), NT)
    return _softmax_times(s, mv)


def _mem_fwd(p, mkv, qg, kg):
    t, ml = p.shape[0], mkv.shape[0]

    def body(mq, mk, mv, qg_r, kg_r, o):
        o[...] = _mem_attn(mq[...], mk[...], mv[...], qg_r[...], kg_r[...]).astype(o.dtype)

    return pl.pallas_call(
        body, grid=(NM,), name="mem_fwd",
        in_specs=[_pcol(t, MQ), pl.BlockSpec((ml, HD), lambda h: (0, h)), pl.BlockSpec((ml, HD), lambda h: (0, NM + h)),
                  _small(1), _small(1)],
        out_specs=pl.BlockSpec((t, HD), lambda h: (0, h)), out_shape=SDS((t, NM * HD), BF16),
        compiler_params=_cp("parallel"),
    )(p, mkv, mkv, qg, kg)


def _mem_bwd(p, mkv, qg, kg, dmix, deps=()):
    t, ml = p.shape[0], mkv.shape[0]

    def body(*refs):
        mq, mk, mv, qg_r, kg_r, do = refs[:6]
        dmq, dmk, dmv, dqg, dkg = refs[6 + len(deps):]
        _, vjp = jax.vjp(_mem_attn, mq[...], mk[...], mv[...], qg_r[...], kg_r[...])
        g_q, g_k, g_v, g_qg, g_kg = vjp(do[...].astype(F32))
        dmq[...] = g_q.astype(dmq.dtype)
        dmk[...] = g_k
        dmv[...] = g_v

        @pl.when(pl.program_id(0) == 0)
        def _():
            dqg[...] = jnp.zeros_like(dqg)
            dkg[...] = jnp.zeros_like(dkg)

        dqg[...] += g_qg
        dkg[...] += g_kg

    return pl.pallas_call(
        body, grid=(NM,), name="mem_bwd",
        in_specs=[_pcol(t, MQ), pl.BlockSpec((ml, HD), lambda h: (0, h)), pl.BlockSpec((ml, HD), lambda h: (0, NM + h)),
                  _small(1), _small(1), pl.BlockSpec((t, HD), lambda h: (0, NF + NG + h))] + [ANY_SPEC] * len(deps),
        out_specs=[pl.BlockSpec((t, HD), lambda h: (0, h)), pl.BlockSpec((ml, HD), lambda h: (0, h)),
                   pl.BlockSpec((ml, HD), lambda h: (0, h)), _small(1), _small(1)],
        out_shape=[SDS((t, NM * HD), BF16), SDS((ml, NM * HD), F32), SDS((ml, NM * HD), F32),
                   SDS((1, HD), F32), SDS((1, HD), F32)],
        compiler_params=_cp("arbitrary"),
    )(p, mkv, mkv, qg, kg, dmix, *deps)


def _shift_down(x, s):
    if s == 0:
        return x
    return jnp.where(_iota(x.shape, 0) >= s, pltpu.roll(x, s, 0), 0.0)


def _shift_up(x, s):
    if s == 0:
        return x
    n = x.shape[0]
    return jnp.where(_iota(x.shape, 0) < n - s, pltpu.roll(x, n - s, 0), 0.0)


@jax.custom_vjp
def _conv4(x, w0, w1, w2, w3):
    return w0 * _shift_down(x, 3) + w1 * _shift_down(x, 2) + w2 * _shift_down(x, 1) + w3 * x


def _conv4_fwd(x, w0, w1, w2, w3):
    return _conv4(x, w0, w1, w2, w3), (x, w0, w1, w2, w3)


def _conv4_bwd(res, dy):
    x, w0, w1, w2, w3 = res
    ups = [_shift_up(dy, 3 - k) for k in range(4)]
    dx = w0 * ups[0] + w1 * ups[1] + w2 * ups[2] + w3 * ups[3]
    return (dx,) + tuple(jnp.sum(up * x, axis=0, keepdims=True) for up in ups)


_conv4.defvjp(_conv4_fwd, _conv4_bwd)


HALO = 8


def _gdn_gates(sm, alog, dtb):
    lane = _iota((1, HD), 1)
    g = -jnp.exp(alog) * _softplus(sm + dtb)
    return (jnp.where((lane >= L_GA) & (lane < L_GA + NG), g,
                      jnp.where((lane >= L_GB) & (lane < L_GB + NG), _sigmoid(sm), 0.0)),)


def _gdn_prep(gq, gk, gv, gates, taps, h):
    q, k, v = [_silu(_conv4(x, *taps[4 * j:4 * j + 4]))[HALO:] for j, x in enumerate((gq, gk, gv))]
    q = q * lax.rsqrt(jnp.sum(q * q, axis=-1, keepdims=True) + NORM_EPS) * (HD ** -0.5)
    k = k * lax.rsqrt(jnp.sum(k * k, axis=-1, keepdims=True) + NORM_EPS)
    return q, k, v, _lane_pick(gates, L_GA + h), _lane_pick(gates, L_GB + h)


def _split(x, n):
    parts, rest = [], x
    for i in range(n):
        parts.append(rest.astype(BF16))
        if i + 1 < n:
            rest = rest - parts[-1].astype(F32)
    return parts


def _raw_dot(a, b, form):
    lead = a.ndim - 2
    ca, cb = {"nn": (1, 0), "nt": (1, 1), "tn": (0, 0)}[form]
    batch = ((0,), (0,)) if lead else ((), ())
    return lax.dot_general(a, b, (((ca + lead,), (cb + lead,)), batch), preferred_element_type=F32)


def _pdot_impl(a, b, form, mode):
    if mode == "1":
        return _raw_dot(a.astype(BF16), b.astype(BF16), form)
    if mode == "3":
        (ah, al), (bh, bl) = _split(a, 2), _split(b, 2)
        return _raw_dot(ah, bh, form) + (_raw_dot(al, bh, form) + _raw_dot(ah, bl, form))
    if mode == "xa":
        return sum(_raw_dot(a.astype(BF16), t, form) for t in reversed(_split(b, 3)))
    return sum(_raw_dot(t, b.astype(BF16), form) for t in reversed(_split(a, 3)))


@functools.partial(jax.custom_vjp, nondiff_argnums=(2, 3))
def _pdot(a, b, form, mode):
    return _pdot_impl(a, b, form, mode)


def _pdot_fwd(a, b, form, mode):
    return _pdot_impl(a, b, form, mode), (a, b)


def _pdot_bwd(form, mode, res, ct):
    a, b = res
    da_args, db_args = {"nn": ((ct, b, "nt"), (a, ct, "tn")), "nt": ((ct, b, "nn"), (ct, a, "tn")),
                        "tn": ((b, ct, "nt"), (a, ct, "nn"))}[form]

    def side(args, exact):
        if mode in ("1", "3"):
            return mode
        return "xa" if args[0] is exact else "xb"

    if mode == "xa":
        return jnp.zeros_like(a), _pdot_impl(*db_args, side(db_args, a))
    if mode == "xb":
        return _pdot_impl(*da_args, side(da_args, b)), jnp.zeros_like(b)
    return _pdot_impl(*da_args, mode), _pdot_impl(*db_args, mode)


_pdot.defvjp(_pdot_fwd, _pdot_bwd)

GDN_QK, GDN_INV, GDN_SCAN = "1", "1", "1"


@jax.custom_vjp
def _tri_inv(low):
    eye = (_iota((CHUNK, CHUNK), 0) == _iota((CHUNK, CHUNK), 1)).astype(F32)
    inv = eye - low
    pw = low
    for _ in range(5):
        pw = _pdot_impl(pw, pw, "nn", GDN_INV)
        inv = inv + _pdot_impl(inv, pw, "nn", GDN_INV)
    return inv


def _tri_inv_fwd(low):
    inv = _tri_inv(low)
    return inv, inv


def _tri_inv_bwd(inv, ct):
    return (-_pdot_impl(_pdot_impl(inv, ct, "tn", GDN_INV), inv, "nt", GDN_INV),)


_tri_inv.defvjp(_tri_inv_fwd, _tri_inv_bwd)


def _gdn_intra(q, k, v, g, beta):
    n = q.shape[0]
    r, c = _iota((CHUNK, CHUNK), 0), _iota((CHUNK, CHUNK), 1)
    tril, strict = r >= c, r > c
    trilf = jnp.broadcast_to(tril.astype(F32), (n, CHUNK, CHUNK))
    gcm = _pdot(trilf, jnp.broadcast_to(g, (n, CHUNK, CHUNK)), "nn", "xa")
    gcf = _pdot(trilf, jnp.broadcast_to(g, (n, CHUNK, HD)), "nn", "xa")
    lane0 = (_iota((1, 1, CHUNK), 2) == 0).astype(F32)
    gcr = _pdot(jnp.ones((n, CHUNK, CHUNK), F32), gcm * lane0, "nt", "xa")
    decay = jnp.where(tril, jnp.exp(jnp.where(tril, gcm - gcr, 0.0)), 0.0)
    egc = jnp.exp(gcf)
    kb = k * beta
    low = jnp.where(strict, _pdot(kb, k, "nt", GDN_QK) * decay, 0.0)
    inv = _tri_inv(low)
    u = _pdot(inv, v * beta, "nn", GDN_INV)
    w = _pdot(inv, kb * egc, "nn", GDN_INV)
    at = jnp.where(tril, _pdot(q, k, "nt", GDN_QK) * decay, 0.0)
    gl = jnp.sum(jnp.broadcast_to(g, (n, CHUNK, HD)), axis=1, keepdims=True)
    kd = k * jnp.exp(gl - gcf)
    return (_pdot(kd, w, "tn", GDN_SCAN), _pdot(kd, u, "tn", GDN_SCAN), q * egc - _pdot(at, w, "nn", GDN_SCAN),
            _pdot(at, u, "nn", GDN_SCAN), gl)


def _gdn_step(s, kw, ku, a, b, gl):
    return _pdot(a, s, "nn", GDN_SCAN) + b, s * jnp.exp(gl) - _pdot(kw, s, "nn", GDN_SCAN) + ku


SCAN_HEADS = 3
SCAN_UNROLL = 4


def _gdn_chunked_scratch(nc):
    big = pltpu.VMEM((nc, CHUNK, HD), F32)
    return [big, big, big, pltpu.VMEM((nc, CHUNK, 1), F32), pltpu.VMEM((nc, CHUNK, 1), F32)]


N_TERMS = 5


def _gdn_term_shapes(nc):
    return [(nc, HD, HD), (nc, HD, HD), (nc, CHUNK, HD), (nc, CHUNK, HD), (nc, 1, HD)]


def _per_head(shape, heads=None, one_buffer=True):
    lead = (None,) if heads is None else (heads,)
    return pl.BlockSpec(lead + tuple(shape), lambda h: (h,) + (0,) * len(shape),
                        pipeline_mode=ONE_BUFFER if one_buffer else None)


def _gdn_in_specs(t):
    cw = lambda cb: pl.BlockSpec((4, HD), lambda h, cb=cb: (0, cb + h))
    return [_pcol(t, GQ), _pcol(t, GK), _pcol(t, GV), _small(t), cw(0), cw(NG), cw(2 * NG)]


def _taps(wq, wk, wv):
    return tuple(w[k:k + 1, :] for w in (wq, wk, wv) for k in range(4))


def _prep_rows(t):
    return min(t, 256)


def _gdn_pad(srcs, pads):
    for src, pad in zip(srcs, pads):
        pad[0:HALO, :] = jnp.zeros((HALO, HD), F32)
        pad[HALO:, :] = src[...]


def _gdn_stage(pads, gates, taps, h, chunked):
    t = gates.shape[0]
    rows = _prep_rows(t)
    per = rows // CHUNK

    def tile(i, carry):
        r0 = pl.multiple_of(i * rows, rows)
        vals = _gdn_prep(*[p[pl.ds(r0, rows + HALO), :] for p in pads], gates[pl.ds(r0, rows), :], taps, h)
        for v, r in zip(vals, chunked):
            r[pl.ds(i * per, per)] = v.reshape(per, CHUNK, v.shape[-1])
        return carry

    lax.fori_loop(0, t // rows, tile, 0)


def _gdn_intra_all(chunked, intra):
    nc = chunked[0].shape[0]
    grp_n = math.gcd(nc, GROUP)

    def grp(i, carry):
        sl = pl.ds(pl.multiple_of(i * grp_n, grp_n), grp_n)
        for r, val in zip(intra, _gdn_intra(*[c[sl] for c in chunked])):
            r[sl] = val
        return carry

    lax.fori_loop(0, nc // grp_n, grp, 0)


def _gdn_fwd(pa, gates, conv):
    t = pa.shape[0]
    nc = t // CHUNK
    terms = _gdn_term_shapes(nc)

    def body(gq, gk, gv, gt, wq, wk, wv, *rest):
        h = pl.program_id(0)
        intra, chunked, pads = rest[:N_TERMS], rest[N_TERMS:N_TERMS + 5], rest[N_TERMS + 5:]
        _gdn_pad((gq, gk, gv), pads)
        _gdn_stage(pads, gt, _taps(wq, wk, wv), h, chunked)
        _gdn_intra_all(chunked, intra)

    qkv = [(nc, CHUNK, HD)] * 3
    outs = pl.pallas_call(
        body, grid=(NG,), name="gdn_fwd", in_specs=_gdn_in_specs(t),
        out_specs=[_per_head(sh, one_buffer=False) for sh in terms + qkv],
        out_shape=[SDS((NG,) + sh, F32) for sh in terms + qkv],
        scratch_shapes=_gdn_chunked_scratch(nc)[3:] + [pltpu.VMEM((t + HALO, HD), F32)] * 3, compiler_params=_cp("parallel"),
    )(pa, pa, pa, gates, conv, conv, conv)
    return list(outs[:N_TERMS]), list(outs[N_TERMS:])


def _gdn_scan(terms_in):
    nc = terms_in[0].shape[1]
    terms = _gdn_term_shapes(nc)

    def body(*refs):
        intra, o, states = refs[:N_TERMS], refs[N_TERMS], refs[N_TERMS + 1]

        def one(c, ss):
            rows = pl.ds(pl.multiple_of(c * CHUNK, CHUNK), CHUNK)
            loaded = [[r[hh, c] for r in intra] for hh in range(SCAN_HEADS)]
            res = [_gdn_step(ss[hh], *loaded[hh]) for hh in range(SCAN_HEADS)]
            for hh in range(SCAN_HEADS):
                states[hh, c] = ss[hh]
                o[rows, hh * HD:(hh + 1) * HD] = res[hh][0]
            return tuple(r[1] for r in res)

        per_trip = math.gcd(nc, SCAN_UNROLL)

        def step(i, ss):
            for k in range(per_trip):
                ss = one(per_trip * i + k, ss)
            return ss

        lax.fori_loop(0, nc // per_trip, step, tuple(jnp.zeros((HD, HD), F32) for _ in range(SCAN_HEADS)))

    return pl.pallas_call(
        body, grid=(NG // SCAN_HEADS,), name="gdn_scan", in_specs=[_per_head(sh, SCAN_HEADS) for sh in terms],
        out_specs=[pl.BlockSpec((nc * CHUNK, SCAN_HEADS * HD), lambda h: (0, h), pipeline_mode=ONE_BUFFER),
                   _per_head((nc, HD, HD), SCAN_HEADS)],
        out_shape=[SDS((nc * CHUNK, NG * HD), F32), SDS((NG, nc, HD, HD), F32)], compiler_params=_cp("parallel"),
    )(*terms_in)


def _gdn_bwd_scan(saved, do_raw):
    nc = saved[0].shape[1]
    terms = _gdn_term_shapes(nc)

    def body(*refs):
        intra, states, do, outs = refs[:N_TERMS], refs[N_TERMS], refs[N_TERMS + 1], refs[N_TERMS + 2:]

        def one(c, dss):
            rows = pl.ds(pl.multiple_of(c * CHUNK, CHUNK), CHUNK)
            loaded = [[states[hh, c]] + [r[hh, c] for r in intra] for hh in range(SCAN_HEADS)]
            cts = [do[rows, hh * HD:(hh + 1) * HD] for hh in range(SCAN_HEADS)]
            grads = [jax.vjp(_gdn_step, *loaded[hh])[1]((cts[hh], dss[hh])) for hh in range(SCAN_HEADS)]
            for hh in range(SCAN_HEADS):
                for r, gval in zip(outs, grads[hh][1:]):
                    r[hh, c] = gval
            return tuple(g[0] for g in grads)

        per_trip = math.gcd(nc, SCAN_UNROLL)

        def bwd(i, dss):
            c = nc - 1 - per_trip * i
            for k in range(per_trip):
                dss = one(c - k, dss)
            return dss

        lax.fori_loop(0, nc // per_trip, bwd, tuple(jnp.zeros((HD, HD), F32) for _ in range(SCAN_HEADS)))

    return pl.pallas_call(
        body, grid=(NG // SCAN_HEADS,), name="gdn_bwd_scan",
        in_specs=[_per_head(sh, SCAN_HEADS) for sh in terms] + [_per_head((nc, HD, HD), SCAN_HEADS)]
        + [pl.BlockSpec((nc * CHUNK, SCAN_HEADS * HD), lambda h: (0, h), pipeline_mode=ONE_BUFFER)],
        out_specs=[_per_head(sh, SCAN_HEADS) for sh in terms],
        out_shape=[SDS((NG,) + sh, F32) for sh in terms], compiler_params=_cp("parallel"),
    )(*saved, do_raw)


def _gdn_bwd(pa, gates, conv, dterms, qkv):
    t = pa.shape[0]
    nc = t // CHUNK
    terms = _gdn_term_shapes(nc)

    def body(*refs):
        gq, gk, gv, gt, wq, wk, wv = refs[:7]
        dintra, qkv = refs[7:7 + N_TERMS], refs[7 + N_TERMS:10 + N_TERMS]
        dgq, dgk, dgv, dgt, dwq, dwk, dwv = refs[10 + N_TERMS:17 + N_TERMS]
        chunked, pads, dpads, dgt_s = (refs[17 + N_TERMS:22 + N_TERMS], refs[22 + N_TERMS:25 + N_TERMS],
                                       refs[25 + N_TERMS:28 + N_TERMS], refs[28 + N_TERMS])
        h = pl.program_id(0)
        taps = _taps(wq, wk, wv)
        _gdn_pad((gq, gk, gv), pads)
        rows = _prep_rows(t)
        per = rows // CHUNK

        def gates_tile(i, carry):
            gtile = gt[pl.ds(pl.multiple_of(i * rows, rows), rows), :]
            chunked[3][pl.ds(i * per, per)] = _lane_pick(gtile, L_GA + h).reshape(per, CHUNK, 1)
            chunked[4][pl.ds(i * per, per)] = _lane_pick(gtile, L_GB + h).reshape(per, CHUNK, 1)
            return carry

        lax.fori_loop(0, t // rows, gates_tile, 0)
        grp_n = math.gcd(nc, GROUP)

        def grp(i, carry):
            sl = pl.ds(pl.multiple_of(i * grp_n, grp_n), grp_n)
            _, vjp = jax.vjp(_gdn_intra, *[r[sl] for r in qkv], chunked[3][sl], chunked[4][sl])
            for r, gval in zip(chunked, vjp(tuple(r[sl] for r in dintra))):
                r[sl] = gval
            return carry

        lax.fori_loop(0, nc // grp_n, grp, 0)

        for r in dpads:
            r[...] = jnp.zeros_like(r)

        def tile(i, dtaps):
            r0 = pl.multiple_of(i * rows, rows)
            win = pl.ds(r0, rows + HALO)
            _, vjp = jax.vjp(lambda *a: _gdn_prep(*a, h), *[p[win, :] for p in pads], gt[pl.ds(r0, rows), :], taps)
            grads = vjp(tuple(r[pl.ds(i * per, per)].reshape(rows, r.shape[-1]) for r in chunked))
            for r, gval in zip(dpads, grads[:3]):
                r[win, :] += gval
            dgt_s[pl.ds(r0, rows), :] = grads[3]
            return jax.tree.map(jnp.add, dtaps, grads[4])

        dtaps = lax.fori_loop(0, t // rows, tile, (jnp.zeros((1, HD), F32),) * 12)
        for r, dpad in zip((dgq, dgk, dgv), dpads):
            r[...] = dpad[HALO:, :].astype(r.dtype)
        for j, r in enumerate((dwq, dwk, dwv)):
            for k in range(4):
                r[k:k + 1, :] = dtaps[4 * j + k]

        @pl.when(h == 0)
        def _():
            dgt[...] = jnp.zeros_like(dgt)

        dgt[...] += dgt_s[...]

    head = _head(t)
    taps = pl.BlockSpec((4, HD), lambda h: (0, h))
    return pl.pallas_call(
        body, grid=(NG,), name="gdn_bwd",
        in_specs=_gdn_in_specs(t) + [_per_head(sh) for sh in terms + [(nc, CHUNK, HD)] * 3],
        out_specs=[head, head, head, _small(t), taps, taps, taps],
        out_shape=[SDS((t, NG * HD), BF16)] * 3 + [SDS((t, HD), F32)] + [SDS((4, NG * HD), F32)] * 3,
        scratch_shapes=_gdn_chunked_scratch(nc) + [pltpu.VMEM((t + HALO, HD), F32)] * 6 + [pltpu.VMEM((t, HD), F32)],
        compiler_params=_cp("arbitrary"),
    )(pa, pa, pa, gates, conv, conv, conv, *dterms, *qkv)


def _gdn_post(o, z, gain):
    return (jnp.concatenate(
        [_rms(o[:, h * HD:(h + 1) * HD], gain) * _silu(z[:, h * HD:(h + 1) * HD]) for h in range(NG)], axis=1),)


def _place():
    return lax.axis_index("x"), lax.axis_index("y"), lax.axis_index("c")


def _sum_blocks(name, parts):
    _, r, c = parts.shape
    tr = 64 if r % 64 == 0 else r

    def body(x, o):
        acc = x[0].astype(F32)
        for d in range(1, N_DEV):
            acc = acc + x[d].astype(F32)
        o[...] = acc

    return pl.pallas_call(
        body, grid=(r // tr,), name=name, in_specs=[pl.BlockSpec((N_DEV, tr, c), lambda i: (0, i, 0))],
        out_specs=pl.BlockSpec((tr, c), lambda i: (i, 0)), out_shape=SDS((r, c), F32), compiler_params=_cp("parallel"),
    )(parts)


def _all_reduce_small(name, x, reduce):
    m_per, n = x.shape

    def body(x_ref, out_ref, send_sems, recv_sems, local_sem):
        px, py, pc = _place()
        me, sibling = (px, py, pc), (px, py, 1 - pc)
        chips = [(1 - px, py), (px, 1 - py), (1 - px, 1 - py)]
        buf = out_ref

        def rows(qx, qy, qc):
            return buf.at[pl.ds((4 * qx + 2 * qy + qc) * m_per, m_per), :]

        def copy(k, block, to, src=None):
            return pltpu.make_async_remote_copy(
                src_ref=rows(*block) if src is None else src, dst_ref=rows(*block),
                send_sem=send_sems.at[k], recv_sem=recv_sems.at[k], device_id=to, device_id_type=MESH)

        mine = pltpu.make_async_copy(x_ref, rows(*me), local_sem)
        mine.start()
        first = [copy(0, me, sibling, src=x_ref)]
        first += [copy(1 + j, me, (*chip, pc), src=x_ref) for j, chip in enumerate(chips)]
        for cp in first:
            cp.start()
        passed = [copy(4 + j, (*chip, pc), sibling) for j, chip in enumerate(chips)]
        for j, chip in enumerate(chips):
            copy(1 + j, (*chip, pc), me).wait_recv()
            passed[j].start()
        copy(0, sibling, me).wait_recv()
        for j, chip in enumerate(chips):
            copy(4 + j, (*chip, 1 - pc), me).wait_recv()
        for cp in first + passed:
            cp.wait_send()
        mine.wait()

    gathered = pl.pallas_call(
        body, name=name, out_shape=SDS((N_DEV * m_per, n), x.dtype),
        in_specs=[pl.BlockSpec(memory_space=pltpu.VMEM)], out_specs=pl.BlockSpec(memory_space=pltpu.VMEM),
        scratch_shapes=[pltpu.SemaphoreType.DMA((7,)), pltpu.SemaphoreType.DMA((7,)), pltpu.SemaphoreType.DMA],
    )(x)
    if not reduce:
        return gathered
    return _sum_blocks(name + "_sum", gathered.reshape(N_DEV, m_per, n))


HBM_SPEC = pl.BlockSpec(memory_space=pltpu.HBM)
SEM_SPEC = pl.BlockSpec(memory_space=pltpu.SEMAPHORE)
EFFECT = pltpu.SideEffectType.DATAFLOW_SIDE_EFFECTING


def _copies_start(name, bufs, n_remote, n_local, build, deps):
    nb, nd = len(bufs), len(deps)
    sem_shapes = [pltpu.SemaphoreType.DMA((n_remote,)), pltpu.SemaphoreType.DMA((n_remote,))]
    if n_local:
        sem_shapes.append(pltpu.SemaphoreType.DMA((n_local,)))
    ns = len(sem_shapes)

    def body(*refs):
        sems = refs[nb + nd:nb + nd + ns]
        remote, local = build(refs[:nb], *sems, *([None] * (3 - ns)))
        for cp in local + remote:
            cp.start()
        refs[-1][...] = jnp.zeros((8, HD), F32)

    outs = pl.pallas_call(
        body, name=name,
        out_shape=(*sem_shapes, *[pltpu.HBM(b.shape, b.dtype) for b in bufs], SDS((8, HD), F32)),
        in_specs=[HBM_SPEC] * nb + [ANY_SPEC] * nd,
        out_specs=(*[SEM_SPEC] * ns, *[HBM_SPEC] * nb, pl.BlockSpec(memory_space=pltpu.VMEM)),
        input_output_aliases={i: ns + i for i in range(nb)},
        compiler_params=pltpu.CompilerParams(has_side_effects=EFFECT),
    )(*[pltpu.with_memory_space_constraint(b, pltpu.HBM) for b in bufs], *deps)
    return list(outs[:ns]), list(outs[ns:ns + nb]), outs[-1]


def _copies_wait(name, bufs, sems, build, after):
    nb, ns = len(bufs), len(sems)

    def body(*refs):
        remote, local = build(refs[:nb], *refs[nb:nb + ns], *([None] * (3 - ns)))
        for cp in local:
            cp.wait()
        for cp in remote:
            cp.wait_send()
            cp.wait_recv()

    outs = pl.pallas_call(
        body, name=name, out_shape=tuple(pltpu.HBM(b.shape, b.dtype) for b in bufs),
        in_specs=[HBM_SPEC] * nb + [SEM_SPEC] * ns + [ANY_SPEC] * len(after), out_specs=tuple([HBM_SPEC] * nb),
        input_output_aliases={i: i for i in range(nb)},
        compiler_params=pltpu.CompilerParams(has_side_effects=EFFECT),
    )(*bufs, *sems, *after)
    return list(outs)


def _remote(src, dst, send, recv, k, to):
    return pltpu.make_async_remote_copy(src_ref=src, dst_ref=dst, send_sem=send.at[k], recv_sem=recv.at[k],
                                        device_id=to, device_id_type=MESH)


class _Gather:
    def __init__(self, name, shards, deps):
        self.name, self.n = name, len(shards)
        lands = [lax.empty((N_DEV,) + s.shape, s.dtype) for s in shards]
        self.sems1, bufs, self.token = _copies_start(
            name + "_s1", list(shards) + lands, 4 * self.n, self.n, self._stage1(range(self.n)), deps)
        self.shards, self.lands, self.sems2 = bufs[:self.n], bufs[self.n:], {}

    def _stage1(self, idxs):
        def build(refs, send, recv, loc):
            x, y, c = _place()
            me = 4 * x + 2 * y + c
            targets = [(x, y, 1 - c), (1 - x, y, c), (x, 1 - y, c), (1 - x, 1 - y, c)]
            remote, local = [], []
            for pos, i in enumerate(idxs):
                src, land = refs[pos], refs[len(idxs) + pos]
                local.append(pltpu.make_async_copy(src, land.at[me], loc.at[i]))
                remote += [_remote(src, land.at[me], send, recv, 4 * i + k, to) for k, to in enumerate(targets)]
            return remote, local
        return build

    @staticmethod
    def _stage2(refs, send, recv, loc):
        x, y, c = _place()
        remote = []
        for pos, land in enumerate(refs):
            for j, (cx, cy) in enumerate([(1 - x, y), (x, 1 - y), (1 - x, 1 - y)]):
                blk = land.at[4 * cx + 2 * cy + c]
                remote.append(_remote(blk, blk, send, recv, 3 * pos + j, (x, y, 1 - c)))
        return remote, []

    def pass_on(self, idxs, after):
        tag, m = "".join(map(str, idxs)), len(idxs)
        bufs = _copies_wait(f"{self.name}_w1_{tag}", [self.shards[i] for i in idxs] + [self.lands[i] for i in idxs],
                            self.sems1, self._stage1(idxs), after)
        self.sems2[tag], lands, token = _copies_start(f"{self.name}_s2_{tag}", bufs[m:], 3 * m, 0, self._stage2, ())
        for pos, i in enumerate(idxs):
            self.lands[i] = lands[pos]
        return [token]

    def get(self, idxs, after):
        tag = "".join(map(str, idxs))
        return _copies_wait(f"{self.name}_w2_{tag}", [self.lands[i] for i in idxs], self.sems2[tag], self._stage2, after)


class _RelayGather:
    def __init__(self, name, shards, deps):
        self.name, self.n = name, len(shards)
        lands = [lax.empty((N_DEV,) + s.shape, s.dtype) for s in shards]
        self.sems, bufs, self.token = _copies_start(name + "_s1", list(shards) + lands, 3 * self.n, self.n, self._stage1, deps)
        self.shards, self.lands = bufs[:self.n], bufs[self.n:]

    def _stage1(self, refs, send, recv, loc):
        x, y, c = _place()
        me = 4 * x + 2 * y + c
        remote, local = [], []
        for i in range(self.n):
            src, land = refs[i], refs[self.n + i]
            local.append(pltpu.make_async_copy(src, land.at[me], loc.at[i]))
            remote += [_remote(src, land.at[me], send, recv, 3 * i + k, to)
                       for k, to in enumerate([(x, y, 1 - c), (1 - x, y, c), (x, 1 - y, c)])]
        return remote, local

    @staticmethod
    def _relay(refs, send, recv, loc):
        x, y, c = _place()
        remote = []
        for i, land in enumerate(refs):
            half = land.shape[1] // 2
            from_x = land.at[4 * (1 - x) + 2 * y + c].at[pl.ds(0, half)]
            from_y = land.at[4 * x + 2 * (1 - y) + c].at[pl.ds(half, half)]
            remote += [_remote(from_x, from_x, send, recv, 2 * i, (x, 1 - y, c)),
                       _remote(from_y, from_y, send, recv, 2 * i + 1, (1 - x, y, c))]
        return remote, []

    def forward(self, after):
        bufs = _copies_wait(self.name + "_w1", self.shards + self.lands, self.sems, self._stage1, after)
        self.sems, self.lands, self.token = _copies_start(self.name + "_sf", bufs[self.n:], 2 * self.n, 0, self._relay, ())
        return [self.token]

    def pass_on(self, after):
        lands = _copies_wait(self.name + "_wf", self.lands, self.sems, self._relay, after)
        self.sems, self.lands, self.token = _copies_start(self.name + "_s2", lands, 3 * self.n, 0, _Gather._stage2, ())
        return [self.token]

    def get(self, after):
        return _copies_wait(self.name + "_w2", self.lands, self.sems, _Gather._stage2, after)


def _rows_tile(r, row_bytes, target=1 << 20):
    tr = r
    while tr % 32 == 0 and tr * row_bytes > target:
        tr //= 2
    return tr


def _pair_add(name, g, got, c):
    _, r, cols = g.shape
    tr = _rows_tile(r, cols * 2)

    def body(s, a, b, o):
        o[...] = (a[...].astype(F32) + b[...].astype(F32)).astype(o.dtype)

    return pl.pallas_call(
        body, name=name, out_shape=SDS((4, r, cols), g.dtype),
        grid_spec=pltpu.PrefetchScalarGridSpec(
            num_scalar_prefetch=1, grid=(4, r // tr),
            in_specs=[pl.BlockSpec((None, tr, cols), lambda j, i, s: (2 * j + s[0], i, 0)),
                      pl.BlockSpec((None, tr, cols), lambda j, i, s: (j, i, 0))],
            out_specs=pl.BlockSpec((None, tr, cols), lambda j, i, s: (j, i, 0))),
        compiler_params=_cp("parallel", "parallel"),
    )(c.reshape(1), g, got)


def _quad_sum(name, part, got, chip, wmv=None):
    _, r, cols = part.shape
    tr = _rows_tile(r, cols * 4)
    n_out = 4 if wmv else 1

    def body(s, a, b1, b2, b3, *rest):
        g = ((a[...].astype(F32) + b1[...].astype(F32)) + b2[...].astype(F32)) + b3[...].astype(F32)
        rest[-n_out][...] = g
        if wmv:
            w, m, v = rest[:3]
            rest[-3][...], rest[-2][...], rest[-1][...] = _adamw(w[...], g, m[...], v[...])

    blk = lambda k: pl.BlockSpec((None, tr, cols), lambda i, s, k=k: (jnp.bitwise_xor(s[0], k), i, 0))
    row = pl.BlockSpec((tr, cols), lambda i, s: (i, 0))
    outs = pl.pallas_call(
        body, name=name, out_shape=[SDS((r, cols), F32)] * n_out,
        grid_spec=pltpu.PrefetchScalarGridSpec(
            num_scalar_prefetch=1, grid=(r // tr,), in_specs=[blk(0), blk(1), blk(2), blk(3)] + [row] * (n_out - 1),
            out_specs=[row] * n_out),
        compiler_params=_cp("parallel"),
    )(chip.reshape(1), part, got, got, got, *(wmv or ()))
    return tuple(outs) if wmv else outs[0]


class _Scatter:
    def __init__(self, name, grads, deps):
        self.name, self.n = name, len(grads)
        got = [lax.empty((4,) + g.shape[1:], g.dtype) for g in grads]
        self.sems, bufs, self.token = _copies_start(name + "_s1", list(grads) + got, 4 * self.n, 0, self._stage1, deps)
        self.grads, self.got = bufs[:self.n], bufs[self.n:]

    def _stage1(self, refs, send, recv, loc):
        x, y, c = _place()
        remote = []
        for i in range(self.n):
            remote += [_remote(refs[i].at[2 * j + 1 - c], refs[self.n + i].at[j], send, recv, 4 * i + j, (x, y, 1 - c))
                       for j in range(4)]
        return remote, []

    def _stage2(self, refs, send, recv, loc):
        x, y, c = _place()
        remote = []
        for i in range(self.n):
            for k in (1, 2, 3):
                tx = 1 - x if k & 2 else x
                ty = 1 - y if k & 1 else y
                remote.append(_remote(refs[i].at[2 * tx + ty], refs[self.n + i].at[2 * x + y], send, recv,
                                      3 * i + k - 1, (tx, ty, c)))
        return remote, []

    def mid(self, after):
        bufs = _copies_wait(self.name + "_w1", self.grads + self.got, self.sems, self._stage1, after)
        c = lax.axis_index("c").astype(jnp.int32)
        parts = [_pair_add(f"{self.name}_add{i}", bufs[i], bufs[self.n + i], c) for i in range(self.n)]
        got = [lax.empty(p.shape, p.dtype) for p in parts]
        self.sems, bufs, self.token = _copies_start(self.name + "_s2", parts + got, 3 * self.n, 0, self._stage2, ())
        self.parts, self.got = bufs[:self.n], bufs[self.n:]

    def end(self, after, wmv=None):
        bufs = _copies_wait(self.name + "_w2", self.parts + self.got, self.sems, self._stage2, after)
        chip = (2 * lax.axis_index("x") + lax.axis_index("y")).astype(jnp.int32)
        wmv = wmv or [None] * self.n
        return [_quad_sum(f"{self.name}_sum{i}", bufs[i], bufs[self.n + i], chip, wmv[i]) for i in range(self.n)]


def _adamw(w, g, m, v):
    m = ADAM_B1 * m + (1.0 - ADAM_B1) * g
    v = ADAM_B2 * v + (1.0 - ADAM_B2) * (g * g)
    m_hat = m / (1.0 - ADAM_B1 ** ADAM_STEP)
    v_hat = v / (1.0 - ADAM_B2 ** ADAM_STEP)
    return -ADAM_LR * (m_hat / (jnp.sqrt(v_hat) + ADAM_EPS) + ADAM_WD * w), m, v


def _adamw_call(name, w, g, m, v):
    r, c = w.shape
    tm = 64 if r % 64 == 0 else r
    return _rowwise(name, _adamw, [w, g, m, v], [], [(c, F32)] * 3, tm)


_IN_COLS = 5906


def _perm_in(w):
    pad = jnp.zeros((w.shape[0], 2 * HALF - _IN_COLS), w.dtype)
    return (jnp.concatenate([w[:, 2310:4614], w[:, 4614:5382]], axis=1),
            jnp.concatenate([w[:, :2304], w[:, 5394:5906], w[:, 2304:2310], w[:, 5382:5394], pad], axis=1))


def _unperm_in(ga, gb):
    return jnp.concatenate([gb[:, :2304], gb[:, 2816:2822], ga[:, :2304], ga[:, 2304:3072], gb[:, 2822:2834],
                            gb[:, 2304:2816]], axis=1)


def _lanes(v, at):
    return jnp.pad(v, ((0, 0), (at, HD - at - v.shape[1])))


_PACK = ("norm_mix", "mem_norm", "norm_ffn", "gdn_conv", "fox_q_norm", "fox_k_norm", "gdn_out_norm", "mem_q_norm",
         "mem_k_norm", "fox_f_bias", "gdn_a_log", "gdn_dt_bias", "loss")


def _pack(vals):
    parts = [vals[n].reshape(-1, HD) for n in _PACK]
    used = sum(p.shape[0] for p in parts)
    buf = jnp.concatenate(parts + [jnp.zeros((-used % 8, HD), F32)], axis=0)
    return buf, [(n, p.shape[0]) for n, p in zip(_PACK, parts)]


def _unpack(buf, layout):
    out, at = {}, 0
    for n, rows in layout:
        out[n] = buf[at:at + rows]
        at += rows
    return out


def kernel(x, mem, norm_mix, w_in, fox_f_bias, fox_q_norm, fox_k_norm, gdn_conv, gdn_a_log, gdn_dt_bias, gdn_out_norm, mem_norm, w_mem_kv, mem_q_norm, mem_k_norm, w_out, norm_ffn, w_gate_up, w_down, loss_target, m_norm_mix, m_w_in, m_fox_f_bias, m_fox_q_norm, m_fox_k_norm, m_gdn_conv, m_gdn_a_log, m_gdn_dt_bias, m_gdn_out_norm, m_mem_norm, m_w_mem_kv, m_mem_q_norm, m_mem_k_norm, m_w_out, m_norm_ffn, m_w_gate_up, m_w_down, v_norm_mix, v_w_in, v_fox_f_bias, v_fox_q_norm, v_fox_k_norm, v_gdn_conv, v_gdn_a_log, v_gdn_dt_bias, v_gdn_out_norm, v_mem_norm, v_w_mem_kv, v_mem_q_norm, v_mem_k_norm, v_w_out, v_norm_ffn, v_w_gate_up, v_w_down):
    args = dict(locals())
    d = x.shape[2]
    me = 4 * lax.axis_index("x") + 2 * lax.axis_index("y") + lax.axis_index("c")

    cshard = gdn_conv[0].shape[1]
    conv_pad = jnp.pad(gdn_conv[0], ((0, 4), (0, 3 * HD - cshard)))
    conv_all = _all_reduce_small("ag_conv", conv_pad, False).reshape(N_DEV, 8, 3 * HD)[:, :4, :cshard]
    conv_all = conv_all.transpose(1, 0, 2).reshape(4, N_DEV * cshard)
    w_in_a, w_in_b = _perm_in(w_in[0])
    wmv = lambda n: (args[n][0], args["m_" + n][0], args["v_" + n][0])
    comm = _StepComm(w_in_b, {"in_a": [w_in_a], "kv_out": [w_mem_kv[0], w_out[0]]}, w_gate_up[0], w_down[0], [conv_all],
                     {"ffn": [wmv("w_down"), wmv("w_gate_up")], "a": [None, wmv("w_out"), wmv("w_mem_kv")], "b": [None]})

    grad_x, loss_local, small_grads = _local_step(
        x[0], mem[0], loss_target[0], norm_mix, fox_f_bias, fox_q_norm, fox_k_norm, gdn_a_log, gdn_dt_bias,
        gdn_out_norm, mem_norm, mem_q_norm, mem_k_norm, norm_ffn, conv_all, comm)

    red = comm.finish([grad_x])
    updated = {"w_down": red["ffn"][0], "w_gate_up": red["ffn"][1], "w_out": red["a"][1], "w_mem_kv": red["a"][2]}
    grads = {n: r[0] for n, r in updated.items()}
    grads["w_in"] = _unperm_in(red["a"][0], red["b"][0])
    small_grads["loss"] = jnp.broadcast_to(loss_local, (1, HD))
    packed, layout = _pack(small_grads)
    small = _unpack(_all_reduce_small("ar_small", packed, True), layout)
    loss = small["loss"][0, 0]
    six = {"fox_f_bias": L_FF, "gdn_a_log": L_GA, "gdn_dt_bias": L_GA}
    for n, rows_n in layout[:-1]:
        gsm = small[n]
        if n == "gdn_conv":
            gsm = lax.dynamic_slice(gsm.reshape(4, N_DEV * cshard), (0, me * cshard), (4, cshard))[None]
        elif n in six:
            gsm = gsm[:, six[n]:six[n] + 6]
        else:
            gsm = gsm.reshape(1, rows_n * HD)
        grads[n] = gsm

    names = ['norm_mix', 'w_in', 'fox_f_bias', 'fox_q_norm', 'fox_k_norm', 'gdn_conv', 'gdn_a_log', 'gdn_dt_bias',
             'gdn_out_norm', 'mem_norm', 'w_mem_kv', 'mem_q_norm', 'mem_k_norm', 'w_out', 'norm_ffn', 'w_gate_up', 'w_down']
    big = ("w_in", "w_mem_kv", "w_out", "w_gate_up", "w_down")
    delta, new_m, new_v = {}, {}, {}
    for n in big:
        res = updated[n][1:] if n in updated else _adamw_call("adamw_" + n, args[n][0], grads[n], *wmv(n)[1:])
        delta[n], new_m[n], new_v[n] = [a[None] for a in res]
        grads[n] = grads[n][None]

    def flat(a):
        a = a.reshape(1, -1)
        return jnp.pad(a, ((0, 0), (0, -a.shape[1] % HD))).reshape(-1, HD)

    smalls = [n for n in names if n not in big]
    pk = lambda pre: jnp.concatenate([flat(grads[n] if pre == "g" else args[pre + n]) for n in smalls], axis=0)
    cat = [pk(""), pk("g"), pk("m_"), pk("v_")]
    padr = -cat[0].shape[0] % 8
    cat = [jnp.pad(a, ((0, padr), (0, 0))) for a in cat]
    res = _adamw_call("adamw_small", *cat)
    at = 0
    for n in smalls:
        shape = args[n].shape
        size = math.prod(shape)
        nrow = -(-size // HD)
        for dst, src in zip((delta, new_m, new_v), res):
            dst[n] = src[at:at + nrow].reshape(-1)[:size].reshape(shape)
        at += nrow

    return (loss, grad_x[None], *[grads[n] for n in names], *[delta[n] for n in names],
            *[new_m[n] for n in names], *[new_v[n] for n in names])


class _StepComm:
    def __init__(self, first, shard_groups, w_gate_up, w_down, after, wmv):
        self.wmv, self.done = wmv, {}
        self.first = _RelayGather("ag_first", [first.astype(BF16)], after)
        self.groups, self.shards = {}, []
        for key, ws in shard_groups.items():
            self.groups[key] = list(range(len(self.shards), len(self.shards) + len(ws)))
            self.shards += [w.astype(BF16) for w in ws]
        self.w_gate_up, self.w_down = w_gate_up.astype(BF16), w_down.astype(BF16)
        self.passed, self.scatters = set(), {}

    def start_deps(self):
        return [self.first.token]

    def first_weights(self, after):
        deps = self.first.forward(after)
        self.gather = _Gather("ag", self.shards, deps)
        self.relay = _RelayGather("ag_gu", [self.w_gate_up], [self.gather.token])
        return self.first.get(self.first.pass_on([self.relay.token]))

    def relay_forward(self, after):
        deps = self.relay.forward(after)
        self.gather_down = _Gather("ag_dn", [self.w_down], deps)
        return [self.gather_down.token]

    def pass_on(self, key, after):
        self.passed.add(key)
        if key == "gate_up":
            return self.relay.pass_on(after)
        return self.gather.pass_on(self.groups[key], after)

    def weights(self, key, after):
        if key == "down":
            return self.gather_down.get([0], self.gather_down.pass_on([0], after))
        if key not in self.passed:
            after = self.pass_on(key, after)
        return self.relay.get(after) if key == "gate_up" else self.gather.get(self.groups[key], after)

    def send(self, tag, grads):
        blocks = [g if g.ndim == 3 else g.reshape(N_DEV, g.shape[0] // N_DEV, g.shape[1]) for g in grads]
        self.scatters[tag] = _Scatter("rs_" + tag, blocks, ())
        return [self.scatters[tag].token]

    def mid(self, tag, after):
        self.scatters[tag].mid(after)
        return [self.scatters[tag].token]

    def finish_group(self, tag, after):
        self.done[tag] = self.scatters.pop(tag).end(after, self.wmv[tag])
        first = self.done[tag][0]
        return [first[0] if isinstance(first, tuple) else first]

    def finish(self, after):
        for tag in list(self.scatters):
            self.finish_group(tag, after)
        return self.done


def _local_step(xs, ms, tgt, norm_mix, fox_f_bias, fox_q_norm, fox_k_norm, gdn_a_log, gdn_dt_bias, gdn_out_norm,
                mem_norm, mem_q_norm, mem_k_norm, norm_ffn, conv_all, comm):
    t, d = xs.shape
    bq = min(t, 256)
    fb, alog, dtb = _lanes(fox_f_bias, L_FF), _lanes(gdn_a_log, L_GA), _lanes(gdn_dt_bias, L_GA)
    flat = lambda w: w.reshape(-1, w.shape[-1])

    rms1 = lambda a, g: (_rms(a, g),)
    (u,) = _rowwise("norm_mix", rms1, [xs], [norm_mix], [(d, BF16)], min(t, 256), deps=comm.start_deps())
    w_in_b = flat(comm.first_weights([u])[0])
    pb = _matmul("proj_in_b", u, w_in_b, NN, F32, 1024, 768)
    o_fox = _fox_fwd(pb, fb, fox_q_norm, fox_k_norm, bq)
    w_in_a = flat(comm.weights("in_a", [o_fox])[0])
    pa = _matmul("proj_in_a", u, w_in_a, NN, F32, 1024, 768)
    smrow = (pb, HD, SM)
    (gates,) = _rowwise("gdn_gates", _gdn_gates, [smrow], [alog, dtb], [(HD, F32)], min(t, 256))
    gdn_terms, gdn_qkv = _gdn_fwd(pa, gates, conv_all)
    o_gdn_raw, gdn_states = _gdn_scan(gdn_terms)
    gdn_saved = list(gdn_terms) + [gdn_states]
    deps = comm.relay_forward([o_gdn_raw])
    zrow = (pa, NG * HD, GZ * HD // (NG * HD))
    (o_gdn,) = _rowwise("gdn_post", _gdn_post, [o_gdn_raw, zrow], [gdn_out_norm], [(NG * HD, BF16)], min(t, 256),
                        deps=deps)
    w_kv_all, w_out_all = [flat(w) for w in comm.weights("kv_out", [o_gdn])]
    (mem_n,) = _rowwise("norm_mem", rms1, [ms], [mem_norm], [(d, BF16)], ms.shape[0])
    mkv = _matmul("proj_mem", mem_n, w_kv_all, NN, F32, 256, 512)
    o_mem = _mem_fwd(pb, mkv, mem_q_norm, mem_k_norm)
    deps = comm.pass_on("gate_up", [o_mem])
    mix = jnp.concatenate([o_fox, o_gdn, o_mem], axis=1)
    h1, h1n = _proj_out_norm(mix, w_out_all, xs, norm_ffn, deps)
    (wgu,) = comm.weights("gate_up", [h1n])
    ffw = wgu.shape[2]
    gu, act = _ffn_up(h1n, wgu.reshape(2, 4, d, ffw))
    w_down_all = flat(comm.weights("down", [act])[0])
    dyb, lsum = _ffn_down_loss(act, w_down_all, h1, tgt)
    loss_local = (0.5 / d) * jnp.sum(lsum[::8, ::HD])

    dgu = _ffn_down_bwd(dyb, w_down_all.reshape(4, ffw, d), gu).reshape(8, t, ffw)
    g_w_down = _matmul("grad_w_down", act, dyb, TN, BF16, 512, 2048)
    g_w_gu = _ffn_up_bwd_w(h1n, dgu)
    deps = comm.send("ffn", [g_w_down, g_w_gu])
    rms2 = lambda a, g: (_rms(a, g), a)
    dh1b, g_norm_ffn = _ffn_up_bwd_x(dgu, wgu, h1, norm_ffn, dyb, deps)

    dmix = _matmul("proj_out_bwd_x", dh1b, w_out_all, NT, BF16, 1024, 1024)
    g_w_out = _matmul("grad_w_out", mix, dh1b, TN, BF16, 1024, 2048)
    deps = comm.mid("ffn", [dmix, g_w_out])
    dmq, dmk, dmv, g_mqn, g_mkn = _mem_bwd(pb, mkv, mem_q_norm, mem_k_norm, dmix, deps=deps)
    dmkv = jnp.concatenate([dmk, dmv], axis=1).astype(BF16)
    g_w_kv = _matmul("grad_w_kv", mem_n, dmkv, TN, BF16, 512, 512)
    do_raw, dgz, g_gon = _rowwise_vjp("gdn_post_bwd", _gdn_post, [o_gdn_raw, zrow], [gdn_out_norm],
                                      [(dmix, NG * HD, 1)], [F32, BF16], min(t, 256), deps=deps)
    dterms = _gdn_bwd_scan(gdn_saved, do_raw)
    dgq, dgk, dgv, dgates, dwq, dwk, dwv = _gdn_bwd(pa, gates, conv_all, dterms, gdn_qkv)
    dsm_gdn, g_alog, g_dtb = _rowwise_vjp("gdn_gates_bwd", _gdn_gates, [smrow], [alog, dtb], [dgates], [F32], min(t, 256))
    dp_a = jnp.concatenate([dgq, dgk, dgv, dgz], axis=1)
    g_w_in_a = _matmul("grad_w_in_a", u, dp_a, TN, BF16, 512, 3072)
    deps = comm.mid("a", comm.finish_group("ffn", comm.send("a", [g_w_in_a, g_w_out, g_w_kv])))
    dfq, dfk, dfv, dsm_fox, g_fb, g_fqn, g_fkn = _fox_bwd(pb, fb, fox_q_norm, fox_k_norm, dmix, 2 * bq if t % (2 * bq) == 0 else bq,
                                                          deps=deps)
    dp_b = jnp.concatenate([dfq, dfk, dfv, dmq, (dsm_fox + dsm_gdn).astype(BF16), jnp.zeros((t, HD), BF16)], axis=1)
    g_w_in_b = _matmul("grad_w_in_b", u, dp_b, TN, BF16, 512, 3072)
    deps = comm.send("b", [g_w_in_b])
    dmem_n = _matmul("proj_mem_bwd_x", dmkv, w_kv_all, NT, F32, 256, 512, deps=deps)
    g_mem_norm = _rowwise_vjp("norm_mem_bwd", rms1, [ms], [mem_norm], [dmem_n], [], ms.shape[0])[0]
    deps = comm.mid("b", [g_mem_norm])
    du_a = _matmul("proj_in_bwd_a", dp_a, w_in_a, NT, F32, 1024, 1024, deps=deps)
    grad_x, g_norm_mix = _proj_in_bwd_norm(dp_b, w_in_b, du_a, xs, norm_mix, dh1b, ())

    small_grads = {
        "norm_mix": g_norm_mix, "mem_norm": g_mem_norm, "norm_ffn": g_norm_ffn,
        "gdn_conv": jnp.concatenate([dwq, dwk, dwv], axis=1),
        "fox_q_norm": g_fqn, "fox_k_norm": g_fkn, "gdn_out_norm": g_gon, "mem_q_norm": g_mqn, "mem_k_norm": g_mkn,
        "fox_f_bias": g_fb, "gdn_a_log": g_alog, "gdn_dt_bias": g_dtb}
    return grad_x, loss_local, small_grads
```

```python
import functools
import math

import jax
import jax.numpy as jnp
from jax import lax
from jax.experimental import pallas as pl
from jax.experimental.pallas import tpu as pltpu

F32 = jnp.float32
BF16 = jnp.bfloat16
SDS = jax.ShapeDtypeStruct

N_DEV = 8
HD = 128
NF, NG, NM = 6, 6, 4
CHUNK = 64
GROUP = 16
NORM_EPS = 1e-6
GQ, GK, GV, GZ = 0, 6, 12, 18
FQ, FK, FV, MQ, SM = 0, 6, 12, 18, 22
HALF = 24 * HD
L_FF, L_GA, L_GB = 0, 6, 12
VMEM_LIMIT = 56 * 1024 * 1024

ADAM_LR, ADAM_B1, ADAM_B2, ADAM_EPS, ADAM_WD, ADAM_STEP = 0.001, 0.9, 0.999, 1e-08, 0.01, 10

NN = (((1,), (0,)), ((), ()))
NT = (((1,), (1,)), ((), ()))
TN = (((0,), (0,)), ((), ()))
MESH = pl.DeviceIdType.MESH


def _cp(*sem):
    return pltpu.CompilerParams(dimension_semantics=tuple(sem) if sem else None, vmem_limit_bytes=VMEM_LIMIT)


def _dot(a, b, dims=NN):
    return lax.dot_general(a, b, dims, preferred_element_type=F32)


def _iota(shape, axis):
    return lax.broadcasted_iota(jnp.int32, shape, axis)


def _rms(x, gain):
    return x * lax.rsqrt(jnp.mean(x * x, axis=-1, keepdims=True) + NORM_EPS) * gain


def _sigmoid(x):
    return 0.5 * jnp.tanh(0.5 * x) + 0.5


def _silu(x):
    return x * _sigmoid(x)


def _softplus(x):
    return jnp.maximum(x, 0.0) + jnp.log(1.0 + jnp.exp(-jnp.abs(x)))


def _lane_pick(x, lane):
    oh = (_iota((1, x.shape[-1]), 1) == lane).astype(F32)
    return jnp.sum(x * oh, axis=-1, keepdims=True)


def _cumsum_rows(x):
    tril = (_iota((HD, HD), 0) >= _iota((HD, HD), 1)).astype(F32)
    carry = jnp.zeros((1, x.shape[1]), F32)
    outs = []
    for b in range(x.shape[0] // HD):
        blk = x[b * HD:(b + 1) * HD]
        outs.append(_pdot(tril, blk, "nn", "xa") + carry)
        carry = carry + jnp.sum(blk, axis=0, keepdims=True)
    return jnp.concatenate(outs, axis=0)


def _row_spec(r, tm):
    if isinstance(r, tuple):
        arr, width, cb = r
        return arr, pl.BlockSpec((tm, width), lambda i, cb=cb: (i, cb))
    return r, pl.BlockSpec((tm, r.shape[1]), lambda i: (i, 0))


ANY_SPEC = pl.BlockSpec(memory_space=pl.ANY)


def _rowwise(name, fn, rows, consts, outs, tm, deps=()):
    arrs, specs = zip(*[_row_spec(r, tm) for r in rows])
    n_rows = arrs[0].shape[0]
    nr, nc, nd = len(rows), len(consts), len(deps)

    def body(*refs):
        res = fn(*[r[...] for r in refs[:nr + nc]])
        for o, v in zip(refs[nr + nc + nd:], res):
            o[...] = v.astype(o.dtype)

    return pl.pallas_call(
        body, grid=(n_rows // tm,), name=name,
        in_specs=list(specs) + [pl.BlockSpec(c.shape, lambda i: (0, 0)) for c in consts] + [ANY_SPEC] * nd,
        out_specs=[pl.BlockSpec((tm, w), lambda i: (i, 0)) for w, _ in outs],
        out_shape=[SDS((n_rows, w), dt) for w, dt in outs],
        compiler_params=_cp("parallel"),
    )(*arrs, *consts, *deps)


def _rowwise_vjp(name, fn, rows, consts, cts, grad_dtypes, tm, deps=()):
    arrs, specs = zip(*[_row_spec(r, tm) for r in rows])
    ct_arrs, ct_specs = zip(*[_row_spec(r, tm) for r in cts])
    n_rows = arrs[0].shape[0]
    nr, nc, nct, nd = len(rows), len(consts), len(cts), len(deps)
    plan = [(j, dt) for j, dts in enumerate(grad_dtypes) for dt in (dts if isinstance(dts, tuple) else (dts,))]
    ng = len(plan)
    widths = [specs[j].block_shape[1] for j, _ in plan]
    grad_dtypes = [dt for _, dt in plan]

    def body(*refs):
        vals = [r[...].astype(F32) for r in refs[:nr + nc]]
        ctv = tuple(r[...].astype(F32) for r in refs[nr + nc:nr + nc + nct])
        _, vjp = jax.vjp(fn, *vals)
        grads = vjp(ctv)
        outs = refs[nr + nc + nct + nd:]
        for o, (j, _) in zip(outs[:ng], plan):
            o[...] = grads[j].astype(o.dtype)

        @pl.when(pl.program_id(0) == 0)
        def _():
            for o in outs[ng:]:
                o[...] = jnp.zeros_like(o)

        for o, g in zip(outs[ng:], grads[nr:]):
            o[...] += g

    return pl.pallas_call(
        body, grid=(n_rows // tm,), name=name,
        in_specs=list(specs) + [pl.BlockSpec(c.shape, lambda i: (0, 0)) for c in consts] + list(ct_specs)
        + [ANY_SPEC] * nd,
        out_specs=[pl.BlockSpec((tm, w), lambda i: (i, 0)) for w in widths]
        + [pl.BlockSpec(c.shape, lambda i: (0, 0)) for c in consts],
        out_shape=[SDS((n_rows, w), dt) for w, dt in zip(widths, grad_dtypes)] + [SDS(c.shape, F32) for c in consts],
        compiler_params=_cp("arbitrary"),
    )(*arrs, *consts, *ct_arrs, *deps)


def _tile(n, pref):
    t = min(n, pref)
    while n % t or (t % HD and t != n):
        t -= 1
    return t


def _matmul(name, a, b, dims, out_dtype, tm, tn, residual=None, deps=()):
    ta, tb = dims == TN, dims == NT
    m = a.shape[1] if ta else a.shape[0]
    k = a.shape[0] if ta else a.shape[1]
    n = b.shape[0] if tb else b.shape[1]
    tm, tn = _tile(m, tm), _tile(n, tn)

    def body(*refs):
        acc = _dot(refs[0][...], refs[1][...], dims)
        if residual is not None:
            acc = acc + refs[2][...]
        refs[-1][...] = acc.astype(out_dtype)

    in_specs = [pl.BlockSpec((k, tm), lambda i, j: (0, i)) if ta else pl.BlockSpec((tm, k), lambda i, j: (i, 0)),
                pl.BlockSpec((tn, k), lambda i, j: (j, 0)) if tb else pl.BlockSpec((k, tn), lambda i, j: (0, j))]
    ops = [a, b]
    if residual is not None:
        in_specs.append(pl.BlockSpec((tm, tn), lambda i, j: (i, j)))
        ops.append(residual)
    in_specs += [ANY_SPEC] * len(deps)
    ops += list(deps)
    return pl.pallas_call(
        body, grid=(m // tm, n // tn), name=name, in_specs=in_specs,
        out_specs=pl.BlockSpec((tm, tn), lambda i, j: (i, j)), out_shape=SDS((m, n), out_dtype),
        compiler_params=_cp("parallel", "parallel"),
    )(*ops)


def _proj_out_norm(mix, w_out, xs, gain, deps):
    t, k = mix.shape
    d = w_out.shape[1]
    tm = _tile(t, 512)

    def body(*refs):
        a, b, x, g = refs[:4]
        h1, h1n = refs[4 + len(deps):]
        acc = _dot(a[...], b[...]) + x[...]
        h1[...] = acc
        h1n[...] = _rms(acc, g[...]).astype(BF16)

    return pl.pallas_call(
        body, grid=(t // tm,), name="proj_out",
        in_specs=[pl.BlockSpec((tm, k), lambda i: (i, 0)), pl.BlockSpec((k, d), lambda i: (0, 0)),
                  pl.BlockSpec((tm, d), lambda i: (i, 0)), pl.BlockSpec((1, d), lambda i: (0, 0))] + [ANY_SPEC] * len(deps),
        out_specs=[pl.BlockSpec((tm, d), lambda i: (i, 0))] * 2, out_shape=[SDS((t, d), F32), SDS((t, d), BF16)],
        compiler_params=_cp("parallel"),
    )(mix, w_out, xs, gain, *deps)


def _proj_in_bwd_norm(dp, w, du_a, xs, gain, dh1b, deps):
    t, k = dp.shape
    d = w.shape[0]
    tm = _tile(t, 256)

    def body(*refs):
        a, b, ua, x, g, dh = refs[:6]
        gx, dgain = refs[6 + len(deps):]
        _, vjp = jax.vjp(lambda xx, gn: _rms(xx, gn), x[...], g[...])
        dx, dg = vjp(_dot(a[...], b[...], NT) + ua[...])
        gx[...] = dx + dh[...].astype(F32)

        @pl.when(pl.program_id(0) == 0)
        def _():
            dgain[...] = jnp.zeros_like(dgain)

        dgain[...] += dg

    row = pl.BlockSpec((tm, d), lambda i: (i, 0))
    vec = pl.BlockSpec((1, d), lambda i: (0, 0))
    return pl.pallas_call(
        body, grid=(t // tm,), name="proj_in_bwd_b",
        in_specs=[pl.BlockSpec((tm, k), lambda i: (i, 0)), pl.BlockSpec((d, k), lambda i: (0, 0), pipeline_mode=ONE_BUFFER),
                  row, row, vec, row] + [ANY_SPEC] * len(deps),
        out_specs=[row, vec], out_shape=[SDS((t, d), F32), SDS((1, d), F32)], compiler_params=_cp("arbitrary"),
    )(dp, w, du_a, xs, gain, dh1b, *deps)


def _ffn_up(h1n, wgu):
    t, d = h1n.shape
    w = wgu.shape[3]
    tm = _tile(t, 512)

    def body(a, b, gu, act):
        x = a[...]
        g = _dot(x, b[0])
        u = _dot(x, b[1])
        gu[0] = g.astype(BF16)
        gu[1] = u.astype(BF16)
        act[...] = (_silu(g) * u).astype(BF16)

    return pl.pallas_call(
        body, grid=(4, t // tm), name="ffn_up",
        in_specs=[pl.BlockSpec((tm, d), lambda j, i: (i, 0)), pl.BlockSpec((2, None, d, w), lambda j, i: (0, j, 0, 0))],
        out_specs=[pl.BlockSpec((2, None, tm, w), lambda j, i: (0, j, i, 0)), pl.BlockSpec((tm, w), lambda j, i: (i, j))],
        out_shape=[SDS((2, 4, t, w), BF16), SDS((t, 4 * w), BF16)],
        compiler_params=_cp("parallel", "parallel"),
    )(h1n, wgu)


def _ffn_down_loss(act, wdown, h1, target):
    t, f = act.shape
    d = wdown.shape[1]
    tm, tn = _tile(t, 1024), _tile(d, 512)

    def body(a, b, h, tg, dyb, ls):
        e = _dot(a[...], b[...]) + h[...] - tg[...]
        dyb[...] = (e * (1.0 / d)).astype(BF16)
        ls[...] = jnp.broadcast_to(jnp.sum(e * e), (8, HD))

    return pl.pallas_call(
        body, grid=(t // tm, d // tn), name="ffn_down_loss",
        in_specs=[pl.BlockSpec((tm, f), lambda i, j: (i, 0)), pl.BlockSpec((f, tn), lambda i, j: (0, j)),
                  pl.BlockSpec((tm, tn), lambda i, j: (i, j)), pl.BlockSpec((tm, tn), lambda i, j: (i, j))],
        out_specs=[pl.BlockSpec((tm, tn), lambda i, j: (i, j)), pl.BlockSpec((8, HD), lambda i, j: (i, j))],
        out_shape=[SDS((t, d), BF16), SDS((8 * (t // tm), HD * (d // tn)), F32)],
        compiler_params=_cp("parallel", "parallel"),
    )(act, wdown, h1, target)


def _ffn_down_bwd(dyb, wdown4, gu):
    t, d = dyb.shape
    w = wdown4.shape[1]
    tm = _tile(t, 512)

    def body(a, b, gu_ref, out):
        da = _dot(a[...], b[...], NT)
        g = gu_ref[0].astype(F32)
        u = gu_ref[1].astype(F32)
        s = _sigmoid(g)
        out[0] = (da * u * (s * (1.0 + g * (1.0 - s)))).astype(BF16)
        out[1] = (da * g * s).astype(BF16)

    return pl.pallas_call(
        body, grid=(4, t // tm), name="ffn_down_bwd",
        in_specs=[pl.BlockSpec((tm, d), lambda j, i: (i, 0)), pl.BlockSpec((None, w, d), lambda j, i: (j, 0, 0)),
                  pl.BlockSpec((2, None, tm, w), lambda j, i: (0, j, i, 0))],
        out_specs=pl.BlockSpec((2, None, tm, w), lambda j, i: (0, j, i, 0)),
        out_shape=SDS((2, 4, t, w), BF16),
        compiler_params=_cp("parallel", "parallel"),
    )(dyb, wdown4, gu)


def _ffn_up_bwd_x(dgu, wgu, h1, gain, dyb, deps):
    _, t, w = dgu.shape
    d = wgu.shape[1]
    tm = _tile(t, 512)

    def body(*refs):
        a, b, h, g, dy = refs[:5]
        dh1, dgain, acc = refs[5 + len(deps):]
        i, j = pl.program_id(0), pl.program_id(1)

        @pl.when(j == 0)
        def _():
            acc[...] = jnp.zeros_like(acc)

        acc[...] += _dot(a[...], b[...], NT)

        @pl.when(j == N_DEV - 1)
        def _():
            _, vjp = jax.vjp(lambda x, gn: _rms(x, gn), h[...], g[...])
            dx, dg = vjp(acc[...])
            dh1[...] = (dx + dy[...].astype(F32)).astype(dh1.dtype)

            @pl.when(i == 0)
            def _():
                dgain[...] = jnp.zeros_like(dgain)

            dgain[...] += dg

    row = pl.BlockSpec((tm, d), lambda i, j: (i, 0))
    return pl.pallas_call(
        body, grid=(t // tm, N_DEV), name="ffn_up_bwd_x",
        in_specs=[pl.BlockSpec((None, tm, w), lambda i, j: (j, i, 0)), pl.BlockSpec((None, d, w), lambda i, j: (j, 0, 0)),
                  row, pl.BlockSpec((1, d), lambda i, j: (0, 0)), row] + [ANY_SPEC] * len(deps),
        out_specs=[row, pl.BlockSpec((1, d), lambda i, j: (0, 0))],
        out_shape=[SDS((t, d), BF16), SDS((1, d), F32)], scratch_shapes=[pltpu.VMEM((tm, d), F32)],
        compiler_params=_cp("arbitrary", "arbitrary"),
    )(dgu, wgu, h1, gain, dyb, *deps)


def _ffn_up_bwd_w(h1n, dgu):
    _, t, w = dgu.shape
    d = h1n.shape[1]
    tm = _tile(d, 512)

    def body(a, b, out):
        out[...] = _dot(a[...], b[...], TN).astype(BF16)

    return pl.pallas_call(
        body, grid=(8, d // tm), name="ffn_up_bwd_w",
        in_specs=[pl.BlockSpec((t, tm), lambda j, i: (0, i)), pl.BlockSpec((None, t, w), lambda j, i: (j, 0, 0))],
        out_specs=pl.BlockSpec((None, tm, w), lambda j, i: (j, i, 0)), out_shape=SDS((8, d, w), BF16),
        compiler_params=_cp("parallel", "parallel"),
    )(h1n, dgu)


def _fox_prep(fq, fk, sm, fb, qg, kg, h):
    qn = _rms(fq, qg)
    kn = _rms(fk, kg)
    c = _cumsum_rows(-_softplus(-(sm + fb)))
    ccol = _lane_pick(c, L_FF + h)
    crow = jnp.sum(c.T * (_iota((HD, 1), 0) == L_FF + h).astype(F32), axis=0, keepdims=True)
    return qn, kn, ccol, crow


def _softmax_times(s, v):
    e = jnp.exp(s - lax.stop_gradient(jnp.max(s, axis=1, keepdims=True)))
    return _dot(e.astype(BF16), v.astype(BF16)) * (1.0 / jnp.sum(e, axis=1, keepdims=True))


def _fox_block(q, k, v, cc, cr, off):
    bq = q.shape[0]
    assert k.shape[0] == off + bq
    s = _dot((q * (HD ** -0.5)).astype(BF16), k.astype(BF16), NT) + cc - cr
    diag = jnp.where(_iota((bq, bq), 1) <= _iota((bq, bq), 0), s[:, off:], -1e30)
    s = jnp.concatenate([s[:, :off], diag], axis=1) if off else diag
    return _softmax_times(s, v)


ONE_BUFFER = pl.Buffered(1)


def _pcol(t, cb):
    return pl.BlockSpec((t, HD), lambda h, cb=cb: (0, cb + h), pipeline_mode=ONE_BUFFER)


def _smcol(t):
    return pl.BlockSpec((t, HD), lambda h: (0, SM), pipeline_mode=ONE_BUFFER)


def _head(t):
    return pl.BlockSpec((t, HD), lambda h: (0, h), pipeline_mode=ONE_BUFFER)


def _small(n):
    return pl.BlockSpec((n, HD), lambda h: (0, 0), pipeline_mode=ONE_BUFFER)


def _fox_fwd(p, fb, qg, kg, bq):
    t = p.shape[0]

    def body(fq, fk, fv, sm, fb_r, qg_r, kg_r, o, qn_s, cc_s):
        h = pl.program_id(0)
        qn, kn, ccol, crow = _fox_prep(fq[...], fk[...], sm[...], fb_r[...], qg_r[...], kg_r[...], h)
        qn_s[...] = qn
        cc_s[...] = ccol
        knb = kn.astype(BF16)
        vb = fv[...].astype(BF16)
        for i in range(t // bq):
            rows, ext = pl.ds(i * bq, bq), (i + 1) * bq
            o[rows, :] = _fox_block(qn_s[rows, :], knb[:ext], vb[:ext], cc_s[rows, :], crow[:, :ext], i * bq).astype(o.dtype)

    return pl.pallas_call(
        body, grid=(NF,), name="fox_fwd",
        in_specs=[_pcol(t, FQ), _pcol(t, FK), _pcol(t, FV), _smcol(t), _small(1), _small(1), _small(1)],
        out_specs=_head(t), out_shape=SDS((t, NF * HD), BF16),
        scratch_shapes=[pltpu.VMEM((t, HD), F32), pltpu.VMEM((t, 1), F32)],
        compiler_params=_cp("parallel"),
    )(p, p, p, p, fb, qg, kg)


def _fox_bwd(p, fb, qg, kg, dmix, bq, deps=()):
    t = p.shape[0]

    def body(*refs):
        fq, fk, fv, sm, fb_r, qg_r, kg_r, do = refs[:8]
        dfq, dfk, dfv, dsm, dfb, dqg, dkg, qn_s, cc_s, dqn_s, dcc_s, dkn_s, dv_s, dcr_s = refs[8 + len(deps):]
        h = pl.program_id(0)
        qn, kn, ccol, crow = _fox_prep(fq[...], fk[...], sm[...], fb_r[...], qg_r[...], kg_r[...], h)
        qn_s[...] = qn
        cc_s[...] = ccol
        v = fv[...]
        dkn_s[...] = jnp.zeros_like(dkn_s)
        dv_s[...] = jnp.zeros_like(dv_s)
        dcr_s[...] = jnp.zeros_like(dcr_s)

        for i in range(t // bq):
            rows, ext = pl.ds(i * bq, bq), (i + 1) * bq
            _, vjp = jax.vjp(lambda a, b, c, d, e, off=i * bq: _fox_block(a, b, c, d, e, off),
                             qn_s[rows, :], kn[:ext], v[:ext], cc_s[rows, :], crow[:, :ext])
            dq, dk, dv, dcc, dcr = vjp(do[rows, :].astype(F32))
            dqn_s[rows, :] = dq
            dcc_s[rows, :] = dcc
            dkn_s[:ext, :] += dk
            dv_s[:ext, :] += dv
            dcr_s[:, :ext] += dcr
        _, prep_vjp = jax.vjp(lambda a, b, c, d, e, f: _fox_prep(a, b, c, d, e, f, h),
                              fq[...], fk[...], sm[...], fb_r[...], qg_r[...], kg_r[...])
        g_fq, g_fk, g_sm, g_fb, g_qg, g_kg = prep_vjp((dqn_s[...], dkn_s[...], dcc_s[...], dcr_s[...]))
        dfq[...] = g_fq.astype(dfq.dtype)
        dfk[...] = g_fk.astype(dfk.dtype)
        dfv[...] = dv_s[...].astype(dfv.dtype)

        @pl.when(h == 0)
        def _():
            for r in (dsm, dfb, dqg, dkg):
                r[...] = jnp.zeros_like(r)

        dsm[...] += g_sm
        dfb[...] += g_fb
        dqg[...] += g_qg
        dkg[...] += g_kg

    head = _head(t)
    return pl.pallas_call(
        body, grid=(NF,), name="fox_bwd",
        in_specs=[_pcol(t, FQ), _pcol(t, FK), _pcol(t, FV), _smcol(t), _small(1), _small(1), _small(1), head]
        + [ANY_SPEC] * len(deps),
        out_specs=[head, head, head, _small(t), _small(1), _small(1), _small(1)],
        out_shape=[SDS((t, NF * HD), BF16)] * 3 + [SDS((t, HD), F32)] + [SDS((1, HD), F32)] * 3,
        scratch_shapes=[pltpu.VMEM((t, HD), F32), pltpu.VMEM((t, 1), F32), pltpu.VMEM((t, HD), F32),
                        pltpu.VMEM((t, 1), F32), pltpu.VMEM((t, HD), F32), pltpu.VMEM((t, HD), F32),
                        pltpu.VMEM((1, t), F32)],
        compiler_params=_cp("arbitrary"),
    )(p, p, p, p, fb, qg, kg, dmix, *deps)


def _mem_attn(mq, mk, mv, qg, kg):
    s = _dot((_rms(mq, qg) * (HD ** -0.5)).astype(BF16), _rms(mk, kg).astype(BF16), NT)
    return _softmax_times(s, mv)


def _mem_fwd(p, mkv, qg, kg):
    t, ml = p.shape[0], mkv.shape[0]

    def body(mq, mk, mv, qg_r, kg_r, o):
        o[...] = _mem_attn(mq[...], mk[...], mv[...], qg_r[...], kg_r[...]).astype(o.dtype)

    return pl.pallas_call(
        body, grid=(NM,), name="mem_fwd",
        in_specs=[_pcol(t, MQ), pl.BlockSpec((ml, HD), lambda h: (0, h)), pl.BlockSpec((ml, HD), lambda h: (0, NM + h)),
                  _small(1), _small(1)],
        out_specs=pl.BlockSpec((t, HD), lambda h: (0, h)), out_shape=SDS((t, NM * HD), BF16),
        compiler_params=_cp("parallel"),
    )(p, mkv, mkv, qg, kg)


def _mem_bwd(p, mkv, qg, kg, dmix, deps=()):
    t, ml = p.shape[0], mkv.shape[0]

    def body(*refs):
        mq, mk, mv, qg_r, kg_r, do = refs[:6]
        dmq, dmk, dmv, dqg, dkg = refs[6 + len(deps):]
        _, vjp = jax.vjp(_mem_attn, mq[...], mk[...], mv[...], qg_r[...], kg_r[...])
        g_q, g_k, g_v, g_qg, g_kg = vjp(do[...].astype(F32))
        dmq[...] = g_q.astype(dmq.dtype)
        dmk[...] = g_k
        dmv[...] = g_v

        @pl.when(pl.program_id(0) == 0)
        def _():
            dqg[...] = jnp.zeros_like(dqg)
            dkg[...] = jnp.zeros_like(dkg)

        dqg[...] += g_qg
        dkg[...] += g_kg

    return pl.pallas_call(
        body, grid=(NM,), name="mem_bwd",
        in_specs=[_pcol(t, MQ), pl.BlockSpec((ml, HD), lambda h: (0, h)), pl.BlockSpec((ml, HD), lambda h: (0, NM + h)),
                  _small(1), _small(1), pl.BlockSpec((t, HD), lambda h: (0, NF + NG + h))] + [ANY_SPEC] * len(deps),
        out_specs=[pl.BlockSpec((t, HD), lambda h: (0, h)), pl.BlockSpec((ml, HD), lambda h: (0, h)),
                   pl.BlockSpec((ml, HD), lambda h: (0, h)), _small(1), _small(1)],
        out_shape=[SDS((t, NM * HD), BF16), SDS((ml, NM * HD), F32), SDS((ml, NM * HD), F32),
                   SDS((1, HD), F32), SDS((1, HD), F32)],
        compiler_params=_cp("arbitrary"),
    )(p, mkv, mkv, qg, kg, dmix, *deps)


def _shift_down(x, s):
    if s == 0:
        return x
    return jnp.where(_iota(x.shape, 0) >= s, pltpu.roll(x, s, 0), 0.0)


def _shift_up(x, s):
    if s == 0:
        return x
    n = x.shape[0]
    return jnp.where(_iota(x.shape, 0) < n - s, pltpu.roll(x, n - s, 0), 0.0)


@jax.custom_vjp
def _conv4(x, w0, w1, w2, w3):
    return w0 * _shift_down(x, 3) + w1 * _shift_down(x, 2) + w2 * _shift_down(x, 1) + w3 * x


def _conv4_fwd(x, w0, w1, w2, w3):
    return _conv4(x, w0, w1, w2, w3), (x, w0, w1, w2, w3)


def _conv4_bwd(res, dy):
    x, w0, w1, w2, w3 = res
    ups = [_shift_up(dy, 3 - k) for k in range(4)]
    dx = w0 * ups[0] + w1 * ups[1] + w2 * ups[2] + w3 * ups[3]
    return (dx,) + tuple(jnp.sum(up * x, axis=0, keepdims=True) for up in ups)


_conv4.defvjp(_conv4_fwd, _conv4_bwd)


HALO = 8


def _gdn_gates(sm, alog, dtb):
    lane = _iota((1, HD), 1)
    g = -jnp.exp(alog) * _softplus(sm + dtb)
    return (jnp.where((lane >= L_GA) & (lane < L_GA + NG), g,
                      jnp.where((lane >= L_GB) & (lane < L_GB + NG), _sigmoid(sm), 0.0)),)


def _gdn_prep(gq, gk, gv, gates, taps, h):
    q, k, v = [_silu(_conv4(x, *taps[4 * j:4 * j + 4]))[HALO:] for j, x in enumerate((gq, gk, gv))]
    q = q * lax.rsqrt(jnp.sum(q * q, axis=-1, keepdims=True) + NORM_EPS) * (HD ** -0.5)
    k = k * lax.rsqrt(jnp.sum(k * k, axis=-1, keepdims=True) + NORM_EPS)
    return q, k, v, _lane_pick(gates, L_GA + h), _lane_pick(gates, L_GB + h)


def _split(x, n):
    parts, rest = [], x
    for i in range(n):
        parts.append(rest.astype(BF16))
        if i + 1 < n:
            rest = rest - parts[-1].astype(F32)
    return parts


def _raw_dot(a, b, form):
    lead = a.ndim - 2
    ca, cb = {"nn": (1, 0), "nt": (1, 1), "tn": (0, 0)}[form]
    batch = ((0,), (0,)) if lead else ((), ())
    return lax.dot_general(a, b, (((ca + lead,), (cb + lead,)), batch), preferred_element_type=F32)


def _pdot_impl(a, b, form, mode):
    if mode == "1":
        return _raw_dot(a.astype(BF16), b.astype(BF16), form)
    if mode == "3":
        (ah, al), (bh, bl) = _split(a, 2), _split(b, 2)
        return _raw_dot(ah, bh, form) + (_raw_dot(al, bh, form) + _raw_dot(ah, bl, form))
    if mode == "xa":
        return sum(_raw_dot(a.astype(BF16), t, form) for t in reversed(_split(b, 3)))
    return sum(_raw_dot(t, b.astype(BF16), form) for t in reversed(_split(a, 3)))


@functools.partial(jax.custom_vjp, nondiff_argnums=(2, 3))
def _pdot(a, b, form, mode):
    return _pdot_impl(a, b, form, mode)


def _pdot_fwd(a, b, form, mode):
    return _pdot_impl(a, b, form, mode), (a, b)


def _pdot_bwd(form, mode, res, ct):
    a, b = res
    da_args, db_args = {"nn": ((ct, b, "nt"), (a, ct, "tn")), "nt": ((ct, b, "nn"), (ct, a, "tn")),
                        "tn": ((b, ct, "nt"), (a, ct, "nn"))}[form]

    def side(args, exact):
        if mode in ("1", "3"):
            return mode
        return "xa" if args[0] is exact else "xb"

    if mode == "xa":
        return jnp.zeros_like(a), _pdot_impl(*db_args, side(db_args, a))
    if mode == "xb":
        return _pdot_impl(*da_args, side(da_args, b)), jnp.zeros_like(b)
    return _pdot_impl(*da_args, mode), _pdot_impl(*db_args, mode)


_pdot.defvjp(_pdot_fwd, _pdot_bwd)

GDN_QK, GDN_INV, GDN_SCAN = "1", "1", "1"


@jax.custom_vjp
def _tri_inv(low):
    eye = (_iota((CHUNK, CHUNK), 0) == _iota((CHUNK, CHUNK), 1)).astype(F32)
    inv = eye - low
    pw = low
    for _ in range(5):
        pw = _pdot_impl(pw, pw, "nn", GDN_INV)
        inv = inv + _pdot_impl(inv, pw, "nn", GDN_INV)
    return inv


def _tri_inv_fwd(low):
    inv = _tri_inv(low)
    return inv, inv


def _tri_inv_bwd(inv, ct):
    return (-_pdot_impl(_pdot_impl(inv, ct, "tn", GDN_INV), inv, "nt", GDN_INV),)


_tri_inv.defvjp(_tri_inv_fwd, _tri_inv_bwd)


def _gdn_intra(q, k, v, g, beta):
    n = q.shape[0]
    r, c = _iota((CHUNK, CHUNK), 0), _iota((CHUNK, CHUNK), 1)
    tril, strict = r >= c, r > c
    trilf = jnp.broadcast_to(tril.astype(F32), (n, CHUNK, CHUNK))
    gcm = _pdot(trilf, jnp.broadcast_to(g, (n, CHUNK, CHUNK)), "nn", "xa")
    gcf = _pdot(trilf, jnp.broadcast_to(g, (n, CHUNK, HD)), "nn", "xa")
    lane0 = (_iota((1, 1, CHUNK), 2) == 0).astype(F32)
    gcr = _pdot(jnp.ones((n, CHUNK, CHUNK), F32), gcm * lane0, "nt", "xa")
    decay = jnp.where(tril, jnp.exp(jnp.where(tril, gcm - gcr, 0.0)), 0.0)
    egc = jnp.exp(gcf)
    kb = k * beta
    low = jnp.where(strict, _pdot(kb, k, "nt", GDN_QK) * decay, 0.0)
    inv = _tri_inv(low)
    u = _pdot(inv, v * beta, "nn", GDN_INV)
    w = _pdot(inv, kb * egc, "nn", GDN_INV)
    at = jnp.where(tril, _pdot(q, k, "nt", GDN_QK) * decay, 0.0)
    gl = jnp.sum(jnp.broadcast_to(g, (n, CHUNK, HD)), axis=1, keepdims=True)
    kd = k * jnp.exp(gl - gcf)
    return (_pdot(kd, w, "tn", GDN_SCAN), _pdot(kd, u, "tn", GDN_SCAN), q * egc - _pdot(at, w, "nn", GDN_SCAN),
            _pdot(at, u, "nn", GDN_SCAN), gl)


def _gdn_step(s, kw, ku, a, b, gl):
    return _pdot(a, s, "nn", GDN_SCAN) + b, s * jnp.exp(gl) - _pdot(kw, s, "nn", GDN_SCAN) + ku


SCAN_HEADS = 3
SCAN_UNROLL = 4


def _gdn_chunked_scratch(nc):
    big = pltpu.VMEM((nc, CHUNK, HD), F32)
    return [big, big, big, pltpu.VMEM((nc, CHUNK, 1), F32), pltpu.VMEM((nc, CHUNK, 1), F32)]


N_TERMS = 5


def _gdn_term_shapes(nc):
    return [(nc, HD, HD), (nc, HD, HD), (nc, CHUNK, HD), (nc, CHUNK, HD), (nc, 1, HD)]


def _per_head(shape, heads=None, one_buffer=True):
    lead = (None,) if heads is None else (heads,)
    return pl.BlockSpec(lead + tuple(shape), lambda h: (h,) + (0,) * len(shape),
                        pipeline_mode=ONE_BUFFER if one_buffer else None)


def _gdn_in_specs(t):
    cw = lambda cb: pl.BlockSpec((4, HD), lambda h, cb=cb: (0, cb + h))
    return [_pcol(t, GQ), _pcol(t, GK), _pcol(t, GV), _small(t), cw(0), cw(NG), cw(2 * NG)]


def _taps(wq, wk, wv):
    return tuple(w[k:k + 1, :] for w in (wq, wk, wv) for k in range(4))


def _prep_rows(t):
    return min(t, 256)


def _gdn_pad(srcs, pads):
    for src, pad in zip(srcs, pads):
        pad[0:HALO, :] = jnp.zeros((HALO, HD), F32)
        pad[HALO:, :] = src[...]


def _gdn_stage(pads, gates, taps, h, chunked):
    t = gates.shape[0]
    rows = _prep_rows(t)
    per = rows // CHUNK

    def tile(i, carry):
        r0 = pl.multiple_of(i * rows, rows)
        vals = _gdn_prep(*[p[pl.ds(r0, rows + HALO), :] for p in pads], gates[pl.ds(r0, rows), :], taps, h)
        for v, r in zip(vals, chunked):
            r[pl.ds(i * per, per)] = v.reshape(per, CHUNK, v.shape[-1])
        return carry

    lax.fori_loop(0, t // rows, tile, 0)


def _gdn_intra_all(chunked, intra):
    nc = chunked[0].shape[0]
    grp_n = math.gcd(nc, GROUP)

    def grp(i, carry):
        sl = pl.ds(pl.multiple_of(i * grp_n, grp_n), grp_n)
        for r, val in zip(intra, _gdn_intra(*[c[sl] for c in chunked])):
            r[sl] = val
        return carry

    lax.fori_loop(0, nc // grp_n, grp, 0)


def _gdn_fwd(pa, gates, conv):
    t = pa.shape[0]
    nc = t // CHUNK
    terms = _gdn_term_shapes(nc)

    def body(gq, gk, gv, gt, wq, wk, wv, *rest):
        h = pl.program_id(0)
        intra, chunked, pads = rest[:N_TERMS], rest[N_TERMS:N_TERMS + 5], rest[N_TERMS + 5:]
        _gdn_pad((gq, gk, gv), pads)
        _gdn_stage(pads, gt, _taps(wq, wk, wv), h, chunked)
        _gdn_intra_all(chunked, intra)

    qkv = [(nc, CHUNK, HD)] * 3
    outs = pl.pallas_call(
        body, grid=(NG,), name="gdn_fwd", in_specs=_gdn_in_specs(t),
        out_specs=[_per_head(sh, one_buffer=False) for sh in terms + qkv],
        out_shape=[SDS((NG,) + sh, F32) for sh in terms + qkv],
        scratch_shapes=_gdn_chunked_scratch(nc)[3:] + [pltpu.VMEM((t + HALO, HD), F32)] * 3, compiler_params=_cp("parallel"),
    )(pa, pa, pa, gates, conv, conv, conv)
    return list(outs[:N_TERMS]), list(outs[N_TERMS:])


def _gdn_scan(terms_in):
    nc = terms_in[0].shape[1]
    terms = _gdn_term_shapes(nc)

    def body(*refs):
        intra, o, states = refs[:N_TERMS], refs[N_TERMS], refs[N_TERMS + 1]

        def one(c, ss):
            rows = pl.ds(pl.multiple_of(c * CHUNK, CHUNK), CHUNK)
            loaded = [[r[hh, c] for r in intra] for hh in range(SCAN_HEADS)]
            res = [_gdn_step(ss[hh], *loaded[hh]) for hh in range(SCAN_HEADS)]
            for hh in range(SCAN_HEADS):
                states[hh, c] = ss[hh]
                o[rows, hh * HD:(hh + 1) * HD] = res[hh][0]
            return tuple(r[1] for r in res)

        per_trip = math.gcd(nc, SCAN_UNROLL)

        def step(i, ss):
            for k in range(per_trip):
                ss = one(per_trip * i + k, ss)
            return ss

        lax.fori_loop(0, nc // per_trip, step, tuple(jnp.zeros((HD, HD), F32) for _ in range(SCAN_HEADS)))

    return pl.pallas_call(
        body, grid=(NG // SCAN_HEADS,), name="gdn_scan", in_specs=[_per_head(sh, SCAN_HEADS) for sh in terms],
        out_specs=[pl.BlockSpec((nc * CHUNK, SCAN_HEADS * HD), lambda h: (0, h), pipeline_mode=ONE_BUFFER),
                   _per_head((nc, HD, HD), SCAN_HEADS)],
        out_shape=[SDS((nc * CHUNK, NG * HD), F32), SDS((NG, nc, HD, HD), F32)], compiler_params=_cp("parallel"),
    )(*terms_in)


def _gdn_bwd_scan(saved, do_raw):
    nc = saved[0].shape[1]
    terms = _gdn_term_shapes(nc)

    def body(*refs):
        intra, states, do, outs = refs[:N_TERMS], refs[N_TERMS], refs[N_TERMS + 1], refs[N_TERMS + 2:]

        def one(c, dss):
            rows = pl.ds(pl.multiple_of(c * CHUNK, CHUNK), CHUNK)
            loaded = [[states[hh, c]] + [r[hh, c] for r in intra] for hh in range(SCAN_HEADS)]
            cts = [do[rows, hh * HD:(hh + 1) * HD] for hh in range(SCAN_HEADS)]
            grads = [jax.vjp(_gdn_step, *loaded[hh])[1]((cts[hh], dss[hh])) for hh in range(SCAN_HEADS)]
            for hh in range(SCAN_HEADS):
                for r, gval in zip(outs, grads[hh][1:]):
                    r[hh, c] = gval
            return tuple(g[0] for g in grads)

        per_trip = math.gcd(nc, SCAN_UNROLL)

        def bwd(i, dss):
            c = nc - 1 - per_trip * i
            for k in range(per_trip):
                dss = one(c - k, dss)
            return dss

        lax.fori_loop(0, nc // per_trip, bwd, tuple(jnp.zeros((HD, HD), F32) for _ in range(SCAN_HEADS)))

    return pl.pallas_call(
        body, grid=(NG // SCAN_HEADS,), name="gdn_bwd_scan",
        in_specs=[_per_head(sh, SCAN_HEADS) for sh in terms] + [_per_head((nc, HD, HD), SCAN_HEADS)]
        + [pl.BlockSpec((nc * CHUNK, SCAN_HEADS * HD), lambda h: (0, h), pipeline_mode=ONE_BUFFER)],
        out_specs=[_per_head(sh, SCAN_HEADS) for sh in terms],
        out_shape=[SDS((NG,) + sh, F32) for sh in terms], compiler_params=_cp("parallel"),
    )(*saved, do_raw)


def _gdn_bwd(pa, gates, conv, dterms, qkv):
    t = pa.shape[0]
    nc = t // CHUNK
    terms = _gdn_term_shapes(nc)

    def body(*refs):
        gq, gk, gv, gt, wq, wk, wv = refs[:7]
        dintra, qkv = refs[7:7 + N_TERMS], refs[7 + N_TERMS:10 + N_TERMS]
        dgq, dgk, dgv, dgt, dwq, dwk, dwv = refs[10 + N_TERMS:17 + N_TERMS]
        chunked, pads, dpads, dgt_s = (refs[17 + N_TERMS:22 + N_TERMS], refs[22 + N_TERMS:25 + N_TERMS],
                                       refs[25 + N_TERMS:28 + N_TERMS], refs[28 + N_TERMS])
        h = pl.program_id(0)
        taps = _taps(wq, wk, wv)
        _gdn_pad((gq, gk, gv), pads)
        rows = _prep_rows(t)
        per = rows // CHUNK

        def gates_tile(i, carry):
            gtile = gt[pl.ds(pl.multiple_of(i * rows, rows), rows), :]
            chunked[3][pl.ds(i * per, per)] = _lane_pick(gtile, L_GA + h).reshape(per, CHUNK, 1)
            chunked[4][pl.ds(i * per, per)] = _lane_pick(gtile, L_GB + h).reshape(per, CHUNK, 1)
            return carry

        lax.fori_loop(0, t // rows, gates_tile, 0)
        grp_n = math.gcd(nc, GROUP)

        def grp(i, carry):
            sl = pl.ds(pl.multiple_of(i * grp_n, grp_n), grp_n)
            _, vjp = jax.vjp(_gdn_intra, *[r[sl] for r in qkv], chunked[3][sl], chunked[4][sl])
            for r, gval in zip(chunked, vjp(tuple(r[sl] for r in dintra))):
                r[sl] = gval
            return carry

        lax.fori_loop(0, nc // grp_n, grp, 0)

        for r in dpads:
            r[...] = jnp.zeros_like(r)

        def tile(i, dtaps):
            r0 = pl.multiple_of(i * rows, rows)
            win = pl.ds(r0, rows + HALO)
            _, vjp = jax.vjp(lambda *a: _gdn_prep(*a, h), *[p[win, :] for p in pads], gt[pl.ds(r0, rows), :], taps)
            grads = vjp(tuple(r[pl.ds(i * per, per)].reshape(rows, r.shape[-1]) for r in chunked))
            for r, gval in zip(dpads, grads[:3]):
                r[win, :] += gval
            dgt_s[pl.ds(r0, rows), :] = grads[3]
            return jax.tree.map(jnp.add, dtaps, grads[4])

        dtaps = lax.fori_loop(0, t // rows, tile, (jnp.zeros((1, HD), F32),) * 12)
        for r, dpad in zip((dgq, dgk, dgv), dpads):
            r[...] = dpad[HALO:, :].astype(r.dtype)
        for j, r in enumerate((dwq, dwk, dwv)):
            for k in range(4):
                r[k:k + 1, :] = dtaps[4 * j + k]

        @pl.when(h == 0)
        def _():
            dgt[...] = jnp.zeros_like(dgt)

        dgt[...] += dgt_s[...]

    head = _head(t)
    taps = pl.BlockSpec((4, HD), lambda h: (0, h))
    return pl.pallas_call(
        body, grid=(NG,), name="gdn_bwd",
        in_specs=_gdn_in_specs(t) + [_per_head(sh) for sh in terms + [(nc, CHUNK, HD)] * 3],
        out_specs=[head, head, head, _small(t), taps, taps, taps],
        out_shape=[SDS((t, NG * HD), BF16)] * 3 + [SDS((t, HD), F32)] + [SDS((4, NG * HD), F32)] * 3,
        scratch_shapes=_gdn_chunked_scratch(nc) + [pltpu.VMEM((t + HALO, HD), F32)] * 6 + [pltpu.VMEM((t, HD), F32)],
        compiler_params=_cp("arbitrary"),
    )(pa, pa, pa, gates, conv, conv, conv, *dterms, *qkv)


def _gdn_post(o, z, gain):
    return (jnp.concatenate(
        [_rms(o[:, h * HD:(h + 1) * HD], gain) * _silu(z[:, h * HD:(h + 1) * HD]) for h in range(NG)], axis=1),)


def _place():
    return lax.axis_index("x"), lax.axis_index("y"), lax.axis_index("c")


def _sum_blocks(name, parts):
    _, r, c = parts.shape
    tr = 64 if r % 64 == 0 else r

    def body(x, o):
        acc = x[0].astype(F32)
        for d in range(1, N_DEV):
            acc = acc + x[d].astype(F32)
        o[...] = acc

    return pl.pallas_call(
        body, grid=(r // tr,), name=name, in_specs=[pl.BlockSpec((N_DEV, tr, c), lambda i: (0, i, 0))],
        out_specs=pl.BlockSpec((tr, c), lambda i: (i, 0)), out_shape=SDS((r, c), F32), compiler_params=_cp("parallel"),
    )(parts)


def _all_reduce_small(name, x, reduce):
    m_per, n = x.shape

    def body(x_ref, out_ref, send_sems, recv_sems, local_sem):
        px, py, pc = _place()
        me, sibling = (px, py, pc), (px, py, 1 - pc)
        chips = [(1 - px, py), (px, 1 - py), (1 - px, 1 - py)]
        buf = out_ref

        def rows(qx, qy, qc):
            return buf.at[pl.ds((4 * qx + 2 * qy + qc) * m_per, m_per), :]

        def copy(k, block, to, src=None):
            return pltpu.make_async_remote_copy(
                src_ref=rows(*block) if src is None else src, dst_ref=rows(*block),
                send_sem=send_sems.at[k], recv_sem=recv_sems.at[k], device_id=to, device_id_type=MESH)

        mine = pltpu.make_async_copy(x_ref, rows(*me), local_sem)
        mine.start()
        first = [copy(0, me, sibling, src=x_ref)]
        first += [copy(1 + j, me, (*chip, pc), src=x_ref) for j, chip in enumerate(chips)]
        for cp in first:
            cp.start()
        passed = [copy(4 + j, (*chip, pc), sibling) for j, chip in enumerate(chips)]
        for j, chip in enumerate(chips):
            copy(1 + j, (*chip, pc), me).wait_recv()
            passed[j].start()
        copy(0, sibling, me).wait_recv()
        for j, chip in enumerate(chips):
            copy(4 + j, (*chip, 1 - pc), me).wait_recv()
        for cp in first + passed:
            cp.wait_send()
        mine.wait()

    gathered = pl.pallas_call(
        body, name=name, out_shape=SDS((N_DEV * m_per, n), x.dtype),
        in_specs=[pl.BlockSpec(memory_space=pltpu.VMEM)], out_specs=pl.BlockSpec(memory_space=pltpu.VMEM),
        scratch_shapes=[pltpu.SemaphoreType.DMA((7,)), pltpu.SemaphoreType.DMA((7,)), pltpu.SemaphoreType.DMA],
    )(x)
    if not reduce:
        return gathered
    return _sum_blocks(name + "_sum", gathered.reshape(N_DEV, m_per, n))


HBM_SPEC = pl.BlockSpec(memory_space=pltpu.HBM)
SEM_SPEC = pl.BlockSpec(memory_space=pltpu.SEMAPHORE)
EFFECT = pltpu.SideEffectType.DATAFLOW_SIDE_EFFECTING


def _copies_start(name, bufs, n_remote, n_local, build, deps):
    nb, nd = len(bufs), len(deps)
    sem_shapes = [pltpu.SemaphoreType.DMA((n_remote,)), pltpu.SemaphoreType.DMA((n_remote,))]
    if n_local:
        sem_shapes.append(pltpu.SemaphoreType.DMA((n_local,)))
    ns = len(sem_shapes)

    def body(*refs):
        sems = refs[nb + nd:nb + nd + ns]
        remote, local = build(refs[:nb], *sems, *([None] * (3 - ns)))
        for cp in local + remote:
            cp.start()
        refs[-1][...] = jnp.zeros((8, HD), F32)

    outs = pl.pallas_call(
        body, name=name,
        out_shape=(*sem_shapes, *[pltpu.HBM(b.shape, b.dtype) for b in bufs], SDS((8, HD), F32)),
        in_specs=[HBM_SPEC] * nb + [ANY_SPEC] * nd,
        out_specs=(*[SEM_SPEC] * ns, *[HBM_SPEC] * nb, pl.BlockSpec(memory_space=pltpu.VMEM)),
        input_output_aliases={i: ns + i for i in range(nb)},
        compiler_params=pltpu.CompilerParams(has_side_effects=EFFECT),
    )(*[pltpu.with_memory_space_constraint(b, pltpu.HBM) for b in bufs], *deps)
    return list(outs[:ns]), list(outs[ns:ns + nb]), outs[-1]


def _copies_wait(name, bufs, sems, build, after):
    nb, ns = len(bufs), len(sems)

    def body(*refs):
        remote, local = build(refs[:nb], *refs[nb:nb + ns], *([None] * (3 - ns)))
        for cp in local:
            cp.wait()
        for cp in remote:
            cp.wait_send()
            cp.wait_recv()

    outs = pl.pallas_call(
        body, name=name, out_shape=tuple(pltpu.HBM(b.shape, b.dtype) for b in bufs),
        in_specs=[HBM_SPEC] * nb + [SEM_SPEC] * ns + [ANY_SPEC] * len(after), out_specs=tuple([HBM_SPEC] * nb),
        input_output_aliases={i: i for i in range(nb)},
        compiler_params=pltpu.CompilerParams(has_side_effects=EFFECT),
    )(*bufs, *sems, *after)
    return list(outs)


def _remote(src, dst, send, recv, k, to):
    return pltpu.make_async_remote_copy(src_ref=src, dst_ref=dst, send_sem=send.at[k], recv_sem=recv.at[k],
                                        device_id=to, device_id_type=MESH)


class _Gather:
    def __init__(self, name, shards, deps):
        self.name, self.n = name, len(shards)
        lands = [lax.empty((N_DEV,) + s.shape, s.dtype) for s in shards]
        self.sems1, bufs, self.token = _copies_start(
            name + "_s1", list(shards) + lands, 4 * self.n, self.n, self._stage1(range(self.n)), deps)
        self.shards, self.lands, self.sems2 = bufs[:self.n], bufs[self.n:], {}

    def _stage1(self, idxs):
        def build(refs, send, recv, loc):
            x, y, c = _place()
            me = 4 * x + 2 * y + c
            targets = [(x, y, 1 - c), (1 - x, y, c), (x, 1 - y, c), (1 - x, 1 - y, c)]
            remote, local = [], []
            for pos, i in enumerate(idxs):
                src, land = refs[pos], refs[len(idxs) + pos]
                local.append(pltpu.make_async_copy(src, land.at[me], loc.at[i]))
                remote += [_remote(src, land.at[me], send, recv, 4 * i + k, to) for k, to in enumerate(targets)]
            return remote, local
        return build

    @staticmethod
    def _stage2(refs, send, recv, loc):
        x, y, c = _place()
        remote = []
        for pos, land in enumerate(refs):
            for j, (cx, cy) in enumerate([(1 - x, y), (x, 1 - y), (1 - x, 1 - y)]):
                blk = land.at[4 * cx + 2 * cy + c]
                remote.append(_remote(blk, blk, send, recv, 3 * pos + j, (x, y, 1 - c)))
        return remote, []

    def pass_on(self, idxs, after):
        tag, m = "".join(map(str, idxs)), len(idxs)
        bufs = _copies_wait(f"{self.name}_w1_{tag}", [self.shards[i] for i in idxs] + [self.lands[i] for i in idxs],
                            self.sems1, self._stage1(idxs), after)
        self.sems2[tag], lands, token = _copies_start(f"{self.name}_s2_{tag}", bufs[m:], 3 * m, 0, self._stage2, ())
        for pos, i in enumerate(idxs):
            self.lands[i] = lands[pos]
        return [token]

    def get(self, idxs, after):
        tag = "".join(map(str, idxs))
        return _copies_wait(f"{self.name}_w2_{tag}", [self.lands[i] for i in idxs], self.sems2[tag], self._stage2, after)


class _RelayGather:
    def __init__(self, name, shards, deps):
        self.name, self.n = name, len(shards)
        lands = [lax.empty((N_DEV,) + s.shape, s.dtype) for s in shards]
        self.sems, bufs, self.token = _copies_start(name + "_s1", list(shards) + lands, 3 * self.n, self.n, self._stage1, deps)
        self.shards, self.lands = bufs[:self.n], bufs[self.n:]

    def _stage1(self, refs, send, recv, loc):
        x, y, c = _place()
        me = 4 * x + 2 * y + c
        remote, local = [], []
        for i in range(self.n):
            src, land = refs[i], refs[self.n + i]
            local.append(pltpu.make_async_copy(src, land.at[me], loc.at[i]))
            remote += [_remote(src, land.at[me], send, recv, 3 * i + k, to)
                       for k, to in enumerate([(x, y, 1 - c), (1 - x, y, c), (x, 1 - y, c)])]
        return remote, local

    @staticmethod
    def _relay(refs, send, recv, loc):
        x, y, c = _place()
        remote = []
        for i, land in enumerate(refs):
            half = land.shape[1] // 2
            from_x = land.at[4 * (1 - x) + 2 * y + c].at[pl.ds(0, half)]
            from_y = land.at[4 * x + 2 * (1 - y) + c].at[pl.ds(half, half)]
            remote += [_remote(from_x, from_x, send, recv, 2 * i, (x, 1 - y, c)),
                       _remote(from_y, from_y, send, recv, 2 * i + 1, (1 - x, y, c))]
        return remote, []

    def forward(self, after):
        bufs = _copies_wait(self.name + "_w1", self.shards + self.lands, self.sems, self._stage1, after)
        self.sems, self.lands, self.token = _copies_start(self.name + "_sf", bufs[self.n:], 2 * self.n, 0, self._relay, ())
        return [self.token]

    def pass_on(self, after):
        lands = _copies_wait(self.name + "_wf", self.lands, self.sems, self._relay, after)
        self.sems, self.lands, self.token = _copies_start(self.name + "_s2", lands, 3 * self.n, 0, _Gather._stage2, ())
        return [self.token]

    def get(self, after):
        return _copies_wait(self.name + "_w2", self.lands, self.sems, _Gather._stage2, after)


def _rows_tile(r, row_bytes, target=1 << 20):
    tr = r
    while tr % 32 == 0 and tr * row_bytes > target:
        tr //= 2
    return tr


def _pair_add(name, g, got, c):
    _, r, cols = g.shape
    tr = _rows_tile(r, cols * 2)

    def body(s, a, b, o):
        o[...] = (a[...].astype(F32) + b[...].astype(F32)).astype(o.dtype)

    return pl.pallas_call(
        body, name=name, out_shape=SDS((4, r, cols), g.dtype),
        grid_spec=pltpu.PrefetchScalarGridSpec(
            num_scalar_prefetch=1, grid=(4, r // tr),
            in_specs=[pl.BlockSpec((None, tr, cols), lambda j, i, s: (2 * j + s[0], i, 0)),
                      pl.BlockSpec((None, tr, cols), lambda j, i, s: (j, i, 0))],
            out_specs=pl.BlockSpec((None, tr, cols), lambda j, i, s: (j, i, 0))),
        compiler_params=_cp("parallel", "parallel"),
    )(c.reshape(1), g, got)


def _quad_sum(name, part, got, chip, wmv=None):
    _, r, cols = part.shape
    tr = _rows_tile(r, cols * 4)
    n_out = 4 if wmv else 1

    def body(s, a, b1, b2, b3, *rest):
        g = ((a[...].astype(F32) + b1[...].astype(F32)) + b2[...].astype(F32)) + b3[...].astype(F32)
        rest[-n_out][...] = g
        if wmv:
            w, m, v = rest[:3]
            rest[-3][...], rest[-2][...], rest[-1][...] = _adamw(w[...], g, m[...], v[...])

    blk = lambda k: pl.BlockSpec((None, tr, cols), lambda i, s, k=k: (jnp.bitwise_xor(s[0], k), i, 0))
    row = pl.BlockSpec((tr, cols), lambda i, s: (i, 0))
    outs = pl.pallas_call(
        body, name=name, out_shape=[SDS((r, cols), F32)] * n_out,
        grid_spec=pltpu.PrefetchScalarGridSpec(
            num_scalar_prefetch=1, grid=(r // tr,), in_specs=[blk(0), blk(1), blk(2), blk(3)] + [row] * (n_out - 1),
            out_specs=[row] * n_out),
        compiler_params=_cp("parallel"),
    )(chip.reshape(1), part, got, got, got, *(wmv or ()))
    return tuple(outs) if wmv else outs[0]


class _Scatter:
    def __init__(self, name, grads, deps):
        self.name, self.n = name, len(grads)
        got = [lax.empty((4,) + g.shape[1:], g.dtype) for g in grads]
        self.sems, bufs, self.token = _copies_start(name + "_s1", list(grads) + got, 4 * self.n, 0, self._stage1, deps)
        self.grads, self.got = bufs[:self.n], bufs[self.n:]

    def _stage1(self, refs, send, recv, loc):
        x, y, c = _place()
        remote = []
        for i in range(self.n):
            remote += [_remote(refs[i].at[2 * j + 1 - c], refs[self.n + i].at[j], send, recv, 4 * i + j, (x, y, 1 - c))
                       for j in range(4)]
        return remote, []

    def _stage2(self, refs, send, recv, loc):
        x, y, c = _place()
        remote = []
        for i in range(self.n):
            for k in (1, 2, 3):
                tx = 1 - x if k & 2 else x
                ty = 1 - y if k & 1 else y
                remote.append(_remote(refs[i].at[2 * tx + ty], refs[self.n + i].at[2 * x + y], send, recv,
                                      3 * i + k - 1, (tx, ty, c)))
        return remote, []

    def mid(self, after):
        bufs = _copies_wait(self.name + "_w1", self.grads + self.got, self.sems, self._stage1, after)
        c = lax.axis_index("c").astype(jnp.int32)
        parts = [_pair_add(f"{self.name}_add{i}", bufs[i], bufs[self.n + i], c) for i in range(self.n)]
        got = [lax.empty(p.shape, p.dtype) for p in parts]
        self.sems, bufs, self.token = _copies_start(self.name + "_s2", parts + got, 3 * self.n, 0, self._stage2, ())
        self.parts, self.got = bufs[:self.n], bufs[self.n:]

    def end(self, after, wmv=None):
        bufs = _copies_wait(self.name + "_w2", self.parts + self.got, self.sems, self._stage2, after)
        chip = (2 * lax.axis_index("x") + lax.axis_index("y")).astype(jnp.int32)
        wmv = wmv or [None] * self.n
        return [_quad_sum(f"{self.name}_sum{i}", bufs[i], bufs[self.n + i], chip, wmv[i]) for i in range(self.n)]


def _adamw(w, g, m, v):
    m = ADAM_B1 * m + (1.0 - ADAM_B1) * g
    v = ADAM_B2 * v + (1.0 - ADAM_B2) * (g * g)
    m_hat = m / (1.0 - ADAM_B1 ** ADAM_STEP)
    v_hat = v / (1.0 - ADAM_B2 ** ADAM_STEP)
    return -ADAM_LR * (m_hat / (jnp.sqrt(v_hat) + ADAM_EPS) + ADAM_WD * w), m, v


def _adamw_call(name, w, g, m, v):
    r, c = w.shape
    tm = 64 if r % 64 == 0 else r
    return _rowwise(name, _adamw, [w, g, m, v], [], [(c, F32)] * 3, tm)


_IN_COLS = 5906


def _perm_in(w):
    pad = jnp.zeros((w.shape[0], 2 * HALF - _IN_COLS), w.dtype)
    return (jnp.concatenate([w[:, 2310:4614], w[:, 4614:5382]], axis=1),
            jnp.concatenate([w[:, :2304], w[:, 5394:5906], w[:, 2304:2310], w[:, 5382:5394], pad], axis=1))


def _unperm_in(ga, gb):
    return jnp.concatenate([gb[:, :2304], gb[:, 2816:2822], ga[:, :2304], ga[:, 2304:3072], gb[:, 2822:2834],
                            gb[:, 2304:2816]], axis=1)


def _lanes(v, at):
    return jnp.pad(v, ((0, 0), (at, HD - at - v.shape[1])))


_PACK = ("norm_mix", "mem_norm", "norm_ffn", "gdn_conv", "fox_q_norm", "fox_k_norm", "gdn_out_norm", "mem_q_norm",
         "mem_k_norm", "fox_f_bias", "gdn_a_log", "gdn_dt_bias", "loss")


def _pack(vals):
    parts = [vals[n].reshape(-1, HD) for n in _PACK]
    used = sum(p.shape[0] for p in parts)
    buf = jnp.concatenate(parts + [jnp.zeros((-used % 8, HD), F32)], axis=0)
    return buf, [(n, p.shape[0]) for n, p in zip(_PACK, parts)]


def _unpack(buf, layout):
    out, at = {}, 0
    for n, rows in layout:
        out[n] = buf[at:at + rows]
        at += rows
    return out


def kernel(x, mem, norm_mix, w_in, fox_f_bias, fox_q_norm, fox_k_norm, gdn_conv, gdn_a_log, gdn_dt_bias, gdn_out_norm, mem_norm, w_mem_kv, mem_q_norm, mem_k_norm, w_out, norm_ffn, w_gate_up, w_down, loss_target, m_norm_mix, m_w_in, m_fox_f_bias, m_fox_q_norm, m_fox_k_norm, m_gdn_conv, m_gdn_a_log, m_gdn_dt_bias, m_gdn_out_norm, m_mem_norm, m_w_mem_kv, m_mem_q_norm, m_mem_k_norm, m_w_out, m_norm_ffn, m_w_gate_up, m_w_down, v_norm_mix, v_w_in, v_fox_f_bias, v_fox_q_norm, v_fox_k_norm, v_gdn_conv, v_gdn_a_log, v_gdn_dt_bias, v_gdn_out_norm, v_mem_norm, v_w_mem_kv, v_mem_q_norm, v_mem_k_norm, v_w_out, v_norm_ffn, v_w_gate_up, v_w_down):
    args = dict(locals())
    d = x.shape[2]
    me = 4 * lax.axis_index("x") + 2 * lax.axis_index("y") + lax.axis_index("c")

    cshard = gdn_conv[0].shape[1]
    conv_pad = jnp.pad(gdn_conv[0], ((0, 4), (0, 3 * HD - cshard)))
    conv_all = _all_reduce_small("ag_conv", conv_pad, False).reshape(N_DEV, 8, 3 * HD)[:, :4, :cshard]
    conv_all = conv_all.transpose(1, 0, 2).reshape(4, N_DEV * cshard)
    w_in_a, w_in_b = _perm_in(w_in[0])
    wmv = lambda n: (args[n][0], args["m_" + n][0], args["v_" + n][0])
    comm = _StepComm(w_in_b, {"in_a": [w_in_a], "kv_out": [w_mem_kv[0], w_out[0]]}, w_gate_up[0], w_down[0], [conv_all],
                     {"ffn": [wmv("w_down"), wmv("w_gate_up")], "a": [None, wmv("w_out"), wmv("w_mem_kv")], "b": [None]})

    grad_x, loss_local, small_grads = _local_step(
        x[0], mem[0], loss_target[0], norm_mix, fox_f_bias, fox_q_norm, fox_k_norm, gdn_a_log, gdn_dt_bias,
        gdn_out_norm, mem_norm, mem_q_norm, mem_k_norm, norm_ffn, conv_all, comm)

    red = comm.finish([grad_x])
    updated = {"w_down": red["ffn"][0], "w_gate_up": red["ffn"][1], "w_out": red["a"][1], "w_mem_kv": red["a"][2]}
    grads = {n: r[0] for n, r in updated.items()}
    grads["w_in"] = _unperm_in(red["a"][0], red["b"][0])
    small_grads["loss"] = jnp.broadcast_to(loss_local, (1, HD))
    packed, layout = _pack(small_grads)
    small = _unpack(_all_reduce_small("ar_small", packed, True), layout)
    loss = small["loss"][0, 0]
    six = {"fox_f_bias": L_FF, "gdn_a_log": L_GA, "gdn_dt_bias": L_GA}
    for n, rows_n in layout[:-1]:
        gsm = small[n]
        if n == "gdn_conv":
            gsm = lax.dynamic_slice(gsm.reshape(4, N_DEV * cshard), (0, me * cshard), (4, cshard))[None]
        elif n in six:
            gsm = gsm[:, six[n]:six[n] + 6]
        else:
            gsm = gsm.reshape(1, rows_n * HD)
        grads[n] = gsm

    names = ['norm_mix', 'w_in', 'fox_f_bias', 'fox_q_norm', 'fox_k_norm', 'gdn_conv', 'gdn_a_log', 'gdn_dt_bias',
             'gdn_out_norm', 'mem_norm', 'w_mem_kv', 'mem_q_norm', 'mem_k_norm', 'w_out', 'norm_ffn', 'w_gate_up', 'w_down']
    big = ("w_in", "w_mem_kv", "w_out", "w_gate_up", "w_down")
    delta, new_m, new_v = {}, {}, {}
    for n in big:
        res = updated[n][1:] if n in updated else _adamw_call("adamw_" + n, args[n][0], grads[n], *wmv(n)[1:])
        delta[n], new_m[n], new_v[n] = [a[None] for a in res]
        grads[n] = grads[n][None]

    def flat(a):
        a = a.reshape(1, -1)
        return jnp.pad(a, ((0, 0), (0, -a.shape[1] % HD))).reshape(-1, HD)

    smalls = [n for n in names if n not in big]
    pk = lambda pre: jnp.concatenate([flat(grads[n] if pre == "g" else args[pre + n]) for n in smalls], axis=0)
    cat = [pk(""), pk("g"), pk("m_"), pk("v_")]
    padr = -cat[0].shape[0] % 8
    cat = [jnp.pad(a, ((0, padr), (0, 0))) for a in cat]
    res = _adamw_call("adamw_small", *cat)
    at = 0
    for n in smalls:
        shape = args[n].shape
        size = math.prod(shape)
        nrow = -(-size // HD)
        for dst, src in zip((delta, new_m, new_v), res):
            dst[n] = src[at:at + nrow].reshape(-1)[:size].reshape(shape)
        at += nrow

    return (loss, grad_x[None], *[grads[n] for n in names], *[delta[n] for n in names],
            *[new_m[n] for n in names], *[new_v[n] for n in names])


class _StepComm:
    def __init__(self, first, shard_groups, w_gate_up, w_down, after, wmv):
        self.wmv, self.done = wmv, {}
        self.first = _RelayGather("ag_first", [first.astype(BF16)], after)
        self.groups, self.shards = {}, []
        for key, ws in shard_groups.items():
            self.groups[key] = list(range(len(self.shards), len(self.shards) + len(ws)))
            self.shards += [w.astype(BF16) for w in ws]
        self.w_gate_up, self.w_down = w_gate_up.astype(BF16), w_down.astype(BF16)
        self.passed, self.scatters = set(), {}

    def start_deps(self):
        return [self.first.token]

    def first_weights(self, after):
        deps = self.first.forward(after)
        self.gather = _Gather("ag", self.shards, deps)
        self.relay = _RelayGather("ag_gu", [self.w_gate_up], [self.gather.token])
        return self.first.get(self.first.pass_on([self.relay.token]))

    def relay_forward(self, after):
        deps = self.relay.forward(after)
        self.gather_down = _Gather("ag_dn", [self.w_down], deps)
        return [self.gather_down.token]

    def pass_on(self, key, after):
        self.passed.add(key)
        if key == "gate_up":
            return self.relay.pass_on(after)
        return self.gather.pass_on(self.groups[key], after)

    def weights(self, key, after):
        if key == "down":
            return self.gather_down.get([0], self.gather_down.pass_on([0], after))
        if key not in self.passed:
            after = self.pass_on(key, after)
        return self.relay.get(after) if key == "gate_up" else self.gather.get(self.groups[key], after)

    def send(self, tag, grads):
        blocks = [g if g.ndim == 3 else g.reshape(N_DEV, g.shape[0] // N_DEV, g.shape[1]) for g in grads]
        self.scatters[tag] = _Scatter("rs_" + tag, blocks, ())
        return [self.scatters[tag].token]

    def mid(self, tag, after):
        self.scatters[tag].mid(after)
        return [self.scatters[tag].token]

    def finish_group(self, tag, after):
        self.done[tag] = self.scatters.pop(tag).end(after, self.wmv[tag])
        first = self.done[tag][0]
        return [first[0] if isinstance(first, tuple) else first]

    def finish(self, after):
        for tag in list(self.scatters):
            self.finish_group(tag, after)
        return self.done


def _local_step(xs, ms, tgt, norm_mix, fox_f_bias, fox_q_norm, fox_k_norm, gdn_a_log, gdn_dt_bias, gdn_out_norm,
                mem_norm, mem_q_norm, mem_k_norm, norm_ffn, conv_all, comm):
    t, d = xs.shape
    bq = min(t, 256)
    fb, alog, dtb = _lanes(fox_f_bias, L_FF), _lanes(gdn_a_log, L_GA), _lanes(gdn_dt_bias, L_GA)
    flat = lambda w: w.reshape(-1, w.shape[-1])

    rms1 = lambda a, g: (_rms(a, g),)
    (u,) = _rowwise("norm_mix", rms1, [xs], [norm_mix], [(d, BF16)], min(t, 256), deps=comm.start_deps())
    w_in_b = flat(comm.first_weights([u])[0])
    pb = _matmul("proj_in_b", u, w_in_b, NN, F32, 1024, 768)
    o_fox = _fox_fwd(pb, fb, fox_q_norm, fox_k_norm, bq)
    w_in_a = flat(comm.weights("in_a", [o_fox])[0])
    pa = _matmul("proj_in_a", u, w_in_a, NN, F32, 1024, 768)
    smrow = (pb, HD, SM)
    (gates,) = _rowwise("gdn_gates", _gdn_gates, [smrow], [alog, dtb], [(HD, F32)], min(t, 256))
    gdn_terms, gdn_qkv = _gdn_fwd(pa, gates, conv_all)
    o_gdn_raw, gdn_states = _gdn_scan(gdn_terms)
    gdn_saved = list(gdn_terms) + [gdn_states]
    deps = comm.relay_forward([o_gdn_raw])
    zrow = (pa, NG * HD, GZ * HD // (NG * HD))
    (o_gdn,) = _rowwise("gdn_post", _gdn_post, [o_gdn_raw, zrow], [gdn_out_norm], [(NG * HD, BF16)], min(t, 256),
                        deps=deps)
    w_kv_all, w_out_all = [flat(w) for w in comm.weights("kv_out", [o_gdn])]
    (mem_n,) = _rowwise("norm_mem", rms1, [ms], [mem_norm], [(d, BF16)], ms.shape[0])
    mkv = _matmul("proj_mem", mem_n, w_kv_all, NN, F32, 256, 512)
    o_mem = _mem_fwd(pb, mkv, mem_q_norm, mem_k_norm)
    deps = comm.pass_on("gate_up", [o_mem])
    mix = jnp.concatenate([o_fox, o_gdn, o_mem], axis=1)
    h1, h1n = _proj_out_norm(mix, w_out_all, xs, norm_ffn, deps)
    (wgu,) = comm.weights("gate_up", [h1n])
    ffw = wgu.shape[2]
    gu, act = _ffn_up(h1n, wgu.reshape(2, 4, d, ffw))
    w_down_all = flat(comm.weights("down", [act])[0])
    dyb, lsum = _ffn_down_loss(act, w_down_all, h1, tgt)
    loss_local = (0.5 / d) * jnp.sum(lsum[::8, ::HD])

    dgu = _ffn_down_bwd(dyb, w_down_all.reshape(4, ffw, d), gu).reshape(8, t, ffw)
    g_w_down = _matmul("grad_w_down", act, dyb, TN, BF16, 512, 2048)
    g_w_gu = _ffn_up_bwd_w(h1n, dgu)
    deps = comm.send("ffn", [g_w_down, g_w_gu])
    rms2 = lambda a, g: (_rms(a, g), a)
    dh1b, g_norm_ffn = _ffn_up_bwd_x(dgu, wgu, h1, norm_ffn, dyb, deps)

    dmix = _matmul("proj_out_bwd_x", dh1b, w_out_all, NT, BF16, 1024, 1024)
    g_w_out = _matmul("grad_w_out", mix, dh1b, TN, BF16, 1024, 2048)
    deps = comm.mid("ffn", [dmix, g_w_out])
    dmq, dmk, dmv, g_mqn, g_mkn = _mem_bwd(pb, mkv, mem_q_norm, mem_k_norm, dmix, deps=deps)
    dmkv = jnp.concatenate([dmk, dmv], axis=1).astype(BF16)
    g_w_kv = _matmul("grad_w_kv", mem_n, dmkv, TN, BF16, 512, 512)
    do_raw, dgz, g_gon = _rowwise_vjp("gdn_post_bwd", _gdn_post, [o_gdn_raw, zrow], [gdn_out_norm],
                                      [(dmix, NG * HD, 1)], [F32, BF16], min(t, 256), deps=deps)
    dterms = _gdn_bwd_scan(gdn_saved, do_raw)
    dgq, dgk, dgv, dgates, dwq, dwk, dwv = _gdn_bwd(pa, gates, conv_all, dterms, gdn_qkv)
    dsm_gdn, g_alog, g_dtb = _rowwise_vjp("gdn_gates_bwd", _gdn_gates, [smrow], [alog, dtb], [dgates], [F32], min(t, 256))
    dp_a = jnp.concatenate([dgq, dgk, dgv, dgz], axis=1)
    g_w_in_a = _matmul("grad_w_in_a", u, dp_a, TN, BF16, 512, 3072)
    deps = comm.send("a", [g_w_in_a, g_w_out, g_w_kv])
    du_a = _matmul("proj_in_bwd_a", dp_a, w_in_a, NT, F32, 1024, 1024, deps=deps)
    deps = comm.mid("a", [du_a])
    dfq, dfk, dfv, dsm_fox, g_fb, g_fqn, g_fkn = _fox_bwd(pb, fb, fox_q_norm, fox_k_norm, dmix, 2 * bq if t % (2 * bq) == 0 else bq,
                                                          deps=deps)
    dp_b = jnp.concatenate([dfq, dfk, dfv, dmq, (dsm_fox + dsm_gdn).astype(BF16), jnp.zeros((t, HD), BF16)], axis=1)
    g_w_in_b = _matmul("grad_w_in_b", u, dp_b, TN, BF16, 512, 3072)
    deps = comm.mid("b", comm.finish_group("ffn", comm.send("b", [g_w_in_b])))
    dmem_n = _matmul("proj_mem_bwd_x", dmkv, w_kv_all, NT, F32, 256, 512, deps=deps)
    g_mem_norm = _rowwise_vjp("norm_mem_bwd", rms1, [ms], [mem_norm], [dmem_n], [], ms.shape[0])[0]
    grad_x, g_norm_mix = _proj_in_bwd_norm(dp_b, w_in_b, du_a, xs, norm_mix, dh1b, [g_mem_norm])

    small_grads = {
        "norm_mix": g_norm_mix, "mem_norm": g_mem_norm, "norm_ffn": g_norm_ffn,
        "gdn_conv": jnp.concatenate([dwq, dwk, dwv], axis=1),
        "fox_q_norm": g_fqn, "fox_k_norm": g_fkn, "gdn_out_norm": g_gon, "mem_q_norm": g_mqn, "mem_k_norm": g_mkn,
        "fox_f_bias": g_fb, "gdn_a_log": g_alog, "gdn_dt_bias": g_dtb}
    return grad_x, loss_local, small_grads
```

```python
import functools
import math

import jax
import jax.numpy as jnp
from jax import lax
from jax.experimental import pallas as pl
from jax.experimental.pallas import tpu as pltpu

F32 = jnp.float32
BF16 = jnp.bfloat16
SDS = jax.ShapeDtypeStruct

N_DEV = 8
HD = 128
NF, NG, NM = 6, 6, 4
CHUNK = 64
GROUP = 16
NORM_EPS = 1e-6
GQ, GK, GV, GZ = 0, 6, 12, 18
FQ, FK, FV, MQ, SM = 0, 6, 12, 18, 22
HALF = 24 * HD
L_FF, L_GA, L_GB = 0, 6, 12
VMEM_LIMIT = 56 * 1024 * 1024

ADAM_LR, ADAM_B1, ADAM_B2, ADAM_EPS, ADAM_WD, ADAM_STEP = 0.001, 0.9, 0.999, 1e-08, 0.01, 10

NN = (((1,), (0,)), ((), ()))
NT = (((1,), (1,)), ((), ()))
TN = (((0,), (0,)), ((), ()))
MESH = pl.DeviceIdType.MESH


def _cp(*sem):
    return pltpu.CompilerParams(dimension_semantics=tuple(sem) if sem else None, vmem_limit_bytes=VMEM_LIMIT)


def _dot(a, b, dims=NN):
    return lax.dot_general(a, b, dims, preferred_element_type=F32)


def _iota(shape, axis):
    return lax.broadcasted_iota(jnp.int32, shape, axis)


def _rms(x, gain):
    return x * lax.rsqrt(jnp.mean(x * x, axis=-1, keepdims=True) + NORM_EPS) * gain


def _sigmoid(x):
    return 0.5 * jnp.tanh(0.5 * x) + 0.5


def _silu(x):
    return x * _sigmoid(x)


def _softplus(x):
    return jnp.maximum(x, 0.0) + jnp.log(1.0 + jnp.exp(-jnp.abs(x)))


def _lane_pick(x, lane):
    oh = (_iota((1, x.shape[-1]), 1) == lane).astype(F32)
    return jnp.sum(x * oh, axis=-1, keepdims=True)


def _cumsum_rows(x):
    tril = (_iota((HD, HD), 0) >= _iota((HD, HD), 1)).astype(F32)
    carry = jnp.zeros((1, x.shape[1]), F32)
    outs = []
    for b in range(x.shape[0] // HD):
        blk = x[b * HD:(b + 1) * HD]
        outs.append(_pdot(tril, blk, "nn", "xa") + carry)
        carry = carry + jnp.sum(blk, axis=0, keepdims=True)
    return jnp.concatenate(outs, axis=0)


def _row_spec(r, tm):
    if isinstance(r, tuple):
        arr, width, cb = r
        return arr, pl.BlockSpec((tm, width), lambda i, cb=cb: (i, cb))
    return r, pl.BlockSpec((tm, r.shape[1]), lambda i: (i, 0))


ANY_SPEC = pl.BlockSpec(memory_space=pl.ANY)


def _rowwise(name, fn, rows, consts, outs, tm, deps=()):
    arrs, specs = zip(*[_row_spec(r, tm) for r in rows])
    n_rows = arrs[0].shape[0]
    nr, nc, nd = len(rows), len(consts), len(deps)

    def body(*refs):
        res = fn(*[r[...] for r in refs[:nr + nc]])
        for o, v in zip(refs[nr + nc + nd:], res):
            o[...] = v.astype(o.dtype)

    return pl.pallas_call(
        body, grid=(n_rows // tm,), name=name,
        in_specs=list(specs) + [pl.BlockSpec(c.shape, lambda i: (0, 0)) for c in consts] + [ANY_SPEC] * nd,
        out_specs=[pl.BlockSpec((tm, w), lambda i: (i, 0)) for w, _ in outs],
        out_shape=[SDS((n_rows, w), dt) for w, dt in outs],
        compiler_params=_cp("parallel"),
    )(*arrs, *consts, *deps)


def _rowwise_vjp(name, fn, rows, consts, cts, grad_dtypes, tm, deps=()):
    arrs, specs = zip(*[_row_spec(r, tm) for r in rows])
    ct_arrs, ct_specs = zip(*[_row_spec(r, tm) for r in cts])
    n_rows = arrs[0].shape[0]
    nr, nc, nct, nd = len(rows), len(consts), len(cts), len(deps)
    plan = [(j, dt) for j, dts in enumerate(grad_dtypes) for dt in (dts if isinstance(dts, tuple) else (dts,))]
    ng = len(plan)
    widths = [specs[j].block_shape[1] for j, _ in plan]
    grad_dtypes = [dt for _, dt in plan]

    def body(*refs):
        vals = [r[...].astype(F32) for r in refs[:nr + nc]]
        ctv = tuple(r[...].astype(F32) for r in refs[nr + nc:nr + nc + nct])
        _, vjp = jax.vjp(fn, *vals)
        grads = vjp(ctv)
        outs = refs[nr + nc + nct + nd:]
        for o, (j, _) in zip(outs[:ng], plan):
            o[...] = grads[j].astype(o.dtype)

        @pl.when(pl.program_id(0) == 0)
        def _():
            for o in outs[ng:]:
                o[...] = jnp.zeros_like(o)

        for o, g in zip(outs[ng:], grads[nr:]):
            o[...] += g

    return pl.pallas_call(
        body, grid=(n_rows // tm,), name=name,
        in_specs=list(specs) + [pl.BlockSpec(c.shape, lambda i: (0, 0)) for c in consts] + list(ct_specs)
        + [ANY_SPEC] * nd,
        out_specs=[pl.BlockSpec((tm, w), lambda i: (i, 0)) for w in widths]
        + [pl.BlockSpec(c.shape, lambda i: (0, 0)) for c in consts],
        out_shape=[SDS((n_rows, w), dt) for w, dt in zip(widths, grad_dtypes)] + [SDS(c.shape, F32) for c in consts],
        compiler_params=_cp("arbitrary"),
    )(*arrs, *consts, *ct_arrs, *deps)


def _tile(n, pref):
    t = min(n, pref)
    while n % t or (t % HD and t != n):
        t -= 1
    return t


def _matmul(name, a, b, dims, out_dtype, tm, tn, residual=None, deps=()):
    ta, tb = dims == TN, dims == NT
    m = a.shape[1] if ta else a.shape[0]
    k = a.shape[0] if ta else a.shape[1]
    n = b.shape[0] if tb else b.shape[1]
    tm, tn = _tile(m, tm), _tile(n, tn)

    def body(*refs):
        acc = _dot(refs[0][...], refs[1][...], dims)
        if residual is not None:
            acc = acc + refs[2][...]
        refs[-1][...] = acc.astype(out_dtype)

    in_specs = [pl.BlockSpec((k, tm), lambda i, j: (0, i)) if ta else pl.BlockSpec((tm, k), lambda i, j: (i, 0)),
                pl.BlockSpec((tn, k), lambda i, j: (j, 0)) if tb else pl.BlockSpec((k, tn), lambda i, j: (0, j))]
    ops = [a, b]
    if residual is not None:
        in_specs.append(pl.BlockSpec((tm, tn), lambda i, j: (i, j)))
        ops.append(residual)
    in_specs += [ANY_SPEC] * len(deps)
    ops += list(deps)
    return pl.pallas_call(
        body, grid=(m // tm, n // tn), name=name, in_specs=in_specs,
        out_specs=pl.BlockSpec((tm, tn), lambda i, j: (i, j)), out_shape=SDS((m, n), out_dtype),
        compiler_params=_cp("parallel", "parallel"),
    )(*ops)


def _proj_out_norm(mix, w_out, xs, gain, deps):
    t, k = mix.shape
    d = w_out.shape[1]
    tm = _tile(t, 512)

    def body(*refs):
        a, b, x, g = refs[:4]
        h1, h1n = refs[4 + len(deps):]
        acc = _dot(a[...], b[...]) + x[...]
        h1[...] = acc
        h1n[...] = _rms(acc, g[...]).astype(BF16)

    return pl.pallas_call(
        body, grid=(t // tm,), name="proj_out",
        in_specs=[pl.BlockSpec((tm, k), lambda i: (i, 0)), pl.BlockSpec((k, d), lambda i: (0, 0)),
                  pl.BlockSpec((tm, d), lambda i: (i, 0)), pl.BlockSpec((1, d), lambda i: (0, 0))] + [ANY_SPEC] * len(deps),
        out_specs=[pl.BlockSpec((tm, d), lambda i: (i, 0))] * 2, out_shape=[SDS((t, d), F32), SDS((t, d), BF16)],
        compiler_params=_cp("parallel"),
    )(mix, w_out, xs, gain, *deps)


def _proj_in_bwd_norm(dp, w, du_a, xs, gain, dh1b, deps):
    t, k = dp.shape
    d = w.shape[0]
    tm = _tile(t, 256)

    def body(*refs):
        a, b, ua, x, g, dh = refs[:6]
        gx, dgain = refs[6 + len(deps):]
        _, vjp = jax.vjp(lambda xx, gn: _rms(xx, gn), x[...], g[...])
        dx, dg = vjp(_dot(a[...], b[...], NT) + ua[...])
        gx[...] = dx + dh[...].astype(F32)

        @pl.when(pl.program_id(0) == 0)
        def _():
            dgain[...] = jnp.zeros_like(dgain)

        dgain[...] += dg

    row = pl.BlockSpec((tm, d), lambda i: (i, 0))
    vec = pl.BlockSpec((1, d), lambda i: (0, 0))
    return pl.pallas_call(
        body, grid=(t // tm,), name="proj_in_bwd_b",
        in_specs=[pl.BlockSpec((tm, k), lambda i: (i, 0)), pl.BlockSpec((d, k), lambda i: (0, 0), pipeline_mode=ONE_BUFFER),
                  row, row, vec, row] + [ANY_SPEC] * len(deps),
        out_specs=[row, vec], out_shape=[SDS((t, d), F32), SDS((1, d), F32)], compiler_params=_cp("arbitrary"),
    )(dp, w, du_a, xs, gain, dh1b, *deps)


def _ffn_up(h1n, wgu):
    t, d = h1n.shape
    w = wgu.shape[3]
    tm = _tile(t, 512)

    def body(a, b, gu, act):
        x = a[...]
        g = _dot(x, b[0])
        u = _dot(x, b[1])
        gu[0] = g.astype(BF16)
        gu[1] = u.astype(BF16)
        act[...] = (_silu(g) * u).astype(BF16)

    return pl.pallas_call(
        body, grid=(4, t // tm), name="ffn_up",
        in_specs=[pl.BlockSpec((tm, d), lambda j, i: (i, 0)), pl.BlockSpec((2, None, d, w), lambda j, i: (0, j, 0, 0))],
        out_specs=[pl.BlockSpec((2, None, tm, w), lambda j, i: (0, j, i, 0)), pl.BlockSpec((tm, w), lambda j, i: (i, j))],
        out_shape=[SDS((2, 4, t, w), BF16), SDS((t, 4 * w), BF16)],
        compiler_params=_cp("parallel", "parallel"),
    )(h1n, wgu)


def _ffn_down_loss(act, wdown, h1, target):
    t, f = act.shape
    d = wdown.shape[1]
    tm, tn = _tile(t, 1024), _tile(d, 512)

    def body(a, b, h, tg, dyb, ls):
        e = _dot(a[...], b[...]) + h[...] - tg[...]
        dyb[...] = (e * (1.0 / d)).astype(BF16)
        ls[...] = jnp.broadcast_to(jnp.sum(e * e), (8, HD))

    return pl.pallas_call(
        body, grid=(t // tm, d // tn), name="ffn_down_loss",
        in_specs=[pl.BlockSpec((tm, f), lambda i, j: (i, 0)), pl.BlockSpec((f, tn), lambda i, j: (0, j)),
                  pl.BlockSpec((tm, tn), lambda i, j: (i, j)), pl.BlockSpec((tm, tn), lambda i, j: (i, j))],
        out_specs=[pl.BlockSpec((tm, tn), lambda i, j: (i, j)), pl.BlockSpec((8, HD), lambda i, j: (i, j))],
        out_shape=[SDS((t, d), BF16), SDS((8 * (t // tm), HD * (d // tn)), F32)],
        compiler_params=_cp("parallel", "parallel"),
    )(act, wdown, h1, target)


def _ffn_down_bwd(dyb, wdown4, gu):
    t, d = dyb.shape
    w = wdown4.shape[1]
    tm = _tile(t, 512)

    def body(a, b, gu_ref, out):
        da = _dot(a[...], b[...], NT)
        g = gu_ref[0].astype(F32)
        u = gu_ref[1].astype(F32)
        s = _sigmoid(g)
        out[0] = (da * u * (s * (1.0 + g * (1.0 - s)))).astype(BF16)
        out[1] = (da * g * s).astype(BF16)

    return pl.pallas_call(
        body, grid=(4, t // tm), name="ffn_down_bwd",
        in_specs=[pl.BlockSpec((tm, d), lambda j, i: (i, 0)), pl.BlockSpec((None, w, d), lambda j, i: (j, 0, 0)),
                  pl.BlockSpec((2, None, tm, w), lambda j, i: (0, j, i, 0))],
        out_specs=pl.BlockSpec((2, None, tm, w), lambda j, i: (0, j, i, 0)),
        out_shape=SDS((2, 4, t, w), BF16),
        compiler_params=_cp("parallel", "parallel"),
    )(dyb, wdown4, gu)


def _ffn_up_bwd_x(dgu, wgu, h1, gain, dyb, deps):
    _, t, w = dgu.shape
    d = wgu.shape[1]
    tm = _tile(t, 512)

    def body(*refs):
        a, b, h, g, dy = refs[:5]
        dh1, dgain, acc = refs[5 + len(deps):]
        i, j = pl.program_id(0), pl.program_id(1)

        @pl.when(j == 0)
        def _():
            acc[...] = jnp.zeros_like(acc)

        acc[...] += _dot(a[...], b[...], NT)

        @pl.when(j == N_DEV - 1)
        def _():
            _, vjp = jax.vjp(lambda x, gn: _rms(x, gn), h[...], g[...])
            dx, dg = vjp(acc[...])
            dh1[...] = (dx + dy[...].astype(F32)).astype(dh1.dtype)

            @pl.when(i == 0)
            def _():
                dgain[...] = jnp.zeros_like(dgain)

            dgain[...] += dg

    row = pl.BlockSpec((tm, d), lambda i, j: (i, 0))
    return pl.pallas_call(
        body, grid=(t // tm, N_DEV), name="ffn_up_bwd_x",
        in_specs=[pl.BlockSpec((None, tm, w), lambda i, j: (j, i, 0)), pl.BlockSpec((None, d, w), lambda i, j: (j, 0, 0)),
                  row, pl.BlockSpec((1, d), lambda i, j: (0, 0)), row] + [ANY_SPEC] * len(deps),
        out_specs=[row, pl.BlockSpec((1, d), lambda i, j: (0, 0))],
        out_shape=[SDS((t, d), BF16), SDS((1, d), F32)], scratch_shapes=[pltpu.VMEM((tm, d), F32)],
        compiler_params=_cp("arbitrary", "arbitrary"),
    )(dgu, wgu, h1, gain, dyb, *deps)


def _ffn_up_bwd_w(h1n, dgu):
    _, t, w = dgu.shape
    d = h1n.shape[1]
    tm = _tile(d, 512)

    def body(a, b, out):
        out[...] = _dot(a[...], b[...], TN).astype(BF16)

    return pl.pallas_call(
        body, grid=(8, d // tm), name="ffn_up_bwd_w",
        in_specs=[pl.BlockSpec((t, tm), lambda j, i: (0, i)), pl.BlockSpec((None, t, w), lambda j, i: (j, 0, 0))],
        out_specs=pl.BlockSpec((None, tm, w), lambda j, i: (j, i, 0)), out_shape=SDS((8, d, w), BF16),
        compiler_params=_cp("parallel", "parallel"),
    )(h1n, dgu)


def _fox_prep(fq, fk, sm, fb, qg, kg, h):
    qn = _rms(fq, qg)
    kn = _rms(fk, kg)
    c = _cumsum_rows(-_softplus(-(sm + fb)))
    ccol = _lane_pick(c, L_FF + h)
    crow = jnp.sum(c.T * (_iota((HD, 1), 0) == L_FF + h).astype(F32), axis=0, keepdims=True)
    return qn, kn, ccol, crow


def _softmax_times(s, v):
    e = jnp.exp(s - lax.stop_gradient(jnp.max(s, axis=1, keepdims=True)))
    return _dot(e.astype(BF16), v.astype(BF16)) * (1.0 / jnp.sum(e, axis=1, keepdims=True))


def _fox_block(q, k, v, cc, cr, off):
    bq = q.shape[0]
    assert k.shape[0] == off + bq
    s = _dot((q * (HD ** -0.5)).astype(BF16), k.astype(BF16), NT) + cc - cr
    diag = jnp.where(_iota((bq, bq), 1) <= _iota((bq, bq), 0), s[:, off:], -1e30)
    s = jnp.concatenate([s[:, :off], diag], axis=1) if off else diag
    return _softmax_times(s, v)


ONE_BUFFER = pl.Buffered(1)


def _pcol(t, cb):
    return pl.BlockSpec((t, HD), lambda h, cb=cb: (0, cb + h), pipeline_mode=ONE_BUFFER)


def _smcol(t):
    return pl.BlockSpec((t, HD), lambda h: (0, SM), pipeline_mode=ONE_BUFFER)


def _head(t):
    return pl.BlockSpec((t, HD), lambda h: (0, h), pipeline_mode=ONE_BUFFER)


def _small(n):
    return pl.BlockSpec((n, HD), lambda h: (0, 0), pipeline_mode=ONE_BUFFER)


def _fox_fwd(p, fb, qg, kg, bq):
    t = p.shape[0]

    def body(fq, fk, fv, sm, fb_r, qg_r, kg_r, o, qn_s, cc_s):
        h = pl.program_id(0)
        qn, kn, ccol, crow = _fox_prep(fq[...], fk[...], sm[...], fb_r[...], qg_r[...], kg_r[...], h)
        qn_s[...] = qn
        cc_s[...] = ccol
        knb = kn.astype(BF16)
        vb = fv[...].astype(BF16)
        for i in range(t // bq):
            rows, ext = pl.ds(i * bq, bq), (i + 1) * bq
            o[rows, :] = _fox_block(qn_s[rows, :], knb[:ext], vb[:ext], cc_s[rows, :], crow[:, :ext], i * bq).astype(o.dtype)

    return pl.pallas_call(
        body, grid=(NF,), name="fox_fwd",
        in_specs=[_pcol(t, FQ), _pcol(t, FK), _pcol(t, FV), _smcol(t), _small(1), _small(1), _small(1)],
        out_specs=_head(t), out_shape=SDS((t, NF * HD), BF16),
        scratch_shapes=[pltpu.VMEM((t, HD), F32), pltpu.VMEM((t, 1), F32)],
        compiler_params=_cp("parallel"),
    )(p, p, p, p, fb, qg, kg)


def _fox_bwd(p, fb, qg, kg, dmix, bq, deps=()):
    t = p.shape[0]

    def body(*refs):
        fq, fk, fv, sm, fb_r, qg_r, kg_r, do = refs[:8]
        dfq, dfk, dfv, dsm, dfb, dqg, dkg, qn_s, cc_s, dqn_s, dcc_s, dkn_s, dv_s, dcr_s = refs[8 + len(deps):]
        h = pl.program_id(0)
        qn, kn, ccol, crow = _fox_prep(fq[...], fk[...], sm[...], fb_r[...], qg_r[...], kg_r[...], h)
        qn_s[...] = qn
        cc_s[...] = ccol
        v = fv[...]
        dkn_s[...] = jnp.zeros_like(dkn_s)
        dv_s[...] = jnp.zeros_like(dv_s)
        dcr_s[...] = jnp.zeros_like(dcr_s)

        for i in range(t // bq):
            rows, ext = pl.ds(i * bq, bq), (i + 1) * bq
            _, vjp = jax.vjp(lambda a, b, c, d, e, off=i * bq: _fox_block(a, b, c, d, e, off),
                             qn_s[rows, :], kn[:ext], v[:ext], cc_s[rows, :], crow[:, :ext])
            dq, dk, dv, dcc, dcr = vjp(do[rows, :].astype(F32))
            dqn_s[rows, :] = dq
            dcc_s[rows, :] = dcc
            dkn_s[:ext, :] += dk
            dv_s[:ext, :] += dv
            dcr_s[:, :ext] += dcr
        _, prep_vjp = jax.vjp(lambda a, b, c, d, e, f: _fox_prep(a, b, c, d, e, f, h),
                              fq[...], fk[...], sm[...], fb_r[...], qg_r[...], kg_r[...])
        g_fq, g_fk, g_sm, g_fb, g_qg, g_kg = prep_vjp((dqn_s[...], dkn_s[...], dcc_s[...], dcr_s[...]))
        dfq[...] = g_fq.astype(dfq.dtype)
        dfk[...] = g_fk.astype(dfk.dtype)
        dfv[...] = dv_s[...].astype(dfv.dtype)

        @pl.when(h == 0)
        def _():
            for r in (dsm, dfb, dqg, dkg):
                r[...] = jnp.zeros_like(r)

        dsm[...] += g_sm
        dfb[...] += g_fb
        dqg[...] += g_qg
        dkg[...] += g_kg

    head = _head(t)
    return pl.pallas_call(
        body, grid=(NF,), name="fox_bwd",
        in_specs=[_pcol(t, FQ), _pcol(t, FK), _pcol(t, FV), _smcol(t), _small(1), _small(1), _small(1), head]
        + [ANY_SPEC] * len(deps),
        out_specs=[head, head, head, _small(t), _small(1), _small(1), _small(1)],
        out_shape=[SDS((t, NF * HD), BF16)] * 3 + [SDS((t, HD), F32)] + [SDS((1, HD), F32)] * 3,
        scratch_shapes=[pltpu.VMEM((t, HD), F32), pltpu.VMEM((t, 1), F32), pltpu.VMEM((t, HD), F32),
                        pltpu.VMEM((t, 1), F32), pltpu.VMEM((t, HD), F32), pltpu.VMEM((t, HD), F32),
                        pltpu.VMEM((1, t), F32)],
        compiler_params=_cp("arbitrary"),
    )(p, p, p, p, fb, qg, kg, dmix, *deps)


def _mem_attn(mq, mk, mv, qg, kg):
    s = _dot((_rms(mq, qg) * (HD ** -0.5)).astype(BF16), _rms(mk, kg).astype(BF16), NT)
    return _softmax_times(s, mv)


def _mem_fwd(p, mkv, qg, kg):
    t, ml = p.shape[0], mkv.shape[0]

    def body(mq, mk, mv, qg_r, kg_r, o):
        o[...] = _mem_attn(mq[...], mk[...], mv[...], qg_r[...], kg_r[...]).astype(o.dtype)

    return pl.pallas_call(
        body, grid=(NM,), name="mem_fwd",
        in_specs=[_pcol(t, MQ), pl.BlockSpec((ml, HD), lambda h: (0, h)), pl.BlockSpec((ml, HD), lambda h: (0, NM + h)),
                  _small(1), _small(1)],
        out_specs=pl.BlockSpec((t, HD), lambda h: (0, h)), out_shape=SDS((t, NM * HD), BF16),
        compiler_params=_cp("parallel"),
    )(p, mkv, mkv, qg, kg)


def _mem_bwd(p, mkv, qg, kg, dmix, deps=()):
    t, ml = p.shape[0], mkv.shape[0]

    def body(*refs):
        mq, mk, mv, qg_r, kg_r, do = refs[:6]
        dmq, dmk, dmv, dqg, dkg = refs[6 + len(deps):]
        _, vjp = jax.vjp(_mem_attn, mq[...], mk[...], mv[...], qg_r[...], kg_r[...])
        g_q, g_k, g_v, g_qg, g_kg = vjp(do[...].astype(F32))
        dmq[...] = g_q.astype(dmq.dtype)
        dmk[...] = g_k
        dmv[...] = g_v

        @pl.when(pl.program_id(0) == 0)
        def _():
            dqg[...] = jnp.zeros_like(dqg)
            dkg[...] = jnp.zeros_like(dkg)

        dqg[...] += g_qg
        dkg[...] += g_kg

    return pl.pallas_call(
        body, grid=(NM,), name="mem_bwd",
        in_specs=[_pcol(t, MQ), pl.BlockSpec((ml, HD), lambda h: (0, h)), pl.BlockSpec((ml, HD), lambda h: (0, NM + h)),
                  _small(1), _small(1), pl.BlockSpec((t, HD), lambda h: (0, NF + NG + h))] + [ANY_SPEC] * len(deps),
        out_specs=[pl.BlockSpec((t, HD), lambda h: (0, h)), pl.BlockSpec((ml, HD), lambda h: (0, h)),
                   pl.BlockSpec((ml, HD), lambda h: (0, h)), _small(1), _small(1)],
        out_shape=[SDS((t, NM * HD), BF16), SDS((ml, NM * HD), F32), SDS((ml, NM * HD), F32),
                   SDS((1, HD), F32), SDS((1, HD), F32)],
        compiler_params=_cp("arbitrary"),
    )(p, mkv, mkv, qg, kg, dmix, *deps)


def _shift_down(x, s):
    if s == 0:
        return x
    return jnp.where(_iota(x.shape, 0) >= s, pltpu.roll(x, s, 0), 0.0)


def _shift_up(x, s):
    if s == 0:
        return x
    n = x.shape[0]
    return jnp.where(_iota(x.shape, 0) < n - s, pltpu.roll(x, n - s, 0), 0.0)


@jax.custom_vjp
def _conv4(x, w0, w1, w2, w3):
    return w0 * _shift_down(x, 3) + w1 * _shift_down(x, 2) + w2 * _shift_down(x, 1) + w3 * x


def _conv4_fwd(x, w0, w1, w2, w3):
    return _conv4(x, w0, w1, w2, w3), (x, w0, w1, w2, w3)


def _conv4_bwd(res, dy):
    x, w0, w1, w2, w3 = res
    ups = [_shift_up(dy, 3 - k) for k in range(4)]
    dx = w0 * ups[0] + w1 * ups[1] + w2 * ups[2] + w3 * ups[3]
    return (dx,) + tuple(jnp.sum(up * x, axis=0, keepdims=True) for up in ups)


_conv4.defvjp(_conv4_fwd, _conv4_bwd)


HALO = 8


def _gdn_gates(sm, alog, dtb):
    lane = _iota((1, HD), 1)
    g = -jnp.exp(alog) * _softplus(sm + dtb)
    return (jnp.where((lane >= L_GA) & (lane < L_GA + NG), g,
                      jnp.where((lane >= L_GB) & (lane < L_GB + NG), _sigmoid(sm), 0.0)),)


def _gdn_prep(gq, gk, gv, gates, taps, h):
    q, k, v = [_silu(_conv4(x, *taps[4 * j:4 * j + 4]))[HALO:] for j, x in enumerate((gq, gk, gv))]
    q = q * lax.rsqrt(jnp.sum(q * q, axis=-1, keepdims=True) + NORM_EPS) * (HD ** -0.5)
    k = k * lax.rsqrt(jnp.sum(k * k, axis=-1, keepdims=True) + NORM_EPS)
    return q, k, v, _lane_pick(gates, L_GA + h), _lane_pick(gates, L_GB + h)


def _split(x, n):
    parts, rest = [], x
    for i in range(n):
        parts.append(rest.astype(BF16))
        if i + 1 < n:
            rest = rest - parts[-1].astype(F32)
    return parts


def _raw_dot(a, b, form):
    lead = a.ndim - 2
    ca, cb = {"nn": (1, 0), "nt": (1, 1), "tn": (0, 0)}[form]
    batch = ((0,), (0,)) if lead else ((), ())
    return lax.dot_general(a, b, (((ca + lead,), (cb + lead,)), batch), preferred_element_type=F32)


def _pdot_impl(a, b, form, mode):
    if mode == "1":
        return _raw_dot(a.astype(BF16), b.astype(BF16), form)
    if mode == "3":
        (ah, al), (bh, bl) = _split(a, 2), _split(b, 2)
        return _raw_dot(ah, bh, form) + (_raw_dot(al, bh, form) + _raw_dot(ah, bl, form))
    if mode == "xa":
        return sum(_raw_dot(a.astype(BF16), t, form) for t in reversed(_split(b, 3)))
    return sum(_raw_dot(t, b.astype(BF16), form) for t in reversed(_split(a, 3)))


@functools.partial(jax.custom_vjp, nondiff_argnums=(2, 3))
def _pdot(a, b, form, mode):
    return _pdot_impl(a, b, form, mode)


def _pdot_fwd(a, b, form, mode):
    return _pdot_impl(a, b, form, mode), (a, b)


def _pdot_bwd(form, mode, res, ct):
    a, b = res
    da_args, db_args = {"nn": ((ct, b, "nt"), (a, ct, "tn")), "nt": ((ct, b, "nn"), (ct, a, "tn")),
                        "tn": ((b, ct, "nt"), (a, ct, "nn"))}[form]

    def side(args, exact):
        if mode in ("1", "3"):
            return mode
        return "xa" if args[0] is exact else "xb"

    if mode == "xa":
        return jnp.zeros_like(a), _pdot_impl(*db_args, side(db_args, a))
    if mode == "xb":
        return _pdot_impl(*da_args, side(da_args, b)), jnp.zeros_like(b)
    return _pdot_impl(*da_args, mode), _pdot_impl(*db_args, mode)


_pdot.defvjp(_pdot_fwd, _pdot_bwd)

GDN_QK, GDN_INV, GDN_SCAN = "1", "1", "1"


@jax.custom_vjp
def _tri_inv(low):
    eye = (_iota((CHUNK, CHUNK), 0) == _iota((CHUNK, CHUNK), 1)).astype(F32)
    inv = eye - low
    pw = low
    for _ in range(5):
        pw = _pdot_impl(pw, pw, "nn", GDN_INV)
        inv = inv + _pdot_impl(inv, pw, "nn", GDN_INV)
    return inv


def _tri_inv_fwd(low):
    inv = _tri_inv(low)
    return inv, inv


def _tri_inv_bwd(inv, ct):
    return (-_pdot_impl(_pdot_impl(inv, ct, "tn", GDN_INV), inv, "nt", GDN_INV),)


_tri_inv.defvjp(_tri_inv_fwd, _tri_inv_bwd)


def _gdn_intra(q, k, v, g, beta):
    n = q.shape[0]
    r, c = _iota((CHUNK, CHUNK), 0), _iota((CHUNK, CHUNK), 1)
    tril, strict = r >= c, r > c
    trilf = jnp.broadcast_to(tril.astype(F32), (n, CHUNK, CHUNK))
    gcm = _pdot(trilf, jnp.broadcast_to(g, (n, CHUNK, CHUNK)), "nn", "xa")
    gcf = _pdot(trilf, jnp.broadcast_to(g, (n, CHUNK, HD)), "nn", "xa")
    lane0 = (_iota((1, 1, CHUNK), 2) == 0).astype(F32)
    gcr = _pdot(jnp.ones((n, CHUNK, CHUNK), F32), gcm * lane0, "nt", "xa")
    decay = jnp.where(tril, jnp.exp(jnp.where(tril, gcm - gcr, 0.0)), 0.0)
    egc = jnp.exp(gcf)
    kb = k * beta
    low = jnp.where(strict, _pdot(kb, k, "nt", GDN_QK) * decay, 0.0)
    inv = _tri_inv(low)
    u = _pdot(inv, v * beta, "nn", GDN_INV)
    w = _pdot(inv, kb * egc, "nn", GDN_INV)
    at = jnp.where(tril, _pdot(q, k, "nt", GDN_QK) * decay, 0.0)
    gl = jnp.sum(jnp.broadcast_to(g, (n, CHUNK, HD)), axis=1, keepdims=True)
    kd = k * jnp.exp(gl - gcf)
    return (_pdot(kd, w, "tn", GDN_SCAN), _pdot(kd, u, "tn", GDN_SCAN), q * egc - _pdot(at, w, "nn", GDN_SCAN),
            _pdot(at, u, "nn", GDN_SCAN), gl)


def _gdn_step(s, kw, ku, a, b, gl):
    return _pdot(a, s, "nn", GDN_SCAN) + b, s * jnp.exp(gl) - _pdot(kw, s, "nn", GDN_SCAN) + ku


SCAN_HEADS = 3
SCAN_UNROLL = 4


def _gdn_chunked_scratch(nc):
    big = pltpu.VMEM((nc, CHUNK, HD), F32)
    return [big, big, big, pltpu.VMEM((nc, CHUNK, 1), F32), pltpu.VMEM((nc, CHUNK, 1), F32)]


N_TERMS = 5


def _gdn_term_shapes(nc):
    return [(nc, HD, HD), (nc, HD, HD), (nc, CHUNK, HD), (nc, CHUNK, HD), (nc, 1, HD)]


def _per_head(shape, heads=None, one_buffer=True):
    lead = (None,) if heads is None else (heads,)
    return pl.BlockSpec(lead + tuple(shape), lambda h: (h,) + (0,) * len(shape),
                        pipeline_mode=ONE_BUFFER if one_buffer else None)


def _gdn_in_specs(t):
    cw = lambda cb: pl.BlockSpec((4, HD), lambda h, cb=cb: (0, cb + h))
    return [_pcol(t, GQ), _pcol(t, GK), _pcol(t, GV), _small(t), cw(0), cw(NG), cw(2 * NG)]


def _taps(wq, wk, wv):
    return tuple(w[k:k + 1, :] for w in (wq, wk, wv) for k in range(4))


def _prep_rows(t):
    return min(t, 256)


def _gdn_pad(srcs, pads):
    for src, pad in zip(srcs, pads):
        pad[0:HALO, :] = jnp.zeros((HALO, HD), F32)
        pad[HALO:, :] = src[...]


def _gdn_stage(pads, gates, taps, h, chunked):
    t = gates.shape[0]
    rows = _prep_rows(t)
    per = rows // CHUNK

    def tile(i, carry):
        r0 = pl.multiple_of(i * rows, rows)
        vals = _gdn_prep(*[p[pl.ds(r0, rows + HALO), :] for p in pads], gates[pl.ds(r0, rows), :], taps, h)
        for v, r in zip(vals, chunked):
            r[pl.ds(i * per, per)] = v.reshape(per, CHUNK, v.shape[-1])
        return carry

    lax.fori_loop(0, t // rows, tile, 0)


def _gdn_intra_all(chunked, intra):
    nc = chunked[0].shape[0]
    grp_n = math.gcd(nc, GROUP)

    def grp(i, carry):
        sl = pl.ds(pl.multiple_of(i * grp_n, grp_n), grp_n)
        for r, val in zip(intra, _gdn_intra(*[c[sl] for c in chunked])):
            r[sl] = val
        return carry

    lax.fori_loop(0, nc // grp_n, grp, 0)


def _gdn_fwd(pa, gates, conv):
    t = pa.shape[0]
    nc = t // CHUNK
    terms = _gdn_term_shapes(nc)

    def body(gq, gk, gv, gt, wq, wk, wv, *rest):
        h = pl.program_id(0)
        intra, chunked, pads = rest[:N_TERMS], rest[N_TERMS:N_TERMS + 5], rest[N_TERMS + 5:]
        _gdn_pad((gq, gk, gv), pads)
        _gdn_stage(pads, gt, _taps(wq, wk, wv), h, chunked)
        _gdn_intra_all(chunked, intra)

    qkv = [(nc, CHUNK, HD)] * 3
    outs = pl.pallas_call(
        body, grid=(NG,), name="gdn_fwd", in_specs=_gdn_in_specs(t),
        out_specs=[_per_head(sh, one_buffer=False) for sh in terms + qkv],
        out_shape=[SDS((NG,) + sh, F32) for sh in terms + qkv],
        scratch_shapes=_gdn_chunked_scratch(nc)[3:] + [pltpu.VMEM((t + HALO, HD), F32)] * 3, compiler_params=_cp("parallel"),
    )(pa, pa, pa, gates, conv, conv, conv)
    return list(outs[:N_TERMS]), list(outs[N_TERMS:])


def _gdn_scan(terms_in):
    nc = terms_in[0].shape[1]
    terms = _gdn_term_shapes(nc)

    def body(*refs):
        intra, o, states = refs[:N_TERMS], refs[N_TERMS], refs[N_TERMS + 1]

        def one(c, ss):
            rows = pl.ds(pl.multiple_of(c * CHUNK, CHUNK), CHUNK)
            loaded = [[r[hh, c] for r in intra] for hh in range(SCAN_HEADS)]
            res = [_gdn_step(ss[hh], *loaded[hh]) for hh in range(SCAN_HEADS)]
            for hh in range(SCAN_HEADS):
                states[hh, c] = ss[hh]
                o[rows, hh * HD:(hh + 1) * HD] = res[hh][0]
            return tuple(r[1] for r in res)

        per_trip = math.gcd(nc, SCAN_UNROLL)

        def step(i, ss):
            for k in range(per_trip):
                ss = one(per_trip * i + k, ss)
            return ss

        lax.fori_loop(0, nc // per_trip, step, tuple(jnp.zeros((HD, HD), F32) for _ in range(SCAN_HEADS)))

    return pl.pallas_call(
        body, grid=(NG // SCAN_HEADS,), name="gdn_scan", in_specs=[_per_head(sh, SCAN_HEADS) for sh in terms],
        out_specs=[pl.BlockSpec((nc * CHUNK, SCAN_HEADS * HD), lambda h: (0, h), pipeline_mode=ONE_BUFFER),
                   _per_head((nc, HD, HD), SCAN_HEADS)],
        out_shape=[SDS((nc * CHUNK, NG * HD), F32), SDS((NG, nc, HD, HD), F32)], compiler_params=_cp("parallel"),
    )(*terms_in)


def _gdn_bwd_scan(saved, do_raw):
    nc = saved[0].shape[1]
    terms = _gdn_term_shapes(nc)

    def body(*refs):
        intra, states, do, outs = refs[:N_TERMS], refs[N_TERMS], refs[N_TERMS + 1], refs[N_TERMS + 2:]

        def one(c, dss):
            rows = pl.ds(pl.multiple_of(c * CHUNK, CHUNK), CHUNK)
            loaded = [[states[hh, c]] + [r[hh, c] for r in intra] for hh in range(SCAN_HEADS)]
            cts = [do[rows, hh * HD:(hh + 1) * HD] for hh in range(SCAN_HEADS)]
            grads = [jax.vjp(_gdn_step, *loaded[hh])[1]((cts[hh], dss[hh])) for hh in range(SCAN_HEADS)]
            for hh in range(SCAN_HEADS):
                for r, gval in zip(outs, grads[hh][1:]):
                    r[hh, c] = gval
            return tuple(g[0] for g in grads)

        per_trip = math.gcd(nc, SCAN_UNROLL)

        def bwd(i, dss):
            c = nc - 1 - per_trip * i
            for k in range(per_trip):
                dss = one(c - k, dss)
            return dss

        lax.fori_loop(0, nc // per_trip, bwd, tuple(jnp.zeros((HD, HD), F32) for _ in range(SCAN_HEADS)))

    return pl.pallas_call(
        body, grid=(NG // SCAN_HEADS,), name="gdn_bwd_scan",
        in_specs=[_per_head(sh, SCAN_HEADS) for sh in terms] + [_per_head((nc, HD, HD), SCAN_HEADS)]
        + [pl.BlockSpec((nc * CHUNK, SCAN_HEADS * HD), lambda h: (0, h), pipeline_mode=ONE_BUFFER)],
        out_specs=[_per_head(sh, SCAN_HEADS) for sh in terms],
        out_shape=[SDS((NG,) + sh, F32) for sh in terms], compiler_params=_cp("parallel"),
    )(*saved, do_raw)


def _gdn_bwd(pa, gates, conv, dterms, qkv):
    t = pa.shape[0]
    nc = t // CHUNK
    terms = _gdn_term_shapes(nc)

    def body(*refs):
        gq, gk, gv, gt, wq, wk, wv = refs[:7]
        dintra, qkv = refs[7:7 + N_TERMS], refs[7 + N_TERMS:10 + N_TERMS]
        dgq, dgk, dgv, dgt, dwq, dwk, dwv = refs[10 + N_TERMS:17 + N_TERMS]
        chunked, pads, dpads, dgt_s = (refs[17 + N_TERMS:22 + N_TERMS], refs[22 + N_TERMS:25 + N_TERMS],
                                       refs[25 + N_TERMS:28 + N_TERMS], refs[28 + N_TERMS])
        h = pl.program_id(0)
        taps = _taps(wq, wk, wv)
        _gdn_pad((gq, gk, gv), pads)
        rows = _prep_rows(t)
        per = rows // CHUNK

        def gates_tile(i, carry):
            gtile = gt[pl.ds(pl.multiple_of(i * rows, rows), rows), :]
            chunked[3][pl.ds(i * per, per)] = _lane_pick(gtile, L_GA + h).reshape(per, CHUNK, 1)
            chunked[4][pl.ds(i * per, per)] = _lane_pick(gtile, L_GB + h).reshape(per, CHUNK, 1)
            return carry

        lax.fori_loop(0, t // rows, gates_tile, 0)
        grp_n = math.gcd(nc, GROUP)

        def grp(i, carry):
            sl = pl.ds(pl.multiple_of(i * grp_n, grp_n), grp_n)
            _, vjp = jax.vjp(_gdn_intra, *[r[sl] for r in qkv], chunked[3][sl], chunked[4][sl])
            for r, gval in zip(chunked, vjp(tuple(r[sl] for r in dintra))):
                r[sl] = gval
            return carry

        lax.fori_loop(0, nc // grp_n, grp, 0)

        for r in dpads:
            r[...] = jnp.zeros_like(r)

        def tile(i, dtaps):
            r0 = pl.multiple_of(i * rows, rows)
            win = pl.ds(r0, rows + HALO)
            _, vjp = jax.vjp(lambda *a: _gdn_prep(*a, h), *[p[win, :] for p in pads], gt[pl.ds(r0, rows), :], taps)
            grads = vjp(tuple(r[pl.ds(i * per, per)].reshape(rows, r.shape[-1]) for r in chunked))
            for r, gval in zip(dpads, grads[:3]):
                r[win, :] += gval
            dgt_s[pl.ds(r0, rows), :] = grads[3]
            return jax.tree.map(jnp.add, dtaps, grads[4])

        dtaps = lax.fori_loop(0, t // rows, tile, (jnp.zeros((1, HD), F32),) * 12)
        for r, dpad in zip((dgq, dgk, dgv), dpads):
            r[...] = dpad[HALO:, :].astype(r.dtype)
        for j, r in enumerate((dwq, dwk, dwv)):
            for k in range(4):
                r[k:k + 1, :] = dtaps[4 * j + k]

        @pl.when(h == 0)
        def _():
            dgt[...] = jnp.zeros_like(dgt)

        dgt[...] += dgt_s[...]

    head = _head(t)
    taps = pl.BlockSpec((4, HD), lambda h: (0, h))
    return pl.pallas_call(
        body, grid=(NG,), name="gdn_bwd",
        in_specs=_gdn_in_specs(t) + [_per_head(sh) for sh in terms + [(nc, CHUNK, HD)] * 3],
        out_specs=[head, head, head, _small(t), taps, taps, taps],
        out_shape=[SDS((t, NG * HD), BF16)] * 3 + [SDS((t, HD), F32)] + [SDS((4, NG * HD), F32)] * 3,
        scratch_shapes=_gdn_chunked_scratch(nc) + [pltpu.VMEM((t + HALO, HD), F32)] * 6 + [pltpu.VMEM((t, HD), F32)],
        compiler_params=_cp("arbitrary"),
    )(pa, pa, pa, gates, conv, conv, conv, *dterms, *qkv)


def _gdn_post(o, z, gain):
    return (jnp.concatenate(
        [_rms(o[:, h * HD:(h + 1) * HD], gain) * _silu(z[:, h * HD:(h + 1) * HD]) for h in range(NG)], axis=1),)


def _place():
    return lax.axis_index("x"), lax.axis_index("y"), lax.axis_index("c")


def _sum_blocks(name, parts):
    _, r, c = parts.shape
    tr = 64 if r % 64 == 0 else r

    def body(x, o):
        acc = x[0].astype(F32)
        for d in range(1, N_DEV):
            acc = acc + x[d].astype(F32)
        o[...] = acc

    return pl.pallas_call(
        body, grid=(r // tr,), name=name, in_specs=[pl.BlockSpec((N_DEV, tr, c), lambda i: (0, i, 0))],
        out_specs=pl.BlockSpec((tr, c), lambda i: (i, 0)), out_shape=SDS((r, c), F32), compiler_params=_cp("parallel"),
    )(parts)


def _all_reduce_small(name, x):
    m_per, n = x.shape

    def body(x_ref, out_ref, send_sems, recv_sems, local_sem):
        px, py, pc = _place()
        me, sibling = (px, py, pc), (px, py, 1 - pc)
        chips = [(1 - px, py), (px, 1 - py), (1 - px, 1 - py)]
        buf = out_ref

        def rows(qx, qy, qc):
            return buf.at[pl.ds((4 * qx + 2 * qy + qc) * m_per, m_per), :]

        def copy(k, block, to, src=None):
            return pltpu.make_async_remote_copy(
                src_ref=rows(*block) if src is None else src, dst_ref=rows(*block),
                send_sem=send_sems.at[k], recv_sem=recv_sems.at[k], device_id=to, device_id_type=MESH)

        mine = pltpu.make_async_copy(x_ref, rows(*me), local_sem)
        mine.start()
        first = [copy(0, me, sibling, src=x_ref)]
        first += [copy(1 + j, me, (*chip, pc), src=x_ref) for j, chip in enumerate(chips)]
        for cp in first:
            cp.start()
        passed = [copy(4 + j, (*chip, pc), sibling) for j, chip in enumerate(chips)]
        for j, chip in enumerate(chips):
            copy(1 + j, (*chip, pc), me).wait_recv()
            passed[j].start()
        copy(0, sibling, me).wait_recv()
        for j, chip in enumerate(chips):
            copy(4 + j, (*chip, 1 - pc), me).wait_recv()
        for cp in first + passed:
            cp.wait_send()
        mine.wait()

    gathered = pl.pallas_call(
        body, name=name, out_shape=SDS((N_DEV * m_per, n), x.dtype),
        in_specs=[pl.BlockSpec(memory_space=pltpu.VMEM)], out_specs=pl.BlockSpec(memory_space=pltpu.VMEM),
        scratch_shapes=[pltpu.SemaphoreType.DMA((7,)), pltpu.SemaphoreType.DMA((7,)), pltpu.SemaphoreType.DMA],
    )(x)
    return _sum_blocks(name + "_sum", gathered.reshape(N_DEV, m_per, n))


HBM_SPEC = pl.BlockSpec(memory_space=pltpu.HBM)
SEM_SPEC = pl.BlockSpec(memory_space=pltpu.SEMAPHORE)
EFFECT = pltpu.SideEffectType.DATAFLOW_SIDE_EFFECTING


def _copies_start(name, bufs, n_remote, n_local, build, deps):
    nb, nd = len(bufs), len(deps)
    sem_shapes = [pltpu.SemaphoreType.DMA((n_remote,)), pltpu.SemaphoreType.DMA((n_remote,))]
    if n_local:
        sem_shapes.append(pltpu.SemaphoreType.DMA((n_local,)))
    ns = len(sem_shapes)

    def body(*refs):
        sems = refs[nb + nd:nb + nd + ns]
        remote, local = build(refs[:nb], *sems, *([None] * (3 - ns)))
        for cp in local + remote:
            cp.start()
        refs[-1][...] = jnp.zeros((8, HD), F32)

    outs = pl.pallas_call(
        body, name=name,
        out_shape=(*sem_shapes, *[pltpu.HBM(b.shape, b.dtype) for b in bufs], SDS((8, HD), F32)),
        in_specs=[HBM_SPEC] * nb + [ANY_SPEC] * nd,
        out_specs=(*[SEM_SPEC] * ns, *[HBM_SPEC] * nb, pl.BlockSpec(memory_space=pltpu.VMEM)),
        input_output_aliases={i: ns + i for i in range(nb)},
        compiler_params=pltpu.CompilerParams(has_side_effects=EFFECT),
    )(*[pltpu.with_memory_space_constraint(b, pltpu.HBM) for b in bufs], *deps)
    return list(outs[:ns]), list(outs[ns:ns + nb]), outs[-1]


def _copies_wait(name, bufs, sems, build, after):
    nb, ns = len(bufs), len(sems)

    def body(*refs):
        remote, local = build(refs[:nb], *refs[nb:nb + ns], *([None] * (3 - ns)))
        for cp in local:
            cp.wait()
        for cp in remote:
            cp.wait_send()
            cp.wait_recv()

    outs = pl.pallas_call(
        body, name=name, out_shape=tuple(pltpu.HBM(b.shape, b.dtype) for b in bufs),
        in_specs=[HBM_SPEC] * nb + [SEM_SPEC] * ns + [ANY_SPEC] * len(after), out_specs=tuple([HBM_SPEC] * nb),
        input_output_aliases={i: i for i in range(nb)},
        compiler_params=pltpu.CompilerParams(has_side_effects=EFFECT),
    )(*bufs, *sems, *after)
    return list(outs)


def _remote(src, dst, send, recv, k, to):
    return pltpu.make_async_remote_copy(src_ref=src, dst_ref=dst, send_sem=send.at[k], recv_sem=recv.at[k],
                                        device_id=to, device_id_type=MESH)


class _Gather:
    def __init__(self, name, shards, deps):
        self.name, self.n = name, len(shards)
        lands = [lax.empty((N_DEV,) + s.shape, s.dtype) for s in shards]
        self.sems1, bufs, self.token = _copies_start(
            name + "_s1", list(shards) + lands, 4 * self.n, self.n, self._stage1(range(self.n)), deps)
        self.shards, self.lands, self.sems2 = bufs[:self.n], bufs[self.n:], {}

    def _stage1(self, idxs):
        def build(refs, send, recv, loc):
            x, y, c = _place()
            me = 4 * x + 2 * y + c
            targets = [(x, y, 1 - c), (1 - x, y, c), (x, 1 - y, c), (1 - x, 1 - y, c)]
            remote, local = [], []
            for pos, i in enumerate(idxs):
                src, land = refs[pos], refs[len(idxs) + pos]
                local.append(pltpu.make_async_copy(src, land.at[me], loc.at[i]))
                remote += [_remote(src, land.at[me], send, recv, 4 * i + k, to) for k, to in enumerate(targets)]
            return remote, local
        return build

    @staticmethod
    def _stage2(refs, send, recv, loc):
        x, y, c = _place()
        remote = []
        for pos, land in enumerate(refs):
            for j, (cx, cy) in enumerate([(1 - x, y), (x, 1 - y), (1 - x, 1 - y)]):
                blk = land.at[4 * cx + 2 * cy + c]
                remote.append(_remote(blk, blk, send, recv, 3 * pos + j, (x, y, 1 - c)))
        return remote, []

    def pass_on(self, idxs, after):
        tag, m = "".join(map(str, idxs)), len(idxs)
        bufs = _copies_wait(f"{self.name}_w1_{tag}", [self.shards[i] for i in idxs] + [self.lands[i] for i in idxs],
                            self.sems1, self._stage1(idxs), after)
        self.sems2[tag], lands, token = _copies_start(f"{self.name}_s2_{tag}", bufs[m:], 3 * m, 0, self._stage2, ())
        for pos, i in enumerate(idxs):
            self.lands[i] = lands[pos]
        return [token]

    def get(self, idxs, after):
        tag = "".join(map(str, idxs))
        return _copies_wait(f"{self.name}_w2_{tag}", [self.lands[i] for i in idxs], self.sems2[tag], self._stage2, after)


class _RelayGather:
    def __init__(self, name, shards, deps):
        self.name, self.n = name, len(shards)
        lands = [lax.empty((N_DEV,) + s.shape, s.dtype) for s in shards]
        self.sems, bufs, self.token = _copies_start(name + "_s1", list(shards) + lands, 3 * self.n, self.n, self._stage1, deps)
        self.shards, self.lands = bufs[:self.n], bufs[self.n:]

    def _stage1(self, refs, send, recv, loc):
        x, y, c = _place()
        me = 4 * x + 2 * y + c
        remote, local = [], []
        for i in range(self.n):
            src, land = refs[i], refs[self.n + i]
            local.append(pltpu.make_async_copy(src, land.at[me], loc.at[i]))
            remote += [_remote(src, land.at[me], send, recv, 3 * i + k, to)
                       for k, to in enumerate([(x, y, 1 - c), (1 - x, y, c), (x, 1 - y, c)])]
        return remote, local

    @staticmethod
    def _relay(refs, send, recv, loc):
        x, y, c = _place()
        remote = []
        for i, land in enumerate(refs):
            half = land.shape[1] // 2
            from_x = land.at[4 * (1 - x) + 2 * y + c].at[pl.ds(0, half)]
            from_y = land.at[4 * x + 2 * (1 - y) + c].at[pl.ds(half, half)]
            remote += [_remote(from_x, from_x, send, recv, 2 * i, (x, 1 - y, c)),
                       _remote(from_y, from_y, send, recv, 2 * i + 1, (1 - x, y, c))]
        return remote, []

    def forward(self, after):
        bufs = _copies_wait(self.name + "_w1", self.shards + self.lands, self.sems, self._stage1, after)
        self.sems, self.lands, self.token = _copies_start(self.name + "_sf", bufs[self.n:], 2 * self.n, 0, self._relay, ())
        return [self.token]

    def pass_on(self, after):
        lands = _copies_wait(self.name + "_wf", self.lands, self.sems, self._relay, after)
        self.sems, self.lands, self.token = _copies_start(self.name + "_s2", lands, 3 * self.n, 0, _Gather._stage2, ())
        return [self.token]

    def get(self, after):
        return _copies_wait(self.name + "_w2", self.lands, self.sems, _Gather._stage2, after)


def _rows_tile(r, row_bytes, target=1 << 20):
    tr = r
    while tr % 32 == 0 and tr * row_bytes > target:
        tr //= 2
    return tr


def _pair_add(name, g, got, c):
    _, r, cols = g.shape
    tr = _rows_tile(r, cols * 2)

    def body(s, a, b, o):
        o[...] = (a[...].astype(F32) + b[...].astype(F32)).astype(o.dtype)

    return pl.pallas_call(
        body, name=name, out_shape=SDS((4, r, cols), g.dtype),
        grid_spec=pltpu.PrefetchScalarGridSpec(
            num_scalar_prefetch=1, grid=(4, r // tr),
            in_specs=[pl.BlockSpec((None, tr, cols), lambda j, i, s: (2 * j + s[0], i, 0)),
                      pl.BlockSpec((None, tr, cols), lambda j, i, s: (j, i, 0))],
            out_specs=pl.BlockSpec((None, tr, cols), lambda j, i, s: (j, i, 0))),
        compiler_params=_cp("parallel", "parallel"),
    )(c.reshape(1), g, got)


def _quad_sum(name, part, got, chip, wmv=None):
    _, r, cols = part.shape
    tr = _rows_tile(r, cols * 4)
    n_out = 4 if wmv else 1

    def body(s, a, b1, b2, b3, *rest):
        g = ((a[...].astype(F32) + b1[...].astype(F32)) + b2[...].astype(F32)) + b3[...].astype(F32)
        rest[-n_out][...] = g
        if wmv:
            w, m, v = rest[:3]
            rest[-3][...], rest[-2][...], rest[-1][...] = _adamw(w[...], g, m[...], v[...])

    blk = lambda k: pl.BlockSpec((None, tr, cols), lambda i, s, k=k: (jnp.bitwise_xor(s[0], k), i, 0))
    row = pl.BlockSpec((tr, cols), lambda i, s: (i, 0))
    outs = pl.pallas_call(
        body, name=name, out_shape=[SDS((r, cols), F32)] * n_out,
        grid_spec=pltpu.PrefetchScalarGridSpec(
            num_scalar_prefetch=1, grid=(r // tr,), in_specs=[blk(0), blk(1), blk(2), blk(3)] + [row] * (n_out - 1),
            out_specs=[row] * n_out),
        compiler_params=_cp("parallel"),
    )(chip.reshape(1), part, got, got, got, *(wmv or ()))
    return tuple(outs) if wmv else outs[0]


class _Scatter:
    def __init__(self, name, grads, deps):
        self.name, self.n = name, len(grads)
        got = [lax.empty((4,) + g.shape[1:], g.dtype) for g in grads]
        self.sems, bufs, self.token = _copies_start(name + "_s1", list(grads) + got, 4 * self.n, 0, self._stage1, deps)
        self.grads, self.got = bufs[:self.n], bufs[self.n:]

    def _stage1(self, refs, send, recv, loc):
        x, y, c = _place()
        remote = []
        for i in range(self.n):
            remote += [_remote(refs[i].at[2 * j + 1 - c], refs[self.n + i].at[j], send, recv, 4 * i + j, (x, y, 1 - c))
                       for j in range(4)]
        return remote, []

    def _stage2(self, refs, send, recv, loc):
        x, y, c = _place()
        remote = []
        for i in range(self.n):
            for k in (1, 2, 3):
                tx = 1 - x if k & 2 else x
                ty = 1 - y if k & 1 else y
                remote.append(_remote(refs[i].at[2 * tx + ty], refs[self.n + i].at[2 * x + y], send, recv,
                                      3 * i + k - 1, (tx, ty, c)))
        return remote, []

    def mid(self, after):
        bufs = _copies_wait(self.name + "_w1", self.grads + self.got, self.sems, self._stage1, after)
        c = lax.axis_index("c").astype(jnp.int32)
        parts = [_pair_add(f"{self.name}_add{i}", bufs[i], bufs[self.n + i], c) for i in range(self.n)]
        got = [lax.empty(p.shape, p.dtype) for p in parts]
        self.sems, bufs, self.token = _copies_start(self.name + "_s2", parts + got, 3 * self.n, 0, self._stage2, ())
        self.parts, self.got = bufs[:self.n], bufs[self.n:]

    def end(self, after, wmv=None):
        bufs = _copies_wait(self.name + "_w2", self.parts + self.got, self.sems, self._stage2, after)
        chip = (2 * lax.axis_index("x") + lax.axis_index("y")).astype(jnp.int32)
        wmv = wmv or [None] * self.n
        return [_quad_sum(f"{self.name}_sum{i}", bufs[i], bufs[self.n + i], chip, wmv[i]) for i in range(self.n)]


def _adamw(w, g, m, v):
    m = ADAM_B1 * m + (1.0 - ADAM_B1) * g
    v = ADAM_B2 * v + (1.0 - ADAM_B2) * (g * g)
    m_hat = m / (1.0 - ADAM_B1 ** ADAM_STEP)
    v_hat = v / (1.0 - ADAM_B2 ** ADAM_STEP)
    return -ADAM_LR * (m_hat / (jnp.sqrt(v_hat) + ADAM_EPS) + ADAM_WD * w), m, v


def _adamw_call(name, w, g, m, v):
    r, c = w.shape
    tm = 64 if r % 64 == 0 else r
    return _rowwise(name, _adamw, [w, g, m, v], [], [(c, F32)] * 3, tm)


_IN_COLS = 5906


def _perm_in(w):
    pad = jnp.zeros((w.shape[0], 2 * HALF - _IN_COLS), w.dtype)
    return (jnp.concatenate([w[:, 2310:4614], w[:, 4614:5382]], axis=1),
            jnp.concatenate([w[:, :2304], w[:, 5394:5906], w[:, 2304:2310], w[:, 5382:5394], pad], axis=1))


def _unperm_in(ga, gb):
    return jnp.concatenate([gb[:, :2304], gb[:, 2816:2822], ga[:, :2304], ga[:, 2304:3072], gb[:, 2822:2834],
                            gb[:, 2304:2816]], axis=1)


def _lanes(v, at):
    return jnp.pad(v, ((0, 0), (at, HD - at - v.shape[1])))


_PACK = ("norm_mix", "mem_norm", "norm_ffn", "gdn_conv", "fox_q_norm", "fox_k_norm", "gdn_out_norm", "mem_q_norm",
         "mem_k_norm", "fox_f_bias", "gdn_a_log", "gdn_dt_bias", "loss")


def _pack(vals):
    parts = [vals[n].reshape(-1, HD) for n in _PACK]
    used = sum(p.shape[0] for p in parts)
    buf = jnp.concatenate(parts + [jnp.zeros((-used % 8, HD), F32)], axis=0)
    return buf, [(n, p.shape[0]) for n, p in zip(_PACK, parts)]


def _unpack(buf, layout):
    out, at = {}, 0
    for n, rows in layout:
        out[n] = buf[at:at + rows]
        at += rows
    return out


def kernel(x, mem, norm_mix, w_in, fox_f_bias, fox_q_norm, fox_k_norm, gdn_conv, gdn_a_log, gdn_dt_bias, gdn_out_norm, mem_norm, w_mem_kv, mem_q_norm, mem_k_norm, w_out, norm_ffn, w_gate_up, w_down, loss_target, m_norm_mix, m_w_in, m_fox_f_bias, m_fox_q_norm, m_fox_k_norm, m_gdn_conv, m_gdn_a_log, m_gdn_dt_bias, m_gdn_out_norm, m_mem_norm, m_w_mem_kv, m_mem_q_norm, m_mem_k_norm, m_w_out, m_norm_ffn, m_w_gate_up, m_w_down, v_norm_mix, v_w_in, v_fox_f_bias, v_fox_q_norm, v_fox_k_norm, v_gdn_conv, v_gdn_a_log, v_gdn_dt_bias, v_gdn_out_norm, v_mem_norm, v_w_mem_kv, v_mem_q_norm, v_mem_k_norm, v_w_out, v_norm_ffn, v_w_gate_up, v_w_down):
    args = dict(locals())
    d = x.shape[2]
    me = 4 * lax.axis_index("x") + 2 * lax.axis_index("y") + lax.axis_index("c")

    cshard = gdn_conv[0].shape[1]
    conv_pad = jnp.pad(gdn_conv[0], ((0, 4), (0, 3 * HD - cshard)))
    w_in_a, w_in_b = _perm_in(w_in[0])
    wmv = lambda n: (args[n][0], args["m_" + n][0], args["v_" + n][0])
    comm = _StepComm(w_in_b.astype(BF16), {"in_a": [w_in_a.astype(BF16), conv_pad],
                                           "kv_out": [w_mem_kv[0].astype(BF16), w_out[0].astype(BF16)]},
                     w_gate_up[0].astype(BF16), w_down[0].astype(BF16), (),
                     {"ffn": [wmv("w_down"), wmv("w_gate_up")], "a": [None, wmv("w_out"), wmv("w_mem_kv")], "b": [None]})

    grad_x, loss_local, small_grads = _local_step(
        x[0], mem[0], loss_target[0], norm_mix, fox_f_bias, fox_q_norm, fox_k_norm, gdn_a_log, gdn_dt_bias,
        gdn_out_norm, mem_norm, mem_q_norm, mem_k_norm, norm_ffn, cshard, comm)

    red = comm.finish([grad_x])
    updated = {"w_down": red["ffn"][0], "w_gate_up": red["ffn"][1], "w_out": red["a"][1], "w_mem_kv": red["a"][2]}
    grads = {n: r[0] for n, r in updated.items()}
    grads["w_in"] = _unperm_in(red["a"][0], red["b"][0])
    small_grads["loss"] = jnp.broadcast_to(loss_local, (1, HD))
    packed, layout = _pack(small_grads)
    small = _unpack(_all_reduce_small("ar_small", packed), layout)
    loss = small["loss"][0, 0]
    six = {"fox_f_bias": L_FF, "gdn_a_log": L_GA, "gdn_dt_bias": L_GA}
    for n, rows_n in layout[:-1]:
        gsm = small[n]
        if n == "gdn_conv":
            gsm = lax.dynamic_slice(gsm.reshape(4, N_DEV * cshard), (0, me * cshard), (4, cshard))[None]
        elif n in six:
            gsm = gsm[:, six[n]:six[n] + 6]
        else:
            gsm = gsm.reshape(1, rows_n * HD)
        grads[n] = gsm

    names = ['norm_mix', 'w_in', 'fox_f_bias', 'fox_q_norm', 'fox_k_norm', 'gdn_conv', 'gdn_a_log', 'gdn_dt_bias',
             'gdn_out_norm', 'mem_norm', 'w_mem_kv', 'mem_q_norm', 'mem_k_norm', 'w_out', 'norm_ffn', 'w_gate_up', 'w_down']
    big = ("w_in", "w_mem_kv", "w_out", "w_gate_up", "w_down")
    delta, new_m, new_v = {}, {}, {}
    for n in big:
        res = updated[n][1:] if n in updated else _adamw_call("adamw_" + n, args[n][0], grads[n], *wmv(n)[1:])
        delta[n], new_m[n], new_v[n] = [a[None] for a in res]
        grads[n] = grads[n][None]

    def flat(a):
        a = a.reshape(1, -1)
        return jnp.pad(a, ((0, 0), (0, -a.shape[1] % HD))).reshape(-1, HD)

    smalls = [n for n in names if n not in big]
    pk = lambda pre: jnp.concatenate([flat(grads[n] if pre == "g" else args[pre + n]) for n in smalls], axis=0)
    cat = [pk(""), pk("g"), pk("m_"), pk("v_")]
    padr = -cat[0].shape[0] % 8
    cat = [jnp.pad(a, ((0, padr), (0, 0))) for a in cat]
    res = _adamw_call("adamw_small", *cat)
    at = 0
    for n in smalls:
        shape = args[n].shape
        size = math.prod(shape)
        nrow = -(-size // HD)
        for dst, src in zip((delta, new_m, new_v), res):
            dst[n] = src[at:at + nrow].reshape(-1)[:size].reshape(shape)
        at += nrow

    return (loss, grad_x[None], *[grads[n] for n in names], *[delta[n] for n in names],
            *[new_m[n] for n in names], *[new_v[n] for n in names])


class _StepComm:
    def __init__(self, first, shard_groups, w_gate_up, w_down, after, wmv):
        self.wmv, self.done = wmv, {}
        self.first = _RelayGather("ag_first", [first], after)
        self.groups, self.shards = {}, []
        for key, ws in shard_groups.items():
            self.groups[key] = list(range(len(self.shards), len(self.shards) + len(ws)))
            self.shards += list(ws)
        self.w_gate_up, self.w_down = w_gate_up, w_down
        self.passed, self.scatters = set(), {}

    def start_deps(self):
        return [self.first.token]

    def first_weights(self, after):
        deps = self.first.forward(after)
        self.gather = _Gather("ag", self.shards, deps)
        self.relay = _RelayGather("ag_gu", [self.w_gate_up], [self.gather.token])
        return self.first.get(self.first.pass_on([self.relay.token]))

    def relay_forward(self, after):
        deps = self.relay.forward(after)
        self.gather_down = _Gather("ag_dn", [self.w_down], deps)
        return [self.gather_down.token]

    def pass_on(self, key, after):
        self.passed.add(key)
        if key == "gate_up":
            return self.relay.pass_on(after)
        return self.gather.pass_on(self.groups[key], after)

    def weights(self, key, after):
        if key == "down":
            return self.gather_down.get([0], self.gather_down.pass_on([0], after))
        if key not in self.passed:
            after = self.pass_on(key, after)
        return self.relay.get(after) if key == "gate_up" else self.gather.get(self.groups[key], after)

    def send(self, tag, grads):
        blocks = [g if g.ndim == 3 else g.reshape(N_DEV, g.shape[0] // N_DEV, g.shape[1]) for g in grads]
        self.scatters[tag] = _Scatter("rs_" + tag, blocks, ())
        return [self.scatters[tag].token]

    def mid(self, tag, after):
        self.scatters[tag].mid(after)
        return [self.scatters[tag].token]

    def finish_group(self, tag, after):
        self.done[tag] = self.scatters.pop(tag).end(after, self.wmv[tag])
        first = self.done[tag][0]
        return [first[0] if isinstance(first, tuple) else first]

    def finish(self, after):
        for tag in list(self.scatters):
            self.finish_group(tag, after)
        return self.done


def _local_step(xs, ms, tgt, norm_mix, fox_f_bias, fox_q_norm, fox_k_norm, gdn_a_log, gdn_dt_bias, gdn_out_norm,
                mem_norm, mem_q_norm, mem_k_norm, norm_ffn, cshard, comm):
    t, d = xs.shape
    bq = min(t, 256)
    fb, alog, dtb = _lanes(fox_f_bias, L_FF), _lanes(gdn_a_log, L_GA), _lanes(gdn_dt_bias, L_GA)
    flat = lambda w: w.reshape(-1, w.shape[-1])

    rms1 = lambda a, g: (_rms(a, g),)
    (u,) = _rowwise("norm_mix", rms1, [xs], [norm_mix], [(d, BF16)], min(t, 256), deps=comm.start_deps())
    w_in_b = flat(comm.first_weights([u])[0])
    pb = _matmul("proj_in_b", u, w_in_b, NN, F32, 1024, 768)
    o_fox = _fox_fwd(pb, fb, fox_q_norm, fox_k_norm, bq)
    w_in_a, conv_parts = comm.weights("in_a", [o_fox])
    w_in_a = flat(w_in_a)
    conv_all = conv_parts[:, :4, :cshard].transpose(1, 0, 2).reshape(4, N_DEV * cshard)
    pa = _matmul("proj_in_a", u, w_in_a, NN, F32, 1024, 768)
    smrow = (pb, HD, SM)
    (gates,) = _rowwise("gdn_gates", _gdn_gates, [smrow], [alog, dtb], [(HD, F32)], min(t, 256))
    gdn_terms, gdn_qkv = _gdn_fwd(pa, gates, conv_all)
    o_gdn_raw, gdn_states = _gdn_scan(gdn_terms)
    gdn_saved = list(gdn_terms) + [gdn_states]
    deps = comm.relay_forward([o_gdn_raw])
    zrow = (pa, NG * HD, GZ * HD // (NG * HD))
    (o_gdn,) = _rowwise("gdn_post", _gdn_post, [o_gdn_raw, zrow], [gdn_out_norm], [(NG * HD, BF16)], min(t, 256),
                        deps=deps)
    w_kv_all, w_out_all = [flat(w) for w in comm.weights("kv_out", [o_gdn])]
    (mem_n,) = _rowwise("norm_mem", rms1, [ms], [mem_norm], [(d, BF16)], ms.shape[0])
    mkv = _matmul("proj_mem", mem_n, w_kv_all, NN, F32, 256, 512)
    o_mem = _mem_fwd(pb, mkv, mem_q_norm, mem_k_norm)
    deps = comm.pass_on("gate_up", [o_mem])
    mix = jnp.concatenate([o_fox, o_gdn, o_mem], axis=1)
    h1, h1n = _proj_out_norm(mix, w_out_all, xs, norm_ffn, deps)
    (wgu,) = comm.weights("gate_up", [h1n])
    ffw = wgu.shape[2]
    gu, act = _ffn_up(h1n, wgu.reshape(2, 4, d, ffw))
    w_down_all = flat(comm.weights("down", [act])[0])
    dyb, lsum = _ffn_down_loss(act, w_down_all, h1, tgt)
    loss_local = (0.5 / d) * jnp.sum(lsum[::8, ::HD])

    dgu = _ffn_down_bwd(dyb, w_down_all.reshape(4, ffw, d), gu).reshape(8, t, ffw)
    g_w_down = _matmul("grad_w_down", act, dyb, TN, BF16, 512, 2048)
    g_w_gu = _ffn_up_bwd_w(h1n, dgu)
    deps = comm.send("ffn", [g_w_down, g_w_gu])
    rms2 = lambda a, g: (_rms(a, g), a)
    dh1b, g_norm_ffn = _ffn_up_bwd_x(dgu, wgu, h1, norm_ffn, dyb, deps)

    dmix = _matmul("proj_out_bwd_x", dh1b, w_out_all, NT, BF16, 1024, 1024)
    g_w_out = _matmul("grad_w_out", mix, dh1b, TN, BF16, 1024, 2048)
    deps = comm.mid("ffn", [dmix, g_w_out])
    dmq, dmk, dmv, g_mqn, g_mkn = _mem_bwd(pb, mkv, mem_q_norm, mem_k_norm, dmix, deps=deps)
    dmkv = jnp.concatenate([dmk, dmv], axis=1).astype(BF16)
    g_w_kv = _matmul("grad_w_kv", mem_n, dmkv, TN, BF16, 512, 512)
    do_raw, dgz, g_gon = _rowwise_vjp("gdn_post_bwd", _gdn_post, [o_gdn_raw, zrow], [gdn_out_norm],
                                      [(dmix, NG * HD, 1)], [F32, BF16], min(t, 256), deps=deps)
    dterms = _gdn_bwd_scan(gdn_saved, do_raw)
    dgq, dgk, dgv, dgates, dwq, dwk, dwv = _gdn_bwd(pa, gates, conv_all, dterms, gdn_qkv)
    dsm_gdn, g_alog, g_dtb = _rowwise_vjp("gdn_gates_bwd", _gdn_gates, [smrow], [alog, dtb], [dgates], [F32], min(t, 256))
    dp_a = jnp.concatenate([dgq, dgk, dgv, dgz], axis=1)
    g_w_in_a = _matmul("grad_w_in_a", u, dp_a, TN, BF16, 512, 3072)
    deps = comm.send("a", [g_w_in_a, g_w_out, g_w_kv])
    du_a = _matmul("proj_in_bwd_a", dp_a, w_in_a, NT, F32, 1024, 1024, deps=deps)
    deps = comm.mid("a", [du_a])
    dfq, dfk, dfv, dsm_fox, g_fb, g_fqn, g_fkn = _fox_bwd(pb, fb, fox_q_norm, fox_k_norm, dmix, 2 * bq if t % (2 * bq) == 0 else bq,
                                                          deps=deps)
    dp_b = jnp.concatenate([dfq, dfk, dfv, dmq, (dsm_fox + dsm_gdn).astype(BF16), jnp.zeros((t, HD), BF16)], axis=1)
    g_w_in_b = _matmul("grad_w_in_b", u, dp_b, TN, BF16, 512, 3072)
    deps = comm.mid("b", comm.finish_group("ffn", comm.send("b", [g_w_in_b])))
    dmem_n = _matmul("proj_mem_bwd_x", dmkv, w_kv_all, NT, F32, 256, 512, deps=deps)
    g_mem_norm = _rowwise_vjp("norm_mem_bwd", rms1, [ms], [mem_norm], [dmem_n], [], ms.shape[0])[0]
    grad_x, g_norm_mix = _proj_in_bwd_norm(dp_b, w_in_b, du_a, xs, norm_mix, dh1b, [g_mem_norm])

    small_grads = {
        "norm_mix": g_norm_mix, "mem_norm": g_mem_norm, "norm_ffn": g_norm_ffn,
        "gdn_conv": jnp.concatenate([dwq, dwk, dwv], axis=1),
        "fox_q_norm": g_fqn, "fox_k_norm": g_fkn, "gdn_out_norm": g_gon, "mem_q_norm": g_mqn, "mem_k_norm": g_mkn,
        "fox_f_bias": g_fb, "gdn_a_log": g_alog, "gdn_dt_bias": g_dtb}
    return grad_x, loss_local, small_grads
```

```python
import functools
import math

import jax
import jax.numpy as jnp
from jax import lax
from jax.experimental import pallas as pl
from jax.experimental.pallas import tpu as pltpu

F32 = jnp.float32
BF16 = jnp.bfloat16
SDS = jax.ShapeDtypeStruct

N_DEV = 8
HD = 128
NF, NG, NM = 6, 6, 4
CHUNK = 64
GROUP = 16
NORM_EPS = 1e-6
GQ, GK, GV, GZ = 0, 6, 12, 18
FQ, FK, FV, MQ, SM = 0, 6, 12, 18, 22
HALF = 24 * HD
L_FF, L_GA, L_GB = 0, 6, 12
VMEM_LIMIT = 56 * 1024 * 1024

ADAM_LR, ADAM_B1, ADAM_B2, ADAM_EPS, ADAM_WD, ADAM_STEP = 0.001, 0.9, 0.999, 1e-08, 0.01, 10

NN = (((1,), (0,)), ((), ()))
NT = (((1,), (1,)), ((), ()))
TN = (((0,), (0,)), ((), ()))
MESH = pl.DeviceIdType.MESH


def _cp(*sem):
    return pltpu.CompilerParams(dimension_semantics=tuple(sem) if sem else None, vmem_limit_bytes=VMEM_LIMIT)


def _dot(a, b, dims=NN):
    return lax.dot_general(a, b, dims, preferred_element_type=F32)


def _iota(shape, axis):
    return lax.broadcasted_iota(jnp.int32, shape, axis)


def _rms(x, gain):
    return x * lax.rsqrt(jnp.mean(x * x, axis=-1, keepdims=True) + NORM_EPS) * gain


def _sigmoid(x):
    return 0.5 * jnp.tanh(0.5 * x) + 0.5


def _silu(x):
    return x * _sigmoid(x)


def _softplus(x):
    return jnp.maximum(x, 0.0) + jnp.log(1.0 + jnp.exp(-jnp.abs(x)))


def _lane_pick(x, lane):
    oh = (_iota((1, x.shape[-1]), 1) == lane).astype(F32)
    return jnp.sum(x * oh, axis=-1, keepdims=True)


def _cumsum_rows(x):
    tril = (_iota((HD, HD), 0) >= _iota((HD, HD), 1)).astype(F32)
    carry = jnp.zeros((1, x.shape[1]), F32)
    outs = []
    for b in range(x.shape[0] // HD):
        blk = x[b * HD:(b + 1) * HD]
        outs.append(_pdot(tril, blk, "nn", "xa") + carry)
        carry = carry + jnp.sum(blk, axis=0, keepdims=True)
    return jnp.concatenate(outs, axis=0)


def _row_spec(r, tm):
    if isinstance(r, tuple):
        arr, width, cb = r
        return arr, pl.BlockSpec((tm, width), lambda i, cb=cb: (i, cb))
    return r, pl.BlockSpec((tm, r.shape[1]), lambda i: (i, 0))


ANY_SPEC = pl.BlockSpec(memory_space=pl.ANY)


def _rowwise(name, fn, rows, consts, outs, tm, deps=()):
    arrs, specs = zip(*[_row_spec(r, tm) for r in rows])
    n_rows = arrs[0].shape[0]
    nr, nc, nd = len(rows), len(consts), len(deps)

    def body(*refs):
        res = fn(*[r[...] for r in refs[:nr + nc]])
        for o, v in zip(refs[nr + nc + nd:], res):
            o[...] = v.astype(o.dtype)

    return pl.pallas_call(
        body, grid=(n_rows // tm,), name=name,
        in_specs=list(specs) + [pl.BlockSpec(c.shape, lambda i: (0, 0)) for c in consts] + [ANY_SPEC] * nd,
        out_specs=[pl.BlockSpec((tm, w), lambda i: (i, 0)) for w, _ in outs],
        out_shape=[SDS((n_rows, w), dt) for w, dt in outs],
        compiler_params=_cp("parallel"),
    )(*arrs, *consts, *deps)


def _rowwise_vjp(name, fn, rows, consts, cts, grad_dtypes, tm, deps=()):
    arrs, specs = zip(*[_row_spec(r, tm) for r in rows])
    ct_arrs, ct_specs = zip(*[_row_spec(r, tm) for r in cts])
    n_rows = arrs[0].shape[0]
    nr, nc, nct, nd = len(rows), len(consts), len(cts), len(deps)
    plan = [(j, dt) for j, dts in enumerate(grad_dtypes) for dt in (dts if isinstance(dts, tuple) else (dts,))]
    ng = len(plan)
    widths = [specs[j].block_shape[1] for j, _ in plan]
    grad_dtypes = [dt for _, dt in plan]

    def body(*refs):
        vals = [r[...].astype(F32) for r in refs[:nr + nc]]
        ctv = tuple(r[...].astype(F32) for r in refs[nr + nc:nr + nc + nct])
        _, vjp = jax.vjp(fn, *vals)
        grads = vjp(ctv)
        outs = refs[nr + nc + nct + nd:]
        for o, (j, _) in zip(outs[:ng], plan):
            o[...] = grads[j].astype(o.dtype)

        @pl.when(pl.program_id(0) == 0)
        def _():
            for o in outs[ng:]:
                o[...] = jnp.zeros_like(o)

        for o, g in zip(outs[ng:], grads[nr:]):
            o[...] += g

    return pl.pallas_call(
        body, grid=(n_rows // tm,), name=name,
        in_specs=list(specs) + [pl.BlockSpec(c.shape, lambda i: (0, 0)) for c in consts] + list(ct_specs)
        + [ANY_SPEC] * nd,
        out_specs=[pl.BlockSpec((tm, w), lambda i: (i, 0)) for w in widths]
        + [pl.BlockSpec(c.shape, lambda i: (0, 0)) for c in consts],
        out_shape=[SDS((n_rows, w), dt) for w, dt in zip(widths, grad_dtypes)] + [SDS(c.shape, F32) for c in consts],
        compiler_params=_cp("arbitrary"),
    )(*arrs, *consts, *ct_arrs, *deps)


def _tile(n, pref):
    t = min(n, pref)
    while n % t or (t % HD and t != n):
        t -= 1
    return t


def _matmul(name, a, b, dims, out_dtype, tm, tn, residual=None, deps=()):
    ta, tb = dims == TN, dims == NT
    m = a.shape[1] if ta else a.shape[0]
    k = a.shape[0] if ta else a.shape[1]
    n = b.shape[0] if tb else b.shape[1]
    tm, tn = _tile(m, tm), _tile(n, tn)

    def body(*refs):
        acc = _dot(refs[0][...], refs[1][...], dims)
        if residual is not None:
            acc = acc + refs[2][...]
        refs[-1][...] = acc.astype(out_dtype)

    in_specs = [pl.BlockSpec((k, tm), lambda i, j: (0, i)) if ta else pl.BlockSpec((tm, k), lambda i, j: (i, 0)),
                pl.BlockSpec((tn, k), lambda i, j: (j, 0)) if tb else pl.BlockSpec((k, tn), lambda i, j: (0, j))]
    ops = [a, b]
    if residual is not None:
        in_specs.append(pl.BlockSpec((tm, tn), lambda i, j: (i, j)))
        ops.append(residual)
    in_specs += [ANY_SPEC] * len(deps)
    ops += list(deps)
    return pl.pallas_call(
        body, grid=(m // tm, n // tn), name=name, in_specs=in_specs,
        out_specs=pl.BlockSpec((tm, tn), lambda i, j: (i, j)), out_shape=SDS((m, n), out_dtype),
        compiler_params=_cp("parallel", "parallel"),
    )(*ops)


def _proj_out_norm(mix, w_out, xs, gain, deps):
    t, k = mix.shape
    d = w_out.shape[1]
    tm = _tile(t, 512)

    def body(*refs):
        a, b, x, g = refs[:4]
        h1, h1n = refs[4 + len(deps):]
        acc = _dot(a[...], b[...]) + x[...]
        h1[...] = acc
        h1n[...] = _rms(acc, g[...]).astype(BF16)

    return pl.pallas_call(
        body, grid=(t // tm,), name="proj_out",
        in_specs=[pl.BlockSpec((tm, k), lambda i: (i, 0)), pl.BlockSpec((k, d), lambda i: (0, 0)),
                  pl.BlockSpec((tm, d), lambda i: (i, 0)), pl.BlockSpec((1, d), lambda i: (0, 0))] + [ANY_SPEC] * len(deps),
        out_specs=[pl.BlockSpec((tm, d), lambda i: (i, 0))] * 2, out_shape=[SDS((t, d), F32), SDS((t, d), BF16)],
        compiler_params=_cp("parallel"),
    )(mix, w_out, xs, gain, *deps)


def _proj_in_bwd_norm(dp, w, du_a, xs, gain, dh1b, deps):
    t, k = dp.shape
    d = w.shape[0]
    tm = _tile(t, 256)

    def body(*refs):
        a, b, ua, x, g, dh = refs[:6]
        gx, dgain = refs[6 + len(deps):]
        _, vjp = jax.vjp(lambda xx, gn: _rms(xx, gn), x[...], g[...])
        dx, dg = vjp(_dot(a[...], b[...], NT) + ua[...])
        gx[...] = dx + dh[...].astype(F32)

        @pl.when(pl.program_id(0) == 0)
        def _():
            dgain[...] = jnp.zeros_like(dgain)

        dgain[...] += dg

    row = pl.BlockSpec((tm, d), lambda i: (i, 0))
    vec = pl.BlockSpec((1, d), lambda i: (0, 0))
    return pl.pallas_call(
        body, grid=(t // tm,), name="proj_in_bwd_b",
        in_specs=[pl.BlockSpec((tm, k), lambda i: (i, 0)), pl.BlockSpec((d, k), lambda i: (0, 0), pipeline_mode=ONE_BUFFER),
                  row, row, vec, row] + [ANY_SPEC] * len(deps),
        out_specs=[row, vec], out_shape=[SDS((t, d), F32), SDS((1, d), F32)], compiler_params=_cp("arbitrary"),
    )(dp, w, du_a, xs, gain, dh1b, *deps)


def _ffn_up(h1n, wgu):
    t, d = h1n.shape
    w = wgu.shape[3]
    tm = _tile(t, 512)

    def body(a, b, gu, act):
        x = a[...]
        g = _dot(x, b[0])
        u = _dot(x, b[1])
        gu[0] = g.astype(BF16)
        gu[1] = u.astype(BF16)
        act[...] = (_silu(g) * u).astype(BF16)

    return pl.pallas_call(
        body, grid=(4, t // tm), name="ffn_up",
        in_specs=[pl.BlockSpec((tm, d), lambda j, i: (i, 0)), pl.BlockSpec((2, None, d, w), lambda j, i: (0, j, 0, 0))],
        out_specs=[pl.BlockSpec((2, None, tm, w), lambda j, i: (0, j, i, 0)), pl.BlockSpec((tm, w), lambda j, i: (i, j))],
        out_shape=[SDS((2, 4, t, w), BF16), SDS((t, 4 * w), BF16)],
        compiler_params=_cp("parallel", "parallel"),
    )(h1n, wgu)


def _ffn_down_loss(act, wdown, h1, target):
    t, f = act.shape
    d = wdown.shape[1]
    tm, tn = _tile(t, 1024), _tile(d, 512)

    def body(a, b, h, tg, dyb, ls):
        e = _dot(a[...], b[...]) + h[...] - tg[...]
        dyb[...] = (e * (1.0 / d)).astype(BF16)
        ls[...] = jnp.broadcast_to(jnp.sum(e * e), (8, HD))

    return pl.pallas_call(
        body, grid=(t // tm, d // tn), name="ffn_down_loss",
        in_specs=[pl.BlockSpec((tm, f), lambda i, j: (i, 0)), pl.BlockSpec((f, tn), lambda i, j: (0, j)),
                  pl.BlockSpec((tm, tn), lambda i, j: (i, j)), pl.BlockSpec((tm, tn), lambda i, j: (i, j))],
        out_specs=[pl.BlockSpec((tm, tn), lambda i, j: (i, j)), pl.BlockSpec((8, HD), lambda i, j: (i, j))],
        out_shape=[SDS((t, d), BF16), SDS((8 * (t // tm), HD * (d // tn)), F32)],
        compiler_params=_cp("parallel", "parallel"),
    )(act, wdown, h1, target)


def _ffn_down_bwd(dyb, wdown4, gu):
    t, d = dyb.shape
    w = wdown4.shape[1]
    tm = _tile(t, 512)

    def body(a, b, gu_ref, out):
        da = _dot(a[...], b[...], NT)
        g = gu_ref[0].astype(F32)
        u = gu_ref[1].astype(F32)
        s = _sigmoid(g)
        out[0] = (da * u * (s * (1.0 + g * (1.0 - s)))).astype(BF16)
        out[1] = (da * g * s).astype(BF16)

    return pl.pallas_call(
        body, grid=(4, t // tm), name="ffn_down_bwd",
        in_specs=[pl.BlockSpec((tm, d), lambda j, i: (i, 0)), pl.BlockSpec((None, w, d), lambda j, i: (j, 0, 0)),
                  pl.BlockSpec((2, None, tm, w), lambda j, i: (0, j, i, 0))],
        out_specs=pl.BlockSpec((2, None, tm, w), lambda j, i: (0, j, i, 0)),
        out_shape=SDS((2, 4, t, w), BF16),
        compiler_params=_cp("parallel", "parallel"),
    )(dyb, wdown4, gu)


def _ffn_up_bwd_x(dgu, wgu, h1, gain, dyb, deps):
    _, t, w = dgu.shape
    d = wgu.shape[1]
    tm = _tile(t, 512)

    def body(*refs):
        a, b, h, g, dy = refs[:5]
        dh1, dgain, acc = refs[5 + len(deps):]
        i, j = pl.program_id(0), pl.program_id(1)

        @pl.when(j == 0)
        def _():
            acc[...] = jnp.zeros_like(acc)

        acc[...] += _dot(a[...], b[...], NT)

        @pl.when(j == N_DEV - 1)
        def _():
            _, vjp = jax.vjp(lambda x, gn: _rms(x, gn), h[...], g[...])
            dx, dg = vjp(acc[...])
            dh1[...] = (dx + dy[...].astype(F32)).astype(dh1.dtype)

            @pl.when(i == 0)
            def _():
                dgain[...] = jnp.zeros_like(dgain)

            dgain[...] += dg

    row = pl.BlockSpec((tm, d), lambda i, j: (i, 0))
    return pl.pallas_call(
        body, grid=(t // tm, N_DEV), name="ffn_up_bwd_x",
        in_specs=[pl.BlockSpec((None, tm, w), lambda i, j: (j, i, 0)), pl.BlockSpec((None, d, w), lambda i, j: (j, 0, 0)),
                  row, pl.BlockSpec((1, d), lambda i, j: (0, 0)), row] + [ANY_SPEC] * len(deps),
        out_specs=[row, pl.BlockSpec((1, d), lambda i, j: (0, 0))],
        out_shape=[SDS((t, d), BF16), SDS((1, d), F32)], scratch_shapes=[pltpu.VMEM((tm, d), F32)],
        compiler_params=_cp("arbitrary", "arbitrary"),
    )(dgu, wgu, h1, gain, dyb, *deps)


def _ffn_up_bwd_w(h1n, dgu):
    _, t, w = dgu.shape
    d = h1n.shape[1]
    tm = _tile(d, 512)

    def body(a, b, out):
        out[...] = _dot(a[...], b[...], TN).astype(BF16)

    return pl.pallas_call(
        body, grid=(8, d // tm), name="ffn_up_bwd_w",
        in_specs=[pl.BlockSpec((t, tm), lambda j, i: (0, i)), pl.BlockSpec((None, t, w), lambda j, i: (j, 0, 0))],
        out_specs=pl.BlockSpec((None, tm, w), lambda j, i: (j, i, 0)), out_shape=SDS((8, d, w), BF16),
        compiler_params=_cp("parallel", "parallel"),
    )(h1n, dgu)


def _fox_prep(fq, fk, sm, fb, qg, kg, h):
    qn = _rms(fq, qg)
    kn = _rms(fk, kg)
    c = _cumsum_rows(-_softplus(-(sm + fb)))
    ccol = _lane_pick(c, L_FF + h)
    crow = jnp.sum(c.T * (_iota((HD, 1), 0) == L_FF + h).astype(F32), axis=0, keepdims=True)
    return qn, kn, ccol, crow


def _softmax_times(s, v):
    e = jnp.exp(s - lax.stop_gradient(jnp.max(s, axis=1, keepdims=True)))
    return _dot(e.astype(BF16), v.astype(BF16)) * (1.0 / jnp.sum(e, axis=1, keepdims=True))


def _fox_block(q, k, v, cc, cr, off):
    bq = q.shape[0]
    assert k.shape[0] == off + bq
    s = _dot((q * (HD ** -0.5)).astype(BF16), k.astype(BF16), NT) + cc - cr
    diag = jnp.where(_iota((bq, bq), 1) <= _iota((bq, bq), 0), s[:, off:], -1e30)
    s = jnp.concatenate([s[:, :off], diag], axis=1) if off else diag
    return _softmax_times(s, v)


ONE_BUFFER = pl.Buffered(1)


def _pcol(t, cb):
    return pl.BlockSpec((t, HD), lambda h, cb=cb: (0, cb + h), pipeline_mode=ONE_BUFFER)


def _smcol(t):
    return pl.BlockSpec((t, HD), lambda h: (0, SM), pipeline_mode=ONE_BUFFER)


def _head(t):
    return pl.BlockSpec((t, HD), lambda h: (0, h), pipeline_mode=ONE_BUFFER)


def _small(n):
    return pl.BlockSpec((n, HD), lambda h: (0, 0), pipeline_mode=ONE_BUFFER)


def _fox_fwd(p, fb, qg, kg, bq):
    t = p.shape[0]

    def body(fq, fk, fv, sm, fb_r, qg_r, kg_r, o, qn_s, cc_s):
        h = pl.program_id(0)
        qn, kn, ccol, crow = _fox_prep(fq[...], fk[...], sm[...], fb_r[...], qg_r[...], kg_r[...], h)
        qn_s[...] = qn
        cc_s[...] = ccol
        knb = kn.astype(BF16)
        vb = fv[...].astype(BF16)
        for i in range(t // bq):
            rows, ext = pl.ds(i * bq, bq), (i + 1) * bq
            o[rows, :] = _fox_block(qn_s[rows, :], knb[:ext], vb[:ext], cc_s[rows, :], crow[:, :ext], i * bq).astype(o.dtype)

    return pl.pallas_call(
        body, grid=(NF,), name="fox_fwd",
        in_specs=[_pcol(t, FQ), _pcol(t, FK), _pcol(t, FV), _smcol(t), _small(1), _small(1), _small(1)],
        out_specs=_head(t), out_shape=SDS((t, NF * HD), BF16),
        scratch_shapes=[pltpu.VMEM((t, HD), F32), pltpu.VMEM((t, 1), F32)],
        compiler_params=_cp("parallel"),
    )(p, p, p, p, fb, qg, kg)


def _fox_bwd(p, fb, qg, kg, dmix, bq, deps=()):
    t = p.shape[0]

    def body(*refs):
        fq, fk, fv, sm, fb_r, qg_r, kg_r, do = refs[:8]
        dfq, dfk, dfv, dsm, dfb, dqg, dkg, qn_s, cc_s, dqn_s, dcc_s, dkn_s, dv_s, dcr_s = refs[8 + len(deps):]
        h = pl.program_id(0)
        qn, kn, ccol, crow = _fox_prep(fq[...], fk[...], sm[...], fb_r[...], qg_r[...], kg_r[...], h)
        qn_s[...] = qn
        cc_s[...] = ccol
        v = fv[...]
        dkn_s[...] = jnp.zeros_like(dkn_s)
        dv_s[...] = jnp.zeros_like(dv_s)
        dcr_s[...] = jnp.zeros_like(dcr_s)

        for i in range(t // bq):
            rows, ext = pl.ds(i * bq, bq), (i + 1) * bq
            _, vjp = jax.vjp(lambda a, b, c, d, e, off=i * bq: _fox_block(a, b, c, d, e, off),
                             qn_s[rows, :], kn[:ext], v[:ext], cc_s[rows, :], crow[:, :ext])
            dq, dk, dv, dcc, dcr = vjp(do[rows, :].astype(F32))
            dqn_s[rows, :] = dq
            dcc_s[rows, :] = dcc
            dkn_s[:ext, :] += dk
            dv_s[:ext, :] += dv
            dcr_s[:, :ext] += dcr
        _, prep_vjp = jax.vjp(lambda a, b, c, d, e, f: _fox_prep(a, b, c, d, e, f, h),
                              fq[...], fk[...], sm[...], fb_r[...], qg_r[...], kg_r[...])
        g_fq, g_fk, g_sm, g_fb, g_qg, g_kg = prep_vjp((dqn_s[...], dkn_s[...], dcc_s[...], dcr_s[...]))
        dfq[...] = g_fq.astype(dfq.dtype)
        dfk[...] = g_fk.astype(dfk.dtype)
        dfv[...] = dv_s[...].astype(dfv.dtype)

        @pl.when(h == 0)
        def _():
            for r in (dsm, dfb, dqg, dkg):
                r[...] = jnp.zeros_like(r)

        dsm[...] += g_sm
        dfb[...] += g_fb
        dqg[...] += g_qg
        dkg[...] += g_kg

    head = _head(t)
    return pl.pallas_call(
        body, grid=(NF,), name="fox_bwd",
        in_specs=[_pcol(t, FQ), _pcol(t, FK), _pcol(t, FV), _smcol(t), _small(1), _small(1), _small(1), head]
        + [ANY_SPEC] * len(deps),
        out_specs=[head, head, head, _small(t), _small(1), _small(1), _small(1)],
        out_shape=[SDS((t, NF * HD), BF16)] * 3 + [SDS((t, HD), F32)] + [SDS((1, HD), F32)] * 3,
        scratch_shapes=[pltpu.VMEM((t, HD), F32), pltpu.VMEM((t, 1), F32), pltpu.VMEM((t, HD), F32),
                        pltpu.VMEM((t, 1), F32), pltpu.VMEM((t, HD), F32), pltpu.VMEM((t, HD), F32),
                        pltpu.VMEM((1, t), F32)],
        compiler_params=_cp("arbitrary"),
    )(p, p, p, p, fb, qg, kg, dmix, *deps)


def _mem_attn(mq, mk, mv, qg, kg):
    s = _dot((_rms(mq, qg) * (HD ** -0.5)).astype(BF16), _rms(mk, kg).astype(BF16), NT)
    return _softmax_times(s, mv)


def _mem_fwd(p, mkv, qg, kg):
    t, ml = p.shape[0], mkv.shape[0]

    def body(mq, mk, mv, qg_r, kg_r, o):
        o[...] = _mem_attn(mq[...], mk[...], mv[...], qg_r[...], kg_r[...]).astype(o.dtype)

    return pl.pallas_call(
        body, grid=(NM,), name="mem_fwd",
        in_specs=[_pcol(t, MQ), pl.BlockSpec((ml, HD), lambda h: (0, h)), pl.BlockSpec((ml, HD), lambda h: (0, NM + h)),
                  _small(1), _small(1)],
        out_specs=pl.BlockSpec((t, HD), lambda h: (0, h)), out_shape=SDS((t, NM * HD), BF16),
        compiler_params=_cp("parallel"),
    )(p, mkv, mkv, qg, kg)


def _mem_bwd(p, mkv, qg, kg, dmix, deps=()):
    t, ml = p.shape[0], mkv.shape[0]

    def body(*refs):
        mq, mk, mv, qg_r, kg_r, do = refs[:6]
        dmq, dmk, dmv, dqg, dkg = refs[6 + len(deps):]
        _, vjp = jax.vjp(_mem_attn, mq[...], mk[...], mv[...], qg_r[...], kg_r[...])
        g_q, g_k, g_v, g_qg, g_kg = vjp(do[...].astype(F32))
        dmq[...] = g_q.astype(dmq.dtype)
        dmk[...] = g_k
        dmv[...] = g_v

        @pl.when(pl.program_id(0) == 0)
        def _():
            dqg[...] = jnp.zeros_like(dqg)
            dkg[...] = jnp.zeros_like(dkg)

        dqg[...] += g_qg
        dkg[...] += g_kg

    return pl.pallas_call(
        body, grid=(NM,), name="mem_bwd",
        in_specs=[_pcol(t, MQ), pl.BlockSpec((ml, HD), lambda h: (0, h)), pl.BlockSpec((ml, HD), lambda h: (0, NM + h)),
                  _small(1), _small(1), pl.BlockSpec((t, HD), lambda h: (0, NF + NG + h))] + [ANY_SPEC] * len(deps),
        out_specs=[pl.BlockSpec((t, HD), lambda h: (0, h)), pl.BlockSpec((ml, HD), lambda h: (0, h)),
                   pl.BlockSpec((ml, HD), lambda h: (0, h)), _small(1), _small(1)],
        out_shape=[SDS((t, NM * HD), BF16), SDS((ml, NM * HD), F32), SDS((ml, NM * HD), F32),
                   SDS((1, HD), F32), SDS((1, HD), F32)],
        compiler_params=_cp("arbitrary"),
    )(p, mkv, mkv, qg, kg, dmix, *deps)


def _shift_down(x, s):
    if s == 0:
        return x
    return jnp.where(_iota(x.shape, 0) >= s, pltpu.roll(x, s, 0), 0.0)


def _shift_up(x, s):
    if s == 0:
        return x
    n = x.shape[0]
    return jnp.where(_iota(x.shape, 0) < n - s, pltpu.roll(x, n - s, 0), 0.0)


@jax.custom_vjp
def _conv4(x, w0, w1, w2, w3):
    return w0 * _shift_down(x, 3) + w1 * _shift_down(x, 2) + w2 * _shift_down(x, 1) + w3 * x


def _conv4_fwd(x, w0, w1, w2, w3):
    return _conv4(x, w0, w1, w2, w3), (x, w0, w1, w2, w3)


def _conv4_bwd(res, dy):
    x, w0, w1, w2, w3 = res
    ups = [_shift_up(dy, 3 - k) for k in range(4)]
    dx = w0 * ups[0] + w1 * ups[1] + w2 * ups[2] + w3 * ups[3]
    return (dx,) + tuple(jnp.sum(up * x, axis=0, keepdims=True) for up in ups)


_conv4.defvjp(_conv4_fwd, _conv4_bwd)


HALO = 8


def _gdn_gates(sm, alog, dtb):
    lane = _iota((1, HD), 1)
    g = -jnp.exp(alog) * _softplus(sm + dtb)
    return (jnp.where((lane >= L_GA) & (lane < L_GA + NG), g,
                      jnp.where((lane >= L_GB) & (lane < L_GB + NG), _sigmoid(sm), 0.0)),)


def _gdn_prep(gq, gk, gv, gates, taps, h):
    q, k, v = [_silu(_conv4(x, *taps[4 * j:4 * j + 4]))[HALO:] for j, x in enumerate((gq, gk, gv))]
    q = q * lax.rsqrt(jnp.sum(q * q, axis=-1, keepdims=True) + NORM_EPS) * (HD ** -0.5)
    k = k * lax.rsqrt(jnp.sum(k * k, axis=-1, keepdims=True) + NORM_EPS)
    return q, k, v, _lane_pick(gates, L_GA + h), _lane_pick(gates, L_GB + h)


def _split(x, n):
    parts, rest = [], x
    for i in range(n):
        parts.append(rest.astype(BF16))
        if i + 1 < n:
            rest = rest - parts[-1].astype(F32)
    return parts


def _raw_dot(a, b, form):
    lead = a.ndim - 2
    ca, cb = {"nn": (1, 0), "nt": (1, 1), "tn": (0, 0)}[form]
    batch = ((0,), (0,)) if lead else ((), ())
    return lax.dot_general(a, b, (((ca + lead,), (cb + lead,)), batch), preferred_element_type=F32)


def _pdot_impl(a, b, form, mode):
    if mode == "1":
        return _raw_dot(a.astype(BF16), b.astype(BF16), form)
    if mode == "3":
        (ah, al), (bh, bl) = _split(a, 2), _split(b, 2)
        return _raw_dot(ah, bh, form) + (_raw_dot(al, bh, form) + _raw_dot(ah, bl, form))
    if mode == "xa":
        return sum(_raw_dot(a.astype(BF16), t, form) for t in reversed(_split(b, 3)))
    return sum(_raw_dot(t, b.astype(BF16), form) for t in reversed(_split(a, 3)))


@functools.partial(jax.custom_vjp, nondiff_argnums=(2, 3))
def _pdot(a, b, form, mode):
    return _pdot_impl(a, b, form, mode)


def _pdot_fwd(a, b, form, mode):
    return _pdot_impl(a, b, form, mode), (a, b)


def _pdot_bwd(form, mode, res, ct):
    a, b = res
    da_args, db_args = {"nn": ((ct, b, "nt"), (a, ct, "tn")), "nt": ((ct, b, "nn"), (ct, a, "tn")),
                        "tn": ((b, ct, "nt"), (a, ct, "nn"))}[form]

    def side(args, exact):
        if mode in ("1", "3"):
            return mode
        return "xa" if args[0] is exact else "xb"

    if mode == "xa":
        return jnp.zeros_like(a), _pdot_impl(*db_args, side(db_args, a))
    if mode == "xb":
        return _pdot_impl(*da_args, side(da_args, b)), jnp.zeros_like(b)
    return _pdot_impl(*da_args, mode), _pdot_impl(*db_args, mode)


_pdot.defvjp(_pdot_fwd, _pdot_bwd)

GDN_QK, GDN_INV, GDN_SCAN = "1", "1", "1"


@jax.custom_vjp
def _tri_inv(low):
    eye = (_iota((CHUNK, CHUNK), 0) == _iota((CHUNK, CHUNK), 1)).astype(F32)
    inv = eye - low
    pw = low
    for _ in range(5):
        pw = _pdot_impl(pw, pw, "nn", GDN_INV)
        inv = inv + _pdot_impl(inv, pw, "nn", GDN_INV)
    return inv


def _tri_inv_fwd(low):
    inv = _tri_inv(low)
    return inv, inv


def _tri_inv_bwd(inv, ct):
    return (-_pdot_impl(_pdot_impl(inv, ct, "tn", GDN_INV), inv, "nt", GDN_INV),)


_tri_inv.defvjp(_tri_inv_fwd, _tri_inv_bwd)


def _gdn_intra(q, k, v, g, beta):
    n = q.shape[0]
    r, c = _iota((CHUNK, CHUNK), 0), _iota((CHUNK, CHUNK), 1)
    tril, strict = r >= c, r > c
    trilf = jnp.broadcast_to(tril.astype(F32), (n, CHUNK, CHUNK))
    gcm = _pdot(trilf, jnp.broadcast_to(g, (n, CHUNK, CHUNK)), "nn", "xa")
    gcf = _pdot(trilf, jnp.broadcast_to(g, (n, CHUNK, HD)), "nn", "xa")
    lane0 = (_iota((1, 1, CHUNK), 2) == 0).astype(F32)
    gcr = _pdot(jnp.ones((n, CHUNK, CHUNK), F32), gcm * lane0, "nt", "xa")
    decay = jnp.where(tril, jnp.exp(jnp.where(tril, gcm - gcr, 0.0)), 0.0)
    egc = jnp.exp(gcf)
    kb = k * beta
    low = jnp.where(strict, _pdot(kb, k, "nt", GDN_QK) * decay, 0.0)
    inv = _tri_inv(low)
    u = _pdot(inv, v * beta, "nn", GDN_INV)
    w = _pdot(inv, kb * egc, "nn", GDN_INV)
    at = jnp.where(tril, _pdot(q, k, "nt", GDN_QK) * decay, 0.0)
    gl = jnp.sum(jnp.broadcast_to(g, (n, CHUNK, HD)), axis=1, keepdims=True)
    kd = k * jnp.exp(gl - gcf)
    return (_pdot(kd, w, "tn", GDN_SCAN), _pdot(kd, u, "tn", GDN_SCAN), q * egc - _pdot(at, w, "nn", GDN_SCAN),
            _pdot(at, u, "nn", GDN_SCAN), gl)


def _gdn_step(s, kw, ku, a, b, gl):
    return _pdot(a, s, "nn", GDN_SCAN) + b, s * jnp.exp(gl) - _pdot(kw, s, "nn", GDN_SCAN) + ku


SCAN_HEADS = 3
SCAN_UNROLL = 4


def _gdn_chunked_scratch(nc):
    big = pltpu.VMEM((nc, CHUNK, HD), F32)
    return [big, big, big, pltpu.VMEM((nc, CHUNK, 1), F32), pltpu.VMEM((nc, CHUNK, 1), F32)]


N_TERMS = 5


def _gdn_term_shapes(nc):
    return [(nc, HD, HD), (nc, HD, HD), (nc, CHUNK, HD), (nc, CHUNK, HD), (nc, 1, HD)]


def _per_head(shape, heads=None, one_buffer=True):
    lead = (None,) if heads is None else (heads,)
    return pl.BlockSpec(lead + tuple(shape), lambda h: (h,) + (0,) * len(shape),
                        pipeline_mode=ONE_BUFFER if one_buffer else None)


def _gdn_in_specs(t):
    cw = lambda cb: pl.BlockSpec((4, HD), lambda h, cb=cb: (0, cb + h))
    return [_pcol(t, GQ), _pcol(t, GK), _pcol(t, GV), _small(t), cw(0), cw(NG), cw(2 * NG)]


def _taps(wq, wk, wv):
    return tuple(w[k:k + 1, :] for w in (wq, wk, wv) for k in range(4))


def _prep_rows(t):
    return min(t, 256)


def _gdn_pad(srcs, pads):
    for src, pad in zip(srcs, pads):
        pad[0:HALO, :] = jnp.zeros((HALO, HD), F32)
        pad[HALO:, :] = src[...]


def _gdn_stage(pads, gates, taps, h, chunked):
    t = gates.shape[0]
    rows = _prep_rows(t)
    per = rows // CHUNK

    def tile(i, carry):
        r0 = pl.multiple_of(i * rows, rows)
        vals = _gdn_prep(*[p[pl.ds(r0, rows + HALO), :] for p in pads], gates[pl.ds(r0, rows), :], taps, h)
        for v, r in zip(vals, chunked):
            r[pl.ds(i * per, per)] = v.reshape(per, CHUNK, v.shape[-1])
        return carry

    lax.fori_loop(0, t // rows, tile, 0)


def _gdn_intra_all(chunked, intra):
    nc = chunked[0].shape[0]
    grp_n = math.gcd(nc, GROUP)

    def grp(i, carry):
        sl = pl.ds(pl.multiple_of(i * grp_n, grp_n), grp_n)
        for r, val in zip(intra, _gdn_intra(*[c[sl] for c in chunked])):
            r[sl] = val
        return carry

    lax.fori_loop(0, nc // grp_n, grp, 0)


def _gdn_fwd(pa, gates, conv):
    t = pa.shape[0]
    nc = t // CHUNK
    terms = _gdn_term_shapes(nc)

    def body(gq, gk, gv, gt, wq, wk, wv, *rest):
        h = pl.program_id(0)
        intra, chunked, pads = rest[:N_TERMS], rest[N_TERMS:N_TERMS + 5], rest[N_TERMS + 5:]
        _gdn_pad((gq, gk, gv), pads)
        _gdn_stage(pads, gt, _taps(wq, wk, wv), h, chunked)
        _gdn_intra_all(chunked, intra)

    qkv = [(nc, CHUNK, HD)] * 3
    outs = pl.pallas_call(
        body, grid=(NG,), name="gdn_fwd", in_specs=_gdn_in_specs(t),
        out_specs=[_per_head(sh, one_buffer=False) for sh in terms + qkv],
        out_shape=[SDS((NG,) + sh, F32) for sh in terms + qkv],
        scratch_shapes=_gdn_chunked_scratch(nc)[3:] + [pltpu.VMEM((t + HALO, HD), F32)] * 3, compiler_params=_cp("parallel"),
    )(pa, pa, pa, gates, conv, conv, conv)
    return list(outs[:N_TERMS]), list(outs[N_TERMS:])


def _gdn_scan(terms_in):
    nc = terms_in[0].shape[1]
    terms = _gdn_term_shapes(nc)

    def body(*refs):
        intra, o, states = refs[:N_TERMS], refs[N_TERMS], refs[N_TERMS + 1]

        def one(c, ss):
            rows = pl.ds(pl.multiple_of(c * CHUNK, CHUNK), CHUNK)
            loaded = [[r[hh, c] for r in intra] for hh in range(SCAN_HEADS)]
            res = [_gdn_step(ss[hh], *loaded[hh]) for hh in range(SCAN_HEADS)]
            for hh in range(SCAN_HEADS):
                states[hh, c] = ss[hh]
                o[rows, hh * HD:(hh + 1) * HD] = res[hh][0]
            return tuple(r[1] for r in res)

        per_trip = math.gcd(nc, SCAN_UNROLL)

        def step(i, ss):
            for k in range(per_trip):
                ss = one(per_trip * i + k, ss)
            return ss

        lax.fori_loop(0, nc // per_trip, step, tuple(jnp.zeros((HD, HD), F32) for _ in range(SCAN_HEADS)))

    return pl.pallas_call(
        body, grid=(NG // SCAN_HEADS,), name="gdn_scan", in_specs=[_per_head(sh, SCAN_HEADS) for sh in terms],
        out_specs=[pl.BlockSpec((nc * CHUNK, SCAN_HEADS * HD), lambda h: (0, h), pipeline_mode=ONE_BUFFER),
                   _per_head((nc, HD, HD), SCAN_HEADS)],
        out_shape=[SDS((nc * CHUNK, NG * HD), F32), SDS((NG, nc, HD, HD), F32)], compiler_params=_cp("parallel"),
    )(*terms_in)


def _gdn_bwd_scan(saved, do_raw):
    nc = saved[0].shape[1]
    terms = _gdn_term_shapes(nc)

    def body(*refs):
        intra, states, do, outs = refs[:N_TERMS], refs[N_TERMS], refs[N_TERMS + 1], refs[N_TERMS + 2:]

        def one(c, dss):
            rows = pl.ds(pl.multiple_of(c * CHUNK, CHUNK), CHUNK)
            loaded = [[states[hh, c]] + [r[hh, c] for r in intra] for hh in range(SCAN_HEADS)]
            cts = [do[rows, hh * HD:(hh + 1) * HD] for hh in range(SCAN_HEADS)]
            grads = [jax.vjp(_gdn_step, *loaded[hh])[1]((cts[hh], dss[hh])) for hh in range(SCAN_HEADS)]
            for hh in range(SCAN_HEADS):
                for r, gval in zip(outs, grads[hh][1:]):
                    r[hh, c] = gval
            return tuple(g[0] for g in grads)

        per_trip = math.gcd(nc, SCAN_UNROLL)

        def bwd(i, dss):
            c = nc - 1 - per_trip * i
            for k in range(per_trip):
                dss = one(c - k, dss)
            return dss

        lax.fori_loop(0, nc // per_trip, bwd, tuple(jnp.zeros((HD, HD), F32) for _ in range(SCAN_HEADS)))

    return pl.pallas_call(
        body, grid=(NG // SCAN_HEADS,), name="gdn_bwd_scan",
        in_specs=[_per_head(sh, SCAN_HEADS) for sh in terms] + [_per_head((nc, HD, HD), SCAN_HEADS)]
        + [pl.BlockSpec((nc * CHUNK, SCAN_HEADS * HD), lambda h: (0, h), pipeline_mode=ONE_BUFFER)],
        out_specs=[_per_head(sh, SCAN_HEADS) for sh in terms],
        out_shape=[SDS((NG,) + sh, F32) for sh in terms], compiler_params=_cp("parallel"),
    )(*saved, do_raw)


def _gdn_bwd(pa, gates, conv, dterms, qkv):
    t = pa.shape[0]
    nc = t // CHUNK
    terms = _gdn_term_shapes(nc)

    def body(*refs):
        gq, gk, gv, gt, wq, wk, wv = refs[:7]
        dintra, qkv = refs[7:7 + N_TERMS], refs[7 + N_TERMS:10 + N_TERMS]
        dgq, dgk, dgv, dgt, dwq, dwk, dwv = refs[10 + N_TERMS:17 + N_TERMS]
        chunked, pads, dpads, dgt_s = (refs[17 + N_TERMS:22 + N_TERMS], refs[22 + N_TERMS:25 + N_TERMS],
                                       refs[25 + N_TERMS:28 + N_TERMS], refs[28 + N_TERMS])
        h = pl.program_id(0)
        taps = _taps(wq, wk, wv)
        _gdn_pad((gq, gk, gv), pads)
        rows = _prep_rows(t)
        per = rows // CHUNK

        def gates_tile(i, carry):
            gtile = gt[pl.ds(pl.multiple_of(i * rows, rows), rows), :]
            chunked[3][pl.ds(i * per, per)] = _lane_pick(gtile, L_GA + h).reshape(per, CHUNK, 1)
            chunked[4][pl.ds(i * per, per)] = _lane_pick(gtile, L_GB + h).reshape(per, CHUNK, 1)
            return carry

        lax.fori_loop(0, t // rows, gates_tile, 0)
        grp_n = math.gcd(nc, GROUP)

        def grp(i, carry):
            sl = pl.ds(pl.multiple_of(i * grp_n, grp_n), grp_n)
            _, vjp = jax.vjp(_gdn_intra, *[r[sl] for r in qkv], chunked[3][sl], chunked[4][sl])
            for r, gval in zip(chunked, vjp(tuple(r[sl] for r in dintra))):
                r[sl] = gval
            return carry

        lax.fori_loop(0, nc // grp_n, grp, 0)

        for r in dpads:
            r[...] = jnp.zeros_like(r)

        def tile(i, dtaps):
            r0 = pl.multiple_of(i * rows, rows)
            win = pl.ds(r0, rows + HALO)
            _, vjp = jax.vjp(lambda *a: _gdn_prep(*a, h), *[p[win, :] for p in pads], gt[pl.ds(r0, rows), :], taps)
            grads = vjp(tuple(r[pl.ds(i * per, per)].reshape(rows, r.shape[-1]) for r in chunked))
            for r, gval in zip(dpads, grads[:3]):
                r[win, :] += gval
            dgt_s[pl.ds(r0, rows), :] = grads[3]
            return jax.tree.map(jnp.add, dtaps, grads[4])

        dtaps = lax.fori_loop(0, t // rows, tile, (jnp.zeros((1, HD), F32),) * 12)
        for r, dpad in zip((dgq, dgk, dgv), dpads):
            r[...] = dpad[HALO:, :].astype(r.dtype)
        for j, r in enumerate((dwq, dwk, dwv)):
            for k in range(4):
                r[k:k + 1, :] = dtaps[4 * j + k]

        @pl.when(h == 0)
        def _():
            dgt[...] = jnp.zeros_like(dgt)

        dgt[...] += dgt_s[...]

    head = _head(t)
    taps = pl.BlockSpec((4, HD), lambda h: (0, h))
    return pl.pallas_call(
        body, grid=(NG,), name="gdn_bwd",
        in_specs=_gdn_in_specs(t) + [_per_head(sh) for sh in terms + [(nc, CHUNK, HD)] * 3],
        out_specs=[head, head, head, _small(t), taps, taps, taps],
        out_shape=[SDS((t, NG * HD), BF16)] * 3 + [SDS((t, HD), F32)] + [SDS((4, NG * HD), F32)] * 3,
        scratch_shapes=_gdn_chunked_scratch(nc) + [pltpu.VMEM((t + HALO, HD), F32)] * 6 + [pltpu.VMEM((t, HD), F32)],
        compiler_params=_cp("arbitrary"),
    )(pa, pa, pa, gates, conv, conv, conv, *dterms, *qkv)


def _gdn_post(o, z, gain):
    return (jnp.concatenate(
        [_rms(o[:, h * HD:(h + 1) * HD], gain) * _silu(z[:, h * HD:(h + 1) * HD]) for h in range(NG)], axis=1),)


def _place():
    return lax.axis_index("x"), lax.axis_index("y"), lax.axis_index("c")


def _sum_blocks(name, parts):
    _, r, c = parts.shape
    tr = 64 if r % 64 == 0 else r

    def body(x, o):
        acc = x[0].astype(F32)
        for d in range(1, N_DEV):
            acc = acc + x[d].astype(F32)
        o[...] = acc

    return pl.pallas_call(
        body, grid=(r // tr,), name=name, in_specs=[pl.BlockSpec((N_DEV, tr, c), lambda i: (0, i, 0))],
        out_specs=pl.BlockSpec((tr, c), lambda i: (i, 0)), out_shape=SDS((r, c), F32), compiler_params=_cp("parallel"),
    )(parts)


def _all_reduce_small(name, x):
    m_per, n = x.shape

    def body(x_ref, out_ref, send_sems, recv_sems, local_sem):
        px, py, pc = _place()
        me, sibling = (px, py, pc), (px, py, 1 - pc)
        chips = [(1 - px, py), (px, 1 - py), (1 - px, 1 - py)]
        buf = out_ref

        def rows(qx, qy, qc):
            return buf.at[pl.ds((4 * qx + 2 * qy + qc) * m_per, m_per), :]

        def copy(k, block, to, src=None):
            return pltpu.make_async_remote_copy(
                src_ref=rows(*block) if src is None else src, dst_ref=rows(*block),
                send_sem=send_sems.at[k], recv_sem=recv_sems.at[k], device_id=to, device_id_type=MESH)

        mine = pltpu.make_async_copy(x_ref, rows(*me), local_sem)
        mine.start()
        first = [copy(0, me, sibling, src=x_ref)]
        first += [copy(1 + j, me, (*chip, pc), src=x_ref) for j, chip in enumerate(chips)]
        for cp in first:
            cp.start()
        passed = [copy(4 + j, (*chip, pc), sibling) for j, chip in enumerate(chips)]
        for j, chip in enumerate(chips):
            copy(1 + j, (*chip, pc), me).wait_recv()
            passed[j].start()
        copy(0, sibling, me).wait_recv()
        for j, chip in enumerate(chips):
            copy(4 + j, (*chip, 1 - pc), me).wait_recv()
        for cp in first + passed:
            cp.wait_send()
        mine.wait()

    gathered = pl.pallas_call(
        body, name=name, out_shape=SDS((N_DEV * m_per, n), x.dtype),
        in_specs=[pl.BlockSpec(memory_space=pltpu.VMEM)], out_specs=pl.BlockSpec(memory_space=pltpu.VMEM),
        scratch_shapes=[pltpu.SemaphoreType.DMA((7,)), pltpu.SemaphoreType.DMA((7,)), pltpu.SemaphoreType.DMA],
    )(x)
    return _sum_blocks(name + "_sum", gathered.reshape(N_DEV, m_per, n))


HBM_SPEC = pl.BlockSpec(memory_space=pltpu.HBM)
SEM_SPEC = pl.BlockSpec(memory_space=pltpu.SEMAPHORE)
EFFECT = pltpu.SideEffectType.DATAFLOW_SIDE_EFFECTING


def _copies_start(name, bufs, n_remote, n_local, build, deps):
    nb, nd = len(bufs), len(deps)
    sem_shapes = [pltpu.SemaphoreType.DMA((n_remote,)), pltpu.SemaphoreType.DMA((n_remote,))]
    if n_local:
        sem_shapes.append(pltpu.SemaphoreType.DMA((n_local,)))
    ns = len(sem_shapes)

    def body(*refs):
        sems = refs[nb + nd:nb + nd + ns]
        remote, local = build(refs[:nb], *sems, *([None] * (3 - ns)))
        for cp in local + remote:
            cp.start()
        refs[-1][...] = jnp.zeros((8, HD), F32)

    outs = pl.pallas_call(
        body, name=name,
        out_shape=(*sem_shapes, *[pltpu.HBM(b.shape, b.dtype) for b in bufs], SDS((8, HD), F32)),
        in_specs=[HBM_SPEC] * nb + [ANY_SPEC] * nd,
        out_specs=(*[SEM_SPEC] * ns, *[HBM_SPEC] * nb, pl.BlockSpec(memory_space=pltpu.VMEM)),
        input_output_aliases={i: ns + i for i in range(nb)},
        compiler_params=pltpu.CompilerParams(has_side_effects=EFFECT),
    )(*[pltpu.with_memory_space_constraint(b, pltpu.HBM) for b in bufs], *deps)
    return list(outs[:ns]), list(outs[ns:ns + nb]), outs[-1]


def _copies_wait(name, bufs, sems, build, after):
    nb, ns = len(bufs), len(sems)

    def body(*refs):
        remote, local = build(refs[:nb], *refs[nb:nb + ns], *([None] * (3 - ns)))
        for cp in local:
            cp.wait()
        for cp in remote:
            cp.wait_send()
            cp.wait_recv()

    outs = pl.pallas_call(
        body, name=name, out_shape=tuple(pltpu.HBM(b.shape, b.dtype) for b in bufs),
        in_specs=[HBM_SPEC] * nb + [SEM_SPEC] * ns + [ANY_SPEC] * len(after), out_specs=tuple([HBM_SPEC] * nb),
        input_output_aliases={i: i for i in range(nb)},
        compiler_params=pltpu.CompilerParams(has_side_effects=EFFECT),
    )(*bufs, *sems, *after)
    return list(outs)


def _remote(src, dst, send, recv, k, to):
    return pltpu.make_async_remote_copy(src_ref=src, dst_ref=dst, send_sem=send.at[k], recv_sem=recv.at[k],
                                        device_id=to, device_id_type=MESH)


class _Gather:
    def __init__(self, name, shards, deps):
        self.name, self.n = name, len(shards)
        lands = [lax.empty((N_DEV,) + s.shape, s.dtype) for s in shards]
        self.sems1, bufs, self.token = _copies_start(
            name + "_s1", list(shards) + lands, 4 * self.n, self.n, self._stage1(range(self.n)), deps)
        self.shards, self.lands, self.sems2 = bufs[:self.n], bufs[self.n:], {}

    def _stage1(self, idxs):
        def build(refs, send, recv, loc):
            x, y, c = _place()
            me = 4 * x + 2 * y + c
            targets = [(x, y, 1 - c), (1 - x, y, c), (x, 1 - y, c), (1 - x, 1 - y, c)]
            remote, local = [], []
            for pos, i in enumerate(idxs):
                src, land = refs[pos], refs[len(idxs) + pos]
                local.append(pltpu.make_async_copy(src, land.at[me], loc.at[i]))
                remote += [_remote(src, land.at[me], send, recv, 4 * i + k, to) for k, to in enumerate(targets)]
            return remote, local
        return build

    @staticmethod
    def _stage2(refs, send, recv, loc):
        x, y, c = _place()
        remote = []
        for pos, land in enumerate(refs):
            for j, (cx, cy) in enumerate([(1 - x, y), (x, 1 - y), (1 - x, 1 - y)]):
                blk = land.at[4 * cx + 2 * cy + c]
                remote.append(_remote(blk, blk, send, recv, 3 * pos + j, (x, y, 1 - c)))
        return remote, []

    def pass_on(self, idxs, after):
        tag, m = "".join(map(str, idxs)), len(idxs)
        bufs = _copies_wait(f"{self.name}_w1_{tag}", [self.shards[i] for i in idxs] + [self.lands[i] for i in idxs],
                            self.sems1, self._stage1(idxs), after)
        self.sems2[tag], lands, token = _copies_start(f"{self.name}_s2_{tag}", bufs[m:], 3 * m, 0, self._stage2, ())
        for pos, i in enumerate(idxs):
            self.lands[i] = lands[pos]
        return [token]

    def get(self, idxs, after):
        tag = "".join(map(str, idxs))
        return _copies_wait(f"{self.name}_w2_{tag}", [self.lands[i] for i in idxs], self.sems2[tag], self._stage2, after)


class _RelayGather:
    def __init__(self, name, shards, deps):
        self.name, self.n = name, len(shards)
        lands = [lax.empty((N_DEV,) + s.shape, s.dtype) for s in shards]
        self.sems, bufs, self.token = _copies_start(name + "_s1", list(shards) + lands, 3 * self.n, self.n, self._stage1, deps)
        self.shards, self.lands = bufs[:self.n], bufs[self.n:]

    def _stage1(self, refs, send, recv, loc):
        x, y, c = _place()
        me = 4 * x + 2 * y + c
        remote, local = [], []
        for i in range(self.n):
            src, land = refs[i], refs[self.n + i]
            local.append(pltpu.make_async_copy(src, land.at[me], loc.at[i]))
            remote += [_remote(src, land.at[me], send, recv, 3 * i + k, to)
                       for k, to in enumerate([(x, y, 1 - c), (1 - x, y, c), (x, 1 - y, c)])]
        return remote, local

    @staticmethod
    def _relay(refs, send, recv, loc):
        x, y, c = _place()
        remote = []
        for i, land in enumerate(refs):
            half = land.shape[1] // 2
            from_x = land.at[4 * (1 - x) + 2 * y + c].at[pl.ds(0, half)]
            from_y = land.at[4 * x + 2 * (1 - y) + c].at[pl.ds(half, half)]
            remote += [_remote(from_x, from_x, send, recv, 2 * i, (x, 1 - y, c)),
                       _remote(from_y, from_y, send, recv, 2 * i + 1, (1 - x, y, c))]
        return remote, []

    def forward(self, after):
        bufs = _copies_wait(self.name + "_w1", self.shards + self.lands, self.sems, self._stage1, after)
        self.sems, self.lands, self.token = _copies_start(self.name + "_sf", bufs[self.n:], 2 * self.n, 0, self._relay, ())
        return [self.token]

    def pass_on(self, after):
        lands = _copies_wait(self.name + "_wf", self.lands, self.sems, self._relay, after)
        self.sems, self.lands, self.token = _copies_start(self.name + "_s2", lands, 3 * self.n, 0, _Gather._stage2, ())
        return [self.token]

    def get(self, after):
        return _copies_wait(self.name + "_w2", self.lands, self.sems, _Gather._stage2, after)


def _rows_tile(r, row_bytes, target=1 << 20):
    tr = r
    while tr % 32 == 0 and tr * row_bytes > target:
        tr //= 2
    return tr


def _pair_add(name, g, got, c):
    _, r, cols = g.shape
    tr = _rows_tile(r, cols * 2)

    def body(s, a, b, o):
        o[...] = (a[...].astype(F32) + b[...].astype(F32)).astype(o.dtype)

    return pl.pallas_call(
        body, name=name, out_shape=SDS((4, r, cols), g.dtype),
        grid_spec=pltpu.PrefetchScalarGridSpec(
            num_scalar_prefetch=1, grid=(4, r // tr),
            in_specs=[pl.BlockSpec((None, tr, cols), lambda j, i, s: (2 * j + s[0], i, 0)),
                      pl.BlockSpec((None, tr, cols), lambda j, i, s: (j, i, 0))],
            out_specs=pl.BlockSpec((None, tr, cols), lambda j, i, s: (j, i, 0))),
        compiler_params=_cp("parallel", "parallel"),
    )(c.reshape(1), g, got)


def _quad_sum(name, part, got, chip, wmv=None):
    _, r, cols = part.shape
    tr = _rows_tile(r, cols * 4)
    n_out = 4 if wmv else 1

    def body(s, a, b1, b2, b3, *rest):
        g = ((a[...].astype(F32) + b1[...].astype(F32)) + b2[...].astype(F32)) + b3[...].astype(F32)
        rest[-n_out][...] = g
        if wmv:
            w, m, v = rest[:3]
            rest[-3][...], rest[-2][...], rest[-1][...] = _adamw(w[...], g, m[...], v[...])

    blk = lambda k: pl.BlockSpec((None, tr, cols), lambda i, s, k=k: (jnp.bitwise_xor(s[0], k), i, 0))
    row = pl.BlockSpec((tr, cols), lambda i, s: (i, 0))
    outs = pl.pallas_call(
        body, name=name, out_shape=[SDS((r, cols), F32)] * n_out,
        grid_spec=pltpu.PrefetchScalarGridSpec(
            num_scalar_prefetch=1, grid=(r // tr,), in_specs=[blk(0), blk(1), blk(2), blk(3)] + [row] * (n_out - 1),
            out_specs=[row] * n_out),
        compiler_params=_cp("parallel"),
    )(chip.reshape(1), part, got, got, got, *(wmv or ()))
    return tuple(outs) if wmv else outs[0]


class _Scatter:
    def __init__(self, name, grads, deps):
        self.name, self.n = name, len(grads)
        got = [lax.empty((4,) + g.shape[1:], g.dtype) for g in grads]
        self.sems, bufs, self.token = _copies_start(name + "_s1", list(grads) + got, 4 * self.n, 0, self._stage1, deps)
        self.grads, self.got = bufs[:self.n], bufs[self.n:]

    def _stage1(self, refs, send, recv, loc):
        x, y, c = _place()
        remote = []
        for i in range(self.n):
            remote += [_remote(refs[i].at[2 * j + 1 - c], refs[self.n + i].at[j], send, recv, 4 * i + j, (x, y, 1 - c))
                       for j in range(4)]
        return remote, []

    def _stage2(self, refs, send, recv, loc):
        x, y, c = _place()
        remote = []
        for i in range(self.n):
            for k in (1, 2, 3):
                tx = 1 - x if k & 2 else x
                ty = 1 - y if k & 1 else y
                remote.append(_remote(refs[i].at[2 * tx + ty], refs[self.n + i].at[2 * x + y], send, recv,
                                      3 * i + k - 1, (tx, ty, c)))
        return remote, []

    def mid(self, after):
        bufs = _copies_wait(self.name + "_w1", self.grads + self.got, self.sems, self._stage1, after)
        c = lax.axis_index("c").astype(jnp.int32)
        parts = [_pair_add(f"{self.name}_add{i}", bufs[i], bufs[self.n + i], c) for i in range(self.n)]
        got = [lax.empty(p.shape, p.dtype) for p in parts]
        self.sems, bufs, self.token = _copies_start(self.name + "_s2", parts + got, 3 * self.n, 0, self._stage2, ())
        self.parts, self.got = bufs[:self.n], bufs[self.n:]

    def end(self, after, wmv=None):
        bufs = _copies_wait(self.name + "_w2", self.parts + self.got, self.sems, self._stage2, after)
        chip = (2 * lax.axis_index("x") + lax.axis_index("y")).astype(jnp.int32)
        wmv = wmv or [None] * self.n
        return [_quad_sum(f"{self.name}_sum{i}", bufs[i], bufs[self.n + i], chip, wmv[i]) for i in range(self.n)]


def _adamw(w, g, m, v):
    m = ADAM_B1 * m + (1.0 - ADAM_B1) * g
    v = ADAM_B2 * v + (1.0 - ADAM_B2) * (g * g)
    m_hat = m / (1.0 - ADAM_B1 ** ADAM_STEP)
    v_hat = v / (1.0 - ADAM_B2 ** ADAM_STEP)
    return -ADAM_LR * (m_hat / (jnp.sqrt(v_hat) + ADAM_EPS) + ADAM_WD * w), m, v


def _adamw_call(name, w, g, m, v):
    r, c = w.shape
    tm = 64 if r % 64 == 0 else r
    return _rowwise(name, _adamw, [w, g, m, v], [], [(c, F32)] * 3, tm)


_IN_COLS = 5906


def _perm_in(w):
    pad = jnp.zeros((w.shape[0], 2 * HALF - _IN_COLS), w.dtype)
    return (jnp.concatenate([w[:, 2310:4614], w[:, 4614:5382]], axis=1),
            jnp.concatenate([w[:, :2304], w[:, 5394:5906], w[:, 2304:2310], w[:, 5382:5394], pad], axis=1))


def _unperm_in(ga, gb):
    return jnp.concatenate([gb[:, :2304], gb[:, 2816:2822], ga[:, :2304], ga[:, 2304:3072], gb[:, 2822:2834],
                            gb[:, 2304:2816]], axis=1)


def _lanes(v, at):
    return jnp.pad(v, ((0, 0), (at, HD - at - v.shape[1])))


_PACK = ("norm_mix", "mem_norm", "norm_ffn", "gdn_conv", "fox_q_norm", "fox_k_norm", "gdn_out_norm", "mem_q_norm",
         "mem_k_norm", "fox_f_bias", "gdn_a_log", "gdn_dt_bias", "loss")


def _pack(vals):
    parts = [vals[n].reshape(-1, HD) for n in _PACK]
    used = sum(p.shape[0] for p in parts)
    buf = jnp.concatenate(parts + [jnp.zeros((-used % 8, HD), F32)], axis=0)
    return buf, [(n, p.shape[0]) for n, p in zip(_PACK, parts)]


def _unpack(buf, layout):
    out, at = {}, 0
    for n, rows in layout:
        out[n] = buf[at:at + rows]
        at += rows
    return out


def kernel(x, mem, norm_mix, w_in, fox_f_bias, fox_q_norm, fox_k_norm, gdn_conv, gdn_a_log, gdn_dt_bias, gdn_out_norm, mem_norm, w_mem_kv, mem_q_norm, mem_k_norm, w_out, norm_ffn, w_gate_up, w_down, loss_target, m_norm_mix, m_w_in, m_fox_f_bias, m_fox_q_norm, m_fox_k_norm, m_gdn_conv, m_gdn_a_log, m_gdn_dt_bias, m_gdn_out_norm, m_mem_norm, m_w_mem_kv, m_mem_q_norm, m_mem_k_norm, m_w_out, m_norm_ffn, m_w_gate_up, m_w_down, v_norm_mix, v_w_in, v_fox_f_bias, v_fox_q_norm, v_fox_k_norm, v_gdn_conv, v_gdn_a_log, v_gdn_dt_bias, v_gdn_out_norm, v_mem_norm, v_w_mem_kv, v_mem_q_norm, v_mem_k_norm, v_w_out, v_norm_ffn, v_w_gate_up, v_w_down):
    args = dict(locals())
    d = x.shape[2]
    me = 4 * lax.axis_index("x") + 2 * lax.axis_index("y") + lax.axis_index("c")

    cshard = gdn_conv[0].shape[1]
    conv_pad = jnp.pad(gdn_conv[0], ((0, 4), (0, 3 * HD - cshard)))
    w_in_a, w_in_b = _perm_in(w_in[0])
    wmv = lambda n: (args[n][0], args["m_" + n][0], args["v_" + n][0])
    comm = _StepComm(w_in_b.astype(BF16), {"in_a": [w_in_a.astype(BF16), conv_pad],
                                           "kv_out": [w_mem_kv[0].astype(BF16), w_out[0].astype(BF16)]},
                     w_gate_up[0].astype(BF16), w_down[0].astype(BF16), (),
                     {"ffn": [wmv("w_down"), wmv("w_gate_up")], "a": [None, wmv("w_out"), wmv("w_mem_kv")], "b": [None]})

    grad_x, loss_local, small_grads = _local_step(
        x[0], mem[0], loss_target[0], norm_mix, fox_f_bias, fox_q_norm, fox_k_norm, gdn_a_log, gdn_dt_bias,
        gdn_out_norm, mem_norm, mem_q_norm, mem_k_norm, norm_ffn, cshard, comm)

    red = comm.finish([grad_x])
    updated = {"w_down": red["ffn"][0], "w_gate_up": red["ffn"][1], "w_out": red["a"][1], "w_mem_kv": red["a"][2]}
    grads = {n: r[0] for n, r in updated.items()}
    grads["w_in"] = _unperm_in(red["a"][0], red["b"][0])
    small_grads["loss"] = jnp.broadcast_to(loss_local, (1, HD))
    packed, layout = _pack(small_grads)
    small = _unpack(_all_reduce_small("ar_small", packed), layout)
    loss = small["loss"][0, 0]
    six = {"fox_f_bias": L_FF, "gdn_a_log": L_GA, "gdn_dt_bias": L_GA}
    for n, rows_n in layout[:-1]:
        gsm = small[n]
        if n == "gdn_conv":
            gsm = lax.dynamic_slice(gsm.reshape(4, N_DEV * cshard), (0, me * cshard), (4, cshard))[None]
        elif n in six:
            gsm = gsm[:, six[n]:six[n] + 6]
        else:
            gsm = gsm.reshape(1, rows_n * HD)
        grads[n] = gsm

    names = ['norm_mix', 'w_in', 'fox_f_bias', 'fox_q_norm', 'fox_k_norm', 'gdn_conv', 'gdn_a_log', 'gdn_dt_bias',
             'gdn_out_norm', 'mem_norm', 'w_mem_kv', 'mem_q_norm', 'mem_k_norm', 'w_out', 'norm_ffn', 'w_gate_up', 'w_down']
    big = ("w_in", "w_mem_kv", "w_out", "w_gate_up", "w_down")
    delta, new_m, new_v = {}, {}, {}
    for n in big:
        res = updated[n][1:] if n in updated else _adamw_call("adamw_" + n, args[n][0], grads[n], *wmv(n)[1:])
        delta[n], new_m[n], new_v[n] = [a[None] for a in res]
        grads[n] = grads[n][None]

    def flat(a):
        a = a.reshape(1, -1)
        return jnp.pad(a, ((0, 0), (0, -a.shape[1] % HD))).reshape(-1, HD)

    smalls = [n for n in names if n not in big]
    pk = lambda pre: jnp.concatenate([flat(grads[n] if pre == "g" else args[pre + n]) for n in smalls], axis=0)
    cat = [pk(""), pk("g"), pk("m_"), pk("v_")]
    padr = -cat[0].shape[0] % 8
    cat = [jnp.pad(a, ((0, padr), (0, 0))) for a in cat]
    res = _adamw_call("adamw_small", *cat)
    at = 0
    for n in smalls:
        shape = args[n].shape
        size = math.prod(shape)
        nrow = -(-size // HD)
        for dst, src in zip((delta, new_m, new_v), res):
            dst[n] = src[at:at + nrow].reshape(-1)[:size].reshape(shape)
        at += nrow

    return (loss, grad_x[None], *[grads[n] for n in names], *[delta[n] for n in names],
            *[new_m[n] for n in names], *[new_v[n] for n in names])


class _StepComm:
    def __init__(self, first, shard_groups, w_gate_up, w_down, after, wmv):
        self.wmv, self.done = wmv, {}
        self.first = _RelayGather("ag_first", [first], after)
        self.groups, self.shards = {}, []
        for key, ws in shard_groups.items():
            self.groups[key] = list(range(len(self.shards), len(self.shards) + len(ws)))
            self.shards += list(ws)
        self.w_gate_up, self.w_down = w_gate_up, w_down
        self.passed, self.scatters = set(), {}

    def start_deps(self):
        return [self.first.token]

    def first_weights(self, after):
        deps = self.first.forward(after)
        self.gather = _Gather("ag", self.shards, deps)
        self.relay = _RelayGather("ag_gu", [self.w_gate_up], [self.gather.token])
        return self.first.get(self.first.pass_on([self.relay.token]))

    def relay_forward(self, after):
        deps = self.relay.forward(after)
        self.gather_down = _Gather("ag_dn", [self.w_down], deps)
        return [self.gather_down.token]

    def pass_on(self, key, after):
        self.passed.add(key)
        if key == "gate_up":
            return self.relay.pass_on(after)
        return self.gather.pass_on(self.groups[key], after)

    def weights(self, key, after):
        if key == "down":
            return self.gather_down.get([0], self.gather_down.pass_on([0], after))
        if key not in self.passed:
            after = self.pass_on(key, after)
        return self.relay.get(after) if key == "gate_up" else self.gather.get(self.groups[key], after)

    def send(self, tag, grads):
        blocks = [g if g.ndim == 3 else g.reshape(N_DEV, g.shape[0] // N_DEV, g.shape[1]) for g in grads]
        self.scatters[tag] = _Scatter("rs_" + tag, blocks, ())
        return [self.scatters[tag].token]

    def mid(self, tag, after):
        self.scatters[tag].mid(after)
        return [self.scatters[tag].token]

    def finish_group(self, tag, after):
        self.done[tag] = self.scatters.pop(tag).end(after, self.wmv[tag])
        first = self.done[tag][0]
        return [first[0] if isinstance(first, tuple) else first]

    def finish(self, after):
        for tag in list(self.scatters):
            self.finish_group(tag, after)
        return self.done


def _local_step(xs, ms, tgt, norm_mix, fox_f_bias, fox_q_norm, fox_k_norm, gdn_a_log, gdn_dt_bias, gdn_out_norm,
                mem_norm, mem_q_norm, mem_k_norm, norm_ffn, cshard, comm):
    t, d = xs.shape
    bq = min(t, 256)
    fb, alog, dtb = _lanes(fox_f_bias, L_FF), _lanes(gdn_a_log, L_GA), _lanes(gdn_dt_bias, L_GA)
    flat = lambda w: w.reshape(-1, w.shape[-1])

    rms1 = lambda a, g: (_rms(a, g),)
    (u,) = _rowwise("norm_mix", rms1, [xs], [norm_mix], [(d, BF16)], min(t, 256), deps=comm.start_deps())
    w_in_b = flat(comm.first_weights([u])[0])
    pb = _matmul("proj_in_b", u, w_in_b, NN, F32, 1024, 768)
    o_fox = _fox_fwd(pb, fb, fox_q_norm, fox_k_norm, bq)
    w_in_a, conv_parts = comm.weights("in_a", [o_fox])
    w_in_a = flat(w_in_a)
    conv_all = conv_parts[:, :4, :cshard].transpose(1, 0, 2).reshape(4, N_DEV * cshard)
    pa = _matmul("proj_in_a", u, w_in_a, NN, F32, 1024, 768)
    smrow = (pb, HD, SM)
    (gates,) = _rowwise("gdn_gates", _gdn_gates, [smrow], [alog, dtb], [(HD, F32)], min(t, 256))
    gdn_terms, gdn_qkv = _gdn_fwd(pa, gates, conv_all)
    o_gdn_raw, gdn_states = _gdn_scan(gdn_terms)
    gdn_saved = list(gdn_terms) + [gdn_states]
    deps = comm.relay_forward([o_gdn_raw])
    zrow = (pa, NG * HD, GZ * HD // (NG * HD))
    (o_gdn,) = _rowwise("gdn_post", _gdn_post, [o_gdn_raw, zrow], [gdn_out_norm], [(NG * HD, BF16)], min(t, 256),
                        deps=deps)
    w_kv_all, w_out_all = [flat(w) for w in comm.weights("kv_out", [o_gdn])]
    (mem_n,) = _rowwise("norm_mem", rms1, [ms], [mem_norm], [(d, BF16)], ms.shape[0])
    mkv = _matmul("proj_mem", mem_n, w_kv_all, NN, F32, 256, 512)
    o_mem = _mem_fwd(pb, mkv, mem_q_norm, mem_k_norm)
    deps = comm.pass_on("gate_up", [o_mem])
    mix = jnp.concatenate([o_fox, o_gdn, o_mem], axis=1)
    h1, h1n = _proj_out_norm(mix, w_out_all, xs, norm_ffn, deps)
    (wgu,) = comm.weights("gate_up", [h1n])
    ffw = wgu.shape[2]
    gu, act = _ffn_up(h1n, wgu.reshape(2, 4, d, ffw))
    w_down_all = flat(comm.weights("down", [act])[0])
    dyb, lsum = _ffn_down_loss(act, w_down_all, h1, tgt)
    loss_local = (0.5 / d) * jnp.sum(lsum[::8, ::HD])

    dgu = _ffn_down_bwd(dyb, w_down_all.reshape(4, ffw, d), gu).reshape(8, t, ffw)
    g_w_down = _matmul("grad_w_down", act, dyb, TN, BF16, 512, 2048)
    g_w_gu = _ffn_up_bwd_w(h1n, dgu)
    deps = comm.send("ffn", [g_w_down, g_w_gu])
    rms2 = lambda a, g: (_rms(a, g), a)
    dh1b, g_norm_ffn = _ffn_up_bwd_x(dgu, wgu, h1, norm_ffn, dyb, deps)

    dmix = _matmul("proj_out_bwd_x", dh1b, w_out_all, NT, BF16, 1024, 1024)
    g_w_out = _matmul("grad_w_out", mix, dh1b, TN, BF16, 1024, 2048)
    deps = comm.mid("ffn", [dmix, g_w_out])
    dmq, dmk, dmv, g_mqn, g_mkn = _mem_bwd(pb, mkv, mem_q_norm, mem_k_norm, dmix, deps=deps)
    dmkv = jnp.concatenate([dmk, dmv], axis=1).astype(BF16)
    g_w_kv = _matmul("grad_w_kv", mem_n, dmkv, TN, BF16, 512, 512)
    do_raw, dgz, g_gon = _rowwise_vjp("gdn_post_bwd", _gdn_post, [o_gdn_raw, zrow], [gdn_out_norm],
                                      [(dmix, NG * HD, 1)], [F32, BF16], min(t, 256), deps=deps)
    dterms = _gdn_bwd_scan(gdn_saved, do_raw)
    dgq, dgk, dgv, dgates, dwq, dwk, dwv = _gdn_bwd(pa, gates, conv_all, dterms, gdn_qkv)
    dsm_gdn, g_alog, g_dtb = _rowwise_vjp("gdn_gates_bwd", _gdn_gates, [smrow], [alog, dtb], [dgates], [F32], min(t, 256))
    dp_a = jnp.concatenate([dgq, dgk, dgv, dgz], axis=1)
    g_w_in_a = _matmul("grad_w_in_a", u, dp_a, TN, BF16, 512, 3072)
    deps = comm.send("a", [g_w_in_a, g_w_out, g_w_kv])
    du_a = _matmul("proj_in_bwd_a", dp_a, w_in_a, NT, F32, 1024, 1024, deps=deps)
    deps = comm.mid("a", [du_a])
    dfq, dfk, dfv, dsm_fox, g_fb, g_fqn, g_fkn = _fox_bwd(pb, fb, fox_q_norm, fox_k_norm, dmix, 2 * bq if t % (2 * bq) == 0 else bq,
                                                          deps=deps)
    dp_b = jnp.concatenate([dfq, dfk, dfv, dmq, (dsm_fox + dsm_gdn).astype(BF16), jnp.zeros((t, HD), BF16)], axis=1)
    g_w_in_b = _matmul("grad_w_in_b", u, dp_b, TN, BF16, 512, 3072)
    deps = comm.mid("b", comm.finish_group("ffn", comm.send("b", [g_w_in_b])))
    deps = comm.finish_group("a", deps)
    dmem_n = _matmul("proj_mem_bwd_x", dmkv, w_kv_all, NT, F32, 256, 512, deps=deps)
    g_mem_norm = _rowwise_vjp("norm_mem_bwd", rms1, [ms], [mem_norm], [dmem_n], [], ms.shape[0])[0]
    grad_x, g_norm_mix = _proj_in_bwd_norm(dp_b, w_in_b, du_a, xs, norm_mix, dh1b, [g_mem_norm])

    small_grads = {
        "norm_mix": g_norm_mix, "mem_norm": g_mem_norm, "norm_ffn": g_norm_ffn,
        "gdn_conv": jnp.concatenate([dwq, dwk, dwv], axis=1),
        "fox_q_norm": g_fqn, "fox_k_norm": g_fkn, "gdn_out_norm": g_gon, "mem_q_norm": g_mqn, "mem_k_norm": g_mkn,
        "fox_f_bias": g_fb, "gdn_a_log": g_alog, "gdn_dt_bias": g_dtb}
    return grad_x, loss_local, small_grads
```

```python
import functools
import math

import jax
import jax.numpy as jnp
from jax import lax
from jax.experimental import pallas as pl
from jax.experimental.pallas import tpu as pltpu

F32 = jnp.float32
BF16 = jnp.bfloat16
SDS = jax.ShapeDtypeStruct

N_DEV = 8
HD = 128
NF, NG, NM = 6, 6, 4
CHUNK = 64
GROUP = 16
NORM_EPS = 1e-6
GQ, GK, GV, GZ = 0, 6, 12, 18
FQ, FK, FV, MQ, SM = 0, 6, 12, 18, 22
HALF = 24 * HD
L_FF, L_GA, L_GB = 0, 6, 12
VMEM_LIMIT = 56 * 1024 * 1024

ADAM_LR, ADAM_B1, ADAM_B2, ADAM_EPS, ADAM_WD, ADAM_STEP = 0.001, 0.9, 0.999, 1e-08, 0.01, 10

NN = (((1,), (0,)), ((), ()))
NT = (((1,), (1,)), ((), ()))
TN = (((0,), (0,)), ((), ()))
MESH = pl.DeviceIdType.MESH


def _cp(*sem):
    return pltpu.CompilerParams(dimension_semantics=tuple(sem) if sem else None, vmem_limit_bytes=VMEM_LIMIT)


def _dot(a, b, dims=NN):
    return lax.dot_general(a, b, dims, preferred_element_type=F32)


def _iota(shape, axis):
    return lax.broadcasted_iota(jnp.int32, shape, axis)


def _rms(x, gain):
    return x * lax.rsqrt(jnp.mean(x * x, axis=-1, keepdims=True) + NORM_EPS) * gain


def _sigmoid(x):
    return 0.5 * jnp.tanh(0.5 * x) + 0.5


def _silu(x):
    return x * _sigmoid(x)


def _softplus(x):
    return jnp.maximum(x, 0.0) + jnp.log(1.0 + jnp.exp(-jnp.abs(x)))


def _lane_pick(x, lane):
    oh = (_iota((1, x.shape[-1]), 1) == lane).astype(F32)
    return jnp.sum(x * oh, axis=-1, keepdims=True)


def _cumsum_rows(x):
    tril = (_iota((HD, HD), 0) >= _iota((HD, HD), 1)).astype(F32)
    carry = jnp.zeros((1, x.shape[1]), F32)
    outs = []
    for b in range(x.shape[0] // HD):
        blk = x[b * HD:(b + 1) * HD]
        outs.append(_pdot(tril, blk, "nn", "xa") + carry)
        carry = carry + jnp.sum(blk, axis=0, keepdims=True)
    return jnp.concatenate(outs, axis=0)


def _row_spec(r, tm):
    if isinstance(r, tuple):
        arr, width, cb = r
        return arr, pl.BlockSpec((tm, width), lambda i, cb=cb: (i, cb))
    return r, pl.BlockSpec((tm, r.shape[1]), lambda i: (i, 0))


ANY_SPEC = pl.BlockSpec(memory_space=pl.ANY)


def _rowwise(name, fn, rows, consts, outs, tm, deps=()):
    arrs, specs = zip(*[_row_spec(r, tm) for r in rows])
    n_rows = arrs[0].shape[0]
    nr, nc, nd = len(rows), len(consts), len(deps)

    def body(*refs):
        res = fn(*[r[...] for r in refs[:nr + nc]])
        for o, v in zip(refs[nr + nc + nd:], res):
            o[...] = v.astype(o.dtype)

    return pl.pallas_call(
        body, grid=(n_rows // tm,), name=name,
        in_specs=list(specs) + [pl.BlockSpec(c.shape, lambda i: (0, 0)) for c in consts] + [ANY_SPEC] * nd,
        out_specs=[pl.BlockSpec((tm, w), lambda i: (i, 0)) for w, _ in outs],
        out_shape=[SDS((n_rows, w), dt) for w, dt in outs],
        compiler_params=_cp("parallel"),
    )(*arrs, *consts, *deps)


def _rowwise_vjp(name, fn, rows, consts, cts, grad_dtypes, tm, deps=()):
    arrs, specs = zip(*[_row_spec(r, tm) for r in rows])
    ct_arrs, ct_specs = zip(*[_row_spec(r, tm) for r in cts])
    n_rows = arrs[0].shape[0]
    nr, nc, nct, nd = len(rows), len(consts), len(cts), len(deps)
    plan = [(j, dt) for j, dts in enumerate(grad_dtypes) for dt in (dts if isinstance(dts, tuple) else (dts,))]
    ng = len(plan)
    widths = [specs[j].block_shape[1] for j, _ in plan]
    grad_dtypes = [dt for _, dt in plan]

    def body(*refs):
        vals = [r[...].astype(F32) for r in refs[:nr + nc]]
        ctv = tuple(r[...].astype(F32) for r in refs[nr + nc:nr + nc + nct])
        _, vjp = jax.vjp(fn, *vals)
        grads = vjp(ctv)
        outs = refs[nr + nc + nct + nd:]
        for o, (j, _) in zip(outs[:ng], plan):
            o[...] = grads[j].astype(o.dtype)

        @pl.when(pl.program_id(0) == 0)
        def _():
            for o in outs[ng:]:
                o[...] = jnp.zeros_like(o)

        for o, g in zip(outs[ng:], grads[nr:]):
            o[...] += g

    return pl.pallas_call(
        body, grid=(n_rows // tm,), name=name,
        in_specs=list(specs) + [pl.BlockSpec(c.shape, lambda i: (0, 0)) for c in consts] + list(ct_specs)
        + [ANY_SPEC] * nd,
        out_specs=[pl.BlockSpec((tm, w), lambda i: (i, 0)) for w in widths]
        + [pl.BlockSpec(c.shape, lambda i: (0, 0)) for c in consts],
        out_shape=[SDS((n_rows, w), dt) for w, dt in zip(widths, grad_dtypes)] + [SDS(c.shape, F32) for c in consts],
        compiler_params=_cp("arbitrary"),
    )(*arrs, *consts, *ct_arrs, *deps)


def _tile(n, pref):
    t = min(n, pref)
    while n % t or (t % HD and t != n):
        t -= 1
    return t


def _matmul(name, a, b, dims, out_dtype, tm, tn, residual=None, deps=()):
    ta, tb = dims == TN, dims == NT
    m = a.shape[1] if ta else a.shape[0]
    k = a.shape[0] if ta else a.shape[1]
    n = b.shape[0] if tb else b.shape[1]
    tm, tn = _tile(m, tm), _tile(n, tn)

    def body(*refs):
        acc = _dot(refs[0][...], refs[1][...], dims)
        if residual is not None:
            acc = acc + refs[2][...]
        refs[-1][...] = acc.astype(out_dtype)

    in_specs = [pl.BlockSpec((k, tm), lambda i, j: (0, i)) if ta else pl.BlockSpec((tm, k), lambda i, j: (i, 0)),
                pl.BlockSpec((tn, k), lambda i, j: (j, 0)) if tb else pl.BlockSpec((k, tn), lambda i, j: (0, j))]
    ops = [a, b]
    if residual is not None:
        in_specs.append(pl.BlockSpec((tm, tn), lambda i, j: (i, j)))
        ops.append(residual)
    in_specs += [ANY_SPEC] * len(deps)
    ops += list(deps)
    return pl.pallas_call(
        body, grid=(m // tm, n // tn), name=name, in_specs=in_specs,
        out_specs=pl.BlockSpec((tm, tn), lambda i, j: (i, j)), out_shape=SDS((m, n), out_dtype),
        compiler_params=_cp("parallel", "parallel"),
    )(*ops)


def _proj_out_norm(mix, w_out, xs, gain, deps):
    t, k = mix.shape
    d = w_out.shape[1]
    tm = _tile(t, 512)

    def body(*refs):
        a, b, x, g = refs[:4]
        h1, h1n = refs[4 + len(deps):]
        acc = _dot(a[...], b[...]) + x[...]
        h1[...] = acc
        h1n[...] = _rms(acc, g[...]).astype(BF16)

    return pl.pallas_call(
        body, grid=(t // tm,), name="proj_out",
        in_specs=[pl.BlockSpec((tm, k), lambda i: (i, 0)), pl.BlockSpec((k, d), lambda i: (0, 0)),
                  pl.BlockSpec((tm, d), lambda i: (i, 0)), pl.BlockSpec((1, d), lambda i: (0, 0))] + [ANY_SPEC] * len(deps),
        out_specs=[pl.BlockSpec((tm, d), lambda i: (i, 0))] * 2, out_shape=[SDS((t, d), F32), SDS((t, d), BF16)],
        compiler_params=_cp("parallel"),
    )(mix, w_out, xs, gain, *deps)


def _proj_in_bwd_norm(dp, w, du_a, xs, gain, dh1b, deps):
    t, k = dp.shape
    d = w.shape[0]
    tm = _tile(t, 256)

    def body(*refs):
        a, b, ua, x, g, dh = refs[:6]
        gx, dgain = refs[6 + len(deps):]
        _, vjp = jax.vjp(lambda xx, gn: _rms(xx, gn), x[...], g[...])
        dx, dg = vjp(_dot(a[...], b[...], NT) + ua[...])
        gx[...] = dx + dh[...].astype(F32)

        @pl.when(pl.program_id(0) == 0)
        def _():
            dgain[...] = jnp.zeros_like(dgain)

        dgain[...] += dg

    row = pl.BlockSpec((tm, d), lambda i: (i, 0))
    vec = pl.BlockSpec((1, d), lambda i: (0, 0))
    return pl.pallas_call(
        body, grid=(t // tm,), name="proj_in_bwd_b",
        in_specs=[pl.BlockSpec((tm, k), lambda i: (i, 0)), pl.BlockSpec((d, k), lambda i: (0, 0), pipeline_mode=ONE_BUFFER),
                  row, row, vec, row] + [ANY_SPEC] * len(deps),
        out_specs=[row, vec], out_shape=[SDS((t, d), F32), SDS((1, d), F32)], compiler_params=_cp("arbitrary"),
    )(dp, w, du_a, xs, gain, dh1b, *deps)


def _ffn_up(h1n, wgu):
    t, d = h1n.shape
    w = wgu.shape[3]
    tm = _tile(t, 512)

    def body(a, b, gu, act):
        x = a[...]
        g = _dot(x, b[0])
        u = _dot(x, b[1])
        gu[0] = g.astype(BF16)
        gu[1] = u.astype(BF16)
        act[...] = (_silu(g) * u).astype(BF16)

    return pl.pallas_call(
        body, grid=(4, t // tm), name="ffn_up",
        in_specs=[pl.BlockSpec((tm, d), lambda j, i: (i, 0)), pl.BlockSpec((2, None, d, w), lambda j, i: (0, j, 0, 0))],
        out_specs=[pl.BlockSpec((2, None, tm, w), lambda j, i: (0, j, i, 0)), pl.BlockSpec((tm, w), lambda j, i: (i, j))],
        out_shape=[SDS((2, 4, t, w), BF16), SDS((t, 4 * w), BF16)],
        compiler_params=_cp("parallel", "parallel"),
    )(h1n, wgu)


def _ffn_down_loss(act, wdown, h1, target):
    t, f = act.shape
    d = wdown.shape[1]
    tm, tn = _tile(t, 1024), _tile(d, 512)

    def body(a, b, h, tg, dyb, ls):
        e = _dot(a[...], b[...]) + h[...] - tg[...]
        dyb[...] = (e * (1.0 / d)).astype(BF16)
        ls[...] = jnp.broadcast_to(jnp.sum(e * e), (8, HD))

    return pl.pallas_call(
        body, grid=(t // tm, d // tn), name="ffn_down_loss",
        in_specs=[pl.BlockSpec((tm, f), lambda i, j: (i, 0)), pl.BlockSpec((f, tn), lambda i, j: (0, j)),
                  pl.BlockSpec((tm, tn), lambda i, j: (i, j)), pl.BlockSpec((tm, tn), lambda i, j: (i, j))],
        out_specs=[pl.BlockSpec((tm, tn), lambda i, j: (i, j)), pl.BlockSpec((8, HD), lambda i, j: (i, j))],
        out_shape=[SDS((t, d), BF16), SDS((8 * (t // tm), HD * (d // tn)), F32)],
        compiler_params=_cp("parallel", "parallel"),
    )(act, wdown, h1, target)


def _ffn_down_bwd(dyb, wdown4, gu):
    t, d = dyb.shape
    w = wdown4.shape[1]
    tm = _tile(t, 512)

    def body(a, b, gu_ref, out):
        da = _dot(a[...], b[...], NT)
        g = gu_ref[0].astype(F32)
        u = gu_ref[1].astype(F32)
        s = _sigmoid(g)
        out[0] = (da * u * (s * (1.0 + g * (1.0 - s)))).astype(BF16)
        out[1] = (da * g * s).astype(BF16)

    return pl.pallas_call(
        body, grid=(4, t // tm), name="ffn_down_bwd",
        in_specs=[pl.BlockSpec((tm, d), lambda j, i: (i, 0)), pl.BlockSpec((None, w, d), lambda j, i: (j, 0, 0)),
                  pl.BlockSpec((2, None, tm, w), lambda j, i: (0, j, i, 0))],
        out_specs=pl.BlockSpec((2, None, tm, w), lambda j, i: (0, j, i, 0)),
        out_shape=SDS((2, 4, t, w), BF16),
        compiler_params=_cp("parallel", "parallel"),
    )(dyb, wdown4, gu)


def _ffn_up_bwd_x(dgu, wgu, h1, gain, dyb, deps):
    _, t, w = dgu.shape
    d = wgu.shape[1]
    tm = _tile(t, 512)

    def body(*refs):
        a, b, h, g, dy = refs[:5]
        dh1, dgain, acc = refs[5 + len(deps):]
        i, j = pl.program_id(0), pl.program_id(1)

        @pl.when(j == 0)
        def _():
            acc[...] = jnp.zeros_like(acc)

        acc[...] += _dot(a[...], b[...], NT)

        @pl.when(j == N_DEV - 1)
        def _():
            _, vjp = jax.vjp(lambda x, gn: _rms(x, gn), h[...], g[...])
            dx, dg = vjp(acc[...])
            dh1[...] = (dx + dy[...].astype(F32)).astype(dh1.dtype)

            @pl.when(i == 0)
            def _():
                dgain[...] = jnp.zeros_like(dgain)

            dgain[...] += dg

    row = pl.BlockSpec((tm, d), lambda i, j: (i, 0))
    return pl.pallas_call(
        body, grid=(t // tm, N_DEV), name="ffn_up_bwd_x",
        in_specs=[pl.BlockSpec((None, tm, w), lambda i, j: (j, i, 0)), pl.BlockSpec((None, d, w), lambda i, j: (j, 0, 0)),
                  row, pl.BlockSpec((1, d), lambda i, j: (0, 0)), row] + [ANY_SPEC] * len(deps),
        out_specs=[row, pl.BlockSpec((1, d), lambda i, j: (0, 0))],
        out_shape=[SDS((t, d), BF16), SDS((1, d), F32)], scratch_shapes=[pltpu.VMEM((tm, d), F32)],
        compiler_params=_cp("arbitrary", "arbitrary"),
    )(dgu, wgu, h1, gain, dyb, *deps)


def _ffn_up_bwd_w(h1n, dgu):
    _, t, w = dgu.shape
    d = h1n.shape[1]
    tm = _tile(d, 512)

    def body(a, b, out):
        out[...] = _dot(a[...], b[...], TN).astype(BF16)

    return pl.pallas_call(
        body, grid=(8, d // tm), name="ffn_up_bwd_w",
        in_specs=[pl.BlockSpec((t, tm), lambda j, i: (0, i)), pl.BlockSpec((None, t, w), lambda j, i: (j, 0, 0))],
        out_specs=pl.BlockSpec((None, tm, w), lambda j, i: (j, i, 0)), out_shape=SDS((8, d, w), BF16),
        compiler_params=_cp("parallel", "parallel"),
    )(h1n, dgu)


def _fox_prep(fq, fk, sm, fb, qg, kg, h):
    qn = _rms(fq, qg)
    kn = _rms(fk, kg)
    c = _cumsum_rows(-_softplus(-(sm + fb)))
    ccol = _lane_pick(c, L_FF + h)
    crow = jnp.sum(c.T * (_iota((HD, 1), 0) == L_FF + h).astype(F32), axis=0, keepdims=True)
    return qn, kn, ccol, crow


def _softmax_times(s, v):
    e = jnp.exp(s - lax.stop_gradient(jnp.max(s, axis=1, keepdims=True)))
    return _dot(e.astype(BF16), v.astype(BF16)) * (1.0 / jnp.sum(e, axis=1, keepdims=True))


def _fox_block(q, k, v, cc, cr, off):
    bq = q.shape[0]
    assert k.shape[0] == off + bq
    s = _dot((q * (HD ** -0.5)).astype(BF16), k.astype(BF16), NT) + cc - cr
    diag = jnp.where(_iota((bq, bq), 1) <= _iota((bq, bq), 0), s[:, off:], -1e30)
    s = jnp.concatenate([s[:, :off], diag], axis=1) if off else diag
    return _softmax_times(s, v)


ONE_BUFFER = pl.Buffered(1)


def _pcol(t, cb):
    return pl.BlockSpec((t, HD), lambda h, cb=cb: (0, cb + h), pipeline_mode=ONE_BUFFER)


def _smcol(t):
    return pl.BlockSpec((t, HD), lambda h: (0, SM), pipeline_mode=ONE_BUFFER)


def _head(t):
    return pl.BlockSpec((t, HD), lambda h: (0, h), pipeline_mode=ONE_BUFFER)


def _small(n):
    return pl.BlockSpec((n, HD), lambda h: (0, 0), pipeline_mode=ONE_BUFFER)


def _fox_fwd(p, fb, qg, kg, bq):
    t = p.shape[0]

    def body(fq, fk, fv, sm, fb_r, qg_r, kg_r, o, qn_s, cc_s):
        h = pl.program_id(0)
        qn, kn, ccol, crow = _fox_prep(fq[...], fk[...], sm[...], fb_r[...], qg_r[...], kg_r[...], h)
        qn_s[...] = qn
        cc_s[...] = ccol
        knb = kn.astype(BF16)
        vb = fv[...].astype(BF16)
        for i in range(t // bq):
            rows, ext = pl.ds(i * bq, bq), (i + 1) * bq
            o[rows, :] = _fox_block(qn_s[rows, :], knb[:ext], vb[:ext], cc_s[rows, :], crow[:, :ext], i * bq).astype(o.dtype)

    return pl.pallas_call(
        body, grid=(NF,), name="fox_fwd",
        in_specs=[_pcol(t, FQ), _pcol(t, FK), _pcol(t, FV), _smcol(t), _small(1), _small(1), _small(1)],
        out_specs=_head(t), out_shape=SDS((t, NF * HD), BF16),
        scratch_shapes=[pltpu.VMEM((t, HD), F32), pltpu.VMEM((t, 1), F32)],
        compiler_params=_cp("parallel"),
    )(p, p, p, p, fb, qg, kg)


def _fox_bwd(p, fb, qg, kg, dmix, bq, deps=()):
    t = p.shape[0]

    def body(*refs):
        fq, fk, fv, sm, fb_r, qg_r, kg_r, do = refs[:8]
        dfq, dfk, dfv, dsm, dfb, dqg, dkg, qn_s, cc_s, dqn_s, dcc_s, dkn_s, dv_s, dcr_s = refs[8 + len(deps):]
        h = pl.program_id(0)
        qn, kn, ccol, crow = _fox_prep(fq[...], fk[...], sm[...], fb_r[...], qg_r[...], kg_r[...], h)
        qn_s[...] = qn
        cc_s[...] = ccol
        v = fv[...]
        dkn_s[...] = jnp.zeros_like(dkn_s)
        dv_s[...] = jnp.zeros_like(dv_s)
        dcr_s[...] = jnp.zeros_like(dcr_s)

        for i in range(t // bq):
            rows, ext = pl.ds(i * bq, bq), (i + 1) * bq
            _, vjp = jax.vjp(lambda a, b, c, d, e, off=i * bq: _fox_block(a, b, c, d, e, off),
                             qn_s[rows, :], kn[:ext], v[:ext], cc_s[rows, :], crow[:, :ext])
            dq, dk, dv, dcc, dcr = vjp(do[rows, :].astype(F32))
            dqn_s[rows, :] = dq
            dcc_s[rows, :] = dcc
            dkn_s[:ext, :] += dk
            dv_s[:ext, :] += dv
            dcr_s[:, :ext] += dcr
        _, prep_vjp = jax.vjp(lambda a, b, c, d, e, f: _fox_prep(a, b, c, d, e, f, h),
                              fq[...], fk[...], sm[...], fb_r[...], qg_r[...], kg_r[...])
        g_fq, g_fk, g_sm, g_fb, g_qg, g_kg = prep_vjp((dqn_s[...], dkn_s[...], dcc_s[...], dcr_s[...]))
        dfq[...] = g_fq.astype(dfq.dtype)
        dfk[...] = g_fk.astype(dfk.dtype)
        dfv[...] = dv_s[...].astype(dfv.dtype)

        @pl.when(h == 0)
        def _():
            for r in (dsm, dfb, dqg, dkg):
                r[...] = jnp.zeros_like(r)

        dsm[...] += g_sm
        dfb[...] += g_fb
        dqg[...] += g_qg
        dkg[...] += g_kg

    head = _head(t)
    return pl.pallas_call(
        body, grid=(NF,), name="fox_bwd",
        in_specs=[_pcol(t, FQ), _pcol(t, FK), _pcol(t, FV), _smcol(t), _small(1), _small(1), _small(1), head]
        + [ANY_SPEC] * len(deps),
        out_specs=[head, head, head, _small(t), _small(1), _small(1), _small(1)],
        out_shape=[SDS((t, NF * HD), BF16)] * 3 + [SDS((t, HD), F32)] + [SDS((1, HD), F32)] * 3,
        scratch_shapes=[pltpu.VMEM((t, HD), F32), pltpu.VMEM((t, 1), F32), pltpu.VMEM((t, HD), F32),
                        pltpu.VMEM((t, 1), F32), pltpu.VMEM((t, HD), F32), pltpu.VMEM((t, HD), F32),
                        pltpu.VMEM((1, t), F32)],
        compiler_params=_cp("arbitrary"),
    )(p, p, p, p, fb, qg, kg, dmix, *deps)


def _mem_attn(mq, mk, mv, qg, kg):
    s = _dot((_rms(mq, qg) * (HD ** -0.5)).astype(BF16), _rms(mk, kg).astype(BF16), NT)
    return _softmax_times(s, mv)


def _mem_fwd(p, mkv, qg, kg):
    t, ml = p.shape[0], mkv.shape[0]

    def body(mq, mk, mv, qg_r, kg_r, o):
        o[...] = _mem_attn(mq[...], mk[...], mv[...], qg_r[...], kg_r[...]).astype(o.dtype)

    return pl.pallas_call(
        body, grid=(NM,), name="mem_fwd",
        in_specs=[_pcol(t, MQ), pl.BlockSpec((ml, HD), lambda h: (0, h)), pl.BlockSpec((ml, HD), lambda h: (0, NM + h)),
                  _small(1), _small(1)],
        out_specs=pl.BlockSpec((t, HD), lambda h: (0, h)), out_shape=SDS((t, NM * HD), BF16),
        compiler_params=_cp("parallel"),
    )(p, mkv, mkv, qg, kg)


def _mem_bwd(p, mkv, qg, kg, dmix, deps=()):
    t, ml = p.shape[0], mkv.shape[0]

    def body(*refs):
        mq, mk, mv, qg_r, kg_r, do = refs[:6]
        dmq, dmk, dmv, dqg, dkg = refs[6 + len(deps):]
        _, vjp = jax.vjp(_mem_attn, mq[...], mk[...], mv[...], qg_r[...], kg_r[...])
        g_q, g_k, g_v, g_qg, g_kg = vjp(do[...].astype(F32))
        dmq[...] = g_q.astype(dmq.dtype)
        dmk[...] = g_k
        dmv[...] = g_v

        @pl.when(pl.program_id(0) == 0)
        def _():
            dqg[...] = jnp.zeros_like(dqg)
            dkg[...] = jnp.zeros_like(dkg)

        dqg[...] += g_qg
        dkg[...] += g_kg

    return pl.pallas_call(
        body, grid=(NM,), name="mem_bwd",
        in_specs=[_pcol(t, MQ), pl.BlockSpec((ml, HD), lambda h: (0, h)), pl.BlockSpec((ml, HD), lambda h: (0, NM + h)),
                  _small(1), _small(1), pl.BlockSpec((t, HD), lambda h: (0, NF + NG + h))] + [ANY_SPEC] * len(deps),
        out_specs=[pl.BlockSpec((t, HD), lambda h: (0, h)), pl.BlockSpec((ml, HD), lambda h: (0, h)),
                   pl.BlockSpec((ml, HD), lambda h: (0, h)), _small(1), _small(1)],
        out_shape=[SDS((t, NM * HD), BF16), SDS((ml, NM * HD), F32), SDS((ml, NM * HD), F32),
                   SDS((1, HD), F32), SDS((1, HD), F32)],
        compiler_params=_cp("arbitrary"),
    )(p, mkv, mkv, qg, kg, dmix, *deps)


def _shift_down(x, s):
    if s == 0:
        return x
    return jnp.where(_iota(x.shape, 0) >= s, pltpu.roll(x, s, 0), 0.0)


def _shift_up(x, s):
    if s == 0:
        return x
    n = x.shape[0]
    return jnp.where(_iota(x.shape, 0) < n - s, pltpu.roll(x, n - s, 0), 0.0)


@jax.custom_vjp
def _conv4(x, w0, w1, w2, w3):
    return w0 * _shift_down(x, 3) + w1 * _shift_down(x, 2) + w2 * _shift_down(x, 1) + w3 * x


def _conv4_fwd(x, w0, w1, w2, w3):
    return _conv4(x, w0, w1, w2, w3), (x, w0, w1, w2, w3)


def _conv4_bwd(res, dy):
    x, w0, w1, w2, w3 = res
    ups = [_shift_up(dy, 3 - k) for k in range(4)]
    dx = w0 * ups[0] + w1 * ups[1] + w2 * ups[2] + w3 * ups[3]
    return (dx,) + tuple(jnp.sum(up * x, axis=0, keepdims=True) for up in ups)


_conv4.defvjp(_conv4_fwd, _conv4_bwd)


HALO = 8


def _gdn_gates(sm, alog, dtb):
    lane = _iota((1, HD), 1)
    g = -jnp.exp(alog) * _softplus(sm + dtb)
    return (jnp.where((lane >= L_GA) & (lane < L_GA + NG), g,
                      jnp.where((lane >= L_GB) & (lane < L_GB + NG), _sigmoid(sm), 0.0)),)


def _gdn_prep(gq, gk, gv, gates, taps, h):
    q, k, v = [_silu(_conv4(x, *taps[4 * j:4 * j + 4]))[HALO:] for j, x in enumerate((gq, gk, gv))]
    q = q * lax.rsqrt(jnp.sum(q * q, axis=-1, keepdims=True) + NORM_EPS) * (HD ** -0.5)
    k = k * lax.rsqrt(jnp.sum(k * k, axis=-1, keepdims=True) + NORM_EPS)
    return q, k, v, _lane_pick(gates, L_GA + h), _lane_pick(gates, L_GB + h)


def _split(x, n):
    parts, rest = [], x
    for i in range(n):
        parts.append(rest.astype(BF16))
        if i + 1 < n:
            rest = rest - parts[-1].astype(F32)
    return parts


def _raw_dot(a, b, form):
    lead = a.ndim - 2
    ca, cb = {"nn": (1, 0), "nt": (1, 1), "tn": (0, 0)}[form]
    batch = ((0,), (0,)) if lead else ((), ())
    return lax.dot_general(a, b, (((ca + lead,), (cb + lead,)), batch), preferred_element_type=F32)


def _pdot_impl(a, b, form, mode):
    if mode == "1":
        return _raw_dot(a.astype(BF16), b.astype(BF16), form)
    if mode == "3":
        (ah, al), (bh, bl) = _split(a, 2), _split(b, 2)
        return _raw_dot(ah, bh, form) + (_raw_dot(al, bh, form) + _raw_dot(ah, bl, form))
    if mode == "xa":
        return sum(_raw_dot(a.astype(BF16), t, form) for t in reversed(_split(b, 3)))
    return sum(_raw_dot(t, b.astype(BF16), form) for t in reversed(_split(a, 3)))


@functools.partial(jax.custom_vjp, nondiff_argnums=(2, 3))
def _pdot(a, b, form, mode):
    return _pdot_impl(a, b, form, mode)


def _pdot_fwd(a, b, form, mode):
    return _pdot_impl(a, b, form, mode), (a, b)


def _pdot_bwd(form, mode, res, ct):
    a, b = res
    da_args, db_args = {"nn": ((ct, b, "nt"), (a, ct, "tn")), "nt": ((ct, b, "nn"), (ct, a, "tn")),
                        "tn": ((b, ct, "nt"), (a, ct, "nn"))}[form]

    def side(args, exact):
        if mode in ("1", "3"):
            return mode
        return "xa" if args[0] is exact else "xb"

    if mode == "xa":
        return jnp.zeros_like(a), _pdot_impl(*db_args, side(db_args, a))
    if mode == "xb":
        return _pdot_impl(*da_args, side(da_args, b)), jnp.zeros_like(b)
    return _pdot_impl(*da_args, mode), _pdot_impl(*db_args, mode)


_pdot.defvjp(_pdot_fwd, _pdot_bwd)

GDN_QK, GDN_INV, GDN_SCAN = "1", "1", "1"


@jax.custom_vjp
def _tri_inv(low):
    eye = (_iota((CHUNK, CHUNK), 0) == _iota((CHUNK, CHUNK), 1)).astype(F32)
    inv = eye - low
    pw = low
    for _ in range(5):
        pw = _pdot_impl(pw, pw, "nn", GDN_INV)
        inv = inv + _pdot_impl(inv, pw, "nn", GDN_INV)
    return inv


def _tri_inv_fwd(low):
    inv = _tri_inv(low)
    return inv, inv


def _tri_inv_bwd(inv, ct):
    return (-_pdot_impl(_pdot_impl(inv, ct, "tn", GDN_INV), inv, "nt", GDN_INV),)


_tri_inv.defvjp(_tri_inv_fwd, _tri_inv_bwd)


def _gdn_intra(q, k, v, g, beta):
    n = q.shape[0]
    r, c = _iota((CHUNK, CHUNK), 0), _iota((CHUNK, CHUNK), 1)
    tril, strict = r >= c, r > c
    trilf = jnp.broadcast_to(tril.astype(F32), (n, CHUNK, CHUNK))
    gcm = _pdot(trilf, jnp.broadcast_to(g, (n, CHUNK, CHUNK)), "nn", "xa")
    gcf = _pdot(trilf, jnp.broadcast_to(g, (n, CHUNK, HD)), "nn", "xa")
    lane0 = (_iota((1, 1, CHUNK), 2) == 0).astype(F32)
    gcr = _pdot(jnp.ones((n, CHUNK, CHUNK), F32), gcm * lane0, "nt", "xa")
    decay = jnp.where(tril, jnp.exp(jnp.where(tril, gcm - gcr, 0.0)), 0.0)
    egc = jnp.exp(gcf)
    kb = k * beta
    low = jnp.where(strict, _pdot(kb, k, "nt", GDN_QK) * decay, 0.0)
    inv = _tri_inv(low)
    u = _pdot(inv, v * beta, "nn", GDN_INV)
    w = _pdot(inv, kb * egc, "nn", GDN_INV)
    at = jnp.where(tril, _pdot(q, k, "nt", GDN_QK) * decay, 0.0)
    gl = jnp.sum(jnp.broadcast_to(g, (n, CHUNK, HD)), axis=1, keepdims=True)
    kd = k * jnp.exp(gl - gcf)
    return (_pdot(kd, w, "tn", GDN_SCAN), _pdot(kd, u, "tn", GDN_SCAN), q * egc - _pdot(at, w, "nn", GDN_SCAN),
            _pdot(at, u, "nn", GDN_SCAN), gl)


def _gdn_step(s, kw, ku, a, b, gl):
    return _pdot(a, s, "nn", GDN_SCAN) + b, s * jnp.exp(gl) - _pdot(kw, s, "nn", GDN_SCAN) + ku


SCAN_HEADS = 3
SCAN_UNROLL = 4


def _gdn_chunked_scratch(nc):
    big = pltpu.VMEM((nc, CHUNK, HD), F32)
    return [big, big, big, pltpu.VMEM((nc, CHUNK, 1), F32), pltpu.VMEM((nc, CHUNK, 1), F32)]


N_TERMS = 5


def _gdn_term_shapes(nc):
    return [(nc, HD, HD), (nc, HD, HD), (nc, CHUNK, HD), (nc, CHUNK, HD), (nc, 1, HD)]


def _per_head(shape, heads=None, one_buffer=True):
    lead = (None,) if heads is None else (heads,)
    return pl.BlockSpec(lead + tuple(shape), lambda h: (h,) + (0,) * len(shape),
                        pipeline_mode=ONE_BUFFER if one_buffer else None)


def _gdn_in_specs(t):
    cw = lambda cb: pl.BlockSpec((4, HD), lambda h, cb=cb: (0, cb + h))
    return [_pcol(t, GQ), _pcol(t, GK), _pcol(t, GV), _small(t), cw(0), cw(NG), cw(2 * NG)]


def _taps(wq, wk, wv):
    return tuple(w[k:k + 1, :] for w in (wq, wk, wv) for k in range(4))


def _prep_rows(t):
    return min(t, 256)


def _gdn_pad(srcs, pads):
    for src, pad in zip(srcs, pads):
        pad[0:HALO, :] = jnp.zeros((HALO, HD), F32)
        pad[HALO:, :] = src[...]


def _gdn_stage(pads, gates, taps, h, chunked):
    t = gates.shape[0]
    rows = _prep_rows(t)
    per = rows // CHUNK

    def tile(i, carry):
        r0 = pl.multiple_of(i * rows, rows)
        vals = _gdn_prep(*[p[pl.ds(r0, rows + HALO), :] for p in pads], gates[pl.ds(r0, rows), :], taps, h)
        for v, r in zip(vals, chunked):
            r[pl.ds(i * per, per)] = v.reshape(per, CHUNK, v.shape[-1])
        return carry

    lax.fori_loop(0, t // rows, tile, 0)


def _gdn_intra_all(chunked, intra):
    nc = chunked[0].shape[0]
    grp_n = math.gcd(nc, GROUP)

    def grp(i, carry):
        sl = pl.ds(pl.multiple_of(i * grp_n, grp_n), grp_n)
        for r, val in zip(intra, _gdn_intra(*[c[sl] for c in chunked])):
            r[sl] = val
        return carry

    lax.fori_loop(0, nc // grp_n, grp, 0)


def _gdn_fwd(pa, gates, conv):
    t = pa.shape[0]
    nc = t // CHUNK
    terms = _gdn_term_shapes(nc)

    def body(gq, gk, gv, gt, wq, wk, wv, *rest):
        h = pl.program_id(0)
        intra, chunked, pads = rest[:N_TERMS], rest[N_TERMS:N_TERMS + 5], rest[N_TERMS + 5:]
        _gdn_pad((gq, gk, gv), pads)
        _gdn_stage(pads, gt, _taps(wq, wk, wv), h, chunked)
        _gdn_intra_all(chunked, intra)

    qkv = [(nc, CHUNK, HD)] * 3
    outs = pl.pallas_call(
        body, grid=(NG,), name="gdn_fwd", in_specs=_gdn_in_specs(t),
        out_specs=[_per_head(sh, one_buffer=False) for sh in terms + qkv],
        out_shape=[SDS((NG,) + sh, F32) for sh in terms + qkv],
        scratch_shapes=_gdn_chunked_scratch(nc)[3:] + [pltpu.VMEM((t + HALO, HD), F32)] * 3, compiler_params=_cp("parallel"),
    )(pa, pa, pa, gates, conv, conv, conv)
    return list(outs[:N_TERMS]), list(outs[N_TERMS:])


def _gdn_scan(terms_in):
    nc = terms_in[0].shape[1]
    terms = _gdn_term_shapes(nc)

    def body(*refs):
        intra, o, states = refs[:N_TERMS], refs[N_TERMS], refs[N_TERMS + 1]

        def one(c, ss):
            rows = pl.ds(pl.multiple_of(c * CHUNK, CHUNK), CHUNK)
            loaded = [[r[hh, c] for r in intra] for hh in range(SCAN_HEADS)]
            res = [_gdn_step(ss[hh], *loaded[hh]) for hh in range(SCAN_HEADS)]
            for hh in range(SCAN_HEADS):
                states[hh, c] = ss[hh]
                o[rows, hh * HD:(hh + 1) * HD] = res[hh][0]
            return tuple(r[1] for r in res)

        per_trip = math.gcd(nc, SCAN_UNROLL)

        def step(i, ss):
            for k in range(per_trip):
                ss = one(per_trip * i + k, ss)
            return ss

        lax.fori_loop(0, nc // per_trip, step, tuple(jnp.zeros((HD, HD), F32) for _ in range(SCAN_HEADS)))

    return pl.pallas_call(
        body, grid=(NG // SCAN_HEADS,), name="gdn_scan", in_specs=[_per_head(sh, SCAN_HEADS) for sh in terms],
        out_specs=[pl.BlockSpec((nc * CHUNK, SCAN_HEADS * HD), lambda h: (0, h), pipeline_mode=ONE_BUFFER),
                   _per_head((nc, HD, HD), SCAN_HEADS)],
        out_shape=[SDS((nc * CHUNK, NG * HD), F32), SDS((NG, nc, HD, HD), F32)], compiler_params=_cp("parallel"),
    )(*terms_in)


def _gdn_bwd_scan(saved, do_raw):
    nc = saved[0].shape[1]
    terms = _gdn_term_shapes(nc)

    def body(*refs):
        intra, states, do, outs = refs[:N_TERMS], refs[N_TERMS], refs[N_TERMS + 1], refs[N_TERMS + 2:]

        def one(c, dss):
            rows = pl.ds(pl.multiple_of(c * CHUNK, CHUNK), CHUNK)
            loaded = [[states[hh, c]] + [r[hh, c] for r in intra] for hh in range(SCAN_HEADS)]
            cts = [do[rows, hh * HD:(hh + 1) * HD] for hh in range(SCAN_HEADS)]
            grads = [jax.vjp(_gdn_step, *loaded[hh])[1]((cts[hh], dss[hh])) for hh in range(SCAN_HEADS)]
            for hh in range(SCAN_HEADS):
                for r, gval in zip(outs, grads[hh][1:]):
                    r[hh, c] = gval
            return tuple(g[0] for g in grads)

        per_trip = math.gcd(nc, SCAN_UNROLL)

        def bwd(i, dss):
            c = nc - 1 - per_trip * i
            for k in range(per_trip):
                dss = one(c - k, dss)
            return dss

        lax.fori_loop(0, nc // per_trip, bwd, tuple(jnp.zeros((HD, HD), F32) for _ in range(SCAN_HEADS)))

    return pl.pallas_call(
        body, grid=(NG // SCAN_HEADS,), name="gdn_bwd_scan",
        in_specs=[_per_head(sh, SCAN_HEADS) for sh in terms] + [_per_head((nc, HD, HD), SCAN_HEADS)]
        + [pl.BlockSpec((nc * CHUNK, SCAN_HEADS * HD), lambda h: (0, h), pipeline_mode=ONE_BUFFER)],
        out_specs=[_per_head(sh, SCAN_HEADS) for sh in terms],
        out_shape=[SDS((NG,) + sh, F32) for sh in terms], compiler_params=_cp("parallel"),
    )(*saved, do_raw)


def _gdn_bwd(pa, gates, conv, dterms, qkv):
    t = pa.shape[0]
    nc = t // CHUNK
    terms = _gdn_term_shapes(nc)

    def body(*refs):
        gq, gk, gv, gt, wq, wk, wv = refs[:7]
        dintra, qkv = refs[7:7 + N_TERMS], refs[7 + N_TERMS:10 + N_TERMS]
        dgq, dgk, dgv, dgt, dwq, dwk, dwv = refs[10 + N_TERMS:17 + N_TERMS]
        chunked, pads, dpads, dgt_s = (refs[17 + N_TERMS:22 + N_TERMS], refs[22 + N_TERMS:25 + N_TERMS],
                                       refs[25 + N_TERMS:28 + N_TERMS], refs[28 + N_TERMS])
        h = pl.program_id(0)
        taps = _taps(wq, wk, wv)
        _gdn_pad((gq, gk, gv), pads)
        rows = _prep_rows(t)
        per = rows // CHUNK

        def gates_tile(i, carry):
            gtile = gt[pl.ds(pl.multiple_of(i * rows, rows), rows), :]
            chunked[3][pl.ds(i * per, per)] = _lane_pick(gtile, L_GA + h).reshape(per, CHUNK, 1)
            chunked[4][pl.ds(i * per, per)] = _lane_pick(gtile, L_GB + h).reshape(per, CHUNK, 1)
            return carry

        lax.fori_loop(0, t // rows, gates_tile, 0)
        grp_n = math.gcd(nc, GROUP)

        def grp(i, carry):
            sl = pl.ds(pl.multiple_of(i * grp_n, grp_n), grp_n)
            _, vjp = jax.vjp(_gdn_intra, *[r[sl] for r in qkv], chunked[3][sl], chunked[4][sl])
            for r, gval in zip(chunked, vjp(tuple(r[sl] for r in dintra))):
                r[sl] = gval
            return carry

        lax.fori_loop(0, nc // grp_n, grp, 0)

        for r in dpads:
            r[...] = jnp.zeros_like(r)

        def tile(i, dtaps):
            r0 = pl.multiple_of(i * rows, rows)
            win = pl.ds(r0, rows + HALO)
            _, vjp = jax.vjp(lambda *a: _gdn_prep(*a, h), *[p[win, :] for p in pads], gt[pl.ds(r0, rows), :], taps)
            grads = vjp(tuple(r[pl.ds(i * per, per)].reshape(rows, r.shape[-1]) for r in chunked))
            for r, gval in zip(dpads, grads[:3]):
                r[win, :] += gval
            dgt_s[pl.ds(r0, rows), :] = grads[3]
            return jax.tree.map(jnp.add, dtaps, grads[4])

        dtaps = lax.fori_loop(0, t // rows, tile, (jnp.zeros((1, HD), F32),) * 12)
        for r, dpad in zip((dgq, dgk, dgv), dpads):
            r[...] = dpad[HALO:, :].astype(r.dtype)
        for j, r in enumerate((dwq, dwk, dwv)):
            for k in range(4):
                r[k:k + 1, :] = dtaps[4 * j + k]

        @pl.when(h == 0)
        def _():
            dgt[...] = jnp.zeros_like(dgt)

        dgt[...] += dgt_s[...]

    head = _head(t)
    taps = pl.BlockSpec((4, HD), lambda h: (0, h))
    return pl.pallas_call(
        body, grid=(NG,), name="gdn_bwd",
        in_specs=_gdn_in_specs(t) + [_per_head(sh) for sh in terms + [(nc, CHUNK, HD)] * 3],
        out_specs=[head, head, head, _small(t), taps, taps, taps],
        out_shape=[SDS((t, NG * HD), BF16)] * 3 + [SDS((t, HD), F32)] + [SDS((4, NG * HD), F32)] * 3,
        scratch_shapes=_gdn_chunked_scratch(nc) + [pltpu.VMEM((t + HALO, HD), F32)] * 6 + [pltpu.VMEM((t, HD), F32)],
        compiler_params=_cp("arbitrary"),
    )(pa, pa, pa, gates, conv, conv, conv, *dterms, *qkv)


def _gdn_post(o, z, gain):
    return (jnp.concatenate(
        [_rms(o[:, h * HD:(h + 1) * HD], gain) * _silu(z[:, h * HD:(h + 1) * HD]) for h in range(NG)], axis=1),)


def _place():
    return lax.axis_index("x"), lax.axis_index("y"), lax.axis_index("c")


def _sum_blocks(name, parts):
    _, r, c = parts.shape
    tr = 64 if r % 64 == 0 else r

    def body(x, o):
        acc = x[0].astype(F32)
        for d in range(1, N_DEV):
            acc = acc + x[d].astype(F32)
        o[...] = acc

    return pl.pallas_call(
        body, grid=(r // tr,), name=name, in_specs=[pl.BlockSpec((N_DEV, tr, c), lambda i: (0, i, 0))],
        out_specs=pl.BlockSpec((tr, c), lambda i: (i, 0)), out_shape=SDS((r, c), F32), compiler_params=_cp("parallel"),
    )(parts)


def _all_reduce_small(name, x):
    m_per, n = x.shape

    def body(x_ref, out_ref, send_sems, recv_sems, local_sem):
        px, py, pc = _place()
        me, sibling = (px, py, pc), (px, py, 1 - pc)
        chips = [(1 - px, py), (px, 1 - py), (1 - px, 1 - py)]
        buf = out_ref

        def rows(qx, qy, qc):
            return buf.at[pl.ds((4 * qx + 2 * qy + qc) * m_per, m_per), :]

        def copy(k, block, to, src=None):
            return pltpu.make_async_remote_copy(
                src_ref=rows(*block) if src is None else src, dst_ref=rows(*block),
                send_sem=send_sems.at[k], recv_sem=recv_sems.at[k], device_id=to, device_id_type=MESH)

        mine = pltpu.make_async_copy(x_ref, rows(*me), local_sem)
        mine.start()
        first = [copy(0, me, sibling, src=x_ref)]
        first += [copy(1 + j, me, (*chip, pc), src=x_ref) for j, chip in enumerate(chips)]
        for cp in first:
            cp.start()
        passed = [copy(4 + j, (*chip, pc), sibling) for j, chip in enumerate(chips)]
        for j, chip in enumerate(chips):
            copy(1 + j, (*chip, pc), me).wait_recv()
            passed[j].start()
        copy(0, sibling, me).wait_recv()
        for j, chip in enumerate(chips):
            copy(4 + j, (*chip, 1 - pc), me).wait_recv()
        for cp in first + passed:
            cp.wait_send()
        mine.wait()

    gathered = pl.pallas_call(
        body, name=name, out_shape=SDS((N_DEV * m_per, n), x.dtype),
        in_specs=[pl.BlockSpec(memory_space=pltpu.VMEM)], out_specs=pl.BlockSpec(memory_space=pltpu.VMEM),
        scratch_shapes=[pltpu.SemaphoreType.DMA((7,)), pltpu.SemaphoreType.DMA((7,)), pltpu.SemaphoreType.DMA],
    )(x)
    return _sum_blocks(name + "_sum", gathered.reshape(N_DEV, m_per, n))


HBM_SPEC = pl.BlockSpec(memory_space=pltpu.HBM)
SEM_SPEC = pl.BlockSpec(memory_space=pltpu.SEMAPHORE)
EFFECT = pltpu.SideEffectType.DATAFLOW_SIDE_EFFECTING


def _copies_start(name, bufs, n_remote, n_local, build, deps):
    nb, nd = len(bufs), len(deps)
    sem_shapes = [pltpu.SemaphoreType.DMA((n_remote,)), pltpu.SemaphoreType.DMA((n_remote,))]
    if n_local:
        sem_shapes.append(pltpu.SemaphoreType.DMA((n_local,)))
    ns = len(sem_shapes)

    def body(*refs):
        sems = refs[nb + nd:nb + nd + ns]
        remote, local = build(refs[:nb], *sems, *([None] * (3 - ns)))
        for cp in local + remote:
            cp.start()
        refs[-1][...] = jnp.zeros((8, HD), F32)

    outs = pl.pallas_call(
        body, name=name,
        out_shape=(*sem_shapes, *[pltpu.HBM(b.shape, b.dtype) for b in bufs], SDS((8, HD), F32)),
        in_specs=[HBM_SPEC] * nb + [ANY_SPEC] * nd,
        out_specs=(*[SEM_SPEC] * ns, *[HBM_SPEC] * nb, pl.BlockSpec(memory_space=pltpu.VMEM)),
        input_output_aliases={i: ns + i for i in range(nb)},
        compiler_params=pltpu.CompilerParams(has_side_effects=EFFECT),
    )(*[pltpu.with_memory_space_constraint(b, pltpu.HBM) for b in bufs], *deps)
    return list(outs[:ns]), list(outs[ns:ns + nb]), outs[-1]


def _copies_wait(name, bufs, sems, build, after):
    nb, ns = len(bufs), len(sems)

    def body(*refs):
        remote, local = build(refs[:nb], *refs[nb:nb + ns], *([None] * (3 - ns)))
        for cp in local:
            cp.wait()
        for cp in remote:
            cp.wait_send()
            cp.wait_recv()

    outs = pl.pallas_call(
        body, name=name, out_shape=tuple(pltpu.HBM(b.shape, b.dtype) for b in bufs),
        in_specs=[HBM_SPEC] * nb + [SEM_SPEC] * ns + [ANY_SPEC] * len(after), out_specs=tuple([HBM_SPEC] * nb),
        input_output_aliases={i: i for i in range(nb)},
        compiler_params=pltpu.CompilerParams(has_side_effects=EFFECT),
    )(*bufs, *sems, *after)
    return list(outs)


def _remote(src, dst, send, recv, k, to):
    return pltpu.make_async_remote_copy(src_ref=src, dst_ref=dst, send_sem=send.at[k], recv_sem=recv.at[k],
                                        device_id=to, device_id_type=MESH)


class _Gather:
    def __init__(self, name, shards, deps):
        self.name, self.n = name, len(shards)
        lands = [lax.empty((N_DEV,) + s.shape, s.dtype) for s in shards]
        self.sems1, bufs, self.token = _copies_start(
            name + "_s1", list(shards) + lands, 4 * self.n, self.n, self._stage1(range(self.n)), deps)
        self.shards, self.lands, self.sems2 = bufs[:self.n], bufs[self.n:], {}

    def _stage1(self, idxs):
        def build(refs, send, recv, loc):
            x, y, c = _place()
            me = 4 * x + 2 * y + c
            targets = [(x, y, 1 - c), (1 - x, y, c), (x, 1 - y, c), (1 - x, 1 - y, c)]
            remote, local = [], []
            for pos, i in enumerate(idxs):
                src, land = refs[pos], refs[len(idxs) + pos]
                local.append(pltpu.make_async_copy(src, land.at[me], loc.at[i]))
                remote += [_remote(src, land.at[me], send, recv, 4 * i + k, to) for k, to in enumerate(targets)]
            return remote, local
        return build

    @staticmethod
    def _stage2(refs, send, recv, loc):
        x, y, c = _place()
        remote = []
        for pos, land in enumerate(refs):
            for j, (cx, cy) in enumerate([(1 - x, y), (x, 1 - y), (1 - x, 1 - y)]):
                blk = land.at[4 * cx + 2 * cy + c]
                remote.append(_remote(blk, blk, send, recv, 3 * pos + j, (x, y, 1 - c)))
        return remote, []

    def pass_on(self, idxs, after):
        tag, m = "".join(map(str, idxs)), len(idxs)
        bufs = _copies_wait(f"{self.name}_w1_{tag}", [self.shards[i] for i in idxs] + [self.lands[i] for i in idxs],
                            self.sems1, self._stage1(idxs), after)
        self.sems2[tag], lands, token = _copies_start(f"{self.name}_s2_{tag}", bufs[m:], 3 * m, 0, self._stage2, ())
        for pos, i in enumerate(idxs):
            self.lands[i] = lands[pos]
        return [token]

    def get(self, idxs, after):
        tag = "".join(map(str, idxs))
        return _copies_wait(f"{self.name}_w2_{tag}", [self.lands[i] for i in idxs], self.sems2[tag], self._stage2, after)


class _RelayGather:
    def __init__(self, name, shards, deps):
        self.name, self.n = name, len(shards)
        lands = [lax.empty((N_DEV,) + s.shape, s.dtype) for s in shards]
        self.sems, bufs, self.token = _copies_start(name + "_s1", list(shards) + lands, 3 * self.n, self.n, self._stage1, deps)
        self.shards, self.lands = bufs[:self.n], bufs[self.n:]

    def _stage1(self, refs, send, recv, loc):
        x, y, c = _place()
        me = 4 * x + 2 * y + c
        remote, local = [], []
        for i in range(self.n):
            src, land = refs[i], refs[self.n + i]
            local.append(pltpu.make_async_copy(src, land.at[me], loc.at[i]))
            remote += [_remote(src, land.at[me], send, recv, 3 * i + k, to)
                       for k, to in enumerate([(x, y, 1 - c), (1 - x, y, c), (x, 1 - y, c)])]
        return remote, local

    @staticmethod
    def _relay(refs, send, recv, loc):
        x, y, c = _place()
        remote = []
        for i, land in enumerate(refs):
            half = land.shape[1] // 2
            x_blk, y_blk = land.at[4 * (1 - x) + 2 * y + c], land.at[4 * x + 2 * (1 - y) + c]
            from_x, from_y = x_blk.at[pl.ds(0, half)], y_blk.at[pl.ds(half, half)]
            remote += [_remote(from_x, from_x, send, recv, 4 * i, (x, 1 - y, c)),
                       _remote(from_y, from_y, send, recv, 4 * i + 1, (1 - x, y, c)),
                       _remote(x_blk, x_blk, send, recv, 4 * i + 2, (x, y, 1 - c)),
                       _remote(y_blk, y_blk, send, recv, 4 * i + 3, (x, y, 1 - c))]
        return remote, []

    @staticmethod
    def _pass_diagonal(refs, send, recv, loc):
        x, y, c = _place()
        blocks = [land.at[4 * (1 - x) + 2 * (1 - y) + c] for land in refs]
        return [_remote(blk, blk, send, recv, i, (x, y, 1 - c)) for i, blk in enumerate(blocks)], []

    def forward(self, after):
        bufs = _copies_wait(self.name + "_w1", self.shards + self.lands, self.sems, self._stage1, after)
        self.sems, self.lands, self.token = _copies_start(self.name + "_sf", bufs[self.n:], 4 * self.n, 0, self._relay, ())
        return [self.token]

    def pass_on(self, after):
        lands = _copies_wait(self.name + "_wf", self.lands, self.sems, self._relay, after)
        self.sems, self.lands, self.token = _copies_start(self.name + "_s2", lands, self.n, 0, self._pass_diagonal, ())
        return [self.token]

    def get(self, after):
        return _copies_wait(self.name + "_w2", self.lands, self.sems, self._pass_diagonal, after)


def _rows_tile(r, row_bytes, target=1 << 20):
    tr = r
    while tr % 32 == 0 and tr * row_bytes > target:
        tr //= 2
    return tr


def _pair_add(name, g, got, c):
    _, r, cols = g.shape
    tr = _rows_tile(r, cols * 2)

    def body(s, a, b, o):
        o[...] = (a[...].astype(F32) + b[...].astype(F32)).astype(o.dtype)

    return pl.pallas_call(
        body, name=name, out_shape=SDS((4, r, cols), g.dtype),
        grid_spec=pltpu.PrefetchScalarGridSpec(
            num_scalar_prefetch=1, grid=(4, r // tr),
            in_specs=[pl.BlockSpec((None, tr, cols), lambda j, i, s: (2 * j + s[0], i, 0)),
                      pl.BlockSpec((None, tr, cols), lambda j, i, s: (j, i, 0))],
            out_specs=pl.BlockSpec((None, tr, cols), lambda j, i, s: (j, i, 0))),
        compiler_params=_cp("parallel", "parallel"),
    )(c.reshape(1), g, got)


def _quad_sum(name, part, got, chip, wmv=None):
    _, r, cols = part.shape
    tr = _rows_tile(r, cols * 4)
    n_out = 4 if wmv else 1

    def body(s, a, b1, b2, b3, *rest):
        g = ((a[...].astype(F32) + b1[...].astype(F32)) + b2[...].astype(F32)) + b3[...].astype(F32)
        rest[-n_out][...] = g
        if wmv:
            w, m, v = rest[:3]
            rest[-3][...], rest[-2][...], rest[-1][...] = _adamw(w[...], g, m[...], v[...])

    blk = lambda k: pl.BlockSpec((None, tr, cols), lambda i, s, k=k: (jnp.bitwise_xor(s[0], k), i, 0))
    row = pl.BlockSpec((tr, cols), lambda i, s: (i, 0))
    outs = pl.pallas_call(
        body, name=name, out_shape=[SDS((r, cols), F32)] * n_out,
        grid_spec=pltpu.PrefetchScalarGridSpec(
            num_scalar_prefetch=1, grid=(r // tr,), in_specs=[blk(0), blk(1), blk(2), blk(3)] + [row] * (n_out - 1),
            out_specs=[row] * n_out),
        compiler_params=_cp("parallel"),
    )(chip.reshape(1), part, got, got, got, *(wmv or ()))
    return tuple(outs) if wmv else outs[0]


class _Scatter:
    def __init__(self, name, grads, deps):
        self.name, self.n = name, len(grads)
        got = [lax.empty((4,) + g.shape[1:], g.dtype) for g in grads]
        self.sems, bufs, self.token = _copies_start(name + "_s1", list(grads) + got, 4 * self.n, 0, self._stage1, deps)
        self.grads, self.got = bufs[:self.n], bufs[self.n:]

    def _stage1(self, refs, send, recv, loc):
        x, y, c = _place()
        remote = []
        for i in range(self.n):
            remote += [_remote(refs[i].at[2 * j + 1 - c], refs[self.n + i].at[j], send, recv, 4 * i + j, (x, y, 1 - c))
                       for j in range(4)]
        return remote, []

    def _stage2(self, refs, send, recv, loc):
        x, y, c = _place()
        remote = []
        for i in range(self.n):
            for k in (1, 2, 3):
                tx = 1 - x if k & 2 else x
                ty = 1 - y if k & 1 else y
                remote.append(_remote(refs[i].at[2 * tx + ty], refs[self.n + i].at[2 * x + y], send, recv,
                                      3 * i + k - 1, (tx, ty, c)))
        return remote, []

    def mid(self, after):
        bufs = _copies_wait(self.name + "_w1", self.grads + self.got, self.sems, self._stage1, after)
        c = lax.axis_index("c").astype(jnp.int32)
        parts = [_pair_add(f"{self.name}_add{i}", bufs[i], bufs[self.n + i], c) for i in range(self.n)]
        got = [lax.empty(p.shape, p.dtype) for p in parts]
        self.sems, bufs, self.token = _copies_start(self.name + "_s2", parts + got, 3 * self.n, 0, self._stage2, ())
        self.parts, self.got = bufs[:self.n], bufs[self.n:]

    def end(self, after, wmv=None):
        bufs = _copies_wait(self.name + "_w2", self.parts + self.got, self.sems, self._stage2, after)
        chip = (2 * lax.axis_index("x") + lax.axis_index("y")).astype(jnp.int32)
        wmv = wmv or [None] * self.n
        return [_quad_sum(f"{self.name}_sum{i}", bufs[i], bufs[self.n + i], chip, wmv[i]) for i in range(self.n)]


def _adamw(w, g, m, v):
    m = ADAM_B1 * m + (1.0 - ADAM_B1) * g
    v = ADAM_B2 * v + (1.0 - ADAM_B2) * (g * g)
    m_hat = m / (1.0 - ADAM_B1 ** ADAM_STEP)
    v_hat = v / (1.0 - ADAM_B2 ** ADAM_STEP)
    return -ADAM_LR * (m_hat / (jnp.sqrt(v_hat) + ADAM_EPS) + ADAM_WD * w), m, v


def _adamw_call(name, w, g, m, v):
    r, c = w.shape
    tm = 64 if r % 64 == 0 else r
    return _rowwise(name, _adamw, [w, g, m, v], [], [(c, F32)] * 3, tm)


_IN_COLS = 5906


def _perm_in(w):
    pad = jnp.zeros((w.shape[0], 2 * HALF - _IN_COLS), w.dtype)
    return (jnp.concatenate([w[:, 2310:4614], w[:, 4614:5382]], axis=1),
            jnp.concatenate([w[:, :2304], w[:, 5394:5906], w[:, 2304:2310], w[:, 5382:5394], pad], axis=1))


def _unperm_in(ga, gb):
    return jnp.concatenate([gb[:, :2304], gb[:, 2816:2822], ga[:, :2304], ga[:, 2304:3072], gb[:, 2822:2834],
                            gb[:, 2304:2816]], axis=1)


def _lanes(v, at):
    return jnp.pad(v, ((0, 0), (at, HD - at - v.shape[1])))


_PACK = ("norm_mix", "mem_norm", "norm_ffn", "gdn_conv", "fox_q_norm", "fox_k_norm", "gdn_out_norm", "mem_q_norm",
         "mem_k_norm", "fox_f_bias", "gdn_a_log", "gdn_dt_bias", "loss")


def _pack(vals):
    parts = [vals[n].reshape(-1, HD) for n in _PACK]
    used = sum(p.shape[0] for p in parts)
    buf = jnp.concatenate(parts + [jnp.zeros((-used % 8, HD), F32)], axis=0)
    return buf, [(n, p.shape[0]) for n, p in zip(_PACK, parts)]


def _unpack(buf, layout):
    out, at = {}, 0
    for n, rows in layout:
        out[n] = buf[at:at + rows]
        at += rows
    return out


def kernel(x, mem, norm_mix, w_in, fox_f_bias, fox_q_norm, fox_k_norm, gdn_conv, gdn_a_log, gdn_dt_bias, gdn_out_norm, mem_norm, w_mem_kv, mem_q_norm, mem_k_norm, w_out, norm_ffn, w_gate_up, w_down, loss_target, m_norm_mix, m_w_in, m_fox_f_bias, m_fox_q_norm, m_fox_k_norm, m_gdn_conv, m_gdn_a_log, m_gdn_dt_bias, m_gdn_out_norm, m_mem_norm, m_w_mem_kv, m_mem_q_norm, m_mem_k_norm, m_w_out, m_norm_ffn, m_w_gate_up, m_w_down, v_norm_mix, v_w_in, v_fox_f_bias, v_fox_q_norm, v_fox_k_norm, v_gdn_conv, v_gdn_a_log, v_gdn_dt_bias, v_gdn_out_norm, v_mem_norm, v_w_mem_kv, v_mem_q_norm, v_mem_k_norm, v_w_out, v_norm_ffn, v_w_gate_up, v_w_down):
    args = dict(locals())
    d = x.shape[2]
    me = 4 * lax.axis_index("x") + 2 * lax.axis_index("y") + lax.axis_index("c")

    cshard = gdn_conv[0].shape[1]
    conv_pad = jnp.pad(gdn_conv[0], ((0, 4), (0, 3 * HD - cshard)))
    w_in_a, w_in_b = _perm_in(w_in[0])
    wmv = lambda n: (args[n][0], args["m_" + n][0], args["v_" + n][0])
    comm = _StepComm(w_in_b.astype(BF16), {"in_a": [w_in_a.astype(BF16), conv_pad],
                                           "kv_out": [w_mem_kv[0].astype(BF16), w_out[0].astype(BF16)]},
                     w_gate_up[0].astype(BF16), w_down[0].astype(BF16), (),
                     {"ffn": [wmv("w_down"), wmv("w_gate_up")], "a": [None, wmv("w_out"), wmv("w_mem_kv")], "b": [None]})

    grad_x, loss_local, small_grads = _local_step(
        x[0], mem[0], loss_target[0], norm_mix, fox_f_bias, fox_q_norm, fox_k_norm, gdn_a_log, gdn_dt_bias,
        gdn_out_norm, mem_norm, mem_q_norm, mem_k_norm, norm_ffn, cshard, comm)

    red = comm.finish([grad_x])
    updated = {"w_down": red["ffn"][0], "w_gate_up": red["ffn"][1], "w_out": red["a"][1], "w_mem_kv": red["a"][2]}
    grads = {n: r[0] for n, r in updated.items()}
    grads["w_in"] = _unperm_in(red["a"][0], red["b"][0])
    small_grads["loss"] = jnp.broadcast_to(loss_local, (1, HD))
    packed, layout = _pack(small_grads)
    small = _unpack(_all_reduce_small("ar_small", packed), layout)
    loss = small["loss"][0, 0]
    six = {"fox_f_bias": L_FF, "gdn_a_log": L_GA, "gdn_dt_bias": L_GA}
    for n, rows_n in layout[:-1]:
        gsm = small[n]
        if n == "gdn_conv":
            gsm = lax.dynamic_slice(gsm.reshape(4, N_DEV * cshard), (0, me * cshard), (4, cshard))[None]
        elif n in six:
            gsm = gsm[:, six[n]:six[n] + 6]
        else:
            gsm = gsm.reshape(1, rows_n * HD)
        grads[n] = gsm

    names = ['norm_mix', 'w_in', 'fox_f_bias', 'fox_q_norm', 'fox_k_norm', 'gdn_conv', 'gdn_a_log', 'gdn_dt_bias',
             'gdn_out_norm', 'mem_norm', 'w_mem_kv', 'mem_q_norm', 'mem_k_norm', 'w_out', 'norm_ffn', 'w_gate_up', 'w_down']
    big = ("w_in", "w_mem_kv", "w_out", "w_gate_up", "w_down")
    delta, new_m, new_v = {}, {}, {}
    for n in big:
        res = updated[n][1:] if n in updated else _adamw_call("adamw_" + n, args[n][0], grads[n], *wmv(n)[1:])
        delta[n], new_m[n], new_v[n] = [a[None] for a in res]
        grads[n] = grads[n][None]

    def flat(a):
        a = a.reshape(1, -1)
        return jnp.pad(a, ((0, 0), (0, -a.shape[1] % HD))).reshape(-1, HD)

    smalls = [n for n in names if n not in big]
    pk = lambda pre: jnp.concatenate([flat(grads[n] if pre == "g" else args[pre + n]) for n in smalls], axis=0)
    cat = [pk(""), pk("g"), pk("m_"), pk("v_")]
    padr = -cat[0].shape[0] % 8
    cat = [jnp.pad(a, ((0, padr), (0, 0))) for a in cat]
    res = _adamw_call("adamw_small", *cat)
    at = 0
    for n in smalls:
        shape = args[n].shape
        size = math.prod(shape)
        nrow = -(-size // HD)
        for dst, src in zip((delta, new_m, new_v), res):
            dst[n] = src[at:at + nrow].reshape(-1)[:size].reshape(shape)
        at += nrow

    return (loss, grad_x[None], *[grads[n] for n in names], *[delta[n] for n in names],
            *[new_m[n] for n in names], *[new_v[n] for n in names])


class _StepComm:
    def __init__(self, first, shard_groups, w_gate_up, w_down, after, wmv):
        self.wmv, self.done = wmv, {}
        self.first = _RelayGather("ag_first", [first], after)
        self.groups, self.shards = {}, []
        for key, ws in shard_groups.items():
            self.groups[key] = list(range(len(self.shards), len(self.shards) + len(ws)))
            self.shards += list(ws)
        self.w_gate_up, self.w_down = w_gate_up, w_down
        self.passed, self.scatters = set(), {}

    def start_deps(self):
        return [self.first.token]

    def first_weights(self, after):
        deps = self.first.forward(after)
        self.gather = _Gather("ag", self.shards, deps)
        self.relay = _RelayGather("ag_gu", [self.w_gate_up], [self.gather.token])
        return self.first.get(self.first.pass_on([self.relay.token]))

    def relay_forward(self, after):
        deps = self.relay.forward(after)
        self.gather_down = _Gather("ag_dn", [self.w_down], deps)
        return [self.gather_down.token]

    def pass_on(self, key, after):
        self.passed.add(key)
        if key == "gate_up":
            return self.relay.pass_on(after)
        return self.gather.pass_on(self.groups[key], after)

    def weights(self, key, after):
        if key == "down":
            return self.gather_down.get([0], self.gather_down.pass_on([0], after))
        if key not in self.passed:
            after = self.pass_on(key, after)
        return self.relay.get(after) if key == "gate_up" else self.gather.get(self.groups[key], after)

    def send(self, tag, grads):
        blocks = [g if g.ndim == 3 else g.reshape(N_DEV, g.shape[0] // N_DEV, g.shape[1]) for g in grads]
        self.scatters[tag] = _Scatter("rs_" + tag, blocks, ())
        return [self.scatters[tag].token]

    def mid(self, tag, after):
        self.scatters[tag].mid(after)
        return [self.scatters[tag].token]

    def finish_group(self, tag, after):
        self.done[tag] = self.scatters.pop(tag).end(after, self.wmv[tag])
        first = self.done[tag][0]
        return [first[0] if isinstance(first, tuple) else first]

    def finish(self, after):
        for tag in list(self.scatters):
            self.finish_group(tag, after)
        return self.done


def _local_step(xs, ms, tgt, norm_mix, fox_f_bias, fox_q_norm, fox_k_norm, gdn_a_log, gdn_dt_bias, gdn_out_norm,
                mem_norm, mem_q_norm, mem_k_norm, norm_ffn, cshard, comm):
    t, d = xs.shape
    bq = min(t, 256)
    fb, alog, dtb = _lanes(fox_f_bias, L_FF), _lanes(gdn_a_log, L_GA), _lanes(gdn_dt_bias, L_GA)
    flat = lambda w: w.reshape(-1, w.shape[-1])

    rms1 = lambda a, g: (_rms(a, g),)
    (u,) = _rowwise("norm_mix", rms1, [xs], [norm_mix], [(d, BF16)], min(t, 256), deps=comm.start_deps())
    w_in_b = flat(comm.first_weights([u])[0])
    pb = _matmul("proj_in_b", u, w_in_b, NN, F32, 1024, 768)
    o_fox = _fox_fwd(pb, fb, fox_q_norm, fox_k_norm, bq)
    w_in_a, conv_parts = comm.weights("in_a", [o_fox])
    w_in_a = flat(w_in_a)
    conv_all = conv_parts[:, :4, :cshard].transpose(1, 0, 2).reshape(4, N_DEV * cshard)
    pa = _matmul("proj_in_a", u, w_in_a, NN, F32, 1024, 768)
    smrow = (pb, HD, SM)
    (gates,) = _rowwise("gdn_gates", _gdn_gates, [smrow], [alog, dtb], [(HD, F32)], min(t, 256))
    gdn_terms, gdn_qkv = _gdn_fwd(pa, gates, conv_all)
    o_gdn_raw, gdn_states = _gdn_scan(gdn_terms)
    gdn_saved = list(gdn_terms) + [gdn_states]
    deps = comm.relay_forward([o_gdn_raw])
    zrow = (pa, NG * HD, GZ * HD // (NG * HD))
    (o_gdn,) = _rowwise("gdn_post", _gdn_post, [o_gdn_raw, zrow], [gdn_out_norm], [(NG * HD, BF16)], min(t, 256),
                        deps=deps)
    w_kv_all, w_out_all = [flat(w) for w in comm.weights("kv_out", [o_gdn])]
    (mem_n,) = _rowwise("norm_mem", rms1, [ms], [mem_norm], [(d, BF16)], ms.shape[0])
    mkv = _matmul("proj_mem", mem_n, w_kv_all, NN, F32, 256, 512)
    o_mem = _mem_fwd(pb, mkv, mem_q_norm, mem_k_norm)
    deps = comm.pass_on("gate_up", [o_mem])
    mix = jnp.concatenate([o_fox, o_gdn, o_mem], axis=1)
    h1, h1n = _proj_out_norm(mix, w_out_all, xs, norm_ffn, deps)
    (wgu,) = comm.weights("gate_up", [h1n])
    ffw = wgu.shape[2]
    gu, act = _ffn_up(h1n, wgu.reshape(2, 4, d, ffw))
    w_down_all = flat(comm.weights("down", [act])[0])
    dyb, lsum = _ffn_down_loss(act, w_down_all, h1, tgt)
    loss_local = (0.5 / d) * jnp.sum(lsum[::8, ::HD])

    dgu = _ffn_down_bwd(dyb, w_down_all.reshape(4, ffw, d), gu).reshape(8, t, ffw)
    g_w_down = _matmul("grad_w_down", act, dyb, TN, BF16, 512, 2048)
    g_w_gu = _ffn_up_bwd_w(h1n, dgu)
    deps = comm.send("ffn", [g_w_down, g_w_gu])
    rms2 = lambda a, g: (_rms(a, g), a)
    dh1b, g_norm_ffn = _ffn_up_bwd_x(dgu, wgu, h1, norm_ffn, dyb, deps)

    dmix = _matmul("proj_out_bwd_x", dh1b, w_out_all, NT, BF16, 1024, 1024)
    g_w_out = _matmul("grad_w_out", mix, dh1b, TN, BF16, 1024, 2048)
    deps = comm.mid("ffn", [dmix, g_w_out])
    dmq, dmk, dmv, g_mqn, g_mkn = _mem_bwd(pb, mkv, mem_q_norm, mem_k_norm, dmix, deps=deps)
    dmkv = jnp.concatenate([dmk, dmv], axis=1).astype(BF16)
    g_w_kv = _matmul("grad_w_kv", mem_n, dmkv, TN, BF16, 512, 512)
    do_raw, dgz, g_gon = _rowwise_vjp("gdn_post_bwd", _gdn_post, [o_gdn_raw, zrow], [gdn_out_norm],
                                      [(dmix, NG * HD, 1)], [F32, BF16], min(t, 256), deps=deps)
    dterms = _gdn_bwd_scan(gdn_saved, do_raw)
    dgq, dgk, dgv, dgates, dwq, dwk, dwv = _gdn_bwd(pa, gates, conv_all, dterms, gdn_qkv)
    dsm_gdn, g_alog, g_dtb = _rowwise_vjp("gdn_gates_bwd", _gdn_gates, [smrow], [alog, dtb], [dgates], [F32], min(t, 256))
    dp_a = jnp.concatenate([dgq, dgk, dgv, dgz], axis=1)
    g_w_in_a = _matmul("grad_w_in_a", u, dp_a, TN, BF16, 512, 3072)
    deps = comm.send("a", [g_w_in_a, g_w_out, g_w_kv])
    du_a = _matmul("proj_in_bwd_a", dp_a, w_in_a, NT, F32, 1024, 1024, deps=deps)
    deps = comm.mid("a", [du_a])
    dfq, dfk, dfv, dsm_fox, g_fb, g_fqn, g_fkn = _fox_bwd(pb, fb, fox_q_norm, fox_k_norm, dmix, 2 * bq if t % (2 * bq) == 0 else bq,
                                                          deps=deps)
    dp_b = jnp.concatenate([dfq, dfk, dfv, dmq, (dsm_fox + dsm_gdn).astype(BF16), jnp.zeros((t, HD), BF16)], axis=1)
    g_w_in_b = _matmul("grad_w_in_b", u, dp_b, TN, BF16, 512, 3072)
    deps = comm.mid("b", comm.finish_group("ffn", comm.send("b", [g_w_in_b])))
    deps = comm.finish_group("a", deps)
    dmem_n = _matmul("proj_mem_bwd_x", dmkv, w_kv_all, NT, F32, 256, 512, deps=deps)
    g_mem_norm = _rowwise_vjp("norm_mem_bwd", rms1, [ms], [mem_norm], [dmem_n], [], ms.shape[0])[0]
    grad_x, g_norm_mix = _proj_in_bwd_norm(dp_b, w_in_b, du_a, xs, norm_mix, dh1b, [g_mem_norm])

    small_grads = {
        "norm_mix": g_norm_mix, "mem_norm": g_mem_norm, "norm_ffn": g_norm_ffn,
        "gdn_conv": jnp.concatenate([dwq, dwk, dwv], axis=1),
        "fox_q_norm": g_fqn, "fox_k_norm": g_fkn, "gdn_out_norm": g_gon, "mem_q_norm": g_mqn, "mem_k_norm": g_mkn,
        "fox_f_bias": g_fb, "gdn_a_log": g_alog, "gdn_dt_bias": g_dtb}
    return grad_x, loss_local, small_grads
```

```python
import functools
import math

import jax
import jax.numpy as jnp
from jax import lax
from jax.experimental import pallas as pl
from jax.experimental.pallas import tpu as pltpu

F32 = jnp.float32
BF16 = jnp.bfloat16
SDS = jax.ShapeDtypeStruct

N_DEV = 8
HD = 128
NF, NG, NM = 6, 6, 4
CHUNK = 64
GROUP = 32
NORM_EPS = 1e-6
GQ, GK, GV, GZ = 0, 6, 12, 18
FQ, FK, FV, MQ, SM = 0, 6, 12, 18, 22
HALF = 24 * HD
L_FF, L_GA, L_GB = 0, 6, 12
VMEM_LIMIT = 56 * 1024 * 1024

ADAM_LR, ADAM_B1, ADAM_B2, ADAM_EPS, ADAM_WD, ADAM_STEP = 0.001, 0.9, 0.999, 1e-08, 0.01, 10

NN = (((1,), (0,)), ((), ()))
NT = (((1,), (1,)), ((), ()))
TN = (((0,), (0,)), ((), ()))
MESH = pl.DeviceIdType.MESH


def _cp(*sem):
    return pltpu.CompilerParams(dimension_semantics=tuple(sem) if sem else None, vmem_limit_bytes=VMEM_LIMIT)


def _dot(a, b, dims=NN):
    return lax.dot_general(a, b, dims, preferred_element_type=F32)


def _iota(shape, axis):
    return lax.broadcasted_iota(jnp.int32, shape, axis)


def _rms(x, gain):
    return x * lax.rsqrt(jnp.mean(x * x, axis=-1, keepdims=True) + NORM_EPS) * gain


def _sigmoid(x):
    return 0.5 * jnp.tanh(0.5 * x) + 0.5


def _silu(x):
    return x * _sigmoid(x)


def _softplus(x):
    return jnp.maximum(x, 0.0) + jnp.log(1.0 + jnp.exp(-jnp.abs(x)))


def _lane_pick(x, lane):
    oh = (_iota((1, x.shape[-1]), 1) == lane).astype(F32)
    return jnp.sum(x * oh, axis=-1, keepdims=True)


def _cumsum_rows(x):
    tril = (_iota((HD, HD), 0) >= _iota((HD, HD), 1)).astype(F32)
    carry = jnp.zeros((1, x.shape[1]), F32)
    outs = []
    for b in range(x.shape[0] // HD):
        blk = x[b * HD:(b + 1) * HD]
        outs.append(_pdot(tril, blk, "nn", "xa") + carry)
        carry = carry + jnp.sum(blk, axis=0, keepdims=True)
    return jnp.concatenate(outs, axis=0)


def _row_spec(r, tm):
    if isinstance(r, tuple):
        arr, width, cb = r
        return arr, pl.BlockSpec((tm, width), lambda i, cb=cb: (i, cb))
    return r, pl.BlockSpec((tm, r.shape[1]), lambda i: (i, 0))


ANY_SPEC = pl.BlockSpec(memory_space=pl.ANY)


def _rowwise(name, fn, rows, consts, outs, tm, deps=()):
    arrs, specs = zip(*[_row_spec(r, tm) for r in rows])
    n_rows = arrs[0].shape[0]
    nr, nc, nd = len(rows), len(consts), len(deps)

    def body(*refs):
        res = fn(*[r[...] for r in refs[:nr + nc]])
        for o, v in zip(refs[nr + nc + nd:], res):
            o[...] = v.astype(o.dtype)

    return pl.pallas_call(
        body, grid=(n_rows // tm,), name=name,
        in_specs=list(specs) + [pl.BlockSpec(c.shape, lambda i: (0, 0)) for c in consts] + [ANY_SPEC] * nd,
        out_specs=[pl.BlockSpec((tm, w), lambda i: (i, 0)) for w, _ in outs],
        out_shape=[SDS((n_rows, w), dt) for w, dt in outs],
        compiler_params=_cp("parallel"),
    )(*arrs, *consts, *deps)


def _rowwise_vjp(name, fn, rows, consts, cts, grad_dtypes, tm, deps=()):
    arrs, specs = zip(*[_row_spec(r, tm) for r in rows])
    ct_arrs, ct_specs = zip(*[_row_spec(r, tm) for r in cts])
    n_rows = arrs[0].shape[0]
    nr, nc, nct, nd = len(rows), len(consts), len(cts), len(deps)
    plan = [(j, dt) for j, dts in enumerate(grad_dtypes) for dt in (dts if isinstance(dts, tuple) else (dts,))]
    ng = len(plan)
    widths = [specs[j].block_shape[1] for j, _ in plan]
    grad_dtypes = [dt for _, dt in plan]

    def body(*refs):
        vals = [r[...].astype(F32) for r in refs[:nr + nc]]
        ctv = tuple(r[...].astype(F32) for r in refs[nr + nc:nr + nc + nct])
        _, vjp = jax.vjp(fn, *vals)
        grads = vjp(ctv)
        outs = refs[nr + nc + nct + nd:]
        for o, (j, _) in zip(outs[:ng], plan):
            o[...] = grads[j].astype(o.dtype)

        @pl.when(pl.program_id(0) == 0)
        def _():
            for o in outs[ng:]:
                o[...] = jnp.zeros_like(o)

        for o, g in zip(outs[ng:], grads[nr:]):
            o[...] += g

    return pl.pallas_call(
        body, grid=(n_rows // tm,), name=name,
        in_specs=list(specs) + [pl.BlockSpec(c.shape, lambda i: (0, 0)) for c in consts] + list(ct_specs)
        + [ANY_SPEC] * nd,
        out_specs=[pl.BlockSpec((tm, w), lambda i: (i, 0)) for w in widths]
        + [pl.BlockSpec(c.shape, lambda i: (0, 0)) for c in consts],
        out_shape=[SDS((n_rows, w), dt) for w, dt in zip(widths, grad_dtypes)] + [SDS(c.shape, F32) for c in consts],
        compiler_params=_cp("arbitrary"),
    )(*arrs, *consts, *ct_arrs, *deps)


def _tile(n, pref):
    t = min(n, pref)
    while n % t or (t % HD and t != n):
        t -= 1
    return t


def _matmul(name, a, b, dims, out_dtype, tm, tn, residual=None, deps=()):
    ta, tb = dims == TN, dims == NT
    m = a.shape[1] if ta else a.shape[0]
    k = a.shape[0] if ta else a.shape[1]
    n = b.shape[0] if tb else b.shape[1]
    tm, tn = _tile(m, tm), _tile(n, tn)

    def body(*refs):
        acc = _dot(refs[0][...], refs[1][...], dims)
        if residual is not None:
            acc = acc + refs[2][...]
        refs[-1][...] = acc.astype(out_dtype)

    in_specs = [pl.BlockSpec((k, tm), lambda i, j: (0, i)) if ta else pl.BlockSpec((tm, k), lambda i, j: (i, 0)),
                pl.BlockSpec((tn, k), lambda i, j: (j, 0)) if tb else pl.BlockSpec((k, tn), lambda i, j: (0, j))]
    ops = [a, b]
    if residual is not None:
        in_specs.append(pl.BlockSpec((tm, tn), lambda i, j: (i, j)))
        ops.append(residual)
    in_specs += [ANY_SPEC] * len(deps)
    ops += list(deps)
    return pl.pallas_call(
        body, grid=(m // tm, n // tn), name=name, in_specs=in_specs,
        out_specs=pl.BlockSpec((tm, tn), lambda i, j: (i, j)), out_shape=SDS((m, n), out_dtype),
        compiler_params=_cp("parallel", "parallel"),
    )(*ops)


def _proj_out_norm(mix, w_out, xs, gain, deps):
    t, k = mix.shape
    d = w_out.shape[1]
    tm = _tile(t, 512)

    def body(*refs):
        a, b, x, g = refs[:4]
        h1, h1n = refs[4 + len(deps):]
        acc = _dot(a[...], b[...]) + x[...]
        h1[...] = acc
        h1n[...] = _rms(acc, g[...]).astype(BF16)

    return pl.pallas_call(
        body, grid=(t // tm,), name="proj_out",
        in_specs=[pl.BlockSpec((tm, k), lambda i: (i, 0)), pl.BlockSpec((k, d), lambda i: (0, 0)),
                  pl.BlockSpec((tm, d), lambda i: (i, 0)), pl.BlockSpec((1, d), lambda i: (0, 0))] + [ANY_SPEC] * len(deps),
        out_specs=[pl.BlockSpec((tm, d), lambda i: (i, 0))] * 2, out_shape=[SDS((t, d), F32), SDS((t, d), BF16)],
        compiler_params=_cp("parallel"),
    )(mix, w_out, xs, gain, *deps)


def _proj_in_bwd_norm(dp, w, du_a, xs, gain, dh1b, deps):
    t, k = dp.shape
    d = w.shape[0]
    tm = _tile(t, 256)

    def body(*refs):
        a, b, ua, x, g, dh = refs[:6]
        gx, dgain = refs[6 + len(deps):]
        _, vjp = jax.vjp(lambda xx, gn: _rms(xx, gn), x[...], g[...])
        dx, dg = vjp(_dot(a[...], b[...], NT) + ua[...])
        gx[...] = dx + dh[...].astype(F32)

        @pl.when(pl.program_id(0) == 0)
        def _():
            dgain[...] = jnp.zeros_like(dgain)

        dgain[...] += dg

    row = pl.BlockSpec((tm, d), lambda i: (i, 0))
    vec = pl.BlockSpec((1, d), lambda i: (0, 0))
    return pl.pallas_call(
        body, grid=(t // tm,), name="proj_in_bwd_b",
        in_specs=[pl.BlockSpec((tm, k), lambda i: (i, 0)), pl.BlockSpec((d, k), lambda i: (0, 0), pipeline_mode=ONE_BUFFER),
                  row, row, vec, row] + [ANY_SPEC] * len(deps),
        out_specs=[row, vec], out_shape=[SDS((t, d), F32), SDS((1, d), F32)], compiler_params=_cp("arbitrary"),
    )(dp, w, du_a, xs, gain, dh1b, *deps)


def _ffn_up(h1n, wgu):
    t, d = h1n.shape
    w = wgu.shape[3]
    tm = _tile(t, 512)

    def body(a, b, gu, act):
        x = a[...]
        g = _dot(x, b[0])
        u = _dot(x, b[1])
        gu[0] = g.astype(BF16)
        gu[1] = u.astype(BF16)
        act[...] = (_silu(g) * u).astype(BF16)

    return pl.pallas_call(
        body, grid=(4, t // tm), name="ffn_up",
        in_specs=[pl.BlockSpec((tm, d), lambda j, i: (i, 0)), pl.BlockSpec((2, None, d, w), lambda j, i: (0, j, 0, 0))],
        out_specs=[pl.BlockSpec((2, None, tm, w), lambda j, i: (0, j, i, 0)), pl.BlockSpec((tm, w), lambda j, i: (i, j))],
        out_shape=[SDS((2, 4, t, w), BF16), SDS((t, 4 * w), BF16)],
        compiler_params=_cp("parallel", "parallel"),
    )(h1n, wgu)


def _ffn_down_loss(act, wdown, h1, target):
    t, f = act.shape
    d = wdown.shape[1]
    tm, tn = _tile(t, 1024), _tile(d, 512)

    def body(a, b, h, tg, dyb, ls):
        e = _dot(a[...], b[...]) + h[...] - tg[...]
        dyb[...] = (e * (1.0 / d)).astype(BF16)
        ls[...] = jnp.broadcast_to(jnp.sum(e * e), (8, HD))

    return pl.pallas_call(
        body, grid=(t // tm, d // tn), name="ffn_down_loss",
        in_specs=[pl.BlockSpec((tm, f), lambda i, j: (i, 0)), pl.BlockSpec((f, tn), lambda i, j: (0, j)),
                  pl.BlockSpec((tm, tn), lambda i, j: (i, j)), pl.BlockSpec((tm, tn), lambda i, j: (i, j))],
        out_specs=[pl.BlockSpec((tm, tn), lambda i, j: (i, j)), pl.BlockSpec((8, HD), lambda i, j: (i, j))],
        out_shape=[SDS((t, d), BF16), SDS((8 * (t // tm), HD * (d // tn)), F32)],
        compiler_params=_cp("parallel", "parallel"),
    )(act, wdown, h1, target)


def _ffn_down_bwd(dyb, wdown4, gu):
    t, d = dyb.shape
    w = wdown4.shape[1]
    tm = _tile(t, 512)

    def body(a, b, gu_ref, out):
        da = _dot(a[...], b[...], NT)
        g = gu_ref[0].astype(F32)
        u = gu_ref[1].astype(F32)
        s = _sigmoid(g)
        out[0] = (da * u * (s * (1.0 + g * (1.0 - s)))).astype(BF16)
        out[1] = (da * g * s).astype(BF16)

    return pl.pallas_call(
        body, grid=(4, t // tm), name="ffn_down_bwd",
        in_specs=[pl.BlockSpec((tm, d), lambda j, i: (i, 0)), pl.BlockSpec((None, w, d), lambda j, i: (j, 0, 0)),
                  pl.BlockSpec((2, None, tm, w), lambda j, i: (0, j, i, 0))],
        out_specs=pl.BlockSpec((2, None, tm, w), lambda j, i: (0, j, i, 0)),
        out_shape=SDS((2, 4, t, w), BF16),
        compiler_params=_cp("parallel", "parallel"),
    )(dyb, wdown4, gu)


def _ffn_up_bwd_x(dgu, wgu, h1, gain, dyb, deps):
    _, t, w = dgu.shape
    d = wgu.shape[1]
    tm = _tile(t, 512)

    def body(*refs):
        a, b, h, g, dy = refs[:5]
        dh1, dgain, acc = refs[5 + len(deps):]
        i, j = pl.program_id(0), pl.program_id(1)

        @pl.when(j == 0)
        def _():
            acc[...] = jnp.zeros_like(acc)

        acc[...] += _dot(a[...], b[...], NT)

        @pl.when(j == N_DEV - 1)
        def _():
            _, vjp = jax.vjp(lambda x, gn: _rms(x, gn), h[...], g[...])
            dx, dg = vjp(acc[...])
            dh1[...] = (dx + dy[...].astype(F32)).astype(dh1.dtype)

            @pl.when(i == 0)
            def _():
                dgain[...] = jnp.zeros_like(dgain)

            dgain[...] += dg

    row = pl.BlockSpec((tm, d), lambda i, j: (i, 0))
    return pl.pallas_call(
        body, grid=(t // tm, N_DEV), name="ffn_up_bwd_x",
        in_specs=[pl.BlockSpec((None, tm, w), lambda i, j: (j, i, 0)), pl.BlockSpec((None, d, w), lambda i, j: (j, 0, 0)),
                  row, pl.BlockSpec((1, d), lambda i, j: (0, 0)), row] + [ANY_SPEC] * len(deps),
        out_specs=[row, pl.BlockSpec((1, d), lambda i, j: (0, 0))],
        out_shape=[SDS((t, d), BF16), SDS((1, d), F32)], scratch_shapes=[pltpu.VMEM((tm, d), F32)],
        compiler_params=_cp("arbitrary", "arbitrary"),
    )(dgu, wgu, h1, gain, dyb, *deps)


def _ffn_up_bwd_w(h1n, dgu):
    _, t, w = dgu.shape
    d = h1n.shape[1]
    tm = _tile(d, 512)

    def body(a, b, out):
        out[...] = _dot(a[...], b[...], TN).astype(BF16)

    return pl.pallas_call(
        body, grid=(8, d // tm), name="ffn_up_bwd_w",
        in_specs=[pl.BlockSpec((t, tm), lambda j, i: (0, i)), pl.BlockSpec((None, t, w), lambda j, i: (j, 0, 0))],
        out_specs=pl.BlockSpec((None, tm, w), lambda j, i: (j, i, 0)), out_shape=SDS((8, d, w), BF16),
        compiler_params=_cp("parallel", "parallel"),
    )(h1n, dgu)


def _fox_prep(fq, fk, sm, fb, qg, kg, h):
    qn = _rms(fq, qg)
    kn = _rms(fk, kg)
    c = _cumsum_rows(-_softplus(-(sm + fb)))
    ccol = _lane_pick(c, L_FF + h)
    crow = jnp.sum(c.T * (_iota((HD, 1), 0) == L_FF + h).astype(F32), axis=0, keepdims=True)
    return qn, kn, ccol, crow


def _softmax_times(s, v):
    e = jnp.exp(s - lax.stop_gradient(jnp.max(s, axis=1, keepdims=True)))
    return _dot(e.astype(BF16), v.astype(BF16)) * (1.0 / jnp.sum(e, axis=1, keepdims=True))


def _fox_block(q, k, v, cc, cr, off):
    bq = q.shape[0]
    assert k.shape[0] == off + bq
    s = _dot((q * (HD ** -0.5)).astype(BF16), k.astype(BF16), NT) + cc - cr
    diag = jnp.where(_iota((bq, bq), 1) <= _iota((bq, bq), 0), s[:, off:], -1e30)
    s = jnp.concatenate([s[:, :off], diag], axis=1) if off else diag
    return _softmax_times(s, v)


ONE_BUFFER = pl.Buffered(1)


def _pcol(t, cb):
    return pl.BlockSpec((t, HD), lambda h, cb=cb: (0, cb + h), pipeline_mode=ONE_BUFFER)


def _smcol(t):
    return pl.BlockSpec((t, HD), lambda h: (0, SM), pipeline_mode=ONE_BUFFER)


def _head(t):
    return pl.BlockSpec((t, HD), lambda h: (0, h), pipeline_mode=ONE_BUFFER)


def _small(n):
    return pl.BlockSpec((n, HD), lambda h: (0, 0), pipeline_mode=ONE_BUFFER)


def _fox_fwd(p, fb, qg, kg, bq):
    t = p.shape[0]

    def body(fq, fk, fv, sm, fb_r, qg_r, kg_r, o, qn_s, cc_s):
        h = pl.program_id(0)
        qn, kn, ccol, crow = _fox_prep(fq[...], fk[...], sm[...], fb_r[...], qg_r[...], kg_r[...], h)
        qn_s[...] = qn
        cc_s[...] = ccol
        knb = kn.astype(BF16)
        vb = fv[...].astype(BF16)
        for i in range(t // bq):
            rows, ext = pl.ds(i * bq, bq), (i + 1) * bq
            o[rows, :] = _fox_block(qn_s[rows, :], knb[:ext], vb[:ext], cc_s[rows, :], crow[:, :ext], i * bq).astype(o.dtype)

    return pl.pallas_call(
        body, grid=(NF,), name="fox_fwd",
        in_specs=[_pcol(t, FQ), _pcol(t, FK), _pcol(t, FV), _smcol(t), _small(1), _small(1), _small(1)],
        out_specs=_head(t), out_shape=SDS((t, NF * HD), BF16),
        scratch_shapes=[pltpu.VMEM((t, HD), F32), pltpu.VMEM((t, 1), F32)],
        compiler_params=_cp("parallel"),
    )(p, p, p, p, fb, qg, kg)


def _fox_bwd(p, fb, qg, kg, dmix, bq, deps=()):
    t = p.shape[0]

    def body(*refs):
        fq, fk, fv, sm, fb_r, qg_r, kg_r, do = refs[:8]
        dfq, dfk, dfv, dsm, dfb, dqg, dkg, qn_s, cc_s, dqn_s, dcc_s, dkn_s, dv_s, dcr_s = refs[8 + len(deps):]
        h = pl.program_id(0)
        qn, kn, ccol, crow = _fox_prep(fq[...], fk[...], sm[...], fb_r[...], qg_r[...], kg_r[...], h)
        qn_s[...] = qn
        cc_s[...] = ccol
        v = fv[...]
        dkn_s[...] = jnp.zeros_like(dkn_s)
        dv_s[...] = jnp.zeros_like(dv_s)
        dcr_s[...] = jnp.zeros_like(dcr_s)

        for i in range(t // bq):
            rows, ext = pl.ds(i * bq, bq), (i + 1) * bq
            _, vjp = jax.vjp(lambda a, b, c, d, e, off=i * bq: _fox_block(a, b, c, d, e, off),
                             qn_s[rows, :], kn[:ext], v[:ext], cc_s[rows, :], crow[:, :ext])
            dq, dk, dv, dcc, dcr = vjp(do[rows, :].astype(F32))
            dqn_s[rows, :] = dq
            dcc_s[rows, :] = dcc
            dkn_s[:ext, :] += dk
            dv_s[:ext, :] += dv
            dcr_s[:, :ext] += dcr
        _, prep_vjp = jax.vjp(lambda a, b, c, d, e, f: _fox_prep(a, b, c, d, e, f, h),
                              fq[...], fk[...], sm[...], fb_r[...], qg_r[...], kg_r[...])
        g_fq, g_fk, g_sm, g_fb, g_qg, g_kg = prep_vjp((dqn_s[...], dkn_s[...], dcc_s[...], dcr_s[...]))
        dfq[...] = g_fq.astype(dfq.dtype)
        dfk[...] = g_fk.astype(dfk.dtype)
        dfv[...] = dv_s[...].astype(dfv.dtype)

        @pl.when(h == 0)
        def _():
            for r in (dsm, dfb, dqg, dkg):
                r[...] = jnp.zeros_like(r)

        dsm[...] += g_sm
        dfb[...] += g_fb
        dqg[...] += g_qg
        dkg[...] += g_kg

    head = _head(t)
    return pl.pallas_call(
        body, grid=(NF,), name="fox_bwd",
        in_specs=[_pcol(t, FQ), _pcol(t, FK), _pcol(t, FV), _smcol(t), _small(1), _small(1), _small(1), head]
        + [ANY_SPEC] * len(deps),
        out_specs=[head, head, head, _small(t), _small(1), _small(1), _small(1)],
        out_shape=[SDS((t, NF * HD), BF16)] * 3 + [SDS((t, HD), F32)] + [SDS((1, HD), F32)] * 3,
        scratch_shapes=[pltpu.VMEM((t, HD), F32), pltpu.VMEM((t, 1), F32), pltpu.VMEM((t, HD), F32),
                        pltpu.VMEM((t, 1), F32), pltpu.VMEM((t, HD), F32), pltpu.VMEM((t, HD), F32),
                        pltpu.VMEM((1, t), F32)],
        compiler_params=_cp("arbitrary"),
    )(p, p, p, p, fb, qg, kg, dmix, *deps)


def _mem_attn(mq, mk, mv, qg, kg):
    s = _dot((_rms(mq, qg) * (HD ** -0.5)).astype(BF16), _rms(mk, kg).astype(BF16), NT)
    return _softmax_times(s, mv)


def _mem_fwd(p, mkv, qg, kg):
    t, ml = p.shape[0], mkv.shape[0]

    def body(mq, mk, mv, qg_r, kg_r, o):
        o[...] = _mem_attn(mq[...], mk[...], mv[...], qg_r[...], kg_r[...]).astype(o.dtype)

    return pl.pallas_call(
        body, grid=(NM,), name="mem_fwd",
        in_specs=[_pcol(t, MQ), pl.BlockSpec((ml, HD), lambda h: (0, h)), pl.BlockSpec((ml, HD), lambda h: (0, NM + h)),
                  _small(1), _small(1)],
        out_specs=pl.BlockSpec((t, HD), lambda h: (0, h)), out_shape=SDS((t, NM * HD), BF16),
        compiler_params=_cp("parallel"),
    )(p, mkv, mkv, qg, kg)


def _mem_bwd(p, mkv, qg, kg, dmix, deps=()):
    t, ml = p.shape[0], mkv.shape[0]

    def body(*refs):
        mq, mk, mv, qg_r, kg_r, do = refs[:6]
        dmq, dmk, dmv, dqg, dkg = refs[6 + len(deps):]
        _, vjp = jax.vjp(_mem_attn, mq[...], mk[...], mv[...], qg_r[...], kg_r[...])
        g_q, g_k, g_v, g_qg, g_kg = vjp(do[...].astype(F32))
        dmq[...] = g_q.astype(dmq.dtype)
        dmk[...] = g_k
        dmv[...] = g_v

        @pl.when(pl.program_id(0) == 0)
        def _():
            dqg[...] = jnp.zeros_like(dqg)
            dkg[...] = jnp.zeros_like(dkg)

        dqg[...] += g_qg
        dkg[...] += g_kg

    return pl.pallas_call(
        body, grid=(NM,), name="mem_bwd",
        in_specs=[_pcol(t, MQ), pl.BlockSpec((ml, HD), lambda h: (0, h)), pl.BlockSpec((ml, HD), lambda h: (0, NM + h)),
                  _small(1), _small(1), pl.BlockSpec((t, HD), lambda h: (0, NF + NG + h))] + [ANY_SPEC] * len(deps),
        out_specs=[pl.BlockSpec((t, HD), lambda h: (0, h)), pl.BlockSpec((ml, HD), lambda h: (0, h)),
                   pl.BlockSpec((ml, HD), lambda h: (0, h)), _small(1), _small(1)],
        out_shape=[SDS((t, NM * HD), BF16), SDS((ml, NM * HD), F32), SDS((ml, NM * HD), F32),
                   SDS((1, HD), F32), SDS((1, HD), F32)],
        compiler_params=_cp("arbitrary"),
    )(p, mkv, mkv, qg, kg, dmix, *deps)


def _shift_down(x, s):
    if s == 0:
        return x
    return jnp.where(_iota(x.shape, 0) >= s, pltpu.roll(x, s, 0), 0.0)


def _shift_up(x, s):
    if s == 0:
        return x
    n = x.shape[0]
    return jnp.where(_iota(x.shape, 0) < n - s, pltpu.roll(x, n - s, 0), 0.0)


@jax.custom_vjp
def _conv4(x, w0, w1, w2, w3):
    return w0 * _shift_down(x, 3) + w1 * _shift_down(x, 2) + w2 * _shift_down(x, 1) + w3 * x


def _conv4_fwd(x, w0, w1, w2, w3):
    return _conv4(x, w0, w1, w2, w3), (x, w0, w1, w2, w3)


def _conv4_bwd(res, dy):
    x, w0, w1, w2, w3 = res
    ups = [_shift_up(dy, 3 - k) for k in range(4)]
    dx = w0 * ups[0] + w1 * ups[1] + w2 * ups[2] + w3 * ups[3]
    return (dx,) + tuple(jnp.sum(up * x, axis=0, keepdims=True) for up in ups)


_conv4.defvjp(_conv4_fwd, _conv4_bwd)


HALO = 8


def _gdn_gates(sm, alog, dtb):
    lane = _iota((1, HD), 1)
    g = -jnp.exp(alog) * _softplus(sm + dtb)
    return (jnp.where((lane >= L_GA) & (lane < L_GA + NG), g,
                      jnp.where((lane >= L_GB) & (lane < L_GB + NG), _sigmoid(sm), 0.0)),)


def _gdn_prep(gq, gk, gv, gates, taps, h):
    q, k, v = [_silu(_conv4(x, *taps[4 * j:4 * j + 4]))[HALO:] for j, x in enumerate((gq, gk, gv))]
    q = q * lax.rsqrt(jnp.sum(q * q, axis=-1, keepdims=True) + NORM_EPS) * (HD ** -0.5)
    k = k * lax.rsqrt(jnp.sum(k * k, axis=-1, keepdims=True) + NORM_EPS)
    return q, k, v, _lane_pick(gates, L_GA + h), _lane_pick(gates, L_GB + h)


def _split(x, n):
    parts, rest = [], x
    for i in range(n):
        parts.append(rest.astype(BF16))
        if i + 1 < n:
            rest = rest - parts[-1].astype(F32)
    return parts


def _raw_dot(a, b, form):
    lead = a.ndim - 2
    ca, cb = {"nn": (1, 0), "nt": (1, 1), "tn": (0, 0)}[form]
    batch = ((0,), (0,)) if lead else ((), ())
    return lax.dot_general(a, b, (((ca + lead,), (cb + lead,)), batch), preferred_element_type=F32)


def _pdot_impl(a, b, form, mode):
    if mode == "1":
        return _raw_dot(a.astype(BF16), b.astype(BF16), form)
    if mode == "3":
        (ah, al), (bh, bl) = _split(a, 2), _split(b, 2)
        return _raw_dot(ah, bh, form) + (_raw_dot(al, bh, form) + _raw_dot(ah, bl, form))
    if mode == "xa":
        return sum(_raw_dot(a.astype(BF16), t, form) for t in reversed(_split(b, 3)))
    return sum(_raw_dot(t, b.astype(BF16), form) for t in reversed(_split(a, 3)))


@functools.partial(jax.custom_vjp, nondiff_argnums=(2, 3))
def _pdot(a, b, form, mode):
    return _pdot_impl(a, b, form, mode)


def _pdot_fwd(a, b, form, mode):
    return _pdot_impl(a, b, form, mode), (a, b)


def _pdot_bwd(form, mode, res, ct):
    a, b = res
    da_args, db_args = {"nn": ((ct, b, "nt"), (a, ct, "tn")), "nt": ((ct, b, "nn"), (ct, a, "tn")),
                        "tn": ((b, ct, "nt"), (a, ct, "nn"))}[form]

    def side(args, exact):
        if mode in ("1", "3"):
            return mode
        return "xa" if args[0] is exact else "xb"

    if mode == "xa":
        return jnp.zeros_like(a), _pdot_impl(*db_args, side(db_args, a))
    if mode == "xb":
        return _pdot_impl(*da_args, side(da_args, b)), jnp.zeros_like(b)
    return _pdot_impl(*da_args, mode), _pdot_impl(*db_args, mode)


_pdot.defvjp(_pdot_fwd, _pdot_bwd)

GDN_QK, GDN_INV, GDN_SCAN = "1", "1", "1"


@jax.custom_vjp
def _tri_inv(low):
    eye = (_iota((CHUNK, CHUNK), 0) == _iota((CHUNK, CHUNK), 1)).astype(F32)
    inv = eye - low
    pw = low
    for _ in range(5):
        pw = _pdot_impl(pw, pw, "nn", GDN_INV)
        inv = inv + _pdot_impl(inv, pw, "nn", GDN_INV)
    return inv


def _tri_inv_fwd(low):
    inv = _tri_inv(low)
    return inv, inv


def _tri_inv_bwd(inv, ct):
    return (-_pdot_impl(_pdot_impl(inv, ct, "tn", GDN_INV), inv, "nt", GDN_INV),)


_tri_inv.defvjp(_tri_inv_fwd, _tri_inv_bwd)


def _gdn_intra(q, k, v, g, beta):
    n = q.shape[0]
    r, c = _iota((CHUNK, CHUNK), 0), _iota((CHUNK, CHUNK), 1)
    tril, strict = r >= c, r > c
    trilf = jnp.broadcast_to(tril.astype(F32), (n, CHUNK, CHUNK))
    gcm = _pdot(trilf, jnp.broadcast_to(g, (n, CHUNK, CHUNK)), "nn", "xa")
    gcf = _pdot(trilf, jnp.broadcast_to(g, (n, CHUNK, HD)), "nn", "xa")
    lane0 = (_iota((1, 1, CHUNK), 2) == 0).astype(F32)
    gcr = _pdot(jnp.ones((n, CHUNK, CHUNK), F32), gcm * lane0, "nt", "xa")
    decay = jnp.where(tril, jnp.exp(jnp.where(tril, gcm - gcr, 0.0)), 0.0)
    egc = jnp.exp(gcf)
    kb = k * beta
    low = jnp.where(strict, _pdot(kb, k, "nt", GDN_QK) * decay, 0.0)
    inv = _tri_inv(low)
    u = _pdot(inv, v * beta, "nn", GDN_INV)
    w = _pdot(inv, kb * egc, "nn", GDN_INV)
    at = jnp.where(tril, _pdot(q, k, "nt", GDN_QK) * decay, 0.0)
    gl = jnp.sum(jnp.broadcast_to(g, (n, CHUNK, HD)), axis=1, keepdims=True)
    kd = k * jnp.exp(gl - gcf)
    return (_pdot(kd, w, "tn", GDN_SCAN), _pdot(kd, u, "tn", GDN_SCAN), q * egc - _pdot(at, w, "nn", GDN_SCAN),
            _pdot(at, u, "nn", GDN_SCAN), gl)


def _gdn_step(s, kw, ku, a, b, gl):
    return _pdot(a, s, "nn", GDN_SCAN) + b, s * jnp.exp(gl) - _pdot(kw, s, "nn", GDN_SCAN) + ku


SCAN_HEADS = 3
SCAN_UNROLL = 4


def _gdn_chunked_scratch(nc):
    big = pltpu.VMEM((nc, CHUNK, HD), F32)
    return [big, big, big, pltpu.VMEM((nc, CHUNK, 1), F32), pltpu.VMEM((nc, CHUNK, 1), F32)]


N_TERMS = 5


def _gdn_term_shapes(nc):
    return [(nc, HD, HD), (nc, HD, HD), (nc, CHUNK, HD), (nc, CHUNK, HD), (nc, 1, HD)]


def _per_head(shape, heads=None, one_buffer=True):
    lead = (None,) if heads is None else (heads,)
    return pl.BlockSpec(lead + tuple(shape), lambda h: (h,) + (0,) * len(shape),
                        pipeline_mode=ONE_BUFFER if one_buffer else None)


def _gdn_in_specs(t):
    cw = lambda cb: pl.BlockSpec((4, HD), lambda h, cb=cb: (0, cb + h))
    return [_pcol(t, GQ), _pcol(t, GK), _pcol(t, GV), _small(t), cw(0), cw(NG), cw(2 * NG)]


def _taps(wq, wk, wv):
    return tuple(w[k:k + 1, :] for w in (wq, wk, wv) for k in range(4))


def _prep_rows(t):
    return min(t, 256)


def _gdn_pad(srcs, pads):
    for src, pad in zip(srcs, pads):
        pad[0:HALO, :] = jnp.zeros((HALO, HD), F32)
        pad[HALO:, :] = src[...]


def _gdn_stage(pads, gates, taps, h, chunked):
    t = gates.shape[0]
    rows = _prep_rows(t)
    per = rows // CHUNK

    def tile(i, carry):
        r0 = pl.multiple_of(i * rows, rows)
        vals = _gdn_prep(*[p[pl.ds(r0, rows + HALO), :] for p in pads], gates[pl.ds(r0, rows), :], taps, h)
        for v, r in zip(vals, chunked):
            r[pl.ds(i * per, per)] = v.reshape(per, CHUNK, v.shape[-1])
        return carry

    lax.fori_loop(0, t // rows, tile, 0)


def _gdn_intra_all(chunked, intra):
    nc = chunked[0].shape[0]
    grp_n = math.gcd(nc, GROUP)

    def grp(i, carry):
        sl = pl.ds(pl.multiple_of(i * grp_n, grp_n), grp_n)
        for r, val in zip(intra, _gdn_intra(*[c[sl] for c in chunked])):
            r[sl] = val
        return carry

    lax.fori_loop(0, nc // grp_n, grp, 0)


def _gdn_fwd(pa, gates, conv):
    t = pa.shape[0]
    nc = t // CHUNK
    terms = _gdn_term_shapes(nc)

    def body(gq, gk, gv, gt, wq, wk, wv, *rest):
        h = pl.program_id(0)
        intra, chunked, pads = rest[:N_TERMS], rest[N_TERMS:N_TERMS + 5], rest[N_TERMS + 5:]
        _gdn_pad((gq, gk, gv), pads)
        _gdn_stage(pads, gt, _taps(wq, wk, wv), h, chunked)
        _gdn_intra_all(chunked, intra)

    qkv = [(nc, CHUNK, HD)] * 3
    outs = pl.pallas_call(
        body, grid=(NG,), name="gdn_fwd", in_specs=_gdn_in_specs(t),
        out_specs=[_per_head(sh, one_buffer=False) for sh in terms + qkv],
        out_shape=[SDS((NG,) + sh, F32) for sh in terms + qkv],
        scratch_shapes=_gdn_chunked_scratch(nc)[3:] + [pltpu.VMEM((t + HALO, HD), F32)] * 3, compiler_params=_cp("parallel"),
    )(pa, pa, pa, gates, conv, conv, conv)
    return list(outs[:N_TERMS]), list(outs[N_TERMS:])


def _gdn_scan(terms_in):
    nc = terms_in[0].shape[1]
    terms = _gdn_term_shapes(nc)

    def body(*refs):
        intra, o, states = refs[:N_TERMS], refs[N_TERMS], refs[N_TERMS + 1]

        def one(c, ss):
            rows = pl.ds(pl.multiple_of(c * CHUNK, CHUNK), CHUNK)
            loaded = [[r[hh, c] for r in intra] for hh in range(SCAN_HEADS)]
            res = [_gdn_step(ss[hh], *loaded[hh]) for hh in range(SCAN_HEADS)]
            for hh in range(SCAN_HEADS):
                states[hh, c] = ss[hh]
                o[rows, hh * HD:(hh + 1) * HD] = res[hh][0]
            return tuple(r[1] for r in res)

        per_trip = math.gcd(nc, SCAN_UNROLL)

        def step(i, ss):
            for k in range(per_trip):
                ss = one(per_trip * i + k, ss)
            return ss

        lax.fori_loop(0, nc // per_trip, step, tuple(jnp.zeros((HD, HD), F32) for _ in range(SCAN_HEADS)))

    return pl.pallas_call(
        body, grid=(NG // SCAN_HEADS,), name="gdn_scan", in_specs=[_per_head(sh, SCAN_HEADS) for sh in terms],
        out_specs=[pl.BlockSpec((nc * CHUNK, SCAN_HEADS * HD), lambda h: (0, h), pipeline_mode=ONE_BUFFER),
                   _per_head((nc, HD, HD), SCAN_HEADS)],
        out_shape=[SDS((nc * CHUNK, NG * HD), F32), SDS((NG, nc, HD, HD), F32)], compiler_params=_cp("parallel"),
    )(*terms_in)


def _gdn_bwd_scan(saved, do_raw):
    nc = saved[0].shape[1]
    terms = _gdn_term_shapes(nc)

    def body(*refs):
        intra, states, do, outs = refs[:N_TERMS], refs[N_TERMS], refs[N_TERMS + 1], refs[N_TERMS + 2:]

        def one(c, dss):
            rows = pl.ds(pl.multiple_of(c * CHUNK, CHUNK), CHUNK)
            loaded = [[states[hh, c]] + [r[hh, c] for r in intra] for hh in range(SCAN_HEADS)]
            cts = [do[rows, hh * HD:(hh + 1) * HD] for hh in range(SCAN_HEADS)]
            grads = [jax.vjp(_gdn_step, *loaded[hh])[1]((cts[hh], dss[hh])) for hh in range(SCAN_HEADS)]
            for hh in range(SCAN_HEADS):
                for r, gval in zip(outs, grads[hh][1:]):
                    r[hh, c] = gval
            return tuple(g[0] for g in grads)

        per_trip = math.gcd(nc, SCAN_UNROLL)

        def bwd(i, dss):
            c = nc - 1 - per_trip * i
            for k in range(per_trip):
                dss = one(c - k, dss)
            return dss

        lax.fori_loop(0, nc // per_trip, bwd, tuple(jnp.zeros((HD, HD), F32) for _ in range(SCAN_HEADS)))

    return pl.pallas_call(
        body, grid=(NG // SCAN_HEADS,), name="gdn_bwd_scan",
        in_specs=[_per_head(sh, SCAN_HEADS) for sh in terms] + [_per_head((nc, HD, HD), SCAN_HEADS)]
        + [pl.BlockSpec((nc * CHUNK, SCAN_HEADS * HD), lambda h: (0, h), pipeline_mode=ONE_BUFFER)],
        out_specs=[_per_head(sh, SCAN_HEADS) for sh in terms],
        out_shape=[SDS((NG,) + sh, F32) for sh in terms], compiler_params=_cp("parallel"),
    )(*saved, do_raw)


def _gdn_bwd(pa, gates, conv, dterms, qkv):
    t = pa.shape[0]
    nc = t // CHUNK
    terms = _gdn_term_shapes(nc)

    def body(*refs):
        gq, gk, gv, gt, wq, wk, wv = refs[:7]
        dintra, qkv = refs[7:7 + N_TERMS], refs[7 + N_TERMS:10 + N_TERMS]
        dgq, dgk, dgv, dgt, dwq, dwk, dwv = refs[10 + N_TERMS:17 + N_TERMS]
        chunked, pads, dpads, dgt_s = (refs[17 + N_TERMS:22 + N_TERMS], refs[22 + N_TERMS:25 + N_TERMS],
                                       refs[25 + N_TERMS:28 + N_TERMS], refs[28 + N_TERMS])
        h = pl.program_id(0)
        taps = _taps(wq, wk, wv)
        _gdn_pad((gq, gk, gv), pads)
        rows = _prep_rows(t)
        per = rows // CHUNK

        def gates_tile(i, carry):
            gtile = gt[pl.ds(pl.multiple_of(i * rows, rows), rows), :]
            chunked[3][pl.ds(i * per, per)] = _lane_pick(gtile, L_GA + h).reshape(per, CHUNK, 1)
            chunked[4][pl.ds(i * per, per)] = _lane_pick(gtile, L_GB + h).reshape(per, CHUNK, 1)
            return carry

        lax.fori_loop(0, t // rows, gates_tile, 0)
        grp_n = math.gcd(nc, GROUP)

        def grp(i, carry):
            sl = pl.ds(pl.multiple_of(i * grp_n, grp_n), grp_n)
            _, vjp = jax.vjp(_gdn_intra, *[r[sl] for r in qkv], chunked[3][sl], chunked[4][sl])
            for r, gval in zip(chunked, vjp(tuple(r[sl] for r in dintra))):
                r[sl] = gval
            return carry

        lax.fori_loop(0, nc // grp_n, grp, 0)

        for r in dpads:
            r[...] = jnp.zeros_like(r)

        def tile(i, dtaps):
            r0 = pl.multiple_of(i * rows, rows)
            win = pl.ds(r0, rows + HALO)
            _, vjp = jax.vjp(lambda *a: _gdn_prep(*a, h), *[p[win, :] for p in pads], gt[pl.ds(r0, rows), :], taps)
            grads = vjp(tuple(r[pl.ds(i * per, per)].reshape(rows, r.shape[-1]) for r in chunked))
            for r, gval in zip(dpads, grads[:3]):
                r[win, :] += gval
            dgt_s[pl.ds(r0, rows), :] = grads[3]
            return jax.tree.map(jnp.add, dtaps, grads[4])

        dtaps = lax.fori_loop(0, t // rows, tile, (jnp.zeros((1, HD), F32),) * 12)
        for r, dpad in zip((dgq, dgk, dgv), dpads):
            r[...] = dpad[HALO:, :].astype(r.dtype)
        for j, r in enumerate((dwq, dwk, dwv)):
            for k in range(4):
                r[k:k + 1, :] = dtaps[4 * j + k]

        @pl.when(h == 0)
        def _():
            dgt[...] = jnp.zeros_like(dgt)

        dgt[...] += dgt_s[...]

    head = _head(t)
    taps = pl.BlockSpec((4, HD), lambda h: (0, h))
    return pl.pallas_call(
        body, grid=(NG,), name="gdn_bwd",
        in_specs=_gdn_in_specs(t) + [_per_head(sh) for sh in terms + [(nc, CHUNK, HD)] * 3],
        out_specs=[head, head, head, _small(t), taps, taps, taps],
        out_shape=[SDS((t, NG * HD), BF16)] * 3 + [SDS((t, HD), F32)] + [SDS((4, NG * HD), F32)] * 3,
        scratch_shapes=_gdn_chunked_scratch(nc) + [pltpu.VMEM((t + HALO, HD), F32)] * 6 + [pltpu.VMEM((t, HD), F32)],
        compiler_params=_cp("arbitrary"),
    )(pa, pa, pa, gates, conv, conv, conv, *dterms, *qkv)


def _gdn_post(o, z, gain):
    return (jnp.concatenate(
        [_rms(o[:, h * HD:(h + 1) * HD], gain) * _silu(z[:, h * HD:(h + 1) * HD]) for h in range(NG)], axis=1),)


def _place():
    return lax.axis_index("x"), lax.axis_index("y"), lax.axis_index("c")


def _sum_blocks(name, parts):
    _, r, c = parts.shape
    tr = 64 if r % 64 == 0 else r

    def body(x, o):
        acc = x[0].astype(F32)
        for d in range(1, N_DEV):
            acc = acc + x[d].astype(F32)
        o[...] = acc

    return pl.pallas_call(
        body, grid=(r // tr,), name=name, in_specs=[pl.BlockSpec((N_DEV, tr, c), lambda i: (0, i, 0))],
        out_specs=pl.BlockSpec((tr, c), lambda i: (i, 0)), out_shape=SDS((r, c), F32), compiler_params=_cp("parallel"),
    )(parts)


def _all_reduce_small(name, x):
    m_per, n = x.shape

    def body(x_ref, out_ref, send_sems, recv_sems, local_sem):
        px, py, pc = _place()
        me, sibling = (px, py, pc), (px, py, 1 - pc)
        chips = [(1 - px, py), (px, 1 - py), (1 - px, 1 - py)]
        buf = out_ref

        def rows(qx, qy, qc):
            return buf.at[pl.ds((4 * qx + 2 * qy + qc) * m_per, m_per), :]

        def copy(k, block, to, src=None):
            return pltpu.make_async_remote_copy(
                src_ref=rows(*block) if src is None else src, dst_ref=rows(*block),
                send_sem=send_sems.at[k], recv_sem=recv_sems.at[k], device_id=to, device_id_type=MESH)

        mine = pltpu.make_async_copy(x_ref, rows(*me), local_sem)
        mine.start()
        first = [copy(0, me, sibling, src=x_ref)]
        first += [copy(1 + j, me, (*chip, pc), src=x_ref) for j, chip in enumerate(chips)]
        for cp in first:
            cp.start()
        passed = [copy(4 + j, (*chip, pc), sibling) for j, chip in enumerate(chips)]
        for j, chip in enumerate(chips):
            copy(1 + j, (*chip, pc), me).wait_recv()
            passed[j].start()
        copy(0, sibling, me).wait_recv()
        for j, chip in enumerate(chips):
            copy(4 + j, (*chip, 1 - pc), me).wait_recv()
        for cp in first + passed:
            cp.wait_send()
        mine.wait()

    gathered = pl.pallas_call(
        body, name=name, out_shape=SDS((N_DEV * m_per, n), x.dtype),
        in_specs=[pl.BlockSpec(memory_space=pltpu.VMEM)], out_specs=pl.BlockSpec(memory_space=pltpu.VMEM),
        scratch_shapes=[pltpu.SemaphoreType.DMA((7,)), pltpu.SemaphoreType.DMA((7,)), pltpu.SemaphoreType.DMA],
    )(x)
    return _sum_blocks(name + "_sum", gathered.reshape(N_DEV, m_per, n))


HBM_SPEC = pl.BlockSpec(memory_space=pltpu.HBM)
SEM_SPEC = pl.BlockSpec(memory_space=pltpu.SEMAPHORE)
EFFECT = pltpu.SideEffectType.DATAFLOW_SIDE_EFFECTING


def _copies_start(name, bufs, n_remote, n_local, build, deps):
    nb, nd = len(bufs), len(deps)
    sem_shapes = [pltpu.SemaphoreType.DMA((n_remote,)), pltpu.SemaphoreType.DMA((n_remote,))]
    if n_local:
        sem_shapes.append(pltpu.SemaphoreType.DMA((n_local,)))
    ns = len(sem_shapes)

    def body(*refs):
        sems = refs[nb + nd:nb + nd + ns]
        remote, local = build(refs[:nb], *sems, *([None] * (3 - ns)))
        for cp in local + remote:
            cp.start()
        refs[-1][...] = jnp.zeros((8, HD), F32)

    outs = pl.pallas_call(
        body, name=name,
        out_shape=(*sem_shapes, *[pltpu.HBM(b.shape, b.dtype) for b in bufs], SDS((8, HD), F32)),
        in_specs=[HBM_SPEC] * nb + [ANY_SPEC] * nd,
        out_specs=(*[SEM_SPEC] * ns, *[HBM_SPEC] * nb, pl.BlockSpec(memory_space=pltpu.VMEM)),
        input_output_aliases={i: ns + i for i in range(nb)},
        compiler_params=pltpu.CompilerParams(has_side_effects=EFFECT),
    )(*[pltpu.with_memory_space_constraint(b, pltpu.HBM) for b in bufs], *deps)
    return list(outs[:ns]), list(outs[ns:ns + nb]), outs[-1]


def _copies_wait(name, bufs, sems, build, after):
    nb, ns = len(bufs), len(sems)

    def body(*refs):
        remote, local = build(refs[:nb], *refs[nb:nb + ns], *([None] * (3 - ns)))
        for cp in local:
            cp.wait()
        for cp in remote:
            cp.wait_send()
            cp.wait_recv()

    outs = pl.pallas_call(
        body, name=name, out_shape=tuple(pltpu.HBM(b.shape, b.dtype) for b in bufs),
        in_specs=[HBM_SPEC] * nb + [SEM_SPEC] * ns + [ANY_SPEC] * len(after), out_specs=tuple([HBM_SPEC] * nb),
        input_output_aliases={i: i for i in range(nb)},
        compiler_params=pltpu.CompilerParams(has_side_effects=EFFECT),
    )(*bufs, *sems, *after)
    return list(outs)


def _remote(src, dst, send, recv, k, to):
    return pltpu.make_async_remote_copy(src_ref=src, dst_ref=dst, send_sem=send.at[k], recv_sem=recv.at[k],
                                        device_id=to, device_id_type=MESH)


class _Gather:
    def __init__(self, name, shards, deps):
        self.name, self.n = name, len(shards)
        lands = [lax.empty((N_DEV,) + s.shape, s.dtype) for s in shards]
        self.sems1, bufs, self.token = _copies_start(
            name + "_s1", list(shards) + lands, 4 * self.n, self.n, self._stage1(range(self.n)), deps)
        self.shards, self.lands, self.sems2 = bufs[:self.n], bufs[self.n:], {}

    def _stage1(self, idxs):
        def build(refs, send, recv, loc):
            x, y, c = _place()
            me = 4 * x + 2 * y + c
            targets = [(x, y, 1 - c), (1 - x, y, c), (x, 1 - y, c), (1 - x, 1 - y, c)]
            remote, local = [], []
            for pos, i in enumerate(idxs):
                src, land = refs[pos], refs[len(idxs) + pos]
                local.append(pltpu.make_async_copy(src, land.at[me], loc.at[i]))
                remote += [_remote(src, land.at[me], send, recv, 4 * i + k, to) for k, to in enumerate(targets)]
            return remote, local
        return build

    @staticmethod
    def _stage2(refs, send, recv, loc):
        x, y, c = _place()
        remote = []
        for pos, land in enumerate(refs):
            for j, (cx, cy) in enumerate([(1 - x, y), (x, 1 - y), (1 - x, 1 - y)]):
                blk = land.at[4 * cx + 2 * cy + c]
                remote.append(_remote(blk, blk, send, recv, 3 * pos + j, (x, y, 1 - c)))
        return remote, []

    def pass_on(self, idxs, after):
        tag, m = "".join(map(str, idxs)), len(idxs)
        bufs = _copies_wait(f"{self.name}_w1_{tag}", [self.shards[i] for i in idxs] + [self.lands[i] for i in idxs],
                            self.sems1, self._stage1(idxs), after)
        self.sems2[tag], lands, token = _copies_start(f"{self.name}_s2_{tag}", bufs[m:], 3 * m, 0, self._stage2, ())
        for pos, i in enumerate(idxs):
            self.lands[i] = lands[pos]
        return [token]

    def get(self, idxs, after):
        tag = "".join(map(str, idxs))
        return _copies_wait(f"{self.name}_w2_{tag}", [self.lands[i] for i in idxs], self.sems2[tag], self._stage2, after)


class _RelayGather:
    def __init__(self, name, shards, deps):
        self.name, self.n = name, len(shards)
        lands = [lax.empty((N_DEV,) + s.shape, s.dtype) for s in shards]
        self.sems, bufs, self.token = _copies_start(name + "_s1", list(shards) + lands, 3 * self.n, self.n, self._stage1, deps)
        self.shards, self.lands = bufs[:self.n], bufs[self.n:]

    def _stage1(self, refs, send, recv, loc):
        x, y, c = _place()
        me = 4 * x + 2 * y + c
        remote, local = [], []
        for i in range(self.n):
            src, land = refs[i], refs[self.n + i]
            local.append(pltpu.make_async_copy(src, land.at[me], loc.at[i]))
            remote += [_remote(src, land.at[me], send, recv, 3 * i + k, to)
                       for k, to in enumerate([(x, y, 1 - c), (1 - x, y, c), (x, 1 - y, c)])]
        return remote, local

    @staticmethod
    def _relay(refs, send, recv, loc):
        x, y, c = _place()
        remote = []
        for i, land in enumerate(refs):
            half = land.shape[1] // 2
            from_x = land.at[4 * (1 - x) + 2 * y + c].at[pl.ds(0, half)]
            from_y = land.at[4 * x + 2 * (1 - y) + c].at[pl.ds(half, half)]
            remote += [_remote(from_x, from_x, send, recv, 2 * i, (x, 1 - y, c)),
                       _remote(from_y, from_y, send, recv, 2 * i + 1, (1 - x, y, c))]
        return remote, []

    def forward(self, after):
        bufs = _copies_wait(self.name + "_w1", self.shards + self.lands, self.sems, self._stage1, after)
        self.sems, self.lands, self.token = _copies_start(self.name + "_sf", bufs[self.n:], 2 * self.n, 0, self._relay, ())
        return [self.token]

    def pass_on(self, after):
        lands = _copies_wait(self.name + "_wf", self.lands, self.sems, self._relay, after)
        self.sems, self.lands, self.token = _copies_start(self.name + "_s2", lands, 3 * self.n, 0, _Gather._stage2, ())
        return [self.token]

    def get(self, after):
        return _copies_wait(self.name + "_w2", self.lands, self.sems, _Gather._stage2, after)


def _rows_tile(r, row_bytes, target=1 << 20):
    tr = r
    while tr % 32 == 0 and tr * row_bytes > target:
        tr //= 2
    return tr


def _pair_add(name, g, got, c):
    _, r, cols = g.shape
    tr = _rows_tile(r, cols * 2)

    def body(s, a, b, o):
        o[...] = (a[...].astype(F32) + b[...].astype(F32)).astype(o.dtype)

    return pl.pallas_call(
        body, name=name, out_shape=SDS((4, r, cols), g.dtype),
        grid_spec=pltpu.PrefetchScalarGridSpec(
            num_scalar_prefetch=1, grid=(4, r // tr),
            in_specs=[pl.BlockSpec((None, tr, cols), lambda j, i, s: (2 * j + s[0], i, 0)),
                      pl.BlockSpec((None, tr, cols), lambda j, i, s: (j, i, 0))],
            out_specs=pl.BlockSpec((None, tr, cols), lambda j, i, s: (j, i, 0))),
        compiler_params=_cp("parallel", "parallel"),
    )(c.reshape(1), g, got)


def _quad_sum(name, part, got, chip, wmv=None):
    _, r, cols = part.shape
    tr = _rows_tile(r, cols * 4)
    n_out = 4 if wmv else 1

    def body(s, a, b1, b2, b3, *rest):
        g = ((a[...].astype(F32) + b1[...].astype(F32)) + b2[...].astype(F32)) + b3[...].astype(F32)
        rest[-n_out][...] = g
        if wmv:
            w, m, v = rest[:3]
            rest[-3][...], rest[-2][...], rest[-1][...] = _adamw(w[...], g, m[...], v[...])

    blk = lambda k: pl.BlockSpec((None, tr, cols), lambda i, s, k=k: (jnp.bitwise_xor(s[0], k), i, 0))
    row = pl.BlockSpec((tr, cols), lambda i, s: (i, 0))
    outs = pl.pallas_call(
        body, name=name, out_shape=[SDS((r, cols), F32)] * n_out,
        grid_spec=pltpu.PrefetchScalarGridSpec(
            num_scalar_prefetch=1, grid=(r // tr,), in_specs=[blk(0), blk(1), blk(2), blk(3)] + [row] * (n_out - 1),
            out_specs=[row] * n_out),
        compiler_params=_cp("parallel"),
    )(chip.reshape(1), part, got, got, got, *(wmv or ()))
    return tuple(outs) if wmv else outs[0]


class _Scatter:
    def __init__(self, name, grads, deps):
        self.name, self.n = name, len(grads)
        got = [lax.empty((4,) + g.shape[1:], g.dtype) for g in grads]
        self.sems, bufs, self.token = _copies_start(name + "_s1", list(grads) + got, 4 * self.n, 0, self._stage1, deps)
        self.grads, self.got = bufs[:self.n], bufs[self.n:]

    def _stage1(self, refs, send, recv, loc):
        x, y, c = _place()
        remote = []
        for i in range(self.n):
            remote += [_remote(refs[i].at[2 * j + 1 - c], refs[self.n + i].at[j], send, recv, 4 * i + j, (x, y, 1 - c))
                       for j in range(4)]
        return remote, []

    def _stage2(self, refs, send, recv, loc):
        x, y, c = _place()
        remote = []
        for i in range(self.n):
            for k in (1, 2, 3):
                tx = 1 - x if k & 2 else x
                ty = 1 - y if k & 1 else y
                remote.append(_remote(refs[i].at[2 * tx + ty], refs[self.n + i].at[2 * x + y], send, recv,
                                      3 * i + k - 1, (tx, ty, c)))
        return remote, []

    def mid(self, after):
        bufs = _copies_wait(self.name + "_w1", self.grads + self.got, self.sems, self._stage1, after)
        c = lax.axis_index("c").astype(jnp.int32)
        parts = [_pair_add(f"{self.name}_add{i}", bufs[i], bufs[self.n + i], c) for i in range(self.n)]
        got = [lax.empty(p.shape, p.dtype) for p in parts]
        self.sems, bufs, self.token = _copies_start(self.name + "_s2", parts + got, 3 * self.n, 0, self._stage2, ())
        self.parts, self.got = bufs[:self.n], bufs[self.n:]

    def end(self, after, wmv=None):
        bufs = _copies_wait(self.name + "_w2", self.parts + self.got, self.sems, self._stage2, after)
        chip = (2 * lax.axis_index("x") + lax.axis_index("y")).astype(jnp.int32)
        wmv = wmv or [None] * self.n
        return [_quad_sum(f"{self.name}_sum{i}", bufs[i], bufs[self.n + i], chip, wmv[i]) for i in range(self.n)]


def _adamw(w, g, m, v):
    m = ADAM_B1 * m + (1.0 - ADAM_B1) * g
    v = ADAM_B2 * v + (1.0 - ADAM_B2) * (g * g)
    m_hat = m / (1.0 - ADAM_B1 ** ADAM_STEP)
    v_hat = v / (1.0 - ADAM_B2 ** ADAM_STEP)
    return -ADAM_LR * (m_hat / (jnp.sqrt(v_hat) + ADAM_EPS) + ADAM_WD * w), m, v


def _adamw_call(name, w, g, m, v):
    r, c = w.shape
    tm = 64 if r % 64 == 0 else r
    return _rowwise(name, _adamw, [w, g, m, v], [], [(c, F32)] * 3, tm)


_IN_COLS = 5906


def _perm_in(w):
    pad = jnp.zeros((w.shape[0], 2 * HALF - _IN_COLS), w.dtype)
    return (jnp.concatenate([w[:, 2310:4614], w[:, 4614:5382]], axis=1),
            jnp.concatenate([w[:, :2304], w[:, 5394:5906], w[:, 2304:2310], w[:, 5382:5394], pad], axis=1))


def _unperm_in(ga, gb):
    return jnp.concatenate([gb[:, :2304], gb[:, 2816:2822], ga[:, :2304], ga[:, 2304:3072], gb[:, 2822:2834],
                            gb[:, 2304:2816]], axis=1)


def _lanes(v, at):
    return jnp.pad(v, ((0, 0), (at, HD - at - v.shape[1])))


_PACK = ("norm_mix", "mem_norm", "norm_ffn", "gdn_conv", "fox_q_norm", "fox_k_norm", "gdn_out_norm", "mem_q_norm",
         "mem_k_norm", "fox_f_bias", "gdn_a_log", "gdn_dt_bias", "loss")


def _pack(vals):
    parts = [vals[n].reshape(-1, HD) for n in _PACK]
    used = sum(p.shape[0] for p in parts)
    buf = jnp.concatenate(parts + [jnp.zeros((-used % 8, HD), F32)], axis=0)
    return buf, [(n, p.shape[0]) for n, p in zip(_PACK, parts)]


def _unpack(buf, layout):
    out, at = {}, 0
    for n, rows in layout:
        out[n] = buf[at:at + rows]
        at += rows
    return out


def kernel(x, mem, norm_mix, w_in, fox_f_bias, fox_q_norm, fox_k_norm, gdn_conv, gdn_a_log, gdn_dt_bias, gdn_out_norm, mem_norm, w_mem_kv, mem_q_norm, mem_k_norm, w_out, norm_ffn, w_gate_up, w_down, loss_target, m_norm_mix, m_w_in, m_fox_f_bias, m_fox_q_norm, m_fox_k_norm, m_gdn_conv, m_gdn_a_log, m_gdn_dt_bias, m_gdn_out_norm, m_mem_norm, m_w_mem_kv, m_mem_q_norm, m_mem_k_norm, m_w_out, m_norm_ffn, m_w_gate_up, m_w_down, v_norm_mix, v_w_in, v_fox_f_bias, v_fox_q_norm, v_fox_k_norm, v_gdn_conv, v_gdn_a_log, v_gdn_dt_bias, v_gdn_out_norm, v_mem_norm, v_w_mem_kv, v_mem_q_norm, v_mem_k_norm, v_w_out, v_norm_ffn, v_w_gate_up, v_w_down):
    args = dict(locals())
    d = x.shape[2]
    me = 4 * lax.axis_index("x") + 2 * lax.axis_index("y") + lax.axis_index("c")

    cshard = gdn_conv[0].shape[1]
    conv_pad = jnp.pad(gdn_conv[0], ((0, 4), (0, 3 * HD - cshard)))
    w_in_a, w_in_b = _perm_in(w_in[0])
    wmv = lambda n: (args[n][0], args["m_" + n][0], args["v_" + n][0])
    comm = _StepComm(w_in_b.astype(BF16), {"in_a": [w_in_a.astype(BF16), conv_pad],
                                           "kv_out": [w_mem_kv[0].astype(BF16), w_out[0].astype(BF16)]},
                     w_gate_up[0].astype(BF16), w_down[0].astype(BF16), (),
                     {"ffn": [wmv("w_down"), wmv("w_gate_up")], "a": [None, wmv("w_out"), wmv("w_mem_kv")], "b": [None]})

    grad_x, loss_local, small_grads = _local_step(
        x[0], mem[0], loss_target[0], norm_mix, fox_f_bias, fox_q_norm, fox_k_norm, gdn_a_log, gdn_dt_bias,
        gdn_out_norm, mem_norm, mem_q_norm, mem_k_norm, norm_ffn, cshard, comm)

    red = comm.finish([grad_x])
    updated = {"w_down": red["ffn"][0], "w_gate_up": red["ffn"][1], "w_out": red["a"][1], "w_mem_kv": red["a"][2]}
    grads = {n: r[0] for n, r in updated.items()}
    grads["w_in"] = _unperm_in(red["a"][0], red["b"][0])
    small_grads["loss"] = jnp.broadcast_to(loss_local, (1, HD))
    packed, layout = _pack(small_grads)
    small = _unpack(_all_reduce_small("ar_small", packed), layout)
    loss = small["loss"][0, 0]
    six = {"fox_f_bias": L_FF, "gdn_a_log": L_GA, "gdn_dt_bias": L_GA}
    for n, rows_n in layout[:-1]:
        gsm = small[n]
        if n == "gdn_conv":
            gsm = lax.dynamic_slice(gsm.reshape(4, N_DEV * cshard), (0, me * cshard), (4, cshard))[None]
        elif n in six:
            gsm = gsm[:, six[n]:six[n] + 6]
        else:
            gsm = gsm.reshape(1, rows_n * HD)
        grads[n] = gsm

    names = ['norm_mix', 'w_in', 'fox_f_bias', 'fox_q_norm', 'fox_k_norm', 'gdn_conv', 'gdn_a_log', 'gdn_dt_bias',
             'gdn_out_norm', 'mem_norm', 'w_mem_kv', 'mem_q_norm', 'mem_k_norm', 'w_out', 'norm_ffn', 'w_gate_up', 'w_down']
    big = ("w_in", "w_mem_kv", "w_out", "w_gate_up", "w_down")
    delta, new_m, new_v = {}, {}, {}
    for n in big:
        res = updated[n][1:] if n in updated else _adamw_call("adamw_" + n, args[n][0], grads[n], *wmv(n)[1:])
        delta[n], new_m[n], new_v[n] = [a[None] for a in res]
        grads[n] = grads[n][None]

    def flat(a):
        a = a.reshape(1, -1)
        return jnp.pad(a, ((0, 0), (0, -a.shape[1] % HD))).reshape(-1, HD)

    smalls = [n for n in names if n not in big]
    pk = lambda pre: jnp.concatenate([flat(grads[n] if pre == "g" else args[pre + n]) for n in smalls], axis=0)
    cat = [pk(""), pk("g"), pk("m_"), pk("v_")]
    padr = -cat[0].shape[0] % 8
    cat = [jnp.pad(a, ((0, padr), (0, 0))) for a in cat]
    res = _adamw_call("adamw_small", *cat)
    at = 0
    for n in smalls:
        shape = args[n].shape
        size = math.prod(shape)
        nrow = -(-size // HD)
        for dst, src in zip((delta, new_m, new_v), res):
            dst[n] = src[at:at + nrow].reshape(-1)[:size].reshape(shape)
        at += nrow

    return (loss, grad_x[None], *[grads[n] for n in names], *[delta[n] for n in names],
            *[new_m[n] for n in names], *[new_v[n] for n in names])


class _StepComm:
    def __init__(self, first, shard_groups, w_gate_up, w_down, after, wmv):
        self.wmv, self.done = wmv, {}
        self.first = _RelayGather("ag_first", [first], after)
        self.groups, self.shards = {}, []
        for key, ws in shard_groups.items():
            self.groups[key] = list(range(len(self.shards), len(self.shards) + len(ws)))
            self.shards += list(ws)
        self.w_gate_up, self.w_down = w_gate_up, w_down
        self.passed, self.scatters = set(), {}

    def start_deps(self):
        return [self.first.token]

    def first_weights(self, after):
        deps = self.first.forward(after)
        self.gather = _Gather("ag", self.shards, deps)
        self.relay = _RelayGather("ag_gu", [self.w_gate_up], [self.gather.token])
        return self.first.get(self.first.pass_on([self.relay.token]))

    def relay_forward(self, after):
        deps = self.relay.forward(after)
        self.gather_down = _Gather("ag_dn", [self.w_down], deps)
        return [self.gather_down.token]

    def pass_on(self, key, after):
        self.passed.add(key)
        if key == "gate_up":
            return self.relay.pass_on(after)
        return self.gather.pass_on(self.groups[key], after)

    def weights(self, key, after):
        if key == "down":
            return self.gather_down.get([0], self.gather_down.pass_on([0], after))
        if key not in self.passed:
            after = self.pass_on(key, after)
        return self.relay.get(after) if key == "gate_up" else self.gather.get(self.groups[key], after)

    def send(self, tag, grads):
        blocks = [g if g.ndim == 3 else g.reshape(N_DEV, g.shape[0] // N_DEV, g.shape[1]) for g in grads]
        self.scatters[tag] = _Scatter("rs_" + tag, blocks, ())
        return [self.scatters[tag].token]

    def mid(self, tag, after):
        self.scatters[tag].mid(after)
        return [self.scatters[tag].token]

    def finish_group(self, tag, after):
        self.done[tag] = self.scatters.pop(tag).end(after, self.wmv[tag])
        first = self.done[tag][0]
        return [first[0] if isinstance(first, tuple) else first]

    def finish(self, after):
        for tag in list(self.scatters):
            self.finish_group(tag, after)
        return self.done


def _local_step(xs, ms, tgt, norm_mix, fox_f_bias, fox_q_norm, fox_k_norm, gdn_a_log, gdn_dt_bias, gdn_out_norm,
                mem_norm, mem_q_norm, mem_k_norm, norm_ffn, cshard, comm):
    t, d = xs.shape
    bq = min(t, 256)
    fb, alog, dtb = _lanes(fox_f_bias, L_FF), _lanes(gdn_a_log, L_GA), _lanes(gdn_dt_bias, L_GA)
    flat = lambda w: w.reshape(-1, w.shape[-1])

    rms1 = lambda a, g: (_rms(a, g),)
    (u,) = _rowwise("norm_mix", rms1, [xs], [norm_mix], [(d, BF16)], min(t, 256), deps=comm.start_deps())
    w_in_b = flat(comm.first_weights([u])[0])
    pb = _matmul("proj_in_b", u, w_in_b, NN, F32, 1024, 768)
    o_fox = _fox_fwd(pb, fb, fox_q_norm, fox_k_norm, bq)
    w_in_a, conv_parts = comm.weights("in_a", [o_fox])
    w_in_a = flat(w_in_a)
    conv_all = conv_parts[:, :4, :cshard].transpose(1, 0, 2).reshape(4, N_DEV * cshard)
    pa = _matmul("proj_in_a", u, w_in_a, NN, F32, 1024, 768)
    smrow = (pb, HD, SM)
    (gates,) = _rowwise("gdn_gates", _gdn_gates, [smrow], [alog, dtb], [(HD, F32)], min(t, 256))
    gdn_terms, gdn_qkv = _gdn_fwd(pa, gates, conv_all)
    o_gdn_raw, gdn_states = _gdn_scan(gdn_terms)
    gdn_saved = list(gdn_terms) + [gdn_states]
    deps = comm.relay_forward([o_gdn_raw])
    zrow = (pa, NG * HD, GZ * HD // (NG * HD))
    (o_gdn,) = _rowwise("gdn_post", _gdn_post, [o_gdn_raw, zrow], [gdn_out_norm], [(NG * HD, BF16)], min(t, 256),
                        deps=deps)
    w_kv_all, w_out_all = [flat(w) for w in comm.weights("kv_out", [o_gdn])]
    (mem_n,) = _rowwise("norm_mem", rms1, [ms], [mem_norm], [(d, BF16)], ms.shape[0])
    mkv = _matmul("proj_mem", mem_n, w_kv_all, NN, F32, 256, 512)
    o_mem = _mem_fwd(pb, mkv, mem_q_norm, mem_k_norm)
    deps = comm.pass_on("gate_up", [o_mem])
    mix = jnp.concatenate([o_fox, o_gdn, o_mem], axis=1)
    h1, h1n = _proj_out_norm(mix, w_out_all, xs, norm_ffn, deps)
    (wgu,) = comm.weights("gate_up", [h1n])
    ffw = wgu.shape[2]
    gu, act = _ffn_up(h1n, wgu.reshape(2, 4, d, ffw))
    w_down_all = flat(comm.weights("down", [act])[0])
    dyb, lsum = _ffn_down_loss(act, w_down_all, h1, tgt)
    loss_local = (0.5 / d) * jnp.sum(lsum[::8, ::HD])

    dgu = _ffn_down_bwd(dyb, w_down_all.reshape(4, ffw, d), gu).reshape(8, t, ffw)
    g_w_down = _matmul("grad_w_down", act, dyb, TN, BF16, 512, 2048)
    g_w_gu = _ffn_up_bwd_w(h1n, dgu)
    deps = comm.send("ffn", [g_w_down, g_w_gu])
    rms2 = lambda a, g: (_rms(a, g), a)
    dh1b, g_norm_ffn = _ffn_up_bwd_x(dgu, wgu, h1, norm_ffn, dyb, deps)

    dmix = _matmul("proj_out_bwd_x", dh1b, w_out_all, NT, BF16, 1024, 1024)
    g_w_out = _matmul("grad_w_out", mix, dh1b, TN, BF16, 1024, 2048)
    deps = comm.mid("ffn", [dmix, g_w_out])
    dmq, dmk, dmv, g_mqn, g_mkn = _mem_bwd(pb, mkv, mem_q_norm, mem_k_norm, dmix, deps=deps)
    dmkv = jnp.concatenate([dmk, dmv], axis=1).astype(BF16)
    g_w_kv = _matmul("grad_w_kv", mem_n, dmkv, TN, BF16, 512, 512)
    do_raw, dgz, g_gon = _rowwise_vjp("gdn_post_bwd", _gdn_post, [o_gdn_raw, zrow], [gdn_out_norm],
                                      [(dmix, NG * HD, 1)], [F32, BF16], min(t, 256), deps=deps)
    dterms = _gdn_bwd_scan(gdn_saved, do_raw)
    dgq, dgk, dgv, dgates, dwq, dwk, dwv = _gdn_bwd(pa, gates, conv_all, dterms, gdn_qkv)
    dsm_gdn, g_alog, g_dtb = _rowwise_vjp("gdn_gates_bwd", _gdn_gates, [smrow], [alog, dtb], [dgates], [F32], min(t, 256))
    dp_a = jnp.concatenate([dgq, dgk, dgv, dgz], axis=1)
    g_w_in_a = _matmul("grad_w_in_a", u, dp_a, TN, BF16, 512, 3072)
    deps = comm.send("a", [g_w_in_a, g_w_out, g_w_kv])
    du_a = _matmul("proj_in_bwd_a", dp_a, w_in_a, NT, F32, 1024, 1024, deps=deps)
    deps = comm.mid("a", [du_a])
    dfq, dfk, dfv, dsm_fox, g_fb, g_fqn, g_fkn = _fox_bwd(pb, fb, fox_q_norm, fox_k_norm, dmix, 2 * bq if t % (2 * bq) == 0 else bq,
                                                          deps=deps)
    dp_b = jnp.concatenate([dfq, dfk, dfv, dmq, (dsm_fox + dsm_gdn).astype(BF16), jnp.zeros((t, HD), BF16)], axis=1)
    g_w_in_b = _matmul("grad_w_in_b", u, dp_b, TN, BF16, 512, 3072)
    deps = comm.mid("b", comm.finish_group("ffn", comm.send("b", [g_w_in_b])))
    deps = comm.finish_group("a", deps)
    dmem_n = _matmul("proj_mem_bwd_x", dmkv, w_kv_all, NT, F32, 256, 512, deps=deps)
    g_mem_norm = _rowwise_vjp("norm_mem_bwd", rms1, [ms], [mem_norm], [dmem_n], [], ms.shape[0])[0]
    grad_x, g_norm_mix = _proj_in_bwd_norm(dp_b, w_in_b, du_a, xs, norm_mix, dh1b, [g_mem_norm])

    small_grads = {
        "norm_mix": g_norm_mix, "mem_norm": g_mem_norm, "norm_ffn": g_norm_ffn,
        "gdn_conv": jnp.concatenate([dwq, dwk, dwv], axis=1),
        "fox_q_norm": g_fqn, "fox_k_norm": g_fkn, "gdn_out_norm": g_gon, "mem_q_norm": g_mqn, "mem_k_norm": g_mkn,
        "fox_f_bias": g_fb, "gdn_a_log": g_alog, "gdn_dt_bias": g_dtb}
    return grad_x, loss_local, small_grads
```

```python
import functools
import math

import jax
import jax.numpy as jnp
from jax import lax
from jax.experimental import pallas as pl
from jax.experimental.pallas import tpu as pltpu

F32 = jnp.float32
BF16 = jnp.bfloat16
SDS = jax.ShapeDtypeStruct

N_DEV = 8
HD = 128
NF, NG, NM = 6, 6, 4
CHUNK = 64
GROUP = 32
NORM_EPS = 1e-6
GQ, GK, GV, GZ = 0, 6, 12, 18
FQ, FK, FV, MQ, SM = 0, 6, 12, 18, 22
HALF = 24 * HD
L_FF, L_GA, L_GB = 0, 6, 12
VMEM_LIMIT = 56 * 1024 * 1024

ADAM_LR, ADAM_B1, ADAM_B2, ADAM_EPS, ADAM_WD, ADAM_STEP = 0.001, 0.9, 0.999, 1e-08, 0.01, 10

NN = (((1,), (0,)), ((), ()))
NT = (((1,), (1,)), ((), ()))
TN = (((0,), (0,)), ((), ()))
MESH = pl.DeviceIdType.MESH


def _cp(*sem):
    return pltpu.CompilerParams(dimension_semantics=tuple(sem) if sem else None, vmem_limit_bytes=VMEM_LIMIT)


def _dot(a, b, dims=NN):
    return lax.dot_general(a, b, dims, preferred_element_type=F32)


def _iota(shape, axis):
    return lax.broadcasted_iota(jnp.int32, shape, axis)


def _rms(x, gain):
    return x * lax.rsqrt(jnp.mean(x * x, axis=-1, keepdims=True) + NORM_EPS) * gain


def _sigmoid(x):
    return 0.5 * jnp.tanh(0.5 * x) + 0.5


def _silu(x):
    return x * _sigmoid(x)


def _softplus(x):
    return jnp.maximum(x, 0.0) + jnp.log(1.0 + jnp.exp(-jnp.abs(x)))


def _lane_pick(x, lane):
    oh = (_iota((1, x.shape[-1]), 1) == lane).astype(F32)
    return jnp.sum(x * oh, axis=-1, keepdims=True)


def _cumsum_rows(x):
    tril = (_iota((HD, HD), 0) >= _iota((HD, HD), 1)).astype(F32)
    carry = jnp.zeros((1, x.shape[1]), F32)
    outs = []
    for b in range(x.shape[0] // HD):
        blk = x[b * HD:(b + 1) * HD]
        outs.append(_pdot(tril, blk, "nn", "xa") + carry)
        carry = carry + jnp.sum(blk, axis=0, keepdims=True)
    return jnp.concatenate(outs, axis=0)


def _row_spec(r, tm):
    if isinstance(r, tuple):
        arr, width, cb = r
        return arr, pl.BlockSpec((tm, width), lambda i, cb=cb: (i, cb))
    return r, pl.BlockSpec((tm, r.shape[1]), lambda i: (i, 0))


ANY_SPEC = pl.BlockSpec(memory_space=pl.ANY)


def _rowwise(name, fn, rows, consts, outs, tm, deps=()):
    arrs, specs = zip(*[_row_spec(r, tm) for r in rows])
    n_rows = arrs[0].shape[0]
    nr, nc, nd = len(rows), len(consts), len(deps)

    def body(*refs):
        res = fn(*[r[...] for r in refs[:nr + nc]])
        for o, v in zip(refs[nr + nc + nd:], res):
            o[...] = v.astype(o.dtype)

    return pl.pallas_call(
        body, grid=(n_rows // tm,), name=name,
        in_specs=list(specs) + [pl.BlockSpec(c.shape, lambda i: (0, 0)) for c in consts] + [ANY_SPEC] * nd,
        out_specs=[pl.BlockSpec((tm, w), lambda i: (i, 0)) for w, _ in outs],
        out_shape=[SDS((n_rows, w), dt) for w, dt in outs],
        compiler_params=_cp("parallel"),
    )(*arrs, *consts, *deps)


def _rowwise_vjp(name, fn, rows, consts, cts, grad_dtypes, tm, deps=()):
    arrs, specs = zip(*[_row_spec(r, tm) for r in rows])
    ct_arrs, ct_specs = zip(*[_row_spec(r, tm) for r in cts])
    n_rows = arrs[0].shape[0]
    nr, nc, nct, nd = len(rows), len(consts), len(cts), len(deps)
    plan = [(j, dt) for j, dts in enumerate(grad_dtypes) for dt in (dts if isinstance(dts, tuple) else (dts,))]
    ng = len(plan)
    widths = [specs[j].block_shape[1] for j, _ in plan]
    grad_dtypes = [dt for _, dt in plan]

    def body(*refs):
        vals = [r[...].astype(F32) for r in refs[:nr + nc]]
        ctv = tuple(r[...].astype(F32) for r in refs[nr + nc:nr + nc + nct])
        _, vjp = jax.vjp(fn, *vals)
        grads = vjp(ctv)
        outs = refs[nr + nc + nct + nd:]
        for o, (j, _) in zip(outs[:ng], plan):
            o[...] = grads[j].astype(o.dtype)

        @pl.when(pl.program_id(0) == 0)
        def _():
            for o in outs[ng:]:
                o[...] = jnp.zeros_like(o)

        for o, g in zip(outs[ng:], grads[nr:]):
            o[...] += g

    return pl.pallas_call(
        body, grid=(n_rows // tm,), name=name,
        in_specs=list(specs) + [pl.BlockSpec(c.shape, lambda i: (0, 0)) for c in consts] + list(ct_specs)
        + [ANY_SPEC] * nd,
        out_specs=[pl.BlockSpec((tm, w), lambda i: (i, 0)) for w in widths]
        + [pl.BlockSpec(c.shape, lambda i: (0, 0)) for c in consts],
        out_shape=[SDS((n_rows, w), dt) for w, dt in zip(widths, grad_dtypes)] + [SDS(c.shape, F32) for c in consts],
        compiler_params=_cp("arbitrary"),
    )(*arrs, *consts, *ct_arrs, *deps)


def _tile(n, pref):
    t = min(n, pref)
    while n % t or (t % HD and t != n):
        t -= 1
    return t


def _matmul(name, a, b, dims, out_dtype, tm, tn, residual=None, deps=()):
    ta, tb = dims == TN, dims == NT
    m = a.shape[1] if ta else a.shape[0]
    k = a.shape[0] if ta else a.shape[1]
    n = b.shape[0] if tb else b.shape[1]
    tm, tn = _tile(m, tm), _tile(n, tn)

    def body(*refs):
        acc = _dot(refs[0][...], refs[1][...], dims)
        if residual is not None:
            acc = acc + refs[2][...]
        refs[-1][...] = acc.astype(out_dtype)

    in_specs = [pl.BlockSpec((k, tm), lambda i, j: (0, i)) if ta else pl.BlockSpec((tm, k), lambda i, j: (i, 0)),
                pl.BlockSpec((tn, k), lambda i, j: (j, 0)) if tb else pl.BlockSpec((k, tn), lambda i, j: (0, j))]
    ops = [a, b]
    if residual is not None:
        in_specs.append(pl.BlockSpec((tm, tn), lambda i, j: (i, j)))
        ops.append(residual)
    in_specs += [ANY_SPEC] * len(deps)
    ops += list(deps)
    return pl.pallas_call(
        body, grid=(m // tm, n // tn), name=name, in_specs=in_specs,
        out_specs=pl.BlockSpec((tm, tn), lambda i, j: (i, j)), out_shape=SDS((m, n), out_dtype),
        compiler_params=_cp("parallel", "parallel"),
    )(*ops)


def _proj_out_norm(mix, w_out, xs, gain, deps):
    t, k = mix.shape
    d = w_out.shape[1]
    tm = _tile(t, 512)

    def body(*refs):
        a, b, x, g = refs[:4]
        h1, h1n = refs[4 + len(deps):]
        acc = _dot(a[...], b[...]) + x[...]
        h1[...] = acc
        h1n[...] = _rms(acc, g[...]).astype(BF16)

    return pl.pallas_call(
        body, grid=(t // tm,), name="proj_out",
        in_specs=[pl.BlockSpec((tm, k), lambda i: (i, 0)), pl.BlockSpec((k, d), lambda i: (0, 0)),
                  pl.BlockSpec((tm, d), lambda i: (i, 0)), pl.BlockSpec((1, d), lambda i: (0, 0))] + [ANY_SPEC] * len(deps),
        out_specs=[pl.BlockSpec((tm, d), lambda i: (i, 0))] * 2, out_shape=[SDS((t, d), F32), SDS((t, d), BF16)],
        compiler_params=_cp("parallel"),
    )(mix, w_out, xs, gain, *deps)


def _proj_in_bwd_norm(dp, w, du_a, xs, gain, dh1b, deps):
    t, k = dp.shape
    d = w.shape[0]
    tm = _tile(t, 256)

    def body(*refs):
        a, b, ua, x, g, dh = refs[:6]
        gx, dgain = refs[6 + len(deps):]
        _, vjp = jax.vjp(lambda xx, gn: _rms(xx, gn), x[...], g[...])
        dx, dg = vjp(_dot(a[...], b[...], NT) + ua[...])
        gx[...] = dx + dh[...].astype(F32)

        @pl.when(pl.program_id(0) == 0)
        def _():
            dgain[...] = jnp.zeros_like(dgain)

        dgain[...] += dg

    row = pl.BlockSpec((tm, d), lambda i: (i, 0))
    vec = pl.BlockSpec((1, d), lambda i: (0, 0))
    return pl.pallas_call(
        body, grid=(t // tm,), name="proj_in_bwd_b",
        in_specs=[pl.BlockSpec((tm, k), lambda i: (i, 0)), pl.BlockSpec((d, k), lambda i: (0, 0), pipeline_mode=ONE_BUFFER),
                  row, row, vec, row] + [ANY_SPEC] * len(deps),
        out_specs=[row, vec], out_shape=[SDS((t, d), F32), SDS((1, d), F32)], compiler_params=_cp("arbitrary"),
    )(dp, w, du_a, xs, gain, dh1b, *deps)


def _ffn_up(h1n, wgu):
    t, d = h1n.shape
    w = wgu.shape[3]
    tm = _tile(t, 512)

    def body(a, b, gu, act):
        x = a[...]
        g = _dot(x, b[0])
        u = _dot(x, b[1])
        gu[0] = g.astype(BF16)
        gu[1] = u.astype(BF16)
        act[...] = (_silu(g) * u).astype(BF16)

    return pl.pallas_call(
        body, grid=(4, t // tm), name="ffn_up",
        in_specs=[pl.BlockSpec((tm, d), lambda j, i: (i, 0)), pl.BlockSpec((2, None, d, w), lambda j, i: (0, j, 0, 0))],
        out_specs=[pl.BlockSpec((2, None, tm, w), lambda j, i: (0, j, i, 0)), pl.BlockSpec((tm, w), lambda j, i: (i, j))],
        out_shape=[SDS((2, 4, t, w), BF16), SDS((t, 4 * w), BF16)],
        compiler_params=_cp("parallel", "parallel"),
    )(h1n, wgu)


def _ffn_down_loss(act, wdown, h1, target):
    t, f = act.shape
    d = wdown.shape[1]
    tm, tn = _tile(t, 1024), _tile(d, 512)

    def body(a, b, h, tg, dyb, ls):
        e = _dot(a[...], b[...]) + h[...] - tg[...]
        dyb[...] = (e * (1.0 / d)).astype(BF16)
        ls[...] = jnp.broadcast_to(jnp.sum(e * e), (8, HD))

    return pl.pallas_call(
        body, grid=(t // tm, d // tn), name="ffn_down_loss",
        in_specs=[pl.BlockSpec((tm, f), lambda i, j: (i, 0)), pl.BlockSpec((f, tn), lambda i, j: (0, j)),
                  pl.BlockSpec((tm, tn), lambda i, j: (i, j)), pl.BlockSpec((tm, tn), lambda i, j: (i, j))],
        out_specs=[pl.BlockSpec((tm, tn), lambda i, j: (i, j)), pl.BlockSpec((8, HD), lambda i, j: (i, j))],
        out_shape=[SDS((t, d), BF16), SDS((8 * (t // tm), HD * (d // tn)), F32)],
        compiler_params=_cp("parallel", "parallel"),
    )(act, wdown, h1, target)


def _ffn_down_bwd(dyb, wdown4, gu):
    t, d = dyb.shape
    w = wdown4.shape[1]
    tm = _tile(t, 512)

    def body(a, b, gu_ref, out):
        da = _dot(a[...], b[...], NT)
        g = gu_ref[0].astype(F32)
        u = gu_ref[1].astype(F32)
        s = _sigmoid(g)
        out[0] = (da * u * (s * (1.0 + g * (1.0 - s)))).astype(BF16)
        out[1] = (da * g * s).astype(BF16)

    return pl.pallas_call(
        body, grid=(4, t // tm), name="ffn_down_bwd",
        in_specs=[pl.BlockSpec((tm, d), lambda j, i: (i, 0)), pl.BlockSpec((None, w, d), lambda j, i: (j, 0, 0)),
                  pl.BlockSpec((2, None, tm, w), lambda j, i: (0, j, i, 0))],
        out_specs=pl.BlockSpec((2, None, tm, w), lambda j, i: (0, j, i, 0)),
        out_shape=SDS((2, 4, t, w), BF16),
        compiler_params=_cp("parallel", "parallel"),
    )(dyb, wdown4, gu)


def _ffn_up_bwd_x(dgu, wgu, h1, gain, dyb, deps):
    _, t, w = dgu.shape
    d = wgu.shape[1]
    tm = _tile(t, 512)

    def body(*refs):
        a, b, h, g, dy = refs[:5]
        dh1, dgain, acc = refs[5 + len(deps):]
        i, j = pl.program_id(0), pl.program_id(1)

        @pl.when(j == 0)
        def _():
            acc[...] = jnp.zeros_like(acc)

        acc[...] += _dot(a[...], b[...], NT)

        @pl.when(j == N_DEV - 1)
        def _():
            _, vjp = jax.vjp(lambda x, gn: _rms(x, gn), h[...], g[...])
            dx, dg = vjp(acc[...])
            dh1[...] = (dx + dy[...].astype(F32)).astype(dh1.dtype)

            @pl.when(i == 0)
            def _():
                dgain[...] = jnp.zeros_like(dgain)

            dgain[...] += dg

    row = pl.BlockSpec((tm, d), lambda i, j: (i, 0))
    return pl.pallas_call(
        body, grid=(t // tm, N_DEV), name="ffn_up_bwd_x",
        in_specs=[pl.BlockSpec((None, tm, w), lambda i, j: (j, i, 0)), pl.BlockSpec((None, d, w), lambda i, j: (j, 0, 0)),
                  row, pl.BlockSpec((1, d), lambda i, j: (0, 0)), row] + [ANY_SPEC] * len(deps),
        out_specs=[row, pl.BlockSpec((1, d), lambda i, j: (0, 0))],
        out_shape=[SDS((t, d), BF16), SDS((1, d), F32)], scratch_shapes=[pltpu.VMEM((tm, d), F32)],
        compiler_params=_cp("arbitrary", "arbitrary"),
    )(dgu, wgu, h1, gain, dyb, *deps)


def _ffn_up_bwd_w(h1n, dgu):
    _, t, w = dgu.shape
    d = h1n.shape[1]
    tm = _tile(d, 512)

    def body(a, b, out):
        out[...] = _dot(a[...], b[...], TN).astype(BF16)

    return pl.pallas_call(
        body, grid=(8, d // tm), name="ffn_up_bwd_w",
        in_specs=[pl.BlockSpec((t, tm), lambda j, i: (0, i)), pl.BlockSpec((None, t, w), lambda j, i: (j, 0, 0))],
        out_specs=pl.BlockSpec((None, tm, w), lambda j, i: (j, i, 0)), out_shape=SDS((8, d, w), BF16),
        compiler_params=_cp("parallel", "parallel"),
    )(h1n, dgu)


def _fox_prep(fq, fk, sm, fb, qg, kg, h):
    qn = _rms(fq, qg)
    kn = _rms(fk, kg)
    c = _cumsum_rows(-_softplus(-(sm + fb)))
    ccol = _lane_pick(c, L_FF + h)
    crow = jnp.sum(c.T * (_iota((HD, 1), 0) == L_FF + h).astype(F32), axis=0, keepdims=True)
    return qn, kn, ccol, crow


def _softmax_times(s, v):
    e = jnp.exp(s - lax.stop_gradient(jnp.max(s, axis=1, keepdims=True)))
    return _dot(e.astype(BF16), v.astype(BF16)) * (1.0 / jnp.sum(e, axis=1, keepdims=True))


def _fox_block(q, k, v, cc, cr, off):
    bq = q.shape[0]
    assert k.shape[0] == off + bq
    s = _dot((q * (HD ** -0.5)).astype(BF16), k.astype(BF16), NT) + cc - cr
    diag = jnp.where(_iota((bq, bq), 1) <= _iota((bq, bq), 0), s[:, off:], -1e30)
    s = jnp.concatenate([s[:, :off], diag], axis=1) if off else diag
    return _softmax_times(s, v)


ONE_BUFFER = pl.Buffered(1)


def _pcol(t, cb):
    return pl.BlockSpec((t, HD), lambda h, cb=cb: (0, cb + h), pipeline_mode=ONE_BUFFER)


def _smcol(t):
    return pl.BlockSpec((t, HD), lambda h: (0, SM), pipeline_mode=ONE_BUFFER)


def _head(t):
    return pl.BlockSpec((t, HD), lambda h: (0, h), pipeline_mode=ONE_BUFFER)


def _small(n):
    return pl.BlockSpec((n, HD), lambda h: (0, 0), pipeline_mode=ONE_BUFFER)


def _fox_fwd(p, fb, qg, kg, bq):
    t = p.shape[0]

    def body(fq, fk, fv, sm, fb_r, qg_r, kg_r, o, qn_s, cc_s):
        h = pl.program_id(0)
        qn, kn, ccol, crow = _fox_prep(fq[...], fk[...], sm[...], fb_r[...], qg_r[...], kg_r[...], h)
        qn_s[...] = qn
        cc_s[...] = ccol
        knb = kn.astype(BF16)
        vb = fv[...].astype(BF16)
        for i in range(t // bq):
            rows, ext = pl.ds(i * bq, bq), (i + 1) * bq
            o[rows, :] = _fox_block(qn_s[rows, :], knb[:ext], vb[:ext], cc_s[rows, :], crow[:, :ext], i * bq).astype(o.dtype)

    return pl.pallas_call(
        body, grid=(NF,), name="fox_fwd",
        in_specs=[_pcol(t, FQ), _pcol(t, FK), _pcol(t, FV), _smcol(t), _small(1), _small(1), _small(1)],
        out_specs=_head(t), out_shape=SDS((t, NF * HD), BF16),
        scratch_shapes=[pltpu.VMEM((t, HD), F32), pltpu.VMEM((t, 1), F32)],
        compiler_params=_cp("parallel"),
    )(p, p, p, p, fb, qg, kg)


def _fox_bwd(p, fb, qg, kg, dmix, bq, deps=()):
    t = p.shape[0]

    def body(*refs):
        fq, fk, fv, sm, fb_r, qg_r, kg_r, do = refs[:8]
        dfq, dfk, dfv, dsm, dfb, dqg, dkg, qn_s, cc_s, dqn_s, dcc_s, dkn_s, dv_s, dcr_s = refs[8 + len(deps):]
        h = pl.program_id(0)
        qn, kn, ccol, crow = _fox_prep(fq[...], fk[...], sm[...], fb_r[...], qg_r[...], kg_r[...], h)
        qn_s[...] = qn
        cc_s[...] = ccol
        v = fv[...]
        dkn_s[...] = jnp.zeros_like(dkn_s)
        dv_s[...] = jnp.zeros_like(dv_s)
        dcr_s[...] = jnp.zeros_like(dcr_s)

        for i in range(t // bq):
            rows, ext = pl.ds(i * bq, bq), (i + 1) * bq
            _, vjp = jax.vjp(lambda a, b, c, d, e, off=i * bq: _fox_block(a, b, c, d, e, off),
                             qn_s[rows, :], kn[:ext], v[:ext], cc_s[rows, :], crow[:, :ext])
            dq, dk, dv, dcc, dcr = vjp(do[rows, :].astype(F32))
            dqn_s[rows, :] = dq
            dcc_s[rows, :] = dcc
            dkn_s[:ext, :] += dk
            dv_s[:ext, :] += dv
            dcr_s[:, :ext] += dcr
        _, prep_vjp = jax.vjp(lambda a, b, c, d, e, f: _fox_prep(a, b, c, d, e, f, h),
                              fq[...], fk[...], sm[...], fb_r[...], qg_r[...], kg_r[...])
        g_fq, g_fk, g_sm, g_fb, g_qg, g_kg = prep_vjp((dqn_s[...], dkn_s[...], dcc_s[...], dcr_s[...]))
        dfq[...] = g_fq.astype(dfq.dtype)
        dfk[...] = g_fk.astype(dfk.dtype)
        dfv[...] = dv_s[...].astype(dfv.dtype)

        @pl.when(h == 0)
        def _():
            for r in (dsm, dfb, dqg, dkg):
                r[...] = jnp.zeros_like(r)

        dsm[...] += g_sm
        dfb[...] += g_fb
        dqg[...] += g_qg
        dkg[...] += g_kg

    head = _head(t)
    return pl.pallas_call(
        body, grid=(NF,), name="fox_bwd",
        in_specs=[_pcol(t, FQ), _pcol(t, FK), _pcol(t, FV), _smcol(t), _small(1), _small(1), _small(1), head]
        + [ANY_SPEC] * len(deps),
        out_specs=[head, head, head, _small(t), _small(1), _small(1), _small(1)],
        out_shape=[SDS((t, NF * HD), BF16)] * 3 + [SDS((t, HD), F32)] + [SDS((1, HD), F32)] * 3,
        scratch_shapes=[pltpu.VMEM((t, HD), F32), pltpu.VMEM((t, 1), F32), pltpu.VMEM((t, HD), F32),
                        pltpu.VMEM((t, 1), F32), pltpu.VMEM((t, HD), F32), pltpu.VMEM((t, HD), F32),
                        pltpu.VMEM((1, t), F32)],
        compiler_params=_cp("arbitrary"),
    )(p, p, p, p, fb, qg, kg, dmix, *deps)


def _mem_attn(mq, mk, mv, qg, kg):
    s = _dot((_rms(mq, qg) * (HD ** -0.5)).astype(BF16), _rms(mk, kg).astype(BF16), NT)
    return _softmax_times(s, mv)


def _mem_fwd(p, mkv, qg, kg):
    t, ml = p.shape[0], mkv.shape[0]

    def body(mq, mk, mv, qg_r, kg_r, o):
        o[...] = _mem_attn(mq[...], mk[...], mv[...], qg_r[...], kg_r[...]).astype(o.dtype)

    return pl.pallas_call(
        body, grid=(NM,), name="mem_fwd",
        in_specs=[_pcol(t, MQ), pl.BlockSpec((ml, HD), lambda h: (0, h)), pl.BlockSpec((ml, HD), lambda h: (0, NM + h)),
                  _small(1), _small(1)],
        out_specs=pl.BlockSpec((t, HD), lambda h: (0, h)), out_shape=SDS((t, NM * HD), BF16),
        compiler_params=_cp("parallel"),
    )(p, mkv, mkv, qg, kg)


def _mem_bwd(p, mkv, qg, kg, dmix, deps=()):
    t, ml = p.shape[0], mkv.shape[0]

    def body(*refs):
        mq, mk, mv, qg_r, kg_r, do = refs[:6]
        dmq, dmk, dmv, dqg, dkg = refs[6 + len(deps):]
        _, vjp = jax.vjp(_mem_attn, mq[...], mk[...], mv[...], qg_r[...], kg_r[...])
        g_q, g_k, g_v, g_qg, g_kg = vjp(do[...].astype(F32))
        dmq[...] = g_q.astype(dmq.dtype)
        dmk[...] = g_k
        dmv[...] = g_v

        @pl.when(pl.program_id(0) == 0)
        def _():
            dqg[...] = jnp.zeros_like(dqg)
            dkg[...] = jnp.zeros_like(dkg)

        dqg[...] += g_qg
        dkg[...] += g_kg

    return pl.pallas_call(
        body, grid=(NM,), name="mem_bwd",
        in_specs=[_pcol(t, MQ), pl.BlockSpec((ml, HD), lambda h: (0, h)), pl.BlockSpec((ml, HD), lambda h: (0, NM + h)),
                  _small(1), _small(1), pl.BlockSpec((t, HD), lambda h: (0, NF + NG + h))] + [ANY_SPEC] * len(deps),
        out_specs=[pl.BlockSpec((t, HD), lambda h: (0, h)), pl.BlockSpec((ml, HD), lambda h: (0, h)),
                   pl.BlockSpec((ml, HD), lambda h: (0, h)), _small(1), _small(1)],
        out_shape=[SDS((t, NM * HD), BF16), SDS((ml, NM * HD), F32), SDS((ml, NM * HD), F32),
                   SDS((1, HD), F32), SDS((1, HD), F32)],
        compiler_params=_cp("arbitrary"),
    )(p, mkv, mkv, qg, kg, dmix, *deps)


def _shift_down(x, s):
    if s == 0:
        return x
    return jnp.where(_iota(x.shape, 0) >= s, pltpu.roll(x, s, 0), 0.0)


def _shift_up(x, s):
    if s == 0:
        return x
    n = x.shape[0]
    return jnp.where(_iota(x.shape, 0) < n - s, pltpu.roll(x, n - s, 0), 0.0)


@jax.custom_vjp
def _conv4(x, w0, w1, w2, w3):
    return w0 * _shift_down(x, 3) + w1 * _shift_down(x, 2) + w2 * _shift_down(x, 1) + w3 * x


def _conv4_fwd(x, w0, w1, w2, w3):
    return _conv4(x, w0, w1, w2, w3), (x, w0, w1, w2, w3)


def _conv4_bwd(res, dy):
    x, w0, w1, w2, w3 = res
    ups = [_shift_up(dy, 3 - k) for k in range(4)]
    dx = w0 * ups[0] + w1 * ups[1] + w2 * ups[2] + w3 * ups[3]
    return (dx,) + tuple(jnp.sum(up * x, axis=0, keepdims=True) for up in ups)


_conv4.defvjp(_conv4_fwd, _conv4_bwd)


HALO = 8


def _gdn_gates(sm, alog, dtb):
    lane = _iota((1, HD), 1)
    g = -jnp.exp(alog) * _softplus(sm + dtb)
    return (jnp.where((lane >= L_GA) & (lane < L_GA + NG), g,
                      jnp.where((lane >= L_GB) & (lane < L_GB + NG), _sigmoid(sm), 0.0)),)


def _gdn_prep(gq, gk, gv, gates, taps, h):
    q, k, v = [_silu(_conv4(x, *taps[4 * j:4 * j + 4]))[HALO:] for j, x in enumerate((gq, gk, gv))]
    q = q * lax.rsqrt(jnp.sum(q * q, axis=-1, keepdims=True) + NORM_EPS) * (HD ** -0.5)
    k = k * lax.rsqrt(jnp.sum(k * k, axis=-1, keepdims=True) + NORM_EPS)
    return q, k, v, _lane_pick(gates, L_GA + h), _lane_pick(gates, L_GB + h)


def _split(x, n):
    parts, rest = [], x
    for i in range(n):
        parts.append(rest.astype(BF16))
        if i + 1 < n:
            rest = rest - parts[-1].astype(F32)
    return parts


def _raw_dot(a, b, form):
    lead = a.ndim - 2
    ca, cb = {"nn": (1, 0), "nt": (1, 1), "tn": (0, 0)}[form]
    batch = ((0,), (0,)) if lead else ((), ())
    return lax.dot_general(a, b, (((ca + lead,), (cb + lead,)), batch), preferred_element_type=F32)


def _pdot_impl(a, b, form, mode):
    if mode == "1":
        return _raw_dot(a.astype(BF16), b.astype(BF16), form)
    if mode == "3":
        (ah, al), (bh, bl) = _split(a, 2), _split(b, 2)
        return _raw_dot(ah, bh, form) + (_raw_dot(al, bh, form) + _raw_dot(ah, bl, form))
    if mode == "xa":
        return sum(_raw_dot(a.astype(BF16), t, form) for t in reversed(_split(b, 3)))
    return sum(_raw_dot(t, b.astype(BF16), form) for t in reversed(_split(a, 3)))


@functools.partial(jax.custom_vjp, nondiff_argnums=(2, 3))
def _pdot(a, b, form, mode):
    return _pdot_impl(a, b, form, mode)


def _pdot_fwd(a, b, form, mode):
    return _pdot_impl(a, b, form, mode), (a, b)


def _pdot_bwd(form, mode, res, ct):
    a, b = res
    da_args, db_args = {"nn": ((ct, b, "nt"), (a, ct, "tn")), "nt": ((ct, b, "nn"), (ct, a, "tn")),
                        "tn": ((b, ct, "nt"), (a, ct, "nn"))}[form]

    def side(args, exact):
        if mode in ("1", "3"):
            return mode
        return "xa" if args[0] is exact else "xb"

    if mode == "xa":
        return jnp.zeros_like(a), _pdot_impl(*db_args, side(db_args, a))
    if mode == "xb":
        return _pdot_impl(*da_args, side(da_args, b)), jnp.zeros_like(b)
    return _pdot_impl(*da_args, mode), _pdot_impl(*db_args, mode)


_pdot.defvjp(_pdot_fwd, _pdot_bwd)

GDN_QK, GDN_INV, GDN_SCAN = "1", "1", "1"


@jax.custom_vjp
def _tri_inv(low):
    eye = (_iota((CHUNK, CHUNK), 0) == _iota((CHUNK, CHUNK), 1)).astype(F32)
    inv = eye - low
    pw = low
    for _ in range(5):
        pw = _pdot_impl(pw, pw, "nn", GDN_INV)
        inv = inv + _pdot_impl(inv, pw, "nn", GDN_INV)
    return inv


def _tri_inv_fwd(low):
    inv = _tri_inv(low)
    return inv, inv


def _tri_inv_bwd(inv, ct):
    return (-_pdot_impl(_pdot_impl(inv, ct, "tn", GDN_INV), inv, "nt", GDN_INV),)


_tri_inv.defvjp(_tri_inv_fwd, _tri_inv_bwd)


def _gdn_intra(q, k, v, g, beta):
    n = q.shape[0]
    r, c = _iota((CHUNK, CHUNK), 0), _iota((CHUNK, CHUNK), 1)
    tril, strict = r >= c, r > c
    trilf = jnp.broadcast_to(tril.astype(F32), (n, CHUNK, CHUNK))
    gcm = _pdot(trilf, jnp.broadcast_to(g, (n, CHUNK, CHUNK)), "nn", "xa")
    gcf = _pdot(trilf, jnp.broadcast_to(g, (n, CHUNK, HD)), "nn", "xa")
    lane0 = (_iota((1, 1, CHUNK), 2) == 0).astype(F32)
    gcr = _pdot(jnp.ones((n, CHUNK, CHUNK), F32), gcm * lane0, "nt", "xa")
    decay = jnp.where(tril, jnp.exp(jnp.where(tril, gcm - gcr, 0.0)), 0.0)
    egc = jnp.exp(gcf)
    kb = k * beta
    low = jnp.where(strict, _pdot(kb, k, "nt", GDN_QK) * decay, 0.0)
    inv = _tri_inv(low)
    u = _pdot(inv, v * beta, "nn", GDN_INV)
    w = _pdot(inv, kb * egc, "nn", GDN_INV)
    at = jnp.where(tril, _pdot(q, k, "nt", GDN_QK) * decay, 0.0)
    gl = jnp.sum(jnp.broadcast_to(g, (n, CHUNK, HD)), axis=1, keepdims=True)
    kd = k * jnp.exp(gl - gcf)
    return (_pdot(kd, w, "tn", GDN_SCAN), _pdot(kd, u, "tn", GDN_SCAN), q * egc - _pdot(at, w, "nn", GDN_SCAN),
            _pdot(at, u, "nn", GDN_SCAN), gl)


def _gdn_step(s, kw, ku, a, b, gl):
    return _pdot(a, s, "nn", GDN_SCAN) + b, s * jnp.exp(gl) - _pdot(kw, s, "nn", GDN_SCAN) + ku


SCAN_HEADS = 3
SCAN_UNROLL = 8


def _gdn_chunked_scratch(nc):
    big = pltpu.VMEM((nc, CHUNK, HD), F32)
    return [big, big, big, pltpu.VMEM((nc, CHUNK, 1), F32), pltpu.VMEM((nc, CHUNK, 1), F32)]


N_TERMS = 5


def _gdn_term_shapes(nc):
    return [(nc, HD, HD), (nc, HD, HD), (nc, CHUNK, HD), (nc, CHUNK, HD), (nc, 1, HD)]


def _per_head(shape, heads=None, one_buffer=True):
    lead = (None,) if heads is None else (heads,)
    return pl.BlockSpec(lead + tuple(shape), lambda h: (h,) + (0,) * len(shape),
                        pipeline_mode=ONE_BUFFER if one_buffer else None)


def _gdn_in_specs(t):
    cw = lambda cb: pl.BlockSpec((4, HD), lambda h, cb=cb: (0, cb + h))
    return [_pcol(t, GQ), _pcol(t, GK), _pcol(t, GV), _small(t), cw(0), cw(NG), cw(2 * NG)]


def _taps(wq, wk, wv):
    return tuple(w[k:k + 1, :] for w in (wq, wk, wv) for k in range(4))


def _prep_rows(t):
    return min(t, 256)


def _gdn_pad(srcs, pads):
    for src, pad in zip(srcs, pads):
        pad[0:HALO, :] = jnp.zeros((HALO, HD), F32)
        pad[HALO:, :] = src[...]


def _gdn_stage(pads, gates, taps, h, chunked):
    t = gates.shape[0]
    rows = _prep_rows(t)
    per = rows // CHUNK

    def tile(i, carry):
        r0 = pl.multiple_of(i * rows, rows)
        vals = _gdn_prep(*[p[pl.ds(r0, rows + HALO), :] for p in pads], gates[pl.ds(r0, rows), :], taps, h)
        for v, r in zip(vals, chunked):
            r[pl.ds(i * per, per)] = v.reshape(per, CHUNK, v.shape[-1])
        return carry

    lax.fori_loop(0, t // rows, tile, 0)


def _gdn_intra_all(chunked, intra):
    nc = chunked[0].shape[0]
    grp_n = math.gcd(nc, GROUP)

    def grp(i, carry):
        sl = pl.ds(pl.multiple_of(i * grp_n, grp_n), grp_n)
        for r, val in zip(intra, _gdn_intra(*[c[sl] for c in chunked])):
            r[sl] = val
        return carry

    lax.fori_loop(0, nc // grp_n, grp, 0)


def _gdn_fwd(pa, gates, conv):
    t = pa.shape[0]
    nc = t // CHUNK
    terms = _gdn_term_shapes(nc)

    def body(gq, gk, gv, gt, wq, wk, wv, *rest):
        h = pl.program_id(0)
        intra, chunked, pads = rest[:N_TERMS], rest[N_TERMS:N_TERMS + 5], rest[N_TERMS + 5:]
        _gdn_pad((gq, gk, gv), pads)
        _gdn_stage(pads, gt, _taps(wq, wk, wv), h, chunked)
        _gdn_intra_all(chunked, intra)

    qkv = [(nc, CHUNK, HD)] * 3
    outs = pl.pallas_call(
        body, grid=(NG,), name="gdn_fwd", in_specs=_gdn_in_specs(t),
        out_specs=[_per_head(sh, one_buffer=False) for sh in terms + qkv],
        out_shape=[SDS((NG,) + sh, F32) for sh in terms + qkv],
        scratch_shapes=_gdn_chunked_scratch(nc)[3:] + [pltpu.VMEM((t + HALO, HD), F32)] * 3, compiler_params=_cp("parallel"),
    )(pa, pa, pa, gates, conv, conv, conv)
    return list(outs[:N_TERMS]), list(outs[N_TERMS:])


def _gdn_scan(terms_in):
    nc = terms_in[0].shape[1]
    terms = _gdn_term_shapes(nc)

    def body(*refs):
        intra, o, states = refs[:N_TERMS], refs[N_TERMS], refs[N_TERMS + 1]

        def one(c, ss):
            rows = pl.ds(pl.multiple_of(c * CHUNK, CHUNK), CHUNK)
            loaded = [[r[hh, c] for r in intra] for hh in range(SCAN_HEADS)]
            res = [_gdn_step(ss[hh], *loaded[hh]) for hh in range(SCAN_HEADS)]
            for hh in range(SCAN_HEADS):
                states[hh, c] = ss[hh]
                o[rows, hh * HD:(hh + 1) * HD] = res[hh][0]
            return tuple(r[1] for r in res)

        per_trip = math.gcd(nc, SCAN_UNROLL)

        def step(i, ss):
            for k in range(per_trip):
                ss = one(per_trip * i + k, ss)
            return ss

        lax.fori_loop(0, nc // per_trip, step, tuple(jnp.zeros((HD, HD), F32) for _ in range(SCAN_HEADS)))

    return pl.pallas_call(
        body, grid=(NG // SCAN_HEADS,), name="gdn_scan", in_specs=[_per_head(sh, SCAN_HEADS) for sh in terms],
        out_specs=[pl.BlockSpec((nc * CHUNK, SCAN_HEADS * HD), lambda h: (0, h), pipeline_mode=ONE_BUFFER),
                   _per_head((nc, HD, HD), SCAN_HEADS)],
        out_shape=[SDS((nc * CHUNK, NG * HD), F32), SDS((NG, nc, HD, HD), F32)], compiler_params=_cp("parallel"),
    )(*terms_in)


def _gdn_bwd_scan(saved, do_raw):
    nc = saved[0].shape[1]
    terms = _gdn_term_shapes(nc)

    def body(*refs):
        intra, states, do, outs = refs[:N_TERMS], refs[N_TERMS], refs[N_TERMS + 1], refs[N_TERMS + 2:]

        def one(c, dss):
            rows = pl.ds(pl.multiple_of(c * CHUNK, CHUNK), CHUNK)
            loaded = [[states[hh, c]] + [r[hh, c] for r in intra] for hh in range(SCAN_HEADS)]
            cts = [do[rows, hh * HD:(hh + 1) * HD] for hh in range(SCAN_HEADS)]
            grads = [jax.vjp(_gdn_step, *loaded[hh])[1]((cts[hh], dss[hh])) for hh in range(SCAN_HEADS)]
            for hh in range(SCAN_HEADS):
                for r, gval in zip(outs, grads[hh][1:]):
                    r[hh, c] = gval
            return tuple(g[0] for g in grads)

        per_trip = math.gcd(nc, SCAN_UNROLL)

        def bwd(i, dss):
            c = nc - 1 - per_trip * i
            for k in range(per_trip):
                dss = one(c - k, dss)
            return dss

        lax.fori_loop(0, nc // per_trip, bwd, tuple(jnp.zeros((HD, HD), F32) for _ in range(SCAN_HEADS)))

    return pl.pallas_call(
        body, grid=(NG // SCAN_HEADS,), name="gdn_bwd_scan",
        in_specs=[_per_head(sh, SCAN_HEADS) for sh in terms] + [_per_head((nc, HD, HD), SCAN_HEADS)]
        + [pl.BlockSpec((nc * CHUNK, SCAN_HEADS * HD), lambda h: (0, h), pipeline_mode=ONE_BUFFER)],
        out_specs=[_per_head(sh, SCAN_HEADS) for sh in terms],
        out_shape=[SDS((NG,) + sh, F32) for sh in terms], compiler_params=_cp("parallel"),
    )(*saved, do_raw)


def _gdn_bwd(pa, gates, conv, dterms, qkv):
    t = pa.shape[0]
    nc = t // CHUNK
    terms = _gdn_term_shapes(nc)

    def body(*refs):
        gq, gk, gv, gt, wq, wk, wv = refs[:7]
        dintra, qkv = refs[7:7 + N_TERMS], refs[7 + N_TERMS:10 + N_TERMS]
        dgq, dgk, dgv, dgt, dwq, dwk, dwv = refs[10 + N_TERMS:17 + N_TERMS]
        chunked, pads, dpads, dgt_s = (refs[17 + N_TERMS:22 + N_TERMS], refs[22 + N_TERMS:25 + N_TERMS],
                                       refs[25 + N_TERMS:28 + N_TERMS], refs[28 + N_TERMS])
        h = pl.program_id(0)
        taps = _taps(wq, wk, wv)
        _gdn_pad((gq, gk, gv), pads)
        rows = _prep_rows(t)
        per = rows // CHUNK

        def gates_tile(i, carry):
            gtile = gt[pl.ds(pl.multiple_of(i * rows, rows), rows), :]
            chunked[3][pl.ds(i * per, per)] = _lane_pick(gtile, L_GA + h).reshape(per, CHUNK, 1)
            chunked[4][pl.ds(i * per, per)] = _lane_pick(gtile, L_GB + h).reshape(per, CHUNK, 1)
            return carry

        lax.fori_loop(0, t // rows, gates_tile, 0)
        grp_n = math.gcd(nc, GROUP)

        def grp(i, carry):
            sl = pl.ds(pl.multiple_of(i * grp_n, grp_n), grp_n)
            _, vjp = jax.vjp(_gdn_intra, *[r[sl] for r in qkv], chunked[3][sl], chunked[4][sl])
            for r, gval in zip(chunked, vjp(tuple(r[sl] for r in dintra))):
                r[sl] = gval
            return carry

        lax.fori_loop(0, nc // grp_n, grp, 0)

        for r in dpads:
            r[...] = jnp.zeros_like(r)

        def tile(i, dtaps):
            r0 = pl.multiple_of(i * rows, rows)
            win = pl.ds(r0, rows + HALO)
            _, vjp = jax.vjp(lambda *a: _gdn_prep(*a, h), *[p[win, :] for p in pads], gt[pl.ds(r0, rows), :], taps)
            grads = vjp(tuple(r[pl.ds(i * per, per)].reshape(rows, r.shape[-1]) for r in chunked))
            for r, gval in zip(dpads, grads[:3]):
                r[win, :] += gval
            dgt_s[pl.ds(r0, rows), :] = grads[3]
            return jax.tree.map(jnp.add, dtaps, grads[4])

        dtaps = lax.fori_loop(0, t // rows, tile, (jnp.zeros((1, HD), F32),) * 12)
        for r, dpad in zip((dgq, dgk, dgv), dpads):
            r[...] = dpad[HALO:, :].astype(r.dtype)
        for j, r in enumerate((dwq, dwk, dwv)):
            for k in range(4):
                r[k:k + 1, :] = dtaps[4 * j + k]

        @pl.when(h == 0)
        def _():
            dgt[...] = jnp.zeros_like(dgt)

        dgt[...] += dgt_s[...]

    head = _head(t)
    taps = pl.BlockSpec((4, HD), lambda h: (0, h))
    return pl.pallas_call(
        body, grid=(NG,), name="gdn_bwd",
        in_specs=_gdn_in_specs(t) + [_per_head(sh) for sh in terms + [(nc, CHUNK, HD)] * 3],
        out_specs=[head, head, head, _small(t), taps, taps, taps],
        out_shape=[SDS((t, NG * HD), BF16)] * 3 + [SDS((t, HD), F32)] + [SDS((4, NG * HD), F32)] * 3,
        scratch_shapes=_gdn_chunked_scratch(nc) + [pltpu.VMEM((t + HALO, HD), F32)] * 6 + [pltpu.VMEM((t, HD), F32)],
        compiler_params=_cp("arbitrary"),
    )(pa, pa, pa, gates, conv, conv, conv, *dterms, *qkv)


def _gdn_post(o, z, gain):
    return (jnp.concatenate(
        [_rms(o[:, h * HD:(h + 1) * HD], gain) * _silu(z[:, h * HD:(h + 1) * HD]) for h in range(NG)], axis=1),)


def _place():
    return lax.axis_index("x"), lax.axis_index("y"), lax.axis_index("c")


def _sum_blocks(name, parts):
    _, r, c = parts.shape
    tr = 64 if r % 64 == 0 else r

    def body(x, o):
        acc = x[0].astype(F32)
        for d in range(1, N_DEV):
            acc = acc + x[d].astype(F32)
        o[...] = acc

    return pl.pallas_call(
        body, grid=(r // tr,), name=name, in_specs=[pl.BlockSpec((N_DEV, tr, c), lambda i: (0, i, 0))],
        out_specs=pl.BlockSpec((tr, c), lambda i: (i, 0)), out_shape=SDS((r, c), F32), compiler_params=_cp("parallel"),
    )(parts)


def _all_reduce_small(name, x):
    m_per, n = x.shape

    def body(x_ref, out_ref, send_sems, recv_sems, local_sem):
        px, py, pc = _place()
        me, sibling = (px, py, pc), (px, py, 1 - pc)
        chips = [(1 - px, py), (px, 1 - py), (1 - px, 1 - py)]
        buf = out_ref

        def rows(qx, qy, qc):
            return buf.at[pl.ds((4 * qx + 2 * qy + qc) * m_per, m_per), :]

        def copy(k, block, to, src=None):
            return pltpu.make_async_remote_copy(
                src_ref=rows(*block) if src is None else src, dst_ref=rows(*block),
                send_sem=send_sems.at[k], recv_sem=recv_sems.at[k], device_id=to, device_id_type=MESH)

        mine = pltpu.make_async_copy(x_ref, rows(*me), local_sem)
        mine.start()
        first = [copy(0, me, sibling, src=x_ref)]
        first += [copy(1 + j, me, (*chip, pc), src=x_ref) for j, chip in enumerate(chips)]
        for cp in first:
            cp.start()
        passed = [copy(4 + j, (*chip, pc), sibling) for j, chip in enumerate(chips)]
        for j, chip in enumerate(chips):
            copy(1 + j, (*chip, pc), me).wait_recv()
            passed[j].start()
        copy(0, sibling, me).wait_recv()
        for j, chip in enumerate(chips):
            copy(4 + j, (*chip, 1 - pc), me).wait_recv()
        for cp in first + passed:
            cp.wait_send()
        mine.wait()

    gathered = pl.pallas_call(
        body, name=name, out_shape=SDS((N_DEV * m_per, n), x.dtype),
        in_specs=[pl.BlockSpec(memory_space=pltpu.VMEM)], out_specs=pl.BlockSpec(memory_space=pltpu.VMEM),
        scratch_shapes=[pltpu.SemaphoreType.DMA((7,)), pltpu.SemaphoreType.DMA((7,)), pltpu.SemaphoreType.DMA],
    )(x)
    return _sum_blocks(name + "_sum", gathered.reshape(N_DEV, m_per, n))


HBM_SPEC = pl.BlockSpec(memory_space=pltpu.HBM)
SEM_SPEC = pl.BlockSpec(memory_space=pltpu.SEMAPHORE)
EFFECT = pltpu.SideEffectType.DATAFLOW_SIDE_EFFECTING


def _copies_start(name, bufs, n_remote, n_local, build, deps):
    nb, nd = len(bufs), len(deps)
    sem_shapes = [pltpu.SemaphoreType.DMA((n_remote,)), pltpu.SemaphoreType.DMA((n_remote,))]
    if n_local:
        sem_shapes.append(pltpu.SemaphoreType.DMA((n_local,)))
    ns = len(sem_shapes)

    def body(*refs):
        sems = refs[nb + nd:nb + nd + ns]
        remote, local = build(refs[:nb], *sems, *([None] * (3 - ns)))
        for cp in local + remote:
            cp.start()
        refs[-1][...] = jnp.zeros((8, HD), F32)

    outs = pl.pallas_call(
        body, name=name,
        out_shape=(*sem_shapes, *[pltpu.HBM(b.shape, b.dtype) for b in bufs], SDS((8, HD), F32)),
        in_specs=[HBM_SPEC] * nb + [ANY_SPEC] * nd,
        out_specs=(*[SEM_SPEC] * ns, *[HBM_SPEC] * nb, pl.BlockSpec(memory_space=pltpu.VMEM)),
        input_output_aliases={i: ns + i for i in range(nb)},
        compiler_params=pltpu.CompilerParams(has_side_effects=EFFECT),
    )(*[pltpu.with_memory_space_constraint(b, pltpu.HBM) for b in bufs], *deps)
    return list(outs[:ns]), list(outs[ns:ns + nb]), outs[-1]


def _copies_wait(name, bufs, sems, build, after):
    nb, ns = len(bufs), len(sems)

    def body(*refs):
        remote, local = build(refs[:nb], *refs[nb:nb + ns], *([None] * (3 - ns)))
        for cp in local:
            cp.wait()
        for cp in remote:
            cp.wait_send()
            cp.wait_recv()

    outs = pl.pallas_call(
        body, name=name, out_shape=tuple(pltpu.HBM(b.shape, b.dtype) for b in bufs),
        in_specs=[HBM_SPEC] * nb + [SEM_SPEC] * ns + [ANY_SPEC] * len(after), out_specs=tuple([HBM_SPEC] * nb),
        input_output_aliases={i: i for i in range(nb)},
        compiler_params=pltpu.CompilerParams(has_side_effects=EFFECT),
    )(*bufs, *sems, *after)
    return list(outs)


def _remote(src, dst, send, recv, k, to):
    return pltpu.make_async_remote_copy(src_ref=src, dst_ref=dst, send_sem=send.at[k], recv_sem=recv.at[k],
                                        device_id=to, device_id_type=MESH)


class _Gather:
    def __init__(self, name, shards, deps):
        self.name, self.n = name, len(shards)
        lands = [lax.empty((N_DEV,) + s.shape, s.dtype) for s in shards]
        self.sems1, bufs, self.token = _copies_start(
            name + "_s1", list(shards) + lands, 4 * self.n, self.n, self._stage1(range(self.n)), deps)
        self.shards, self.lands, self.sems2 = bufs[:self.n], bufs[self.n:], {}

    def _stage1(self, idxs):
        def build(refs, send, recv, loc):
            x, y, c = _place()
            me = 4 * x + 2 * y + c
            targets = [(x, y, 1 - c), (1 - x, y, c), (x, 1 - y, c), (1 - x, 1 - y, c)]
            remote, local = [], []
            for pos, i in enumerate(idxs):
                src, land = refs[pos], refs[len(idxs) + pos]
                local.append(pltpu.make_async_copy(src, land.at[me], loc.at[i]))
                remote += [_remote(src, land.at[me], send, recv, 4 * i + k, to) for k, to in enumerate(targets)]
            return remote, local
        return build

    @staticmethod
    def _stage2(refs, send, recv, loc):
        x, y, c = _place()
        remote = []
        for pos, land in enumerate(refs):
            for j, (cx, cy) in enumerate([(1 - x, y), (x, 1 - y), (1 - x, 1 - y)]):
                blk = land.at[4 * cx + 2 * cy + c]
                remote.append(_remote(blk, blk, send, recv, 3 * pos + j, (x, y, 1 - c)))
        return remote, []

    def pass_on(self, idxs, after):
        tag, m = "".join(map(str, idxs)), len(idxs)
        bufs = _copies_wait(f"{self.name}_w1_{tag}", [self.shards[i] for i in idxs] + [self.lands[i] for i in idxs],
                            self.sems1, self._stage1(idxs), after)
        self.sems2[tag], lands, token = _copies_start(f"{self.name}_s2_{tag}", bufs[m:], 3 * m, 0, self._stage2, ())
        for pos, i in enumerate(idxs):
            self.lands[i] = lands[pos]
        return [token]

    def get(self, idxs, after):
        tag = "".join(map(str, idxs))
        return _copies_wait(f"{self.name}_w2_{tag}", [self.lands[i] for i in idxs], self.sems2[tag], self._stage2, after)


class _RelayGather:
    def __init__(self, name, shards, deps):
        self.name, self.n = name, len(shards)
        lands = [lax.empty((N_DEV,) + s.shape, s.dtype) for s in shards]
        self.sems, bufs, self.token = _copies_start(name + "_s1", list(shards) + lands, 3 * self.n, self.n, self._stage1, deps)
        self.shards, self.lands = bufs[:self.n], bufs[self.n:]

    def _stage1(self, refs, send, recv, loc):
        x, y, c = _place()
        me = 4 * x + 2 * y + c
        remote, local = [], []
        for i in range(self.n):
            src, land = refs[i], refs[self.n + i]
            local.append(pltpu.make_async_copy(src, land.at[me], loc.at[i]))
            remote += [_remote(src, land.at[me], send, recv, 3 * i + k, to)
                       for k, to in enumerate([(x, y, 1 - c), (1 - x, y, c), (x, 1 - y, c)])]
        return remote, local

    @staticmethod
    def _relay(refs, send, recv, loc):
        x, y, c = _place()
        remote = []
        for i, land in enumerate(refs):
            half = land.shape[1] // 2
            from_x = land.at[4 * (1 - x) + 2 * y + c].at[pl.ds(0, half)]
            from_y = land.at[4 * x + 2 * (1 - y) + c].at[pl.ds(half, half)]
            remote += [_remote(from_x, from_x, send, recv, 2 * i, (x, 1 - y, c)),
                       _remote(from_y, from_y, send, recv, 2 * i + 1, (1 - x, y, c))]
        return remote, []

    def forward(self, after):
        bufs = _copies_wait(self.name + "_w1", self.shards + self.lands, self.sems, self._stage1, after)
        self.sems, self.lands, self.token = _copies_start(self.name + "_sf", bufs[self.n:], 2 * self.n, 0, self._relay, ())
        return [self.token]

    def pass_on(self, after):
        lands = _copies_wait(self.name + "_wf", self.lands, self.sems, self._relay, after)
        self.sems, self.lands, self.token = _copies_start(self.name + "_s2", lands, 3 * self.n, 0, _Gather._stage2, ())
        return [self.token]

    def get(self, after):
        return _copies_wait(self.name + "_w2", self.lands, self.sems, _Gather._stage2, after)


def _rows_tile(r, row_bytes, target=1 << 20):
    tr = r
    while tr % 32 == 0 and tr * row_bytes > target:
        tr //= 2
    return tr


def _pair_add(name, g, got, c):
    _, r, cols = g.shape
    tr = _rows_tile(r, cols * 2)

    def body(s, a, b, o):
        o[...] = (a[...].astype(F32) + b[...].astype(F32)).astype(o.dtype)

    return pl.pallas_call(
        body, name=name, out_shape=SDS((4, r, cols), g.dtype),
        grid_spec=pltpu.PrefetchScalarGridSpec(
            num_scalar_prefetch=1, grid=(4, r // tr),
            in_specs=[pl.BlockSpec((None, tr, cols), lambda j, i, s: (2 * j + s[0], i, 0)),
                      pl.BlockSpec((None, tr, cols), lambda j, i, s: (j, i, 0))],
            out_specs=pl.BlockSpec((None, tr, cols), lambda j, i, s: (j, i, 0))),
        compiler_params=_cp("parallel", "parallel"),
    )(c.reshape(1), g, got)


def _quad_sum(name, part, got, chip, wmv=None):
    _, r, cols = part.shape
    tr = _rows_tile(r, cols * 4)
    n_out = 4 if wmv else 1

    def body(s, a, b1, b2, b3, *rest):
        g = ((a[...].astype(F32) + b1[...].astype(F32)) + b2[...].astype(F32)) + b3[...].astype(F32)
        rest[-n_out][...] = g
        if wmv:
            w, m, v = rest[:3]
            rest[-3][...], rest[-2][...], rest[-1][...] = _adamw(w[...], g, m[...], v[...])

    blk = lambda k: pl.BlockSpec((None, tr, cols), lambda i, s, k=k: (jnp.bitwise_xor(s[0], k), i, 0))
    row = pl.BlockSpec((tr, cols), lambda i, s: (i, 0))
    outs = pl.pallas_call(
        body, name=name, out_shape=[SDS((r, cols), F32)] * n_out,
        grid_spec=pltpu.PrefetchScalarGridSpec(
            num_scalar_prefetch=1, grid=(r // tr,), in_specs=[blk(0), blk(1), blk(2), blk(3)] + [row] * (n_out - 1),
            out_specs=[row] * n_out),
        compiler_params=_cp("parallel"),
    )(chip.reshape(1), part, got, got, got, *(wmv or ()))
    return tuple(outs) if wmv else outs[0]


class _Scatter:
    def __init__(self, name, grads, deps):
        self.name, self.n = name, len(grads)
        got = [lax.empty((4,) + g.shape[1:], g.dtype) for g in grads]
        self.sems, bufs, self.token = _copies_start(name + "_s1", list(grads) + got, 4 * self.n, 0, self._stage1, deps)
        self.grads, self.got = bufs[:self.n], bufs[self.n:]

    def _stage1(self, refs, send, recv, loc):
        x, y, c = _place()
        remote = []
        for i in range(self.n):
            remote += [_remote(refs[i].at[2 * j + 1 - c], refs[self.n + i].at[j], send, recv, 4 * i + j, (x, y, 1 - c))
                       for j in range(4)]
        return remote, []

    def _stage2(self, refs, send, recv, loc):
        x, y, c = _place()
        remote = []
        for i in range(self.n):
            for k in (1, 2, 3):
                tx = 1 - x if k & 2 else x
                ty = 1 - y if k & 1 else y
                remote.append(_remote(refs[i].at[2 * tx + ty], refs[self.n + i].at[2 * x + y], send, recv,
                                      3 * i + k - 1, (tx, ty, c)))
        return remote, []

    def mid(self, after):
        bufs = _copies_wait(self.name + "_w1", self.grads + self.got, self.sems, self._stage1, after)
        c = lax.axis_index("c").astype(jnp.int32)
        parts = [_pair_add(f"{self.name}_add{i}", bufs[i], bufs[self.n + i], c) for i in range(self.n)]
        got = [lax.empty(p.shape, p.dtype) for p in parts]
        self.sems, bufs, self.token = _copies_start(self.name + "_s2", parts + got, 3 * self.n, 0, self._stage2, ())
        self.parts, self.got = bufs[:self.n], bufs[self.n:]

    def end(self, after, wmv=None):
        bufs = _copies_wait(self.name + "_w2", self.parts + self.got, self.sems, self._stage2, after)
        chip = (2 * lax.axis_index("x") + lax.axis_index("y")).astype(jnp.int32)
        wmv = wmv or [None] * self.n
        return [_quad_sum(f"{self.name}_sum{i}", bufs[i], bufs[self.n + i], chip, wmv[i]) for i in range(self.n)]


def _adamw(w, g, m, v):
    m = ADAM_B1 * m + (1.0 - ADAM_B1) * g
    v = ADAM_B2 * v + (1.0 - ADAM_B2) * (g * g)
    m_hat = m / (1.0 - ADAM_B1 ** ADAM_STEP)
    v_hat = v / (1.0 - ADAM_B2 ** ADAM_STEP)
    return -ADAM_LR * (m_hat / (jnp.sqrt(v_hat) + ADAM_EPS) + ADAM_WD * w), m, v


def _adamw_call(name, w, g, m, v):
    r, c = w.shape
    tm = 64 if r % 64 == 0 else r
    return _rowwise(name, _adamw, [w, g, m, v], [], [(c, F32)] * 3, tm)


_IN_COLS = 5906


def _perm_in(w):
    pad = jnp.zeros((w.shape[0], 2 * HALF - _IN_COLS), w.dtype)
    return (jnp.concatenate([w[:, 2310:4614], w[:, 4614:5382]], axis=1),
            jnp.concatenate([w[:, :2304], w[:, 5394:5906], w[:, 2304:2310], w[:, 5382:5394], pad], axis=1))


def _unperm_in(ga, gb):
    return jnp.concatenate([gb[:, :2304], gb[:, 2816:2822], ga[:, :2304], ga[:, 2304:3072], gb[:, 2822:2834],
                            gb[:, 2304:2816]], axis=1)


def _lanes(v, at):
    return jnp.pad(v, ((0, 0), (at, HD - at - v.shape[1])))


_PACK = ("norm_mix", "mem_norm", "norm_ffn", "gdn_conv", "fox_q_norm", "fox_k_norm", "gdn_out_norm", "mem_q_norm",
         "mem_k_norm", "fox_f_bias", "gdn_a_log", "gdn_dt_bias", "loss")


def _pack(vals):
    parts = [vals[n].reshape(-1, HD) for n in _PACK]
    used = sum(p.shape[0] for p in parts)
    buf = jnp.concatenate(parts + [jnp.zeros((-used % 8, HD), F32)], axis=0)
    return buf, [(n, p.shape[0]) for n, p in zip(_PACK, parts)]


def _unpack(buf, layout):
    out, at = {}, 0
    for n, rows in layout:
        out[n] = buf[at:at + rows]
        at += rows
    return out


def kernel(x, mem, norm_mix, w_in, fox_f_bias, fox_q_norm, fox_k_norm, gdn_conv, gdn_a_log, gdn_dt_bias, gdn_out_norm, mem_norm, w_mem_kv, mem_q_norm, mem_k_norm, w_out, norm_ffn, w_gate_up, w_down, loss_target, m_norm_mix, m_w_in, m_fox_f_bias, m_fox_q_norm, m_fox_k_norm, m_gdn_conv, m_gdn_a_log, m_gdn_dt_bias, m_gdn_out_norm, m_mem_norm, m_w_mem_kv, m_mem_q_norm, m_mem_k_norm, m_w_out, m_norm_ffn, m_w_gate_up, m_w_down, v_norm_mix, v_w_in, v_fox_f_bias, v_fox_q_norm, v_fox_k_norm, v_gdn_conv, v_gdn_a_log, v_gdn_dt_bias, v_gdn_out_norm, v_mem_norm, v_w_mem_kv, v_mem_q_norm, v_mem_k_norm, v_w_out, v_norm_ffn, v_w_gate_up, v_w_down):
    args = dict(locals())
    d = x.shape[2]
    me = 4 * lax.axis_index("x") + 2 * lax.axis_index("y") + lax.axis_index("c")

    cshard = gdn_conv[0].shape[1]
    conv_pad = jnp.pad(gdn_conv[0], ((0, 4), (0, 3 * HD - cshard)))
    w_in_a, w_in_b = _perm_in(w_in[0])
    wmv = lambda n: (args[n][0], args["m_" + n][0], args["v_" + n][0])
    comm = _StepComm(w_in_b.astype(BF16), {"in_a": [w_in_a.astype(BF16), conv_pad],
                                           "kv_out": [w_mem_kv[0].astype(BF16), w_out[0].astype(BF16)]},
                     w_gate_up[0].astype(BF16), w_down[0].astype(BF16), (),
                     {"ffn": [wmv("w_down"), wmv("w_gate_up")], "a": [None, wmv("w_out"), wmv("w_mem_kv")], "b": [None]})

    grad_x, loss_local, small_grads = _local_step(
        x[0], mem[0], loss_target[0], norm_mix, fox_f_bias, fox_q_norm, fox_k_norm, gdn_a_log, gdn_dt_bias,
        gdn_out_norm, mem_norm, mem_q_norm, mem_k_norm, norm_ffn, cshard, comm)

    red = comm.finish([grad_x])
    updated = {"w_down": red["ffn"][0], "w_gate_up": red["ffn"][1], "w_out": red["a"][1], "w_mem_kv": red["a"][2]}
    grads = {n: r[0] for n, r in updated.items()}
    grads["w_in"] = _unperm_in(red["a"][0], red["b"][0])
    small_grads["loss"] = jnp.broadcast_to(loss_local, (1, HD))
    packed, layout = _pack(small_grads)
    small = _unpack(_all_reduce_small("ar_small", packed), layout)
    loss = small["loss"][0, 0]
    six = {"fox_f_bias": L_FF, "gdn_a_log": L_GA, "gdn_dt_bias": L_GA}
    for n, rows_n in layout[:-1]:
        gsm = small[n]
        if n == "gdn_conv":
            gsm = lax.dynamic_slice(gsm.reshape(4, N_DEV * cshard), (0, me * cshard), (4, cshard))[None]
        elif n in six:
            gsm = gsm[:, six[n]:six[n] + 6]
        else:
            gsm = gsm.reshape(1, rows_n * HD)
        grads[n] = gsm

    names = ['norm_mix', 'w_in', 'fox_f_bias', 'fox_q_norm', 'fox_k_norm', 'gdn_conv', 'gdn_a_log', 'gdn_dt_bias',
             'gdn_out_norm', 'mem_norm', 'w_mem_kv', 'mem_q_norm', 'mem_k_norm', 'w_out', 'norm_ffn', 'w_gate_up', 'w_down']
    big = ("w_in", "w_mem_kv", "w_out", "w_gate_up", "w_down")
    delta, new_m, new_v = {}, {}, {}
    for n in big:
        res = updated[n][1:] if n in updated else _adamw_call("adamw_" + n, args[n][0], grads[n], *wmv(n)[1:])
        delta[n], new_m[n], new_v[n] = [a[None] for a in res]
        grads[n] = grads[n][None]

    def flat(a):
        a = a.reshape(1, -1)
        return jnp.pad(a, ((0, 0), (0, -a.shape[1] % HD))).reshape(-1, HD)

    smalls = [n for n in names if n not in big]
    pk = lambda pre: jnp.concatenate([flat(grads[n] if pre == "g" else args[pre + n]) for n in smalls], axis=0)
    cat = [pk(""), pk("g"), pk("m_"), pk("v_")]
    padr = -cat[0].shape[0] % 8
    cat = [jnp.pad(a, ((0, padr), (0, 0))) for a in cat]
    res = _adamw_call("adamw_small", *cat)
    at = 0
    for n in smalls:
        shape = args[n].shape
        size = math.prod(shape)
        nrow = -(-size // HD)
        for dst, src in zip((delta, new_m, new_v), res):
            dst[n] = src[at:at + nrow].reshape(-1)[:size].reshape(shape)
        at += nrow

    return (loss, grad_x[None], *[grads[n] for n in names], *[delta[n] for n in names],
            *[new_m[n] for n in names], *[new_v[n] for n in names])


class _StepComm:
    def __init__(self, first, shard_groups, w_gate_up, w_down, after, wmv):
        self.wmv, self.done = wmv, {}
        self.first = _RelayGather("ag_first", [first], after)
        self.groups, self.shards = {}, []
        for key, ws in shard_groups.items():
            self.groups[key] = list(range(len(self.shards), len(self.shards) + len(ws)))
            self.shards += list(ws)
        self.w_gate_up, self.w_down = w_gate_up, w_down
        self.passed, self.scatters = set(), {}

    def start_deps(self):
        return [self.first.token]

    def first_weights(self, after):
        deps = self.first.forward(after)
        self.gather = _Gather("ag", self.shards, deps)
        self.relay = _RelayGather("ag_gu", [self.w_gate_up], [self.gather.token])
        return self.first.get(self.first.pass_on([self.relay.token]))

    def relay_forward(self, after):
        deps = self.relay.forward(after)
        self.gather_down = _Gather("ag_dn", [self.w_down], deps)
        return [self.gather_down.token]

    def pass_on(self, key, after):
        self.passed.add(key)
        if key == "gate_up":
            return self.relay.pass_on(after)
        return self.gather.pass_on(self.groups[key], after)

    def weights(self, key, after):
        if key == "down":
            return self.gather_down.get([0], self.gather_down.pass_on([0], after))
        if key not in self.passed:
            after = self.pass_on(key, after)
        return self.relay.get(after) if key == "gate_up" else self.gather.get(self.groups[key], after)

    def send(self, tag, grads):
        blocks = [g if g.ndim == 3 else g.reshape(N_DEV, g.shape[0] // N_DEV, g.shape[1]) for g in grads]
        self.scatters[tag] = _Scatter("rs_" + tag, blocks, ())
        return [self.scatters[tag].token]

    def mid(self, tag, after):
        self.scatters[tag].mid(after)
        return [self.scatters[tag].token]

    def finish_group(self, tag, after):
        self.done[tag] = self.scatters.pop(tag).end(after, self.wmv[tag])
        first = self.done[tag][0]
        return [first[0] if isinstance(first, tuple) else first]

    def finish(self, after):
        for tag in list(self.scatters):
            self.finish_group(tag, after)
        return self.done


def _local_step(xs, ms, tgt, norm_mix, fox_f_bias, fox_q_norm, fox_k_norm, gdn_a_log, gdn_dt_bias, gdn_out_norm,
                mem_norm, mem_q_norm, mem_k_norm, norm_ffn, cshard, comm):
    t, d = xs.shape
    bq = min(t, 256)
    fb, alog, dtb = _lanes(fox_f_bias, L_FF), _lanes(gdn_a_log, L_GA), _lanes(gdn_dt_bias, L_GA)
    flat = lambda w: w.reshape(-1, w.shape[-1])

    rms1 = lambda a, g: (_rms(a, g),)
    (u,) = _rowwise("norm_mix", rms1, [xs], [norm_mix], [(d, BF16)], min(t, 256), deps=comm.start_deps())
    w_in_b = flat(comm.first_weights([u])[0])
    pb = _matmul("proj_in_b", u, w_in_b, NN, F32, 1024, 768)
    o_fox = _fox_fwd(pb, fb, fox_q_norm, fox_k_norm, bq)
    w_in_a, conv_parts = comm.weights("in_a", [o_fox])
    w_in_a = flat(w_in_a)
    conv_all = conv_parts[:, :4, :cshard].transpose(1, 0, 2).reshape(4, N_DEV * cshard)
    pa = _matmul("proj_in_a", u, w_in_a, NN, F32, 1024, 768)
    smrow = (pb, HD, SM)
    (gates,) = _rowwise("gdn_gates", _gdn_gates, [smrow], [alog, dtb], [(HD, F32)], min(t, 256))
    gdn_terms, gdn_qkv = _gdn_fwd(pa, gates, conv_all)
    o_gdn_raw, gdn_states = _gdn_scan(gdn_terms)
    gdn_saved = list(gdn_terms) + [gdn_states]
    deps = comm.relay_forward([o_gdn_raw])
    zrow = (pa, NG * HD, GZ * HD // (NG * HD))
    (o_gdn,) = _rowwise("gdn_post", _gdn_post, [o_gdn_raw, zrow], [gdn_out_norm], [(NG * HD, BF16)], min(t, 256),
                        deps=deps)
    w_kv_all, w_out_all = [flat(w) for w in comm.weights("kv_out", [o_gdn])]
    (mem_n,) = _rowwise("norm_mem", rms1, [ms], [mem_norm], [(d, BF16)], ms.shape[0])
    mkv = _matmul("proj_mem", mem_n, w_kv_all, NN, F32, 256, 512)
    o_mem = _mem_fwd(pb, mkv, mem_q_norm, mem_k_norm)
    deps = comm.pass_on("gate_up", [o_mem])
    mix = jnp.concatenate([o_fox, o_gdn, o_mem], axis=1)
    h1, h1n = _proj_out_norm(mix, w_out_all, xs, norm_ffn, deps)
    (wgu,) = comm.weights("gate_up", [h1n])
    ffw = wgu.shape[2]
    gu, act = _ffn_up(h1n, wgu.reshape(2, 4, d, ffw))
    w_down_all = flat(comm.weights("down", [act])[0])
    dyb, lsum = _ffn_down_loss(act, w_down_all, h1, tgt)
    loss_local = (0.5 / d) * jnp.sum(lsum[::8, ::HD])

    dgu = _ffn_down_bwd(dyb, w_down_all.reshape(4, ffw, d), gu).reshape(8, t, ffw)
    g_w_down = _matmul("grad_w_down", act, dyb, TN, BF16, 512, 2048)
    g_w_gu = _ffn_up_bwd_w(h1n, dgu)
    deps = comm.send("ffn", [g_w_down, g_w_gu])
    rms2 = lambda a, g: (_rms(a, g), a)
    dh1b, g_norm_ffn = _ffn_up_bwd_x(dgu, wgu, h1, norm_ffn, dyb, deps)

    dmix = _matmul("proj_out_bwd_x", dh1b, w_out_all, NT, BF16, 1024, 1024)
    g_w_out = _matmul("grad_w_out", mix, dh1b, TN, BF16, 1024, 2048)
    deps = comm.mid("ffn", [dmix, g_w_out])
    dmq, dmk, dmv, g_mqn, g_mkn = _mem_bwd(pb, mkv, mem_q_norm, mem_k_norm, dmix, deps=deps)
    dmkv = jnp.concatenate([dmk, dmv], axis=1).astype(BF16)
    g_w_kv = _matmul("grad_w_kv", mem_n, dmkv, TN, BF16, 512, 512)
    do_raw, dgz, g_gon = _rowwise_vjp("gdn_post_bwd", _gdn_post, [o_gdn_raw, zrow], [gdn_out_norm],
                                      [(dmix, NG * HD, 1)], [F32, BF16], min(t, 256), deps=deps)
    dterms = _gdn_bwd_scan(gdn_saved, do_raw)
    dgq, dgk, dgv, dgates, dwq, dwk, dwv = _gdn_bwd(pa, gates, conv_all, dterms, gdn_qkv)
    dsm_gdn, g_alog, g_dtb = _rowwise_vjp("gdn_gates_bwd", _gdn_gates, [smrow], [alog, dtb], [dgates], [F32], min(t, 256))
    dp_a = jnp.concatenate([dgq, dgk, dgv, dgz], axis=1)
    g_w_in_a = _matmul("grad_w_in_a", u, dp_a, TN, BF16, 512, 3072)
    deps = comm.send("a", [g_w_in_a, g_w_out, g_w_kv])
    du_a = _matmul("proj_in_bwd_a", dp_a, w_in_a, NT, F32, 1024, 1024, deps=deps)
    deps = comm.mid("a", [du_a])
    dfq, dfk, dfv, dsm_fox, g_fb, g_fqn, g_fkn = _fox_bwd(pb, fb, fox_q_norm, fox_k_norm, dmix, 2 * bq if t % (2 * bq) == 0 else bq,
                                                          deps=deps)
    dp_b = jnp.concatenate([dfq, dfk, dfv, dmq, (dsm_fox + dsm_gdn).astype(BF16), jnp.zeros((t, HD), BF16)], axis=1)
    g_w_in_b = _matmul("grad_w_in_b", u, dp_b, TN, BF16, 512, 3072)
    deps = comm.mid("b", comm.finish_group("ffn", comm.send("b", [g_w_in_b])))
    deps = comm.finish_group("a", deps)
    dmem_n = _matmul("proj_mem_bwd_x", dmkv, w_kv_all, NT, F32, 256, 512, deps=deps)
    g_mem_norm = _rowwise_vjp("norm_mem_bwd", rms1, [ms], [mem_norm], [dmem_n], [], ms.shape[0])[0]
    grad_x, g_norm_mix = _proj_in_bwd_norm(dp_b, w_in_b, du_a, xs, norm_mix, dh1b, [g_mem_norm])

    small_grads = {
        "norm_mix": g_norm_mix, "mem_norm": g_mem_norm, "norm_ffn": g_norm_ffn,
        "gdn_conv": jnp.concatenate([dwq, dwk, dwv], axis=1),
        "fox_q_norm": g_fqn, "fox_k_norm": g_fkn, "gdn_out_norm": g_gon, "mem_q_norm": g_mqn, "mem_k_norm": g_mkn,
        "fox_f_bias": g_fb, "gdn_a_log": g_alog, "gdn_dt_bias": g_dtb}
    return grad_x, loss_local, small_grads
```
